```python
import jax, jax.numpy as jnp
from jax import lax
import numpy as np

D_MODEL = 2048
BATCH = 16
SEQ = 2048
DEPTH = 1

HEAD_DIM = 128
ATTN_PATTERNS = ((128, 1), (512, 4), (2048, 16))
N_GROUPS = len(ATTN_PATTERNS)
HEADS_PER_GROUP = 4
ATTN_QKV = N_GROUPS * HEADS_PER_GROUP * HEAD_DIM
ATTN_OUT = HEADS_PER_GROUP * HEAD_DIM
BLK = 128
CONV_WIDTH = 1024
CONV_K = 3
MEM_LEN = 256
MEM_HEADS = 4
MEM_HEAD_DIM = 256
MEM_W = MEM_HEADS * MEM_HEAD_DIM
N_BRANCH = 3
EPS = 1e-6

SPLIT_SIZES = (ATTN_QKV, ATTN_QKV, ATTN_QKV, ATTN_OUT,
               CONV_WIDTH, CONV_WIDTH, CONV_WIDTH, CONV_WIDTH,
               MEM_W, MEM_W, N_BRANCH * D_MODEL)
IN_COLS = int(sum(SPLIT_SIZES))
SPLIT_IDX = [int(i) for i in np.cumsum(SPLIT_SIZES)[:-1]]

kernel_name = "hybrid_dilated_attn_shortconv_memory_gated_merge"


def rms_norm(t, g):
    tf = t.astype(jnp.float32)
    y = tf * lax.rsqrt(jnp.mean(tf * tf, axis=-1, keepdims=True) + EPS) * g.astype(jnp.float32)
    return y.astype(t.dtype)


def banded_causal_attn(q, k, v, span):
    N, L, H, E = q.shape
    nb = -(-L // BLK)
    pad = nb * BLK - L
    q = jnp.pad(q, ((0, 0), (0, pad), (0, 0), (0, 0)))
    k = jnp.pad(k, ((0, 0), (BLK, pad), (0, 0), (0, 0)))
    v = jnp.pad(v, ((0, 0), (BLK, pad), (0, 0), (0, 0)))
    qb = q.reshape(N, nb, BLK, H, E)

    def two_blocks(t):
        t = t.reshape(N, nb + 1, BLK, H, E)
        return jnp.concatenate([t[:, :-1], t[:, 1:]], axis=2)

    kb, vb = two_blocks(k), two_blocks(v)
    s = jnp.einsum('nbqhe,nbkhe->nbhqk', qb.astype(jnp.float32), kb.astype(jnp.float32)) * (E ** -0.5)
    qpos = jnp.arange(BLK)[:, None] + BLK
    kpos = jnp.arange(2 * BLK)[None, :]
    rel = qpos - kpos
    band = (rel >= 0) & (rel <= span)
    valid = (jnp.arange(nb)[:, None] * BLK + kpos - BLK) >= 0
    mask = band[None] & valid[:, None, :]
    s = jnp.where(mask[None, :, None], s, -jnp.inf)
    m = jnp.max(s, axis=-1, keepdims=True)
    p = jnp.exp(s - m)
    den = jnp.sum(p, axis=-1, keepdims=True)
    o = jnp.einsum('nbhqk,nbkhe->nbqhe', p / den, vb.astype(jnp.float32))
    lse = (m + jnp.log(den))[..., 0]
    o = o.reshape(N, nb * BLK, H, E)[:, :L]
    lse = lse.transpose(0, 1, 3, 2).reshape(N, nb * BLK, H)[:, :L]
    return o, lse


def dilated_causal_attn(q, k, v, window, dilation):
    B, S, H, E = q.shape
    L = S // dilation

    def to_classes(t):
        return t.reshape(B, L, dilation, H, E).transpose(0, 2, 1, 3, 4).reshape(B * dilation, L, H, E)

    o, lse = banded_causal_attn(to_classes(q), to_classes(k), to_classes(v), window // dilation)
    o = o.reshape(B, dilation, L, H, E).transpose(0, 2, 1, 3, 4).reshape(B, S, H, E)
    lse = lse.reshape(B, dilation, L, H).transpose(0, 2, 1, 3).reshape(B, S, H)
    return o, lse


def _fwd_setup_inputs(seed: int = 0) -> dict:
    key = jax.random.key(seed)
    ks = jax.random.split(key, 16)
    f32 = jnp.float32

    def w(k, shape, fan_in):
        return jax.random.normal(k, shape, f32) * (fan_in ** -0.5)

    def gain(k, shape):
        return 1.0 + 0.02 * jax.random.normal(k, shape, f32)

    return {
        "x": jax.random.normal(ks[0], (BATCH, SEQ, D_MODEL), f32),
        "mem": jax.random.normal(ks[1], (BATCH, MEM_LEN, D_MODEL), f32),
        "norm_g": gain(ks[2], (D_MODEL,)),
        "mem_norm_g": gain(ks[3], (D_MODEL,)),
        "w_in": w(ks[4], (D_MODEL, IN_COLS), D_MODEL),
        "attn_q_norm": gain(ks[5], (N_GROUPS, HEAD_DIM)),
        "attn_k_norm": gain(ks[6], (N_GROUPS, HEAD_DIM)),
        "conv_w": w(ks[7], (CONV_K, CONV_WIDTH), CONV_K),
        "mem_w_kv": w(ks[8], (D_MODEL, 2 * MEM_W), D_MODEL),
        "mem_q_norm": gain(ks[9], (MEM_HEAD_DIM,)),
        "mem_k_norm": gain(ks[10], (MEM_HEAD_DIM,)),
        "w_br_attn": w(ks[11], (ATTN_OUT, D_MODEL), ATTN_OUT),
        "w_br_conv": w(ks[12], (CONV_WIDTH, D_MODEL), CONV_WIDTH),
        "w_br_mem": w(ks[13], (MEM_W, D_MODEL), MEM_W),
        "w_out": w(ks[14], (D_MODEL, D_MODEL), D_MODEL),
    }


def _fwd_reference(x, mem, norm_g, mem_norm_g, w_in, attn_q_norm, attn_k_norm, conv_w,
              mem_w_kv, mem_q_norm, mem_k_norm, w_br_attn, w_br_conv, w_br_mem, w_out):
    B, S, D = x.shape
    for _layer in range(DEPTH):
        h = rms_norm(x, norm_g)
        proj = jnp.einsum('bsd,dc->bsc', h, w_in)
        (q, k, v, z_attn, conv_b, conv_c, conv_v, z_conv,
         mem_q, z_mem, gates) = jnp.split(proj, SPLIT_IDX, axis=-1)

        q = q.reshape(B, S, N_GROUPS, HEADS_PER_GROUP, HEAD_DIM)
        k = k.reshape(B, S, N_GROUPS, HEADS_PER_GROUP, HEAD_DIM)
        v = v.reshape(B, S, N_GROUPS, HEADS_PER_GROUP, HEAD_DIM)
        outs, lses = [], []
        for g, (window, dilation) in enumerate(ATTN_PATTERNS):
            qg = rms_norm(q[:, :, g], attn_q_norm[g])
            kg = rms_norm(k[:, :, g], attn_k_norm[g])
            o, lse = dilated_causal_attn(qg, kg, v[:, :, g], window, dilation)
            outs.append(o)
            lses.append(lse)
        alpha = jax.nn.softmax(jnp.stack(lses, axis=0), axis=0)
        a = jnp.sum(alpha[..., None] * jnp.stack(outs, axis=0), axis=0)
        a = a.reshape(B, S, ATTN_OUT).astype(x.dtype) * jax.nn.silu(z_attn)

        u = conv_c * conv_v
        y = sum(conv_w[j] * jnp.pad(u, ((0, 0), (j, 0), (0, 0)))[:, :S] for j in range(CONV_K))
        c = conv_b * y * jax.nn.silu(z_conv)

        mh = rms_norm(mem, mem_norm_g)
        mkv = jnp.einsum('bmd,dc->bmc', mh, mem_w_kv)
        mk, mv = jnp.split(mkv, 2, axis=-1)
        M = mem.shape[1]
        mq = rms_norm(mem_q.reshape(B, S, MEM_HEADS, MEM_HEAD_DIM), mem_q_norm)
        mk = rms_norm(mk.reshape(B, M, MEM_HEADS, MEM_HEAD_DIM), mem_k_norm)
        mv = mv.reshape(B, M, MEM_HEADS, MEM_HEAD_DIM)
        ms = jnp.einsum('bshe,bmhe->bhsm', mq.astype(jnp.float32), mk.astype(jnp.float32)) * (MEM_HEAD_DIM ** -0.5)
        mp = jax.nn.softmax(ms, axis=-1)
        mo = jnp.einsum('bhsm,bmhe->bshe', mp, mv.astype(jnp.float32))
        mo = mo.reshape(B, S, MEM_W).astype(x.dtype) * jax.nn.silu(z_mem)

        gt = jax.nn.sigmoid(gates.astype(jnp.float32).reshape(B, S, N_BRANCH, D)).astype(x.dtype)
        merged = (gt[:, :, 0] * jnp.einsum('bsc,cd->bsd', a, w_br_attn)
                  + gt[:, :, 1] * jnp.einsum('bsc,cd->bsd', c, w_br_conv)
                  + gt[:, :, 2] * jnp.einsum('bsc,cd->bsd', mo, w_br_mem))
        x = x + jnp.einsum('bsd,de->bse', merged, w_out)
    return x


import jax as _jax
import jax.numpy as _jnp

TWIN_FORMAT = 'train_step'
FWD_PARAMS = ['x', 'mem', 'norm_g', 'mem_norm_g', 'w_in', 'attn_q_norm', 'attn_k_norm', 'conv_w', 'mem_w_kv', 'mem_q_norm', 'mem_k_norm', 'w_br_attn', 'w_br_conv', 'w_br_mem', 'w_out']
TWIN_WEIGHTS = ['norm_g', 'mem_norm_g', 'w_in', 'attn_q_norm', 'attn_k_norm', 'conv_w', 'mem_w_kv', 'mem_q_norm', 'mem_k_norm', 'w_br_attn', 'w_br_conv', 'w_br_mem', 'w_out']
TWIN_DIFF_INPUT = 'x'
TWIN_INPUTS = ['x', 'mem', 'norm_g', 'mem_norm_g', 'w_in', 'attn_q_norm', 'attn_k_norm', 'conv_w', 'mem_w_kv', 'mem_q_norm', 'mem_k_norm', 'w_br_attn', 'w_br_conv', 'w_br_mem', 'w_out', 'loss_target', 'm_norm_g', 'm_mem_norm_g', 'm_w_in', 'm_attn_q_norm', 'm_attn_k_norm', 'm_conv_w', 'm_mem_w_kv', 'm_mem_q_norm', 'm_mem_k_norm', 'm_w_br_attn', 'm_w_br_conv', 'm_w_br_mem', 'm_w_out', 'v_norm_g', 'v_mem_norm_g', 'v_w_in', 'v_attn_q_norm', 'v_attn_k_norm', 'v_conv_w', 'v_mem_w_kv', 'v_mem_q_norm', 'v_mem_k_norm', 'v_w_br_attn', 'v_w_br_conv', 'v_w_br_mem', 'v_w_out']
TWIN_OUTPUTS = ['loss', 'grad_x', 'grad_norm_g', 'grad_mem_norm_g', 'grad_w_in', 'grad_attn_q_norm', 'grad_attn_k_norm', 'grad_conv_w', 'grad_mem_w_kv', 'grad_mem_q_norm', 'grad_mem_k_norm', 'grad_w_br_attn', 'grad_w_br_conv', 'grad_w_br_mem', 'grad_w_out', 'delta_norm_g', 'delta_mem_norm_g', 'delta_w_in', 'delta_attn_q_norm', 'delta_attn_k_norm', 'delta_conv_w', 'delta_mem_w_kv', 'delta_mem_q_norm', 'delta_mem_k_norm', 'delta_w_br_attn', 'delta_w_br_conv', 'delta_w_br_mem', 'delta_w_out', 'new_m_norm_g', 'new_m_mem_norm_g', 'new_m_w_in', 'new_m_attn_q_norm', 'new_m_attn_k_norm', 'new_m_conv_w', 'new_m_mem_w_kv', 'new_m_mem_q_norm', 'new_m_mem_k_norm', 'new_m_w_br_attn', 'new_m_w_br_conv', 'new_m_w_br_mem', 'new_m_w_out', 'new_v_norm_g', 'new_v_mem_norm_g', 'new_v_w_in', 'new_v_attn_q_norm', 'new_v_attn_k_norm', 'new_v_conv_w', 'new_v_mem_w_kv', 'new_v_mem_q_norm', 'new_v_mem_k_norm', 'new_v_w_br_attn', 'new_v_w_br_conv', 'new_v_w_br_mem', 'new_v_w_out']
TWIN_LEAF_KINDS = {'loss': 'loss', 'grad_x': 'grad_x', 'grad_norm_g': 'grad_w', 'grad_mem_norm_g': 'grad_w', 'grad_w_in': 'grad_w', 'grad_attn_q_norm': 'grad_w', 'grad_attn_k_norm': 'grad_w', 'grad_conv_w': 'grad_w', 'grad_mem_w_kv': 'grad_w', 'grad_mem_q_norm': 'grad_w', 'grad_mem_k_norm': 'grad_w', 'grad_w_br_attn': 'grad_w', 'grad_w_br_conv': 'grad_w', 'grad_w_br_mem': 'grad_w', 'grad_w_out': 'grad_w', 'delta_norm_g': 'delta_w', 'delta_mem_norm_g': 'delta_w', 'delta_w_in': 'delta_w', 'delta_attn_q_norm': 'delta_w', 'delta_attn_k_norm': 'delta_w', 'delta_conv_w': 'delta_w', 'delta_mem_w_kv': 'delta_w', 'delta_mem_q_norm': 'delta_w', 'delta_mem_k_norm': 'delta_w', 'delta_w_br_attn': 'delta_w', 'delta_w_br_conv': 'delta_w', 'delta_w_br_mem': 'delta_w', 'delta_w_out': 'delta_w', 'new_m_norm_g': 'new_m', 'new_m_mem_norm_g': 'new_m', 'new_m_w_in': 'new_m', 'new_m_attn_q_norm': 'new_m', 'new_m_attn_k_norm': 'new_m', 'new_m_conv_w': 'new_m', 'new_m_mem_w_kv': 'new_m', 'new_m_mem_q_norm': 'new_m', 'new_m_mem_k_norm': 'new_m', 'new_m_w_br_attn': 'new_m', 'new_m_w_br_conv': 'new_m', 'new_m_w_br_mem': 'new_m', 'new_m_w_out': 'new_m', 'new_v_norm_g': 'new_v', 'new_v_mem_norm_g': 'new_v', 'new_v_w_in': 'new_v', 'new_v_attn_q_norm': 'new_v', 'new_v_attn_k_norm': 'new_v', 'new_v_conv_w': 'new_v', 'new_v_mem_w_kv': 'new_v', 'new_v_mem_q_norm': 'new_v', 'new_v_mem_k_norm': 'new_v', 'new_v_w_br_attn': 'new_v', 'new_v_w_br_conv': 'new_v', 'new_v_w_br_mem': 'new_v', 'new_v_w_out': 'new_v'}


def _forward(args):
    return _fwd_reference(*[args[k] for k in FWD_PARAMS])


def _output_shape():
    out = _jax.eval_shape(lambda: _forward(_fwd_setup_inputs(0)))
    return out.shape, out.dtype

N_MICROBATCH = 1
ADAM_LR = 0.001
ADAM_B1 = 0.9
ADAM_B2 = 0.999
ADAM_EPS = 1e-08
ADAM_WD = 0.01
ADAM_STEP = 10
PER_EXAMPLE_BATCH_AXIS = {'x': 0, 'mem': 0, 'loss_target': 0}
SHARED_INPUTS = []
_WEIGHT_DTYPES = {'norm_g': _jnp.float32, 'mem_norm_g': _jnp.float32, 'w_in': _jnp.float32, 'attn_q_norm': _jnp.float32, 'attn_k_norm': _jnp.float32, 'conv_w': _jnp.float32, 'mem_w_kv': _jnp.float32, 'mem_q_norm': _jnp.float32, 'mem_k_norm': _jnp.float32, 'w_br_attn': _jnp.float32, 'w_br_conv': _jnp.float32, 'w_br_mem': _jnp.float32, 'w_out': _jnp.float32}
MOMENT_SCALE = {'norm_g': 6.915830e+00, 'mem_norm_g': 1.403278e-02, 'w_in': 9.377028e-02, 'attn_q_norm': 4.906805e-02, 'attn_k_norm': 4.899751e-02, 'conv_w': 1.746656e+00, 'mem_w_kv': 6.917050e-03, 'mem_q_norm': 1.213095e-01, 'mem_k_norm': 1.209440e-01, 'w_br_attn': 7.873478e-03, 'w_br_conv': 6.645585e-02, 'w_br_mem': 4.811807e-03, 'w_out': 5.576065e-02}


def _to_microbatches(a, axis):
    t = _jnp.moveaxis(a, axis, 0)
    t = t.reshape((N_MICROBATCH, t.shape[0] // N_MICROBATCH) + t.shape[1:])
    return _jnp.moveaxis(t, 1, axis + 1)


def setup_inputs(seed: int = 0) -> dict:
    inp = _fwd_setup_inputs(seed)
    key = _jax.random.fold_in(_jax.random.key(seed), 7919)
    shape, _ = _output_shape()
    out = dict(inp)
    out["loss_target"] = _jax.random.normal(_jax.random.fold_in(key, 0), shape, _jnp.float32)
    for i, name in enumerate(TWIN_WEIGHTS):
        w = inp[name].astype(_jnp.float32)
        if MOMENT_SCALE is None:
            s = _jnp.sqrt(_jnp.mean(_jnp.square(w)) + 1e-30)
        else:
            s = MOMENT_SCALE[name]
        km, kv = _jax.random.split(_jax.random.fold_in(key, i + 1))
        out[name] = w
        out["m_" + name] = s * _jax.random.normal(km, w.shape, _jnp.float32)
        out["v_" + name] = (s * s) * _jax.random.uniform(kv, w.shape, _jnp.float32, 0.5, 1.5)
    if N_MICROBATCH > 1:
        for name, axis in PER_EXAMPLE_BATCH_AXIS.items():
            out[name] = _to_microbatches(out[name], axis)
    return {'x': out['x'], 'mem': out['mem'], 'norm_g': out['norm_g'], 'mem_norm_g': out['mem_norm_g'], 'w_in': out['w_in'], 'attn_q_norm': out['attn_q_norm'], 'attn_k_norm': out['attn_k_norm'], 'conv_w': out['conv_w'], 'mem_w_kv': out['mem_w_kv'], 'mem_q_norm': out['mem_q_norm'], 'mem_k_norm': out['mem_k_norm'], 'w_br_attn': out['w_br_attn'], 'w_br_conv': out['w_br_conv'], 'w_br_mem': out['w_br_mem'], 'w_out': out['w_out'], 'loss_target': out['loss_target'], 'm_norm_g': out['m_norm_g'], 'm_mem_norm_g': out['m_mem_norm_g'], 'm_w_in': out['m_w_in'], 'm_attn_q_norm': out['m_attn_q_norm'], 'm_attn_k_norm': out['m_attn_k_norm'], 'm_conv_w': out['m_conv_w'], 'm_mem_w_kv': out['m_mem_w_kv'], 'm_mem_q_norm': out['m_mem_q_norm'], 'm_mem_k_norm': out['m_mem_k_norm'], 'm_w_br_attn': out['m_w_br_attn'], 'm_w_br_conv': out['m_w_br_conv'], 'm_w_br_mem': out['m_w_br_mem'], 'm_w_out': out['m_w_out'], 'v_norm_g': out['v_norm_g'], 'v_mem_norm_g': out['v_mem_norm_g'], 'v_w_in': out['v_w_in'], 'v_attn_q_norm': out['v_attn_q_norm'], 'v_attn_k_norm': out['v_attn_k_norm'], 'v_conv_w': out['v_conv_w'], 'v_mem_w_kv': out['v_mem_w_kv'], 'v_mem_q_norm': out['v_mem_q_norm'], 'v_mem_k_norm': out['v_mem_k_norm'], 'v_w_br_attn': out['v_w_br_attn'], 'v_w_br_conv': out['v_w_br_conv'], 'v_w_br_mem': out['v_w_br_mem'], 'v_w_out': out['v_w_out']}


def _loss(weights, diff, rest, loss_target):
    with _jax.named_scope("forward"):
        args = {**rest, TWIN_DIFF_INPUT: diff, **{k: w.astype(_WEIGHT_DTYPES[k]) for k, w in weights.items()}}
        y = _forward(args)
    with _jax.named_scope("loss_head"):
        err = _jnp.square(y.astype(_jnp.float32) - loss_target)
        return 0.5 * _jnp.sum(_jnp.mean(err, axis=-1)) if err.ndim else 0.5 * err


def _adamw(w, g, m, v):
    m = ADAM_B1 * m + (1.0 - ADAM_B1) * g
    v = ADAM_B2 * v + (1.0 - ADAM_B2) * _jnp.square(g)
    m_hat = m / (1.0 - ADAM_B1 ** ADAM_STEP)
    v_hat = v / (1.0 - ADAM_B2 ** ADAM_STEP)
    delta = -ADAM_LR * (m_hat / (_jnp.sqrt(v_hat) + ADAM_EPS) + ADAM_WD * w)
    return delta, m, v


def reference(x, mem, norm_g, mem_norm_g, w_in, attn_q_norm, attn_k_norm, conv_w, mem_w_kv, mem_q_norm, mem_k_norm, w_br_attn, w_br_conv, w_br_mem, w_out, loss_target, m_norm_g, m_mem_norm_g, m_w_in, m_attn_q_norm, m_attn_k_norm, m_conv_w, m_mem_w_kv, m_mem_q_norm, m_mem_k_norm, m_w_br_attn, m_w_br_conv, m_w_br_mem, m_w_out, v_norm_g, v_mem_norm_g, v_w_in, v_attn_q_norm, v_attn_k_norm, v_conv_w, v_mem_w_kv, v_mem_q_norm, v_mem_k_norm, v_w_br_attn, v_w_br_conv, v_w_br_mem, v_w_out):
    given = dict(x=x, mem=mem, norm_g=norm_g, mem_norm_g=mem_norm_g, w_in=w_in, attn_q_norm=attn_q_norm, attn_k_norm=attn_k_norm, conv_w=conv_w, mem_w_kv=mem_w_kv, mem_q_norm=mem_q_norm, mem_k_norm=mem_k_norm, w_br_attn=w_br_attn, w_br_conv=w_br_conv, w_br_mem=w_br_mem, w_out=w_out, loss_target=loss_target, m_norm_g=m_norm_g, m_mem_norm_g=m_mem_norm_g, m_w_in=m_w_in, m_attn_q_norm=m_attn_q_norm, m_attn_k_norm=m_attn_k_norm, m_conv_w=m_conv_w, m_mem_w_kv=m_mem_w_kv, m_mem_q_norm=m_mem_q_norm, m_mem_k_norm=m_mem_k_norm, m_w_br_attn=m_w_br_attn, m_w_br_conv=m_w_br_conv, m_w_br_mem=m_w_br_mem, m_w_out=m_w_out, v_norm_g=v_norm_g, v_mem_norm_g=v_mem_norm_g, v_w_in=v_w_in, v_attn_q_norm=v_attn_q_norm, v_attn_k_norm=v_attn_k_norm, v_conv_w=v_conv_w, v_mem_w_kv=v_mem_w_kv, v_mem_q_norm=v_mem_q_norm, v_mem_k_norm=v_mem_k_norm, v_w_br_attn=v_w_br_attn, v_w_br_conv=v_w_br_conv, v_w_br_mem=v_w_br_mem, v_w_out=v_w_out)
    weights = {n: given[n] for n in TWIN_WEIGHTS}
    shared = {n: given[n] for n in SHARED_INPUTS}
    per_example = {n: given[n] for n in ['x', 'mem']}
    grad_fn = _jax.value_and_grad(_loss, argnums=(0, 1))

    def one_microbatch(ex, loss_target):
        ex = dict(ex)
        diff = ex.pop(TWIN_DIFF_INPUT)
        return grad_fn(weights, diff, {**shared, **ex}, loss_target)

    if N_MICROBATCH == 1:
        loss, (grad_w, grad_x) = one_microbatch(per_example, given["loss_target"])
    else:
        def body(carry, xs):
            loss_sum, grad_sum = carry
            l_k, (gw_k, gx_k) = one_microbatch(xs[0], xs[1])
            with _jax.named_scope("update"):
                return (loss_sum + l_k, _jax.tree.map(_jnp.add, grad_sum, gw_k)), gx_k

        init = (_jnp.zeros((), _jnp.float32), _jax.tree.map(_jnp.zeros_like, weights))
        (loss, grad_w), grad_x = _jax.lax.scan(body, init, (per_example, given["loss_target"]))
    with _jax.named_scope("update"):
        delta_w, new_m, new_v = {}, {}, {}
        for n in TWIN_WEIGHTS:
            delta_w[n], new_m[n], new_v[n] = _adamw(weights[n], grad_w[n], given["m_" + n], given["v_" + n])
    return (loss, grad_x, *[grad_w[n] for n in TWIN_WEIGHTS], *[delta_w[n] for n in TWIN_WEIGHTS],
            *[new_m[n] for n in TWIN_WEIGHTS], *[new_v[n] for n in TWIN_WEIGHTS])
```

```python
import functools

import jax
import jax.numpy as jnp
from jax import lax
from jax.experimental import pallas as pl
from jax.experimental.pallas import tpu as pltpu

f32 = jnp.float32
bf16 = jnp.bfloat16

N_DEV = 8
HEAD_DIM = 128
DILATIONS = (1, 4, 16)
N_GROUPS = 3
HEADS = 4
BLK = 128
ATTN_QKV = N_GROUPS * HEADS * HEAD_DIM
ATTN_OUT = HEADS * HEAD_DIM
CONV_W = 1024
MEM_LEN = 256
MEM_HEADS = 4
MEM_HD = 256
MEM_W = MEM_HEADS * MEM_HD
EPS = 1e-6
Q0, K0, V0 = 0, ATTN_QKV, 2 * ATTN_QKV
ZA = 3 * ATTN_QKV
CB, CC, CV, ZC = ZA + ATTN_OUT, ZA + ATTN_OUT + CONV_W, ZA + ATTN_OUT + 2 * CONV_W, ZA + ATTN_OUT + 3 * CONV_W
MQ = ZC + CONV_W
ZM = MQ + MEM_W
GT = ZM + MEM_W

ADAM_LR, ADAM_B1, ADAM_B2, ADAM_EPS, ADAM_WD, ADAM_STEP = 0.001, 0.9, 0.999, 1e-08, 0.01, 10

VMEM_LIMIT = 56 * 1024 * 1024
LANES = 128

NT_DIMS = (((1,), (1,)), ((), ()))
TN_DIMS = (((0,), (0,)), ((), ()))
NN_DIMS = (((1,), (0,)), ((), ()))


def _params(sem=None):
    return pltpu.CompilerParams(dimension_semantics=sem, vmem_limit_bytes=VMEM_LIMIT)


def _pick(n, pref, q=LANES):
    t = (min(pref, n) // q) * q
    while t >= q:
        if n % t == 0:
            return t
        t -= q
    return n


def _dot(a, b, dims):
    return lax.dot_general(a.astype(bf16), b.astype(bf16), dims, preferred_element_type=f32)


def _sigmoid(z):
    return 1.0 / (1.0 + jnp.exp(-z))


def _rstd(v):
    return lax.rsqrt(jnp.mean(v * v, axis=-1, keepdims=True) + EPS)


def _mm(a, b, *, mode, out_dtype, name, tm=1024, tn=1024, tk=512):
    if mode == "nn":
        (M, K), (K2, N) = a.shape, b.shape
    elif mode == "nt":
        (M, K), (N, K2) = a.shape, b.shape
    else:
        (K, M), (K2, N) = a.shape, b.shape
    assert K == K2
    tm, tn, tk = _pick(M, tm), _pick(N, tn), _pick(K, tk)
    nk = K // tk
    dims = {"nn": NN_DIMS, "nt": NT_DIMS, "tn": TN_DIMS}[mode]

    def body(a_ref, b_ref, o_ref, acc_ref, *, nk):
        k = pl.program_id(2)
        part = _dot(a_ref[...], b_ref[...], dims)

        @pl.when(k == 0)
        def _():
            acc_ref[...] = part

        @pl.when(k > 0)
        def _():
            acc_ref[...] += part

        @pl.when(k == nk - 1)
        def _():
            o_ref[...] = acc_ref[...].astype(o_ref.dtype)

    if mode == "tn":
        a_spec = pl.BlockSpec((tk, tm), lambda i, j, k: (k, i))
    else:
        a_spec = pl.BlockSpec((tm, tk), lambda i, j, k: (i, k))
    if mode == "nt":
        b_spec = pl.BlockSpec((tn, tk), lambda i, j, k: (j, k))
    else:
        b_spec = pl.BlockSpec((tk, tn), lambda i, j, k: (k, j))
    return pl.pallas_call(
        functools.partial(body, nk=nk),
        grid=(M // tm, N // tn, nk),
        in_specs=[a_spec, b_spec],
        out_specs=pl.BlockSpec((tm, tn), lambda i, j, k: (i, j)),
        out_shape=jax.ShapeDtypeStruct((M, N), out_dtype),
        scratch_shapes=[pltpu.VMEM((tm, tn), f32)],
        compiler_params=_params(("parallel", "parallel", "arbitrary")),
        name=name,
    )(a, b)


def _rms_fwd(x, g, name):
    T, D = x.shape
    tm = _pick(T, 256, 8)

    def body(x_ref, g_ref, h_ref):
        xv = x_ref[...]
        h_ref[...] = (xv * _rstd(xv) * g_ref[...]).astype(bf16)

    return pl.pallas_call(
        body, grid=(T // tm,),
        in_specs=[pl.BlockSpec((tm, D), lambda i: (i, 0)), pl.BlockSpec((1, D), lambda i: (0, 0))],
        out_specs=pl.BlockSpec((tm, D), lambda i: (i, 0)),
        out_shape=jax.ShapeDtypeStruct((T, D), bf16),
        compiler_params=_params(("parallel",)), name=name,
    )(x, g)


def _rms_bwd(x, g, dh, dy, name):
    T, D = x.shape
    tm = _pick(T, 256, 8)
    with_dx = dy is not None

    def body(*refs):
        if with_dx:
            x_ref, g_ref, dh_ref, dy_ref, dx_ref, dg_ref = refs
        else:
            x_ref, g_ref, dh_ref, dg_ref = refs
        xv = x_ref[...]
        r = _rstd(xv)
        xh = xv * r
        dhv = dh_ref[...]
        part = jnp.sum(dhv * xh, axis=0, keepdims=True)

        @pl.when(pl.program_id(0) == 0)
        def _():
            dg_ref[...] = part

        @pl.when(pl.program_id(0) > 0)
        def _():
            dg_ref[...] += part

        if with_dx:
            dxh = dhv * g_ref[...]
            dx_ref[...] = dy_ref[...] + r * (dxh - xh * jnp.mean(dxh * xh, axis=-1, keepdims=True))

    row = pl.BlockSpec((tm, D), lambda i: (i, 0))
    vec = pl.BlockSpec((1, D), lambda i: (0, 0))
    if with_dx:
        return pl.pallas_call(
            body, grid=(T // tm,), in_specs=[row, vec, row, row], out_specs=[row, vec],
            out_shape=[jax.ShapeDtypeStruct((T, D), f32), jax.ShapeDtypeStruct((1, D), f32)],
            compiler_params=_params(("arbitrary",)), name=name,
        )(x, g, dh, dy)
    return pl.pallas_call(
        body, grid=(T // tm,), in_specs=[row, vec, row], out_specs=vec,
        out_shape=jax.ShapeDtypeStruct((1, D), f32),
        compiler_params=_params(("arbitrary",)), name=name,
    )(x, g, dh)


def _rows(start, size, stride):
    return pl.ds(start, size) if stride == 1 else pl.ds(start, size, stride=stride)


def _attn_units(S, d, first, prev):
    nb = S // d // BLK

    def do_first(r, c):
        first(_rows(r, BLK, d))
        return c

    lax.fori_loop(0, d, do_first, 0)
    if nb > 1:
        def do_prev(u, c):
            r = u // (nb - 1)
            b = u % (nb - 1) + 1
            base = r + d * b * BLK
            prev(_rows(base, BLK, d), _rows(base - d * BLK, 2 * BLK, d))
            return c

        lax.fori_loop(0, d * (nb - 1), do_prev, 0)


def _band_mask(nkeys):
    i = lax.broadcasted_iota(jnp.int32, (BLK, nkeys), 0)
    j = lax.broadcasted_iota(jnp.int32, (BLK, nkeys), 1)
    if nkeys == BLK:
        return j <= i
    return (j >= i) & (j <= i + BLK)


def _normalize_into(src_ref, gain, dst_ref, S):
    for c in range(S // 256):
        rows = pl.ds(c * 256, 256)
        v = src_ref[rows, :]
        dst_ref[rows, :] = v * _rstd(v) * gain


def _attn_fwd(proj, gq, gk, B, S):
    T, NC = proj.shape
    scale = HEAD_DIM ** -0.5

    def body(q_ref, k_ref, v_ref, z_ref, gq_ref, gk_ref, ag_ref, a_ref, lse_ref, qs, ks, o0, o1, o2, l0, l1, l2):
        g = pl.program_id(2)
        _normalize_into(q_ref, gq_ref[pl.ds(g, 1), :], qs, S)
        _normalize_into(k_ref, gk_ref[pl.ds(g, 1), :], ks, S)
        o_s, l_s = (o0, o1, o2), (l0, l1, l2)

        def run(gi):
            def unit(qr, kr, nkeys):
                s = _dot(qs[qr, :], ks[kr, :], NT_DIMS) * scale
                s = jnp.where(_band_mask(nkeys), s, -jnp.inf)
                m = jnp.max(s, axis=-1, keepdims=True)
                p = jnp.exp(s - m)
                den = jnp.sum(p, axis=-1, keepdims=True)
                o_s[gi][qr, :] = _dot(p / den, v_ref[kr, :], NN_DIMS)
                l_s[gi][qr, :] = jnp.broadcast_to(m + jnp.log(den), (BLK, HEAD_DIM))

            _attn_units(S, DILATIONS[gi], lambda qr: unit(qr, qr, BLK), lambda qr, kr: unit(qr, kr, 2 * BLK))

        for gi in range(N_GROUPS):
            pl.when(g == gi)(functools.partial(run, gi))

        @pl.when(g == N_GROUPS - 1)
        def _():
            for c in range(S // 256):
                rows = pl.ds(c * 256, 256)
                la, lb, lc = l0[rows, :], l1[rows, :], l2[rows, :]
                m = jnp.maximum(jnp.maximum(la, lb), lc)
                wa, wb, wc = jnp.exp(la - m), jnp.exp(lb - m), jnp.exp(lc - m)
                tot = wa + wb + wc
                a = (wa / tot) * o0[rows, :] + (wb / tot) * o1[rows, :] + (wc / tot) * o2[rows, :]
                z = z_ref[rows, :]
                a_ref[rows, :] = a
                lse_ref[rows, :] = m + jnp.log(tot)
                ag_ref[rows, :] = (a * (z * _sigmoid(z))).astype(bf16)

    def slab(col0):
        return pl.BlockSpec((S, HEAD_DIM), lambda b, h, g: (b, col0 // HEAD_DIM + g * HEADS + h))

    gain = pl.BlockSpec((N_GROUPS, HEAD_DIM), lambda b, h, g: (0, 0))
    out = pl.BlockSpec((S, HEAD_DIM), lambda b, h, g: (b, h))
    return pl.pallas_call(
        body, grid=(B, HEADS, N_GROUPS),
        in_specs=[slab(Q0), slab(K0), slab(V0), pl.BlockSpec((S, HEAD_DIM), lambda b, h, g: (b, ZA // HEAD_DIM + h)), gain, gain],
        out_specs=[out, out, out],
        out_shape=[jax.ShapeDtypeStruct((T, ATTN_OUT), bf16), jax.ShapeDtypeStruct((T, ATTN_OUT), f32),
                   jax.ShapeDtypeStruct((T, ATTN_OUT), f32)],
        scratch_shapes=[pltpu.VMEM((S, HEAD_DIM), f32)] * 8,
        compiler_params=_params(("arbitrary", "arbitrary", "arbitrary")), name="attn_fwd",
    )(proj, proj, proj, proj, gq, gk)


def _rms_bwd_rows(raw, gain, dn):
    r = _rstd(raw)
    xh = raw * r
    dxh = dn * gain
    return r * (dxh - xh * jnp.mean(dxh * xh, axis=-1, keepdims=True)), jnp.sum(dn * xh, axis=0, keepdims=True)


def _attn_bwd(proj, gq, gk, a_comb, lse, dag, dproj, B, S):
    T, NC = proj.shape
    scale = HEAD_DIM ** -0.5

    def body(q_ref, k_ref, v_ref, z_ref, gq_ref, gk_ref, a_ref, lse_ref, dag_ref, dproj_in, dproj_ref, dgq_ref, dgk_ref,
             qs, ks, da_s, dl_s, dq_s, dk_s, dv_s, stage, sem):
        b, h, g = pl.program_id(0), pl.program_id(1), pl.program_id(2)
        gq_row, gk_row = gq_ref[pl.ds(g, 1), :], gk_ref[pl.ds(g, 1), :]
        _normalize_into(q_ref, gq_row, qs, S)
        _normalize_into(k_ref, gk_row, ks, S)

        def put(slot, col0):
            cp = pltpu.make_async_copy(stage.at[slot], dproj_ref.at[pl.ds(b * S, S), pl.ds(col0, HEAD_DIM)], sem.at[slot])
            cp.start()
            return cp

        @pl.when((b == 0) & (h == 0) & (g == 0))
        def _():
            dgq_ref[...] = jnp.zeros_like(dgq_ref)
            dgk_ref[...] = jnp.zeros_like(dgk_ref)

        @pl.when(g == 0)
        def _():
            for c in range(S // 256):
                rows = pl.ds(c * 256, 256)
                z, a, dg_ = z_ref[rows, :], a_ref[rows, :], dag_ref[rows, :]
                sg = _sigmoid(z)
                da = dg_ * (z * sg)
                da_s[rows, :] = da
                dl_s[rows, :] = jnp.broadcast_to(jnp.sum(da * a, axis=-1, keepdims=True), (256, HEAD_DIM))
                stage[3, rows, :] = (dg_ * a * (sg * (1.0 + z * (1.0 - sg)))).astype(bf16)
            put(3, ZA + h * HEAD_DIM).wait()

        dk_s[...] = jnp.zeros_like(dk_s)
        dv_s[...] = jnp.zeros_like(dv_s)

        def run(gi):
            def unit(qr, kr, nkeys):
                q, k, v = qs[qr, :], ks[kr, :], v_ref[kr, :]
                s = _dot(q, k, NT_DIMS) * scale
                p = jnp.where(_band_mask(nkeys), jnp.exp(s - lse_ref[qr, :][:, :1]), 0.0)
                da = da_s[qr, :]
                dp = _dot(da, v, NT_DIMS)
                ds = p * (dp - dl_s[qr, :][:, :1]) * scale
                dq_s[qr, :] = _dot(ds, k, NN_DIMS)
                dk_s[kr, :] += _dot(ds, q, TN_DIMS)
                dv_s[kr, :] += _dot(p, da, TN_DIMS)

            _attn_units(S, DILATIONS[gi], lambda qr: unit(qr, qr, BLK), lambda qr, kr: unit(qr, kr, 2 * BLK))

        for gi in range(N_GROUPS):
            pl.when(g == gi)(functools.partial(run, gi))

        gq_acc = jnp.zeros((1, HEAD_DIM), f32)
        gk_acc = jnp.zeros((1, HEAD_DIM), f32)
        for c in range(S // 256):
            rows = pl.ds(c * 256, 256)
            dq, gq_p = _rms_bwd_rows(q_ref[rows, :], gq_row, dq_s[rows, :])
            dk, gk_p = _rms_bwd_rows(k_ref[rows, :], gk_row, dk_s[rows, :])
            gq_acc, gk_acc = gq_acc + gq_p, gk_acc + gk_p
            stage[0, rows, :] = dq.astype(bf16)
            stage[1, rows, :] = dk.astype(bf16)
            stage[2, rows, :] = dv_s[rows, :].astype(bf16)
        dgq_ref[pl.ds(g, 1), :] += gq_acc
        dgk_ref[pl.ds(g, 1), :] += gk_acc
        col = (g * HEADS + h) * HEAD_DIM
        cps = [put(0, Q0 + col), put(1, K0 + col), put(2, V0 + col)]
        for cp in cps:
            cp.wait()

    def slab(col0):
        return pl.BlockSpec((S, HEAD_DIM), lambda b, h, g: (b, col0 // HEAD_DIM + g * HEADS + h))

    gain = pl.BlockSpec((N_GROUPS, HEAD_DIM), lambda b, h, g: (0, 0))
    per_slot = pl.BlockSpec((S, HEAD_DIM), lambda b, h, g: (b, h))
    return pl.pallas_call(
        body, grid=(B, HEADS, N_GROUPS),
        in_specs=[slab(Q0), slab(K0), slab(V0), pl.BlockSpec((S, HEAD_DIM), lambda b, h, g: (b, ZA // HEAD_DIM + h)), gain, gain,
                  per_slot, per_slot, per_slot, pl.BlockSpec(memory_space=pl.ANY)],
        out_specs=[pl.BlockSpec(memory_space=pl.ANY), gain, gain],
        out_shape=[jax.ShapeDtypeStruct(dproj.shape, dproj.dtype), jax.ShapeDtypeStruct((N_GROUPS, HEAD_DIM), f32),
                   jax.ShapeDtypeStruct((N_GROUPS, HEAD_DIM), f32)],
        scratch_shapes=[pltpu.VMEM((S, HEAD_DIM), f32)] * 7 + [pltpu.VMEM((4, S, HEAD_DIM), bf16), pltpu.SemaphoreType.DMA((4,))],
        input_output_aliases={9: 0},
        compiler_params=_params(("arbitrary", "arbitrary", "arbitrary")), name="attn_bwd",
    )(proj, proj, proj, proj, gq, gk, a_comb, lse, dag, dproj)


def _conv_fwd(proj, conv_w, B, S):
    T, NC = proj.shape
    tc = LANES

    def body(cb_ref, cc_ref, cv_ref, z_ref, w_ref, c_ref, ub):
        u = cc_ref[...] * cv_ref[...]
        ub[0:8, :] = jnp.zeros((8, tc), f32)
        ub[8:8 + S, :] = u
        w = w_ref[...]
        y = w[0:1] * u + w[1:2] * ub[pl.ds(7, S), :] + w[2:3] * ub[pl.ds(6, S), :]
        z = z_ref[...]
        c_ref[...] = (cb_ref[...] * y * (z * _sigmoid(z))).astype(bf16)

    def seg(col0):
        return pl.BlockSpec((S, tc), lambda j, b: (b, col0 // tc + j))

    return pl.pallas_call(
        body, grid=(CONV_W // tc, B),
        in_specs=[seg(CB), seg(CC), seg(CV), seg(ZC), pl.BlockSpec((3, tc), lambda j, b: (0, j))],
        out_specs=pl.BlockSpec((S, tc), lambda j, b: (b, j)),
        out_shape=jax.ShapeDtypeStruct((T, CONV_W), bf16),
        scratch_shapes=[pltpu.VMEM((S + 8, tc), f32)],
        compiler_params=_params(("parallel", "parallel")), name="conv_fwd",
    )(proj, proj, proj, proj, conv_w)


def _conv_bwd(proj, conv_w, dc, dproj, B, S):
    T, NC = proj.shape
    tc = LANES

    def body(cb_ref, cc_ref, cv_ref, z_ref, w_ref, dc_ref, dproj_in, dproj_ref, dw_ref, ub, db, stage, sem):
        j, b = pl.program_id(0), pl.program_id(1)
        cb, cc, cv, z, dcv = cb_ref[...], cc_ref[...], cv_ref[...], z_ref[...], dc_ref[...]
        u = cc * cv
        ub[0:8, :] = jnp.zeros((8, tc), f32)
        ub[8:8 + S, :] = u
        u1, u2 = ub[pl.ds(7, S), :], ub[pl.ds(6, S), :]
        w = w_ref[...]
        y = w[0:1] * u + w[1:2] * u1 + w[2:3] * u2
        sg = _sigmoid(z)
        si = z * sg
        dyv = dcv * cb * si
        db[0:S, :] = dyv
        db[S:S + 8, :] = jnp.zeros((8, tc), f32)
        du = w[0:1] * dyv + w[1:2] * db[pl.ds(1, S), :] + w[2:3] * db[pl.ds(2, S), :]
        stage[0] = (dcv * y * si).astype(bf16)
        stage[1] = (du * cv).astype(bf16)
        stage[2] = (du * cc).astype(bf16)
        stage[3] = (dcv * cb * y * (sg * (1.0 + z * (1.0 - sg)))).astype(bf16)
        cps = []
        for slot, col0 in enumerate((CB, CC, CV, ZC)):
            cp = pltpu.make_async_copy(stage.at[slot], dproj_ref.at[pl.ds(b * S, S), pl.ds(col0 + j * tc, tc)], sem.at[slot])
            cp.start()
            cps.append(cp)
        part = jnp.concatenate([jnp.sum(dyv * u, axis=0, keepdims=True), jnp.sum(dyv * u1, axis=0, keepdims=True),
                                jnp.sum(dyv * u2, axis=0, keepdims=True)], axis=0)

        @pl.when(b == 0)
        def _():
            dw_ref[...] = part

        @pl.when(b > 0)
        def _():
            dw_ref[...] += part

        for cp in cps:
            cp.wait()

    def seg(col0):
        return pl.BlockSpec((S, tc), lambda j, b: (b, col0 // tc + j))

    return pl.pallas_call(
        body, grid=(CONV_W // tc, B),
        in_specs=[seg(CB), seg(CC), seg(CV), seg(ZC), pl.BlockSpec((3, tc), lambda j, b: (0, j)),
                  pl.BlockSpec((S, tc), lambda j, b: (b, j)), pl.BlockSpec(memory_space=pl.ANY)],
        out_specs=[pl.BlockSpec(memory_space=pl.ANY), pl.BlockSpec((3, tc), lambda j, b: (0, j))],
        out_shape=[jax.ShapeDtypeStruct(dproj.shape, dproj.dtype), jax.ShapeDtypeStruct((3, CONV_W), f32)],
        scratch_shapes=[pltpu.VMEM((S + 8, tc), f32), pltpu.VMEM((S + 8, tc), f32), pltpu.VMEM((4, S, tc), bf16),
                        pltpu.SemaphoreType.DMA((4,))],
        input_output_aliases={6: 0},
        compiler_params=_params(("arbitrary", "arbitrary")), name="conv_bwd",
    )(proj, proj, proj, proj, conv_w, dc, dproj)


MEM_CHUNK = 256


def _mem_fwd(proj, mkv, gq, gk, B, S):
    T, NC = proj.shape
    scale = MEM_HD ** -0.5

    def body(q_ref, z_ref, k_ref, v_ref, gq_ref, gk_ref, o_ref):
        kv = k_ref[...]
        kn = (kv * _rstd(kv) * gk_ref[...]).astype(bf16)
        vv = v_ref[...].astype(bf16)

        def chunk(c, carry):
            rows = pl.ds(pl.multiple_of(c * MEM_CHUNK, MEM_CHUNK), MEM_CHUNK)
            q = q_ref[rows, :]
            qn = q * _rstd(q) * gq_ref[...]
            s = _dot(qn, kn, NT_DIMS) * scale
            p = jnp.exp(s - jnp.max(s, axis=-1, keepdims=True))
            p = p / jnp.sum(p, axis=-1, keepdims=True)
            z = z_ref[rows, :]
            o_ref[rows, :] = (_dot(p, vv, NN_DIMS) * (z * _sigmoid(z))).astype(bf16)
            return carry

        lax.fori_loop(0, S // MEM_CHUNK, chunk, 0)

    gain = pl.BlockSpec((1, MEM_HD), lambda b, h: (0, 0))
    return pl.pallas_call(
        body, grid=(B, MEM_HEADS),
        in_specs=[pl.BlockSpec((S, MEM_HD), lambda b, h: (b, MQ // MEM_HD + h)),
                  pl.BlockSpec((S, MEM_HD), lambda b, h: (b, ZM // MEM_HD + h)),
                  pl.BlockSpec((MEM_LEN, MEM_HD), lambda b, h: (b, h)),
                  pl.BlockSpec((MEM_LEN, MEM_HD), lambda b, h: (b, MEM_HEADS + h)), gain, gain],
        out_specs=pl.BlockSpec((S, MEM_HD), lambda b, h: (b, h)),
        out_shape=jax.ShapeDtypeStruct((T, MEM_W), bf16),
        compiler_params=_params(("parallel", "parallel")), name="mem_fwd",
    )(proj, proj, mkv, mkv, gq, gk)


def _mem_bwd(proj, mkv, gq, gk, dmo, dproj, B, S):
    T, NC = proj.shape
    scale = MEM_HD ** -0.5

    def body(q_ref, z_ref, k_ref, v_ref, gq_ref, gk_ref, dmo_ref, dproj_in, dproj_ref, dmk_ref, dmv_ref, dgq_ref, dgk_ref,
             dkn_s, dv_s, gq_s, stage, sem):
        b, h = pl.program_id(0), pl.program_id(1)
        kv = k_ref[...]
        kn = (kv * _rstd(kv) * gk_ref[...]).astype(bf16)
        vv = v_ref[...].astype(bf16)
        dkn_s[...] = jnp.zeros_like(dkn_s)
        dv_s[...] = jnp.zeros_like(dv_s)
        gq_s[...] = jnp.zeros_like(gq_s)

        def chunk(c, carry):
            rows = pl.ds(pl.multiple_of(c * MEM_CHUNK, MEM_CHUNK), MEM_CHUNK)
            q = q_ref[rows, :]
            qn = q * _rstd(q) * gq_ref[...]
            s = _dot(qn, kn, NT_DIMS) * scale
            p = jnp.exp(s - jnp.max(s, axis=-1, keepdims=True))
            p = p / jnp.sum(p, axis=-1, keepdims=True)
            mo = _dot(p, vv, NN_DIMS)
            z, dg_ = z_ref[rows, :], dmo_ref[rows, :]
            sg = _sigmoid(z)
            do = dg_ * (z * sg)
            stage[1, rows, :] = (dg_ * mo * (sg * (1.0 + z * (1.0 - sg)))).astype(bf16)
            dp = _dot(do, vv, NT_DIMS)
            ds = p * (dp - jnp.sum(do * mo, axis=-1, keepdims=True)) * scale
            dq, gq_p = _rms_bwd_rows(q, gq_ref[...], _dot(ds, kn, NN_DIMS))
            stage[0, rows, :] = dq.astype(bf16)
            gq_s[...] += gq_p
            dkn_s[...] += _dot(ds, qn, TN_DIMS)
            dv_s[...] += _dot(p, do, TN_DIMS)
            return carry

        lax.fori_loop(0, S // MEM_CHUNK, chunk, 0)
        cps = []
        for slot, col0 in enumerate((MQ, ZM)):
            cp = pltpu.make_async_copy(stage.at[slot], dproj_ref.at[pl.ds(b * S, S), pl.ds(col0 + h * MEM_HD, MEM_HD)], sem.at[slot])
            cp.start()
            cps.append(cp)
        dk, gk_p = _rms_bwd_rows(kv, gk_ref[...], dkn_s[...])
        dmk_ref[...] = dk
        dmv_ref[...] = dv_s[...]

        @pl.when((b == 0) & (h == 0))
        def _():
            dgq_ref[...] = gq_s[...]
            dgk_ref[...] = gk_p

        @pl.when((b > 0) | (h > 0))
        def _():
            dgq_ref[...] += gq_s[...]
            dgk_ref[...] += gk_p

        for cp in cps:
            cp.wait()

    gain = pl.BlockSpec((1, MEM_HD), lambda b, h: (0, 0))
    kvb = pl.BlockSpec((MEM_LEN, MEM_HD), lambda b, h: (b, h))
    return pl.pallas_call(
        body, grid=(B, MEM_HEADS),
        in_specs=[pl.BlockSpec((S, MEM_HD), lambda b, h: (b, MQ // MEM_HD + h)),
                  pl.BlockSpec((S, MEM_HD), lambda b, h: (b, ZM // MEM_HD + h)),
                  kvb, pl.BlockSpec((MEM_LEN, MEM_HD), lambda b, h: (b, MEM_HEADS + h)), gain, gain,
                  pl.BlockSpec((S, MEM_HD), lambda b, h: (b, h)), pl.BlockSpec(memory_space=pl.ANY)],
        out_specs=[pl.BlockSpec(memory_space=pl.ANY), kvb, kvb, gain, gain],
        out_shape=[jax.ShapeDtypeStruct(dproj.shape, dproj.dtype), jax.ShapeDtypeStruct((B * MEM_LEN, MEM_W), f32),
                   jax.ShapeDtypeStruct((B * MEM_LEN, MEM_W), f32), jax.ShapeDtypeStruct((1, MEM_HD), f32),
                   jax.ShapeDtypeStruct((1, MEM_HD), f32)],
        scratch_shapes=[pltpu.VMEM((MEM_LEN, MEM_HD), f32), pltpu.VMEM((MEM_LEN, MEM_HD), f32), pltpu.VMEM((1, MEM_HD), f32),
                        pltpu.VMEM((2, S, MEM_HD), bf16), pltpu.SemaphoreType.DMA((2,))],
        input_output_aliases={7: 0},
        compiler_params=_params(("arbitrary", "arbitrary")), name="mem_bwd",
    )(proj, proj, mkv, mkv, gq, gk, dmo, dproj)


def _merge_fwd(proj, ag, cg, mg, wa, wc, wm):
    T, NC = proj.shape
    D = wa.shape[1]
    tm, tn = _pick(T, 1024), _pick(D, 512)
    assert GT % tn == 0

    def body(ag_ref, cg_ref, mg_ref, wa_ref, wc_ref, wm_ref, g0_ref, g1_ref, g2_ref, mer_ref, ba_ref, bc_ref, bm_ref):
        ba = _dot(ag_ref[...], wa_ref[...], NN_DIMS)
        bc = _dot(cg_ref[...], wc_ref[...], NN_DIMS)
        bm = _dot(mg_ref[...], wm_ref[...], NN_DIMS)
        mer_ref[...] = (_sigmoid(g0_ref[...]) * ba + _sigmoid(g1_ref[...]) * bc + _sigmoid(g2_ref[...]) * bm).astype(bf16)
        ba_ref[...] = ba.astype(bf16)
        bc_ref[...] = bc.astype(bf16)
        bm_ref[...] = bm.astype(bf16)

    def act(w):
        return pl.BlockSpec((tm, w), lambda i, j: (i, 0))

    def wt(k):
        return pl.BlockSpec((k, tn), lambda i, j: (0, j))

    def gate(n):
        return pl.BlockSpec((tm, tn), lambda i, j: (i, (GT + n * D) // tn + j))

    out = pl.BlockSpec((tm, tn), lambda i, j: (i, j))
    return pl.pallas_call(
        body, grid=(T // tm, D // tn),
        in_specs=[act(ATTN_OUT), act(CONV_W), act(MEM_W), wt(ATTN_OUT), wt(CONV_W), wt(MEM_W), gate(0), gate(1), gate(2)],
        out_specs=[out] * 4, out_shape=[jax.ShapeDtypeStruct((T, D), bf16)] * 4,
        compiler_params=_params(("parallel", "parallel")), name="merge_fwd",
    )(ag, cg, mg, wa, wc, wm, proj, proj, proj)


def _out_loss(merged, w_out, x, tgt):
    T, D = x.shape
    tm, tn = _pick(T, 512), _pick(D, 512)

    def body(m_ref, w_ref, x_ref, t_ref, dy_ref, dyb_ref, lp_ref):
        e = x_ref[...] + _dot(m_ref[...], w_ref[...], NN_DIMS) - t_ref[...]
        dy = e / D
        dy_ref[...] = dy
        dyb_ref[...] = dy.astype(bf16)
        part = jnp.full((1, 8, LANES), jnp.sum(e * e), f32)

        @pl.when(pl.program_id(1) == 0)
        def _():
            lp_ref[...] = part

        @pl.when(pl.program_id(1) > 0)
        def _():
            lp_ref[...] += part

    tile = pl.BlockSpec((tm, tn), lambda i, j: (i, j))
    return pl.pallas_call(
        body, grid=(T // tm, D // tn),
        in_specs=[pl.BlockSpec((tm, D), lambda i, j: (i, 0)), pl.BlockSpec((D, tn), lambda i, j: (0, j)), tile, tile],
        out_specs=[tile, tile, pl.BlockSpec((1, 8, LANES), lambda i, j: (i, 0, 0))],
        out_shape=[jax.ShapeDtypeStruct((T, D), f32), jax.ShapeDtypeStruct((T, D), bf16),
                   jax.ShapeDtypeStruct((T // tm, 8, LANES), f32)],
        compiler_params=_params(("parallel", "arbitrary")), name="out_loss",
    )(merged, w_out, x, tgt)


def _merge_bwd(proj, dyb, w_out, ba, bc, bm):
    T, NC = proj.shape
    D = w_out.shape[0]
    tm, tn = _pick(T, 512), _pick(D, 512)
    assert GT % tn == 0

    def body(dy_ref, w_ref, ba_ref, bc_ref, bm_ref, g0_ref, g1_ref, g2_ref, da_ref, dc_ref, dm_ref, dproj_ref, stage, sem):
        i, j = pl.program_id(0), pl.program_id(1)
        dmer = _dot(dy_ref[...], w_ref[...], NT_DIMS)
        cps = []
        for n, (g_ref, br_ref, o_ref) in enumerate(((g0_ref, ba_ref, da_ref), (g1_ref, bc_ref, dc_ref), (g2_ref, bm_ref, dm_ref))):
            sg = _sigmoid(g_ref[...])
            o_ref[...] = (sg * dmer).astype(bf16)
            stage[n] = (dmer * br_ref[...].astype(f32) * (sg * (1.0 - sg))).astype(bf16)
            cp = pltpu.make_async_copy(stage.at[n], dproj_ref.at[pl.ds(i * tm, tm), pl.ds(GT + n * D + j * tn, tn)], sem.at[n])
            cp.start()
            cps.append(cp)
        for cp in cps:
            cp.wait()

    tile = pl.BlockSpec((tm, tn), lambda i, j: (i, j))

    def gate(n):
        return pl.BlockSpec((tm, tn), lambda i, j: (i, (GT + n * D) // tn + j))

    return pl.pallas_call(
        body, grid=(T // tm, D // tn),
        in_specs=[pl.BlockSpec((tm, D), lambda i, j: (i, 0)), pl.BlockSpec((tn, D), lambda i, j: (j, 0)), tile, tile, tile,
                  gate(0), gate(1), gate(2)],
        out_specs=[tile, tile, tile, pl.BlockSpec(memory_space=pl.ANY)],
        out_shape=[jax.ShapeDtypeStruct((T, D), bf16)] * 3 + [jax.ShapeDtypeStruct((T, NC), bf16)],
        scratch_shapes=[pltpu.VMEM((3, tm, tn), bf16), pltpu.SemaphoreType.DMA((3,))],
        compiler_params=_params(("arbitrary", "arbitrary")), name="merge_bwd",
    )(dyb, w_out, ba, bc, bm, proj, proj, proj)


def _local_step(x, mem, tgt, norm_g, mem_norm_g, attn_q_norm, attn_k_norm, conv_w, mem_q_norm, mem_k_norm,
                w_in, w_kv, w_a, w_c, w_m, w_out):
    B, S, D = x.shape
    T = B * S
    x2, t2, mem2 = x.reshape(T, D), tgt.reshape(T, D), mem.reshape(B * MEM_LEN, D)
    ng, mng = norm_g.reshape(1, D), mem_norm_g.reshape(1, D)
    mqn, mkn = mem_q_norm.reshape(1, MEM_HD), mem_k_norm.reshape(1, MEM_HD)

    h = _rms_fwd(x2, ng, "rms_x")
    proj = _mm(h, w_in, mode="nn", out_dtype=f32, name="proj")
    mh = _rms_fwd(mem2, mng, "rms_mem")
    mkv = _mm(mh, w_kv, mode="nn", out_dtype=f32, name="mkv")
    ag, a_comb, lse = _attn_fwd(proj, attn_q_norm, attn_k_norm, B, S)
    cg = _conv_fwd(proj, conv_w, B, S)
    mg = _mem_fwd(proj, mkv, mqn, mkn, B, S)
    merged, ba, bc, bm = _merge_fwd(proj, ag, cg, mg, w_a, w_c, w_m)
    dy, dyb, lp = _out_loss(merged, w_out, x2, t2)
    sq_err = jnp.sum(lp[:, 0, 0])

    g = {}
    g["w_out"] = _mm(merged, dyb, mode="tn", out_dtype=f32, name="dw_out")
    dba, dbc, dbm, dproj = _merge_bwd(proj, dyb, w_out, ba, bc, bm)
    g["w_br_attn"] = _mm(ag, dba, mode="tn", out_dtype=f32, name="dw_br_attn")
    g["w_br_conv"] = _mm(cg, dbc, mode="tn", out_dtype=f32, name="dw_br_conv")
    g["w_br_mem"] = _mm(mg, dbm, mode="tn", out_dtype=f32, name="dw_br_mem")
    dag = _mm(dba, w_a, mode="nt", out_dtype=f32, name="d_attn_out")
    dcg = _mm(dbc, w_c, mode="nt", out_dtype=f32, name="d_conv_out")
    dmg = _mm(dbm, w_m, mode="nt", out_dtype=f32, name="d_mem_out")
    dproj, g["attn_q_norm"], g["attn_k_norm"] = _attn_bwd(proj, attn_q_norm, attn_k_norm, a_comb, lse, dag, dproj, B, S)
    dproj, g["conv_w"] = _conv_bwd(proj, conv_w, dcg, dproj, B, S)
    dproj, dmk, dmv, dgmq, dgmk = _mem_bwd(proj, mkv, mqn, mkn, dmg, dproj, B, S)
    g["mem_q_norm"], g["mem_k_norm"] = dgmq.reshape(MEM_HD), dgmk.reshape(MEM_HD)
    dmkv = jnp.concatenate([dmk, dmv], axis=1)
    g["mem_w_kv"] = _mm(mh, dmkv, mode="tn", out_dtype=f32, name="dw_kv")
    dmh = _mm(dmkv, w_kv, mode="nt", out_dtype=f32, name="d_mem_h")
    g["mem_norm_g"] = _rms_bwd(mem2, mng, dmh, None, "rms_mem_bwd").reshape(D)
    g["w_in"] = _mm(h, dproj, mode="tn", out_dtype=f32, name="dw_in")
    dh = _mm(dproj, w_in, mode="nt", out_dtype=f32, name="d_h")
    dx, dng = _rms_bwd(x2, ng, dh, dy, "rms_x_bwd")
    g["norm_g"] = dng.reshape(D)
    return sq_err, dx, g


MESH = pl.DeviceIdType.MESH
HBM = pl.BlockSpec(memory_space=pl.ANY)


def _me():
    x, y, c = lax.axis_index("x"), lax.axis_index("y"), lax.axis_index("c")
    return (x, y, c), 4 * x + 2 * y + c


def _peer(k):
    (x, y, c), _ = _me()
    p = (1 - x if k & 4 else x, 1 - y if k & 2 else y, 1 - c if k & 1 else c)
    return p, 4 * p[0] + 2 * p[1] + p[2]


def _window(ref, axis, size, idx):
    if axis is None:
        return ref.at[idx]
    if axis == 0:
        return ref.at[pl.ds(idx * size, size), :]
    return ref.at[:, pl.ds(idx * size, size)]


def _gather_weights(shards, axes):
    n = len(shards)

    def full_shape(s, axis):
        if axis is None:
            return (N_DEV,) + s.shape
        return tuple(N_DEV * d if i == axis else d for i, d in enumerate(s.shape))

    def body(*refs):
        ins, outs = refs[:n], refs[n:2 * n]
        send_sems, recv_sems, local_sems = refs[2 * n:]
        (x, y, c), me = _me()
        sibling, sib_id = _peer(1)
        chips = [_peer(4), _peer(2), _peer(6)]

        def win(a, idx):
            return _window(outs[a], axes[a], shards[a].shape[axes[a]] if axes[a] is not None else None, idx)

        def copy(a, k, block, to, src=None):
            return pltpu.make_async_remote_copy(
                src_ref=win(a, block) if src is None else src, dst_ref=win(a, block),
                send_sem=send_sems.at[a, k], recv_sem=recv_sems.at[a, k], device_id=to, device_id_type=MESH)

        mine = [pltpu.make_async_copy(ins[a], win(a, me), local_sems.at[a]) for a in range(n)]
        for cp in mine:
            cp.start()
        first = []
        for a in range(n):
            first.append(copy(a, 0, me, sibling, src=ins[a]))
            first += [copy(a, 1 + j, me, dev, src=ins[a]) for j, (dev, _) in enumerate(chips)]
        for cp in first:
            cp.start()
        passed = []
        for j, (dev, dev_id) in enumerate(chips):
            for a in range(n):
                copy(a, 1 + j, dev_id, (x, y, c)).wait_recv()
                cp = copy(a, 4 + j, dev_id, sibling)
                cp.start()
                passed.append(cp)
        for a in range(n):
            copy(a, 0, sib_id, (x, y, c)).wait_recv()
        for j, (dev, dev_id) in enumerate(chips):
            for a in range(n):
                copy(a, 4 + j, dev_id + 1 - 2 * c, (x, y, c)).wait_recv()
        for cp in first + passed:
            cp.wait_send()
        for cp in mine:
            cp.wait()

    return pl.pallas_call(
        body, in_specs=[HBM] * n, out_specs=[HBM] * n,
        out_shape=[jax.ShapeDtypeStruct(full_shape(s, ax), s.dtype) for s, ax in zip(shards, axes)],
        scratch_shapes=[pltpu.SemaphoreType.DMA((n, 7)), pltpu.SemaphoreType.DMA((n, 7)), pltpu.SemaphoreType.DMA((n,))],
        name="gather_weights",
    )(*shards)


def _scatter_grads(grads, axes):
    n = len(grads)
    sizes = [g.shape[ax] // N_DEV for g, ax in zip(grads, axes)]

    def shard_shape(g, ax):
        return tuple(d // N_DEV if i == ax else d for i, d in enumerate(g.shape))

    def body(*refs):
        ins, lands = refs[:n], refs[n:2 * n]
        send_sems, recv_sems = refs[2 * n:]
        copies = []
        for k in range(1, N_DEV):
            dev, dev_id = _peer(k)
            for a in range(n):
                copies.append(pltpu.make_async_remote_copy(
                    src_ref=_window(ins[a], axes[a], sizes[a], dev_id), dst_ref=lands[a].at[k - 1],
                    send_sem=send_sems.at[a, k - 1], recv_sem=recv_sems.at[a, k - 1], device_id=dev, device_id_type=MESH))
        for cp in copies:
            cp.start()
        for cp in copies:
            cp.wait()

    return pl.pallas_call(
        body, in_specs=[HBM] * n, out_specs=[HBM] * n,
        out_shape=[jax.ShapeDtypeStruct((N_DEV - 1,) + shard_shape(g, ax), g.dtype) for g, ax in zip(grads, axes)],
        scratch_shapes=[pltpu.SemaphoreType.DMA((n, N_DEV - 1)), pltpu.SemaphoreType.DMA((n, N_DEV - 1))],
        name="scatter_grads",
    )(*grads)


def _allreduce_small(part):
    R = part.shape[0]

    def body(p_ref, o_ref, buf, send_sems, recv_sems):
        (x, y, c), me = _me()
        buf[me] = p_ref[...]
        copies = []
        for k in range(1, N_DEV):
            dev, dev_id = _peer(k)
            copies.append(pltpu.make_async_remote_copy(
                src_ref=p_ref, dst_ref=buf.at[me], send_sem=send_sems.at[k - 1], recv_sem=recv_sems.at[k - 1],
                device_id=dev, device_id_type=MESH))
        for cp in copies:
            cp.start()
        for k in range(1, N_DEV):
            _, dev_id = _peer(k)
            pltpu.make_async_remote_copy(
                src_ref=p_ref, dst_ref=buf.at[dev_id], send_sem=send_sems.at[k - 1], recv_sem=recv_sems.at[k - 1],
                device_id=(x, y, c), device_id_type=MESH).wait_recv()
        for cp in copies:
            cp.wait_send()
        acc = buf[0]
        for d in range(1, N_DEV):
            acc = acc + buf[d]
        o_ref[...] = acc

    return pl.pallas_call(
        body, in_specs=[pl.BlockSpec(memory_space=pltpu.VMEM)], out_specs=pl.BlockSpec(memory_space=pltpu.VMEM),
        out_shape=jax.ShapeDtypeStruct((R, LANES), f32),
        scratch_shapes=[pltpu.VMEM((N_DEV, R, LANES), f32), pltpu.SemaphoreType.DMA((N_DEV - 1,)), pltpu.SemaphoreType.DMA((N_DEV - 1,))],
        name="allreduce_small",
    )(part)


def _adamw_math(w, g, m, v):
    m2 = ADAM_B1 * m + (1.0 - ADAM_B1) * g
    v2 = ADAM_B2 * v + (1.0 - ADAM_B2) * (g * g)
    m_hat = m2 / (1.0 - ADAM_B1 ** ADAM_STEP)
    v_hat = v2 / (1.0 - ADAM_B2 ** ADAM_STEP)
    return -ADAM_LR * (m_hat / (jnp.sqrt(v_hat) + ADAM_EPS) + ADAM_WD * w), m2, v2


def _adamw_shard(grad, land, w, m, v, me, axis, name):
    R, C = w.shape
    tr = _pick(R, max(8, (128 * 1024) // C), 8)

    def body(me_ref, g_ref, l_ref, w_ref, m_ref, v_ref, go_ref, d_ref, mo_ref, vo_ref):
        g = g_ref[...]
        for k in range(N_DEV - 1):
            g = g + l_ref[k].astype(f32)
        d, m2, v2 = _adamw_math(w_ref[...], g, m_ref[...], v_ref[...])
        go_ref[...] = g
        d_ref[...] = d
        mo_ref[...] = m2
        vo_ref[...] = v2

    if axis == 0:
        g_spec = pl.BlockSpec((tr, C), lambda i, me_ref: (me_ref[0] * (R // tr) + i, 0))
    else:
        g_spec = pl.BlockSpec((tr, C), lambda i, me_ref: (i, me_ref[0]))
    blk = pl.BlockSpec((tr, C), lambda i, me_ref: (i, 0))
    return pl.pallas_call(
        body,
        grid_spec=pltpu.PrefetchScalarGridSpec(
            num_scalar_prefetch=1, grid=(R // tr,),
            in_specs=[g_spec, pl.BlockSpec((N_DEV - 1, tr, C), lambda i, me_ref: (0, i, 0)), blk, blk, blk],
            out_specs=[blk] * 4),
        out_shape=[jax.ShapeDtypeStruct((R, C), f32)] * 4,
        compiler_params=_params(("parallel",)), name=name,
    )(me, grad, land, w, m, v)


def _adamw_small(g, w, m, v):
    def body(g_ref, w_ref, m_ref, v_ref, d_ref, mo_ref, vo_ref):
        d, m2, v2 = _adamw_math(w_ref[...], g_ref[...], m_ref[...], v_ref[...])
        d_ref[...] = d
        mo_ref[...] = m2
        vo_ref[...] = v2

    return pl.pallas_call(body, out_shape=[jax.ShapeDtypeStruct(g.shape, f32)] * 3, name="adamw_small")(g, w, m, v)


def _pack_rows(parts, total_rows):
    rows = jnp.concatenate([p.reshape(-1, LANES) for p in parts], axis=0)
    return jnp.pad(rows, ((0, total_rows - rows.shape[0]), (0, 0)))


BIG = ("w_in", "mem_w_kv", "w_br_attn", "w_br_conv", "w_br_mem", "w_out")
BIG_AXIS = {"w_in": 1, "mem_w_kv": 0, "w_br_attn": 1, "w_br_conv": 1, "w_br_mem": 1, "w_out": 0}
SMALL = ("norm_g", "mem_norm_g", "attn_q_norm", "attn_k_norm", "mem_q_norm", "mem_k_norm")
ALL_W = ("norm_g", "mem_norm_g", "w_in", "attn_q_norm", "attn_k_norm", "conv_w", "mem_w_kv", "mem_q_norm", "mem_k_norm",
         "w_br_attn", "w_br_conv", "w_br_mem", "w_out")


def kernel(x, mem, norm_g, mem_norm_g, w_in, attn_q_norm, attn_k_norm, conv_w, mem_w_kv, mem_q_norm, mem_k_norm, w_br_attn, w_br_conv, w_br_mem, w_out, loss_target, m_norm_g, m_mem_norm_g, m_w_in, m_attn_q_norm, m_attn_k_norm, m_conv_w, m_mem_w_kv, m_mem_q_norm, m_mem_k_norm, m_w_br_attn, m_w_br_conv, m_w_br_mem, m_w_out, v_norm_g, v_mem_norm_g, v_w_in, v_attn_q_norm, v_attn_k_norm, v_conv_w, v_mem_w_kv, v_mem_q_norm, v_mem_k_norm, v_w_br_attn, v_w_br_conv, v_w_br_mem, v_w_out):
    w = dict(norm_g=norm_g, mem_norm_g=mem_norm_g, w_in=w_in, attn_q_norm=attn_q_norm, attn_k_norm=attn_k_norm, conv_w=conv_w,
             mem_w_kv=mem_w_kv, mem_q_norm=mem_q_norm, mem_k_norm=mem_k_norm, w_br_attn=w_br_attn, w_br_conv=w_br_conv,
             w_br_mem=w_br_mem, w_out=w_out)
    mo = dict(norm_g=m_norm_g, mem_norm_g=m_mem_norm_g, w_in=m_w_in, attn_q_norm=m_attn_q_norm, attn_k_norm=m_attn_k_norm,
              conv_w=m_conv_w, mem_w_kv=m_mem_w_kv, mem_q_norm=m_mem_q_norm, mem_k_norm=m_mem_k_norm, w_br_attn=m_w_br_attn,
              w_br_conv=m_w_br_conv, w_br_mem=m_w_br_mem, w_out=m_w_out)
    vo = dict(norm_g=v_norm_g, mem_norm_g=v_mem_norm_g, w_in=v_w_in, attn_q_norm=v_attn_q_norm, attn_k_norm=v_attn_k_norm,
              conv_w=v_conv_w, mem_w_kv=v_mem_w_kv, mem_q_norm=v_mem_q_norm, mem_k_norm=v_mem_k_norm, w_br_attn=v_w_br_attn,
              w_br_conv=v_w_br_conv, w_br_mem=v_w_br_mem, w_out=v_w_out)
    B, S, D = x.shape
    me = (4 * lax.axis_index("x") + 2 * lax.axis_index("y") + lax.axis_index("c")).astype(jnp.int32)

    conv_pad = jnp.pad(conv_w, ((0, 8 - conv_w.shape[0]), (0, 0)))
    full = _gather_weights([w[n].astype(bf16) for n in BIG] + [conv_pad], [BIG_AXIS[n] for n in BIG] + [None])
    wf = dict(zip(BIG, full[:-1]))
    conv_full = full[-1][:, :3, :].transpose(1, 0, 2).reshape(3, CONV_W)

    sq_err, dx, g = _local_step(x, mem, loss_target, norm_g, mem_norm_g, attn_q_norm, attn_k_norm, conv_full, mem_q_norm,
                                mem_k_norm, wf["w_in"], wf["mem_w_kv"], wf["w_br_attn"], wf["w_br_conv"], wf["w_br_mem"], wf["w_out"])

    lands = _scatter_grads([g[n] for n in BIG], [BIG_AXIS[n] for n in BIG])
    grad, delta, new_m, new_v = {}, {}, {}, {}
    for n, land in zip(BIG, lands):
        grad[n], delta[n], new_m[n], new_v[n] = _adamw_shard(g[n], land, w[n], mo[n], vo[n], me.reshape(1), BIG_AXIS[n], "adamw_" + n)

    n_rep = sum(w[n].size for n in SMALL) // LANES
    conv_rows = g["conv_w"].reshape(3, N_DEV, LANES).transpose(1, 0, 2).reshape(3 * N_DEV, LANES)
    n_rows = -(-(n_rep + 3 * N_DEV + 1) // 8) * 8
    part = _pack_rows([g[n] for n in SMALL] + [conv_rows, jnp.full((1, LANES), sq_err, f32)], n_rows)
    tot = _allreduce_small(part)
    loss = tot[n_rep + 3 * N_DEV, 0] * (0.5 / D)
    n_small = -(-(n_rep + 3) // 8) * 8
    g_small = _pack_rows([tot[:n_rep], lax.dynamic_slice(tot, (n_rep + 3 * me, 0), (3, LANES))], n_small)
    names = SMALL + ("conv_w",)
    d_s, m_s, v_s = _adamw_small(g_small, _pack_rows([w[n] for n in names], n_small), _pack_rows([mo[n] for n in names], n_small),
                                 _pack_rows([vo[n] for n in names], n_small))
    off = 0
    for n in names:
        r = w[n].size // LANES
        grad[n], delta[n] = g_small[off:off + r].reshape(w[n].shape), d_s[off:off + r].reshape(w[n].shape)
        new_m[n], new_v[n] = m_s[off:off + r].reshape(w[n].shape), v_s[off:off + r].reshape(w[n].shape)
        off += r
    return (loss, dx.reshape(B, S, D), *[grad[n] for n in ALL_W], *[delta[n] for n in ALL_W], *[new_m[n] for n in ALL_W],
            *[new_v[n] for n in ALL_W])
```

```python
import functools

import jax
import jax.numpy as jnp
from jax import lax
from jax.experimental import pallas as pl
from jax.experimental.pallas import tpu as pltpu

f32 = jnp.float32
bf16 = jnp.bfloat16

N_DEV = 8
HEAD_DIM = 128
DILATIONS = (1, 4, 16)
N_GROUPS = 3
HEADS = 4
BLK = 128
ATTN_QKV = N_GROUPS * HEADS * HEAD_DIM
ATTN_OUT = HEADS * HEAD_DIM
CONV_W = 1024
MEM_LEN = 256
MEM_HEADS = 4
MEM_HD = 256
MEM_W = MEM_HEADS * MEM_HD
EPS = 1e-6
Q0, K0, V0 = 0, ATTN_QKV, 2 * ATTN_QKV
ZA = 3 * ATTN_QKV
CB, CC, CV, ZC = ZA + ATTN_OUT, ZA + ATTN_OUT + CONV_W, ZA + ATTN_OUT + 2 * CONV_W, ZA + ATTN_OUT + 3 * CONV_W
MQ = ZC + CONV_W
ZM = MQ + MEM_W
GT = ZM + MEM_W

ADAM_LR, ADAM_B1, ADAM_B2, ADAM_EPS, ADAM_WD, ADAM_STEP = 0.001, 0.9, 0.999, 1e-08, 0.01, 10

VMEM_LIMIT = 56 * 1024 * 1024
LANES = 128

NT_DIMS = (((1,), (1,)), ((), ()))
TN_DIMS = (((0,), (0,)), ((), ()))
NN_DIMS = (((1,), (0,)), ((), ()))


def _params(sem=None):
    return pltpu.CompilerParams(dimension_semantics=sem, vmem_limit_bytes=VMEM_LIMIT)


def _pick(n, pref, q=LANES):
    t = (min(pref, n) // q) * q
    while t >= q:
        if n % t == 0:
            return t
        t -= q
    return n


def _dot(a, b, dims):
    return lax.dot_general(a.astype(bf16), b.astype(bf16), dims, preferred_element_type=f32)


def _sigmoid(z):
    return 1.0 / (1.0 + jnp.exp(-z))


def _rstd(v):
    return lax.rsqrt(jnp.mean(v * v, axis=-1, keepdims=True) + EPS)


def _mm(a, b, *, mode, out_dtype, name, tm=1024, tn=1024, tk=512):
    if mode == "nn":
        (M, K), (K2, N) = a.shape, b.shape
    elif mode == "nt":
        (M, K), (N, K2) = a.shape, b.shape
    else:
        (K, M), (K2, N) = a.shape, b.shape
    assert K == K2
    tm, tn, tk = _pick(M, tm), _pick(N, tn), _pick(K, tk)
    nk = K // tk
    dims = {"nn": NN_DIMS, "nt": NT_DIMS, "tn": TN_DIMS}[mode]

    def body(a_ref, b_ref, o_ref, acc_ref, *, nk):
        k = pl.program_id(2)
        part = _dot(a_ref[...], b_ref[...], dims)

        @pl.when(k == 0)
        def _():
            acc_ref[...] = part

        @pl.when(k > 0)
        def _():
            acc_ref[...] += part

        @pl.when(k == nk - 1)
        def _():
            o_ref[...] = acc_ref[...].astype(o_ref.dtype)

    if mode == "tn":
        a_spec = pl.BlockSpec((tk, tm), lambda i, j, k: (k, i))
    else:
        a_spec = pl.BlockSpec((tm, tk), lambda i, j, k: (i, k))
    if mode == "nt":
        b_spec = pl.BlockSpec((tn, tk), lambda i, j, k: (j, k))
    else:
        b_spec = pl.BlockSpec((tk, tn), lambda i, j, k: (k, j))
    return pl.pallas_call(
        functools.partial(body, nk=nk),
        grid=(M // tm, N // tn, nk),
        in_specs=[a_spec, b_spec],
        out_specs=pl.BlockSpec((tm, tn), lambda i, j, k: (i, j)),
        out_shape=jax.ShapeDtypeStruct((M, N), out_dtype),
        scratch_shapes=[pltpu.VMEM((tm, tn), f32)],
        compiler_params=_params(("parallel", "parallel", "arbitrary")),
        name=name,
    )(a, b)


def _rms_fwd(x, g, name):
    T, D = x.shape
    tm = _pick(T, 256, 8)

    def body(x_ref, g_ref, h_ref):
        xv = x_ref[...]
        h_ref[...] = (xv * _rstd(xv) * g_ref[...]).astype(bf16)

    return pl.pallas_call(
        body, grid=(T // tm,),
        in_specs=[pl.BlockSpec((tm, D), lambda i: (i, 0)), pl.BlockSpec((1, D), lambda i: (0, 0))],
        out_specs=pl.BlockSpec((tm, D), lambda i: (i, 0)),
        out_shape=jax.ShapeDtypeStruct((T, D), bf16),
        compiler_params=_params(("parallel",)), name=name,
    )(x, g)


def _rms_bwd(x, g, dh, dy, name):
    T, D = x.shape
    tm = _pick(T, 256, 8)
    with_dx = dy is not None

    def body(*refs):
        if with_dx:
            x_ref, g_ref, dh_ref, dy_ref, dx_ref, dg_ref = refs
        else:
            x_ref, g_ref, dh_ref, dg_ref = refs
        xv = x_ref[...]
        r = _rstd(xv)
        xh = xv * r
        dhv = dh_ref[...]
        part = jnp.sum(dhv * xh, axis=0, keepdims=True)

        @pl.when(pl.program_id(0) == 0)
        def _():
            dg_ref[...] = part

        @pl.when(pl.program_id(0) > 0)
        def _():
            dg_ref[...] += part

        if with_dx:
            dxh = dhv * g_ref[...]
            dx_ref[...] = dy_ref[...] + r * (dxh - xh * jnp.mean(dxh * xh, axis=-1, keepdims=True))

    row = pl.BlockSpec((tm, D), lambda i: (i, 0))
    vec = pl.BlockSpec((1, D), lambda i: (0, 0))
    if with_dx:
        return pl.pallas_call(
            body, grid=(T // tm,), in_specs=[row, vec, row, row], out_specs=[row, vec],
            out_shape=[jax.ShapeDtypeStruct((T, D), f32), jax.ShapeDtypeStruct((1, D), f32)],
            compiler_params=_params(("arbitrary",)), name=name,
        )(x, g, dh, dy)
    return pl.pallas_call(
        body, grid=(T // tm,), in_specs=[row, vec, row], out_specs=vec,
        out_shape=jax.ShapeDtypeStruct((1, D), f32),
        compiler_params=_params(("arbitrary",)), name=name,
    )(x, g, dh)


def _rows(start, size, stride):
    return pl.ds(start, size) if stride == 1 else pl.ds(start, size, stride=stride)


def _attn_units(S, d, first, prev):
    nb = S // d // BLK

    def do_first(r, c):
        first(_rows(r, BLK, d))
        return c

    lax.fori_loop(0, d, do_first, 0)
    if nb > 1:
        def do_prev(u, c):
            r = u // (nb - 1)
            b = u % (nb - 1) + 1
            base = r + d * b * BLK
            prev(_rows(base, BLK, d), _rows(base - d * BLK, 2 * BLK, d))
            return c

        lax.fori_loop(0, d * (nb - 1), do_prev, 0)


def _band_mask(nkeys):
    i = lax.broadcasted_iota(jnp.int32, (BLK, nkeys), 0)
    j = lax.broadcasted_iota(jnp.int32, (BLK, nkeys), 1)
    if nkeys == BLK:
        return j <= i
    return (j >= i) & (j <= i + BLK)


def _normalize_into(src_ref, gain, dst_ref, S):
    for c in range(S // 256):
        rows = pl.ds(c * 256, 256)
        v = src_ref[rows, :]
        dst_ref[rows, :] = v * _rstd(v) * gain


def _attn_fwd(proj, gq, gk, B, S):
    T, NC = proj.shape
    scale = HEAD_DIM ** -0.5

    def body(q_ref, k_ref, v_ref, z_ref, gq_ref, gk_ref, ag_ref, a_ref, lse_ref, qs, ks, o0, o1, o2, l0, l1, l2):
        g = pl.program_id(2)
        _normalize_into(q_ref, gq_ref[pl.ds(g, 1), :], qs, S)
        _normalize_into(k_ref, gk_ref[pl.ds(g, 1), :], ks, S)
        o_s, l_s = (o0, o1, o2), (l0, l1, l2)

        def run(gi):
            def unit(qr, kr, nkeys):
                s = _dot(qs[qr, :], ks[kr, :], NT_DIMS) * scale
                s = jnp.where(_band_mask(nkeys), s, -jnp.inf)
                m = jnp.max(s, axis=-1, keepdims=True)
                p = jnp.exp(s - m)
                den = jnp.sum(p, axis=-1, keepdims=True)
                o_s[gi][qr, :] = _dot(p / den, v_ref[kr, :], NN_DIMS)
                l_s[gi][qr, :] = jnp.broadcast_to(m + jnp.log(den), (BLK, HEAD_DIM))

            _attn_units(S, DILATIONS[gi], lambda qr: unit(qr, qr, BLK), lambda qr, kr: unit(qr, kr, 2 * BLK))

        for gi in range(N_GROUPS):
            pl.when(g == gi)(functools.partial(run, gi))

        @pl.when(g == N_GROUPS - 1)
        def _():
            for c in range(S // 256):
                rows = pl.ds(c * 256, 256)
                la, lb, lc = l0[rows, :], l1[rows, :], l2[rows, :]
                m = jnp.maximum(jnp.maximum(la, lb), lc)
                wa, wb, wc = jnp.exp(la - m), jnp.exp(lb - m), jnp.exp(lc - m)
                tot = wa + wb + wc
                a = (wa / tot) * o0[rows, :] + (wb / tot) * o1[rows, :] + (wc / tot) * o2[rows, :]
                z = z_ref[rows, :]
                a_ref[rows, :] = a
                lse_ref[rows, :] = m + jnp.log(tot)
                ag_ref[rows, :] = (a * (z * _sigmoid(z))).astype(bf16)

    def slab(col0):
        return pl.BlockSpec((S, HEAD_DIM), lambda b, h, g: (b, col0 // HEAD_DIM + g * HEADS + h))

    gain = pl.BlockSpec((N_GROUPS, HEAD_DIM), lambda b, h, g: (0, 0))
    out = pl.BlockSpec((S, HEAD_DIM), lambda b, h, g: (b, h))
    return pl.pallas_call(
        body, grid=(B, HEADS, N_GROUPS),
        in_specs=[slab(Q0), slab(K0), slab(V0), pl.BlockSpec((S, HEAD_DIM), lambda b, h, g: (b, ZA // HEAD_DIM + h)), gain, gain],
        out_specs=[out, out, out],
        out_shape=[jax.ShapeDtypeStruct((T, ATTN_OUT), bf16), jax.ShapeDtypeStruct((T, ATTN_OUT), f32),
                   jax.ShapeDtypeStruct((T, ATTN_OUT), f32)],
        scratch_shapes=[pltpu.VMEM((S, HEAD_DIM), f32)] * 8,
        compiler_params=_params(("arbitrary", "arbitrary", "arbitrary")), name="attn_fwd",
    )(proj, proj, proj, proj, gq, gk)


def _rms_bwd_rows(raw, gain, dn):
    r = _rstd(raw)
    xh = raw * r
    dxh = dn * gain
    return r * (dxh - xh * jnp.mean(dxh * xh, axis=-1, keepdims=True)), jnp.sum(dn * xh, axis=0, keepdims=True)


def _attn_bwd(proj, gq, gk, a_comb, lse, dag, dproj, B, S):
    T, NC = proj.shape
    scale = HEAD_DIM ** -0.5

    def body(q_ref, k_ref, v_ref, z_ref, gq_ref, gk_ref, a_ref, lse_ref, dag_ref, dproj_in, dproj_ref, dgq_ref, dgk_ref,
             qs, ks, da_s, dl_s, dq_s, dk_s, dv_s, stage, sem):
        b, h, g = pl.program_id(0), pl.program_id(1), pl.program_id(2)
        gq_row, gk_row = gq_ref[pl.ds(g, 1), :], gk_ref[pl.ds(g, 1), :]
        _normalize_into(q_ref, gq_row, qs, S)
        _normalize_into(k_ref, gk_row, ks, S)

        def put(slot, col0):
            cp = pltpu.make_async_copy(stage.at[slot], dproj_ref.at[pl.ds(b * S, S), pl.ds(col0, HEAD_DIM)], sem.at[slot])
            cp.start()
            return cp

        @pl.when((b == 0) & (h == 0) & (g == 0))
        def _():
            dgq_ref[...] = jnp.zeros_like(dgq_ref)
            dgk_ref[...] = jnp.zeros_like(dgk_ref)

        @pl.when(g == 0)
        def _():
            for c in range(S // 256):
                rows = pl.ds(c * 256, 256)
                z, a, dg_ = z_ref[rows, :], a_ref[rows, :], dag_ref[rows, :]
                sg = _sigmoid(z)
                da = dg_ * (z * sg)
                da_s[rows, :] = da
                dl_s[rows, :] = jnp.broadcast_to(jnp.sum(da * a, axis=-1, keepdims=True), (256, HEAD_DIM))
                stage[3, rows, :] = (dg_ * a * (sg * (1.0 + z * (1.0 - sg)))).astype(bf16)
            put(3, ZA + h * HEAD_DIM).wait()

        dk_s[...] = jnp.zeros_like(dk_s)
        dv_s[...] = jnp.zeros_like(dv_s)

        def run(gi):
            def unit(qr, kr, nkeys):
                q, k, v = qs[qr, :], ks[kr, :], v_ref[kr, :]
                s = _dot(q, k, NT_DIMS) * scale
                p = jnp.where(_band_mask(nkeys), jnp.exp(s - lse_ref[qr, :][:, :1]), 0.0)
                da = da_s[qr, :]
                dp = _dot(da, v, NT_DIMS)
                ds = p * (dp - dl_s[qr, :][:, :1]) * scale
                dq_s[qr, :] = _dot(ds, k, NN_DIMS)
                dk_s[kr, :] += _dot(ds, q, TN_DIMS)
                dv_s[kr, :] += _dot(p, da, TN_DIMS)

            _attn_units(S, DILATIONS[gi], lambda qr: unit(qr, qr, BLK), lambda qr, kr: unit(qr, kr, 2 * BLK))

        for gi in range(N_GROUPS):
            pl.when(g == gi)(functools.partial(run, gi))

        gq_acc = jnp.zeros((1, HEAD_DIM), f32)
        gk_acc = jnp.zeros((1, HEAD_DIM), f32)
        for c in range(S // 256):
            rows = pl.ds(c * 256, 256)
            dq, gq_p = _rms_bwd_rows(q_ref[rows, :], gq_row, dq_s[rows, :])
            dk, gk_p = _rms_bwd_rows(k_ref[rows, :], gk_row, dk_s[rows, :])
            gq_acc, gk_acc = gq_acc + gq_p, gk_acc + gk_p
            stage[0, rows, :] = dq.astype(bf16)
            stage[1, rows, :] = dk.astype(bf16)
            stage[2, rows, :] = dv_s[rows, :].astype(bf16)
        dgq_ref[pl.ds(g, 1), :] += gq_acc
        dgk_ref[pl.ds(g, 1), :] += gk_acc
        col = (g * HEADS + h) * HEAD_DIM
        cps = [put(0, Q0 + col), put(1, K0 + col), put(2, V0 + col)]
        for cp in cps:
            cp.wait()

    def slab(col0):
        return pl.BlockSpec((S, HEAD_DIM), lambda b, h, g: (b, col0 // HEAD_DIM + g * HEADS + h))

    gain = pl.BlockSpec((N_GROUPS, HEAD_DIM), lambda b, h, g: (0, 0))
    per_slot = pl.BlockSpec((S, HEAD_DIM), lambda b, h, g: (b, h))
    return pl.pallas_call(
        body, grid=(B, HEADS, N_GROUPS),
        in_specs=[slab(Q0), slab(K0), slab(V0), pl.BlockSpec((S, HEAD_DIM), lambda b, h, g: (b, ZA // HEAD_DIM + h)), gain, gain,
                  per_slot, per_slot, per_slot, pl.BlockSpec(memory_space=pl.ANY)],
        out_specs=[pl.BlockSpec(memory_space=pl.ANY), gain, gain],
        out_shape=[jax.ShapeDtypeStruct(dproj.shape, dproj.dtype), jax.ShapeDtypeStruct((N_GROUPS, HEAD_DIM), f32),
                   jax.ShapeDtypeStruct((N_GROUPS, HEAD_DIM), f32)],
        scratch_shapes=[pltpu.VMEM((S, HEAD_DIM), f32)] * 7 + [pltpu.VMEM((4, S, HEAD_DIM), bf16), pltpu.SemaphoreType.DMA((4,))],
        input_output_aliases={9: 0},
        compiler_params=_params(("arbitrary", "arbitrary", "arbitrary")), name="attn_bwd",
    )(proj, proj, proj, proj, gq, gk, a_comb, lse, dag, dproj)


def _conv_fwd(proj, conv_w, B, S):
    T, NC = proj.shape
    tc = LANES

    def body(cb_ref, cc_ref, cv_ref, z_ref, w_ref, c_ref, ub):
        u = cc_ref[...] * cv_ref[...]
        ub[0:8, :] = jnp.zeros((8, tc), f32)
        ub[8:8 + S, :] = u
        w = w_ref[...]
        y = w[0:1] * u + w[1:2] * ub[pl.ds(7, S), :] + w[2:3] * ub[pl.ds(6, S), :]
        z = z_ref[...]
        c_ref[...] = (cb_ref[...] * y * (z * _sigmoid(z))).astype(bf16)

    def seg(col0):
        return pl.BlockSpec((S, tc), lambda j, b: (b, col0 // tc + j))

    return pl.pallas_call(
        body, grid=(CONV_W // tc, B),
        in_specs=[seg(CB), seg(CC), seg(CV), seg(ZC), pl.BlockSpec((3, tc), lambda j, b: (0, j))],
        out_specs=pl.BlockSpec((S, tc), lambda j, b: (b, j)),
        out_shape=jax.ShapeDtypeStruct((T, CONV_W), bf16),
        scratch_shapes=[pltpu.VMEM((S + 8, tc), f32)],
        compiler_params=_params(("parallel", "parallel")), name="conv_fwd",
    )(proj, proj, proj, proj, conv_w)


def _conv_bwd(proj, conv_w, dc, dproj, B, S):
    T, NC = proj.shape
    tc = LANES

    def body(cb_ref, cc_ref, cv_ref, z_ref, w_ref, dc_ref, dproj_in, dproj_ref, dw_ref, ub, db, stage, sem):
        j, b = pl.program_id(0), pl.program_id(1)
        cb, cc, cv, z, dcv = cb_ref[...], cc_ref[...], cv_ref[...], z_ref[...], dc_ref[...]
        u = cc * cv
        ub[0:8, :] = jnp.zeros((8, tc), f32)
        ub[8:8 + S, :] = u
        u1, u2 = ub[pl.ds(7, S), :], ub[pl.ds(6, S), :]
        w = w_ref[...]
        y = w[0:1] * u + w[1:2] * u1 + w[2:3] * u2
        sg = _sigmoid(z)
        si = z * sg
        dyv = dcv * cb * si
        db[0:S, :] = dyv
        db[S:S + 8, :] = jnp.zeros((8, tc), f32)
        du = w[0:1] * dyv + w[1:2] * db[pl.ds(1, S), :] + w[2:3] * db[pl.ds(2, S), :]
        stage[0] = (dcv * y * si).astype(bf16)
        stage[1] = (du * cv).astype(bf16)
        stage[2] = (du * cc).astype(bf16)
        stage[3] = (dcv * cb * y * (sg * (1.0 + z * (1.0 - sg)))).astype(bf16)
        cps = []
        for slot, col0 in enumerate((CB, CC, CV, ZC)):
            cp = pltpu.make_async_copy(stage.at[slot], dproj_ref.at[pl.ds(b * S, S), pl.ds(col0 + j * tc, tc)], sem.at[slot])
            cp.start()
            cps.append(cp)
        part = jnp.concatenate([jnp.sum(dyv * u, axis=0, keepdims=True), jnp.sum(dyv * u1, axis=0, keepdims=True),
                                jnp.sum(dyv * u2, axis=0, keepdims=True)], axis=0)

        @pl.when(b == 0)
        def _():
            dw_ref[...] = part

        @pl.when(b > 0)
        def _():
            dw_ref[...] += part

        for cp in cps:
            cp.wait()

    def seg(col0):
        return pl.BlockSpec((S, tc), lambda j, b: (b, col0 // tc + j))

    return pl.pallas_call(
        body, grid=(CONV_W // tc, B),
        in_specs=[seg(CB), seg(CC), seg(CV), seg(ZC), pl.BlockSpec((3, tc), lambda j, b: (0, j)),
                  pl.BlockSpec((S, tc), lambda j, b: (b, j)), pl.BlockSpec(memory_space=pl.ANY)],
        out_specs=[pl.BlockSpec(memory_space=pl.ANY), pl.BlockSpec((3, tc), lambda j, b: (0, j))],
        out_shape=[jax.ShapeDtypeStruct(dproj.shape, dproj.dtype), jax.ShapeDtypeStruct((3, CONV_W), f32)],
        scratch_shapes=[pltpu.VMEM((S + 8, tc), f32), pltpu.VMEM((S + 8, tc), f32), pltpu.VMEM((4, S, tc), bf16),
                        pltpu.SemaphoreType.DMA((4,))],
        input_output_aliases={6: 0},
        compiler_params=_params(("arbitrary", "arbitrary")), name="conv_bwd",
    )(proj, proj, proj, proj, conv_w, dc, dproj)


MEM_CHUNK = 256


def _mem_fwd(proj, mkv, gq, gk, B, S):
    T, NC = proj.shape
    scale = MEM_HD ** -0.5

    def body(q_ref, z_ref, k_ref, v_ref, gq_ref, gk_ref, o_ref):
        kv = k_ref[...]
        kn = (kv * _rstd(kv) * gk_ref[...]).astype(bf16)
        vv = v_ref[...].astype(bf16)

        def chunk(c, carry):
            rows = pl.ds(pl.multiple_of(c * MEM_CHUNK, MEM_CHUNK), MEM_CHUNK)
            q = q_ref[rows, :]
            qn = q * _rstd(q) * gq_ref[...]
            s = _dot(qn, kn, NT_DIMS) * scale
            p = jnp.exp(s - jnp.max(s, axis=-1, keepdims=True))
            p = p / jnp.sum(p, axis=-1, keepdims=True)
            z = z_ref[rows, :]
            o_ref[rows, :] = (_dot(p, vv, NN_DIMS) * (z * _sigmoid(z))).astype(bf16)
            return carry

        lax.fori_loop(0, S // MEM_CHUNK, chunk, 0)

    gain = pl.BlockSpec((1, MEM_HD), lambda b, h: (0, 0))
    return pl.pallas_call(
        body, grid=(B, MEM_HEADS),
        in_specs=[pl.BlockSpec((S, MEM_HD), lambda b, h: (b, MQ // MEM_HD + h)),
                  pl.BlockSpec((S, MEM_HD), lambda b, h: (b, ZM // MEM_HD + h)),
                  pl.BlockSpec((MEM_LEN, MEM_HD), lambda b, h: (b, h)),
                  pl.BlockSpec((MEM_LEN, MEM_HD), lambda b, h: (b, MEM_HEADS + h)), gain, gain],
        out_specs=pl.BlockSpec((S, MEM_HD), lambda b, h: (b, h)),
        out_shape=jax.ShapeDtypeStruct((T, MEM_W), bf16),
        compiler_params=_params(("parallel", "parallel")), name="mem_fwd",
    )(proj, proj, mkv, mkv, gq, gk)


def _mem_bwd(proj, mkv, gq, gk, dmo, dproj, B, S):
    T, NC = proj.shape
    scale = MEM_HD ** -0.5

    def body(q_ref, z_ref, k_ref, v_ref, gq_ref, gk_ref, dmo_ref, dproj_in, dproj_ref, dmk_ref, dmv_ref, dgq_ref, dgk_ref,
             dkn_s, dv_s, gq_s, stage, sem):
        b, h = pl.program_id(0), pl.program_id(1)
        kv = k_ref[...]
        kn = (kv * _rstd(kv) * gk_ref[...]).astype(bf16)
        vv = v_ref[...].astype(bf16)
        dkn_s[...] = jnp.zeros_like(dkn_s)
        dv_s[...] = jnp.zeros_like(dv_s)
        gq_s[...] = jnp.zeros_like(gq_s)

        def chunk(c, carry):
            rows = pl.ds(pl.multiple_of(c * MEM_CHUNK, MEM_CHUNK), MEM_CHUNK)
            q = q_ref[rows, :]
            qn = q * _rstd(q) * gq_ref[...]
            s = _dot(qn, kn, NT_DIMS) * scale
            p = jnp.exp(s - jnp.max(s, axis=-1, keepdims=True))
            p = p / jnp.sum(p, axis=-1, keepdims=True)
            mo = _dot(p, vv, NN_DIMS)
            z, dg_ = z_ref[rows, :], dmo_ref[rows, :]
            sg = _sigmoid(z)
            do = dg_ * (z * sg)
            stage[1, rows, :] = (dg_ * mo * (sg * (1.0 + z * (1.0 - sg)))).astype(bf16)
            dp = _dot(do, vv, NT_DIMS)
            ds = p * (dp - jnp.sum(do * mo, axis=-1, keepdims=True)) * scale
            dq, gq_p = _rms_bwd_rows(q, gq_ref[...], _dot(ds, kn, NN_DIMS))
            stage[0, rows, :] = dq.astype(bf16)
            gq_s[...] += gq_p
            dkn_s[...] += _dot(ds, qn, TN_DIMS)
            dv_s[...] += _dot(p, do, TN_DIMS)
            return carry

        lax.fori_loop(0, S // MEM_CHUNK, chunk, 0)
        cps = []
        for slot, col0 in enumerate((MQ, ZM)):
            cp = pltpu.make_async_copy(stage.at[slot], dproj_ref.at[pl.ds(b * S, S), pl.ds(col0 + h * MEM_HD, MEM_HD)], sem.at[slot])
            cp.start()
            cps.append(cp)
        dk, gk_p = _rms_bwd_rows(kv, gk_ref[...], dkn_s[...])
        dmk_ref[...] = dk
        dmv_ref[...] = dv_s[...]

        @pl.when((b == 0) & (h == 0))
        def _():
            dgq_ref[...] = gq_s[...]
            dgk_ref[...] = gk_p

        @pl.when((b > 0) | (h > 0))
        def _():
            dgq_ref[...] += gq_s[...]
            dgk_ref[...] += gk_p

        for cp in cps:
            cp.wait()

    gain = pl.BlockSpec((1, MEM_HD), lambda b, h: (0, 0))
    kvb = pl.BlockSpec((MEM_LEN, MEM_HD), lambda b, h: (b, h))
    return pl.pallas_call(
        body, grid=(B, MEM_HEADS),
        in_specs=[pl.BlockSpec((S, MEM_HD), lambda b, h: (b, MQ // MEM_HD + h)),
                  pl.BlockSpec((S, MEM_HD), lambda b, h: (b, ZM // MEM_HD + h)),
                  kvb, pl.BlockSpec((MEM_LEN, MEM_HD), lambda b, h: (b, MEM_HEADS + h)), gain, gain,
                  pl.BlockSpec((S, MEM_HD), lambda b, h: (b, h)), pl.BlockSpec(memory_space=pl.ANY)],
        out_specs=[pl.BlockSpec(memory_space=pl.ANY), kvb, kvb, gain, gain],
        out_shape=[jax.ShapeDtypeStruct(dproj.shape, dproj.dtype), jax.ShapeDtypeStruct((B * MEM_LEN, MEM_W), f32),
                   jax.ShapeDtypeStruct((B * MEM_LEN, MEM_W), f32), jax.ShapeDtypeStruct((1, MEM_HD), f32),
                   jax.ShapeDtypeStruct((1, MEM_HD), f32)],
        scratch_shapes=[pltpu.VMEM((MEM_LEN, MEM_HD), f32), pltpu.VMEM((MEM_LEN, MEM_HD), f32), pltpu.VMEM((1, MEM_HD), f32),
                        pltpu.VMEM((2, S, MEM_HD), bf16), pltpu.SemaphoreType.DMA((2,))],
        input_output_aliases={7: 0},
        compiler_params=_params(("arbitrary", "arbitrary")), name="mem_bwd",
    )(proj, proj, mkv, mkv, gq, gk, dmo, dproj)


def _merge_fwd(proj, ag, cg, mg, wa, wc, wm):
    T, NC = proj.shape
    D = wa.shape[1]
    tm, tn = _pick(T, 1024), _pick(D, 512)
    assert GT % tn == 0

    def body(ag_ref, cg_ref, mg_ref, wa_ref, wc_ref, wm_ref, g0_ref, g1_ref, g2_ref, mer_ref, ba_ref, bc_ref, bm_ref):
        ba = _dot(ag_ref[...], wa_ref[...], NN_DIMS)
        bc = _dot(cg_ref[...], wc_ref[...], NN_DIMS)
        bm = _dot(mg_ref[...], wm_ref[...], NN_DIMS)
        mer_ref[...] = (_sigmoid(g0_ref[...]) * ba + _sigmoid(g1_ref[...]) * bc + _sigmoid(g2_ref[...]) * bm).astype(bf16)
        ba_ref[...] = ba.astype(bf16)
        bc_ref[...] = bc.astype(bf16)
        bm_ref[...] = bm.astype(bf16)

    def act(w):
        return pl.BlockSpec((tm, w), lambda i, j: (i, 0))

    def wt(k):
        return pl.BlockSpec((k, tn), lambda i, j: (0, j))

    def gate(n):
        return pl.BlockSpec((tm, tn), lambda i, j: (i, (GT + n * D) // tn + j))

    out = pl.BlockSpec((tm, tn), lambda i, j: (i, j))
    return pl.pallas_call(
        body, grid=(T // tm, D // tn),
        in_specs=[act(ATTN_OUT), act(CONV_W), act(MEM_W), wt(ATTN_OUT), wt(CONV_W), wt(MEM_W), gate(0), gate(1), gate(2)],
        out_specs=[out] * 4, out_shape=[jax.ShapeDtypeStruct((T, D), bf16)] * 4,
        compiler_params=_params(("parallel", "parallel")), name="merge_fwd",
    )(ag, cg, mg, wa, wc, wm, proj, proj, proj)


def _out_loss(merged, w_out, x, tgt):
    T, D = x.shape
    tm, tn = _pick(T, 512), _pick(D, 512)

    def body(m_ref, w_ref, x_ref, t_ref, dy_ref, dyb_ref, lp_ref):
        e = x_ref[...] + _dot(m_ref[...], w_ref[...], NN_DIMS) - t_ref[...]
        dy = e / D
        dy_ref[...] = dy
        dyb_ref[...] = dy.astype(bf16)
        part = jnp.full((1, 8, LANES), jnp.sum(e * e), f32)

        @pl.when(pl.program_id(1) == 0)
        def _():
            lp_ref[...] = part

        @pl.when(pl.program_id(1) > 0)
        def _():
            lp_ref[...] += part

    tile = pl.BlockSpec((tm, tn), lambda i, j: (i, j))
    return pl.pallas_call(
        body, grid=(T // tm, D // tn),
        in_specs=[pl.BlockSpec((tm, D), lambda i, j: (i, 0)), pl.BlockSpec((D, tn), lambda i, j: (0, j)), tile, tile],
        out_specs=[tile, tile, pl.BlockSpec((1, 8, LANES), lambda i, j: (i, 0, 0))],
        out_shape=[jax.ShapeDtypeStruct((T, D), f32), jax.ShapeDtypeStruct((T, D), bf16),
                   jax.ShapeDtypeStruct((T // tm, 8, LANES), f32)],
        compiler_params=_params(("parallel", "arbitrary")), name="out_loss",
    )(merged, w_out, x, tgt)


def _merge_bwd(proj, dyb, w_out, ba, bc, bm):
    T, NC = proj.shape
    D = w_out.shape[0]
    tm, tn = _pick(T, 512), _pick(D, 512)
    assert GT % tn == 0

    def body(dy_ref, w_ref, ba_ref, bc_ref, bm_ref, g0_ref, g1_ref, g2_ref, da_ref, dc_ref, dm_ref, dproj_ref, stage, sem):
        i, j = pl.program_id(0), pl.program_id(1)
        dmer = _dot(dy_ref[...], w_ref[...], NT_DIMS)
        cps = []
        for n, (g_ref, br_ref, o_ref) in enumerate(((g0_ref, ba_ref, da_ref), (g1_ref, bc_ref, dc_ref), (g2_ref, bm_ref, dm_ref))):
            sg = _sigmoid(g_ref[...])
            o_ref[...] = (sg * dmer).astype(bf16)
            stage[n] = (dmer * br_ref[...].astype(f32) * (sg * (1.0 - sg))).astype(bf16)
            cp = pltpu.make_async_copy(stage.at[n], dproj_ref.at[pl.ds(i * tm, tm), pl.ds(GT + n * D + j * tn, tn)], sem.at[n])
            cp.start()
            cps.append(cp)
        for cp in cps:
            cp.wait()

    tile = pl.BlockSpec((tm, tn), lambda i, j: (i, j))

    def gate(n):
        return pl.BlockSpec((tm, tn), lambda i, j: (i, (GT + n * D) // tn + j))

    return pl.pallas_call(
        body, grid=(T // tm, D // tn),
        in_specs=[pl.BlockSpec((tm, D), lambda i, j: (i, 0)), pl.BlockSpec((tn, D), lambda i, j: (j, 0)), tile, tile, tile,
                  gate(0), gate(1), gate(2)],
        out_specs=[tile, tile, tile, pl.BlockSpec(memory_space=pl.ANY)],
        out_shape=[jax.ShapeDtypeStruct((T, D), bf16)] * 3 + [jax.ShapeDtypeStruct((T, NC), bf16)],
        scratch_shapes=[pltpu.VMEM((3, tm, tn), bf16), pltpu.SemaphoreType.DMA((3,))],
        compiler_params=_params(("arbitrary", "arbitrary")), name="merge_bwd",
    )(dyb, w_out, ba, bc, bm, proj, proj, proj)


def _local_step(x, mem, tgt, norm_g, mem_norm_g, attn_q_norm, attn_k_norm, conv_w, mem_q_norm, mem_k_norm,
                w_in, w_kv, w_a, w_c, w_m, w_out):
    B, S, D = x.shape
    T = B * S
    x2, t2, mem2 = x.reshape(T, D), tgt.reshape(T, D), mem.reshape(B * MEM_LEN, D)
    ng, mng = norm_g.reshape(1, D), mem_norm_g.reshape(1, D)
    mqn, mkn = mem_q_norm.reshape(1, MEM_HD), mem_k_norm.reshape(1, MEM_HD)

    h = _rms_fwd(x2, ng, "rms_x")
    proj = _mm(h, w_in, mode="nn", out_dtype=f32, name="proj")
    mh = _rms_fwd(mem2, mng, "rms_mem")
    mkv = _mm(mh, w_kv, mode="nn", out_dtype=f32, name="mkv")
    ag, a_comb, lse = _attn_fwd(proj, attn_q_norm, attn_k_norm, B, S)
    cg = _conv_fwd(proj, conv_w, B, S)
    mg = _mem_fwd(proj, mkv, mqn, mkn, B, S)
    merged, ba, bc, bm = _merge_fwd(proj, ag, cg, mg, w_a, w_c, w_m)
    dy, dyb, lp = _out_loss(merged, w_out, x2, t2)
    sq_err = jnp.sum(lp[:, 0, 0])

    g = {}
    g["w_out"] = _mm(merged, dyb, mode="tn", out_dtype=f32, name="dw_out")
    dba, dbc, dbm, dproj = _merge_bwd(proj, dyb, w_out, ba, bc, bm)
    g["w_br_attn"] = _mm(ag, dba, mode="tn", out_dtype=f32, name="dw_br_attn")
    g["w_br_conv"] = _mm(cg, dbc, mode="tn", out_dtype=f32, name="dw_br_conv")
    g["w_br_mem"] = _mm(mg, dbm, mode="tn", out_dtype=f32, name="dw_br_mem")
    dag = _mm(dba, w_a, mode="nt", out_dtype=f32, name="d_attn_out")
    dcg = _mm(dbc, w_c, mode="nt", out_dtype=f32, name="d_conv_out")
    dmg = _mm(dbm, w_m, mode="nt", out_dtype=f32, name="d_mem_out")
    dproj, g["attn_q_norm"], g["attn_k_norm"] = _attn_bwd(proj, attn_q_norm, attn_k_norm, a_comb, lse, dag, dproj, B, S)
    dproj, g["conv_w"] = _conv_bwd(proj, conv_w, dcg, dproj, B, S)
    dproj, dmk, dmv, dgmq, dgmk = _mem_bwd(proj, mkv, mqn, mkn, dmg, dproj, B, S)
    g["mem_q_norm"], g["mem_k_norm"] = dgmq.reshape(MEM_HD), dgmk.reshape(MEM_HD)
    dmkv = jnp.concatenate([dmk, dmv], axis=1)
    g["mem_w_kv"] = _mm(mh, dmkv, mode="tn", out_dtype=f32, name="dw_kv")
    dmh = _mm(dmkv, w_kv, mode="nt", out_dtype=f32, name="d_mem_h")
    g["mem_norm_g"] = _rms_bwd(mem2, mng, dmh, None, "rms_mem_bwd").reshape(D)
    g["w_in"] = _mm(h, dproj, mode="tn", out_dtype=f32, name="dw_in")
    dh = _mm(dproj, w_in, mode="nt", out_dtype=f32, name="d_h")
    dx, dng = _rms_bwd(x2, ng, dh, dy, "rms_x_bwd")
    g["norm_g"] = dng.reshape(D)
    return sq_err, dx, g


MESH = pl.DeviceIdType.MESH
HBM = pl.BlockSpec(memory_space=pl.ANY)


def _me():
    x, y, c = lax.axis_index("x"), lax.axis_index("y"), lax.axis_index("c")
    return (x, y, c), 4 * x + 2 * y + c


def _peer(k):
    (x, y, c), _ = _me()
    p = (1 - x if k & 4 else x, 1 - y if k & 2 else y, 1 - c if k & 1 else c)
    return p, 4 * p[0] + 2 * p[1] + p[2]


def _window(ref, axis, size, idx):
    if axis is None:
        return ref.at[idx]
    if axis == 0:
        return ref.at[pl.ds(idx * size, size), :]
    return ref.at[:, pl.ds(idx * size, size)]


def _gather_weights(shards, axes):
    n = len(shards)

    def full_shape(s, axis):
        if axis is None:
            return (N_DEV,) + s.shape
        return tuple(N_DEV * d if i == axis else d for i, d in enumerate(s.shape))

    def body(*refs):
        ins, outs = refs[:n], refs[n:2 * n]
        send_sems, recv_sems, local_sems = refs[2 * n:]
        (x, y, c), me = _me()
        sibling, sib_id = _peer(1)
        chips = [_peer(4), _peer(2), _peer(6)]

        def win(a, idx):
            return _window(outs[a], axes[a], shards[a].shape[axes[a]] if axes[a] is not None else None, idx)

        def copy(a, k, block, to, src=None):
            return pltpu.make_async_remote_copy(
                src_ref=win(a, block) if src is None else src, dst_ref=win(a, block),
                send_sem=send_sems.at[a, k], recv_sem=recv_sems.at[a, k], device_id=to, device_id_type=MESH)

        mine = [pltpu.make_async_copy(ins[a], win(a, me), local_sems.at[a]) for a in range(n)]
        for cp in mine:
            cp.start()
        first = []
        for a in range(n):
            first.append(copy(a, 0, me, sibling, src=ins[a]))
            first += [copy(a, 1 + j, me, dev, src=ins[a]) for j, (dev, _) in enumerate(chips)]
        for cp in first:
            cp.start()
        passed = []
        for j, (dev, dev_id) in enumerate(chips):
            for a in range(n):
                copy(a, 1 + j, dev_id, (x, y, c)).wait_recv()
                cp = copy(a, 4 + j, dev_id, sibling)
                cp.start()
                passed.append(cp)
        for a in range(n):
            copy(a, 0, sib_id, (x, y, c)).wait_recv()
        for j, (dev, dev_id) in enumerate(chips):
            for a in range(n):
                copy(a, 4 + j, dev_id + 1 - 2 * c, (x, y, c)).wait_recv()
        for cp in first + passed:
            cp.wait_send()
        for cp in mine:
            cp.wait()

    return pl.pallas_call(
        body, in_specs=[HBM] * n, out_specs=[HBM] * n,
        out_shape=[jax.ShapeDtypeStruct(full_shape(s, ax), s.dtype) for s, ax in zip(shards, axes)],
        scratch_shapes=[pltpu.SemaphoreType.DMA((n, 7)), pltpu.SemaphoreType.DMA((n, 7)), pltpu.SemaphoreType.DMA((n,))],
        name="gather_weights",
    )(*shards)


N_CHIP = 4


def _shard_shape(g, ax):
    return tuple(d // N_DEV if i == ax else d for i, d in enumerate(g.shape))


def _rs_sibling(grads, axes):
    n = len(grads)
    sizes = [g.shape[ax] // N_DEV for g, ax in zip(grads, axes)]

    def body(*refs):
        ins, lands = refs[:n], refs[n:2 * n]
        send_sems, recv_sems = refs[2 * n:]
        sibling, _ = _peer(1)
        copies = []
        for j in range(N_CHIP):
            _, owner = _peer(2 * j + 1)
            for a in range(n):
                copies.append(pltpu.make_async_remote_copy(
                    src_ref=_window(ins[a], axes[a], sizes[a], owner), dst_ref=lands[a].at[j],
                    send_sem=send_sems.at[a, j], recv_sem=recv_sems.at[a, j], device_id=sibling, device_id_type=MESH))
        for cp in copies:
            cp.start()
        for cp in copies:
            cp.wait()

    return pl.pallas_call(
        body, in_specs=[HBM] * n, out_specs=[HBM] * n,
        out_shape=[jax.ShapeDtypeStruct((N_CHIP,) + _shard_shape(g, ax), g.dtype) for g, ax in zip(grads, axes)],
        scratch_shapes=[pltpu.SemaphoreType.DMA((n, N_CHIP)), pltpu.SemaphoreType.DMA((n, N_CHIP))],
        name="rs_sibling",
    )(*grads)


def _rs_chip_sum(grad, land, xyc, axis, name):
    R, C = land.shape[1:]
    tr = _pick(R, max(8, (256 * 1024) // C), 8)

    def body(xyc_ref, g_ref, l_ref, o_ref):
        o_ref[0] = (g_ref[...] + l_ref[0]).astype(bf16)

    def owner(j, xyc_ref):
        jj = j + 1
        return 4 * (xyc_ref[0] ^ (jj >> 1)) + 2 * (xyc_ref[1] ^ (jj & 1)) + xyc_ref[2]

    if axis == 0:
        g_spec = pl.BlockSpec((tr, C), lambda j, i, xyc_ref: (owner(j, xyc_ref) * (R // tr) + i, 0))
    else:
        g_spec = pl.BlockSpec((tr, C), lambda j, i, xyc_ref: (i, owner(j, xyc_ref)))
    return pl.pallas_call(
        body,
        grid_spec=pltpu.PrefetchScalarGridSpec(
            num_scalar_prefetch=1, grid=(N_CHIP - 1, R // tr),
            in_specs=[g_spec, pl.BlockSpec((1, tr, C), lambda j, i, xyc_ref: (j + 1, i, 0))],
            out_specs=pl.BlockSpec((1, tr, C), lambda j, i, xyc_ref: (j, i, 0))),
        out_shape=jax.ShapeDtypeStruct((N_CHIP - 1, R, C), bf16),
        compiler_params=_params(("parallel", "parallel")), name=name,
    )(xyc, grad, land)


def _rs_chips(sums):
    n = len(sums)

    def body(*refs):
        ins, lands = refs[:n], refs[n:2 * n]
        send_sems, recv_sems = refs[2 * n:]
        copies = []
        for j in range(1, N_CHIP):
            dev, _ = _peer(2 * j)
            for a in range(n):
                copies.append(pltpu.make_async_remote_copy(
                    src_ref=ins[a].at[j - 1], dst_ref=lands[a].at[j - 1],
                    send_sem=send_sems.at[a, j - 1], recv_sem=recv_sems.at[a, j - 1], device_id=dev, device_id_type=MESH))
        for cp in copies:
            cp.start()
        for cp in copies:
            cp.wait()

    return pl.pallas_call(
        body, in_specs=[HBM] * n, out_specs=[HBM] * n,
        out_shape=[jax.ShapeDtypeStruct(s.shape, s.dtype) for s in sums],
        scratch_shapes=[pltpu.SemaphoreType.DMA((n, N_CHIP - 1)), pltpu.SemaphoreType.DMA((n, N_CHIP - 1))],
        name="rs_chips",
    )(*sums)


def _allreduce_small(part):
    R = part.shape[0]

    def body(p_ref, o_ref, buf, send_sems, recv_sems):
        (x, y, c), me = _me()
        buf[me] = p_ref[...]
        copies = []
        for k in range(1, N_DEV):
            dev, dev_id = _peer(k)
            copies.append(pltpu.make_async_remote_copy(
                src_ref=p_ref, dst_ref=buf.at[me], send_sem=send_sems.at[k - 1], recv_sem=recv_sems.at[k - 1],
                device_id=dev, device_id_type=MESH))
        for cp in copies:
            cp.start()
        for k in range(1, N_DEV):
            _, dev_id = _peer(k)
            pltpu.make_async_remote_copy(
                src_ref=p_ref, dst_ref=buf.at[dev_id], send_sem=send_sems.at[k - 1], recv_sem=recv_sems.at[k - 1],
                device_id=(x, y, c), device_id_type=MESH).wait_recv()
        for cp in copies:
            cp.wait_send()
        acc = buf[0]
        for d in range(1, N_DEV):
            acc = acc + buf[d]
        o_ref[...] = acc

    return pl.pallas_call(
        body, in_specs=[pl.BlockSpec(memory_space=pltpu.VMEM)], out_specs=pl.BlockSpec(memory_space=pltpu.VMEM),
        out_shape=jax.ShapeDtypeStruct((R, LANES), f32),
        scratch_shapes=[pltpu.VMEM((N_DEV, R, LANES), f32), pltpu.SemaphoreType.DMA((N_DEV - 1,)), pltpu.SemaphoreType.DMA((N_DEV - 1,))],
        name="allreduce_small",
    )(part)


def _adamw_math(w, g, m, v):
    m2 = ADAM_B1 * m + (1.0 - ADAM_B1) * g
    v2 = ADAM_B2 * v + (1.0 - ADAM_B2) * (g * g)
    m_hat = m2 / (1.0 - ADAM_B1 ** ADAM_STEP)
    v_hat = v2 / (1.0 - ADAM_B2 ** ADAM_STEP)
    return -ADAM_LR * (m_hat / (jnp.sqrt(v_hat) + ADAM_EPS) + ADAM_WD * w), m2, v2


def _adamw_shard(grad, land_sib, land_chips, w, m, v, me, axis, name):
    R, C = w.shape
    tr = _pick(R, max(8, (128 * 1024) // C), 8)

    def body(me_ref, g_ref, s_ref, l_ref, w_ref, m_ref, v_ref, go_ref, d_ref, mo_ref, vo_ref):
        g = g_ref[...] + s_ref[0]
        for j in range(N_CHIP - 1):
            g = g + l_ref[j].astype(f32)
        d, m2, v2 = _adamw_math(w_ref[...], g, m_ref[...], v_ref[...])
        go_ref[...] = g
        d_ref[...] = d
        mo_ref[...] = m2
        vo_ref[...] = v2

    if axis == 0:
        g_spec = pl.BlockSpec((tr, C), lambda i, me_ref: (me_ref[0] * (R // tr) + i, 0))
    else:
        g_spec = pl.BlockSpec((tr, C), lambda i, me_ref: (i, me_ref[0]))
    blk = pl.BlockSpec((tr, C), lambda i, me_ref: (i, 0))
    return pl.pallas_call(
        body,
        grid_spec=pltpu.PrefetchScalarGridSpec(
            num_scalar_prefetch=1, grid=(R // tr,),
            in_specs=[g_spec, pl.BlockSpec((1, tr, C), lambda i, me_ref: (0, i, 0)),
                      pl.BlockSpec((N_CHIP - 1, tr, C), lambda i, me_ref: (0, i, 0)), blk, blk, blk],
            out_specs=[blk] * 4),
        out_shape=[jax.ShapeDtypeStruct((R, C), f32)] * 4,
        compiler_params=_params(("parallel",)), name=name,
    )(me, grad, land_sib, land_chips, w, m, v)


def _adamw_small(g, w, m, v):
    def body(g_ref, w_ref, m_ref, v_ref, d_ref, mo_ref, vo_ref):
        d, m2, v2 = _adamw_math(w_ref[...], g_ref[...], m_ref[...], v_ref[...])
        d_ref[...] = d
        mo_ref[...] = m2
        vo_ref[...] = v2

    return pl.pallas_call(body, out_shape=[jax.ShapeDtypeStruct(g.shape, f32)] * 3, name="adamw_small")(g, w, m, v)


def _pack_rows(parts, total_rows):
    rows = jnp.concatenate([p.reshape(-1, LANES) for p in parts], axis=0)
    return jnp.pad(rows, ((0, total_rows - rows.shape[0]), (0, 0)))


BIG = ("w_in", "mem_w_kv", "w_br_attn", "w_br_conv", "w_br_mem", "w_out")
BIG_AXIS = {"w_in": 1, "mem_w_kv": 0, "w_br_attn": 1, "w_br_conv": 1, "w_br_mem": 1, "w_out": 0}
SMALL = ("norm_g", "mem_norm_g", "attn_q_norm", "attn_k_norm", "mem_q_norm", "mem_k_norm")
ALL_W = ("norm_g", "mem_norm_g", "w_in", "attn_q_norm", "attn_k_norm", "conv_w", "mem_w_kv", "mem_q_norm", "mem_k_norm",
         "w_br_attn", "w_br_conv", "w_br_mem", "w_out")


def kernel(x, mem, norm_g, mem_norm_g, w_in, attn_q_norm, attn_k_norm, conv_w, mem_w_kv, mem_q_norm, mem_k_norm, w_br_attn, w_br_conv, w_br_mem, w_out, loss_target, m_norm_g, m_mem_norm_g, m_w_in, m_attn_q_norm, m_attn_k_norm, m_conv_w, m_mem_w_kv, m_mem_q_norm, m_mem_k_norm, m_w_br_attn, m_w_br_conv, m_w_br_mem, m_w_out, v_norm_g, v_mem_norm_g, v_w_in, v_attn_q_norm, v_attn_k_norm, v_conv_w, v_mem_w_kv, v_mem_q_norm, v_mem_k_norm, v_w_br_attn, v_w_br_conv, v_w_br_mem, v_w_out):
    w = dict(norm_g=norm_g, mem_norm_g=mem_norm_g, w_in=w_in, attn_q_norm=attn_q_norm, attn_k_norm=attn_k_norm, conv_w=conv_w,
             mem_w_kv=mem_w_kv, mem_q_norm=mem_q_norm, mem_k_norm=mem_k_norm, w_br_attn=w_br_attn, w_br_conv=w_br_conv,
             w_br_mem=w_br_mem, w_out=w_out)
    mo = dict(norm_g=m_norm_g, mem_norm_g=m_mem_norm_g, w_in=m_w_in, attn_q_norm=m_attn_q_norm, attn_k_norm=m_attn_k_norm,
              conv_w=m_conv_w, mem_w_kv=m_mem_w_kv, mem_q_norm=m_mem_q_norm, mem_k_norm=m_mem_k_norm, w_br_attn=m_w_br_attn,
              w_br_conv=m_w_br_conv, w_br_mem=m_w_br_mem, w_out=m_w_out)
    vo = dict(norm_g=v_norm_g, mem_norm_g=v_mem_norm_g, w_in=v_w_in, attn_q_norm=v_attn_q_norm, attn_k_norm=v_attn_k_norm,
              conv_w=v_conv_w, mem_w_kv=v_mem_w_kv, mem_q_norm=v_mem_q_norm, mem_k_norm=v_mem_k_norm, w_br_attn=v_w_br_attn,
              w_br_conv=v_w_br_conv, w_br_mem=v_w_br_mem, w_out=v_w_out)
    B, S, D = x.shape
    me = (4 * lax.axis_index("x") + 2 * lax.axis_index("y") + lax.axis_index("c")).astype(jnp.int32)

    conv_pad = jnp.pad(conv_w, ((0, 8 - conv_w.shape[0]), (0, 0)))
    full = _gather_weights([w[n].astype(bf16) for n in BIG] + [conv_pad], [BIG_AXIS[n] for n in BIG] + [None])
    wf = dict(zip(BIG, full[:-1]))
    conv_full = full[-1][:, :3, :].transpose(1, 0, 2).reshape(3, CONV_W)

    sq_err, dx, g = _local_step(x, mem, loss_target, norm_g, mem_norm_g, attn_q_norm, attn_k_norm, conv_full, mem_q_norm,
                                mem_k_norm, wf["w_in"], wf["mem_w_kv"], wf["w_br_attn"], wf["w_br_conv"], wf["w_br_mem"], wf["w_out"])

    xyc = jnp.stack([lax.axis_index("x"), lax.axis_index("y"), lax.axis_index("c")]).astype(jnp.int32)
    land_sib = _rs_sibling([g[n] for n in BIG], [BIG_AXIS[n] for n in BIG])
    sums = [_rs_chip_sum(g[n], ls, xyc, BIG_AXIS[n], "rs_sum_" + n) for n, ls in zip(BIG, land_sib)]
    land_chips = _rs_chips(sums)
    grad, delta, new_m, new_v = {}, {}, {}, {}
    for n, ls, lc in zip(BIG, land_sib, land_chips):
        grad[n], delta[n], new_m[n], new_v[n] = _adamw_shard(g[n], ls, lc, w[n], mo[n], vo[n], me.reshape(1), BIG_AXIS[n],
                                                             "adamw_" + n)

    n_rep = sum(w[n].size for n in SMALL) // LANES
    conv_rows = g["conv_w"].reshape(3, N_DEV, LANES).transpose(1, 0, 2).reshape(3 * N_DEV, LANES)
    n_rows = -(-(n_rep + 3 * N_DEV + 1) // 8) * 8
    part = _pack_rows([g[n] for n in SMALL] + [conv_rows, jnp.full((1, LANES), sq_err, f32)], n_rows)
    tot = _allreduce_small(part)
    loss = tot[n_rep + 3 * N_DEV, 0] * (0.5 / D)
    n_small = -(-(n_rep + 3) // 8) * 8
    g_small = _pack_rows([tot[:n_rep], lax.dynamic_slice(tot, (n_rep + 3 * me, 0), (3, LANES))], n_small)
    names = SMALL + ("conv_w",)
    d_s, m_s, v_s = _adamw_small(g_small, _pack_rows([w[n] for n in names], n_small), _pack_rows([mo[n] for n in names], n_small),
                                 _pack_rows([vo[n] for n in names], n_small))
    off = 0
    for n in names:
        r = w[n].size // LANES
        grad[n], delta[n] = g_small[off:off + r].reshape(w[n].shape), d_s[off:off + r].reshape(w[n].shape)
        new_m[n], new_v[n] = m_s[off:off + r].reshape(w[n].shape), v_s[off:off + r].reshape(w[n].shape)
        off += r
    return (loss, dx.reshape(B, S, D), *[grad[n] for n in ALL_W], *[delta[n] for n in ALL_W], *[new_m[n] for n in ALL_W],
            *[new_v[n] for n in ALL_W])
```

```python
import functools

import jax
import jax.numpy as jnp
from jax import lax
from jax.experimental import pallas as pl
from jax.experimental.pallas import tpu as pltpu

f32 = jnp.float32
bf16 = jnp.bfloat16

N_DEV = 8
HEAD_DIM = 128
DILATIONS = (1, 4, 16)
N_GROUPS = 3
HEADS = 4
BLK = 128
ATTN_QKV = N_GROUPS * HEADS * HEAD_DIM
ATTN_OUT = HEADS * HEAD_DIM
CONV_W = 1024
MEM_LEN = 256
MEM_HEADS = 4
MEM_HD = 256
MEM_W = MEM_HEADS * MEM_HD
EPS = 1e-6
Q0, K0, V0 = 0, ATTN_QKV, 2 * ATTN_QKV
ZA = 3 * ATTN_QKV
CB, CC, CV, ZC = ZA + ATTN_OUT, ZA + ATTN_OUT + CONV_W, ZA + ATTN_OUT + 2 * CONV_W, ZA + ATTN_OUT + 3 * CONV_W
MQ = ZC + CONV_W
ZM = MQ + MEM_W
GT = ZM + MEM_W

ADAM_LR, ADAM_B1, ADAM_B2, ADAM_EPS, ADAM_WD, ADAM_STEP = 0.001, 0.9, 0.999, 1e-08, 0.01, 10

VMEM_LIMIT = 56 * 1024 * 1024
LANES = 128

NT_DIMS = (((1,), (1,)), ((), ()))
TN_DIMS = (((0,), (0,)), ((), ()))
NN_DIMS = (((1,), (0,)), ((), ()))


def _params(sem=None):
    return pltpu.CompilerParams(dimension_semantics=sem, vmem_limit_bytes=VMEM_LIMIT)


def _pick(n, pref, q=LANES):
    t = (min(pref, n) // q) * q
    while t >= q:
        if n % t == 0:
            return t
        t -= q
    return n


def _dot(a, b, dims):
    return lax.dot_general(a.astype(bf16), b.astype(bf16), dims, preferred_element_type=f32)


def _sigmoid(z):
    return 1.0 / (1.0 + jnp.exp(-z))


def _rstd(v):
    return lax.rsqrt(jnp.mean(v * v, axis=-1, keepdims=True) + EPS)


def _mm(a, b, *, mode, out_dtype, name, tm=1024, tn=1024, tk=4096):
    if mode == "nn":
        (M, K), (K2, N) = a.shape, b.shape
    elif mode == "nt":
        (M, K), (N, K2) = a.shape, b.shape
    else:
        (K, M), (K2, N) = a.shape, b.shape
    assert K == K2
    tm, tn, tk = _pick(M, tm), _pick(N, tn), _pick(K, tk)
    nk = K // tk
    dims = {"nn": NN_DIMS, "nt": NT_DIMS, "tn": TN_DIMS}[mode]

    def body(a_ref, b_ref, o_ref, *scratch, nk):
        if nk == 1:
            o_ref[...] = _dot(a_ref[...], b_ref[...], dims).astype(o_ref.dtype)
            return
        acc_ref, = scratch
        k = pl.program_id(2)

        @pl.when(k == 0)
        def _():
            acc_ref[...] = _dot(a_ref[...], b_ref[...], dims)

        @pl.when((k > 0) & (k < nk - 1))
        def _():
            acc_ref[...] += _dot(a_ref[...], b_ref[...], dims)

        @pl.when(k == nk - 1)
        def _():
            o_ref[...] = (acc_ref[...] + _dot(a_ref[...], b_ref[...], dims)).astype(o_ref.dtype)

    if mode == "tn":
        a_spec = pl.BlockSpec((tk, tm), lambda i, j, k: (k, i))
    else:
        a_spec = pl.BlockSpec((tm, tk), lambda i, j, k: (i, k))
    if mode == "nt":
        b_spec = pl.BlockSpec((tn, tk), lambda i, j, k: (j, k))
    else:
        b_spec = pl.BlockSpec((tk, tn), lambda i, j, k: (k, j))
    return pl.pallas_call(
        functools.partial(body, nk=nk),
        grid=(M // tm, N // tn, nk),
        in_specs=[a_spec, b_spec],
        out_specs=pl.BlockSpec((tm, tn), lambda i, j, k: (i, j)),
        out_shape=jax.ShapeDtypeStruct((M, N), out_dtype),
        scratch_shapes=[pltpu.VMEM((tm, tn), f32)] if nk > 1 else [],
        compiler_params=_params(("parallel", "parallel", "arbitrary")),
        name=name,
    )(a, b)


def _rms_fwd(x, g, name):
    T, D = x.shape
    tm = _pick(T, 256, 8)

    def body(x_ref, g_ref, h_ref):
        xv = x_ref[...]
        h_ref[...] = (xv * _rstd(xv) * g_ref[...]).astype(bf16)

    return pl.pallas_call(
        body, grid=(T // tm,),
        in_specs=[pl.BlockSpec((tm, D), lambda i: (i, 0)), pl.BlockSpec((1, D), lambda i: (0, 0))],
        out_specs=pl.BlockSpec((tm, D), lambda i: (i, 0)),
        out_shape=jax.ShapeDtypeStruct((T, D), bf16),
        compiler_params=_params(("parallel",)), name=name,
    )(x, g)


def _rms_bwd(x, g, dh, dy, name):
    T, D = x.shape
    tm = _pick(T, 256, 8)
    with_dx = dy is not None

    def body(*refs):
        if with_dx:
            x_ref, g_ref, dh_ref, dy_ref, dx_ref, dg_ref = refs
        else:
            x_ref, g_ref, dh_ref, dg_ref = refs
        xv = x_ref[...]
        r = _rstd(xv)
        xh = xv * r
        dhv = dh_ref[...]
        part = jnp.sum(dhv * xh, axis=0, keepdims=True)

        @pl.when(pl.program_id(0) == 0)
        def _():
            dg_ref[...] = part

        @pl.when(pl.program_id(0) > 0)
        def _():
            dg_ref[...] += part

        if with_dx:
            dxh = dhv * g_ref[...]
            dx_ref[...] = dy_ref[...] + r * (dxh - xh * jnp.mean(dxh * xh, axis=-1, keepdims=True))

    row = pl.BlockSpec((tm, D), lambda i: (i, 0))
    vec = pl.BlockSpec((1, D), lambda i: (0, 0))
    if with_dx:
        return pl.pallas_call(
            body, grid=(T // tm,), in_specs=[row, vec, row, row], out_specs=[row, vec],
            out_shape=[jax.ShapeDtypeStruct((T, D), f32), jax.ShapeDtypeStruct((1, D), f32)],
            compiler_params=_params(("arbitrary",)), name=name,
        )(x, g, dh, dy)
    return pl.pallas_call(
        body, grid=(T // tm,), in_specs=[row, vec, row], out_specs=vec,
        out_shape=jax.ShapeDtypeStruct((1, D), f32),
        compiler_params=_params(("arbitrary",)), name=name,
    )(x, g, dh)


def _rows(start, size, stride):
    return pl.ds(start, size) if stride == 1 else pl.ds(start, size, stride=stride)


def _attn_units(S, d, first, prev):
    nb = S // d // BLK

    def do_first(r, c):
        first(_rows(r, BLK, d))
        return c

    lax.fori_loop(0, d, do_first, 0, unroll=min(d, 2))
    if nb > 1:
        def do_prev(u, c):
            r = u // (nb - 1)
            b = u % (nb - 1) + 1
            base = r + d * b * BLK
            prev(_rows(base, BLK, d), _rows(base - d * BLK, 2 * BLK, d))
            return c

        lax.fori_loop(0, d * (nb - 1), do_prev, 0, unroll=2)


def _band_mask(nkeys):
    i = lax.broadcasted_iota(jnp.int32, (BLK, nkeys), 0)
    j = lax.broadcasted_iota(jnp.int32, (BLK, nkeys), 1)
    if nkeys == BLK:
        return j <= i
    return (j >= i) & (j <= i + BLK)


def _normalize_into(src_ref, gain, dst_ref, S):
    for c in range(S // 256):
        rows = pl.ds(c * 256, 256)
        v = src_ref[rows, :]
        dst_ref[rows, :] = v * _rstd(v) * gain


def _attn_fwd(proj, gq, gk, B, S):
    T, NC = proj.shape
    scale = HEAD_DIM ** -0.5

    def body(q_ref, k_ref, v_ref, z_ref, gq_ref, gk_ref, ag_ref, a_ref, lse_ref, qs, ks, o0, o1, o2, l0, l1, l2):
        g = pl.program_id(2)
        _normalize_into(q_ref, gq_ref[pl.ds(g, 1), :], qs, S)
        _normalize_into(k_ref, gk_ref[pl.ds(g, 1), :], ks, S)
        o_s, l_s = (o0, o1, o2), (l0, l1, l2)

        def run(gi):
            def unit(qr, kr, nkeys):
                s = _dot(qs[qr, :], ks[kr, :], NT_DIMS) * scale
                s = jnp.where(_band_mask(nkeys), s, -jnp.inf)
                m = jnp.max(s, axis=-1, keepdims=True)
                p = jnp.exp(s - m)
                den = jnp.sum(p, axis=-1, keepdims=True)
                o_s[gi][qr, :] = _dot(p / den, v_ref[kr, :], NN_DIMS)
                l_s[gi][qr, :] = jnp.broadcast_to(m + jnp.log(den), (BLK, HEAD_DIM))

            _attn_units(S, DILATIONS[gi], lambda qr: unit(qr, qr, BLK), lambda qr, kr: unit(qr, kr, 2 * BLK))

        for gi in range(N_GROUPS):
            pl.when(g == gi)(functools.partial(run, gi))

        @pl.when(g == N_GROUPS - 1)
        def _():
            for c in range(S // 256):
                rows = pl.ds(c * 256, 256)
                la, lb, lc = l0[rows, :], l1[rows, :], l2[rows, :]
                m = jnp.maximum(jnp.maximum(la, lb), lc)
                wa, wb, wc = jnp.exp(la - m), jnp.exp(lb - m), jnp.exp(lc - m)
                tot = wa + wb + wc
                a = (wa / tot) * o0[rows, :] + (wb / tot) * o1[rows, :] + (wc / tot) * o2[rows, :]
                z = z_ref[rows, :]
                a_ref[rows, :] = a
                lse_ref[rows, :] = m + jnp.log(tot)
                ag_ref[rows, :] = (a * (z * _sigmoid(z))).astype(bf16)

    def slab(col0):
        return pl.BlockSpec((S, HEAD_DIM), lambda b, h, g: (b, col0 // HEAD_DIM + g * HEADS + h))

    gain = pl.BlockSpec((N_GROUPS, HEAD_DIM), lambda b, h, g: (0, 0))
    out = pl.BlockSpec((S, HEAD_DIM), lambda b, h, g: (b, h))
    return pl.pallas_call(
        body, grid=(B, HEADS, N_GROUPS),
        in_specs=[slab(Q0), slab(K0), slab(V0), pl.BlockSpec((S, HEAD_DIM), lambda b, h, g: (b, ZA // HEAD_DIM + h)), gain, gain],
        out_specs=[out, out, out],
        out_shape=[jax.ShapeDtypeStruct((T, ATTN_OUT), bf16), jax.ShapeDtypeStruct((T, ATTN_OUT), f32),
                   jax.ShapeDtypeStruct((T, ATTN_OUT), f32)],
        scratch_shapes=[pltpu.VMEM((S, HEAD_DIM), f32)] * 8,
        compiler_params=_params(("arbitrary", "arbitrary", "arbitrary")), name="attn_fwd",
    )(proj, proj, proj, proj, gq, gk)


def _rms_bwd_rows(raw, gain, dn):
    r = _rstd(raw)
    xh = raw * r
    dxh = dn * gain
    return r * (dxh - xh * jnp.mean(dxh * xh, axis=-1, keepdims=True)), jnp.sum(dn * xh, axis=0, keepdims=True)


def _attn_bwd(proj, gq, gk, a_comb, lse, dag, dproj, B, S):
    T, NC = proj.shape
    scale = HEAD_DIM ** -0.5

    def body(q_ref, k_ref, v_ref, z_ref, gq_ref, gk_ref, a_ref, lse_ref, dag_ref, dproj_in, dproj_ref, dgq_ref, dgk_ref,
             qs, ks, da_s, dl_s, dq_s, dk_s, dv_s, stage, sem):
        b, h, g = pl.program_id(0), pl.program_id(1), pl.program_id(2)
        gq_row, gk_row = gq_ref[pl.ds(g, 1), :], gk_ref[pl.ds(g, 1), :]
        _normalize_into(q_ref, gq_row, qs, S)
        _normalize_into(k_ref, gk_row, ks, S)

        def put(slot, col0):
            cp = pltpu.make_async_copy(stage.at[slot], dproj_ref.at[pl.ds(b * S, S), pl.ds(col0, HEAD_DIM)], sem.at[slot])
            cp.start()
            return cp

        @pl.when((b == 0) & (h == 0) & (g == 0))
        def _():
            dgq_ref[...] = jnp.zeros_like(dgq_ref)
            dgk_ref[...] = jnp.zeros_like(dgk_ref)

        @pl.when(g == 0)
        def _():
            for c in range(S // 256):
                rows = pl.ds(c * 256, 256)
                z, a, dg_ = z_ref[rows, :], a_ref[rows, :], dag_ref[rows, :]
                sg = _sigmoid(z)
                da = dg_ * (z * sg)
                da_s[rows, :] = da
                dl_s[rows, :] = jnp.broadcast_to(jnp.sum(da * a, axis=-1, keepdims=True), (256, HEAD_DIM))
                stage[3, rows, :] = (dg_ * a * (sg * (1.0 + z * (1.0 - sg)))).astype(bf16)
            put(3, ZA + h * HEAD_DIM).wait()

        dk_s[...] = jnp.zeros_like(dk_s)
        dv_s[...] = jnp.zeros_like(dv_s)

        def run(gi):
            def unit(qr, kr, nkeys):
                q, k, v = qs[qr, :], ks[kr, :], v_ref[kr, :]
                s = _dot(q, k, NT_DIMS) * scale
                p = jnp.where(_band_mask(nkeys), jnp.exp(s - lse_ref[qr, :][:, :1]), 0.0)
                da = da_s[qr, :]
                dp = _dot(da, v, NT_DIMS)
                ds = p * (dp - dl_s[qr, :][:, :1]) * scale
                dq_s[qr, :] = _dot(ds, k, NN_DIMS)
                dk_s[kr, :] += _dot(ds, q, TN_DIMS)
                dv_s[kr, :] += _dot(p, da, TN_DIMS)

            _attn_units(S, DILATIONS[gi], lambda qr: unit(qr, qr, BLK), lambda qr, kr: unit(qr, kr, 2 * BLK))

        for gi in range(N_GROUPS):
            pl.when(g == gi)(functools.partial(run, gi))

        gq_acc = jnp.zeros((1, HEAD_DIM), f32)
        gk_acc = jnp.zeros((1, HEAD_DIM), f32)
        for c in range(S // 256):
            rows = pl.ds(c * 256, 256)
            dq, gq_p = _rms_bwd_rows(q_ref[rows, :], gq_row, dq_s[rows, :])
            dk, gk_p = _rms_bwd_rows(k_ref[rows, :], gk_row, dk_s[rows, :])
            gq_acc, gk_acc = gq_acc + gq_p, gk_acc + gk_p
            stage[0, rows, :] = dq.astype(bf16)
            stage[1, rows, :] = dk.astype(bf16)
            stage[2, rows, :] = dv_s[rows, :].astype(bf16)
        dgq_ref[pl.ds(g, 1), :] += gq_acc
        dgk_ref[pl.ds(g, 1), :] += gk_acc
        col = (g * HEADS + h) * HEAD_DIM
        cps = [put(0, Q0 + col), put(1, K0 + col), put(2, V0 + col)]
        for cp in cps:
            cp.wait()

    def slab(col0):
        return pl.BlockSpec((S, HEAD_DIM), lambda b, h, g: (b, col0 // HEAD_DIM + g * HEADS + h))

    gain = pl.BlockSpec((N_GROUPS, HEAD_DIM), lambda b, h, g: (0, 0))
    per_slot = pl.BlockSpec((S, HEAD_DIM), lambda b, h, g: (b, h))
    return pl.pallas_call(
        body, grid=(B, HEADS, N_GROUPS),
        in_specs=[slab(Q0), slab(K0), slab(V0), pl.BlockSpec((S, HEAD_DIM), lambda b, h, g: (b, ZA // HEAD_DIM + h)), gain, gain,
                  per_slot, per_slot, per_slot, pl.BlockSpec(memory_space=pl.ANY)],
        out_specs=[pl.BlockSpec(memory_space=pl.ANY), gain, gain],
        out_shape=[jax.ShapeDtypeStruct(dproj.shape, dproj.dtype), jax.ShapeDtypeStruct((N_GROUPS, HEAD_DIM), f32),
                   jax.ShapeDtypeStruct((N_GROUPS, HEAD_DIM), f32)],
        scratch_shapes=[pltpu.VMEM((S, HEAD_DIM), f32)] * 7 + [pltpu.VMEM((4, S, HEAD_DIM), bf16), pltpu.SemaphoreType.DMA((4,))],
        input_output_aliases={9: 0},
        compiler_params=_params(("arbitrary", "arbitrary", "arbitrary")), name="attn_bwd",
    )(proj, proj, proj, proj, gq, gk, a_comb, lse, dag, dproj)


def _conv_fwd(proj, conv_w, B, S):
    T, NC = proj.shape
    tc = LANES

    def body(cb_ref, cc_ref, cv_ref, z_ref, w_ref, c_ref, ub):
        u = cc_ref[...] * cv_ref[...]
        ub[0:8, :] = jnp.zeros((8, tc), f32)
        ub[8:8 + S, :] = u
        w = w_ref[...]
        y = w[0:1] * u + w[1:2] * ub[pl.ds(7, S), :] + w[2:3] * ub[pl.ds(6, S), :]
        z = z_ref[...]
        c_ref[...] = (cb_ref[...] * y * (z * _sigmoid(z))).astype(bf16)

    def seg(col0):
        return pl.BlockSpec((S, tc), lambda j, b: (b, col0 // tc + j))

    return pl.pallas_call(
        body, grid=(CONV_W // tc, B),
        in_specs=[seg(CB), seg(CC), seg(CV), seg(ZC), pl.BlockSpec((3, tc), lambda j, b: (0, j))],
        out_specs=pl.BlockSpec((S, tc), lambda j, b: (b, j)),
        out_shape=jax.ShapeDtypeStruct((T, CONV_W), bf16),
        scratch_shapes=[pltpu.VMEM((S + 8, tc), f32)],
        compiler_params=_params(("parallel", "parallel")), name="conv_fwd",
    )(proj, proj, proj, proj, conv_w)


def _conv_bwd(proj, conv_w, dc, dproj, B, S):
    T, NC = proj.shape
    tc = LANES

    def body(cb_ref, cc_ref, cv_ref, z_ref, w_ref, dc_ref, dproj_in, dproj_ref, dw_ref, ub, db, stage, sem):
        j, b = pl.program_id(0), pl.program_id(1)
        cb, cc, cv, z, dcv = cb_ref[...], cc_ref[...], cv_ref[...], z_ref[...], dc_ref[...]
        u = cc * cv
        ub[0:8, :] = jnp.zeros((8, tc), f32)
        ub[8:8 + S, :] = u
        u1, u2 = ub[pl.ds(7, S), :], ub[pl.ds(6, S), :]
        w = w_ref[...]
        y = w[0:1] * u + w[1:2] * u1 + w[2:3] * u2
        sg = _sigmoid(z)
        si = z * sg
        dyv = dcv * cb * si
        db[0:S, :] = dyv
        db[S:S + 8, :] = jnp.zeros((8, tc), f32)
        du = w[0:1] * dyv + w[1:2] * db[pl.ds(1, S), :] + w[2:3] * db[pl.ds(2, S), :]
        stage[0] = (dcv * y * si).astype(bf16)
        stage[1] = (du * cv).astype(bf16)
        stage[2] = (du * cc).astype(bf16)
        stage[3] = (dcv * cb * y * (sg * (1.0 + z * (1.0 - sg)))).astype(bf16)
        cps = []
        for slot, col0 in enumerate((CB, CC, CV, ZC)):
            cp = pltpu.make_async_copy(stage.at[slot], dproj_ref.at[pl.ds(b * S, S), pl.ds(col0 + j * tc, tc)], sem.at[slot])
            cp.start()
            cps.append(cp)
        part = jnp.concatenate([jnp.sum(dyv * u, axis=0, keepdims=True), jnp.sum(dyv * u1, axis=0, keepdims=True),
                                jnp.sum(dyv * u2, axis=0, keepdims=True)], axis=0)

        @pl.when(b == 0)
        def _():
            dw_ref[...] = part

        @pl.when(b > 0)
        def _():
            dw_ref[...] += part

        for cp in cps:
            cp.wait()

    def seg(col0):
        return pl.BlockSpec((S, tc), lambda j, b: (b, col0 // tc + j))

    return pl.pallas_call(
        body, grid=(CONV_W // tc, B),
        in_specs=[seg(CB), seg(CC), seg(CV), seg(ZC), pl.BlockSpec((3, tc), lambda j, b: (0, j)),
                  pl.BlockSpec((S, tc), lambda j, b: (b, j)), pl.BlockSpec(memory_space=pl.ANY)],
        out_specs=[pl.BlockSpec(memory_space=pl.ANY), pl.BlockSpec((3, tc), lambda j, b: (0, j))],
        out_shape=[jax.ShapeDtypeStruct(dproj.shape, dproj.dtype), jax.ShapeDtypeStruct((3, CONV_W), f32)],
        scratch_shapes=[pltpu.VMEM((S + 8, tc), f32), pltpu.VMEM((S + 8, tc), f32), pltpu.VMEM((4, S, tc), bf16),
                        pltpu.SemaphoreType.DMA((4,))],
        input_output_aliases={6: 0},
        compiler_params=_params(("arbitrary", "arbitrary")), name="conv_bwd",
    )(proj, proj, proj, proj, conv_w, dc, dproj)


MEM_CHUNK = 256


def _mem_fwd(proj, mkv, gq, gk, B, S):
    T, NC = proj.shape
    scale = MEM_HD ** -0.5

    def body(q_ref, z_ref, k_ref, v_ref, gq_ref, gk_ref, o_ref):
        kv = k_ref[...]
        kn = (kv * _rstd(kv) * gk_ref[...]).astype(bf16)
        vv = v_ref[...].astype(bf16)

        def chunk(c, carry):
            rows = pl.ds(pl.multiple_of(c * MEM_CHUNK, MEM_CHUNK), MEM_CHUNK)
            q = q_ref[rows, :]
            qn = q * _rstd(q) * gq_ref[...]
            s = _dot(qn, kn, NT_DIMS) * scale
            p = jnp.exp(s - jnp.max(s, axis=-1, keepdims=True))
            p = p / jnp.sum(p, axis=-1, keepdims=True)
            z = z_ref[rows, :]
            o_ref[rows, :] = (_dot(p, vv, NN_DIMS) * (z * _sigmoid(z))).astype(bf16)
            return carry

        lax.fori_loop(0, S // MEM_CHUNK, chunk, 0)

    gain = pl.BlockSpec((1, MEM_HD), lambda b, h: (0, 0))
    return pl.pallas_call(
        body, grid=(B, MEM_HEADS),
        in_specs=[pl.BlockSpec((S, MEM_HD), lambda b, h: (b, MQ // MEM_HD + h)),
                  pl.BlockSpec((S, MEM_HD), lambda b, h: (b, ZM // MEM_HD + h)),
                  pl.BlockSpec((MEM_LEN, MEM_HD), lambda b, h: (b, h)),
                  pl.BlockSpec((MEM_LEN, MEM_HD), lambda b, h: (b, MEM_HEADS + h)), gain, gain],
        out_specs=pl.BlockSpec((S, MEM_HD), lambda b, h: (b, h)),
        out_shape=jax.ShapeDtypeStruct((T, MEM_W), bf16),
        compiler_params=_params(("parallel", "parallel")), name="mem_fwd",
    )(proj, proj, mkv, mkv, gq, gk)


def _mem_bwd(proj, mkv, gq, gk, dmo, dproj, B, S):
    T, NC = proj.shape
    scale = MEM_HD ** -0.5

    def body(q_ref, z_ref, k_ref, v_ref, gq_ref, gk_ref, dmo_ref, dproj_in, dproj_ref, dmk_ref, dmv_ref, dgq_ref, dgk_ref,
             dkn_s, dv_s, gq_s, stage, sem):
        b, h = pl.program_id(0), pl.program_id(1)
        kv = k_ref[...]
        kn = (kv * _rstd(kv) * gk_ref[...]).astype(bf16)
        vv = v_ref[...].astype(bf16)
        dkn_s[...] = jnp.zeros_like(dkn_s)
        dv_s[...] = jnp.zeros_like(dv_s)
        gq_s[...] = jnp.zeros_like(gq_s)

        def chunk(c, carry):
            rows = pl.ds(pl.multiple_of(c * MEM_CHUNK, MEM_CHUNK), MEM_CHUNK)
            q = q_ref[rows, :]
            qn = q * _rstd(q) * gq_ref[...]
            s = _dot(qn, kn, NT_DIMS) * scale
            p = jnp.exp(s - jnp.max(s, axis=-1, keepdims=True))
            p = p / jnp.sum(p, axis=-1, keepdims=True)
            mo = _dot(p, vv, NN_DIMS)
            z, dg_ = z_ref[rows, :], dmo_ref[rows, :]
            sg = _sigmoid(z)
            do = dg_ * (z * sg)
            stage[1, rows, :] = (dg_ * mo * (sg * (1.0 + z * (1.0 - sg)))).astype(bf16)
            dp = _dot(do, vv, NT_DIMS)
            ds = p * (dp - jnp.sum(do * mo, axis=-1, keepdims=True)) * scale
            dq, gq_p = _rms_bwd_rows(q, gq_ref[...], _dot(ds, kn, NN_DIMS))
            stage[0, rows, :] = dq.astype(bf16)
            gq_s[...] += gq_p
            dkn_s[...] += _dot(ds, qn, TN_DIMS)
            dv_s[...] += _dot(p, do, TN_DIMS)
            return carry

        lax.fori_loop(0, S // MEM_CHUNK, chunk, 0)
        cps = []
        for slot, col0 in enumerate((MQ, ZM)):
            cp = pltpu.make_async_copy(stage.at[slot], dproj_ref.at[pl.ds(b * S, S), pl.ds(col0 + h * MEM_HD, MEM_HD)], sem.at[slot])
            cp.start()
            cps.append(cp)
        dk, gk_p = _rms_bwd_rows(kv, gk_ref[...], dkn_s[...])
        dmk_ref[...] = dk
        dmv_ref[...] = dv_s[...]

        @pl.when((b == 0) & (h == 0))
        def _():
            dgq_ref[...] = gq_s[...]
            dgk_ref[...] = gk_p

        @pl.when((b > 0) | (h > 0))
        def _():
            dgq_ref[...] += gq_s[...]
            dgk_ref[...] += gk_p

        for cp in cps:
            cp.wait()

    gain = pl.BlockSpec((1, MEM_HD), lambda b, h: (0, 0))
    kvb = pl.BlockSpec((MEM_LEN, MEM_HD), lambda b, h: (b, h))
    return pl.pallas_call(
        body, grid=(B, MEM_HEADS),
        in_specs=[pl.BlockSpec((S, MEM_HD), lambda b, h: (b, MQ // MEM_HD + h)),
                  pl.BlockSpec((S, MEM_HD), lambda b, h: (b, ZM // MEM_HD + h)),
                  kvb, pl.BlockSpec((MEM_LEN, MEM_HD), lambda b, h: (b, MEM_HEADS + h)), gain, gain,
                  pl.BlockSpec((S, MEM_HD), lambda b, h: (b, h)), pl.BlockSpec(memory_space=pl.ANY)],
        out_specs=[pl.BlockSpec(memory_space=pl.ANY), kvb, kvb, gain, gain],
        out_shape=[jax.ShapeDtypeStruct(dproj.shape, dproj.dtype), jax.ShapeDtypeStruct((B * MEM_LEN, MEM_W), f32),
                   jax.ShapeDtypeStruct((B * MEM_LEN, MEM_W), f32), jax.ShapeDtypeStruct((1, MEM_HD), f32),
                   jax.ShapeDtypeStruct((1, MEM_HD), f32)],
        scratch_shapes=[pltpu.VMEM((MEM_LEN, MEM_HD), f32), pltpu.VMEM((MEM_LEN, MEM_HD), f32), pltpu.VMEM((1, MEM_HD), f32),
                        pltpu.VMEM((2, S, MEM_HD), bf16), pltpu.SemaphoreType.DMA((2,))],
        input_output_aliases={7: 0},
        compiler_params=_params(("arbitrary", "arbitrary")), name="mem_bwd",
    )(proj, proj, mkv, mkv, gq, gk, dmo, dproj)


def _merge_fwd(proj, ag, cg, mg, wa, wc, wm):
    T, NC = proj.shape
    D = wa.shape[1]
    tm, tn = _pick(T, 1024), _pick(D, 512)
    assert GT % tn == 0

    def body(ag_ref, cg_ref, mg_ref, wa_ref, wc_ref, wm_ref, g0_ref, g1_ref, g2_ref, mer_ref, ba_ref, bc_ref, bm_ref):
        ba = _dot(ag_ref[...], wa_ref[...], NN_DIMS)
        bc = _dot(cg_ref[...], wc_ref[...], NN_DIMS)
        bm = _dot(mg_ref[...], wm_ref[...], NN_DIMS)
        mer_ref[...] = (_sigmoid(g0_ref[...]) * ba + _sigmoid(g1_ref[...]) * bc + _sigmoid(g2_ref[...]) * bm).astype(bf16)
        ba_ref[...] = ba.astype(bf16)
        bc_ref[...] = bc.astype(bf16)
        bm_ref[...] = bm.astype(bf16)

    def act(w):
        return pl.BlockSpec((tm, w), lambda i, j: (i, 0))

    def wt(k):
        return pl.BlockSpec((k, tn), lambda i, j: (0, j))

    def gate(n):
        return pl.BlockSpec((tm, tn), lambda i, j: (i, (GT + n * D) // tn + j))

    out = pl.BlockSpec((tm, tn), lambda i, j: (i, j))
    return pl.pallas_call(
        body, grid=(T // tm, D // tn),
        in_specs=[act(ATTN_OUT), act(CONV_W), act(MEM_W), wt(ATTN_OUT), wt(CONV_W), wt(MEM_W), gate(0), gate(1), gate(2)],
        out_specs=[out] * 4, out_shape=[jax.ShapeDtypeStruct((T, D), bf16)] * 4,
        compiler_params=_params(("parallel", "parallel")), name="merge_fwd",
    )(ag, cg, mg, wa, wc, wm, proj, proj, proj)


def _out_loss(merged, w_out, x, tgt):
    T, D = x.shape
    tm, tn = _pick(T, 512), _pick(D, 512)

    def body(m_ref, w_ref, x_ref, t_ref, dy_ref, dyb_ref, lp_ref):
        e = x_ref[...] + _dot(m_ref[...], w_ref[...], NN_DIMS) - t_ref[...]
        dy = e / D
        dy_ref[...] = dy
        dyb_ref[...] = dy.astype(bf16)
        part = jnp.full((1, 8, LANES), jnp.sum(e * e), f32)

        @pl.when(pl.program_id(1) == 0)
        def _():
            lp_ref[...] = part

        @pl.when(pl.program_id(1) > 0)
        def _():
            lp_ref[...] += part

    tile = pl.BlockSpec((tm, tn), lambda i, j: (i, j))
    return pl.pallas_call(
        body, grid=(T // tm, D // tn),
        in_specs=[pl.BlockSpec((tm, D), lambda i, j: (i, 0)), pl.BlockSpec((D, tn), lambda i, j: (0, j)), tile, tile],
        out_specs=[tile, tile, pl.BlockSpec((1, 8, LANES), lambda i, j: (i, 0, 0))],
        out_shape=[jax.ShapeDtypeStruct((T, D), f32), jax.ShapeDtypeStruct((T, D), bf16),
                   jax.ShapeDtypeStruct((T // tm, 8, LANES), f32)],
        compiler_params=_params(("parallel", "arbitrary")), name="out_loss",
    )(merged, w_out, x, tgt)


def _merge_bwd(proj, dyb, w_out, ba, bc, bm):
    T, NC = proj.shape
    D = w_out.shape[0]
    tm, tn = _pick(T, 512), _pick(D, 512)
    assert GT % tn == 0

    def body(dy_ref, w_ref, ba_ref, bc_ref, bm_ref, g0_ref, g1_ref, g2_ref, da_ref, dc_ref, dm_ref, dproj_ref, stage, sem):
        i, j = pl.program_id(0), pl.program_id(1)
        dmer = _dot(dy_ref[...], w_ref[...], NT_DIMS)
        cps = []
        for n, (g_ref, br_ref, o_ref) in enumerate(((g0_ref, ba_ref, da_ref), (g1_ref, bc_ref, dc_ref), (g2_ref, bm_ref, dm_ref))):
            sg = _sigmoid(g_ref[...])
            o_ref[...] = (sg * dmer).astype(bf16)
            stage[n] = (dmer * br_ref[...].astype(f32) * (sg * (1.0 - sg))).astype(bf16)
            cp = pltpu.make_async_copy(stage.at[n], dproj_ref.at[pl.ds(i * tm, tm), pl.ds(GT + n * D + j * tn, tn)], sem.at[n])
            cp.start()
            cps.append(cp)
        for cp in cps:
            cp.wait()

    tile = pl.BlockSpec((tm, tn), lambda i, j: (i, j))

    def gate(n):
        return pl.BlockSpec((tm, tn), lambda i, j: (i, (GT + n * D) // tn + j))

    return pl.pallas_call(
        body, grid=(T // tm, D // tn),
        in_specs=[pl.BlockSpec((tm, D), lambda i, j: (i, 0)), pl.BlockSpec((tn, D), lambda i, j: (j, 0)), tile, tile, tile,
                  gate(0), gate(1), gate(2)],
        out_specs=[tile, tile, tile, pl.BlockSpec(memory_space=pl.ANY)],
        out_shape=[jax.ShapeDtypeStruct((T, D), bf16)] * 3 + [jax.ShapeDtypeStruct((T, NC), bf16)],
        scratch_shapes=[pltpu.VMEM((3, tm, tn), bf16), pltpu.SemaphoreType.DMA((3,))],
        compiler_params=_params(("arbitrary", "arbitrary")), name="merge_bwd",
    )(dyb, w_out, ba, bc, bm, proj, proj, proj)


def _local_step(x, mem, tgt, norm_g, mem_norm_g, attn_q_norm, attn_k_norm, conv_w, mem_q_norm, mem_k_norm,
                w_in, w_kv, w_a, w_c, w_m, w_out):
    B, S, D = x.shape
    T = B * S
    x2, t2, mem2 = x.reshape(T, D), tgt.reshape(T, D), mem.reshape(B * MEM_LEN, D)
    ng, mng = norm_g.reshape(1, D), mem_norm_g.reshape(1, D)
    mqn, mkn = mem_q_norm.reshape(1, MEM_HD), mem_k_norm.reshape(1, MEM_HD)

    h = _rms_fwd(x2, ng, "rms_x")
    proj = _mm(h, w_in, mode="nn", out_dtype=f32, name="proj")
    mh = _rms_fwd(mem2, mng, "rms_mem")
    mkv = _mm(mh, w_kv, mode="nn", out_dtype=f32, name="mkv")
    ag, a_comb, lse = _attn_fwd(proj, attn_q_norm, attn_k_norm, B, S)
    cg = _conv_fwd(proj, conv_w, B, S)
    mg = _mem_fwd(proj, mkv, mqn, mkn, B, S)
    merged, ba, bc, bm = _merge_fwd(proj, ag, cg, mg, w_a, w_c, w_m)
    dy, dyb, lp = _out_loss(merged, w_out, x2, t2)
    sq_err = jnp.sum(lp[:, 0, 0])

    g = {}
    g["w_out"] = _mm(merged, dyb, mode="tn", out_dtype=f32, name="dw_out")
    dba, dbc, dbm, dproj = _merge_bwd(proj, dyb, w_out, ba, bc, bm)
    g["w_br_attn"] = _mm(ag, dba, mode="tn", out_dtype=f32, name="dw_br_attn")
    g["w_br_conv"] = _mm(cg, dbc, mode="tn", out_dtype=f32, name="dw_br_conv")
    g["w_br_mem"] = _mm(mg, dbm, mode="tn", out_dtype=f32, name="dw_br_mem")
    dag = _mm(dba, w_a, mode="nt", out_dtype=f32, name="d_attn_out")
    dcg = _mm(dbc, w_c, mode="nt", out_dtype=f32, name="d_conv_out")
    dmg = _mm(dbm, w_m, mode="nt", out_dtype=f32, name="d_mem_out")
    dproj, g["attn_q_norm"], g["attn_k_norm"] = _attn_bwd(proj, attn_q_norm, attn_k_norm, a_comb, lse, dag, dproj, B, S)
    dproj, g["conv_w"] = _conv_bwd(proj, conv_w, dcg, dproj, B, S)
    dproj, dmk, dmv, dgmq, dgmk = _mem_bwd(proj, mkv, mqn, mkn, dmg, dproj, B, S)
    g["mem_q_norm"], g["mem_k_norm"] = dgmq.reshape(MEM_HD), dgmk.reshape(MEM_HD)
    dmkv = jnp.concatenate([dmk, dmv], axis=1)
    g["mem_w_kv"] = _mm(mh, dmkv, mode="tn", out_dtype=f32, name="dw_kv")
    dmh = _mm(dmkv, w_kv, mode="nt", out_dtype=f32, name="d_mem_h")
    g["mem_norm_g"] = _rms_bwd(mem2, mng, dmh, None, "rms_mem_bwd").reshape(D)
    g["w_in"] = _mm(h, dproj, mode="tn", out_dtype=f32, name="dw_in")
    dh = _mm(dproj, w_in, mode="nt", out_dtype=f32, name="d_h")
    dx, dng = _rms_bwd(x2, ng, dh, dy, "rms_x_bwd")
    g["norm_g"] = dng.reshape(D)
    return sq_err, dx, g


MESH = pl.DeviceIdType.MESH
HBM = pl.BlockSpec(memory_space=pl.ANY)


def _me():
    x, y, c = lax.axis_index("x"), lax.axis_index("y"), lax.axis_index("c")
    return (x, y, c), 4 * x + 2 * y + c


def _peer(k):
    (x, y, c), _ = _me()
    p = (1 - x if k & 4 else x, 1 - y if k & 2 else y, 1 - c if k & 1 else c)
    return p, 4 * p[0] + 2 * p[1] + p[2]


def _window(ref, axis, size, idx):
    if axis is None:
        return ref.at[idx]
    if axis == 0:
        return ref.at[pl.ds(idx * size, size), :]
    return ref.at[:, pl.ds(idx * size, size)]


def _gather_weights(shards, axes):
    n = len(shards)

    def full_shape(s, axis):
        if axis is None:
            return (N_DEV,) + s.shape
        return tuple(N_DEV * d if i == axis else d for i, d in enumerate(s.shape))

    def body(*refs):
        ins, outs = refs[:n], refs[n:2 * n]
        send_sems, recv_sems, local_sems = refs[2 * n:]
        (x, y, c), me = _me()
        sibling, sib_id = _peer(1)
        chips = [_peer(4), _peer(2), _peer(6)]

        def win(a, idx):
            return _window(outs[a], axes[a], shards[a].shape[axes[a]] if axes[a] is not None else None, idx)

        def copy(a, k, block, to, src=None):
            return pltpu.make_async_remote_copy(
                src_ref=win(a, block) if src is None else src, dst_ref=win(a, block),
                send_sem=send_sems.at[a, k], recv_sem=recv_sems.at[a, k], device_id=to, device_id_type=MESH)

        mine = [pltpu.make_async_copy(ins[a], win(a, me), local_sems.at[a]) for a in range(n)]
        for cp in mine:
            cp.start()
        first = []
        for a in range(n):
            first.append(copy(a, 0, me, sibling, src=ins[a]))
            first += [copy(a, 1 + j, me, dev, src=ins[a]) for j, (dev, _) in enumerate(chips)]
        for cp in first:
            cp.start()
        passed = []
        for j, (dev, dev_id) in enumerate(chips):
            for a in range(n):
                copy(a, 1 + j, dev_id, (x, y, c)).wait_recv()
                cp = copy(a, 4 + j, dev_id, sibling)
                cp.start()
                passed.append(cp)
        for a in range(n):
            copy(a, 0, sib_id, (x, y, c)).wait_recv()
        for j, (dev, dev_id) in enumerate(chips):
            for a in range(n):
                copy(a, 4 + j, dev_id + 1 - 2 * c, (x, y, c)).wait_recv()
        for cp in first + passed:
            cp.wait_send()
        for cp in mine:
            cp.wait()

    return pl.pallas_call(
        body, in_specs=[HBM] * n, out_specs=[HBM] * n,
        out_shape=[jax.ShapeDtypeStruct(full_shape(s, ax), s.dtype) for s, ax in zip(shards, axes)],
        scratch_shapes=[pltpu.SemaphoreType.DMA((n, 7)), pltpu.SemaphoreType.DMA((n, 7)), pltpu.SemaphoreType.DMA((n,))],
        name="gather_weights",
    )(*shards)


N_CHIP = 4


def _shard_shape(g, ax):
    return tuple(d // N_DEV if i == ax else d for i, d in enumerate(g.shape))


def _rs_sibling(grads, axes):
    n = len(grads)
    sizes = [g.shape[ax] // N_DEV for g, ax in zip(grads, axes)]

    def body(*refs):
        ins, lands = refs[:n], refs[n:2 * n]
        send_sems, recv_sems = refs[2 * n:]
        sibling, _ = _peer(1)
        copies = []
        for j in range(N_CHIP):
            _, owner = _peer(2 * j + 1)
            for a in range(n):
                copies.append(pltpu.make_async_remote_copy(
                    src_ref=_window(ins[a], axes[a], sizes[a], owner), dst_ref=lands[a].at[j],
                    send_sem=send_sems.at[a, j], recv_sem=recv_sems.at[a, j], device_id=sibling, device_id_type=MESH))
        for cp in copies:
            cp.start()
        for cp in copies:
            cp.wait()

    return pl.pallas_call(
        body, in_specs=[HBM] * n, out_specs=[HBM] * n,
        out_shape=[jax.ShapeDtypeStruct((N_CHIP,) + _shard_shape(g, ax), g.dtype) for g, ax in zip(grads, axes)],
        scratch_shapes=[pltpu.SemaphoreType.DMA((n, N_CHIP)), pltpu.SemaphoreType.DMA((n, N_CHIP))],
        name="rs_sibling",
    )(*grads)


def _rs_chip_sum(grad, land, xyc, axis, name):
    R, C = land.shape[1:]
    tr = _pick(R, max(8, (256 * 1024) // C), 8)

    def body(xyc_ref, g_ref, l_ref, o_ref):
        o_ref[0] = (g_ref[...] + l_ref[0]).astype(bf16)

    def owner(j, xyc_ref):
        jj = j + 1
        return 4 * (xyc_ref[0] ^ (jj >> 1)) + 2 * (xyc_ref[1] ^ (jj & 1)) + xyc_ref[2]

    if axis == 0:
        g_spec = pl.BlockSpec((tr, C), lambda j, i, xyc_ref: (owner(j, xyc_ref) * (R // tr) + i, 0))
    else:
        g_spec = pl.BlockSpec((tr, C), lambda j, i, xyc_ref: (i, owner(j, xyc_ref)))
    return pl.pallas_call(
        body,
        grid_spec=pltpu.PrefetchScalarGridSpec(
            num_scalar_prefetch=1, grid=(N_CHIP - 1, R // tr),
            in_specs=[g_spec, pl.BlockSpec((1, tr, C), lambda j, i, xyc_ref: (j + 1, i, 0))],
            out_specs=pl.BlockSpec((1, tr, C), lambda j, i, xyc_ref: (j, i, 0))),
        out_shape=jax.ShapeDtypeStruct((N_CHIP - 1, R, C), bf16),
        compiler_params=_params(("parallel", "parallel")), name=name,
    )(xyc, grad, land)


def _rs_chips(sums):
    n = len(sums)

    def body(*refs):
        ins, lands = refs[:n], refs[n:2 * n]
        send_sems, recv_sems = refs[2 * n:]
        copies = []
        for j in range(1, N_CHIP):
            dev, _ = _peer(2 * j)
            for a in range(n):
                copies.append(pltpu.make_async_remote_copy(
                    src_ref=ins[a].at[j - 1], dst_ref=lands[a].at[j - 1],
                    send_sem=send_sems.at[a, j - 1], recv_sem=recv_sems.at[a, j - 1], device_id=dev, device_id_type=MESH))
        for cp in copies:
            cp.start()
        for cp in copies:
            cp.wait()

    return pl.pallas_call(
        body, in_specs=[HBM] * n, out_specs=[HBM] * n,
        out_shape=[jax.ShapeDtypeStruct(s.shape, s.dtype) for s in sums],
        scratch_shapes=[pltpu.SemaphoreType.DMA((n, N_CHIP - 1)), pltpu.SemaphoreType.DMA((n, N_CHIP - 1))],
        name="rs_chips",
    )(*sums)


def _allreduce_small(part):
    R = part.shape[0]

    def body(p_ref, o_ref, buf, send_sems, recv_sems):
        (x, y, c), me = _me()
        buf[me] = p_ref[...]
        copies = []
        for k in range(1, N_DEV):
            dev, dev_id = _peer(k)
            copies.append(pltpu.make_async_remote_copy(
                src_ref=p_ref, dst_ref=buf.at[me], send_sem=send_sems.at[k - 1], recv_sem=recv_sems.at[k - 1],
                device_id=dev, device_id_type=MESH))
        for cp in copies:
            cp.start()
        for k in range(1, N_DEV):
            _, dev_id = _peer(k)
            pltpu.make_async_remote_copy(
                src_ref=p_ref, dst_ref=buf.at[dev_id], send_sem=send_sems.at[k - 1], recv_sem=recv_sems.at[k - 1],
                device_id=(x, y, c), device_id_type=MESH).wait_recv()
        for cp in copies:
            cp.wait_send()
        acc = buf[0]
        for d in range(1, N_DEV):
            acc = acc + buf[d]
        o_ref[...] = acc

    return pl.pallas_call(
        body, in_specs=[pl.BlockSpec(memory_space=pltpu.VMEM)], out_specs=pl.BlockSpec(memory_space=pltpu.VMEM),
        out_shape=jax.ShapeDtypeStruct((R, LANES), f32),
        scratch_shapes=[pltpu.VMEM((N_DEV, R, LANES), f32), pltpu.SemaphoreType.DMA((N_DEV - 1,)), pltpu.SemaphoreType.DMA((N_DEV - 1,))],
        name="allreduce_small",
    )(part)


def _adamw_math(w, g, m, v):
    m2 = ADAM_B1 * m + (1.0 - ADAM_B1) * g
    v2 = ADAM_B2 * v + (1.0 - ADAM_B2) * (g * g)
    m_hat = m2 / (1.0 - ADAM_B1 ** ADAM_STEP)
    v_hat = v2 / (1.0 - ADAM_B2 ** ADAM_STEP)
    return -ADAM_LR * (m_hat / (jnp.sqrt(v_hat) + ADAM_EPS) + ADAM_WD * w), m2, v2


def _adamw_shard(grad, land_sib, land_chips, w, m, v, me, axis, name):
    R, C = w.shape
    tr = _pick(R, max(8, (128 * 1024) // C), 8)

    def body(me_ref, g_ref, s_ref, l_ref, w_ref, m_ref, v_ref, go_ref, d_ref, mo_ref, vo_ref):
        g = g_ref[...] + s_ref[0]
        for j in range(N_CHIP - 1):
            g = g + l_ref[j].astype(f32)
        d, m2, v2 = _adamw_math(w_ref[...], g, m_ref[...], v_ref[...])
        go_ref[...] = g
        d_ref[...] = d
        mo_ref[...] = m2
        vo_ref[...] = v2

    if axis == 0:
        g_spec = pl.BlockSpec((tr, C), lambda i, me_ref: (me_ref[0] * (R // tr) + i, 0))
    else:
        g_spec = pl.BlockSpec((tr, C), lambda i, me_ref: (i, me_ref[0]))
    blk = pl.BlockSpec((tr, C), lambda i, me_ref: (i, 0))
    return pl.pallas_call(
        body,
        grid_spec=pltpu.PrefetchScalarGridSpec(
            num_scalar_prefetch=1, grid=(R // tr,),
            in_specs=[g_spec, pl.BlockSpec((1, tr, C), lambda i, me_ref: (0, i, 0)),
                      pl.BlockSpec((N_CHIP - 1, tr, C), lambda i, me_ref: (0, i, 0)), blk, blk, blk],
            out_specs=[blk] * 4),
        out_shape=[jax.ShapeDtypeStruct((R, C), f32)] * 4,
        compiler_params=_params(("parallel",)), name=name,
    )(me, grad, land_sib, land_chips, w, m, v)


def _adamw_small(g, w, m, v):
    def body(g_ref, w_ref, m_ref, v_ref, d_ref, mo_ref, vo_ref):
        d, m2, v2 = _adamw_math(w_ref[...], g_ref[...], m_ref[...], v_ref[...])
        d_ref[...] = d
        mo_ref[...] = m2
        vo_ref[...] = v2

    return pl.pallas_call(body, out_shape=[jax.ShapeDtypeStruct(g.shape, f32)] * 3, name="adamw_small")(g, w, m, v)


def _pack_rows(parts, total_rows):
    rows = jnp.concatenate([p.reshape(-1, LANES) for p in parts], axis=0)
    return jnp.pad(rows, ((0, total_rows - rows.shape[0]), (0, 0)))


BIG = ("w_in", "mem_w_kv", "w_br_attn", "w_br_conv", "w_br_mem", "w_out")
BIG_AXIS = {"w_in": 1, "mem_w_kv": 0, "w_br_attn": 1, "w_br_conv": 1, "w_br_mem": 1, "w_out": 0}
SMALL = ("norm_g", "mem_norm_g", "attn_q_norm", "attn_k_norm", "mem_q_norm", "mem_k_norm")
ALL_W = ("norm_g", "mem_norm_g", "w_in", "attn_q_norm", "attn_k_norm", "conv_w", "mem_w_kv", "mem_q_norm", "mem_k_norm",
         "w_br_attn", "w_br_conv", "w_br_mem", "w_out")


def kernel(x, mem, norm_g, mem_norm_g, w_in, attn_q_norm, attn_k_norm, conv_w, mem_w_kv, mem_q_norm, mem_k_norm, w_br_attn, w_br_conv, w_br_mem, w_out, loss_target, m_norm_g, m_mem_norm_g, m_w_in, m_attn_q_norm, m_attn_k_norm, m_conv_w, m_mem_w_kv, m_mem_q_norm, m_mem_k_norm, m_w_br_attn, m_w_br_conv, m_w_br_mem, m_w_out, v_norm_g, v_mem_norm_g, v_w_in, v_attn_q_norm, v_attn_k_norm, v_conv_w, v_mem_w_kv, v_mem_q_norm, v_mem_k_norm, v_w_br_attn, v_w_br_conv, v_w_br_mem, v_w_out):
    w = dict(norm_g=norm_g, mem_norm_g=mem_norm_g, w_in=w_in, attn_q_norm=attn_q_norm, attn_k_norm=attn_k_norm, conv_w=conv_w,
             mem_w_kv=mem_w_kv, mem_q_norm=mem_q_norm, mem_k_norm=mem_k_norm, w_br_attn=w_br_attn, w_br_conv=w_br_conv,
             w_br_mem=w_br_mem, w_out=w_out)
    mo = dict(norm_g=m_norm_g, mem_norm_g=m_mem_norm_g, w_in=m_w_in, attn_q_norm=m_attn_q_norm, attn_k_norm=m_attn_k_norm,
              conv_w=m_conv_w, mem_w_kv=m_mem_w_kv, mem_q_norm=m_mem_q_norm, mem_k_norm=m_mem_k_norm, w_br_attn=m_w_br_attn,
              w_br_conv=m_w_br_conv, w_br_mem=m_w_br_mem, w_out=m_w_out)
    vo = dict(norm_g=v_norm_g, mem_norm_g=v_mem_norm_g, w_in=v_w_in, attn_q_norm=v_attn_q_norm, attn_k_norm=v_attn_k_norm,
              conv_w=v_conv_w, mem_w_kv=v_mem_w_kv, mem_q_norm=v_mem_q_norm, mem_k_norm=v_mem_k_norm, w_br_attn=v_w_br_attn,
              w_br_conv=v_w_br_conv, w_br_mem=v_w_br_mem, w_out=v_w_out)
    B, S, D = x.shape
    me = (4 * lax.axis_index("x") + 2 * lax.axis_index("y") + lax.axis_index("c")).astype(jnp.int32)

    conv_pad = jnp.pad(conv_w, ((0, 8 - conv_w.shape[0]), (0, 0)))
    full = _gather_weights([w[n].astype(bf16) for n in BIG] + [conv_pad], [BIG_AXIS[n] for n in BIG] + [None])
    wf = dict(zip(BIG, full[:-1]))
    conv_full = full[-1][:, :3, :].transpose(1, 0, 2).reshape(3, CONV_W)

    sq_err, dx, g = _local_step(x, mem, loss_target, norm_g, mem_norm_g, attn_q_norm, attn_k_norm, conv_full, mem_q_norm,
                                mem_k_norm, wf["w_in"], wf["mem_w_kv"], wf["w_br_attn"], wf["w_br_conv"], wf["w_br_mem"], wf["w_out"])

    xyc = jnp.stack([lax.axis_index("x"), lax.axis_index("y"), lax.axis_index("c")]).astype(jnp.int32)
    land_sib = _rs_sibling([g[n] for n in BIG], [BIG_AXIS[n] for n in BIG])
    sums = [_rs_chip_sum(g[n], ls, xyc, BIG_AXIS[n], "rs_sum_" + n) for n, ls in zip(BIG, land_sib)]
    land_chips = _rs_chips(sums)
    grad, delta, new_m, new_v = {}, {}, {}, {}
    for n, ls, lc in zip(BIG, land_sib, land_chips):
        grad[n], delta[n], new_m[n], new_v[n] = _adamw_shard(g[n], ls, lc, w[n], mo[n], vo[n], me.reshape(1), BIG_AXIS[n],
                                                             "adamw_" + n)

    n_rep = sum(w[n].size for n in SMALL) // LANES
    conv_rows = g["conv_w"].reshape(3, N_DEV, LANES).transpose(1, 0, 2).reshape(3 * N_DEV, LANES)
    n_rows = -(-(n_rep + 3 * N_DEV + 1) // 8) * 8
    part = _pack_rows([g[n] for n in SMALL] + [conv_rows, jnp.full((1, LANES), sq_err, f32)], n_rows)
    tot = _allreduce_small(part)
    loss = tot[n_rep + 3 * N_DEV, 0] * (0.5 / D)
    n_small = -(-(n_rep + 3) // 8) * 8
    g_small = _pack_rows([tot[:n_rep], lax.dynamic_slice(tot, (n_rep + 3 * me, 0), (3, LANES))], n_small)
    names = SMALL + ("conv_w",)
    d_s, m_s, v_s = _adamw_small(g_small, _pack_rows([w[n] for n in names], n_small), _pack_rows([mo[n] for n in names], n_small),
                                 _pack_rows([vo[n] for n in names], n_small))
    off = 0
    for n in names:
        r = w[n].size // LANES
        grad[n], delta[n] = g_small[off:off + r].reshape(w[n].shape), d_s[off:off + r].reshape(w[n].shape)
        new_m[n], new_v[n] = m_s[off:off + r].reshape(w[n].shape), v_s[off:off + r].reshape(w[n].shape)
        off += r
    return (loss, dx.reshape(B, S, D), *[grad[n] for n in ALL_W], *[delta[n] for n in ALL_W], *[new_m[n] for n in ALL_W],
            *[new_v[n] for n in ALL_W])
```

```python
import functools

import jax
import jax.numpy as jnp
from jax import lax
from jax.experimental import pallas as pl
from jax.experimental.pallas import tpu as pltpu

f32 = jnp.float32
bf16 = jnp.bfloat16

N_DEV = 8
HEAD_DIM = 128
DILATIONS = (1, 4, 16)
N_GROUPS = 3
HEADS = 4
BLK = 128
ATTN_QKV = N_GROUPS * HEADS * HEAD_DIM
ATTN_OUT = HEADS * HEAD_DIM
CONV_W = 1024
MEM_LEN = 256
MEM_HEADS = 4
MEM_HD = 256
MEM_W = MEM_HEADS * MEM_HD
EPS = 1e-6
Q0, K0, V0 = 0, ATTN_QKV, 2 * ATTN_QKV
ZA = 3 * ATTN_QKV
CB, CC, CV, ZC = ZA + ATTN_OUT, ZA + ATTN_OUT + CONV_W, ZA + ATTN_OUT + 2 * CONV_W, ZA + ATTN_OUT + 3 * CONV_W
MQ = ZC + CONV_W
ZM = MQ + MEM_W
GT = ZM + MEM_W

ADAM_LR, ADAM_B1, ADAM_B2, ADAM_EPS, ADAM_WD, ADAM_STEP = 0.001, 0.9, 0.999, 1e-08, 0.01, 10

VMEM_LIMIT = 56 * 1024 * 1024
LANES = 128

NT_DIMS = (((1,), (1,)), ((), ()))
TN_DIMS = (((0,), (0,)), ((), ()))
NN_DIMS = (((1,), (0,)), ((), ()))


def _params(sem=None):
    return pltpu.CompilerParams(dimension_semantics=sem, vmem_limit_bytes=VMEM_LIMIT)


def _pick(n, pref, q=LANES):
    t = (min(pref, n) // q) * q
    while t >= q:
        if n % t == 0:
            return t
        t -= q
    return n


def _dot(a, b, dims):
    return lax.dot_general(a.astype(bf16), b.astype(bf16), dims, preferred_element_type=f32)


def _sigmoid(z):
    return 1.0 / (1.0 + jnp.exp(-z))


def _rstd(v):
    return lax.rsqrt(jnp.mean(v * v, axis=-1, keepdims=True) + EPS)


class _Comm:
    def __init__(self, ins, out_shapes, sem_shape, copies):
        self.ins, self.out_shapes, self.sem_shape, self.copies = list(ins), list(out_shapes), sem_shape, copies

    def scratch(self):
        return [pltpu.SemaphoreType.DMA(self.sem_shape), pltpu.SemaphoreType.DMA(self.sem_shape)]


def _mm(a, b, *, mode, out_dtype, name, tm=1024, tn=1024, tk=4096, m_tiles=None, into=None, comm=None):
    if mode == "nn":
        (M, K), (K2, N) = a.shape, b.shape
    elif mode == "nt":
        (M, K), (N, K2) = a.shape, b.shape
    else:
        (K, M), (K2, N) = a.shape, b.shape
    assert K == K2
    tm, tn, tk = _pick(M, tm), _pick(N, tn), _pick(K, tk)
    nk = K // tk
    m0, mc = m_tiles if m_tiles is not None else (0, M // tm)
    o0 = 0 if into is None else m0
    aliased = into is not None and not isinstance(into, str)
    n_ci = len(comm.ins) if comm else 0
    n_co = len(comm.out_shapes) if comm else 0
    grid = (mc, N // tn, nk)
    dims = {"nn": NN_DIMS, "nt": NT_DIMS, "tn": TN_DIMS}[mode]

    def body(*refs, nk):
        a_ref, b_ref = refs[:2]
        pos = 3 if aliased else 2
        c_ins, o_ref, c_outs = refs[pos:pos + n_ci], refs[pos + n_ci], refs[pos + n_ci + 1:pos + n_ci + 1 + n_co]
        scratch = refs[pos + n_ci + 1 + n_co:]
        ids = [pl.program_id(d) for d in range(3)]
        if comm:
            sems = scratch[-2:]

            @pl.when((ids[0] == 0) & (ids[1] == 0) & (ids[2] == 0))
            def _():
                for cp in comm.copies(c_ins, c_outs, *sems):
                    cp.start()

        if nk == 1:
            o_ref[...] = _dot(a_ref[...], b_ref[...], dims).astype(o_ref.dtype)
        else:
            acc_ref, k = scratch[0], ids[2]

            @pl.when(k == 0)
            def _():
                acc_ref[...] = _dot(a_ref[...], b_ref[...], dims)

            @pl.when((k > 0) & (k < nk - 1))
            def _():
                acc_ref[...] += _dot(a_ref[...], b_ref[...], dims)

            @pl.when(k == nk - 1)
            def _():
                o_ref[...] = (acc_ref[...] + _dot(a_ref[...], b_ref[...], dims)).astype(o_ref.dtype)

        if comm:
            @pl.when((ids[0] == grid[0] - 1) & (ids[1] == grid[1] - 1) & (ids[2] == grid[2] - 1))
            def _():
                for cp in comm.copies(c_ins, c_outs, *sems):
                    cp.wait()

    if mode == "tn":
        a_spec = pl.BlockSpec((tk, tm), lambda i, j, k: (k, m0 + i))
    else:
        a_spec = pl.BlockSpec((tm, tk), lambda i, j, k: (m0 + i, k))
    if mode == "nt":
        b_spec = pl.BlockSpec((tn, tk), lambda i, j, k: (j, k))
    else:
        b_spec = pl.BlockSpec((tk, tn), lambda i, j, k: (k, j))
    hbm = pl.BlockSpec(memory_space=pl.ANY)
    res = pl.pallas_call(
        functools.partial(body, nk=nk),
        grid=grid,
        in_specs=[a_spec, b_spec] + [hbm] * (aliased + n_ci),
        out_specs=[pl.BlockSpec((tm, tn), lambda i, j, k: (o0 + i, j))] + [hbm] * n_co,
        out_shape=[jax.ShapeDtypeStruct((mc * tm if into is None else M, N), out_dtype)] + (comm.out_shapes if comm else []),
        scratch_shapes=([pltpu.VMEM((tm, tn), f32)] if nk > 1 else []) + (comm.scratch() if comm else []),
        input_output_aliases={2: 0} if aliased else {},
        compiler_params=_params(("arbitrary",) * 3 if comm else ("parallel", "parallel", "arbitrary")),
        name=name,
    )(a, b, *([into] if aliased else []), *(comm.ins if comm else []))
    return (res[0], list(res[1:])) if comm else res[0]


def _rms_fwd(x, g, name):
    T, D = x.shape
    tm = _pick(T, 256, 8)

    def body(x_ref, g_ref, h_ref):
        xv = x_ref[...]
        h_ref[...] = (xv * _rstd(xv) * g_ref[...]).astype(bf16)

    return pl.pallas_call(
        body, grid=(T // tm,),
        in_specs=[pl.BlockSpec((tm, D), lambda i: (i, 0)), pl.BlockSpec((1, D), lambda i: (0, 0))],
        out_specs=pl.BlockSpec((tm, D), lambda i: (i, 0)),
        out_shape=jax.ShapeDtypeStruct((T, D), bf16),
        compiler_params=_params(("parallel",)), name=name,
    )(x, g)


def _rms_bwd(x, g, dh, dy, name):
    T, D = x.shape
    tm = _pick(T, 256, 8)
    with_dx = dy is not None

    def body(*refs):
        if with_dx:
            x_ref, g_ref, dh_ref, dy_ref, dx_ref, dg_ref = refs
        else:
            x_ref, g_ref, dh_ref, dg_ref = refs
        xv = x_ref[...]
        r = _rstd(xv)
        xh = xv * r
        dhv = dh_ref[...]
        part = jnp.sum(dhv * xh, axis=0, keepdims=True)

        @pl.when(pl.program_id(0) == 0)
        def _():
            dg_ref[...] = part

        @pl.when(pl.program_id(0) > 0)
        def _():
            dg_ref[...] += part

        if with_dx:
            dxh = dhv * g_ref[...]
            dx_ref[...] = dy_ref[...] + r * (dxh - xh * jnp.mean(dxh * xh, axis=-1, keepdims=True))

    row = pl.BlockSpec((tm, D), lambda i: (i, 0))
    vec = pl.BlockSpec((1, D), lambda i: (0, 0))
    if with_dx:
        return pl.pallas_call(
            body, grid=(T // tm,), in_specs=[row, vec, row, row], out_specs=[row, vec],
            out_shape=[jax.ShapeDtypeStruct((T, D), f32), jax.ShapeDtypeStruct((1, D), f32)],
            compiler_params=_params(("arbitrary",)), name=name,
        )(x, g, dh, dy)
    return pl.pallas_call(
        body, grid=(T // tm,), in_specs=[row, vec, row], out_specs=vec,
        out_shape=jax.ShapeDtypeStruct((1, D), f32),
        compiler_params=_params(("arbitrary",)), name=name,
    )(x, g, dh)


def _rows(start, size, stride):
    return pl.ds(start, size) if stride == 1 else pl.ds(start, size, stride=stride)


def _attn_units(S, d, first, prev):
    nb = S // d // BLK

    def do_first(r, c):
        first(_rows(r, BLK, d))
        return c

    lax.fori_loop(0, d, do_first, 0, unroll=min(d, 2))
    if nb > 1:
        def do_prev(u, c):
            r = u // (nb - 1)
            b = u % (nb - 1) + 1
            base = r + d * b * BLK
            prev(_rows(base, BLK, d), _rows(base - d * BLK, 2 * BLK, d))
            return c

        lax.fori_loop(0, d * (nb - 1), do_prev, 0, unroll=2)


def _band_mask(nkeys):
    i = lax.broadcasted_iota(jnp.int32, (BLK, nkeys), 0)
    j = lax.broadcasted_iota(jnp.int32, (BLK, nkeys), 1)
    if nkeys == BLK:
        return j <= i
    return (j >= i) & (j <= i + BLK)


def _normalize_into(src_ref, gain, dst_ref, S):
    for c in range(S // 256):
        rows = pl.ds(c * 256, 256)
        v = src_ref[rows, :]
        dst_ref[rows, :] = v * _rstd(v) * gain


def _attn_fwd(proj, gq, gk, B, S):
    T, NC = proj.shape
    scale = HEAD_DIM ** -0.5

    def body(q_ref, k_ref, v_ref, z_ref, gq_ref, gk_ref, ag_ref, a_ref, lse_ref, qs, ks, o0, o1, o2, l0, l1, l2):
        g = pl.program_id(2)
        _normalize_into(q_ref, gq_ref[pl.ds(g, 1), :], qs, S)
        _normalize_into(k_ref, gk_ref[pl.ds(g, 1), :], ks, S)
        o_s, l_s = (o0, o1, o2), (l0, l1, l2)

        def run(gi):
            def unit(qr, kr, nkeys):
                s = _dot(qs[qr, :], ks[kr, :], NT_DIMS) * scale
                s = jnp.where(_band_mask(nkeys), s, -jnp.inf)
                m = jnp.max(s, axis=-1, keepdims=True)
                p = jnp.exp(s - m)
                den = jnp.sum(p, axis=-1, keepdims=True)
                o_s[gi][qr, :] = _dot(p / den, v_ref[kr, :], NN_DIMS)
                l_s[gi][qr, :] = jnp.broadcast_to(m + jnp.log(den), (BLK, HEAD_DIM))

            _attn_units(S, DILATIONS[gi], lambda qr: unit(qr, qr, BLK), lambda qr, kr: unit(qr, kr, 2 * BLK))

        for gi in range(N_GROUPS):
            pl.when(g == gi)(functools.partial(run, gi))

        @pl.when(g == N_GROUPS - 1)
        def _():
            for c in range(S // 256):
                rows = pl.ds(c * 256, 256)
                la, lb, lc = l0[rows, :], l1[rows, :], l2[rows, :]
                m = jnp.maximum(jnp.maximum(la, lb), lc)
                wa, wb, wc = jnp.exp(la - m), jnp.exp(lb - m), jnp.exp(lc - m)
                tot = wa + wb + wc
                a = (wa / tot) * o0[rows, :] + (wb / tot) * o1[rows, :] + (wc / tot) * o2[rows, :]
                z = z_ref[rows, :]
                a_ref[rows, :] = a
                lse_ref[rows, :] = m + jnp.log(tot)
                ag_ref[rows, :] = (a * (z * _sigmoid(z))).astype(bf16)

    def slab(col0):
        return pl.BlockSpec((S, HEAD_DIM), lambda b, h, g: (b, col0 // HEAD_DIM + g * HEADS + h))

    gain = pl.BlockSpec((N_GROUPS, HEAD_DIM), lambda b, h, g: (0, 0))
    out = pl.BlockSpec((S, HEAD_DIM), lambda b, h, g: (b, h))
    return pl.pallas_call(
        body, grid=(B, HEADS, N_GROUPS),
        in_specs=[slab(Q0), slab(K0), slab(V0), pl.BlockSpec((S, HEAD_DIM), lambda b, h, g: (b, ZA // HEAD_DIM + h)), gain, gain],
        out_specs=[out, out, out],
        out_shape=[jax.ShapeDtypeStruct((T, ATTN_OUT), bf16), jax.ShapeDtypeStruct((T, ATTN_OUT), f32),
                   jax.ShapeDtypeStruct((T, ATTN_OUT), f32)],
        scratch_shapes=[pltpu.VMEM((S, HEAD_DIM), f32)] * 8,
        compiler_params=_params(("arbitrary", "arbitrary", "arbitrary")), name="attn_fwd",
    )(proj, proj, proj, proj, gq, gk)


def _rms_bwd_rows(raw, gain, dn):
    r = _rstd(raw)
    xh = raw * r
    dxh = dn * gain
    return r * (dxh - xh * jnp.mean(dxh * xh, axis=-1, keepdims=True)), jnp.sum(dn * xh, axis=0, keepdims=True)


def _attn_bwd(proj, gq, gk, a_comb, lse, dag, dproj, B, S):
    T, NC = proj.shape
    scale = HEAD_DIM ** -0.5

    def body(q_ref, k_ref, v_ref, z_ref, gq_ref, gk_ref, a_ref, lse_ref, dag_ref, dproj_in, dproj_ref, dgq_ref, dgk_ref,
             qs, ks, da_s, dl_s, dq_s, dk_s, dv_s, stage, sem):
        b, h, g = pl.program_id(0), pl.program_id(1), pl.program_id(2)
        gq_row, gk_row = gq_ref[pl.ds(g, 1), :], gk_ref[pl.ds(g, 1), :]
        _normalize_into(q_ref, gq_row, qs, S)
        _normalize_into(k_ref, gk_row, ks, S)

        def put(slot, col0):
            cp = pltpu.make_async_copy(stage.at[slot], dproj_ref.at[pl.ds(b * S, S), pl.ds(col0, HEAD_DIM)], sem.at[slot])
            cp.start()
            return cp

        @pl.when((b == 0) & (h == 0) & (g == 0))
        def _():
            dgq_ref[...] = jnp.zeros_like(dgq_ref)
            dgk_ref[...] = jnp.zeros_like(dgk_ref)

        @pl.when(g == 0)
        def _():
            for c in range(S // 256):
                rows = pl.ds(c * 256, 256)
                z, a, dg_ = z_ref[rows, :], a_ref[rows, :], dag_ref[rows, :]
                sg = _sigmoid(z)
                da = dg_ * (z * sg)
                da_s[rows, :] = da
                dl_s[rows, :] = jnp.broadcast_to(jnp.sum(da * a, axis=-1, keepdims=True), (256, HEAD_DIM))
                stage[3, rows, :] = (dg_ * a * (sg * (1.0 + z * (1.0 - sg)))).astype(bf16)
            put(3, ZA + h * HEAD_DIM).wait()

        dk_s[...] = jnp.zeros_like(dk_s)
        dv_s[...] = jnp.zeros_like(dv_s)

        def run(gi):
            def unit(qr, kr, nkeys):
                q, k, v = qs[qr, :], ks[kr, :], v_ref[kr, :]
                s = _dot(q, k, NT_DIMS) * scale
                p = jnp.where(_band_mask(nkeys), jnp.exp(s - lse_ref[qr, :][:, :1]), 0.0)
                da = da_s[qr, :]
                dp = _dot(da, v, NT_DIMS)
                ds = p * (dp - dl_s[qr, :][:, :1]) * scale
                dq_s[qr, :] = _dot(ds, k, NN_DIMS)
                dk_s[kr, :] += _dot(ds, q, TN_DIMS)
                dv_s[kr, :] += _dot(p, da, TN_DIMS)

            _attn_units(S, DILATIONS[gi], lambda qr: unit(qr, qr, BLK), lambda qr, kr: unit(qr, kr, 2 * BLK))

        for gi in range(N_GROUPS):
            pl.when(g == gi)(functools.partial(run, gi))

        gq_acc = jnp.zeros((1, HEAD_DIM), f32)
        gk_acc = jnp.zeros((1, HEAD_DIM), f32)
        for c in range(S // 256):
            rows = pl.ds(c * 256, 256)
            dq, gq_p = _rms_bwd_rows(q_ref[rows, :], gq_row, dq_s[rows, :])
            dk, gk_p = _rms_bwd_rows(k_ref[rows, :], gk_row, dk_s[rows, :])
            gq_acc, gk_acc = gq_acc + gq_p, gk_acc + gk_p
            stage[0, rows, :] = dq.astype(bf16)
            stage[1, rows, :] = dk.astype(bf16)
            stage[2, rows, :] = dv_s[rows, :].astype(bf16)
        dgq_ref[pl.ds(g, 1), :] += gq_acc
        dgk_ref[pl.ds(g, 1), :] += gk_acc
        col = (g * HEADS + h) * HEAD_DIM
        cps = [put(0, Q0 + col), put(1, K0 + col), put(2, V0 + col)]
        for cp in cps:
            cp.wait()

    def slab(col0):
        return pl.BlockSpec((S, HEAD_DIM), lambda b, h, g: (b, col0 // HEAD_DIM + g * HEADS + h))

    gain = pl.BlockSpec((N_GROUPS, HEAD_DIM), lambda b, h, g: (0, 0))
    per_slot = pl.BlockSpec((S, HEAD_DIM), lambda b, h, g: (b, h))
    return pl.pallas_call(
        body, grid=(B, HEADS, N_GROUPS),
        in_specs=[slab(Q0), slab(K0), slab(V0), pl.BlockSpec((S, HEAD_DIM), lambda b, h, g: (b, ZA // HEAD_DIM + h)), gain, gain,
                  per_slot, per_slot, per_slot, pl.BlockSpec(memory_space=pl.ANY)],
        out_specs=[pl.BlockSpec(memory_space=pl.ANY), gain, gain],
        out_shape=[jax.ShapeDtypeStruct(dproj.shape, dproj.dtype), jax.ShapeDtypeStruct((N_GROUPS, HEAD_DIM), f32),
                   jax.ShapeDtypeStruct((N_GROUPS, HEAD_DIM), f32)],
        scratch_shapes=[pltpu.VMEM((S, HEAD_DIM), f32)] * 7 + [pltpu.VMEM((4, S, HEAD_DIM), bf16), pltpu.SemaphoreType.DMA((4,))],
        input_output_aliases={9: 0},
        compiler_params=_params(("arbitrary", "arbitrary", "arbitrary")), name="attn_bwd",
    )(proj, proj, proj, proj, gq, gk, a_comb, lse, dag, dproj)


def _conv_fwd(proj, conv_w, B, S):
    T, NC = proj.shape
    tc = LANES

    def body(cb_ref, cc_ref, cv_ref, z_ref, w_ref, c_ref, ub):
        u = cc_ref[...] * cv_ref[...]
        ub[0:8, :] = jnp.zeros((8, tc), f32)
        ub[8:8 + S, :] = u
        w = w_ref[...]
        y = w[0:1] * u + w[1:2] * ub[pl.ds(7, S), :] + w[2:3] * ub[pl.ds(6, S), :]
        z = z_ref[...]
        c_ref[...] = (cb_ref[...] * y * (z * _sigmoid(z))).astype(bf16)

    def seg(col0):
        return pl.BlockSpec((S, tc), lambda j, b: (b, col0 // tc + j))

    return pl.pallas_call(
        body, grid=(CONV_W // tc, B),
        in_specs=[seg(CB), seg(CC), seg(CV), seg(ZC), pl.BlockSpec((3, tc), lambda j, b: (0, j))],
        out_specs=pl.BlockSpec((S, tc), lambda j, b: (b, j)),
        out_shape=jax.ShapeDtypeStruct((T, CONV_W), bf16),
        scratch_shapes=[pltpu.VMEM((S + 8, tc), f32)],
        compiler_params=_params(("parallel", "parallel")), name="conv_fwd",
    )(proj, proj, proj, proj, conv_w)


def _conv_bwd(proj, conv_w, dc, dproj, B, S):
    T, NC = proj.shape
    tc = LANES

    def body(cb_ref, cc_ref, cv_ref, z_ref, w_ref, dc_ref, dproj_in, dproj_ref, dw_ref, ub, db, stage, sem):
        j, b = pl.program_id(0), pl.program_id(1)
        cb, cc, cv, z, dcv = cb_ref[...], cc_ref[...], cv_ref[...], z_ref[...], dc_ref[...]
        u = cc * cv
        ub[0:8, :] = jnp.zeros((8, tc), f32)
        ub[8:8 + S, :] = u
        u1, u2 = ub[pl.ds(7, S), :], ub[pl.ds(6, S), :]
        w = w_ref[...]
        y = w[0:1] * u + w[1:2] * u1 + w[2:3] * u2
        sg = _sigmoid(z)
        si = z * sg
        dyv = dcv * cb * si
        db[0:S, :] = dyv
        db[S:S + 8, :] = jnp.zeros((8, tc), f32)
        du = w[0:1] * dyv + w[1:2] * db[pl.ds(1, S), :] + w[2:3] * db[pl.ds(2, S), :]
        stage[0] = (dcv * y * si).astype(bf16)
        stage[1] = (du * cv).astype(bf16)
        stage[2] = (du * cc).astype(bf16)
        stage[3] = (dcv * cb * y * (sg * (1.0 + z * (1.0 - sg)))).astype(bf16)
        cps = []
        for slot, col0 in enumerate((CB, CC, CV, ZC)):
            cp = pltpu.make_async_copy(stage.at[slot], dproj_ref.at[pl.ds(b * S, S), pl.ds(col0 + j * tc, tc)], sem.at[slot])
            cp.start()
            cps.append(cp)
        part = jnp.concatenate([jnp.sum(dyv * u, axis=0, keepdims=True), jnp.sum(dyv * u1, axis=0, keepdims=True),
                                jnp.sum(dyv * u2, axis=0, keepdims=True)], axis=0)

        @pl.when(b == 0)
        def _():
            dw_ref[...] = part

        @pl.when(b > 0)
        def _():
            dw_ref[...] += part

        for cp in cps:
            cp.wait()

    def seg(col0):
        return pl.BlockSpec((S, tc), lambda j, b: (b, col0 // tc + j))

    return pl.pallas_call(
        body, grid=(CONV_W // tc, B),
        in_specs=[seg(CB), seg(CC), seg(CV), seg(ZC), pl.BlockSpec((3, tc), lambda j, b: (0, j)),
                  pl.BlockSpec((S, tc), lambda j, b: (b, j)), pl.BlockSpec(memory_space=pl.ANY)],
        out_specs=[pl.BlockSpec(memory_space=pl.ANY), pl.BlockSpec((3, tc), lambda j, b: (0, j))],
        out_shape=[jax.ShapeDtypeStruct(dproj.shape, dproj.dtype), jax.ShapeDtypeStruct((3, CONV_W), f32)],
        scratch_shapes=[pltpu.VMEM((S + 8, tc), f32), pltpu.VMEM((S + 8, tc), f32), pltpu.VMEM((4, S, tc), bf16),
                        pltpu.SemaphoreType.DMA((4,))],
        input_output_aliases={6: 0},
        compiler_params=_params(("arbitrary", "arbitrary")), name="conv_bwd",
    )(proj, proj, proj, proj, conv_w, dc, dproj)


MEM_CHUNK = 256


def _mem_fwd(proj, mkv, gq, gk, B, S):
    T, NC = proj.shape
    scale = MEM_HD ** -0.5

    def body(q_ref, z_ref, k_ref, v_ref, gq_ref, gk_ref, o_ref):
        kv = k_ref[...]
        kn = (kv * _rstd(kv) * gk_ref[...]).astype(bf16)
        vv = v_ref[...].astype(bf16)

        def chunk(c, carry):
            rows = pl.ds(pl.multiple_of(c * MEM_CHUNK, MEM_CHUNK), MEM_CHUNK)
            q = q_ref[rows, :]
            qn = q * _rstd(q) * gq_ref[...]
            s = _dot(qn, kn, NT_DIMS) * scale
            p = jnp.exp(s - jnp.max(s, axis=-1, keepdims=True))
            p = p / jnp.sum(p, axis=-1, keepdims=True)
            z = z_ref[rows, :]
            o_ref[rows, :] = (_dot(p, vv, NN_DIMS) * (z * _sigmoid(z))).astype(bf16)
            return carry

        lax.fori_loop(0, S // MEM_CHUNK, chunk, 0)

    gain = pl.BlockSpec((1, MEM_HD), lambda b, h: (0, 0))
    return pl.pallas_call(
        body, grid=(B, MEM_HEADS),
        in_specs=[pl.BlockSpec((S, MEM_HD), lambda b, h: (b, MQ // MEM_HD + h)),
                  pl.BlockSpec((S, MEM_HD), lambda b, h: (b, ZM // MEM_HD + h)),
                  pl.BlockSpec((MEM_LEN, MEM_HD), lambda b, h: (b, h)),
                  pl.BlockSpec((MEM_LEN, MEM_HD), lambda b, h: (b, MEM_HEADS + h)), gain, gain],
        out_specs=pl.BlockSpec((S, MEM_HD), lambda b, h: (b, h)),
        out_shape=jax.ShapeDtypeStruct((T, MEM_W), bf16),
        compiler_params=_params(("parallel", "parallel")), name="mem_fwd",
    )(proj, proj, mkv, mkv, gq, gk)


def _mem_bwd(proj, mkv, gq, gk, dmo, dproj, B, S):
    T, NC = proj.shape
    scale = MEM_HD ** -0.5

    def body(q_ref, z_ref, k_ref, v_ref, gq_ref, gk_ref, dmo_ref, dproj_in, dproj_ref, dmk_ref, dmv_ref, dgq_ref, dgk_ref,
             dkn_s, dv_s, gq_s, stage, sem):
        b, h = pl.program_id(0), pl.program_id(1)
        kv = k_ref[...]
        kn = (kv * _rstd(kv) * gk_ref[...]).astype(bf16)
        vv = v_ref[...].astype(bf16)
        dkn_s[...] = jnp.zeros_like(dkn_s)
        dv_s[...] = jnp.zeros_like(dv_s)
        gq_s[...] = jnp.zeros_like(gq_s)

        def chunk(c, carry):
            rows = pl.ds(pl.multiple_of(c * MEM_CHUNK, MEM_CHUNK), MEM_CHUNK)
            q = q_ref[rows, :]
            qn = q * _rstd(q) * gq_ref[...]
            s = _dot(qn, kn, NT_DIMS) * scale
            p = jnp.exp(s - jnp.max(s, axis=-1, keepdims=True))
            p = p / jnp.sum(p, axis=-1, keepdims=True)
            mo = _dot(p, vv, NN_DIMS)
            z, dg_ = z_ref[rows, :], dmo_ref[rows, :]
            sg = _sigmoid(z)
            do = dg_ * (z * sg)
            stage[1, rows, :] = (dg_ * mo * (sg * (1.0 + z * (1.0 - sg)))).astype(bf16)
            dp = _dot(do, vv, NT_DIMS)
            ds = p * (dp - jnp.sum(do * mo, axis=-1, keepdims=True)) * scale
            dq, gq_p = _rms_bwd_rows(q, gq_ref[...], _dot(ds, kn, NN_DIMS))
            stage[0, rows, :] = dq.astype(bf16)
            gq_s[...] += gq_p
            dkn_s[...] += _dot(ds, qn, TN_DIMS)
            dv_s[...] += _dot(p, do, TN_DIMS)
            return carry

        lax.fori_loop(0, S // MEM_CHUNK, chunk, 0)
        cps = []
        for slot, col0 in enumerate((MQ, ZM)):
            cp = pltpu.make_async_copy(stage.at[slot], dproj_ref.at[pl.ds(b * S, S), pl.ds(col0 + h * MEM_HD, MEM_HD)], sem.at[slot])
            cp.start()
            cps.append(cp)
        dk, gk_p = _rms_bwd_rows(kv, gk_ref[...], dkn_s[...])
        dmk_ref[...] = dk
        dmv_ref[...] = dv_s[...]

        @pl.when((b == 0) & (h == 0))
        def _():
            dgq_ref[...] = gq_s[...]
            dgk_ref[...] = gk_p

        @pl.when((b > 0) | (h > 0))
        def _():
            dgq_ref[...] += gq_s[...]
            dgk_ref[...] += gk_p

        for cp in cps:
            cp.wait()

    gain = pl.BlockSpec((1, MEM_HD), lambda b, h: (0, 0))
    kvb = pl.BlockSpec((MEM_LEN, MEM_HD), lambda b, h: (b, h))
    return pl.pallas_call(
        body, grid=(B, MEM_HEADS),
        in_specs=[pl.BlockSpec((S, MEM_HD), lambda b, h: (b, MQ // MEM_HD + h)),
                  pl.BlockSpec((S, MEM_HD), lambda b, h: (b, ZM // MEM_HD + h)),
                  kvb, pl.BlockSpec((MEM_LEN, MEM_HD), lambda b, h: (b, MEM_HEADS + h)), gain, gain,
                  pl.BlockSpec((S, MEM_HD), lambda b, h: (b, h)), pl.BlockSpec(memory_space=pl.ANY)],
        out_specs=[pl.BlockSpec(memory_space=pl.ANY), kvb, kvb, gain, gain],
        out_shape=[jax.ShapeDtypeStruct(dproj.shape, dproj.dtype), jax.ShapeDtypeStruct((B * MEM_LEN, MEM_W), f32),
                   jax.ShapeDtypeStruct((B * MEM_LEN, MEM_W), f32), jax.ShapeDtypeStruct((1, MEM_HD), f32),
                   jax.ShapeDtypeStruct((1, MEM_HD), f32)],
        scratch_shapes=[pltpu.VMEM((MEM_LEN, MEM_HD), f32), pltpu.VMEM((MEM_LEN, MEM_HD), f32), pltpu.VMEM((1, MEM_HD), f32),
                        pltpu.VMEM((2, S, MEM_HD), bf16), pltpu.SemaphoreType.DMA((2,))],
        input_output_aliases={7: 0},
        compiler_params=_params(("arbitrary", "arbitrary")), name="mem_bwd",
    )(proj, proj, mkv, mkv, gq, gk, dmo, dproj)


def _merge_fwd(proj, ag, cg, mg, wa, wc, wm):
    T, NC = proj.shape
    D = wa.shape[1]
    tm, tn = _pick(T, 1024), _pick(D, 512)
    assert GT % tn == 0

    def body(ag_ref, cg_ref, mg_ref, wa_ref, wc_ref, wm_ref, g0_ref, g1_ref, g2_ref, mer_ref, ba_ref, bc_ref, bm_ref):
        ba = _dot(ag_ref[...], wa_ref[...], NN_DIMS)
        bc = _dot(cg_ref[...], wc_ref[...], NN_DIMS)
        bm = _dot(mg_ref[...], wm_ref[...], NN_DIMS)
        mer_ref[...] = (_sigmoid(g0_ref[...]) * ba + _sigmoid(g1_ref[...]) * bc + _sigmoid(g2_ref[...]) * bm).astype(bf16)
        ba_ref[...] = ba.astype(bf16)
        bc_ref[...] = bc.astype(bf16)
        bm_ref[...] = bm.astype(bf16)

    def act(w):
        return pl.BlockSpec((tm, w), lambda i, j: (i, 0))

    def wt(k):
        return pl.BlockSpec((k, tn), lambda i, j: (0, j))

    def gate(n):
        return pl.BlockSpec((tm, tn), lambda i, j: (i, (GT + n * D) // tn + j))

    out = pl.BlockSpec((tm, tn), lambda i, j: (i, j))
    return pl.pallas_call(
        body, grid=(T // tm, D // tn),
        in_specs=[act(ATTN_OUT), act(CONV_W), act(MEM_W), wt(ATTN_OUT), wt(CONV_W), wt(MEM_W), gate(0), gate(1), gate(2)],
        out_specs=[out] * 4, out_shape=[jax.ShapeDtypeStruct((T, D), bf16)] * 4,
        compiler_params=_params(("parallel", "parallel")), name="merge_fwd",
    )(ag, cg, mg, wa, wc, wm, proj, proj, proj)


def _out_loss(merged, w_out, x, tgt):
    T, D = x.shape
    tm, tn = _pick(T, 512), _pick(D, 512)

    def body(m_ref, w_ref, x_ref, t_ref, dy_ref, dyb_ref, lp_ref):
        e = x_ref[...] + _dot(m_ref[...], w_ref[...], NN_DIMS) - t_ref[...]
        dy = e / D
        dy_ref[...] = dy
        dyb_ref[...] = dy.astype(bf16)
        part = jnp.full((1, 8, LANES), jnp.sum(e * e), f32)

        @pl.when(pl.program_id(1) == 0)
        def _():
            lp_ref[...] = part

        @pl.when(pl.program_id(1) > 0)
        def _():
            lp_ref[...] += part

    tile = pl.BlockSpec((tm, tn), lambda i, j: (i, j))
    return pl.pallas_call(
        body, grid=(T // tm, D // tn),
        in_specs=[pl.BlockSpec((tm, D), lambda i, j: (i, 0)), pl.BlockSpec((D, tn), lambda i, j: (0, j)), tile, tile],
        out_specs=[tile, tile, pl.BlockSpec((1, 8, LANES), lambda i, j: (i, 0, 0))],
        out_shape=[jax.ShapeDtypeStruct((T, D), f32), jax.ShapeDtypeStruct((T, D), bf16),
                   jax.ShapeDtypeStruct((T // tm, 8, LANES), f32)],
        compiler_params=_params(("parallel", "arbitrary")), name="out_loss",
    )(merged, w_out, x, tgt)


def _merge_bwd(proj, dyb, w_out, ba, bc, bm):
    T, NC = proj.shape
    D = w_out.shape[0]
    tm, tn = _pick(T, 512), _pick(D, 512)
    assert GT % tn == 0

    def body(dy_ref, w_ref, ba_ref, bc_ref, bm_ref, g0_ref, g1_ref, g2_ref, da_ref, dc_ref, dm_ref, dproj_ref, stage, sem):
        i, j = pl.program_id(0), pl.program_id(1)
        dmer = _dot(dy_ref[...], w_ref[...], NT_DIMS)
        cps = []
        for n, (g_ref, br_ref, o_ref) in enumerate(((g0_ref, ba_ref, da_ref), (g1_ref, bc_ref, dc_ref), (g2_ref, bm_ref, dm_ref))):
            sg = _sigmoid(g_ref[...])
            o_ref[...] = (sg * dmer).astype(bf16)
            stage[n] = (dmer * br_ref[...].astype(f32) * (sg * (1.0 - sg))).astype(bf16)
            cp = pltpu.make_async_copy(stage.at[n], dproj_ref.at[pl.ds(i * tm, tm), pl.ds(GT + n * D + j * tn, tn)], sem.at[n])
            cp.start()
            cps.append(cp)
        for cp in cps:
            cp.wait()

    tile = pl.BlockSpec((tm, tn), lambda i, j: (i, j))

    def gate(n):
        return pl.BlockSpec((tm, tn), lambda i, j: (i, (GT + n * D) // tn + j))

    return pl.pallas_call(
        body, grid=(T // tm, D // tn),
        in_specs=[pl.BlockSpec((tm, D), lambda i, j: (i, 0)), pl.BlockSpec((tn, D), lambda i, j: (j, 0)), tile, tile, tile,
                  gate(0), gate(1), gate(2)],
        out_specs=[tile, tile, tile, pl.BlockSpec(memory_space=pl.ANY)],
        out_shape=[jax.ShapeDtypeStruct((T, D), bf16)] * 3 + [jax.ShapeDtypeStruct((T, NC), bf16)],
        scratch_shapes=[pltpu.VMEM((3, tm, tn), bf16), pltpu.SemaphoreType.DMA((3,))],
        compiler_params=_params(("arbitrary", "arbitrary")), name="merge_bwd",
    )(dyb, w_out, ba, bc, bm, proj, proj, proj)


def _fwd_bwd_head(x, mem, tgt, norm_g, mem_norm_g, attn_q_norm, attn_k_norm, conv_w, mem_q_norm, mem_k_norm,
                  w_in, w_kv, w_a, w_c, w_m, w_out):
    B, S, D = x.shape
    T = B * S
    x2, t2, mem2 = x.reshape(T, D), tgt.reshape(T, D), mem.reshape(B * MEM_LEN, D)
    ng, mng = norm_g.reshape(1, D), mem_norm_g.reshape(1, D)
    mqn, mkn = mem_q_norm.reshape(1, MEM_HD), mem_k_norm.reshape(1, MEM_HD)

    h = _rms_fwd(x2, ng, "rms_x")
    proj = _mm(h, w_in, mode="nn", out_dtype=f32, name="proj")
    mh = _rms_fwd(mem2, mng, "rms_mem")
    mkv = _mm(mh, w_kv, mode="nn", out_dtype=f32, name="mkv")
    ag, a_comb, lse = _attn_fwd(proj, attn_q_norm, attn_k_norm, B, S)
    cg = _conv_fwd(proj, conv_w, B, S)
    mg = _mem_fwd(proj, mkv, mqn, mkn, B, S)
    merged, ba, bc, bm = _merge_fwd(proj, ag, cg, mg, w_a, w_c, w_m)
    dy, dyb, lp = _out_loss(merged, w_out, x2, t2)
    sq_err = jnp.sum(lp[:, 0, 0])

    g = {}
    g["w_out"] = _mm(merged, dyb, mode="tn", out_dtype=f32, name="dw_out")
    dba, dbc, dbm, dproj = _merge_bwd(proj, dyb, w_out, ba, bc, bm)
    g["w_br_attn"] = _mm(ag, dba, mode="tn", out_dtype=f32, name="dw_br_attn")
    g["w_br_conv"] = _mm(cg, dbc, mode="tn", out_dtype=f32, name="dw_br_conv")
    g["w_br_mem"] = _mm(mg, dbm, mode="tn", out_dtype=f32, name="dw_br_mem")
    dag = _mm(dba, w_a, mode="nt", out_dtype=f32, name="d_attn_out")
    dcg = _mm(dbc, w_c, mode="nt", out_dtype=f32, name="d_conv_out")
    dmg = _mm(dbm, w_m, mode="nt", out_dtype=f32, name="d_mem_out")
    dproj, g["attn_q_norm"], g["attn_k_norm"] = _attn_bwd(proj, attn_q_norm, attn_k_norm, a_comb, lse, dag, dproj, B, S)
    dproj, g["conv_w"] = _conv_bwd(proj, conv_w, dcg, dproj, B, S)
    dproj, dmk, dmv, dgmq, dgmk = _mem_bwd(proj, mkv, mqn, mkn, dmg, dproj, B, S)
    g["mem_q_norm"], g["mem_k_norm"] = dgmq.reshape(MEM_HD), dgmk.reshape(MEM_HD)
    dmkv = jnp.concatenate([dmk, dmv], axis=1)
    dmh = _mm(dmkv, w_kv, mode="nt", out_dtype=f32, name="d_mem_h")
    g["mem_norm_g"] = _rms_bwd(mem2, mng, dmh, None, "rms_mem_bwd").reshape(D)
    return sq_err, g, dict(x2=x2, ng=ng, dy=dy, h=h, dproj=dproj, mh=mh, dmkv=dmkv)


MESH = pl.DeviceIdType.MESH
HBM = pl.BlockSpec(memory_space=pl.ANY)


def _me():
    x, y, c = lax.axis_index("x"), lax.axis_index("y"), lax.axis_index("c")
    return (x, y, c), 4 * x + 2 * y + c


def _peer(k):
    (x, y, c), _ = _me()
    p = (1 - x if k & 4 else x, 1 - y if k & 2 else y, 1 - c if k & 1 else c)
    return p, 4 * p[0] + 2 * p[1] + p[2]


def _window(ref, axis, size, idx):
    if axis is None:
        return ref.at[idx]
    if axis == 0:
        return ref.at[pl.ds(idx * size, size), :]
    return ref.at[:, pl.ds(idx * size, size)]


def _gather_weights(shards, axes):
    n = len(shards)

    def full_shape(s, axis):
        if axis is None:
            return (N_DEV,) + s.shape
        return tuple(N_DEV * d if i == axis else d for i, d in enumerate(s.shape))

    def body(*refs):
        ins, outs = refs[:n], refs[n:2 * n]
        send_sems, recv_sems, local_sems = refs[2 * n:]
        (x, y, c), me = _me()
        sibling, sib_id = _peer(1)
        chips = [_peer(4), _peer(2), _peer(6)]

        def win(a, idx):
            return _window(outs[a], axes[a], shards[a].shape[axes[a]] if axes[a] is not None else None, idx)

        def copy(a, k, block, to, src=None):
            return pltpu.make_async_remote_copy(
                src_ref=win(a, block) if src is None else src, dst_ref=win(a, block),
                send_sem=send_sems.at[a, k], recv_sem=recv_sems.at[a, k], device_id=to, device_id_type=MESH)

        mine = [pltpu.make_async_copy(ins[a], win(a, me), local_sems.at[a]) for a in range(n)]
        for cp in mine:
            cp.start()
        first = []
        for a in range(n):
            first.append(copy(a, 0, me, sibling, src=ins[a]))
            first += [copy(a, 1 + j, me, dev, src=ins[a]) for j, (dev, _) in enumerate(chips)]
        for cp in first:
            cp.start()
        passed = []
        for j, (dev, dev_id) in enumerate(chips):
            for a in range(n):
                copy(a, 1 + j, dev_id, (x, y, c)).wait_recv()
                cp = copy(a, 4 + j, dev_id, sibling)
                cp.start()
                passed.append(cp)
        for a in range(n):
            copy(a, 0, sib_id, (x, y, c)).wait_recv()
        for j, (dev, dev_id) in enumerate(chips):
            for a in range(n):
                copy(a, 4 + j, dev_id + 1 - 2 * c, (x, y, c)).wait_recv()
        for cp in first + passed:
            cp.wait_send()
        for cp in mine:
            cp.wait()

    return pl.pallas_call(
        body, in_specs=[HBM] * n, out_specs=[HBM] * n,
        out_shape=[jax.ShapeDtypeStruct(full_shape(s, ax), s.dtype) for s, ax in zip(shards, axes)],
        scratch_shapes=[pltpu.SemaphoreType.DMA((n, 7)), pltpu.SemaphoreType.DMA((n, 7)), pltpu.SemaphoreType.DMA((n,))],
        name="gather_weights",
    )(*shards)


N_CHIP = 4


def _shard_shape(g, ax):
    return tuple(d // N_DEV if i == ax else d for i, d in enumerate(g.shape))


def _sibling_comm(grads, axes):
    n = len(grads)
    sizes = [g.shape[ax] // N_DEV for g, ax in zip(grads, axes)]

    def copies(ins, lands, send_sems, recv_sems, base=0):
        sibling, _ = _peer(1)
        out = []
        for j in range(N_CHIP):
            _, owner = _peer(2 * j + 1)
            for a in range(n):
                out.append(pltpu.make_async_remote_copy(
                    src_ref=_window(ins[a], axes[a], sizes[a], owner), dst_ref=lands[a].at[j],
                    send_sem=send_sems.at[base + a, j], recv_sem=recv_sems.at[base + a, j], device_id=sibling,
                    device_id_type=MESH))
        return out

    shapes = [jax.ShapeDtypeStruct((N_CHIP,) + _shard_shape(g, ax), g.dtype) for g, ax in zip(grads, axes)]
    return _Comm(grads, shapes, (n, N_CHIP), copies)


def _chips_comm(sums):
    n = len(sums)

    def copies(ins, lands, send_sems, recv_sems, base=0):
        out = []
        for j in range(1, N_CHIP):
            dev, _ = _peer(2 * j)
            for a in range(n):
                out.append(pltpu.make_async_remote_copy(
                    src_ref=ins[a].at[j - 1], dst_ref=lands[a].at[j - 1],
                    send_sem=send_sems.at[base + a, j - 1], recv_sem=recv_sems.at[base + a, j - 1], device_id=dev,
                    device_id_type=MESH))
        return out

    return _Comm(sums, [jax.ShapeDtypeStruct(s.shape, s.dtype) for s in sums], (n, N_CHIP), copies)


def _join(c1, c2):
    n1, m1, k1 = len(c1.ins), len(c1.out_shapes), c1.sem_shape[0]

    def copies(ins, lands, send_sems, recv_sems):
        return (c1.copies(ins[:n1], lands[:m1], send_sems, recv_sems, base=0)
                + c2.copies(ins[n1:], lands[m1:], send_sems, recv_sems, base=k1))

    return _Comm(c1.ins + c2.ins, c1.out_shapes + c2.out_shapes, (k1 + c2.sem_shape[0], N_CHIP), copies)


def _rs_chip_sum(grad, land, xyc, axis, name):
    R, C = land.shape[1:]
    tr = _pick(R, max(8, (256 * 1024) // C), 8)

    def body(xyc_ref, g_ref, l_ref, o_ref):
        o_ref[0] = (g_ref[...] + l_ref[0]).astype(bf16)

    def owner(j, xyc_ref):
        jj = j + 1
        return 4 * (xyc_ref[0] ^ (jj >> 1)) + 2 * (xyc_ref[1] ^ (jj & 1)) + xyc_ref[2]

    if axis == 0:
        g_spec = pl.BlockSpec((tr, C), lambda j, i, xyc_ref: (owner(j, xyc_ref) * (R // tr) + i, 0))
    else:
        g_spec = pl.BlockSpec((tr, C), lambda j, i, xyc_ref: (i, owner(j, xyc_ref)))
    return pl.pallas_call(
        body,
        grid_spec=pltpu.PrefetchScalarGridSpec(
            num_scalar_prefetch=1, grid=(N_CHIP - 1, R // tr),
            in_specs=[g_spec, pl.BlockSpec((1, tr, C), lambda j, i, xyc_ref: (j + 1, i, 0))],
            out_specs=pl.BlockSpec((1, tr, C), lambda j, i, xyc_ref: (j, i, 0))),
        out_shape=jax.ShapeDtypeStruct((N_CHIP - 1, R, C), bf16),
        compiler_params=_params(("parallel", "parallel")), name=name,
    )(xyc, grad, land)


def _allreduce_small(part):
    R = part.shape[0]

    def body(p_ref, o_ref, buf, send_sems, recv_sems):
        (x, y, c), me = _me()
        buf[me] = p_ref[...]
        copies = []
        for k in range(1, N_DEV):
            dev, dev_id = _peer(k)
            copies.append(pltpu.make_async_remote_copy(
                src_ref=p_ref, dst_ref=buf.at[me], send_sem=send_sems.at[k - 1], recv_sem=recv_sems.at[k - 1],
                device_id=dev, device_id_type=MESH))
        for cp in copies:
            cp.start()
        for k in range(1, N_DEV):
            _, dev_id = _peer(k)
            pltpu.make_async_remote_copy(
                src_ref=p_ref, dst_ref=buf.at[dev_id], send_sem=send_sems.at[k - 1], recv_sem=recv_sems.at[k - 1],
                device_id=(x, y, c), device_id_type=MESH).wait_recv()
        for cp in copies:
            cp.wait_send()
        acc = buf[0]
        for d in range(1, N_DEV):
            acc = acc + buf[d]
        o_ref[...] = acc

    return pl.pallas_call(
        body, in_specs=[pl.BlockSpec(memory_space=pltpu.VMEM)], out_specs=pl.BlockSpec(memory_space=pltpu.VMEM),
        out_shape=jax.ShapeDtypeStruct((R, LANES), f32),
        scratch_shapes=[pltpu.VMEM((N_DEV, R, LANES), f32), pltpu.SemaphoreType.DMA((N_DEV - 1,)), pltpu.SemaphoreType.DMA((N_DEV - 1,))],
        name="allreduce_small",
    )(part)


def _adamw_math(w, g, m, v):
    m2 = ADAM_B1 * m + (1.0 - ADAM_B1) * g
    v2 = ADAM_B2 * v + (1.0 - ADAM_B2) * (g * g)
    m_hat = m2 / (1.0 - ADAM_B1 ** ADAM_STEP)
    v_hat = v2 / (1.0 - ADAM_B2 ** ADAM_STEP)
    return -ADAM_LR * (m_hat / (jnp.sqrt(v_hat) + ADAM_EPS) + ADAM_WD * w), m2, v2


def _adamw_shard(grad, land_sib, land_chips, w, m, v, me, axis, name, row0=0, prev=None):
    R, C = land_sib.shape[1:]
    assert axis == 1 or R == w.shape[0]
    tr = _pick(R, max(8, (128 * 1024) // C), 8)
    r0 = row0 // tr
    n_prev = len(prev) if prev else 0

    def body(me_ref, g_ref, s_ref, l_ref, w_ref, m_ref, v_ref, *rest):
        go_ref, d_ref, mo_ref, vo_ref = rest[n_prev:]
        g = g_ref[...] + s_ref[0]
        for j in range(N_CHIP - 1):
            g = g + l_ref[j].astype(f32)
        d, m2, v2 = _adamw_math(w_ref[...], g, m_ref[...], v_ref[...])
        go_ref[...] = g
        d_ref[...] = d
        mo_ref[...] = m2
        vo_ref[...] = v2

    if axis == 0:
        g_spec = pl.BlockSpec((tr, C), lambda i, me_ref: (me_ref[0] * (R // tr) + i, 0))
    else:
        g_spec = pl.BlockSpec((tr, C), lambda i, me_ref: (i, me_ref[0]))
    blk = pl.BlockSpec((tr, C), lambda i, me_ref: (r0 + i, 0))
    return pl.pallas_call(
        body,
        grid_spec=pltpu.PrefetchScalarGridSpec(
            num_scalar_prefetch=1, grid=(R // tr,),
            in_specs=[g_spec, pl.BlockSpec((1, tr, C), lambda i, me_ref: (0, i, 0)),
                      pl.BlockSpec((N_CHIP - 1, tr, C), lambda i, me_ref: (0, i, 0)), blk, blk, blk]
            + [pl.BlockSpec(memory_space=pl.ANY)] * n_prev,
            out_specs=[blk] * 4),
        out_shape=[jax.ShapeDtypeStruct(w.shape, f32)] * 4,
        input_output_aliases={7 + i: i for i in range(n_prev)},
        compiler_params=_params(("parallel",)), name=name,
    )(me, grad, land_sib, land_chips, w, m, v, *(prev or []))


def _adamw_small(g, w, m, v):
    def body(g_ref, w_ref, m_ref, v_ref, d_ref, mo_ref, vo_ref):
        d, m2, v2 = _adamw_math(w_ref[...], g_ref[...], m_ref[...], v_ref[...])
        d_ref[...] = d
        mo_ref[...] = m2
        vo_ref[...] = v2

    return pl.pallas_call(body, out_shape=[jax.ShapeDtypeStruct(g.shape, f32)] * 3, name="adamw_small")(g, w, m, v)


def _pack_rows(parts, total_rows):
    rows = jnp.concatenate([p.reshape(-1, LANES) for p in parts], axis=0)
    return jnp.pad(rows, ((0, total_rows - rows.shape[0]), (0, 0)))


BIG = ("w_in", "mem_w_kv", "w_br_attn", "w_br_conv", "w_br_mem", "w_out")
BIG_AXIS = {"w_in": 1, "mem_w_kv": 0, "w_br_attn": 1, "w_br_conv": 1, "w_br_mem": 1, "w_out": 0}
SMALL = ("norm_g", "mem_norm_g", "attn_q_norm", "attn_k_norm", "mem_q_norm", "mem_k_norm")
ALL_W = ("norm_g", "mem_norm_g", "w_in", "attn_q_norm", "attn_k_norm", "conv_w", "mem_w_kv", "mem_q_norm", "mem_k_norm",
         "w_br_attn", "w_br_conv", "w_br_mem", "w_out")


def kernel(x, mem, norm_g, mem_norm_g, w_in, attn_q_norm, attn_k_norm, conv_w, mem_w_kv, mem_q_norm, mem_k_norm, w_br_attn, w_br_conv, w_br_mem, w_out, loss_target, m_norm_g, m_mem_norm_g, m_w_in, m_attn_q_norm, m_attn_k_norm, m_conv_w, m_mem_w_kv, m_mem_q_norm, m_mem_k_norm, m_w_br_attn, m_w_br_conv, m_w_br_mem, m_w_out, v_norm_g, v_mem_norm_g, v_w_in, v_attn_q_norm, v_attn_k_norm, v_conv_w, v_mem_w_kv, v_mem_q_norm, v_mem_k_norm, v_w_br_attn, v_w_br_conv, v_w_br_mem, v_w_out):
    w = dict(norm_g=norm_g, mem_norm_g=mem_norm_g, w_in=w_in, attn_q_norm=attn_q_norm, attn_k_norm=attn_k_norm, conv_w=conv_w,
             mem_w_kv=mem_w_kv, mem_q_norm=mem_q_norm, mem_k_norm=mem_k_norm, w_br_attn=w_br_attn, w_br_conv=w_br_conv,
             w_br_mem=w_br_mem, w_out=w_out)
    mo = dict(norm_g=m_norm_g, mem_norm_g=m_mem_norm_g, w_in=m_w_in, attn_q_norm=m_attn_q_norm, attn_k_norm=m_attn_k_norm,
              conv_w=m_conv_w, mem_w_kv=m_mem_w_kv, mem_q_norm=m_mem_q_norm, mem_k_norm=m_mem_k_norm, w_br_attn=m_w_br_attn,
              w_br_conv=m_w_br_conv, w_br_mem=m_w_br_mem, w_out=m_w_out)
    vo = dict(norm_g=v_norm_g, mem_norm_g=v_mem_norm_g, w_in=v_w_in, attn_q_norm=v_attn_q_norm, attn_k_norm=v_attn_k_norm,
              conv_w=v_conv_w, mem_w_kv=v_mem_w_kv, mem_q_norm=v_mem_q_norm, mem_k_norm=v_mem_k_norm, w_br_attn=v_w_br_attn,
              w_br_conv=v_w_br_conv, w_br_mem=v_w_br_mem, w_out=v_w_out)
    B, S, D = x.shape
    me = (4 * lax.axis_index("x") + 2 * lax.axis_index("y") + lax.axis_index("c")).astype(jnp.int32)

    conv_pad = jnp.pad(conv_w, ((0, 8 - conv_w.shape[0]), (0, 0)))
    full = _gather_weights([w[n].astype(bf16) for n in BIG] + [conv_pad], [BIG_AXIS[n] for n in BIG] + [None])
    wf = dict(zip(BIG, full[:-1]))
    conv_full = full[-1][:, :3, :].transpose(1, 0, 2).reshape(3, CONV_W)

    sq_err, g, t = _fwd_bwd_head(x, mem, loss_target, norm_g, mem_norm_g, attn_q_norm, attn_k_norm, conv_full, mem_q_norm,
                                 mem_k_norm, wf["w_in"], wf["mem_w_kv"], wf["w_br_attn"], wf["w_br_conv"], wf["w_br_mem"], wf["w_out"])

    xyc = jnp.stack([lax.axis_index("x"), lax.axis_index("y"), lax.axis_index("c")]).astype(jnp.int32)
    me1 = me.reshape(1)
    early = ("w_out", "w_br_attn", "w_br_conv", "w_br_mem")
    g["mem_w_kv"], sib_early = _mm(t["mh"], t["dmkv"], mode="tn", out_dtype=f32, name="dw_kv",
                                   comm=_sibling_comm([g[n] for n in early], [BIG_AXIS[n] for n in early]))
    sums_early = [_rs_chip_sum(g[n], ls, xyc, BIG_AXIS[n], "rs_sum_" + n) for n, ls in zip(early, sib_early)]
    tm_w = _pick(D // 2, 1024)
    half = (D // 2) // tm_w
    g_top, chips_early = _mm(t["h"], t["dproj"], mode="tn", out_dtype=f32, name="dw_in_top", tm=tm_w, m_tiles=(0, half),
                             comm=_chips_comm(sums_early))
    g_bot, (sib_top, sib_kv) = _mm(t["h"], t["dproj"], mode="tn", out_dtype=f32, name="dw_in_bot", tm=tm_w, m_tiles=(half, half),
                                   comm=_sibling_comm([g_top, g["mem_w_kv"]], [1, 0]))
    sum_top = _rs_chip_sum(g_top, sib_top, xyc, 1, "rs_sum_w_in_top")
    sum_kv = _rs_chip_sum(g["mem_w_kv"], sib_kv, xyc, 0, "rs_sum_mem_w_kv")
    T = B * S
    tm_t = _pick(T // 2, 1024)
    halfm = (T // 2) // tm_t
    dh, (chips_top, chips_kv, sib_bot) = _mm(t["dproj"], wf["w_in"], mode="nt", out_dtype=f32, name="d_h_a", tm=tm_t,
                                             m_tiles=(0, halfm), into="new",
                                             comm=_join(_chips_comm([sum_top, sum_kv]), _sibling_comm([g_bot], [1])))
    sum_bot = _rs_chip_sum(g_bot, sib_bot, xyc, 1, "rs_sum_w_in_bot")
    dh, (chips_bot,) = _mm(t["dproj"], wf["w_in"], mode="nt", out_dtype=f32, name="d_h_b", tm=tm_t, m_tiles=(halfm, halfm),
                           into=dh, comm=_chips_comm([sum_bot]))
    dx, dng = _rms_bwd(t["x2"], t["ng"], dh, t["dy"], "rms_x_bwd")
    g["norm_g"] = dng.reshape(D)

    grad, delta, new_m, new_v = {}, {}, {}, {}
    for n, ls, lc in zip(early + ("mem_w_kv",), sib_early + [sib_kv], chips_early + [chips_kv]):
        grad[n], delta[n], new_m[n], new_v[n] = _adamw_shard(g[n], ls, lc, w[n], mo[n], vo[n], me1, BIG_AXIS[n], "adamw_" + n)
    top = _adamw_shard(g_top, sib_top, chips_top, w_in, m_w_in, v_w_in, me1, 1, "adamw_w_in_top")
    grad["w_in"], delta["w_in"], new_m["w_in"], new_v["w_in"] = _adamw_shard(
        g_bot, sib_bot, chips_bot, w_in, m_w_in, v_w_in, me1, 1, "adamw_w_in_bot", row0=D // 2, prev=top)

    n_rep = sum(w[n].size for n in SMALL) // LANES
    conv_rows = g["conv_w"].reshape(3, N_DEV, LANES).transpose(1, 0, 2).reshape(3 * N_DEV, LANES)
    n_rows = -(-(n_rep + 3 * N_DEV + 1) // 8) * 8
    part = _pack_rows([g[n] for n in SMALL] + [conv_rows, jnp.full((1, LANES), sq_err, f32)], n_rows)
    tot = _allreduce_small(part)
    loss = tot[n_rep + 3 * N_DEV, 0] * (0.5 / D)
    n_small = -(-(n_rep + 3) // 8) * 8
    g_small = _pack_rows([tot[:n_rep], lax.dynamic_slice(tot, (n_rep + 3 * me, 0), (3, LANES))], n_small)
    names = SMALL + ("conv_w",)
    d_s, m_s, v_s = _adamw_small(g_small, _pack_rows([w[n] for n in names], n_small), _pack_rows([mo[n] for n in names], n_small),
                                 _pack_rows([vo[n] for n in names], n_small))
    off = 0
    for n in names:
        r = w[n].size // LANES
        grad[n], delta[n] = g_small[off:off + r].reshape(w[n].shape), d_s[off:off + r].reshape(w[n].shape)
        new_m[n], new_v[n] = m_s[off:off + r].reshape(w[n].shape), v_s[off:off + r].reshape(w[n].shape)
        off += r
    return (loss, dx.reshape(B, S, D), *[grad[n] for n in ALL_W], *[delta[n] for n in ALL_W], *[new_m[n] for n in ALL_W],
            *[new_v[n] for n in ALL_W])
```

```python
import functools

import jax
import jax.numpy as jnp
from jax import lax
from jax.experimental import pallas as pl
from jax.experimental.pallas import tpu as pltpu

f32 = jnp.float32
bf16 = jnp.bfloat16

N_DEV = 8
HEAD_DIM = 128
DILATIONS = (1, 4, 16)
N_GROUPS = 3
HEADS = 4
BLK = 128
ATTN_QKV = N_GROUPS * HEADS * HEAD_DIM
ATTN_OUT = HEADS * HEAD_DIM
CONV_W = 1024
MEM_LEN = 256
MEM_HEADS = 4
MEM_HD = 256
MEM_W = MEM_HEADS * MEM_HD
EPS = 1e-6
Q0, K0, V0 = 0, ATTN_QKV, 2 * ATTN_QKV
ZA = 3 * ATTN_QKV
CB, CC, CV, ZC = ZA + ATTN_OUT, ZA + ATTN_OUT + CONV_W, ZA + ATTN_OUT + 2 * CONV_W, ZA + ATTN_OUT + 3 * CONV_W
MQ = ZC + CONV_W
ZM = MQ + MEM_W
GT = ZM + MEM_W

ADAM_LR, ADAM_B1, ADAM_B2, ADAM_EPS, ADAM_WD, ADAM_STEP = 0.001, 0.9, 0.999, 1e-08, 0.01, 10

VMEM_LIMIT = 56 * 1024 * 1024
LANES = 128

NT_DIMS = (((1,), (1,)), ((), ()))
TN_DIMS = (((0,), (0,)), ((), ()))
NN_DIMS = (((1,), (0,)), ((), ()))


def _params(sem=None):
    return pltpu.CompilerParams(dimension_semantics=sem, vmem_limit_bytes=VMEM_LIMIT)


def _pick(n, pref, q=LANES):
    t = (min(pref, n) // q) * q
    while t >= q:
        if n % t == 0:
            return t
        t -= q
    return n


def _dot(a, b, dims):
    return lax.dot_general(a.astype(bf16), b.astype(bf16), dims, preferred_element_type=f32)


def _sigmoid(z):
    return 1.0 / (1.0 + jnp.exp(-z))


def _rstd(v):
    return lax.rsqrt(jnp.mean(v * v, axis=-1, keepdims=True) + EPS)


class _Comm:
    def __init__(self, ins, out_shapes, sem_shape, copies):
        self.ins, self.out_shapes, self.sem_shape, self.copies = list(ins), list(out_shapes), sem_shape, copies

    def scratch(self):
        return [pltpu.SemaphoreType.DMA(self.sem_shape), pltpu.SemaphoreType.DMA(self.sem_shape)]


def _mm(a, b, *, mode, out_dtype, name, tm=1024, tn=1024, tk=4096, m_tiles=None, into=None, comm=None):
    if mode == "nn":
        (M, K), (K2, N) = a.shape, b.shape
    elif mode == "nt":
        (M, K), (N, K2) = a.shape, b.shape
    else:
        (K, M), (K2, N) = a.shape, b.shape
    assert K == K2
    tm, tn, tk = _pick(M, tm), _pick(N, tn), _pick(K, tk)
    nk = K // tk
    m0, mc = m_tiles if m_tiles is not None else (0, M // tm)
    o0 = 0 if into is None else m0
    aliased = into is not None and not isinstance(into, str)
    n_ci = len(comm.ins) if comm else 0
    n_co = len(comm.out_shapes) if comm else 0
    grid = (mc, N // tn, nk)
    dims = {"nn": NN_DIMS, "nt": NT_DIMS, "tn": TN_DIMS}[mode]

    def body(*refs, nk):
        a_ref, b_ref = refs[:2]
        pos = 3 if aliased else 2
        c_ins, o_ref, c_outs = refs[pos:pos + n_ci], refs[pos + n_ci], refs[pos + n_ci + 1:pos + n_ci + 1 + n_co]
        scratch = refs[pos + n_ci + 1 + n_co:]
        ids = [pl.program_id(d) for d in range(3)]
        if comm:
            sems = scratch[-2:]

            @pl.when((ids[0] == 0) & (ids[1] == 0) & (ids[2] == 0))
            def _():
                for cp in comm.copies(c_ins, c_outs, *sems):
                    cp.start()

        if nk == 1:
            o_ref[...] = _dot(a_ref[...], b_ref[...], dims).astype(o_ref.dtype)
        else:
            acc_ref, k = scratch[0], ids[2]

            @pl.when(k == 0)
            def _():
                acc_ref[...] = _dot(a_ref[...], b_ref[...], dims)

            @pl.when((k > 0) & (k < nk - 1))
            def _():
                acc_ref[...] += _dot(a_ref[...], b_ref[...], dims)

            @pl.when(k == nk - 1)
            def _():
                o_ref[...] = (acc_ref[...] + _dot(a_ref[...], b_ref[...], dims)).astype(o_ref.dtype)

        if comm:
            @pl.when((ids[0] == grid[0] - 1) & (ids[1] == grid[1] - 1) & (ids[2] == grid[2] - 1))
            def _():
                for cp in comm.copies(c_ins, c_outs, *sems):
                    cp.wait()

    if mode == "tn":
        a_spec = pl.BlockSpec((tk, tm), lambda i, j, k: (k, m0 + i))
    else:
        a_spec = pl.BlockSpec((tm, tk), lambda i, j, k: (m0 + i, k))
    if mode == "nt":
        b_spec = pl.BlockSpec((tn, tk), lambda i, j, k: (j, k))
    else:
        b_spec = pl.BlockSpec((tk, tn), lambda i, j, k: (k, j))
    hbm = pl.BlockSpec(memory_space=pl.ANY)
    res = pl.pallas_call(
        functools.partial(body, nk=nk),
        grid=grid,
        in_specs=[a_spec, b_spec] + [hbm] * (aliased + n_ci),
        out_specs=[pl.BlockSpec((tm, tn), lambda i, j, k: (o0 + i, j))] + [hbm] * n_co,
        out_shape=[jax.ShapeDtypeStruct((mc * tm if into is None else M, N), out_dtype)] + (comm.out_shapes if comm else []),
        scratch_shapes=([pltpu.VMEM((tm, tn), f32)] if nk > 1 else []) + (comm.scratch() if comm else []),
        input_output_aliases={2: 0} if aliased else {},
        compiler_params=_params(("arbitrary",) * 3 if comm else ("parallel", "parallel", "arbitrary")),
        name=name,
    )(a, b, *([into] if aliased else []), *(comm.ins if comm else []))
    return (res[0], list(res[1:])) if comm else res[0]


def _rms_fwd(x, g, name):
    T, D = x.shape
    tm = _pick(T, 256, 8)

    def body(x_ref, g_ref, h_ref):
        xv = x_ref[...]
        h_ref[...] = (xv * _rstd(xv) * g_ref[...]).astype(bf16)

    return pl.pallas_call(
        body, grid=(T // tm,),
        in_specs=[pl.BlockSpec((tm, D), lambda i: (i, 0)), pl.BlockSpec((1, D), lambda i: (0, 0))],
        out_specs=pl.BlockSpec((tm, D), lambda i: (i, 0)),
        out_shape=jax.ShapeDtypeStruct((T, D), bf16),
        compiler_params=_params(("parallel",)), name=name,
    )(x, g)


def _rms_bwd(x, g, dh, dy, name):
    T, D = x.shape
    tm = _pick(T, 256, 8)
    with_dx = dy is not None

    def body(*refs):
        if with_dx:
            x_ref, g_ref, dh_ref, dy_ref, dx_ref, dg_ref = refs
        else:
            x_ref, g_ref, dh_ref, dg_ref = refs
        xv = x_ref[...]
        r = _rstd(xv)
        xh = xv * r
        dhv = dh_ref[...]
        part = jnp.sum(dhv * xh, axis=0, keepdims=True)

        @pl.when(pl.program_id(0) == 0)
        def _():
            dg_ref[...] = part

        @pl.when(pl.program_id(0) > 0)
        def _():
            dg_ref[...] += part

        if with_dx:
            dxh = dhv * g_ref[...]
            dx_ref[...] = dy_ref[...] + r * (dxh - xh * jnp.mean(dxh * xh, axis=-1, keepdims=True))

    row = pl.BlockSpec((tm, D), lambda i: (i, 0))
    vec = pl.BlockSpec((1, D), lambda i: (0, 0))
    if with_dx:
        return pl.pallas_call(
            body, grid=(T // tm,), in_specs=[row, vec, row, row], out_specs=[row, vec],
            out_shape=[jax.ShapeDtypeStruct((T, D), f32), jax.ShapeDtypeStruct((1, D), f32)],
            compiler_params=_params(("arbitrary",)), name=name,
        )(x, g, dh, dy)
    return pl.pallas_call(
        body, grid=(T // tm,), in_specs=[row, vec, row], out_specs=vec,
        out_shape=jax.ShapeDtypeStruct((1, D), f32),
        compiler_params=_params(("arbitrary",)), name=name,
    )(x, g, dh)


def _rows(start, size, stride):
    return pl.ds(start, size) if stride == 1 else pl.ds(start, size, stride=stride)


def _attn_units(S, d, first, prev):
    nb = S // d // BLK

    def do_first(r, c):
        first(_rows(r, BLK, d))
        return c

    lax.fori_loop(0, d, do_first, 0, unroll=min(d, 2))
    if nb > 1:
        def do_prev(u, c):
            r = u // (nb - 1)
            b = u % (nb - 1) + 1
            base = r + d * b * BLK
            prev(_rows(base, BLK, d), _rows(base - d * BLK, 2 * BLK, d))
            return c

        lax.fori_loop(0, d * (nb - 1), do_prev, 0, unroll=2)


def _band_mask(nkeys):
    i = lax.broadcasted_iota(jnp.int32, (BLK, nkeys), 0)
    j = lax.broadcasted_iota(jnp.int32, (BLK, nkeys), 1)
    if nkeys == BLK:
        return j <= i
    return (j >= i) & (j <= i + BLK)


def _normalize_into(src_ref, gain, dst_ref, S):
    for c in range(S // 256):
        rows = pl.ds(c * 256, 256)
        v = src_ref[rows, :]
        dst_ref[rows, :] = v * _rstd(v) * gain


def _attn_fwd(proj, gq, gk, B, S):
    T, NC = proj.shape
    scale = HEAD_DIM ** -0.5

    def body(q_ref, k_ref, v_ref, z_ref, gq_ref, gk_ref, ag_ref, a_ref, lse_ref, qs, ks, o0, o1, o2, l0, l1, l2):
        g = pl.program_id(2)
        _normalize_into(q_ref, gq_ref[pl.ds(g, 1), :], qs, S)
        _normalize_into(k_ref, gk_ref[pl.ds(g, 1), :], ks, S)
        o_s, l_s = (o0, o1, o2), (l0, l1, l2)

        def run(gi):
            def unit(qr, kr, nkeys):
                s = _dot(qs[qr, :], ks[kr, :], NT_DIMS) * scale
                s = jnp.where(_band_mask(nkeys), s, -jnp.inf)
                m = jnp.max(s, axis=-1, keepdims=True)
                p = jnp.exp(s - m)
                den = jnp.sum(p, axis=-1, keepdims=True)
                o_s[gi][qr, :] = _dot(p / den, v_ref[kr, :], NN_DIMS)
                l_s[gi][qr, :] = jnp.broadcast_to(m + jnp.log(den), (BLK, HEAD_DIM))

            _attn_units(S, DILATIONS[gi], lambda qr: unit(qr, qr, BLK), lambda qr, kr: unit(qr, kr, 2 * BLK))

        for gi in range(N_GROUPS):
            pl.when(g == gi)(functools.partial(run, gi))

        @pl.when(g == N_GROUPS - 1)
        def _():
            for c in range(S // 256):
                rows = pl.ds(c * 256, 256)
                la, lb, lc = l0[rows, :], l1[rows, :], l2[rows, :]
                m = jnp.maximum(jnp.maximum(la, lb), lc)
                wa, wb, wc = jnp.exp(la - m), jnp.exp(lb - m), jnp.exp(lc - m)
                tot = wa + wb + wc
                a = (wa / tot) * o0[rows, :] + (wb / tot) * o1[rows, :] + (wc / tot) * o2[rows, :]
                z = z_ref[rows, :]
                a_ref[rows, :] = a
                lse_ref[rows, :] = m + jnp.log(tot)
                ag_ref[rows, :] = (a * (z * _sigmoid(z))).astype(bf16)

    def slab(col0):
        return pl.BlockSpec((S, HEAD_DIM), lambda b, h, g: (b, col0 // HEAD_DIM + g * HEADS + h))

    gain = pl.BlockSpec((N_GROUPS, HEAD_DIM), lambda b, h, g: (0, 0))
    out = pl.BlockSpec((S, HEAD_DIM), lambda b, h, g: (b, h))
    return pl.pallas_call(
        body, grid=(B, HEADS, N_GROUPS),
        in_specs=[slab(Q0), slab(K0), slab(V0), pl.BlockSpec((S, HEAD_DIM), lambda b, h, g: (b, ZA // HEAD_DIM + h)), gain, gain],
        out_specs=[out, out, out],
        out_shape=[jax.ShapeDtypeStruct((T, ATTN_OUT), bf16), jax.ShapeDtypeStruct((T, ATTN_OUT), f32),
                   jax.ShapeDtypeStruct((T, ATTN_OUT), f32)],
        scratch_shapes=[pltpu.VMEM((S, HEAD_DIM), f32)] * 8,
        compiler_params=_params(("arbitrary", "arbitrary", "arbitrary")), name="attn_fwd",
    )(proj, proj, proj, proj, gq, gk)


def _rms_bwd_rows(raw, gain, dn):
    r = _rstd(raw)
    xh = raw * r
    dxh = dn * gain
    return r * (dxh - xh * jnp.mean(dxh * xh, axis=-1, keepdims=True)), jnp.sum(dn * xh, axis=0, keepdims=True)


def _attn_bwd(proj, gq, gk, a_comb, lse, dag, dproj, B, S):
    T, NC = proj.shape
    scale = HEAD_DIM ** -0.5

    def body(q_ref, k_ref, v_ref, z_ref, gq_ref, gk_ref, a_ref, lse_ref, dag_ref, dproj_in, dproj_ref, dgq_ref, dgk_ref,
             qs, ks, da_s, dl_s, dq_s, dk_s, dv_s, stage, sem):
        b, h, g = pl.program_id(0), pl.program_id(1), pl.program_id(2)
        gq_row, gk_row = gq_ref[pl.ds(g, 1), :], gk_ref[pl.ds(g, 1), :]
        _normalize_into(q_ref, gq_row, qs, S)
        _normalize_into(k_ref, gk_row, ks, S)

        def put(slot, col0):
            cp = pltpu.make_async_copy(stage.at[slot], dproj_ref.at[pl.ds(b * S, S), pl.ds(col0, HEAD_DIM)], sem.at[slot])
            cp.start()
            return cp

        @pl.when((b == 0) & (h == 0) & (g == 0))
        def _():
            dgq_ref[...] = jnp.zeros_like(dgq_ref)
            dgk_ref[...] = jnp.zeros_like(dgk_ref)

        @pl.when(g == 0)
        def _():
            for c in range(S // 256):
                rows = pl.ds(c * 256, 256)
                z, a, dg_ = z_ref[rows, :], a_ref[rows, :], dag_ref[rows, :]
                sg = _sigmoid(z)
                da = dg_ * (z * sg)
                da_s[rows, :] = da
                dl_s[rows, :] = jnp.broadcast_to(jnp.sum(da * a, axis=-1, keepdims=True), (256, HEAD_DIM))
                stage[3, rows, :] = (dg_ * a * (sg * (1.0 + z * (1.0 - sg)))).astype(bf16)
            put(3, ZA + h * HEAD_DIM).wait()

        dk_s[...] = jnp.zeros_like(dk_s)
        dv_s[...] = jnp.zeros_like(dv_s)

        def run(gi):
            def unit(qr, kr, nkeys):
                q, k, v = qs[qr, :], ks[kr, :], v_ref[kr, :]
                s = _dot(q, k, NT_DIMS) * scale
                p = jnp.where(_band_mask(nkeys), jnp.exp(s - lse_ref[qr, :][:, :1]), 0.0)
                da = da_s[qr, :]
                dp = _dot(da, v, NT_DIMS)
                ds = p * (dp - dl_s[qr, :][:, :1]) * scale
                dq_s[qr, :] = _dot(ds, k, NN_DIMS)
                dk_s[kr, :] += _dot(ds, q, TN_DIMS)
                dv_s[kr, :] += _dot(p, da, TN_DIMS)

            _attn_units(S, DILATIONS[gi], lambda qr: unit(qr, qr, BLK), lambda qr, kr: unit(qr, kr, 2 * BLK))

        for gi in range(N_GROUPS):
            pl.when(g == gi)(functools.partial(run, gi))

        gq_acc = jnp.zeros((1, HEAD_DIM), f32)
        gk_acc = jnp.zeros((1, HEAD_DIM), f32)
        for c in range(S // 256):
            rows = pl.ds(c * 256, 256)
            dq, gq_p = _rms_bwd_rows(q_ref[rows, :], gq_row, dq_s[rows, :])
            dk, gk_p = _rms_bwd_rows(k_ref[rows, :], gk_row, dk_s[rows, :])
            gq_acc, gk_acc = gq_acc + gq_p, gk_acc + gk_p
            stage[0, rows, :] = dq.astype(bf16)
            stage[1, rows, :] = dk.astype(bf16)
            stage[2, rows, :] = dv_s[rows, :].astype(bf16)
        dgq_ref[pl.ds(g, 1), :] += gq_acc
        dgk_ref[pl.ds(g, 1), :] += gk_acc
        col = (g * HEADS + h) * HEAD_DIM
        cps = [put(0, Q0 + col), put(1, K0 + col), put(2, V0 + col)]
        for cp in cps:
            cp.wait()

    def slab(col0):
        return pl.BlockSpec((S, HEAD_DIM), lambda b, h, g: (b, col0 // HEAD_DIM + g * HEADS + h))

    gain = pl.BlockSpec((N_GROUPS, HEAD_DIM), lambda b, h, g: (0, 0))
    per_slot = pl.BlockSpec((S, HEAD_DIM), lambda b, h, g: (b, h))
    return pl.pallas_call(
        body, grid=(B, HEADS, N_GROUPS),
        in_specs=[slab(Q0), slab(K0), slab(V0), pl.BlockSpec((S, HEAD_DIM), lambda b, h, g: (b, ZA // HEAD_DIM + h)), gain, gain,
                  per_slot, per_slot, per_slot, pl.BlockSpec(memory_space=pl.ANY)],
        out_specs=[pl.BlockSpec(memory_space=pl.ANY), gain, gain],
        out_shape=[jax.ShapeDtypeStruct(dproj.shape, dproj.dtype), jax.ShapeDtypeStruct((N_GROUPS, HEAD_DIM), f32),
                   jax.ShapeDtypeStruct((N_GROUPS, HEAD_DIM), f32)],
        scratch_shapes=[pltpu.VMEM((S, HEAD_DIM), f32)] * 7 + [pltpu.VMEM((4, S, HEAD_DIM), bf16), pltpu.SemaphoreType.DMA((4,))],
        input_output_aliases={9: 0},
        compiler_params=_params(("arbitrary", "arbitrary", "arbitrary")), name="attn_bwd",
    )(proj, proj, proj, proj, gq, gk, a_comb, lse, dag, dproj)


def _conv_fwd(proj, conv_w, B, S):
    T, NC = proj.shape
    tc = LANES

    def body(cb_ref, cc_ref, cv_ref, z_ref, w_ref, c_ref, ub):
        u = cc_ref[...] * cv_ref[...]
        ub[0:8, :] = jnp.zeros((8, tc), f32)
        ub[8:8 + S, :] = u
        w = w_ref[...]
        y = w[0:1] * u + w[1:2] * ub[pl.ds(7, S), :] + w[2:3] * ub[pl.ds(6, S), :]
        z = z_ref[...]
        c_ref[...] = (cb_ref[...] * y * (z * _sigmoid(z))).astype(bf16)

    def seg(col0):
        return pl.BlockSpec((S, tc), lambda j, b: (b, col0 // tc + j))

    return pl.pallas_call(
        body, grid=(CONV_W // tc, B),
        in_specs=[seg(CB), seg(CC), seg(CV), seg(ZC), pl.BlockSpec((3, tc), lambda j, b: (0, j))],
        out_specs=pl.BlockSpec((S, tc), lambda j, b: (b, j)),
        out_shape=jax.ShapeDtypeStruct((T, CONV_W), bf16),
        scratch_shapes=[pltpu.VMEM((S + 8, tc), f32)],
        compiler_params=_params(("parallel", "parallel")), name="conv_fwd",
    )(proj, proj, proj, proj, conv_w)


def _conv_bwd(proj, conv_w, dc, dproj, B, S):
    T, NC = proj.shape
    tc = LANES

    def body(cb_ref, cc_ref, cv_ref, z_ref, w_ref, dc_ref, dproj_in, dproj_ref, dw_ref, ub, db, stage, sem):
        j, b = pl.program_id(0), pl.program_id(1)
        cb, cc, cv, z, dcv = cb_ref[...], cc_ref[...], cv_ref[...], z_ref[...], dc_ref[...]
        u = cc * cv
        ub[0:8, :] = jnp.zeros((8, tc), f32)
        ub[8:8 + S, :] = u
        u1, u2 = ub[pl.ds(7, S), :], ub[pl.ds(6, S), :]
        w = w_ref[...]
        y = w[0:1] * u + w[1:2] * u1 + w[2:3] * u2
        sg = _sigmoid(z)
        si = z * sg
        dyv = dcv * cb * si
        db[0:S, :] = dyv
        db[S:S + 8, :] = jnp.zeros((8, tc), f32)
        du = w[0:1] * dyv + w[1:2] * db[pl.ds(1, S), :] + w[2:3] * db[pl.ds(2, S), :]
        stage[0] = (dcv * y * si).astype(bf16)
        stage[1] = (du * cv).astype(bf16)
        stage[2] = (du * cc).astype(bf16)
        stage[3] = (dcv * cb * y * (sg * (1.0 + z * (1.0 - sg)))).astype(bf16)
        cps = []
        for slot, col0 in enumerate((CB, CC, CV, ZC)):
            cp = pltpu.make_async_copy(stage.at[slot], dproj_ref.at[pl.ds(b * S, S), pl.ds(col0 + j * tc, tc)], sem.at[slot])
            cp.start()
            cps.append(cp)
        part = jnp.concatenate([jnp.sum(dyv * u, axis=0, keepdims=True), jnp.sum(dyv * u1, axis=0, keepdims=True),
                                jnp.sum(dyv * u2, axis=0, keepdims=True)], axis=0)

        @pl.when(b == 0)
        def _():
            dw_ref[...] = part

        @pl.when(b > 0)
        def _():
            dw_ref[...] += part

        for cp in cps:
            cp.wait()

    def seg(col0):
        return pl.BlockSpec((S, tc), lambda j, b: (b, col0 // tc + j))

    return pl.pallas_call(
        body, grid=(CONV_W // tc, B),
        in_specs=[seg(CB), seg(CC), seg(CV), seg(ZC), pl.BlockSpec((3, tc), lambda j, b: (0, j)),
                  pl.BlockSpec((S, tc), lambda j, b: (b, j)), pl.BlockSpec(memory_space=pl.ANY)],
        out_specs=[pl.BlockSpec(memory_space=pl.ANY), pl.BlockSpec((3, tc), lambda j, b: (0, j))],
        out_shape=[jax.ShapeDtypeStruct(dproj.shape, dproj.dtype), jax.ShapeDtypeStruct((3, CONV_W), f32)],
        scratch_shapes=[pltpu.VMEM((S + 8, tc), f32), pltpu.VMEM((S + 8, tc), f32), pltpu.VMEM((4, S, tc), bf16),
                        pltpu.SemaphoreType.DMA((4,))],
        input_output_aliases={6: 0},
        compiler_params=_params(("arbitrary", "arbitrary")), name="conv_bwd",
    )(proj, proj, proj, proj, conv_w, dc, dproj)


MEM_CHUNK = 256


def _mem_fwd(proj, mkv, gq, gk, B, S):
    T, NC = proj.shape
    scale = MEM_HD ** -0.5

    def body(q_ref, z_ref, k_ref, v_ref, gq_ref, gk_ref, o_ref):
        kv = k_ref[...]
        kn = (kv * _rstd(kv) * gk_ref[...]).astype(bf16)
        vv = v_ref[...].astype(bf16)

        def chunk(c, carry):
            rows = pl.ds(pl.multiple_of(c * MEM_CHUNK, MEM_CHUNK), MEM_CHUNK)
            q = q_ref[rows, :]
            qn = q * _rstd(q) * gq_ref[...]
            s = _dot(qn, kn, NT_DIMS) * scale
            p = jnp.exp(s - jnp.max(s, axis=-1, keepdims=True))
            p = p / jnp.sum(p, axis=-1, keepdims=True)
            z = z_ref[rows, :]
            o_ref[rows, :] = (_dot(p, vv, NN_DIMS) * (z * _sigmoid(z))).astype(bf16)
            return carry

        lax.fori_loop(0, S // MEM_CHUNK, chunk, 0)

    gain = pl.BlockSpec((1, MEM_HD), lambda b, h: (0, 0))
    return pl.pallas_call(
        body, grid=(B, MEM_HEADS),
        in_specs=[pl.BlockSpec((S, MEM_HD), lambda b, h: (b, MQ // MEM_HD + h)),
                  pl.BlockSpec((S, MEM_HD), lambda b, h: (b, ZM // MEM_HD + h)),
                  pl.BlockSpec((MEM_LEN, MEM_HD), lambda b, h: (b, h)),
                  pl.BlockSpec((MEM_LEN, MEM_HD), lambda b, h: (b, MEM_HEADS + h)), gain, gain],
        out_specs=pl.BlockSpec((S, MEM_HD), lambda b, h: (b, h)),
        out_shape=jax.ShapeDtypeStruct((T, MEM_W), bf16),
        compiler_params=_params(("parallel", "parallel")), name="mem_fwd",
    )(proj, proj, mkv, mkv, gq, gk)


def _mem_bwd(proj, mkv, gq, gk, dmo, dproj, B, S):
    T, NC = proj.shape
    scale = MEM_HD ** -0.5

    def body(q_ref, z_ref, k_ref, v_ref, gq_ref, gk_ref, dmo_ref, dproj_in, dproj_ref, dmk_ref, dmv_ref, dgq_ref, dgk_ref,
             dkn_s, dv_s, gq_s, stage, sem):
        b, h = pl.program_id(0), pl.program_id(1)
        kv = k_ref[...]
        kn = (kv * _rstd(kv) * gk_ref[...]).astype(bf16)
        vv = v_ref[...].astype(bf16)
        dkn_s[...] = jnp.zeros_like(dkn_s)
        dv_s[...] = jnp.zeros_like(dv_s)
        gq_s[...] = jnp.zeros_like(gq_s)

        def chunk(c, carry):
            rows = pl.ds(pl.multiple_of(c * MEM_CHUNK, MEM_CHUNK), MEM_CHUNK)
            q = q_ref[rows, :]
            qn = q * _rstd(q) * gq_ref[...]
            s = _dot(qn, kn, NT_DIMS) * scale
            p = jnp.exp(s - jnp.max(s, axis=-1, keepdims=True))
            p = p / jnp.sum(p, axis=-1, keepdims=True)
            mo = _dot(p, vv, NN_DIMS)
            z, dg_ = z_ref[rows, :], dmo_ref[rows, :]
            sg = _sigmoid(z)
            do = dg_ * (z * sg)
            stage[1, rows, :] = (dg_ * mo * (sg * (1.0 + z * (1.0 - sg)))).astype(bf16)
            dp = _dot(do, vv, NT_DIMS)
            ds = p * (dp - jnp.sum(do * mo, axis=-1, keepdims=True)) * scale
            dq, gq_p = _rms_bwd_rows(q, gq_ref[...], _dot(ds, kn, NN_DIMS))
            stage[0, rows, :] = dq.astype(bf16)
            gq_s[...] += gq_p
            dkn_s[...] += _dot(ds, qn, TN_DIMS)
            dv_s[...] += _dot(p, do, TN_DIMS)
            return carry

        lax.fori_loop(0, S // MEM_CHUNK, chunk, 0)
        cps = []
        for slot, col0 in enumerate((MQ, ZM)):
            cp = pltpu.make_async_copy(stage.at[slot], dproj_ref.at[pl.ds(b * S, S), pl.ds(col0 + h * MEM_HD, MEM_HD)], sem.at[slot])
            cp.start()
            cps.append(cp)
        dk, gk_p = _rms_bwd_rows(kv, gk_ref[...], dkn_s[...])
        dmk_ref[...] = dk
        dmv_ref[...] = dv_s[...]

        @pl.when((b == 0) & (h == 0))
        def _():
            dgq_ref[...] = gq_s[...]
            dgk_ref[...] = gk_p

        @pl.when((b > 0) | (h > 0))
        def _():
            dgq_ref[...] += gq_s[...]
            dgk_ref[...] += gk_p

        for cp in cps:
            cp.wait()

    gain = pl.BlockSpec((1, MEM_HD), lambda b, h: (0, 0))
    kvb = pl.BlockSpec((MEM_LEN, MEM_HD), lambda b, h: (b, h))
    return pl.pallas_call(
        body, grid=(B, MEM_HEADS),
        in_specs=[pl.BlockSpec((S, MEM_HD), lambda b, h: (b, MQ // MEM_HD + h)),
                  pl.BlockSpec((S, MEM_HD), lambda b, h: (b, ZM // MEM_HD + h)),
                  kvb, pl.BlockSpec((MEM_LEN, MEM_HD), lambda b, h: (b, MEM_HEADS + h)), gain, gain,
                  pl.BlockSpec((S, MEM_HD), lambda b, h: (b, h)), pl.BlockSpec(memory_space=pl.ANY)],
        out_specs=[pl.BlockSpec(memory_space=pl.ANY), kvb, kvb, gain, gain],
        out_shape=[jax.ShapeDtypeStruct(dproj.shape, dproj.dtype), jax.ShapeDtypeStruct((B * MEM_LEN, MEM_W), f32),
                   jax.ShapeDtypeStruct((B * MEM_LEN, MEM_W), f32), jax.ShapeDtypeStruct((1, MEM_HD), f32),
                   jax.ShapeDtypeStruct((1, MEM_HD), f32)],
        scratch_shapes=[pltpu.VMEM((MEM_LEN, MEM_HD), f32), pltpu.VMEM((MEM_LEN, MEM_HD), f32), pltpu.VMEM((1, MEM_HD), f32),
                        pltpu.VMEM((2, S, MEM_HD), bf16), pltpu.SemaphoreType.DMA((2,))],
        input_output_aliases={7: 0},
        compiler_params=_params(("arbitrary", "arbitrary")), name="mem_bwd",
    )(proj, proj, mkv, mkv, gq, gk, dmo, dproj)


def _merge_fwd(proj, ag, cg, mg, wa, wc, wm):
    T, NC = proj.shape
    D = wa.shape[1]
    tm, tn = _pick(T, 1024), _pick(D, 512)
    assert GT % tn == 0

    def body(ag_ref, cg_ref, mg_ref, wa_ref, wc_ref, wm_ref, g0_ref, g1_ref, g2_ref, mer_ref, ba_ref, bc_ref, bm_ref):
        ba = _dot(ag_ref[...], wa_ref[...], NN_DIMS)
        bc = _dot(cg_ref[...], wc_ref[...], NN_DIMS)
        bm = _dot(mg_ref[...], wm_ref[...], NN_DIMS)
        mer_ref[...] = (_sigmoid(g0_ref[...]) * ba + _sigmoid(g1_ref[...]) * bc + _sigmoid(g2_ref[...]) * bm).astype(bf16)
        ba_ref[...] = ba.astype(bf16)
        bc_ref[...] = bc.astype(bf16)
        bm_ref[...] = bm.astype(bf16)

    def act(w):
        return pl.BlockSpec((tm, w), lambda i, j: (i, 0))

    def wt(k):
        return pl.BlockSpec((k, tn), lambda i, j: (0, j))

    def gate(n):
        return pl.BlockSpec((tm, tn), lambda i, j: (i, (GT + n * D) // tn + j))

    out = pl.BlockSpec((tm, tn), lambda i, j: (i, j))
    return pl.pallas_call(
        body, grid=(T // tm, D // tn),
        in_specs=[act(ATTN_OUT), act(CONV_W), act(MEM_W), wt(ATTN_OUT), wt(CONV_W), wt(MEM_W), gate(0), gate(1), gate(2)],
        out_specs=[out] * 4, out_shape=[jax.ShapeDtypeStruct((T, D), bf16)] * 4,
        compiler_params=_params(("parallel", "parallel")), name="merge_fwd",
    )(ag, cg, mg, wa, wc, wm, proj, proj, proj)


def _out_loss(merged, w_out, x, tgt):
    T, D = x.shape
    tm, tn = _pick(T, 512), _pick(D, 512)

    def body(m_ref, w_ref, x_ref, t_ref, dy_ref, dyb_ref, lp_ref):
        e = x_ref[...] + _dot(m_ref[...], w_ref[...], NN_DIMS) - t_ref[...]
        dy = e / D
        dy_ref[...] = dy
        dyb_ref[...] = dy.astype(bf16)
        part = jnp.full((1, 8, LANES), jnp.sum(e * e), f32)

        @pl.when(pl.program_id(1) == 0)
        def _():
            lp_ref[...] = part

        @pl.when(pl.program_id(1) > 0)
        def _():
            lp_ref[...] += part

    tile = pl.BlockSpec((tm, tn), lambda i, j: (i, j))
    return pl.pallas_call(
        body, grid=(T // tm, D // tn),
        in_specs=[pl.BlockSpec((tm, D), lambda i, j: (i, 0)), pl.BlockSpec((D, tn), lambda i, j: (0, j)), tile, tile],
        out_specs=[tile, tile, pl.BlockSpec((1, 8, LANES), lambda i, j: (i, 0, 0))],
        out_shape=[jax.ShapeDtypeStruct((T, D), f32), jax.ShapeDtypeStruct((T, D), bf16),
                   jax.ShapeDtypeStruct((T // tm, 8, LANES), f32)],
        compiler_params=_params(("parallel", "arbitrary")), name="out_loss",
    )(merged, w_out, x, tgt)


def _merge_bwd(proj, dyb, w_out, ba, bc, bm):
    T, NC = proj.shape
    D = w_out.shape[0]
    tm, tn = _pick(T, 512), _pick(D, 512)
    assert GT % tn == 0

    def body(dy_ref, w_ref, ba_ref, bc_ref, bm_ref, g0_ref, g1_ref, g2_ref, da_ref, dc_ref, dm_ref, dproj_ref, stage, sem):
        i, j = pl.program_id(0), pl.program_id(1)
        dmer = _dot(dy_ref[...], w_ref[...], NT_DIMS)
        cps = []
        for n, (g_ref, br_ref, o_ref) in enumerate(((g0_ref, ba_ref, da_ref), (g1_ref, bc_ref, dc_ref), (g2_ref, bm_ref, dm_ref))):
            sg = _sigmoid(g_ref[...])
            o_ref[...] = (sg * dmer).astype(bf16)
            stage[n] = (dmer * br_ref[...].astype(f32) * (sg * (1.0 - sg))).astype(bf16)
            cp = pltpu.make_async_copy(stage.at[n], dproj_ref.at[pl.ds(i * tm, tm), pl.ds(GT + n * D + j * tn, tn)], sem.at[n])
            cp.start()
            cps.append(cp)
        for cp in cps:
            cp.wait()

    tile = pl.BlockSpec((tm, tn), lambda i, j: (i, j))

    def gate(n):
        return pl.BlockSpec((tm, tn), lambda i, j: (i, (GT + n * D) // tn + j))

    return pl.pallas_call(
        body, grid=(T // tm, D // tn),
        in_specs=[pl.BlockSpec((tm, D), lambda i, j: (i, 0)), pl.BlockSpec((tn, D), lambda i, j: (j, 0)), tile, tile, tile,
                  gate(0), gate(1), gate(2)],
        out_specs=[tile, tile, tile, pl.BlockSpec(memory_space=pl.ANY)],
        out_shape=[jax.ShapeDtypeStruct((T, D), bf16)] * 3 + [jax.ShapeDtypeStruct((T, NC), bf16)],
        scratch_shapes=[pltpu.VMEM((3, tm, tn), bf16), pltpu.SemaphoreType.DMA((3,))],
        compiler_params=_params(("arbitrary", "arbitrary")), name="merge_bwd",
    )(dyb, w_out, ba, bc, bm, proj, proj, proj)


def _fwd_bwd_head(x2, mem2, t2, proj, B, S, mem_norm_g, attn_q_norm, attn_k_norm, conv_w, mem_q_norm, mem_k_norm,
                  w_kv, w_a, w_c, w_m, w_out):
    D = x2.shape[1]
    mng = mem_norm_g.reshape(1, D)
    mqn, mkn = mem_q_norm.reshape(1, MEM_HD), mem_k_norm.reshape(1, MEM_HD)

    mh = _rms_fwd(mem2, mng, "rms_mem")
    mkv = _mm(mh, w_kv, mode="nn", out_dtype=f32, name="mkv")
    ag, a_comb, lse = _attn_fwd(proj, attn_q_norm, attn_k_norm, B, S)
    cg = _conv_fwd(proj, conv_w, B, S)
    mg = _mem_fwd(proj, mkv, mqn, mkn, B, S)
    merged, ba, bc, bm = _merge_fwd(proj, ag, cg, mg, w_a, w_c, w_m)
    dy, dyb, lp = _out_loss(merged, w_out, x2, t2)
    sq_err = jnp.sum(lp[:, 0, 0])

    g = {}
    g["w_out"] = _mm(merged, dyb, mode="tn", out_dtype=f32, name="dw_out")
    dba, dbc, dbm, dproj = _merge_bwd(proj, dyb, w_out, ba, bc, bm)
    g["w_br_attn"] = _mm(ag, dba, mode="tn", out_dtype=f32, name="dw_br_attn")
    g["w_br_conv"] = _mm(cg, dbc, mode="tn", out_dtype=f32, name="dw_br_conv")
    g["w_br_mem"] = _mm(mg, dbm, mode="tn", out_dtype=f32, name="dw_br_mem")
    dag = _mm(dba, w_a, mode="nt", out_dtype=f32, name="d_attn_out")
    dcg = _mm(dbc, w_c, mode="nt", out_dtype=f32, name="d_conv_out")
    dmg = _mm(dbm, w_m, mode="nt", out_dtype=f32, name="d_mem_out")
    dproj, g["attn_q_norm"], g["attn_k_norm"] = _attn_bwd(proj, attn_q_norm, attn_k_norm, a_comb, lse, dag, dproj, B, S)
    dproj, g["conv_w"] = _conv_bwd(proj, conv_w, dcg, dproj, B, S)
    dproj, dmk, dmv, dgmq, dgmk = _mem_bwd(proj, mkv, mqn, mkn, dmg, dproj, B, S)
    g["mem_q_norm"], g["mem_k_norm"] = dgmq.reshape(MEM_HD), dgmk.reshape(MEM_HD)
    dmkv = jnp.concatenate([dmk, dmv], axis=1)
    dmh = _mm(dmkv, w_kv, mode="nt", out_dtype=f32, name="d_mem_h")
    g["mem_norm_g"] = _rms_bwd(mem2, mng, dmh, None, "rms_mem_bwd").reshape(D)
    return sq_err, g, dict(dy=dy, dproj=dproj, mh=mh, dmkv=dmkv)


MESH = pl.DeviceIdType.MESH
HBM = pl.BlockSpec(memory_space=pl.ANY)


def _me():
    x, y, c = lax.axis_index("x"), lax.axis_index("y"), lax.axis_index("c")
    return (x, y, c), 4 * x + 2 * y + c


def _peer(k):
    (x, y, c), _ = _me()
    p = (1 - x if k & 4 else x, 1 - y if k & 2 else y, 1 - c if k & 1 else c)
    return p, 4 * p[0] + 2 * p[1] + p[2]


def _window(ref, axis, size, idx):
    if axis is None:
        return ref.at[idx]
    if axis == 0:
        return ref.at[pl.ds(idx * size, size), :]
    return ref.at[:, pl.ds(idx * size, size)]


def _full_shape(s, axis):
    if axis is None:
        return (N_DEV,) + s.shape
    return tuple(N_DEV * d if i == axis else d for i, d in enumerate(s.shape))


OTHER_CHIPS = (4, 2, 6)


def _spread_comm(shards, axes):
    n = len(shards)

    def copies(ins, outs, send_sems, recv_sems, base=0):
        _, me = _me()
        out = []
        for a in range(n):
            mine = _window(outs[a], axes[a], shards[a].shape[axes[a]], me)
            out.append(pltpu.make_async_copy(ins[a], mine, send_sems.at[base + a, N_CHIP]))
            for k, dist in enumerate((1,) + OTHER_CHIPS):
                dev, _ = _peer(dist)
                out.append(pltpu.make_async_remote_copy(
                    src_ref=ins[a], dst_ref=mine, send_sem=send_sems.at[base + a, k], recv_sem=recv_sems.at[base + a, k],
                    device_id=dev, device_id_type=MESH))
        return out

    shapes = [jax.ShapeDtypeStruct(_full_shape(s, ax), s.dtype) for s, ax in zip(shards, axes)]
    return _Comm(shards, shapes, (n, N_CHIP + 1), copies)


def _forward_blocks(fulls, axes):
    n = len(fulls)
    sizes = [f.shape[ax] // N_DEV for f, ax in zip(fulls, axes)]

    def body(*refs):
        outs = refs[n:2 * n]
        send_sems, recv_sems = refs[2 * n:]
        sibling, _ = _peer(1)
        copies = []
        for j, dist in enumerate(OTHER_CHIPS):
            _, held = _peer(dist)
            for a in range(n):
                block = _window(outs[a], axes[a], sizes[a], held)
                copies.append(pltpu.make_async_remote_copy(
                    src_ref=block, dst_ref=block, send_sem=send_sems.at[a, j], recv_sem=recv_sems.at[a, j],
                    device_id=sibling, device_id_type=MESH))
        for cp in copies:
            cp.start()
        for cp in copies:
            cp.wait()

    return pl.pallas_call(
        body, in_specs=[HBM] * n, out_specs=[HBM] * n,
        out_shape=[jax.ShapeDtypeStruct(f.shape, f.dtype) for f in fulls],
        scratch_shapes=[pltpu.SemaphoreType.DMA((n, len(OTHER_CHIPS))), pltpu.SemaphoreType.DMA((n, len(OTHER_CHIPS)))],
        input_output_aliases={a: a for a in range(n)},
        name="forward_blocks",
    )(*fulls)


def _gather_weights(shards, axes):
    n = len(shards)
    full_shape = _full_shape

    def body(*refs):
        ins, outs = refs[:n], refs[n:2 * n]
        send_sems, recv_sems, local_sems = refs[2 * n:]
        (x, y, c), me = _me()
        sibling, sib_id = _peer(1)
        chips = [_peer(4), _peer(2), _peer(6)]

        def win(a, idx):
            return _window(outs[a], axes[a], shards[a].shape[axes[a]] if axes[a] is not None else None, idx)

        def copy(a, k, block, to, src=None):
            return pltpu.make_async_remote_copy(
                src_ref=win(a, block) if src is None else src, dst_ref=win(a, block),
                send_sem=send_sems.at[a, k], recv_sem=recv_sems.at[a, k], device_id=to, device_id_type=MESH)

        mine = [pltpu.make_async_copy(ins[a], win(a, me), local_sems.at[a]) for a in range(n)]
        for cp in mine:
            cp.start()
        first = []
        for a in range(n):
            first.append(copy(a, 0, me, sibling, src=ins[a]))
            first += [copy(a, 1 + j, me, dev, src=ins[a]) for j, (dev, _) in enumerate(chips)]
        for cp in first:
            cp.start()
        passed = []
        for j, (dev, dev_id) in enumerate(chips):
            for a in range(n):
                copy(a, 1 + j, dev_id, (x, y, c)).wait_recv()
                cp = copy(a, 4 + j, dev_id, sibling)
                cp.start()
                passed.append(cp)
        for a in range(n):
            copy(a, 0, sib_id, (x, y, c)).wait_recv()
        for j, (dev, dev_id) in enumerate(chips):
            for a in range(n):
                copy(a, 4 + j, dev_id + 1 - 2 * c, (x, y, c)).wait_recv()
        for cp in first + passed:
            cp.wait_send()
        for cp in mine:
            cp.wait()

    return pl.pallas_call(
        body, in_specs=[HBM] * n, out_specs=[HBM] * n,
        out_shape=[jax.ShapeDtypeStruct(full_shape(s, ax), s.dtype) for s, ax in zip(shards, axes)],
        scratch_shapes=[pltpu.SemaphoreType.DMA((n, 7)), pltpu.SemaphoreType.DMA((n, 7)), pltpu.SemaphoreType.DMA((n,))],
        name="gather_weights",
    )(*shards)


N_CHIP = 4


def _shard_shape(g, ax):
    return tuple(d // N_DEV if i == ax else d for i, d in enumerate(g.shape))


def _sibling_comm(grads, axes):
    n = len(grads)
    sizes = [g.shape[ax] // N_DEV for g, ax in zip(grads, axes)]

    def copies(ins, lands, send_sems, recv_sems, base=0):
        sibling, _ = _peer(1)
        out = []
        for j in range(N_CHIP):
            _, owner = _peer(2 * j + 1)
            for a in range(n):
                out.append(pltpu.make_async_remote_copy(
                    src_ref=_window(ins[a], axes[a], sizes[a], owner), dst_ref=lands[a].at[j],
                    send_sem=send_sems.at[base + a, j], recv_sem=recv_sems.at[base + a, j], device_id=sibling,
                    device_id_type=MESH))
        return out

    shapes = [jax.ShapeDtypeStruct((N_CHIP,) + _shard_shape(g, ax), g.dtype) for g, ax in zip(grads, axes)]
    return _Comm(grads, shapes, (n, N_CHIP), copies)


def _chips_comm(sums):
    n = len(sums)

    def copies(ins, lands, send_sems, recv_sems, base=0):
        out = []
        for j in range(1, N_CHIP):
            dev, _ = _peer(2 * j)
            for a in range(n):
                out.append(pltpu.make_async_remote_copy(
                    src_ref=ins[a].at[j - 1], dst_ref=lands[a].at[j - 1],
                    send_sem=send_sems.at[base + a, j - 1], recv_sem=recv_sems.at[base + a, j - 1], device_id=dev,
                    device_id_type=MESH))
        return out

    return _Comm(sums, [jax.ShapeDtypeStruct(s.shape, s.dtype) for s in sums], (n, N_CHIP), copies)


def _join(c1, c2):
    n1, m1, k1 = len(c1.ins), len(c1.out_shapes), c1.sem_shape[0]

    def copies(ins, lands, send_sems, recv_sems):
        return (c1.copies(ins[:n1], lands[:m1], send_sems, recv_sems, base=0)
                + c2.copies(ins[n1:], lands[m1:], send_sems, recv_sems, base=k1))

    return _Comm(c1.ins + c2.ins, c1.out_shapes + c2.out_shapes, (k1 + c2.sem_shape[0], N_CHIP), copies)


def _rs_chip_sum(grad, land, xyc, axis, name):
    R, C = land.shape[1:]
    tr = _pick(R, max(8, (256 * 1024) // C), 8)

    def body(xyc_ref, g_ref, l_ref, o_ref):
        o_ref[0] = (g_ref[...] + l_ref[0]).astype(bf16)

    def owner(j, xyc_ref):
        jj = j + 1
        return 4 * (xyc_ref[0] ^ (jj >> 1)) + 2 * (xyc_ref[1] ^ (jj & 1)) + xyc_ref[2]

    if axis == 0:
        g_spec = pl.BlockSpec((tr, C), lambda j, i, xyc_ref: (owner(j, xyc_ref) * (R // tr) + i, 0))
    else:
        g_spec = pl.BlockSpec((tr, C), lambda j, i, xyc_ref: (i, owner(j, xyc_ref)))
    return pl.pallas_call(
        body,
        grid_spec=pltpu.PrefetchScalarGridSpec(
            num_scalar_prefetch=1, grid=(N_CHIP - 1, R // tr),
            in_specs=[g_spec, pl.BlockSpec((1, tr, C), lambda j, i, xyc_ref: (j + 1, i, 0))],
            out_specs=pl.BlockSpec((1, tr, C), lambda j, i, xyc_ref: (j, i, 0))),
        out_shape=jax.ShapeDtypeStruct((N_CHIP - 1, R, C), bf16),
        compiler_params=_params(("parallel", "parallel")), name=name,
    )(xyc, grad, land)


def _allreduce_small(part):
    R = part.shape[0]

    def body(p_ref, o_ref, buf, send_sems, recv_sems):
        (x, y, c), me = _me()
        buf[me] = p_ref[...]
        copies = []
        for k in range(1, N_DEV):
            dev, dev_id = _peer(k)
            copies.append(pltpu.make_async_remote_copy(
                src_ref=p_ref, dst_ref=buf.at[me], send_sem=send_sems.at[k - 1], recv_sem=recv_sems.at[k - 1],
                device_id=dev, device_id_type=MESH))
        for cp in copies:
            cp.start()
        for k in range(1, N_DEV):
            _, dev_id = _peer(k)
            pltpu.make_async_remote_copy(
                src_ref=p_ref, dst_ref=buf.at[dev_id], send_sem=send_sems.at[k - 1], recv_sem=recv_sems.at[k - 1],
                device_id=(x, y, c), device_id_type=MESH).wait_recv()
        for cp in copies:
            cp.wait_send()
        acc = buf[0]
        for d in range(1, N_DEV):
            acc = acc + buf[d]
        o_ref[...] = acc

    return pl.pallas_call(
        body, in_specs=[pl.BlockSpec(memory_space=pltpu.VMEM)], out_specs=pl.BlockSpec(memory_space=pltpu.VMEM),
        out_shape=jax.ShapeDtypeStruct((R, LANES), f32),
        scratch_shapes=[pltpu.VMEM((N_DEV, R, LANES), f32), pltpu.SemaphoreType.DMA((N_DEV - 1,)), pltpu.SemaphoreType.DMA((N_DEV - 1,))],
        name="allreduce_small",
    )(part)


def _adamw_math(w, g, m, v):
    m2 = ADAM_B1 * m + (1.0 - ADAM_B1) * g
    v2 = ADAM_B2 * v + (1.0 - ADAM_B2) * (g * g)
    m_hat = m2 / (1.0 - ADAM_B1 ** ADAM_STEP)
    v_hat = v2 / (1.0 - ADAM_B2 ** ADAM_STEP)
    return -ADAM_LR * (m_hat / (jnp.sqrt(v_hat) + ADAM_EPS) + ADAM_WD * w), m2, v2


def _adamw_shard(grad, land_sib, land_chips, w, m, v, me, axis, name, row0=0, prev=None):
    R, C = land_sib.shape[1:]
    assert axis == 1 or R == w.shape[0]
    tr = _pick(R, max(8, (128 * 1024) // C), 8)
    r0 = row0 // tr
    n_prev = len(prev) if prev else 0

    def body(me_ref, g_ref, s_ref, l_ref, w_ref, m_ref, v_ref, *rest):
        go_ref, d_ref, mo_ref, vo_ref = rest[n_prev:]
        g = g_ref[...] + s_ref[0]
        for j in range(N_CHIP - 1):
            g = g + l_ref[j].astype(f32)
        d, m2, v2 = _adamw_math(w_ref[...], g, m_ref[...], v_ref[...])
        go_ref[...] = g
        d_ref[...] = d
        mo_ref[...] = m2
        vo_ref[...] = v2

    if axis == 0:
        g_spec = pl.BlockSpec((tr, C), lambda i, me_ref: (me_ref[0] * (R // tr) + i, 0))
    else:
        g_spec = pl.BlockSpec((tr, C), lambda i, me_ref: (i, me_ref[0]))
    blk = pl.BlockSpec((tr, C), lambda i, me_ref: (r0 + i, 0))
    return pl.pallas_call(
        body,
        grid_spec=pltpu.PrefetchScalarGridSpec(
            num_scalar_prefetch=1, grid=(R // tr,),
            in_specs=[g_spec, pl.BlockSpec((1, tr, C), lambda i, me_ref: (0, i, 0)),
                      pl.BlockSpec((N_CHIP - 1, tr, C), lambda i, me_ref: (0, i, 0)), blk, blk, blk]
            + [pl.BlockSpec(memory_space=pl.ANY)] * n_prev,
            out_specs=[blk] * 4),
        out_shape=[jax.ShapeDtypeStruct(w.shape, f32)] * 4,
        input_output_aliases={7 + i: i for i in range(n_prev)},
        compiler_params=_params(("parallel",)), name=name,
    )(me, grad, land_sib, land_chips, w, m, v, *(prev or []))


def _adamw_small(g, w, m, v):
    def body(g_ref, w_ref, m_ref, v_ref, d_ref, mo_ref, vo_ref):
        d, m2, v2 = _adamw_math(w_ref[...], g_ref[...], m_ref[...], v_ref[...])
        d_ref[...] = d
        mo_ref[...] = m2
        vo_ref[...] = v2

    return pl.pallas_call(body, out_shape=[jax.ShapeDtypeStruct(g.shape, f32)] * 3, name="adamw_small")(g, w, m, v)


def _pack_rows(parts, total_rows):
    rows = jnp.concatenate([p.reshape(-1, LANES) for p in parts], axis=0)
    return jnp.pad(rows, ((0, total_rows - rows.shape[0]), (0, 0)))


BIG = ("w_in", "mem_w_kv", "w_br_attn", "w_br_conv", "w_br_mem", "w_out")
BIG_AXIS = {"w_in": 1, "mem_w_kv": 0, "w_br_attn": 1, "w_br_conv": 1, "w_br_mem": 1, "w_out": 0}
SMALL = ("norm_g", "mem_norm_g", "attn_q_norm", "attn_k_norm", "mem_q_norm", "mem_k_norm")
ALL_W = ("norm_g", "mem_norm_g", "w_in", "attn_q_norm", "attn_k_norm", "conv_w", "mem_w_kv", "mem_q_norm", "mem_k_norm",
         "w_br_attn", "w_br_conv", "w_br_mem", "w_out")


def kernel(x, mem, norm_g, mem_norm_g, w_in, attn_q_norm, attn_k_norm, conv_w, mem_w_kv, mem_q_norm, mem_k_norm, w_br_attn, w_br_conv, w_br_mem, w_out, loss_target, m_norm_g, m_mem_norm_g, m_w_in, m_attn_q_norm, m_attn_k_norm, m_conv_w, m_mem_w_kv, m_mem_q_norm, m_mem_k_norm, m_w_br_attn, m_w_br_conv, m_w_br_mem, m_w_out, v_norm_g, v_mem_norm_g, v_w_in, v_attn_q_norm, v_attn_k_norm, v_conv_w, v_mem_w_kv, v_mem_q_norm, v_mem_k_norm, v_w_br_attn, v_w_br_conv, v_w_br_mem, v_w_out):
    w = dict(norm_g=norm_g, mem_norm_g=mem_norm_g, w_in=w_in, attn_q_norm=attn_q_norm, attn_k_norm=attn_k_norm, conv_w=conv_w,
             mem_w_kv=mem_w_kv, mem_q_norm=mem_q_norm, mem_k_norm=mem_k_norm, w_br_attn=w_br_attn, w_br_conv=w_br_conv,
             w_br_mem=w_br_mem, w_out=w_out)
    mo = dict(norm_g=m_norm_g, mem_norm_g=m_mem_norm_g, w_in=m_w_in, attn_q_norm=m_attn_q_norm, attn_k_norm=m_attn_k_norm,
              conv_w=m_conv_w, mem_w_kv=m_mem_w_kv, mem_q_norm=m_mem_q_norm, mem_k_norm=m_mem_k_norm, w_br_attn=m_w_br_attn,
              w_br_conv=m_w_br_conv, w_br_mem=m_w_br_mem, w_out=m_w_out)
    vo = dict(norm_g=v_norm_g, mem_norm_g=v_mem_norm_g, w_in=v_w_in, attn_q_norm=v_attn_q_norm, attn_k_norm=v_attn_k_norm,
              conv_w=v_conv_w, mem_w_kv=v_mem_w_kv, mem_q_norm=v_mem_q_norm, mem_k_norm=v_mem_k_norm, w_br_attn=v_w_br_attn,
              w_br_conv=v_w_br_conv, w_br_mem=v_w_br_mem, w_out=v_w_out)
    B, S, D = x.shape
    me = (4 * lax.axis_index("x") + 2 * lax.axis_index("y") + lax.axis_index("c")).astype(jnp.int32)

    conv_pad = jnp.pad(conv_w, ((0, 8 - conv_w.shape[0]), (0, 0)))
    w_in_full, conv_g = _gather_weights([w_in.astype(bf16), conv_pad], [1, None])
    conv_full = conv_g[:, :3, :].transpose(1, 0, 2).reshape(3, CONV_W)
    T = B * S
    x2, t2, mem2, ng = x.reshape(T, D), loss_target.reshape(T, D), mem.reshape(B * MEM_LEN, D), norm_g.reshape(1, D)
    h = _rms_fwd(x2, ng, "rms_x")
    later = BIG[1:]
    later_axes = [BIG_AXIS[n] for n in later]
    proj, spread = _mm(h, w_in_full, mode="nn", out_dtype=f32, name="proj",
                       comm=_spread_comm([w[n].astype(bf16) for n in later], later_axes))
    wf = dict(zip(later, _forward_blocks(spread, later_axes)), w_in=w_in_full)

    sq_err, g, t = _fwd_bwd_head(x2, mem2, t2, proj, B, S, mem_norm_g, attn_q_norm, attn_k_norm, conv_full, mem_q_norm,
                                 mem_k_norm, wf["mem_w_kv"], wf["w_br_attn"], wf["w_br_conv"], wf["w_br_mem"], wf["w_out"])
    t.update(x2=x2, ng=ng, h=h)

    xyc = jnp.stack([lax.axis_index("x"), lax.axis_index("y"), lax.axis_index("c")]).astype(jnp.int32)
    me1 = me.reshape(1)
    early = ("w_out", "w_br_attn", "w_br_conv", "w_br_mem")
    g["mem_w_kv"], sib_early = _mm(t["mh"], t["dmkv"], mode="tn", out_dtype=f32, name="dw_kv",
                                   comm=_sibling_comm([g[n] for n in early], [BIG_AXIS[n] for n in early]))
    sums_early = [_rs_chip_sum(g[n], ls, xyc, BIG_AXIS[n], "rs_sum_" + n) for n, ls in zip(early, sib_early)]
    tm_w = _pick(D // 2, 1024)
    half = (D // 2) // tm_w
    g_top, chips_early = _mm(t["h"], t["dproj"], mode="tn", out_dtype=f32, name="dw_in_top", tm=tm_w, m_tiles=(0, half),
                             comm=_chips_comm(sums_early))
    g_bot, (sib_top, sib_kv) = _mm(t["h"], t["dproj"], mode="tn", out_dtype=f32, name="dw_in_bot", tm=tm_w, m_tiles=(half, half),
                                   comm=_sibling_comm([g_top, g["mem_w_kv"]], [1, 0]))
    sum_top = _rs_chip_sum(g_top, sib_top, xyc, 1, "rs_sum_w_in_top")
    sum_kv = _rs_chip_sum(g["mem_w_kv"], sib_kv, xyc, 0, "rs_sum_mem_w_kv")
    T = B * S
    tm_t = _pick(T // 2, 1024)
    halfm = (T // 2) // tm_t
    dh, (chips_top, chips_kv, sib_bot) = _mm(t["dproj"], wf["w_in"], mode="nt", out_dtype=f32, name="d_h_a", tm=tm_t,
                                             m_tiles=(0, halfm), into="new",
                                             comm=_join(_chips_comm([sum_top, sum_kv]), _sibling_comm([g_bot], [1])))
    sum_bot = _rs_chip_sum(g_bot, sib_bot, xyc, 1, "rs_sum_w_in_bot")
    dh, (chips_bot,) = _mm(t["dproj"], wf["w_in"], mode="nt", out_dtype=f32, name="d_h_b", tm=tm_t, m_tiles=(halfm, halfm),
                           into=dh, comm=_chips_comm([sum_bot]))
    dx, dng = _rms_bwd(t["x2"], t["ng"], dh, t["dy"], "rms_x_bwd")
    g["norm_g"] = dng.reshape(D)

    grad, delta, new_m, new_v = {}, {}, {}, {}
    for n, ls, lc in zip(early + ("mem_w_kv",), sib_early + [sib_kv], chips_early + [chips_kv]):
        grad[n], delta[n], new_m[n], new_v[n] = _adamw_shard(g[n], ls, lc, w[n], mo[n], vo[n], me1, BIG_AXIS[n], "adamw_" + n)
    top = _adamw_shard(g_top, sib_top, chips_top, w_in, m_w_in, v_w_in, me1, 1, "adamw_w_in_top")
    grad["w_in"], delta["w_in"], new_m["w_in"], new_v["w_in"] = _adamw_shard(
        g_bot, sib_bot, chips_bot, w_in, m_w_in, v_w_in, me1, 1, "adamw_w_in_bot", row0=D // 2, prev=top)

    n_rep = sum(w[n].size for n in SMALL) // LANES
    conv_rows = g["conv_w"].reshape(3, N_DEV, LANES).transpose(1, 0, 2).reshape(3 * N_DEV, LANES)
    n_rows = -(-(n_rep + 3 * N_DEV + 1) // 8) * 8
    part = _pack_rows([g[n] for n in SMALL] + [conv_rows, jnp.full((1, LANES), sq_err, f32)], n_rows)
    tot = _allreduce_small(part)
    loss = tot[n_rep + 3 * N_DEV, 0] * (0.5 / D)
    n_small = -(-(n_rep + 3) // 8) * 8
    g_small = _pack_rows([tot[:n_rep], lax.dynamic_slice(tot, (n_rep + 3 * me, 0), (3, LANES))], n_small)
    names = SMALL + ("conv_w",)
    d_s, m_s, v_s = _adamw_small(g_small, _pack_rows([w[n] for n in names], n_small), _pack_rows([mo[n] for n in names], n_small),
                                 _pack_rows([vo[n] for n in names], n_small))
    off = 0
    for n in names:
        r = w[n].size // LANES
        grad[n], delta[n] = g_small[off:off + r].reshape(w[n].shape), d_s[off:off + r].reshape(w[n].shape)
        new_m[n], new_v[n] = m_s[off:off + r].reshape(w[n].shape), v_s[off:off + r].reshape(w[n].shape)
        off += r
    return (loss, dx.reshape(B, S, D), *[grad[n] for n in ALL_W], *[delta[n] for n in ALL_W], *[new_m[n] for n in ALL_W],
            *[new_v[n] for n in ALL_W])
```

```python
import functools

import jax
import jax.numpy as jnp
from jax import lax
from jax.experimental import pallas as pl
from jax.experimental.pallas import tpu as pltpu

f32 = jnp.float32
bf16 = jnp.bfloat16

N_DEV = 8
HEAD_DIM = 128
DILATIONS = (1, 4, 16)
N_GROUPS = 3
HEADS = 4
BLK = 128
ATTN_QKV = N_GROUPS * HEADS * HEAD_DIM
ATTN_OUT = HEADS * HEAD_DIM
CONV_W = 1024
MEM_LEN = 256
MEM_HEADS = 4
MEM_HD = 256
MEM_W = MEM_HEADS * MEM_HD
EPS = 1e-6
Q0, K0, V0 = 0, ATTN_QKV, 2 * ATTN_QKV
ZA = 3 * ATTN_QKV
CB, CC, CV, ZC = ZA + ATTN_OUT, ZA + ATTN_OUT + CONV_W, ZA + ATTN_OUT + 2 * CONV_W, ZA + ATTN_OUT + 3 * CONV_W
MQ = ZC + CONV_W
ZM = MQ + MEM_W
GT = ZM + MEM_W

ADAM_LR, ADAM_B1, ADAM_B2, ADAM_EPS, ADAM_WD, ADAM_STEP = 0.001, 0.9, 0.999, 1e-08, 0.01, 10

VMEM_LIMIT = 56 * 1024 * 1024
LANES = 128

NT_DIMS = (((1,), (1,)), ((), ()))
TN_DIMS = (((0,), (0,)), ((), ()))
NN_DIMS = (((1,), (0,)), ((), ()))


def _params(sem=None):
    return pltpu.CompilerParams(dimension_semantics=sem, vmem_limit_bytes=VMEM_LIMIT)


def _pick(n, pref, q=LANES):
    t = (min(pref, n) // q) * q
    while t >= q:
        if n % t == 0:
            return t
        t -= q
    return n


def _dot(a, b, dims):
    return lax.dot_general(a.astype(bf16), b.astype(bf16), dims, preferred_element_type=f32)


def _sigmoid(z):
    return 0.5 * jnp.tanh(0.5 * z) + 0.5


def _rstd(v):
    return lax.rsqrt(jnp.mean(v * v, axis=-1, keepdims=True) + EPS)


def _row_sum(v):
    hi = v.astype(bf16)
    lo = (v - hi.astype(f32)).astype(bf16)
    ones = jnp.ones((v.shape[1], v.shape[1]), bf16)
    return _dot(hi, ones, NN_DIMS) + _dot(lo, ones, NN_DIMS)


def _rstd_head(v):
    return lax.rsqrt(_row_sum(v * v) * (1.0 / v.shape[1]) + EPS)


class _Comm:
    def __init__(self, ins, out_shapes, sem_shape, copies):
        self.ins, self.out_shapes, self.sem_shape, self.copies = list(ins), list(out_shapes), sem_shape, copies

    def scratch(self):
        return [pltpu.SemaphoreType.DMA(self.sem_shape), pltpu.SemaphoreType.DMA(self.sem_shape)]


def _mm(a, b, *, mode, out_dtype, name, tm=1024, tn=1024, tk=4096, m_tiles=None, into=None, comm=None):
    if mode == "nn":
        (M, K), (K2, N) = a.shape, b.shape
    elif mode == "nt":
        (M, K), (N, K2) = a.shape, b.shape
    else:
        (K, M), (K2, N) = a.shape, b.shape
    assert K == K2
    tm, tn, tk = _pick(M, tm), _pick(N, tn), _pick(K, tk)
    nk = K // tk
    m0, mc = m_tiles if m_tiles is not None else (0, M // tm)
    o0 = 0 if into is None else m0
    aliased = into is not None and not isinstance(into, str)
    n_ci = len(comm.ins) if comm else 0
    n_co = len(comm.out_shapes) if comm else 0
    grid = (mc, N // tn, nk)
    dims = {"nn": NN_DIMS, "nt": NT_DIMS, "tn": TN_DIMS}[mode]

    def body(*refs, nk):
        a_ref, b_ref = refs[:2]
        pos = 3 if aliased else 2
        c_ins, o_ref, c_outs = refs[pos:pos + n_ci], refs[pos + n_ci], refs[pos + n_ci + 1:pos + n_ci + 1 + n_co]
        scratch = refs[pos + n_ci + 1 + n_co:]
        ids = [pl.program_id(d) for d in range(3)]
        if comm:
            sems = scratch[-2:]

            @pl.when((ids[0] == 0) & (ids[1] == 0) & (ids[2] == 0))
            def _():
                for cp in comm.copies(c_ins, c_outs, *sems):
                    cp.start()

        if nk == 1:
            o_ref[...] = _dot(a_ref[...], b_ref[...], dims).astype(o_ref.dtype)
        else:
            acc_ref, k = scratch[0], ids[2]

            @pl.when(k == 0)
            def _():
                acc_ref[...] = _dot(a_ref[...], b_ref[...], dims)

            @pl.when((k > 0) & (k < nk - 1))
            def _():
                acc_ref[...] += _dot(a_ref[...], b_ref[...], dims)

            @pl.when(k == nk - 1)
            def _():
                o_ref[...] = (acc_ref[...] + _dot(a_ref[...], b_ref[...], dims)).astype(o_ref.dtype)

        if comm:
            @pl.when((ids[0] == grid[0] - 1) & (ids[1] == grid[1] - 1) & (ids[2] == grid[2] - 1))
            def _():
                for cp in comm.copies(c_ins, c_outs, *sems):
                    cp.wait()

    if mode == "tn":
        a_spec = pl.BlockSpec((tk, tm), lambda i, j, k: (k, m0 + i))
    else:
        a_spec = pl.BlockSpec((tm, tk), lambda i, j, k: (m0 + i, k))
    if mode == "nt":
        b_spec = pl.BlockSpec((tn, tk), lambda i, j, k: (j, k))
    else:
        b_spec = pl.BlockSpec((tk, tn), lambda i, j, k: (k, j))
    hbm = pl.BlockSpec(memory_space=pl.ANY)
    res = pl.pallas_call(
        functools.partial(body, nk=nk),
        grid=grid,
        in_specs=[a_spec, b_spec] + [hbm] * (aliased + n_ci),
        out_specs=[pl.BlockSpec((tm, tn), lambda i, j, k: (o0 + i, j))] + [hbm] * n_co,
        out_shape=[jax.ShapeDtypeStruct((mc * tm if into is None else M, N), out_dtype)] + (comm.out_shapes if comm else []),
        scratch_shapes=([pltpu.VMEM((tm, tn), f32)] if nk > 1 else []) + (comm.scratch() if comm else []),
        input_output_aliases={2: 0} if aliased else {},
        compiler_params=_params(("arbitrary",) * 3 if comm else ("parallel", "parallel", "arbitrary")),
        name=name,
    )(a, b, *([into] if aliased else []), *(comm.ins if comm else []))
    return (res[0], list(res[1:])) if comm else res[0]


def _rms_fwd(x, g, name):
    T, D = x.shape
    tm = _pick(T, 256, 8)

    def body(x_ref, g_ref, h_ref):
        xv = x_ref[...]
        h_ref[...] = (xv * _rstd(xv) * g_ref[...]).astype(bf16)

    return pl.pallas_call(
        body, grid=(T // tm,),
        in_specs=[pl.BlockSpec((tm, D), lambda i: (i, 0)), pl.BlockSpec((1, D), lambda i: (0, 0))],
        out_specs=pl.BlockSpec((tm, D), lambda i: (i, 0)),
        out_shape=jax.ShapeDtypeStruct((T, D), bf16),
        compiler_params=_params(("parallel",)), name=name,
    )(x, g)


def _rms_bwd(x, g, dh, dy, name):
    T, D = x.shape
    tm = _pick(T, 256, 8)
    with_dx = dy is not None

    def body(*refs):
        if with_dx:
            x_ref, g_ref, dh_ref, dy_ref, dx_ref, dg_ref = refs
        else:
            x_ref, g_ref, dh_ref, dg_ref = refs
        xv = x_ref[...]
        r = _rstd(xv)
        xh = xv * r
        dhv = dh_ref[...]
        part = jnp.sum(dhv * xh, axis=0, keepdims=True)

        @pl.when(pl.program_id(0) == 0)
        def _():
            dg_ref[...] = part

        @pl.when(pl.program_id(0) > 0)
        def _():
            dg_ref[...] += part

        if with_dx:
            dxh = dhv * g_ref[...]
            dx_ref[...] = dy_ref[...] + r * (dxh - xh * jnp.mean(dxh * xh, axis=-1, keepdims=True))

    row = pl.BlockSpec((tm, D), lambda i: (i, 0))
    vec = pl.BlockSpec((1, D), lambda i: (0, 0))
    if with_dx:
        return pl.pallas_call(
            body, grid=(T // tm,), in_specs=[row, vec, row, row], out_specs=[row, vec],
            out_shape=[jax.ShapeDtypeStruct((T, D), f32), jax.ShapeDtypeStruct((1, D), f32)],
            compiler_params=_params(("arbitrary",)), name=name,
        )(x, g, dh, dy)
    return pl.pallas_call(
        body, grid=(T // tm,), in_specs=[row, vec, row], out_specs=vec,
        out_shape=jax.ShapeDtypeStruct((1, D), f32),
        compiler_params=_params(("arbitrary",)), name=name,
    )(x, g, dh)


UNIT_UNROLL = 4


def _rows(start, size, stride):
    return pl.ds(start, size) if stride == 1 else pl.ds(start, size, stride=stride)


def _attn_units(S, d, first, prev):
    nb = S // d // BLK

    def do_first(r, c):
        first(_rows(r, BLK, d))
        return c

    lax.fori_loop(0, d, do_first, 0, unroll=min(d, UNIT_UNROLL))
    if nb > 1:
        def do_prev(u, c):
            r = u // (nb - 1)
            b = u % (nb - 1) + 1
            base = r + d * b * BLK
            prev(_rows(base, BLK, d), _rows(base - d * BLK, 2 * BLK, d))
            return c

        lax.fori_loop(0, d * (nb - 1), do_prev, 0, unroll=UNIT_UNROLL)


def _band_bias(nkeys):
    i = lax.broadcasted_iota(jnp.int32, (BLK, nkeys), 0)
    j = lax.broadcasted_iota(jnp.int32, (BLK, nkeys), 1)
    ok = (j <= i) if nkeys == BLK else ((j >= i) & (j <= i + BLK))
    return jnp.where(ok, 0.0, -jnp.inf).astype(f32)


def _normalize_into(src_ref, gain, dst_ref, S):
    for c in range(S // 256):
        rows = pl.ds(c * 256, 256)
        v = src_ref[rows, :]
        dst_ref[rows, :] = v * _rstd_head(v) * gain


def _attn_fwd(proj, gq, gk, B, S):
    T, NC = proj.shape
    scale = HEAD_DIM ** -0.5

    def body(q_ref, k_ref, v_ref, z_ref, gq_ref, gk_ref, ag_ref, a_ref, lse_ref, qs, ks, o0, o1, o2, l0, l1, l2, bias1, bias2):
        g = pl.program_id(2)
        bias1[...] = _band_bias(BLK)
        bias2[...] = _band_bias(2 * BLK)
        _normalize_into(q_ref, gq_ref[pl.ds(g, 1), :], qs, S)
        _normalize_into(k_ref, gk_ref[pl.ds(g, 1), :], ks, S)
        o_s, l_s = (o0, o1, o2), (l0, l1, l2)

        def run(gi):
            def unit(qr, kr, nkeys):
                s = _dot(qs[qr, :], ks[kr, :], NT_DIMS) * scale + (bias1 if nkeys == BLK else bias2)[...]
                m = jnp.max(s, axis=-1, keepdims=True)
                p = jnp.exp(s - m)
                den = jnp.sum(p, axis=-1, keepdims=True)
                o_s[gi][qr, :] = _dot(p, v_ref[kr, :], NN_DIMS) * (1.0 / den)
                l_s[gi][qr, :] = jnp.broadcast_to(m + jnp.log(den), (BLK, HEAD_DIM))

            _attn_units(S, DILATIONS[gi], lambda qr: unit(qr, qr, BLK), lambda qr, kr: unit(qr, kr, 2 * BLK))

        for gi in range(N_GROUPS):
            pl.when(g == gi)(functools.partial(run, gi))

        @pl.when(g == N_GROUPS - 1)
        def _():
            for c in range(S // 256):
                rows = pl.ds(c * 256, 256)
                la, lb, lc = l0[rows, :], l1[rows, :], l2[rows, :]
                m = jnp.maximum(jnp.maximum(la, lb), lc)
                wa, wb, wc = jnp.exp(la - m), jnp.exp(lb - m), jnp.exp(lc - m)
                tot = wa + wb + wc
                a = (wa / tot) * o0[rows, :] + (wb / tot) * o1[rows, :] + (wc / tot) * o2[rows, :]
                z = z_ref[rows, :]
                a_ref[rows, :] = a
                lse_ref[rows, :] = m + jnp.log(tot)
                ag_ref[rows, :] = (a * (z * _sigmoid(z))).astype(bf16)

    def slab(col0):
        return pl.BlockSpec((S, HEAD_DIM), lambda b, h, g: (b, col0 // HEAD_DIM + g * HEADS + h))

    gain = pl.BlockSpec((N_GROUPS, HEAD_DIM), lambda b, h, g: (0, 0))
    out = pl.BlockSpec((S, HEAD_DIM), lambda b, h, g: (b, h))
    return pl.pallas_call(
        body, grid=(B, HEADS, N_GROUPS),
        in_specs=[slab(Q0), slab(K0), slab(V0), pl.BlockSpec((S, HEAD_DIM), lambda b, h, g: (b, ZA // HEAD_DIM + h)), gain, gain],
        out_specs=[out, out, out],
        out_shape=[jax.ShapeDtypeStruct((T, ATTN_OUT), bf16), jax.ShapeDtypeStruct((T, ATTN_OUT), f32),
                   jax.ShapeDtypeStruct((T, ATTN_OUT), f32)],
        scratch_shapes=[pltpu.VMEM((S, HEAD_DIM), f32)] * 8 + [pltpu.VMEM((BLK, BLK), f32), pltpu.VMEM((BLK, 2 * BLK), f32)],
        compiler_params=_params(("arbitrary", "arbitrary", "arbitrary")), name="attn_fwd",
    )(proj, proj, proj, proj, gq, gk)


def _rms_bwd_rows(raw, gain, dn):
    r = _rstd_head(raw)
    xh = raw * r
    dxh = dn * gain
    mean = _row_sum(dxh * xh) * (1.0 / raw.shape[1])
    return r * (dxh - xh * mean), jnp.sum(dn * xh, axis=0, keepdims=True)


def _attn_bwd(proj, gq, gk, a_comb, lse, dag, dproj, B, S):
    T, NC = proj.shape
    scale = HEAD_DIM ** -0.5

    def body(q_ref, k_ref, v_ref, z_ref, gq_ref, gk_ref, a_ref, lse_ref, dag_ref, dproj_in, dproj_ref, dgq_ref, dgk_ref,
             qs, ks, da_s, dl_s, dq_s, dk_s, dv_s, bias1, bias2, stage, sem):
        b, h, g = pl.program_id(0), pl.program_id(1), pl.program_id(2)
        bias1[...] = _band_bias(BLK)
        bias2[...] = _band_bias(2 * BLK)
        gq_row, gk_row = gq_ref[pl.ds(g, 1), :], gk_ref[pl.ds(g, 1), :]
        _normalize_into(q_ref, gq_row, qs, S)
        _normalize_into(k_ref, gk_row, ks, S)

        def put(slot, col0):
            cp = pltpu.make_async_copy(stage.at[slot], dproj_ref.at[pl.ds(b * S, S), pl.ds(col0, HEAD_DIM)], sem.at[slot])
            cp.start()
            return cp

        @pl.when((b == 0) & (h == 0) & (g == 0))
        def _():
            dgq_ref[...] = jnp.zeros_like(dgq_ref)
            dgk_ref[...] = jnp.zeros_like(dgk_ref)

        @pl.when(g == 0)
        def _():
            for c in range(S // 256):
                rows = pl.ds(c * 256, 256)
                z, a, dg_ = z_ref[rows, :], a_ref[rows, :], dag_ref[rows, :]
                sg = _sigmoid(z)
                da = dg_ * (z * sg)
                da_s[rows, :] = da
                dl_s[rows, :] = _row_sum(da * a)
                stage[3, rows, :] = (dg_ * a * (sg * (1.0 + z * (1.0 - sg)))).astype(bf16)
            put(3, ZA + h * HEAD_DIM).wait()

        dk_s[...] = jnp.zeros_like(dk_s)
        dv_s[...] = jnp.zeros_like(dv_s)

        def run(gi):
            def unit(qr, kr, nkeys):
                q, k, v = qs[qr, :], ks[kr, :], v_ref[kr, :]
                s = _dot(q, k, NT_DIMS) * scale
                p = jnp.exp(s + (bias1 if nkeys == BLK else bias2)[...] - lse_ref[qr, :][:, :1])
                da = da_s[qr, :]
                dp = _dot(da, v, NT_DIMS)
                ds = p * (dp - dl_s[qr, :][:, :1]) * scale
                dq_s[qr, :] = _dot(ds, k, NN_DIMS)
                dk_s[kr, :] += _dot(ds, q, TN_DIMS)
                dv_s[kr, :] += _dot(p, da, TN_DIMS)

            _attn_units(S, DILATIONS[gi], lambda qr: unit(qr, qr, BLK), lambda qr, kr: unit(qr, kr, 2 * BLK))

        for gi in range(N_GROUPS):
            pl.when(g == gi)(functools.partial(run, gi))

        gq_acc = jnp.zeros((1, HEAD_DIM), f32)
        gk_acc = jnp.zeros((1, HEAD_DIM), f32)
        for c in range(S // 256):
            rows = pl.ds(c * 256, 256)
            dq, gq_p = _rms_bwd_rows(q_ref[rows, :], gq_row, dq_s[rows, :])
            dk, gk_p = _rms_bwd_rows(k_ref[rows, :], gk_row, dk_s[rows, :])
            gq_acc, gk_acc = gq_acc + gq_p, gk_acc + gk_p
            stage[0, rows, :] = dq.astype(bf16)
            stage[1, rows, :] = dk.astype(bf16)
            stage[2, rows, :] = dv_s[rows, :].astype(bf16)
        dgq_ref[pl.ds(g, 1), :] += gq_acc
        dgk_ref[pl.ds(g, 1), :] += gk_acc
        col = (g * HEADS + h) * HEAD_DIM
        cps = [put(0, Q0 + col), put(1, K0 + col), put(2, V0 + col)]
        for cp in cps:
            cp.wait()

    def slab(col0):
        return pl.BlockSpec((S, HEAD_DIM), lambda b, h, g: (b, col0 // HEAD_DIM + g * HEADS + h))

    gain = pl.BlockSpec((N_GROUPS, HEAD_DIM), lambda b, h, g: (0, 0))
    per_slot = pl.BlockSpec((S, HEAD_DIM), lambda b, h, g: (b, h))
    return pl.pallas_call(
        body, grid=(B, HEADS, N_GROUPS),
        in_specs=[slab(Q0), slab(K0), slab(V0), pl.BlockSpec((S, HEAD_DIM), lambda b, h, g: (b, ZA // HEAD_DIM + h)), gain, gain,
                  per_slot, per_slot, per_slot, pl.BlockSpec(memory_space=pl.ANY)],
        out_specs=[pl.BlockSpec(memory_space=pl.ANY), gain, gain],
        out_shape=[jax.ShapeDtypeStruct(dproj.shape, dproj.dtype), jax.ShapeDtypeStruct((N_GROUPS, HEAD_DIM), f32),
                   jax.ShapeDtypeStruct((N_GROUPS, HEAD_DIM), f32)],
        scratch_shapes=[pltpu.VMEM((S, HEAD_DIM), f32)] * 7 + [pltpu.VMEM((BLK, BLK), f32), pltpu.VMEM((BLK, 2 * BLK), f32),
                                                                pltpu.VMEM((4, S, HEAD_DIM), bf16), pltpu.SemaphoreType.DMA((4,))],
        input_output_aliases={9: 0},
        compiler_params=_params(("arbitrary", "arbitrary", "arbitrary")), name="attn_bwd",
    )(proj, proj, proj, proj, gq, gk, a_comb, lse, dag, dproj)


def _conv_fwd(proj, conv_w, B, S):
    T, NC = proj.shape
    tc = LANES

    def body(cb_ref, cc_ref, cv_ref, z_ref, w_ref, c_ref, ub):
        u = cc_ref[...] * cv_ref[...]
        ub[0:8, :] = jnp.zeros((8, tc), f32)
        ub[8:8 + S, :] = u
        w = w_ref[...]
        y = w[0:1] * u + w[1:2] * ub[pl.ds(7, S), :] + w[2:3] * ub[pl.ds(6, S), :]
        z = z_ref[...]
        c_ref[...] = (cb_ref[...] * y * (z * _sigmoid(z))).astype(bf16)

    def seg(col0):
        return pl.BlockSpec((S, tc), lambda j, b: (b, col0 // tc + j))

    return pl.pallas_call(
        body, grid=(CONV_W // tc, B),
        in_specs=[seg(CB), seg(CC), seg(CV), seg(ZC), pl.BlockSpec((3, tc), lambda j, b: (0, j))],
        out_specs=pl.BlockSpec((S, tc), lambda j, b: (b, j)),
        out_shape=jax.ShapeDtypeStruct((T, CONV_W), bf16),
        scratch_shapes=[pltpu.VMEM((S + 8, tc), f32)],
        compiler_params=_params(("parallel", "parallel")), name="conv_fwd",
    )(proj, proj, proj, proj, conv_w)


def _conv_bwd(proj, conv_w, dc, dproj, B, S):
    T, NC = proj.shape
    tc = LANES

    def body(cb_ref, cc_ref, cv_ref, z_ref, w_ref, dc_ref, dproj_in, dproj_ref, dw_ref, ub, db, stage, sem):
        j, b = pl.program_id(0), pl.program_id(1)
        cb, cc, cv, z, dcv = cb_ref[...], cc_ref[...], cv_ref[...], z_ref[...], dc_ref[...]
        u = cc * cv
        ub[0:8, :] = jnp.zeros((8, tc), f32)
        ub[8:8 + S, :] = u
        u1, u2 = ub[pl.ds(7, S), :], ub[pl.ds(6, S), :]
        w = w_ref[...]
        y = w[0:1] * u + w[1:2] * u1 + w[2:3] * u2
        sg = _sigmoid(z)
        si = z * sg
        dyv = dcv * cb * si
        db[0:S, :] = dyv
        db[S:S + 8, :] = jnp.zeros((8, tc), f32)
        du = w[0:1] * dyv + w[1:2] * db[pl.ds(1, S), :] + w[2:3] * db[pl.ds(2, S), :]
        stage[0] = (dcv * y * si).astype(bf16)
        stage[1] = (du * cv).astype(bf16)
        stage[2] = (du * cc).astype(bf16)
        stage[3] = (dcv * cb * y * (sg * (1.0 + z * (1.0 - sg)))).astype(bf16)
        cps = []
        for slot, col0 in enumerate((CB, CC, CV, ZC)):
            cp = pltpu.make_async_copy(stage.at[slot], dproj_ref.at[pl.ds(b * S, S), pl.ds(col0 + j * tc, tc)], sem.at[slot])
            cp.start()
            cps.append(cp)
        part = jnp.concatenate([jnp.sum(dyv * u, axis=0, keepdims=True), jnp.sum(dyv * u1, axis=0, keepdims=True),
                                jnp.sum(dyv * u2, axis=0, keepdims=True)], axis=0)

        @pl.when(b == 0)
        def _():
            dw_ref[...] = part

        @pl.when(b > 0)
        def _():
            dw_ref[...] += part

        for cp in cps:
            cp.wait()

    def seg(col0):
        return pl.BlockSpec((S, tc), lambda j, b: (b, col0 // tc + j))

    return pl.pallas_call(
        body, grid=(CONV_W // tc, B),
        in_specs=[seg(CB), seg(CC), seg(CV), seg(ZC), pl.BlockSpec((3, tc), lambda j, b: (0, j)),
                  pl.BlockSpec((S, tc), lambda j, b: (b, j)), pl.BlockSpec(memory_space=pl.ANY)],
        out_specs=[pl.BlockSpec(memory_space=pl.ANY), pl.BlockSpec((3, tc), lambda j, b: (0, j))],
        out_shape=[jax.ShapeDtypeStruct(dproj.shape, dproj.dtype), jax.ShapeDtypeStruct((3, CONV_W), f32)],
        scratch_shapes=[pltpu.VMEM((S + 8, tc), f32), pltpu.VMEM((S + 8, tc), f32), pltpu.VMEM((4, S, tc), bf16),
                        pltpu.SemaphoreType.DMA((4,))],
        input_output_aliases={6: 0},
        compiler_params=_params(("arbitrary", "arbitrary")), name="conv_bwd",
    )(proj, proj, proj, proj, conv_w, dc, dproj)


MEM_CHUNK = 256


def _mem_fwd(proj, mkv, gq, gk, B, S):
    T, NC = proj.shape
    scale = MEM_HD ** -0.5

    def body(q_ref, z_ref, k_ref, v_ref, gq_ref, gk_ref, o_ref):
        kv = k_ref[...]
        kn = (kv * _rstd_head(kv) * gk_ref[...]).astype(bf16)
        vv = v_ref[...].astype(bf16)

        def chunk(c, carry):
            rows = pl.ds(pl.multiple_of(c * MEM_CHUNK, MEM_CHUNK), MEM_CHUNK)
            q = q_ref[rows, :]
            qn = q * _rstd_head(q) * gq_ref[...]
            s = _dot(qn, kn, NT_DIMS) * scale
            p = jnp.exp(s - jnp.max(s, axis=-1, keepdims=True))
            p = p / _row_sum(p)
            z = z_ref[rows, :]
            o_ref[rows, :] = (_dot(p, vv, NN_DIMS) * (z * _sigmoid(z))).astype(bf16)
            return carry

        lax.fori_loop(0, S // MEM_CHUNK, chunk, 0)

    gain = pl.BlockSpec((1, MEM_HD), lambda b, h: (0, 0))
    return pl.pallas_call(
        body, grid=(B, MEM_HEADS),
        in_specs=[pl.BlockSpec((S, MEM_HD), lambda b, h: (b, MQ // MEM_HD + h)),
                  pl.BlockSpec((S, MEM_HD), lambda b, h: (b, ZM // MEM_HD + h)),
                  pl.BlockSpec((MEM_LEN, MEM_HD), lambda b, h: (b, h)),
                  pl.BlockSpec((MEM_LEN, MEM_HD), lambda b, h: (b, MEM_HEADS + h)), gain, gain],
        out_specs=pl.BlockSpec((S, MEM_HD), lambda b, h: (b, h)),
        out_shape=jax.ShapeDtypeStruct((T, MEM_W), bf16),
        compiler_params=_params(("parallel", "parallel")), name="mem_fwd",
    )(proj, proj, mkv, mkv, gq, gk)


def _mem_bwd(proj, mkv, gq, gk, dmo, dproj, B, S):
    T, NC = proj.shape
    scale = MEM_HD ** -0.5

    def body(q_ref, z_ref, k_ref, v_ref, gq_ref, gk_ref, dmo_ref, dproj_in, dproj_ref, dmk_ref, dmv_ref, dgq_ref, dgk_ref,
             dkn_s, dv_s, gq_s, stage, sem):
        b, h = pl.program_id(0), pl.program_id(1)
        kv = k_ref[...]
        kn = (kv * _rstd_head(kv) * gk_ref[...]).astype(bf16)
        vv = v_ref[...].astype(bf16)
        dkn_s[...] = jnp.zeros_like(dkn_s)
        dv_s[...] = jnp.zeros_like(dv_s)
        gq_s[...] = jnp.zeros_like(gq_s)

        def chunk(c, carry):
            rows = pl.ds(pl.multiple_of(c * MEM_CHUNK, MEM_CHUNK), MEM_CHUNK)
            q = q_ref[rows, :]
            qn = q * _rstd_head(q) * gq_ref[...]
            s = _dot(qn, kn, NT_DIMS) * scale
            p = jnp.exp(s - jnp.max(s, axis=-1, keepdims=True))
            p = p / _row_sum(p)
            mo = _dot(p, vv, NN_DIMS)
            z, dg_ = z_ref[rows, :], dmo_ref[rows, :]
            sg = _sigmoid(z)
            do = dg_ * (z * sg)
            stage[1, rows, :] = (dg_ * mo * (sg * (1.0 + z * (1.0 - sg)))).astype(bf16)
            dp = _dot(do, vv, NT_DIMS)
            ds = p * (dp - _row_sum(do * mo)) * scale
            dq, gq_p = _rms_bwd_rows(q, gq_ref[...], _dot(ds, kn, NN_DIMS))
            stage[0, rows, :] = dq.astype(bf16)
            gq_s[...] += gq_p
            dkn_s[...] += _dot(ds, qn, TN_DIMS)
            dv_s[...] += _dot(p, do, TN_DIMS)
            return carry

        lax.fori_loop(0, S // MEM_CHUNK, chunk, 0)
        cps = []
        for slot, col0 in enumerate((MQ, ZM)):
            cp = pltpu.make_async_copy(stage.at[slot], dproj_ref.at[pl.ds(b * S, S), pl.ds(col0 + h * MEM_HD, MEM_HD)], sem.at[slot])
            cp.start()
            cps.append(cp)
        dk, gk_p = _rms_bwd_rows(kv, gk_ref[...], dkn_s[...])
        dmk_ref[...] = dk
        dmv_ref[...] = dv_s[...]

        @pl.when((b == 0) & (h == 0))
        def _():
            dgq_ref[...] = gq_s[...]
            dgk_ref[...] = gk_p

        @pl.when((b > 0) | (h > 0))
        def _():
            dgq_ref[...] += gq_s[...]
            dgk_ref[...] += gk_p

        for cp in cps:
            cp.wait()

    gain = pl.BlockSpec((1, MEM_HD), lambda b, h: (0, 0))
    kvb = pl.BlockSpec((MEM_LEN, MEM_HD), lambda b, h: (b, h))
    return pl.pallas_call(
        body, grid=(B, MEM_HEADS),
        in_specs=[pl.BlockSpec((S, MEM_HD), lambda b, h: (b, MQ // MEM_HD + h)),
                  pl.BlockSpec((S, MEM_HD), lambda b, h: (b, ZM // MEM_HD + h)),
                  kvb, pl.BlockSpec((MEM_LEN, MEM_HD), lambda b, h: (b, MEM_HEADS + h)), gain, gain,
                  pl.BlockSpec((S, MEM_HD), lambda b, h: (b, h)), pl.BlockSpec(memory_space=pl.ANY)],
        out_specs=[pl.BlockSpec(memory_space=pl.ANY), kvb, kvb, gain, gain],
        out_shape=[jax.ShapeDtypeStruct(dproj.shape, dproj.dtype), jax.ShapeDtypeStruct((B * MEM_LEN, MEM_W), f32),
                   jax.ShapeDtypeStruct((B * MEM_LEN, MEM_W), f32), jax.ShapeDtypeStruct((1, MEM_HD), f32),
                   jax.ShapeDtypeStruct((1, MEM_HD), f32)],
        scratch_shapes=[pltpu.VMEM((MEM_LEN, MEM_HD), f32), pltpu.VMEM((MEM_LEN, MEM_HD), f32), pltpu.VMEM((1, MEM_HD), f32),
                        pltpu.VMEM((2, S, MEM_HD), bf16), pltpu.SemaphoreType.DMA((2,))],
        input_output_aliases={7: 0},
        compiler_params=_params(("arbitrary", "arbitrary")), name="mem_bwd",
    )(proj, proj, mkv, mkv, gq, gk, dmo, dproj)


def _merge_fwd(proj, ag, cg, mg, wa, wc, wm):
    T, NC = proj.shape
    D = wa.shape[1]
    tm, tn = _pick(T, 1024), _pick(D, 512)
    assert GT % tn == 0

    def body(ag_ref, cg_ref, mg_ref, wa_ref, wc_ref, wm_ref, g0_ref, g1_ref, g2_ref, mer_ref, ba_ref, bc_ref, bm_ref):
        ba = _dot(ag_ref[...], wa_ref[...], NN_DIMS)
        bc = _dot(cg_ref[...], wc_ref[...], NN_DIMS)
        bm = _dot(mg_ref[...], wm_ref[...], NN_DIMS)
        mer_ref[...] = (_sigmoid(g0_ref[...]) * ba + _sigmoid(g1_ref[...]) * bc + _sigmoid(g2_ref[...]) * bm).astype(bf16)
        ba_ref[...] = ba.astype(bf16)
        bc_ref[...] = bc.astype(bf16)
        bm_ref[...] = bm.astype(bf16)

    def act(w):
        return pl.BlockSpec((tm, w), lambda i, j: (i, 0))

    def wt(k):
        return pl.BlockSpec((k, tn), lambda i, j: (0, j))

    def gate(n):
        return pl.BlockSpec((tm, tn), lambda i, j: (i, (GT + n * D) // tn + j))

    out = pl.BlockSpec((tm, tn), lambda i, j: (i, j))
    return pl.pallas_call(
        body, grid=(T // tm, D // tn),
        in_specs=[act(ATTN_OUT), act(CONV_W), act(MEM_W), wt(ATTN_OUT), wt(CONV_W), wt(MEM_W), gate(0), gate(1), gate(2)],
        out_specs=[out] * 4, out_shape=[jax.ShapeDtypeStruct((T, D), bf16)] * 4,
        compiler_params=_params(("parallel", "parallel")), name="merge_fwd",
    )(ag, cg, mg, wa, wc, wm, proj, proj, proj)


def _out_loss(merged, w_out, x, tgt):
    T, D = x.shape
    tm, tn = _pick(T, 512), _pick(D, 512)

    def body(m_ref, w_ref, x_ref, t_ref, dy_ref, dyb_ref, lp_ref):
        e = x_ref[...] + _dot(m_ref[...], w_ref[...], NN_DIMS) - t_ref[...]
        dy = e / D
        dy_ref[...] = dy
        dyb_ref[...] = dy.astype(bf16)
        part = jnp.full((1, 8, LANES), jnp.sum(e * e), f32)

        @pl.when(pl.program_id(1) == 0)
        def _():
            lp_ref[...] = part

        @pl.when(pl.program_id(1) > 0)
        def _():
            lp_ref[...] += part

    tile = pl.BlockSpec((tm, tn), lambda i, j: (i, j))
    return pl.pallas_call(
        body, grid=(T // tm, D // tn),
        in_specs=[pl.BlockSpec((tm, D), lambda i, j: (i, 0)), pl.BlockSpec((D, tn), lambda i, j: (0, j)), tile, tile],
        out_specs=[tile, tile, pl.BlockSpec((1, 8, LANES), lambda i, j: (i, 0, 0))],
        out_shape=[jax.ShapeDtypeStruct((T, D), f32), jax.ShapeDtypeStruct((T, D), bf16),
                   jax.ShapeDtypeStruct((T // tm, 8, LANES), f32)],
        compiler_params=_params(("parallel", "arbitrary")), name="out_loss",
    )(merged, w_out, x, tgt)


def _merge_bwd(proj, dyb, w_out, ba, bc, bm):
    T, NC = proj.shape
    D = w_out.shape[0]
    tm, tn = _pick(T, 512), _pick(D, 512)
    assert GT % tn == 0

    def body(dy_ref, w_ref, ba_ref, bc_ref, bm_ref, g0_ref, g1_ref, g2_ref, da_ref, dc_ref, dm_ref, dproj_ref, stage, sem):
        i, j = pl.program_id(0), pl.program_id(1)
        dmer = _dot(dy_ref[...], w_ref[...], NT_DIMS)
        cps = []
        for n, (g_ref, br_ref, o_ref) in enumerate(((g0_ref, ba_ref, da_ref), (g1_ref, bc_ref, dc_ref), (g2_ref, bm_ref, dm_ref))):
            sg = _sigmoid(g_ref[...])
            o_ref[...] = (sg * dmer).astype(bf16)
            stage[n] = (dmer * br_ref[...].astype(f32) * (sg * (1.0 - sg))).astype(bf16)
            cp = pltpu.make_async_copy(stage.at[n], dproj_ref.at[pl.ds(i * tm, tm), pl.ds(GT + n * D + j * tn, tn)], sem.at[n])
            cp.start()
            cps.append(cp)
        for cp in cps:
            cp.wait()

    tile = pl.BlockSpec((tm, tn), lambda i, j: (i, j))

    def gate(n):
        return pl.BlockSpec((tm, tn), lambda i, j: (i, (GT + n * D) // tn + j))

    return pl.pallas_call(
        body, grid=(T // tm, D // tn),
        in_specs=[pl.BlockSpec((tm, D), lambda i, j: (i, 0)), pl.BlockSpec((tn, D), lambda i, j: (j, 0)), tile, tile, tile,
                  gate(0), gate(1), gate(2)],
        out_specs=[tile, tile, tile, pl.BlockSpec(memory_space=pl.ANY)],
        out_shape=[jax.ShapeDtypeStruct((T, D), bf16)] * 3 + [jax.ShapeDtypeStruct((T, NC), bf16)],
        scratch_shapes=[pltpu.VMEM((3, tm, tn), bf16), pltpu.SemaphoreType.DMA((3,))],
        compiler_params=_params(("arbitrary", "arbitrary")), name="merge_bwd",
    )(dyb, w_out, ba, bc, bm, proj, proj, proj)


def _fwd_bwd_head(x2, mem2, t2, proj, B, S, mem_norm_g, attn_q_norm, attn_k_norm, conv_w, mem_q_norm, mem_k_norm,
                  w_kv, w_a, w_c, w_m, w_out):
    D = x2.shape[1]
    mng = mem_norm_g.reshape(1, D)
    mqn, mkn = mem_q_norm.reshape(1, MEM_HD), mem_k_norm.reshape(1, MEM_HD)

    mh = _rms_fwd(mem2, mng, "rms_mem")
    mkv = _mm(mh, w_kv, mode="nn", out_dtype=f32, name="mkv")
    ag, a_comb, lse = _attn_fwd(proj, attn_q_norm, attn_k_norm, B, S)
    cg = _conv_fwd(proj, conv_w, B, S)
    mg = _mem_fwd(proj, mkv, mqn, mkn, B, S)
    merged, ba, bc, bm = _merge_fwd(proj, ag, cg, mg, w_a, w_c, w_m)
    dy, dyb, lp = _out_loss(merged, w_out, x2, t2)
    sq_err = jnp.sum(lp[:, 0, 0])

    g = {}
    g["w_out"] = _mm(merged, dyb, mode="tn", out_dtype=f32, name="dw_out")
    dba, dbc, dbm, dproj = _merge_bwd(proj, dyb, w_out, ba, bc, bm)
    g["w_br_attn"] = _mm(ag, dba, mode="tn", out_dtype=f32, name="dw_br_attn")
    g["w_br_conv"] = _mm(cg, dbc, mode="tn", out_dtype=f32, name="dw_br_conv")
    g["w_br_mem"] = _mm(mg, dbm, mode="tn", out_dtype=f32, name="dw_br_mem")
    dag = _mm(dba, w_a, mode="nt", out_dtype=f32, name="d_attn_out")
    dcg = _mm(dbc, w_c, mode="nt", out_dtype=f32, name="d_conv_out")
    dmg = _mm(dbm, w_m, mode="nt", out_dtype=f32, name="d_mem_out")
    dproj, g["attn_q_norm"], g["attn_k_norm"] = _attn_bwd(proj, attn_q_norm, attn_k_norm, a_comb, lse, dag, dproj, B, S)
    dproj, g["conv_w"] = _conv_bwd(proj, conv_w, dcg, dproj, B, S)
    dproj, dmk, dmv, dgmq, dgmk = _mem_bwd(proj, mkv, mqn, mkn, dmg, dproj, B, S)
    g["mem_q_norm"], g["mem_k_norm"] = dgmq.reshape(MEM_HD), dgmk.reshape(MEM_HD)
    dmkv = jnp.concatenate([dmk, dmv], axis=1)
    dmh = _mm(dmkv, w_kv, mode="nt", out_dtype=f32, name="d_mem_h")
    g["mem_norm_g"] = _rms_bwd(mem2, mng, dmh, None, "rms_mem_bwd").reshape(D)
    return sq_err, g, dict(dy=dy, dproj=dproj, mh=mh, dmkv=dmkv)


MESH = pl.DeviceIdType.MESH
HBM = pl.BlockSpec(memory_space=pl.ANY)


def _me():
    x, y, c = lax.axis_index("x"), lax.axis_index("y"), lax.axis_index("c")
    return (x, y, c), 4 * x + 2 * y + c


def _peer(k):
    (x, y, c), _ = _me()
    p = (1 - x if k & 4 else x, 1 - y if k & 2 else y, 1 - c if k & 1 else c)
    return p, 4 * p[0] + 2 * p[1] + p[2]


def _window(ref, axis, size, idx):
    if axis is None:
        return ref.at[idx]
    if axis == 0:
        return ref.at[pl.ds(idx * size, size), :]
    return ref.at[:, pl.ds(idx * size, size)]


def _full_shape(s, axis):
    if axis is None:
        return (N_DEV,) + s.shape
    return tuple(N_DEV * d if i == axis else d for i, d in enumerate(s.shape))


OTHER_CHIPS = (4, 2, 6)


def _spread_comm(shards, axes):
    n = len(shards)

    def copies(ins, outs, send_sems, recv_sems, base=0):
        _, me = _me()
        out = []
        for a in range(n):
            mine = _window(outs[a], axes[a], shards[a].shape[axes[a]], me)
            out.append(pltpu.make_async_copy(ins[a], mine, send_sems.at[base + a, N_CHIP]))
            for k, dist in enumerate((1,) + OTHER_CHIPS):
                dev, _ = _peer(dist)
                out.append(pltpu.make_async_remote_copy(
                    src_ref=ins[a], dst_ref=mine, send_sem=send_sems.at[base + a, k], recv_sem=recv_sems.at[base + a, k],
                    device_id=dev, device_id_type=MESH))
        return out

    shapes = [jax.ShapeDtypeStruct(_full_shape(s, ax), s.dtype) for s, ax in zip(shards, axes)]
    return _Comm(shards, shapes, (n, N_CHIP + 1), copies)


def _forward_blocks(fulls, axes):
    n = len(fulls)
    sizes = [f.shape[ax] // N_DEV for f, ax in zip(fulls, axes)]

    def body(*refs):
        outs = refs[n:2 * n]
        send_sems, recv_sems = refs[2 * n:]
        sibling, _ = _peer(1)
        copies = []
        for j, dist in enumerate(OTHER_CHIPS):
            _, held = _peer(dist)
            for a in range(n):
                block = _window(outs[a], axes[a], sizes[a], held)
                copies.append(pltpu.make_async_remote_copy(
                    src_ref=block, dst_ref=block, send_sem=send_sems.at[a, j], recv_sem=recv_sems.at[a, j],
                    device_id=sibling, device_id_type=MESH))
        for cp in copies:
            cp.start()
        for cp in copies:
            cp.wait()

    return pl.pallas_call(
        body, in_specs=[HBM] * n, out_specs=[HBM] * n,
        out_shape=[jax.ShapeDtypeStruct(f.shape, f.dtype) for f in fulls],
        scratch_shapes=[pltpu.SemaphoreType.DMA((n, len(OTHER_CHIPS))), pltpu.SemaphoreType.DMA((n, len(OTHER_CHIPS)))],
        input_output_aliases={a: a for a in range(n)},
        name="forward_blocks",
    )(*fulls)


def _gather_weights(shards, axes):
    n = len(shards)
    full_shape = _full_shape

    def body(*refs):
        ins, outs = refs[:n], refs[n:2 * n]
        send_sems, recv_sems, local_sems = refs[2 * n:]
        (x, y, c), me = _me()
        sibling, sib_id = _peer(1)
        chips = [_peer(4), _peer(2), _peer(6)]

        def win(a, idx):
            return _window(outs[a], axes[a], shards[a].shape[axes[a]] if axes[a] is not None else None, idx)

        def copy(a, k, block, to, src=None):
            return pltpu.make_async_remote_copy(
                src_ref=win(a, block) if src is None else src, dst_ref=win(a, block),
                send_sem=send_sems.at[a, k], recv_sem=recv_sems.at[a, k], device_id=to, device_id_type=MESH)

        mine = [pltpu.make_async_copy(ins[a], win(a, me), local_sems.at[a]) for a in range(n)]
        for cp in mine:
            cp.start()
        first = []
        for a in range(n):
            first.append(copy(a, 0, me, sibling, src=ins[a]))
            first += [copy(a, 1 + j, me, dev, src=ins[a]) for j, (dev, _) in enumerate(chips)]
        for cp in first:
            cp.start()
        passed = []
        for j, (dev, dev_id) in enumerate(chips):
            for a in range(n):
                copy(a, 1 + j, dev_id, (x, y, c)).wait_recv()
                cp = copy(a, 4 + j, dev_id, sibling)
                cp.start()
                passed.append(cp)
        for a in range(n):
            copy(a, 0, sib_id, (x, y, c)).wait_recv()
        for j, (dev, dev_id) in enumerate(chips):
            for a in range(n):
                copy(a, 4 + j, dev_id + 1 - 2 * c, (x, y, c)).wait_recv()
        for cp in first + passed:
            cp.wait_send()
        for cp in mine:
            cp.wait()

    return pl.pallas_call(
        body, in_specs=[HBM] * n, out_specs=[HBM] * n,
        out_shape=[jax.ShapeDtypeStruct(full_shape(s, ax), s.dtype) for s, ax in zip(shards, axes)],
        scratch_shapes=[pltpu.SemaphoreType.DMA((n, 7)), pltpu.SemaphoreType.DMA((n, 7)), pltpu.SemaphoreType.DMA((n,))],
        name="gather_weights",
    )(*shards)


N_CHIP = 4


def _shard_shape(g, ax):
    return tuple(d // N_DEV if i == ax else d for i, d in enumerate(g.shape))


def _sibling_comm(grads, axes):
    n = len(grads)
    sizes = [g.shape[ax] // N_DEV for g, ax in zip(grads, axes)]

    def copies(ins, lands, send_sems, recv_sems, base=0):
        sibling, _ = _peer(1)
        out = []
        for j in range(N_CHIP):
            _, owner = _peer(2 * j + 1)
            for a in range(n):
                out.append(pltpu.make_async_remote_copy(
                    src_ref=_window(ins[a], axes[a], sizes[a], owner), dst_ref=lands[a].at[j],
                    send_sem=send_sems.at[base + a, j], recv_sem=recv_sems.at[base + a, j], device_id=sibling,
                    device_id_type=MESH))
        return out

    shapes = [jax.ShapeDtypeStruct((N_CHIP,) + _shard_shape(g, ax), g.dtype) for g, ax in zip(grads, axes)]
    return _Comm(grads, shapes, (n, N_CHIP), copies)


def _chips_comm(sums):
    n = len(sums)

    def copies(ins, lands, send_sems, recv_sems, base=0):
        out = []
        for j in range(1, N_CHIP):
            dev, _ = _peer(2 * j)
            for a in range(n):
                out.append(pltpu.make_async_remote_copy(
                    src_ref=ins[a].at[j - 1], dst_ref=lands[a].at[j - 1],
                    send_sem=send_sems.at[base + a, j - 1], recv_sem=recv_sems.at[base + a, j - 1], device_id=dev,
                    device_id_type=MESH))
        return out

    return _Comm(sums, [jax.ShapeDtypeStruct(s.shape, s.dtype) for s in sums], (n, N_CHIP), copies)


def _join(c1, c2):
    n1, m1, k1 = len(c1.ins), len(c1.out_shapes), c1.sem_shape[0]

    def copies(ins, lands, send_sems, recv_sems):
        return (c1.copies(ins[:n1], lands[:m1], send_sems, recv_sems, base=0)
                + c2.copies(ins[n1:], lands[m1:], send_sems, recv_sems, base=k1))

    return _Comm(c1.ins + c2.ins, c1.out_shapes + c2.out_shapes, (k1 + c2.sem_shape[0], N_CHIP), copies)


def _rs_chip_sum(grad, land, xyc, axis, name):
    R, C = land.shape[1:]
    tr = _pick(R, max(8, (256 * 1024) // C), 8)

    def body(xyc_ref, g_ref, l_ref, o_ref):
        o_ref[0] = (g_ref[...] + l_ref[0]).astype(bf16)

    def owner(j, xyc_ref):
        jj = j + 1
        return 4 * (xyc_ref[0] ^ (jj >> 1)) + 2 * (xyc_ref[1] ^ (jj & 1)) + xyc_ref[2]

    if axis == 0:
        g_spec = pl.BlockSpec((tr, C), lambda j, i, xyc_ref: (owner(j, xyc_ref) * (R // tr) + i, 0))
    else:
        g_spec = pl.BlockSpec((tr, C), lambda j, i, xyc_ref: (i, owner(j, xyc_ref)))
    return pl.pallas_call(
        body,
        grid_spec=pltpu.PrefetchScalarGridSpec(
            num_scalar_prefetch=1, grid=(N_CHIP - 1, R // tr),
            in_specs=[g_spec, pl.BlockSpec((1, tr, C), lambda j, i, xyc_ref: (j + 1, i, 0))],
            out_specs=pl.BlockSpec((1, tr, C), lambda j, i, xyc_ref: (j, i, 0))),
        out_shape=jax.ShapeDtypeStruct((N_CHIP - 1, R, C), bf16),
        compiler_params=_params(("parallel", "parallel")), name=name,
    )(xyc, grad, land)


def _allreduce_small(part):
    R = part.shape[0]

    def body(p_ref, o_ref, buf, send_sems, recv_sems):
        (x, y, c), me = _me()
        buf[me] = p_ref[...]
        copies = []
        for k in range(1, N_DEV):
            dev, dev_id = _peer(k)
            copies.append(pltpu.make_async_remote_copy(
                src_ref=p_ref, dst_ref=buf.at[me], send_sem=send_sems.at[k - 1], recv_sem=recv_sems.at[k - 1],
                device_id=dev, device_id_type=MESH))
        for cp in copies:
            cp.start()
        for k in range(1, N_DEV):
            _, dev_id = _peer(k)
            pltpu.make_async_remote_copy(
                src_ref=p_ref, dst_ref=buf.at[dev_id], send_sem=send_sems.at[k - 1], recv_sem=recv_sems.at[k - 1],
                device_id=(x, y, c), device_id_type=MESH).wait_recv()
        for cp in copies:
            cp.wait_send()
        acc = buf[0]
        for d in range(1, N_DEV):
            acc = acc + buf[d]
        o_ref[...] = acc

    return pl.pallas_call(
        body, in_specs=[pl.BlockSpec(memory_space=pltpu.VMEM)], out_specs=pl.BlockSpec(memory_space=pltpu.VMEM),
        out_shape=jax.ShapeDtypeStruct((R, LANES), f32),
        scratch_shapes=[pltpu.VMEM((N_DEV, R, LANES), f32), pltpu.SemaphoreType.DMA((N_DEV - 1,)), pltpu.SemaphoreType.DMA((N_DEV - 1,))],
        name="allreduce_small",
    )(part)


def _adamw_math(w, g, m, v):
    m2 = ADAM_B1 * m + (1.0 - ADAM_B1) * g
    v2 = ADAM_B2 * v + (1.0 - ADAM_B2) * (g * g)
    m_hat = m2 / (1.0 - ADAM_B1 ** ADAM_STEP)
    v_hat = v2 / (1.0 - ADAM_B2 ** ADAM_STEP)
    return -ADAM_LR * (m_hat / (jnp.sqrt(v_hat) + ADAM_EPS) + ADAM_WD * w), m2, v2


def _adamw_shard(grad, land_sib, land_chips, w, m, v, me, axis, name, row0=0, prev=None):
    R, C = land_sib.shape[1:]
    assert axis == 1 or R == w.shape[0]
    tr = _pick(R, max(8, (128 * 1024) // C), 8)
    r0 = row0 // tr
    n_prev = len(prev) if prev else 0

    def body(me_ref, g_ref, s_ref, l_ref, w_ref, m_ref, v_ref, *rest):
        go_ref, d_ref, mo_ref, vo_ref = rest[n_prev:]
        g = g_ref[...] + s_ref[0]
        for j in range(N_CHIP - 1):
            g = g + l_ref[j].astype(f32)
        d, m2, v2 = _adamw_math(w_ref[...], g, m_ref[...], v_ref[...])
        go_ref[...] = g
        d_ref[...] = d
        mo_ref[...] = m2
        vo_ref[...] = v2

    if axis == 0:
        g_spec = pl.BlockSpec((tr, C), lambda i, me_ref: (me_ref[0] * (R // tr) + i, 0))
    else:
        g_spec = pl.BlockSpec((tr, C), lambda i, me_ref: (i, me_ref[0]))
    blk = pl.BlockSpec((tr, C), lambda i, me_ref: (r0 + i, 0))
    return pl.pallas_call(
        body,
        grid_spec=pltpu.PrefetchScalarGridSpec(
            num_scalar_prefetch=1, grid=(R // tr,),
            in_specs=[g_spec, pl.BlockSpec((1, tr, C), lambda i, me_ref: (0, i, 0)),
                      pl.BlockSpec((N_CHIP - 1, tr, C), lambda i, me_ref: (0, i, 0)), blk, blk, blk]
            + [pl.BlockSpec(memory_space=pl.ANY)] * n_prev,
            out_specs=[blk] * 4),
        out_shape=[jax.ShapeDtypeStruct(w.shape, f32)] * 4,
        input_output_aliases={7 + i: i for i in range(n_prev)},
        compiler_params=_params(("parallel",)), name=name,
    )(me, grad, land_sib, land_chips, w, m, v, *(prev or []))


def _adamw_small(g, w, m, v):
    def body(g_ref, w_ref, m_ref, v_ref, d_ref, mo_ref, vo_ref):
        d, m2, v2 = _adamw_math(w_ref[...], g_ref[...], m_ref[...], v_ref[...])
        d_ref[...] = d
        mo_ref[...] = m2
        vo_ref[...] = v2

    return pl.pallas_call(body, out_shape=[jax.ShapeDtypeStruct(g.shape, f32)] * 3, name="adamw_small")(g, w, m, v)


def _pack_rows(parts, total_rows):
    rows = jnp.concatenate([p.reshape(-1, LANES) for p in parts], axis=0)
    return jnp.pad(rows, ((0, total_rows - rows.shape[0]), (0, 0)))


BIG = ("w_in", "mem_w_kv", "w_br_attn", "w_br_conv", "w_br_mem", "w_out")
BIG_AXIS = {"w_in": 1, "mem_w_kv": 0, "w_br_attn": 1, "w_br_conv": 1, "w_br_mem": 1, "w_out": 0}
SMALL = ("norm_g", "mem_norm_g", "attn_q_norm", "attn_k_norm", "mem_q_norm", "mem_k_norm")
ALL_W = ("norm_g", "mem_norm_g", "w_in", "attn_q_norm", "attn_k_norm", "conv_w", "mem_w_kv", "mem_q_norm", "mem_k_norm",
         "w_br_attn", "w_br_conv", "w_br_mem", "w_out")


def kernel(x, mem, norm_g, mem_norm_g, w_in, attn_q_norm, attn_k_norm, conv_w, mem_w_kv, mem_q_norm, mem_k_norm, w_br_attn, w_br_conv, w_br_mem, w_out, loss_target, m_norm_g, m_mem_norm_g, m_w_in, m_attn_q_norm, m_attn_k_norm, m_conv_w, m_mem_w_kv, m_mem_q_norm, m_mem_k_norm, m_w_br_attn, m_w_br_conv, m_w_br_mem, m_w_out, v_norm_g, v_mem_norm_g, v_w_in, v_attn_q_norm, v_attn_k_norm, v_conv_w, v_mem_w_kv, v_mem_q_norm, v_mem_k_norm, v_w_br_attn, v_w_br_conv, v_w_br_mem, v_w_out):
    w = dict(norm_g=norm_g, mem_norm_g=mem_norm_g, w_in=w_in, attn_q_norm=attn_q_norm, attn_k_norm=attn_k_norm, conv_w=conv_w,
             mem_w_kv=mem_w_kv, mem_q_norm=mem_q_norm, mem_k_norm=mem_k_norm, w_br_attn=w_br_attn, w_br_conv=w_br_conv,
             w_br_mem=w_br_mem, w_out=w_out)
    mo = dict(norm_g=m_norm_g, mem_norm_g=m_mem_norm_g, w_in=m_w_in, attn_q_norm=m_attn_q_norm, attn_k_norm=m_attn_k_norm,
              conv_w=m_conv_w, mem_w_kv=m_mem_w_kv, mem_q_norm=m_mem_q_norm, mem_k_norm=m_mem_k_norm, w_br_attn=m_w_br_attn,
              w_br_conv=m_w_br_conv, w_br_mem=m_w_br_mem, w_out=m_w_out)
    vo = dict(norm_g=v_norm_g, mem_norm_g=v_mem_norm_g, w_in=v_w_in, attn_q_norm=v_attn_q_norm, attn_k_norm=v_attn_k_norm,
              conv_w=v_conv_w, mem_w_kv=v_mem_w_kv, mem_q_norm=v_mem_q_norm, mem_k_norm=v_mem_k_norm, w_br_attn=v_w_br_attn,
              w_br_conv=v_w_br_conv, w_br_mem=v_w_br_mem, w_out=v_w_out)
    B, S, D = x.shape
    me = (4 * lax.axis_index("x") + 2 * lax.axis_index("y") + lax.axis_index("c")).astype(jnp.int32)

    conv_pad = jnp.pad(conv_w, ((0, 8 - conv_w.shape[0]), (0, 0)))
    w_in_full, conv_g = _gather_weights([w_in.astype(bf16), conv_pad], [1, None])
    conv_full = conv_g[:, :3, :].transpose(1, 0, 2).reshape(3, CONV_W)
    T = B * S
    x2, t2, mem2, ng = x.reshape(T, D), loss_target.reshape(T, D), mem.reshape(B * MEM_LEN, D), norm_g.reshape(1, D)
    h = _rms_fwd(x2, ng, "rms_x")
    later = BIG[1:]
    later_axes = [BIG_AXIS[n] for n in later]
    proj, spread = _mm(h, w_in_full, mode="nn", out_dtype=f32, name="proj",
                       comm=_spread_comm([w[n].astype(bf16) for n in later], later_axes))
    wf = dict(zip(later, _forward_blocks(spread, later_axes)), w_in=w_in_full)

    sq_err, g, t = _fwd_bwd_head(x2, mem2, t2, proj, B, S, mem_norm_g, attn_q_norm, attn_k_norm, conv_full, mem_q_norm,
                                 mem_k_norm, wf["mem_w_kv"], wf["w_br_attn"], wf["w_br_conv"], wf["w_br_mem"], wf["w_out"])
    t.update(x2=x2, ng=ng, h=h)

    xyc = jnp.stack([lax.axis_index("x"), lax.axis_index("y"), lax.axis_index("c")]).astype(jnp.int32)
    me1 = me.reshape(1)
    early = ("w_out", "w_br_attn", "w_br_conv", "w_br_mem")
    g["mem_w_kv"], sib_early = _mm(t["mh"], t["dmkv"], mode="tn", out_dtype=f32, name="dw_kv",
                                   comm=_sibling_comm([g[n] for n in early], [BIG_AXIS[n] for n in early]))
    sums_early = [_rs_chip_sum(g[n], ls, xyc, BIG_AXIS[n], "rs_sum_" + n) for n, ls in zip(early, sib_early)]
    tm_w = _pick(D // 2, 1024)
    half = (D // 2) // tm_w
    g_top, chips_early = _mm(t["h"], t["dproj"], mode="tn", out_dtype=f32, name="dw_in_top", tm=tm_w, m_tiles=(0, half),
                             comm=_chips_comm(sums_early))
    g_bot, (sib_top, sib_kv) = _mm(t["h"], t["dproj"], mode="tn", out_dtype=f32, name="dw_in_bot", tm=tm_w, m_tiles=(half, half),
                                   comm=_sibling_comm([g_top, g["mem_w_kv"]], [1, 0]))
    sum_top = _rs_chip_sum(g_top, sib_top, xyc, 1, "rs_sum_w_in_top")
    sum_kv = _rs_chip_sum(g["mem_w_kv"], sib_kv, xyc, 0, "rs_sum_mem_w_kv")
    T = B * S
    tm_t = _pick(T // 2, 1024)
    halfm = (T // 2) // tm_t
    dh, (chips_top, chips_kv, sib_bot) = _mm(t["dproj"], wf["w_in"], mode="nt", out_dtype=f32, name="d_h_a", tm=tm_t,
                                             m_tiles=(0, halfm), into="new",
                                             comm=_join(_chips_comm([sum_top, sum_kv]), _sibling_comm([g_bot], [1])))
    sum_bot = _rs_chip_sum(g_bot, sib_bot, xyc, 1, "rs_sum_w_in_bot")
    dh, (chips_bot,) = _mm(t["dproj"], wf["w_in"], mode="nt", out_dtype=f32, name="d_h_b", tm=tm_t, m_tiles=(halfm, halfm),
                           into=dh, comm=_chips_comm([sum_bot]))
    dx, dng = _rms_bwd(t["x2"], t["ng"], dh, t["dy"], "rms_x_bwd")
    g["norm_g"] = dng.reshape(D)

    grad, delta, new_m, new_v = {}, {}, {}, {}
    for n, ls, lc in zip(early + ("mem_w_kv",), sib_early + [sib_kv], chips_early + [chips_kv]):
        grad[n], delta[n], new_m[n], new_v[n] = _adamw_shard(g[n], ls, lc, w[n], mo[n], vo[n], me1, BIG_AXIS[n], "adamw_" + n)
    top = _adamw_shard(g_top, sib_top, chips_top, w_in, m_w_in, v_w_in, me1, 1, "adamw_w_in_top")
    grad["w_in"], delta["w_in"], new_m["w_in"], new_v["w_in"] = _adamw_shard(
        g_bot, sib_bot, chips_bot, w_in, m_w_in, v_w_in, me1, 1, "adamw_w_in_bot", row0=D // 2, prev=top)

    n_rep = sum(w[n].size for n in SMALL) // LANES
    conv_rows = g["conv_w"].reshape(3, N_DEV, LANES).transpose(1, 0, 2).reshape(3 * N_DEV, LANES)
    n_rows = -(-(n_rep + 3 * N_DEV + 1) // 8) * 8
    part = _pack_rows([g[n] for n in SMALL] + [conv_rows, jnp.full((1, LANES), sq_err, f32)], n_rows)
    tot = _allreduce_small(part)
    loss = tot[n_rep + 3 * N_DEV, 0] * (0.5 / D)
    n_small = -(-(n_rep + 3) // 8) * 8
    g_small = _pack_rows([tot[:n_rep], lax.dynamic_slice(tot, (n_rep + 3 * me, 0), (3, LANES))], n_small)
    names = SMALL + ("conv_w",)
    d_s, m_s, v_s = _adamw_small(g_small, _pack_rows([w[n] for n in names], n_small), _pack_rows([mo[n] for n in names], n_small),
                                 _pack_rows([vo[n] for n in names], n_small))
    off = 0
    for n in names:
        r = w[n].size // LANES
        grad[n], delta[n] = g_small[off:off + r].reshape(w[n].shape), d_s[off:off + r].reshape(w[n].shape)
        new_m[n], new_v[n] = m_s[off:off + r].reshape(w[n].shape), v_s[off:off + r].reshape(w[n].shape)
        off += r
    return (loss, dx.reshape(B, S, D), *[grad[n] for n in ALL_W], *[delta[n] for n in ALL_W], *[new_m[n] for n in ALL_W],
            *[new_v[n] for n in ALL_W])
```

```python
import functools

import jax
import jax.numpy as jnp
from jax import lax
from jax.experimental import pallas as pl
from jax.experimental.pallas import tpu as pltpu

f32 = jnp.float32
bf16 = jnp.bfloat16

N_DEV = 8
HEAD_DIM = 128
DILATIONS = (1, 4, 16)
N_GROUPS = 3
HEADS = 4
BLK = 128
ATTN_QKV = N_GROUPS * HEADS * HEAD_DIM
ATTN_OUT = HEADS * HEAD_DIM
CONV_W = 1024
MEM_LEN = 256
MEM_HEADS = 4
MEM_HD = 256
MEM_W = MEM_HEADS * MEM_HD
EPS = 1e-6
Q0, K0, V0 = 0, ATTN_QKV, 2 * ATTN_QKV
ZA = 3 * ATTN_QKV
CB, CC, CV, ZC = ZA + ATTN_OUT, ZA + ATTN_OUT + CONV_W, ZA + ATTN_OUT + 2 * CONV_W, ZA + ATTN_OUT + 3 * CONV_W
MQ = ZC + CONV_W
ZM = MQ + MEM_W
GT = ZM + MEM_W

ADAM_LR, ADAM_B1, ADAM_B2, ADAM_EPS, ADAM_WD, ADAM_STEP = 0.001, 0.9, 0.999, 1e-08, 0.01, 10

VMEM_LIMIT = 56 * 1024 * 1024
LANES = 128

NT_DIMS = (((1,), (1,)), ((), ()))
TN_DIMS = (((0,), (0,)), ((), ()))
NN_DIMS = (((1,), (0,)), ((), ()))


def _params(sem=None):
    return pltpu.CompilerParams(dimension_semantics=sem, vmem_limit_bytes=VMEM_LIMIT)


def _pick(n, pref, q=LANES):
    t = (min(pref, n) // q) * q
    while t >= q:
        if n % t == 0:
            return t
        t -= q
    return n


def _dot(a, b, dims):
    return lax.dot_general(a.astype(bf16), b.astype(bf16), dims, preferred_element_type=f32)


def _sigmoid(z):
    return 0.5 * jnp.tanh(0.5 * z) + 0.5


def _rstd(v):
    return lax.rsqrt(jnp.mean(v * v, axis=-1, keepdims=True) + EPS)


class _Deferred:
    def __init__(self, stage, sems, step, last, n):
        assert last >= 1
        self.stage, self.sems, self.step, self.last, self.n = stage, sems, step, last, n
        self.slot = step % 2

    def _drain(self, slot, like):
        for c in range(self.n):
            pltpu.make_async_copy(self.stage.at[slot, c], like(c), self.sems.at[slot, c]).wait()

    def begin(self, like):
        pl.when(self.step >= 2)(lambda: self._drain(self.slot, like))

    def start(self, c, dst):
        pltpu.make_async_copy(self.stage.at[self.slot, c], dst, self.sems.at[self.slot, c]).start()

    def end(self, like):
        @pl.when(self.step == self.last)
        def _():
            self._drain(self.slot, like)
            self._drain(1 - self.slot, like)


def _row_sum(v):
    hi = v.astype(bf16)
    lo = (v - hi.astype(f32)).astype(bf16)
    ones = jnp.ones((v.shape[1], v.shape[1]), bf16)
    return _dot(hi, ones, NN_DIMS) + _dot(lo, ones, NN_DIMS)


def _rstd_head(v):
    return lax.rsqrt(_row_sum(v * v) * (1.0 / v.shape[1]) + EPS)


class _Comm:
    def __init__(self, ins, out_shapes, sem_shape, copies):
        self.ins, self.out_shapes, self.sem_shape, self.copies = list(ins), list(out_shapes), sem_shape, copies

    def scratch(self):
        return [pltpu.SemaphoreType.DMA(self.sem_shape), pltpu.SemaphoreType.DMA(self.sem_shape)]


def _mm(a, b, *, mode, out_dtype, name, tm=1024, tn=1024, tk=4096, m_tiles=None, into=None, comm=None):
    if mode == "nn":
        (M, K), (K2, N) = a.shape, b.shape
    elif mode == "nt":
        (M, K), (N, K2) = a.shape, b.shape
    else:
        (K, M), (K2, N) = a.shape, b.shape
    assert K == K2
    tm, tn, tk = _pick(M, tm), _pick(N, tn), _pick(K, tk)
    nk = K // tk
    m0, mc = m_tiles if m_tiles is not None else (0, M // tm)
    o0 = 0 if into is None else m0
    aliased = into is not None and not isinstance(into, str)
    n_ci = len(comm.ins) if comm else 0
    n_co = len(comm.out_shapes) if comm else 0
    grid = (mc, N // tn, nk)
    dims = {"nn": NN_DIMS, "nt": NT_DIMS, "tn": TN_DIMS}[mode]

    def body(*refs, nk):
        a_ref, b_ref = refs[:2]
        pos = 3 if aliased else 2
        c_ins, o_ref, c_outs = refs[pos:pos + n_ci], refs[pos + n_ci], refs[pos + n_ci + 1:pos + n_ci + 1 + n_co]
        scratch = refs[pos + n_ci + 1 + n_co:]
        ids = [pl.program_id(d) for d in range(3)]
        if comm:
            sems = scratch[-2:]

            @pl.when((ids[0] == 0) & (ids[1] == 0) & (ids[2] == 0))
            def _():
                for cp in comm.copies(c_ins, c_outs, *sems):
                    cp.start()

        if nk == 1:
            o_ref[...] = _dot(a_ref[...], b_ref[...], dims).astype(o_ref.dtype)
        else:
            acc_ref, k = scratch[0], ids[2]

            @pl.when(k == 0)
            def _():
                acc_ref[...] = _dot(a_ref[...], b_ref[...], dims)

            @pl.when((k > 0) & (k < nk - 1))
            def _():
                acc_ref[...] += _dot(a_ref[...], b_ref[...], dims)

            @pl.when(k == nk - 1)
            def _():
                o_ref[...] = (acc_ref[...] + _dot(a_ref[...], b_ref[...], dims)).astype(o_ref.dtype)

        if comm:
            @pl.when((ids[0] == grid[0] - 1) & (ids[1] == grid[1] - 1) & (ids[2] == grid[2] - 1))
            def _():
                for cp in comm.copies(c_ins, c_outs, *sems):
                    cp.wait()

    if mode == "tn":
        a_spec = pl.BlockSpec((tk, tm), lambda i, j, k: (k, m0 + i))
    else:
        a_spec = pl.BlockSpec((tm, tk), lambda i, j, k: (m0 + i, k))
    if mode == "nt":
        b_spec = pl.BlockSpec((tn, tk), lambda i, j, k: (j, k))
    else:
        b_spec = pl.BlockSpec((tk, tn), lambda i, j, k: (k, j))
    hbm = pl.BlockSpec(memory_space=pl.ANY)
    res = pl.pallas_call(
        functools.partial(body, nk=nk),
        grid=grid,
        in_specs=[a_spec, b_spec] + [hbm] * (aliased + n_ci),
        out_specs=[pl.BlockSpec((tm, tn), lambda i, j, k: (o0 + i, j))] + [hbm] * n_co,
        out_shape=[jax.ShapeDtypeStruct((mc * tm if into is None else M, N), out_dtype)] + (comm.out_shapes if comm else []),
        scratch_shapes=([pltpu.VMEM((tm, tn), f32)] if nk > 1 else []) + (comm.scratch() if comm else []),
        input_output_aliases={2: 0} if aliased else {},
        compiler_params=_params(("arbitrary",) * 3 if comm else ("parallel", "parallel", "arbitrary")),
        name=name,
    )(a, b, *([into] if aliased else []), *(comm.ins if comm else []))
    return (res[0], list(res[1:])) if comm else res[0]


def _rms_fwd(x, g, name):
    T, D = x.shape
    tm = _pick(T, 256, 8)

    def body(x_ref, g_ref, h_ref):
        xv = x_ref[...]
        h_ref[...] = (xv * _rstd(xv) * g_ref[...]).astype(bf16)

    return pl.pallas_call(
        body, grid=(T // tm,),
        in_specs=[pl.BlockSpec((tm, D), lambda i: (i, 0)), pl.BlockSpec((1, D), lambda i: (0, 0))],
        out_specs=pl.BlockSpec((tm, D), lambda i: (i, 0)),
        out_shape=jax.ShapeDtypeStruct((T, D), bf16),
        compiler_params=_params(("parallel",)), name=name,
    )(x, g)


def _rms_bwd(x, g, dh, dy, name):
    T, D = x.shape
    tm = _pick(T, 256, 8)
    with_dx = dy is not None

    def body(*refs):
        if with_dx:
            x_ref, g_ref, dh_ref, dy_ref, dx_ref, dg_ref = refs
        else:
            x_ref, g_ref, dh_ref, dg_ref = refs
        xv = x_ref[...]
        r = _rstd(xv)
        xh = xv * r
        dhv = dh_ref[...]
        part = jnp.sum(dhv * xh, axis=0, keepdims=True)

        @pl.when(pl.program_id(0) == 0)
        def _():
            dg_ref[...] = part

        @pl.when(pl.program_id(0) > 0)
        def _():
            dg_ref[...] += part

        if with_dx:
            dxh = dhv * g_ref[...]
            dx_ref[...] = dy_ref[...] + r * (dxh - xh * jnp.mean(dxh * xh, axis=-1, keepdims=True))

    row = pl.BlockSpec((tm, D), lambda i: (i, 0))
    vec = pl.BlockSpec((1, D), lambda i: (0, 0))
    if with_dx:
        return pl.pallas_call(
            body, grid=(T // tm,), in_specs=[row, vec, row, row], out_specs=[row, vec],
            out_shape=[jax.ShapeDtypeStruct((T, D), f32), jax.ShapeDtypeStruct((1, D), f32)],
            compiler_params=_params(("arbitrary",)), name=name,
        )(x, g, dh, dy)
    return pl.pallas_call(
        body, grid=(T // tm,), in_specs=[row, vec, row], out_specs=vec,
        out_shape=jax.ShapeDtypeStruct((1, D), f32),
        compiler_params=_params(("arbitrary",)), name=name,
    )(x, g, dh)


UNIT_UNROLL = 4


def _rows(start, size, stride):
    return pl.ds(start, size) if stride == 1 else pl.ds(start, size, stride=stride)


def _attn_units(S, d, first, prev):
    nb = S // d // BLK

    def do_first(r, c):
        first(_rows(r, BLK, d))
        return c

    lax.fori_loop(0, d, do_first, 0, unroll=min(d, UNIT_UNROLL))
    if nb > 1:
        def do_prev(u, c):
            r = u // (nb - 1)
            b = u % (nb - 1) + 1
            base = r + d * b * BLK
            prev(_rows(base, BLK, d), _rows(base - d * BLK, 2 * BLK, d))
            return c

        lax.fori_loop(0, d * (nb - 1), do_prev, 0, unroll=UNIT_UNROLL)


def _band_bias(nkeys):
    i = lax.broadcasted_iota(jnp.int32, (BLK, nkeys), 0)
    j = lax.broadcasted_iota(jnp.int32, (BLK, nkeys), 1)
    ok = (j <= i) if nkeys == BLK else ((j >= i) & (j <= i + BLK))
    return jnp.where(ok, 0.0, -jnp.inf).astype(f32)


def _normalize_into(src_ref, gain, dst_ref, S):
    for c in range(S // 256):
        rows = pl.ds(c * 256, 256)
        v = src_ref[rows, :]
        dst_ref[rows, :] = v * _rstd_head(v) * gain


def _attn_fwd(proj, gq, gk, B, S):
    T, NC = proj.shape
    scale = HEAD_DIM ** -0.5

    def body(q_ref, k_ref, v_ref, z_ref, gq_ref, gk_ref, ag_ref, a_ref, lse_ref, qs, ks, o0, o1, o2, l0, l1, l2, bias1, bias2):
        g = pl.program_id(2)
        bias1[...] = _band_bias(BLK)
        bias2[...] = _band_bias(2 * BLK)
        _normalize_into(q_ref, gq_ref[pl.ds(g, 1), :], qs, S)
        _normalize_into(k_ref, gk_ref[pl.ds(g, 1), :], ks, S)
        o_s, l_s = (o0, o1, o2), (l0, l1, l2)

        def run(gi):
            def unit(qr, kr, nkeys):
                s = _dot(qs[qr, :], ks[kr, :], NT_DIMS) * scale + (bias1 if nkeys == BLK else bias2)[...]
                m = jnp.max(s, axis=-1, keepdims=True)
                p = jnp.exp(s - m)
                den = jnp.sum(p, axis=-1, keepdims=True)
                o_s[gi][qr, :] = _dot(p, v_ref[kr, :], NN_DIMS) * (1.0 / den)
                l_s[gi][qr, :] = jnp.broadcast_to(m + jnp.log(den), (BLK, HEAD_DIM))

            _attn_units(S, DILATIONS[gi], lambda qr: unit(qr, qr, BLK), lambda qr, kr: unit(qr, kr, 2 * BLK))

        for gi in range(N_GROUPS):
            pl.when(g == gi)(functools.partial(run, gi))

        @pl.when(g == N_GROUPS - 1)
        def _():
            for c in range(S // 256):
                rows = pl.ds(c * 256, 256)
                la, lb, lc = l0[rows, :], l1[rows, :], l2[rows, :]
                m = jnp.maximum(jnp.maximum(la, lb), lc)
                wa, wb, wc = jnp.exp(la - m), jnp.exp(lb - m), jnp.exp(lc - m)
                tot = wa + wb + wc
                a = (wa / tot) * o0[rows, :] + (wb / tot) * o1[rows, :] + (wc / tot) * o2[rows, :]
                z = z_ref[rows, :]
                a_ref[rows, :] = a
                lse_ref[rows, :] = m + jnp.log(tot)
                ag_ref[rows, :] = (a * (z * _sigmoid(z))).astype(bf16)

    def slab(col0):
        return pl.BlockSpec((S, HEAD_DIM), lambda b, h, g: (b, col0 // HEAD_DIM + g * HEADS + h))

    gain = pl.BlockSpec((N_GROUPS, HEAD_DIM), lambda b, h, g: (0, 0))
    out = pl.BlockSpec((S, HEAD_DIM), lambda b, h, g: (b, h))
    return pl.pallas_call(
        body, grid=(B, HEADS, N_GROUPS),
        in_specs=[slab(Q0), slab(K0), slab(V0), pl.BlockSpec((S, HEAD_DIM), lambda b, h, g: (b, ZA // HEAD_DIM + h)), gain, gain],
        out_specs=[out, out, out],
        out_shape=[jax.ShapeDtypeStruct((T, ATTN_OUT), bf16), jax.ShapeDtypeStruct((T, ATTN_OUT), f32),
                   jax.ShapeDtypeStruct((T, ATTN_OUT), f32)],
        scratch_shapes=[pltpu.VMEM((S, HEAD_DIM), f32)] * 8 + [pltpu.VMEM((BLK, BLK), f32), pltpu.VMEM((BLK, 2 * BLK), f32)],
        compiler_params=_params(("arbitrary", "arbitrary", "arbitrary")), name="attn_fwd",
    )(proj, proj, proj, proj, gq, gk)


def _rms_bwd_rows(raw, gain, dn, on_mxu=True):
    r = _rstd_head(raw) if on_mxu else _rstd(raw)
    xh = raw * r
    dxh = dn * gain
    if on_mxu:
        mean = _row_sum(dxh * xh) * (1.0 / raw.shape[1])
    else:
        mean = jnp.mean(dxh * xh, axis=-1, keepdims=True)
    return r * (dxh - xh * mean), jnp.sum(dn * xh, axis=0, keepdims=True)


def _attn_bwd(proj, gq, gk, a_comb, lse, dag, dproj, B, S):
    T, NC = proj.shape
    scale = HEAD_DIM ** -0.5

    def body(q_ref, k_ref, v_ref, z_ref, gq_ref, gk_ref, a_ref, lse_ref, dag_ref, dproj_in, dproj_ref, dgq_ref, dgk_ref,
             qs, ks, da_s, dl_s, dq_s, dk_s, dv_s, bias1, bias2, stage, dz_stage, sem, dz_sem):
        b, h, g = pl.program_id(0), pl.program_id(1), pl.program_id(2)
        bias1[...] = _band_bias(BLK)
        bias2[...] = _band_bias(2 * BLK)
        gq_row, gk_row = gq_ref[pl.ds(g, 1), :], gk_ref[pl.ds(g, 1), :]
        _normalize_into(q_ref, gq_row, qs, S)
        _normalize_into(k_ref, gk_row, ks, S)

        def dst(c):
            return dproj_ref.at[pl.ds(b * S, S), pl.ds((Q0, K0, V0)[c] + (g * HEADS + h) * HEAD_DIM, HEAD_DIM)]

        out = _Deferred(stage, sem, (b * HEADS + h) * N_GROUPS + g, B * HEADS * N_GROUPS - 1, 3)
        out.begin(dst)

        @pl.when((b == 0) & (h == 0) & (g == 0))
        def _():
            dgq_ref[...] = jnp.zeros_like(dgq_ref)
            dgk_ref[...] = jnp.zeros_like(dgk_ref)

        @pl.when(g == 0)
        def _():
            for c in range(S // 256):
                rows = pl.ds(c * 256, 256)
                z, a, dg_ = z_ref[rows, :], a_ref[rows, :], dag_ref[rows, :]
                sg = _sigmoid(z)
                da = dg_ * (z * sg)
                da_s[rows, :] = da
                dl_s[rows, :] = _row_sum(da * a)
                dz_stage[rows, :] = (dg_ * a * (sg * (1.0 + z * (1.0 - sg)))).astype(bf16)
            cp = pltpu.make_async_copy(dz_stage, dproj_ref.at[pl.ds(b * S, S), pl.ds(ZA + h * HEAD_DIM, HEAD_DIM)], dz_sem)
            cp.start()
            cp.wait()

        dk_s[...] = jnp.zeros_like(dk_s)
        dv_s[...] = jnp.zeros_like(dv_s)

        def run(gi):
            def unit(qr, kr, nkeys):
                q, k, v = qs[qr, :], ks[kr, :], v_ref[kr, :]
                s = _dot(q, k, NT_DIMS) * scale
                p = jnp.exp(s + (bias1 if nkeys == BLK else bias2)[...] - lse_ref[qr, :][:, :1])
                da = da_s[qr, :]
                dp = _dot(da, v, NT_DIMS)
                ds = p * (dp - dl_s[qr, :][:, :1]) * scale
                dq_s[qr, :] = _dot(ds, k, NN_DIMS)
                dk_s[kr, :] += _dot(ds, q, TN_DIMS)
                dv_s[kr, :] += _dot(p, da, TN_DIMS)

            _attn_units(S, DILATIONS[gi], lambda qr: unit(qr, qr, BLK), lambda qr, kr: unit(qr, kr, 2 * BLK))

        for gi in range(N_GROUPS):
            pl.when(g == gi)(functools.partial(run, gi))

        gq_acc = jnp.zeros((1, HEAD_DIM), f32)
        gk_acc = jnp.zeros((1, HEAD_DIM), f32)
        for c in range(S // 256):
            rows = pl.ds(c * 256, 256)
            dq, gq_p = _rms_bwd_rows(q_ref[rows, :], gq_row, dq_s[rows, :])
            dk, gk_p = _rms_bwd_rows(k_ref[rows, :], gk_row, dk_s[rows, :])
            gq_acc, gk_acc = gq_acc + gq_p, gk_acc + gk_p
            stage[out.slot, 0, rows, :] = dq.astype(bf16)
            stage[out.slot, 1, rows, :] = dk.astype(bf16)
            stage[out.slot, 2, rows, :] = dv_s[rows, :].astype(bf16)
        dgq_ref[pl.ds(g, 1), :] += gq_acc
        dgk_ref[pl.ds(g, 1), :] += gk_acc
        for c in range(3):
            out.start(c, dst(c))
        out.end(dst)

    def slab(col0):
        return pl.BlockSpec((S, HEAD_DIM), lambda b, h, g: (b, col0 // HEAD_DIM + g * HEADS + h))

    gain = pl.BlockSpec((N_GROUPS, HEAD_DIM), lambda b, h, g: (0, 0))
    per_slot = pl.BlockSpec((S, HEAD_DIM), lambda b, h, g: (b, h))
    return pl.pallas_call(
        body, grid=(B, HEADS, N_GROUPS),
        in_specs=[slab(Q0), slab(K0), slab(V0), pl.BlockSpec((S, HEAD_DIM), lambda b, h, g: (b, ZA // HEAD_DIM + h)), gain, gain,
                  per_slot, per_slot, per_slot, pl.BlockSpec(memory_space=pl.ANY)],
        out_specs=[pl.BlockSpec(memory_space=pl.ANY), gain, gain],
        out_shape=[jax.ShapeDtypeStruct(dproj.shape, dproj.dtype), jax.ShapeDtypeStruct((N_GROUPS, HEAD_DIM), f32),
                   jax.ShapeDtypeStruct((N_GROUPS, HEAD_DIM), f32)],
        scratch_shapes=[pltpu.VMEM((S, HEAD_DIM), f32)] * 7 + [pltpu.VMEM((BLK, BLK), f32), pltpu.VMEM((BLK, 2 * BLK), f32),
                                                                pltpu.VMEM((2, 3, S, HEAD_DIM), bf16), pltpu.VMEM((S, HEAD_DIM), bf16),
                                                                pltpu.SemaphoreType.DMA((2, 3)), pltpu.SemaphoreType.DMA],
        input_output_aliases={9: 0},
        compiler_params=_params(("arbitrary", "arbitrary", "arbitrary")), name="attn_bwd",
    )(proj, proj, proj, proj, gq, gk, a_comb, lse, dag, dproj)


def _conv_fwd(proj, conv_w, B, S):
    T, NC = proj.shape
    tc = LANES

    def body(cb_ref, cc_ref, cv_ref, z_ref, w_ref, c_ref, ub):
        u = cc_ref[...] * cv_ref[...]
        ub[0:8, :] = jnp.zeros((8, tc), f32)
        ub[8:8 + S, :] = u
        w = w_ref[...]
        y = w[0:1] * u + w[1:2] * ub[pl.ds(7, S), :] + w[2:3] * ub[pl.ds(6, S), :]
        z = z_ref[...]
        c_ref[...] = (cb_ref[...] * y * (z * _sigmoid(z))).astype(bf16)

    def seg(col0):
        return pl.BlockSpec((S, tc), lambda j, b: (b, col0 // tc + j))

    return pl.pallas_call(
        body, grid=(CONV_W // tc, B),
        in_specs=[seg(CB), seg(CC), seg(CV), seg(ZC), pl.BlockSpec((3, tc), lambda j, b: (0, j))],
        out_specs=pl.BlockSpec((S, tc), lambda j, b: (b, j)),
        out_shape=jax.ShapeDtypeStruct((T, CONV_W), bf16),
        scratch_shapes=[pltpu.VMEM((S + 8, tc), f32)],
        compiler_params=_params(("parallel", "parallel")), name="conv_fwd",
    )(proj, proj, proj, proj, conv_w)


def _conv_bwd(proj, conv_w, dc, dproj, B, S):
    T, NC = proj.shape
    tc = LANES

    def body(cb_ref, cc_ref, cv_ref, z_ref, w_ref, dc_ref, dproj_in, dproj_ref, dw_ref, ub, db, stage, sem):
        j, b = pl.program_id(0), pl.program_id(1)

        def dst(c):
            return dproj_ref.at[pl.ds(b * S, S), pl.ds((CB, CC, CV, ZC)[c] + j * tc, tc)]

        out = _Deferred(stage, sem, j * B + b, (CONV_W // tc) * B - 1, 4)
        out.begin(dst)
        cb, cc, cv, z, dcv = cb_ref[...], cc_ref[...], cv_ref[...], z_ref[...], dc_ref[...]
        u = cc * cv
        ub[0:8, :] = jnp.zeros((8, tc), f32)
        ub[8:8 + S, :] = u
        u1, u2 = ub[pl.ds(7, S), :], ub[pl.ds(6, S), :]
        w = w_ref[...]
        y = w[0:1] * u + w[1:2] * u1 + w[2:3] * u2
        sg = _sigmoid(z)
        si = z * sg
        dyv = dcv * cb * si
        db[0:S, :] = dyv
        db[S:S + 8, :] = jnp.zeros((8, tc), f32)
        du = w[0:1] * dyv + w[1:2] * db[pl.ds(1, S), :] + w[2:3] * db[pl.ds(2, S), :]
        stage[out.slot, 0] = (dcv * y * si).astype(bf16)
        stage[out.slot, 1] = (du * cv).astype(bf16)
        stage[out.slot, 2] = (du * cc).astype(bf16)
        stage[out.slot, 3] = (dcv * cb * y * (sg * (1.0 + z * (1.0 - sg)))).astype(bf16)
        for c in range(4):
            out.start(c, dst(c))
        part = jnp.concatenate([jnp.sum(dyv * u, axis=0, keepdims=True), jnp.sum(dyv * u1, axis=0, keepdims=True),
                                jnp.sum(dyv * u2, axis=0, keepdims=True)], axis=0)

        @pl.when(b == 0)
        def _():
            dw_ref[...] = part

        @pl.when(b > 0)
        def _():
            dw_ref[...] += part

        out.end(dst)

    def seg(col0):
        return pl.BlockSpec((S, tc), lambda j, b: (b, col0 // tc + j))

    return pl.pallas_call(
        body, grid=(CONV_W // tc, B),
        in_specs=[seg(CB), seg(CC), seg(CV), seg(ZC), pl.BlockSpec((3, tc), lambda j, b: (0, j)),
                  pl.BlockSpec((S, tc), lambda j, b: (b, j)), pl.BlockSpec(memory_space=pl.ANY)],
        out_specs=[pl.BlockSpec(memory_space=pl.ANY), pl.BlockSpec((3, tc), lambda j, b: (0, j))],
        out_shape=[jax.ShapeDtypeStruct(dproj.shape, dproj.dtype), jax.ShapeDtypeStruct((3, CONV_W), f32)],
        scratch_shapes=[pltpu.VMEM((S + 8, tc), f32), pltpu.VMEM((S + 8, tc), f32), pltpu.VMEM((2, 4, S, tc), bf16),
                        pltpu.SemaphoreType.DMA((2, 4))],
        input_output_aliases={6: 0},
        compiler_params=_params(("arbitrary", "arbitrary")), name="conv_bwd",
    )(proj, proj, proj, proj, conv_w, dc, dproj)


MEM_CHUNK = 256


def _mem_fwd(proj, mkv, gq, gk, B, S):
    T, NC = proj.shape
    scale = MEM_HD ** -0.5

    def body(q_ref, z_ref, k_ref, v_ref, gq_ref, gk_ref, o_ref):
        kv = k_ref[...]
        kn = (kv * _rstd_head(kv) * gk_ref[...]).astype(bf16)
        vv = v_ref[...].astype(bf16)

        def chunk(c, carry):
            rows = pl.ds(pl.multiple_of(c * MEM_CHUNK, MEM_CHUNK), MEM_CHUNK)
            q = q_ref[rows, :]
            qn = q * _rstd(q) * gq_ref[...]
            s = _dot(qn, kn, NT_DIMS) * scale
            p = jnp.exp(s - jnp.max(s, axis=-1, keepdims=True))
            p = p / jnp.sum(p, axis=-1, keepdims=True)
            z = z_ref[rows, :]
            o_ref[rows, :] = (_dot(p, vv, NN_DIMS) * (z * _sigmoid(z))).astype(bf16)
            return carry

        lax.fori_loop(0, S // MEM_CHUNK, chunk, 0)

    gain = pl.BlockSpec((1, MEM_HD), lambda b, h: (0, 0))
    return pl.pallas_call(
        body, grid=(B, MEM_HEADS),
        in_specs=[pl.BlockSpec((S, MEM_HD), lambda b, h: (b, MQ // MEM_HD + h)),
                  pl.BlockSpec((S, MEM_HD), lambda b, h: (b, ZM // MEM_HD + h)),
                  pl.BlockSpec((MEM_LEN, MEM_HD), lambda b, h: (b, h)),
                  pl.BlockSpec((MEM_LEN, MEM_HD), lambda b, h: (b, MEM_HEADS + h)), gain, gain],
        out_specs=pl.BlockSpec((S, MEM_HD), lambda b, h: (b, h)),
        out_shape=jax.ShapeDtypeStruct((T, MEM_W), bf16),
        compiler_params=_params(("parallel", "parallel")), name="mem_fwd",
    )(proj, proj, mkv, mkv, gq, gk)


def _mem_bwd(proj, mkv, gq, gk, dmo, dproj, B, S):
    T, NC = proj.shape
    scale = MEM_HD ** -0.5

    def body(q_ref, z_ref, k_ref, v_ref, gq_ref, gk_ref, dmo_ref, dproj_in, dproj_ref, dmk_ref, dmv_ref, dgq_ref, dgk_ref,
             dkn_s, dv_s, gq_s, stage, sem):
        b, h = pl.program_id(0), pl.program_id(1)

        def dst(c):
            return dproj_ref.at[pl.ds(b * S, S), pl.ds((MQ, ZM)[c] + h * MEM_HD, MEM_HD)]

        out = _Deferred(stage, sem, b * MEM_HEADS + h, B * MEM_HEADS - 1, 2)
        out.begin(dst)
        kv = k_ref[...]
        kn = (kv * _rstd_head(kv) * gk_ref[...]).astype(bf16)
        vv = v_ref[...].astype(bf16)
        dkn_s[...] = jnp.zeros_like(dkn_s)
        dv_s[...] = jnp.zeros_like(dv_s)
        gq_s[...] = jnp.zeros_like(gq_s)

        def chunk(c, carry):
            rows = pl.ds(pl.multiple_of(c * MEM_CHUNK, MEM_CHUNK), MEM_CHUNK)
            q = q_ref[rows, :]
            qn = q * _rstd(q) * gq_ref[...]
            s = _dot(qn, kn, NT_DIMS) * scale
            p = jnp.exp(s - jnp.max(s, axis=-1, keepdims=True))
            p = p / jnp.sum(p, axis=-1, keepdims=True)
            mo = _dot(p, vv, NN_DIMS)
            z, dg_ = z_ref[rows, :], dmo_ref[rows, :]
            sg = _sigmoid(z)
            do = dg_ * (z * sg)
            stage[out.slot, 1, rows, :] = (dg_ * mo * (sg * (1.0 + z * (1.0 - sg)))).astype(bf16)
            dp = _dot(do, vv, NT_DIMS)
            ds = p * (dp - jnp.sum(do * mo, axis=-1, keepdims=True)) * scale
            dq, gq_p = _rms_bwd_rows(q, gq_ref[...], _dot(ds, kn, NN_DIMS), on_mxu=False)
            stage[out.slot, 0, rows, :] = dq.astype(bf16)
            gq_s[...] += gq_p
            dkn_s[...] += _dot(ds, qn, TN_DIMS)
            dv_s[...] += _dot(p, do, TN_DIMS)
            return carry

        lax.fori_loop(0, S // MEM_CHUNK, chunk, 0)
        for c in range(2):
            out.start(c, dst(c))
        dk, gk_p = _rms_bwd_rows(kv, gk_ref[...], dkn_s[...])
        dmk_ref[...] = dk
        dmv_ref[...] = dv_s[...]

        @pl.when((b == 0) & (h == 0))
        def _():
            dgq_ref[...] = gq_s[...]
            dgk_ref[...] = gk_p

        @pl.when((b > 0) | (h > 0))
        def _():
            dgq_ref[...] += gq_s[...]
            dgk_ref[...] += gk_p

        out.end(dst)

    gain = pl.BlockSpec((1, MEM_HD), lambda b, h: (0, 0))
    kvb = pl.BlockSpec((MEM_LEN, MEM_HD), lambda b, h: (b, h))
    return pl.pallas_call(
        body, grid=(B, MEM_HEADS),
        in_specs=[pl.BlockSpec((S, MEM_HD), lambda b, h: (b, MQ // MEM_HD + h)),
                  pl.BlockSpec((S, MEM_HD), lambda b, h: (b, ZM // MEM_HD + h)),
                  kvb, pl.BlockSpec((MEM_LEN, MEM_HD), lambda b, h: (b, MEM_HEADS + h)), gain, gain,
                  pl.BlockSpec((S, MEM_HD), lambda b, h: (b, h)), pl.BlockSpec(memory_space=pl.ANY)],
        out_specs=[pl.BlockSpec(memory_space=pl.ANY), kvb, kvb, gain, gain],
        out_shape=[jax.ShapeDtypeStruct(dproj.shape, dproj.dtype), jax.ShapeDtypeStruct((B * MEM_LEN, MEM_W), f32),
                   jax.ShapeDtypeStruct((B * MEM_LEN, MEM_W), f32), jax.ShapeDtypeStruct((1, MEM_HD), f32),
                   jax.ShapeDtypeStruct((1, MEM_HD), f32)],
        scratch_shapes=[pltpu.VMEM((MEM_LEN, MEM_HD), f32), pltpu.VMEM((MEM_LEN, MEM_HD), f32), pltpu.VMEM((1, MEM_HD), f32),
                        pltpu.VMEM((2, 2, S, MEM_HD), bf16), pltpu.SemaphoreType.DMA((2, 2))],
        input_output_aliases={7: 0},
        compiler_params=_params(("arbitrary", "arbitrary")), name="mem_bwd",
    )(proj, proj, mkv, mkv, gq, gk, dmo, dproj)


def _merge_fwd(proj, ag, cg, mg, wa, wc, wm):
    T, NC = proj.shape
    D = wa.shape[1]
    tm, tn = _pick(T, 1024), _pick(D, 512)
    assert GT % tn == 0

    def body(ag_ref, cg_ref, mg_ref, wa_ref, wc_ref, wm_ref, g0_ref, g1_ref, g2_ref, mer_ref, ba_ref, bc_ref, bm_ref):
        ba = _dot(ag_ref[...], wa_ref[...], NN_DIMS)
        bc = _dot(cg_ref[...], wc_ref[...], NN_DIMS)
        bm = _dot(mg_ref[...], wm_ref[...], NN_DIMS)
        mer_ref[...] = (_sigmoid(g0_ref[...]) * ba + _sigmoid(g1_ref[...]) * bc + _sigmoid(g2_ref[...]) * bm).astype(bf16)
        ba_ref[...] = ba.astype(bf16)
        bc_ref[...] = bc.astype(bf16)
        bm_ref[...] = bm.astype(bf16)

    def act(w):
        return pl.BlockSpec((tm, w), lambda i, j: (i, 0))

    def wt(k):
        return pl.BlockSpec((k, tn), lambda i, j: (0, j))

    def gate(n):
        return pl.BlockSpec((tm, tn), lambda i, j: (i, (GT + n * D) // tn + j))

    out = pl.BlockSpec((tm, tn), lambda i, j: (i, j))
    return pl.pallas_call(
        body, grid=(T // tm, D // tn),
        in_specs=[act(ATTN_OUT), act(CONV_W), act(MEM_W), wt(ATTN_OUT), wt(CONV_W), wt(MEM_W), gate(0), gate(1), gate(2)],
        out_specs=[out] * 4, out_shape=[jax.ShapeDtypeStruct((T, D), bf16)] * 4,
        compiler_params=_params(("parallel", "parallel")), name="merge_fwd",
    )(ag, cg, mg, wa, wc, wm, proj, proj, proj)


def _out_loss(merged, w_out, x, tgt):
    T, D = x.shape
    tm, tn = _pick(T, 512), _pick(D, 512)

    def body(m_ref, w_ref, x_ref, t_ref, dy_ref, dyb_ref, lp_ref):
        e = x_ref[...] + _dot(m_ref[...], w_ref[...], NN_DIMS) - t_ref[...]
        dy = e / D
        dy_ref[...] = dy
        dyb_ref[...] = dy.astype(bf16)
        part = jnp.full((1, 8, LANES), jnp.sum(e * e), f32)

        @pl.when(pl.program_id(1) == 0)
        def _():
            lp_ref[...] = part

        @pl.when(pl.program_id(1) > 0)
        def _():
            lp_ref[...] += part

    tile = pl.BlockSpec((tm, tn), lambda i, j: (i, j))
    return pl.pallas_call(
        body, grid=(T // tm, D // tn),
        in_specs=[pl.BlockSpec((tm, D), lambda i, j: (i, 0)), pl.BlockSpec((D, tn), lambda i, j: (0, j)), tile, tile],
        out_specs=[tile, tile, pl.BlockSpec((1, 8, LANES), lambda i, j: (i, 0, 0))],
        out_shape=[jax.ShapeDtypeStruct((T, D), f32), jax.ShapeDtypeStruct((T, D), bf16),
                   jax.ShapeDtypeStruct((T // tm, 8, LANES), f32)],
        compiler_params=_params(("parallel", "arbitrary")), name="out_loss",
    )(merged, w_out, x, tgt)


def _merge_bwd(proj, dyb, w_out, ba, bc, bm):
    T, NC = proj.shape
    D = w_out.shape[0]
    tm, tn = _pick(T, 512), _pick(D, 512)
    assert GT % tn == 0

    def body(dy_ref, w_ref, ba_ref, bc_ref, bm_ref, g0_ref, g1_ref, g2_ref, da_ref, dc_ref, dm_ref, dproj_ref, stage, sem):
        i, j = pl.program_id(0), pl.program_id(1)

        def dst(n):
            return dproj_ref.at[pl.ds(i * tm, tm), pl.ds(GT + n * D + j * tn, tn)]

        out = _Deferred(stage, sem, i * (D // tn) + j, (T // tm) * (D // tn) - 1, 3)
        out.begin(dst)
        dmer = _dot(dy_ref[...], w_ref[...], NT_DIMS)
        for n, (g_ref, br_ref, o_ref) in enumerate(((g0_ref, ba_ref, da_ref), (g1_ref, bc_ref, dc_ref), (g2_ref, bm_ref, dm_ref))):
            sg = _sigmoid(g_ref[...])
            o_ref[...] = (sg * dmer).astype(bf16)
            stage[out.slot, n] = (dmer * br_ref[...].astype(f32) * (sg * (1.0 - sg))).astype(bf16)
            out.start(n, dst(n))
        out.end(dst)

    tile = pl.BlockSpec((tm, tn), lambda i, j: (i, j))

    def gate(n):
        return pl.BlockSpec((tm, tn), lambda i, j: (i, (GT + n * D) // tn + j))

    return pl.pallas_call(
        body, grid=(T // tm, D // tn),
        in_specs=[pl.BlockSpec((tm, D), lambda i, j: (i, 0)), pl.BlockSpec((tn, D), lambda i, j: (j, 0)), tile, tile, tile,
                  gate(0), gate(1), gate(2)],
        out_specs=[tile, tile, tile, pl.BlockSpec(memory_space=pl.ANY)],
        out_shape=[jax.ShapeDtypeStruct((T, D), bf16)] * 3 + [jax.ShapeDtypeStruct((T, NC), bf16)],
        scratch_shapes=[pltpu.VMEM((2, 3, tm, tn), bf16), pltpu.SemaphoreType.DMA((2, 3))],
        compiler_params=_params(("arbitrary", "arbitrary")), name="merge_bwd",
    )(dyb, w_out, ba, bc, bm, proj, proj, proj)


def _fwd_bwd_head(x2, mem2, t2, proj, B, S, mem_norm_g, attn_q_norm, attn_k_norm, conv_w, mem_q_norm, mem_k_norm,
                  w_kv, w_a, w_c, w_m, w_out):
    D = x2.shape[1]
    mng = mem_norm_g.reshape(1, D)
    mqn, mkn = mem_q_norm.reshape(1, MEM_HD), mem_k_norm.reshape(1, MEM_HD)

    mh = _rms_fwd(mem2, mng, "rms_mem")
    mkv = _mm(mh, w_kv, mode="nn", out_dtype=f32, name="mkv")
    ag, a_comb, lse = _attn_fwd(proj, attn_q_norm, attn_k_norm, B, S)
    cg = _conv_fwd(proj, conv_w, B, S)
    mg = _mem_fwd(proj, mkv, mqn, mkn, B, S)
    merged, ba, bc, bm = _merge_fwd(proj, ag, cg, mg, w_a, w_c, w_m)
    dy, dyb, lp = _out_loss(merged, w_out, x2, t2)
    sq_err = jnp.sum(lp[:, 0, 0])

    g = {}
    g["w_out"] = _mm(merged, dyb, mode="tn", out_dtype=f32, name="dw_out")
    dba, dbc, dbm, dproj = _merge_bwd(proj, dyb, w_out, ba, bc, bm)
    g["w_br_attn"] = _mm(ag, dba, mode="tn", out_dtype=f32, name="dw_br_attn")
    g["w_br_conv"] = _mm(cg, dbc, mode="tn", out_dtype=f32, name="dw_br_conv")
    g["w_br_mem"] = _mm(mg, dbm, mode="tn", out_dtype=f32, name="dw_br_mem")
    dag = _mm(dba, w_a, mode="nt", out_dtype=f32, name="d_attn_out")
    dcg = _mm(dbc, w_c, mode="nt", out_dtype=f32, name="d_conv_out")
    dmg = _mm(dbm, w_m, mode="nt", out_dtype=f32, name="d_mem_out")
    dproj, g["attn_q_norm"], g["attn_k_norm"] = _attn_bwd(proj, attn_q_norm, attn_k_norm, a_comb, lse, dag, dproj, B, S)
    dproj, g["conv_w"] = _conv_bwd(proj, conv_w, dcg, dproj, B, S)
    dproj, dmk, dmv, dgmq, dgmk = _mem_bwd(proj, mkv, mqn, mkn, dmg, dproj, B, S)
    g["mem_q_norm"], g["mem_k_norm"] = dgmq.reshape(MEM_HD), dgmk.reshape(MEM_HD)
    dmkv = jnp.concatenate([dmk, dmv], axis=1)
    dmh = _mm(dmkv, w_kv, mode="nt", out_dtype=f32, name="d_mem_h")
    g["mem_norm_g"] = _rms_bwd(mem2, mng, dmh, None, "rms_mem_bwd").reshape(D)
    return sq_err, g, dict(dy=dy, dproj=dproj, mh=mh, dmkv=dmkv)


MESH = pl.DeviceIdType.MESH
HBM = pl.BlockSpec(memory_space=pl.ANY)


def _me():
    x, y, c = lax.axis_index("x"), lax.axis_index("y"), lax.axis_index("c")
    return (x, y, c), 4 * x + 2 * y + c


def _peer(k):
    (x, y, c), _ = _me()
    p = (1 - x if k & 4 else x, 1 - y if k & 2 else y, 1 - c if k & 1 else c)
    return p, 4 * p[0] + 2 * p[1] + p[2]


def _window(ref, axis, size, idx):
    if axis is None:
        return ref.at[idx]
    if axis == 0:
        return ref.at[pl.ds(idx * size, size), :]
    return ref.at[:, pl.ds(idx * size, size)]


def _full_shape(s, axis):
    if axis is None:
        return (N_DEV,) + s.shape
    return tuple(N_DEV * d if i == axis else d for i, d in enumerate(s.shape))


OTHER_CHIPS = (4, 2, 6)


def _spread_comm(shards, axes):
    n = len(shards)

    def copies(ins, outs, send_sems, recv_sems, base=0):
        _, me = _me()
        out = []
        for a in range(n):
            mine = _window(outs[a], axes[a], shards[a].shape[axes[a]], me)
            out.append(pltpu.make_async_copy(ins[a], mine, send_sems.at[base + a, N_CHIP]))
            for k, dist in enumerate((1,) + OTHER_CHIPS):
                dev, _ = _peer(dist)
                out.append(pltpu.make_async_remote_copy(
                    src_ref=ins[a], dst_ref=mine, send_sem=send_sems.at[base + a, k], recv_sem=recv_sems.at[base + a, k],
                    device_id=dev, device_id_type=MESH))
        return out

    shapes = [jax.ShapeDtypeStruct(_full_shape(s, ax), s.dtype) for s, ax in zip(shards, axes)]
    return _Comm(shards, shapes, (n, N_CHIP + 1), copies)


def _forward_blocks(fulls, axes):
    n = len(fulls)
    sizes = [f.shape[ax] // N_DEV for f, ax in zip(fulls, axes)]

    def body(*refs):
        outs = refs[n:2 * n]
        send_sems, recv_sems = refs[2 * n:]
        sibling, _ = _peer(1)
        copies = []
        for j, dist in enumerate(OTHER_CHIPS):
            _, held = _peer(dist)
            for a in range(n):
                block = _window(outs[a], axes[a], sizes[a], held)
                copies.append(pltpu.make_async_remote_copy(
                    src_ref=block, dst_ref=block, send_sem=send_sems.at[a, j], recv_sem=recv_sems.at[a, j],
                    device_id=sibling, device_id_type=MESH))
        for cp in copies:
            cp.start()
        for cp in copies:
            cp.wait()

    return pl.pallas_call(
        body, in_specs=[HBM] * n, out_specs=[HBM] * n,
        out_shape=[jax.ShapeDtypeStruct(f.shape, f.dtype) for f in fulls],
        scratch_shapes=[pltpu.SemaphoreType.DMA((n, len(OTHER_CHIPS))), pltpu.SemaphoreType.DMA((n, len(OTHER_CHIPS)))],
        input_output_aliases={a: a for a in range(n)},
        name="forward_blocks",
    )(*fulls)


def _gather_weights(shards, axes):
    n = len(shards)
    full_shape = _full_shape

    def body(*refs):
        ins, outs = refs[:n], refs[n:2 * n]
        send_sems, recv_sems, local_sems = refs[2 * n:]
        (x, y, c), me = _me()
        sibling, sib_id = _peer(1)
        chips = [_peer(4), _peer(2), _peer(6)]

        def win(a, idx):
            return _window(outs[a], axes[a], shards[a].shape[axes[a]] if axes[a] is not None else None, idx)

        def copy(a, k, block, to, src=None):
            return pltpu.make_async_remote_copy(
                src_ref=win(a, block) if src is None else src, dst_ref=win(a, block),
                send_sem=send_sems.at[a, k], recv_sem=recv_sems.at[a, k], device_id=to, device_id_type=MESH)

        mine = [pltpu.make_async_copy(ins[a], win(a, me), local_sems.at[a]) for a in range(n)]
        for cp in mine:
            cp.start()
        first = []
        for a in range(n):
            first.append(copy(a, 0, me, sibling, src=ins[a]))
            first += [copy(a, 1 + j, me, dev, src=ins[a]) for j, (dev, _) in enumerate(chips)]
        for cp in first:
            cp.start()
        passed = []
        for j, (dev, dev_id) in enumerate(chips):
            for a in range(n):
                copy(a, 1 + j, dev_id, (x, y, c)).wait_recv()
                cp = copy(a, 4 + j, dev_id, sibling)
                cp.start()
                passed.append(cp)
        for a in range(n):
            copy(a, 0, sib_id, (x, y, c)).wait_recv()
        for j, (dev, dev_id) in enumerate(chips):
            for a in range(n):
                copy(a, 4 + j, dev_id + 1 - 2 * c, (x, y, c)).wait_recv()
        for cp in first + passed:
            cp.wait_send()
        for cp in mine:
            cp.wait()

    return pl.pallas_call(
        body, in_specs=[HBM] * n, out_specs=[HBM] * n,
        out_shape=[jax.ShapeDtypeStruct(full_shape(s, ax), s.dtype) for s, ax in zip(shards, axes)],
        scratch_shapes=[pltpu.SemaphoreType.DMA((n, 7)), pltpu.SemaphoreType.DMA((n, 7)), pltpu.SemaphoreType.DMA((n,))],
        name="gather_weights",
    )(*shards)


N_CHIP = 4


def _shard_shape(g, ax):
    return tuple(d // N_DEV if i == ax else d for i, d in enumerate(g.shape))


def _sibling_comm(grads, axes):
    n = len(grads)
    sizes = [g.shape[ax] // N_DEV for g, ax in zip(grads, axes)]

    def copies(ins, lands, send_sems, recv_sems, base=0):
        sibling, _ = _peer(1)
        out = []
        for j in range(N_CHIP):
            _, owner = _peer(2 * j + 1)
            for a in range(n):
                out.append(pltpu.make_async_remote_copy(
                    src_ref=_window(ins[a], axes[a], sizes[a], owner), dst_ref=lands[a].at[j],
                    send_sem=send_sems.at[base + a, j], recv_sem=recv_sems.at[base + a, j], device_id=sibling,
                    device_id_type=MESH))
        return out

    shapes = [jax.ShapeDtypeStruct((N_CHIP,) + _shard_shape(g, ax), g.dtype) for g, ax in zip(grads, axes)]
    return _Comm(grads, shapes, (n, N_CHIP), copies)


def _chips_comm(sums):
    n = len(sums)

    def copies(ins, lands, send_sems, recv_sems, base=0):
        out = []
        for j in range(1, N_CHIP):
            dev, _ = _peer(2 * j)
            for a in range(n):
                out.append(pltpu.make_async_remote_copy(
                    src_ref=ins[a].at[j - 1], dst_ref=lands[a].at[j - 1],
                    send_sem=send_sems.at[base + a, j - 1], recv_sem=recv_sems.at[base + a, j - 1], device_id=dev,
                    device_id_type=MESH))
        return out

    return _Comm(sums, [jax.ShapeDtypeStruct(s.shape, s.dtype) for s in sums], (n, N_CHIP), copies)


def _join(c1, c2):
    n1, m1, k1 = len(c1.ins), len(c1.out_shapes), c1.sem_shape[0]

    def copies(ins, lands, send_sems, recv_sems):
        return (c1.copies(ins[:n1], lands[:m1], send_sems, recv_sems, base=0)
                + c2.copies(ins[n1:], lands[m1:], send_sems, recv_sems, base=k1))

    return _Comm(c1.ins + c2.ins, c1.out_shapes + c2.out_shapes, (k1 + c2.sem_shape[0], N_CHIP), copies)


def _rs_chip_sum(grad, land, xyc, axis, name):
    R, C = land.shape[1:]
    tr = _pick(R, max(8, (256 * 1024) // C), 8)

    def body(xyc_ref, g_ref, l_ref, o_ref):
        o_ref[0] = (g_ref[...] + l_ref[0]).astype(bf16)

    def owner(j, xyc_ref):
        jj = j + 1
        return 4 * (xyc_ref[0] ^ (jj >> 1)) + 2 * (xyc_ref[1] ^ (jj & 1)) + xyc_ref[2]

    if axis == 0:
        g_spec = pl.BlockSpec((tr, C), lambda j, i, xyc_ref: (owner(j, xyc_ref) * (R // tr) + i, 0))
    else:
        g_spec = pl.BlockSpec((tr, C), lambda j, i, xyc_ref: (i, owner(j, xyc_ref)))
    return pl.pallas_call(
        body,
        grid_spec=pltpu.PrefetchScalarGridSpec(
            num_scalar_prefetch=1, grid=(N_CHIP - 1, R // tr),
            in_specs=[g_spec, pl.BlockSpec((1, tr, C), lambda j, i, xyc_ref: (j + 1, i, 0))],
            out_specs=pl.BlockSpec((1, tr, C), lambda j, i, xyc_ref: (j, i, 0))),
        out_shape=jax.ShapeDtypeStruct((N_CHIP - 1, R, C), bf16),
        compiler_params=_params(("parallel", "parallel")), name=name,
    )(xyc, grad, land)


def _allreduce_small(part):
    R = part.shape[0]

    def body(p_ref, o_ref, buf, send_sems, recv_sems):
        (x, y, c), me = _me()
        buf[me] = p_ref[...]
        copies = []
        for k in range(1, N_DEV):
            dev, dev_id = _peer(k)
            copies.append(pltpu.make_async_remote_copy(
                src_ref=p_ref, dst_ref=buf.at[me], send_sem=send_sems.at[k - 1], recv_sem=recv_sems.at[k - 1],
                device_id=dev, device_id_type=MESH))
        for cp in copies:
            cp.start()
        for k in range(1, N_DEV):
            _, dev_id = _peer(k)
            pltpu.make_async_remote_copy(
                src_ref=p_ref, dst_ref=buf.at[dev_id], send_sem=send_sems.at[k - 1], recv_sem=recv_sems.at[k - 1],
                device_id=(x, y, c), device_id_type=MESH).wait_recv()
        for cp in copies:
            cp.wait_send()
        acc = buf[0]
        for d in range(1, N_DEV):
            acc = acc + buf[d]
        o_ref[...] = acc

    return pl.pallas_call(
        body, in_specs=[pl.BlockSpec(memory_space=pltpu.VMEM)], out_specs=pl.BlockSpec(memory_space=pltpu.VMEM),
        out_shape=jax.ShapeDtypeStruct((R, LANES), f32),
        scratch_shapes=[pltpu.VMEM((N_DEV, R, LANES), f32), pltpu.SemaphoreType.DMA((N_DEV - 1,)), pltpu.SemaphoreType.DMA((N_DEV - 1,))],
        name="allreduce_small",
    )(part)


def _adamw_math(w, g, m, v):
    m2 = ADAM_B1 * m + (1.0 - ADAM_B1) * g
    v2 = ADAM_B2 * v + (1.0 - ADAM_B2) * (g * g)
    m_hat = m2 / (1.0 - ADAM_B1 ** ADAM_STEP)
    v_hat = v2 / (1.0 - ADAM_B2 ** ADAM_STEP)
    return -ADAM_LR * (m_hat / (jnp.sqrt(v_hat) + ADAM_EPS) + ADAM_WD * w), m2, v2


def _adamw_shard(grad, land_sib, land_chips, w, m, v, me, axis, name, row0=0, prev=None):
    R, C = land_sib.shape[1:]
    assert axis == 1 or R == w.shape[0]
    tr = _pick(R, max(8, (128 * 1024) // C), 8)
    r0 = row0 // tr
    n_prev = len(prev) if prev else 0

    def body(me_ref, g_ref, s_ref, l_ref, w_ref, m_ref, v_ref, *rest):
        go_ref, d_ref, mo_ref, vo_ref = rest[n_prev:]
        g = g_ref[...] + s_ref[0]
        for j in range(N_CHIP - 1):
            g = g + l_ref[j].astype(f32)
        d, m2, v2 = _adamw_math(w_ref[...], g, m_ref[...], v_ref[...])
        go_ref[...] = g
        d_ref[...] = d
        mo_ref[...] = m2
        vo_ref[...] = v2

    if axis == 0:
        g_spec = pl.BlockSpec((tr, C), lambda i, me_ref: (me_ref[0] * (R // tr) + i, 0))
    else:
        g_spec = pl.BlockSpec((tr, C), lambda i, me_ref: (i, me_ref[0]))
    blk = pl.BlockSpec((tr, C), lambda i, me_ref: (r0 + i, 0))
    return pl.pallas_call(
        body,
        grid_spec=pltpu.PrefetchScalarGridSpec(
            num_scalar_prefetch=1, grid=(R // tr,),
            in_specs=[g_spec, pl.BlockSpec((1, tr, C), lambda i, me_ref: (0, i, 0)),
                      pl.BlockSpec((N_CHIP - 1, tr, C), lambda i, me_ref: (0, i, 0)), blk, blk, blk]
            + [pl.BlockSpec(memory_space=pl.ANY)] * n_prev,
            out_specs=[blk] * 4),
        out_shape=[jax.ShapeDtypeStruct(w.shape, f32)] * 4,
        input_output_aliases={7 + i: i for i in range(n_prev)},
        compiler_params=_params(("parallel",)), name=name,
    )(me, grad, land_sib, land_chips, w, m, v, *(prev or []))


def _adamw_small(g, w, m, v):
    def body(g_ref, w_ref, m_ref, v_ref, d_ref, mo_ref, vo_ref):
        d, m2, v2 = _adamw_math(w_ref[...], g_ref[...], m_ref[...], v_ref[...])
        d_ref[...] = d
        mo_ref[...] = m2
        vo_ref[...] = v2

    return pl.pallas_call(body, out_shape=[jax.ShapeDtypeStruct(g.shape, f32)] * 3, name="adamw_small")(g, w, m, v)


def _pack_rows(parts, total_rows):
    rows = jnp.concatenate([p.reshape(-1, LANES) for p in parts], axis=0)
    return jnp.pad(rows, ((0, total_rows - rows.shape[0]), (0, 0)))


BIG = ("w_in", "mem_w_kv", "w_br_attn", "w_br_conv", "w_br_mem", "w_out")
BIG_AXIS = {"w_in": 1, "mem_w_kv": 0, "w_br_attn": 1, "w_br_conv": 1, "w_br_mem": 1, "w_out": 0}
SMALL = ("norm_g", "mem_norm_g", "attn_q_norm", "attn_k_norm", "mem_q_norm", "mem_k_norm")
ALL_W = ("norm_g", "mem_norm_g", "w_in", "attn_q_norm", "attn_k_norm", "conv_w", "mem_w_kv", "mem_q_norm", "mem_k_norm",
         "w_br_attn", "w_br_conv", "w_br_mem", "w_out")


def kernel(x, mem, norm_g, mem_norm_g, w_in, attn_q_norm, attn_k_norm, conv_w, mem_w_kv, mem_q_norm, mem_k_norm, w_br_attn, w_br_conv, w_br_mem, w_out, loss_target, m_norm_g, m_mem_norm_g, m_w_in, m_attn_q_norm, m_attn_k_norm, m_conv_w, m_mem_w_kv, m_mem_q_norm, m_mem_k_norm, m_w_br_attn, m_w_br_conv, m_w_br_mem, m_w_out, v_norm_g, v_mem_norm_g, v_w_in, v_attn_q_norm, v_attn_k_norm, v_conv_w, v_mem_w_kv, v_mem_q_norm, v_mem_k_norm, v_w_br_attn, v_w_br_conv, v_w_br_mem, v_w_out):
    w = dict(norm_g=norm_g, mem_norm_g=mem_norm_g, w_in=w_in, attn_q_norm=attn_q_norm, attn_k_norm=attn_k_norm, conv_w=conv_w,
             mem_w_kv=mem_w_kv, mem_q_norm=mem_q_norm, mem_k_norm=mem_k_norm, w_br_attn=w_br_attn, w_br_conv=w_br_conv,
             w_br_mem=w_br_mem, w_out=w_out)
    mo = dict(norm_g=m_norm_g, mem_norm_g=m_mem_norm_g, w_in=m_w_in, attn_q_norm=m_attn_q_norm, attn_k_norm=m_attn_k_norm,
              conv_w=m_conv_w, mem_w_kv=m_mem_w_kv, mem_q_norm=m_mem_q_norm, mem_k_norm=m_mem_k_norm, w_br_attn=m_w_br_attn,
              w_br_conv=m_w_br_conv, w_br_mem=m_w_br_mem, w_out=m_w_out)
    vo = dict(norm_g=v_norm_g, mem_norm_g=v_mem_norm_g, w_in=v_w_in, attn_q_norm=v_attn_q_norm, attn_k_norm=v_attn_k_norm,
              conv_w=v_conv_w, mem_w_kv=v_mem_w_kv, mem_q_norm=v_mem_q_norm, mem_k_norm=v_mem_k_norm, w_br_attn=v_w_br_attn,
              w_br_conv=v_w_br_conv, w_br_mem=v_w_br_mem, w_out=v_w_out)
    B, S, D = x.shape
    me = (4 * lax.axis_index("x") + 2 * lax.axis_index("y") + lax.axis_index("c")).astype(jnp.int32)

    conv_pad = jnp.pad(conv_w, ((0, 8 - conv_w.shape[0]), (0, 0)))
    w_in_full, conv_g = _gather_weights([w_in.astype(bf16), conv_pad], [1, None])
    conv_full = conv_g[:, :3, :].transpose(1, 0, 2).reshape(3, CONV_W)
    T = B * S
    x2, t2, mem2, ng = x.reshape(T, D), loss_target.reshape(T, D), mem.reshape(B * MEM_LEN, D), norm_g.reshape(1, D)
    h = _rms_fwd(x2, ng, "rms_x")
    later = BIG[1:]
    later_axes = [BIG_AXIS[n] for n in later]
    proj, spread = _mm(h, w_in_full, mode="nn", out_dtype=f32, name="proj",
                       comm=_spread_comm([w[n].astype(bf16) for n in later], later_axes))
    wf = dict(zip(later, _forward_blocks(spread, later_axes)), w_in=w_in_full)

    sq_err, g, t = _fwd_bwd_head(x2, mem2, t2, proj, B, S, mem_norm_g, attn_q_norm, attn_k_norm, conv_full, mem_q_norm,
                                 mem_k_norm, wf["mem_w_kv"], wf["w_br_attn"], wf["w_br_conv"], wf["w_br_mem"], wf["w_out"])
    t.update(x2=x2, ng=ng, h=h)

    xyc = jnp.stack([lax.axis_index("x"), lax.axis_index("y"), lax.axis_index("c")]).astype(jnp.int32)
    me1 = me.reshape(1)
    early = ("w_out", "w_br_attn", "w_br_conv", "w_br_mem")
    g["mem_w_kv"], sib_early = _mm(t["mh"], t["dmkv"], mode="tn", out_dtype=f32, name="dw_kv",
                                   comm=_sibling_comm([g[n] for n in early], [BIG_AXIS[n] for n in early]))
    sums_early = [_rs_chip_sum(g[n], ls, xyc, BIG_AXIS[n], "rs_sum_" + n) for n, ls in zip(early, sib_early)]
    tm_w = _pick(D // 2, 1024)
    half = (D // 2) // tm_w
    g_top, chips_early = _mm(t["h"], t["dproj"], mode="tn", out_dtype=f32, name="dw_in_top", tm=tm_w, m_tiles=(0, half),
                             comm=_chips_comm(sums_early))
    g_bot, (sib_top, sib_kv) = _mm(t["h"], t["dproj"], mode="tn", out_dtype=f32, name="dw_in_bot", tm=tm_w, m_tiles=(half, half),
                                   comm=_sibling_comm([g_top, g["mem_w_kv"]], [1, 0]))
    sum_top = _rs_chip_sum(g_top, sib_top, xyc, 1, "rs_sum_w_in_top")
    sum_kv = _rs_chip_sum(g["mem_w_kv"], sib_kv, xyc, 0, "rs_sum_mem_w_kv")
    T = B * S
    tm_t = _pick(T // 2, 1024)
    halfm = (T // 2) // tm_t
    dh, (chips_top, chips_kv, sib_bot) = _mm(t["dproj"], wf["w_in"], mode="nt", out_dtype=f32, name="d_h_a", tm=tm_t,
                                             m_tiles=(0, halfm), into="new",
                                             comm=_join(_chips_comm([sum_top, sum_kv]), _sibling_comm([g_bot], [1])))
    sum_bot = _rs_chip_sum(g_bot, sib_bot, xyc, 1, "rs_sum_w_in_bot")
    dh, (chips_bot,) = _mm(t["dproj"], wf["w_in"], mode="nt", out_dtype=f32, name="d_h_b", tm=tm_t, m_tiles=(halfm, halfm),
                           into=dh, comm=_chips_comm([sum_bot]))
    dx, dng = _rms_bwd(t["x2"], t["ng"], dh, t["dy"], "rms_x_bwd")
    g["norm_g"] = dng.reshape(D)

    grad, delta, new_m, new_v = {}, {}, {}, {}
    for n, ls, lc in zip(early + ("mem_w_kv",), sib_early + [sib_kv], chips_early + [chips_kv]):
        grad[n], delta[n], new_m[n], new_v[n] = _adamw_shard(g[n], ls, lc, w[n], mo[n], vo[n], me1, BIG_AXIS[n], "adamw_" + n)
    top = _adamw_shard(g_top, sib_top, chips_top, w_in, m_w_in, v_w_in, me1, 1, "adamw_w_in_top")
    grad["w_in"], delta["w_in"], new_m["w_in"], new_v["w_in"] = _adamw_shard(
        g_bot, sib_bot, chips_bot, w_in, m_w_in, v_w_in, me1, 1, "adamw_w_in_bot", row0=D // 2, prev=top)

    n_rep = sum(w[n].size for n in SMALL) // LANES
    conv_rows = g["conv_w"].reshape(3, N_DEV, LANES).transpose(1, 0, 2).reshape(3 * N_DEV, LANES)
    n_rows = -(-(n_rep + 3 * N_DEV + 1) // 8) * 8
    part = _pack_rows([g[n] for n in SMALL] + [conv_rows, jnp.full((1, LANES), sq_err, f32)], n_rows)
    tot = _allreduce_small(part)
    loss = tot[n_rep + 3 * N_DEV, 0] * (0.5 / D)
    n_small = -(-(n_rep + 3) // 8) * 8
    g_small = _pack_rows([tot[:n_rep], lax.dynamic_slice(tot, (n_rep + 3 * me, 0), (3, LANES))], n_small)
    names = SMALL + ("conv_w",)
    d_s, m_s, v_s = _adamw_small(g_small, _pack_rows([w[n] for n in names], n_small), _pack_rows([mo[n] for n in names], n_small),
                                 _pack_rows([vo[n] for n in names], n_small))
    off = 0
    for n in names:
        r = w[n].size // LANES
        grad[n], delta[n] = g_small[off:off + r].reshape(w[n].shape), d_s[off:off + r].reshape(w[n].shape)
        new_m[n], new_v[n] = m_s[off:off + r].reshape(w[n].shape), v_s[off:off + r].reshape(w[n].shape)
        off += r
    return (loss, dx.reshape(B, S, D), *[grad[n] for n in ALL_W], *[delta[n] for n in ALL_W], *[new_m[n] for n in ALL_W],
            *[new_v[n] for n in ALL_W])
```

```python
import functools

import jax
import jax.numpy as jnp
from jax import lax
from jax.experimental import pallas as pl
from jax.experimental.pallas import tpu as pltpu

f32 = jnp.float32
bf16 = jnp.bfloat16

N_DEV = 8
HEAD_DIM = 128
DILATIONS = (1, 4, 16)
N_GROUPS = 3
HEADS = 4
BLK = 128
ATTN_QKV = N_GROUPS * HEADS * HEAD_DIM
ATTN_OUT = HEADS * HEAD_DIM
CONV_W = 1024
MEM_LEN = 256
MEM_HEADS = 4
MEM_HD = 256
MEM_W = MEM_HEADS * MEM_HD
EPS = 1e-6
Q0, K0, V0 = 0, ATTN_QKV, 2 * ATTN_QKV
ZA = 3 * ATTN_QKV
CB, CC, CV, ZC = ZA + ATTN_OUT, ZA + ATTN_OUT + CONV_W, ZA + ATTN_OUT + 2 * CONV_W, ZA + ATTN_OUT + 3 * CONV_W
MQ = ZC + CONV_W
ZM = MQ + MEM_W
GT = ZM + MEM_W

ADAM_LR, ADAM_B1, ADAM_B2, ADAM_EPS, ADAM_WD, ADAM_STEP = 0.001, 0.9, 0.999, 1e-08, 0.01, 10

VMEM_LIMIT = 56 * 1024 * 1024
LANES = 128

NT_DIMS = (((1,), (1,)), ((), ()))
TN_DIMS = (((0,), (0,)), ((), ()))
NN_DIMS = (((1,), (0,)), ((), ()))


def _params(sem=None):
    return pltpu.CompilerParams(dimension_semantics=sem, vmem_limit_bytes=VMEM_LIMIT)


def _pick(n, pref, q=LANES):
    t = (min(pref, n) // q) * q
    while t >= q:
        if n % t == 0:
            return t
        t -= q
    return n


def _dot(a, b, dims):
    return lax.dot_general(a.astype(bf16), b.astype(bf16), dims, preferred_element_type=f32)


def _sigmoid(z):
    return 0.5 * jnp.tanh(0.5 * z) + 0.5


def _rstd(v):
    return lax.rsqrt(jnp.mean(v * v, axis=-1, keepdims=True) + EPS)


class _Deferred:
    def __init__(self, stage, sems, step, last, n):
        assert last >= 1
        self.stage, self.sems, self.step, self.last, self.n = stage, sems, step, last, n
        self.slot = step % 2

    def _drain(self, slot, like):
        for c in range(self.n):
            pltpu.make_async_copy(self.stage.at[slot, c], like(c), self.sems.at[slot, c]).wait()

    def begin(self, like):
        pl.when(self.step >= 2)(lambda: self._drain(self.slot, like))

    def start(self, c, dst):
        pltpu.make_async_copy(self.stage.at[self.slot, c], dst, self.sems.at[self.slot, c]).start()

    def end(self, like):
        @pl.when(self.step == self.last)
        def _():
            self._drain(self.slot, like)
            self._drain(1 - self.slot, like)


def _row_sum(v):
    hi = v.astype(bf16)
    lo = (v - hi.astype(f32)).astype(bf16)
    ones = jnp.ones((v.shape[1], v.shape[1]), bf16)
    return _dot(hi, ones, NN_DIMS) + _dot(lo, ones, NN_DIMS)


def _rstd_head(v):
    return lax.rsqrt(_row_sum(v * v) * (1.0 / v.shape[1]) + EPS)


class _Comm:
    def __init__(self, ins, out_shapes, sem_shape, copies):
        self.ins, self.out_shapes, self.sem_shape, self.copies = list(ins), list(out_shapes), sem_shape, copies

    def scratch(self):
        return [pltpu.SemaphoreType.DMA(self.sem_shape), pltpu.SemaphoreType.DMA(self.sem_shape)]


def _mm(a, b, *, mode, out_dtype, name, tm=1024, tn=1024, tk=4096, m_tiles=None, into=None, comm=None):
    if mode == "nn":
        (M, K), (K2, N) = a.shape, b.shape
    elif mode == "nt":
        (M, K), (N, K2) = a.shape, b.shape
    else:
        (K, M), (K2, N) = a.shape, b.shape
    assert K == K2
    tm, tn, tk = _pick(M, tm), _pick(N, tn), _pick(K, tk)
    nk = K // tk
    m0, mc = m_tiles if m_tiles is not None else (0, M // tm)
    o0 = 0 if into is None else m0
    aliased = into is not None and not isinstance(into, str)
    n_ci = len(comm.ins) if comm else 0
    n_co = len(comm.out_shapes) if comm else 0
    grid = (mc, N // tn, nk)
    dims = {"nn": NN_DIMS, "nt": NT_DIMS, "tn": TN_DIMS}[mode]

    def body(*refs, nk):
        a_ref, b_ref = refs[:2]
        pos = 3 if aliased else 2
        c_ins, o_ref, c_outs = refs[pos:pos + n_ci], refs[pos + n_ci], refs[pos + n_ci + 1:pos + n_ci + 1 + n_co]
        scratch = refs[pos + n_ci + 1 + n_co:]
        ids = [pl.program_id(d) for d in range(3)]
        if comm:
            sems = scratch[-2:]

            @pl.when((ids[0] == 0) & (ids[1] == 0) & (ids[2] == 0))
            def _():
                for cp in comm.copies(c_ins, c_outs, *sems):
                    cp.start()

        if nk == 1:
            o_ref[...] = _dot(a_ref[...], b_ref[...], dims).astype(o_ref.dtype)
        else:
            acc_ref, k = scratch[0], ids[2]

            @pl.when(k == 0)
            def _():
                acc_ref[...] = _dot(a_ref[...], b_ref[...], dims)

            @pl.when((k > 0) & (k < nk - 1))
            def _():
                acc_ref[...] += _dot(a_ref[...], b_ref[...], dims)

            @pl.when(k == nk - 1)
            def _():
                o_ref[...] = (acc_ref[...] + _dot(a_ref[...], b_ref[...], dims)).astype(o_ref.dtype)

        if comm:
            @pl.when((ids[0] == grid[0] - 1) & (ids[1] == grid[1] - 1) & (ids[2] == grid[2] - 1))
            def _():
                for cp in comm.copies(c_ins, c_outs, *sems):
                    cp.wait()

    if mode == "tn":
        a_spec = pl.BlockSpec((tk, tm), lambda i, j, k: (k, m0 + i))
    else:
        a_spec = pl.BlockSpec((tm, tk), lambda i, j, k: (m0 + i, k))
    if mode == "nt":
        b_spec = pl.BlockSpec((tn, tk), lambda i, j, k: (j, k))
    else:
        b_spec = pl.BlockSpec((tk, tn), lambda i, j, k: (k, j))
    hbm = pl.BlockSpec(memory_space=pl.ANY)
    res = pl.pallas_call(
        functools.partial(body, nk=nk),
        grid=grid,
        in_specs=[a_spec, b_spec] + [hbm] * (aliased + n_ci),
        out_specs=[pl.BlockSpec((tm, tn), lambda i, j, k: (o0 + i, j))] + [hbm] * n_co,
        out_shape=[jax.ShapeDtypeStruct((mc * tm if into is None else M, N), out_dtype)] + (comm.out_shapes if comm else []),
        scratch_shapes=([pltpu.VMEM((tm, tn), f32)] if nk > 1 else []) + (comm.scratch() if comm else []),
        input_output_aliases={2: 0} if aliased else {},
        compiler_params=_params(("arbitrary",) * 3 if comm else ("parallel", "parallel", "arbitrary")),
        name=name,
    )(a, b, *([into] if aliased else []), *(comm.ins if comm else []))
    return (res[0], list(res[1:])) if comm else res[0]


def _rms_fwd(x, g, name):
    T, D = x.shape
    tm = _pick(T, 256, 8)

    def body(x_ref, g_ref, h_ref):
        xv = x_ref[...]
        h_ref[...] = (xv * _rstd(xv) * g_ref[...]).astype(bf16)

    return pl.pallas_call(
        body, grid=(T // tm,),
        in_specs=[pl.BlockSpec((tm, D), lambda i: (i, 0)), pl.BlockSpec((1, D), lambda i: (0, 0))],
        out_specs=pl.BlockSpec((tm, D), lambda i: (i, 0)),
        out_shape=jax.ShapeDtypeStruct((T, D), bf16),
        compiler_params=_params(("parallel",)), name=name,
    )(x, g)


def _rms_bwd(x, g, dh, dy, name):
    T, D = x.shape
    tm = _pick(T, 256, 8)
    with_dx = dy is not None

    def body(*refs):
        if with_dx:
            x_ref, g_ref, dh_ref, dy_ref, dx_ref, dg_ref = refs
        else:
            x_ref, g_ref, dh_ref, dg_ref = refs
        xv = x_ref[...]
        r = _rstd(xv)
        xh = xv * r
        dhv = dh_ref[...]
        part = jnp.sum(dhv * xh, axis=0, keepdims=True)

        @pl.when(pl.program_id(0) == 0)
        def _():
            dg_ref[...] = part

        @pl.when(pl.program_id(0) > 0)
        def _():
            dg_ref[...] += part

        if with_dx:
            dxh = dhv * g_ref[...]
            dx_ref[...] = dy_ref[...] + r * (dxh - xh * jnp.mean(dxh * xh, axis=-1, keepdims=True))

    row = pl.BlockSpec((tm, D), lambda i: (i, 0))
    vec = pl.BlockSpec((1, D), lambda i: (0, 0))
    if with_dx:
        return pl.pallas_call(
            body, grid=(T // tm,), in_specs=[row, vec, row, row], out_specs=[row, vec],
            out_shape=[jax.ShapeDtypeStruct((T, D), f32), jax.ShapeDtypeStruct((1, D), f32)],
            compiler_params=_params(("arbitrary",)), name=name,
        )(x, g, dh, dy)
    return pl.pallas_call(
        body, grid=(T // tm,), in_specs=[row, vec, row], out_specs=vec,
        out_shape=jax.ShapeDtypeStruct((1, D), f32),
        compiler_params=_params(("arbitrary",)), name=name,
    )(x, g, dh)


UNIT_UNROLL = 4


def _rows(start, size, stride):
    return pl.ds(start, size) if stride == 1 else pl.ds(start, size, stride=stride)


def _attn_units(S, d, first, prev):
    nb = S // d // BLK

    def do_first(r, c):
        first(_rows(r, BLK, d))
        return c

    lax.fori_loop(0, d, do_first, 0, unroll=min(d, UNIT_UNROLL))
    if nb > 1:
        def do_prev(u, c):
            r = u // (nb - 1)
            b = u % (nb - 1) + 1
            base = r + d * b * BLK
            prev(_rows(base, BLK, d), _rows(base - d * BLK, 2 * BLK, d))
            return c

        lax.fori_loop(0, d * (nb - 1), do_prev, 0, unroll=UNIT_UNROLL)


def _band_bias(nkeys):
    i = lax.broadcasted_iota(jnp.int32, (BLK, nkeys), 0)
    j = lax.broadcasted_iota(jnp.int32, (BLK, nkeys), 1)
    ok = (j <= i) if nkeys == BLK else ((j >= i) & (j <= i + BLK))
    return jnp.where(ok, 0.0, -jnp.inf).astype(f32)


def _normalize_into(src_ref, gain, dst_ref, S):
    for c in range(S // 256):
        rows = pl.ds(c * 256, 256)
        v = src_ref[rows, :]
        dst_ref[rows, :] = v * _rstd_head(v) * gain


def _attn_fwd(proj, gq, gk, B, S, comm=None):
    T, NC = proj.shape
    scale = HEAD_DIM ** -0.5
    n_ci = len(comm.ins) if comm else 0
    n_co = len(comm.out_shapes) if comm else 0

    def body(*refs):
        q_ref, k_ref, v_ref, z_ref, gq_ref, gk_ref = refs[:6]
        c_ins = refs[6:6 + n_ci]
        ag_ref, a_ref, lse_ref = refs[6 + n_ci:9 + n_ci]
        c_outs = refs[9 + n_ci:9 + n_ci + n_co]
        qs, ks, o0, o1, o2, l0, l1, l2, bias1, bias2 = refs[9 + n_ci + n_co:19 + n_ci + n_co]
        sems = refs[19 + n_ci + n_co:]
        g = pl.program_id(2)
        if comm:
            @pl.when((pl.program_id(0) == 0) & (pl.program_id(1) == 0) & (g == 0))
            def _():
                for cp in comm.copies(c_ins, c_outs, *sems):
                    cp.start()

        bias1[...] = _band_bias(BLK)
        bias2[...] = _band_bias(2 * BLK)
        _normalize_into(q_ref, gq_ref[pl.ds(g, 1), :], qs, S)
        _normalize_into(k_ref, gk_ref[pl.ds(g, 1), :], ks, S)
        o_s, l_s = (o0, o1, o2), (l0, l1, l2)

        def run(gi):
            def unit(qr, kr, nkeys):
                s = _dot(qs[qr, :], ks[kr, :], NT_DIMS) * scale + (bias1 if nkeys == BLK else bias2)[...]
                m = jnp.max(s, axis=-1, keepdims=True)
                p = jnp.exp(s - m)
                den = jnp.sum(p, axis=-1, keepdims=True)
                o_s[gi][qr, :] = _dot(p, v_ref[kr, :], NN_DIMS) * (1.0 / den)
                l_s[gi][qr, :] = jnp.broadcast_to(m + jnp.log(den), (BLK, HEAD_DIM))

            _attn_units(S, DILATIONS[gi], lambda qr: unit(qr, qr, BLK), lambda qr, kr: unit(qr, kr, 2 * BLK))

        for gi in range(N_GROUPS):
            pl.when(g == gi)(functools.partial(run, gi))

        @pl.when(g == N_GROUPS - 1)
        def _():
            for c in range(S // 256):
                rows = pl.ds(c * 256, 256)
                la, lb, lc = l0[rows, :], l1[rows, :], l2[rows, :]
                m = jnp.maximum(jnp.maximum(la, lb), lc)
                wa, wb, wc = jnp.exp(la - m), jnp.exp(lb - m), jnp.exp(lc - m)
                tot = wa + wb + wc
                a = (wa / tot) * o0[rows, :] + (wb / tot) * o1[rows, :] + (wc / tot) * o2[rows, :]
                z = z_ref[rows, :]
                a_ref[rows, :] = a
                lse_ref[rows, :] = m + jnp.log(tot)
                ag_ref[rows, :] = (a * (z * _sigmoid(z))).astype(bf16)

        if comm:
            @pl.when((pl.program_id(0) == B - 1) & (pl.program_id(1) == HEADS - 1) & (g == N_GROUPS - 1))
            def _():
                for cp in comm.copies(c_ins, c_outs, *sems):
                    cp.wait()

    def slab(col0):
        return pl.BlockSpec((S, HEAD_DIM), lambda b, h, g: (b, col0 // HEAD_DIM + g * HEADS + h))

    gain = pl.BlockSpec((N_GROUPS, HEAD_DIM), lambda b, h, g: (0, 0))
    out = pl.BlockSpec((S, HEAD_DIM), lambda b, h, g: (b, h))
    hbm = pl.BlockSpec(memory_space=pl.ANY)
    res = pl.pallas_call(
        body, grid=(B, HEADS, N_GROUPS),
        in_specs=[slab(Q0), slab(K0), slab(V0), pl.BlockSpec((S, HEAD_DIM), lambda b, h, g: (b, ZA // HEAD_DIM + h)), gain, gain]
        + [hbm] * n_ci,
        out_specs=[out, out, out] + [hbm] * n_co,
        out_shape=[jax.ShapeDtypeStruct((T, ATTN_OUT), bf16), jax.ShapeDtypeStruct((T, ATTN_OUT), f32),
                   jax.ShapeDtypeStruct((T, ATTN_OUT), f32)] + (comm.out_shapes if comm else []),
        scratch_shapes=[pltpu.VMEM((S, HEAD_DIM), f32)] * 8 + [pltpu.VMEM((BLK, BLK), f32), pltpu.VMEM((BLK, 2 * BLK), f32)]
        + (comm.scratch() if comm else []),
        compiler_params=_params(("arbitrary", "arbitrary", "arbitrary")), name="attn_fwd",
    )(proj, proj, proj, proj, gq, gk, *(comm.ins if comm else []))
    return res[0], res[1], res[2], list(res[3:])


def _rms_bwd_rows(raw, gain, dn, on_mxu=True):
    r = _rstd_head(raw) if on_mxu else _rstd(raw)
    xh = raw * r
    dxh = dn * gain
    if on_mxu:
        mean = _row_sum(dxh * xh) * (1.0 / raw.shape[1])
    else:
        mean = jnp.mean(dxh * xh, axis=-1, keepdims=True)
    return r * (dxh - xh * mean), jnp.sum(dn * xh, axis=0, keepdims=True)


def _attn_bwd(proj, gq, gk, a_comb, lse, dag, dproj, B, S):
    T, NC = proj.shape
    scale = HEAD_DIM ** -0.5

    def body(q_ref, k_ref, v_ref, z_ref, gq_ref, gk_ref, a_ref, lse_ref, dag_ref, dproj_in, dproj_ref, dgq_ref, dgk_ref,
             qs, ks, da_s, dl_s, dq_s, dk_s, dv_s, bias1, bias2, stage, dz_stage, sem, dz_sem):
        b, h, g = pl.program_id(0), pl.program_id(1), pl.program_id(2)
        bias1[...] = _band_bias(BLK)
        bias2[...] = _band_bias(2 * BLK)
        gq_row, gk_row = gq_ref[pl.ds(g, 1), :], gk_ref[pl.ds(g, 1), :]
        _normalize_into(q_ref, gq_row, qs, S)
        _normalize_into(k_ref, gk_row, ks, S)

        def dst(c):
            return dproj_ref.at[pl.ds(b * S, S), pl.ds((Q0, K0, V0)[c] + (g * HEADS + h) * HEAD_DIM, HEAD_DIM)]

        out = _Deferred(stage, sem, (b * HEADS + h) * N_GROUPS + g, B * HEADS * N_GROUPS - 1, 3)
        out.begin(dst)

        @pl.when((b == 0) & (h == 0) & (g == 0))
        def _():
            dgq_ref[...] = jnp.zeros_like(dgq_ref)
            dgk_ref[...] = jnp.zeros_like(dgk_ref)

        @pl.when(g == 0)
        def _():
            for c in range(S // 256):
                rows = pl.ds(c * 256, 256)
                z, a, dg_ = z_ref[rows, :], a_ref[rows, :], dag_ref[rows, :]
                sg = _sigmoid(z)
                da = dg_ * (z * sg)
                da_s[rows, :] = da
                dl_s[rows, :] = _row_sum(da * a)
                dz_stage[rows, :] = (dg_ * a * (sg * (1.0 + z * (1.0 - sg)))).astype(bf16)
            cp = pltpu.make_async_copy(dz_stage, dproj_ref.at[pl.ds(b * S, S), pl.ds(ZA + h * HEAD_DIM, HEAD_DIM)], dz_sem)
            cp.start()
            cp.wait()

        dk_s[...] = jnp.zeros_like(dk_s)
        dv_s[...] = jnp.zeros_like(dv_s)

        def run(gi):
            def unit(qr, kr, nkeys):
                q, k, v = qs[qr, :], ks[kr, :], v_ref[kr, :]
                s = _dot(q, k, NT_DIMS) * scale
                p = jnp.exp(s + (bias1 if nkeys == BLK else bias2)[...] - lse_ref[qr, :][:, :1])
                da = da_s[qr, :]
                dp = _dot(da, v, NT_DIMS)
                ds = p * (dp - dl_s[qr, :][:, :1]) * scale
                dq_s[qr, :] = _dot(ds, k, NN_DIMS)
                dk_s[kr, :] += _dot(ds, q, TN_DIMS)
                dv_s[kr, :] += _dot(p, da, TN_DIMS)

            _attn_units(S, DILATIONS[gi], lambda qr: unit(qr, qr, BLK), lambda qr, kr: unit(qr, kr, 2 * BLK))

        for gi in range(N_GROUPS):
            pl.when(g == gi)(functools.partial(run, gi))

        gq_acc = jnp.zeros((1, HEAD_DIM), f32)
        gk_acc = jnp.zeros((1, HEAD_DIM), f32)
        for c in range(S // 256):
            rows = pl.ds(c * 256, 256)
            dq, gq_p = _rms_bwd_rows(q_ref[rows, :], gq_row, dq_s[rows, :])
            dk, gk_p = _rms_bwd_rows(k_ref[rows, :], gk_row, dk_s[rows, :])
            gq_acc, gk_acc = gq_acc + gq_p, gk_acc + gk_p
            stage[out.slot, 0, rows, :] = dq.astype(bf16)
            stage[out.slot, 1, rows, :] = dk.astype(bf16)
            stage[out.slot, 2, rows, :] = dv_s[rows, :].astype(bf16)
        dgq_ref[pl.ds(g, 1), :] += gq_acc
        dgk_ref[pl.ds(g, 1), :] += gk_acc
        for c in range(3):
            out.start(c, dst(c))
        out.end(dst)

    def slab(col0):
        return pl.BlockSpec((S, HEAD_DIM), lambda b, h, g: (b, col0 // HEAD_DIM + g * HEADS + h))

    gain = pl.BlockSpec((N_GROUPS, HEAD_DIM), lambda b, h, g: (0, 0))
    per_slot = pl.BlockSpec((S, HEAD_DIM), lambda b, h, g: (b, h))
    return pl.pallas_call(
        body, grid=(B, HEADS, N_GROUPS),
        in_specs=[slab(Q0), slab(K0), slab(V0), pl.BlockSpec((S, HEAD_DIM), lambda b, h, g: (b, ZA // HEAD_DIM + h)), gain, gain,
                  per_slot, per_slot, per_slot, pl.BlockSpec(memory_space=pl.ANY)],
        out_specs=[pl.BlockSpec(memory_space=pl.ANY), gain, gain],
        out_shape=[jax.ShapeDtypeStruct(dproj.shape, dproj.dtype), jax.ShapeDtypeStruct((N_GROUPS, HEAD_DIM), f32),
                   jax.ShapeDtypeStruct((N_GROUPS, HEAD_DIM), f32)],
        scratch_shapes=[pltpu.VMEM((S, HEAD_DIM), f32)] * 7 + [pltpu.VMEM((BLK, BLK), f32), pltpu.VMEM((BLK, 2 * BLK), f32),
                                                                pltpu.VMEM((2, 3, S, HEAD_DIM), bf16), pltpu.VMEM((S, HEAD_DIM), bf16),
                                                                pltpu.SemaphoreType.DMA((2, 3)), pltpu.SemaphoreType.DMA],
        input_output_aliases={9: 0},
        compiler_params=_params(("arbitrary", "arbitrary", "arbitrary")), name="attn_bwd",
    )(proj, proj, proj, proj, gq, gk, a_comb, lse, dag, dproj)


def _conv_fwd(proj, conv_w, B, S):
    T, NC = proj.shape
    tc = LANES

    def body(cb_ref, cc_ref, cv_ref, z_ref, w_ref, c_ref, ub):
        u = cc_ref[...] * cv_ref[...]
        ub[0:8, :] = jnp.zeros((8, tc), f32)
        ub[8:8 + S, :] = u
        w = w_ref[...]
        y = w[0:1] * u + w[1:2] * ub[pl.ds(7, S), :] + w[2:3] * ub[pl.ds(6, S), :]
        z = z_ref[...]
        c_ref[...] = (cb_ref[...] * y * (z * _sigmoid(z))).astype(bf16)

    def seg(col0):
        return pl.BlockSpec((S, tc), lambda j, b: (b, col0 // tc + j))

    return pl.pallas_call(
        body, grid=(CONV_W // tc, B),
        in_specs=[seg(CB), seg(CC), seg(CV), seg(ZC), pl.BlockSpec((3, tc), lambda j, b: (0, j))],
        out_specs=pl.BlockSpec((S, tc), lambda j, b: (b, j)),
        out_shape=jax.ShapeDtypeStruct((T, CONV_W), bf16),
        scratch_shapes=[pltpu.VMEM((S + 8, tc), f32)],
        compiler_params=_params(("parallel", "parallel")), name="conv_fwd",
    )(proj, proj, proj, proj, conv_w)


def _conv_bwd(proj, conv_w, dc, dproj, B, S):
    T, NC = proj.shape
    tc = LANES

    def body(cb_ref, cc_ref, cv_ref, z_ref, w_ref, dc_ref, dproj_in, dproj_ref, dw_ref, ub, db, stage, sem):
        j, b = pl.program_id(0), pl.program_id(1)

        def dst(c):
            return dproj_ref.at[pl.ds(b * S, S), pl.ds((CB, CC, CV, ZC)[c] + j * tc, tc)]

        out = _Deferred(stage, sem, j * B + b, (CONV_W // tc) * B - 1, 4)
        out.begin(dst)
        cb, cc, cv, z, dcv = cb_ref[...], cc_ref[...], cv_ref[...], z_ref[...], dc_ref[...]
        u = cc * cv
        ub[0:8, :] = jnp.zeros((8, tc), f32)
        ub[8:8 + S, :] = u
        u1, u2 = ub[pl.ds(7, S), :], ub[pl.ds(6, S), :]
        w = w_ref[...]
        y = w[0:1] * u + w[1:2] * u1 + w[2:3] * u2
        sg = _sigmoid(z)
        si = z * sg
        dyv = dcv * cb * si
        db[0:S, :] = dyv
        db[S:S + 8, :] = jnp.zeros((8, tc), f32)
        du = w[0:1] * dyv + w[1:2] * db[pl.ds(1, S), :] + w[2:3] * db[pl.ds(2, S), :]
        stage[out.slot, 0] = (dcv * y * si).astype(bf16)
        stage[out.slot, 1] = (du * cv).astype(bf16)
        stage[out.slot, 2] = (du * cc).astype(bf16)
        stage[out.slot, 3] = (dcv * cb * y * (sg * (1.0 + z * (1.0 - sg)))).astype(bf16)
        for c in range(4):
            out.start(c, dst(c))
        part = jnp.concatenate([jnp.sum(dyv * u, axis=0, keepdims=True), jnp.sum(dyv * u1, axis=0, keepdims=True),
                                jnp.sum(dyv * u2, axis=0, keepdims=True)], axis=0)

        @pl.when(b == 0)
        def _():
            dw_ref[...] = part

        @pl.when(b > 0)
        def _():
            dw_ref[...] += part

        out.end(dst)

    def seg(col0):
        return pl.BlockSpec((S, tc), lambda j, b: (b, col0 // tc + j))

    return pl.pallas_call(
        body, grid=(CONV_W // tc, B),
        in_specs=[seg(CB), seg(CC), seg(CV), seg(ZC), pl.BlockSpec((3, tc), lambda j, b: (0, j)),
                  pl.BlockSpec((S, tc), lambda j, b: (b, j)), pl.BlockSpec(memory_space=pl.ANY)],
        out_specs=[pl.BlockSpec(memory_space=pl.ANY), pl.BlockSpec((3, tc), lambda j, b: (0, j))],
        out_shape=[jax.ShapeDtypeStruct(dproj.shape, dproj.dtype), jax.ShapeDtypeStruct((3, CONV_W), f32)],
        scratch_shapes=[pltpu.VMEM((S + 8, tc), f32), pltpu.VMEM((S + 8, tc), f32), pltpu.VMEM((2, 4, S, tc), bf16),
                        pltpu.SemaphoreType.DMA((2, 4))],
        input_output_aliases={6: 0},
        compiler_params=_params(("arbitrary", "arbitrary")), name="conv_bwd",
    )(proj, proj, proj, proj, conv_w, dc, dproj)


MEM_CHUNK = 256


def _mem_fwd(proj, mkv, gq, gk, B, S):
    T, NC = proj.shape
    scale = MEM_HD ** -0.5

    def body(q_ref, z_ref, k_ref, v_ref, gq_ref, gk_ref, o_ref):
        kv = k_ref[...]
        kn = (kv * _rstd_head(kv) * gk_ref[...]).astype(bf16)
        vv = v_ref[...].astype(bf16)

        def chunk(c, carry):
            rows = pl.ds(pl.multiple_of(c * MEM_CHUNK, MEM_CHUNK), MEM_CHUNK)
            q = q_ref[rows, :]
            qn = q * _rstd(q) * gq_ref[...]
            s = _dot(qn, kn, NT_DIMS) * scale
            p = jnp.exp(s - jnp.max(s, axis=-1, keepdims=True))
            p = p / jnp.sum(p, axis=-1, keepdims=True)
            z = z_ref[rows, :]
            o_ref[rows, :] = (_dot(p, vv, NN_DIMS) * (z * _sigmoid(z))).astype(bf16)
            return carry

        lax.fori_loop(0, S // MEM_CHUNK, chunk, 0)

    gain = pl.BlockSpec((1, MEM_HD), lambda b, h: (0, 0))
    return pl.pallas_call(
        body, grid=(B, MEM_HEADS),
        in_specs=[pl.BlockSpec((S, MEM_HD), lambda b, h: (b, MQ // MEM_HD + h)),
                  pl.BlockSpec((S, MEM_HD), lambda b, h: (b, ZM // MEM_HD + h)),
                  pl.BlockSpec((MEM_LEN, MEM_HD), lambda b, h: (b, h)),
                  pl.BlockSpec((MEM_LEN, MEM_HD), lambda b, h: (b, MEM_HEADS + h)), gain, gain],
        out_specs=pl.BlockSpec((S, MEM_HD), lambda b, h: (b, h)),
        out_shape=jax.ShapeDtypeStruct((T, MEM_W), bf16),
        compiler_params=_params(("parallel", "parallel")), name="mem_fwd",
    )(proj, proj, mkv, mkv, gq, gk)


def _mem_bwd(proj, mkv, gq, gk, dmo, dproj, B, S):
    T, NC = proj.shape
    scale = MEM_HD ** -0.5

    def body(q_ref, z_ref, k_ref, v_ref, gq_ref, gk_ref, dmo_ref, dproj_in, dproj_ref, dmk_ref, dmv_ref, dgq_ref, dgk_ref,
             dkn_s, dv_s, gq_s, stage, sem):
        b, h = pl.program_id(0), pl.program_id(1)

        def dst(c):
            return dproj_ref.at[pl.ds(b * S, S), pl.ds((MQ, ZM)[c] + h * MEM_HD, MEM_HD)]

        out = _Deferred(stage, sem, b * MEM_HEADS + h, B * MEM_HEADS - 1, 2)
        out.begin(dst)
        kv = k_ref[...]
        kn = (kv * _rstd_head(kv) * gk_ref[...]).astype(bf16)
        vv = v_ref[...].astype(bf16)
        dkn_s[...] = jnp.zeros_like(dkn_s)
        dv_s[...] = jnp.zeros_like(dv_s)
        gq_s[...] = jnp.zeros_like(gq_s)

        def chunk(c, carry):
            rows = pl.ds(pl.multiple_of(c * MEM_CHUNK, MEM_CHUNK), MEM_CHUNK)
            q = q_ref[rows, :]
            qn = q * _rstd(q) * gq_ref[...]
            s = _dot(qn, kn, NT_DIMS) * scale
            p = jnp.exp(s - jnp.max(s, axis=-1, keepdims=True))
            p = p / jnp.sum(p, axis=-1, keepdims=True)
            mo = _dot(p, vv, NN_DIMS)
            z, dg_ = z_ref[rows, :], dmo_ref[rows, :]
            sg = _sigmoid(z)
            do = dg_ * (z * sg)
            stage[out.slot, 1, rows, :] = (dg_ * mo * (sg * (1.0 + z * (1.0 - sg)))).astype(bf16)
            dp = _dot(do, vv, NT_DIMS)
            ds = p * (dp - jnp.sum(do * mo, axis=-1, keepdims=True)) * scale
            dq, gq_p = _rms_bwd_rows(q, gq_ref[...], _dot(ds, kn, NN_DIMS), on_mxu=False)
            stage[out.slot, 0, rows, :] = dq.astype(bf16)
            gq_s[...] += gq_p
            dkn_s[...] += _dot(ds, qn, TN_DIMS)
            dv_s[...] += _dot(p, do, TN_DIMS)
            return carry

        lax.fori_loop(0, S // MEM_CHUNK, chunk, 0)
        for c in range(2):
            out.start(c, dst(c))
        dk, gk_p = _rms_bwd_rows(kv, gk_ref[...], dkn_s[...])
        dmk_ref[...] = dk
        dmv_ref[...] = dv_s[...]

        @pl.when((b == 0) & (h == 0))
        def _():
            dgq_ref[...] = gq_s[...]
            dgk_ref[...] = gk_p

        @pl.when((b > 0) | (h > 0))
        def _():
            dgq_ref[...] += gq_s[...]
            dgk_ref[...] += gk_p

        out.end(dst)

    gain = pl.BlockSpec((1, MEM_HD), lambda b, h: (0, 0))
    kvb = pl.BlockSpec((MEM_LEN, MEM_HD), lambda b, h: (b, h))
    return pl.pallas_call(
        body, grid=(B, MEM_HEADS),
        in_specs=[pl.BlockSpec((S, MEM_HD), lambda b, h: (b, MQ // MEM_HD + h)),
                  pl.BlockSpec((S, MEM_HD), lambda b, h: (b, ZM // MEM_HD + h)),
                  kvb, pl.BlockSpec((MEM_LEN, MEM_HD), lambda b, h: (b, MEM_HEADS + h)), gain, gain,
                  pl.BlockSpec((S, MEM_HD), lambda b, h: (b, h)), pl.BlockSpec(memory_space=pl.ANY)],
        out_specs=[pl.BlockSpec(memory_space=pl.ANY), kvb, kvb, gain, gain],
        out_shape=[jax.ShapeDtypeStruct(dproj.shape, dproj.dtype), jax.ShapeDtypeStruct((B * MEM_LEN, MEM_W), f32),
                   jax.ShapeDtypeStruct((B * MEM_LEN, MEM_W), f32), jax.ShapeDtypeStruct((1, MEM_HD), f32),
                   jax.ShapeDtypeStruct((1, MEM_HD), f32)],
        scratch_shapes=[pltpu.VMEM((MEM_LEN, MEM_HD), f32), pltpu.VMEM((MEM_LEN, MEM_HD), f32), pltpu.VMEM((1, MEM_HD), f32),
                        pltpu.VMEM((2, 2, S, MEM_HD), bf16), pltpu.SemaphoreType.DMA((2, 2))],
        input_output_aliases={7: 0},
        compiler_params=_params(("arbitrary", "arbitrary")), name="mem_bwd",
    )(proj, proj, mkv, mkv, gq, gk, dmo, dproj)


def _merge_fwd(proj, ag, cg, mg, wa, wc, wm):
    T, NC = proj.shape
    D = wa.shape[1]
    tm, tn = _pick(T, 1024), _pick(D, 512)
    assert GT % tn == 0

    def body(ag_ref, cg_ref, mg_ref, wa_ref, wc_ref, wm_ref, g0_ref, g1_ref, g2_ref, mer_ref, ba_ref, bc_ref, bm_ref):
        ba = _dot(ag_ref[...], wa_ref[...], NN_DIMS)
        bc = _dot(cg_ref[...], wc_ref[...], NN_DIMS)
        bm = _dot(mg_ref[...], wm_ref[...], NN_DIMS)
        mer_ref[...] = (_sigmoid(g0_ref[...]) * ba + _sigmoid(g1_ref[...]) * bc + _sigmoid(g2_ref[...]) * bm).astype(bf16)
        ba_ref[...] = ba.astype(bf16)
        bc_ref[...] = bc.astype(bf16)
        bm_ref[...] = bm.astype(bf16)

    def act(w):
        return pl.BlockSpec((tm, w), lambda i, j: (i, 0))

    def wt(k):
        return pl.BlockSpec((k, tn), lambda i, j: (0, j))

    def gate(n):
        return pl.BlockSpec((tm, tn), lambda i, j: (i, (GT + n * D) // tn + j))

    out = pl.BlockSpec((tm, tn), lambda i, j: (i, j))
    return pl.pallas_call(
        body, grid=(T // tm, D // tn),
        in_specs=[act(ATTN_OUT), act(CONV_W), act(MEM_W), wt(ATTN_OUT), wt(CONV_W), wt(MEM_W), gate(0), gate(1), gate(2)],
        out_specs=[out] * 4, out_shape=[jax.ShapeDtypeStruct((T, D), bf16)] * 4,
        compiler_params=_params(("parallel", "parallel")), name="merge_fwd",
    )(ag, cg, mg, wa, wc, wm, proj, proj, proj)


def _out_loss(merged, w_out, x, tgt):
    T, D = x.shape
    tm, tn = _pick(T, 512), _pick(D, 512)

    def body(m_ref, w_ref, x_ref, t_ref, dy_ref, dyb_ref, lp_ref):
        e = x_ref[...] + _dot(m_ref[...], w_ref[...], NN_DIMS) - t_ref[...]
        dy = e / D
        dy_ref[...] = dy
        dyb_ref[...] = dy.astype(bf16)
        part = jnp.full((1, 8, LANES), jnp.sum(e * e), f32)

        @pl.when(pl.program_id(1) == 0)
        def _():
            lp_ref[...] = part

        @pl.when(pl.program_id(1) > 0)
        def _():
            lp_ref[...] += part

    tile = pl.BlockSpec((tm, tn), lambda i, j: (i, j))
    return pl.pallas_call(
        body, grid=(T // tm, D // tn),
        in_specs=[pl.BlockSpec((tm, D), lambda i, j: (i, 0)), pl.BlockSpec((D, tn), lambda i, j: (0, j)), tile, tile],
        out_specs=[tile, tile, pl.BlockSpec((1, 8, LANES), lambda i, j: (i, 0, 0))],
        out_shape=[jax.ShapeDtypeStruct((T, D), f32), jax.ShapeDtypeStruct((T, D), bf16),
                   jax.ShapeDtypeStruct((T // tm, 8, LANES), f32)],
        compiler_params=_params(("parallel", "arbitrary")), name="out_loss",
    )(merged, w_out, x, tgt)


def _merge_bwd(proj, dyb, w_out, ba, bc, bm):
    T, NC = proj.shape
    D = w_out.shape[0]
    tm, tn = _pick(T, 512), _pick(D, 512)
    assert GT % tn == 0

    def body(dy_ref, w_ref, ba_ref, bc_ref, bm_ref, g0_ref, g1_ref, g2_ref, da_ref, dc_ref, dm_ref, dproj_ref, stage, sem):
        i, j = pl.program_id(0), pl.program_id(1)

        def dst(n):
            return dproj_ref.at[pl.ds(i * tm, tm), pl.ds(GT + n * D + j * tn, tn)]

        out = _Deferred(stage, sem, i * (D // tn) + j, (T // tm) * (D // tn) - 1, 3)
        out.begin(dst)
        dmer = _dot(dy_ref[...], w_ref[...], NT_DIMS)
        for n, (g_ref, br_ref, o_ref) in enumerate(((g0_ref, ba_ref, da_ref), (g1_ref, bc_ref, dc_ref), (g2_ref, bm_ref, dm_ref))):
            sg = _sigmoid(g_ref[...])
            o_ref[...] = (sg * dmer).astype(bf16)
            stage[out.slot, n] = (dmer * br_ref[...].astype(f32) * (sg * (1.0 - sg))).astype(bf16)
            out.start(n, dst(n))
        out.end(dst)

    tile = pl.BlockSpec((tm, tn), lambda i, j: (i, j))

    def gate(n):
        return pl.BlockSpec((tm, tn), lambda i, j: (i, (GT + n * D) // tn + j))

    return pl.pallas_call(
        body, grid=(T // tm, D // tn),
        in_specs=[pl.BlockSpec((tm, D), lambda i, j: (i, 0)), pl.BlockSpec((tn, D), lambda i, j: (j, 0)), tile, tile, tile,
                  gate(0), gate(1), gate(2)],
        out_specs=[tile, tile, tile, pl.BlockSpec(memory_space=pl.ANY)],
        out_shape=[jax.ShapeDtypeStruct((T, D), bf16)] * 3 + [jax.ShapeDtypeStruct((T, NC), bf16)],
        scratch_shapes=[pltpu.VMEM((2, 3, tm, tn), bf16), pltpu.SemaphoreType.DMA((2, 3))],
        compiler_params=_params(("arbitrary", "arbitrary")), name="merge_bwd",
    )(dyb, w_out, ba, bc, bm, proj, proj, proj)


def _fwd_bwd_head(x2, mem2, t2, proj, attn, B, S, mem_norm_g, attn_q_norm, attn_k_norm, conv_w, mem_q_norm, mem_k_norm,
                  w_kv, w_a, w_c, w_m, w_out):
    D = x2.shape[1]
    mng = mem_norm_g.reshape(1, D)
    mqn, mkn = mem_q_norm.reshape(1, MEM_HD), mem_k_norm.reshape(1, MEM_HD)
    ag, a_comb, lse = attn

    mh = _rms_fwd(mem2, mng, "rms_mem")
    mkv = _mm(mh, w_kv, mode="nn", out_dtype=f32, name="mkv")
    cg = _conv_fwd(proj, conv_w, B, S)
    mg = _mem_fwd(proj, mkv, mqn, mkn, B, S)
    merged, ba, bc, bm = _merge_fwd(proj, ag, cg, mg, w_a, w_c, w_m)
    dy, dyb, lp = _out_loss(merged, w_out, x2, t2)
    sq_err = jnp.sum(lp[:, 0, 0])

    g = {}
    g["w_out"] = _mm(merged, dyb, mode="tn", out_dtype=f32, name="dw_out")
    dba, dbc, dbm, dproj = _merge_bwd(proj, dyb, w_out, ba, bc, bm)
    g["w_br_attn"] = _mm(ag, dba, mode="tn", out_dtype=f32, name="dw_br_attn")
    g["w_br_conv"] = _mm(cg, dbc, mode="tn", out_dtype=f32, name="dw_br_conv")
    g["w_br_mem"] = _mm(mg, dbm, mode="tn", out_dtype=f32, name="dw_br_mem")
    dag = _mm(dba, w_a, mode="nt", out_dtype=f32, name="d_attn_out")
    dcg = _mm(dbc, w_c, mode="nt", out_dtype=f32, name="d_conv_out")
    dmg = _mm(dbm, w_m, mode="nt", out_dtype=f32, name="d_mem_out")
    dproj, g["attn_q_norm"], g["attn_k_norm"] = _attn_bwd(proj, attn_q_norm, attn_k_norm, a_comb, lse, dag, dproj, B, S)
    dproj, g["conv_w"] = _conv_bwd(proj, conv_w, dcg, dproj, B, S)
    dproj, dmk, dmv, dgmq, dgmk = _mem_bwd(proj, mkv, mqn, mkn, dmg, dproj, B, S)
    g["mem_q_norm"], g["mem_k_norm"] = dgmq.reshape(MEM_HD), dgmk.reshape(MEM_HD)
    dmkv = jnp.concatenate([dmk, dmv], axis=1)
    dmh = _mm(dmkv, w_kv, mode="nt", out_dtype=f32, name="d_mem_h")
    g["mem_norm_g"] = _rms_bwd(mem2, mng, dmh, None, "rms_mem_bwd").reshape(D)
    return sq_err, g, dict(dy=dy, dproj=dproj, mh=mh, dmkv=dmkv)


MESH = pl.DeviceIdType.MESH
HBM = pl.BlockSpec(memory_space=pl.ANY)


def _me():
    x, y, c = lax.axis_index("x"), lax.axis_index("y"), lax.axis_index("c")
    return (x, y, c), 4 * x + 2 * y + c


def _peer(k):
    (x, y, c), _ = _me()
    p = (1 - x if k & 4 else x, 1 - y if k & 2 else y, 1 - c if k & 1 else c)
    return p, 4 * p[0] + 2 * p[1] + p[2]


def _window(ref, axis, size, idx):
    if axis is None:
        return ref.at[idx]
    if axis == 0:
        return ref.at[pl.ds(idx * size, size), :]
    return ref.at[:, pl.ds(idx * size, size)]


def _full_shape(s, axis):
    if axis is None:
        return (N_DEV,) + s.shape
    return tuple(N_DEV * d if i == axis else d for i, d in enumerate(s.shape))


OTHER_CHIPS = (4, 2, 6)


def _spread_comm(shards, axes):
    n = len(shards)

    def copies(ins, outs, send_sems, recv_sems, base=0):
        _, me = _me()
        out = []
        for a in range(n):
            mine = _window(outs[a], axes[a], None if axes[a] is None else shards[a].shape[axes[a]], me)
            out.append(pltpu.make_async_copy(ins[a], mine, send_sems.at[base + a, N_CHIP]))
            for k, dist in enumerate((1,) + OTHER_CHIPS):
                dev, _ = _peer(dist)
                out.append(pltpu.make_async_remote_copy(
                    src_ref=ins[a], dst_ref=mine, send_sem=send_sems.at[base + a, k], recv_sem=recv_sems.at[base + a, k],
                    device_id=dev, device_id_type=MESH))
        return out

    shapes = [jax.ShapeDtypeStruct(_full_shape(s, ax), s.dtype) for s, ax in zip(shards, axes)]
    return _Comm(shards, shapes, (n, N_CHIP + 1), copies)


def _forward_blocks(fulls, axes):
    n = len(fulls)
    sizes = [None if ax is None else f.shape[ax] // N_DEV for f, ax in zip(fulls, axes)]

    def body(*refs):
        outs = refs[n:2 * n]
        send_sems, recv_sems = refs[2 * n:]
        sibling, _ = _peer(1)
        copies = []
        for j, dist in enumerate(OTHER_CHIPS):
            _, held = _peer(dist)
            for a in range(n):
                block = _window(outs[a], axes[a], sizes[a], held)
                copies.append(pltpu.make_async_remote_copy(
                    src_ref=block, dst_ref=block, send_sem=send_sems.at[a, j], recv_sem=recv_sems.at[a, j],
                    device_id=sibling, device_id_type=MESH))
        for cp in copies:
            cp.start()
        for cp in copies:
            cp.wait()

    return pl.pallas_call(
        body, in_specs=[HBM] * n, out_specs=[HBM] * n,
        out_shape=[jax.ShapeDtypeStruct(f.shape, f.dtype) for f in fulls],
        scratch_shapes=[pltpu.SemaphoreType.DMA((n, len(OTHER_CHIPS))), pltpu.SemaphoreType.DMA((n, len(OTHER_CHIPS)))],
        input_output_aliases={a: a for a in range(n)},
        name="forward_blocks",
    )(*fulls)


def _shard_order():
    (x, y, c), me = _me()
    ids = [me, _peer(1)[1]]
    for dist in OTHER_CHIPS:
        ids += [_peer(dist)[1], _peer(dist + 1)[1]]
    return jnp.stack(ids).astype(jnp.int32)


def _proj_gather(h, w_shard, order):
    T, K = h.shape
    C = w_shard.shape[1]
    tm = _pick(T, 1024)
    nm = T // tm
    ahead = max(nm - 2, 0)

    def body(order_ref, h_ref, ws_ref, o_ref, wf_ref, wbuf, send_sems, recv_sems, own_sem, buf_sems):
        t, i = pl.program_id(0), pl.program_id(1)
        (x, y, c), me = _me()
        sibling, sib_id = _peer(1)
        chips = [_peer(dist) for dist in OTHER_CHIPS]

        def win(idx):
            return wf_ref.at[:, pl.ds(idx * C, C)]

        def copy(k, block, to, src=None):
            return pltpu.make_async_remote_copy(
                src_ref=win(block) if src is None else src, dst_ref=win(block),
                send_sem=send_sems.at[k], recv_sem=recv_sems.at[k], device_id=to, device_id_type=MESH)

        def fetch(tt, src):
            return pltpu.make_async_copy(src, wbuf.at[tt % 2], buf_sems.at[tt % 2])

        own = pltpu.make_async_copy(ws_ref, win(me), own_sem)

        @pl.when((t == 0) & (i == 0))
        def _():
            fetch(0, ws_ref).start()
            own.start()
            copy(0, me, sibling, src=ws_ref).start()
            for j, (dev, _) in enumerate(chips):
                copy(1 + j, me, dev, src=ws_ref).start()

        for tt in range(N_DEV):
            @pl.when((t == tt) & (i == 0))
            def _(tt=tt):
                fetch(tt, ws_ref).wait()

        o_ref[...] = _dot(h_ref[...], wbuf[t % 2], NN_DIMS)

        for tt in range(N_DEV - 1):
            @pl.when((t == tt) & (i == ahead))
            def _(tt=tt):
                nxt = tt + 1
                j = (nxt - 2) // 2
                if nxt == 1:
                    copy(0, sib_id, (x, y, c)).wait_recv()
                    block = sib_id
                elif nxt % 2 == 0:
                    block = chips[j][1]
                    copy(1 + j, block, (x, y, c)).wait_recv()
                    copy(4 + j, block, sibling).start()
                else:
                    block = chips[j][1] + 1 - 2 * c
                    copy(4 + j, block, (x, y, c)).wait_recv()
                fetch(nxt, win(block)).start()

        @pl.when((t == N_DEV - 1) & (i == nm - 1))
        def _():
            copy(0, me, sibling, src=ws_ref).wait_send()
            for j, (dev, dev_id) in enumerate(chips):
                copy(1 + j, me, dev, src=ws_ref).wait_send()
                copy(4 + j, dev_id, sibling).wait_send()
            own.wait()

    return pl.pallas_call(
        body,
        grid_spec=pltpu.PrefetchScalarGridSpec(
            num_scalar_prefetch=1, grid=(N_DEV, nm),
            in_specs=[pl.BlockSpec((tm, K), lambda t, i, order_ref: (i, 0)), pl.BlockSpec(memory_space=pl.ANY)],
            out_specs=[pl.BlockSpec((tm, C), lambda t, i, order_ref: (i, order_ref[t])), pl.BlockSpec(memory_space=pl.ANY)],
            scratch_shapes=[pltpu.VMEM((2, K, C), bf16), pltpu.SemaphoreType.DMA((7,)), pltpu.SemaphoreType.DMA((7,)),
                            pltpu.SemaphoreType.DMA, pltpu.SemaphoreType.DMA((2,))]),
        out_shape=[jax.ShapeDtypeStruct((T, N_DEV * C), f32), jax.ShapeDtypeStruct((K, N_DEV * C), bf16)],
        compiler_params=_params(("arbitrary", "arbitrary")), name="proj_gather",
    )(order, h, w_shard)


N_CHIP = 4


def _shard_shape(g, ax):
    return tuple(d // N_DEV if i == ax else d for i, d in enumerate(g.shape))


def _sibling_comm(grads, axes):
    n = len(grads)
    sizes = [g.shape[ax] // N_DEV for g, ax in zip(grads, axes)]

    def copies(ins, lands, send_sems, recv_sems, base=0):
        sibling, _ = _peer(1)
        out = []
        for j in range(N_CHIP):
            _, owner = _peer(2 * j + 1)
            for a in range(n):
                out.append(pltpu.make_async_remote_copy(
                    src_ref=_window(ins[a], axes[a], sizes[a], owner), dst_ref=lands[a].at[j],
                    send_sem=send_sems.at[base + a, j], recv_sem=recv_sems.at[base + a, j], device_id=sibling,
                    device_id_type=MESH))
        return out

    shapes = [jax.ShapeDtypeStruct((N_CHIP,) + _shard_shape(g, ax), g.dtype) for g, ax in zip(grads, axes)]
    return _Comm(grads, shapes, (n, N_CHIP), copies)


def _chips_comm(sums):
    n = len(sums)

    def copies(ins, lands, send_sems, recv_sems, base=0):
        out = []
        for j in range(1, N_CHIP):
            dev, _ = _peer(2 * j)
            for a in range(n):
                out.append(pltpu.make_async_remote_copy(
                    src_ref=ins[a].at[j - 1], dst_ref=lands[a].at[j - 1],
                    send_sem=send_sems.at[base + a, j - 1], recv_sem=recv_sems.at[base + a, j - 1], device_id=dev,
                    device_id_type=MESH))
        return out

    return _Comm(sums, [jax.ShapeDtypeStruct(s.shape, s.dtype) for s in sums], (n, N_CHIP), copies)


def _join(c1, c2):
    n1, m1, k1 = len(c1.ins), len(c1.out_shapes), c1.sem_shape[0]

    def copies(ins, lands, send_sems, recv_sems):
        return (c1.copies(ins[:n1], lands[:m1], send_sems, recv_sems, base=0)
                + c2.copies(ins[n1:], lands[m1:], send_sems, recv_sems, base=k1))

    return _Comm(c1.ins + c2.ins, c1.out_shapes + c2.out_shapes, (k1 + c2.sem_shape[0], N_CHIP), copies)


def _rs_chip_sum(grad, land, xyc, axis, name):
    R, C = land.shape[1:]
    tr = _pick(R, max(8, (256 * 1024) // C), 8)

    def body(xyc_ref, g_ref, l_ref, o_ref):
        o_ref[0] = (g_ref[...] + l_ref[0]).astype(bf16)

    def owner(j, xyc_ref):
        jj = j + 1
        return 4 * (xyc_ref[0] ^ (jj >> 1)) + 2 * (xyc_ref[1] ^ (jj & 1)) + xyc_ref[2]

    if axis == 0:
        g_spec = pl.BlockSpec((tr, C), lambda j, i, xyc_ref: (owner(j, xyc_ref) * (R // tr) + i, 0))
    else:
        g_spec = pl.BlockSpec((tr, C), lambda j, i, xyc_ref: (i, owner(j, xyc_ref)))
    return pl.pallas_call(
        body,
        grid_spec=pltpu.PrefetchScalarGridSpec(
            num_scalar_prefetch=1, grid=(N_CHIP - 1, R // tr),
            in_specs=[g_spec, pl.BlockSpec((1, tr, C), lambda j, i, xyc_ref: (j + 1, i, 0))],
            out_specs=pl.BlockSpec((1, tr, C), lambda j, i, xyc_ref: (j, i, 0))),
        out_shape=jax.ShapeDtypeStruct((N_CHIP - 1, R, C), bf16),
        compiler_params=_params(("parallel", "parallel")), name=name,
    )(xyc, grad, land)


def _allreduce_small(part):
    R = part.shape[0]

    def body(p_ref, o_ref, buf, send_sems, recv_sems):
        (x, y, c), me = _me()
        buf[me] = p_ref[...]
        copies = []
        for k in range(1, N_DEV):
            dev, dev_id = _peer(k)
            copies.append(pltpu.make_async_remote_copy(
                src_ref=p_ref, dst_ref=buf.at[me], send_sem=send_sems.at[k - 1], recv_sem=recv_sems.at[k - 1],
                device_id=dev, device_id_type=MESH))
        for cp in copies:
            cp.start()
        for k in range(1, N_DEV):
            _, dev_id = _peer(k)
            pltpu.make_async_remote_copy(
                src_ref=p_ref, dst_ref=buf.at[dev_id], send_sem=send_sems.at[k - 1], recv_sem=recv_sems.at[k - 1],
                device_id=(x, y, c), device_id_type=MESH).wait_recv()
        for cp in copies:
            cp.wait_send()
        acc = buf[0]
        for d in range(1, N_DEV):
            acc = acc + buf[d]
        o_ref[...] = acc

    return pl.pallas_call(
        body, in_specs=[pl.BlockSpec(memory_space=pltpu.VMEM)], out_specs=pl.BlockSpec(memory_space=pltpu.VMEM),
        out_shape=jax.ShapeDtypeStruct((R, LANES), f32),
        scratch_shapes=[pltpu.VMEM((N_DEV, R, LANES), f32), pltpu.SemaphoreType.DMA((N_DEV - 1,)), pltpu.SemaphoreType.DMA((N_DEV - 1,))],
        name="allreduce_small",
    )(part)


def _adamw_math(w, g, m, v):
    m2 = ADAM_B1 * m + (1.0 - ADAM_B1) * g
    v2 = ADAM_B2 * v + (1.0 - ADAM_B2) * (g * g)
    m_hat = m2 / (1.0 - ADAM_B1 ** ADAM_STEP)
    v_hat = v2 / (1.0 - ADAM_B2 ** ADAM_STEP)
    return -ADAM_LR * (m_hat / (jnp.sqrt(v_hat) + ADAM_EPS) + ADAM_WD * w), m2, v2


def _adamw_shard(grad, land_sib, land_chips, w, m, v, me, axis, name, row0=0, prev=None):
    R, C = land_sib.shape[1:]
    assert axis == 1 or R == w.shape[0]
    tr = _pick(R, max(8, (128 * 1024) // C), 8)
    r0 = row0 // tr
    n_prev = len(prev) if prev else 0

    def body(me_ref, g_ref, s_ref, l_ref, w_ref, m_ref, v_ref, *rest):
        go_ref, d_ref, mo_ref, vo_ref = rest[n_prev:]
        g = g_ref[...] + s_ref[0]
        for j in range(N_CHIP - 1):
            g = g + l_ref[j].astype(f32)
        d, m2, v2 = _adamw_math(w_ref[...], g, m_ref[...], v_ref[...])
        go_ref[...] = g
        d_ref[...] = d
        mo_ref[...] = m2
        vo_ref[...] = v2

    if axis == 0:
        g_spec = pl.BlockSpec((tr, C), lambda i, me_ref: (me_ref[0] * (R // tr) + i, 0))
    else:
        g_spec = pl.BlockSpec((tr, C), lambda i, me_ref: (i, me_ref[0]))
    blk = pl.BlockSpec((tr, C), lambda i, me_ref: (r0 + i, 0))
    return pl.pallas_call(
        body,
        grid_spec=pltpu.PrefetchScalarGridSpec(
            num_scalar_prefetch=1, grid=(R // tr,),
            in_specs=[g_spec, pl.BlockSpec((1, tr, C), lambda i, me_ref: (0, i, 0)),
                      pl.BlockSpec((N_CHIP - 1, tr, C), lambda i, me_ref: (0, i, 0)), blk, blk, blk]
            + [pl.BlockSpec(memory_space=pl.ANY)] * n_prev,
            out_specs=[blk] * 4),
        out_shape=[jax.ShapeDtypeStruct(w.shape, f32)] * 4,
        input_output_aliases={7 + i: i for i in range(n_prev)},
        compiler_params=_params(("parallel",)), name=name,
    )(me, grad, land_sib, land_chips, w, m, v, *(prev or []))


def _adamw_small(g, w, m, v):
    def body(g_ref, w_ref, m_ref, v_ref, d_ref, mo_ref, vo_ref):
        d, m2, v2 = _adamw_math(w_ref[...], g_ref[...], m_ref[...], v_ref[...])
        d_ref[...] = d
        mo_ref[...] = m2
        vo_ref[...] = v2

    return pl.pallas_call(body, out_shape=[jax.ShapeDtypeStruct(g.shape, f32)] * 3, name="adamw_small")(g, w, m, v)


def _pack_rows(parts, total_rows):
    rows = jnp.concatenate([p.reshape(-1, LANES) for p in parts], axis=0)
    return jnp.pad(rows, ((0, total_rows - rows.shape[0]), (0, 0)))


BIG = ("w_in", "mem_w_kv", "w_br_attn", "w_br_conv", "w_br_mem", "w_out")
BIG_AXIS = {"w_in": 1, "mem_w_kv": 0, "w_br_attn": 1, "w_br_conv": 1, "w_br_mem": 1, "w_out": 0}
SMALL = ("norm_g", "mem_norm_g", "attn_q_norm", "attn_k_norm", "mem_q_norm", "mem_k_norm")
ALL_W = ("norm_g", "mem_norm_g", "w_in", "attn_q_norm", "attn_k_norm", "conv_w", "mem_w_kv", "mem_q_norm", "mem_k_norm",
         "w_br_attn", "w_br_conv", "w_br_mem", "w_out")


def kernel(x, mem, norm_g, mem_norm_g, w_in, attn_q_norm, attn_k_norm, conv_w, mem_w_kv, mem_q_norm, mem_k_norm, w_br_attn, w_br_conv, w_br_mem, w_out, loss_target, m_norm_g, m_mem_norm_g, m_w_in, m_attn_q_norm, m_attn_k_norm, m_conv_w, m_mem_w_kv, m_mem_q_norm, m_mem_k_norm, m_w_br_attn, m_w_br_conv, m_w_br_mem, m_w_out, v_norm_g, v_mem_norm_g, v_w_in, v_attn_q_norm, v_attn_k_norm, v_conv_w, v_mem_w_kv, v_mem_q_norm, v_mem_k_norm, v_w_br_attn, v_w_br_conv, v_w_br_mem, v_w_out):
    w = dict(norm_g=norm_g, mem_norm_g=mem_norm_g, w_in=w_in, attn_q_norm=attn_q_norm, attn_k_norm=attn_k_norm, conv_w=conv_w,
             mem_w_kv=mem_w_kv, mem_q_norm=mem_q_norm, mem_k_norm=mem_k_norm, w_br_attn=w_br_attn, w_br_conv=w_br_conv,
             w_br_mem=w_br_mem, w_out=w_out)
    mo = dict(norm_g=m_norm_g, mem_norm_g=m_mem_norm_g, w_in=m_w_in, attn_q_norm=m_attn_q_norm, attn_k_norm=m_attn_k_norm,
              conv_w=m_conv_w, mem_w_kv=m_mem_w_kv, mem_q_norm=m_mem_q_norm, mem_k_norm=m_mem_k_norm, w_br_attn=m_w_br_attn,
              w_br_conv=m_w_br_conv, w_br_mem=m_w_br_mem, w_out=m_w_out)
    vo = dict(norm_g=v_norm_g, mem_norm_g=v_mem_norm_g, w_in=v_w_in, attn_q_norm=v_attn_q_norm, attn_k_norm=v_attn_k_norm,
              conv_w=v_conv_w, mem_w_kv=v_mem_w_kv, mem_q_norm=v_mem_q_norm, mem_k_norm=v_mem_k_norm, w_br_attn=v_w_br_attn,
              w_br_conv=v_w_br_conv, w_br_mem=v_w_br_mem, w_out=v_w_out)
    B, S, D = x.shape
    me = (4 * lax.axis_index("x") + 2 * lax.axis_index("y") + lax.axis_index("c")).astype(jnp.int32)

    T = B * S
    x2, t2, mem2, ng = x.reshape(T, D), loss_target.reshape(T, D), mem.reshape(B * MEM_LEN, D), norm_g.reshape(1, D)
    h = _rms_fwd(x2, ng, "rms_x")
    proj, w_in_full = _proj_gather(h, w_in.astype(bf16), _shard_order())
    later = BIG[1:]
    later_axes = [BIG_AXIS[n] for n in later] + [None]
    conv_pad = jnp.pad(conv_w, ((0, 8 - conv_w.shape[0]), (0, 0)))
    *attn, spread = _attn_fwd(proj, attn_q_norm, attn_k_norm, B, S,
                              comm=_spread_comm([w[n].astype(bf16) for n in later] + [conv_pad], later_axes))
    *fulls, conv_g = _forward_blocks(spread, later_axes)
    wf = dict(zip(later, fulls), w_in=w_in_full)
    conv_full = conv_g[:, :3, :].transpose(1, 0, 2).reshape(3, CONV_W)

    sq_err, g, t = _fwd_bwd_head(x2, mem2, t2, proj, attn, B, S, mem_norm_g, attn_q_norm, attn_k_norm, conv_full, mem_q_norm,
                                 mem_k_norm, wf["mem_w_kv"], wf["w_br_attn"], wf["w_br_conv"], wf["w_br_mem"], wf["w_out"])
    t.update(x2=x2, ng=ng, h=h)

    xyc = jnp.stack([lax.axis_index("x"), lax.axis_index("y"), lax.axis_index("c")]).astype(jnp.int32)
    me1 = me.reshape(1)
    early = ("w_out", "w_br_attn", "w_br_conv", "w_br_mem")
    g["mem_w_kv"], sib_early = _mm(t["mh"], t["dmkv"], mode="tn", out_dtype=f32, name="dw_kv",
                                   comm=_sibling_comm([g[n] for n in early], [BIG_AXIS[n] for n in early]))
    sums_early = [_rs_chip_sum(g[n], ls, xyc, BIG_AXIS[n], "rs_sum_" + n) for n, ls in zip(early, sib_early)]
    tm_w = _pick(D // 2, 1024)
    half = (D // 2) // tm_w
    g_top, chips_early = _mm(t["h"], t["dproj"], mode="tn", out_dtype=f32, name="dw_in_top", tm=tm_w, m_tiles=(0, half),
                             comm=_chips_comm(sums_early))
    g_bot, (sib_top, sib_kv) = _mm(t["h"], t["dproj"], mode="tn", out_dtype=f32, name="dw_in_bot", tm=tm_w, m_tiles=(half, half),
                                   comm=_sibling_comm([g_top, g["mem_w_kv"]], [1, 0]))
    sum_top = _rs_chip_sum(g_top, sib_top, xyc, 1, "rs_sum_w_in_top")
    sum_kv = _rs_chip_sum(g["mem_w_kv"], sib_kv, xyc, 0, "rs_sum_mem_w_kv")
    T = B * S
    tm_t = _pick(T // 2, 1024)
    halfm = (T // 2) // tm_t
    dh, (chips_top, chips_kv, sib_bot) = _mm(t["dproj"], wf["w_in"], mode="nt", out_dtype=f32, name="d_h_a", tm=tm_t,
                                             m_tiles=(0, halfm), into="new",
                                             comm=_join(_chips_comm([sum_top, sum_kv]), _sibling_comm([g_bot], [1])))
    sum_bot = _rs_chip_sum(g_bot, sib_bot, xyc, 1, "rs_sum_w_in_bot")
    dh, (chips_bot,) = _mm(t["dproj"], wf["w_in"], mode="nt", out_dtype=f32, name="d_h_b", tm=tm_t, m_tiles=(halfm, halfm),
                           into=dh, comm=_chips_comm([sum_bot]))
    dx, dng = _rms_bwd(t["x2"], t["ng"], dh, t["dy"], "rms_x_bwd")
    g["norm_g"] = dng.reshape(D)

    grad, delta, new_m, new_v = {}, {}, {}, {}
    for n, ls, lc in zip(early + ("mem_w_kv",), sib_early + [sib_kv], chips_early + [chips_kv]):
        grad[n], delta[n], new_m[n], new_v[n] = _adamw_shard(g[n], ls, lc, w[n], mo[n], vo[n], me1, BIG_AXIS[n], "adamw_" + n)
    top = _adamw_shard(g_top, sib_top, chips_top, w_in, m_w_in, v_w_in, me1, 1, "adamw_w_in_top")
    grad["w_in"], delta["w_in"], new_m["w_in"], new_v["w_in"] = _adamw_shard(
        g_bot, sib_bot, chips_bot, w_in, m_w_in, v_w_in, me1, 1, "adamw_w_in_bot", row0=D // 2, prev=top)

    n_rep = sum(w[n].size for n in SMALL) // LANES
    conv_rows = g["conv_w"].reshape(3, N_DEV, LANES).transpose(1, 0, 2).reshape(3 * N_DEV, LANES)
    n_rows = -(-(n_rep + 3 * N_DEV + 1) // 8) * 8
    part = _pack_rows([g[n] for n in SMALL] + [conv_rows, jnp.full((1, LANES), sq_err, f32)], n_rows)
    tot = _allreduce_small(part)
    loss = tot[n_rep + 3 * N_DEV, 0] * (0.5 / D)
    n_small = -(-(n_rep + 3) // 8) * 8
    g_small = _pack_rows([tot[:n_rep], lax.dynamic_slice(tot, (n_rep + 3 * me, 0), (3, LANES))], n_small)
    names = SMALL + ("conv_w",)
    d_s, m_s, v_s = _adamw_small(g_small, _pack_rows([w[n] for n in names], n_small), _pack_rows([mo[n] for n in names], n_small),
                                 _pack_rows([vo[n] for n in names], n_small))
    off = 0
    for n in names:
        r = w[n].size // LANES
        grad[n], delta[n] = g_small[off:off + r].reshape(w[n].shape), d_s[off:off + r].reshape(w[n].shape)
        new_m[n], new_v[n] = m_s[off:off + r].reshape(w[n].shape), v_s[off:off + r].reshape(w[n].shape)
        off += r
    return (loss, dx.reshape(B, S, D), *[grad[n] for n in ALL_W], *[delta[n] for n in ALL_W], *[new_m[n] for n in ALL_W],
            *[new_v[n] for n in ALL_W])
```

```python
import functools

import jax
import jax.numpy as jnp
from jax import lax
from jax.experimental import pallas as pl
from jax.experimental.pallas import tpu as pltpu

f32 = jnp.float32
bf16 = jnp.bfloat16

N_DEV = 8
HEAD_DIM = 128
DILATIONS = (1, 4, 16)
N_GROUPS = 3
HEADS = 4
BLK = 128
ATTN_QKV = N_GROUPS * HEADS * HEAD_DIM
ATTN_OUT = HEADS * HEAD_DIM
CONV_W = 1024
MEM_LEN = 256
MEM_HEADS = 4
MEM_HD = 256
MEM_W = MEM_HEADS * MEM_HD
EPS = 1e-6
Q0, K0, V0 = 0, ATTN_QKV, 2 * ATTN_QKV
ZA = 3 * ATTN_QKV
CB, CC, CV, ZC = ZA + ATTN_OUT, ZA + ATTN_OUT + CONV_W, ZA + ATTN_OUT + 2 * CONV_W, ZA + ATTN_OUT + 3 * CONV_W
MQ = ZC + CONV_W
ZM = MQ + MEM_W
GT = ZM + MEM_W

ADAM_LR, ADAM_B1, ADAM_B2, ADAM_EPS, ADAM_WD, ADAM_STEP = 0.001, 0.9, 0.999, 1e-08, 0.01, 10

VMEM_LIMIT = 56 * 1024 * 1024
LANES = 128

NT_DIMS = (((1,), (1,)), ((), ()))
TN_DIMS = (((0,), (0,)), ((), ()))
NN_DIMS = (((1,), (0,)), ((), ()))


def _params(sem=None):
    return pltpu.CompilerParams(dimension_semantics=sem, vmem_limit_bytes=VMEM_LIMIT)


def _pick(n, pref, q=LANES):
    t = (min(pref, n) // q) * q
    while t >= q:
        if n % t == 0:
            return t
        t -= q
    return n


def _dot(a, b, dims):
    return lax.dot_general(a.astype(bf16), b.astype(bf16), dims, preferred_element_type=f32)


def _sigmoid(z):
    return 0.5 * jnp.tanh(0.5 * z) + 0.5


def _rstd(v):
    return lax.rsqrt(jnp.mean(v * v, axis=-1, keepdims=True) + EPS)


class _Deferred:
    def __init__(self, stage, sems, step, last, n):
        assert last >= 1
        self.stage, self.sems, self.step, self.last, self.n = stage, sems, step, last, n
        self.slot = step % 2

    def _drain(self, slot, like):
        for c in range(self.n):
            pltpu.make_async_copy(self.stage.at[slot, c], like(c), self.sems.at[slot, c]).wait()

    def begin(self, like):
        pl.when(self.step >= 2)(lambda: self._drain(self.slot, like))

    def start(self, c, dst):
        pltpu.make_async_copy(self.stage.at[self.slot, c], dst, self.sems.at[self.slot, c]).start()

    def end(self, like):
        @pl.when(self.step == self.last)
        def _():
            self._drain(self.slot, like)
            self._drain(1 - self.slot, like)


def _row_sum(v):
    hi = v.astype(bf16)
    lo = (v - hi.astype(f32)).astype(bf16)
    ones = jnp.ones((v.shape[1], v.shape[1]), bf16)
    return _dot(hi, ones, NN_DIMS) + _dot(lo, ones, NN_DIMS)


def _rstd_head(v):
    return lax.rsqrt(_row_sum(v * v) * (1.0 / v.shape[1]) + EPS)


class _Comm:
    def __init__(self, ins, out_shapes, sem_shape, copies):
        self.ins, self.out_shapes, self.sem_shape, self.copies = list(ins), list(out_shapes), sem_shape, copies

    def scratch(self):
        return [pltpu.SemaphoreType.DMA(self.sem_shape), pltpu.SemaphoreType.DMA(self.sem_shape)]


def _mm(a, b, *, mode, out_dtype, name, tm=1024, tn=1024, tk=4096, m_tiles=None, into=None, comm=None):
    if mode == "nn":
        (M, K), (K2, N) = a.shape, b.shape
    elif mode == "nt":
        (M, K), (N, K2) = a.shape, b.shape
    else:
        (K, M), (K2, N) = a.shape, b.shape
    assert K == K2
    tm, tn, tk = _pick(M, tm), _pick(N, tn), _pick(K, tk)
    nk = K // tk
    m0, mc = m_tiles if m_tiles is not None else (0, M // tm)
    o0 = 0 if into is None else m0
    aliased = into is not None and not isinstance(into, str)
    n_ci = len(comm.ins) if comm else 0
    n_co = len(comm.out_shapes) if comm else 0
    grid = (mc, N // tn, nk)
    dims = {"nn": NN_DIMS, "nt": NT_DIMS, "tn": TN_DIMS}[mode]

    def body(*refs, nk):
        a_ref, b_ref = refs[:2]
        pos = 3 if aliased else 2
        c_ins, o_ref, c_outs = refs[pos:pos + n_ci], refs[pos + n_ci], refs[pos + n_ci + 1:pos + n_ci + 1 + n_co]
        scratch = refs[pos + n_ci + 1 + n_co:]
        ids = [pl.program_id(d) for d in range(3)]
        if comm:
            sems = scratch[-2:]

            @pl.when((ids[0] == 0) & (ids[1] == 0) & (ids[2] == 0))
            def _():
                for cp in comm.copies(c_ins, c_outs, *sems):
                    cp.start()

        if nk == 1:
            o_ref[...] = _dot(a_ref[...], b_ref[...], dims).astype(o_ref.dtype)
        else:
            acc_ref, k = scratch[0], ids[2]

            @pl.when(k == 0)
            def _():
                acc_ref[...] = _dot(a_ref[...], b_ref[...], dims)

            @pl.when((k > 0) & (k < nk - 1))
            def _():
                acc_ref[...] += _dot(a_ref[...], b_ref[...], dims)

            @pl.when(k == nk - 1)
            def _():
                o_ref[...] = (acc_ref[...] + _dot(a_ref[...], b_ref[...], dims)).astype(o_ref.dtype)

        if comm:
            @pl.when((ids[0] == grid[0] - 1) & (ids[1] == grid[1] - 1) & (ids[2] == grid[2] - 1))
            def _():
                for cp in comm.copies(c_ins, c_outs, *sems):
                    cp.wait()

    if mode == "tn":
        a_spec = pl.BlockSpec((tk, tm), lambda i, j, k: (k, m0 + i))
    else:
        a_spec = pl.BlockSpec((tm, tk), lambda i, j, k: (m0 + i, k))
    if mode == "nt":
        b_spec = pl.BlockSpec((tn, tk), lambda i, j, k: (j, k))
    else:
        b_spec = pl.BlockSpec((tk, tn), lambda i, j, k: (k, j))
    hbm = pl.BlockSpec(memory_space=pl.ANY)
    res = pl.pallas_call(
        functools.partial(body, nk=nk),
        grid=grid,
        in_specs=[a_spec, b_spec] + [hbm] * (aliased + n_ci),
        out_specs=[pl.BlockSpec((tm, tn), lambda i, j, k: (o0 + i, j))] + [hbm] * n_co,
        out_shape=[jax.ShapeDtypeStruct((mc * tm if into is None else M, N), out_dtype)] + (comm.out_shapes if comm else []),
        scratch_shapes=([pltpu.VMEM((tm, tn), f32)] if nk > 1 else []) + (comm.scratch() if comm else []),
        input_output_aliases={2: 0} if aliased else {},
        compiler_params=_params(("arbitrary",) * 3 if comm else ("parallel", "parallel", "arbitrary")),
        name=name,
    )(a, b, *([into] if aliased else []), *(comm.ins if comm else []))
    return (res[0], list(res[1:])) if comm else res[0]


def _rms_fwd(x, g, name):
    T, D = x.shape
    tm = _pick(T, 256, 8)

    def body(x_ref, g_ref, h_ref):
        xv = x_ref[...]
        h_ref[...] = (xv * _rstd(xv) * g_ref[...]).astype(bf16)

    return pl.pallas_call(
        body, grid=(T // tm,),
        in_specs=[pl.BlockSpec((tm, D), lambda i: (i, 0)), pl.BlockSpec((1, D), lambda i: (0, 0))],
        out_specs=pl.BlockSpec((tm, D), lambda i: (i, 0)),
        out_shape=jax.ShapeDtypeStruct((T, D), bf16),
        compiler_params=_params(("parallel",)), name=name,
    )(x, g)


def _rms_bwd(x, g, dh, dy, name):
    T, D = x.shape
    tm = _pick(T, 256, 8)
    with_dx = dy is not None

    def body(*refs):
        if with_dx:
            x_ref, g_ref, dh_ref, dy_ref, dx_ref, dg_ref = refs
        else:
            x_ref, g_ref, dh_ref, dg_ref = refs
        xv = x_ref[...]
        r = _rstd(xv)
        xh = xv * r
        dhv = dh_ref[...]
        part = jnp.sum(dhv * xh, axis=0, keepdims=True)

        @pl.when(pl.program_id(0) == 0)
        def _():
            dg_ref[...] = part

        @pl.when(pl.program_id(0) > 0)
        def _():
            dg_ref[...] += part

        if with_dx:
            dxh = dhv * g_ref[...]
            dx_ref[...] = dy_ref[...] + r * (dxh - xh * jnp.mean(dxh * xh, axis=-1, keepdims=True))

    row = pl.BlockSpec((tm, D), lambda i: (i, 0))
    vec = pl.BlockSpec((1, D), lambda i: (0, 0))
    if with_dx:
        return pl.pallas_call(
            body, grid=(T // tm,), in_specs=[row, vec, row, row], out_specs=[row, vec],
            out_shape=[jax.ShapeDtypeStruct((T, D), f32), jax.ShapeDtypeStruct((1, D), f32)],
            compiler_params=_params(("arbitrary",)), name=name,
        )(x, g, dh, dy)
    return pl.pallas_call(
        body, grid=(T // tm,), in_specs=[row, vec, row], out_specs=vec,
        out_shape=jax.ShapeDtypeStruct((1, D), f32),
        compiler_params=_params(("arbitrary",)), name=name,
    )(x, g, dh)


UNIT_UNROLL = 4


def _rows(start, size, stride):
    return pl.ds(start, size) if stride == 1 else pl.ds(start, size, stride=stride)


def _attn_units(S, d, first, prev):
    nb = S // d // BLK

    def do_first(r, c):
        first(_rows(r, BLK, d))
        return c

    lax.fori_loop(0, d, do_first, 0, unroll=min(d, UNIT_UNROLL))
    if nb > 1:
        def do_prev(u, c):
            r = u // (nb - 1)
            b = u % (nb - 1) + 1
            base = r + d * b * BLK
            prev(_rows(base, BLK, d), _rows(base - d * BLK, 2 * BLK, d))
            return c

        lax.fori_loop(0, d * (nb - 1), do_prev, 0, unroll=UNIT_UNROLL)


def _band_bias(nkeys):
    i = lax.broadcasted_iota(jnp.int32, (BLK, nkeys), 0)
    j = lax.broadcasted_iota(jnp.int32, (BLK, nkeys), 1)
    ok = (j <= i) if nkeys == BLK else ((j >= i) & (j <= i + BLK))
    return jnp.where(ok, 0.0, -jnp.inf).astype(f32)


def _normalize_into(src_ref, gain, dst_ref, S):
    for c in range(S // 256):
        rows = pl.ds(c * 256, 256)
        v = src_ref[rows, :]
        dst_ref[rows, :] = v * _rstd_head(v) * gain


def _attn_fwd(proj, gq, gk, B, S, comm=None):
    T, NC = proj.shape
    scale = HEAD_DIM ** -0.5
    n_ci = len(comm.ins) if comm else 0
    n_co = len(comm.out_shapes) if comm else 0

    def body(*refs):
        q_ref, k_ref, v_ref, z_ref, gq_ref, gk_ref = refs[:6]
        c_ins = refs[6:6 + n_ci]
        ag_ref, a_ref, lse_ref = refs[6 + n_ci:9 + n_ci]
        c_outs = refs[9 + n_ci:9 + n_ci + n_co]
        qs, ks, o0, o1, o2, l0, l1, l2, bias1, bias2 = refs[9 + n_ci + n_co:19 + n_ci + n_co]
        sems = refs[19 + n_ci + n_co:]
        g = pl.program_id(2)
        if comm:
            @pl.when((pl.program_id(0) == 0) & (pl.program_id(1) == 0) & (g == 0))
            def _():
                for cp in comm.copies(c_ins, c_outs, *sems):
                    cp.start()

        bias1[...] = _band_bias(BLK)
        bias2[...] = _band_bias(2 * BLK)
        _normalize_into(q_ref, gq_ref[pl.ds(g, 1), :], qs, S)
        _normalize_into(k_ref, gk_ref[pl.ds(g, 1), :], ks, S)
        o_s, l_s = (o0, o1, o2), (l0, l1, l2)

        def run(gi):
            def unit(qr, kr, nkeys):
                s = _dot(qs[qr, :], ks[kr, :], NT_DIMS) * scale + (bias1 if nkeys == BLK else bias2)[...]
                m = jnp.max(s, axis=-1, keepdims=True)
                p = jnp.exp(s - m)
                den = jnp.sum(p, axis=-1, keepdims=True)
                o_s[gi][qr, :] = _dot(p, v_ref[kr, :], NN_DIMS) * (1.0 / den)
                l_s[gi][qr, :] = jnp.broadcast_to(m + jnp.log(den), (BLK, HEAD_DIM))

            _attn_units(S, DILATIONS[gi], lambda qr: unit(qr, qr, BLK), lambda qr, kr: unit(qr, kr, 2 * BLK))

        for gi in range(N_GROUPS):
            pl.when(g == gi)(functools.partial(run, gi))

        @pl.when(g == N_GROUPS - 1)
        def _():
            for c in range(S // 256):
                rows = pl.ds(c * 256, 256)
                la, lb, lc = l0[rows, :], l1[rows, :], l2[rows, :]
                m = jnp.maximum(jnp.maximum(la, lb), lc)
                wa, wb, wc = jnp.exp(la - m), jnp.exp(lb - m), jnp.exp(lc - m)
                tot = wa + wb + wc
                a = (wa / tot) * o0[rows, :] + (wb / tot) * o1[rows, :] + (wc / tot) * o2[rows, :]
                z = z_ref[rows, :]
                a_ref[rows, :] = a
                lse_ref[rows, :] = m + jnp.log(tot)
                ag_ref[rows, :] = (a * (z * _sigmoid(z))).astype(bf16)

        if comm:
            @pl.when((pl.program_id(0) == B - 1) & (pl.program_id(1) == HEADS - 1) & (g == N_GROUPS - 1))
            def _():
                for cp in comm.copies(c_ins, c_outs, *sems):
                    cp.wait()

    def slab(col0):
        return pl.BlockSpec((S, HEAD_DIM), lambda b, h, g: (b, col0 // HEAD_DIM + g * HEADS + h))

    gain = pl.BlockSpec((N_GROUPS, HEAD_DIM), lambda b, h, g: (0, 0))
    out = pl.BlockSpec((S, HEAD_DIM), lambda b, h, g: (b, h))
    hbm = pl.BlockSpec(memory_space=pl.ANY)
    res = pl.pallas_call(
        body, grid=(B, HEADS, N_GROUPS),
        in_specs=[slab(Q0), slab(K0), slab(V0), pl.BlockSpec((S, HEAD_DIM), lambda b, h, g: (b, ZA // HEAD_DIM + h)), gain, gain]
        + [hbm] * n_ci,
        out_specs=[out, out, out] + [hbm] * n_co,
        out_shape=[jax.ShapeDtypeStruct((T, ATTN_OUT), bf16), jax.ShapeDtypeStruct((T, ATTN_OUT), f32),
                   jax.ShapeDtypeStruct((T, ATTN_OUT), f32)] + (comm.out_shapes if comm else []),
        scratch_shapes=[pltpu.VMEM((S, HEAD_DIM), f32)] * 8 + [pltpu.VMEM((BLK, BLK), f32), pltpu.VMEM((BLK, 2 * BLK), f32)]
        + (comm.scratch() if comm else []),
        compiler_params=_params(("arbitrary", "arbitrary", "arbitrary")), name="attn_fwd",
    )(proj, proj, proj, proj, gq, gk, *(comm.ins if comm else []))
    return res[0], res[1], res[2], list(res[3:])


def _rms_bwd_rows(raw, gain, dn, on_mxu=True):
    r = _rstd_head(raw) if on_mxu else _rstd(raw)
    xh = raw * r
    dxh = dn * gain
    if on_mxu:
        mean = _row_sum(dxh * xh) * (1.0 / raw.shape[1])
    else:
        mean = jnp.mean(dxh * xh, axis=-1, keepdims=True)
    return r * (dxh - xh * mean), jnp.sum(dn * xh, axis=0, keepdims=True)


def _attn_bwd(proj, gq, gk, a_comb, lse, dag, dproj, B, S):
    T, NC = proj.shape
    scale = HEAD_DIM ** -0.5

    def body(q_ref, k_ref, v_ref, z_ref, gq_ref, gk_ref, a_ref, lse_ref, dag_ref, dproj_in, dproj_ref, dgq_ref, dgk_ref,
             qs, ks, da_s, dl_s, dq_s, dk_s, dv_s, bias1, bias2, stage, dz_stage, sem, dz_sem):
        b, h, g = pl.program_id(0), pl.program_id(1), pl.program_id(2)
        bias1[...] = _band_bias(BLK)
        bias2[...] = _band_bias(2 * BLK)
        gq_row, gk_row = gq_ref[pl.ds(g, 1), :], gk_ref[pl.ds(g, 1), :]
        _normalize_into(q_ref, gq_row, qs, S)
        _normalize_into(k_ref, gk_row, ks, S)

        def dst(c):
            return dproj_ref.at[pl.ds(b * S, S), pl.ds((Q0, K0, V0)[c] + (g * HEADS + h) * HEAD_DIM, HEAD_DIM)]

        out = _Deferred(stage, sem, (b * HEADS + h) * N_GROUPS + g, B * HEADS * N_GROUPS - 1, 3)
        out.begin(dst)

        @pl.when((b == 0) & (h == 0) & (g == 0))
        def _():
            dgq_ref[...] = jnp.zeros_like(dgq_ref)
            dgk_ref[...] = jnp.zeros_like(dgk_ref)

        @pl.when(g == 0)
        def _():
            for c in range(S // 256):
                rows = pl.ds(c * 256, 256)
                z, a, dg_ = z_ref[rows, :], a_ref[rows, :], dag_ref[rows, :]
                sg = _sigmoid(z)
                da = dg_ * (z * sg)
                da_s[rows, :] = da
                dl_s[rows, :] = _row_sum(da * a)
                dz_stage[rows, :] = (dg_ * a * (sg * (1.0 + z * (1.0 - sg)))).astype(bf16)
            cp = pltpu.make_async_copy(dz_stage, dproj_ref.at[pl.ds(b * S, S), pl.ds(ZA + h * HEAD_DIM, HEAD_DIM)], dz_sem)
            cp.start()
            cp.wait()

        dk_s[...] = jnp.zeros_like(dk_s)
        dv_s[...] = jnp.zeros_like(dv_s)

        def run(gi):
            def unit(qr, kr, nkeys):
                q, k, v = qs[qr, :], ks[kr, :], v_ref[kr, :]
                s = _dot(q, k, NT_DIMS) * scale
                p = jnp.exp(s + (bias1 if nkeys == BLK else bias2)[...] - lse_ref[qr, :][:, :1])
                da = da_s[qr, :]
                dp = _dot(da, v, NT_DIMS)
                ds = p * (dp - dl_s[qr, :][:, :1]) * scale
                dq_s[qr, :] = _dot(ds, k, NN_DIMS)
                dk_s[kr, :] += _dot(ds, q, TN_DIMS)
                dv_s[kr, :] += _dot(p, da, TN_DIMS)

            _attn_units(S, DILATIONS[gi], lambda qr: unit(qr, qr, BLK), lambda qr, kr: unit(qr, kr, 2 * BLK))

        for gi in range(N_GROUPS):
            pl.when(g == gi)(functools.partial(run, gi))

        gq_acc = jnp.zeros((1, HEAD_DIM), f32)
        gk_acc = jnp.zeros((1, HEAD_DIM), f32)
        for c in range(S // 256):
            rows = pl.ds(c * 256, 256)
            dq, gq_p = _rms_bwd_rows(q_ref[rows, :], gq_row, dq_s[rows, :])
            dk, gk_p = _rms_bwd_rows(k_ref[rows, :], gk_row, dk_s[rows, :])
            gq_acc, gk_acc = gq_acc + gq_p, gk_acc + gk_p
            stage[out.slot, 0, rows, :] = dq.astype(bf16)
            stage[out.slot, 1, rows, :] = dk.astype(bf16)
            stage[out.slot, 2, rows, :] = dv_s[rows, :].astype(bf16)
        dgq_ref[pl.ds(g, 1), :] += gq_acc
        dgk_ref[pl.ds(g, 1), :] += gk_acc
        for c in range(3):
            out.start(c, dst(c))
        out.end(dst)

    def slab(col0):
        return pl.BlockSpec((S, HEAD_DIM), lambda b, h, g: (b, col0 // HEAD_DIM + g * HEADS + h))

    gain = pl.BlockSpec((N_GROUPS, HEAD_DIM), lambda b, h, g: (0, 0))
    per_slot = pl.BlockSpec((S, HEAD_DIM), lambda b, h, g: (b, h))
    return pl.pallas_call(
        body, grid=(B, HEADS, N_GROUPS),
        in_specs=[slab(Q0), slab(K0), slab(V0), pl.BlockSpec((S, HEAD_DIM), lambda b, h, g: (b, ZA // HEAD_DIM + h)), gain, gain,
                  per_slot, per_slot, per_slot, pl.BlockSpec(memory_space=pl.ANY)],
        out_specs=[pl.BlockSpec(memory_space=pl.ANY), gain, gain],
        out_shape=[jax.ShapeDtypeStruct(dproj.shape, dproj.dtype), jax.ShapeDtypeStruct((N_GROUPS, HEAD_DIM), f32),
                   jax.ShapeDtypeStruct((N_GROUPS, HEAD_DIM), f32)],
        scratch_shapes=[pltpu.VMEM((S, HEAD_DIM), f32)] * 7 + [pltpu.VMEM((BLK, BLK), f32), pltpu.VMEM((BLK, 2 * BLK), f32),
                                                                pltpu.VMEM((2, 3, S, HEAD_DIM), bf16), pltpu.VMEM((S, HEAD_DIM), bf16),
                                                                pltpu.SemaphoreType.DMA((2, 3)), pltpu.SemaphoreType.DMA],
        input_output_aliases={9: 0},
        compiler_params=_params(("arbitrary", "arbitrary", "arbitrary")), name="attn_bwd",
    )(proj, proj, proj, proj, gq, gk, a_comb, lse, dag, dproj)


def _conv_fwd(proj, conv_w, B, S):
    T, NC = proj.shape
    tc = LANES

    def body(cb_ref, cc_ref, cv_ref, z_ref, w_ref, c_ref, ub):
        u = cc_ref[...] * cv_ref[...]
        ub[0:8, :] = jnp.zeros((8, tc), f32)
        ub[8:8 + S, :] = u
        w = w_ref[...]
        y = w[0:1] * u + w[1:2] * ub[pl.ds(7, S), :] + w[2:3] * ub[pl.ds(6, S), :]
        z = z_ref[...]
        c_ref[...] = (cb_ref[...] * y * (z * _sigmoid(z))).astype(bf16)

    def seg(col0):
        return pl.BlockSpec((S, tc), lambda j, b: (b, col0 // tc + j))

    return pl.pallas_call(
        body, grid=(CONV_W // tc, B),
        in_specs=[seg(CB), seg(CC), seg(CV), seg(ZC), pl.BlockSpec((3, tc), lambda j, b: (0, j))],
        out_specs=pl.BlockSpec((S, tc), lambda j, b: (b, j)),
        out_shape=jax.ShapeDtypeStruct((T, CONV_W), bf16),
        scratch_shapes=[pltpu.VMEM((S + 8, tc), f32)],
        compiler_params=_params(("parallel", "parallel")), name="conv_fwd",
    )(proj, proj, proj, proj, conv_w)


def _conv_bwd(proj, conv_w, dc, dproj, B, S):
    T, NC = proj.shape
    tc = LANES

    def body(cb_ref, cc_ref, cv_ref, z_ref, w_ref, dc_ref, dproj_in, dproj_ref, dw_ref, ub, db, stage, sem):
        j, b = pl.program_id(0), pl.program_id(1)

        def dst(c):
            return dproj_ref.at[pl.ds(b * S, S), pl.ds((CB, CC, CV, ZC)[c] + j * tc, tc)]

        out = _Deferred(stage, sem, j * B + b, (CONV_W // tc) * B - 1, 4)
        out.begin(dst)
        cb, cc, cv, z, dcv = cb_ref[...], cc_ref[...], cv_ref[...], z_ref[...], dc_ref[...]
        u = cc * cv
        ub[0:8, :] = jnp.zeros((8, tc), f32)
        ub[8:8 + S, :] = u
        u1, u2 = ub[pl.ds(7, S), :], ub[pl.ds(6, S), :]
        w = w_ref[...]
        y = w[0:1] * u + w[1:2] * u1 + w[2:3] * u2
        sg = _sigmoid(z)
        si = z * sg
        dyv = dcv * cb * si
        db[0:S, :] = dyv
        db[S:S + 8, :] = jnp.zeros((8, tc), f32)
        du = w[0:1] * dyv + w[1:2] * db[pl.ds(1, S), :] + w[2:3] * db[pl.ds(2, S), :]
        stage[out.slot, 0] = (dcv * y * si).astype(bf16)
        stage[out.slot, 1] = (du * cv).astype(bf16)
        stage[out.slot, 2] = (du * cc).astype(bf16)
        stage[out.slot, 3] = (dcv * cb * y * (sg * (1.0 + z * (1.0 - sg)))).astype(bf16)
        for c in range(4):
            out.start(c, dst(c))
        part = jnp.concatenate([jnp.sum(dyv * u, axis=0, keepdims=True), jnp.sum(dyv * u1, axis=0, keepdims=True),
                                jnp.sum(dyv * u2, axis=0, keepdims=True)], axis=0)

        @pl.when(b == 0)
        def _():
            dw_ref[...] = part

        @pl.when(b > 0)
        def _():
            dw_ref[...] += part

        out.end(dst)

    def seg(col0):
        return pl.BlockSpec((S, tc), lambda j, b: (b, col0 // tc + j))

    return pl.pallas_call(
        body, grid=(CONV_W // tc, B),
        in_specs=[seg(CB), seg(CC), seg(CV), seg(ZC), pl.BlockSpec((3, tc), lambda j, b: (0, j)),
                  pl.BlockSpec((S, tc), lambda j, b: (b, j)), pl.BlockSpec(memory_space=pl.ANY)],
        out_specs=[pl.BlockSpec(memory_space=pl.ANY), pl.BlockSpec((3, tc), lambda j, b: (0, j))],
        out_shape=[jax.ShapeDtypeStruct(dproj.shape, dproj.dtype), jax.ShapeDtypeStruct((3, CONV_W), f32)],
        scratch_shapes=[pltpu.VMEM((S + 8, tc), f32), pltpu.VMEM((S + 8, tc), f32), pltpu.VMEM((2, 4, S, tc), bf16),
                        pltpu.SemaphoreType.DMA((2, 4))],
        input_output_aliases={6: 0},
        compiler_params=_params(("arbitrary", "arbitrary")), name="conv_bwd",
    )(proj, proj, proj, proj, conv_w, dc, dproj)


MEM_CHUNK = 256


def _mem_fwd(proj, mkv, gq, gk, B, S):
    T, NC = proj.shape
    scale = MEM_HD ** -0.5

    def body(q_ref, z_ref, k_ref, v_ref, gq_ref, gk_ref, o_ref):
        kv = k_ref[...]
        kn = (kv * _rstd_head(kv) * gk_ref[...]).astype(bf16)
        vv = v_ref[...].astype(bf16)

        def chunk(c, carry):
            rows = pl.ds(pl.multiple_of(c * MEM_CHUNK, MEM_CHUNK), MEM_CHUNK)
            q = q_ref[rows, :]
            qn = q * _rstd(q) * gq_ref[...]
            s = _dot(qn, kn, NT_DIMS) * scale
            p = jnp.exp(s - jnp.max(s, axis=-1, keepdims=True))
            p = p / jnp.sum(p, axis=-1, keepdims=True)
            z = z_ref[rows, :]
            o_ref[rows, :] = (_dot(p, vv, NN_DIMS) * (z * _sigmoid(z))).astype(bf16)
            return carry

        lax.fori_loop(0, S // MEM_CHUNK, chunk, 0)

    gain = pl.BlockSpec((1, MEM_HD), lambda b, h: (0, 0))
    return pl.pallas_call(
        body, grid=(B, MEM_HEADS),
        in_specs=[pl.BlockSpec((S, MEM_HD), lambda b, h: (b, MQ // MEM_HD + h)),
                  pl.BlockSpec((S, MEM_HD), lambda b, h: (b, ZM // MEM_HD + h)),
                  pl.BlockSpec((MEM_LEN, MEM_HD), lambda b, h: (b, h)),
                  pl.BlockSpec((MEM_LEN, MEM_HD), lambda b, h: (b, MEM_HEADS + h)), gain, gain],
        out_specs=pl.BlockSpec((S, MEM_HD), lambda b, h: (b, h)),
        out_shape=jax.ShapeDtypeStruct((T, MEM_W), bf16),
        compiler_params=_params(("parallel", "parallel")), name="mem_fwd",
    )(proj, proj, mkv, mkv, gq, gk)


def _mem_bwd(proj, mkv, gq, gk, dmo, dproj, B, S):
    T, NC = proj.shape
    scale = MEM_HD ** -0.5

    def body(q_ref, z_ref, k_ref, v_ref, gq_ref, gk_ref, dmo_ref, dproj_in, dproj_ref, dmk_ref, dmv_ref, dgq_ref, dgk_ref,
             dkn_s, dv_s, gq_s, stage, sem):
        b, h = pl.program_id(0), pl.program_id(1)

        def dst(c):
            return dproj_ref.at[pl.ds(b * S, S), pl.ds((MQ, ZM)[c] + h * MEM_HD, MEM_HD)]

        out = _Deferred(stage, sem, b * MEM_HEADS + h, B * MEM_HEADS - 1, 2)
        out.begin(dst)
        kv = k_ref[...]
        kn = (kv * _rstd_head(kv) * gk_ref[...]).astype(bf16)
        vv = v_ref[...].astype(bf16)
        dkn_s[...] = jnp.zeros_like(dkn_s)
        dv_s[...] = jnp.zeros_like(dv_s)
        gq_s[...] = jnp.zeros_like(gq_s)

        def chunk(c, carry):
            rows = pl.ds(pl.multiple_of(c * MEM_CHUNK, MEM_CHUNK), MEM_CHUNK)
            q = q_ref[rows, :]
            qn = q * _rstd(q) * gq_ref[...]
            s = _dot(qn, kn, NT_DIMS) * scale
            p = jnp.exp(s - jnp.max(s, axis=-1, keepdims=True))
            p = p / jnp.sum(p, axis=-1, keepdims=True)
            mo = _dot(p, vv, NN_DIMS)
            z, dg_ = z_ref[rows, :], dmo_ref[rows, :]
            sg = _sigmoid(z)
            do = dg_ * (z * sg)
            stage[out.slot, 1, rows, :] = (dg_ * mo * (sg * (1.0 + z * (1.0 - sg)))).astype(bf16)
            dp = _dot(do, vv, NT_DIMS)
            ds = p * (dp - jnp.sum(do * mo, axis=-1, keepdims=True)) * scale
            dq, gq_p = _rms_bwd_rows(q, gq_ref[...], _dot(ds, kn, NN_DIMS), on_mxu=False)
            stage[out.slot, 0, rows, :] = dq.astype(bf16)
            gq_s[...] += gq_p
            dkn_s[...] += _dot(ds, qn, TN_DIMS)
            dv_s[...] += _dot(p, do, TN_DIMS)
            return carry

        lax.fori_loop(0, S // MEM_CHUNK, chunk, 0)
        for c in range(2):
            out.start(c, dst(c))
        dk, gk_p = _rms_bwd_rows(kv, gk_ref[...], dkn_s[...])
        dmk_ref[...] = dk
        dmv_ref[...] = dv_s[...]

        @pl.when((b == 0) & (h == 0))
        def _():
            dgq_ref[...] = gq_s[...]
            dgk_ref[...] = gk_p

        @pl.when((b > 0) | (h > 0))
        def _():
            dgq_ref[...] += gq_s[...]
            dgk_ref[...] += gk_p

        out.end(dst)

    gain = pl.BlockSpec((1, MEM_HD), lambda b, h: (0, 0))
    kvb = pl.BlockSpec((MEM_LEN, MEM_HD), lambda b, h: (b, h))
    return pl.pallas_call(
        body, grid=(B, MEM_HEADS),
        in_specs=[pl.BlockSpec((S, MEM_HD), lambda b, h: (b, MQ // MEM_HD + h)),
                  pl.BlockSpec((S, MEM_HD), lambda b, h: (b, ZM // MEM_HD + h)),
                  kvb, pl.BlockSpec((MEM_LEN, MEM_HD), lambda b, h: (b, MEM_HEADS + h)), gain, gain,
                  pl.BlockSpec((S, MEM_HD), lambda b, h: (b, h)), pl.BlockSpec(memory_space=pl.ANY)],
        out_specs=[pl.BlockSpec(memory_space=pl.ANY), kvb, kvb, gain, gain],
        out_shape=[jax.ShapeDtypeStruct(dproj.shape, dproj.dtype), jax.ShapeDtypeStruct((B * MEM_LEN, MEM_W), f32),
                   jax.ShapeDtypeStruct((B * MEM_LEN, MEM_W), f32), jax.ShapeDtypeStruct((1, MEM_HD), f32),
                   jax.ShapeDtypeStruct((1, MEM_HD), f32)],
        scratch_shapes=[pltpu.VMEM((MEM_LEN, MEM_HD), f32), pltpu.VMEM((MEM_LEN, MEM_HD), f32), pltpu.VMEM((1, MEM_HD), f32),
                        pltpu.VMEM((2, 2, S, MEM_HD), bf16), pltpu.SemaphoreType.DMA((2, 2))],
        input_output_aliases={7: 0},
        compiler_params=_params(("arbitrary", "arbitrary")), name="mem_bwd",
    )(proj, proj, mkv, mkv, gq, gk, dmo, dproj)


def _merge_fwd(proj, ag, cg, mg, wa, wc, wm):
    T, NC = proj.shape
    D = wa.shape[1]
    tm, tn = _pick(T, 1024), _pick(D, 512)
    assert GT % tn == 0

    def body(ag_ref, cg_ref, mg_ref, wa_ref, wc_ref, wm_ref, g0_ref, g1_ref, g2_ref, mer_ref, ba_ref, bc_ref, bm_ref):
        ba = _dot(ag_ref[...], wa_ref[...], NN_DIMS)
        bc = _dot(cg_ref[...], wc_ref[...], NN_DIMS)
        bm = _dot(mg_ref[...], wm_ref[...], NN_DIMS)
        mer_ref[...] = (_sigmoid(g0_ref[...]) * ba + _sigmoid(g1_ref[...]) * bc + _sigmoid(g2_ref[...]) * bm).astype(bf16)
        ba_ref[...] = ba.astype(bf16)
        bc_ref[...] = bc.astype(bf16)
        bm_ref[...] = bm.astype(bf16)

    def act(w):
        return pl.BlockSpec((tm, w), lambda i, j: (i, 0))

    def wt(k):
        return pl.BlockSpec((k, tn), lambda i, j: (0, j))

    def gate(n):
        return pl.BlockSpec((tm, tn), lambda i, j: (i, (GT + n * D) // tn + j))

    out = pl.BlockSpec((tm, tn), lambda i, j: (i, j))
    return pl.pallas_call(
        body, grid=(T // tm, D // tn),
        in_specs=[act(ATTN_OUT), act(CONV_W), act(MEM_W), wt(ATTN_OUT), wt(CONV_W), wt(MEM_W), gate(0), gate(1), gate(2)],
        out_specs=[out] * 4, out_shape=[jax.ShapeDtypeStruct((T, D), bf16)] * 4,
        compiler_params=_params(("parallel", "parallel")), name="merge_fwd",
    )(ag, cg, mg, wa, wc, wm, proj, proj, proj)


def _out_loss(merged, w_out, x, tgt):
    T, D = x.shape
    tm, tn = _pick(T, 512), _pick(D, 512)

    def body(m_ref, w_ref, x_ref, t_ref, dy_ref, dyb_ref, lp_ref):
        e = x_ref[...] + _dot(m_ref[...], w_ref[...], NN_DIMS) - t_ref[...]
        dy = e / D
        dy_ref[...] = dy
        dyb_ref[...] = dy.astype(bf16)
        part = jnp.full((1, 8, LANES), jnp.sum(e * e), f32)

        @pl.when(pl.program_id(1) == 0)
        def _():
            lp_ref[...] = part

        @pl.when(pl.program_id(1) > 0)
        def _():
            lp_ref[...] += part

    tile = pl.BlockSpec((tm, tn), lambda i, j: (i, j))
    return pl.pallas_call(
        body, grid=(T // tm, D // tn),
        in_specs=[pl.BlockSpec((tm, D), lambda i, j: (i, 0)), pl.BlockSpec((D, tn), lambda i, j: (0, j)), tile, tile],
        out_specs=[tile, tile, pl.BlockSpec((1, 8, LANES), lambda i, j: (i, 0, 0))],
        out_shape=[jax.ShapeDtypeStruct((T, D), f32), jax.ShapeDtypeStruct((T, D), bf16),
                   jax.ShapeDtypeStruct((T // tm, 8, LANES), f32)],
        compiler_params=_params(("parallel", "arbitrary")), name="out_loss",
    )(merged, w_out, x, tgt)


def _merge_bwd(proj, dyb, w_out, ba, bc, bm):
    T, NC = proj.shape
    D = w_out.shape[0]
    tm, tn = _pick(T, 512), _pick(D, 512)
    assert GT % tn == 0

    def body(dy_ref, w_ref, ba_ref, bc_ref, bm_ref, g0_ref, g1_ref, g2_ref, da_ref, dc_ref, dm_ref, dproj_ref, stage, sem):
        i, j = pl.program_id(0), pl.program_id(1)

        def dst(n):
            return dproj_ref.at[pl.ds(i * tm, tm), pl.ds(GT + n * D + j * tn, tn)]

        out = _Deferred(stage, sem, i * (D // tn) + j, (T // tm) * (D // tn) - 1, 3)
        out.begin(dst)
        dmer = _dot(dy_ref[...], w_ref[...], NT_DIMS)
        for n, (g_ref, br_ref, o_ref) in enumerate(((g0_ref, ba_ref, da_ref), (g1_ref, bc_ref, dc_ref), (g2_ref, bm_ref, dm_ref))):
            sg = _sigmoid(g_ref[...])
            o_ref[...] = (sg * dmer).astype(bf16)
            stage[out.slot, n] = (dmer * br_ref[...].astype(f32) * (sg * (1.0 - sg))).astype(bf16)
            out.start(n, dst(n))
        out.end(dst)

    tile = pl.BlockSpec((tm, tn), lambda i, j: (i, j))

    def gate(n):
        return pl.BlockSpec((tm, tn), lambda i, j: (i, (GT + n * D) // tn + j))

    return pl.pallas_call(
        body, grid=(T // tm, D // tn),
        in_specs=[pl.BlockSpec((tm, D), lambda i, j: (i, 0)), pl.BlockSpec((tn, D), lambda i, j: (j, 0)), tile, tile, tile,
                  gate(0), gate(1), gate(2)],
        out_specs=[tile, tile, tile, pl.BlockSpec(memory_space=pl.ANY)],
        out_shape=[jax.ShapeDtypeStruct((T, D), bf16)] * 3 + [jax.ShapeDtypeStruct((T, NC), bf16)],
        scratch_shapes=[pltpu.VMEM((2, 3, tm, tn), bf16), pltpu.SemaphoreType.DMA((2, 3))],
        compiler_params=_params(("arbitrary", "arbitrary")), name="merge_bwd",
    )(dyb, w_out, ba, bc, bm, proj, proj, proj)


def _fwd_bwd_head(x2, mem2, t2, proj, attn, B, S, mem_norm_g, attn_q_norm, attn_k_norm, conv_w, mem_q_norm, mem_k_norm,
                  w_kv, w_a, w_c, w_m, w_out):
    D = x2.shape[1]
    mng = mem_norm_g.reshape(1, D)
    mqn, mkn = mem_q_norm.reshape(1, MEM_HD), mem_k_norm.reshape(1, MEM_HD)
    ag, a_comb, lse = attn

    mh = _rms_fwd(mem2, mng, "rms_mem")
    mkv = _mm(mh, w_kv, mode="nn", out_dtype=f32, name="mkv")
    cg = _conv_fwd(proj, conv_w, B, S)
    mg = _mem_fwd(proj, mkv, mqn, mkn, B, S)
    merged, ba, bc, bm = _merge_fwd(proj, ag, cg, mg, w_a, w_c, w_m)
    dy, dyb, lp = _out_loss(merged, w_out, x2, t2)
    sq_err = jnp.sum(lp[:, 0, 0])

    g = {}
    g["w_out"] = _mm(merged, dyb, mode="tn", out_dtype=f32, name="dw_out")
    dba, dbc, dbm, dproj = _merge_bwd(proj, dyb, w_out, ba, bc, bm)
    g["w_br_attn"] = _mm(ag, dba, mode="tn", out_dtype=f32, name="dw_br_attn")
    g["w_br_conv"] = _mm(cg, dbc, mode="tn", out_dtype=f32, name="dw_br_conv")
    g["w_br_mem"] = _mm(mg, dbm, mode="tn", out_dtype=f32, name="dw_br_mem")
    dag = _mm(dba, w_a, mode="nt", out_dtype=f32, name="d_attn_out")
    dcg = _mm(dbc, w_c, mode="nt", out_dtype=f32, name="d_conv_out")
    dmg = _mm(dbm, w_m, mode="nt", out_dtype=f32, name="d_mem_out")
    dproj, g["attn_q_norm"], g["attn_k_norm"] = _attn_bwd(proj, attn_q_norm, attn_k_norm, a_comb, lse, dag, dproj, B, S)
    dproj, g["conv_w"] = _conv_bwd(proj, conv_w, dcg, dproj, B, S)
    dproj, dmk, dmv, dgmq, dgmk = _mem_bwd(proj, mkv, mqn, mkn, dmg, dproj, B, S)
    g["mem_q_norm"], g["mem_k_norm"] = dgmq.reshape(MEM_HD), dgmk.reshape(MEM_HD)
    dmkv = jnp.concatenate([dmk, dmv], axis=1)
    dmh = _mm(dmkv, w_kv, mode="nt", out_dtype=f32, name="d_mem_h")
    g["mem_norm_g"] = _rms_bwd(mem2, mng, dmh, None, "rms_mem_bwd").reshape(D)
    return sq_err, g, dict(dy=dy, dproj=dproj, mh=mh, dmkv=dmkv)


MESH = pl.DeviceIdType.MESH
HBM = pl.BlockSpec(memory_space=pl.ANY)


def _me():
    x, y, c = lax.axis_index("x"), lax.axis_index("y"), lax.axis_index("c")
    return (x, y, c), 4 * x + 2 * y + c


def _peer(k):
    (x, y, c), _ = _me()
    p = (1 - x if k & 4 else x, 1 - y if k & 2 else y, 1 - c if k & 1 else c)
    return p, 4 * p[0] + 2 * p[1] + p[2]


def _window(ref, axis, size, idx):
    if axis is None:
        return ref.at[idx]
    if axis == 0:
        return ref.at[pl.ds(idx * size, size), :]
    return ref.at[:, pl.ds(idx * size, size)]


def _full_shape(s, axis):
    if axis is None:
        return (N_DEV,) + s.shape
    return tuple(N_DEV * d if i == axis else d for i, d in enumerate(s.shape))


OTHER_CHIPS = (4, 2, 6)


def _spread_comm(shards, axes):
    n = len(shards)

    def copies(ins, outs, send_sems, recv_sems, base=0):
        _, me = _me()
        out = []
        for a in range(n):
            mine = _window(outs[a], axes[a], None if axes[a] is None else shards[a].shape[axes[a]], me)
            out.append(pltpu.make_async_copy(ins[a], mine, send_sems.at[base + a, N_CHIP]))
            for k, dist in enumerate((1,) + OTHER_CHIPS):
                dev, _ = _peer(dist)
                out.append(pltpu.make_async_remote_copy(
                    src_ref=ins[a], dst_ref=mine, send_sem=send_sems.at[base + a, k], recv_sem=recv_sems.at[base + a, k],
                    device_id=dev, device_id_type=MESH))
        return out

    shapes = [jax.ShapeDtypeStruct(_full_shape(s, ax), s.dtype) for s, ax in zip(shards, axes)]
    return _Comm(shards, shapes, (n, N_CHIP + 1), copies)


def _forward_blocks(fulls, axes):
    n = len(fulls)
    sizes = [None if ax is None else f.shape[ax] // N_DEV for f, ax in zip(fulls, axes)]

    def body(*refs):
        outs = refs[n:2 * n]
        send_sems, recv_sems = refs[2 * n:]
        sibling, _ = _peer(1)
        copies = []
        for j, dist in enumerate(OTHER_CHIPS):
            _, held = _peer(dist)
            for a in range(n):
                block = _window(outs[a], axes[a], sizes[a], held)
                copies.append(pltpu.make_async_remote_copy(
                    src_ref=block, dst_ref=block, send_sem=send_sems.at[a, j], recv_sem=recv_sems.at[a, j],
                    device_id=sibling, device_id_type=MESH))
        for cp in copies:
            cp.start()
        for cp in copies:
            cp.wait()

    return pl.pallas_call(
        body, in_specs=[HBM] * n, out_specs=[HBM] * n,
        out_shape=[jax.ShapeDtypeStruct(f.shape, f.dtype) for f in fulls],
        scratch_shapes=[pltpu.SemaphoreType.DMA((n, len(OTHER_CHIPS))), pltpu.SemaphoreType.DMA((n, len(OTHER_CHIPS)))],
        input_output_aliases={a: a for a in range(n)},
        name="forward_blocks",
    )(*fulls)


def _shard_order():
    (x, y, c), me = _me()
    ids = [me, _peer(1)[1]]
    for dist in OTHER_CHIPS:
        ids += [_peer(dist)[1], _peer(dist + 1)[1]]
    return jnp.stack(ids).astype(jnp.int32)


def _proj_gather(h, w_shard, order, comm):
    T, K = h.shape
    C = w_shard.shape[1]
    tm = _pick(T, 1024)
    nm = T // tm
    ahead = max(nm - 2, 0)
    n_ci, n_co = len(comm.ins), len(comm.out_shapes)
    comm_at = N_DEV // 2 - 1

    def body(*refs):
        order_ref, h_ref, ws_ref = refs[:3]
        c_ins = refs[3:3 + n_ci]
        o_ref, wf_ref = refs[3 + n_ci:5 + n_ci]
        c_outs = refs[5 + n_ci:5 + n_ci + n_co]
        wbuf, send_sems, recv_sems, own_sem, buf_sems, c_send, c_recv = refs[5 + n_ci + n_co:]
        t, i = pl.program_id(0), pl.program_id(1)

        @pl.when((t == comm_at) & (i == 0))
        def _():
            for cp in comm.copies(c_ins, c_outs, c_send, c_recv):
                cp.start()
        (x, y, c), me = _me()
        sibling, sib_id = _peer(1)
        chips = [_peer(dist) for dist in OTHER_CHIPS]

        def win(idx):
            return wf_ref.at[:, pl.ds(idx * C, C)]

        def copy(k, block, to, src=None):
            return pltpu.make_async_remote_copy(
                src_ref=win(block) if src is None else src, dst_ref=win(block),
                send_sem=send_sems.at[k], recv_sem=recv_sems.at[k], device_id=to, device_id_type=MESH)

        def fetch(tt, src):
            return pltpu.make_async_copy(src, wbuf.at[tt % 2], buf_sems.at[tt % 2])

        own = pltpu.make_async_copy(ws_ref, win(me), own_sem)

        @pl.when((t == 0) & (i == 0))
        def _():
            fetch(0, ws_ref).start()
            own.start()
            copy(0, me, sibling, src=ws_ref).start()
            for j, (dev, _) in enumerate(chips):
                copy(1 + j, me, dev, src=ws_ref).start()

        for tt in range(N_DEV):
            @pl.when((t == tt) & (i == 0))
            def _(tt=tt):
                fetch(tt, ws_ref).wait()

        o_ref[...] = _dot(h_ref[...], wbuf[t % 2], NN_DIMS)

        for tt in range(N_DEV - 1):
            @pl.when((t == tt) & (i == ahead))
            def _(tt=tt):
                nxt = tt + 1
                j = (nxt - 2) // 2
                if nxt == 1:
                    copy(0, sib_id, (x, y, c)).wait_recv()
                    block = sib_id
                elif nxt % 2 == 0:
                    block = chips[j][1]
                    copy(1 + j, block, (x, y, c)).wait_recv()
                    copy(4 + j, block, sibling).start()
                else:
                    block = chips[j][1] + 1 - 2 * c
                    copy(4 + j, block, (x, y, c)).wait_recv()
                fetch(nxt, win(block)).start()

        @pl.when((t == N_DEV - 1) & (i == nm - 1))
        def _():
            copy(0, me, sibling, src=ws_ref).wait_send()
            for j, (dev, dev_id) in enumerate(chips):
                copy(1 + j, me, dev, src=ws_ref).wait_send()
                copy(4 + j, dev_id, sibling).wait_send()
            own.wait()
            for cp in comm.copies(c_ins, c_outs, c_send, c_recv):
                cp.wait()

    hbm = pl.BlockSpec(memory_space=pl.ANY)
    res = pl.pallas_call(
        body,
        grid_spec=pltpu.PrefetchScalarGridSpec(
            num_scalar_prefetch=1, grid=(N_DEV, nm),
            in_specs=[pl.BlockSpec((tm, K), lambda t, i, order_ref: (i, 0)), hbm] + [hbm] * n_ci,
            out_specs=[pl.BlockSpec((tm, C), lambda t, i, order_ref: (i, order_ref[t])), hbm] + [hbm] * n_co,
            scratch_shapes=[pltpu.VMEM((2, K, C), bf16), pltpu.SemaphoreType.DMA((7,)), pltpu.SemaphoreType.DMA((7,)),
                            pltpu.SemaphoreType.DMA, pltpu.SemaphoreType.DMA((2,))] + comm.scratch()),
        out_shape=[jax.ShapeDtypeStruct((T, N_DEV * C), f32), jax.ShapeDtypeStruct((K, N_DEV * C), bf16)] + comm.out_shapes,
        compiler_params=_params(("arbitrary", "arbitrary")), name="proj_gather",
    )(order, h, w_shard, *comm.ins)
    return res[0], res[1], list(res[2:])


N_CHIP = 4


def _shard_shape(g, ax):
    return tuple(d // N_DEV if i == ax else d for i, d in enumerate(g.shape))


def _sibling_comm(grads, axes):
    n = len(grads)
    sizes = [g.shape[ax] // N_DEV for g, ax in zip(grads, axes)]

    def copies(ins, lands, send_sems, recv_sems, base=0):
        sibling, _ = _peer(1)
        out = []
        for j in range(N_CHIP):
            _, owner = _peer(2 * j + 1)
            for a in range(n):
                out.append(pltpu.make_async_remote_copy(
                    src_ref=_window(ins[a], axes[a], sizes[a], owner), dst_ref=lands[a].at[j],
                    send_sem=send_sems.at[base + a, j], recv_sem=recv_sems.at[base + a, j], device_id=sibling,
                    device_id_type=MESH))
        return out

    shapes = [jax.ShapeDtypeStruct((N_CHIP,) + _shard_shape(g, ax), g.dtype) for g, ax in zip(grads, axes)]
    return _Comm(grads, shapes, (n, N_CHIP), copies)


def _chips_comm(sums):
    n = len(sums)

    def copies(ins, lands, send_sems, recv_sems, base=0):
        out = []
        for j in range(1, N_CHIP):
            dev, _ = _peer(2 * j)
            for a in range(n):
                out.append(pltpu.make_async_remote_copy(
                    src_ref=ins[a].at[j - 1], dst_ref=lands[a].at[j - 1],
                    send_sem=send_sems.at[base + a, j - 1], recv_sem=recv_sems.at[base + a, j - 1], device_id=dev,
                    device_id_type=MESH))
        return out

    return _Comm(sums, [jax.ShapeDtypeStruct(s.shape, s.dtype) for s in sums], (n, N_CHIP), copies)


def _join(c1, c2):
    n1, m1, k1 = len(c1.ins), len(c1.out_shapes), c1.sem_shape[0]

    def copies(ins, lands, send_sems, recv_sems):
        return (c1.copies(ins[:n1], lands[:m1], send_sems, recv_sems, base=0)
                + c2.copies(ins[n1:], lands[m1:], send_sems, recv_sems, base=k1))

    return _Comm(c1.ins + c2.ins, c1.out_shapes + c2.out_shapes, (k1 + c2.sem_shape[0], N_CHIP), copies)


def _rs_chip_sum(grad, land, xyc, axis, name):
    R, C = land.shape[1:]
    tr = _pick(R, max(8, (256 * 1024) // C), 8)

    def body(xyc_ref, g_ref, l_ref, o_ref):
        o_ref[0] = (g_ref[...] + l_ref[0]).astype(bf16)

    def owner(j, xyc_ref):
        jj = j + 1
        return 4 * (xyc_ref[0] ^ (jj >> 1)) + 2 * (xyc_ref[1] ^ (jj & 1)) + xyc_ref[2]

    if axis == 0:
        g_spec = pl.BlockSpec((tr, C), lambda j, i, xyc_ref: (owner(j, xyc_ref) * (R // tr) + i, 0))
    else:
        g_spec = pl.BlockSpec((tr, C), lambda j, i, xyc_ref: (i, owner(j, xyc_ref)))
    return pl.pallas_call(
        body,
        grid_spec=pltpu.PrefetchScalarGridSpec(
            num_scalar_prefetch=1, grid=(N_CHIP - 1, R // tr),
            in_specs=[g_spec, pl.BlockSpec((1, tr, C), lambda j, i, xyc_ref: (j + 1, i, 0))],
            out_specs=pl.BlockSpec((1, tr, C), lambda j, i, xyc_ref: (j, i, 0))),
        out_shape=jax.ShapeDtypeStruct((N_CHIP - 1, R, C), bf16),
        compiler_params=_params(("parallel", "parallel")), name=name,
    )(xyc, grad, land)


def _allreduce_small(part):
    R = part.shape[0]

    def body(p_ref, o_ref, buf, send_sems, recv_sems):
        (x, y, c), me = _me()
        buf[me] = p_ref[...]
        copies = []
        for k in range(1, N_DEV):
            dev, dev_id = _peer(k)
            copies.append(pltpu.make_async_remote_copy(
                src_ref=p_ref, dst_ref=buf.at[me], send_sem=send_sems.at[k - 1], recv_sem=recv_sems.at[k - 1],
                device_id=dev, device_id_type=MESH))
        for cp in copies:
            cp.start()
        for k in range(1, N_DEV):
            _, dev_id = _peer(k)
            pltpu.make_async_remote_copy(
                src_ref=p_ref, dst_ref=buf.at[dev_id], send_sem=send_sems.at[k - 1], recv_sem=recv_sems.at[k - 1],
                device_id=(x, y, c), device_id_type=MESH).wait_recv()
        for cp in copies:
            cp.wait_send()
        acc = buf[0]
        for d in range(1, N_DEV):
            acc = acc + buf[d]
        o_ref[...] = acc

    return pl.pallas_call(
        body, in_specs=[pl.BlockSpec(memory_space=pltpu.VMEM)], out_specs=pl.BlockSpec(memory_space=pltpu.VMEM),
        out_shape=jax.ShapeDtypeStruct((R, LANES), f32),
        scratch_shapes=[pltpu.VMEM((N_DEV, R, LANES), f32), pltpu.SemaphoreType.DMA((N_DEV - 1,)), pltpu.SemaphoreType.DMA((N_DEV - 1,))],
        name="allreduce_small",
    )(part)


def _adamw_math(w, g, m, v):
    m2 = ADAM_B1 * m + (1.0 - ADAM_B1) * g
    v2 = ADAM_B2 * v + (1.0 - ADAM_B2) * (g * g)
    m_hat = m2 / (1.0 - ADAM_B1 ** ADAM_STEP)
    v_hat = v2 / (1.0 - ADAM_B2 ** ADAM_STEP)
    return -ADAM_LR * (m_hat / (jnp.sqrt(v_hat) + ADAM_EPS) + ADAM_WD * w), m2, v2


def _adamw_shard(grad, land_sib, land_chips, w, m, v, me, axis, name, row0=0, prev=None):
    R, C = land_sib.shape[1:]
    assert axis == 1 or R == w.shape[0]
    tr = _pick(R, max(8, (128 * 1024) // C), 8)
    r0 = row0 // tr
    n_prev = len(prev) if prev else 0

    def body(me_ref, g_ref, s_ref, l_ref, w_ref, m_ref, v_ref, *rest):
        go_ref, d_ref, mo_ref, vo_ref = rest[n_prev:]
        g = g_ref[...] + s_ref[0]
        for j in range(N_CHIP - 1):
            g = g + l_ref[j].astype(f32)
        d, m2, v2 = _adamw_math(w_ref[...], g, m_ref[...], v_ref[...])
        go_ref[...] = g
        d_ref[...] = d
        mo_ref[...] = m2
        vo_ref[...] = v2

    if axis == 0:
        g_spec = pl.BlockSpec((tr, C), lambda i, me_ref: (me_ref[0] * (R // tr) + i, 0))
    else:
        g_spec = pl.BlockSpec((tr, C), lambda i, me_ref: (i, me_ref[0]))
    blk = pl.BlockSpec((tr, C), lambda i, me_ref: (r0 + i, 0))
    return pl.pallas_call(
        body,
        grid_spec=pltpu.PrefetchScalarGridSpec(
            num_scalar_prefetch=1, grid=(R // tr,),
            in_specs=[g_spec, pl.BlockSpec((1, tr, C), lambda i, me_ref: (0, i, 0)),
                      pl.BlockSpec((N_CHIP - 1, tr, C), lambda i, me_ref: (0, i, 0)), blk, blk, blk]
            + [pl.BlockSpec(memory_space=pl.ANY)] * n_prev,
            out_specs=[blk] * 4),
        out_shape=[jax.ShapeDtypeStruct(w.shape, f32)] * 4,
        input_output_aliases={7 + i: i for i in range(n_prev)},
        compiler_params=_params(("parallel",)), name=name,
    )(me, grad, land_sib, land_chips, w, m, v, *(prev or []))


def _adamw_small(g, w, m, v):
    def body(g_ref, w_ref, m_ref, v_ref, d_ref, mo_ref, vo_ref):
        d, m2, v2 = _adamw_math(w_ref[...], g_ref[...], m_ref[...], v_ref[...])
        d_ref[...] = d
        mo_ref[...] = m2
        vo_ref[...] = v2

    return pl.pallas_call(body, out_shape=[jax.ShapeDtypeStruct(g.shape, f32)] * 3, name="adamw_small")(g, w, m, v)


def _pack_rows(parts, total_rows):
    rows = jnp.concatenate([p.reshape(-1, LANES) for p in parts], axis=0)
    return jnp.pad(rows, ((0, total_rows - rows.shape[0]), (0, 0)))


BIG = ("w_in", "mem_w_kv", "w_br_attn", "w_br_conv", "w_br_mem", "w_out")
BIG_AXIS = {"w_in": 1, "mem_w_kv": 0, "w_br_attn": 1, "w_br_conv": 1, "w_br_mem": 1, "w_out": 0}
SMALL = ("norm_g", "mem_norm_g", "attn_q_norm", "attn_k_norm", "mem_q_norm", "mem_k_norm")
ALL_W = ("norm_g", "mem_norm_g", "w_in", "attn_q_norm", "attn_k_norm", "conv_w", "mem_w_kv", "mem_q_norm", "mem_k_norm",
         "w_br_attn", "w_br_conv", "w_br_mem", "w_out")


def kernel(x, mem, norm_g, mem_norm_g, w_in, attn_q_norm, attn_k_norm, conv_w, mem_w_kv, mem_q_norm, mem_k_norm, w_br_attn, w_br_conv, w_br_mem, w_out, loss_target, m_norm_g, m_mem_norm_g, m_w_in, m_attn_q_norm, m_attn_k_norm, m_conv_w, m_mem_w_kv, m_mem_q_norm, m_mem_k_norm, m_w_br_attn, m_w_br_conv, m_w_br_mem, m_w_out, v_norm_g, v_mem_norm_g, v_w_in, v_attn_q_norm, v_attn_k_norm, v_conv_w, v_mem_w_kv, v_mem_q_norm, v_mem_k_norm, v_w_br_attn, v_w_br_conv, v_w_br_mem, v_w_out):
    w = dict(norm_g=norm_g, mem_norm_g=mem_norm_g, w_in=w_in, attn_q_norm=attn_q_norm, attn_k_norm=attn_k_norm, conv_w=conv_w,
             mem_w_kv=mem_w_kv, mem_q_norm=mem_q_norm, mem_k_norm=mem_k_norm, w_br_attn=w_br_attn, w_br_conv=w_br_conv,
             w_br_mem=w_br_mem, w_out=w_out)
    mo = dict(norm_g=m_norm_g, mem_norm_g=m_mem_norm_g, w_in=m_w_in, attn_q_norm=m_attn_q_norm, attn_k_norm=m_attn_k_norm,
              conv_w=m_conv_w, mem_w_kv=m_mem_w_kv, mem_q_norm=m_mem_q_norm, mem_k_norm=m_mem_k_norm, w_br_attn=m_w_br_attn,
              w_br_conv=m_w_br_conv, w_br_mem=m_w_br_mem, w_out=m_w_out)
    vo = dict(norm_g=v_norm_g, mem_norm_g=v_mem_norm_g, w_in=v_w_in, attn_q_norm=v_attn_q_norm, attn_k_norm=v_attn_k_norm,
              conv_w=v_conv_w, mem_w_kv=v_mem_w_kv, mem_q_norm=v_mem_q_norm, mem_k_norm=v_mem_k_norm, w_br_attn=v_w_br_attn,
              w_br_conv=v_w_br_conv, w_br_mem=v_w_br_mem, w_out=v_w_out)
    B, S, D = x.shape
    me = (4 * lax.axis_index("x") + 2 * lax.axis_index("y") + lax.axis_index("c")).astype(jnp.int32)

    T = B * S
    x2, t2, mem2, ng = x.reshape(T, D), loss_target.reshape(T, D), mem.reshape(B * MEM_LEN, D), norm_g.reshape(1, D)
    h = _rms_fwd(x2, ng, "rms_x")
    later = BIG[1:]
    later_axes = [BIG_AXIS[n] for n in later] + [None]
    conv_pad = jnp.pad(conv_w, ((0, 8 - conv_w.shape[0]), (0, 0)))
    proj, w_in_full, spread = _proj_gather(h, w_in.astype(bf16), _shard_order(),
                                           _spread_comm([w[n].astype(bf16) for n in later] + [conv_pad], later_axes))
    attn = _attn_fwd(proj, attn_q_norm, attn_k_norm, B, S)[:3]
    *fulls, conv_g = _forward_blocks(spread, later_axes)
    wf = dict(zip(later, fulls), w_in=w_in_full)
    conv_full = conv_g[:, :3, :].transpose(1, 0, 2).reshape(3, CONV_W)

    sq_err, g, t = _fwd_bwd_head(x2, mem2, t2, proj, attn, B, S, mem_norm_g, attn_q_norm, attn_k_norm, conv_full, mem_q_norm,
                                 mem_k_norm, wf["mem_w_kv"], wf["w_br_attn"], wf["w_br_conv"], wf["w_br_mem"], wf["w_out"])
    t.update(x2=x2, ng=ng, h=h)

    xyc = jnp.stack([lax.axis_index("x"), lax.axis_index("y"), lax.axis_index("c")]).astype(jnp.int32)
    me1 = me.reshape(1)
    early = ("w_out", "w_br_attn", "w_br_conv", "w_br_mem")
    g["mem_w_kv"], sib_early = _mm(t["mh"], t["dmkv"], mode="tn", out_dtype=f32, name="dw_kv",
                                   comm=_sibling_comm([g[n] for n in early], [BIG_AXIS[n] for n in early]))
    sums_early = [_rs_chip_sum(g[n], ls, xyc, BIG_AXIS[n], "rs_sum_" + n) for n, ls in zip(early, sib_early)]
    tm_w = _pick(D // 2, 1024)
    half = (D // 2) // tm_w
    g_top, (*chips_early, sib_kv) = _mm(t["h"], t["dproj"], mode="tn", out_dtype=f32, name="dw_in_top", tm=tm_w,
                                        m_tiles=(0, half),
                                        comm=_join(_chips_comm(sums_early), _sibling_comm([g["mem_w_kv"]], [0])))
    sum_kv = _rs_chip_sum(g["mem_w_kv"], sib_kv, xyc, 0, "rs_sum_mem_w_kv")
    g_bot, (chips_kv, sib_top) = _mm(t["h"], t["dproj"], mode="tn", out_dtype=f32, name="dw_in_bot", tm=tm_w,
                                     m_tiles=(half, half), comm=_join(_chips_comm([sum_kv]), _sibling_comm([g_top], [1])))
    sum_top = _rs_chip_sum(g_top, sib_top, xyc, 1, "rs_sum_w_in_top")
    tm_t = _pick(T // 2, 1024)
    halfm = (T // 2) // tm_t
    dh, (chips_top, sib_bot) = _mm(t["dproj"], wf["w_in"], mode="nt", out_dtype=f32, name="d_h_a", tm=tm_t,
                                   m_tiles=(0, halfm), into="new",
                                   comm=_join(_chips_comm([sum_top]), _sibling_comm([g_bot], [1])))
    sum_bot = _rs_chip_sum(g_bot, sib_bot, xyc, 1, "rs_sum_w_in_bot")
    dh, (chips_bot,) = _mm(t["dproj"], wf["w_in"], mode="nt", out_dtype=f32, name="d_h_b", tm=tm_t, m_tiles=(halfm, halfm),
                           into=dh, comm=_chips_comm([sum_bot]))
    dx, dng = _rms_bwd(t["x2"], t["ng"], dh, t["dy"], "rms_x_bwd")
    g["norm_g"] = dng.reshape(D)

    grad, delta, new_m, new_v = {}, {}, {}, {}
    for n, ls, lc in zip(early + ("mem_w_kv",), sib_early + [sib_kv], chips_early + [chips_kv]):
        grad[n], delta[n], new_m[n], new_v[n] = _adamw_shard(g[n], ls, lc, w[n], mo[n], vo[n], me1, BIG_AXIS[n], "adamw_" + n)
    top = _adamw_shard(g_top, sib_top, chips_top, w_in, m_w_in, v_w_in, me1, 1, "adamw_w_in_top")
    grad["w_in"], delta["w_in"], new_m["w_in"], new_v["w_in"] = _adamw_shard(
        g_bot, sib_bot, chips_bot, w_in, m_w_in, v_w_in, me1, 1, "adamw_w_in_bot", row0=D // 2, prev=top)

    n_rep = sum(w[n].size for n in SMALL) // LANES
    conv_rows = g["conv_w"].reshape(3, N_DEV, LANES).transpose(1, 0, 2).reshape(3 * N_DEV, LANES)
    n_rows = -(-(n_rep + 3 * N_DEV + 1) // 8) * 8
    part = _pack_rows([g[n] for n in SMALL] + [conv_rows, jnp.full((1, LANES), sq_err, f32)], n_rows)
    tot = _allreduce_small(part)
    loss = tot[n_rep + 3 * N_DEV, 0] * (0.5 / D)
    n_small = -(-(n_rep + 3) // 8) * 8
    g_small = _pack_rows([tot[:n_rep], lax.dynamic_slice(tot, (n_rep + 3 * me, 0), (3, LANES))], n_small)
    names = SMALL + ("conv_w",)
    d_s, m_s, v_s = _adamw_small(g_small, _pack_rows([w[n] for n in names], n_small), _pack_rows([mo[n] for n in names], n_small),
                                 _pack_rows([vo[n] for n in names], n_small))
    off = 0
    for n in names:
        r = w[n].size // LANES
        grad[n], delta[n] = g_small[off:off + r].reshape(w[n].shape), d_s[off:off + r].reshape(w[n].shape)
        new_m[n], new_v[n] = m_s[off:off + r].reshape(w[n].shape), v_s[off:off + r].reshape(w[n].shape)
        off += r
    return (loss, dx.reshape(B, S, D), *[grad[n] for n in ALL_W], *[delta[n] for n in ALL_W], *[new_m[n] for n in ALL_W],
            *[new_v[n] for n in ALL_W])
```

```python
import functools

import jax
import jax.numpy as jnp
from jax import lax
from jax.experimental import pallas as pl
from jax.experimental.pallas import tpu as pltpu

f32 = jnp.float32
bf16 = jnp.bfloat16

N_DEV = 8
HEAD_DIM = 128
DILATIONS = (1, 4, 16)
N_GROUPS = 3
HEADS = 4
BLK = 128
ATTN_QKV = N_GROUPS * HEADS * HEAD_DIM
ATTN_OUT = HEADS * HEAD_DIM
CONV_W = 1024
MEM_LEN = 256
MEM_HEADS = 4
MEM_HD = 256
MEM_W = MEM_HEADS * MEM_HD
EPS = 1e-6
Q0, K0, V0 = 0, ATTN_QKV, 2 * ATTN_QKV
ZA = 3 * ATTN_QKV
CB, CC, CV, ZC = ZA + ATTN_OUT, ZA + ATTN_OUT + CONV_W, ZA + ATTN_OUT + 2 * CONV_W, ZA + ATTN_OUT + 3 * CONV_W
MQ = ZC + CONV_W
ZM = MQ + MEM_W
GT = ZM + MEM_W

ADAM_LR, ADAM_B1, ADAM_B2, ADAM_EPS, ADAM_WD, ADAM_STEP = 0.001, 0.9, 0.999, 1e-08, 0.01, 10

VMEM_LIMIT = 56 * 1024 * 1024
LANES = 128

NT_DIMS = (((1,), (1,)), ((), ()))
TN_DIMS = (((0,), (0,)), ((), ()))
NN_DIMS = (((1,), (0,)), ((), ()))


def _params(sem=None):
    return pltpu.CompilerParams(dimension_semantics=sem, vmem_limit_bytes=VMEM_LIMIT)


def _pick(n, pref, q=LANES):
    t = (min(pref, n) // q) * q
    while t >= q:
        if n % t == 0:
            return t
        t -= q
    return n


def _dot(a, b, dims):
    return lax.dot_general(a.astype(bf16), b.astype(bf16), dims, preferred_element_type=f32)


def _sigmoid(z):
    return 0.5 * jnp.tanh(0.5 * z) + 0.5


def _rstd(v):
    return lax.rsqrt(jnp.mean(v * v, axis=-1, keepdims=True) + EPS)


class _Deferred:
    def __init__(self, stage, sems, step, last, n):
        assert last >= 1
        self.stage, self.sems, self.step, self.last, self.n = stage, sems, step, last, n
        self.slot = step % 2

    def _drain(self, slot, like):
        for c in range(self.n):
            pltpu.make_async_copy(self.stage.at[slot, c], like(c), self.sems.at[slot, c]).wait()

    def begin(self, like):
        pl.when(self.step >= 2)(lambda: self._drain(self.slot, like))

    def start(self, c, dst):
        pltpu.make_async_copy(self.stage.at[self.slot, c], dst, self.sems.at[self.slot, c]).start()

    def end(self, like):
        @pl.when(self.step == self.last)
        def _():
            self._drain(self.slot, like)
            self._drain(1 - self.slot, like)


def _row_sum(v):
    hi = v.astype(bf16)
    lo = (v - hi.astype(f32)).astype(bf16)
    ones = jnp.ones((v.shape[1], v.shape[1]), bf16)
    return _dot(hi, ones, NN_DIMS) + _dot(lo, ones, NN_DIMS)


def _rstd_head(v):
    return lax.rsqrt(_row_sum(v * v) * (1.0 / v.shape[1]) + EPS)


class _Comm:
    def __init__(self, ins, out_shapes, sem_shape, copies):
        self.ins, self.out_shapes, self.sem_shape, self.copies = list(ins), list(out_shapes), sem_shape, copies

    def scratch(self):
        return [pltpu.SemaphoreType.DMA(self.sem_shape), pltpu.SemaphoreType.DMA(self.sem_shape)]


def _mm(a, b, *, mode, out_dtype, name, tm=1024, tn=1024, tk=4096, m_tiles=None, into=None, comm=None):
    if mode == "nn":
        (M, K), (K2, N) = a.shape, b.shape
    elif mode == "nt":
        (M, K), (N, K2) = a.shape, b.shape
    else:
        (K, M), (K2, N) = a.shape, b.shape
    assert K == K2
    tm, tn, tk = _pick(M, tm), _pick(N, tn), _pick(K, tk)
    nk = K // tk
    m0, mc = m_tiles if m_tiles is not None else (0, M // tm)
    o0 = 0 if into is None else m0
    aliased = into is not None and not isinstance(into, str)
    n_ci = len(comm.ins) if comm else 0
    n_co = len(comm.out_shapes) if comm else 0
    grid = (mc, N // tn, nk)
    dims = {"nn": NN_DIMS, "nt": NT_DIMS, "tn": TN_DIMS}[mode]

    def body(*refs, nk):
        a_ref, b_ref = refs[:2]
        pos = 3 if aliased else 2
        c_ins, o_ref, c_outs = refs[pos:pos + n_ci], refs[pos + n_ci], refs[pos + n_ci + 1:pos + n_ci + 1 + n_co]
        scratch = refs[pos + n_ci + 1 + n_co:]
        ids = [pl.program_id(d) for d in range(3)]
        if comm:
            sems = scratch[-2:]

            @pl.when((ids[0] == 0) & (ids[1] == 0) & (ids[2] == 0))
            def _():
                for cp in comm.copies(c_ins, c_outs, *sems):
                    cp.start()

        if nk == 1:
            o_ref[...] = _dot(a_ref[...], b_ref[...], dims).astype(o_ref.dtype)
        else:
            acc_ref, k = scratch[0], ids[2]

            @pl.when(k == 0)
            def _():
                acc_ref[...] = _dot(a_ref[...], b_ref[...], dims)

            @pl.when((k > 0) & (k < nk - 1))
            def _():
                acc_ref[...] += _dot(a_ref[...], b_ref[...], dims)

            @pl.when(k == nk - 1)
            def _():
                o_ref[...] = (acc_ref[...] + _dot(a_ref[...], b_ref[...], dims)).astype(o_ref.dtype)

        if comm:
            @pl.when((ids[0] == grid[0] - 1) & (ids[1] == grid[1] - 1) & (ids[2] == grid[2] - 1))
            def _():
                for cp in comm.copies(c_ins, c_outs, *sems):
                    cp.wait()

    if mode == "tn":
        a_spec = pl.BlockSpec((tk, tm), lambda i, j, k: (k, m0 + i))
    else:
        a_spec = pl.BlockSpec((tm, tk), lambda i, j, k: (m0 + i, k))
    if mode == "nt":
        b_spec = pl.BlockSpec((tn, tk), lambda i, j, k: (j, k))
    else:
        b_spec = pl.BlockSpec((tk, tn), lambda i, j, k: (k, j))
    hbm = pl.BlockSpec(memory_space=pl.ANY)
    res = pl.pallas_call(
        functools.partial(body, nk=nk),
        grid=grid,
        in_specs=[a_spec, b_spec] + [hbm] * (aliased + n_ci),
        out_specs=[pl.BlockSpec((tm, tn), lambda i, j, k: (o0 + i, j))] + [hbm] * n_co,
        out_shape=[jax.ShapeDtypeStruct((mc * tm if into is None else M, N), out_dtype)] + (comm.out_shapes if comm else []),
        scratch_shapes=([pltpu.VMEM((tm, tn), f32)] if nk > 1 else []) + (comm.scratch() if comm else []),
        input_output_aliases={2: 0} if aliased else {},
        compiler_params=_params(("arbitrary",) * 3 if comm else ("parallel", "parallel", "arbitrary")),
        name=name,
    )(a, b, *([into] if aliased else []), *(comm.ins if comm else []))
    return (res[0], list(res[1:])) if comm else res[0]


def _rms_fwd(x, g, name):
    T, D = x.shape
    tm = _pick(T, 256, 8)

    def body(x_ref, g_ref, h_ref):
        xv = x_ref[...]
        h_ref[...] = (xv * _rstd(xv) * g_ref[...]).astype(bf16)

    return pl.pallas_call(
        body, grid=(T // tm,),
        in_specs=[pl.BlockSpec((tm, D), lambda i: (i, 0)), pl.BlockSpec((1, D), lambda i: (0, 0))],
        out_specs=pl.BlockSpec((tm, D), lambda i: (i, 0)),
        out_shape=jax.ShapeDtypeStruct((T, D), bf16),
        compiler_params=_params(("parallel",)), name=name,
    )(x, g)


def _rms_bwd(x, g, dh, dy, name):
    T, D = x.shape
    tm = _pick(T, 256, 8)
    with_dx = dy is not None

    def body(*refs):
        if with_dx:
            x_ref, g_ref, dh_ref, dy_ref, dx_ref, dg_ref = refs
        else:
            x_ref, g_ref, dh_ref, dg_ref = refs
        xv = x_ref[...]
        r = _rstd(xv)
        xh = xv * r
        dhv = dh_ref[...]
        part = jnp.sum(dhv * xh, axis=0, keepdims=True)

        @pl.when(pl.program_id(0) == 0)
        def _():
            dg_ref[...] = part

        @pl.when(pl.program_id(0) > 0)
        def _():
            dg_ref[...] += part

        if with_dx:
            dxh = dhv * g_ref[...]
            dx_ref[...] = dy_ref[...] + r * (dxh - xh * jnp.mean(dxh * xh, axis=-1, keepdims=True))

    row = pl.BlockSpec((tm, D), lambda i: (i, 0))
    vec = pl.BlockSpec((1, D), lambda i: (0, 0))
    if with_dx:
        return pl.pallas_call(
            body, grid=(T // tm,), in_specs=[row, vec, row, row], out_specs=[row, vec],
            out_shape=[jax.ShapeDtypeStruct((T, D), f32), jax.ShapeDtypeStruct((1, D), f32)],
            compiler_params=_params(("arbitrary",)), name=name,
        )(x, g, dh, dy)
    return pl.pallas_call(
        body, grid=(T // tm,), in_specs=[row, vec, row], out_specs=vec,
        out_shape=jax.ShapeDtypeStruct((1, D), f32),
        compiler_params=_params(("arbitrary",)), name=name,
    )(x, g, dh)


UNIT_UNROLL = 4


def _rows(start, size, stride):
    return pl.ds(start, size) if stride == 1 else pl.ds(start, size, stride=stride)


def _attn_units(S, d, first, prev):
    nb = S // d // BLK

    def do_first(r, c):
        first(_rows(r, BLK, d))
        return c

    lax.fori_loop(0, d, do_first, 0, unroll=min(d, UNIT_UNROLL))
    if nb > 1:
        def do_prev(u, c):
            r = u // (nb - 1)
            b = u % (nb - 1) + 1
            base = r + d * b * BLK
            prev(_rows(base, BLK, d), _rows(base - d * BLK, 2 * BLK, d))
            return c

        lax.fori_loop(0, d * (nb - 1), do_prev, 0, unroll=UNIT_UNROLL)


def _band_bias(nkeys):
    i = lax.broadcasted_iota(jnp.int32, (BLK, nkeys), 0)
    j = lax.broadcasted_iota(jnp.int32, (BLK, nkeys), 1)
    ok = (j <= i) if nkeys == BLK else ((j >= i) & (j <= i + BLK))
    return jnp.where(ok, 0.0, -jnp.inf).astype(f32)


def _normalize_into(src_ref, gain, dst_ref, S):
    for c in range(S // 256):
        rows = pl.ds(c * 256, 256)
        v = src_ref[rows, :]
        dst_ref[rows, :] = v * _rstd_head(v) * gain


def _attn_fwd(proj, gq, gk, B, S, comm=None):
    T, NC = proj.shape
    scale = HEAD_DIM ** -0.5
    n_ci = len(comm.ins) if comm else 0
    n_co = len(comm.out_shapes) if comm else 0

    def body(*refs):
        q_ref, k_ref, v_ref, z_ref, gq_ref, gk_ref = refs[:6]
        c_ins = refs[6:6 + n_ci]
        ag_ref, a_ref, lse_ref = refs[6 + n_ci:9 + n_ci]
        c_outs = refs[9 + n_ci:9 + n_ci + n_co]
        qs, ks, o0, o1, o2, l0, l1, l2, bias1, bias2 = refs[9 + n_ci + n_co:19 + n_ci + n_co]
        sems = refs[19 + n_ci + n_co:]
        g = pl.program_id(2)
        if comm:
            @pl.when((pl.program_id(0) == 0) & (pl.program_id(1) == 0) & (g == 0))
            def _():
                for cp in comm.copies(c_ins, c_outs, *sems):
                    cp.start()

        bias1[...] = _band_bias(BLK)
        bias2[...] = _band_bias(2 * BLK)
        _normalize_into(q_ref, gq_ref[pl.ds(g, 1), :], qs, S)
        _normalize_into(k_ref, gk_ref[pl.ds(g, 1), :], ks, S)
        o_s, l_s = (o0, o1, o2), (l0, l1, l2)

        def run(gi):
            def unit(qr, kr, nkeys):
                s = _dot(qs[qr, :], ks[kr, :], NT_DIMS) * scale + (bias1 if nkeys == BLK else bias2)[...]
                m = jnp.max(s, axis=-1, keepdims=True)
                p = jnp.exp(s - m)
                den = jnp.sum(p, axis=-1, keepdims=True)
                o_s[gi][qr, :] = _dot(p, v_ref[kr, :], NN_DIMS) * (1.0 / den)
                l_s[gi][qr, :] = jnp.broadcast_to(m + jnp.log(den), (BLK, HEAD_DIM))

            _attn_units(S, DILATIONS[gi], lambda qr: unit(qr, qr, BLK), lambda qr, kr: unit(qr, kr, 2 * BLK))

        for gi in range(N_GROUPS):
            pl.when(g == gi)(functools.partial(run, gi))

        @pl.when(g == N_GROUPS - 1)
        def _():
            for c in range(S // 256):
                rows = pl.ds(c * 256, 256)
                la, lb, lc = l0[rows, :], l1[rows, :], l2[rows, :]
                m = jnp.maximum(jnp.maximum(la, lb), lc)
                wa, wb, wc = jnp.exp(la - m), jnp.exp(lb - m), jnp.exp(lc - m)
                tot = wa + wb + wc
                a = (wa / tot) * o0[rows, :] + (wb / tot) * o1[rows, :] + (wc / tot) * o2[rows, :]
                z = z_ref[rows, :]
                a_ref[rows, :] = a
                lse_ref[rows, :] = m + jnp.log(tot)
                ag_ref[rows, :] = (a * (z * _sigmoid(z))).astype(bf16)

        if comm:
            @pl.when((pl.program_id(0) == B - 1) & (pl.program_id(1) == HEADS - 1) & (g == N_GROUPS - 1))
            def _():
                for cp in comm.copies(c_ins, c_outs, *sems):
                    cp.wait()

    def slab(col0):
        return pl.BlockSpec((S, HEAD_DIM), lambda b, h, g: (b, col0 // HEAD_DIM + g * HEADS + h))

    gain = pl.BlockSpec((N_GROUPS, HEAD_DIM), lambda b, h, g: (0, 0))
    out = pl.BlockSpec((S, HEAD_DIM), lambda b, h, g: (b, h))
    hbm = pl.BlockSpec(memory_space=pl.ANY)
    res = pl.pallas_call(
        body, grid=(B, HEADS, N_GROUPS),
        in_specs=[slab(Q0), slab(K0), slab(V0), pl.BlockSpec((S, HEAD_DIM), lambda b, h, g: (b, ZA // HEAD_DIM + h)), gain, gain]
        + [hbm] * n_ci,
        out_specs=[out, out, out] + [hbm] * n_co,
        out_shape=[jax.ShapeDtypeStruct((T, ATTN_OUT), bf16), jax.ShapeDtypeStruct((T, ATTN_OUT), f32),
                   jax.ShapeDtypeStruct((T, ATTN_OUT), f32)] + (comm.out_shapes if comm else []),
        scratch_shapes=[pltpu.VMEM((S, HEAD_DIM), f32)] * 8 + [pltpu.VMEM((BLK, BLK), f32), pltpu.VMEM((BLK, 2 * BLK), f32)]
        + (comm.scratch() if comm else []),
        compiler_params=_params(("arbitrary", "arbitrary", "arbitrary")), name="attn_fwd",
    )(proj, proj, proj, proj, gq, gk, *(comm.ins if comm else []))
    return res[0], res[1], res[2], list(res[3:])


def _rms_bwd_rows(raw, gain, dn, on_mxu=True):
    r = _rstd_head(raw) if on_mxu else _rstd(raw)
    xh = raw * r
    dxh = dn * gain
    if on_mxu:
        mean = _row_sum(dxh * xh) * (1.0 / raw.shape[1])
    else:
        mean = jnp.mean(dxh * xh, axis=-1, keepdims=True)
    return r * (dxh - xh * mean), jnp.sum(dn * xh, axis=0, keepdims=True)


def _attn_bwd(proj, gq, gk, a_comb, lse, dag, dproj, B, S):
    T, NC = proj.shape
    scale = HEAD_DIM ** -0.5

    def body(q_ref, k_ref, v_ref, z_ref, gq_ref, gk_ref, a_ref, lse_ref, dag_ref, dproj_in, dproj_ref, dgq_ref, dgk_ref,
             qs, ks, da_s, dl_s, dq_s, dk_s, dv_s, bias1, bias2, stage, dz_stage, sem, dz_sem):
        b, h, g = pl.program_id(0), pl.program_id(1), pl.program_id(2)
        bias1[...] = _band_bias(BLK)
        bias2[...] = _band_bias(2 * BLK)
        gq_row, gk_row = gq_ref[pl.ds(g, 1), :], gk_ref[pl.ds(g, 1), :]
        _normalize_into(q_ref, gq_row, qs, S)
        _normalize_into(k_ref, gk_row, ks, S)

        def dst(c):
            return dproj_ref.at[pl.ds(b * S, S), pl.ds((Q0, K0, V0)[c] + (g * HEADS + h) * HEAD_DIM, HEAD_DIM)]

        out = _Deferred(stage, sem, (b * HEADS + h) * N_GROUPS + g, B * HEADS * N_GROUPS - 1, 3)
        out.begin(dst)

        @pl.when((b == 0) & (h == 0) & (g == 0))
        def _():
            dgq_ref[...] = jnp.zeros_like(dgq_ref)
            dgk_ref[...] = jnp.zeros_like(dgk_ref)

        @pl.when(g == 0)
        def _():
            for c in range(S // 256):
                rows = pl.ds(c * 256, 256)
                z, a, dg_ = z_ref[rows, :], a_ref[rows, :], dag_ref[rows, :]
                sg = _sigmoid(z)
                da = dg_ * (z * sg)
                da_s[rows, :] = da
                dl_s[rows, :] = _row_sum(da * a)
                dz_stage[rows, :] = (dg_ * a * (sg * (1.0 + z * (1.0 - sg)))).astype(bf16)
            cp = pltpu.make_async_copy(dz_stage, dproj_ref.at[pl.ds(b * S, S), pl.ds(ZA + h * HEAD_DIM, HEAD_DIM)], dz_sem)
            cp.start()
            cp.wait()

        dk_s[...] = jnp.zeros_like(dk_s)
        dv_s[...] = jnp.zeros_like(dv_s)

        def run(gi):
            def unit(qr, kr, nkeys):
                q, k, v = qs[qr, :], ks[kr, :], v_ref[kr, :]
                s = _dot(q, k, NT_DIMS) * scale
                p = jnp.exp(s + (bias1 if nkeys == BLK else bias2)[...] - lse_ref[qr, :][:, :1])
                da = da_s[qr, :]
                dp = _dot(da, v, NT_DIMS)
                ds = p * (dp - dl_s[qr, :][:, :1]) * scale
                dq_s[qr, :] = _dot(ds, k, NN_DIMS)
                dk_s[kr, :] += _dot(ds, q, TN_DIMS)
                dv_s[kr, :] += _dot(p, da, TN_DIMS)

            _attn_units(S, DILATIONS[gi], lambda qr: unit(qr, qr, BLK), lambda qr, kr: unit(qr, kr, 2 * BLK))

        for gi in range(N_GROUPS):
            pl.when(g == gi)(functools.partial(run, gi))

        gq_acc = jnp.zeros((1, HEAD_DIM), f32)
        gk_acc = jnp.zeros((1, HEAD_DIM), f32)
        for c in range(S // 256):
            rows = pl.ds(c * 256, 256)
            dq, gq_p = _rms_bwd_rows(q_ref[rows, :], gq_row, dq_s[rows, :])
            dk, gk_p = _rms_bwd_rows(k_ref[rows, :], gk_row, dk_s[rows, :])
            gq_acc, gk_acc = gq_acc + gq_p, gk_acc + gk_p
            stage[out.slot, 0, rows, :] = dq.astype(bf16)
            stage[out.slot, 1, rows, :] = dk.astype(bf16)
            stage[out.slot, 2, rows, :] = dv_s[rows, :].astype(bf16)
        dgq_ref[pl.ds(g, 1), :] += gq_acc
        dgk_ref[pl.ds(g, 1), :] += gk_acc
        for c in range(3):
            out.start(c, dst(c))
        out.end(dst)

    def slab(col0):
        return pl.BlockSpec((S, HEAD_DIM), lambda b, h, g: (b, col0 // HEAD_DIM + g * HEADS + h))

    gain = pl.BlockSpec((N_GROUPS, HEAD_DIM), lambda b, h, g: (0, 0))
    per_slot = pl.BlockSpec((S, HEAD_DIM), lambda b, h, g: (b, h))
    return pl.pallas_call(
        body, grid=(B, HEADS, N_GROUPS),
        in_specs=[slab(Q0), slab(K0), slab(V0), pl.BlockSpec((S, HEAD_DIM), lambda b, h, g: (b, ZA // HEAD_DIM + h)), gain, gain,
                  per_slot, per_slot, per_slot, pl.BlockSpec(memory_space=pl.ANY)],
        out_specs=[pl.BlockSpec(memory_space=pl.ANY), gain, gain],
        out_shape=[jax.ShapeDtypeStruct(dproj.shape, dproj.dtype), jax.ShapeDtypeStruct((N_GROUPS, HEAD_DIM), f32),
                   jax.ShapeDtypeStruct((N_GROUPS, HEAD_DIM), f32)],
        scratch_shapes=[pltpu.VMEM((S, HEAD_DIM), f32)] * 7 + [pltpu.VMEM((BLK, BLK), f32), pltpu.VMEM((BLK, 2 * BLK), f32),
                                                                pltpu.VMEM((2, 3, S, HEAD_DIM), bf16), pltpu.VMEM((S, HEAD_DIM), bf16),
                                                                pltpu.SemaphoreType.DMA((2, 3)), pltpu.SemaphoreType.DMA],
        input_output_aliases={9: 0},
        compiler_params=_params(("arbitrary", "arbitrary", "arbitrary")), name="attn_bwd",
    )(proj, proj, proj, proj, gq, gk, a_comb, lse, dag, dproj)


def _conv_fwd(proj, conv_w, B, S):
    T, NC = proj.shape
    tc = LANES

    def body(cb_ref, cc_ref, cv_ref, z_ref, w_ref, c_ref, ub):
        u = cc_ref[...] * cv_ref[...]
        ub[0:8, :] = jnp.zeros((8, tc), f32)
        ub[8:8 + S, :] = u
        w = w_ref[...]
        y = w[0:1] * u + w[1:2] * ub[pl.ds(7, S), :] + w[2:3] * ub[pl.ds(6, S), :]
        z = z_ref[...]
        c_ref[...] = (cb_ref[...] * y * (z * _sigmoid(z))).astype(bf16)

    def seg(col0):
        return pl.BlockSpec((S, tc), lambda j, b: (b, col0 // tc + j))

    return pl.pallas_call(
        body, grid=(CONV_W // tc, B),
        in_specs=[seg(CB), seg(CC), seg(CV), seg(ZC), pl.BlockSpec((3, tc), lambda j, b: (0, j))],
        out_specs=pl.BlockSpec((S, tc), lambda j, b: (b, j)),
        out_shape=jax.ShapeDtypeStruct((T, CONV_W), bf16),
        scratch_shapes=[pltpu.VMEM((S + 8, tc), f32)],
        compiler_params=_params(("parallel", "parallel")), name="conv_fwd",
    )(proj, proj, proj, proj, conv_w)


def _conv_bwd(proj, conv_w, dc, dproj, B, S):
    T, NC = proj.shape
    tc = LANES

    def body(cb_ref, cc_ref, cv_ref, z_ref, w_ref, dc_ref, dproj_in, dproj_ref, dw_ref, ub, db, stage, sem):
        j, b = pl.program_id(0), pl.program_id(1)

        def dst(c):
            return dproj_ref.at[pl.ds(b * S, S), pl.ds((CB, CC, CV, ZC)[c] + j * tc, tc)]

        out = _Deferred(stage, sem, j * B + b, (CONV_W // tc) * B - 1, 4)
        out.begin(dst)
        cb, cc, cv, z, dcv = cb_ref[...], cc_ref[...], cv_ref[...], z_ref[...], dc_ref[...]
        u = cc * cv
        ub[0:8, :] = jnp.zeros((8, tc), f32)
        ub[8:8 + S, :] = u
        u1, u2 = ub[pl.ds(7, S), :], ub[pl.ds(6, S), :]
        w = w_ref[...]
        y = w[0:1] * u + w[1:2] * u1 + w[2:3] * u2
        sg = _sigmoid(z)
        si = z * sg
        dyv = dcv * cb * si
        db[0:S, :] = dyv
        db[S:S + 8, :] = jnp.zeros((8, tc), f32)
        du = w[0:1] * dyv + w[1:2] * db[pl.ds(1, S), :] + w[2:3] * db[pl.ds(2, S), :]
        stage[out.slot, 0] = (dcv * y * si).astype(bf16)
        stage[out.slot, 1] = (du * cv).astype(bf16)
        stage[out.slot, 2] = (du * cc).astype(bf16)
        stage[out.slot, 3] = (dcv * cb * y * (sg * (1.0 + z * (1.0 - sg)))).astype(bf16)
        for c in range(4):
            out.start(c, dst(c))
        part = jnp.concatenate([jnp.sum(dyv * u, axis=0, keepdims=True), jnp.sum(dyv * u1, axis=0, keepdims=True),
                                jnp.sum(dyv * u2, axis=0, keepdims=True)], axis=0)

        @pl.when(b == 0)
        def _():
            dw_ref[...] = part

        @pl.when(b > 0)
        def _():
            dw_ref[...] += part

        out.end(dst)

    def seg(col0):
        return pl.BlockSpec((S, tc), lambda j, b: (b, col0 // tc + j))

    return pl.pallas_call(
        body, grid=(CONV_W // tc, B),
        in_specs=[seg(CB), seg(CC), seg(CV), seg(ZC), pl.BlockSpec((3, tc), lambda j, b: (0, j)),
                  pl.BlockSpec((S, tc), lambda j, b: (b, j)), pl.BlockSpec(memory_space=pl.ANY)],
        out_specs=[pl.BlockSpec(memory_space=pl.ANY), pl.BlockSpec((3, tc), lambda j, b: (0, j))],
        out_shape=[jax.ShapeDtypeStruct(dproj.shape, dproj.dtype), jax.ShapeDtypeStruct((3, CONV_W), f32)],
        scratch_shapes=[pltpu.VMEM((S + 8, tc), f32), pltpu.VMEM((S + 8, tc), f32), pltpu.VMEM((2, 4, S, tc), bf16),
                        pltpu.SemaphoreType.DMA((2, 4))],
        input_output_aliases={6: 0},
        compiler_params=_params(("arbitrary", "arbitrary")), name="conv_bwd",
    )(proj, proj, proj, proj, conv_w, dc, dproj)


MEM_CHUNK = 256


def _mem_fwd(proj, mkv, gq, gk, B, S):
    T, NC = proj.shape
    scale = MEM_HD ** -0.5

    def body(q_ref, z_ref, k_ref, v_ref, gq_ref, gk_ref, o_ref):
        kv = k_ref[...]
        kn = (kv * _rstd_head(kv) * gk_ref[...]).astype(bf16)
        vv = v_ref[...].astype(bf16)

        def chunk(c, carry):
            rows = pl.ds(pl.multiple_of(c * MEM_CHUNK, MEM_CHUNK), MEM_CHUNK)
            q = q_ref[rows, :]
            qn = q * _rstd(q) * gq_ref[...]
            s = _dot(qn, kn, NT_DIMS) * scale
            p = jnp.exp(s - jnp.max(s, axis=-1, keepdims=True))
            p = p / jnp.sum(p, axis=-1, keepdims=True)
            z = z_ref[rows, :]
            o_ref[rows, :] = (_dot(p, vv, NN_DIMS) * (z * _sigmoid(z))).astype(bf16)
            return carry

        lax.fori_loop(0, S // MEM_CHUNK, chunk, 0)

    gain = pl.BlockSpec((1, MEM_HD), lambda b, h: (0, 0))
    return pl.pallas_call(
        body, grid=(B, MEM_HEADS),
        in_specs=[pl.BlockSpec((S, MEM_HD), lambda b, h: (b, MQ // MEM_HD + h)),
                  pl.BlockSpec((S, MEM_HD), lambda b, h: (b, ZM // MEM_HD + h)),
                  pl.BlockSpec((MEM_LEN, MEM_HD), lambda b, h: (b, h)),
                  pl.BlockSpec((MEM_LEN, MEM_HD), lambda b, h: (b, MEM_HEADS + h)), gain, gain],
        out_specs=pl.BlockSpec((S, MEM_HD), lambda b, h: (b, h)),
        out_shape=jax.ShapeDtypeStruct((T, MEM_W), bf16),
        compiler_params=_params(("parallel", "parallel")), name="mem_fwd",
    )(proj, proj, mkv, mkv, gq, gk)


def _mem_bwd(proj, mkv, gq, gk, dmo, dproj, B, S):
    T, NC = proj.shape
    scale = MEM_HD ** -0.5

    def body(q_ref, z_ref, k_ref, v_ref, gq_ref, gk_ref, dmo_ref, dproj_in, dproj_ref, dmk_ref, dmv_ref, dgq_ref, dgk_ref,
             dkn_s, dv_s, gq_s, stage, sem):
        b, h = pl.program_id(0), pl.program_id(1)

        def dst(c):
            return dproj_ref.at[pl.ds(b * S, S), pl.ds((MQ, ZM)[c] + h * MEM_HD, MEM_HD)]

        out = _Deferred(stage, sem, b * MEM_HEADS + h, B * MEM_HEADS - 1, 2)
        out.begin(dst)
        kv = k_ref[...]
        kn = (kv * _rstd_head(kv) * gk_ref[...]).astype(bf16)
        vv = v_ref[...].astype(bf16)
        dkn_s[...] = jnp.zeros_like(dkn_s)
        dv_s[...] = jnp.zeros_like(dv_s)
        gq_s[...] = jnp.zeros_like(gq_s)

        def chunk(c, carry):
            rows = pl.ds(pl.multiple_of(c * MEM_CHUNK, MEM_CHUNK), MEM_CHUNK)
            q = q_ref[rows, :]
            qn = q * _rstd(q) * gq_ref[...]
            s = _dot(qn, kn, NT_DIMS) * scale
            p = jnp.exp(s - jnp.max(s, axis=-1, keepdims=True))
            p = p / jnp.sum(p, axis=-1, keepdims=True)
            mo = _dot(p, vv, NN_DIMS)
            z, dg_ = z_ref[rows, :], dmo_ref[rows, :]
            sg = _sigmoid(z)
            do = dg_ * (z * sg)
            stage[out.slot, 1, rows, :] = (dg_ * mo * (sg * (1.0 + z * (1.0 - sg)))).astype(bf16)
            dp = _dot(do, vv, NT_DIMS)
            ds = p * (dp - jnp.sum(do * mo, axis=-1, keepdims=True)) * scale
            dq, gq_p = _rms_bwd_rows(q, gq_ref[...], _dot(ds, kn, NN_DIMS), on_mxu=False)
            stage[out.slot, 0, rows, :] = dq.astype(bf16)
            gq_s[...] += gq_p
            dkn_s[...] += _dot(ds, qn, TN_DIMS)
            dv_s[...] += _dot(p, do, TN_DIMS)
            return carry

        lax.fori_loop(0, S // MEM_CHUNK, chunk, 0)
        for c in range(2):
            out.start(c, dst(c))
        dk, gk_p = _rms_bwd_rows(kv, gk_ref[...], dkn_s[...])
        dmk_ref[...] = dk
        dmv_ref[...] = dv_s[...]

        @pl.when((b == 0) & (h == 0))
        def _():
            dgq_ref[...] = gq_s[...]
            dgk_ref[...] = gk_p

        @pl.when((b > 0) | (h > 0))
        def _():
            dgq_ref[...] += gq_s[...]
            dgk_ref[...] += gk_p

        out.end(dst)

    gain = pl.BlockSpec((1, MEM_HD), lambda b, h: (0, 0))
    kvb = pl.BlockSpec((MEM_LEN, MEM_HD), lambda b, h: (b, h))
    return pl.pallas_call(
        body, grid=(B, MEM_HEADS),
        in_specs=[pl.BlockSpec((S, MEM_HD), lambda b, h: (b, MQ // MEM_HD + h)),
                  pl.BlockSpec((S, MEM_HD), lambda b, h: (b, ZM // MEM_HD + h)),
                  kvb, pl.BlockSpec((MEM_LEN, MEM_HD), lambda b, h: (b, MEM_HEADS + h)), gain, gain,
                  pl.BlockSpec((S, MEM_HD), lambda b, h: (b, h)), pl.BlockSpec(memory_space=pl.ANY)],
        out_specs=[pl.BlockSpec(memory_space=pl.ANY), kvb, kvb, gain, gain],
        out_shape=[jax.ShapeDtypeStruct(dproj.shape, dproj.dtype), jax.ShapeDtypeStruct((B * MEM_LEN, MEM_W), f32),
                   jax.ShapeDtypeStruct((B * MEM_LEN, MEM_W), f32), jax.ShapeDtypeStruct((1, MEM_HD), f32),
                   jax.ShapeDtypeStruct((1, MEM_HD), f32)],
        scratch_shapes=[pltpu.VMEM((MEM_LEN, MEM_HD), f32), pltpu.VMEM((MEM_LEN, MEM_HD), f32), pltpu.VMEM((1, MEM_HD), f32),
                        pltpu.VMEM((2, 2, S, MEM_HD), bf16), pltpu.SemaphoreType.DMA((2, 2))],
        input_output_aliases={7: 0},
        compiler_params=_params(("arbitrary", "arbitrary")), name="mem_bwd",
    )(proj, proj, mkv, mkv, gq, gk, dmo, dproj)


def _merge_fwd(proj, ag, cg, mg, wa, wc, wm):
    T, NC = proj.shape
    D = wa.shape[1]
    tm, tn = _pick(T, 1024), _pick(D, 512)
    assert GT % tn == 0

    def body(ag_ref, cg_ref, mg_ref, wa_ref, wc_ref, wm_ref, g0_ref, g1_ref, g2_ref, mer_ref, ba_ref, bc_ref, bm_ref):
        ba = _dot(ag_ref[...], wa_ref[...], NN_DIMS)
        bc = _dot(cg_ref[...], wc_ref[...], NN_DIMS)
        bm = _dot(mg_ref[...], wm_ref[...], NN_DIMS)
        mer_ref[...] = (_sigmoid(g0_ref[...]) * ba + _sigmoid(g1_ref[...]) * bc + _sigmoid(g2_ref[...]) * bm).astype(bf16)
        ba_ref[...] = ba.astype(bf16)
        bc_ref[...] = bc.astype(bf16)
        bm_ref[...] = bm.astype(bf16)

    def act(w):
        return pl.BlockSpec((tm, w), lambda i, j: (i, 0))

    def wt(k):
        return pl.BlockSpec((k, tn), lambda i, j: (0, j))

    def gate(n):
        return pl.BlockSpec((tm, tn), lambda i, j: (i, (GT + n * D) // tn + j))

    out = pl.BlockSpec((tm, tn), lambda i, j: (i, j))
    return pl.pallas_call(
        body, grid=(T // tm, D // tn),
        in_specs=[act(ATTN_OUT), act(CONV_W), act(MEM_W), wt(ATTN_OUT), wt(CONV_W), wt(MEM_W), gate(0), gate(1), gate(2)],
        out_specs=[out] * 4, out_shape=[jax.ShapeDtypeStruct((T, D), bf16)] * 4,
        compiler_params=_params(("parallel", "parallel")), name="merge_fwd",
    )(ag, cg, mg, wa, wc, wm, proj, proj, proj)


def _out_loss(merged, w_out, x, tgt):
    T, D = x.shape
    tm, tn = _pick(T, 512), _pick(D, 512)

    def body(m_ref, w_ref, x_ref, t_ref, dy_ref, dyb_ref, lp_ref):
        e = x_ref[...] + _dot(m_ref[...], w_ref[...], NN_DIMS) - t_ref[...]
        dy = e / D
        dy_ref[...] = dy
        dyb_ref[...] = dy.astype(bf16)
        part = jnp.full((1, 8, LANES), jnp.sum(e * e), f32)

        @pl.when(pl.program_id(1) == 0)
        def _():
            lp_ref[...] = part

        @pl.when(pl.program_id(1) > 0)
        def _():
            lp_ref[...] += part

    tile = pl.BlockSpec((tm, tn), lambda i, j: (i, j))
    return pl.pallas_call(
        body, grid=(T // tm, D // tn),
        in_specs=[pl.BlockSpec((tm, D), lambda i, j: (i, 0)), pl.BlockSpec((D, tn), lambda i, j: (0, j)), tile, tile],
        out_specs=[tile, tile, pl.BlockSpec((1, 8, LANES), lambda i, j: (i, 0, 0))],
        out_shape=[jax.ShapeDtypeStruct((T, D), f32), jax.ShapeDtypeStruct((T, D), bf16),
                   jax.ShapeDtypeStruct((T // tm, 8, LANES), f32)],
        compiler_params=_params(("parallel", "arbitrary")), name="out_loss",
    )(merged, w_out, x, tgt)


def _merge_bwd(proj, dyb, w_out, ba, bc, bm):
    T, NC = proj.shape
    D = w_out.shape[0]
    tm, tn = _pick(T, 512), _pick(D, 512)
    assert GT % tn == 0

    def body(dy_ref, w_ref, ba_ref, bc_ref, bm_ref, g0_ref, g1_ref, g2_ref, da_ref, dc_ref, dm_ref, dproj_ref, stage, sem):
        i, j = pl.program_id(0), pl.program_id(1)

        def dst(n):
            return dproj_ref.at[pl.ds(i * tm, tm), pl.ds(GT + n * D + j * tn, tn)]

        out = _Deferred(stage, sem, i * (D // tn) + j, (T // tm) * (D // tn) - 1, 3)
        out.begin(dst)
        dmer = _dot(dy_ref[...], w_ref[...], NT_DIMS)
        for n, (g_ref, br_ref, o_ref) in enumerate(((g0_ref, ba_ref, da_ref), (g1_ref, bc_ref, dc_ref), (g2_ref, bm_ref, dm_ref))):
            sg = _sigmoid(g_ref[...])
            o_ref[...] = (sg * dmer).astype(bf16)
            stage[out.slot, n] = (dmer * br_ref[...].astype(f32) * (sg * (1.0 - sg))).astype(bf16)
            out.start(n, dst(n))
        out.end(dst)

    tile = pl.BlockSpec((tm, tn), lambda i, j: (i, j))

    def gate(n):
        return pl.BlockSpec((tm, tn), lambda i, j: (i, (GT + n * D) // tn + j))

    return pl.pallas_call(
        body, grid=(T // tm, D // tn),
        in_specs=[pl.BlockSpec((tm, D), lambda i, j: (i, 0)), pl.BlockSpec((tn, D), lambda i, j: (j, 0)), tile, tile, tile,
                  gate(0), gate(1), gate(2)],
        out_specs=[tile, tile, tile, pl.BlockSpec(memory_space=pl.ANY)],
        out_shape=[jax.ShapeDtypeStruct((T, D), bf16)] * 3 + [jax.ShapeDtypeStruct((T, NC), bf16)],
        scratch_shapes=[pltpu.VMEM((2, 3, tm, tn), bf16), pltpu.SemaphoreType.DMA((2, 3))],
        compiler_params=_params(("arbitrary", "arbitrary")), name="merge_bwd",
    )(dyb, w_out, ba, bc, bm, proj, proj, proj)


def _fwd_bwd_head(x2, mem2, t2, proj, attn, B, S, mem_norm_g, attn_q_norm, attn_k_norm, conv_w, mem_q_norm, mem_k_norm,
                  w_kv, w_a, w_c, w_m, w_out):
    D = x2.shape[1]
    mng = mem_norm_g.reshape(1, D)
    mqn, mkn = mem_q_norm.reshape(1, MEM_HD), mem_k_norm.reshape(1, MEM_HD)
    ag, a_comb, lse = attn

    mh = _rms_fwd(mem2, mng, "rms_mem")
    mkv = _mm(mh, w_kv, mode="nn", out_dtype=f32, name="mkv")
    cg = _conv_fwd(proj, conv_w, B, S)
    mg = _mem_fwd(proj, mkv, mqn, mkn, B, S)
    merged, ba, bc, bm = _merge_fwd(proj, ag, cg, mg, w_a, w_c, w_m)
    dy, dyb, lp = _out_loss(merged, w_out, x2, t2)
    sq_err = jnp.sum(lp[:, 0, 0])

    g = {}
    g["w_out"] = _mm(merged, dyb, mode="tn", out_dtype=f32, name="dw_out")
    dba, dbc, dbm, dproj = _merge_bwd(proj, dyb, w_out, ba, bc, bm)
    g["w_br_attn"] = _mm(ag, dba, mode="tn", out_dtype=f32, name="dw_br_attn")
    g["w_br_conv"] = _mm(cg, dbc, mode="tn", out_dtype=f32, name="dw_br_conv")
    g["w_br_mem"] = _mm(mg, dbm, mode="tn", out_dtype=f32, name="dw_br_mem")
    dag = _mm(dba, w_a, mode="nt", out_dtype=f32, name="d_attn_out")
    dcg = _mm(dbc, w_c, mode="nt", out_dtype=f32, name="d_conv_out")
    dmg = _mm(dbm, w_m, mode="nt", out_dtype=f32, name="d_mem_out")
    dproj, g["attn_q_norm"], g["attn_k_norm"] = _attn_bwd(proj, attn_q_norm, attn_k_norm, a_comb, lse, dag, dproj, B, S)
    dproj, g["conv_w"] = _conv_bwd(proj, conv_w, dcg, dproj, B, S)
    dproj, dmk, dmv, dgmq, dgmk = _mem_bwd(proj, mkv, mqn, mkn, dmg, dproj, B, S)
    g["mem_q_norm"], g["mem_k_norm"] = dgmq.reshape(MEM_HD), dgmk.reshape(MEM_HD)
    dmkv = jnp.concatenate([dmk, dmv], axis=1)
    dmh = _mm(dmkv, w_kv, mode="nt", out_dtype=f32, name="d_mem_h")
    g["mem_norm_g"] = _rms_bwd(mem2, mng, dmh, None, "rms_mem_bwd").reshape(D)
    return sq_err, g, dict(dy=dy, dproj=dproj, mh=mh, dmkv=dmkv)


MESH = pl.DeviceIdType.MESH
HBM = pl.BlockSpec(memory_space=pl.ANY)


def _me():
    x, y, c = lax.axis_index("x"), lax.axis_index("y"), lax.axis_index("c")
    return (x, y, c), 4 * x + 2 * y + c


def _peer(k):
    (x, y, c), _ = _me()
    p = (1 - x if k & 4 else x, 1 - y if k & 2 else y, 1 - c if k & 1 else c)
    return p, 4 * p[0] + 2 * p[1] + p[2]


def _window(ref, axis, size, idx):
    if axis is None:
        return ref.at[idx]
    if axis == 0:
        return ref.at[pl.ds(idx * size, size), :]
    return ref.at[:, pl.ds(idx * size, size)]


def _full_shape(s, axis):
    if axis is None:
        return (N_DEV,) + s.shape
    return tuple(N_DEV * d if i == axis else d for i, d in enumerate(s.shape))


OTHER_CHIPS = (4, 2, 6)


def _spread_comm(shards, axes):
    n = len(shards)

    def copies(ins, outs, send_sems, recv_sems, base=0):
        _, me = _me()
        out = []
        for a in range(n):
            mine = _window(outs[a], axes[a], None if axes[a] is None else shards[a].shape[axes[a]], me)
            out.append(pltpu.make_async_copy(ins[a], mine, send_sems.at[base + a, N_CHIP]))
            for k, dist in enumerate((1,) + OTHER_CHIPS):
                dev, _ = _peer(dist)
                out.append(pltpu.make_async_remote_copy(
                    src_ref=ins[a], dst_ref=mine, send_sem=send_sems.at[base + a, k], recv_sem=recv_sems.at[base + a, k],
                    device_id=dev, device_id_type=MESH))
        return out

    shapes = [jax.ShapeDtypeStruct(_full_shape(s, ax), s.dtype) for s, ax in zip(shards, axes)]
    return _Comm(shards, shapes, (n, N_CHIP + 1), copies)


def _forward_blocks(fulls, axes):
    n = len(fulls)
    sizes = [None if ax is None else f.shape[ax] // N_DEV for f, ax in zip(fulls, axes)]

    def body(*refs):
        outs = refs[n:2 * n]
        send_sems, recv_sems = refs[2 * n:]
        sibling, _ = _peer(1)
        copies = []
        for j, dist in enumerate(OTHER_CHIPS):
            _, held = _peer(dist)
            for a in range(n):
                block = _window(outs[a], axes[a], sizes[a], held)
                copies.append(pltpu.make_async_remote_copy(
                    src_ref=block, dst_ref=block, send_sem=send_sems.at[a, j], recv_sem=recv_sems.at[a, j],
                    device_id=sibling, device_id_type=MESH))
        for cp in copies:
            cp.start()
        for cp in copies:
            cp.wait()

    return pl.pallas_call(
        body, in_specs=[HBM] * n, out_specs=[HBM] * n,
        out_shape=[jax.ShapeDtypeStruct(f.shape, f.dtype) for f in fulls],
        scratch_shapes=[pltpu.SemaphoreType.DMA((n, len(OTHER_CHIPS))), pltpu.SemaphoreType.DMA((n, len(OTHER_CHIPS)))],
        input_output_aliases={a: a for a in range(n)},
        name="forward_blocks",
    )(*fulls)


def _shard_order():
    (x, y, c), me = _me()
    ids = [me, _peer(1)[1]]
    for dist in OTHER_CHIPS:
        ids += [_peer(dist)[1], _peer(dist + 1)[1]]
    return jnp.stack(ids).astype(jnp.int32)


def _proj_gather(h, w_shard, order, comm):
    T, K = h.shape
    C = w_shard.shape[1]
    tm = _pick(T, 1024)
    nm = T // tm
    ahead = max(nm - 2, 0)
    n_ci, n_co = len(comm.ins), len(comm.out_shapes)
    comm_at = N_DEV // 2 - 1

    def body(*refs):
        order_ref, h_ref, ws_ref = refs[:3]
        c_ins = refs[3:3 + n_ci]
        o_ref, wf_ref = refs[3 + n_ci:5 + n_ci]
        c_outs = refs[5 + n_ci:5 + n_ci + n_co]
        wbuf, send_sems, recv_sems, own_sem, buf_sems, c_send, c_recv = refs[5 + n_ci + n_co:]
        t, i = pl.program_id(0), pl.program_id(1)

        @pl.when((t == comm_at) & (i == 0))
        def _():
            for cp in comm.copies(c_ins, c_outs, c_send, c_recv):
                cp.start()
        (x, y, c), me = _me()
        sibling, sib_id = _peer(1)
        chips = [_peer(dist) for dist in OTHER_CHIPS]

        def win(idx):
            return wf_ref.at[:, pl.ds(idx * C, C)]

        def copy(k, block, to, src=None):
            return pltpu.make_async_remote_copy(
                src_ref=win(block) if src is None else src, dst_ref=win(block),
                send_sem=send_sems.at[k], recv_sem=recv_sems.at[k], device_id=to, device_id_type=MESH)

        def fetch(tt, src):
            return pltpu.make_async_copy(src, wbuf.at[tt % 2], buf_sems.at[tt % 2])

        own = pltpu.make_async_copy(ws_ref, win(me), own_sem)
        (x_nbr, x_id), (y_nbr, y_id), (_, d_id) = chips

        def half(k, block, top, to):
            rows = wf_ref.at[pl.ds(0 if top else K // 2, K // 2), pl.ds(block * C, C)]
            return pltpu.make_async_remote_copy(src_ref=rows, dst_ref=rows, send_sem=send_sems.at[k], recv_sem=recv_sems.at[k],
                                                device_id=to, device_id_type=MESH)

        @pl.when((t == 0) & (i == 0))
        def _():
            fetch(0, ws_ref).start()
            own.start()
            copy(0, me, sibling, src=ws_ref).start()
            copy(1, me, x_nbr, src=ws_ref).start()
            copy(2, me, y_nbr, src=ws_ref).start()

        for tt in range(N_DEV):
            @pl.when((t == tt) & (i == 0))
            def _(tt=tt):
                fetch(tt, ws_ref).wait()

        o_ref[...] = _dot(h_ref[...], wbuf[t % 2], NN_DIMS)

        for tt in range(N_DEV - 1):
            @pl.when((t == tt) & (i == ahead))
            def _(tt=tt):
                nxt = tt + 1
                j = (nxt - 2) // 2
                if nxt == 1:
                    copy(0, sib_id, (x, y, c)).wait_recv()
                    block = sib_id
                elif nxt == 2:
                    block = x_id
                    copy(1, x_id, (x, y, c)).wait_recv()
                    copy(2, y_id, (x, y, c)).wait_recv()
                    half(7, y_id, True, x_nbr).start()
                    half(8, x_id, False, y_nbr).start()
                    copy(4, x_id, sibling).start()
                    copy(5, y_id, sibling).start()
                elif nxt == 4:
                    block = y_id
                elif nxt == 6:
                    block = d_id
                    half(7, d_id, True, (x, y, c)).wait_recv()
                    half(8, d_id, False, (x, y, c)).wait_recv()
                    copy(6, d_id, sibling).start()
                else:
                    block = chips[j][1] + 1 - 2 * c
                    copy(4 + j, block, (x, y, c)).wait_recv()
                fetch(nxt, win(block)).start()

        @pl.when((t == N_DEV - 1) & (i == nm - 1))
        def _():
            copy(0, me, sibling, src=ws_ref).wait_send()
            copy(1, me, x_nbr, src=ws_ref).wait_send()
            copy(2, me, y_nbr, src=ws_ref).wait_send()
            half(7, y_id, True, x_nbr).wait_send()
            half(8, x_id, False, y_nbr).wait_send()
            for j, (_, dev_id) in enumerate(chips):
                copy(4 + j, dev_id, sibling).wait_send()
            own.wait()
            for cp in comm.copies(c_ins, c_outs, c_send, c_recv):
                cp.wait()

    hbm = pl.BlockSpec(memory_space=pl.ANY)
    res = pl.pallas_call(
        body,
        grid_spec=pltpu.PrefetchScalarGridSpec(
            num_scalar_prefetch=1, grid=(N_DEV, nm),
            in_specs=[pl.BlockSpec((tm, K), lambda t, i, order_ref: (i, 0)), hbm] + [hbm] * n_ci,
            out_specs=[pl.BlockSpec((tm, C), lambda t, i, order_ref: (i, order_ref[t])), hbm] + [hbm] * n_co,
            scratch_shapes=[pltpu.VMEM((2, K, C), bf16), pltpu.SemaphoreType.DMA((9,)), pltpu.SemaphoreType.DMA((9,)),
                            pltpu.SemaphoreType.DMA, pltpu.SemaphoreType.DMA((2,))] + comm.scratch()),
        out_shape=[jax.ShapeDtypeStruct((T, N_DEV * C), f32), jax.ShapeDtypeStruct((K, N_DEV * C), bf16)] + comm.out_shapes,
        compiler_params=_params(("arbitrary", "arbitrary")), name="proj_gather",
    )(order, h, w_shard, *comm.ins)
    return res[0], res[1], list(res[2:])


N_CHIP = 4


def _shard_shape(g, ax):
    return tuple(d // N_DEV if i == ax else d for i, d in enumerate(g.shape))


def _sibling_comm(grads, axes):
    n = len(grads)
    sizes = [g.shape[ax] // N_DEV for g, ax in zip(grads, axes)]

    def copies(ins, lands, send_sems, recv_sems, base=0):
        sibling, _ = _peer(1)
        out = []
        for j in range(N_CHIP):
            _, owner = _peer(2 * j + 1)
            for a in range(n):
                out.append(pltpu.make_async_remote_copy(
                    src_ref=_window(ins[a], axes[a], sizes[a], owner), dst_ref=lands[a].at[j],
                    send_sem=send_sems.at[base + a, j], recv_sem=recv_sems.at[base + a, j], device_id=sibling,
                    device_id_type=MESH))
        return out

    shapes = [jax.ShapeDtypeStruct((N_CHIP,) + _shard_shape(g, ax), g.dtype) for g, ax in zip(grads, axes)]
    return _Comm(grads, shapes, (n, N_CHIP), copies)


def _chips_comm(sums):
    n = len(sums)

    def copies(ins, lands, send_sems, recv_sems, base=0):
        out = []
        for j in range(1, N_CHIP):
            dev, _ = _peer(2 * j)
            for a in range(n):
                out.append(pltpu.make_async_remote_copy(
                    src_ref=ins[a].at[j - 1], dst_ref=lands[a].at[j - 1],
                    send_sem=send_sems.at[base + a, j - 1], recv_sem=recv_sems.at[base + a, j - 1], device_id=dev,
                    device_id_type=MESH))
        return out

    return _Comm(sums, [jax.ShapeDtypeStruct(s.shape, s.dtype) for s in sums], (n, N_CHIP), copies)


def _join(c1, c2):
    n1, m1, k1 = len(c1.ins), len(c1.out_shapes), c1.sem_shape[0]

    def copies(ins, lands, send_sems, recv_sems):
        return (c1.copies(ins[:n1], lands[:m1], send_sems, recv_sems, base=0)
                + c2.copies(ins[n1:], lands[m1:], send_sems, recv_sems, base=k1))

    return _Comm(c1.ins + c2.ins, c1.out_shapes + c2.out_shapes, (k1 + c2.sem_shape[0], N_CHIP), copies)


def _rs_chip_sum(grad, land, xyc, axis, name):
    R, C = land.shape[1:]
    tr = _pick(R, max(8, (640 * 1024) // C), 8)

    def body(xyc_ref, g_ref, l_ref, o_ref):
        o_ref[0] = (g_ref[...] + l_ref[0]).astype(bf16)

    def owner(j, xyc_ref):
        jj = j + 1
        return 4 * (xyc_ref[0] ^ (jj >> 1)) + 2 * (xyc_ref[1] ^ (jj & 1)) + xyc_ref[2]

    if axis == 0:
        g_spec = pl.BlockSpec((tr, C), lambda j, i, xyc_ref: (owner(j, xyc_ref) * (R // tr) + i, 0))
    else:
        g_spec = pl.BlockSpec((tr, C), lambda j, i, xyc_ref: (i, owner(j, xyc_ref)))
    return pl.pallas_call(
        body,
        grid_spec=pltpu.PrefetchScalarGridSpec(
            num_scalar_prefetch=1, grid=(N_CHIP - 1, R // tr),
            in_specs=[g_spec, pl.BlockSpec((1, tr, C), lambda j, i, xyc_ref: (j + 1, i, 0))],
            out_specs=pl.BlockSpec((1, tr, C), lambda j, i, xyc_ref: (j, i, 0))),
        out_shape=jax.ShapeDtypeStruct((N_CHIP - 1, R, C), bf16),
        compiler_params=_params(("parallel", "parallel")), name=name,
    )(xyc, grad, land)


def _allreduce_small(part):
    R = part.shape[0]

    def body(p_ref, o_ref, buf, send_sems, recv_sems):
        (x, y, c), me = _me()
        buf[me] = p_ref[...]
        copies = []
        for k in range(1, N_DEV):
            dev, dev_id = _peer(k)
            copies.append(pltpu.make_async_remote_copy(
                src_ref=p_ref, dst_ref=buf.at[me], send_sem=send_sems.at[k - 1], recv_sem=recv_sems.at[k - 1],
                device_id=dev, device_id_type=MESH))
        for cp in copies:
            cp.start()
        for k in range(1, N_DEV):
            _, dev_id = _peer(k)
            pltpu.make_async_remote_copy(
                src_ref=p_ref, dst_ref=buf.at[dev_id], send_sem=send_sems.at[k - 1], recv_sem=recv_sems.at[k - 1],
                device_id=(x, y, c), device_id_type=MESH).wait_recv()
        for cp in copies:
            cp.wait_send()
        acc = buf[0]
        for d in range(1, N_DEV):
            acc = acc + buf[d]
        o_ref[...] = acc

    return pl.pallas_call(
        body, in_specs=[pl.BlockSpec(memory_space=pltpu.VMEM)], out_specs=pl.BlockSpec(memory_space=pltpu.VMEM),
        out_shape=jax.ShapeDtypeStruct((R, LANES), f32),
        scratch_shapes=[pltpu.VMEM((N_DEV, R, LANES), f32), pltpu.SemaphoreType.DMA((N_DEV - 1,)), pltpu.SemaphoreType.DMA((N_DEV - 1,))],
        name="allreduce_small",
    )(part)


def _adamw_math(w, g, m, v):
    m2 = ADAM_B1 * m + (1.0 - ADAM_B1) * g
    v2 = ADAM_B2 * v + (1.0 - ADAM_B2) * (g * g)
    m_hat = m2 / (1.0 - ADAM_B1 ** ADAM_STEP)
    v_hat = v2 / (1.0 - ADAM_B2 ** ADAM_STEP)
    return -ADAM_LR * (m_hat / (jnp.sqrt(v_hat) + ADAM_EPS) + ADAM_WD * w), m2, v2


def _adamw_shard(grad, land_sib, land_chips, w, m, v, me, axis, name, row0=0, prev=None):
    R, C = land_sib.shape[1:]
    assert axis == 1 or R == w.shape[0]
    tr = _pick(R, max(8, (320 * 1024) // C), 8)
    r0 = row0 // tr
    n_prev = len(prev) if prev else 0

    def body(me_ref, g_ref, s_ref, l_ref, w_ref, m_ref, v_ref, *rest):
        go_ref, d_ref, mo_ref, vo_ref = rest[n_prev:]
        g = g_ref[...] + s_ref[0]
        for j in range(N_CHIP - 1):
            g = g + l_ref[j].astype(f32)
        d, m2, v2 = _adamw_math(w_ref[...], g, m_ref[...], v_ref[...])
        go_ref[...] = g
        d_ref[...] = d
        mo_ref[...] = m2
        vo_ref[...] = v2

    if axis == 0:
        g_spec = pl.BlockSpec((tr, C), lambda i, me_ref: (me_ref[0] * (R // tr) + i, 0))
    else:
        g_spec = pl.BlockSpec((tr, C), lambda i, me_ref: (i, me_ref[0]))
    blk = pl.BlockSpec((tr, C), lambda i, me_ref: (r0 + i, 0))
    return pl.pallas_call(
        body,
        grid_spec=pltpu.PrefetchScalarGridSpec(
            num_scalar_prefetch=1, grid=(R // tr,),
            in_specs=[g_spec, pl.BlockSpec((1, tr, C), lambda i, me_ref: (0, i, 0)),
                      pl.BlockSpec((N_CHIP - 1, tr, C), lambda i, me_ref: (0, i, 0)), blk, blk, blk]
            + [pl.BlockSpec(memory_space=pl.ANY)] * n_prev,
            out_specs=[blk] * 4),
        out_shape=[jax.ShapeDtypeStruct(w.shape, f32)] * 4,
        input_output_aliases={7 + i: i for i in range(n_prev)},
        compiler_params=_params(("parallel",)), name=name,
    )(me, grad, land_sib, land_chips, w, m, v, *(prev or []))


def _adamw_small(g, w, m, v):
    def body(g_ref, w_ref, m_ref, v_ref, d_ref, mo_ref, vo_ref):
        d, m2, v2 = _adamw_math(w_ref[...], g_ref[...], m_ref[...], v_ref[...])
        d_ref[...] = d
        mo_ref[...] = m2
        vo_ref[...] = v2

    return pl.pallas_call(body, out_shape=[jax.ShapeDtypeStruct(g.shape, f32)] * 3, name="adamw_small")(g, w, m, v)


def _pack_rows(parts, total_rows):
    rows = jnp.concatenate([p.reshape(-1, LANES) for p in parts], axis=0)
    return jnp.pad(rows, ((0, total_rows - rows.shape[0]), (0, 0)))


BIG = ("w_in", "mem_w_kv", "w_br_attn", "w_br_conv", "w_br_mem", "w_out")
BIG_AXIS = {"w_in": 1, "mem_w_kv": 0, "w_br_attn": 1, "w_br_conv": 1, "w_br_mem": 1, "w_out": 0}
SMALL = ("norm_g", "mem_norm_g", "attn_q_norm", "attn_k_norm", "mem_q_norm", "mem_k_norm")
ALL_W = ("norm_g", "mem_norm_g", "w_in", "attn_q_norm", "attn_k_norm", "conv_w", "mem_w_kv", "mem_q_norm", "mem_k_norm",
         "w_br_attn", "w_br_conv", "w_br_mem", "w_out")


def kernel(x, mem, norm_g, mem_norm_g, w_in, attn_q_norm, attn_k_norm, conv_w, mem_w_kv, mem_q_norm, mem_k_norm, w_br_attn, w_br_conv, w_br_mem, w_out, loss_target, m_norm_g, m_mem_norm_g, m_w_in, m_attn_q_norm, m_attn_k_norm, m_conv_w, m_mem_w_kv, m_mem_q_norm, m_mem_k_norm, m_w_br_attn, m_w_br_conv, m_w_br_mem, m_w_out, v_norm_g, v_mem_norm_g, v_w_in, v_attn_q_norm, v_attn_k_norm, v_conv_w, v_mem_w_kv, v_mem_q_norm, v_mem_k_norm, v_w_br_attn, v_w_br_conv, v_w_br_mem, v_w_out):
    w = dict(norm_g=norm_g, mem_norm_g=mem_norm_g, w_in=w_in, attn_q_norm=attn_q_norm, attn_k_norm=attn_k_norm, conv_w=conv_w,
             mem_w_kv=mem_w_kv, mem_q_norm=mem_q_norm, mem_k_norm=mem_k_norm, w_br_attn=w_br_attn, w_br_conv=w_br_conv,
             w_br_mem=w_br_mem, w_out=w_out)
    mo = dict(norm_g=m_norm_g, mem_norm_g=m_mem_norm_g, w_in=m_w_in, attn_q_norm=m_attn_q_norm, attn_k_norm=m_attn_k_norm,
              conv_w=m_conv_w, mem_w_kv=m_mem_w_kv, mem_q_norm=m_mem_q_norm, mem_k_norm=m_mem_k_norm, w_br_attn=m_w_br_attn,
              w_br_conv=m_w_br_conv, w_br_mem=m_w_br_mem, w_out=m_w_out)
    vo = dict(norm_g=v_norm_g, mem_norm_g=v_mem_norm_g, w_in=v_w_in, attn_q_norm=v_attn_q_norm, attn_k_norm=v_attn_k_norm,
              conv_w=v_conv_w, mem_w_kv=v_mem_w_kv, mem_q_norm=v_mem_q_norm, mem_k_norm=v_mem_k_norm, w_br_attn=v_w_br_attn,
              w_br_conv=v_w_br_conv, w_br_mem=v_w_br_mem, w_out=v_w_out)
    B, S, D = x.shape
    me = (4 * lax.axis_index("x") + 2 * lax.axis_index("y") + lax.axis_index("c")).astype(jnp.int32)

    T = B * S
    x2, t2, mem2, ng = x.reshape(T, D), loss_target.reshape(T, D), mem.reshape(B * MEM_LEN, D), norm_g.reshape(1, D)
    h = _rms_fwd(x2, ng, "rms_x")
    later = BIG[1:]
    later_axes = [BIG_AXIS[n] for n in later] + [None]
    conv_pad = jnp.pad(conv_w, ((0, 8 - conv_w.shape[0]), (0, 0)))
    proj, w_in_full, spread = _proj_gather(h, w_in.astype(bf16), _shard_order(),
                                           _spread_comm([w[n].astype(bf16) for n in later] + [conv_pad], later_axes))
    attn = _attn_fwd(proj, attn_q_norm, attn_k_norm, B, S)[:3]
    *fulls, conv_g = _forward_blocks(spread, later_axes)
    wf = dict(zip(later, fulls), w_in=w_in_full)
    conv_full = conv_g[:, :3, :].transpose(1, 0, 2).reshape(3, CONV_W)

    sq_err, g, t = _fwd_bwd_head(x2, mem2, t2, proj, attn, B, S, mem_norm_g, attn_q_norm, attn_k_norm, conv_full, mem_q_norm,
                                 mem_k_norm, wf["mem_w_kv"], wf["w_br_attn"], wf["w_br_conv"], wf["w_br_mem"], wf["w_out"])
    t.update(x2=x2, ng=ng, h=h)

    xyc = jnp.stack([lax.axis_index("x"), lax.axis_index("y"), lax.axis_index("c")]).astype(jnp.int32)
    me1 = me.reshape(1)
    early = ("w_out", "w_br_attn", "w_br_conv", "w_br_mem")
    g["mem_w_kv"], sib_early = _mm(t["mh"], t["dmkv"], mode="tn", out_dtype=f32, name="dw_kv",
                                   comm=_sibling_comm([g[n] for n in early], [BIG_AXIS[n] for n in early]))
    sums_early = [_rs_chip_sum(g[n], ls, xyc, BIG_AXIS[n], "rs_sum_" + n) for n, ls in zip(early, sib_early)]
    tm_w = _pick(D // 2, 1024)
    half = (D // 2) // tm_w
    g_top, (*chips_early, sib_kv) = _mm(t["h"], t["dproj"], mode="tn", out_dtype=f32, name="dw_in_top", tm=tm_w,
                                        m_tiles=(0, half),
                                        comm=_join(_chips_comm(sums_early), _sibling_comm([g["mem_w_kv"]], [0])))
    sum_kv = _rs_chip_sum(g["mem_w_kv"], sib_kv, xyc, 0, "rs_sum_mem_w_kv")
    g_bot, (chips_kv, sib_top) = _mm(t["h"], t["dproj"], mode="tn", out_dtype=f32, name="dw_in_bot", tm=tm_w,
                                     m_tiles=(half, half), comm=_join(_chips_comm([sum_kv]), _sibling_comm([g_top], [1])))
    sum_top = _rs_chip_sum(g_top, sib_top, xyc, 1, "rs_sum_w_in_top")
    tm_t = _pick(T // 2, 1024)
    halfm = (T // 2) // tm_t
    dh, (chips_top, sib_bot) = _mm(t["dproj"], wf["w_in"], mode="nt", out_dtype=f32, name="d_h_a", tm=tm_t,
                                   m_tiles=(0, halfm), into="new",
                                   comm=_join(_chips_comm([sum_top]), _sibling_comm([g_bot], [1])))
    sum_bot = _rs_chip_sum(g_bot, sib_bot, xyc, 1, "rs_sum_w_in_bot")
    dh, (chips_bot,) = _mm(t["dproj"], wf["w_in"], mode="nt", out_dtype=f32, name="d_h_b", tm=tm_t, m_tiles=(halfm, halfm),
                           into=dh, comm=_chips_comm([sum_bot]))
    dx, dng = _rms_bwd(t["x2"], t["ng"], dh, t["dy"], "rms_x_bwd")
    g["norm_g"] = dng.reshape(D)

    grad, delta, new_m, new_v = {}, {}, {}, {}
    for n, ls, lc in zip(early + ("mem_w_kv",), sib_early + [sib_kv], chips_early + [chips_kv]):
        grad[n], delta[n], new_m[n], new_v[n] = _adamw_shard(g[n], ls, lc, w[n], mo[n], vo[n], me1, BIG_AXIS[n], "adamw_" + n)
    top = _adamw_shard(g_top, sib_top, chips_top, w_in, m_w_in, v_w_in, me1, 1, "adamw_w_in_top")
    grad["w_in"], delta["w_in"], new_m["w_in"], new_v["w_in"] = _adamw_shard(
        g_bot, sib_bot, chips_bot, w_in, m_w_in, v_w_in, me1, 1, "adamw_w_in_bot", row0=D // 2, prev=top)

    n_rep = sum(w[n].size for n in SMALL) // LANES
    conv_rows = g["conv_w"].reshape(3, N_DEV, LANES).transpose(1, 0, 2).reshape(3 * N_DEV, LANES)
    n_rows = -(-(n_rep + 3 * N_DEV + 1) // 8) * 8
    part = _pack_rows([g[n] for n in SMALL] + [conv_rows, jnp.full((1, LANES), sq_err, f32)], n_rows)
    tot = _allreduce_small(part)
    loss = tot[n_rep + 3 * N_DEV, 0] * (0.5 / D)
    n_small = -(-(n_rep + 3) // 8) * 8
    g_small = _pack_rows([tot[:n_rep], lax.dynamic_slice(tot, (n_rep + 3 * me, 0), (3, LANES))], n_small)
    names = SMALL + ("conv_w",)
    d_s, m_s, v_s = _adamw_small(g_small, _pack_rows([w[n] for n in names], n_small), _pack_rows([mo[n] for n in names], n_small),
                                 _pack_rows([vo[n] for n in names], n_small))
    off = 0
    for n in names:
        r = w[n].size // LANES
        grad[n], delta[n] = g_small[off:off + r].reshape(w[n].shape), d_s[off:off + r].reshape(w[n].shape)
        new_m[n], new_v[n] = m_s[off:off + r].reshape(w[n].shape), v_s[off:off + r].reshape(w[n].shape)
        off += r
    return (loss, dx.reshape(B, S, D), *[grad[n] for n in ALL_W], *[delta[n] for n in ALL_W], *[new_m[n] for n in ALL_W],
            *[new_v[n] for n in ALL_W])
```

```python
import functools

import jax
import jax.numpy as jnp
from jax import lax
from jax.experimental import pallas as pl
from jax.experimental.pallas import tpu as pltpu

f32 = jnp.float32
bf16 = jnp.bfloat16

N_DEV = 8
HEAD_DIM = 128
DILATIONS = (1, 4, 16)
N_GROUPS = 3
HEADS = 4
BLK = 128
ATTN_QKV = N_GROUPS * HEADS * HEAD_DIM
ATTN_OUT = HEADS * HEAD_DIM
CONV_W = 1024
MEM_LEN = 256
MEM_HEADS = 4
MEM_HD = 256
MEM_W = MEM_HEADS * MEM_HD
EPS = 1e-6
Q0, K0, V0 = 0, ATTN_QKV, 2 * ATTN_QKV
ZA = 3 * ATTN_QKV
CB, CC, CV, ZC = ZA + ATTN_OUT, ZA + ATTN_OUT + CONV_W, ZA + ATTN_OUT + 2 * CONV_W, ZA + ATTN_OUT + 3 * CONV_W
MQ = ZC + CONV_W
ZM = MQ + MEM_W
GT = ZM + MEM_W

ADAM_LR, ADAM_B1, ADAM_B2, ADAM_EPS, ADAM_WD, ADAM_STEP = 0.001, 0.9, 0.999, 1e-08, 0.01, 10

VMEM_LIMIT = 56 * 1024 * 1024
LANES = 128

NT_DIMS = (((1,), (1,)), ((), ()))
TN_DIMS = (((0,), (0,)), ((), ()))
NN_DIMS = (((1,), (0,)), ((), ()))


def _params(sem=None):
    return pltpu.CompilerParams(dimension_semantics=sem, vmem_limit_bytes=VMEM_LIMIT)


def _pick(n, pref, q=LANES):
    t = (min(pref, n) // q) * q
    while t >= q:
        if n % t == 0:
            return t
        t -= q
    return n


def _dot(a, b, dims):
    return lax.dot_general(a.astype(bf16), b.astype(bf16), dims, preferred_element_type=f32)


def _sigmoid(z):
    return 0.5 * jnp.tanh(0.5 * z) + 0.5


def _rstd(v):
    return lax.rsqrt(jnp.mean(v * v, axis=-1, keepdims=True) + EPS)


class _Deferred:
    def __init__(self, stage, sems, step, last, n):
        assert last >= 1
        self.stage, self.sems, self.step, self.last, self.n = stage, sems, step, last, n
        self.slot = step % 2

    def _drain(self, slot, like):
        for c in range(self.n):
            pltpu.make_async_copy(self.stage.at[slot, c], like(c), self.sems.at[slot, c]).wait()

    def begin(self, like):
        pl.when(self.step >= 2)(lambda: self._drain(self.slot, like))

    def start(self, c, dst):
        pltpu.make_async_copy(self.stage.at[self.slot, c], dst, self.sems.at[self.slot, c]).start()

    def end(self, like):
        @pl.when(self.step == self.last)
        def _():
            self._drain(self.slot, like)
            self._drain(1 - self.slot, like)


def _row_sum(v):
    hi = v.astype(bf16)
    lo = (v - hi.astype(f32)).astype(bf16)
    ones = jnp.ones((v.shape[1], v.shape[1]), bf16)
    return _dot(hi, ones, NN_DIMS) + _dot(lo, ones, NN_DIMS)


def _rstd_head(v):
    return lax.rsqrt(_row_sum(v * v) * (1.0 / v.shape[1]) + EPS)


class _Comm:
    def __init__(self, ins, out_shapes, sem_shape, copies):
        self.ins, self.out_shapes, self.sem_shape, self.copies = list(ins), list(out_shapes), sem_shape, copies

    def scratch(self):
        return [pltpu.SemaphoreType.DMA(self.sem_shape), pltpu.SemaphoreType.DMA(self.sem_shape)]


def _mm(a, b, *, mode, out_dtype, name, tm=1024, tn=1024, tk=4096, m_tiles=None, into=None, comm=None):
    if mode == "nn":
        (M, K), (K2, N) = a.shape, b.shape
    elif mode == "nt":
        (M, K), (N, K2) = a.shape, b.shape
    else:
        (K, M), (K2, N) = a.shape, b.shape
    assert K == K2
    tm, tn, tk = _pick(M, tm), _pick(N, tn), _pick(K, tk)
    nk = K // tk
    m0, mc = m_tiles if m_tiles is not None else (0, M // tm)
    o0 = 0 if into is None else m0
    aliased = into is not None and not isinstance(into, str)
    n_ci = len(comm.ins) if comm else 0
    n_co = len(comm.out_shapes) if comm else 0
    grid = (mc, N // tn, nk)
    dims = {"nn": NN_DIMS, "nt": NT_DIMS, "tn": TN_DIMS}[mode]

    def body(*refs, nk):
        a_ref, b_ref = refs[:2]
        pos = 3 if aliased else 2
        c_ins, o_ref, c_outs = refs[pos:pos + n_ci], refs[pos + n_ci], refs[pos + n_ci + 1:pos + n_ci + 1 + n_co]
        scratch = refs[pos + n_ci + 1 + n_co:]
        ids = [pl.program_id(d) for d in range(3)]
        if comm:
            sems = scratch[-2:]

            @pl.when((ids[0] == 0) & (ids[1] == 0) & (ids[2] == 0))
            def _():
                for cp in comm.copies(c_ins, c_outs, *sems):
                    cp.start()

        if nk == 1:
            o_ref[...] = _dot(a_ref[...], b_ref[...], dims).astype(o_ref.dtype)
        else:
            acc_ref, k = scratch[0], ids[2]

            @pl.when(k == 0)
            def _():
                acc_ref[...] = _dot(a_ref[...], b_ref[...], dims)

            @pl.when((k > 0) & (k < nk - 1))
            def _():
                acc_ref[...] += _dot(a_ref[...], b_ref[...], dims)

            @pl.when(k == nk - 1)
            def _():
                o_ref[...] = (acc_ref[...] + _dot(a_ref[...], b_ref[...], dims)).astype(o_ref.dtype)

        if comm:
            @pl.when((ids[0] == grid[0] - 1) & (ids[1] == grid[1] - 1) & (ids[2] == grid[2] - 1))
            def _():
                for cp in comm.copies(c_ins, c_outs, *sems):
                    cp.wait()

    if mode == "tn":
        a_spec = pl.BlockSpec((tk, tm), lambda i, j, k: (k, m0 + i))
    else:
        a_spec = pl.BlockSpec((tm, tk), lambda i, j, k: (m0 + i, k))
    if mode == "nt":
        b_spec = pl.BlockSpec((tn, tk), lambda i, j, k: (j, k))
    else:
        b_spec = pl.BlockSpec((tk, tn), lambda i, j, k: (k, j))
    hbm = pl.BlockSpec(memory_space=pl.ANY)
    res = pl.pallas_call(
        functools.partial(body, nk=nk),
        grid=grid,
        in_specs=[a_spec, b_spec] + [hbm] * (aliased + n_ci),
        out_specs=[pl.BlockSpec((tm, tn), lambda i, j, k: (o0 + i, j))] + [hbm] * n_co,
        out_shape=[jax.ShapeDtypeStruct((mc * tm if into is None else M, N), out_dtype)] + (comm.out_shapes if comm else []),
        scratch_shapes=([pltpu.VMEM((tm, tn), f32)] if nk > 1 else []) + (comm.scratch() if comm else []),
        input_output_aliases={2: 0} if aliased else {},
        compiler_params=_params(("arbitrary",) * 3 if comm else ("parallel", "parallel", "arbitrary")),
        name=name,
    )(a, b, *([into] if aliased else []), *(comm.ins if comm else []))
    return (res[0], list(res[1:])) if comm else res[0]


def _rms_fwd(x, g, name):
    T, D = x.shape
    tm = _pick(T, 256, 8)

    def body(x_ref, g_ref, h_ref):
        xv = x_ref[...]
        h_ref[...] = (xv * _rstd(xv) * g_ref[...]).astype(bf16)

    return pl.pallas_call(
        body, grid=(T // tm,),
        in_specs=[pl.BlockSpec((tm, D), lambda i: (i, 0)), pl.BlockSpec((1, D), lambda i: (0, 0))],
        out_specs=pl.BlockSpec((tm, D), lambda i: (i, 0)),
        out_shape=jax.ShapeDtypeStruct((T, D), bf16),
        compiler_params=_params(("parallel",)), name=name,
    )(x, g)


def _rms_bwd(x, g, dh, dy, name):
    T, D = x.shape
    tm = _pick(T, 256, 8)
    with_dx = dy is not None

    def body(*refs):
        if with_dx:
            x_ref, g_ref, dh_ref, dy_ref, dx_ref, dg_ref = refs
        else:
            x_ref, g_ref, dh_ref, dg_ref = refs
        xv = x_ref[...]
        r = _rstd(xv)
        xh = xv * r
        dhv = dh_ref[...]
        part = jnp.sum(dhv * xh, axis=0, keepdims=True)

        @pl.when(pl.program_id(0) == 0)
        def _():
            dg_ref[...] = part

        @pl.when(pl.program_id(0) > 0)
        def _():
            dg_ref[...] += part

        if with_dx:
            dxh = dhv * g_ref[...]
            dx_ref[...] = dy_ref[...] + r * (dxh - xh * jnp.mean(dxh * xh, axis=-1, keepdims=True))

    row = pl.BlockSpec((tm, D), lambda i: (i, 0))
    vec = pl.BlockSpec((1, D), lambda i: (0, 0))
    if with_dx:
        return pl.pallas_call(
            body, grid=(T // tm,), in_specs=[row, vec, row, row], out_specs=[row, vec],
            out_shape=[jax.ShapeDtypeStruct((T, D), f32), jax.ShapeDtypeStruct((1, D), f32)],
            compiler_params=_params(("arbitrary",)), name=name,
        )(x, g, dh, dy)
    return pl.pallas_call(
        body, grid=(T // tm,), in_specs=[row, vec, row], out_specs=vec,
        out_shape=jax.ShapeDtypeStruct((1, D), f32),
        compiler_params=_params(("arbitrary",)), name=name,
    )(x, g, dh)


UNIT_UNROLL = 4


def _rows(start, size, stride):
    return pl.ds(start, size) if stride == 1 else pl.ds(start, size, stride=stride)


def _attn_units(S, d, first, prev):
    nb = S // d // BLK

    def do_first(r, c):
        first(_rows(r, BLK, d))
        return c

    lax.fori_loop(0, d, do_first, 0, unroll=min(d, UNIT_UNROLL))
    if nb > 1:
        def do_prev(u, c):
            r = u // (nb - 1)
            b = u % (nb - 1) + 1
            base = r + d * b * BLK
            prev(_rows(base, BLK, d), _rows(base - d * BLK, 2 * BLK, d))
            return c

        lax.fori_loop(0, d * (nb - 1), do_prev, 0, unroll=UNIT_UNROLL)


def _band_bias(nkeys):
    i = lax.broadcasted_iota(jnp.int32, (BLK, nkeys), 0)
    j = lax.broadcasted_iota(jnp.int32, (BLK, nkeys), 1)
    ok = (j <= i) if nkeys == BLK else ((j >= i) & (j <= i + BLK))
    return jnp.where(ok, 0.0, -jnp.inf).astype(f32)


def _normalize_into(src_ref, gain, dst_ref, S):
    for c in range(S // 256):
        rows = pl.ds(c * 256, 256)
        v = src_ref[rows, :]
        dst_ref[rows, :] = v * _rstd_head(v) * gain


def _attn_fwd(proj, gq, gk, B, S, comm=None):
    T, NC = proj.shape
    scale = HEAD_DIM ** -0.5
    n_ci = len(comm.ins) if comm else 0
    n_co = len(comm.out_shapes) if comm else 0

    def body(*refs):
        q_ref, k_ref, v_ref, z_ref, gq_ref, gk_ref = refs[:6]
        c_ins = refs[6:6 + n_ci]
        ag_ref, a_ref, lse_ref = refs[6 + n_ci:9 + n_ci]
        c_outs = refs[9 + n_ci:9 + n_ci + n_co]
        qs, ks, o0, o1, o2, l0, l1, l2, bias1, bias2 = refs[9 + n_ci + n_co:19 + n_ci + n_co]
        sems = refs[19 + n_ci + n_co:]
        g = pl.program_id(2)
        if comm:
            @pl.when((pl.program_id(0) == 0) & (pl.program_id(1) == 0) & (g == 0))
            def _():
                for cp in comm.copies(c_ins, c_outs, *sems):
                    cp.start()

        bias1[...] = _band_bias(BLK)
        bias2[...] = _band_bias(2 * BLK)
        _normalize_into(q_ref, gq_ref[pl.ds(g, 1), :], qs, S)
        _normalize_into(k_ref, gk_ref[pl.ds(g, 1), :], ks, S)
        o_s, l_s = (o0, o1, o2), (l0, l1, l2)

        def run(gi):
            def unit(qr, kr, nkeys):
                s = _dot(qs[qr, :], ks[kr, :], NT_DIMS) * scale + (bias1 if nkeys == BLK else bias2)[...]
                m = jnp.max(s, axis=-1, keepdims=True)
                p = jnp.exp(s - m)
                den = jnp.sum(p, axis=-1, keepdims=True)
                o_s[gi][qr, :] = _dot(p, v_ref[kr, :], NN_DIMS) * (1.0 / den)
                l_s[gi][qr, :] = jnp.broadcast_to(m + jnp.log(den), (BLK, HEAD_DIM))

            _attn_units(S, DILATIONS[gi], lambda qr: unit(qr, qr, BLK), lambda qr, kr: unit(qr, kr, 2 * BLK))

        for gi in range(N_GROUPS):
            pl.when(g == gi)(functools.partial(run, gi))

        @pl.when(g == N_GROUPS - 1)
        def _():
            for c in range(S // 256):
                rows = pl.ds(c * 256, 256)
                la, lb, lc = l0[rows, :], l1[rows, :], l2[rows, :]
                m = jnp.maximum(jnp.maximum(la, lb), lc)
                wa, wb, wc = jnp.exp(la - m), jnp.exp(lb - m), jnp.exp(lc - m)
                tot = wa + wb + wc
                a = (wa / tot) * o0[rows, :] + (wb / tot) * o1[rows, :] + (wc / tot) * o2[rows, :]
                z = z_ref[rows, :]
                a_ref[rows, :] = a
                lse_ref[rows, :] = m + jnp.log(tot)
                ag_ref[rows, :] = (a * (z * _sigmoid(z))).astype(bf16)

        if comm:
            @pl.when((pl.program_id(0) == B - 1) & (pl.program_id(1) == HEADS - 1) & (g == N_GROUPS - 1))
            def _():
                for cp in comm.copies(c_ins, c_outs, *sems):
                    cp.wait()

    def slab(col0):
        return pl.BlockSpec((S, HEAD_DIM), lambda b, h, g: (b, col0 // HEAD_DIM + g * HEADS + h))

    gain = pl.BlockSpec((N_GROUPS, HEAD_DIM), lambda b, h, g: (0, 0))
    out = pl.BlockSpec((S, HEAD_DIM), lambda b, h, g: (b, h))
    hbm = pl.BlockSpec(memory_space=pl.ANY)
    res = pl.pallas_call(
        body, grid=(B, HEADS, N_GROUPS),
        in_specs=[slab(Q0), slab(K0), slab(V0), pl.BlockSpec((S, HEAD_DIM), lambda b, h, g: (b, ZA // HEAD_DIM + h)), gain, gain]
        + [hbm] * n_ci,
        out_specs=[out, out, out] + [hbm] * n_co,
        out_shape=[jax.ShapeDtypeStruct((T, ATTN_OUT), bf16), jax.ShapeDtypeStruct((T, ATTN_OUT), f32),
                   jax.ShapeDtypeStruct((T, ATTN_OUT), f32)] + (comm.out_shapes if comm else []),
        scratch_shapes=[pltpu.VMEM((S, HEAD_DIM), f32)] * 8 + [pltpu.VMEM((BLK, BLK), f32), pltpu.VMEM((BLK, 2 * BLK), f32)]
        + (comm.scratch() if comm else []),
        compiler_params=_params(("arbitrary", "arbitrary", "arbitrary")), name="attn_fwd",
    )(proj, proj, proj, proj, gq, gk, *(comm.ins if comm else []))
    return res[0], res[1], res[2], list(res[3:])


def _rms_bwd_rows(raw, gain, dn, on_mxu=True):
    r = _rstd_head(raw) if on_mxu else _rstd(raw)
    xh = raw * r
    dxh = dn * gain
    if on_mxu:
        mean = _row_sum(dxh * xh) * (1.0 / raw.shape[1])
    else:
        mean = jnp.mean(dxh * xh, axis=-1, keepdims=True)
    return r * (dxh - xh * mean), jnp.sum(dn * xh, axis=0, keepdims=True)


def _attn_bwd(proj, gq, gk, a_comb, lse, dag, dproj, B, S):
    T, NC = proj.shape
    scale = HEAD_DIM ** -0.5

    def body(q_ref, k_ref, v_ref, z_ref, gq_ref, gk_ref, a_ref, lse_ref, dag_ref, dproj_in, dproj_ref, dgq_ref, dgk_ref,
             qs, ks, da_s, dl_s, dq_s, dk_s, dv_s, bias1, bias2, stage, dz_stage, sem, dz_sem):
        b, h, g = pl.program_id(0), pl.program_id(1), pl.program_id(2)
        bias1[...] = _band_bias(BLK)
        bias2[...] = _band_bias(2 * BLK)
        gq_row, gk_row = gq_ref[pl.ds(g, 1), :], gk_ref[pl.ds(g, 1), :]
        _normalize_into(q_ref, gq_row, qs, S)
        _normalize_into(k_ref, gk_row, ks, S)

        def dst(c):
            return dproj_ref.at[pl.ds(b * S, S), pl.ds((Q0, K0, V0)[c] + (g * HEADS + h) * HEAD_DIM, HEAD_DIM)]

        out = _Deferred(stage, sem, (b * HEADS + h) * N_GROUPS + g, B * HEADS * N_GROUPS - 1, 3)
        out.begin(dst)

        @pl.when((b == 0) & (h == 0) & (g == 0))
        def _():
            dgq_ref[...] = jnp.zeros_like(dgq_ref)
            dgk_ref[...] = jnp.zeros_like(dgk_ref)

        @pl.when(g == 0)
        def _():
            for c in range(S // 256):
                rows = pl.ds(c * 256, 256)
                z, a, dg_ = z_ref[rows, :], a_ref[rows, :], dag_ref[rows, :]
                sg = _sigmoid(z)
                da = dg_ * (z * sg)
                da_s[rows, :] = da
                dl_s[rows, :] = _row_sum(da * a)
                dz_stage[rows, :] = (dg_ * a * (sg * (1.0 + z * (1.0 - sg)))).astype(bf16)
            cp = pltpu.make_async_copy(dz_stage, dproj_ref.at[pl.ds(b * S, S), pl.ds(ZA + h * HEAD_DIM, HEAD_DIM)], dz_sem)
            cp.start()
            cp.wait()

        dk_s[...] = jnp.zeros_like(dk_s)
        dv_s[...] = jnp.zeros_like(dv_s)

        def run(gi):
            def unit(qr, kr, nkeys):
                q, k, v = qs[qr, :], ks[kr, :], v_ref[kr, :]
                s = _dot(q, k, NT_DIMS) * scale
                p = jnp.exp(s + (bias1 if nkeys == BLK else bias2)[...] - lse_ref[qr, :][:, :1])
                da = da_s[qr, :]
                dp = _dot(da, v, NT_DIMS)
                ds = p * (dp - dl_s[qr, :][:, :1]) * scale
                dq_s[qr, :] = _dot(ds, k, NN_DIMS)
                dk_s[kr, :] += _dot(ds, q, TN_DIMS)
                dv_s[kr, :] += _dot(p, da, TN_DIMS)

            _attn_units(S, DILATIONS[gi], lambda qr: unit(qr, qr, BLK), lambda qr, kr: unit(qr, kr, 2 * BLK))

        for gi in range(N_GROUPS):
            pl.when(g == gi)(functools.partial(run, gi))

        gq_acc = jnp.zeros((1, HEAD_DIM), f32)
        gk_acc = jnp.zeros((1, HEAD_DIM), f32)
        for c in range(S // 256):
            rows = pl.ds(c * 256, 256)
            dq, gq_p = _rms_bwd_rows(q_ref[rows, :], gq_row, dq_s[rows, :])
            dk, gk_p = _rms_bwd_rows(k_ref[rows, :], gk_row, dk_s[rows, :])
            gq_acc, gk_acc = gq_acc + gq_p, gk_acc + gk_p
            stage[out.slot, 0, rows, :] = dq.astype(bf16)
            stage[out.slot, 1, rows, :] = dk.astype(bf16)
            stage[out.slot, 2, rows, :] = dv_s[rows, :].astype(bf16)
        dgq_ref[pl.ds(g, 1), :] += gq_acc
        dgk_ref[pl.ds(g, 1), :] += gk_acc
        for c in range(3):
            out.start(c, dst(c))
        out.end(dst)

    def slab(col0):
        return pl.BlockSpec((S, HEAD_DIM), lambda b, h, g: (b, col0 // HEAD_DIM + g * HEADS + h))

    gain = pl.BlockSpec((N_GROUPS, HEAD_DIM), lambda b, h, g: (0, 0))
    per_slot = pl.BlockSpec((S, HEAD_DIM), lambda b, h, g: (b, h))
    return pl.pallas_call(
        body, grid=(B, HEADS, N_GROUPS),
        in_specs=[slab(Q0), slab(K0), slab(V0), pl.BlockSpec((S, HEAD_DIM), lambda b, h, g: (b, ZA // HEAD_DIM + h)), gain, gain,
                  per_slot, per_slot, per_slot, pl.BlockSpec(memory_space=pl.ANY)],
        out_specs=[pl.BlockSpec(memory_space=pl.ANY), gain, gain],
        out_shape=[jax.ShapeDtypeStruct(dproj.shape, dproj.dtype), jax.ShapeDtypeStruct((N_GROUPS, HEAD_DIM), f32),
                   jax.ShapeDtypeStruct((N_GROUPS, HEAD_DIM), f32)],
        scratch_shapes=[pltpu.VMEM((S, HEAD_DIM), f32)] * 7 + [pltpu.VMEM((BLK, BLK), f32), pltpu.VMEM((BLK, 2 * BLK), f32),
                                                                pltpu.VMEM((2, 3, S, HEAD_DIM), bf16), pltpu.VMEM((S, HEAD_DIM), bf16),
                                                                pltpu.SemaphoreType.DMA((2, 3)), pltpu.SemaphoreType.DMA],
        input_output_aliases={9: 0},
        compiler_params=_params(("arbitrary", "arbitrary", "arbitrary")), name="attn_bwd",
    )(proj, proj, proj, proj, gq, gk, a_comb, lse, dag, dproj)


def _conv_fwd(proj, conv_w, B, S):
    T, NC = proj.shape
    tc = LANES

    def body(cb_ref, cc_ref, cv_ref, z_ref, w_ref, c_ref, ub):
        u = cc_ref[...] * cv_ref[...]
        ub[0:8, :] = jnp.zeros((8, tc), f32)
        ub[8:8 + S, :] = u
        w = w_ref[...]
        y = w[0:1] * u + w[1:2] * ub[pl.ds(7, S), :] + w[2:3] * ub[pl.ds(6, S), :]
        z = z_ref[...]
        c_ref[...] = (cb_ref[...] * y * (z * _sigmoid(z))).astype(bf16)

    def seg(col0):
        return pl.BlockSpec((S, tc), lambda j, b: (b, col0 // tc + j))

    return pl.pallas_call(
        body, grid=(CONV_W // tc, B),
        in_specs=[seg(CB), seg(CC), seg(CV), seg(ZC), pl.BlockSpec((3, tc), lambda j, b: (0, j))],
        out_specs=pl.BlockSpec((S, tc), lambda j, b: (b, j)),
        out_shape=jax.ShapeDtypeStruct((T, CONV_W), bf16),
        scratch_shapes=[pltpu.VMEM((S + 8, tc), f32)],
        compiler_params=_params(("parallel", "parallel")), name="conv_fwd",
    )(proj, proj, proj, proj, conv_w)


def _conv_bwd(proj, conv_w, dc, dproj, B, S):
    T, NC = proj.shape
    tc = LANES

    def body(cb_ref, cc_ref, cv_ref, z_ref, w_ref, dc_ref, dproj_in, dproj_ref, dw_ref, ub, db, stage, sem):
        j, b = pl.program_id(0), pl.program_id(1)

        def dst(c):
            return dproj_ref.at[pl.ds(b * S, S), pl.ds((CB, CC, CV, ZC)[c] + j * tc, tc)]

        out = _Deferred(stage, sem, j * B + b, (CONV_W // tc) * B - 1, 4)
        out.begin(dst)
        cb, cc, cv, z, dcv = cb_ref[...], cc_ref[...], cv_ref[...], z_ref[...], dc_ref[...]
        u = cc * cv
        ub[0:8, :] = jnp.zeros((8, tc), f32)
        ub[8:8 + S, :] = u
        u1, u2 = ub[pl.ds(7, S), :], ub[pl.ds(6, S), :]
        w = w_ref[...]
        y = w[0:1] * u + w[1:2] * u1 + w[2:3] * u2
        sg = _sigmoid(z)
        si = z * sg
        dyv = dcv * cb * si
        db[0:S, :] = dyv
        db[S:S + 8, :] = jnp.zeros((8, tc), f32)
        du = w[0:1] * dyv + w[1:2] * db[pl.ds(1, S), :] + w[2:3] * db[pl.ds(2, S), :]
        stage[out.slot, 0] = (dcv * y * si).astype(bf16)
        stage[out.slot, 1] = (du * cv).astype(bf16)
        stage[out.slot, 2] = (du * cc).astype(bf16)
        stage[out.slot, 3] = (dcv * cb * y * (sg * (1.0 + z * (1.0 - sg)))).astype(bf16)
        for c in range(4):
            out.start(c, dst(c))
        part = jnp.concatenate([jnp.sum(dyv * u, axis=0, keepdims=True), jnp.sum(dyv * u1, axis=0, keepdims=True),
                                jnp.sum(dyv * u2, axis=0, keepdims=True)], axis=0)

        @pl.when(b == 0)
        def _():
            dw_ref[...] = part

        @pl.when(b > 0)
        def _():
            dw_ref[...] += part

        out.end(dst)

    def seg(col0):
        return pl.BlockSpec((S, tc), lambda j, b: (b, col0 // tc + j))

    return pl.pallas_call(
        body, grid=(CONV_W // tc, B),
        in_specs=[seg(CB), seg(CC), seg(CV), seg(ZC), pl.BlockSpec((3, tc), lambda j, b: (0, j)),
                  pl.BlockSpec((S, tc), lambda j, b: (b, j)), pl.BlockSpec(memory_space=pl.ANY)],
        out_specs=[pl.BlockSpec(memory_space=pl.ANY), pl.BlockSpec((3, tc), lambda j, b: (0, j))],
        out_shape=[jax.ShapeDtypeStruct(dproj.shape, dproj.dtype), jax.ShapeDtypeStruct((3, CONV_W), f32)],
        scratch_shapes=[pltpu.VMEM((S + 8, tc), f32), pltpu.VMEM((S + 8, tc), f32), pltpu.VMEM((2, 4, S, tc), bf16),
                        pltpu.SemaphoreType.DMA((2, 4))],
        input_output_aliases={6: 0},
        compiler_params=_params(("arbitrary", "arbitrary")), name="conv_bwd",
    )(proj, proj, proj, proj, conv_w, dc, dproj)


MEM_CHUNK = 256


def _mem_fwd(proj, mkv, gq, gk, B, S):
    T, NC = proj.shape
    scale = MEM_HD ** -0.5

    def body(q_ref, z_ref, k_ref, v_ref, gq_ref, gk_ref, o_ref):
        kv = k_ref[...]
        kn = (kv * _rstd_head(kv) * gk_ref[...]).astype(bf16)
        vv = v_ref[...].astype(bf16)

        def chunk(c, carry):
            rows = pl.ds(pl.multiple_of(c * MEM_CHUNK, MEM_CHUNK), MEM_CHUNK)
            q = q_ref[rows, :]
            qn = q * _rstd(q) * gq_ref[...]
            s = _dot(qn, kn, NT_DIMS) * scale
            p = jnp.exp(s - jnp.max(s, axis=-1, keepdims=True))
            p = p / jnp.sum(p, axis=-1, keepdims=True)
            z = z_ref[rows, :]
            o_ref[rows, :] = (_dot(p, vv, NN_DIMS) * (z * _sigmoid(z))).astype(bf16)
            return carry

        lax.fori_loop(0, S // MEM_CHUNK, chunk, 0)

    gain = pl.BlockSpec((1, MEM_HD), lambda b, h: (0, 0))
    return pl.pallas_call(
        body, grid=(B, MEM_HEADS),
        in_specs=[pl.BlockSpec((S, MEM_HD), lambda b, h: (b, MQ // MEM_HD + h)),
                  pl.BlockSpec((S, MEM_HD), lambda b, h: (b, ZM // MEM_HD + h)),
                  pl.BlockSpec((MEM_LEN, MEM_HD), lambda b, h: (b, h)),
                  pl.BlockSpec((MEM_LEN, MEM_HD), lambda b, h: (b, MEM_HEADS + h)), gain, gain],
        out_specs=pl.BlockSpec((S, MEM_HD), lambda b, h: (b, h)),
        out_shape=jax.ShapeDtypeStruct((T, MEM_W), bf16),
        compiler_params=_params(("parallel", "parallel")), name="mem_fwd",
    )(proj, proj, mkv, mkv, gq, gk)


def _mem_bwd(proj, mkv, gq, gk, dmo, dproj, B, S):
    T, NC = proj.shape
    scale = MEM_HD ** -0.5

    def body(q_ref, z_ref, k_ref, v_ref, gq_ref, gk_ref, dmo_ref, dproj_in, dproj_ref, dmk_ref, dmv_ref, dgq_ref, dgk_ref,
             dkn_s, dv_s, gq_s, stage, sem):
        b, h = pl.program_id(0), pl.program_id(1)

        def dst(c):
            return dproj_ref.at[pl.ds(b * S, S), pl.ds((MQ, ZM)[c] + h * MEM_HD, MEM_HD)]

        out = _Deferred(stage, sem, b * MEM_HEADS + h, B * MEM_HEADS - 1, 2)
        out.begin(dst)
        kv = k_ref[...]
        kn = (kv * _rstd_head(kv) * gk_ref[...]).astype(bf16)
        vv = v_ref[...].astype(bf16)
        dkn_s[...] = jnp.zeros_like(dkn_s)
        dv_s[...] = jnp.zeros_like(dv_s)
        gq_s[...] = jnp.zeros_like(gq_s)

        def chunk(c, carry):
            rows = pl.ds(pl.multiple_of(c * MEM_CHUNK, MEM_CHUNK), MEM_CHUNK)
            q = q_ref[rows, :]
            qn = q * _rstd(q) * gq_ref[...]
            s = _dot(qn, kn, NT_DIMS) * scale
            p = jnp.exp(s - jnp.max(s, axis=-1, keepdims=True))
            p = p / jnp.sum(p, axis=-1, keepdims=True)
            mo = _dot(p, vv, NN_DIMS)
            z, dg_ = z_ref[rows, :], dmo_ref[rows, :]
            sg = _sigmoid(z)
            do = dg_ * (z * sg)
            stage[out.slot, 1, rows, :] = (dg_ * mo * (sg * (1.0 + z * (1.0 - sg)))).astype(bf16)
            dp = _dot(do, vv, NT_DIMS)
            ds = p * (dp - jnp.sum(do * mo, axis=-1, keepdims=True)) * scale
            dq, gq_p = _rms_bwd_rows(q, gq_ref[...], _dot(ds, kn, NN_DIMS), on_mxu=False)
            stage[out.slot, 0, rows, :] = dq.astype(bf16)
            gq_s[...] += gq_p
            dkn_s[...] += _dot(ds, qn, TN_DIMS)
            dv_s[...] += _dot(p, do, TN_DIMS)
            return carry

        lax.fori_loop(0, S // MEM_CHUNK, chunk, 0)
        for c in range(2):
            out.start(c, dst(c))
        dk, gk_p = _rms_bwd_rows(kv, gk_ref[...], dkn_s[...])
        dmk_ref[...] = dk
        dmv_ref[...] = dv_s[...]

        @pl.when((b == 0) & (h == 0))
        def _():
            dgq_ref[...] = gq_s[...]
            dgk_ref[...] = gk_p

        @pl.when((b > 0) | (h > 0))
        def _():
            dgq_ref[...] += gq_s[...]
            dgk_ref[...] += gk_p

        out.end(dst)

    gain = pl.BlockSpec((1, MEM_HD), lambda b, h: (0, 0))
    kvb = pl.BlockSpec((MEM_LEN, MEM_HD), lambda b, h: (b, h))
    return pl.pallas_call(
        body, grid=(B, MEM_HEADS),
        in_specs=[pl.BlockSpec((S, MEM_HD), lambda b, h: (b, MQ // MEM_HD + h)),
                  pl.BlockSpec((S, MEM_HD), lambda b, h: (b, ZM // MEM_HD + h)),
                  kvb, pl.BlockSpec((MEM_LEN, MEM_HD), lambda b, h: (b, MEM_HEADS + h)), gain, gain,
                  pl.BlockSpec((S, MEM_HD), lambda b, h: (b, h)), pl.BlockSpec(memory_space=pl.ANY)],
        out_specs=[pl.BlockSpec(memory_space=pl.ANY), kvb, kvb, gain, gain],
        out_shape=[jax.ShapeDtypeStruct(dproj.shape, dproj.dtype), jax.ShapeDtypeStruct((B * MEM_LEN, MEM_W), f32),
                   jax.ShapeDtypeStruct((B * MEM_LEN, MEM_W), f32), jax.ShapeDtypeStruct((1, MEM_HD), f32),
                   jax.ShapeDtypeStruct((1, MEM_HD), f32)],
        scratch_shapes=[pltpu.VMEM((MEM_LEN, MEM_HD), f32), pltpu.VMEM((MEM_LEN, MEM_HD), f32), pltpu.VMEM((1, MEM_HD), f32),
                        pltpu.VMEM((2, 2, S, MEM_HD), bf16), pltpu.SemaphoreType.DMA((2, 2))],
        input_output_aliases={7: 0},
        compiler_params=_params(("arbitrary", "arbitrary")), name="mem_bwd",
    )(proj, proj, mkv, mkv, gq, gk, dmo, dproj)


def _merge_fwd(proj, ag, cg, mg, wa, wc, wm):
    T, NC = proj.shape
    D = wa.shape[1]
    tm, tn = _pick(T, 1024), _pick(D, 512)
    assert GT % tn == 0

    def body(ag_ref, cg_ref, mg_ref, wa_ref, wc_ref, wm_ref, g0_ref, g1_ref, g2_ref, mer_ref, ba_ref, bc_ref, bm_ref):
        ba = _dot(ag_ref[...], wa_ref[...], NN_DIMS)
        bc = _dot(cg_ref[...], wc_ref[...], NN_DIMS)
        bm = _dot(mg_ref[...], wm_ref[...], NN_DIMS)
        mer_ref[...] = (_sigmoid(g0_ref[...]) * ba + _sigmoid(g1_ref[...]) * bc + _sigmoid(g2_ref[...]) * bm).astype(bf16)
        ba_ref[...] = ba.astype(bf16)
        bc_ref[...] = bc.astype(bf16)
        bm_ref[...] = bm.astype(bf16)

    def act(w):
        return pl.BlockSpec((tm, w), lambda i, j: (i, 0))

    def wt(k):
        return pl.BlockSpec((k, tn), lambda i, j: (0, j))

    def gate(n):
        return pl.BlockSpec((tm, tn), lambda i, j: (i, (GT + n * D) // tn + j))

    out = pl.BlockSpec((tm, tn), lambda i, j: (i, j))
    return pl.pallas_call(
        body, grid=(T // tm, D // tn),
        in_specs=[act(ATTN_OUT), act(CONV_W), act(MEM_W), wt(ATTN_OUT), wt(CONV_W), wt(MEM_W), gate(0), gate(1), gate(2)],
        out_specs=[out] * 4, out_shape=[jax.ShapeDtypeStruct((T, D), bf16)] * 4,
        compiler_params=_params(("parallel", "parallel")), name="merge_fwd",
    )(ag, cg, mg, wa, wc, wm, proj, proj, proj)


def _out_loss(merged, w_out, x, tgt):
    T, D = x.shape
    tm, tn = _pick(T, 512), _pick(D, 512)

    def body(m_ref, w_ref, x_ref, t_ref, dy_ref, dyb_ref, lp_ref):
        e = x_ref[...] + _dot(m_ref[...], w_ref[...], NN_DIMS) - t_ref[...]
        dy = e / D
        dy_ref[...] = dy
        dyb_ref[...] = dy.astype(bf16)
        part = jnp.full((1, 8, LANES), jnp.sum(e * e), f32)

        @pl.when(pl.program_id(1) == 0)
        def _():
            lp_ref[...] = part

        @pl.when(pl.program_id(1) > 0)
        def _():
            lp_ref[...] += part

    tile = pl.BlockSpec((tm, tn), lambda i, j: (i, j))
    return pl.pallas_call(
        body, grid=(T // tm, D // tn),
        in_specs=[pl.BlockSpec((tm, D), lambda i, j: (i, 0)), pl.BlockSpec((D, tn), lambda i, j: (0, j)), tile, tile],
        out_specs=[tile, tile, pl.BlockSpec((1, 8, LANES), lambda i, j: (i, 0, 0))],
        out_shape=[jax.ShapeDtypeStruct((T, D), f32), jax.ShapeDtypeStruct((T, D), bf16),
                   jax.ShapeDtypeStruct((T // tm, 8, LANES), f32)],
        compiler_params=_params(("parallel", "arbitrary")), name="out_loss",
    )(merged, w_out, x, tgt)


def _merge_bwd(proj, dyb, w_out, ba, bc, bm):
    T, NC = proj.shape
    D = w_out.shape[0]
    tm, tn = _pick(T, 512), _pick(D, 512)
    assert GT % tn == 0

    def body(dy_ref, w_ref, ba_ref, bc_ref, bm_ref, g0_ref, g1_ref, g2_ref, da_ref, dc_ref, dm_ref, dproj_ref, stage, sem):
        i, j = pl.program_id(0), pl.program_id(1)

        def dst(n):
            return dproj_ref.at[pl.ds(i * tm, tm), pl.ds(GT + n * D + j * tn, tn)]

        out = _Deferred(stage, sem, i * (D // tn) + j, (T // tm) * (D // tn) - 1, 3)
        out.begin(dst)
        dmer = _dot(dy_ref[...], w_ref[...], NT_DIMS)
        for n, (g_ref, br_ref, o_ref) in enumerate(((g0_ref, ba_ref, da_ref), (g1_ref, bc_ref, dc_ref), (g2_ref, bm_ref, dm_ref))):
            sg = _sigmoid(g_ref[...])
            o_ref[...] = (sg * dmer).astype(bf16)
            stage[out.slot, n] = (dmer * br_ref[...].astype(f32) * (sg * (1.0 - sg))).astype(bf16)
            out.start(n, dst(n))
        out.end(dst)

    tile = pl.BlockSpec((tm, tn), lambda i, j: (i, j))

    def gate(n):
        return pl.BlockSpec((tm, tn), lambda i, j: (i, (GT + n * D) // tn + j))

    return pl.pallas_call(
        body, grid=(T // tm, D // tn),
        in_specs=[pl.BlockSpec((tm, D), lambda i, j: (i, 0)), pl.BlockSpec((tn, D), lambda i, j: (j, 0)), tile, tile, tile,
                  gate(0), gate(1), gate(2)],
        out_specs=[tile, tile, tile, pl.BlockSpec(memory_space=pl.ANY)],
        out_shape=[jax.ShapeDtypeStruct((T, D), bf16)] * 3 + [jax.ShapeDtypeStruct((T, NC), bf16)],
        scratch_shapes=[pltpu.VMEM((2, 3, tm, tn), bf16), pltpu.SemaphoreType.DMA((2, 3))],
        compiler_params=_params(("arbitrary", "arbitrary")), name="merge_bwd",
    )(dyb, w_out, ba, bc, bm, proj, proj, proj)


def _fwd_bwd_head(x2, mem2, t2, proj, attn, B, S, mem_norm_g, attn_q_norm, attn_k_norm, conv_w, mem_q_norm, mem_k_norm,
                  w_kv, w_a, w_c, w_m, w_out):
    D = x2.shape[1]
    mng = mem_norm_g.reshape(1, D)
    mqn, mkn = mem_q_norm.reshape(1, MEM_HD), mem_k_norm.reshape(1, MEM_HD)
    ag, a_comb, lse = attn

    mh = _rms_fwd(mem2, mng, "rms_mem")
    mkv = _mm(mh, w_kv, mode="nn", out_dtype=f32, name="mkv")
    cg = _conv_fwd(proj, conv_w, B, S)
    mg = _mem_fwd(proj, mkv, mqn, mkn, B, S)
    merged, ba, bc, bm = _merge_fwd(proj, ag, cg, mg, w_a, w_c, w_m)
    dy, dyb, lp = _out_loss(merged, w_out, x2, t2)
    sq_err = jnp.sum(lp[:, 0, 0])

    g = {}
    g["w_out"] = _mm(merged, dyb, mode="tn", out_dtype=f32, name="dw_out")
    dba, dbc, dbm, dproj = _merge_bwd(proj, dyb, w_out, ba, bc, bm)
    g["w_br_attn"] = _mm(ag, dba, mode="tn", out_dtype=f32, name="dw_br_attn")
    g["w_br_conv"] = _mm(cg, dbc, mode="tn", out_dtype=f32, name="dw_br_conv")
    g["w_br_mem"] = _mm(mg, dbm, mode="tn", out_dtype=f32, name="dw_br_mem")
    dag = _mm(dba, w_a, mode="nt", out_dtype=f32, name="d_attn_out")
    dcg = _mm(dbc, w_c, mode="nt", out_dtype=f32, name="d_conv_out")
    dmg = _mm(dbm, w_m, mode="nt", out_dtype=f32, name="d_mem_out")
    dproj, g["attn_q_norm"], g["attn_k_norm"] = _attn_bwd(proj, attn_q_norm, attn_k_norm, a_comb, lse, dag, dproj, B, S)
    dproj, g["conv_w"] = _conv_bwd(proj, conv_w, dcg, dproj, B, S)
    dproj, dmk, dmv, dgmq, dgmk = _mem_bwd(proj, mkv, mqn, mkn, dmg, dproj, B, S)
    g["mem_q_norm"], g["mem_k_norm"] = dgmq.reshape(MEM_HD), dgmk.reshape(MEM_HD)
    dmkv = jnp.concatenate([dmk, dmv], axis=1)
    dmh = _mm(dmkv, w_kv, mode="nt", out_dtype=f32, name="d_mem_h")
    g["mem_norm_g"] = _rms_bwd(mem2, mng, dmh, None, "rms_mem_bwd").reshape(D)
    return sq_err, g, dict(dy=dy, dproj=dproj, mh=mh, dmkv=dmkv)


MESH = pl.DeviceIdType.MESH
HBM = pl.BlockSpec(memory_space=pl.ANY)


def _me():
    x, y, c = lax.axis_index("x"), lax.axis_index("y"), lax.axis_index("c")
    return (x, y, c), 4 * x + 2 * y + c


def _peer(k):
    (x, y, c), _ = _me()
    p = (1 - x if k & 4 else x, 1 - y if k & 2 else y, 1 - c if k & 1 else c)
    return p, 4 * p[0] + 2 * p[1] + p[2]


def _window(ref, axis, size, idx):
    if axis is None:
        return ref.at[idx]
    if axis == 0:
        return ref.at[pl.ds(idx * size, size), :]
    return ref.at[:, pl.ds(idx * size, size)]


def _full_shape(s, axis):
    if axis is None:
        return (N_DEV,) + s.shape
    return tuple(N_DEV * d if i == axis else d for i, d in enumerate(s.shape))


OTHER_CHIPS = (4, 2, 6)


def _spread_comm(shards, axes):
    n = len(shards)

    def copies(ins, outs, send_sems, recv_sems, base=0):
        _, me = _me()
        out = []
        for a in range(n):
            mine = _window(outs[a], axes[a], None if axes[a] is None else shards[a].shape[axes[a]], me)
            out.append(pltpu.make_async_copy(ins[a], mine, send_sems.at[base + a, N_CHIP]))
            for k, dist in enumerate((1,) + OTHER_CHIPS):
                dev, _ = _peer(dist)
                out.append(pltpu.make_async_remote_copy(
                    src_ref=ins[a], dst_ref=mine, send_sem=send_sems.at[base + a, k], recv_sem=recv_sems.at[base + a, k],
                    device_id=dev, device_id_type=MESH))
        return out

    shapes = [jax.ShapeDtypeStruct(_full_shape(s, ax), s.dtype) for s, ax in zip(shards, axes)]
    return _Comm(shards, shapes, (n, N_CHIP + 1), copies)


def _forward_blocks(fulls, axes):
    n = len(fulls)
    sizes = [None if ax is None else f.shape[ax] // N_DEV for f, ax in zip(fulls, axes)]

    def body(*refs):
        outs = refs[n:2 * n]
        send_sems, recv_sems = refs[2 * n:]
        sibling, _ = _peer(1)
        copies = []
        for j, dist in enumerate(OTHER_CHIPS):
            _, held = _peer(dist)
            for a in range(n):
                block = _window(outs[a], axes[a], sizes[a], held)
                copies.append(pltpu.make_async_remote_copy(
                    src_ref=block, dst_ref=block, send_sem=send_sems.at[a, j], recv_sem=recv_sems.at[a, j],
                    device_id=sibling, device_id_type=MESH))
        for cp in copies:
            cp.start()
        for cp in copies:
            cp.wait()

    return pl.pallas_call(
        body, in_specs=[HBM] * n, out_specs=[HBM] * n,
        out_shape=[jax.ShapeDtypeStruct(f.shape, f.dtype) for f in fulls],
        scratch_shapes=[pltpu.SemaphoreType.DMA((n, len(OTHER_CHIPS))), pltpu.SemaphoreType.DMA((n, len(OTHER_CHIPS)))],
        input_output_aliases={a: a for a in range(n)},
        name="forward_blocks",
    )(*fulls)


def _shard_order():
    (x, y, c), me = _me()
    xs, ys, xo, yo = _peer(4)[1], _peer(2)[1], _peer(5)[1], _peer(3)[1]
    north = c == 1
    ids = [me, _peer(1)[1], jnp.where(north, xs, ys), jnp.where(north, yo, xo), jnp.where(north, ys, xs),
           jnp.where(north, xo, yo), _peer(6)[1], _peer(7)[1]]
    return jnp.stack(ids).astype(jnp.int32)


def _proj_gather(h, w_shard, order, comm, comm_at):
    T, K = h.shape
    C = w_shard.shape[1]
    tm = _pick(T, 1024)
    nm = T // tm
    ahead = max(nm - 2, 0)
    n_ci, n_co = len(comm.ins), len(comm.out_shapes)

    def body(*refs):
        order_ref, h_ref, ws_ref = refs[:3]
        c_ins = refs[3:3 + n_ci]
        o_ref, wf_ref = refs[3 + n_ci:5 + n_ci]
        c_outs = refs[5 + n_ci:5 + n_ci + n_co]
        wbuf, send_sems, recv_sems, own_sem, buf_sems, c_send, c_recv = refs[5 + n_ci + n_co:]
        t, i = pl.program_id(0), pl.program_id(1)

        @pl.when((t == comm_at) & (i == 0))
        def _():
            for cp in comm.copies(c_ins, c_outs, c_send, c_recv):
                cp.start()
        (x, y, c), me = _me()
        sibling, sib_id = _peer(1)
        chips = [_peer(dist) for dist in OTHER_CHIPS]

        def win(idx):
            return wf_ref.at[:, pl.ds(idx * C, C)]

        def copy(k, block, to, src=None):
            return pltpu.make_async_remote_copy(
                src_ref=win(block) if src is None else src, dst_ref=win(block),
                send_sem=send_sems.at[k], recv_sem=recv_sems.at[k], device_id=to, device_id_type=MESH)

        def fetch(tt, src):
            return pltpu.make_async_copy(src, wbuf.at[tt % 2], buf_sems.at[tt % 2])

        own = pltpu.make_async_copy(ws_ref, win(me), own_sem)
        (x_nbr, x_id), (y_nbr, y_id), (_, d_id) = chips

        def half(k, block, top, to):
            rows = wf_ref.at[pl.ds(0 if top else K // 2, K // 2), pl.ds(block * C, C)]
            return pltpu.make_async_remote_copy(src_ref=rows, dst_ref=rows, send_sem=send_sems.at[k], recv_sem=recv_sems.at[k],
                                                device_id=to, device_id_type=MESH)

        here = (x, y, c)

        def pass_on(from_x):
            if from_x:
                copy(4, x_id, sibling).start()
                half(8, x_id, False, y_nbr).start()
            else:
                copy(5, y_id, sibling).start()
                half(7, y_id, True, x_nbr).start()

        @pl.when((t == 0) & (i == 0))
        def _():
            fetch(0, ws_ref).start()
            own.start()
            copy(0, me, sibling, src=ws_ref).start()

        for tt in range(N_DEV):
            @pl.when((t == tt) & (i == 0))
            def _(tt=tt):
                fetch(tt, ws_ref).wait()

        o_ref[...] = _dot(h_ref[...], wbuf[t % 2], NN_DIMS)

        def schedule(first_x):
            on = c == (1 if first_x else 0)
            (f_dev, f_id), (g_dev, g_id) = ((x_nbr, x_id), (y_nbr, y_id)) if first_x else ((y_nbr, y_id), (x_nbr, x_id))
            kf, kg = (1, 2) if first_x else (2, 1)
            pf, pg = (4, 5) if first_x else (5, 4)

            @pl.when((t == 0) & (i == 0) & on)
            def _():
                copy(kf, me, f_dev, src=ws_ref).start()

            def event(nxt):
                if nxt == 1:
                    copy(0, sib_id, here).wait_recv()
                    return sib_id
                if nxt == 2:
                    copy(kf, f_id, here).wait_recv()
                    copy(kf, me, f_dev, src=ws_ref).wait_send()
                    copy(kg, me, g_dev, src=ws_ref).start()
                    pass_on(first_x)
                    return f_id
                if nxt == 3:
                    copy(pg, g_id + 1 - 2 * c, here).wait_recv()
                    return g_id + 1 - 2 * c
                if nxt == 4:
                    copy(kg, g_id, here).wait_recv()
                    pass_on(not first_x)
                    return g_id
                if nxt == 5:
                    copy(pf, f_id + 1 - 2 * c, here).wait_recv()
                    return f_id + 1 - 2 * c
                if nxt == 6:
                    half(7, d_id, True, here).wait_recv()
                    half(8, d_id, False, here).wait_recv()
                    copy(6, d_id, sibling).start()
                    return d_id
                copy(6, d_id + 1 - 2 * c, here).wait_recv()
                return d_id + 1 - 2 * c

            for tt in range(N_DEV - 1):
                @pl.when((t == tt) & (i == ahead) & on)
                def _(tt=tt):
                    fetch(tt + 1, win(event(tt + 1))).start()

            @pl.when((t == N_DEV - 1) & (i == nm - 1) & on)
            def _():
                copy(kg, me, g_dev, src=ws_ref).wait_send()

        schedule(True)
        schedule(False)

        @pl.when((t == N_DEV - 1) & (i == nm - 1))
        def _():
            copy(0, me, sibling, src=ws_ref).wait_send()
            half(7, y_id, True, x_nbr).wait_send()
            half(8, x_id, False, y_nbr).wait_send()
            for j, (_, dev_id) in enumerate(chips):
                copy(4 + j, dev_id, sibling).wait_send()
            own.wait()
            for cp in comm.copies(c_ins, c_outs, c_send, c_recv):
                cp.wait()

    hbm = pl.BlockSpec(memory_space=pl.ANY)
    res = pl.pallas_call(
        body,
        grid_spec=pltpu.PrefetchScalarGridSpec(
            num_scalar_prefetch=1, grid=(N_DEV, nm),
            in_specs=[pl.BlockSpec((tm, K), lambda t, i, order_ref: (i, 0)), hbm] + [hbm] * n_ci,
            out_specs=[pl.BlockSpec((tm, C), lambda t, i, order_ref: (i, order_ref[t])), hbm] + [hbm] * n_co,
            scratch_shapes=[pltpu.VMEM((2, K, C), bf16), pltpu.SemaphoreType.DMA((9,)), pltpu.SemaphoreType.DMA((9,)),
                            pltpu.SemaphoreType.DMA, pltpu.SemaphoreType.DMA((2,))] + comm.scratch()),
        out_shape=[jax.ShapeDtypeStruct((T, N_DEV * C), f32), jax.ShapeDtypeStruct((K, N_DEV * C), bf16)] + comm.out_shapes,
        compiler_params=_params(("arbitrary", "arbitrary")), name="proj_gather",
    )(order, h, w_shard, *comm.ins)
    return res[0], res[1], list(res[2:])


N_CHIP = 4


def _shard_shape(g, ax):
    return tuple(d // N_DEV if i == ax else d for i, d in enumerate(g.shape))


def _sibling_comm(grads, axes):
    n = len(grads)
    sizes = [g.shape[ax] // N_DEV for g, ax in zip(grads, axes)]

    def copies(ins, lands, send_sems, recv_sems, base=0):
        sibling, _ = _peer(1)
        out = []
        for j in range(N_CHIP):
            _, owner = _peer(2 * j + 1)
            for a in range(n):
                out.append(pltpu.make_async_remote_copy(
                    src_ref=_window(ins[a], axes[a], sizes[a], owner), dst_ref=lands[a].at[j],
                    send_sem=send_sems.at[base + a, j], recv_sem=recv_sems.at[base + a, j], device_id=sibling,
                    device_id_type=MESH))
        return out

    shapes = [jax.ShapeDtypeStruct((N_CHIP,) + _shard_shape(g, ax), g.dtype) for g, ax in zip(grads, axes)]
    return _Comm(grads, shapes, (n, N_CHIP), copies)


def _chips_comm(sums):
    n = len(sums)

    def copies(ins, lands, send_sems, recv_sems, base=0):
        out = []
        for j in range(1, N_CHIP):
            dev, _ = _peer(2 * j)
            for a in range(n):
                out.append(pltpu.make_async_remote_copy(
                    src_ref=ins[a].at[j - 1], dst_ref=lands[a].at[j - 1],
                    send_sem=send_sems.at[base + a, j - 1], recv_sem=recv_sems.at[base + a, j - 1], device_id=dev,
                    device_id_type=MESH))
        return out

    return _Comm(sums, [jax.ShapeDtypeStruct(s.shape, s.dtype) for s in sums], (n, N_CHIP), copies)


def _join(c1, c2):
    n1, m1, k1 = len(c1.ins), len(c1.out_shapes), c1.sem_shape[0]

    def copies(ins, lands, send_sems, recv_sems):
        return (c1.copies(ins[:n1], lands[:m1], send_sems, recv_sems, base=0)
                + c2.copies(ins[n1:], lands[m1:], send_sems, recv_sems, base=k1))

    return _Comm(c1.ins + c2.ins, c1.out_shapes + c2.out_shapes, (k1 + c2.sem_shape[0], N_CHIP), copies)


def _rs_chip_sum(grad, land, xyc, axis, name):
    R, C = land.shape[1:]
    tr = _pick(R, max(8, (640 * 1024) // C), 8)

    def body(xyc_ref, g_ref, l_ref, o_ref):
        o_ref[0] = (g_ref[...] + l_ref[0]).astype(bf16)

    def owner(j, xyc_ref):
        jj = j + 1
        return 4 * (xyc_ref[0] ^ (jj >> 1)) + 2 * (xyc_ref[1] ^ (jj & 1)) + xyc_ref[2]

    if axis == 0:
        g_spec = pl.BlockSpec((tr, C), lambda j, i, xyc_ref: (owner(j, xyc_ref) * (R // tr) + i, 0))
    else:
        g_spec = pl.BlockSpec((tr, C), lambda j, i, xyc_ref: (i, owner(j, xyc_ref)))
    return pl.pallas_call(
        body,
        grid_spec=pltpu.PrefetchScalarGridSpec(
            num_scalar_prefetch=1, grid=(N_CHIP - 1, R // tr),
            in_specs=[g_spec, pl.BlockSpec((1, tr, C), lambda j, i, xyc_ref: (j + 1, i, 0))],
            out_specs=pl.BlockSpec((1, tr, C), lambda j, i, xyc_ref: (j, i, 0))),
        out_shape=jax.ShapeDtypeStruct((N_CHIP - 1, R, C), bf16),
        compiler_params=_params(("parallel", "parallel")), name=name,
    )(xyc, grad, land)


def _allreduce_small(part):
    R = part.shape[0]

    def body(p_ref, o_ref, buf, send_sems, recv_sems):
        (x, y, c), me = _me()
        buf[me] = p_ref[...]
        copies = []
        for k in range(1, N_DEV):
            dev, dev_id = _peer(k)
            copies.append(pltpu.make_async_remote_copy(
                src_ref=p_ref, dst_ref=buf.at[me], send_sem=send_sems.at[k - 1], recv_sem=recv_sems.at[k - 1],
                device_id=dev, device_id_type=MESH))
        for cp in copies:
            cp.start()
        for k in range(1, N_DEV):
            _, dev_id = _peer(k)
            pltpu.make_async_remote_copy(
                src_ref=p_ref, dst_ref=buf.at[dev_id], send_sem=send_sems.at[k - 1], recv_sem=recv_sems.at[k - 1],
                device_id=(x, y, c), device_id_type=MESH).wait_recv()
        for cp in copies:
            cp.wait_send()
        acc = buf[0]
        for d in range(1, N_DEV):
            acc = acc + buf[d]
        o_ref[...] = acc

    return pl.pallas_call(
        body, in_specs=[pl.BlockSpec(memory_space=pltpu.VMEM)], out_specs=pl.BlockSpec(memory_space=pltpu.VMEM),
        out_shape=jax.ShapeDtypeStruct((R, LANES), f32),
        scratch_shapes=[pltpu.VMEM((N_DEV, R, LANES), f32), pltpu.SemaphoreType.DMA((N_DEV - 1,)), pltpu.SemaphoreType.DMA((N_DEV - 1,))],
        name="allreduce_small",
    )(part)


def _adamw_math(w, g, m, v):
    m2 = ADAM_B1 * m + (1.0 - ADAM_B1) * g
    v2 = ADAM_B2 * v + (1.0 - ADAM_B2) * (g * g)
    m_hat = m2 / (1.0 - ADAM_B1 ** ADAM_STEP)
    v_hat = v2 / (1.0 - ADAM_B2 ** ADAM_STEP)
    return -ADAM_LR * (m_hat / (jnp.sqrt(v_hat) + ADAM_EPS) + ADAM_WD * w), m2, v2


def _adamw_shard(grad, land_sib, land_chips, w, m, v, me, axis, name, row0=0, prev=None):
    R, C = land_sib.shape[1:]
    assert axis == 1 or R == w.shape[0]
    tr = _pick(R, max(8, (320 * 1024) // C), 8)
    r0 = row0 // tr
    n_prev = len(prev) if prev else 0

    def body(me_ref, g_ref, s_ref, l_ref, w_ref, m_ref, v_ref, *rest):
        go_ref, d_ref, mo_ref, vo_ref = rest[n_prev:]
        g = g_ref[...] + s_ref[0]
        for j in range(N_CHIP - 1):
            g = g + l_ref[j].astype(f32)
        d, m2, v2 = _adamw_math(w_ref[...], g, m_ref[...], v_ref[...])
        go_ref[...] = g
        d_ref[...] = d
        mo_ref[...] = m2
        vo_ref[...] = v2

    if axis == 0:
        g_spec = pl.BlockSpec((tr, C), lambda i, me_ref: (me_ref[0] * (R // tr) + i, 0))
    else:
        g_spec = pl.BlockSpec((tr, C), lambda i, me_ref: (i, me_ref[0]))
    blk = pl.BlockSpec((tr, C), lambda i, me_ref: (r0 + i, 0))
    return pl.pallas_call(
        body,
        grid_spec=pltpu.PrefetchScalarGridSpec(
            num_scalar_prefetch=1, grid=(R // tr,),
            in_specs=[g_spec, pl.BlockSpec((1, tr, C), lambda i, me_ref: (0, i, 0)),
                      pl.BlockSpec((N_CHIP - 1, tr, C), lambda i, me_ref: (0, i, 0)), blk, blk, blk]
            + [pl.BlockSpec(memory_space=pl.ANY)] * n_prev,
            out_specs=[blk] * 4),
        out_shape=[jax.ShapeDtypeStruct(w.shape, f32)] * 4,
        input_output_aliases={7 + i: i for i in range(n_prev)},
        compiler_params=_params(("parallel",)), name=name,
    )(me, grad, land_sib, land_chips, w, m, v, *(prev or []))


def _adamw_small(g, w, m, v):
    def body(g_ref, w_ref, m_ref, v_ref, d_ref, mo_ref, vo_ref):
        d, m2, v2 = _adamw_math(w_ref[...], g_ref[...], m_ref[...], v_ref[...])
        d_ref[...] = d
        mo_ref[...] = m2
        vo_ref[...] = v2

    return pl.pallas_call(body, out_shape=[jax.ShapeDtypeStruct(g.shape, f32)] * 3, name="adamw_small")(g, w, m, v)


def _pack_rows(parts, total_rows):
    rows = jnp.concatenate([p.reshape(-1, LANES) for p in parts], axis=0)
    return jnp.pad(rows, ((0, total_rows - rows.shape[0]), (0, 0)))


BIG = ("w_in", "mem_w_kv", "w_br_attn", "w_br_conv", "w_br_mem", "w_out")
BIG_AXIS = {"w_in": 1, "mem_w_kv": 0, "w_br_attn": 1, "w_br_conv": 1, "w_br_mem": 1, "w_out": 0}
SMALL = ("norm_g", "mem_norm_g", "attn_q_norm", "attn_k_norm", "mem_q_norm", "mem_k_norm")
ALL_W = ("norm_g", "mem_norm_g", "w_in", "attn_q_norm", "attn_k_norm", "conv_w", "mem_w_kv", "mem_q_norm", "mem_k_norm",
         "w_br_attn", "w_br_conv", "w_br_mem", "w_out")


def kernel(x, mem, norm_g, mem_norm_g, w_in, attn_q_norm, attn_k_norm, conv_w, mem_w_kv, mem_q_norm, mem_k_norm, w_br_attn, w_br_conv, w_br_mem, w_out, loss_target, m_norm_g, m_mem_norm_g, m_w_in, m_attn_q_norm, m_attn_k_norm, m_conv_w, m_mem_w_kv, m_mem_q_norm, m_mem_k_norm, m_w_br_attn, m_w_br_conv, m_w_br_mem, m_w_out, v_norm_g, v_mem_norm_g, v_w_in, v_attn_q_norm, v_attn_k_norm, v_conv_w, v_mem_w_kv, v_mem_q_norm, v_mem_k_norm, v_w_br_attn, v_w_br_conv, v_w_br_mem, v_w_out):
    w = dict(norm_g=norm_g, mem_norm_g=mem_norm_g, w_in=w_in, attn_q_norm=attn_q_norm, attn_k_norm=attn_k_norm, conv_w=conv_w,
             mem_w_kv=mem_w_kv, mem_q_norm=mem_q_norm, mem_k_norm=mem_k_norm, w_br_attn=w_br_attn, w_br_conv=w_br_conv,
             w_br_mem=w_br_mem, w_out=w_out)
    mo = dict(norm_g=m_norm_g, mem_norm_g=m_mem_norm_g, w_in=m_w_in, attn_q_norm=m_attn_q_norm, attn_k_norm=m_attn_k_norm,
              conv_w=m_conv_w, mem_w_kv=m_mem_w_kv, mem_q_norm=m_mem_q_norm, mem_k_norm=m_mem_k_norm, w_br_attn=m_w_br_attn,
              w_br_conv=m_w_br_conv, w_br_mem=m_w_br_mem, w_out=m_w_out)
    vo = dict(norm_g=v_norm_g, mem_norm_g=v_mem_norm_g, w_in=v_w_in, attn_q_norm=v_attn_q_norm, attn_k_norm=v_attn_k_norm,
              conv_w=v_conv_w, mem_w_kv=v_mem_w_kv, mem_q_norm=v_mem_q_norm, mem_k_norm=v_mem_k_norm, w_br_attn=v_w_br_attn,
              w_br_conv=v_w_br_conv, w_br_mem=v_w_br_mem, w_out=v_w_out)
    B, S, D = x.shape
    me = (4 * lax.axis_index("x") + 2 * lax.axis_index("y") + lax.axis_index("c")).astype(jnp.int32)

    T = B * S
    x2, t2, mem2, ng = x.reshape(T, D), loss_target.reshape(T, D), mem.reshape(B * MEM_LEN, D), norm_g.reshape(1, D)
    h = _rms_fwd(x2, ng, "rms_x")
    rows_sharded, cols_sharded = ("mem_w_kv", "w_out"), ("w_br_attn", "w_br_conv", "w_br_mem")
    later = rows_sharded + cols_sharded
    later_axes = [BIG_AXIS[n] for n in later] + [None]
    conv_pad = jnp.pad(conv_w, ((0, 8 - conv_w.shape[0]), (0, 0)))
    proj, w_in_full, spread_a = _proj_gather(
        h, w_in.astype(bf16), _shard_order(),
        _spread_comm([w[n].astype(bf16) for n in rows_sharded], [BIG_AXIS[n] for n in rows_sharded]), comm_at=N_DEV - 3)
    *attn, spread_b = _attn_fwd(proj, attn_q_norm, attn_k_norm, B, S,
                                comm=_spread_comm([w[n].astype(bf16) for n in cols_sharded] + [conv_pad],
                                                  [BIG_AXIS[n] for n in cols_sharded] + [None]))
    *fulls, conv_g = _forward_blocks(spread_a + spread_b, later_axes)
    wf = dict(zip(later, fulls), w_in=w_in_full)
    conv_full = conv_g[:, :3, :].transpose(1, 0, 2).reshape(3, CONV_W)

    sq_err, g, t = _fwd_bwd_head(x2, mem2, t2, proj, attn, B, S, mem_norm_g, attn_q_norm, attn_k_norm, conv_full, mem_q_norm,
                                 mem_k_norm, wf["mem_w_kv"], wf["w_br_attn"], wf["w_br_conv"], wf["w_br_mem"], wf["w_out"])
    t.update(x2=x2, ng=ng, h=h)

    xyc = jnp.stack([lax.axis_index("x"), lax.axis_index("y"), lax.axis_index("c")]).astype(jnp.int32)
    me1 = me.reshape(1)
    early = ("w_out", "w_br_attn", "w_br_conv", "w_br_mem")
    g["mem_w_kv"], sib_early = _mm(t["mh"], t["dmkv"], mode="tn", out_dtype=f32, name="dw_kv",
                                   comm=_sibling_comm([g[n] for n in early], [BIG_AXIS[n] for n in early]))
    sums_early = [_rs_chip_sum(g[n], ls, xyc, BIG_AXIS[n], "rs_sum_" + n) for n, ls in zip(early, sib_early)]
    tm_w = _pick(D // 2, 1024)
    half = (D // 2) // tm_w
    g_top, (*chips_early, sib_kv) = _mm(t["h"], t["dproj"], mode="tn", out_dtype=f32, name="dw_in_top", tm=tm_w,
                                        m_tiles=(0, half),
                                        comm=_join(_chips_comm(sums_early), _sibling_comm([g["mem_w_kv"]], [0])))
    sum_kv = _rs_chip_sum(g["mem_w_kv"], sib_kv, xyc, 0, "rs_sum_mem_w_kv")
    g_bot, (chips_kv, sib_top) = _mm(t["h"], t["dproj"], mode="tn", out_dtype=f32, name="dw_in_bot", tm=tm_w,
                                     m_tiles=(half, half), comm=_join(_chips_comm([sum_kv]), _sibling_comm([g_top], [1])))
    sum_top = _rs_chip_sum(g_top, sib_top, xyc, 1, "rs_sum_w_in_top")
    tm_t = _pick(T // 2, 1024)
    halfm = (T // 2) // tm_t
    dh, (chips_top, sib_bot) = _mm(t["dproj"], wf["w_in"], mode="nt", out_dtype=f32, name="d_h_a", tm=tm_t,
                                   m_tiles=(0, halfm), into="new",
                                   comm=_join(_chips_comm([sum_top]), _sibling_comm([g_bot], [1])))
    sum_bot = _rs_chip_sum(g_bot, sib_bot, xyc, 1, "rs_sum_w_in_bot")
    dh, (chips_bot,) = _mm(t["dproj"], wf["w_in"], mode="nt", out_dtype=f32, name="d_h_b", tm=tm_t, m_tiles=(halfm, halfm),
                           into=dh, comm=_chips_comm([sum_bot]))
    dx, dng = _rms_bwd(t["x2"], t["ng"], dh, t["dy"], "rms_x_bwd")
    g["norm_g"] = dng.reshape(D)

    grad, delta, new_m, new_v = {}, {}, {}, {}
    for n, ls, lc in zip(early + ("mem_w_kv",), sib_early + [sib_kv], chips_early + [chips_kv]):
        grad[n], delta[n], new_m[n], new_v[n] = _adamw_shard(g[n], ls, lc, w[n], mo[n], vo[n], me1, BIG_AXIS[n], "adamw_" + n)
    top = _adamw_shard(g_top, sib_top, chips_top, w_in, m_w_in, v_w_in, me1, 1, "adamw_w_in_top")
    grad["w_in"], delta["w_in"], new_m["w_in"], new_v["w_in"] = _adamw_shard(
        g_bot, sib_bot, chips_bot, w_in, m_w_in, v_w_in, me1, 1, "adamw_w_in_bot", row0=D // 2, prev=top)

    n_rep = sum(w[n].size for n in SMALL) // LANES
    conv_rows = g["conv_w"].reshape(3, N_DEV, LANES).transpose(1, 0, 2).reshape(3 * N_DEV, LANES)
    n_rows = -(-(n_rep + 3 * N_DEV + 1) // 8) * 8
    part = _pack_rows([g[n] for n in SMALL] + [conv_rows, jnp.full((1, LANES), sq_err, f32)], n_rows)
    tot = _allreduce_small(part)
    loss = tot[n_rep + 3 * N_DEV, 0] * (0.5 / D)
    n_small = -(-(n_rep + 3) // 8) * 8
    g_small = _pack_rows([tot[:n_rep], lax.dynamic_slice(tot, (n_rep + 3 * me, 0), (3, LANES))], n_small)
    names = SMALL + ("conv_w",)
    d_s, m_s, v_s = _adamw_small(g_small, _pack_rows([w[n] for n in names], n_small), _pack_rows([mo[n] for n in names], n_small),
                                 _pack_rows([vo[n] for n in names], n_small))
    off = 0
    for n in names:
        r = w[n].size // LANES
        grad[n], delta[n] = g_small[off:off + r].reshape(w[n].shape), d_s[off:off + r].reshape(w[n].shape)
        new_m[n], new_v[n] = m_s[off:off + r].reshape(w[n].shape), v_s[off:off + r].reshape(w[n].shape)
        off += r
    return (loss, dx.reshape(B, S, D), *[grad[n] for n in ALL_W], *[delta[n] for n in ALL_W], *[new_m[n] for n in ALL_W],
            *[new_v[n] for n in ALL_W])
```

```python
import functools

import jax
import jax.numpy as jnp
from jax import lax
from jax.experimental import pallas as pl
from jax.experimental.pallas import tpu as pltpu

f32 = jnp.float32
bf16 = jnp.bfloat16

N_DEV = 8
HEAD_DIM = 128
DILATIONS = (1, 4, 16)
N_GROUPS = 3
HEADS = 4
BLK = 128
ATTN_QKV = N_GROUPS * HEADS * HEAD_DIM
ATTN_OUT = HEADS * HEAD_DIM
CONV_W = 1024
MEM_LEN = 256
MEM_HEADS = 4
MEM_HD = 256
MEM_W = MEM_HEADS * MEM_HD
EPS = 1e-6
Q0, K0, V0 = 0, ATTN_QKV, 2 * ATTN_QKV
ZA = 3 * ATTN_QKV
CB, CC, CV, ZC = ZA + ATTN_OUT, ZA + ATTN_OUT + CONV_W, ZA + ATTN_OUT + 2 * CONV_W, ZA + ATTN_OUT + 3 * CONV_W
MQ = ZC + CONV_W
ZM = MQ + MEM_W
GT = ZM + MEM_W

ADAM_LR, ADAM_B1, ADAM_B2, ADAM_EPS, ADAM_WD, ADAM_STEP = 0.001, 0.9, 0.999, 1e-08, 0.01, 10

VMEM_LIMIT = 56 * 1024 * 1024
D_H_TK = 4352
LANES = 128

NT_DIMS = (((1,), (1,)), ((), ()))
TN_DIMS = (((0,), (0,)), ((), ()))
NN_DIMS = (((1,), (0,)), ((), ()))


def _params(sem=None):
    return pltpu.CompilerParams(dimension_semantics=sem, vmem_limit_bytes=VMEM_LIMIT)


def _pick(n, pref, q=LANES):
    t = (min(pref, n) // q) * q
    while t >= q:
        if n % t == 0:
            return t
        t -= q
    return n


def _dot(a, b, dims):
    return lax.dot_general(a.astype(bf16), b.astype(bf16), dims, preferred_element_type=f32)


def _sigmoid(z):
    return 0.5 * jnp.tanh(0.5 * z) + 0.5


def _rstd(v):
    return lax.rsqrt(jnp.mean(v * v, axis=-1, keepdims=True) + EPS)


class _Deferred:
    def __init__(self, stage, sems, step, last, n):
        assert last >= 1
        self.stage, self.sems, self.step, self.last, self.n = stage, sems, step, last, n
        self.slot = step % 2

    def _drain(self, slot, like):
        for c in range(self.n):
            pltpu.make_async_copy(self.stage.at[slot, c], like(c), self.sems.at[slot, c]).wait()

    def begin(self, like):
        pl.when(self.step >= 2)(lambda: self._drain(self.slot, like))

    def start(self, c, dst):
        pltpu.make_async_copy(self.stage.at[self.slot, c], dst, self.sems.at[self.slot, c]).start()

    def end(self, like):
        @pl.when(self.step == self.last)
        def _():
            self._drain(self.slot, like)
            self._drain(1 - self.slot, like)


def _row_sum(v):
    hi = v.astype(bf16)
    lo = (v - hi.astype(f32)).astype(bf16)
    ones = jnp.ones((v.shape[1], v.shape[1]), bf16)
    return _dot(hi, ones, NN_DIMS) + _dot(lo, ones, NN_DIMS)


def _rstd_head(v):
    return lax.rsqrt(_row_sum(v * v) * (1.0 / v.shape[1]) + EPS)


class _Comm:
    def __init__(self, ins, out_shapes, sem_shape, copies):
        self.ins, self.out_shapes, self.sem_shape, self.copies = list(ins), list(out_shapes), sem_shape, copies

    def scratch(self):
        return [pltpu.SemaphoreType.DMA(self.sem_shape), pltpu.SemaphoreType.DMA(self.sem_shape)]


def _mm(a, b, *, mode, out_dtype, name, tm=1024, tn=1024, tk=4096, m_tiles=None, into=None, comm=None):
    if mode == "nn":
        (M, K), (K2, N) = a.shape, b.shape
    elif mode == "nt":
        (M, K), (N, K2) = a.shape, b.shape
    else:
        (K, M), (K2, N) = a.shape, b.shape
    assert K == K2
    tm, tn, tk = _pick(M, tm), _pick(N, tn), _pick(K, tk)
    nk = K // tk
    m0, mc = m_tiles if m_tiles is not None else (0, M // tm)
    o0 = 0 if into is None else m0
    aliased = into is not None and not isinstance(into, str)
    n_ci = len(comm.ins) if comm else 0
    n_co = len(comm.out_shapes) if comm else 0
    grid = (mc, N // tn, nk)
    dims = {"nn": NN_DIMS, "nt": NT_DIMS, "tn": TN_DIMS}[mode]

    def body(*refs, nk):
        a_ref, b_ref = refs[:2]
        pos = 3 if aliased else 2
        c_ins, o_ref, c_outs = refs[pos:pos + n_ci], refs[pos + n_ci], refs[pos + n_ci + 1:pos + n_ci + 1 + n_co]
        scratch = refs[pos + n_ci + 1 + n_co:]
        ids = [pl.program_id(d) for d in range(3)]
        if comm:
            sems = scratch[-2:]

            @pl.when((ids[0] == 0) & (ids[1] == 0) & (ids[2] == 0))
            def _():
                for cp in comm.copies(c_ins, c_outs, *sems):
                    cp.start()

        if nk == 1:
            o_ref[...] = _dot(a_ref[...], b_ref[...], dims).astype(o_ref.dtype)
        else:
            acc_ref, k = scratch[0], ids[2]

            @pl.when(k == 0)
            def _():
                acc_ref[...] = _dot(a_ref[...], b_ref[...], dims)

            @pl.when((k > 0) & (k < nk - 1))
            def _():
                acc_ref[...] += _dot(a_ref[...], b_ref[...], dims)

            @pl.when(k == nk - 1)
            def _():
                o_ref[...] = (acc_ref[...] + _dot(a_ref[...], b_ref[...], dims)).astype(o_ref.dtype)

        if comm:
            @pl.when((ids[0] == grid[0] - 1) & (ids[1] == grid[1] - 1) & (ids[2] == grid[2] - 1))
            def _():
                for cp in comm.copies(c_ins, c_outs, *sems):
                    cp.wait()

    if mode == "tn":
        a_spec = pl.BlockSpec((tk, tm), lambda i, j, k: (k, m0 + i))
    else:
        a_spec = pl.BlockSpec((tm, tk), lambda i, j, k: (m0 + i, k))
    if mode == "nt":
        b_spec = pl.BlockSpec((tn, tk), lambda i, j, k: (j, k))
    else:
        b_spec = pl.BlockSpec((tk, tn), lambda i, j, k: (k, j))
    hbm = pl.BlockSpec(memory_space=pl.ANY)
    res = pl.pallas_call(
        functools.partial(body, nk=nk),
        grid=grid,
        in_specs=[a_spec, b_spec] + [hbm] * (aliased + n_ci),
        out_specs=[pl.BlockSpec((tm, tn), lambda i, j, k: (o0 + i, j))] + [hbm] * n_co,
        out_shape=[jax.ShapeDtypeStruct((mc * tm if into is None else M, N), out_dtype)] + (comm.out_shapes if comm else []),
        scratch_shapes=([pltpu.VMEM((tm, tn), f32)] if nk > 1 else []) + (comm.scratch() if comm else []),
        input_output_aliases={2: 0} if aliased else {},
        compiler_params=_params(("arbitrary",) * 3 if comm else ("parallel", "parallel", "arbitrary")),
        name=name,
    )(a, b, *([into] if aliased else []), *(comm.ins if comm else []))
    return (res[0], list(res[1:])) if comm else res[0]


def _rms_fwd(x, g, name):
    T, D = x.shape
    tm = _pick(T, 256, 8)

    def body(x_ref, g_ref, h_ref):
        xv = x_ref[...]
        h_ref[...] = (xv * _rstd(xv) * g_ref[...]).astype(bf16)

    return pl.pallas_call(
        body, grid=(T // tm,),
        in_specs=[pl.BlockSpec((tm, D), lambda i: (i, 0)), pl.BlockSpec((1, D), lambda i: (0, 0))],
        out_specs=pl.BlockSpec((tm, D), lambda i: (i, 0)),
        out_shape=jax.ShapeDtypeStruct((T, D), bf16),
        compiler_params=_params(("parallel",)), name=name,
    )(x, g)


def _rms_bwd(x, g, dh, dy, name):
    T, D = x.shape
    tm = _pick(T, 256, 8)
    with_dx = dy is not None

    def body(*refs):
        if with_dx:
            x_ref, g_ref, dh_ref, dy_ref, dx_ref, dg_ref = refs
        else:
            x_ref, g_ref, dh_ref, dg_ref = refs
        xv = x_ref[...]
        r = _rstd(xv)
        xh = xv * r
        dhv = dh_ref[...]
        part = jnp.sum(dhv * xh, axis=0, keepdims=True)

        @pl.when(pl.program_id(0) == 0)
        def _():
            dg_ref[...] = part

        @pl.when(pl.program_id(0) > 0)
        def _():
            dg_ref[...] += part

        if with_dx:
            dxh = dhv * g_ref[...]
            dx_ref[...] = dy_ref[...] + r * (dxh - xh * jnp.mean(dxh * xh, axis=-1, keepdims=True))

    row = pl.BlockSpec((tm, D), lambda i: (i, 0))
    vec = pl.BlockSpec((1, D), lambda i: (0, 0))
    if with_dx:
        return pl.pallas_call(
            body, grid=(T // tm,), in_specs=[row, vec, row, row], out_specs=[row, vec],
            out_shape=[jax.ShapeDtypeStruct((T, D), f32), jax.ShapeDtypeStruct((1, D), f32)],
            compiler_params=_params(("arbitrary",)), name=name,
        )(x, g, dh, dy)
    return pl.pallas_call(
        body, grid=(T // tm,), in_specs=[row, vec, row], out_specs=vec,
        out_shape=jax.ShapeDtypeStruct((1, D), f32),
        compiler_params=_params(("arbitrary",)), name=name,
    )(x, g, dh)


UNIT_UNROLL = 4


def _rows(start, size, stride):
    return pl.ds(start, size) if stride == 1 else pl.ds(start, size, stride=stride)


def _attn_units(S, d, first, prev):
    nb = S // d // BLK

    def do_first(r, c):
        first(_rows(r, BLK, d))
        return c

    lax.fori_loop(0, d, do_first, 0, unroll=min(d, UNIT_UNROLL))
    if nb > 1:
        def do_prev(u, c):
            r = u // (nb - 1)
            b = u % (nb - 1) + 1
            base = r + d * b * BLK
            prev(_rows(base, BLK, d), _rows(base - d * BLK, 2 * BLK, d))
            return c

        lax.fori_loop(0, d * (nb - 1), do_prev, 0, unroll=UNIT_UNROLL)


def _band_bias(nkeys):
    i = lax.broadcasted_iota(jnp.int32, (BLK, nkeys), 0)
    j = lax.broadcasted_iota(jnp.int32, (BLK, nkeys), 1)
    ok = (j <= i) if nkeys == BLK else ((j >= i) & (j <= i + BLK))
    return jnp.where(ok, 0.0, -jnp.inf).astype(f32)


def _normalize_into(src_ref, gain, dst_ref, S):
    for c in range(S // 256):
        rows = pl.ds(c * 256, 256)
        v = src_ref[rows, :]
        dst_ref[rows, :] = v * _rstd_head(v) * gain


def _attn_fwd(proj, gq, gk, B, S, comm=None):
    T, NC = proj.shape
    scale = HEAD_DIM ** -0.5
    n_ci = len(comm.ins) if comm else 0
    n_co = len(comm.out_shapes) if comm else 0

    def body(*refs):
        q_ref, k_ref, v_ref, z_ref, gq_ref, gk_ref = refs[:6]
        c_ins = refs[6:6 + n_ci]
        ag_ref, a_ref, lse_ref = refs[6 + n_ci:9 + n_ci]
        c_outs = refs[9 + n_ci:9 + n_ci + n_co]
        qs, ks, o0, o1, o2, l0, l1, l2, bias1, bias2 = refs[9 + n_ci + n_co:19 + n_ci + n_co]
        sems = refs[19 + n_ci + n_co:]
        g = pl.program_id(2)
        if comm:
            @pl.when((pl.program_id(0) == 0) & (pl.program_id(1) == 0) & (g == 0))
            def _():
                for cp in comm.copies(c_ins, c_outs, *sems):
                    cp.start()

        bias1[...] = _band_bias(BLK)
        bias2[...] = _band_bias(2 * BLK)
        _normalize_into(q_ref, gq_ref[pl.ds(g, 1), :], qs, S)
        _normalize_into(k_ref, gk_ref[pl.ds(g, 1), :], ks, S)
        o_s, l_s = (o0, o1, o2), (l0, l1, l2)

        def run(gi):
            def unit(qr, kr, nkeys):
                s = _dot(qs[qr, :], ks[kr, :], NT_DIMS) * scale + (bias1 if nkeys == BLK else bias2)[...]
                m = jnp.max(s, axis=-1, keepdims=True)
                p = jnp.exp(s - m)
                den = jnp.sum(p, axis=-1, keepdims=True)
                o_s[gi][qr, :] = _dot(p, v_ref[kr, :], NN_DIMS) * (1.0 / den)
                l_s[gi][qr, :] = jnp.broadcast_to(m + jnp.log(den), (BLK, HEAD_DIM))

            _attn_units(S, DILATIONS[gi], lambda qr: unit(qr, qr, BLK), lambda qr, kr: unit(qr, kr, 2 * BLK))

        for gi in range(N_GROUPS):
            pl.when(g == gi)(functools.partial(run, gi))

        @pl.when(g == N_GROUPS - 1)
        def _():
            for c in range(S // 256):
                rows = pl.ds(c * 256, 256)
                la, lb, lc = l0[rows, :], l1[rows, :], l2[rows, :]
                m = jnp.maximum(jnp.maximum(la, lb), lc)
                wa, wb, wc = jnp.exp(la - m), jnp.exp(lb - m), jnp.exp(lc - m)
                tot = wa + wb + wc
                a = (wa / tot) * o0[rows, :] + (wb / tot) * o1[rows, :] + (wc / tot) * o2[rows, :]
                z = z_ref[rows, :]
                a_ref[rows, :] = a
                lse_ref[rows, :] = m + jnp.log(tot)
                ag_ref[rows, :] = (a * (z * _sigmoid(z))).astype(bf16)

        if comm:
            @pl.when((pl.program_id(0) == B - 1) & (pl.program_id(1) == HEADS - 1) & (g == N_GROUPS - 1))
            def _():
                for cp in comm.copies(c_ins, c_outs, *sems):
                    cp.wait()

    def slab(col0):
        return pl.BlockSpec((S, HEAD_DIM), lambda b, h, g: (b, col0 // HEAD_DIM + g * HEADS + h))

    gain = pl.BlockSpec((N_GROUPS, HEAD_DIM), lambda b, h, g: (0, 0))
    out = pl.BlockSpec((S, HEAD_DIM), lambda b, h, g: (b, h))
    hbm = pl.BlockSpec(memory_space=pl.ANY)
    res = pl.pallas_call(
        body, grid=(B, HEADS, N_GROUPS),
        in_specs=[slab(Q0), slab(K0), slab(V0), pl.BlockSpec((S, HEAD_DIM), lambda b, h, g: (b, ZA // HEAD_DIM + h)), gain, gain]
        + [hbm] * n_ci,
        out_specs=[out, out, out] + [hbm] * n_co,
        out_shape=[jax.ShapeDtypeStruct((T, ATTN_OUT), bf16), jax.ShapeDtypeStruct((T, ATTN_OUT), f32),
                   jax.ShapeDtypeStruct((T, ATTN_OUT), f32)] + (comm.out_shapes if comm else []),
        scratch_shapes=[pltpu.VMEM((S, HEAD_DIM), f32)] * 8 + [pltpu.VMEM((BLK, BLK), f32), pltpu.VMEM((BLK, 2 * BLK), f32)]
        + (comm.scratch() if comm else []),
        compiler_params=_params(("arbitrary", "arbitrary", "arbitrary")), name="attn_fwd",
    )(proj, proj, proj, proj, gq, gk, *(comm.ins if comm else []))
    return res[0], res[1], res[2], list(res[3:])


def _rms_bwd_rows(raw, gain, dn, on_mxu=True):
    r = _rstd_head(raw) if on_mxu else _rstd(raw)
    xh = raw * r
    dxh = dn * gain
    if on_mxu:
        mean = _row_sum(dxh * xh) * (1.0 / raw.shape[1])
    else:
        mean = jnp.mean(dxh * xh, axis=-1, keepdims=True)
    return r * (dxh - xh * mean), jnp.sum(dn * xh, axis=0, keepdims=True)


def _attn_bwd(proj, gq, gk, a_comb, lse, dag, dproj, B, S):
    T, NC = proj.shape
    scale = HEAD_DIM ** -0.5

    def body(q_ref, k_ref, v_ref, z_ref, gq_ref, gk_ref, a_ref, lse_ref, dag_ref, dproj_in, dproj_ref, dgq_ref, dgk_ref,
             qs, ks, da_s, dl_s, dq_s, dk_s, dv_s, bias1, bias2, stage, dz_stage, sem, dz_sem):
        b, h, g = pl.program_id(0), pl.program_id(1), pl.program_id(2)
        bias1[...] = _band_bias(BLK)
        bias2[...] = _band_bias(2 * BLK)
        gq_row, gk_row = gq_ref[pl.ds(g, 1), :], gk_ref[pl.ds(g, 1), :]
        _normalize_into(q_ref, gq_row, qs, S)
        _normalize_into(k_ref, gk_row, ks, S)

        def dst(c):
            return dproj_ref.at[pl.ds(b * S, S), pl.ds((Q0, K0, V0)[c] + (g * HEADS + h) * HEAD_DIM, HEAD_DIM)]

        out = _Deferred(stage, sem, (b * HEADS + h) * N_GROUPS + g, B * HEADS * N_GROUPS - 1, 3)
        out.begin(dst)

        @pl.when((b == 0) & (h == 0) & (g == 0))
        def _():
            dgq_ref[...] = jnp.zeros_like(dgq_ref)
            dgk_ref[...] = jnp.zeros_like(dgk_ref)

        @pl.when(g == 0)
        def _():
            for c in range(S // 256):
                rows = pl.ds(c * 256, 256)
                z, a, dg_ = z_ref[rows, :], a_ref[rows, :], dag_ref[rows, :]
                sg = _sigmoid(z)
                da = dg_ * (z * sg)
                da_s[rows, :] = da
                dl_s[rows, :] = _row_sum(da * a)
                dz_stage[rows, :] = (dg_ * a * (sg * (1.0 + z * (1.0 - sg)))).astype(bf16)
            cp = pltpu.make_async_copy(dz_stage, dproj_ref.at[pl.ds(b * S, S), pl.ds(ZA + h * HEAD_DIM, HEAD_DIM)], dz_sem)
            cp.start()
            cp.wait()

        dk_s[...] = jnp.zeros_like(dk_s)
        dv_s[...] = jnp.zeros_like(dv_s)

        def run(gi):
            def unit(qr, kr, nkeys):
                q, k, v = qs[qr, :], ks[kr, :], v_ref[kr, :]
                s = _dot(q, k, NT_DIMS) * scale
                p = jnp.exp(s + (bias1 if nkeys == BLK else bias2)[...] - lse_ref[qr, :][:, :1])
                da = da_s[qr, :]
                dp = _dot(da, v, NT_DIMS)
                ds = p * (dp - dl_s[qr, :][:, :1]) * scale
                dq_s[qr, :] = _dot(ds, k, NN_DIMS)
                dk_s[kr, :] += _dot(ds, q, TN_DIMS)
                dv_s[kr, :] += _dot(p, da, TN_DIMS)

            _attn_units(S, DILATIONS[gi], lambda qr: unit(qr, qr, BLK), lambda qr, kr: unit(qr, kr, 2 * BLK))

        for gi in range(N_GROUPS):
            pl.when(g == gi)(functools.partial(run, gi))

        gq_acc = jnp.zeros((1, HEAD_DIM), f32)
        gk_acc = jnp.zeros((1, HEAD_DIM), f32)
        for c in range(S // 256):
            rows = pl.ds(c * 256, 256)
            dq, gq_p = _rms_bwd_rows(q_ref[rows, :], gq_row, dq_s[rows, :])
            dk, gk_p = _rms_bwd_rows(k_ref[rows, :], gk_row, dk_s[rows, :])
            gq_acc, gk_acc = gq_acc + gq_p, gk_acc + gk_p
            stage[out.slot, 0, rows, :] = dq.astype(bf16)
            stage[out.slot, 1, rows, :] = dk.astype(bf16)
            stage[out.slot, 2, rows, :] = dv_s[rows, :].astype(bf16)
        dgq_ref[pl.ds(g, 1), :] += gq_acc
        dgk_ref[pl.ds(g, 1), :] += gk_acc
        for c in range(3):
            out.start(c, dst(c))
        out.end(dst)

    def slab(col0):
        return pl.BlockSpec((S, HEAD_DIM), lambda b, h, g: (b, col0 // HEAD_DIM + g * HEADS + h))

    gain = pl.BlockSpec((N_GROUPS, HEAD_DIM), lambda b, h, g: (0, 0))
    per_slot = pl.BlockSpec((S, HEAD_DIM), lambda b, h, g: (b, h))
    return pl.pallas_call(
        body, grid=(B, HEADS, N_GROUPS),
        in_specs=[slab(Q0), slab(K0), slab(V0), pl.BlockSpec((S, HEAD_DIM), lambda b, h, g: (b, ZA // HEAD_DIM + h)), gain, gain,
                  per_slot, per_slot, per_slot, pl.BlockSpec(memory_space=pl.ANY)],
        out_specs=[pl.BlockSpec(memory_space=pl.ANY), gain, gain],
        out_shape=[jax.ShapeDtypeStruct(dproj.shape, dproj.dtype), jax.ShapeDtypeStruct((N_GROUPS, HEAD_DIM), f32),
                   jax.ShapeDtypeStruct((N_GROUPS, HEAD_DIM), f32)],
        scratch_shapes=[pltpu.VMEM((S, HEAD_DIM), f32)] * 7 + [pltpu.VMEM((BLK, BLK), f32), pltpu.VMEM((BLK, 2 * BLK), f32),
                                                                pltpu.VMEM((2, 3, S, HEAD_DIM), bf16), pltpu.VMEM((S, HEAD_DIM), bf16),
                                                                pltpu.SemaphoreType.DMA((2, 3)), pltpu.SemaphoreType.DMA],
        input_output_aliases={9: 0},
        compiler_params=_params(("arbitrary", "arbitrary", "arbitrary")), name="attn_bwd",
    )(proj, proj, proj, proj, gq, gk, a_comb, lse, dag, dproj)


def _conv_fwd(proj, conv_w, B, S):
    T, NC = proj.shape
    tc = LANES

    def body(cb_ref, cc_ref, cv_ref, z_ref, w_ref, c_ref, ub):
        u = cc_ref[...] * cv_ref[...]
        ub[0:8, :] = jnp.zeros((8, tc), f32)
        ub[8:8 + S, :] = u
        w = w_ref[...]
        y = w[0:1] * u + w[1:2] * ub[pl.ds(7, S), :] + w[2:3] * ub[pl.ds(6, S), :]
        z = z_ref[...]
        c_ref[...] = (cb_ref[...] * y * (z * _sigmoid(z))).astype(bf16)

    def seg(col0):
        return pl.BlockSpec((S, tc), lambda j, b: (b, col0 // tc + j))

    return pl.pallas_call(
        body, grid=(CONV_W // tc, B),
        in_specs=[seg(CB), seg(CC), seg(CV), seg(ZC), pl.BlockSpec((3, tc), lambda j, b: (0, j))],
        out_specs=pl.BlockSpec((S, tc), lambda j, b: (b, j)),
        out_shape=jax.ShapeDtypeStruct((T, CONV_W), bf16),
        scratch_shapes=[pltpu.VMEM((S + 8, tc), f32)],
        compiler_params=_params(("parallel", "parallel")), name="conv_fwd",
    )(proj, proj, proj, proj, conv_w)


def _conv_bwd(proj, conv_w, dc, dproj, B, S):
    T, NC = proj.shape
    tc = LANES

    def body(cb_ref, cc_ref, cv_ref, z_ref, w_ref, dc_ref, dproj_in, dproj_ref, dw_ref, ub, db, stage, sem):
        j, b = pl.program_id(0), pl.program_id(1)

        def dst(c):
            return dproj_ref.at[pl.ds(b * S, S), pl.ds((CB, CC, CV, ZC)[c] + j * tc, tc)]

        out = _Deferred(stage, sem, j * B + b, (CONV_W // tc) * B - 1, 4)
        out.begin(dst)
        cb, cc, cv, z, dcv = cb_ref[...], cc_ref[...], cv_ref[...], z_ref[...], dc_ref[...]
        u = cc * cv
        ub[0:8, :] = jnp.zeros((8, tc), f32)
        ub[8:8 + S, :] = u
        u1, u2 = ub[pl.ds(7, S), :], ub[pl.ds(6, S), :]
        w = w_ref[...]
        y = w[0:1] * u + w[1:2] * u1 + w[2:3] * u2
        sg = _sigmoid(z)
        si = z * sg
        dyv = dcv * cb * si
        db[0:S, :] = dyv
        db[S:S + 8, :] = jnp.zeros((8, tc), f32)
        du = w[0:1] * dyv + w[1:2] * db[pl.ds(1, S), :] + w[2:3] * db[pl.ds(2, S), :]
        stage[out.slot, 0] = (dcv * y * si).astype(bf16)
        stage[out.slot, 1] = (du * cv).astype(bf16)
        stage[out.slot, 2] = (du * cc).astype(bf16)
        stage[out.slot, 3] = (dcv * cb * y * (sg * (1.0 + z * (1.0 - sg)))).astype(bf16)
        for c in range(4):
            out.start(c, dst(c))
        part = jnp.concatenate([jnp.sum(dyv * u, axis=0, keepdims=True), jnp.sum(dyv * u1, axis=0, keepdims=True),
                                jnp.sum(dyv * u2, axis=0, keepdims=True)], axis=0)

        @pl.when(b == 0)
        def _():
            dw_ref[...] = part

        @pl.when(b > 0)
        def _():
            dw_ref[...] += part

        out.end(dst)

    def seg(col0):
        return pl.BlockSpec((S, tc), lambda j, b: (b, col0 // tc + j))

    return pl.pallas_call(
        body, grid=(CONV_W // tc, B),
        in_specs=[seg(CB), seg(CC), seg(CV), seg(ZC), pl.BlockSpec((3, tc), lambda j, b: (0, j)),
                  pl.BlockSpec((S, tc), lambda j, b: (b, j)), pl.BlockSpec(memory_space=pl.ANY)],
        out_specs=[pl.BlockSpec(memory_space=pl.ANY), pl.BlockSpec((3, tc), lambda j, b: (0, j))],
        out_shape=[jax.ShapeDtypeStruct(dproj.shape, dproj.dtype), jax.ShapeDtypeStruct((3, CONV_W), f32)],
        scratch_shapes=[pltpu.VMEM((S + 8, tc), f32), pltpu.VMEM((S + 8, tc), f32), pltpu.VMEM((2, 4, S, tc), bf16),
                        pltpu.SemaphoreType.DMA((2, 4))],
        input_output_aliases={6: 0},
        compiler_params=_params(("arbitrary", "arbitrary")), name="conv_bwd",
    )(proj, proj, proj, proj, conv_w, dc, dproj)


MEM_CHUNK = 256


def _mem_fwd(proj, mkv, gq, gk, B, S):
    T, NC = proj.shape
    scale = MEM_HD ** -0.5

    def body(q_ref, z_ref, k_ref, v_ref, gq_ref, gk_ref, o_ref):
        kv = k_ref[...]
        kn = (kv * _rstd_head(kv) * gk_ref[...]).astype(bf16)
        vv = v_ref[...].astype(bf16)

        def chunk(c, carry):
            rows = pl.ds(pl.multiple_of(c * MEM_CHUNK, MEM_CHUNK), MEM_CHUNK)
            q = q_ref[rows, :]
            qn = q * _rstd(q) * gq_ref[...]
            s = _dot(qn, kn, NT_DIMS) * scale
            p = jnp.exp(s - jnp.max(s, axis=-1, keepdims=True))
            p = p / jnp.sum(p, axis=-1, keepdims=True)
            z = z_ref[rows, :]
            o_ref[rows, :] = (_dot(p, vv, NN_DIMS) * (z * _sigmoid(z))).astype(bf16)
            return carry

        lax.fori_loop(0, S // MEM_CHUNK, chunk, 0)

    gain = pl.BlockSpec((1, MEM_HD), lambda b, h: (0, 0))
    return pl.pallas_call(
        body, grid=(B, MEM_HEADS),
        in_specs=[pl.BlockSpec((S, MEM_HD), lambda b, h: (b, MQ // MEM_HD + h)),
                  pl.BlockSpec((S, MEM_HD), lambda b, h: (b, ZM // MEM_HD + h)),
                  pl.BlockSpec((MEM_LEN, MEM_HD), lambda b, h: (b, h)),
                  pl.BlockSpec((MEM_LEN, MEM_HD), lambda b, h: (b, MEM_HEADS + h)), gain, gain],
        out_specs=pl.BlockSpec((S, MEM_HD), lambda b, h: (b, h)),
        out_shape=jax.ShapeDtypeStruct((T, MEM_W), bf16),
        compiler_params=_params(("parallel", "parallel")), name="mem_fwd",
    )(proj, proj, mkv, mkv, gq, gk)


def _mem_bwd(proj, mkv, gq, gk, dmo, dproj, B, S):
    T, NC = proj.shape
    scale = MEM_HD ** -0.5

    def body(q_ref, z_ref, k_ref, v_ref, gq_ref, gk_ref, dmo_ref, dproj_in, dproj_ref, dmk_ref, dmv_ref, dgq_ref, dgk_ref,
             dkn_s, dv_s, gq_s, stage, sem):
        b, h = pl.program_id(0), pl.program_id(1)

        def dst(c):
            return dproj_ref.at[pl.ds(b * S, S), pl.ds((MQ, ZM)[c] + h * MEM_HD, MEM_HD)]

        out = _Deferred(stage, sem, b * MEM_HEADS + h, B * MEM_HEADS - 1, 2)
        out.begin(dst)
        kv = k_ref[...]
        kn = (kv * _rstd_head(kv) * gk_ref[...]).astype(bf16)
        vv = v_ref[...].astype(bf16)
        dkn_s[...] = jnp.zeros_like(dkn_s)
        dv_s[...] = jnp.zeros_like(dv_s)
        gq_s[...] = jnp.zeros_like(gq_s)

        def chunk(c, carry):
            rows = pl.ds(pl.multiple_of(c * MEM_CHUNK, MEM_CHUNK), MEM_CHUNK)
            q = q_ref[rows, :]
            qn = q * _rstd(q) * gq_ref[...]
            s = _dot(qn, kn, NT_DIMS) * scale
            p = jnp.exp(s - jnp.max(s, axis=-1, keepdims=True))
            p = p / jnp.sum(p, axis=-1, keepdims=True)
            mo = _dot(p, vv, NN_DIMS)
            z, dg_ = z_ref[rows, :], dmo_ref[rows, :]
            sg = _sigmoid(z)
            do = dg_ * (z * sg)
            stage[out.slot, 1, rows, :] = (dg_ * mo * (sg * (1.0 + z * (1.0 - sg)))).astype(bf16)
            dp = _dot(do, vv, NT_DIMS)
            ds = p * (dp - jnp.sum(do * mo, axis=-1, keepdims=True)) * scale
            dq, gq_p = _rms_bwd_rows(q, gq_ref[...], _dot(ds, kn, NN_DIMS), on_mxu=False)
            stage[out.slot, 0, rows, :] = dq.astype(bf16)
            gq_s[...] += gq_p
            dkn_s[...] += _dot(ds, qn, TN_DIMS)
            dv_s[...] += _dot(p, do, TN_DIMS)
            return carry

        lax.fori_loop(0, S // MEM_CHUNK, chunk, 0)
        for c in range(2):
            out.start(c, dst(c))
        dk, gk_p = _rms_bwd_rows(kv, gk_ref[...], dkn_s[...])
        dmk_ref[...] = dk
        dmv_ref[...] = dv_s[...]

        @pl.when((b == 0) & (h == 0))
        def _():
            dgq_ref[...] = gq_s[...]
            dgk_ref[...] = gk_p

        @pl.when((b > 0) | (h > 0))
        def _():
            dgq_ref[...] += gq_s[...]
            dgk_ref[...] += gk_p

        out.end(dst)

    gain = pl.BlockSpec((1, MEM_HD), lambda b, h: (0, 0))
    kvb = pl.BlockSpec((MEM_LEN, MEM_HD), lambda b, h: (b, h))
    return pl.pallas_call(
        body, grid=(B, MEM_HEADS),
        in_specs=[pl.BlockSpec((S, MEM_HD), lambda b, h: (b, MQ // MEM_HD + h)),
                  pl.BlockSpec((S, MEM_HD), lambda b, h: (b, ZM // MEM_HD + h)),
                  kvb, pl.BlockSpec((MEM_LEN, MEM_HD), lambda b, h: (b, MEM_HEADS + h)), gain, gain,
                  pl.BlockSpec((S, MEM_HD), lambda b, h: (b, h)), pl.BlockSpec(memory_space=pl.ANY)],
        out_specs=[pl.BlockSpec(memory_space=pl.ANY), kvb, kvb, gain, gain],
        out_shape=[jax.ShapeDtypeStruct(dproj.shape, dproj.dtype), jax.ShapeDtypeStruct((B * MEM_LEN, MEM_W), f32),
                   jax.ShapeDtypeStruct((B * MEM_LEN, MEM_W), f32), jax.ShapeDtypeStruct((1, MEM_HD), f32),
                   jax.ShapeDtypeStruct((1, MEM_HD), f32)],
        scratch_shapes=[pltpu.VMEM((MEM_LEN, MEM_HD), f32), pltpu.VMEM((MEM_LEN, MEM_HD), f32), pltpu.VMEM((1, MEM_HD), f32),
                        pltpu.VMEM((2, 2, S, MEM_HD), bf16), pltpu.SemaphoreType.DMA((2, 2))],
        input_output_aliases={7: 0},
        compiler_params=_params(("arbitrary", "arbitrary")), name="mem_bwd",
    )(proj, proj, mkv, mkv, gq, gk, dmo, dproj)


def _merge_fwd(proj, ag, cg, mg, wa, wc, wm):
    T, NC = proj.shape
    D = wa.shape[1]
    tm, tn = _pick(T, 1024), _pick(D, 512)
    assert GT % tn == 0

    def body(ag_ref, cg_ref, mg_ref, wa_ref, wc_ref, wm_ref, g0_ref, g1_ref, g2_ref, mer_ref, ba_ref, bc_ref, bm_ref):
        ba = _dot(ag_ref[...], wa_ref[...], NN_DIMS)
        bc = _dot(cg_ref[...], wc_ref[...], NN_DIMS)
        bm = _dot(mg_ref[...], wm_ref[...], NN_DIMS)
        mer_ref[...] = (_sigmoid(g0_ref[...]) * ba + _sigmoid(g1_ref[...]) * bc + _sigmoid(g2_ref[...]) * bm).astype(bf16)
        ba_ref[...] = ba.astype(bf16)
        bc_ref[...] = bc.astype(bf16)
        bm_ref[...] = bm.astype(bf16)

    def act(w):
        return pl.BlockSpec((tm, w), lambda i, j: (i, 0))

    def wt(k):
        return pl.BlockSpec((k, tn), lambda i, j: (0, j))

    def gate(n):
        return pl.BlockSpec((tm, tn), lambda i, j: (i, (GT + n * D) // tn + j))

    out = pl.BlockSpec((tm, tn), lambda i, j: (i, j))
    return pl.pallas_call(
        body, grid=(T // tm, D // tn),
        in_specs=[act(ATTN_OUT), act(CONV_W), act(MEM_W), wt(ATTN_OUT), wt(CONV_W), wt(MEM_W), gate(0), gate(1), gate(2)],
        out_specs=[out] * 4, out_shape=[jax.ShapeDtypeStruct((T, D), bf16)] * 4,
        compiler_params=_params(("parallel", "parallel")), name="merge_fwd",
    )(ag, cg, mg, wa, wc, wm, proj, proj, proj)


def _out_loss(merged, w_out, x, tgt):
    T, D = x.shape
    tm, tn = _pick(T, 1024), _pick(D, 512)

    def body(m_ref, w_ref, x_ref, t_ref, dy_ref, dyb_ref, lp_ref):
        e = x_ref[...] + _dot(m_ref[...], w_ref[...], NN_DIMS) - t_ref[...]
        dy = e / D
        dy_ref[...] = dy
        dyb_ref[...] = dy.astype(bf16)
        part = jnp.full((1, 8, LANES), jnp.sum(e * e), f32)

        @pl.when(pl.program_id(1) == 0)
        def _():
            lp_ref[...] = part

        @pl.when(pl.program_id(1) > 0)
        def _():
            lp_ref[...] += part

    tile = pl.BlockSpec((tm, tn), lambda i, j: (i, j))
    return pl.pallas_call(
        body, grid=(T // tm, D // tn),
        in_specs=[pl.BlockSpec((tm, D), lambda i, j: (i, 0)), pl.BlockSpec((D, tn), lambda i, j: (0, j)), tile, tile],
        out_specs=[tile, tile, pl.BlockSpec((1, 8, LANES), lambda i, j: (i, 0, 0))],
        out_shape=[jax.ShapeDtypeStruct((T, D), f32), jax.ShapeDtypeStruct((T, D), bf16),
                   jax.ShapeDtypeStruct((T // tm, 8, LANES), f32)],
        compiler_params=_params(("parallel", "arbitrary")), name="out_loss",
    )(merged, w_out, x, tgt)


def _merge_bwd(proj, dyb, w_out, ba, bc, bm):
    T, NC = proj.shape
    D = w_out.shape[0]
    tm, tn = _pick(T, 1024), _pick(D, 512)
    assert GT % tn == 0

    def body(dy_ref, w_ref, ba_ref, bc_ref, bm_ref, g0_ref, g1_ref, g2_ref, da_ref, dc_ref, dm_ref, dproj_ref, stage, sem):
        i, j = pl.program_id(0), pl.program_id(1)

        def dst(n):
            return dproj_ref.at[pl.ds(i * tm, tm), pl.ds(GT + n * D + j * tn, tn)]

        out = _Deferred(stage, sem, i * (D // tn) + j, (T // tm) * (D // tn) - 1, 3)
        out.begin(dst)
        dmer = _dot(dy_ref[...], w_ref[...], NT_DIMS)
        for n, (g_ref, br_ref, o_ref) in enumerate(((g0_ref, ba_ref, da_ref), (g1_ref, bc_ref, dc_ref), (g2_ref, bm_ref, dm_ref))):
            sg = _sigmoid(g_ref[...])
            o_ref[...] = (sg * dmer).astype(bf16)
            stage[out.slot, n] = (dmer * br_ref[...].astype(f32) * (sg * (1.0 - sg))).astype(bf16)
            out.start(n, dst(n))
        out.end(dst)

    tile = pl.BlockSpec((tm, tn), lambda i, j: (i, j))

    def gate(n):
        return pl.BlockSpec((tm, tn), lambda i, j: (i, (GT + n * D) // tn + j))

    return pl.pallas_call(
        body, grid=(T // tm, D // tn),
        in_specs=[pl.BlockSpec((tm, D), lambda i, j: (i, 0)), pl.BlockSpec((tn, D), lambda i, j: (j, 0)), tile, tile, tile,
                  gate(0), gate(1), gate(2)],
        out_specs=[tile, tile, tile, pl.BlockSpec(memory_space=pl.ANY)],
        out_shape=[jax.ShapeDtypeStruct((T, D), bf16)] * 3 + [jax.ShapeDtypeStruct((T, NC), bf16)],
        scratch_shapes=[pltpu.VMEM((2, 3, tm, tn), bf16), pltpu.SemaphoreType.DMA((2, 3))],
        compiler_params=_params(("arbitrary", "arbitrary")), name="merge_bwd",
    )(dyb, w_out, ba, bc, bm, proj, proj, proj)


def _fwd_bwd_head(x2, mem2, t2, proj, attn, B, S, mem_norm_g, attn_q_norm, attn_k_norm, conv_w, mem_q_norm, mem_k_norm,
                  w_kv, w_a, w_c, w_m, w_out):
    D = x2.shape[1]
    mng = mem_norm_g.reshape(1, D)
    mqn, mkn = mem_q_norm.reshape(1, MEM_HD), mem_k_norm.reshape(1, MEM_HD)
    ag, a_comb, lse = attn

    mh = _rms_fwd(mem2, mng, "rms_mem")
    mkv = _mm(mh, w_kv, mode="nn", out_dtype=f32, name="mkv")
    cg = _conv_fwd(proj, conv_w, B, S)
    mg = _mem_fwd(proj, mkv, mqn, mkn, B, S)
    merged, ba, bc, bm = _merge_fwd(proj, ag, cg, mg, w_a, w_c, w_m)
    dy, dyb, lp = _out_loss(merged, w_out, x2, t2)
    sq_err = jnp.sum(lp[:, 0, 0])

    g = {}
    g["w_out"] = _mm(merged, dyb, mode="tn", out_dtype=f32, name="dw_out")
    dba, dbc, dbm, dproj = _merge_bwd(proj, dyb, w_out, ba, bc, bm)
    g["w_br_attn"] = _mm(ag, dba, mode="tn", out_dtype=f32, name="dw_br_attn")
    g["w_br_conv"] = _mm(cg, dbc, mode="tn", out_dtype=f32, name="dw_br_conv")
    g["w_br_mem"] = _mm(mg, dbm, mode="tn", out_dtype=f32, name="dw_br_mem")
    dag = _mm(dba, w_a, mode="nt", out_dtype=f32, name="d_attn_out")
    dcg = _mm(dbc, w_c, mode="nt", out_dtype=f32, name="d_conv_out")
    dmg = _mm(dbm, w_m, mode="nt", out_dtype=f32, name="d_mem_out")
    dproj, g["attn_q_norm"], g["attn_k_norm"] = _attn_bwd(proj, attn_q_norm, attn_k_norm, a_comb, lse, dag, dproj, B, S)
    dproj, g["conv_w"] = _conv_bwd(proj, conv_w, dcg, dproj, B, S)
    dproj, dmk, dmv, dgmq, dgmk = _mem_bwd(proj, mkv, mqn, mkn, dmg, dproj, B, S)
    g["mem_q_norm"], g["mem_k_norm"] = dgmq.reshape(MEM_HD), dgmk.reshape(MEM_HD)
    dmkv = jnp.concatenate([dmk, dmv], axis=1)
    dmh = _mm(dmkv, w_kv, mode="nt", out_dtype=f32, name="d_mem_h")
    g["mem_norm_g"] = _rms_bwd(mem2, mng, dmh, None, "rms_mem_bwd").reshape(D)
    return sq_err, g, dict(dy=dy, dproj=dproj, mh=mh, dmkv=dmkv)


MESH = pl.DeviceIdType.MESH
HBM = pl.BlockSpec(memory_space=pl.ANY)


def _me():
    x, y, c = lax.axis_index("x"), lax.axis_index("y"), lax.axis_index("c")
    return (x, y, c), 4 * x + 2 * y + c


def _peer(k):
    (x, y, c), _ = _me()
    p = (1 - x if k & 4 else x, 1 - y if k & 2 else y, 1 - c if k & 1 else c)
    return p, 4 * p[0] + 2 * p[1] + p[2]


def _window(ref, axis, size, idx):
    if axis is None:
        return ref.at[idx]
    if axis == 0:
        return ref.at[pl.ds(idx * size, size), :]
    return ref.at[:, pl.ds(idx * size, size)]


def _full_shape(s, axis):
    if axis is None:
        return (N_DEV,) + s.shape
    return tuple(N_DEV * d if i == axis else d for i, d in enumerate(s.shape))


OTHER_CHIPS = (4, 2, 6)


def _spread_comm(shards, axes):
    n = len(shards)

    def copies(ins, outs, send_sems, recv_sems, base=0):
        _, me = _me()
        out = []
        for a in range(n):
            mine = _window(outs[a], axes[a], None if axes[a] is None else shards[a].shape[axes[a]], me)
            out.append(pltpu.make_async_copy(ins[a], mine, send_sems.at[base + a, N_CHIP]))
            for k, dist in enumerate((1,) + OTHER_CHIPS):
                dev, _ = _peer(dist)
                out.append(pltpu.make_async_remote_copy(
                    src_ref=ins[a], dst_ref=mine, send_sem=send_sems.at[base + a, k], recv_sem=recv_sems.at[base + a, k],
                    device_id=dev, device_id_type=MESH))
        return out

    shapes = [jax.ShapeDtypeStruct(_full_shape(s, ax), s.dtype) for s, ax in zip(shards, axes)]
    return _Comm(shards, shapes, (n, N_CHIP + 1), copies)


def _forward_blocks(fulls, axes):
    n = len(fulls)
    sizes = [None if ax is None else f.shape[ax] // N_DEV for f, ax in zip(fulls, axes)]

    def body(*refs):
        outs = refs[n:2 * n]
        send_sems, recv_sems = refs[2 * n:]
        sibling, _ = _peer(1)
        copies = []
        for j, dist in enumerate(OTHER_CHIPS):
            _, held = _peer(dist)
            for a in range(n):
                block = _window(outs[a], axes[a], sizes[a], held)
                copies.append(pltpu.make_async_remote_copy(
                    src_ref=block, dst_ref=block, send_sem=send_sems.at[a, j], recv_sem=recv_sems.at[a, j],
                    device_id=sibling, device_id_type=MESH))
        for cp in copies:
            cp.start()
        for cp in copies:
            cp.wait()

    return pl.pallas_call(
        body, in_specs=[HBM] * n, out_specs=[HBM] * n,
        out_shape=[jax.ShapeDtypeStruct(f.shape, f.dtype) for f in fulls],
        scratch_shapes=[pltpu.SemaphoreType.DMA((n, len(OTHER_CHIPS))), pltpu.SemaphoreType.DMA((n, len(OTHER_CHIPS)))],
        input_output_aliases={a: a for a in range(n)},
        name="forward_blocks",
    )(*fulls)


def _shard_order():
    (x, y, c), me = _me()
    xs, ys, xo, yo = _peer(4)[1], _peer(2)[1], _peer(5)[1], _peer(3)[1]
    north = c == 1
    ids = [me, _peer(1)[1], jnp.where(north, xs, ys), jnp.where(north, yo, xo), jnp.where(north, ys, xs),
           jnp.where(north, xo, yo), _peer(6)[1], _peer(7)[1]]
    return jnp.stack(ids).astype(jnp.int32)


def _proj_gather(h, w_shard, order, comm, comm_at):
    T, K = h.shape
    C = w_shard.shape[1]
    tm = _pick(T, 1024)
    nm = T // tm
    ahead = max(nm - 2, 0)
    n_ci, n_co = len(comm.ins), len(comm.out_shapes)

    def body(*refs):
        order_ref, h_ref, ws_ref = refs[:3]
        c_ins = refs[3:3 + n_ci]
        o_ref, wf_ref = refs[3 + n_ci:5 + n_ci]
        c_outs = refs[5 + n_ci:5 + n_ci + n_co]
        wbuf, send_sems, recv_sems, own_sem, buf_sems, c_send, c_recv = refs[5 + n_ci + n_co:]
        t, i = pl.program_id(0), pl.program_id(1)

        @pl.when((t == comm_at) & (i == 0))
        def _():
            for cp in comm.copies(c_ins, c_outs, c_send, c_recv):
                cp.start()
        (x, y, c), me = _me()
        sibling, sib_id = _peer(1)
        chips = [_peer(dist) for dist in OTHER_CHIPS]

        def win(idx):
            return wf_ref.at[:, pl.ds(idx * C, C)]

        def copy(k, block, to, src=None):
            return pltpu.make_async_remote_copy(
                src_ref=win(block) if src is None else src, dst_ref=win(block),
                send_sem=send_sems.at[k], recv_sem=recv_sems.at[k], device_id=to, device_id_type=MESH)

        def fetch(tt, src):
            return pltpu.make_async_copy(src, wbuf.at[tt % 2], buf_sems.at[tt % 2])

        own = pltpu.make_async_copy(ws_ref, win(me), own_sem)
        (x_nbr, x_id), (y_nbr, y_id), (_, d_id) = chips

        def half(k, block, top, to):
            rows = wf_ref.at[pl.ds(0 if top else K // 2, K // 2), pl.ds(block * C, C)]
            return pltpu.make_async_remote_copy(src_ref=rows, dst_ref=rows, send_sem=send_sems.at[k], recv_sem=recv_sems.at[k],
                                                device_id=to, device_id_type=MESH)

        here = (x, y, c)

        def pass_on(from_x):
            if from_x:
                copy(4, x_id, sibling).start()
                half(8, x_id, False, y_nbr).start()
            else:
                copy(5, y_id, sibling).start()
                half(7, y_id, True, x_nbr).start()

        @pl.when((t == 0) & (i == 0))
        def _():
            fetch(0, ws_ref).start()
            own.start()
            copy(0, me, sibling, src=ws_ref).start()

        for tt in range(N_DEV):
            @pl.when((t == tt) & (i == 0))
            def _(tt=tt):
                fetch(tt, ws_ref).wait()

        o_ref[...] = _dot(h_ref[...], wbuf[t % 2], NN_DIMS)

        def schedule(first_x):
            on = c == (1 if first_x else 0)
            (f_dev, f_id), (g_dev, g_id) = ((x_nbr, x_id), (y_nbr, y_id)) if first_x else ((y_nbr, y_id), (x_nbr, x_id))
            kf, kg = (1, 2) if first_x else (2, 1)
            pf, pg = (4, 5) if first_x else (5, 4)

            @pl.when((t == 0) & (i == 0) & on)
            def _():
                copy(kf, me, f_dev, src=ws_ref).start()

            def event(nxt):
                if nxt == 1:
                    copy(0, sib_id, here).wait_recv()
                    return sib_id
                if nxt == 2:
                    copy(kf, f_id, here).wait_recv()
                    copy(kf, me, f_dev, src=ws_ref).wait_send()
                    copy(kg, me, g_dev, src=ws_ref).start()
                    pass_on(first_x)
                    return f_id
                if nxt == 3:
                    copy(pg, g_id + 1 - 2 * c, here).wait_recv()
                    return g_id + 1 - 2 * c
                if nxt == 4:
                    copy(kg, g_id, here).wait_recv()
                    pass_on(not first_x)
                    return g_id
                if nxt == 5:
                    copy(pf, f_id + 1 - 2 * c, here).wait_recv()
                    return f_id + 1 - 2 * c
                if nxt == 6:
                    half(7, d_id, True, here).wait_recv()
                    half(8, d_id, False, here).wait_recv()
                    copy(6, d_id, sibling).start()
                    return d_id
                copy(6, d_id + 1 - 2 * c, here).wait_recv()
                return d_id + 1 - 2 * c

            for tt in range(N_DEV - 1):
                @pl.when((t == tt) & (i == ahead) & on)
                def _(tt=tt):
                    fetch(tt + 1, win(event(tt + 1))).start()

            @pl.when((t == N_DEV - 1) & (i == nm - 1) & on)
            def _():
                copy(kg, me, g_dev, src=ws_ref).wait_send()

        schedule(True)
        schedule(False)

        @pl.when((t == N_DEV - 1) & (i == nm - 1))
        def _():
            copy(0, me, sibling, src=ws_ref).wait_send()
            half(7, y_id, True, x_nbr).wait_send()
            half(8, x_id, False, y_nbr).wait_send()
            for j, (_, dev_id) in enumerate(chips):
                copy(4 + j, dev_id, sibling).wait_send()
            own.wait()
            for cp in comm.copies(c_ins, c_outs, c_send, c_recv):
                cp.wait()

    hbm = pl.BlockSpec(memory_space=pl.ANY)
    res = pl.pallas_call(
        body,
        grid_spec=pltpu.PrefetchScalarGridSpec(
            num_scalar_prefetch=1, grid=(N_DEV, nm),
            in_specs=[pl.BlockSpec((tm, K), lambda t, i, order_ref: (i, 0)), hbm] + [hbm] * n_ci,
            out_specs=[pl.BlockSpec((tm, C), lambda t, i, order_ref: (i, order_ref[t])), hbm] + [hbm] * n_co,
            scratch_shapes=[pltpu.VMEM((2, K, C), bf16), pltpu.SemaphoreType.DMA((9,)), pltpu.SemaphoreType.DMA((9,)),
                            pltpu.SemaphoreType.DMA, pltpu.SemaphoreType.DMA((2,))] + comm.scratch()),
        out_shape=[jax.ShapeDtypeStruct((T, N_DEV * C), f32), jax.ShapeDtypeStruct((K, N_DEV * C), bf16)] + comm.out_shapes,
        compiler_params=_params(("arbitrary", "arbitrary")), name="proj_gather",
    )(order, h, w_shard, *comm.ins)
    return res[0], res[1], list(res[2:])


N_CHIP = 4


def _shard_shape(g, ax):
    return tuple(d // N_DEV if i == ax else d for i, d in enumerate(g.shape))


def _sibling_comm(grads, axes):
    n = len(grads)
    sizes = [g.shape[ax] // N_DEV for g, ax in zip(grads, axes)]

    def copies(ins, lands, send_sems, recv_sems, base=0):
        sibling, _ = _peer(1)
        out = []
        for j in range(N_CHIP):
            _, owner = _peer(2 * j + 1)
            for a in range(n):
                out.append(pltpu.make_async_remote_copy(
                    src_ref=_window(ins[a], axes[a], sizes[a], owner), dst_ref=lands[a].at[j],
                    send_sem=send_sems.at[base + a, j], recv_sem=recv_sems.at[base + a, j], device_id=sibling,
                    device_id_type=MESH))
        return out

    shapes = [jax.ShapeDtypeStruct((N_CHIP,) + _shard_shape(g, ax), g.dtype) for g, ax in zip(grads, axes)]
    return _Comm(grads, shapes, (n, N_CHIP), copies)


def _chips_comm(sums):
    n = len(sums)

    def copies(ins, lands, send_sems, recv_sems, base=0):
        out = []
        for j in range(1, N_CHIP):
            dev, _ = _peer(2 * j)
            for a in range(n):
                out.append(pltpu.make_async_remote_copy(
                    src_ref=ins[a].at[j - 1], dst_ref=lands[a].at[j - 1],
                    send_sem=send_sems.at[base + a, j - 1], recv_sem=recv_sems.at[base + a, j - 1], device_id=dev,
                    device_id_type=MESH))
        return out

    return _Comm(sums, [jax.ShapeDtypeStruct(s.shape, s.dtype) for s in sums], (n, N_CHIP), copies)


def _join(c1, c2):
    n1, m1, k1 = len(c1.ins), len(c1.out_shapes), c1.sem_shape[0]

    def copies(ins, lands, send_sems, recv_sems):
        return (c1.copies(ins[:n1], lands[:m1], send_sems, recv_sems, base=0)
                + c2.copies(ins[n1:], lands[m1:], send_sems, recv_sems, base=k1))

    return _Comm(c1.ins + c2.ins, c1.out_shapes + c2.out_shapes, (k1 + c2.sem_shape[0], N_CHIP), copies)


def _rs_chip_sum(grad, land, xyc, axis, name):
    R, C = land.shape[1:]
    tr = _pick(R, max(8, (640 * 1024) // C), 8)

    def body(xyc_ref, g_ref, l_ref, o_ref):
        o_ref[0] = (g_ref[...] + l_ref[0]).astype(bf16)

    def owner(j, xyc_ref):
        jj = j + 1
        return 4 * (xyc_ref[0] ^ (jj >> 1)) + 2 * (xyc_ref[1] ^ (jj & 1)) + xyc_ref[2]

    if axis == 0:
        g_spec = pl.BlockSpec((tr, C), lambda j, i, xyc_ref: (owner(j, xyc_ref) * (R // tr) + i, 0))
    else:
        g_spec = pl.BlockSpec((tr, C), lambda j, i, xyc_ref: (i, owner(j, xyc_ref)))
    return pl.pallas_call(
        body,
        grid_spec=pltpu.PrefetchScalarGridSpec(
            num_scalar_prefetch=1, grid=(N_CHIP - 1, R // tr),
            in_specs=[g_spec, pl.BlockSpec((1, tr, C), lambda j, i, xyc_ref: (j + 1, i, 0))],
            out_specs=pl.BlockSpec((1, tr, C), lambda j, i, xyc_ref: (j, i, 0))),
        out_shape=jax.ShapeDtypeStruct((N_CHIP - 1, R, C), bf16),
        compiler_params=_params(("parallel", "parallel")), name=name,
    )(xyc, grad, land)


def _allreduce_small(part):
    R = part.shape[0]

    def body(p_ref, o_ref, buf, send_sems, recv_sems):
        (x, y, c), me = _me()
        buf[me] = p_ref[...]
        copies = []
        for k in range(1, N_DEV):
            dev, dev_id = _peer(k)
            copies.append(pltpu.make_async_remote_copy(
                src_ref=p_ref, dst_ref=buf.at[me], send_sem=send_sems.at[k - 1], recv_sem=recv_sems.at[k - 1],
                device_id=dev, device_id_type=MESH))
        for cp in copies:
            cp.start()
        for k in range(1, N_DEV):
            _, dev_id = _peer(k)
            pltpu.make_async_remote_copy(
                src_ref=p_ref, dst_ref=buf.at[dev_id], send_sem=send_sems.at[k - 1], recv_sem=recv_sems.at[k - 1],
                device_id=(x, y, c), device_id_type=MESH).wait_recv()
        for cp in copies:
            cp.wait_send()
        acc = buf[0]
        for d in range(1, N_DEV):
            acc = acc + buf[d]
        o_ref[...] = acc

    return pl.pallas_call(
        body, in_specs=[pl.BlockSpec(memory_space=pltpu.VMEM)], out_specs=pl.BlockSpec(memory_space=pltpu.VMEM),
        out_shape=jax.ShapeDtypeStruct((R, LANES), f32),
        scratch_shapes=[pltpu.VMEM((N_DEV, R, LANES), f32), pltpu.SemaphoreType.DMA((N_DEV - 1,)), pltpu.SemaphoreType.DMA((N_DEV - 1,))],
        name="allreduce_small",
    )(part)


def _adamw_math(w, g, m, v):
    m2 = ADAM_B1 * m + (1.0 - ADAM_B1) * g
    v2 = ADAM_B2 * v + (1.0 - ADAM_B2) * (g * g)
    m_hat = m2 / (1.0 - ADAM_B1 ** ADAM_STEP)
    v_hat = v2 / (1.0 - ADAM_B2 ** ADAM_STEP)
    return -ADAM_LR * (m_hat / (jnp.sqrt(v_hat) + ADAM_EPS) + ADAM_WD * w), m2, v2


def _adamw_shard(grad, land_sib, land_chips, w, m, v, me, axis, name, row0=0, prev=None):
    R, C = land_sib.shape[1:]
    assert axis == 1 or R == w.shape[0]
    tr = _pick(R, max(8, (320 * 1024) // C), 8)
    r0 = row0 // tr
    n_prev = len(prev) if prev else 0

    def body(me_ref, g_ref, s_ref, l_ref, w_ref, m_ref, v_ref, *rest):
        go_ref, d_ref, mo_ref, vo_ref = rest[n_prev:]
        g = g_ref[...] + s_ref[0]
        for j in range(N_CHIP - 1):
            g = g + l_ref[j].astype(f32)
        d, m2, v2 = _adamw_math(w_ref[...], g, m_ref[...], v_ref[...])
        go_ref[...] = g
        d_ref[...] = d
        mo_ref[...] = m2
        vo_ref[...] = v2

    if axis == 0:
        g_spec = pl.BlockSpec((tr, C), lambda i, me_ref: (me_ref[0] * (R // tr) + i, 0))
    else:
        g_spec = pl.BlockSpec((tr, C), lambda i, me_ref: (i, me_ref[0]))
    blk = pl.BlockSpec((tr, C), lambda i, me_ref: (r0 + i, 0))
    return pl.pallas_call(
        body,
        grid_spec=pltpu.PrefetchScalarGridSpec(
            num_scalar_prefetch=1, grid=(R // tr,),
            in_specs=[g_spec, pl.BlockSpec((1, tr, C), lambda i, me_ref: (0, i, 0)),
                      pl.BlockSpec((N_CHIP - 1, tr, C), lambda i, me_ref: (0, i, 0)), blk, blk, blk]
            + [pl.BlockSpec(memory_space=pl.ANY)] * n_prev,
            out_specs=[blk] * 4),
        out_shape=[jax.ShapeDtypeStruct(w.shape, f32)] * 4,
        input_output_aliases={7 + i: i for i in range(n_prev)},
        compiler_params=_params(("parallel",)), name=name,
    )(me, grad, land_sib, land_chips, w, m, v, *(prev or []))


def _adamw_small(g, w, m, v):
    def body(g_ref, w_ref, m_ref, v_ref, d_ref, mo_ref, vo_ref):
        d, m2, v2 = _adamw_math(w_ref[...], g_ref[...], m_ref[...], v_ref[...])
        d_ref[...] = d
        mo_ref[...] = m2
        vo_ref[...] = v2

    return pl.pallas_call(body, out_shape=[jax.ShapeDtypeStruct(g.shape, f32)] * 3, name="adamw_small")(g, w, m, v)


def _pack_rows(parts, total_rows):
    rows = jnp.concatenate([p.reshape(-1, LANES) for p in parts], axis=0)
    return jnp.pad(rows, ((0, total_rows - rows.shape[0]), (0, 0)))


BIG = ("w_in", "mem_w_kv", "w_br_attn", "w_br_conv", "w_br_mem", "w_out")
BIG_AXIS = {"w_in": 1, "mem_w_kv": 0, "w_br_attn": 1, "w_br_conv": 1, "w_br_mem": 1, "w_out": 0}
SMALL = ("norm_g", "mem_norm_g", "attn_q_norm", "attn_k_norm", "mem_q_norm", "mem_k_norm")
ALL_W = ("norm_g", "mem_norm_g", "w_in", "attn_q_norm", "attn_k_norm", "conv_w", "mem_w_kv", "mem_q_norm", "mem_k_norm",
         "w_br_attn", "w_br_conv", "w_br_mem", "w_out")


def kernel(x, mem, norm_g, mem_norm_g, w_in, attn_q_norm, attn_k_norm, conv_w, mem_w_kv, mem_q_norm, mem_k_norm, w_br_attn, w_br_conv, w_br_mem, w_out, loss_target, m_norm_g, m_mem_norm_g, m_w_in, m_attn_q_norm, m_attn_k_norm, m_conv_w, m_mem_w_kv, m_mem_q_norm, m_mem_k_norm, m_w_br_attn, m_w_br_conv, m_w_br_mem, m_w_out, v_norm_g, v_mem_norm_g, v_w_in, v_attn_q_norm, v_attn_k_norm, v_conv_w, v_mem_w_kv, v_mem_q_norm, v_mem_k_norm, v_w_br_attn, v_w_br_conv, v_w_br_mem, v_w_out):
    w = dict(norm_g=norm_g, mem_norm_g=mem_norm_g, w_in=w_in, attn_q_norm=attn_q_norm, attn_k_norm=attn_k_norm, conv_w=conv_w,
             mem_w_kv=mem_w_kv, mem_q_norm=mem_q_norm, mem_k_norm=mem_k_norm, w_br_attn=w_br_attn, w_br_conv=w_br_conv,
             w_br_mem=w_br_mem, w_out=w_out)
    mo = dict(norm_g=m_norm_g, mem_norm_g=m_mem_norm_g, w_in=m_w_in, attn_q_norm=m_attn_q_norm, attn_k_norm=m_attn_k_norm,
              conv_w=m_conv_w, mem_w_kv=m_mem_w_kv, mem_q_norm=m_mem_q_norm, mem_k_norm=m_mem_k_norm, w_br_attn=m_w_br_attn,
              w_br_conv=m_w_br_conv, w_br_mem=m_w_br_mem, w_out=m_w_out)
    vo = dict(norm_g=v_norm_g, mem_norm_g=v_mem_norm_g, w_in=v_w_in, attn_q_norm=v_attn_q_norm, attn_k_norm=v_attn_k_norm,
              conv_w=v_conv_w, mem_w_kv=v_mem_w_kv, mem_q_norm=v_mem_q_norm, mem_k_norm=v_mem_k_norm, w_br_attn=v_w_br_attn,
              w_br_conv=v_w_br_conv, w_br_mem=v_w_br_mem, w_out=v_w_out)
    B, S, D = x.shape
    me = (4 * lax.axis_index("x") + 2 * lax.axis_index("y") + lax.axis_index("c")).astype(jnp.int32)

    T = B * S
    x2, t2, mem2, ng = x.reshape(T, D), loss_target.reshape(T, D), mem.reshape(B * MEM_LEN, D), norm_g.reshape(1, D)
    h = _rms_fwd(x2, ng, "rms_x")
    rows_sharded, cols_sharded = ("mem_w_kv", "w_out"), ("w_br_attn", "w_br_conv", "w_br_mem")
    later = rows_sharded + cols_sharded
    later_axes = [BIG_AXIS[n] for n in later] + [None]
    conv_pad = jnp.pad(conv_w, ((0, 8 - conv_w.shape[0]), (0, 0)))
    proj, w_in_full, spread_a = _proj_gather(
        h, w_in.astype(bf16), _shard_order(),
        _spread_comm([w[n].astype(bf16) for n in rows_sharded], [BIG_AXIS[n] for n in rows_sharded]), comm_at=N_DEV - 3)
    *attn, spread_b = _attn_fwd(proj, attn_q_norm, attn_k_norm, B, S,
                                comm=_spread_comm([w[n].astype(bf16) for n in cols_sharded] + [conv_pad],
                                                  [BIG_AXIS[n] for n in cols_sharded] + [None]))
    *fulls, conv_g = _forward_blocks(spread_a + spread_b, later_axes)
    wf = dict(zip(later, fulls), w_in=w_in_full)
    conv_full = conv_g[:, :3, :].transpose(1, 0, 2).reshape(3, CONV_W)

    sq_err, g, t = _fwd_bwd_head(x2, mem2, t2, proj, attn, B, S, mem_norm_g, attn_q_norm, attn_k_norm, conv_full, mem_q_norm,
                                 mem_k_norm, wf["mem_w_kv"], wf["w_br_attn"], wf["w_br_conv"], wf["w_br_mem"], wf["w_out"])
    t.update(x2=x2, ng=ng, h=h)

    xyc = jnp.stack([lax.axis_index("x"), lax.axis_index("y"), lax.axis_index("c")]).astype(jnp.int32)
    me1 = me.reshape(1)
    early = ("w_out", "w_br_attn", "w_br_conv", "w_br_mem")
    g["mem_w_kv"], sib_early = _mm(t["mh"], t["dmkv"], mode="tn", out_dtype=f32, name="dw_kv",
                                   comm=_sibling_comm([g[n] for n in early], [BIG_AXIS[n] for n in early]))
    sums_early = [_rs_chip_sum(g[n], ls, xyc, BIG_AXIS[n], "rs_sum_" + n) for n, ls in zip(early, sib_early)]
    tm_w = _pick(D // 2, 1024)
    half = (D // 2) // tm_w
    g_top, (*chips_early, sib_kv) = _mm(t["h"], t["dproj"], mode="tn", out_dtype=f32, name="dw_in_top", tm=tm_w,
                                        m_tiles=(0, half),
                                        comm=_join(_chips_comm(sums_early), _sibling_comm([g["mem_w_kv"]], [0])))
    sum_kv = _rs_chip_sum(g["mem_w_kv"], sib_kv, xyc, 0, "rs_sum_mem_w_kv")
    g_bot, (chips_kv, sib_top) = _mm(t["h"], t["dproj"], mode="tn", out_dtype=f32, name="dw_in_bot", tm=tm_w,
                                     m_tiles=(half, half), comm=_join(_chips_comm([sum_kv]), _sibling_comm([g_top], [1])))
    sum_top = _rs_chip_sum(g_top, sib_top, xyc, 1, "rs_sum_w_in_top")
    tm_t = _pick(T // 2, 1024)
    halfm = (T // 2) // tm_t
    dh, (chips_top, sib_bot) = _mm(t["dproj"], wf["w_in"], mode="nt", out_dtype=f32, name="d_h_a", tm=tm_t, tk=D_H_TK,
                                   m_tiles=(0, halfm), into="new",
                                   comm=_join(_chips_comm([sum_top]), _sibling_comm([g_bot], [1])))
    sum_bot = _rs_chip_sum(g_bot, sib_bot, xyc, 1, "rs_sum_w_in_bot")
    dh, (chips_bot,) = _mm(t["dproj"], wf["w_in"], mode="nt", out_dtype=f32, name="d_h_b", tm=tm_t, tk=D_H_TK, m_tiles=(halfm, halfm),
                           into=dh, comm=_chips_comm([sum_bot]))
    dx, dng = _rms_bwd(t["x2"], t["ng"], dh, t["dy"], "rms_x_bwd")
    g["norm_g"] = dng.reshape(D)

    grad, delta, new_m, new_v = {}, {}, {}, {}
    for n, ls, lc in zip(early + ("mem_w_kv",), sib_early + [sib_kv], chips_early + [chips_kv]):
        grad[n], delta[n], new_m[n], new_v[n] = _adamw_shard(g[n], ls, lc, w[n], mo[n], vo[n], me1, BIG_AXIS[n], "adamw_" + n)
    top = _adamw_shard(g_top, sib_top, chips_top, w_in, m_w_in, v_w_in, me1, 1, "adamw_w_in_top")
    grad["w_in"], delta["w_in"], new_m["w_in"], new_v["w_in"] = _adamw_shard(
        g_bot, sib_bot, chips_bot, w_in, m_w_in, v_w_in, me1, 1, "adamw_w_in_bot", row0=D // 2, prev=top)

    n_rep = sum(w[n].size for n in SMALL) // LANES
    conv_rows = g["conv_w"].reshape(3, N_DEV, LANES).transpose(1, 0, 2).reshape(3 * N_DEV, LANES)
    n_rows = -(-(n_rep + 3 * N_DEV + 1) // 8) * 8
    part = _pack_rows([g[n] for n in SMALL] + [conv_rows, jnp.full((1, LANES), sq_err, f32)], n_rows)
    tot = _allreduce_small(part)
    loss = tot[n_rep + 3 * N_DEV, 0] * (0.5 / D)
    n_small = -(-(n_rep + 3) // 8) * 8
    g_small = _pack_rows([tot[:n_rep], lax.dynamic_slice(tot, (n_rep + 3 * me, 0), (3, LANES))], n_small)
    names = SMALL + ("conv_w",)
    d_s, m_s, v_s = _adamw_small(g_small, _pack_rows([w[n] for n in names], n_small), _pack_rows([mo[n] for n in names], n_small),
                                 _pack_rows([vo[n] for n in names], n_small))
    off = 0
    for n in names:
        r = w[n].size // LANES
        grad[n], delta[n] = g_small[off:off + r].reshape(w[n].shape), d_s[off:off + r].reshape(w[n].shape)
        new_m[n], new_v[n] = m_s[off:off + r].reshape(w[n].shape), v_s[off:off + r].reshape(w[n].shape)
        off += r
    return (loss, dx.reshape(B, S, D), *[grad[n] for n in ALL_W], *[delta[n] for n in ALL_W], *[new_m[n] for n in ALL_W],
            *[new_v[n] for n in ALL_W])
```

```python
import functools

import jax
import jax.numpy as jnp
from jax import lax
from jax.experimental import pallas as pl
from jax.experimental.pallas import tpu as pltpu

f32 = jnp.float32
bf16 = jnp.bfloat16

N_DEV = 8
HEAD_DIM = 128
DILATIONS = (1, 4, 16)
N_GROUPS = 3
HEADS = 4
BLK = 128
ATTN_QKV = N_GROUPS * HEADS * HEAD_DIM
ATTN_OUT = HEADS * HEAD_DIM
CONV_W = 1024
MEM_LEN = 256
MEM_HEADS = 4
MEM_HD = 256
MEM_W = MEM_HEADS * MEM_HD
EPS = 1e-6
Q0, K0, V0 = 0, ATTN_QKV, 2 * ATTN_QKV
ZA = 3 * ATTN_QKV
CB, CC, CV, ZC = ZA + ATTN_OUT, ZA + ATTN_OUT + CONV_W, ZA + ATTN_OUT + 2 * CONV_W, ZA + ATTN_OUT + 3 * CONV_W
MQ = ZC + CONV_W
ZM = MQ + MEM_W
GT = ZM + MEM_W

ADAM_LR, ADAM_B1, ADAM_B2, ADAM_EPS, ADAM_WD, ADAM_STEP = 0.001, 0.9, 0.999, 1e-08, 0.01, 10

VMEM_LIMIT = 56 * 1024 * 1024
D_H_TK = 4352
LANES = 128

NT_DIMS = (((1,), (1,)), ((), ()))
TN_DIMS = (((0,), (0,)), ((), ()))
NN_DIMS = (((1,), (0,)), ((), ()))


def _params(sem=None):
    return pltpu.CompilerParams(dimension_semantics=sem, vmem_limit_bytes=VMEM_LIMIT)


def _pick(n, pref, q=LANES):
    t = (min(pref, n) // q) * q
    while t >= q:
        if n % t == 0:
            return t
        t -= q
    return n


def _dot(a, b, dims):
    return lax.dot_general(a.astype(bf16), b.astype(bf16), dims, preferred_element_type=f32)


def _sigmoid(z):
    return 0.5 * jnp.tanh(0.5 * z) + 0.5


def _rstd(v):
    return lax.rsqrt(jnp.mean(v * v, axis=-1, keepdims=True) + EPS)


class _Deferred:
    def __init__(self, stage, sems, step, last, n):
        assert last >= 1
        self.stage, self.sems, self.step, self.last, self.n = stage, sems, step, last, n
        self.slot = step % 2

    def _drain(self, slot, like):
        for c in range(self.n):
            pltpu.make_async_copy(self.stage.at[slot, c], like(c), self.sems.at[slot, c]).wait()

    def begin(self, like):
        pl.when(self.step >= 2)(lambda: self._drain(self.slot, like))

    def start(self, c, dst):
        pltpu.make_async_copy(self.stage.at[self.slot, c], dst, self.sems.at[self.slot, c]).start()

    def end(self, like):
        @pl.when(self.step == self.last)
        def _():
            self._drain(self.slot, like)
            self._drain(1 - self.slot, like)


def _row_sum(v):
    hi = v.astype(bf16)
    lo = (v - hi.astype(f32)).astype(bf16)
    ones = jnp.ones((v.shape[1], v.shape[1]), bf16)
    return _dot(hi, ones, NN_DIMS) + _dot(lo, ones, NN_DIMS)


def _rstd_head(v):
    return lax.rsqrt(_row_sum(v * v) * (1.0 / v.shape[1]) + EPS)


class _Comm:
    def __init__(self, ins, out_shapes, sem_shape, copies):
        self.ins, self.out_shapes, self.sem_shape, self.copies = list(ins), list(out_shapes), sem_shape, copies

    def scratch(self):
        return [pltpu.SemaphoreType.DMA(self.sem_shape), pltpu.SemaphoreType.DMA(self.sem_shape)]


def _mm(a, b, *, mode, out_dtype, name, tm=1024, tn=1024, tk=4096, m_tiles=None, into=None, comm=None):
    if mode == "nn":
        (M, K), (K2, N) = a.shape, b.shape
    elif mode == "nt":
        (M, K), (N, K2) = a.shape, b.shape
    else:
        (K, M), (K2, N) = a.shape, b.shape
    assert K == K2
    tm, tn, tk = _pick(M, tm), _pick(N, tn), _pick(K, tk)
    nk = K // tk
    m0, mc = m_tiles if m_tiles is not None else (0, M // tm)
    o0 = 0 if into is None else m0
    aliased = into is not None and not isinstance(into, str)
    n_ci = len(comm.ins) if comm else 0
    n_co = len(comm.out_shapes) if comm else 0
    grid = (mc, N // tn, nk)
    dims = {"nn": NN_DIMS, "nt": NT_DIMS, "tn": TN_DIMS}[mode]

    def body(*refs, nk):
        a_ref, b_ref = refs[:2]
        pos = 3 if aliased else 2
        c_ins, o_ref, c_outs = refs[pos:pos + n_ci], refs[pos + n_ci], refs[pos + n_ci + 1:pos + n_ci + 1 + n_co]
        scratch = refs[pos + n_ci + 1 + n_co:]
        ids = [pl.program_id(d) for d in range(3)]
        if comm:
            sems = scratch[-2:]

            @pl.when((ids[0] == 0) & (ids[1] == 0) & (ids[2] == 0))
            def _():
                for cp in comm.copies(c_ins, c_outs, *sems):
                    cp.start()

        if nk == 1:
            o_ref[...] = _dot(a_ref[...], b_ref[...], dims).astype(o_ref.dtype)
        else:
            acc_ref, k = scratch[0], ids[2]

            @pl.when(k == 0)
            def _():
                acc_ref[...] = _dot(a_ref[...], b_ref[...], dims)

            @pl.when((k > 0) & (k < nk - 1))
            def _():
                acc_ref[...] += _dot(a_ref[...], b_ref[...], dims)

            @pl.when(k == nk - 1)
            def _():
                o_ref[...] = (acc_ref[...] + _dot(a_ref[...], b_ref[...], dims)).astype(o_ref.dtype)

        if comm:
            @pl.when((ids[0] == grid[0] - 1) & (ids[1] == grid[1] - 1) & (ids[2] == grid[2] - 1))
            def _():
                for cp in comm.copies(c_ins, c_outs, *sems):
                    cp.wait()

    if mode == "tn":
        a_spec = pl.BlockSpec((tk, tm), lambda i, j, k: (k, m0 + i))
    else:
        a_spec = pl.BlockSpec((tm, tk), lambda i, j, k: (m0 + i, k))
    if mode == "nt":
        b_spec = pl.BlockSpec((tn, tk), lambda i, j, k: (j, k))
    else:
        b_spec = pl.BlockSpec((tk, tn), lambda i, j, k: (k, j))
    hbm = pl.BlockSpec(memory_space=pl.ANY)
    res = pl.pallas_call(
        functools.partial(body, nk=nk),
        grid=grid,
        in_specs=[a_spec, b_spec] + [hbm] * (aliased + n_ci),
        out_specs=[pl.BlockSpec((tm, tn), lambda i, j, k: (o0 + i, j))] + [hbm] * n_co,
        out_shape=[jax.ShapeDtypeStruct((mc * tm if into is None else M, N), out_dtype)] + (comm.out_shapes if comm else []),
        scratch_shapes=([pltpu.VMEM((tm, tn), f32)] if nk > 1 else []) + (comm.scratch() if comm else []),
        input_output_aliases={2: 0} if aliased else {},
        compiler_params=_params(("arbitrary",) * 3 if comm else ("parallel", "parallel", "arbitrary")),
        name=name,
    )(a, b, *([into] if aliased else []), *(comm.ins if comm else []))
    return (res[0], list(res[1:])) if comm else res[0]


def _rms_fwd(x, g, name):
    T, D = x.shape
    tm = _pick(T, 256, 8)

    def body(x_ref, g_ref, h_ref):
        xv = x_ref[...]
        h_ref[...] = (xv * _rstd(xv) * g_ref[...]).astype(bf16)

    return pl.pallas_call(
        body, grid=(T // tm,),
        in_specs=[pl.BlockSpec((tm, D), lambda i: (i, 0)), pl.BlockSpec((1, D), lambda i: (0, 0))],
        out_specs=pl.BlockSpec((tm, D), lambda i: (i, 0)),
        out_shape=jax.ShapeDtypeStruct((T, D), bf16),
        compiler_params=_params(("parallel",)), name=name,
    )(x, g)


def _rms_bwd(x, g, dh, dy, name):
    T, D = x.shape
    tm = _pick(T, 256, 8)
    with_dx = dy is not None

    def body(*refs):
        if with_dx:
            x_ref, g_ref, dh_ref, dy_ref, dx_ref, dg_ref = refs
        else:
            x_ref, g_ref, dh_ref, dg_ref = refs
        xv = x_ref[...]
        r = _rstd(xv)
        xh = xv * r
        dhv = dh_ref[...]
        part = jnp.sum(dhv * xh, axis=0, keepdims=True)

        @pl.when(pl.program_id(0) == 0)
        def _():
            dg_ref[...] = part

        @pl.when(pl.program_id(0) > 0)
        def _():
            dg_ref[...] += part

        if with_dx:
            dxh = dhv * g_ref[...]
            dx_ref[...] = dy_ref[...] + r * (dxh - xh * jnp.mean(dxh * xh, axis=-1, keepdims=True))

    row = pl.BlockSpec((tm, D), lambda i: (i, 0))
    vec = pl.BlockSpec((1, D), lambda i: (0, 0))
    if with_dx:
        return pl.pallas_call(
            body, grid=(T // tm,), in_specs=[row, vec, row, row], out_specs=[row, vec],
            out_shape=[jax.ShapeDtypeStruct((T, D), f32), jax.ShapeDtypeStruct((1, D), f32)],
            compiler_params=_params(("arbitrary",)), name=name,
        )(x, g, dh, dy)
    return pl.pallas_call(
        body, grid=(T // tm,), in_specs=[row, vec, row], out_specs=vec,
        out_shape=jax.ShapeDtypeStruct((1, D), f32),
        compiler_params=_params(("arbitrary",)), name=name,
    )(x, g, dh)


UNIT_UNROLL = 4


def _rows(start, size, stride):
    return pl.ds(start, size) if stride == 1 else pl.ds(start, size, stride=stride)


def _attn_units(S, d, first, prev):
    nb = S // d // BLK

    def do_first(r, c):
        first(_rows(r, BLK, d))
        return c

    lax.fori_loop(0, d, do_first, 0, unroll=min(d, UNIT_UNROLL))
    if nb > 1:
        def do_prev(u, c):
            r = u // (nb - 1)
            b = u % (nb - 1) + 1
            base = r + d * b * BLK
            prev(_rows(base, BLK, d), _rows(base - d * BLK, 2 * BLK, d))
            return c

        lax.fori_loop(0, d * (nb - 1), do_prev, 0, unroll=UNIT_UNROLL)


def _band_bias(nkeys):
    i = lax.broadcasted_iota(jnp.int32, (BLK, nkeys), 0)
    j = lax.broadcasted_iota(jnp.int32, (BLK, nkeys), 1)
    ok = (j <= i) if nkeys == BLK else ((j >= i) & (j <= i + BLK))
    return jnp.where(ok, 0.0, -jnp.inf).astype(f32)


def _normalize_into(src_ref, gain, dst_ref, S):
    for c in range(S // 256):
        rows = pl.ds(c * 256, 256)
        v = src_ref[rows, :]
        dst_ref[rows, :] = v * _rstd_head(v) * gain


def _attn_fwd(proj, gq, gk, B, S, comm=None):
    T, NC = proj.shape
    scale = HEAD_DIM ** -0.5
    n_ci = len(comm.ins) if comm else 0
    n_co = len(comm.out_shapes) if comm else 0

    def body(*refs):
        q_ref, k_ref, v_ref, z_ref, gq_ref, gk_ref = refs[:6]
        c_ins = refs[6:6 + n_ci]
        ag_ref, a_ref, lse_ref = refs[6 + n_ci:9 + n_ci]
        c_outs = refs[9 + n_ci:9 + n_ci + n_co]
        qs, ks, o0, o1, o2, l0, l1, l2, bias1, bias2 = refs[9 + n_ci + n_co:19 + n_ci + n_co]
        sems = refs[19 + n_ci + n_co:]
        g = pl.program_id(2)
        if comm:
            @pl.when((pl.program_id(0) == 0) & (pl.program_id(1) == 0) & (g == 0))
            def _():
                for cp in comm.copies(c_ins, c_outs, *sems):
                    cp.start()

        bias1[...] = _band_bias(BLK)
        bias2[...] = _band_bias(2 * BLK)
        _normalize_into(q_ref, gq_ref[pl.ds(g, 1), :], qs, S)
        _normalize_into(k_ref, gk_ref[pl.ds(g, 1), :], ks, S)
        o_s, l_s = (o0, o1, o2), (l0, l1, l2)

        def run(gi):
            def unit(qr, kr, nkeys):
                s = _dot(qs[qr, :], ks[kr, :], NT_DIMS) * scale + (bias1 if nkeys == BLK else bias2)[...]
                m = jnp.max(s, axis=-1, keepdims=True)
                p = jnp.exp(s - m)
                den = jnp.sum(p, axis=-1, keepdims=True)
                o_s[gi][qr, :] = _dot(p, v_ref[kr, :], NN_DIMS) * (1.0 / den)
                l_s[gi][qr, :] = jnp.broadcast_to(m + jnp.log(den), (BLK, HEAD_DIM))

            _attn_units(S, DILATIONS[gi], lambda qr: unit(qr, qr, BLK), lambda qr, kr: unit(qr, kr, 2 * BLK))

        for gi in range(N_GROUPS):
            pl.when(g == gi)(functools.partial(run, gi))

        @pl.when(g == N_GROUPS - 1)
        def _():
            for c in range(S // 256):
                rows = pl.ds(c * 256, 256)
                la, lb, lc = l0[rows, :], l1[rows, :], l2[rows, :]
                m = jnp.maximum(jnp.maximum(la, lb), lc)
                wa, wb, wc = jnp.exp(la - m), jnp.exp(lb - m), jnp.exp(lc - m)
                tot = wa + wb + wc
                a = (wa / tot) * o0[rows, :] + (wb / tot) * o1[rows, :] + (wc / tot) * o2[rows, :]
                z = z_ref[rows, :]
                a_ref[rows, :] = a
                lse_ref[rows, :] = m + jnp.log(tot)
                ag_ref[rows, :] = (a * (z * _sigmoid(z))).astype(bf16)

        if comm:
            @pl.when((pl.program_id(0) == B - 1) & (pl.program_id(1) == HEADS - 1) & (g == N_GROUPS - 1))
            def _():
                for cp in comm.copies(c_ins, c_outs, *sems):
                    cp.wait()

    def slab(col0):
        return pl.BlockSpec((S, HEAD_DIM), lambda b, h, g: (b, col0 // HEAD_DIM + g * HEADS + h))

    gain = pl.BlockSpec((N_GROUPS, HEAD_DIM), lambda b, h, g: (0, 0))
    out = pl.BlockSpec((S, HEAD_DIM), lambda b, h, g: (b, h))
    hbm = pl.BlockSpec(memory_space=pl.ANY)
    res = pl.pallas_call(
        body, grid=(B, HEADS, N_GROUPS),
        in_specs=[slab(Q0), slab(K0), slab(V0), pl.BlockSpec((S, HEAD_DIM), lambda b, h, g: (b, ZA // HEAD_DIM + h)), gain, gain]
        + [hbm] * n_ci,
        out_specs=[out, out, out] + [hbm] * n_co,
        out_shape=[jax.ShapeDtypeStruct((T, ATTN_OUT), bf16), jax.ShapeDtypeStruct((T, ATTN_OUT), f32),
                   jax.ShapeDtypeStruct((T, ATTN_OUT), f32)] + (comm.out_shapes if comm else []),
        scratch_shapes=[pltpu.VMEM((S, HEAD_DIM), f32)] * 8 + [pltpu.VMEM((BLK, BLK), f32), pltpu.VMEM((BLK, 2 * BLK), f32)]
        + (comm.scratch() if comm else []),
        compiler_params=_params(("arbitrary", "arbitrary", "arbitrary")), name="attn_fwd",
    )(proj, proj, proj, proj, gq, gk, *(comm.ins if comm else []))
    return res[0], res[1], res[2], list(res[3:])


def _rms_bwd_rows(raw, gain, dn, on_mxu=True):
    r = _rstd_head(raw) if on_mxu else _rstd(raw)
    xh = raw * r
    dxh = dn * gain
    if on_mxu:
        mean = _row_sum(dxh * xh) * (1.0 / raw.shape[1])
    else:
        mean = jnp.mean(dxh * xh, axis=-1, keepdims=True)
    return r * (dxh - xh * mean), jnp.sum(dn * xh, axis=0, keepdims=True)


def _attn_bwd(proj, gq, gk, a_comb, lse, dag, dproj, B, S):
    T, NC = proj.shape
    scale = HEAD_DIM ** -0.5

    def body(q_ref, k_ref, v_ref, z_ref, gq_ref, gk_ref, a_ref, lse_ref, dag_ref, dproj_in, dproj_ref, dgq_ref, dgk_ref,
             qs, ks, da_s, dl_s, dq_s, dk_s, dv_s, bias1, bias2, stage, dz_stage, sem, dz_sem):
        b, h, g = pl.program_id(0), pl.program_id(1), pl.program_id(2)
        bias1[...] = _band_bias(BLK)
        bias2[...] = _band_bias(2 * BLK)
        gq_row, gk_row = gq_ref[pl.ds(g, 1), :], gk_ref[pl.ds(g, 1), :]
        _normalize_into(q_ref, gq_row, qs, S)
        _normalize_into(k_ref, gk_row, ks, S)

        def dst(c):
            return dproj_ref.at[pl.ds(b * S, S), pl.ds((Q0, K0, V0)[c] + (g * HEADS + h) * HEAD_DIM, HEAD_DIM)]

        out = _Deferred(stage, sem, (b * HEADS + h) * N_GROUPS + g, B * HEADS * N_GROUPS - 1, 3)
        out.begin(dst)

        @pl.when((b == 0) & (h == 0) & (g == 0))
        def _():
            dgq_ref[...] = jnp.zeros_like(dgq_ref)
            dgk_ref[...] = jnp.zeros_like(dgk_ref)

        @pl.when(g == 0)
        def _():
            for c in range(S // 256):
                rows = pl.ds(c * 256, 256)
                z, a, dg_ = z_ref[rows, :], a_ref[rows, :], dag_ref[rows, :]
                sg = _sigmoid(z)
                da = dg_ * (z * sg)
                da_s[rows, :] = da
                dl_s[rows, :] = _row_sum(da * a)
                dz_stage[rows, :] = (dg_ * a * (sg * (1.0 + z * (1.0 - sg)))).astype(bf16)
            cp = pltpu.make_async_copy(dz_stage, dproj_ref.at[pl.ds(b * S, S), pl.ds(ZA + h * HEAD_DIM, HEAD_DIM)], dz_sem)
            cp.start()
            cp.wait()

        dk_s[...] = jnp.zeros_like(dk_s)
        dv_s[...] = jnp.zeros_like(dv_s)

        def run(gi):
            def unit(qr, kr, nkeys):
                q, k, v = qs[qr, :], ks[kr, :], v_ref[kr, :]
                s = _dot(q, k, NT_DIMS) * scale
                p = jnp.exp(s + (bias1 if nkeys == BLK else bias2)[...] - lse_ref[qr, :][:, :1])
                da = da_s[qr, :]
                dp = _dot(da, v, NT_DIMS)
                ds = p * (dp - dl_s[qr, :][:, :1]) * scale
                dq_s[qr, :] = _dot(ds, k, NN_DIMS)
                dk_s[kr, :] += _dot(ds, q, TN_DIMS)
                dv_s[kr, :] += _dot(p, da, TN_DIMS)

            _attn_units(S, DILATIONS[gi], lambda qr: unit(qr, qr, BLK), lambda qr, kr: unit(qr, kr, 2 * BLK))

        for gi in range(N_GROUPS):
            pl.when(g == gi)(functools.partial(run, gi))

        gq_acc = jnp.zeros((1, HEAD_DIM), f32)
        gk_acc = jnp.zeros((1, HEAD_DIM), f32)
        for c in range(S // 256):
            rows = pl.ds(c * 256, 256)
            dq, gq_p = _rms_bwd_rows(q_ref[rows, :], gq_row, dq_s[rows, :])
            dk, gk_p = _rms_bwd_rows(k_ref[rows, :], gk_row, dk_s[rows, :])
            gq_acc, gk_acc = gq_acc + gq_p, gk_acc + gk_p
            stage[out.slot, 0, rows, :] = dq.astype(bf16)
            stage[out.slot, 1, rows, :] = dk.astype(bf16)
            stage[out.slot, 2, rows, :] = dv_s[rows, :].astype(bf16)
        dgq_ref[pl.ds(g, 1), :] += gq_acc
        dgk_ref[pl.ds(g, 1), :] += gk_acc
        for c in range(3):
            out.start(c, dst(c))
        out.end(dst)

    def slab(col0):
        return pl.BlockSpec((S, HEAD_DIM), lambda b, h, g: (b, col0 // HEAD_DIM + g * HEADS + h))

    gain = pl.BlockSpec((N_GROUPS, HEAD_DIM), lambda b, h, g: (0, 0))
    per_slot = pl.BlockSpec((S, HEAD_DIM), lambda b, h, g: (b, h))
    return pl.pallas_call(
        body, grid=(B, HEADS, N_GROUPS),
        in_specs=[slab(Q0), slab(K0), slab(V0), pl.BlockSpec((S, HEAD_DIM), lambda b, h, g: (b, ZA // HEAD_DIM + h)), gain, gain,
                  per_slot, per_slot, per_slot, pl.BlockSpec(memory_space=pl.ANY)],
        out_specs=[pl.BlockSpec(memory_space=pl.ANY), gain, gain],
        out_shape=[jax.ShapeDtypeStruct(dproj.shape, dproj.dtype), jax.ShapeDtypeStruct((N_GROUPS, HEAD_DIM), f32),
                   jax.ShapeDtypeStruct((N_GROUPS, HEAD_DIM), f32)],
        scratch_shapes=[pltpu.VMEM((S, HEAD_DIM), f32)] * 7 + [pltpu.VMEM((BLK, BLK), f32), pltpu.VMEM((BLK, 2 * BLK), f32),
                                                                pltpu.VMEM((2, 3, S, HEAD_DIM), bf16), pltpu.VMEM((S, HEAD_DIM), bf16),
                                                                pltpu.SemaphoreType.DMA((2, 3)), pltpu.SemaphoreType.DMA],
        input_output_aliases={9: 0},
        compiler_params=_params(("arbitrary", "arbitrary", "arbitrary")), name="attn_bwd",
    )(proj, proj, proj, proj, gq, gk, a_comb, lse, dag, dproj)


def _conv_fwd(proj, conv_w, B, S):
    T, NC = proj.shape
    tc = LANES

    def body(cb_ref, cc_ref, cv_ref, z_ref, w_ref, c_ref, ub):
        u = cc_ref[...] * cv_ref[...]
        ub[0:8, :] = jnp.zeros((8, tc), f32)
        ub[8:8 + S, :] = u
        w = w_ref[...]
        y = w[0:1] * u + w[1:2] * ub[pl.ds(7, S), :] + w[2:3] * ub[pl.ds(6, S), :]
        z = z_ref[...]
        c_ref[...] = (cb_ref[...] * y * (z * _sigmoid(z))).astype(bf16)

    def seg(col0):
        return pl.BlockSpec((S, tc), lambda j, b: (b, col0 // tc + j))

    return pl.pallas_call(
        body, grid=(CONV_W // tc, B),
        in_specs=[seg(CB), seg(CC), seg(CV), seg(ZC), pl.BlockSpec((3, tc), lambda j, b: (0, j))],
        out_specs=pl.BlockSpec((S, tc), lambda j, b: (b, j)),
        out_shape=jax.ShapeDtypeStruct((T, CONV_W), bf16),
        scratch_shapes=[pltpu.VMEM((S + 8, tc), f32)],
        compiler_params=_params(("parallel", "parallel")), name="conv_fwd",
    )(proj, proj, proj, proj, conv_w)


def _conv_bwd(proj, conv_w, dc, dproj, B, S):
    T, NC = proj.shape
    tc = LANES

    def body(cb_ref, cc_ref, cv_ref, z_ref, w_ref, dc_ref, dproj_in, dproj_ref, dw_ref, ub, db, stage, sem):
        j, b = pl.program_id(0), pl.program_id(1)

        def dst(c):
            return dproj_ref.at[pl.ds(b * S, S), pl.ds((CB, CC, CV, ZC)[c] + j * tc, tc)]

        out = _Deferred(stage, sem, j * B + b, (CONV_W // tc) * B - 1, 4)
        out.begin(dst)
        cb, cc, cv, z, dcv = cb_ref[...], cc_ref[...], cv_ref[...], z_ref[...], dc_ref[...]
        u = cc * cv
        ub[0:8, :] = jnp.zeros((8, tc), f32)
        ub[8:8 + S, :] = u
        u1, u2 = ub[pl.ds(7, S), :], ub[pl.ds(6, S), :]
        w = w_ref[...]
        y = w[0:1] * u + w[1:2] * u1 + w[2:3] * u2
        sg = _sigmoid(z)
        si = z * sg
        dyv = dcv * cb * si
        db[0:S, :] = dyv
        db[S:S + 8, :] = jnp.zeros((8, tc), f32)
        du = w[0:1] * dyv + w[1:2] * db[pl.ds(1, S), :] + w[2:3] * db[pl.ds(2, S), :]
        stage[out.slot, 0] = (dcv * y * si).astype(bf16)
        stage[out.slot, 1] = (du * cv).astype(bf16)
        stage[out.slot, 2] = (du * cc).astype(bf16)
        stage[out.slot, 3] = (dcv * cb * y * (sg * (1.0 + z * (1.0 - sg)))).astype(bf16)
        for c in range(4):
            out.start(c, dst(c))
        part = jnp.concatenate([jnp.sum(dyv * u, axis=0, keepdims=True), jnp.sum(dyv * u1, axis=0, keepdims=True),
                                jnp.sum(dyv * u2, axis=0, keepdims=True)], axis=0)

        @pl.when(b == 0)
        def _():
            dw_ref[...] = part

        @pl.when(b > 0)
        def _():
            dw_ref[...] += part

        out.end(dst)

    def seg(col0):
        return pl.BlockSpec((S, tc), lambda j, b: (b, col0 // tc + j))

    return pl.pallas_call(
        body, grid=(CONV_W // tc, B),
        in_specs=[seg(CB), seg(CC), seg(CV), seg(ZC), pl.BlockSpec((3, tc), lambda j, b: (0, j)),
                  pl.BlockSpec((S, tc), lambda j, b: (b, j)), pl.BlockSpec(memory_space=pl.ANY)],
        out_specs=[pl.BlockSpec(memory_space=pl.ANY), pl.BlockSpec((3, tc), lambda j, b: (0, j))],
        out_shape=[jax.ShapeDtypeStruct(dproj.shape, dproj.dtype), jax.ShapeDtypeStruct((3, CONV_W), f32)],
        scratch_shapes=[pltpu.VMEM((S + 8, tc), f32), pltpu.VMEM((S + 8, tc), f32), pltpu.VMEM((2, 4, S, tc), bf16),
                        pltpu.SemaphoreType.DMA((2, 4))],
        input_output_aliases={6: 0},
        compiler_params=_params(("arbitrary", "arbitrary")), name="conv_bwd",
    )(proj, proj, proj, proj, conv_w, dc, dproj)


MEM_CHUNK = 1024


def _mem_fwd(proj, mkv, gq, gk, B, S):
    T, NC = proj.shape
    scale = MEM_HD ** -0.5

    def body(q_ref, z_ref, k_ref, v_ref, gq_ref, gk_ref, o_ref):
        kv = k_ref[...]
        kn = (kv * _rstd_head(kv) * gk_ref[...]).astype(bf16)
        vv = v_ref[...].astype(bf16)

        def chunk(c, carry):
            rows = pl.ds(pl.multiple_of(c * MEM_CHUNK, MEM_CHUNK), MEM_CHUNK)
            q = q_ref[rows, :]
            qn = q * _rstd(q) * gq_ref[...]
            s = _dot(qn, kn, NT_DIMS) * scale
            p = jnp.exp(s - jnp.max(s, axis=-1, keepdims=True))
            p = p / jnp.sum(p, axis=-1, keepdims=True)
            z = z_ref[rows, :]
            o_ref[rows, :] = (_dot(p, vv, NN_DIMS) * (z * _sigmoid(z))).astype(bf16)
            return carry

        lax.fori_loop(0, S // MEM_CHUNK, chunk, 0)

    gain = pl.BlockSpec((1, MEM_HD), lambda b, h: (0, 0))
    return pl.pallas_call(
        body, grid=(B, MEM_HEADS),
        in_specs=[pl.BlockSpec((S, MEM_HD), lambda b, h: (b, MQ // MEM_HD + h)),
                  pl.BlockSpec((S, MEM_HD), lambda b, h: (b, ZM // MEM_HD + h)),
                  pl.BlockSpec((MEM_LEN, MEM_HD), lambda b, h: (b, h)),
                  pl.BlockSpec((MEM_LEN, MEM_HD), lambda b, h: (b, MEM_HEADS + h)), gain, gain],
        out_specs=pl.BlockSpec((S, MEM_HD), lambda b, h: (b, h)),
        out_shape=jax.ShapeDtypeStruct((T, MEM_W), bf16),
        compiler_params=_params(("parallel", "parallel")), name="mem_fwd",
    )(proj, proj, mkv, mkv, gq, gk)


def _mem_bwd(proj, mkv, gq, gk, dmo, dproj, B, S):
    T, NC = proj.shape
    scale = MEM_HD ** -0.5

    def body(q_ref, z_ref, k_ref, v_ref, gq_ref, gk_ref, dmo_ref, dproj_in, dproj_ref, dmk_ref, dmv_ref, dgq_ref, dgk_ref,
             dkn_s, dv_s, gq_s, stage, sem):
        b, h = pl.program_id(0), pl.program_id(1)

        def dst(c):
            return dproj_ref.at[pl.ds(b * S, S), pl.ds((MQ, ZM)[c] + h * MEM_HD, MEM_HD)]

        out = _Deferred(stage, sem, b * MEM_HEADS + h, B * MEM_HEADS - 1, 2)
        out.begin(dst)
        kv = k_ref[...]
        kn = (kv * _rstd_head(kv) * gk_ref[...]).astype(bf16)
        vv = v_ref[...].astype(bf16)
        dkn_s[...] = jnp.zeros_like(dkn_s)
        dv_s[...] = jnp.zeros_like(dv_s)
        gq_s[...] = jnp.zeros_like(gq_s)

        def chunk(c, carry):
            rows = pl.ds(pl.multiple_of(c * MEM_CHUNK, MEM_CHUNK), MEM_CHUNK)
            q = q_ref[rows, :]
            qn = q * _rstd(q) * gq_ref[...]
            s = _dot(qn, kn, NT_DIMS) * scale
            p = jnp.exp(s - jnp.max(s, axis=-1, keepdims=True))
            p = p / jnp.sum(p, axis=-1, keepdims=True)
            mo = _dot(p, vv, NN_DIMS)
            z, dg_ = z_ref[rows, :], dmo_ref[rows, :]
            sg = _sigmoid(z)
            do = dg_ * (z * sg)
            stage[out.slot, 1, rows, :] = (dg_ * mo * (sg * (1.0 + z * (1.0 - sg)))).astype(bf16)
            dp = _dot(do, vv, NT_DIMS)
            ds = p * (dp - jnp.sum(do * mo, axis=-1, keepdims=True)) * scale
            dq, gq_p = _rms_bwd_rows(q, gq_ref[...], _dot(ds, kn, NN_DIMS), on_mxu=False)
            stage[out.slot, 0, rows, :] = dq.astype(bf16)
            gq_s[...] += gq_p
            dkn_s[...] += _dot(ds, qn, TN_DIMS)
            dv_s[...] += _dot(p, do, TN_DIMS)
            return carry

        lax.fori_loop(0, S // MEM_CHUNK, chunk, 0)
        for c in range(2):
            out.start(c, dst(c))
        dk, gk_p = _rms_bwd_rows(kv, gk_ref[...], dkn_s[...])
        dmk_ref[...] = dk
        dmv_ref[...] = dv_s[...]

        @pl.when((b == 0) & (h == 0))
        def _():
            dgq_ref[...] = gq_s[...]
            dgk_ref[...] = gk_p

        @pl.when((b > 0) | (h > 0))
        def _():
            dgq_ref[...] += gq_s[...]
            dgk_ref[...] += gk_p

        out.end(dst)

    gain = pl.BlockSpec((1, MEM_HD), lambda b, h: (0, 0))
    kvb = pl.BlockSpec((MEM_LEN, MEM_HD), lambda b, h: (b, h))
    return pl.pallas_call(
        body, grid=(B, MEM_HEADS),
        in_specs=[pl.BlockSpec((S, MEM_HD), lambda b, h: (b, MQ // MEM_HD + h)),
                  pl.BlockSpec((S, MEM_HD), lambda b, h: (b, ZM // MEM_HD + h)),
                  kvb, pl.BlockSpec((MEM_LEN, MEM_HD), lambda b, h: (b, MEM_HEADS + h)), gain, gain,
                  pl.BlockSpec((S, MEM_HD), lambda b, h: (b, h)), pl.BlockSpec(memory_space=pl.ANY)],
        out_specs=[pl.BlockSpec(memory_space=pl.ANY), kvb, kvb, gain, gain],
        out_shape=[jax.ShapeDtypeStruct(dproj.shape, dproj.dtype), jax.ShapeDtypeStruct((B * MEM_LEN, MEM_W), f32),
                   jax.ShapeDtypeStruct((B * MEM_LEN, MEM_W), f32), jax.ShapeDtypeStruct((1, MEM_HD), f32),
                   jax.ShapeDtypeStruct((1, MEM_HD), f32)],
        scratch_shapes=[pltpu.VMEM((MEM_LEN, MEM_HD), f32), pltpu.VMEM((MEM_LEN, MEM_HD), f32), pltpu.VMEM((1, MEM_HD), f32),
                        pltpu.VMEM((2, 2, S, MEM_HD), bf16), pltpu.SemaphoreType.DMA((2, 2))],
        input_output_aliases={7: 0},
        compiler_params=_params(("arbitrary", "arbitrary")), name="mem_bwd",
    )(proj, proj, mkv, mkv, gq, gk, dmo, dproj)


def _merge_fwd(proj, ag, cg, mg, wa, wc, wm):
    T, NC = proj.shape
    D = wa.shape[1]
    tm, tn = _pick(T, 1024), _pick(D, 512)
    assert GT % tn == 0

    def body(ag_ref, cg_ref, mg_ref, wa_ref, wc_ref, wm_ref, g0_ref, g1_ref, g2_ref, mer_ref, ba_ref, bc_ref, bm_ref):
        ba = _dot(ag_ref[...], wa_ref[...], NN_DIMS)
        bc = _dot(cg_ref[...], wc_ref[...], NN_DIMS)
        bm = _dot(mg_ref[...], wm_ref[...], NN_DIMS)
        mer_ref[...] = (_sigmoid(g0_ref[...]) * ba + _sigmoid(g1_ref[...]) * bc + _sigmoid(g2_ref[...]) * bm).astype(bf16)
        ba_ref[...] = ba.astype(bf16)
        bc_ref[...] = bc.astype(bf16)
        bm_ref[...] = bm.astype(bf16)

    def act(w):
        return pl.BlockSpec((tm, w), lambda i, j: (i, 0))

    def wt(k):
        return pl.BlockSpec((k, tn), lambda i, j: (0, j))

    def gate(n):
        return pl.BlockSpec((tm, tn), lambda i, j: (i, (GT + n * D) // tn + j))

    out = pl.BlockSpec((tm, tn), lambda i, j: (i, j))
    return pl.pallas_call(
        body, grid=(T // tm, D // tn),
        in_specs=[act(ATTN_OUT), act(CONV_W), act(MEM_W), wt(ATTN_OUT), wt(CONV_W), wt(MEM_W), gate(0), gate(1), gate(2)],
        out_specs=[out] * 4, out_shape=[jax.ShapeDtypeStruct((T, D), bf16)] * 4,
        compiler_params=_params(("parallel", "parallel")), name="merge_fwd",
    )(ag, cg, mg, wa, wc, wm, proj, proj, proj)


def _out_loss(merged, w_out, x, tgt):
    T, D = x.shape
    tm, tn = _pick(T, 1024), _pick(D, 512)

    def body(m_ref, w_ref, x_ref, t_ref, dy_ref, dyb_ref, lp_ref):
        e = x_ref[...] + _dot(m_ref[...], w_ref[...], NN_DIMS) - t_ref[...]
        dy = e / D
        dy_ref[...] = dy
        dyb_ref[...] = dy.astype(bf16)
        part = jnp.full((1, 8, LANES), jnp.sum(e * e), f32)

        @pl.when(pl.program_id(1) == 0)
        def _():
            lp_ref[...] = part

        @pl.when(pl.program_id(1) > 0)
        def _():
            lp_ref[...] += part

    tile = pl.BlockSpec((tm, tn), lambda i, j: (i, j))
    return pl.pallas_call(
        body, grid=(T // tm, D // tn),
        in_specs=[pl.BlockSpec((tm, D), lambda i, j: (i, 0)), pl.BlockSpec((D, tn), lambda i, j: (0, j)), tile, tile],
        out_specs=[tile, tile, pl.BlockSpec((1, 8, LANES), lambda i, j: (i, 0, 0))],
        out_shape=[jax.ShapeDtypeStruct((T, D), f32), jax.ShapeDtypeStruct((T, D), bf16),
                   jax.ShapeDtypeStruct((T // tm, 8, LANES), f32)],
        compiler_params=_params(("parallel", "arbitrary")), name="out_loss",
    )(merged, w_out, x, tgt)


def _merge_bwd(proj, dyb, w_out, ba, bc, bm):
    T, NC = proj.shape
    D = w_out.shape[0]
    tm, tn = _pick(T, 1024), _pick(D, 512)
    assert GT % tn == 0

    def body(dy_ref, w_ref, ba_ref, bc_ref, bm_ref, g0_ref, g1_ref, g2_ref, da_ref, dc_ref, dm_ref, dproj_ref, stage, sem):
        i, j = pl.program_id(0), pl.program_id(1)

        def dst(n):
            return dproj_ref.at[pl.ds(i * tm, tm), pl.ds(GT + n * D + j * tn, tn)]

        out = _Deferred(stage, sem, i * (D // tn) + j, (T // tm) * (D // tn) - 1, 3)
        out.begin(dst)
        dmer = _dot(dy_ref[...], w_ref[...], NT_DIMS)
        for n, (g_ref, br_ref, o_ref) in enumerate(((g0_ref, ba_ref, da_ref), (g1_ref, bc_ref, dc_ref), (g2_ref, bm_ref, dm_ref))):
            sg = _sigmoid(g_ref[...])
            o_ref[...] = (sg * dmer).astype(bf16)
            stage[out.slot, n] = (dmer * br_ref[...].astype(f32) * (sg * (1.0 - sg))).astype(bf16)
            out.start(n, dst(n))
        out.end(dst)

    tile = pl.BlockSpec((tm, tn), lambda i, j: (i, j))

    def gate(n):
        return pl.BlockSpec((tm, tn), lambda i, j: (i, (GT + n * D) // tn + j))

    return pl.pallas_call(
        body, grid=(T // tm, D // tn),
        in_specs=[pl.BlockSpec((tm, D), lambda i, j: (i, 0)), pl.BlockSpec((tn, D), lambda i, j: (j, 0)), tile, tile, tile,
                  gate(0), gate(1), gate(2)],
        out_specs=[tile, tile, tile, pl.BlockSpec(memory_space=pl.ANY)],
        out_shape=[jax.ShapeDtypeStruct((T, D), bf16)] * 3 + [jax.ShapeDtypeStruct((T, NC), bf16)],
        scratch_shapes=[pltpu.VMEM((2, 3, tm, tn), bf16), pltpu.SemaphoreType.DMA((2, 3))],
        compiler_params=_params(("arbitrary", "arbitrary")), name="merge_bwd",
    )(dyb, w_out, ba, bc, bm, proj, proj, proj)


def _fwd_bwd_head(x2, mem2, t2, proj, attn, B, S, mem_norm_g, attn_q_norm, attn_k_norm, conv_w, mem_q_norm, mem_k_norm,
                  w_kv, w_a, w_c, w_m, w_out):
    D = x2.shape[1]
    mng = mem_norm_g.reshape(1, D)
    mqn, mkn = mem_q_norm.reshape(1, MEM_HD), mem_k_norm.reshape(1, MEM_HD)
    ag, a_comb, lse = attn

    mh = _rms_fwd(mem2, mng, "rms_mem")
    mkv = _mm(mh, w_kv, mode="nn", out_dtype=f32, name="mkv")
    cg = _conv_fwd(proj, conv_w, B, S)
    mg = _mem_fwd(proj, mkv, mqn, mkn, B, S)
    merged, ba, bc, bm = _merge_fwd(proj, ag, cg, mg, w_a, w_c, w_m)
    dy, dyb, lp = _out_loss(merged, w_out, x2, t2)
    sq_err = jnp.sum(lp[:, 0, 0])

    g = {}
    g["w_out"] = _mm(merged, dyb, mode="tn", out_dtype=f32, name="dw_out")
    dba, dbc, dbm, dproj = _merge_bwd(proj, dyb, w_out, ba, bc, bm)
    g["w_br_attn"] = _mm(ag, dba, mode="tn", out_dtype=f32, name="dw_br_attn")
    g["w_br_conv"] = _mm(cg, dbc, mode="tn", out_dtype=f32, name="dw_br_conv")
    g["w_br_mem"] = _mm(mg, dbm, mode="tn", out_dtype=f32, name="dw_br_mem")
    dag = _mm(dba, w_a, mode="nt", out_dtype=f32, name="d_attn_out")
    dcg = _mm(dbc, w_c, mode="nt", out_dtype=f32, name="d_conv_out")
    dmg = _mm(dbm, w_m, mode="nt", out_dtype=f32, name="d_mem_out")
    dproj, g["attn_q_norm"], g["attn_k_norm"] = _attn_bwd(proj, attn_q_norm, attn_k_norm, a_comb, lse, dag, dproj, B, S)
    dproj, g["conv_w"] = _conv_bwd(proj, conv_w, dcg, dproj, B, S)
    dproj, dmk, dmv, dgmq, dgmk = _mem_bwd(proj, mkv, mqn, mkn, dmg, dproj, B, S)
    g["mem_q_norm"], g["mem_k_norm"] = dgmq.reshape(MEM_HD), dgmk.reshape(MEM_HD)
    dmkv = jnp.concatenate([dmk, dmv], axis=1)
    dmh = _mm(dmkv, w_kv, mode="nt", out_dtype=f32, name="d_mem_h")
    g["mem_norm_g"] = _rms_bwd(mem2, mng, dmh, None, "rms_mem_bwd").reshape(D)
    return sq_err, g, dict(dy=dy, dproj=dproj, mh=mh, dmkv=dmkv)


MESH = pl.DeviceIdType.MESH
HBM = pl.BlockSpec(memory_space=pl.ANY)


def _me():
    x, y, c = lax.axis_index("x"), lax.axis_index("y"), lax.axis_index("c")
    return (x, y, c), 4 * x + 2 * y + c


def _peer(k):
    (x, y, c), _ = _me()
    p = (1 - x if k & 4 else x, 1 - y if k & 2 else y, 1 - c if k & 1 else c)
    return p, 4 * p[0] + 2 * p[1] + p[2]


def _window(ref, axis, size, idx):
    if axis is None:
        return ref.at[idx]
    if axis == 0:
        return ref.at[pl.ds(idx * size, size), :]
    return ref.at[:, pl.ds(idx * size, size)]


def _full_shape(s, axis):
    if axis is None:
        return (N_DEV,) + s.shape
    return tuple(N_DEV * d if i == axis else d for i, d in enumerate(s.shape))


OTHER_CHIPS = (4, 2, 6)


def _spread_comm(shards, axes):
    n = len(shards)

    def copies(ins, outs, send_sems, recv_sems, base=0):
        _, me = _me()
        out = []
        for a in range(n):
            mine = _window(outs[a], axes[a], None if axes[a] is None else shards[a].shape[axes[a]], me)
            out.append(pltpu.make_async_copy(ins[a], mine, send_sems.at[base + a, N_CHIP]))
            for k, dist in enumerate((1,) + OTHER_CHIPS):
                dev, _ = _peer(dist)
                out.append(pltpu.make_async_remote_copy(
                    src_ref=ins[a], dst_ref=mine, send_sem=send_sems.at[base + a, k], recv_sem=recv_sems.at[base + a, k],
                    device_id=dev, device_id_type=MESH))
        return out

    shapes = [jax.ShapeDtypeStruct(_full_shape(s, ax), s.dtype) for s, ax in zip(shards, axes)]
    return _Comm(shards, shapes, (n, N_CHIP + 1), copies)


def _forward_blocks(fulls, axes):
    n = len(fulls)
    sizes = [None if ax is None else f.shape[ax] // N_DEV for f, ax in zip(fulls, axes)]

    def body(*refs):
        outs = refs[n:2 * n]
        send_sems, recv_sems = refs[2 * n:]
        sibling, _ = _peer(1)
        copies = []
        for j, dist in enumerate(OTHER_CHIPS):
            _, held = _peer(dist)
            for a in range(n):
                block = _window(outs[a], axes[a], sizes[a], held)
                copies.append(pltpu.make_async_remote_copy(
                    src_ref=block, dst_ref=block, send_sem=send_sems.at[a, j], recv_sem=recv_sems.at[a, j],
                    device_id=sibling, device_id_type=MESH))
        for cp in copies:
            cp.start()
        for cp in copies:
            cp.wait()

    return pl.pallas_call(
        body, in_specs=[HBM] * n, out_specs=[HBM] * n,
        out_shape=[jax.ShapeDtypeStruct(f.shape, f.dtype) for f in fulls],
        scratch_shapes=[pltpu.SemaphoreType.DMA((n, len(OTHER_CHIPS))), pltpu.SemaphoreType.DMA((n, len(OTHER_CHIPS)))],
        input_output_aliases={a: a for a in range(n)},
        name="forward_blocks",
    )(*fulls)


def _shard_order():
    (x, y, c), me = _me()
    xs, ys, xo, yo = _peer(4)[1], _peer(2)[1], _peer(5)[1], _peer(3)[1]
    north = c == 1
    ids = [me, _peer(1)[1], jnp.where(north, xs, ys), jnp.where(north, yo, xo), jnp.where(north, ys, xs),
           jnp.where(north, xo, yo), _peer(6)[1], _peer(7)[1]]
    return jnp.stack(ids).astype(jnp.int32)


def _proj_gather(h, w_shard, order, comm, comm_at):
    T, K = h.shape
    C = w_shard.shape[1]
    tm = _pick(T, 1024)
    nm = T // tm
    ahead = max(nm - 2, 0)
    n_ci, n_co = len(comm.ins), len(comm.out_shapes)

    def body(*refs):
        order_ref, h_ref, ws_ref = refs[:3]
        c_ins = refs[3:3 + n_ci]
        o_ref, wf_ref = refs[3 + n_ci:5 + n_ci]
        c_outs = refs[5 + n_ci:5 + n_ci + n_co]
        wbuf, send_sems, recv_sems, own_sem, buf_sems, c_send, c_recv = refs[5 + n_ci + n_co:]
        t, i = pl.program_id(0), pl.program_id(1)

        @pl.when((t == comm_at) & (i == 0))
        def _():
            for cp in comm.copies(c_ins, c_outs, c_send, c_recv):
                cp.start()
        (x, y, c), me = _me()
        sibling, sib_id = _peer(1)
        chips = [_peer(dist) for dist in OTHER_CHIPS]

        def win(idx):
            return wf_ref.at[:, pl.ds(idx * C, C)]

        def copy(k, block, to, src=None):
            return pltpu.make_async_remote_copy(
                src_ref=win(block) if src is None else src, dst_ref=win(block),
                send_sem=send_sems.at[k], recv_sem=recv_sems.at[k], device_id=to, device_id_type=MESH)

        def fetch(tt, src):
            return pltpu.make_async_copy(src, wbuf.at[tt % 2], buf_sems.at[tt % 2])

        own = pltpu.make_async_copy(ws_ref, win(me), own_sem)
        (x_nbr, x_id), (y_nbr, y_id), (_, d_id) = chips

        def half(k, block, top, to):
            rows = wf_ref.at[pl.ds(0 if top else K // 2, K // 2), pl.ds(block * C, C)]
            return pltpu.make_async_remote_copy(src_ref=rows, dst_ref=rows, send_sem=send_sems.at[k], recv_sem=recv_sems.at[k],
                                                device_id=to, device_id_type=MESH)

        here = (x, y, c)

        def pass_on(from_x):
            if from_x:
                copy(4, x_id, sibling).start()
                half(8, x_id, False, y_nbr).start()
            else:
                copy(5, y_id, sibling).start()
                half(7, y_id, True, x_nbr).start()

        @pl.when((t == 0) & (i == 0))
        def _():
            fetch(0, ws_ref).start()
            own.start()
            copy(0, me, sibling, src=ws_ref).start()

        for tt in range(N_DEV):
            @pl.when((t == tt) & (i == 0))
            def _(tt=tt):
                fetch(tt, ws_ref).wait()

        o_ref[...] = _dot(h_ref[...], wbuf[t % 2], NN_DIMS)

        def schedule(first_x):
            on = c == (1 if first_x else 0)
            (f_dev, f_id), (g_dev, g_id) = ((x_nbr, x_id), (y_nbr, y_id)) if first_x else ((y_nbr, y_id), (x_nbr, x_id))
            kf, kg = (1, 2) if first_x else (2, 1)
            pf, pg = (4, 5) if first_x else (5, 4)

            @pl.when((t == 0) & (i == 0) & on)
            def _():
                copy(kf, me, f_dev, src=ws_ref).start()

            def event(nxt):
                if nxt == 1:
                    copy(0, sib_id, here).wait_recv()
                    return sib_id
                if nxt == 2:
                    copy(kf, f_id, here).wait_recv()
                    copy(kf, me, f_dev, src=ws_ref).wait_send()
                    copy(kg, me, g_dev, src=ws_ref).start()
                    pass_on(first_x)
                    return f_id
                if nxt == 3:
                    copy(pg, g_id + 1 - 2 * c, here).wait_recv()
                    return g_id + 1 - 2 * c
                if nxt == 4:
                    copy(kg, g_id, here).wait_recv()
                    pass_on(not first_x)
                    return g_id
                if nxt == 5:
                    copy(pf, f_id + 1 - 2 * c, here).wait_recv()
                    return f_id + 1 - 2 * c
                if nxt == 6:
                    half(7, d_id, True, here).wait_recv()
                    half(8, d_id, False, here).wait_recv()
                    copy(6, d_id, sibling).start()
                    return d_id
                copy(6, d_id + 1 - 2 * c, here).wait_recv()
                return d_id + 1 - 2 * c

            for tt in range(N_DEV - 1):
                @pl.when((t == tt) & (i == ahead) & on)
                def _(tt=tt):
                    fetch(tt + 1, win(event(tt + 1))).start()

            @pl.when((t == N_DEV - 1) & (i == nm - 1) & on)
            def _():
                copy(kg, me, g_dev, src=ws_ref).wait_send()

        schedule(True)
        schedule(False)

        @pl.when((t == N_DEV - 1) & (i == nm - 1))
        def _():
            copy(0, me, sibling, src=ws_ref).wait_send()
            half(7, y_id, True, x_nbr).wait_send()
            half(8, x_id, False, y_nbr).wait_send()
            for j, (_, dev_id) in enumerate(chips):
                copy(4 + j, dev_id, sibling).wait_send()
            own.wait()
            for cp in comm.copies(c_ins, c_outs, c_send, c_recv):
                cp.wait()

    hbm = pl.BlockSpec(memory_space=pl.ANY)
    res = pl.pallas_call(
        body,
        grid_spec=pltpu.PrefetchScalarGridSpec(
            num_scalar_prefetch=1, grid=(N_DEV, nm),
            in_specs=[pl.BlockSpec((tm, K), lambda t, i, order_ref: (i, 0)), hbm] + [hbm] * n_ci,
            out_specs=[pl.BlockSpec((tm, C), lambda t, i, order_ref: (i, order_ref[t])), hbm] + [hbm] * n_co,
            scratch_shapes=[pltpu.VMEM((2, K, C), bf16), pltpu.SemaphoreType.DMA((9,)), pltpu.SemaphoreType.DMA((9,)),
                            pltpu.SemaphoreType.DMA, pltpu.SemaphoreType.DMA((2,))] + comm.scratch()),
        out_shape=[jax.ShapeDtypeStruct((T, N_DEV * C), f32), jax.ShapeDtypeStruct((K, N_DEV * C), bf16)] + comm.out_shapes,
        compiler_params=_params(("arbitrary", "arbitrary")), name="proj_gather",
    )(order, h, w_shard, *comm.ins)
    return res[0], res[1], list(res[2:])


N_CHIP = 4


def _shard_shape(g, ax):
    return tuple(d // N_DEV if i == ax else d for i, d in enumerate(g.shape))


def _sibling_comm(grads, axes):
    n = len(grads)
    sizes = [g.shape[ax] // N_DEV for g, ax in zip(grads, axes)]

    def copies(ins, lands, send_sems, recv_sems, base=0):
        sibling, _ = _peer(1)
        out = []
        for j in range(N_CHIP):
            _, owner = _peer(2 * j + 1)
            for a in range(n):
                out.append(pltpu.make_async_remote_copy(
                    src_ref=_window(ins[a], axes[a], sizes[a], owner), dst_ref=lands[a].at[j],
                    send_sem=send_sems.at[base + a, j], recv_sem=recv_sems.at[base + a, j], device_id=sibling,
                    device_id_type=MESH))
        return out

    shapes = [jax.ShapeDtypeStruct((N_CHIP,) + _shard_shape(g, ax), g.dtype) for g, ax in zip(grads, axes)]
    return _Comm(grads, shapes, (n, N_CHIP), copies)


def _chips_comm(sums):
    n = len(sums)

    def copies(ins, lands, send_sems, recv_sems, base=0):
        out = []
        for j in range(1, N_CHIP):
            dev, _ = _peer(2 * j)
            for a in range(n):
                out.append(pltpu.make_async_remote_copy(
                    src_ref=ins[a].at[j - 1], dst_ref=lands[a].at[j - 1],
                    send_sem=send_sems.at[base + a, j - 1], recv_sem=recv_sems.at[base + a, j - 1], device_id=dev,
                    device_id_type=MESH))
        return out

    return _Comm(sums, [jax.ShapeDtypeStruct(s.shape, s.dtype) for s in sums], (n, N_CHIP), copies)


def _join(c1, c2):
    n1, m1, k1 = len(c1.ins), len(c1.out_shapes), c1.sem_shape[0]

    def copies(ins, lands, send_sems, recv_sems):
        return (c1.copies(ins[:n1], lands[:m1], send_sems, recv_sems, base=0)
                + c2.copies(ins[n1:], lands[m1:], send_sems, recv_sems, base=k1))

    return _Comm(c1.ins + c2.ins, c1.out_shapes + c2.out_shapes, (k1 + c2.sem_shape[0], N_CHIP), copies)


def _rs_chip_sum(grad, land, xyc, axis, name):
    R, C = land.shape[1:]
    tr = _pick(R, max(8, (640 * 1024) // C), 8)

    def body(xyc_ref, g_ref, l_ref, o_ref):
        o_ref[0] = (g_ref[...] + l_ref[0]).astype(bf16)

    def owner(j, xyc_ref):
        jj = j + 1
        return 4 * (xyc_ref[0] ^ (jj >> 1)) + 2 * (xyc_ref[1] ^ (jj & 1)) + xyc_ref[2]

    if axis == 0:
        g_spec = pl.BlockSpec((tr, C), lambda j, i, xyc_ref: (owner(j, xyc_ref) * (R // tr) + i, 0))
    else:
        g_spec = pl.BlockSpec((tr, C), lambda j, i, xyc_ref: (i, owner(j, xyc_ref)))
    return pl.pallas_call(
        body,
        grid_spec=pltpu.PrefetchScalarGridSpec(
            num_scalar_prefetch=1, grid=(N_CHIP - 1, R // tr),
            in_specs=[g_spec, pl.BlockSpec((1, tr, C), lambda j, i, xyc_ref: (j + 1, i, 0))],
            out_specs=pl.BlockSpec((1, tr, C), lambda j, i, xyc_ref: (j, i, 0))),
        out_shape=jax.ShapeDtypeStruct((N_CHIP - 1, R, C), bf16),
        compiler_params=_params(("parallel", "parallel")), name=name,
    )(xyc, grad, land)


def _allreduce_small(part):
    R = part.shape[0]

    def body(p_ref, o_ref, buf, send_sems, recv_sems):
        (x, y, c), me = _me()
        buf[me] = p_ref[...]
        copies = []
        for k in range(1, N_DEV):
            dev, dev_id = _peer(k)
            copies.append(pltpu.make_async_remote_copy(
                src_ref=p_ref, dst_ref=buf.at[me], send_sem=send_sems.at[k - 1], recv_sem=recv_sems.at[k - 1],
                device_id=dev, device_id_type=MESH))
        for cp in copies:
            cp.start()
        for k in range(1, N_DEV):
            _, dev_id = _peer(k)
            pltpu.make_async_remote_copy(
                src_ref=p_ref, dst_ref=buf.at[dev_id], send_sem=send_sems.at[k - 1], recv_sem=recv_sems.at[k - 1],
                device_id=(x, y, c), device_id_type=MESH).wait_recv()
        for cp in copies:
            cp.wait_send()
        acc = buf[0]
        for d in range(1, N_DEV):
            acc = acc + buf[d]
        o_ref[...] = acc

    return pl.pallas_call(
        body, in_specs=[pl.BlockSpec(memory_space=pltpu.VMEM)], out_specs=pl.BlockSpec(memory_space=pltpu.VMEM),
        out_shape=jax.ShapeDtypeStruct((R, LANES), f32),
        scratch_shapes=[pltpu.VMEM((N_DEV, R, LANES), f32), pltpu.SemaphoreType.DMA((N_DEV - 1,)), pltpu.SemaphoreType.DMA((N_DEV - 1,))],
        name="allreduce_small",
    )(part)


def _adamw_math(w, g, m, v):
    m2 = ADAM_B1 * m + (1.0 - ADAM_B1) * g
    v2 = ADAM_B2 * v + (1.0 - ADAM_B2) * (g * g)
    m_hat = m2 / (1.0 - ADAM_B1 ** ADAM_STEP)
    v_hat = v2 / (1.0 - ADAM_B2 ** ADAM_STEP)
    return -ADAM_LR * (m_hat / (jnp.sqrt(v_hat) + ADAM_EPS) + ADAM_WD * w), m2, v2


def _adamw_shard(grad, land_sib, land_chips, w, m, v, me, axis, name, row0=0, prev=None):
    R, C = land_sib.shape[1:]
    assert axis == 1 or R == w.shape[0]
    tr = _pick(R, max(8, (320 * 1024) // C), 8)
    r0 = row0 // tr
    n_prev = len(prev) if prev else 0

    def body(me_ref, g_ref, s_ref, l_ref, w_ref, m_ref, v_ref, *rest):
        go_ref, d_ref, mo_ref, vo_ref = rest[n_prev:]
        g = g_ref[...] + s_ref[0]
        for j in range(N_CHIP - 1):
            g = g + l_ref[j].astype(f32)
        d, m2, v2 = _adamw_math(w_ref[...], g, m_ref[...], v_ref[...])
        go_ref[...] = g
        d_ref[...] = d
        mo_ref[...] = m2
        vo_ref[...] = v2

    if axis == 0:
        g_spec = pl.BlockSpec((tr, C), lambda i, me_ref: (me_ref[0] * (R // tr) + i, 0))
    else:
        g_spec = pl.BlockSpec((tr, C), lambda i, me_ref: (i, me_ref[0]))
    blk = pl.BlockSpec((tr, C), lambda i, me_ref: (r0 + i, 0))
    return pl.pallas_call(
        body,
        grid_spec=pltpu.PrefetchScalarGridSpec(
            num_scalar_prefetch=1, grid=(R // tr,),
            in_specs=[g_spec, pl.BlockSpec((1, tr, C), lambda i, me_ref: (0, i, 0)),
                      pl.BlockSpec((N_CHIP - 1, tr, C), lambda i, me_ref: (0, i, 0)), blk, blk, blk]
            + [pl.BlockSpec(memory_space=pl.ANY)] * n_prev,
            out_specs=[blk] * 4),
        out_shape=[jax.ShapeDtypeStruct(w.shape, f32)] * 4,
        input_output_aliases={7 + i: i for i in range(n_prev)},
        compiler_params=_params(("parallel",)), name=name,
    )(me, grad, land_sib, land_chips, w, m, v, *(prev or []))


def _adamw_small(g, w, m, v):
    def body(g_ref, w_ref, m_ref, v_ref, d_ref, mo_ref, vo_ref):
        d, m2, v2 = _adamw_math(w_ref[...], g_ref[...], m_ref[...], v_ref[...])
        d_ref[...] = d
        mo_ref[...] = m2
        vo_ref[...] = v2

    return pl.pallas_call(body, out_shape=[jax.ShapeDtypeStruct(g.shape, f32)] * 3, name="adamw_small")(g, w, m, v)


def _pack_rows(parts, total_rows):
    rows = jnp.concatenate([p.reshape(-1, LANES) for p in parts], axis=0)
    return jnp.pad(rows, ((0, total_rows - rows.shape[0]), (0, 0)))


BIG = ("w_in", "mem_w_kv", "w_br_attn", "w_br_conv", "w_br_mem", "w_out")
BIG_AXIS = {"w_in": 1, "mem_w_kv": 0, "w_br_attn": 1, "w_br_conv": 1, "w_br_mem": 1, "w_out": 0}
SMALL = ("norm_g", "mem_norm_g", "attn_q_norm", "attn_k_norm", "mem_q_norm", "mem_k_norm")
ALL_W = ("norm_g", "mem_norm_g", "w_in", "attn_q_norm", "attn_k_norm", "conv_w", "mem_w_kv", "mem_q_norm", "mem_k_norm",
         "w_br_attn", "w_br_conv", "w_br_mem", "w_out")


def kernel(x, mem, norm_g, mem_norm_g, w_in, attn_q_norm, attn_k_norm, conv_w, mem_w_kv, mem_q_norm, mem_k_norm, w_br_attn, w_br_conv, w_br_mem, w_out, loss_target, m_norm_g, m_mem_norm_g, m_w_in, m_attn_q_norm, m_attn_k_norm, m_conv_w, m_mem_w_kv, m_mem_q_norm, m_mem_k_norm, m_w_br_attn, m_w_br_conv, m_w_br_mem, m_w_out, v_norm_g, v_mem_norm_g, v_w_in, v_attn_q_norm, v_attn_k_norm, v_conv_w, v_mem_w_kv, v_mem_q_norm, v_mem_k_norm, v_w_br_attn, v_w_br_conv, v_w_br_mem, v_w_out):
    w = dict(norm_g=norm_g, mem_norm_g=mem_norm_g, w_in=w_in, attn_q_norm=attn_q_norm, attn_k_norm=attn_k_norm, conv_w=conv_w,
             mem_w_kv=mem_w_kv, mem_q_norm=mem_q_norm, mem_k_norm=mem_k_norm, w_br_attn=w_br_attn, w_br_conv=w_br_conv,
             w_br_mem=w_br_mem, w_out=w_out)
    mo = dict(norm_g=m_norm_g, mem_norm_g=m_mem_norm_g, w_in=m_w_in, attn_q_norm=m_attn_q_norm, attn_k_norm=m_attn_k_norm,
              conv_w=m_conv_w, mem_w_kv=m_mem_w_kv, mem_q_norm=m_mem_q_norm, mem_k_norm=m_mem_k_norm, w_br_attn=m_w_br_attn,
              w_br_conv=m_w_br_conv, w_br_mem=m_w_br_mem, w_out=m_w_out)
    vo = dict(norm_g=v_norm_g, mem_norm_g=v_mem_norm_g, w_in=v_w_in, attn_q_norm=v_attn_q_norm, attn_k_norm=v_attn_k_norm,
              conv_w=v_conv_w, mem_w_kv=v_mem_w_kv, mem_q_norm=v_mem_q_norm, mem_k_norm=v_mem_k_norm, w_br_attn=v_w_br_attn,
              w_br_conv=v_w_br_conv, w_br_mem=v_w_br_mem, w_out=v_w_out)
    B, S, D = x.shape
    me = (4 * lax.axis_index("x") + 2 * lax.axis_index("y") + lax.axis_index("c")).astype(jnp.int32)

    T = B * S
    x2, t2, mem2, ng = x.reshape(T, D), loss_target.reshape(T, D), mem.reshape(B * MEM_LEN, D), norm_g.reshape(1, D)
    h = _rms_fwd(x2, ng, "rms_x")
    rows_sharded, cols_sharded = ("mem_w_kv", "w_out"), ("w_br_attn", "w_br_conv", "w_br_mem")
    later = rows_sharded + cols_sharded
    later_axes = [BIG_AXIS[n] for n in later] + [None]
    conv_pad = jnp.pad(conv_w, ((0, 8 - conv_w.shape[0]), (0, 0)))
    proj, w_in_full, spread_a = _proj_gather(
        h, w_in.astype(bf16), _shard_order(),
        _spread_comm([w[n].astype(bf16) for n in rows_sharded], [BIG_AXIS[n] for n in rows_sharded]), comm_at=N_DEV - 3)
    *attn, spread_b = _attn_fwd(proj, attn_q_norm, attn_k_norm, B, S,
                                comm=_spread_comm([w[n].astype(bf16) for n in cols_sharded] + [conv_pad],
                                                  [BIG_AXIS[n] for n in cols_sharded] + [None]))
    *fulls, conv_g = _forward_blocks(spread_a + spread_b, later_axes)
    wf = dict(zip(later, fulls), w_in=w_in_full)
    conv_full = conv_g[:, :3, :].transpose(1, 0, 2).reshape(3, CONV_W)

    sq_err, g, t = _fwd_bwd_head(x2, mem2, t2, proj, attn, B, S, mem_norm_g, attn_q_norm, attn_k_norm, conv_full, mem_q_norm,
                                 mem_k_norm, wf["mem_w_kv"], wf["w_br_attn"], wf["w_br_conv"], wf["w_br_mem"], wf["w_out"])
    t.update(x2=x2, ng=ng, h=h)

    xyc = jnp.stack([lax.axis_index("x"), lax.axis_index("y"), lax.axis_index("c")]).astype(jnp.int32)
    me1 = me.reshape(1)
    early = ("w_out", "w_br_attn", "w_br_conv", "w_br_mem")
    g["mem_w_kv"], sib_early = _mm(t["mh"], t["dmkv"], mode="tn", out_dtype=f32, name="dw_kv",
                                   comm=_sibling_comm([g[n] for n in early], [BIG_AXIS[n] for n in early]))
    sums_early = [_rs_chip_sum(g[n], ls, xyc, BIG_AXIS[n], "rs_sum_" + n) for n, ls in zip(early, sib_early)]
    tm_w = _pick(D // 2, 1024)
    half = (D // 2) // tm_w
    g_top, (*chips_early, sib_kv) = _mm(t["h"], t["dproj"], mode="tn", out_dtype=f32, name="dw_in_top", tm=tm_w,
                                        m_tiles=(0, half),
                                        comm=_join(_chips_comm(sums_early), _sibling_comm([g["mem_w_kv"]], [0])))
    sum_kv = _rs_chip_sum(g["mem_w_kv"], sib_kv, xyc, 0, "rs_sum_mem_w_kv")
    g_bot, (chips_kv, sib_top) = _mm(t["h"], t["dproj"], mode="tn", out_dtype=f32, name="dw_in_bot", tm=tm_w,
                                     m_tiles=(half, half), comm=_join(_chips_comm([sum_kv]), _sibling_comm([g_top], [1])))
    sum_top = _rs_chip_sum(g_top, sib_top, xyc, 1, "rs_sum_w_in_top")
    tm_t = _pick(T // 2, 1024)
    halfm = (T // 2) // tm_t
    dh, (chips_top, sib_bot) = _mm(t["dproj"], wf["w_in"], mode="nt", out_dtype=f32, name="d_h_a", tm=tm_t, tk=D_H_TK,
                                   m_tiles=(0, halfm), into="new",
                                   comm=_join(_chips_comm([sum_top]), _sibling_comm([g_bot], [1])))
    sum_bot = _rs_chip_sum(g_bot, sib_bot, xyc, 1, "rs_sum_w_in_bot")
    dh, (chips_bot,) = _mm(t["dproj"], wf["w_in"], mode="nt", out_dtype=f32, name="d_h_b", tm=tm_t, tk=D_H_TK, m_tiles=(halfm, halfm),
                           into=dh, comm=_chips_comm([sum_bot]))
    dx, dng = _rms_bwd(t["x2"], t["ng"], dh, t["dy"], "rms_x_bwd")
    g["norm_g"] = dng.reshape(D)

    grad, delta, new_m, new_v = {}, {}, {}, {}
    for n, ls, lc in zip(early + ("mem_w_kv",), sib_early + [sib_kv], chips_early + [chips_kv]):
        grad[n], delta[n], new_m[n], new_v[n] = _adamw_shard(g[n], ls, lc, w[n], mo[n], vo[n], me1, BIG_AXIS[n], "adamw_" + n)
    top = _adamw_shard(g_top, sib_top, chips_top, w_in, m_w_in, v_w_in, me1, 1, "adamw_w_in_top")
    grad["w_in"], delta["w_in"], new_m["w_in"], new_v["w_in"] = _adamw_shard(
        g_bot, sib_bot, chips_bot, w_in, m_w_in, v_w_in, me1, 1, "adamw_w_in_bot", row0=D // 2, prev=top)

    n_rep = sum(w[n].size for n in SMALL) // LANES
    conv_rows = g["conv_w"].reshape(3, N_DEV, LANES).transpose(1, 0, 2).reshape(3 * N_DEV, LANES)
    n_rows = -(-(n_rep + 3 * N_DEV + 1) // 8) * 8
    part = _pack_rows([g[n] for n in SMALL] + [conv_rows, jnp.full((1, LANES), sq_err, f32)], n_rows)
    tot = _allreduce_small(part)
    loss = tot[n_rep + 3 * N_DEV, 0] * (0.5 / D)
    n_small = -(-(n_rep + 3) // 8) * 8
    g_small = _pack_rows([tot[:n_rep], lax.dynamic_slice(tot, (n_rep + 3 * me, 0), (3, LANES))], n_small)
    names = SMALL + ("conv_w",)
    d_s, m_s, v_s = _adamw_small(g_small, _pack_rows([w[n] for n in names], n_small), _pack_rows([mo[n] for n in names], n_small),
                                 _pack_rows([vo[n] for n in names], n_small))
    off = 0
    for n in names:
        r = w[n].size // LANES
        grad[n], delta[n] = g_small[off:off + r].reshape(w[n].shape), d_s[off:off + r].reshape(w[n].shape)
        new_m[n], new_v[n] = m_s[off:off + r].reshape(w[n].shape), v_s[off:off + r].reshape(w[n].shape)
        off += r
    return (loss, dx.reshape(B, S, D), *[grad[n] for n in ALL_W], *[delta[n] for n in ALL_W], *[new_m[n] for n in ALL_W],
            *[new_v[n] for n in ALL_W])
```

```python
import functools

import jax
import jax.numpy as jnp
from jax import lax
from jax.experimental import pallas as pl
from jax.experimental.pallas import tpu as pltpu

f32 = jnp.float32
bf16 = jnp.bfloat16

N_DEV = 8
HEAD_DIM = 128
DILATIONS = (1, 4, 16)
N_GROUPS = 3
HEADS = 4
BLK = 128
ATTN_QKV = N_GROUPS * HEADS * HEAD_DIM
ATTN_OUT = HEADS * HEAD_DIM
CONV_W = 1024
MEM_LEN = 256
MEM_HEADS = 4
MEM_HD = 256
MEM_W = MEM_HEADS * MEM_HD
EPS = 1e-6
Q0, K0, V0 = 0, ATTN_QKV, 2 * ATTN_QKV
ZA = 3 * ATTN_QKV
CB, CC, CV, ZC = ZA + ATTN_OUT, ZA + ATTN_OUT + CONV_W, ZA + ATTN_OUT + 2 * CONV_W, ZA + ATTN_OUT + 3 * CONV_W
MQ = ZC + CONV_W
ZM = MQ + MEM_W
GT = ZM + MEM_W

ADAM_LR, ADAM_B1, ADAM_B2, ADAM_EPS, ADAM_WD, ADAM_STEP = 0.001, 0.9, 0.999, 1e-08, 0.01, 10

VMEM_LIMIT = 56 * 1024 * 1024
D_H_TK = 4352
LANES = 128

NT_DIMS = (((1,), (1,)), ((), ()))
TN_DIMS = (((0,), (0,)), ((), ()))
NN_DIMS = (((1,), (0,)), ((), ()))


def _params(sem=None):
    return pltpu.CompilerParams(dimension_semantics=sem, vmem_limit_bytes=VMEM_LIMIT)


def _pick(n, pref, q=LANES):
    t = (min(pref, n) // q) * q
    while t >= q:
        if n % t == 0:
            return t
        t -= q
    return n


def _dot(a, b, dims):
    return lax.dot_general(a.astype(bf16), b.astype(bf16), dims, preferred_element_type=f32)


def _sigmoid(z):
    return 0.5 * jnp.tanh(0.5 * z) + 0.5


def _rstd(v):
    return lax.rsqrt(jnp.mean(v * v, axis=-1, keepdims=True) + EPS)


class _Deferred:
    def __init__(self, stage, sems, step, last, n):
        assert last >= 1
        self.stage, self.sems, self.step, self.last, self.n = stage, sems, step, last, n
        self.slot = step % 2

    def _drain(self, slot, like):
        for c in range(self.n):
            pltpu.make_async_copy(self.stage.at[slot, c], like(c), self.sems.at[slot, c]).wait()

    def begin(self, like):
        pl.when(self.step >= 2)(lambda: self._drain(self.slot, like))

    def start(self, c, dst):
        pltpu.make_async_copy(self.stage.at[self.slot, c], dst, self.sems.at[self.slot, c]).start()

    def end(self, like):
        @pl.when(self.step == self.last)
        def _():
            self._drain(self.slot, like)
            self._drain(1 - self.slot, like)


def _row_sum(v):
    hi = v.astype(bf16)
    lo = (v - hi.astype(f32)).astype(bf16)
    ones = jnp.ones((v.shape[1], v.shape[1]), bf16)
    return _dot(hi, ones, NN_DIMS) + _dot(lo, ones, NN_DIMS)


def _rstd_head(v):
    return lax.rsqrt(_row_sum(v * v) * (1.0 / v.shape[1]) + EPS)


class _Comm:
    def __init__(self, ins, out_shapes, sem_shape, copies):
        self.ins, self.out_shapes, self.sem_shape, self.copies = list(ins), list(out_shapes), sem_shape, copies

    def scratch(self):
        return [pltpu.SemaphoreType.DMA(self.sem_shape), pltpu.SemaphoreType.DMA(self.sem_shape)]


def _mm(a, b, *, mode, out_dtype, name, tm=1024, tn=1024, tk=4096, m_tiles=None, into=None, comm=None):
    if mode == "nn":
        (M, K), (K2, N) = a.shape, b.shape
    elif mode == "nt":
        (M, K), (N, K2) = a.shape, b.shape
    else:
        (K, M), (K2, N) = a.shape, b.shape
    assert K == K2
    tm, tn, tk = _pick(M, tm), _pick(N, tn), _pick(K, tk)
    nk = K // tk
    m0, mc = m_tiles if m_tiles is not None else (0, M // tm)
    o0 = 0 if into is None else m0
    aliased = into is not None and not isinstance(into, str)
    n_ci = len(comm.ins) if comm else 0
    n_co = len(comm.out_shapes) if comm else 0
    grid = (mc, N // tn, nk)
    dims = {"nn": NN_DIMS, "nt": NT_DIMS, "tn": TN_DIMS}[mode]

    def body(*refs, nk):
        a_ref, b_ref = refs[:2]
        pos = 3 if aliased else 2
        c_ins, o_ref, c_outs = refs[pos:pos + n_ci], refs[pos + n_ci], refs[pos + n_ci + 1:pos + n_ci + 1 + n_co]
        scratch = refs[pos + n_ci + 1 + n_co:]
        ids = [pl.program_id(d) for d in range(3)]
        if comm:
            sems = scratch[-2:]

            @pl.when((ids[0] == 0) & (ids[1] == 0) & (ids[2] == 0))
            def _():
                for cp in comm.copies(c_ins, c_outs, *sems):
                    cp.start()

        if nk == 1:
            o_ref[...] = _dot(a_ref[...], b_ref[...], dims).astype(o_ref.dtype)
        else:
            acc_ref, k = scratch[0], ids[2]

            @pl.when(k == 0)
            def _():
                acc_ref[...] = _dot(a_ref[...], b_ref[...], dims)

            @pl.when((k > 0) & (k < nk - 1))
            def _():
                acc_ref[...] += _dot(a_ref[...], b_ref[...], dims)

            @pl.when(k == nk - 1)
            def _():
                o_ref[...] = (acc_ref[...] + _dot(a_ref[...], b_ref[...], dims)).astype(o_ref.dtype)

        if comm:
            @pl.when((ids[0] == grid[0] - 1) & (ids[1] == grid[1] - 1) & (ids[2] == grid[2] - 1))
            def _():
                for cp in comm.copies(c_ins, c_outs, *sems):
                    cp.wait()

    if mode == "tn":
        a_spec = pl.BlockSpec((tk, tm), lambda i, j, k: (k, m0 + i))
    else:
        a_spec = pl.BlockSpec((tm, tk), lambda i, j, k: (m0 + i, k))
    if mode == "nt":
        b_spec = pl.BlockSpec((tn, tk), lambda i, j, k: (j, k))
    else:
        b_spec = pl.BlockSpec((tk, tn), lambda i, j, k: (k, j))
    hbm = pl.BlockSpec(memory_space=pl.ANY)
    res = pl.pallas_call(
        functools.partial(body, nk=nk),
        grid=grid,
        in_specs=[a_spec, b_spec] + [hbm] * (aliased + n_ci),
        out_specs=[pl.BlockSpec((tm, tn), lambda i, j, k: (o0 + i, j))] + [hbm] * n_co,
        out_shape=[jax.ShapeDtypeStruct((mc * tm if into is None else M, N), out_dtype)] + (comm.out_shapes if comm else []),
        scratch_shapes=([pltpu.VMEM((tm, tn), f32)] if nk > 1 else []) + (comm.scratch() if comm else []),
        input_output_aliases={2: 0} if aliased else {},
        compiler_params=_params(("arbitrary",) * 3 if comm else ("parallel", "parallel", "arbitrary")),
        name=name,
    )(a, b, *([into] if aliased else []), *(comm.ins if comm else []))
    return (res[0], list(res[1:])) if comm else res[0]


def _rms_fwd(x, g, name):
    T, D = x.shape
    tm = _pick(T, 256, 8)

    def body(x_ref, g_ref, h_ref):
        xv = x_ref[...]
        h_ref[...] = (xv * _rstd(xv) * g_ref[...]).astype(bf16)

    return pl.pallas_call(
        body, grid=(T // tm,),
        in_specs=[pl.BlockSpec((tm, D), lambda i: (i, 0)), pl.BlockSpec((1, D), lambda i: (0, 0))],
        out_specs=pl.BlockSpec((tm, D), lambda i: (i, 0)),
        out_shape=jax.ShapeDtypeStruct((T, D), bf16),
        compiler_params=_params(("parallel",)), name=name,
    )(x, g)


def _rms_bwd(x, g, dh, dy, name):
    T, D = x.shape
    tm = _pick(T, 256, 8)
    with_dx = dy is not None

    def body(*refs):
        if with_dx:
            x_ref, g_ref, dh_ref, dy_ref, dx_ref, dg_ref = refs
        else:
            x_ref, g_ref, dh_ref, dg_ref = refs
        xv = x_ref[...]
        r = _rstd(xv)
        xh = xv * r
        dhv = dh_ref[...]
        part = jnp.sum(dhv * xh, axis=0, keepdims=True)

        @pl.when(pl.program_id(0) == 0)
        def _():
            dg_ref[...] = part

        @pl.when(pl.program_id(0) > 0)
        def _():
            dg_ref[...] += part

        if with_dx:
            dxh = dhv * g_ref[...]
            dx_ref[...] = dy_ref[...] + r * (dxh - xh * jnp.mean(dxh * xh, axis=-1, keepdims=True))

    row = pl.BlockSpec((tm, D), lambda i: (i, 0))
    vec = pl.BlockSpec((1, D), lambda i: (0, 0))
    if with_dx:
        return pl.pallas_call(
            body, grid=(T // tm,), in_specs=[row, vec, row, row], out_specs=[row, vec],
            out_shape=[jax.ShapeDtypeStruct((T, D), f32), jax.ShapeDtypeStruct((1, D), f32)],
            compiler_params=_params(("arbitrary",)), name=name,
        )(x, g, dh, dy)
    return pl.pallas_call(
        body, grid=(T // tm,), in_specs=[row, vec, row], out_specs=vec,
        out_shape=jax.ShapeDtypeStruct((1, D), f32),
        compiler_params=_params(("arbitrary",)), name=name,
    )(x, g, dh)


MAX_UNIT_UNROLL = 6


def _unit_unroll(trips):
    return max(u for u in range(1, MAX_UNIT_UNROLL + 1) if trips % u == 0)


def _rows(start, size, stride):
    return pl.ds(start, size) if stride == 1 else pl.ds(start, size, stride=stride)


def _attn_units(S, d, first, prev):
    nb = S // d // BLK

    def do_first(r, c):
        first(_rows(r, BLK, d))
        return c

    lax.fori_loop(0, d, do_first, 0, unroll=_unit_unroll(d))
    if nb > 1:
        def do_prev(u, c):
            r = u // (nb - 1)
            b = u % (nb - 1) + 1
            base = r + d * b * BLK
            prev(_rows(base, BLK, d), _rows(base - d * BLK, 2 * BLK, d))
            return c

        lax.fori_loop(0, d * (nb - 1), do_prev, 0, unroll=_unit_unroll(d * (nb - 1)))


def _band_bias(nkeys):
    i = lax.broadcasted_iota(jnp.int32, (BLK, nkeys), 0)
    j = lax.broadcasted_iota(jnp.int32, (BLK, nkeys), 1)
    ok = (j <= i) if nkeys == BLK else ((j >= i) & (j <= i + BLK))
    return jnp.where(ok, 0.0, -jnp.inf).astype(f32)


def _normalize_into(src_ref, gain, dst_ref, S):
    for c in range(S // 256):
        rows = pl.ds(c * 256, 256)
        v = src_ref[rows, :]
        dst_ref[rows, :] = v * _rstd_head(v) * gain


def _attn_fwd(proj, gq, gk, B, S, comm=None):
    T, NC = proj.shape
    scale = HEAD_DIM ** -0.5
    n_ci = len(comm.ins) if comm else 0
    n_co = len(comm.out_shapes) if comm else 0

    def body(*refs):
        q_ref, k_ref, v_ref, z_ref, gq_ref, gk_ref = refs[:6]
        c_ins = refs[6:6 + n_ci]
        ag_ref, a_ref, lse_ref = refs[6 + n_ci:9 + n_ci]
        c_outs = refs[9 + n_ci:9 + n_ci + n_co]
        qs, ks, o0, o1, o2, l0, l1, l2, bias1, bias2 = refs[9 + n_ci + n_co:19 + n_ci + n_co]
        sems = refs[19 + n_ci + n_co:]
        g = pl.program_id(2)
        if comm:
            @pl.when((pl.program_id(0) == 0) & (pl.program_id(1) == 0) & (g == 0))
            def _():
                for cp in comm.copies(c_ins, c_outs, *sems):
                    cp.start()

        bias1[...] = _band_bias(BLK)
        bias2[...] = _band_bias(2 * BLK)
        _normalize_into(q_ref, gq_ref[pl.ds(g, 1), :], qs, S)
        _normalize_into(k_ref, gk_ref[pl.ds(g, 1), :], ks, S)
        o_s, l_s = (o0, o1, o2), (l0, l1, l2)

        def run(gi):
            def unit(qr, kr, nkeys):
                s = _dot(qs[qr, :], ks[kr, :], NT_DIMS) * scale + (bias1 if nkeys == BLK else bias2)[...]
                m = jnp.max(s, axis=-1, keepdims=True)
                p = jnp.exp(s - m)
                den = jnp.sum(p, axis=-1, keepdims=True)
                o_s[gi][qr, :] = _dot(p, v_ref[kr, :], NN_DIMS) * (1.0 / den)
                l_s[gi][qr, :] = jnp.broadcast_to(m + jnp.log(den), (BLK, HEAD_DIM))

            _attn_units(S, DILATIONS[gi], lambda qr: unit(qr, qr, BLK), lambda qr, kr: unit(qr, kr, 2 * BLK))

        for gi in range(N_GROUPS):
            pl.when(g == gi)(functools.partial(run, gi))

        @pl.when(g == N_GROUPS - 1)
        def _():
            for c in range(S // 256):
                rows = pl.ds(c * 256, 256)
                la, lb, lc = l0[rows, :], l1[rows, :], l2[rows, :]
                m = jnp.maximum(jnp.maximum(la, lb), lc)
                wa, wb, wc = jnp.exp(la - m), jnp.exp(lb - m), jnp.exp(lc - m)
                tot = wa + wb + wc
                a = (wa / tot) * o0[rows, :] + (wb / tot) * o1[rows, :] + (wc / tot) * o2[rows, :]
                z = z_ref[rows, :]
                a_ref[rows, :] = a
                lse_ref[rows, :] = m + jnp.log(tot)
                ag_ref[rows, :] = (a * (z * _sigmoid(z))).astype(bf16)

        if comm:
            @pl.when((pl.program_id(0) == B - 1) & (pl.program_id(1) == HEADS - 1) & (g == N_GROUPS - 1))
            def _():
                for cp in comm.copies(c_ins, c_outs, *sems):
                    cp.wait()

    def slab(col0):
        return pl.BlockSpec((S, HEAD_DIM), lambda b, h, g: (b, col0 // HEAD_DIM + g * HEADS + h))

    gain = pl.BlockSpec((N_GROUPS, HEAD_DIM), lambda b, h, g: (0, 0))
    out = pl.BlockSpec((S, HEAD_DIM), lambda b, h, g: (b, h))
    hbm = pl.BlockSpec(memory_space=pl.ANY)
    res = pl.pallas_call(
        body, grid=(B, HEADS, N_GROUPS),
        in_specs=[slab(Q0), slab(K0), slab(V0), pl.BlockSpec((S, HEAD_DIM), lambda b, h, g: (b, ZA // HEAD_DIM + h)), gain, gain]
        + [hbm] * n_ci,
        out_specs=[out, out, out] + [hbm] * n_co,
        out_shape=[jax.ShapeDtypeStruct((T, ATTN_OUT), bf16), jax.ShapeDtypeStruct((T, ATTN_OUT), f32),
                   jax.ShapeDtypeStruct((T, ATTN_OUT), f32)] + (comm.out_shapes if comm else []),
        scratch_shapes=[pltpu.VMEM((S, HEAD_DIM), f32)] * 8 + [pltpu.VMEM((BLK, BLK), f32), pltpu.VMEM((BLK, 2 * BLK), f32)]
        + (comm.scratch() if comm else []),
        compiler_params=_params(("arbitrary", "arbitrary", "arbitrary")), name="attn_fwd",
    )(proj, proj, proj, proj, gq, gk, *(comm.ins if comm else []))
    return res[0], res[1], res[2], list(res[3:])


def _rms_bwd_rows(raw, gain, dn, on_mxu=True):
    r = _rstd_head(raw) if on_mxu else _rstd(raw)
    xh = raw * r
    dxh = dn * gain
    if on_mxu:
        mean = _row_sum(dxh * xh) * (1.0 / raw.shape[1])
    else:
        mean = jnp.mean(dxh * xh, axis=-1, keepdims=True)
    return r * (dxh - xh * mean), jnp.sum(dn * xh, axis=0, keepdims=True)


def _attn_bwd(proj, gq, gk, a_comb, lse, dag, dproj, B, S):
    T, NC = proj.shape
    scale = HEAD_DIM ** -0.5

    def body(q_ref, k_ref, v_ref, z_ref, gq_ref, gk_ref, a_ref, lse_ref, dag_ref, dproj_in, dproj_ref, dgq_ref, dgk_ref,
             qs, ks, da_s, dl_s, dq_s, dk_s, dv_s, bias1, bias2, stage, dz_stage, sem, dz_sem):
        b, h, g = pl.program_id(0), pl.program_id(1), pl.program_id(2)
        bias1[...] = _band_bias(BLK)
        bias2[...] = _band_bias(2 * BLK)
        gq_row, gk_row = gq_ref[pl.ds(g, 1), :], gk_ref[pl.ds(g, 1), :]
        _normalize_into(q_ref, gq_row, qs, S)
        _normalize_into(k_ref, gk_row, ks, S)

        def dst(c):
            return dproj_ref.at[pl.ds(b * S, S), pl.ds((Q0, K0, V0)[c] + (g * HEADS + h) * HEAD_DIM, HEAD_DIM)]

        out = _Deferred(stage, sem, (b * HEADS + h) * N_GROUPS + g, B * HEADS * N_GROUPS - 1, 3)
        out.begin(dst)

        @pl.when((b == 0) & (h == 0) & (g == 0))
        def _():
            dgq_ref[...] = jnp.zeros_like(dgq_ref)
            dgk_ref[...] = jnp.zeros_like(dgk_ref)

        @pl.when(g == 0)
        def _():
            for c in range(S // 256):
                rows = pl.ds(c * 256, 256)
                z, a, dg_ = z_ref[rows, :], a_ref[rows, :], dag_ref[rows, :]
                sg = _sigmoid(z)
                da = dg_ * (z * sg)
                da_s[rows, :] = da
                dl_s[rows, :] = _row_sum(da * a)
                dz_stage[rows, :] = (dg_ * a * (sg * (1.0 + z * (1.0 - sg)))).astype(bf16)
            cp = pltpu.make_async_copy(dz_stage, dproj_ref.at[pl.ds(b * S, S), pl.ds(ZA + h * HEAD_DIM, HEAD_DIM)], dz_sem)
            cp.start()
            cp.wait()

        dk_s[...] = jnp.zeros_like(dk_s)
        dv_s[...] = jnp.zeros_like(dv_s)

        def run(gi):
            def unit(qr, kr, nkeys):
                q, k, v = qs[qr, :], ks[kr, :], v_ref[kr, :]
                s = _dot(q, k, NT_DIMS) * scale
                p = jnp.exp(s + (bias1 if nkeys == BLK else bias2)[...] - lse_ref[qr, :][:, :1])
                da = da_s[qr, :]
                dp = _dot(da, v, NT_DIMS)
                ds = p * (dp - dl_s[qr, :][:, :1]) * scale
                dq_s[qr, :] = _dot(ds, k, NN_DIMS)
                dk_s[kr, :] += _dot(ds, q, TN_DIMS)
                dv_s[kr, :] += _dot(p, da, TN_DIMS)

            _attn_units(S, DILATIONS[gi], lambda qr: unit(qr, qr, BLK), lambda qr, kr: unit(qr, kr, 2 * BLK))

        for gi in range(N_GROUPS):
            pl.when(g == gi)(functools.partial(run, gi))

        gq_acc = jnp.zeros((1, HEAD_DIM), f32)
        gk_acc = jnp.zeros((1, HEAD_DIM), f32)
        for c in range(S // 256):
            rows = pl.ds(c * 256, 256)
            dq, gq_p = _rms_bwd_rows(q_ref[rows, :], gq_row, dq_s[rows, :])
            dk, gk_p = _rms_bwd_rows(k_ref[rows, :], gk_row, dk_s[rows, :])
            gq_acc, gk_acc = gq_acc + gq_p, gk_acc + gk_p
            stage[out.slot, 0, rows, :] = dq.astype(bf16)
            stage[out.slot, 1, rows, :] = dk.astype(bf16)
            stage[out.slot, 2, rows, :] = dv_s[rows, :].astype(bf16)
        dgq_ref[pl.ds(g, 1), :] += gq_acc
        dgk_ref[pl.ds(g, 1), :] += gk_acc
        for c in range(3):
            out.start(c, dst(c))
        out.end(dst)

    def slab(col0):
        return pl.BlockSpec((S, HEAD_DIM), lambda b, h, g: (b, col0 // HEAD_DIM + g * HEADS + h))

    gain = pl.BlockSpec((N_GROUPS, HEAD_DIM), lambda b, h, g: (0, 0))
    per_slot = pl.BlockSpec((S, HEAD_DIM), lambda b, h, g: (b, h))
    return pl.pallas_call(
        body, grid=(B, HEADS, N_GROUPS),
        in_specs=[slab(Q0), slab(K0), slab(V0), pl.BlockSpec((S, HEAD_DIM), lambda b, h, g: (b, ZA // HEAD_DIM + h)), gain, gain,
                  per_slot, per_slot, per_slot, pl.BlockSpec(memory_space=pl.ANY)],
        out_specs=[pl.BlockSpec(memory_space=pl.ANY), gain, gain],
        out_shape=[jax.ShapeDtypeStruct(dproj.shape, dproj.dtype), jax.ShapeDtypeStruct((N_GROUPS, HEAD_DIM), f32),
                   jax.ShapeDtypeStruct((N_GROUPS, HEAD_DIM), f32)],
        scratch_shapes=[pltpu.VMEM((S, HEAD_DIM), f32)] * 7 + [pltpu.VMEM((BLK, BLK), f32), pltpu.VMEM((BLK, 2 * BLK), f32),
                                                                pltpu.VMEM((2, 3, S, HEAD_DIM), bf16), pltpu.VMEM((S, HEAD_DIM), bf16),
                                                                pltpu.SemaphoreType.DMA((2, 3)), pltpu.SemaphoreType.DMA],
        input_output_aliases={9: 0},
        compiler_params=_params(("arbitrary", "arbitrary", "arbitrary")), name="attn_bwd",
    )(proj, proj, proj, proj, gq, gk, a_comb, lse, dag, dproj)


def _conv_fwd(proj, conv_w, B, S):
    T, NC = proj.shape
    tc = LANES

    def body(cb_ref, cc_ref, cv_ref, z_ref, w_ref, c_ref, ub):
        u = cc_ref[...] * cv_ref[...]
        ub[0:8, :] = jnp.zeros((8, tc), f32)
        ub[8:8 + S, :] = u
        w = w_ref[...]
        y = w[0:1] * u + w[1:2] * ub[pl.ds(7, S), :] + w[2:3] * ub[pl.ds(6, S), :]
        z = z_ref[...]
        c_ref[...] = (cb_ref[...] * y * (z * _sigmoid(z))).astype(bf16)

    def seg(col0):
        return pl.BlockSpec((S, tc), lambda j, b: (b, col0 // tc + j))

    return pl.pallas_call(
        body, grid=(CONV_W // tc, B),
        in_specs=[seg(CB), seg(CC), seg(CV), seg(ZC), pl.BlockSpec((3, tc), lambda j, b: (0, j))],
        out_specs=pl.BlockSpec((S, tc), lambda j, b: (b, j)),
        out_shape=jax.ShapeDtypeStruct((T, CONV_W), bf16),
        scratch_shapes=[pltpu.VMEM((S + 8, tc), f32)],
        compiler_params=_params(("parallel", "parallel")), name="conv_fwd",
    )(proj, proj, proj, proj, conv_w)


def _conv_bwd(proj, conv_w, dc, dproj, B, S):
    T, NC = proj.shape
    tc = LANES

    def body(cb_ref, cc_ref, cv_ref, z_ref, w_ref, dc_ref, dproj_in, dproj_ref, dw_ref, ub, db, stage, sem):
        j, b = pl.program_id(0), pl.program_id(1)

        def dst(c):
            return dproj_ref.at[pl.ds(b * S, S), pl.ds((CB, CC, CV, ZC)[c] + j * tc, tc)]

        out = _Deferred(stage, sem, j * B + b, (CONV_W // tc) * B - 1, 4)
        out.begin(dst)
        cb, cc, cv, z, dcv = cb_ref[...], cc_ref[...], cv_ref[...], z_ref[...], dc_ref[...]
        u = cc * cv
        ub[0:8, :] = jnp.zeros((8, tc), f32)
        ub[8:8 + S, :] = u
        u1, u2 = ub[pl.ds(7, S), :], ub[pl.ds(6, S), :]
        w = w_ref[...]
        y = w[0:1] * u + w[1:2] * u1 + w[2:3] * u2
        sg = _sigmoid(z)
        si = z * sg
        dyv = dcv * cb * si
        db[0:S, :] = dyv
        db[S:S + 8, :] = jnp.zeros((8, tc), f32)
        du = w[0:1] * dyv + w[1:2] * db[pl.ds(1, S), :] + w[2:3] * db[pl.ds(2, S), :]
        stage[out.slot, 0] = (dcv * y * si).astype(bf16)
        stage[out.slot, 1] = (du * cv).astype(bf16)
        stage[out.slot, 2] = (du * cc).astype(bf16)
        stage[out.slot, 3] = (dcv * cb * y * (sg * (1.0 + z * (1.0 - sg)))).astype(bf16)
        for c in range(4):
            out.start(c, dst(c))
        part = jnp.concatenate([jnp.sum(dyv * u, axis=0, keepdims=True), jnp.sum(dyv * u1, axis=0, keepdims=True),
                                jnp.sum(dyv * u2, axis=0, keepdims=True)], axis=0)

        @pl.when(b == 0)
        def _():
            dw_ref[...] = part

        @pl.when(b > 0)
        def _():
            dw_ref[...] += part

        out.end(dst)

    def seg(col0):
        return pl.BlockSpec((S, tc), lambda j, b: (b, col0 // tc + j))

    return pl.pallas_call(
        body, grid=(CONV_W // tc, B),
        in_specs=[seg(CB), seg(CC), seg(CV), seg(ZC), pl.BlockSpec((3, tc), lambda j, b: (0, j)),
                  pl.BlockSpec((S, tc), lambda j, b: (b, j)), pl.BlockSpec(memory_space=pl.ANY)],
        out_specs=[pl.BlockSpec(memory_space=pl.ANY), pl.BlockSpec((3, tc), lambda j, b: (0, j))],
        out_shape=[jax.ShapeDtypeStruct(dproj.shape, dproj.dtype), jax.ShapeDtypeStruct((3, CONV_W), f32)],
        scratch_shapes=[pltpu.VMEM((S + 8, tc), f32), pltpu.VMEM((S + 8, tc), f32), pltpu.VMEM((2, 4, S, tc), bf16),
                        pltpu.SemaphoreType.DMA((2, 4))],
        input_output_aliases={6: 0},
        compiler_params=_params(("arbitrary", "arbitrary")), name="conv_bwd",
    )(proj, proj, proj, proj, conv_w, dc, dproj)


MEM_CHUNK = 1024


def _mem_fwd(proj, mkv, gq, gk, B, S):
    T, NC = proj.shape
    scale = MEM_HD ** -0.5

    def body(q_ref, z_ref, k_ref, v_ref, gq_ref, gk_ref, o_ref):
        kv = k_ref[...]
        kn = (kv * _rstd_head(kv) * gk_ref[...]).astype(bf16)
        vv = v_ref[...].astype(bf16)

        def chunk(c, carry):
            rows = pl.ds(pl.multiple_of(c * MEM_CHUNK, MEM_CHUNK), MEM_CHUNK)
            q = q_ref[rows, :]
            qn = q * _rstd(q) * gq_ref[...]
            s = _dot(qn, kn, NT_DIMS) * scale
            p = jnp.exp(s - jnp.max(s, axis=-1, keepdims=True))
            p = p / jnp.sum(p, axis=-1, keepdims=True)
            z = z_ref[rows, :]
            o_ref[rows, :] = (_dot(p, vv, NN_DIMS) * (z * _sigmoid(z))).astype(bf16)
            return carry

        lax.fori_loop(0, S // MEM_CHUNK, chunk, 0)

    gain = pl.BlockSpec((1, MEM_HD), lambda b, h: (0, 0))
    return pl.pallas_call(
        body, grid=(B, MEM_HEADS),
        in_specs=[pl.BlockSpec((S, MEM_HD), lambda b, h: (b, MQ // MEM_HD + h)),
                  pl.BlockSpec((S, MEM_HD), lambda b, h: (b, ZM // MEM_HD + h)),
                  pl.BlockSpec((MEM_LEN, MEM_HD), lambda b, h: (b, h)),
                  pl.BlockSpec((MEM_LEN, MEM_HD), lambda b, h: (b, MEM_HEADS + h)), gain, gain],
        out_specs=pl.BlockSpec((S, MEM_HD), lambda b, h: (b, h)),
        out_shape=jax.ShapeDtypeStruct((T, MEM_W), bf16),
        compiler_params=_params(("parallel", "parallel")), name="mem_fwd",
    )(proj, proj, mkv, mkv, gq, gk)


def _mem_bwd(proj, mkv, gq, gk, dmo, dproj, B, S):
    T, NC = proj.shape
    scale = MEM_HD ** -0.5

    def body(q_ref, z_ref, k_ref, v_ref, gq_ref, gk_ref, dmo_ref, dproj_in, dproj_ref, dmk_ref, dmv_ref, dgq_ref, dgk_ref,
             dkn_s, dv_s, gq_s, stage, sem):
        b, h = pl.program_id(0), pl.program_id(1)

        def dst(c):
            return dproj_ref.at[pl.ds(b * S, S), pl.ds((MQ, ZM)[c] + h * MEM_HD, MEM_HD)]

        out = _Deferred(stage, sem, b * MEM_HEADS + h, B * MEM_HEADS - 1, 2)
        out.begin(dst)
        kv = k_ref[...]
        kn = (kv * _rstd_head(kv) * gk_ref[...]).astype(bf16)
        vv = v_ref[...].astype(bf16)
        dkn_s[...] = jnp.zeros_like(dkn_s)
        dv_s[...] = jnp.zeros_like(dv_s)
        gq_s[...] = jnp.zeros_like(gq_s)

        def chunk(c, carry):
            rows = pl.ds(pl.multiple_of(c * MEM_CHUNK, MEM_CHUNK), MEM_CHUNK)
            q = q_ref[rows, :]
            qn = q * _rstd(q) * gq_ref[...]
            s = _dot(qn, kn, NT_DIMS) * scale
            p = jnp.exp(s - jnp.max(s, axis=-1, keepdims=True))
            p = p / jnp.sum(p, axis=-1, keepdims=True)
            mo = _dot(p, vv, NN_DIMS)
            z, dg_ = z_ref[rows, :], dmo_ref[rows, :]
            sg = _sigmoid(z)
            do = dg_ * (z * sg)
            stage[out.slot, 1, rows, :] = (dg_ * mo * (sg * (1.0 + z * (1.0 - sg)))).astype(bf16)
            dp = _dot(do, vv, NT_DIMS)
            ds = p * (dp - jnp.sum(do * mo, axis=-1, keepdims=True)) * scale
            dq, gq_p = _rms_bwd_rows(q, gq_ref[...], _dot(ds, kn, NN_DIMS), on_mxu=False)
            stage[out.slot, 0, rows, :] = dq.astype(bf16)
            gq_s[...] += gq_p
            dkn_s[...] += _dot(ds, qn, TN_DIMS)
            dv_s[...] += _dot(p, do, TN_DIMS)
            return carry

        lax.fori_loop(0, S // MEM_CHUNK, chunk, 0)
        for c in range(2):
            out.start(c, dst(c))
        dk, gk_p = _rms_bwd_rows(kv, gk_ref[...], dkn_s[...])
        dmk_ref[...] = dk
        dmv_ref[...] = dv_s[...]

        @pl.when((b == 0) & (h == 0))
        def _():
            dgq_ref[...] = gq_s[...]
            dgk_ref[...] = gk_p

        @pl.when((b > 0) | (h > 0))
        def _():
            dgq_ref[...] += gq_s[...]
            dgk_ref[...] += gk_p

        out.end(dst)

    gain = pl.BlockSpec((1, MEM_HD), lambda b, h: (0, 0))
    kvb = pl.BlockSpec((MEM_LEN, MEM_HD), lambda b, h: (b, h))
    return pl.pallas_call(
        body, grid=(B, MEM_HEADS),
        in_specs=[pl.BlockSpec((S, MEM_HD), lambda b, h: (b, MQ // MEM_HD + h)),
                  pl.BlockSpec((S, MEM_HD), lambda b, h: (b, ZM // MEM_HD + h)),
                  kvb, pl.BlockSpec((MEM_LEN, MEM_HD), lambda b, h: (b, MEM_HEADS + h)), gain, gain,
                  pl.BlockSpec((S, MEM_HD), lambda b, h: (b, h)), pl.BlockSpec(memory_space=pl.ANY)],
        out_specs=[pl.BlockSpec(memory_space=pl.ANY), kvb, kvb, gain, gain],
        out_shape=[jax.ShapeDtypeStruct(dproj.shape, dproj.dtype), jax.ShapeDtypeStruct((B * MEM_LEN, MEM_W), f32),
                   jax.ShapeDtypeStruct((B * MEM_LEN, MEM_W), f32), jax.ShapeDtypeStruct((1, MEM_HD), f32),
                   jax.ShapeDtypeStruct((1, MEM_HD), f32)],
        scratch_shapes=[pltpu.VMEM((MEM_LEN, MEM_HD), f32), pltpu.VMEM((MEM_LEN, MEM_HD), f32), pltpu.VMEM((1, MEM_HD), f32),
                        pltpu.VMEM((2, 2, S, MEM_HD), bf16), pltpu.SemaphoreType.DMA((2, 2))],
        input_output_aliases={7: 0},
        compiler_params=_params(("arbitrary", "arbitrary")), name="mem_bwd",
    )(proj, proj, mkv, mkv, gq, gk, dmo, dproj)


def _merge_fwd(proj, ag, cg, mg, wa, wc, wm):
    T, NC = proj.shape
    D = wa.shape[1]
    tm, tn = _pick(T, 1024), _pick(D, 512)
    assert GT % tn == 0

    def body(ag_ref, cg_ref, mg_ref, wa_ref, wc_ref, wm_ref, g0_ref, g1_ref, g2_ref, mer_ref, ba_ref, bc_ref, bm_ref):
        ba = _dot(ag_ref[...], wa_ref[...], NN_DIMS)
        bc = _dot(cg_ref[...], wc_ref[...], NN_DIMS)
        bm = _dot(mg_ref[...], wm_ref[...], NN_DIMS)
        mer_ref[...] = (_sigmoid(g0_ref[...]) * ba + _sigmoid(g1_ref[...]) * bc + _sigmoid(g2_ref[...]) * bm).astype(bf16)
        ba_ref[...] = ba.astype(bf16)
        bc_ref[...] = bc.astype(bf16)
        bm_ref[...] = bm.astype(bf16)

    def act(w):
        return pl.BlockSpec((tm, w), lambda i, j: (i, 0))

    def wt(k):
        return pl.BlockSpec((k, tn), lambda i, j: (0, j))

    def gate(n):
        return pl.BlockSpec((tm, tn), lambda i, j: (i, (GT + n * D) // tn + j))

    out = pl.BlockSpec((tm, tn), lambda i, j: (i, j))
    return pl.pallas_call(
        body, grid=(T // tm, D // tn),
        in_specs=[act(ATTN_OUT), act(CONV_W), act(MEM_W), wt(ATTN_OUT), wt(CONV_W), wt(MEM_W), gate(0), gate(1), gate(2)],
        out_specs=[out] * 4, out_shape=[jax.ShapeDtypeStruct((T, D), bf16)] * 4,
        compiler_params=_params(("parallel", "parallel")), name="merge_fwd",
    )(ag, cg, mg, wa, wc, wm, proj, proj, proj)


def _out_loss(merged, w_out, x, tgt):
    T, D = x.shape
    tm, tn = _pick(T, 1024), _pick(D, 512)

    def body(m_ref, w_ref, x_ref, t_ref, dy_ref, dyb_ref, lp_ref):
        e = x_ref[...] + _dot(m_ref[...], w_ref[...], NN_DIMS) - t_ref[...]
        dy = e / D
        dy_ref[...] = dy
        dyb_ref[...] = dy.astype(bf16)
        part = jnp.full((1, 8, LANES), jnp.sum(e * e), f32)

        @pl.when(pl.program_id(1) == 0)
        def _():
            lp_ref[...] = part

        @pl.when(pl.program_id(1) > 0)
        def _():
            lp_ref[...] += part

    tile = pl.BlockSpec((tm, tn), lambda i, j: (i, j))
    return pl.pallas_call(
        body, grid=(T // tm, D // tn),
        in_specs=[pl.BlockSpec((tm, D), lambda i, j: (i, 0)), pl.BlockSpec((D, tn), lambda i, j: (0, j)), tile, tile],
        out_specs=[tile, tile, pl.BlockSpec((1, 8, LANES), lambda i, j: (i, 0, 0))],
        out_shape=[jax.ShapeDtypeStruct((T, D), f32), jax.ShapeDtypeStruct((T, D), bf16),
                   jax.ShapeDtypeStruct((T // tm, 8, LANES), f32)],
        compiler_params=_params(("parallel", "arbitrary")), name="out_loss",
    )(merged, w_out, x, tgt)


def _merge_bwd(proj, dyb, w_out, ba, bc, bm):
    T, NC = proj.shape
    D = w_out.shape[0]
    tm, tn = _pick(T, 1024), _pick(D, 512)
    assert GT % tn == 0

    def body(dy_ref, w_ref, ba_ref, bc_ref, bm_ref, g0_ref, g1_ref, g2_ref, da_ref, dc_ref, dm_ref, dproj_ref, stage, sem):
        i, j = pl.program_id(0), pl.program_id(1)

        def dst(n):
            return dproj_ref.at[pl.ds(i * tm, tm), pl.ds(GT + n * D + j * tn, tn)]

        out = _Deferred(stage, sem, i * (D // tn) + j, (T // tm) * (D // tn) - 1, 3)
        out.begin(dst)
        dmer = _dot(dy_ref[...], w_ref[...], NT_DIMS)
        for n, (g_ref, br_ref, o_ref) in enumerate(((g0_ref, ba_ref, da_ref), (g1_ref, bc_ref, dc_ref), (g2_ref, bm_ref, dm_ref))):
            sg = _sigmoid(g_ref[...])
            o_ref[...] = (sg * dmer).astype(bf16)
            stage[out.slot, n] = (dmer * br_ref[...].astype(f32) * (sg * (1.0 - sg))).astype(bf16)
            out.start(n, dst(n))
        out.end(dst)

    tile = pl.BlockSpec((tm, tn), lambda i, j: (i, j))

    def gate(n):
        return pl.BlockSpec((tm, tn), lambda i, j: (i, (GT + n * D) // tn + j))

    return pl.pallas_call(
        body, grid=(T // tm, D // tn),
        in_specs=[pl.BlockSpec((tm, D), lambda i, j: (i, 0)), pl.BlockSpec((tn, D), lambda i, j: (j, 0)), tile, tile, tile,
                  gate(0), gate(1), gate(2)],
        out_specs=[tile, tile, tile, pl.BlockSpec(memory_space=pl.ANY)],
        out_shape=[jax.ShapeDtypeStruct((T, D), bf16)] * 3 + [jax.ShapeDtypeStruct((T, NC), bf16)],
        scratch_shapes=[pltpu.VMEM((2, 3, tm, tn), bf16), pltpu.SemaphoreType.DMA((2, 3))],
        compiler_params=_params(("arbitrary", "arbitrary")), name="merge_bwd",
    )(dyb, w_out, ba, bc, bm, proj, proj, proj)


def _fwd_bwd_head(x2, mem2, t2, proj, attn, B, S, mem_norm_g, attn_q_norm, attn_k_norm, conv_w, mem_q_norm, mem_k_norm,
                  w_kv, w_a, w_c, w_m, w_out):
    D = x2.shape[1]
    mng = mem_norm_g.reshape(1, D)
    mqn, mkn = mem_q_norm.reshape(1, MEM_HD), mem_k_norm.reshape(1, MEM_HD)
    ag, a_comb, lse = attn

    mh = _rms_fwd(mem2, mng, "rms_mem")
    mkv = _mm(mh, w_kv, mode="nn", out_dtype=f32, name="mkv")
    cg = _conv_fwd(proj, conv_w, B, S)
    mg = _mem_fwd(proj, mkv, mqn, mkn, B, S)
    merged, ba, bc, bm = _merge_fwd(proj, ag, cg, mg, w_a, w_c, w_m)
    dy, dyb, lp = _out_loss(merged, w_out, x2, t2)
    sq_err = jnp.sum(lp[:, 0, 0])

    g = {}
    g["w_out"] = _mm(merged, dyb, mode="tn", out_dtype=f32, name="dw_out")
    dba, dbc, dbm, dproj = _merge_bwd(proj, dyb, w_out, ba, bc, bm)
    g["w_br_attn"] = _mm(ag, dba, mode="tn", out_dtype=f32, name="dw_br_attn")
    g["w_br_conv"] = _mm(cg, dbc, mode="tn", out_dtype=f32, name="dw_br_conv")
    g["w_br_mem"] = _mm(mg, dbm, mode="tn", out_dtype=f32, name="dw_br_mem")
    dag = _mm(dba, w_a, mode="nt", out_dtype=f32, name="d_attn_out")
    dcg = _mm(dbc, w_c, mode="nt", out_dtype=f32, name="d_conv_out")
    dmg = _mm(dbm, w_m, mode="nt", out_dtype=f32, name="d_mem_out")
    dproj, g["attn_q_norm"], g["attn_k_norm"] = _attn_bwd(proj, attn_q_norm, attn_k_norm, a_comb, lse, dag, dproj, B, S)
    dproj, g["conv_w"] = _conv_bwd(proj, conv_w, dcg, dproj, B, S)
    dproj, dmk, dmv, dgmq, dgmk = _mem_bwd(proj, mkv, mqn, mkn, dmg, dproj, B, S)
    g["mem_q_norm"], g["mem_k_norm"] = dgmq.reshape(MEM_HD), dgmk.reshape(MEM_HD)
    dmkv = jnp.concatenate([dmk, dmv], axis=1)
    dmh = _mm(dmkv, w_kv, mode="nt", out_dtype=f32, name="d_mem_h")
    g["mem_norm_g"] = _rms_bwd(mem2, mng, dmh, None, "rms_mem_bwd").reshape(D)
    return sq_err, g, dict(dy=dy, dproj=dproj, mh=mh, dmkv=dmkv)


MESH = pl.DeviceIdType.MESH
HBM = pl.BlockSpec(memory_space=pl.ANY)


def _me():
    x, y, c = lax.axis_index("x"), lax.axis_index("y"), lax.axis_index("c")
    return (x, y, c), 4 * x + 2 * y + c


def _peer(k):
    (x, y, c), _ = _me()
    p = (1 - x if k & 4 else x, 1 - y if k & 2 else y, 1 - c if k & 1 else c)
    return p, 4 * p[0] + 2 * p[1] + p[2]


def _window(ref, axis, size, idx):
    if axis is None:
        return ref.at[idx]
    if axis == 0:
        return ref.at[pl.ds(idx * size, size), :]
    return ref.at[:, pl.ds(idx * size, size)]


def _full_shape(s, axis):
    if axis is None:
        return (N_DEV,) + s.shape
    return tuple(N_DEV * d if i == axis else d for i, d in enumerate(s.shape))


OTHER_CHIPS = (4, 2, 6)


def _spread_comm(shards, axes):
    n = len(shards)

    def copies(ins, outs, send_sems, recv_sems, base=0):
        _, me = _me()
        out = []
        for a in range(n):
            mine = _window(outs[a], axes[a], None if axes[a] is None else shards[a].shape[axes[a]], me)
            out.append(pltpu.make_async_copy(ins[a], mine, send_sems.at[base + a, N_CHIP]))
            for k, dist in enumerate((1,) + OTHER_CHIPS):
                dev, _ = _peer(dist)
                out.append(pltpu.make_async_remote_copy(
                    src_ref=ins[a], dst_ref=mine, send_sem=send_sems.at[base + a, k], recv_sem=recv_sems.at[base + a, k],
                    device_id=dev, device_id_type=MESH))
        return out

    shapes = [jax.ShapeDtypeStruct(_full_shape(s, ax), s.dtype) for s, ax in zip(shards, axes)]
    return _Comm(shards, shapes, (n, N_CHIP + 1), copies)


def _forward_blocks(fulls, axes):
    n = len(fulls)
    sizes = [None if ax is None else f.shape[ax] // N_DEV for f, ax in zip(fulls, axes)]

    def body(*refs):
        outs = refs[n:2 * n]
        send_sems, recv_sems = refs[2 * n:]
        sibling, _ = _peer(1)
        copies = []
        for j, dist in enumerate(OTHER_CHIPS):
            _, held = _peer(dist)
            for a in range(n):
                block = _window(outs[a], axes[a], sizes[a], held)
                copies.append(pltpu.make_async_remote_copy(
                    src_ref=block, dst_ref=block, send_sem=send_sems.at[a, j], recv_sem=recv_sems.at[a, j],
                    device_id=sibling, device_id_type=MESH))
        for cp in copies:
            cp.start()
        for cp in copies:
            cp.wait()

    return pl.pallas_call(
        body, in_specs=[HBM] * n, out_specs=[HBM] * n,
        out_shape=[jax.ShapeDtypeStruct(f.shape, f.dtype) for f in fulls],
        scratch_shapes=[pltpu.SemaphoreType.DMA((n, len(OTHER_CHIPS))), pltpu.SemaphoreType.DMA((n, len(OTHER_CHIPS)))],
        input_output_aliases={a: a for a in range(n)},
        name="forward_blocks",
    )(*fulls)


def _shard_order():
    (x, y, c), me = _me()
    xs, ys, xo, yo = _peer(4)[1], _peer(2)[1], _peer(5)[1], _peer(3)[1]
    north = c == 1
    ids = [me, _peer(1)[1], jnp.where(north, xs, ys), jnp.where(north, yo, xo), jnp.where(north, ys, xs),
           jnp.where(north, xo, yo), _peer(6)[1], _peer(7)[1]]
    return jnp.stack(ids).astype(jnp.int32)


def _proj_gather(h, w_shard, order, comm, comm_at):
    T, K = h.shape
    C = w_shard.shape[1]
    tm = _pick(T, 1024)
    nm = T // tm
    ahead = max(nm - 2, 0)
    n_ci, n_co = len(comm.ins), len(comm.out_shapes)

    def body(*refs):
        order_ref, h_ref, ws_ref = refs[:3]
        c_ins = refs[3:3 + n_ci]
        o_ref, wf_ref = refs[3 + n_ci:5 + n_ci]
        c_outs = refs[5 + n_ci:5 + n_ci + n_co]
        wbuf, send_sems, recv_sems, own_sem, buf_sems, c_send, c_recv = refs[5 + n_ci + n_co:]
        t, i = pl.program_id(0), pl.program_id(1)

        @pl.when((t == comm_at) & (i == 0))
        def _():
            for cp in comm.copies(c_ins, c_outs, c_send, c_recv):
                cp.start()
        (x, y, c), me = _me()
        sibling, sib_id = _peer(1)
        chips = [_peer(dist) for dist in OTHER_CHIPS]

        def win(idx):
            return wf_ref.at[:, pl.ds(idx * C, C)]

        def copy(k, block, to, src=None):
            return pltpu.make_async_remote_copy(
                src_ref=win(block) if src is None else src, dst_ref=win(block),
                send_sem=send_sems.at[k], recv_sem=recv_sems.at[k], device_id=to, device_id_type=MESH)

        def fetch(tt, src):
            return pltpu.make_async_copy(src, wbuf.at[tt % 2], buf_sems.at[tt % 2])

        own = pltpu.make_async_copy(ws_ref, win(me), own_sem)
        (x_nbr, x_id), (y_nbr, y_id), (_, d_id) = chips

        def half(k, block, top, to):
            rows = wf_ref.at[pl.ds(0 if top else K // 2, K // 2), pl.ds(block * C, C)]
            return pltpu.make_async_remote_copy(src_ref=rows, dst_ref=rows, send_sem=send_sems.at[k], recv_sem=recv_sems.at[k],
                                                device_id=to, device_id_type=MESH)

        here = (x, y, c)

        def pass_on(from_x):
            if from_x:
                copy(4, x_id, sibling).start()
                half(8, x_id, False, y_nbr).start()
            else:
                copy(5, y_id, sibling).start()
                half(7, y_id, True, x_nbr).start()

        @pl.when((t == 0) & (i == 0))
        def _():
            fetch(0, ws_ref).start()
            own.start()
            copy(0, me, sibling, src=ws_ref).start()

        for tt in range(N_DEV):
            @pl.when((t == tt) & (i == 0))
            def _(tt=tt):
                fetch(tt, ws_ref).wait()

        o_ref[...] = _dot(h_ref[...], wbuf[t % 2], NN_DIMS)

        def schedule(first_x):
            on = c == (1 if first_x else 0)
            (f_dev, f_id), (g_dev, g_id) = ((x_nbr, x_id), (y_nbr, y_id)) if first_x else ((y_nbr, y_id), (x_nbr, x_id))
            kf, kg = (1, 2) if first_x else (2, 1)
            pf, pg = (4, 5) if first_x else (5, 4)

            @pl.when((t == 0) & (i == 0) & on)
            def _():
                copy(kf, me, f_dev, src=ws_ref).start()

            def event(nxt):
                if nxt == 1:
                    copy(0, sib_id, here).wait_recv()
                    return sib_id
                if nxt == 2:
                    copy(kf, f_id, here).wait_recv()
                    copy(kf, me, f_dev, src=ws_ref).wait_send()
                    copy(kg, me, g_dev, src=ws_ref).start()
                    pass_on(first_x)
                    return f_id
                if nxt == 3:
                    copy(pg, g_id + 1 - 2 * c, here).wait_recv()
                    return g_id + 1 - 2 * c
                if nxt == 4:
                    copy(kg, g_id, here).wait_recv()
                    pass_on(not first_x)
                    return g_id
                if nxt == 5:
                    copy(pf, f_id + 1 - 2 * c, here).wait_recv()
                    return f_id + 1 - 2 * c
                if nxt == 6:
                    half(7, d_id, True, here).wait_recv()
                    half(8, d_id, False, here).wait_recv()
                    copy(6, d_id, sibling).start()
                    return d_id
                copy(6, d_id + 1 - 2 * c, here).wait_recv()
                return d_id + 1 - 2 * c

            for tt in range(N_DEV - 1):
                @pl.when((t == tt) & (i == ahead) & on)
                def _(tt=tt):
                    fetch(tt + 1, win(event(tt + 1))).start()

            @pl.when((t == N_DEV - 1) & (i == nm - 1) & on)
            def _():
                copy(kg, me, g_dev, src=ws_ref).wait_send()

        schedule(True)
        schedule(False)

        @pl.when((t == N_DEV - 1) & (i == nm - 1))
        def _():
            copy(0, me, sibling, src=ws_ref).wait_send()
            half(7, y_id, True, x_nbr).wait_send()
            half(8, x_id, False, y_nbr).wait_send()
            for j, (_, dev_id) in enumerate(chips):
                copy(4 + j, dev_id, sibling).wait_send()
            own.wait()
            for cp in comm.copies(c_ins, c_outs, c_send, c_recv):
                cp.wait()

    hbm = pl.BlockSpec(memory_space=pl.ANY)
    res = pl.pallas_call(
        body,
        grid_spec=pltpu.PrefetchScalarGridSpec(
            num_scalar_prefetch=1, grid=(N_DEV, nm),
            in_specs=[pl.BlockSpec((tm, K), lambda t, i, order_ref: (i, 0)), hbm] + [hbm] * n_ci,
            out_specs=[pl.BlockSpec((tm, C), lambda t, i, order_ref: (i, order_ref[t])), hbm] + [hbm] * n_co,
            scratch_shapes=[pltpu.VMEM((2, K, C), bf16), pltpu.SemaphoreType.DMA((9,)), pltpu.SemaphoreType.DMA((9,)),
                            pltpu.SemaphoreType.DMA, pltpu.SemaphoreType.DMA((2,))] + comm.scratch()),
        out_shape=[jax.ShapeDtypeStruct((T, N_DEV * C), f32), jax.ShapeDtypeStruct((K, N_DEV * C), bf16)] + comm.out_shapes,
        compiler_params=_params(("arbitrary", "arbitrary")), name="proj_gather",
    )(order, h, w_shard, *comm.ins)
    return res[0], res[1], list(res[2:])


N_CHIP = 4


def _shard_shape(g, ax):
    return tuple(d // N_DEV if i == ax else d for i, d in enumerate(g.shape))


def _sibling_comm(grads, axes):
    n = len(grads)
    sizes = [g.shape[ax] // N_DEV for g, ax in zip(grads, axes)]

    def copies(ins, lands, send_sems, recv_sems, base=0):
        sibling, _ = _peer(1)
        out = []
        for j in range(N_CHIP):
            _, owner = _peer(2 * j + 1)
            for a in range(n):
                out.append(pltpu.make_async_remote_copy(
                    src_ref=_window(ins[a], axes[a], sizes[a], owner), dst_ref=lands[a].at[j],
                    send_sem=send_sems.at[base + a, j], recv_sem=recv_sems.at[base + a, j], device_id=sibling,
                    device_id_type=MESH))
        return out

    shapes = [jax.ShapeDtypeStruct((N_CHIP,) + _shard_shape(g, ax), g.dtype) for g, ax in zip(grads, axes)]
    return _Comm(grads, shapes, (n, N_CHIP), copies)


def _chips_comm(sums):
    n = len(sums)

    def copies(ins, lands, send_sems, recv_sems, base=0):
        out = []
        for j in range(1, N_CHIP):
            dev, _ = _peer(2 * j)
            for a in range(n):
                out.append(pltpu.make_async_remote_copy(
                    src_ref=ins[a].at[j - 1], dst_ref=lands[a].at[j - 1],
                    send_sem=send_sems.at[base + a, j - 1], recv_sem=recv_sems.at[base + a, j - 1], device_id=dev,
                    device_id_type=MESH))
        return out

    return _Comm(sums, [jax.ShapeDtypeStruct(s.shape, s.dtype) for s in sums], (n, N_CHIP), copies)


def _join(c1, c2):
    n1, m1, k1 = len(c1.ins), len(c1.out_shapes), c1.sem_shape[0]

    def copies(ins, lands, send_sems, recv_sems):
        return (c1.copies(ins[:n1], lands[:m1], send_sems, recv_sems, base=0)
                + c2.copies(ins[n1:], lands[m1:], send_sems, recv_sems, base=k1))

    return _Comm(c1.ins + c2.ins, c1.out_shapes + c2.out_shapes, (k1 + c2.sem_shape[0], N_CHIP), copies)


def _rs_chip_sum(grad, land, xyc, axis, name):
    R, C = land.shape[1:]
    tr = _pick(R, max(8, (640 * 1024) // C), 8)

    def body(xyc_ref, g_ref, l_ref, o_ref):
        o_ref[0] = (g_ref[...] + l_ref[0]).astype(bf16)

    def owner(j, xyc_ref):
        jj = j + 1
        return 4 * (xyc_ref[0] ^ (jj >> 1)) + 2 * (xyc_ref[1] ^ (jj & 1)) + xyc_ref[2]

    if axis == 0:
        g_spec = pl.BlockSpec((tr, C), lambda j, i, xyc_ref: (owner(j, xyc_ref) * (R // tr) + i, 0))
    else:
        g_spec = pl.BlockSpec((tr, C), lambda j, i, xyc_ref: (i, owner(j, xyc_ref)))
    return pl.pallas_call(
        body,
        grid_spec=pltpu.PrefetchScalarGridSpec(
            num_scalar_prefetch=1, grid=(N_CHIP - 1, R // tr),
            in_specs=[g_spec, pl.BlockSpec((1, tr, C), lambda j, i, xyc_ref: (j + 1, i, 0))],
            out_specs=pl.BlockSpec((1, tr, C), lambda j, i, xyc_ref: (j, i, 0))),
        out_shape=jax.ShapeDtypeStruct((N_CHIP - 1, R, C), bf16),
        compiler_params=_params(("parallel", "parallel")), name=name,
    )(xyc, grad, land)


def _allreduce_small(part):
    R = part.shape[0]

    def body(p_ref, o_ref, buf, send_sems, recv_sems):
        (x, y, c), me = _me()
        buf[me] = p_ref[...]
        copies = []
        for k in range(1, N_DEV):
            dev, dev_id = _peer(k)
            copies.append(pltpu.make_async_remote_copy(
                src_ref=p_ref, dst_ref=buf.at[me], send_sem=send_sems.at[k - 1], recv_sem=recv_sems.at[k - 1],
                device_id=dev, device_id_type=MESH))
        for cp in copies:
            cp.start()
        for k in range(1, N_DEV):
            _, dev_id = _peer(k)
            pltpu.make_async_remote_copy(
                src_ref=p_ref, dst_ref=buf.at[dev_id], send_sem=send_sems.at[k - 1], recv_sem=recv_sems.at[k - 1],
                device_id=(x, y, c), device_id_type=MESH).wait_recv()
        for cp in copies:
            cp.wait_send()
        acc = buf[0]
        for d in range(1, N_DEV):
            acc = acc + buf[d]
        o_ref[...] = acc

    return pl.pallas_call(
        body, in_specs=[pl.BlockSpec(memory_space=pltpu.VMEM)], out_specs=pl.BlockSpec(memory_space=pltpu.VMEM),
        out_shape=jax.ShapeDtypeStruct((R, LANES), f32),
        scratch_shapes=[pltpu.VMEM((N_DEV, R, LANES), f32), pltpu.SemaphoreType.DMA((N_DEV - 1,)), pltpu.SemaphoreType.DMA((N_DEV - 1,))],
        name="allreduce_small",
    )(part)


def _adamw_math(w, g, m, v):
    m2 = ADAM_B1 * m + (1.0 - ADAM_B1) * g
    v2 = ADAM_B2 * v + (1.0 - ADAM_B2) * (g * g)
    m_hat = m2 / (1.0 - ADAM_B1 ** ADAM_STEP)
    v_hat = v2 / (1.0 - ADAM_B2 ** ADAM_STEP)
    return -ADAM_LR * (m_hat / (jnp.sqrt(v_hat) + ADAM_EPS) + ADAM_WD * w), m2, v2


def _adamw_shard(grad, land_sib, land_chips, w, m, v, me, axis, name, row0=0, prev=None):
    R, C = land_sib.shape[1:]
    assert axis == 1 or R == w.shape[0]
    tr = _pick(R, max(8, (320 * 1024) // C), 8)
    r0 = row0 // tr
    n_prev = len(prev) if prev else 0

    def body(me_ref, g_ref, s_ref, l_ref, w_ref, m_ref, v_ref, *rest):
        go_ref, d_ref, mo_ref, vo_ref = rest[n_prev:]
        g = g_ref[...] + s_ref[0]
        for j in range(N_CHIP - 1):
            g = g + l_ref[j].astype(f32)
        d, m2, v2 = _adamw_math(w_ref[...], g, m_ref[...], v_ref[...])
        go_ref[...] = g
        d_ref[...] = d
        mo_ref[...] = m2
        vo_ref[...] = v2

    if axis == 0:
        g_spec = pl.BlockSpec((tr, C), lambda i, me_ref: (me_ref[0] * (R // tr) + i, 0))
    else:
        g_spec = pl.BlockSpec((tr, C), lambda i, me_ref: (i, me_ref[0]))
    blk = pl.BlockSpec((tr, C), lambda i, me_ref: (r0 + i, 0))
    return pl.pallas_call(
        body,
        grid_spec=pltpu.PrefetchScalarGridSpec(
            num_scalar_prefetch=1, grid=(R // tr,),
            in_specs=[g_spec, pl.BlockSpec((1, tr, C), lambda i, me_ref: (0, i, 0)),
                      pl.BlockSpec((N_CHIP - 1, tr, C), lambda i, me_ref: (0, i, 0)), blk, blk, blk]
            + [pl.BlockSpec(memory_space=pl.ANY)] * n_prev,
            out_specs=[blk] * 4),
        out_shape=[jax.ShapeDtypeStruct(w.shape, f32)] * 4,
        input_output_aliases={7 + i: i for i in range(n_prev)},
        compiler_params=_params(("parallel",)), name=name,
    )(me, grad, land_sib, land_chips, w, m, v, *(prev or []))


def _adamw_small(g, w, m, v):
    def body(g_ref, w_ref, m_ref, v_ref, d_ref, mo_ref, vo_ref):
        d, m2, v2 = _adamw_math(w_ref[...], g_ref[...], m_ref[...], v_ref[...])
        d_ref[...] = d
        mo_ref[...] = m2
        vo_ref[...] = v2

    return pl.pallas_call(body, out_shape=[jax.ShapeDtypeStruct(g.shape, f32)] * 3, name="adamw_small")(g, w, m, v)


def _pack_rows(parts, total_rows):
    rows = jnp.concatenate([p.reshape(-1, LANES) for p in parts], axis=0)
    return jnp.pad(rows, ((0, total_rows - rows.shape[0]), (0, 0)))


BIG = ("w_in", "mem_w_kv", "w_br_attn", "w_br_conv", "w_br_mem", "w_out")
BIG_AXIS = {"w_in": 1, "mem_w_kv": 0, "w_br_attn": 1, "w_br_conv": 1, "w_br_mem": 1, "w_out": 0}
SMALL = ("norm_g", "mem_norm_g", "attn_q_norm", "attn_k_norm", "mem_q_norm", "mem_k_norm")
ALL_W = ("norm_g", "mem_norm_g", "w_in", "attn_q_norm", "attn_k_norm", "conv_w", "mem_w_kv", "mem_q_norm", "mem_k_norm",
         "w_br_attn", "w_br_conv", "w_br_mem", "w_out")


def kernel(x, mem, norm_g, mem_norm_g, w_in, attn_q_norm, attn_k_norm, conv_w, mem_w_kv, mem_q_norm, mem_k_norm, w_br_attn, w_br_conv, w_br_mem, w_out, loss_target, m_norm_g, m_mem_norm_g, m_w_in, m_attn_q_norm, m_attn_k_norm, m_conv_w, m_mem_w_kv, m_mem_q_norm, m_mem_k_norm, m_w_br_attn, m_w_br_conv, m_w_br_mem, m_w_out, v_norm_g, v_mem_norm_g, v_w_in, v_attn_q_norm, v_attn_k_norm, v_conv_w, v_mem_w_kv, v_mem_q_norm, v_mem_k_norm, v_w_br_attn, v_w_br_conv, v_w_br_mem, v_w_out):
    w = dict(norm_g=norm_g, mem_norm_g=mem_norm_g, w_in=w_in, attn_q_norm=attn_q_norm, attn_k_norm=attn_k_norm, conv_w=conv_w,
             mem_w_kv=mem_w_kv, mem_q_norm=mem_q_norm, mem_k_norm=mem_k_norm, w_br_attn=w_br_attn, w_br_conv=w_br_conv,
             w_br_mem=w_br_mem, w_out=w_out)
    mo = dict(norm_g=m_norm_g, mem_norm_g=m_mem_norm_g, w_in=m_w_in, attn_q_norm=m_attn_q_norm, attn_k_norm=m_attn_k_norm,
              conv_w=m_conv_w, mem_w_kv=m_mem_w_kv, mem_q_norm=m_mem_q_norm, mem_k_norm=m_mem_k_norm, w_br_attn=m_w_br_attn,
              w_br_conv=m_w_br_conv, w_br_mem=m_w_br_mem, w_out=m_w_out)
    vo = dict(norm_g=v_norm_g, mem_norm_g=v_mem_norm_g, w_in=v_w_in, attn_q_norm=v_attn_q_norm, attn_k_norm=v_attn_k_norm,
              conv_w=v_conv_w, mem_w_kv=v_mem_w_kv, mem_q_norm=v_mem_q_norm, mem_k_norm=v_mem_k_norm, w_br_attn=v_w_br_attn,
              w_br_conv=v_w_br_conv, w_br_mem=v_w_br_mem, w_out=v_w_out)
    B, S, D = x.shape
    me = (4 * lax.axis_index("x") + 2 * lax.axis_index("y") + lax.axis_index("c")).astype(jnp.int32)

    T = B * S
    x2, t2, mem2, ng = x.reshape(T, D), loss_target.reshape(T, D), mem.reshape(B * MEM_LEN, D), norm_g.reshape(1, D)
    h = _rms_fwd(x2, ng, "rms_x")
    rows_sharded, cols_sharded = ("mem_w_kv", "w_out"), ("w_br_attn", "w_br_conv", "w_br_mem")
    later = rows_sharded + cols_sharded
    later_axes = [BIG_AXIS[n] for n in later] + [None]
    conv_pad = jnp.pad(conv_w, ((0, 8 - conv_w.shape[0]), (0, 0)))
    proj, w_in_full, spread_a = _proj_gather(
        h, w_in.astype(bf16), _shard_order(),
        _spread_comm([w[n].astype(bf16) for n in rows_sharded], [BIG_AXIS[n] for n in rows_sharded]), comm_at=N_DEV - 3)
    *attn, spread_b = _attn_fwd(proj, attn_q_norm, attn_k_norm, B, S,
                                comm=_spread_comm([w[n].astype(bf16) for n in cols_sharded] + [conv_pad],
                                                  [BIG_AXIS[n] for n in cols_sharded] + [None]))
    *fulls, conv_g = _forward_blocks(spread_a + spread_b, later_axes)
    wf = dict(zip(later, fulls), w_in=w_in_full)
    conv_full = conv_g[:, :3, :].transpose(1, 0, 2).reshape(3, CONV_W)

    sq_err, g, t = _fwd_bwd_head(x2, mem2, t2, proj, attn, B, S, mem_norm_g, attn_q_norm, attn_k_norm, conv_full, mem_q_norm,
                                 mem_k_norm, wf["mem_w_kv"], wf["w_br_attn"], wf["w_br_conv"], wf["w_br_mem"], wf["w_out"])
    t.update(x2=x2, ng=ng, h=h)

    xyc = jnp.stack([lax.axis_index("x"), lax.axis_index("y"), lax.axis_index("c")]).astype(jnp.int32)
    me1 = me.reshape(1)
    early = ("w_out", "w_br_attn", "w_br_conv", "w_br_mem")
    g["mem_w_kv"], sib_early = _mm(t["mh"], t["dmkv"], mode="tn", out_dtype=f32, name="dw_kv",
                                   comm=_sibling_comm([g[n] for n in early], [BIG_AXIS[n] for n in early]))
    sums_early = [_rs_chip_sum(g[n], ls, xyc, BIG_AXIS[n], "rs_sum_" + n) for n, ls in zip(early, sib_early)]
    tm_w = _pick(D // 2, 1024)
    half = (D // 2) // tm_w
    g_top, (*chips_early, sib_kv) = _mm(t["h"], t["dproj"], mode="tn", out_dtype=f32, name="dw_in_top", tm=tm_w,
                                        m_tiles=(0, half),
                                        comm=_join(_chips_comm(sums_early), _sibling_comm([g["mem_w_kv"]], [0])))
    sum_kv = _rs_chip_sum(g["mem_w_kv"], sib_kv, xyc, 0, "rs_sum_mem_w_kv")
    g_bot, (chips_kv, sib_top) = _mm(t["h"], t["dproj"], mode="tn", out_dtype=f32, name="dw_in_bot", tm=tm_w,
                                     m_tiles=(half, half), comm=_join(_chips_comm([sum_kv]), _sibling_comm([g_top], [1])))
    sum_top = _rs_chip_sum(g_top, sib_top, xyc, 1, "rs_sum_w_in_top")
    tm_t = _pick(T // 2, 1024)
    halfm = (T // 2) // tm_t
    dh, (chips_top, sib_bot) = _mm(t["dproj"], wf["w_in"], mode="nt", out_dtype=f32, name="d_h_a", tm=tm_t, tk=D_H_TK,
                                   m_tiles=(0, halfm), into="new",
                                   comm=_join(_chips_comm([sum_top]), _sibling_comm([g_bot], [1])))
    sum_bot = _rs_chip_sum(g_bot, sib_bot, xyc, 1, "rs_sum_w_in_bot")
    dh, (chips_bot,) = _mm(t["dproj"], wf["w_in"], mode="nt", out_dtype=f32, name="d_h_b", tm=tm_t, tk=D_H_TK, m_tiles=(halfm, halfm),
                           into=dh, comm=_chips_comm([sum_bot]))
    dx, dng = _rms_bwd(t["x2"], t["ng"], dh, t["dy"], "rms_x_bwd")
    g["norm_g"] = dng.reshape(D)

    grad, delta, new_m, new_v = {}, {}, {}, {}
    for n, ls, lc in zip(early + ("mem_w_kv",), sib_early + [sib_kv], chips_early + [chips_kv]):
        grad[n], delta[n], new_m[n], new_v[n] = _adamw_shard(g[n], ls, lc, w[n], mo[n], vo[n], me1, BIG_AXIS[n], "adamw_" + n)
    top = _adamw_shard(g_top, sib_top, chips_top, w_in, m_w_in, v_w_in, me1, 1, "adamw_w_in_top")
    grad["w_in"], delta["w_in"], new_m["w_in"], new_v["w_in"] = _adamw_shard(
        g_bot, sib_bot, chips_bot, w_in, m_w_in, v_w_in, me1, 1, "adamw_w_in_bot", row0=D // 2, prev=top)

    n_rep = sum(w[n].size for n in SMALL) // LANES
    conv_rows = g["conv_w"].reshape(3, N_DEV, LANES).transpose(1, 0, 2).reshape(3 * N_DEV, LANES)
    n_rows = -(-(n_rep + 3 * N_DEV + 1) // 8) * 8
    part = _pack_rows([g[n] for n in SMALL] + [conv_rows, jnp.full((1, LANES), sq_err, f32)], n_rows)
    tot = _allreduce_small(part)
    loss = tot[n_rep + 3 * N_DEV, 0] * (0.5 / D)
    n_small = -(-(n_rep + 3) // 8) * 8
    g_small = _pack_rows([tot[:n_rep], lax.dynamic_slice(tot, (n_rep + 3 * me, 0), (3, LANES))], n_small)
    names = SMALL + ("conv_w",)
    d_s, m_s, v_s = _adamw_small(g_small, _pack_rows([w[n] for n in names], n_small), _pack_rows([mo[n] for n in names], n_small),
                                 _pack_rows([vo[n] for n in names], n_small))
    off = 0
    for n in names:
        r = w[n].size // LANES
        grad[n], delta[n] = g_small[off:off + r].reshape(w[n].shape), d_s[off:off + r].reshape(w[n].shape)
        new_m[n], new_v[n] = m_s[off:off + r].reshape(w[n].shape), v_s[off:off + r].reshape(w[n].shape)
        off += r
    return (loss, dx.reshape(B, S, D), *[grad[n] for n in ALL_W], *[delta[n] for n in ALL_W], *[new_m[n] for n in ALL_W],
            *[new_v[n] for n in ALL_W])
```

```python
import functools

import jax
import jax.numpy as jnp
from jax import lax
from jax.experimental import pallas as pl
from jax.experimental.pallas import tpu as pltpu

f32 = jnp.float32
bf16 = jnp.bfloat16

N_DEV = 8
HEAD_DIM = 128
DILATIONS = (1, 4, 16)
N_GROUPS = 3
HEADS = 4
BLK = 128
ATTN_QKV = N_GROUPS * HEADS * HEAD_DIM
ATTN_OUT = HEADS * HEAD_DIM
CONV_W = 1024
MEM_LEN = 256
MEM_HEADS = 4
MEM_HD = 256
MEM_W = MEM_HEADS * MEM_HD
EPS = 1e-6
Q0, K0, V0 = 0, ATTN_QKV, 2 * ATTN_QKV
ZA = 3 * ATTN_QKV
CB, CC, CV, ZC = ZA + ATTN_OUT, ZA + ATTN_OUT + CONV_W, ZA + ATTN_OUT + 2 * CONV_W, ZA + ATTN_OUT + 3 * CONV_W
MQ = ZC + CONV_W
ZM = MQ + MEM_W
GT = ZM + MEM_W

ADAM_LR, ADAM_B1, ADAM_B2, ADAM_EPS, ADAM_WD, ADAM_STEP = 0.001, 0.9, 0.999, 1e-08, 0.01, 10

VMEM_LIMIT = 56 * 1024 * 1024
D_H_TK = 4352
LANES = 128

NT_DIMS = (((1,), (1,)), ((), ()))
TN_DIMS = (((0,), (0,)), ((), ()))
NN_DIMS = (((1,), (0,)), ((), ()))


def _params(sem=None):
    return pltpu.CompilerParams(dimension_semantics=sem, vmem_limit_bytes=VMEM_LIMIT)


def _pick(n, pref, q=LANES):
    t = (min(pref, n) // q) * q
    while t >= q:
        if n % t == 0:
            return t
        t -= q
    return n


def _dot(a, b, dims):
    return lax.dot_general(a.astype(bf16), b.astype(bf16), dims, preferred_element_type=f32)


def _sigmoid(z):
    return 0.5 * jnp.tanh(0.5 * z) + 0.5


def _rstd(v):
    return lax.rsqrt(jnp.mean(v * v, axis=-1, keepdims=True) + EPS)


class _Deferred:
    def __init__(self, stage, sems, step, last, n):
        assert last >= 1
        self.stage, self.sems, self.step, self.last, self.n = stage, sems, step, last, n
        self.slot = step % 2

    def _drain(self, slot, like):
        for c in range(self.n):
            pltpu.make_async_copy(self.stage.at[slot, c], like(c), self.sems.at[slot, c]).wait()

    def begin(self, like):
        pl.when(self.step >= 2)(lambda: self._drain(self.slot, like))

    def start(self, c, dst):
        pltpu.make_async_copy(self.stage.at[self.slot, c], dst, self.sems.at[self.slot, c]).start()

    def end(self, like):
        @pl.when(self.step == self.last)
        def _():
            self._drain(self.slot, like)
            self._drain(1 - self.slot, like)


def _row_sum(v):
    hi = v.astype(bf16)
    lo = (v - hi.astype(f32)).astype(bf16)
    ones = jnp.ones((v.shape[1], v.shape[1]), bf16)
    return _dot(hi, ones, NN_DIMS) + _dot(lo, ones, NN_DIMS)


def _rstd_head(v):
    return lax.rsqrt(_row_sum(v * v) * (1.0 / v.shape[1]) + EPS)


class _Comm:
    def __init__(self, ins, out_shapes, sem_shape, copies, aliases=None):
        self.ins, self.out_shapes, self.sem_shape, self.copies = list(ins), list(out_shapes), sem_shape, copies
        self.aliases = dict(aliases or {})

    def scratch(self):
        return [pltpu.SemaphoreType.DMA(self.sem_shape), pltpu.SemaphoreType.DMA(self.sem_shape)]

    def io_aliases(self, first_in, first_out):
        return {first_in + k: first_out + v for k, v in self.aliases.items()}


def _mm(a, b, *, mode, out_dtype, name, tm=1024, tn=1024, tk=4096, m_tiles=None, into=None, comm=None):
    if mode == "nn":
        (M, K), (K2, N) = a.shape, b.shape
    elif mode == "nt":
        (M, K), (N, K2) = a.shape, b.shape
    else:
        (K, M), (K2, N) = a.shape, b.shape
    assert K == K2
    tm, tn, tk = _pick(M, tm), _pick(N, tn), _pick(K, tk)
    nk = K // tk
    m0, mc = m_tiles if m_tiles is not None else (0, M // tm)
    o0 = 0 if into is None else m0
    aliased = into is not None and not isinstance(into, str)
    n_ci = len(comm.ins) if comm else 0
    n_co = len(comm.out_shapes) if comm else 0
    grid = (mc, N // tn, nk)
    dims = {"nn": NN_DIMS, "nt": NT_DIMS, "tn": TN_DIMS}[mode]

    def body(*refs, nk):
        a_ref, b_ref = refs[:2]
        pos = 3 if aliased else 2
        c_ins, o_ref, c_outs = refs[pos:pos + n_ci], refs[pos + n_ci], refs[pos + n_ci + 1:pos + n_ci + 1 + n_co]
        scratch = refs[pos + n_ci + 1 + n_co:]
        ids = [pl.program_id(d) for d in range(3)]
        if comm:
            sems = scratch[-2:]

            @pl.when((ids[0] == 0) & (ids[1] == 0) & (ids[2] == 0))
            def _():
                for cp in comm.copies(c_ins, c_outs, *sems):
                    cp.start()

        if nk == 1:
            o_ref[...] = _dot(a_ref[...], b_ref[...], dims).astype(o_ref.dtype)
        else:
            acc_ref, k = scratch[0], ids[2]

            @pl.when(k == 0)
            def _():
                acc_ref[...] = _dot(a_ref[...], b_ref[...], dims)

            @pl.when((k > 0) & (k < nk - 1))
            def _():
                acc_ref[...] += _dot(a_ref[...], b_ref[...], dims)

            @pl.when(k == nk - 1)
            def _():
                o_ref[...] = (acc_ref[...] + _dot(a_ref[...], b_ref[...], dims)).astype(o_ref.dtype)

        if comm:
            @pl.when((ids[0] == grid[0] - 1) & (ids[1] == grid[1] - 1) & (ids[2] == grid[2] - 1))
            def _():
                for cp in comm.copies(c_ins, c_outs, *sems):
                    cp.wait()

    if mode == "tn":
        a_spec = pl.BlockSpec((tk, tm), lambda i, j, k: (k, m0 + i))
    else:
        a_spec = pl.BlockSpec((tm, tk), lambda i, j, k: (m0 + i, k))
    if mode == "nt":
        b_spec = pl.BlockSpec((tn, tk), lambda i, j, k: (j, k))
    else:
        b_spec = pl.BlockSpec((tk, tn), lambda i, j, k: (k, j))
    hbm = pl.BlockSpec(memory_space=pl.ANY)
    res = pl.pallas_call(
        functools.partial(body, nk=nk),
        grid=grid,
        in_specs=[a_spec, b_spec] + [hbm] * (aliased + n_ci),
        out_specs=[pl.BlockSpec((tm, tn), lambda i, j, k: (o0 + i, j))] + [hbm] * n_co,
        out_shape=[jax.ShapeDtypeStruct((mc * tm if into is None else M, N), out_dtype)] + (comm.out_shapes if comm else []),
        scratch_shapes=([pltpu.VMEM((tm, tn), f32)] if nk > 1 else []) + (comm.scratch() if comm else []),
        input_output_aliases={**({2: 0} if aliased else {}), **(comm.io_aliases(2 + aliased, 1) if comm else {})},
        compiler_params=_params(("arbitrary",) * 3 if comm else ("parallel", "parallel", "arbitrary")),
        name=name,
    )(a, b, *([into] if aliased else []), *(comm.ins if comm else []))
    return (res[0], list(res[1:])) if comm else res[0]


def _rms_fwd(x, g, name):
    T, D = x.shape
    tm = _pick(T, 256, 8)

    def body(x_ref, g_ref, h_ref):
        xv = x_ref[...]
        h_ref[...] = (xv * _rstd(xv) * g_ref[...]).astype(bf16)

    return pl.pallas_call(
        body, grid=(T // tm,),
        in_specs=[pl.BlockSpec((tm, D), lambda i: (i, 0)), pl.BlockSpec((1, D), lambda i: (0, 0))],
        out_specs=pl.BlockSpec((tm, D), lambda i: (i, 0)),
        out_shape=jax.ShapeDtypeStruct((T, D), bf16),
        compiler_params=_params(("parallel",)), name=name,
    )(x, g)


def _rms_bwd(x, g, dh, dy, name, comm=None):
    T, D = x.shape
    tm = _pick(T, 256, 8)
    with_dx = dy is not None
    n_ci = len(comm.ins) if comm else 0
    n_co = len(comm.out_shapes) if comm else 0

    def body(*refs):
        if with_dx:
            x_ref, g_ref, dh_ref, dy_ref = refs[:4]
            c_ins, (dx_ref, dg_ref) = refs[4:4 + n_ci], refs[4 + n_ci:6 + n_ci]
            c_outs, sems = refs[6 + n_ci:6 + n_ci + n_co], refs[6 + n_ci + n_co:]
        else:
            x_ref, g_ref, dh_ref, dg_ref = refs
        if comm:
            @pl.when(pl.program_id(0) == 0)
            def _():
                for cp in comm.copies(c_ins, c_outs, *sems):
                    cp.start()

        xv = x_ref[...]
        r = _rstd(xv)
        xh = xv * r
        dhv = dh_ref[...]
        part = jnp.sum(dhv * xh, axis=0, keepdims=True)

        @pl.when(pl.program_id(0) == 0)
        def _():
            dg_ref[...] = part

        @pl.when(pl.program_id(0) > 0)
        def _():
            dg_ref[...] += part

        if with_dx:
            dxh = dhv * g_ref[...]
            dx_ref[...] = dy_ref[...] + r * (dxh - xh * jnp.mean(dxh * xh, axis=-1, keepdims=True))

        if comm:
            @pl.when(pl.program_id(0) == T // tm - 1)
            def _():
                for cp in comm.copies(c_ins, c_outs, *sems):
                    cp.wait()

    row = pl.BlockSpec((tm, D), lambda i: (i, 0))
    vec = pl.BlockSpec((1, D), lambda i: (0, 0))
    hbm = pl.BlockSpec(memory_space=pl.ANY)
    if with_dx:
        res = pl.pallas_call(
            body, grid=(T // tm,), in_specs=[row, vec, row, row] + [hbm] * n_ci, out_specs=[row, vec] + [hbm] * n_co,
            out_shape=[jax.ShapeDtypeStruct((T, D), f32), jax.ShapeDtypeStruct((1, D), f32)] + (comm.out_shapes if comm else []),
            scratch_shapes=comm.scratch() if comm else [],
            input_output_aliases=comm.io_aliases(4, 2) if comm else {},
            compiler_params=_params(("arbitrary",)), name=name,
        )(x, g, dh, dy, *(comm.ins if comm else []))
        return (res[0], res[1], list(res[2:])) if comm else res
    return pl.pallas_call(
        body, grid=(T // tm,), in_specs=[row, vec, row], out_specs=vec,
        out_shape=jax.ShapeDtypeStruct((1, D), f32),
        compiler_params=_params(("arbitrary",)), name=name,
    )(x, g, dh)


MAX_UNIT_UNROLL = 6


def _unit_unroll(trips):
    return max(u for u in range(1, MAX_UNIT_UNROLL + 1) if trips % u == 0)


def _rows(start, size, stride):
    return pl.ds(start, size) if stride == 1 else pl.ds(start, size, stride=stride)


def _attn_units(S, d, first, prev):
    nb = S // d // BLK

    def do_first(r, c):
        first(_rows(r, BLK, d))
        return c

    lax.fori_loop(0, d, do_first, 0, unroll=_unit_unroll(d))
    if nb > 1:
        def do_prev(u, c):
            r = u // (nb - 1)
            b = u % (nb - 1) + 1
            base = r + d * b * BLK
            prev(_rows(base, BLK, d), _rows(base - d * BLK, 2 * BLK, d))
            return c

        lax.fori_loop(0, d * (nb - 1), do_prev, 0, unroll=_unit_unroll(d * (nb - 1)))


def _band_bias(nkeys):
    i = lax.broadcasted_iota(jnp.int32, (BLK, nkeys), 0)
    j = lax.broadcasted_iota(jnp.int32, (BLK, nkeys), 1)
    ok = (j <= i) if nkeys == BLK else ((j >= i) & (j <= i + BLK))
    return jnp.where(ok, 0.0, -jnp.inf).astype(f32)


def _normalize_into(src_ref, gain, dst_ref, S):
    for c in range(S // 256):
        rows = pl.ds(c * 256, 256)
        v = src_ref[rows, :]
        dst_ref[rows, :] = v * _rstd_head(v) * gain


def _attn_fwd(proj, gq, gk, B, S, comm=None):
    T, NC = proj.shape
    scale = HEAD_DIM ** -0.5
    n_ci = len(comm.ins) if comm else 0
    n_co = len(comm.out_shapes) if comm else 0

    def body(*refs):
        q_ref, k_ref, v_ref, z_ref, gq_ref, gk_ref = refs[:6]
        c_ins = refs[6:6 + n_ci]
        ag_ref, a_ref, lse_ref = refs[6 + n_ci:9 + n_ci]
        c_outs = refs[9 + n_ci:9 + n_ci + n_co]
        qs, ks, o0, o1, o2, l0, l1, l2, bias1, bias2 = refs[9 + n_ci + n_co:19 + n_ci + n_co]
        sems = refs[19 + n_ci + n_co:]
        g = pl.program_id(2)
        if comm:
            @pl.when((pl.program_id(0) == 0) & (pl.program_id(1) == 0) & (g == 0))
            def _():
                for cp in comm.copies(c_ins, c_outs, *sems):
                    cp.start()

        bias1[...] = _band_bias(BLK)
        bias2[...] = _band_bias(2 * BLK)
        _normalize_into(q_ref, gq_ref[pl.ds(g, 1), :], qs, S)
        _normalize_into(k_ref, gk_ref[pl.ds(g, 1), :], ks, S)
        o_s, l_s = (o0, o1, o2), (l0, l1, l2)

        def run(gi):
            def unit(qr, kr, nkeys):
                s = _dot(qs[qr, :], ks[kr, :], NT_DIMS) * scale + (bias1 if nkeys == BLK else bias2)[...]
                m = jnp.max(s, axis=-1, keepdims=True)
                p = jnp.exp(s - m)
                den = jnp.sum(p, axis=-1, keepdims=True)
                o_s[gi][qr, :] = _dot(p, v_ref[kr, :], NN_DIMS) * (1.0 / den)
                l_s[gi][qr, :] = jnp.broadcast_to(m + jnp.log(den), (BLK, HEAD_DIM))

            _attn_units(S, DILATIONS[gi], lambda qr: unit(qr, qr, BLK), lambda qr, kr: unit(qr, kr, 2 * BLK))

        for gi in range(N_GROUPS):
            pl.when(g == gi)(functools.partial(run, gi))

        @pl.when(g == N_GROUPS - 1)
        def _():
            for c in range(S // 256):
                rows = pl.ds(c * 256, 256)
                la, lb, lc = l0[rows, :], l1[rows, :], l2[rows, :]
                m = jnp.maximum(jnp.maximum(la, lb), lc)
                wa, wb, wc = jnp.exp(la - m), jnp.exp(lb - m), jnp.exp(lc - m)
                tot = wa + wb + wc
                a = (wa / tot) * o0[rows, :] + (wb / tot) * o1[rows, :] + (wc / tot) * o2[rows, :]
                z = z_ref[rows, :]
                a_ref[rows, :] = a
                lse_ref[rows, :] = m + jnp.log(tot)
                ag_ref[rows, :] = (a * (z * _sigmoid(z))).astype(bf16)

        if comm:
            @pl.when((pl.program_id(0) == B - 1) & (pl.program_id(1) == HEADS - 1) & (g == N_GROUPS - 1))
            def _():
                for cp in comm.copies(c_ins, c_outs, *sems):
                    cp.wait()

    def slab(col0):
        return pl.BlockSpec((S, HEAD_DIM), lambda b, h, g: (b, col0 // HEAD_DIM + g * HEADS + h))

    gain = pl.BlockSpec((N_GROUPS, HEAD_DIM), lambda b, h, g: (0, 0))
    out = pl.BlockSpec((S, HEAD_DIM), lambda b, h, g: (b, h))
    hbm = pl.BlockSpec(memory_space=pl.ANY)
    res = pl.pallas_call(
        body, grid=(B, HEADS, N_GROUPS),
        in_specs=[slab(Q0), slab(K0), slab(V0), pl.BlockSpec((S, HEAD_DIM), lambda b, h, g: (b, ZA // HEAD_DIM + h)), gain, gain]
        + [hbm] * n_ci,
        out_specs=[out, out, out] + [hbm] * n_co,
        out_shape=[jax.ShapeDtypeStruct((T, ATTN_OUT), bf16), jax.ShapeDtypeStruct((T, ATTN_OUT), f32),
                   jax.ShapeDtypeStruct((T, ATTN_OUT), f32)] + (comm.out_shapes if comm else []),
        scratch_shapes=[pltpu.VMEM((S, HEAD_DIM), f32)] * 8 + [pltpu.VMEM((BLK, BLK), f32), pltpu.VMEM((BLK, 2 * BLK), f32)]
        + (comm.scratch() if comm else []),
        compiler_params=_params(("arbitrary", "arbitrary", "arbitrary")), name="attn_fwd",
    )(proj, proj, proj, proj, gq, gk, *(comm.ins if comm else []))
    return res[0], res[1], res[2], list(res[3:])


def _rms_bwd_rows(raw, gain, dn, on_mxu=True):
    r = _rstd_head(raw) if on_mxu else _rstd(raw)
    xh = raw * r
    dxh = dn * gain
    if on_mxu:
        mean = _row_sum(dxh * xh) * (1.0 / raw.shape[1])
    else:
        mean = jnp.mean(dxh * xh, axis=-1, keepdims=True)
    return r * (dxh - xh * mean), jnp.sum(dn * xh, axis=0, keepdims=True)


def _attn_bwd(proj, gq, gk, a_comb, lse, dag, dproj, B, S):
    T, NC = proj.shape
    scale = HEAD_DIM ** -0.5

    def body(q_ref, k_ref, v_ref, z_ref, gq_ref, gk_ref, a_ref, lse_ref, dag_ref, dproj_in, dproj_ref, dgq_ref, dgk_ref,
             qs, ks, da_s, dl_s, dq_s, dk_s, dv_s, bias1, bias2, stage, dz_stage, sem, dz_sem):
        b, h, g = pl.program_id(0), pl.program_id(1), pl.program_id(2)
        bias1[...] = _band_bias(BLK)
        bias2[...] = _band_bias(2 * BLK)
        gq_row, gk_row = gq_ref[pl.ds(g, 1), :], gk_ref[pl.ds(g, 1), :]
        _normalize_into(q_ref, gq_row, qs, S)
        _normalize_into(k_ref, gk_row, ks, S)

        def dst(c):
            return dproj_ref.at[pl.ds(b * S, S), pl.ds((Q0, K0, V0)[c] + (g * HEADS + h) * HEAD_DIM, HEAD_DIM)]

        out = _Deferred(stage, sem, (b * HEADS + h) * N_GROUPS + g, B * HEADS * N_GROUPS - 1, 3)
        out.begin(dst)

        @pl.when((b == 0) & (h == 0) & (g == 0))
        def _():
            dgq_ref[...] = jnp.zeros_like(dgq_ref)
            dgk_ref[...] = jnp.zeros_like(dgk_ref)

        @pl.when(g == 0)
        def _():
            for c in range(S // 256):
                rows = pl.ds(c * 256, 256)
                z, a, dg_ = z_ref[rows, :], a_ref[rows, :], dag_ref[rows, :]
                sg = _sigmoid(z)
                da = dg_ * (z * sg)
                da_s[rows, :] = da
                dl_s[rows, :] = _row_sum(da * a)
                dz_stage[rows, :] = (dg_ * a * (sg * (1.0 + z * (1.0 - sg)))).astype(bf16)
            cp = pltpu.make_async_copy(dz_stage, dproj_ref.at[pl.ds(b * S, S), pl.ds(ZA + h * HEAD_DIM, HEAD_DIM)], dz_sem)
            cp.start()
            cp.wait()

        dk_s[...] = jnp.zeros_like(dk_s)
        dv_s[...] = jnp.zeros_like(dv_s)

        def run(gi):
            def unit(qr, kr, nkeys):
                q, k, v = qs[qr, :], ks[kr, :], v_ref[kr, :]
                s = _dot(q, k, NT_DIMS) * scale
                p = jnp.exp(s + (bias1 if nkeys == BLK else bias2)[...] - lse_ref[qr, :][:, :1])
                da = da_s[qr, :]
                dp = _dot(da, v, NT_DIMS)
                ds = p * (dp - dl_s[qr, :][:, :1]) * scale
                dq_s[qr, :] = _dot(ds, k, NN_DIMS)
                dk_s[kr, :] += _dot(ds, q, TN_DIMS)
                dv_s[kr, :] += _dot(p, da, TN_DIMS)

            _attn_units(S, DILATIONS[gi], lambda qr: unit(qr, qr, BLK), lambda qr, kr: unit(qr, kr, 2 * BLK))

        for gi in range(N_GROUPS):
            pl.when(g == gi)(functools.partial(run, gi))

        gq_acc = jnp.zeros((1, HEAD_DIM), f32)
        gk_acc = jnp.zeros((1, HEAD_DIM), f32)
        for c in range(S // 256):
            rows = pl.ds(c * 256, 256)
            dq, gq_p = _rms_bwd_rows(q_ref[rows, :], gq_row, dq_s[rows, :])
            dk, gk_p = _rms_bwd_rows(k_ref[rows, :], gk_row, dk_s[rows, :])
            gq_acc, gk_acc = gq_acc + gq_p, gk_acc + gk_p
            stage[out.slot, 0, rows, :] = dq.astype(bf16)
            stage[out.slot, 1, rows, :] = dk.astype(bf16)
            stage[out.slot, 2, rows, :] = dv_s[rows, :].astype(bf16)
        dgq_ref[pl.ds(g, 1), :] += gq_acc
        dgk_ref[pl.ds(g, 1), :] += gk_acc
        for c in range(3):
            out.start(c, dst(c))
        out.end(dst)

    def slab(col0):
        return pl.BlockSpec((S, HEAD_DIM), lambda b, h, g: (b, col0 // HEAD_DIM + g * HEADS + h))

    gain = pl.BlockSpec((N_GROUPS, HEAD_DIM), lambda b, h, g: (0, 0))
    per_slot = pl.BlockSpec((S, HEAD_DIM), lambda b, h, g: (b, h))
    return pl.pallas_call(
        body, grid=(B, HEADS, N_GROUPS),
        in_specs=[slab(Q0), slab(K0), slab(V0), pl.BlockSpec((S, HEAD_DIM), lambda b, h, g: (b, ZA // HEAD_DIM + h)), gain, gain,
                  per_slot, per_slot, per_slot, pl.BlockSpec(memory_space=pl.ANY)],
        out_specs=[pl.BlockSpec(memory_space=pl.ANY), gain, gain],
        out_shape=[jax.ShapeDtypeStruct(dproj.shape, dproj.dtype), jax.ShapeDtypeStruct((N_GROUPS, HEAD_DIM), f32),
                   jax.ShapeDtypeStruct((N_GROUPS, HEAD_DIM), f32)],
        scratch_shapes=[pltpu.VMEM((S, HEAD_DIM), f32)] * 7 + [pltpu.VMEM((BLK, BLK), f32), pltpu.VMEM((BLK, 2 * BLK), f32),
                                                                pltpu.VMEM((2, 3, S, HEAD_DIM), bf16), pltpu.VMEM((S, HEAD_DIM), bf16),
                                                                pltpu.SemaphoreType.DMA((2, 3)), pltpu.SemaphoreType.DMA],
        input_output_aliases={9: 0},
        compiler_params=_params(("arbitrary", "arbitrary", "arbitrary")), name="attn_bwd",
    )(proj, proj, proj, proj, gq, gk, a_comb, lse, dag, dproj)


def _conv_fwd(proj, conv_w, B, S):
    T, NC = proj.shape
    tc = LANES

    def body(cb_ref, cc_ref, cv_ref, z_ref, w_ref, c_ref, ub):
        u = cc_ref[...] * cv_ref[...]
        ub[0:8, :] = jnp.zeros((8, tc), f32)
        ub[8:8 + S, :] = u
        w = w_ref[...]
        y = w[0:1] * u + w[1:2] * ub[pl.ds(7, S), :] + w[2:3] * ub[pl.ds(6, S), :]
        z = z_ref[...]
        c_ref[...] = (cb_ref[...] * y * (z * _sigmoid(z))).astype(bf16)

    def seg(col0):
        return pl.BlockSpec((S, tc), lambda j, b: (b, col0 // tc + j))

    return pl.pallas_call(
        body, grid=(CONV_W // tc, B),
        in_specs=[seg(CB), seg(CC), seg(CV), seg(ZC), pl.BlockSpec((3, tc), lambda j, b: (0, j))],
        out_specs=pl.BlockSpec((S, tc), lambda j, b: (b, j)),
        out_shape=jax.ShapeDtypeStruct((T, CONV_W), bf16),
        scratch_shapes=[pltpu.VMEM((S + 8, tc), f32)],
        compiler_params=_params(("parallel", "parallel")), name="conv_fwd",
    )(proj, proj, proj, proj, conv_w)


def _conv_bwd(proj, conv_w, dc, dproj, B, S):
    T, NC = proj.shape
    tc = LANES

    def body(cb_ref, cc_ref, cv_ref, z_ref, w_ref, dc_ref, dproj_in, dproj_ref, dw_ref, ub, db, stage, sem):
        j, b = pl.program_id(0), pl.program_id(1)

        def dst(c):
            return dproj_ref.at[pl.ds(b * S, S), pl.ds((CB, CC, CV, ZC)[c] + j * tc, tc)]

        out = _Deferred(stage, sem, j * B + b, (CONV_W // tc) * B - 1, 4)
        out.begin(dst)
        cb, cc, cv, z, dcv = cb_ref[...], cc_ref[...], cv_ref[...], z_ref[...], dc_ref[...]
        u = cc * cv
        ub[0:8, :] = jnp.zeros((8, tc), f32)
        ub[8:8 + S, :] = u
        u1, u2 = ub[pl.ds(7, S), :], ub[pl.ds(6, S), :]
        w = w_ref[...]
        y = w[0:1] * u + w[1:2] * u1 + w[2:3] * u2
        sg = _sigmoid(z)
        si = z * sg
        dyv = dcv * cb * si
        db[0:S, :] = dyv
        db[S:S + 8, :] = jnp.zeros((8, tc), f32)
        du = w[0:1] * dyv + w[1:2] * db[pl.ds(1, S), :] + w[2:3] * db[pl.ds(2, S), :]
        stage[out.slot, 0] = (dcv * y * si).astype(bf16)
        stage[out.slot, 1] = (du * cv).astype(bf16)
        stage[out.slot, 2] = (du * cc).astype(bf16)
        stage[out.slot, 3] = (dcv * cb * y * (sg * (1.0 + z * (1.0 - sg)))).astype(bf16)
        for c in range(4):
            out.start(c, dst(c))
        part = jnp.concatenate([jnp.sum(dyv * u, axis=0, keepdims=True), jnp.sum(dyv * u1, axis=0, keepdims=True),
                                jnp.sum(dyv * u2, axis=0, keepdims=True)], axis=0)

        @pl.when(b == 0)
        def _():
            dw_ref[...] = part

        @pl.when(b > 0)
        def _():
            dw_ref[...] += part

        out.end(dst)

    def seg(col0):
        return pl.BlockSpec((S, tc), lambda j, b: (b, col0 // tc + j))

    return pl.pallas_call(
        body, grid=(CONV_W // tc, B),
        in_specs=[seg(CB), seg(CC), seg(CV), seg(ZC), pl.BlockSpec((3, tc), lambda j, b: (0, j)),
                  pl.BlockSpec((S, tc), lambda j, b: (b, j)), pl.BlockSpec(memory_space=pl.ANY)],
        out_specs=[pl.BlockSpec(memory_space=pl.ANY), pl.BlockSpec((3, tc), lambda j, b: (0, j))],
        out_shape=[jax.ShapeDtypeStruct(dproj.shape, dproj.dtype), jax.ShapeDtypeStruct((3, CONV_W), f32)],
        scratch_shapes=[pltpu.VMEM((S + 8, tc), f32), pltpu.VMEM((S + 8, tc), f32), pltpu.VMEM((2, 4, S, tc), bf16),
                        pltpu.SemaphoreType.DMA((2, 4))],
        input_output_aliases={6: 0},
        compiler_params=_params(("arbitrary", "arbitrary")), name="conv_bwd",
    )(proj, proj, proj, proj, conv_w, dc, dproj)


MEM_CHUNK = 1024


def _mem_fwd(proj, mkv, gq, gk, B, S):
    T, NC = proj.shape
    scale = MEM_HD ** -0.5

    def body(q_ref, z_ref, k_ref, v_ref, gq_ref, gk_ref, o_ref):
        kv = k_ref[...]
        kn = (kv * _rstd_head(kv) * gk_ref[...]).astype(bf16)
        vv = v_ref[...].astype(bf16)

        def chunk(c, carry):
            rows = pl.ds(pl.multiple_of(c * MEM_CHUNK, MEM_CHUNK), MEM_CHUNK)
            q = q_ref[rows, :]
            qn = q * _rstd(q) * gq_ref[...]
            s = _dot(qn, kn, NT_DIMS) * scale
            p = jnp.exp(s - jnp.max(s, axis=-1, keepdims=True))
            p = p / jnp.sum(p, axis=-1, keepdims=True)
            z = z_ref[rows, :]
            o_ref[rows, :] = (_dot(p, vv, NN_DIMS) * (z * _sigmoid(z))).astype(bf16)
            return carry

        lax.fori_loop(0, S // MEM_CHUNK, chunk, 0)

    gain = pl.BlockSpec((1, MEM_HD), lambda b, h: (0, 0))
    return pl.pallas_call(
        body, grid=(B, MEM_HEADS),
        in_specs=[pl.BlockSpec((S, MEM_HD), lambda b, h: (b, MQ // MEM_HD + h)),
                  pl.BlockSpec((S, MEM_HD), lambda b, h: (b, ZM // MEM_HD + h)),
                  pl.BlockSpec((MEM_LEN, MEM_HD), lambda b, h: (b, h)),
                  pl.BlockSpec((MEM_LEN, MEM_HD), lambda b, h: (b, MEM_HEADS + h)), gain, gain],
        out_specs=pl.BlockSpec((S, MEM_HD), lambda b, h: (b, h)),
        out_shape=jax.ShapeDtypeStruct((T, MEM_W), bf16),
        compiler_params=_params(("parallel", "parallel")), name="mem_fwd",
    )(proj, proj, mkv, mkv, gq, gk)


def _mem_bwd(proj, mkv, gq, gk, dmo, dproj, B, S):
    T, NC = proj.shape
    scale = MEM_HD ** -0.5

    def body(q_ref, z_ref, k_ref, v_ref, gq_ref, gk_ref, dmo_ref, dproj_in, dproj_ref, dmk_ref, dmv_ref, dgq_ref, dgk_ref,
             dkn_s, dv_s, gq_s, stage, sem):
        b, h = pl.program_id(0), pl.program_id(1)

        def dst(c):
            return dproj_ref.at[pl.ds(b * S, S), pl.ds((MQ, ZM)[c] + h * MEM_HD, MEM_HD)]

        out = _Deferred(stage, sem, b * MEM_HEADS + h, B * MEM_HEADS - 1, 2)
        out.begin(dst)
        kv = k_ref[...]
        kn = (kv * _rstd_head(kv) * gk_ref[...]).astype(bf16)
        vv = v_ref[...].astype(bf16)
        dkn_s[...] = jnp.zeros_like(dkn_s)
        dv_s[...] = jnp.zeros_like(dv_s)
        gq_s[...] = jnp.zeros_like(gq_s)

        def chunk(c, carry):
            rows = pl.ds(pl.multiple_of(c * MEM_CHUNK, MEM_CHUNK), MEM_CHUNK)
            q = q_ref[rows, :]
            qn = q * _rstd(q) * gq_ref[...]
            s = _dot(qn, kn, NT_DIMS) * scale
            p = jnp.exp(s - jnp.max(s, axis=-1, keepdims=True))
            p = p / jnp.sum(p, axis=-1, keepdims=True)
            mo = _dot(p, vv, NN_DIMS)
            z, dg_ = z_ref[rows, :], dmo_ref[rows, :]
            sg = _sigmoid(z)
            do = dg_ * (z * sg)
            stage[out.slot, 1, rows, :] = (dg_ * mo * (sg * (1.0 + z * (1.0 - sg)))).astype(bf16)
            dp = _dot(do, vv, NT_DIMS)
            ds = p * (dp - jnp.sum(do * mo, axis=-1, keepdims=True)) * scale
            dq, gq_p = _rms_bwd_rows(q, gq_ref[...], _dot(ds, kn, NN_DIMS), on_mxu=False)
            stage[out.slot, 0, rows, :] = dq.astype(bf16)
            gq_s[...] += gq_p
            dkn_s[...] += _dot(ds, qn, TN_DIMS)
            dv_s[...] += _dot(p, do, TN_DIMS)
            return carry

        lax.fori_loop(0, S // MEM_CHUNK, chunk, 0)
        for c in range(2):
            out.start(c, dst(c))
        dk, gk_p = _rms_bwd_rows(kv, gk_ref[...], dkn_s[...])
        dmk_ref[...] = dk
        dmv_ref[...] = dv_s[...]

        @pl.when((b == 0) & (h == 0))
        def _():
            dgq_ref[...] = gq_s[...]
            dgk_ref[...] = gk_p

        @pl.when((b > 0) | (h > 0))
        def _():
            dgq_ref[...] += gq_s[...]
            dgk_ref[...] += gk_p

        out.end(dst)

    gain = pl.BlockSpec((1, MEM_HD), lambda b, h: (0, 0))
    kvb = pl.BlockSpec((MEM_LEN, MEM_HD), lambda b, h: (b, h))
    return pl.pallas_call(
        body, grid=(B, MEM_HEADS),
        in_specs=[pl.BlockSpec((S, MEM_HD), lambda b, h: (b, MQ // MEM_HD + h)),
                  pl.BlockSpec((S, MEM_HD), lambda b, h: (b, ZM // MEM_HD + h)),
                  kvb, pl.BlockSpec((MEM_LEN, MEM_HD), lambda b, h: (b, MEM_HEADS + h)), gain, gain,
                  pl.BlockSpec((S, MEM_HD), lambda b, h: (b, h)), pl.BlockSpec(memory_space=pl.ANY)],
        out_specs=[pl.BlockSpec(memory_space=pl.ANY), kvb, kvb, gain, gain],
        out_shape=[jax.ShapeDtypeStruct(dproj.shape, dproj.dtype), jax.ShapeDtypeStruct((B * MEM_LEN, MEM_W), f32),
                   jax.ShapeDtypeStruct((B * MEM_LEN, MEM_W), f32), jax.ShapeDtypeStruct((1, MEM_HD), f32),
                   jax.ShapeDtypeStruct((1, MEM_HD), f32)],
        scratch_shapes=[pltpu.VMEM((MEM_LEN, MEM_HD), f32), pltpu.VMEM((MEM_LEN, MEM_HD), f32), pltpu.VMEM((1, MEM_HD), f32),
                        pltpu.VMEM((2, 2, S, MEM_HD), bf16), pltpu.SemaphoreType.DMA((2, 2))],
        input_output_aliases={7: 0},
        compiler_params=_params(("arbitrary", "arbitrary")), name="mem_bwd",
    )(proj, proj, mkv, mkv, gq, gk, dmo, dproj)


def _merge_fwd(proj, ag, cg, mg, wa, wc, wm):
    T, NC = proj.shape
    D = wa.shape[1]
    tm, tn = _pick(T, 1024), _pick(D, 512)
    assert GT % tn == 0

    def body(ag_ref, cg_ref, mg_ref, wa_ref, wc_ref, wm_ref, g0_ref, g1_ref, g2_ref, mer_ref, ba_ref, bc_ref, bm_ref):
        ba = _dot(ag_ref[...], wa_ref[...], NN_DIMS)
        bc = _dot(cg_ref[...], wc_ref[...], NN_DIMS)
        bm = _dot(mg_ref[...], wm_ref[...], NN_DIMS)
        mer_ref[...] = (_sigmoid(g0_ref[...]) * ba + _sigmoid(g1_ref[...]) * bc + _sigmoid(g2_ref[...]) * bm).astype(bf16)
        ba_ref[...] = ba.astype(bf16)
        bc_ref[...] = bc.astype(bf16)
        bm_ref[...] = bm.astype(bf16)

    def act(w):
        return pl.BlockSpec((tm, w), lambda i, j: (i, 0))

    def wt(k):
        return pl.BlockSpec((k, tn), lambda i, j: (0, j))

    def gate(n):
        return pl.BlockSpec((tm, tn), lambda i, j: (i, (GT + n * D) // tn + j))

    out = pl.BlockSpec((tm, tn), lambda i, j: (i, j))
    return pl.pallas_call(
        body, grid=(T // tm, D // tn),
        in_specs=[act(ATTN_OUT), act(CONV_W), act(MEM_W), wt(ATTN_OUT), wt(CONV_W), wt(MEM_W), gate(0), gate(1), gate(2)],
        out_specs=[out] * 4, out_shape=[jax.ShapeDtypeStruct((T, D), bf16)] * 4,
        compiler_params=_params(("parallel", "parallel")), name="merge_fwd",
    )(ag, cg, mg, wa, wc, wm, proj, proj, proj)


def _out_loss(merged, w_out, x, tgt):
    T, D = x.shape
    tm, tn = _pick(T, 1024), _pick(D, 512)

    def body(m_ref, w_ref, x_ref, t_ref, dy_ref, dyb_ref, lp_ref):
        e = x_ref[...] + _dot(m_ref[...], w_ref[...], NN_DIMS) - t_ref[...]
        dy = e / D
        dy_ref[...] = dy
        dyb_ref[...] = dy.astype(bf16)
        part = jnp.full((1, 8, LANES), jnp.sum(e * e), f32)

        @pl.when(pl.program_id(1) == 0)
        def _():
            lp_ref[...] = part

        @pl.when(pl.program_id(1) > 0)
        def _():
            lp_ref[...] += part

    tile = pl.BlockSpec((tm, tn), lambda i, j: (i, j))
    return pl.pallas_call(
        body, grid=(T // tm, D // tn),
        in_specs=[pl.BlockSpec((tm, D), lambda i, j: (i, 0)), pl.BlockSpec((D, tn), lambda i, j: (0, j)), tile, tile],
        out_specs=[tile, tile, pl.BlockSpec((1, 8, LANES), lambda i, j: (i, 0, 0))],
        out_shape=[jax.ShapeDtypeStruct((T, D), f32), jax.ShapeDtypeStruct((T, D), bf16),
                   jax.ShapeDtypeStruct((T // tm, 8, LANES), f32)],
        compiler_params=_params(("parallel", "arbitrary")), name="out_loss",
    )(merged, w_out, x, tgt)


def _merge_bwd(proj, dyb, w_out, ba, bc, bm):
    T, NC = proj.shape
    D = w_out.shape[0]
    tm, tn = _pick(T, 1024), _pick(D, 512)
    assert GT % tn == 0

    def body(dy_ref, w_ref, ba_ref, bc_ref, bm_ref, g0_ref, g1_ref, g2_ref, da_ref, dc_ref, dm_ref, dproj_ref, stage, sem):
        i, j = pl.program_id(0), pl.program_id(1)

        def dst(n):
            return dproj_ref.at[pl.ds(i * tm, tm), pl.ds(GT + n * D + j * tn, tn)]

        out = _Deferred(stage, sem, i * (D // tn) + j, (T // tm) * (D // tn) - 1, 3)
        out.begin(dst)
        dmer = _dot(dy_ref[...], w_ref[...], NT_DIMS)
        for n, (g_ref, br_ref, o_ref) in enumerate(((g0_ref, ba_ref, da_ref), (g1_ref, bc_ref, dc_ref), (g2_ref, bm_ref, dm_ref))):
            sg = _sigmoid(g_ref[...])
            o_ref[...] = (sg * dmer).astype(bf16)
            stage[out.slot, n] = (dmer * br_ref[...].astype(f32) * (sg * (1.0 - sg))).astype(bf16)
            out.start(n, dst(n))
        out.end(dst)

    tile = pl.BlockSpec((tm, tn), lambda i, j: (i, j))

    def gate(n):
        return pl.BlockSpec((tm, tn), lambda i, j: (i, (GT + n * D) // tn + j))

    return pl.pallas_call(
        body, grid=(T // tm, D // tn),
        in_specs=[pl.BlockSpec((tm, D), lambda i, j: (i, 0)), pl.BlockSpec((tn, D), lambda i, j: (j, 0)), tile, tile, tile,
                  gate(0), gate(1), gate(2)],
        out_specs=[tile, tile, tile, pl.BlockSpec(memory_space=pl.ANY)],
        out_shape=[jax.ShapeDtypeStruct((T, D), bf16)] * 3 + [jax.ShapeDtypeStruct((T, NC), bf16)],
        scratch_shapes=[pltpu.VMEM((2, 3, tm, tn), bf16), pltpu.SemaphoreType.DMA((2, 3))],
        compiler_params=_params(("arbitrary", "arbitrary")), name="merge_bwd",
    )(dyb, w_out, ba, bc, bm, proj, proj, proj)


def _fwd_bwd_head(x2, mem2, t2, proj, attn, B, S, mem_norm_g, attn_q_norm, attn_k_norm, conv_w, mem_q_norm, mem_k_norm,
                  w_kv, w_a, w_c, w_m, w_out):
    D = x2.shape[1]
    mng = mem_norm_g.reshape(1, D)
    mqn, mkn = mem_q_norm.reshape(1, MEM_HD), mem_k_norm.reshape(1, MEM_HD)
    ag, a_comb, lse = attn

    mh = _rms_fwd(mem2, mng, "rms_mem")
    mkv = _mm(mh, w_kv, mode="nn", out_dtype=f32, name="mkv")
    cg = _conv_fwd(proj, conv_w, B, S)
    mg = _mem_fwd(proj, mkv, mqn, mkn, B, S)
    merged, ba, bc, bm = _merge_fwd(proj, ag, cg, mg, w_a, w_c, w_m)
    dy, dyb, lp = _out_loss(merged, w_out, x2, t2)
    sq_err = jnp.sum(lp[:, 0, 0])

    g = {}
    g["w_out"] = _mm(merged, dyb, mode="tn", out_dtype=f32, name="dw_out")
    dba, dbc, dbm, dproj = _merge_bwd(proj, dyb, w_out, ba, bc, bm)
    g["w_br_attn"] = _mm(ag, dba, mode="tn", out_dtype=f32, name="dw_br_attn")
    g["w_br_conv"] = _mm(cg, dbc, mode="tn", out_dtype=f32, name="dw_br_conv")
    g["w_br_mem"] = _mm(mg, dbm, mode="tn", out_dtype=f32, name="dw_br_mem")
    dag = _mm(dba, w_a, mode="nt", out_dtype=f32, name="d_attn_out")
    dcg = _mm(dbc, w_c, mode="nt", out_dtype=f32, name="d_conv_out")
    dmg = _mm(dbm, w_m, mode="nt", out_dtype=f32, name="d_mem_out")
    dproj, g["attn_q_norm"], g["attn_k_norm"] = _attn_bwd(proj, attn_q_norm, attn_k_norm, a_comb, lse, dag, dproj, B, S)
    dproj, g["conv_w"] = _conv_bwd(proj, conv_w, dcg, dproj, B, S)
    dproj, dmk, dmv, dgmq, dgmk = _mem_bwd(proj, mkv, mqn, mkn, dmg, dproj, B, S)
    g["mem_q_norm"], g["mem_k_norm"] = dgmq.reshape(MEM_HD), dgmk.reshape(MEM_HD)
    dmkv = jnp.concatenate([dmk, dmv], axis=1)
    dmh = _mm(dmkv, w_kv, mode="nt", out_dtype=f32, name="d_mem_h")
    g["mem_norm_g"] = _rms_bwd(mem2, mng, dmh, None, "rms_mem_bwd").reshape(D)
    return sq_err, g, dict(dy=dy, dproj=dproj, mh=mh, dmkv=dmkv)


MESH = pl.DeviceIdType.MESH
HBM = pl.BlockSpec(memory_space=pl.ANY)


def _me():
    x, y, c = lax.axis_index("x"), lax.axis_index("y"), lax.axis_index("c")
    return (x, y, c), 4 * x + 2 * y + c


def _peer(k):
    (x, y, c), _ = _me()
    p = (1 - x if k & 4 else x, 1 - y if k & 2 else y, 1 - c if k & 1 else c)
    return p, 4 * p[0] + 2 * p[1] + p[2]


def _window(ref, axis, size, idx):
    if axis is None:
        return ref.at[idx]
    if axis == 0:
        return ref.at[pl.ds(idx * size, size), :]
    return ref.at[:, pl.ds(idx * size, size)]


def _full_shape(s, axis):
    if axis is None:
        return (N_DEV,) + s.shape
    return tuple(N_DEV * d if i == axis else d for i, d in enumerate(s.shape))


OTHER_CHIPS = (4, 2, 6)


def _spread_comm(shards, axes):
    n = len(shards)

    def copies(ins, outs, send_sems, recv_sems, base=0):
        _, me = _me()
        out = []
        for a in range(n):
            mine = _window(outs[a], axes[a], None if axes[a] is None else shards[a].shape[axes[a]], me)
            out.append(pltpu.make_async_copy(ins[a], mine, send_sems.at[base + a, N_CHIP]))
            for k, dist in enumerate((1,) + OTHER_CHIPS):
                dev, _ = _peer(dist)
                out.append(pltpu.make_async_remote_copy(
                    src_ref=ins[a], dst_ref=mine, send_sem=send_sems.at[base + a, k], recv_sem=recv_sems.at[base + a, k],
                    device_id=dev, device_id_type=MESH))
        return out

    shapes = [jax.ShapeDtypeStruct(_full_shape(s, ax), s.dtype) for s, ax in zip(shards, axes)]
    return _Comm(shards, shapes, (n, N_CHIP + 1), copies)


def _forward_blocks(fulls, axes):
    n = len(fulls)
    sizes = [None if ax is None else f.shape[ax] // N_DEV for f, ax in zip(fulls, axes)]

    def body(*refs):
        outs = refs[n:2 * n]
        send_sems, recv_sems = refs[2 * n:]
        sibling, _ = _peer(1)
        copies = []
        for j, dist in enumerate(OTHER_CHIPS):
            _, held = _peer(dist)
            for a in range(n):
                block = _window(outs[a], axes[a], sizes[a], held)
                copies.append(pltpu.make_async_remote_copy(
                    src_ref=block, dst_ref=block, send_sem=send_sems.at[a, j], recv_sem=recv_sems.at[a, j],
                    device_id=sibling, device_id_type=MESH))
        for cp in copies:
            cp.start()
        for cp in copies:
            cp.wait()

    return pl.pallas_call(
        body, in_specs=[HBM] * n, out_specs=[HBM] * n,
        out_shape=[jax.ShapeDtypeStruct(f.shape, f.dtype) for f in fulls],
        scratch_shapes=[pltpu.SemaphoreType.DMA((n, len(OTHER_CHIPS))), pltpu.SemaphoreType.DMA((n, len(OTHER_CHIPS)))],
        input_output_aliases={a: a for a in range(n)},
        name="forward_blocks",
    )(*fulls)


def _shard_order():
    (x, y, c), me = _me()
    xs, ys, xo, yo = _peer(4)[1], _peer(2)[1], _peer(5)[1], _peer(3)[1]
    north = c == 1
    ids = [me, _peer(1)[1], jnp.where(north, xs, ys), jnp.where(north, yo, xo), jnp.where(north, ys, xs),
           jnp.where(north, xo, yo), _peer(6)[1], _peer(7)[1]]
    return jnp.stack(ids).astype(jnp.int32)


def _proj_gather(h, w_shard, order, comm, comm_at):
    T, K = h.shape
    C = w_shard.shape[1]
    tm = _pick(T, 1024)
    nm = T // tm
    ahead = max(nm - 2, 0)
    n_ci, n_co = len(comm.ins), len(comm.out_shapes)

    def body(*refs):
        order_ref, h_ref, ws_ref = refs[:3]
        c_ins = refs[3:3 + n_ci]
        o_ref, wf_ref = refs[3 + n_ci:5 + n_ci]
        c_outs = refs[5 + n_ci:5 + n_ci + n_co]
        wbuf, send_sems, recv_sems, own_sem, buf_sems, c_send, c_recv = refs[5 + n_ci + n_co:]
        t, i = pl.program_id(0), pl.program_id(1)

        @pl.when((t == comm_at) & (i == 0))
        def _():
            for cp in comm.copies(c_ins, c_outs, c_send, c_recv):
                cp.start()
        (x, y, c), me = _me()
        sibling, sib_id = _peer(1)
        chips = [_peer(dist) for dist in OTHER_CHIPS]

        def win(idx):
            return wf_ref.at[:, pl.ds(idx * C, C)]

        def copy(k, block, to, src=None):
            return pltpu.make_async_remote_copy(
                src_ref=win(block) if src is None else src, dst_ref=win(block),
                send_sem=send_sems.at[k], recv_sem=recv_sems.at[k], device_id=to, device_id_type=MESH)

        def fetch(tt, src):
            return pltpu.make_async_copy(src, wbuf.at[tt % 2], buf_sems.at[tt % 2])

        own = pltpu.make_async_copy(ws_ref, win(me), own_sem)
        (x_nbr, x_id), (y_nbr, y_id), (_, d_id) = chips

        def half(k, block, top, to):
            rows = wf_ref.at[pl.ds(0 if top else K // 2, K // 2), pl.ds(block * C, C)]
            return pltpu.make_async_remote_copy(src_ref=rows, dst_ref=rows, send_sem=send_sems.at[k], recv_sem=recv_sems.at[k],
                                                device_id=to, device_id_type=MESH)

        here = (x, y, c)

        def pass_on(from_x):
            if from_x:
                copy(4, x_id, sibling).start()
                half(8, x_id, False, y_nbr).start()
            else:
                copy(5, y_id, sibling).start()
                half(7, y_id, True, x_nbr).start()

        @pl.when((t == 0) & (i == 0))
        def _():
            fetch(0, ws_ref).start()
            own.start()
            copy(0, me, sibling, src=ws_ref).start()

        for tt in range(N_DEV):
            @pl.when((t == tt) & (i == 0))
            def _(tt=tt):
                fetch(tt, ws_ref).wait()

        o_ref[...] = _dot(h_ref[...], wbuf[t % 2], NN_DIMS)

        def schedule(first_x):
            on = c == (1 if first_x else 0)
            (f_dev, f_id), (g_dev, g_id) = ((x_nbr, x_id), (y_nbr, y_id)) if first_x else ((y_nbr, y_id), (x_nbr, x_id))
            kf, kg = (1, 2) if first_x else (2, 1)
            pf, pg = (4, 5) if first_x else (5, 4)

            @pl.when((t == 0) & (i == 0) & on)
            def _():
                copy(kf, me, f_dev, src=ws_ref).start()

            def event(nxt):
                if nxt == 1:
                    copy(0, sib_id, here).wait_recv()
                    return sib_id
                if nxt == 2:
                    copy(kf, f_id, here).wait_recv()
                    copy(kf, me, f_dev, src=ws_ref).wait_send()
                    copy(kg, me, g_dev, src=ws_ref).start()
                    pass_on(first_x)
                    return f_id
                if nxt == 3:
                    copy(pg, g_id + 1 - 2 * c, here).wait_recv()
                    return g_id + 1 - 2 * c
                if nxt == 4:
                    copy(kg, g_id, here).wait_recv()
                    pass_on(not first_x)
                    return g_id
                if nxt == 5:
                    copy(pf, f_id + 1 - 2 * c, here).wait_recv()
                    return f_id + 1 - 2 * c
                if nxt == 6:
                    half(7, d_id, True, here).wait_recv()
                    half(8, d_id, False, here).wait_recv()
                    copy(6, d_id, sibling).start()
                    return d_id
                copy(6, d_id + 1 - 2 * c, here).wait_recv()
                return d_id + 1 - 2 * c

            for tt in range(N_DEV - 1):
                @pl.when((t == tt) & (i == ahead) & on)
                def _(tt=tt):
                    fetch(tt + 1, win(event(tt + 1))).start()

            @pl.when((t == N_DEV - 1) & (i == nm - 1) & on)
            def _():
                copy(kg, me, g_dev, src=ws_ref).wait_send()

        schedule(True)
        schedule(False)

        @pl.when((t == N_DEV - 1) & (i == nm - 1))
        def _():
            copy(0, me, sibling, src=ws_ref).wait_send()
            half(7, y_id, True, x_nbr).wait_send()
            half(8, x_id, False, y_nbr).wait_send()
            for j, (_, dev_id) in enumerate(chips):
                copy(4 + j, dev_id, sibling).wait_send()
            own.wait()
            for cp in comm.copies(c_ins, c_outs, c_send, c_recv):
                cp.wait()

    hbm = pl.BlockSpec(memory_space=pl.ANY)
    res = pl.pallas_call(
        body,
        grid_spec=pltpu.PrefetchScalarGridSpec(
            num_scalar_prefetch=1, grid=(N_DEV, nm),
            in_specs=[pl.BlockSpec((tm, K), lambda t, i, order_ref: (i, 0)), hbm] + [hbm] * n_ci,
            out_specs=[pl.BlockSpec((tm, C), lambda t, i, order_ref: (i, order_ref[t])), hbm] + [hbm] * n_co,
            scratch_shapes=[pltpu.VMEM((2, K, C), bf16), pltpu.SemaphoreType.DMA((9,)), pltpu.SemaphoreType.DMA((9,)),
                            pltpu.SemaphoreType.DMA, pltpu.SemaphoreType.DMA((2,))] + comm.scratch()),
        out_shape=[jax.ShapeDtypeStruct((T, N_DEV * C), f32), jax.ShapeDtypeStruct((K, N_DEV * C), bf16)] + comm.out_shapes,
        compiler_params=_params(("arbitrary", "arbitrary")), name="proj_gather",
    )(order, h, w_shard, *comm.ins)
    return res[0], res[1], list(res[2:])


N_CHIP = 4


def _shard_shape(g, ax):
    return tuple(d // N_DEV if i == ax else d for i, d in enumerate(g.shape))


def _sibling_comm(grads, axes):
    n = len(grads)
    sizes = [g.shape[ax] // N_DEV for g, ax in zip(grads, axes)]

    def copies(ins, lands, send_sems, recv_sems, base=0):
        sibling, _ = _peer(1)
        out = []
        for j in range(N_CHIP):
            _, owner = _peer(2 * j + 1)
            for a in range(n):
                out.append(pltpu.make_async_remote_copy(
                    src_ref=_window(ins[a], axes[a], sizes[a], owner), dst_ref=lands[a].at[j],
                    send_sem=send_sems.at[base + a, j], recv_sem=recv_sems.at[base + a, j], device_id=sibling,
                    device_id_type=MESH))
        return out

    shapes = [jax.ShapeDtypeStruct((N_CHIP,) + _shard_shape(g, ax), g.dtype) for g, ax in zip(grads, axes)]
    return _Comm(grads, shapes, (n, N_CHIP), copies)


def _chips_comm(sums):
    n = len(sums)

    def copies(ins, lands, send_sems, recv_sems, base=0):
        out = []
        for j in range(1, N_CHIP):
            dev, _ = _peer(2 * j)
            for a in range(n):
                out.append(pltpu.make_async_remote_copy(
                    src_ref=ins[a].at[j - 1], dst_ref=lands[a].at[j - 1],
                    send_sem=send_sems.at[base + a, j - 1], recv_sem=recv_sems.at[base + a, j - 1], device_id=dev,
                    device_id_type=MESH))
        return out

    return _Comm(sums, [jax.ShapeDtypeStruct(s.shape, s.dtype) for s in sums], (n, N_CHIP), copies)


def _join(c1, c2):
    n1, m1, k1 = len(c1.ins), len(c1.out_shapes), c1.sem_shape[0]

    def copies(ins, lands, send_sems, recv_sems):
        return (c1.copies(ins[:n1], lands[:m1], send_sems, recv_sems, base=0)
                + c2.copies(ins[n1:], lands[m1:], send_sems, recv_sems, base=k1))

    aliases = {**c1.aliases, **{n1 + k: m1 + v for k, v in c2.aliases.items()}}
    return _Comm(c1.ins + c2.ins, c1.out_shapes + c2.out_shapes, (k1 + c2.sem_shape[0], N_CHIP), copies, aliases)


def _chips_first_hop(sums):
    n = len(sums)

    def copies(ins, outs, send_sems, recv_sems, base=0):
        lands, relays = outs[:n], outs[n:]
        (y_dev, _), (x_dev, _) = _peer(2), _peer(4)
        out = []
        for a in range(n):
            half = sums[a].shape[1] // 2
            for k, (src, dst, dev) in enumerate((
                    (ins[a].at[0], lands[a].at[0], y_dev), (ins[a].at[1], lands[a].at[1], x_dev),
                    (ins[a].at[2, pl.ds(0, half)], relays[a].at[0], x_dev),
                    (ins[a].at[2, pl.ds(half, half)], relays[a].at[1], y_dev))):
                out.append(pltpu.make_async_remote_copy(
                    src_ref=src, dst_ref=dst, send_sem=send_sems.at[base + a, k], recv_sem=recv_sems.at[base + a, k],
                    device_id=dev, device_id_type=MESH))
        return out

    shapes = ([jax.ShapeDtypeStruct(s.shape, s.dtype) for s in sums]
              + [jax.ShapeDtypeStruct((2, s.shape[1] // 2) + s.shape[2:], s.dtype) for s in sums])
    return _Comm(sums, shapes, (n, N_CHIP), copies)


def _chips_relay(relays, lands):
    n = len(relays)

    def copies(ins, outs, send_sems, recv_sems, base=0):
        (y_dev, _), (x_dev, _) = _peer(2), _peer(4)
        out = []
        for a in range(n):
            half = relays[a].shape[1]
            for k, (rows, dev) in enumerate(((pl.ds(0, half), y_dev), (pl.ds(half, half), x_dev))):
                out.append(pltpu.make_async_remote_copy(
                    src_ref=ins[a].at[k], dst_ref=outs[a].at[2, rows], send_sem=send_sems.at[base + a, k],
                    recv_sem=recv_sems.at[base + a, k], device_id=dev, device_id_type=MESH))
        return out

    return _Comm(list(relays) + list(lands), [jax.ShapeDtypeStruct(l.shape, l.dtype) for l in lands], (n, N_CHIP), copies,
                 aliases={n + a: a for a in range(n)})


def _rs_chip_sum(grad, land, xyc, axis, name):
    R, C = land.shape[1:]
    tr = _pick(R, max(8, (640 * 1024) // C), 8)

    def body(xyc_ref, g_ref, l_ref, o_ref):
        o_ref[0] = (g_ref[...] + l_ref[0]).astype(bf16)

    def owner(j, xyc_ref):
        jj = j + 1
        return 4 * (xyc_ref[0] ^ (jj >> 1)) + 2 * (xyc_ref[1] ^ (jj & 1)) + xyc_ref[2]

    if axis == 0:
        g_spec = pl.BlockSpec((tr, C), lambda j, i, xyc_ref: (owner(j, xyc_ref) * (R // tr) + i, 0))
    else:
        g_spec = pl.BlockSpec((tr, C), lambda j, i, xyc_ref: (i, owner(j, xyc_ref)))
    return pl.pallas_call(
        body,
        grid_spec=pltpu.PrefetchScalarGridSpec(
            num_scalar_prefetch=1, grid=(N_CHIP - 1, R // tr),
            in_specs=[g_spec, pl.BlockSpec((1, tr, C), lambda j, i, xyc_ref: (j + 1, i, 0))],
            out_specs=pl.BlockSpec((1, tr, C), lambda j, i, xyc_ref: (j, i, 0))),
        out_shape=jax.ShapeDtypeStruct((N_CHIP - 1, R, C), bf16),
        compiler_params=_params(("parallel", "parallel")), name=name,
    )(xyc, grad, land)


def _allreduce_small(part):
    R = part.shape[0]

    def body(p_ref, o_ref, buf, send_sems, recv_sems):
        (x, y, c), me = _me()
        buf[me] = p_ref[...]
        copies = []
        for k in range(1, N_DEV):
            dev, dev_id = _peer(k)
            copies.append(pltpu.make_async_remote_copy(
                src_ref=p_ref, dst_ref=buf.at[me], send_sem=send_sems.at[k - 1], recv_sem=recv_sems.at[k - 1],
                device_id=dev, device_id_type=MESH))
        for cp in copies:
            cp.start()
        for k in range(1, N_DEV):
            _, dev_id = _peer(k)
            pltpu.make_async_remote_copy(
                src_ref=p_ref, dst_ref=buf.at[dev_id], send_sem=send_sems.at[k - 1], recv_sem=recv_sems.at[k - 1],
                device_id=(x, y, c), device_id_type=MESH).wait_recv()
        for cp in copies:
            cp.wait_send()
        acc = buf[0]
        for d in range(1, N_DEV):
            acc = acc + buf[d]
        o_ref[...] = acc

    return pl.pallas_call(
        body, in_specs=[pl.BlockSpec(memory_space=pltpu.VMEM)], out_specs=pl.BlockSpec(memory_space=pltpu.VMEM),
        out_shape=jax.ShapeDtypeStruct((R, LANES), f32),
        scratch_shapes=[pltpu.VMEM((N_DEV, R, LANES), f32), pltpu.SemaphoreType.DMA((N_DEV - 1,)), pltpu.SemaphoreType.DMA((N_DEV - 1,))],
        name="allreduce_small",
    )(part)


def _adamw_math(w, g, m, v):
    m2 = ADAM_B1 * m + (1.0 - ADAM_B1) * g
    v2 = ADAM_B2 * v + (1.0 - ADAM_B2) * (g * g)
    m_hat = m2 / (1.0 - ADAM_B1 ** ADAM_STEP)
    v_hat = v2 / (1.0 - ADAM_B2 ** ADAM_STEP)
    return -ADAM_LR * (m_hat / (jnp.sqrt(v_hat) + ADAM_EPS) + ADAM_WD * w), m2, v2


def _adamw_shard(grad, land_sib, land_chips, w, m, v, me, axis, name, row0=0, prev=None):
    R, C = land_sib.shape[1:]
    assert axis == 1 or R == w.shape[0]
    tr = _pick(R, max(8, (320 * 1024) // C), 8)
    r0 = row0 // tr
    n_prev = len(prev) if prev else 0

    def body(me_ref, g_ref, s_ref, l_ref, w_ref, m_ref, v_ref, *rest):
        go_ref, d_ref, mo_ref, vo_ref = rest[n_prev:]
        g = g_ref[...] + s_ref[0]
        for j in range(N_CHIP - 1):
            g = g + l_ref[j].astype(f32)
        d, m2, v2 = _adamw_math(w_ref[...], g, m_ref[...], v_ref[...])
        go_ref[...] = g
        d_ref[...] = d
        mo_ref[...] = m2
        vo_ref[...] = v2

    if axis == 0:
        g_spec = pl.BlockSpec((tr, C), lambda i, me_ref: (me_ref[0] * (R // tr) + i, 0))
    else:
        g_spec = pl.BlockSpec((tr, C), lambda i, me_ref: (i, me_ref[0]))
    blk = pl.BlockSpec((tr, C), lambda i, me_ref: (r0 + i, 0))
    return pl.pallas_call(
        body,
        grid_spec=pltpu.PrefetchScalarGridSpec(
            num_scalar_prefetch=1, grid=(R // tr,),
            in_specs=[g_spec, pl.BlockSpec((1, tr, C), lambda i, me_ref: (0, i, 0)),
                      pl.BlockSpec((N_CHIP - 1, tr, C), lambda i, me_ref: (0, i, 0)), blk, blk, blk]
            + [pl.BlockSpec(memory_space=pl.ANY)] * n_prev,
            out_specs=[blk] * 4),
        out_shape=[jax.ShapeDtypeStruct(w.shape, f32)] * 4,
        input_output_aliases={7 + i: i for i in range(n_prev)},
        compiler_params=_params(("parallel",)), name=name,
    )(me, grad, land_sib, land_chips, w, m, v, *(prev or []))


def _adamw_small(g, w, m, v):
    def body(g_ref, w_ref, m_ref, v_ref, d_ref, mo_ref, vo_ref):
        d, m2, v2 = _adamw_math(w_ref[...], g_ref[...], m_ref[...], v_ref[...])
        d_ref[...] = d
        mo_ref[...] = m2
        vo_ref[...] = v2

    return pl.pallas_call(body, out_shape=[jax.ShapeDtypeStruct(g.shape, f32)] * 3, name="adamw_small")(g, w, m, v)


def _pack_rows(parts, total_rows):
    rows = jnp.concatenate([p.reshape(-1, LANES) for p in parts], axis=0)
    return jnp.pad(rows, ((0, total_rows - rows.shape[0]), (0, 0)))


BIG = ("w_in", "mem_w_kv", "w_br_attn", "w_br_conv", "w_br_mem", "w_out")
BIG_AXIS = {"w_in": 1, "mem_w_kv": 0, "w_br_attn": 1, "w_br_conv": 1, "w_br_mem": 1, "w_out": 0}
SMALL = ("norm_g", "mem_norm_g", "attn_q_norm", "attn_k_norm", "mem_q_norm", "mem_k_norm")
ALL_W = ("norm_g", "mem_norm_g", "w_in", "attn_q_norm", "attn_k_norm", "conv_w", "mem_w_kv", "mem_q_norm", "mem_k_norm",
         "w_br_attn", "w_br_conv", "w_br_mem", "w_out")


def kernel(x, mem, norm_g, mem_norm_g, w_in, attn_q_norm, attn_k_norm, conv_w, mem_w_kv, mem_q_norm, mem_k_norm, w_br_attn, w_br_conv, w_br_mem, w_out, loss_target, m_norm_g, m_mem_norm_g, m_w_in, m_attn_q_norm, m_attn_k_norm, m_conv_w, m_mem_w_kv, m_mem_q_norm, m_mem_k_norm, m_w_br_attn, m_w_br_conv, m_w_br_mem, m_w_out, v_norm_g, v_mem_norm_g, v_w_in, v_attn_q_norm, v_attn_k_norm, v_conv_w, v_mem_w_kv, v_mem_q_norm, v_mem_k_norm, v_w_br_attn, v_w_br_conv, v_w_br_mem, v_w_out):
    w = dict(norm_g=norm_g, mem_norm_g=mem_norm_g, w_in=w_in, attn_q_norm=attn_q_norm, attn_k_norm=attn_k_norm, conv_w=conv_w,
             mem_w_kv=mem_w_kv, mem_q_norm=mem_q_norm, mem_k_norm=mem_k_norm, w_br_attn=w_br_attn, w_br_conv=w_br_conv,
             w_br_mem=w_br_mem, w_out=w_out)
    mo = dict(norm_g=m_norm_g, mem_norm_g=m_mem_norm_g, w_in=m_w_in, attn_q_norm=m_attn_q_norm, attn_k_norm=m_attn_k_norm,
              conv_w=m_conv_w, mem_w_kv=m_mem_w_kv, mem_q_norm=m_mem_q_norm, mem_k_norm=m_mem_k_norm, w_br_attn=m_w_br_attn,
              w_br_conv=m_w_br_conv, w_br_mem=m_w_br_mem, w_out=m_w_out)
    vo = dict(norm_g=v_norm_g, mem_norm_g=v_mem_norm_g, w_in=v_w_in, attn_q_norm=v_attn_q_norm, attn_k_norm=v_attn_k_norm,
              conv_w=v_conv_w, mem_w_kv=v_mem_w_kv, mem_q_norm=v_mem_q_norm, mem_k_norm=v_mem_k_norm, w_br_attn=v_w_br_attn,
              w_br_conv=v_w_br_conv, w_br_mem=v_w_br_mem, w_out=v_w_out)
    B, S, D = x.shape
    me = (4 * lax.axis_index("x") + 2 * lax.axis_index("y") + lax.axis_index("c")).astype(jnp.int32)

    T = B * S
    x2, t2, mem2, ng = x.reshape(T, D), loss_target.reshape(T, D), mem.reshape(B * MEM_LEN, D), norm_g.reshape(1, D)
    h = _rms_fwd(x2, ng, "rms_x")
    rows_sharded, cols_sharded = ("mem_w_kv", "w_out"), ("w_br_attn", "w_br_conv", "w_br_mem")
    later = rows_sharded + cols_sharded
    later_axes = [BIG_AXIS[n] for n in later] + [None]
    conv_pad = jnp.pad(conv_w, ((0, 8 - conv_w.shape[0]), (0, 0)))
    proj, w_in_full, spread_a = _proj_gather(
        h, w_in.astype(bf16), _shard_order(),
        _spread_comm([w[n].astype(bf16) for n in rows_sharded], [BIG_AXIS[n] for n in rows_sharded]), comm_at=N_DEV - 3)
    *attn, spread_b = _attn_fwd(proj, attn_q_norm, attn_k_norm, B, S,
                                comm=_spread_comm([w[n].astype(bf16) for n in cols_sharded] + [conv_pad],
                                                  [BIG_AXIS[n] for n in cols_sharded] + [None]))
    *fulls, conv_g = _forward_blocks(spread_a + spread_b, later_axes)
    wf = dict(zip(later, fulls), w_in=w_in_full)
    conv_full = conv_g[:, :3, :].transpose(1, 0, 2).reshape(3, CONV_W)

    sq_err, g, t = _fwd_bwd_head(x2, mem2, t2, proj, attn, B, S, mem_norm_g, attn_q_norm, attn_k_norm, conv_full, mem_q_norm,
                                 mem_k_norm, wf["mem_w_kv"], wf["w_br_attn"], wf["w_br_conv"], wf["w_br_mem"], wf["w_out"])
    t.update(x2=x2, ng=ng, h=h)

    xyc = jnp.stack([lax.axis_index("x"), lax.axis_index("y"), lax.axis_index("c")]).astype(jnp.int32)
    me1 = me.reshape(1)
    early = ("w_out", "w_br_attn", "w_br_conv", "w_br_mem")
    g["mem_w_kv"], sib_early = _mm(t["mh"], t["dmkv"], mode="tn", out_dtype=f32, name="dw_kv",
                                   comm=_sibling_comm([g[n] for n in early], [BIG_AXIS[n] for n in early]))
    sums_early = [_rs_chip_sum(g[n], ls, xyc, BIG_AXIS[n], "rs_sum_" + n) for n, ls in zip(early, sib_early)]
    tm_w = _pick(D // 2, 1024)
    half = (D // 2) // tm_w
    g_top, (*chips_early, sib_kv) = _mm(t["h"], t["dproj"], mode="tn", out_dtype=f32, name="dw_in_top", tm=tm_w,
                                        m_tiles=(0, half),
                                        comm=_join(_chips_comm(sums_early), _sibling_comm([g["mem_w_kv"]], [0])))
    sum_kv = _rs_chip_sum(g["mem_w_kv"], sib_kv, xyc, 0, "rs_sum_mem_w_kv")
    g_bot, (chips_kv, sib_top) = _mm(t["h"], t["dproj"], mode="tn", out_dtype=f32, name="dw_in_bot", tm=tm_w,
                                     m_tiles=(half, half), comm=_join(_chips_comm([sum_kv]), _sibling_comm([g_top], [1])))
    sum_top = _rs_chip_sum(g_top, sib_top, xyc, 1, "rs_sum_w_in_top")
    tm_t = _pick(T // 2, 1024)
    halfm = (T // 2) // tm_t
    dh, (part_top, relay_top, sib_bot) = _mm(t["dproj"], wf["w_in"], mode="nt", out_dtype=f32, name="d_h_a", tm=tm_t, tk=D_H_TK,
                                             m_tiles=(0, halfm), into="new",
                                             comm=_join(_chips_first_hop([sum_top]), _sibling_comm([g_bot], [1])))
    sum_bot = _rs_chip_sum(g_bot, sib_bot, xyc, 1, "rs_sum_w_in_bot")
    dh, (part_bot, relay_bot, chips_top) = _mm(t["dproj"], wf["w_in"], mode="nt", out_dtype=f32, name="d_h_b", tm=tm_t,
                                               tk=D_H_TK, m_tiles=(halfm, halfm), into=dh,
                                               comm=_join(_chips_first_hop([sum_bot]), _chips_relay([relay_top], [part_top])))
    dx, dng, (chips_bot,) = _rms_bwd(t["x2"], t["ng"], dh, t["dy"], "rms_x_bwd", comm=_chips_relay([relay_bot], [part_bot]))
    g["norm_g"] = dng.reshape(D)

    grad, delta, new_m, new_v = {}, {}, {}, {}
    for n, ls, lc in zip(early + ("mem_w_kv",), sib_early + [sib_kv], chips_early + [chips_kv]):
        grad[n], delta[n], new_m[n], new_v[n] = _adamw_shard(g[n], ls, lc, w[n], mo[n], vo[n], me1, BIG_AXIS[n], "adamw_" + n)
    top = _adamw_shard(g_top, sib_top, chips_top, w_in, m_w_in, v_w_in, me1, 1, "adamw_w_in_top")
    grad["w_in"], delta["w_in"], new_m["w_in"], new_v["w_in"] = _adamw_shard(
        g_bot, sib_bot, chips_bot, w_in, m_w_in, v_w_in, me1, 1, "adamw_w_in_bot", row0=D // 2, prev=top)

    n_rep = sum(w[n].size for n in SMALL) // LANES
    conv_rows = g["conv_w"].reshape(3, N_DEV, LANES).transpose(1, 0, 2).reshape(3 * N_DEV, LANES)
    n_rows = -(-(n_rep + 3 * N_DEV + 1) // 8) * 8
    part = _pack_rows([g[n] for n in SMALL] + [conv_rows, jnp.full((1, LANES), sq_err, f32)], n_rows)
    tot = _allreduce_small(part)
    loss = tot[n_rep + 3 * N_DEV, 0] * (0.5 / D)
    n_small = -(-(n_rep + 3) // 8) * 8
    g_small = _pack_rows([tot[:n_rep], lax.dynamic_slice(tot, (n_rep + 3 * me, 0), (3, LANES))], n_small)
    names = SMALL + ("conv_w",)
    d_s, m_s, v_s = _adamw_small(g_small, _pack_rows([w[n] for n in names], n_small), _pack_rows([mo[n] for n in names], n_small),
                                 _pack_rows([vo[n] for n in names], n_small))
    off = 0
    for n in names:
        r = w[n].size // LANES
        grad[n], delta[n] = g_small[off:off + r].reshape(w[n].shape), d_s[off:off + r].reshape(w[n].shape)
        new_m[n], new_v[n] = m_s[off:off + r].reshape(w[n].shape), v_s[off:off + r].reshape(w[n].shape)
        off += r
    return (loss, dx.reshape(B, S, D), *[grad[n] for n in ALL_W], *[delta[n] for n in ALL_W], *[new_m[n] for n in ALL_W],
            *[new_v[n] for n in ALL_W])
```

```python
import functools

import jax
import jax.numpy as jnp
from jax import lax
from jax.experimental import pallas as pl
from jax.experimental.pallas import tpu as pltpu

f32 = jnp.float32
bf16 = jnp.bfloat16

N_DEV = 8
HEAD_DIM = 128
DILATIONS = (1, 4, 16)
N_GROUPS = 3
HEADS = 4
BLK = 128
ATTN_QKV = N_GROUPS * HEADS * HEAD_DIM
ATTN_OUT = HEADS * HEAD_DIM
CONV_W = 1024
MEM_LEN = 256
MEM_HEADS = 4
MEM_HD = 256
MEM_W = MEM_HEADS * MEM_HD
EPS = 1e-6
Q0, K0, V0 = 0, ATTN_QKV, 2 * ATTN_QKV
ZA = 3 * ATTN_QKV
CB, CC, CV, ZC = ZA + ATTN_OUT, ZA + ATTN_OUT + CONV_W, ZA + ATTN_OUT + 2 * CONV_W, ZA + ATTN_OUT + 3 * CONV_W
MQ = ZC + CONV_W
ZM = MQ + MEM_W
GT = ZM + MEM_W

ADAM_LR, ADAM_B1, ADAM_B2, ADAM_EPS, ADAM_WD, ADAM_STEP = 0.001, 0.9, 0.999, 1e-08, 0.01, 10

VMEM_LIMIT = 56 * 1024 * 1024
D_H_TK = 4352
LANES = 128

NT_DIMS = (((1,), (1,)), ((), ()))
TN_DIMS = (((0,), (0,)), ((), ()))
NN_DIMS = (((1,), (0,)), ((), ()))


def _params(sem=None):
    return pltpu.CompilerParams(dimension_semantics=sem, vmem_limit_bytes=VMEM_LIMIT)


def _pick(n, pref, q=LANES):
    t = (min(pref, n) // q) * q
    while t >= q:
        if n % t == 0:
            return t
        t -= q
    return n


def _dot(a, b, dims):
    return lax.dot_general(a.astype(bf16), b.astype(bf16), dims, preferred_element_type=f32)


def _sigmoid(z):
    return 0.5 * jnp.tanh(0.5 * z) + 0.5


def _rstd(v):
    return lax.rsqrt(jnp.mean(v * v, axis=-1, keepdims=True) + EPS)


class _Deferred:
    def __init__(self, stage, sems, step, last, n):
        assert last >= 1
        self.stage, self.sems, self.step, self.last, self.n = stage, sems, step, last, n
        self.slot = step % 2

    def _drain(self, slot, like):
        for c in range(self.n):
            pltpu.make_async_copy(self.stage.at[slot, c], like(c), self.sems.at[slot, c]).wait()

    def begin(self, like):
        pl.when(self.step >= 2)(lambda: self._drain(self.slot, like))

    def start(self, c, dst):
        pltpu.make_async_copy(self.stage.at[self.slot, c], dst, self.sems.at[self.slot, c]).start()

    def end(self, like):
        @pl.when(self.step == self.last)
        def _():
            self._drain(self.slot, like)
            self._drain(1 - self.slot, like)


def _row_sum(v):
    hi = v.astype(bf16)
    lo = (v - hi.astype(f32)).astype(bf16)
    ones = jnp.ones((v.shape[1], v.shape[1]), bf16)
    return _dot(hi, ones, NN_DIMS) + _dot(lo, ones, NN_DIMS)


def _rstd_head(v):
    return lax.rsqrt(_row_sum(v * v) * (1.0 / v.shape[1]) + EPS)


class _Comm:
    def __init__(self, ins, out_shapes, sem_shape, copies, aliases=None):
        self.ins, self.out_shapes, self.sem_shape, self.copies = list(ins), list(out_shapes), sem_shape, copies
        self.aliases = dict(aliases or {})

    def scratch(self):
        return [pltpu.SemaphoreType.DMA(self.sem_shape), pltpu.SemaphoreType.DMA(self.sem_shape)]

    def io_aliases(self, first_in, first_out):
        return {first_in + k: first_out + v for k, v in self.aliases.items()}


def _mm(a, b, *, mode, out_dtype, name, tm=1024, tn=1024, tk=4096, m_tiles=None, into=None, comm=None):
    if mode == "nn":
        (M, K), (K2, N) = a.shape, b.shape
    elif mode == "nt":
        (M, K), (N, K2) = a.shape, b.shape
    else:
        (K, M), (K2, N) = a.shape, b.shape
    assert K == K2
    tm, tn, tk = _pick(M, tm), _pick(N, tn), _pick(K, tk)
    nk = K // tk
    m0, mc = m_tiles if m_tiles is not None else (0, M // tm)
    o0 = 0 if into is None else m0
    aliased = into is not None and not isinstance(into, str)
    n_ci = len(comm.ins) if comm else 0
    n_co = len(comm.out_shapes) if comm else 0
    grid = (mc, N // tn, nk)
    dims = {"nn": NN_DIMS, "nt": NT_DIMS, "tn": TN_DIMS}[mode]

    def body(*refs, nk):
        a_ref, b_ref = refs[:2]
        pos = 3 if aliased else 2
        c_ins, o_ref, c_outs = refs[pos:pos + n_ci], refs[pos + n_ci], refs[pos + n_ci + 1:pos + n_ci + 1 + n_co]
        scratch = refs[pos + n_ci + 1 + n_co:]
        ids = [pl.program_id(d) for d in range(3)]
        if comm:
            sems = scratch[-2:]

            @pl.when((ids[0] == 0) & (ids[1] == 0) & (ids[2] == 0))
            def _():
                for cp in comm.copies(c_ins, c_outs, *sems):
                    cp.start()

        if nk == 1:
            o_ref[...] = _dot(a_ref[...], b_ref[...], dims).astype(o_ref.dtype)
        else:
            acc_ref, k = scratch[0], ids[2]

            @pl.when(k == 0)
            def _():
                acc_ref[...] = _dot(a_ref[...], b_ref[...], dims)

            @pl.when((k > 0) & (k < nk - 1))
            def _():
                acc_ref[...] += _dot(a_ref[...], b_ref[...], dims)

            @pl.when(k == nk - 1)
            def _():
                o_ref[...] = (acc_ref[...] + _dot(a_ref[...], b_ref[...], dims)).astype(o_ref.dtype)

        if comm:
            @pl.when((ids[0] == grid[0] - 1) & (ids[1] == grid[1] - 1) & (ids[2] == grid[2] - 1))
            def _():
                for cp in comm.copies(c_ins, c_outs, *sems):
                    cp.wait()

    if mode == "tn":
        a_spec = pl.BlockSpec((tk, tm), lambda i, j, k: (k, m0 + i))
    else:
        a_spec = pl.BlockSpec((tm, tk), lambda i, j, k: (m0 + i, k))
    if mode == "nt":
        b_spec = pl.BlockSpec((tn, tk), lambda i, j, k: (j, k))
    else:
        b_spec = pl.BlockSpec((tk, tn), lambda i, j, k: (k, j))
    hbm = pl.BlockSpec(memory_space=pl.ANY)
    res = pl.pallas_call(
        functools.partial(body, nk=nk),
        grid=grid,
        in_specs=[a_spec, b_spec] + [hbm] * (aliased + n_ci),
        out_specs=[pl.BlockSpec((tm, tn), lambda i, j, k: (o0 + i, j))] + [hbm] * n_co,
        out_shape=[jax.ShapeDtypeStruct((mc * tm if into is None else M, N), out_dtype)] + (comm.out_shapes if comm else []),
        scratch_shapes=([pltpu.VMEM((tm, tn), f32)] if nk > 1 else []) + (comm.scratch() if comm else []),
        input_output_aliases={**({2: 0} if aliased else {}), **(comm.io_aliases(2 + aliased, 1) if comm else {})},
        compiler_params=_params(("arbitrary",) * 3 if comm else ("parallel", "parallel", "arbitrary")),
        name=name,
    )(a, b, *([into] if aliased else []), *(comm.ins if comm else []))
    return (res[0], list(res[1:])) if comm else res[0]


def _rms_fwd(x, g, name):
    T, D = x.shape
    tm = _pick(T, 256, 8)

    def body(x_ref, g_ref, h_ref):
        xv = x_ref[...]
        h_ref[...] = (xv * _rstd(xv) * g_ref[...]).astype(bf16)

    return pl.pallas_call(
        body, grid=(T // tm,),
        in_specs=[pl.BlockSpec((tm, D), lambda i: (i, 0)), pl.BlockSpec((1, D), lambda i: (0, 0))],
        out_specs=pl.BlockSpec((tm, D), lambda i: (i, 0)),
        out_shape=jax.ShapeDtypeStruct((T, D), bf16),
        compiler_params=_params(("parallel",)), name=name,
    )(x, g)


def _rms_bwd(x, g, dh, dy, name, comm=None):
    T, D = x.shape
    tm = _pick(T, 256, 8)
    with_dx = dy is not None
    n_ci = len(comm.ins) if comm else 0
    n_co = len(comm.out_shapes) if comm else 0

    def body(*refs):
        if with_dx:
            x_ref, g_ref, dh_ref, dy_ref = refs[:4]
            c_ins, (dx_ref, dg_ref) = refs[4:4 + n_ci], refs[4 + n_ci:6 + n_ci]
            c_outs, sems = refs[6 + n_ci:6 + n_ci + n_co], refs[6 + n_ci + n_co:]
        else:
            x_ref, g_ref, dh_ref, dg_ref = refs
        if comm:
            @pl.when(pl.program_id(0) == 0)
            def _():
                for cp in comm.copies(c_ins, c_outs, *sems):
                    cp.start()

        xv = x_ref[...]
        r = _rstd(xv)
        xh = xv * r
        dhv = dh_ref[...]
        part = jnp.sum(dhv * xh, axis=0, keepdims=True)

        @pl.when(pl.program_id(0) == 0)
        def _():
            dg_ref[...] = part

        @pl.when(pl.program_id(0) > 0)
        def _():
            dg_ref[...] += part

        if with_dx:
            dxh = dhv * g_ref[...]
            dx_ref[...] = dy_ref[...] + r * (dxh - xh * jnp.mean(dxh * xh, axis=-1, keepdims=True))

        if comm:
            @pl.when(pl.program_id(0) == T // tm - 1)
            def _():
                for cp in comm.copies(c_ins, c_outs, *sems):
                    cp.wait()

    row = pl.BlockSpec((tm, D), lambda i: (i, 0))
    vec = pl.BlockSpec((1, D), lambda i: (0, 0))
    hbm = pl.BlockSpec(memory_space=pl.ANY)
    if with_dx:
        res = pl.pallas_call(
            body, grid=(T // tm,), in_specs=[row, vec, row, row] + [hbm] * n_ci, out_specs=[row, vec] + [hbm] * n_co,
            out_shape=[jax.ShapeDtypeStruct((T, D), f32), jax.ShapeDtypeStruct((1, D), f32)] + (comm.out_shapes if comm else []),
            scratch_shapes=comm.scratch() if comm else [],
            input_output_aliases=comm.io_aliases(4, 2) if comm else {},
            compiler_params=_params(("arbitrary",)), name=name,
        )(x, g, dh, dy, *(comm.ins if comm else []))
        return (res[0], res[1], list(res[2:])) if comm else res
    return pl.pallas_call(
        body, grid=(T // tm,), in_specs=[row, vec, row], out_specs=vec,
        out_shape=jax.ShapeDtypeStruct((1, D), f32),
        compiler_params=_params(("arbitrary",)), name=name,
    )(x, g, dh)


MAX_UNIT_UNROLL = 6


def _unit_unroll(trips):
    return max(u for u in range(1, MAX_UNIT_UNROLL + 1) if trips % u == 0)


def _rows(start, size, stride):
    return pl.ds(start, size) if stride == 1 else pl.ds(start, size, stride=stride)


def _attn_units(S, d, first, prev):
    nb = S // d // BLK

    def do_first(r, c):
        first(_rows(r, BLK, d))
        return c

    lax.fori_loop(0, d, do_first, 0, unroll=_unit_unroll(d))
    if nb > 1:
        def do_prev(u, c):
            r = u // (nb - 1)
            b = u % (nb - 1) + 1
            base = r + d * b * BLK
            prev(_rows(base, BLK, d), _rows(base - d * BLK, 2 * BLK, d))
            return c

        lax.fori_loop(0, d * (nb - 1), do_prev, 0, unroll=_unit_unroll(d * (nb - 1)))


def _band_bias(nkeys):
    i = lax.broadcasted_iota(jnp.int32, (BLK, nkeys), 0)
    j = lax.broadcasted_iota(jnp.int32, (BLK, nkeys), 1)
    ok = (j <= i) if nkeys == BLK else ((j >= i) & (j <= i + BLK))
    return jnp.where(ok, 0.0, -jnp.inf).astype(f32)


def _normalize_into(src_ref, gain, dst_ref, S):
    for c in range(S // 256):
        rows = pl.ds(c * 256, 256)
        v = src_ref[rows, :]
        dst_ref[rows, :] = v * _rstd_head(v) * gain


def _attn_fwd(proj, gq, gk, B, S, comm=None):
    T, NC = proj.shape
    scale = HEAD_DIM ** -0.5
    n_ci = len(comm.ins) if comm else 0
    n_co = len(comm.out_shapes) if comm else 0

    def body(*refs):
        q_ref, k_ref, v_ref, z_ref, gq_ref, gk_ref = refs[:6]
        c_ins = refs[6:6 + n_ci]
        ag_ref, a_ref, lse_ref = refs[6 + n_ci:9 + n_ci]
        c_outs = refs[9 + n_ci:9 + n_ci + n_co]
        qs, ks, o0, o1, o2, l0, l1, l2, bias1, bias2 = refs[9 + n_ci + n_co:19 + n_ci + n_co]
        sems = refs[19 + n_ci + n_co:]
        g = pl.program_id(2)
        if comm:
            @pl.when((pl.program_id(0) == 0) & (pl.program_id(1) == 0) & (g == 0))
            def _():
                for cp in comm.copies(c_ins, c_outs, *sems):
                    cp.start()

        bias1[...] = _band_bias(BLK)
        bias2[...] = _band_bias(2 * BLK)
        _normalize_into(q_ref, gq_ref[pl.ds(g, 1), :], qs, S)
        _normalize_into(k_ref, gk_ref[pl.ds(g, 1), :], ks, S)
        o_s, l_s = (o0, o1, o2), (l0, l1, l2)

        def run(gi):
            def unit(qr, kr, nkeys):
                s = _dot(qs[qr, :], ks[kr, :], NT_DIMS) * scale + (bias1 if nkeys == BLK else bias2)[...]
                m = jnp.max(s, axis=-1, keepdims=True)
                p = jnp.exp(s - m)
                den = jnp.sum(p, axis=-1, keepdims=True)
                o_s[gi][qr, :] = _dot(p, v_ref[kr, :], NN_DIMS) * (1.0 / den)
                l_s[gi][qr, :] = jnp.broadcast_to(m + jnp.log(den), (BLK, HEAD_DIM))

            _attn_units(S, DILATIONS[gi], lambda qr: unit(qr, qr, BLK), lambda qr, kr: unit(qr, kr, 2 * BLK))

        for gi in range(N_GROUPS):
            pl.when(g == gi)(functools.partial(run, gi))

        @pl.when(g == N_GROUPS - 1)
        def _():
            for c in range(S // 256):
                rows = pl.ds(c * 256, 256)
                la, lb, lc = l0[rows, :], l1[rows, :], l2[rows, :]
                m = jnp.maximum(jnp.maximum(la, lb), lc)
                wa, wb, wc = jnp.exp(la - m), jnp.exp(lb - m), jnp.exp(lc - m)
                tot = wa + wb + wc
                a = (wa / tot) * o0[rows, :] + (wb / tot) * o1[rows, :] + (wc / tot) * o2[rows, :]
                z = z_ref[rows, :]
                a_ref[rows, :] = a
                lse_ref[rows, :] = m + jnp.log(tot)
                ag_ref[rows, :] = (a * (z * _sigmoid(z))).astype(bf16)

        if comm:
            @pl.when((pl.program_id(0) == B - 1) & (pl.program_id(1) == HEADS - 1) & (g == N_GROUPS - 1))
            def _():
                for cp in comm.copies(c_ins, c_outs, *sems):
                    cp.wait()

    def slab(col0):
        return pl.BlockSpec((S, HEAD_DIM), lambda b, h, g: (b, col0 // HEAD_DIM + g * HEADS + h))

    gain = pl.BlockSpec((N_GROUPS, HEAD_DIM), lambda b, h, g: (0, 0))
    out = pl.BlockSpec((S, HEAD_DIM), lambda b, h, g: (b, h))
    hbm = pl.BlockSpec(memory_space=pl.ANY)
    res = pl.pallas_call(
        body, grid=(B, HEADS, N_GROUPS),
        in_specs=[slab(Q0), slab(K0), slab(V0), pl.BlockSpec((S, HEAD_DIM), lambda b, h, g: (b, ZA // HEAD_DIM + h)), gain, gain]
        + [hbm] * n_ci,
        out_specs=[out, out, out] + [hbm] * n_co,
        out_shape=[jax.ShapeDtypeStruct((T, ATTN_OUT), bf16), jax.ShapeDtypeStruct((T, ATTN_OUT), f32),
                   jax.ShapeDtypeStruct((T, ATTN_OUT), f32)] + (comm.out_shapes if comm else []),
        scratch_shapes=[pltpu.VMEM((S, HEAD_DIM), f32)] * 8 + [pltpu.VMEM((BLK, BLK), f32), pltpu.VMEM((BLK, 2 * BLK), f32)]
        + (comm.scratch() if comm else []),
        compiler_params=_params(("arbitrary", "arbitrary", "arbitrary")), name="attn_fwd",
    )(proj, proj, proj, proj, gq, gk, *(comm.ins if comm else []))
    return res[0], res[1], res[2], list(res[3:])


def _rms_bwd_rows(raw, gain, dn, on_mxu=True):
    r = _rstd_head(raw) if on_mxu else _rstd(raw)
    xh = raw * r
    dxh = dn * gain
    if on_mxu:
        mean = _row_sum(dxh * xh) * (1.0 / raw.shape[1])
    else:
        mean = jnp.mean(dxh * xh, axis=-1, keepdims=True)
    return r * (dxh - xh * mean), jnp.sum(dn * xh, axis=0, keepdims=True)


def _attn_bwd(proj, gq, gk, a_comb, lse, dag, dproj, B, S):
    T, NC = proj.shape
    scale = HEAD_DIM ** -0.5

    def body(q_ref, k_ref, v_ref, z_ref, gq_ref, gk_ref, a_ref, lse_ref, dag_ref, dproj_in, dproj_ref, dgq_ref, dgk_ref,
             qs, ks, da_s, dl_s, dq_s, dk_s, dv_s, bias1, bias2, stage, dz_stage, sem, dz_sem):
        b, h, g = pl.program_id(0), pl.program_id(1), pl.program_id(2)
        bias1[...] = _band_bias(BLK)
        bias2[...] = _band_bias(2 * BLK)
        gq_row, gk_row = gq_ref[pl.ds(g, 1), :], gk_ref[pl.ds(g, 1), :]
        _normalize_into(q_ref, gq_row, qs, S)
        _normalize_into(k_ref, gk_row, ks, S)

        def dst(c):
            return dproj_ref.at[pl.ds(b * S, S), pl.ds((Q0, K0, V0)[c] + (g * HEADS + h) * HEAD_DIM, HEAD_DIM)]

        out = _Deferred(stage, sem, (b * HEADS + h) * N_GROUPS + g, B * HEADS * N_GROUPS - 1, 3)
        out.begin(dst)

        @pl.when((b == 0) & (h == 0) & (g == 0))
        def _():
            dgq_ref[...] = jnp.zeros_like(dgq_ref)
            dgk_ref[...] = jnp.zeros_like(dgk_ref)

        @pl.when(g == 0)
        def _():
            for c in range(S // 256):
                rows = pl.ds(c * 256, 256)
                z, a, dg_ = z_ref[rows, :], a_ref[rows, :], dag_ref[rows, :]
                sg = _sigmoid(z)
                da = dg_ * (z * sg)
                da_s[rows, :] = da
                dl_s[rows, :] = _row_sum(da * a)
                dz_stage[rows, :] = (dg_ * a * (sg * (1.0 + z * (1.0 - sg)))).astype(bf16)
            cp = pltpu.make_async_copy(dz_stage, dproj_ref.at[pl.ds(b * S, S), pl.ds(ZA + h * HEAD_DIM, HEAD_DIM)], dz_sem)
            cp.start()
            cp.wait()

        dk_s[...] = jnp.zeros_like(dk_s)
        dv_s[...] = jnp.zeros_like(dv_s)

        def run(gi):
            def unit(qr, kr, nkeys):
                q, k, v = qs[qr, :], ks[kr, :], v_ref[kr, :]
                s = _dot(q, k, NT_DIMS) * scale
                p = jnp.exp(s + (bias1 if nkeys == BLK else bias2)[...] - lse_ref[qr, :][:, :1])
                da = da_s[qr, :]
                dp = _dot(da, v, NT_DIMS)
                ds = p * (dp - dl_s[qr, :][:, :1]) * scale
                dq_s[qr, :] = _dot(ds, k, NN_DIMS)
                dk_s[kr, :] += _dot(ds, q, TN_DIMS)
                dv_s[kr, :] += _dot(p, da, TN_DIMS)

            _attn_units(S, DILATIONS[gi], lambda qr: unit(qr, qr, BLK), lambda qr, kr: unit(qr, kr, 2 * BLK))

        for gi in range(N_GROUPS):
            pl.when(g == gi)(functools.partial(run, gi))

        gq_acc = jnp.zeros((1, HEAD_DIM), f32)
        gk_acc = jnp.zeros((1, HEAD_DIM), f32)
        for c in range(S // 256):
            rows = pl.ds(c * 256, 256)
            dq, gq_p = _rms_bwd_rows(q_ref[rows, :], gq_row, dq_s[rows, :])
            dk, gk_p = _rms_bwd_rows(k_ref[rows, :], gk_row, dk_s[rows, :])
            gq_acc, gk_acc = gq_acc + gq_p, gk_acc + gk_p
            stage[out.slot, 0, rows, :] = dq.astype(bf16)
            stage[out.slot, 1, rows, :] = dk.astype(bf16)
            stage[out.slot, 2, rows, :] = dv_s[rows, :].astype(bf16)
        dgq_ref[pl.ds(g, 1), :] += gq_acc
        dgk_ref[pl.ds(g, 1), :] += gk_acc
        for c in range(3):
            out.start(c, dst(c))
        out.end(dst)

    def slab(col0):
        return pl.BlockSpec((S, HEAD_DIM), lambda b, h, g: (b, col0 // HEAD_DIM + g * HEADS + h))

    gain = pl.BlockSpec((N_GROUPS, HEAD_DIM), lambda b, h, g: (0, 0))
    per_slot = pl.BlockSpec((S, HEAD_DIM), lambda b, h, g: (b, h))
    return pl.pallas_call(
        body, grid=(B, HEADS, N_GROUPS),
        in_specs=[slab(Q0), slab(K0), slab(V0), pl.BlockSpec((S, HEAD_DIM), lambda b, h, g: (b, ZA // HEAD_DIM + h)), gain, gain,
                  per_slot, per_slot, per_slot, pl.BlockSpec(memory_space=pl.ANY)],
        out_specs=[pl.BlockSpec(memory_space=pl.ANY), gain, gain],
        out_shape=[jax.ShapeDtypeStruct(dproj.shape, dproj.dtype), jax.ShapeDtypeStruct((N_GROUPS, HEAD_DIM), f32),
                   jax.ShapeDtypeStruct((N_GROUPS, HEAD_DIM), f32)],
        scratch_shapes=[pltpu.VMEM((S, HEAD_DIM), f32)] * 7 + [pltpu.VMEM((BLK, BLK), f32), pltpu.VMEM((BLK, 2 * BLK), f32),
                                                                pltpu.VMEM((2, 3, S, HEAD_DIM), bf16), pltpu.VMEM((S, HEAD_DIM), bf16),
                                                                pltpu.SemaphoreType.DMA((2, 3)), pltpu.SemaphoreType.DMA],
        input_output_aliases={9: 0},
        compiler_params=_params(("arbitrary", "arbitrary", "arbitrary")), name="attn_bwd",
    )(proj, proj, proj, proj, gq, gk, a_comb, lse, dag, dproj)


def _conv_fwd(proj, conv_w, B, S):
    T, NC = proj.shape
    tc = LANES

    def body(cb_ref, cc_ref, cv_ref, z_ref, w_ref, c_ref, ub):
        u = cc_ref[...] * cv_ref[...]
        ub[0:8, :] = jnp.zeros((8, tc), f32)
        ub[8:8 + S, :] = u
        w = w_ref[...]
        y = w[0:1] * u + w[1:2] * ub[pl.ds(7, S), :] + w[2:3] * ub[pl.ds(6, S), :]
        z = z_ref[...]
        c_ref[...] = (cb_ref[...] * y * (z * _sigmoid(z))).astype(bf16)

    def seg(col0):
        return pl.BlockSpec((S, tc), lambda j, b: (b, col0 // tc + j))

    return pl.pallas_call(
        body, grid=(CONV_W // tc, B),
        in_specs=[seg(CB), seg(CC), seg(CV), seg(ZC), pl.BlockSpec((3, tc), lambda j, b: (0, j))],
        out_specs=pl.BlockSpec((S, tc), lambda j, b: (b, j)),
        out_shape=jax.ShapeDtypeStruct((T, CONV_W), bf16),
        scratch_shapes=[pltpu.VMEM((S + 8, tc), f32)],
        compiler_params=_params(("parallel", "parallel")), name="conv_fwd",
    )(proj, proj, proj, proj, conv_w)


def _conv_bwd(proj, conv_w, dc, dproj, B, S):
    T, NC = proj.shape
    tc = LANES

    def body(cb_ref, cc_ref, cv_ref, z_ref, w_ref, dc_ref, dproj_in, dproj_ref, dw_ref, ub, db, stage, sem):
        j, b = pl.program_id(0), pl.program_id(1)

        def dst(c):
            return dproj_ref.at[pl.ds(b * S, S), pl.ds((CB, CC, CV, ZC)[c] + j * tc, tc)]

        out = _Deferred(stage, sem, j * B + b, (CONV_W // tc) * B - 1, 4)
        out.begin(dst)
        cb, cc, cv, z, dcv = cb_ref[...], cc_ref[...], cv_ref[...], z_ref[...], dc_ref[...]
        u = cc * cv
        ub[0:8, :] = jnp.zeros((8, tc), f32)
        ub[8:8 + S, :] = u
        u1, u2 = ub[pl.ds(7, S), :], ub[pl.ds(6, S), :]
        w = w_ref[...]
        y = w[0:1] * u + w[1:2] * u1 + w[2:3] * u2
        sg = _sigmoid(z)
        si = z * sg
        dyv = dcv * cb * si
        db[0:S, :] = dyv
        db[S:S + 8, :] = jnp.zeros((8, tc), f32)
        du = w[0:1] * dyv + w[1:2] * db[pl.ds(1, S), :] + w[2:3] * db[pl.ds(2, S), :]
        stage[out.slot, 0] = (dcv * y * si).astype(bf16)
        stage[out.slot, 1] = (du * cv).astype(bf16)
        stage[out.slot, 2] = (du * cc).astype(bf16)
        stage[out.slot, 3] = (dcv * cb * y * (sg * (1.0 + z * (1.0 - sg)))).astype(bf16)
        for c in range(4):
            out.start(c, dst(c))
        part = jnp.concatenate([jnp.sum(dyv * u, axis=0, keepdims=True), jnp.sum(dyv * u1, axis=0, keepdims=True),
                                jnp.sum(dyv * u2, axis=0, keepdims=True)], axis=0)

        @pl.when(b == 0)
        def _():
            dw_ref[...] = part

        @pl.when(b > 0)
        def _():
            dw_ref[...] += part

        out.end(dst)

    def seg(col0):
        return pl.BlockSpec((S, tc), lambda j, b: (b, col0 // tc + j))

    return pl.pallas_call(
        body, grid=(CONV_W // tc, B),
        in_specs=[seg(CB), seg(CC), seg(CV), seg(ZC), pl.BlockSpec((3, tc), lambda j, b: (0, j)),
                  pl.BlockSpec((S, tc), lambda j, b: (b, j)), pl.BlockSpec(memory_space=pl.ANY)],
        out_specs=[pl.BlockSpec(memory_space=pl.ANY), pl.BlockSpec((3, tc), lambda j, b: (0, j))],
        out_shape=[jax.ShapeDtypeStruct(dproj.shape, dproj.dtype), jax.ShapeDtypeStruct((3, CONV_W), f32)],
        scratch_shapes=[pltpu.VMEM((S + 8, tc), f32), pltpu.VMEM((S + 8, tc), f32), pltpu.VMEM((2, 4, S, tc), bf16),
                        pltpu.SemaphoreType.DMA((2, 4))],
        input_output_aliases={6: 0},
        compiler_params=_params(("arbitrary", "arbitrary")), name="conv_bwd",
    )(proj, proj, proj, proj, conv_w, dc, dproj)


MEM_CHUNK = 1024


def _mem_fwd(proj, mkv, gq, gk, B, S):
    T, NC = proj.shape
    scale = MEM_HD ** -0.5

    def body(q_ref, z_ref, k_ref, v_ref, gq_ref, gk_ref, o_ref):
        kv = k_ref[...]
        kn = (kv * _rstd_head(kv) * gk_ref[...]).astype(bf16)
        vv = v_ref[...].astype(bf16)

        def chunk(c, carry):
            rows = pl.ds(pl.multiple_of(c * MEM_CHUNK, MEM_CHUNK), MEM_CHUNK)
            q = q_ref[rows, :]
            qn = q * _rstd(q) * gq_ref[...]
            s = _dot(qn, kn, NT_DIMS) * scale
            p = jnp.exp(s - jnp.max(s, axis=-1, keepdims=True))
            p = p / jnp.sum(p, axis=-1, keepdims=True)
            z = z_ref[rows, :]
            o_ref[rows, :] = (_dot(p, vv, NN_DIMS) * (z * _sigmoid(z))).astype(bf16)
            return carry

        lax.fori_loop(0, S // MEM_CHUNK, chunk, 0)

    gain = pl.BlockSpec((1, MEM_HD), lambda b, h: (0, 0))
    return pl.pallas_call(
        body, grid=(B, MEM_HEADS),
        in_specs=[pl.BlockSpec((S, MEM_HD), lambda b, h: (b, MQ // MEM_HD + h)),
                  pl.BlockSpec((S, MEM_HD), lambda b, h: (b, ZM // MEM_HD + h)),
                  pl.BlockSpec((MEM_LEN, MEM_HD), lambda b, h: (b, h)),
                  pl.BlockSpec((MEM_LEN, MEM_HD), lambda b, h: (b, MEM_HEADS + h)), gain, gain],
        out_specs=pl.BlockSpec((S, MEM_HD), lambda b, h: (b, h)),
        out_shape=jax.ShapeDtypeStruct((T, MEM_W), bf16),
        compiler_params=_params(("parallel", "parallel")), name="mem_fwd",
    )(proj, proj, mkv, mkv, gq, gk)


def _mem_bwd(proj, mkv, gq, gk, dmo, dproj, B, S):
    T, NC = proj.shape
    scale = MEM_HD ** -0.5

    def body(q_ref, z_ref, k_ref, v_ref, gq_ref, gk_ref, dmo_ref, dproj_in, dproj_ref, dmk_ref, dmv_ref, dgq_ref, dgk_ref,
             dkn_s, dv_s, gq_s, stage, sem):
        b, h = pl.program_id(0), pl.program_id(1)

        def dst(c):
            return dproj_ref.at[pl.ds(b * S, S), pl.ds((MQ, ZM)[c] + h * MEM_HD, MEM_HD)]

        out = _Deferred(stage, sem, b * MEM_HEADS + h, B * MEM_HEADS - 1, 2)
        out.begin(dst)
        kv = k_ref[...]
        kn = (kv * _rstd_head(kv) * gk_ref[...]).astype(bf16)
        vv = v_ref[...].astype(bf16)
        dkn_s[...] = jnp.zeros_like(dkn_s)
        dv_s[...] = jnp.zeros_like(dv_s)
        gq_s[...] = jnp.zeros_like(gq_s)

        def chunk(c, carry):
            rows = pl.ds(pl.multiple_of(c * MEM_CHUNK, MEM_CHUNK), MEM_CHUNK)
            q = q_ref[rows, :]
            qn = q * _rstd(q) * gq_ref[...]
            s = _dot(qn, kn, NT_DIMS) * scale
            p = jnp.exp(s - jnp.max(s, axis=-1, keepdims=True))
            p = p / jnp.sum(p, axis=-1, keepdims=True)
            mo = _dot(p, vv, NN_DIMS)
            z, dg_ = z_ref[rows, :], dmo_ref[rows, :]
            sg = _sigmoid(z)
            do = dg_ * (z * sg)
            stage[out.slot, 1, rows, :] = (dg_ * mo * (sg * (1.0 + z * (1.0 - sg)))).astype(bf16)
            dp = _dot(do, vv, NT_DIMS)
            ds = p * (dp - jnp.sum(do * mo, axis=-1, keepdims=True)) * scale
            dq, gq_p = _rms_bwd_rows(q, gq_ref[...], _dot(ds, kn, NN_DIMS), on_mxu=False)
            stage[out.slot, 0, rows, :] = dq.astype(bf16)
            gq_s[...] += gq_p
            dkn_s[...] += _dot(ds, qn, TN_DIMS)
            dv_s[...] += _dot(p, do, TN_DIMS)
            return carry

        lax.fori_loop(0, S // MEM_CHUNK, chunk, 0)
        for c in range(2):
            out.start(c, dst(c))
        dk, gk_p = _rms_bwd_rows(kv, gk_ref[...], dkn_s[...])
        dmk_ref[...] = dk
        dmv_ref[...] = dv_s[...]

        @pl.when((b == 0) & (h == 0))
        def _():
            dgq_ref[...] = gq_s[...]
            dgk_ref[...] = gk_p

        @pl.when((b > 0) | (h > 0))
        def _():
            dgq_ref[...] += gq_s[...]
            dgk_ref[...] += gk_p

        out.end(dst)

    gain = pl.BlockSpec((1, MEM_HD), lambda b, h: (0, 0))
    kvb = pl.BlockSpec((MEM_LEN, MEM_HD), lambda b, h: (b, h))
    return pl.pallas_call(
        body, grid=(B, MEM_HEADS),
        in_specs=[pl.BlockSpec((S, MEM_HD), lambda b, h: (b, MQ // MEM_HD + h)),
                  pl.BlockSpec((S, MEM_HD), lambda b, h: (b, ZM // MEM_HD + h)),
                  kvb, pl.BlockSpec((MEM_LEN, MEM_HD), lambda b, h: (b, MEM_HEADS + h)), gain, gain,
                  pl.BlockSpec((S, MEM_HD), lambda b, h: (b, h)), pl.BlockSpec(memory_space=pl.ANY)],
        out_specs=[pl.BlockSpec(memory_space=pl.ANY), kvb, kvb, gain, gain],
        out_shape=[jax.ShapeDtypeStruct(dproj.shape, dproj.dtype), jax.ShapeDtypeStruct((B * MEM_LEN, MEM_W), f32),
                   jax.ShapeDtypeStruct((B * MEM_LEN, MEM_W), f32), jax.ShapeDtypeStruct((1, MEM_HD), f32),
                   jax.ShapeDtypeStruct((1, MEM_HD), f32)],
        scratch_shapes=[pltpu.VMEM((MEM_LEN, MEM_HD), f32), pltpu.VMEM((MEM_LEN, MEM_HD), f32), pltpu.VMEM((1, MEM_HD), f32),
                        pltpu.VMEM((2, 2, S, MEM_HD), bf16), pltpu.SemaphoreType.DMA((2, 2))],
        input_output_aliases={7: 0},
        compiler_params=_params(("arbitrary", "arbitrary")), name="mem_bwd",
    )(proj, proj, mkv, mkv, gq, gk, dmo, dproj)


def _merge_fwd(proj, ag, cg, mg, wa, wc, wm):
    T, NC = proj.shape
    D = wa.shape[1]
    tm, tn = _pick(T, 1024), _pick(D, 512)
    assert GT % tn == 0

    def body(ag_ref, cg_ref, mg_ref, wa_ref, wc_ref, wm_ref, g0_ref, g1_ref, g2_ref, mer_ref, ba_ref, bc_ref, bm_ref):
        ba = _dot(ag_ref[...], wa_ref[...], NN_DIMS)
        bc = _dot(cg_ref[...], wc_ref[...], NN_DIMS)
        bm = _dot(mg_ref[...], wm_ref[...], NN_DIMS)
        mer_ref[...] = (_sigmoid(g0_ref[...]) * ba + _sigmoid(g1_ref[...]) * bc + _sigmoid(g2_ref[...]) * bm).astype(bf16)
        ba_ref[...] = ba.astype(bf16)
        bc_ref[...] = bc.astype(bf16)
        bm_ref[...] = bm.astype(bf16)

    def act(w):
        return pl.BlockSpec((tm, w), lambda i, j: (i, 0))

    def wt(k):
        return pl.BlockSpec((k, tn), lambda i, j: (0, j))

    def gate(n):
        return pl.BlockSpec((tm, tn), lambda i, j: (i, (GT + n * D) // tn + j))

    out = pl.BlockSpec((tm, tn), lambda i, j: (i, j))
    return pl.pallas_call(
        body, grid=(T // tm, D // tn),
        in_specs=[act(ATTN_OUT), act(CONV_W), act(MEM_W), wt(ATTN_OUT), wt(CONV_W), wt(MEM_W), gate(0), gate(1), gate(2)],
        out_specs=[out] * 4, out_shape=[jax.ShapeDtypeStruct((T, D), bf16)] * 4,
        compiler_params=_params(("parallel", "parallel")), name="merge_fwd",
    )(ag, cg, mg, wa, wc, wm, proj, proj, proj)


def _out_loss(merged, w_out, x, tgt):
    T, D = x.shape
    tm, tn = _pick(T, 1024), _pick(D, 512)

    def body(m_ref, w_ref, x_ref, t_ref, dy_ref, dyb_ref, lp_ref):
        e = x_ref[...] + _dot(m_ref[...], w_ref[...], NN_DIMS) - t_ref[...]
        dy = e / D
        dy_ref[...] = dy
        dyb_ref[...] = dy.astype(bf16)
        part = jnp.full((1, 8, LANES), jnp.sum(e * e), f32)

        @pl.when(pl.program_id(1) == 0)
        def _():
            lp_ref[...] = part

        @pl.when(pl.program_id(1) > 0)
        def _():
            lp_ref[...] += part

    tile = pl.BlockSpec((tm, tn), lambda i, j: (i, j))
    return pl.pallas_call(
        body, grid=(T // tm, D // tn),
        in_specs=[pl.BlockSpec((tm, D), lambda i, j: (i, 0)), pl.BlockSpec((D, tn), lambda i, j: (0, j)), tile, tile],
        out_specs=[tile, tile, pl.BlockSpec((1, 8, LANES), lambda i, j: (i, 0, 0))],
        out_shape=[jax.ShapeDtypeStruct((T, D), f32), jax.ShapeDtypeStruct((T, D), bf16),
                   jax.ShapeDtypeStruct((T // tm, 8, LANES), f32)],
        compiler_params=_params(("parallel", "arbitrary")), name="out_loss",
    )(merged, w_out, x, tgt)


def _merge_bwd(proj, dyb, w_out, ba, bc, bm):
    T, NC = proj.shape
    D = w_out.shape[0]
    tm, tn = _pick(T, 1024), _pick(D, 512)
    assert GT % tn == 0

    def body(dy_ref, w_ref, ba_ref, bc_ref, bm_ref, g0_ref, g1_ref, g2_ref, da_ref, dc_ref, dm_ref, dproj_ref, stage, sem):
        i, j = pl.program_id(0), pl.program_id(1)

        def dst(n):
            return dproj_ref.at[pl.ds(i * tm, tm), pl.ds(GT + n * D + j * tn, tn)]

        out = _Deferred(stage, sem, i * (D // tn) + j, (T // tm) * (D // tn) - 1, 3)
        out.begin(dst)
        dmer = _dot(dy_ref[...], w_ref[...], NT_DIMS)
        for n, (g_ref, br_ref, o_ref) in enumerate(((g0_ref, ba_ref, da_ref), (g1_ref, bc_ref, dc_ref), (g2_ref, bm_ref, dm_ref))):
            sg = _sigmoid(g_ref[...])
            o_ref[...] = (sg * dmer).astype(bf16)
            stage[out.slot, n] = (dmer * br_ref[...].astype(f32) * (sg * (1.0 - sg))).astype(bf16)
            out.start(n, dst(n))
        out.end(dst)

    tile = pl.BlockSpec((tm, tn), lambda i, j: (i, j))

    def gate(n):
        return pl.BlockSpec((tm, tn), lambda i, j: (i, (GT + n * D) // tn + j))

    return pl.pallas_call(
        body, grid=(T // tm, D // tn),
        in_specs=[pl.BlockSpec((tm, D), lambda i, j: (i, 0)), pl.BlockSpec((tn, D), lambda i, j: (j, 0)), tile, tile, tile,
                  gate(0), gate(1), gate(2)],
        out_specs=[tile, tile, tile, pl.BlockSpec(memory_space=pl.ANY)],
        out_shape=[jax.ShapeDtypeStruct((T, D), bf16)] * 3 + [jax.ShapeDtypeStruct((T, NC), bf16)],
        scratch_shapes=[pltpu.VMEM((2, 3, tm, tn), bf16), pltpu.SemaphoreType.DMA((2, 3))],
        compiler_params=_params(("arbitrary", "arbitrary")), name="merge_bwd",
    )(dyb, w_out, ba, bc, bm, proj, proj, proj)


def _fwd_bwd_head(x2, mem2, t2, proj, attn, B, S, mem_norm_g, attn_q_norm, attn_k_norm, conv_w, mem_q_norm, mem_k_norm,
                  w_kv, w_a, w_c, w_m, w_out):
    D = x2.shape[1]
    mng = mem_norm_g.reshape(1, D)
    mqn, mkn = mem_q_norm.reshape(1, MEM_HD), mem_k_norm.reshape(1, MEM_HD)
    ag, a_comb, lse = attn

    mh = _rms_fwd(mem2, mng, "rms_mem")
    mkv = _mm(mh, w_kv, mode="nn", out_dtype=f32, name="mkv")
    cg = _conv_fwd(proj, conv_w, B, S)
    mg = _mem_fwd(proj, mkv, mqn, mkn, B, S)
    merged, ba, bc, bm = _merge_fwd(proj, ag, cg, mg, w_a, w_c, w_m)
    dy, dyb, lp = _out_loss(merged, w_out, x2, t2)
    sq_err = jnp.sum(lp[:, 0, 0])

    g = {}
    g["w_out"] = _mm(merged, dyb, mode="tn", out_dtype=f32, name="dw_out")
    dba, dbc, dbm, dproj = _merge_bwd(proj, dyb, w_out, ba, bc, bm)
    g["w_br_attn"] = _mm(ag, dba, mode="tn", out_dtype=f32, name="dw_br_attn")
    g["w_br_conv"] = _mm(cg, dbc, mode="tn", out_dtype=f32, name="dw_br_conv")
    g["w_br_mem"] = _mm(mg, dbm, mode="tn", out_dtype=f32, name="dw_br_mem")
    dag = _mm(dba, w_a, mode="nt", out_dtype=f32, name="d_attn_out")
    dcg = _mm(dbc, w_c, mode="nt", out_dtype=f32, name="d_conv_out")
    dmg = _mm(dbm, w_m, mode="nt", out_dtype=f32, name="d_mem_out")
    dproj, g["attn_q_norm"], g["attn_k_norm"] = _attn_bwd(proj, attn_q_norm, attn_k_norm, a_comb, lse, dag, dproj, B, S)
    dproj, g["conv_w"] = _conv_bwd(proj, conv_w, dcg, dproj, B, S)
    dproj, dmk, dmv, dgmq, dgmk = _mem_bwd(proj, mkv, mqn, mkn, dmg, dproj, B, S)
    g["mem_q_norm"], g["mem_k_norm"] = dgmq.reshape(MEM_HD), dgmk.reshape(MEM_HD)
    dmkv = jnp.concatenate([dmk, dmv], axis=1)
    dmh = _mm(dmkv, w_kv, mode="nt", out_dtype=f32, name="d_mem_h")
    g["mem_norm_g"] = _rms_bwd(mem2, mng, dmh, None, "rms_mem_bwd").reshape(D)
    return sq_err, g, dict(dy=dy, dproj=dproj, mh=mh, dmkv=dmkv)


MESH = pl.DeviceIdType.MESH
HBM = pl.BlockSpec(memory_space=pl.ANY)


def _me():
    x, y, c = lax.axis_index("x"), lax.axis_index("y"), lax.axis_index("c")
    return (x, y, c), 4 * x + 2 * y + c


def _peer(k):
    (x, y, c), _ = _me()
    p = (1 - x if k & 4 else x, 1 - y if k & 2 else y, 1 - c if k & 1 else c)
    return p, 4 * p[0] + 2 * p[1] + p[2]


def _window(ref, axis, size, idx):
    if axis is None:
        return ref.at[idx]
    if axis == 0:
        return ref.at[pl.ds(idx * size, size), :]
    return ref.at[:, pl.ds(idx * size, size)]


def _full_shape(s, axis):
    if axis is None:
        return (N_DEV,) + s.shape
    return tuple(N_DEV * d if i == axis else d for i, d in enumerate(s.shape))


OTHER_CHIPS = (4, 2, 6)


def _spread_comm(shards, axes):
    n = len(shards)

    def copies(ins, outs, send_sems, recv_sems, base=0):
        _, me = _me()
        out = []
        for a in range(n):
            mine = _window(outs[a], axes[a], None if axes[a] is None else shards[a].shape[axes[a]], me)
            out.append(pltpu.make_async_copy(ins[a], mine, send_sems.at[base + a, N_CHIP]))
            for k, dist in enumerate((1,) + OTHER_CHIPS):
                dev, _ = _peer(dist)
                out.append(pltpu.make_async_remote_copy(
                    src_ref=ins[a], dst_ref=mine, send_sem=send_sems.at[base + a, k], recv_sem=recv_sems.at[base + a, k],
                    device_id=dev, device_id_type=MESH))
        return out

    shapes = [jax.ShapeDtypeStruct(_full_shape(s, ax), s.dtype) for s, ax in zip(shards, axes)]
    return _Comm(shards, shapes, (n, N_CHIP + 1), copies)


def _forward_blocks(fulls, axes):
    n = len(fulls)
    sizes = [None if ax is None else f.shape[ax] // N_DEV for f, ax in zip(fulls, axes)]

    def body(*refs):
        outs = refs[n:2 * n]
        send_sems, recv_sems = refs[2 * n:]
        sibling, _ = _peer(1)
        copies = []
        for j, dist in enumerate(OTHER_CHIPS):
            _, held = _peer(dist)
            for a in range(n):
                block = _window(outs[a], axes[a], sizes[a], held)
                copies.append(pltpu.make_async_remote_copy(
                    src_ref=block, dst_ref=block, send_sem=send_sems.at[a, j], recv_sem=recv_sems.at[a, j],
                    device_id=sibling, device_id_type=MESH))
        for cp in copies:
            cp.start()
        for cp in copies:
            cp.wait()

    return pl.pallas_call(
        body, in_specs=[HBM] * n, out_specs=[HBM] * n,
        out_shape=[jax.ShapeDtypeStruct(f.shape, f.dtype) for f in fulls],
        scratch_shapes=[pltpu.SemaphoreType.DMA((n, len(OTHER_CHIPS))), pltpu.SemaphoreType.DMA((n, len(OTHER_CHIPS)))],
        input_output_aliases={a: a for a in range(n)},
        name="forward_blocks",
    )(*fulls)


def _shard_order():
    (x, y, c), me = _me()
    xs, ys, xo, yo = _peer(4)[1], _peer(2)[1], _peer(5)[1], _peer(3)[1]
    north = c == 1
    ids = [me, _peer(1)[1], jnp.where(north, xs, ys), jnp.where(north, yo, xo), jnp.where(north, ys, xs),
           jnp.where(north, xo, yo), _peer(6)[1], _peer(7)[1]]
    return jnp.stack(ids).astype(jnp.int32)


def _proj_gather(h, w_shard, order, comm, comm_at):
    T, K = h.shape
    C = w_shard.shape[1]
    tm = _pick(T, 1024)
    nm = T // tm
    ahead = max(nm - 2, 0)
    n_ci, n_co = len(comm.ins), len(comm.out_shapes)

    def body(*refs):
        order_ref, h_ref, ws_ref = refs[:3]
        c_ins = refs[3:3 + n_ci]
        o_ref, wf_ref = refs[3 + n_ci:5 + n_ci]
        c_outs = refs[5 + n_ci:5 + n_ci + n_co]
        wbuf, send_sems, recv_sems, own_sem, buf_sems, c_send, c_recv = refs[5 + n_ci + n_co:]
        t, i = pl.program_id(0), pl.program_id(1)

        @pl.when((t == comm_at) & (i == 0))
        def _():
            for cp in comm.copies(c_ins, c_outs, c_send, c_recv):
                cp.start()
        (x, y, c), me = _me()
        sibling, sib_id = _peer(1)
        chips = [_peer(dist) for dist in OTHER_CHIPS]

        def win(idx):
            return wf_ref.at[:, pl.ds(idx * C, C)]

        def copy(k, block, to, src=None):
            return pltpu.make_async_remote_copy(
                src_ref=win(block) if src is None else src, dst_ref=win(block),
                send_sem=send_sems.at[k], recv_sem=recv_sems.at[k], device_id=to, device_id_type=MESH)

        def fetch(tt, src):
            return pltpu.make_async_copy(src, wbuf.at[tt % 2], buf_sems.at[tt % 2])

        own = pltpu.make_async_copy(ws_ref, win(me), own_sem)
        (x_nbr, x_id), (y_nbr, y_id), (_, d_id) = chips

        def half(k, block, top, to):
            rows = wf_ref.at[pl.ds(0 if top else K // 2, K // 2), pl.ds(block * C, C)]
            return pltpu.make_async_remote_copy(src_ref=rows, dst_ref=rows, send_sem=send_sems.at[k], recv_sem=recv_sems.at[k],
                                                device_id=to, device_id_type=MESH)

        here = (x, y, c)

        def pass_on(from_x):
            if from_x:
                copy(4, x_id, sibling).start()
                half(8, x_id, False, y_nbr).start()
            else:
                copy(5, y_id, sibling).start()
                half(7, y_id, True, x_nbr).start()

        @pl.when((t == 0) & (i == 0))
        def _():
            fetch(0, ws_ref).start()
            own.start()
            copy(0, me, sibling, src=ws_ref).start()

        for tt in range(N_DEV):
            @pl.when((t == tt) & (i == 0))
            def _(tt=tt):
                fetch(tt, ws_ref).wait()

        o_ref[...] = _dot(h_ref[...], wbuf[t % 2], NN_DIMS)

        def schedule(first_x):
            on = c == (1 if first_x else 0)
            (f_dev, f_id), (g_dev, g_id) = ((x_nbr, x_id), (y_nbr, y_id)) if first_x else ((y_nbr, y_id), (x_nbr, x_id))
            kf, kg = (1, 2) if first_x else (2, 1)
            pf, pg = (4, 5) if first_x else (5, 4)

            @pl.when((t == 0) & (i == 0) & on)
            def _():
                copy(kf, me, f_dev, src=ws_ref).start()

            def event(nxt):
                if nxt == 1:
                    copy(0, sib_id, here).wait_recv()
                    return sib_id
                if nxt == 2:
                    copy(kf, f_id, here).wait_recv()
                    copy(kf, me, f_dev, src=ws_ref).wait_send()
                    copy(kg, me, g_dev, src=ws_ref).start()
                    pass_on(first_x)
                    return f_id
                if nxt == 3:
                    copy(pg, g_id + 1 - 2 * c, here).wait_recv()
                    return g_id + 1 - 2 * c
                if nxt == 4:
                    copy(kg, g_id, here).wait_recv()
                    pass_on(not first_x)
                    return g_id
                if nxt == 5:
                    copy(pf, f_id + 1 - 2 * c, here).wait_recv()
                    return f_id + 1 - 2 * c
                if nxt == 6:
                    half(7, d_id, True, here).wait_recv()
                    half(8, d_id, False, here).wait_recv()
                    copy(6, d_id, sibling).start()
                    return d_id
                copy(6, d_id + 1 - 2 * c, here).wait_recv()
                return d_id + 1 - 2 * c

            for tt in range(N_DEV - 1):
                @pl.when((t == tt) & (i == ahead) & on)
                def _(tt=tt):
                    fetch(tt + 1, win(event(tt + 1))).start()

            @pl.when((t == N_DEV - 1) & (i == nm - 1) & on)
            def _():
                copy(kg, me, g_dev, src=ws_ref).wait_send()

        schedule(True)
        schedule(False)

        @pl.when((t == N_DEV - 1) & (i == nm - 1))
        def _():
            copy(0, me, sibling, src=ws_ref).wait_send()
            half(7, y_id, True, x_nbr).wait_send()
            half(8, x_id, False, y_nbr).wait_send()
            for j, (_, dev_id) in enumerate(chips):
                copy(4 + j, dev_id, sibling).wait_send()
            own.wait()
            for cp in comm.copies(c_ins, c_outs, c_send, c_recv):
                cp.wait()

    hbm = pl.BlockSpec(memory_space=pl.ANY)
    res = pl.pallas_call(
        body,
        grid_spec=pltpu.PrefetchScalarGridSpec(
            num_scalar_prefetch=1, grid=(N_DEV, nm),
            in_specs=[pl.BlockSpec((tm, K), lambda t, i, order_ref: (i, 0)), hbm] + [hbm] * n_ci,
            out_specs=[pl.BlockSpec((tm, C), lambda t, i, order_ref: (i, order_ref[t])), hbm] + [hbm] * n_co,
            scratch_shapes=[pltpu.VMEM((2, K, C), bf16), pltpu.SemaphoreType.DMA((9,)), pltpu.SemaphoreType.DMA((9,)),
                            pltpu.SemaphoreType.DMA, pltpu.SemaphoreType.DMA((2,))] + comm.scratch()),
        out_shape=[jax.ShapeDtypeStruct((T, N_DEV * C), f32), jax.ShapeDtypeStruct((K, N_DEV * C), bf16)] + comm.out_shapes,
        compiler_params=_params(("arbitrary", "arbitrary")), name="proj_gather",
    )(order, h, w_shard, *comm.ins)
    return res[0], res[1], list(res[2:])


N_CHIP = 4


def _shard_shape(g, ax):
    return tuple(d // N_DEV if i == ax else d for i, d in enumerate(g.shape))


def _sibling_comm(grads, axes):
    n = len(grads)
    sizes = [g.shape[ax] // N_DEV for g, ax in zip(grads, axes)]

    def copies(ins, lands, send_sems, recv_sems, base=0):
        sibling, _ = _peer(1)
        out = []
        for j in range(N_CHIP):
            _, owner = _peer(2 * j + 1)
            for a in range(n):
                out.append(pltpu.make_async_remote_copy(
                    src_ref=_window(ins[a], axes[a], sizes[a], owner), dst_ref=lands[a].at[j],
                    send_sem=send_sems.at[base + a, j], recv_sem=recv_sems.at[base + a, j], device_id=sibling,
                    device_id_type=MESH))
        return out

    shapes = [jax.ShapeDtypeStruct((N_CHIP,) + _shard_shape(g, ax), g.dtype) for g, ax in zip(grads, axes)]
    return _Comm(grads, shapes, (n, N_CHIP), copies)


def _chips_comm(sums):
    n = len(sums)

    def copies(ins, lands, send_sems, recv_sems, base=0):
        out = []
        for j in range(1, N_CHIP):
            dev, _ = _peer(2 * j)
            for a in range(n):
                out.append(pltpu.make_async_remote_copy(
                    src_ref=ins[a].at[j - 1], dst_ref=lands[a].at[j - 1],
                    send_sem=send_sems.at[base + a, j - 1], recv_sem=recv_sems.at[base + a, j - 1], device_id=dev,
                    device_id_type=MESH))
        return out

    return _Comm(sums, [jax.ShapeDtypeStruct(s.shape, s.dtype) for s in sums], (n, N_CHIP), copies)


def _join(c1, c2):
    n1, m1, k1 = len(c1.ins), len(c1.out_shapes), c1.sem_shape[0]

    def copies(ins, lands, send_sems, recv_sems):
        return (c1.copies(ins[:n1], lands[:m1], send_sems, recv_sems, base=0)
                + c2.copies(ins[n1:], lands[m1:], send_sems, recv_sems, base=k1))

    aliases = {**c1.aliases, **{n1 + k: m1 + v for k, v in c2.aliases.items()}}
    return _Comm(c1.ins + c2.ins, c1.out_shapes + c2.out_shapes, (k1 + c2.sem_shape[0], N_CHIP), copies, aliases)


def _chips_first_hop(sums):
    n = len(sums)

    def copies(ins, outs, send_sems, recv_sems, base=0):
        lands, relays = outs[:n], outs[n:]
        (y_dev, _), (x_dev, _) = _peer(2), _peer(4)
        out = []
        for a in range(n):
            half = sums[a].shape[1] // 2
            for k, (src, dst, dev) in enumerate((
                    (ins[a].at[0], lands[a].at[0], y_dev), (ins[a].at[1], lands[a].at[1], x_dev),
                    (ins[a].at[2, pl.ds(0, half)], relays[a].at[0], x_dev),
                    (ins[a].at[2, pl.ds(half, half)], relays[a].at[1], y_dev))):
                out.append(pltpu.make_async_remote_copy(
                    src_ref=src, dst_ref=dst, send_sem=send_sems.at[base + a, k], recv_sem=recv_sems.at[base + a, k],
                    device_id=dev, device_id_type=MESH))
        return out

    shapes = ([jax.ShapeDtypeStruct(s.shape, s.dtype) for s in sums]
              + [jax.ShapeDtypeStruct((2, s.shape[1] // 2) + s.shape[2:], s.dtype) for s in sums])
    return _Comm(sums, shapes, (n, N_CHIP), copies)


def _chips_relay(relays, lands):
    n = len(relays)

    def copies(ins, outs, send_sems, recv_sems, base=0):
        (y_dev, _), (x_dev, _) = _peer(2), _peer(4)
        out = []
        for a in range(n):
            half = relays[a].shape[1]
            for k, (rows, dev) in enumerate(((pl.ds(0, half), y_dev), (pl.ds(half, half), x_dev))):
                out.append(pltpu.make_async_remote_copy(
                    src_ref=ins[a].at[k], dst_ref=outs[a].at[2, rows], send_sem=send_sems.at[base + a, k],
                    recv_sem=recv_sems.at[base + a, k], device_id=dev, device_id_type=MESH))
        return out

    return _Comm(list(relays) + list(lands), [jax.ShapeDtypeStruct(l.shape, l.dtype) for l in lands], (n, N_CHIP), copies,
                 aliases={n + a: a for a in range(n)})


def _rs_chip_sum(grad, land, xyc, axis, name):
    R, C = land.shape[1:]
    tr = _pick(R, max(8, (640 * 1024) // C), 8)

    def body(xyc_ref, g_ref, l_ref, o_ref):
        o_ref[0] = (g_ref[...] + l_ref[0]).astype(bf16)

    def owner(j, xyc_ref):
        jj = j + 1
        return 4 * (xyc_ref[0] ^ (jj >> 1)) + 2 * (xyc_ref[1] ^ (jj & 1)) + xyc_ref[2]

    if axis == 0:
        g_spec = pl.BlockSpec((tr, C), lambda j, i, xyc_ref: (owner(j, xyc_ref) * (R // tr) + i, 0))
    else:
        g_spec = pl.BlockSpec((tr, C), lambda j, i, xyc_ref: (i, owner(j, xyc_ref)))
    return pl.pallas_call(
        body,
        grid_spec=pltpu.PrefetchScalarGridSpec(
            num_scalar_prefetch=1, grid=(N_CHIP - 1, R // tr),
            in_specs=[g_spec, pl.BlockSpec((1, tr, C), lambda j, i, xyc_ref: (j + 1, i, 0))],
            out_specs=pl.BlockSpec((1, tr, C), lambda j, i, xyc_ref: (j, i, 0))),
        out_shape=jax.ShapeDtypeStruct((N_CHIP - 1, R, C), bf16),
        compiler_params=_params(("parallel", "parallel")), name=name,
    )(xyc, grad, land)


def _allreduce_small(part):
    R = part.shape[0]

    def body(p_ref, o_ref, buf, send_sems, recv_sems):
        (x, y, c), me = _me()
        buf[me] = p_ref[...]
        copies = []
        for k in range(1, N_DEV):
            dev, dev_id = _peer(k)
            copies.append(pltpu.make_async_remote_copy(
                src_ref=p_ref, dst_ref=buf.at[me], send_sem=send_sems.at[k - 1], recv_sem=recv_sems.at[k - 1],
                device_id=dev, device_id_type=MESH))
        for cp in copies:
            cp.start()
        for k in range(1, N_DEV):
            _, dev_id = _peer(k)
            pltpu.make_async_remote_copy(
                src_ref=p_ref, dst_ref=buf.at[dev_id], send_sem=send_sems.at[k - 1], recv_sem=recv_sems.at[k - 1],
                device_id=(x, y, c), device_id_type=MESH).wait_recv()
        for cp in copies:
            cp.wait_send()
        acc = buf[0]
        for d in range(1, N_DEV):
            acc = acc + buf[d]
        o_ref[...] = acc

    return pl.pallas_call(
        body, in_specs=[pl.BlockSpec(memory_space=pltpu.VMEM)], out_specs=pl.BlockSpec(memory_space=pltpu.VMEM),
        out_shape=jax.ShapeDtypeStruct((R, LANES), f32),
        scratch_shapes=[pltpu.VMEM((N_DEV, R, LANES), f32), pltpu.SemaphoreType.DMA((N_DEV - 1,)), pltpu.SemaphoreType.DMA((N_DEV - 1,))],
        name="allreduce_small",
    )(part)


def _adamw_math(w, g, m, v):
    m2 = ADAM_B1 * m + (1.0 - ADAM_B1) * g
    v2 = ADAM_B2 * v + (1.0 - ADAM_B2) * (g * g)
    m_hat = m2 / (1.0 - ADAM_B1 ** ADAM_STEP)
    v_hat = v2 / (1.0 - ADAM_B2 ** ADAM_STEP)
    return -ADAM_LR * (m_hat / (jnp.sqrt(v_hat) + ADAM_EPS) + ADAM_WD * w), m2, v2


def _adamw_shard(grad, land_sib, land_chips, w, m, v, me, axis, name, row0=0, prev=None):
    R, C = land_sib.shape[1:]
    assert axis == 1 or R == w.shape[0]
    tr = _pick(R, max(8, (320 * 1024) // C), 8)
    r0 = row0 // tr
    n_prev = len(prev) if prev else 0

    def body(me_ref, g_ref, s_ref, l_ref, w_ref, m_ref, v_ref, *rest):
        go_ref, d_ref, mo_ref, vo_ref = rest[n_prev:]
        g = g_ref[...] + s_ref[0]
        for j in range(N_CHIP - 1):
            g = g + l_ref[j].astype(f32)
        d, m2, v2 = _adamw_math(w_ref[...], g, m_ref[...], v_ref[...])
        go_ref[...] = g
        d_ref[...] = d
        mo_ref[...] = m2
        vo_ref[...] = v2

    if axis == 0:
        g_spec = pl.BlockSpec((tr, C), lambda i, me_ref: (me_ref[0] * (R // tr) + i, 0))
    else:
        g_spec = pl.BlockSpec((tr, C), lambda i, me_ref: (i, me_ref[0]))
    blk = pl.BlockSpec((tr, C), lambda i, me_ref: (r0 + i, 0))
    return pl.pallas_call(
        body,
        grid_spec=pltpu.PrefetchScalarGridSpec(
            num_scalar_prefetch=1, grid=(R // tr,),
            in_specs=[g_spec, pl.BlockSpec((1, tr, C), lambda i, me_ref: (0, i, 0)),
                      pl.BlockSpec((N_CHIP - 1, tr, C), lambda i, me_ref: (0, i, 0)), blk, blk, blk]
            + [pl.BlockSpec(memory_space=pl.ANY)] * n_prev,
            out_specs=[blk] * 4),
        out_shape=[jax.ShapeDtypeStruct(w.shape, f32)] * 4,
        input_output_aliases={7 + i: i for i in range(n_prev)},
        compiler_params=_params(("parallel",)), name=name,
    )(me, grad, land_sib, land_chips, w, m, v, *(prev or []))


def _adamw_small(g, w, m, v):
    def body(g_ref, w_ref, m_ref, v_ref, d_ref, mo_ref, vo_ref):
        d, m2, v2 = _adamw_math(w_ref[...], g_ref[...], m_ref[...], v_ref[...])
        d_ref[...] = d
        mo_ref[...] = m2
        vo_ref[...] = v2

    return pl.pallas_call(body, out_shape=[jax.ShapeDtypeStruct(g.shape, f32)] * 3, name="adamw_small")(g, w, m, v)


def _pack_rows(parts, total_rows):
    rows = jnp.concatenate([p.reshape(-1, LANES) for p in parts], axis=0)
    return jnp.pad(rows, ((0, total_rows - rows.shape[0]), (0, 0)))


BIG = ("w_in", "mem_w_kv", "w_br_attn", "w_br_conv", "w_br_mem", "w_out")
BIG_AXIS = {"w_in": 1, "mem_w_kv": 0, "w_br_attn": 1, "w_br_conv": 1, "w_br_mem": 1, "w_out": 0}
SMALL = ("norm_g", "mem_norm_g", "attn_q_norm", "attn_k_norm", "mem_q_norm", "mem_k_norm")
ALL_W = ("norm_g", "mem_norm_g", "w_in", "attn_q_norm", "attn_k_norm", "conv_w", "mem_w_kv", "mem_q_norm", "mem_k_norm",
         "w_br_attn", "w_br_conv", "w_br_mem", "w_out")


def kernel(x, mem, norm_g, mem_norm_g, w_in, attn_q_norm, attn_k_norm, conv_w, mem_w_kv, mem_q_norm, mem_k_norm, w_br_attn, w_br_conv, w_br_mem, w_out, loss_target, m_norm_g, m_mem_norm_g, m_w_in, m_attn_q_norm, m_attn_k_norm, m_conv_w, m_mem_w_kv, m_mem_q_norm, m_mem_k_norm, m_w_br_attn, m_w_br_conv, m_w_br_mem, m_w_out, v_norm_g, v_mem_norm_g, v_w_in, v_attn_q_norm, v_attn_k_norm, v_conv_w, v_mem_w_kv, v_mem_q_norm, v_mem_k_norm, v_w_br_attn, v_w_br_conv, v_w_br_mem, v_w_out):
    w = dict(norm_g=norm_g, mem_norm_g=mem_norm_g, w_in=w_in, attn_q_norm=attn_q_norm, attn_k_norm=attn_k_norm, conv_w=conv_w,
             mem_w_kv=mem_w_kv, mem_q_norm=mem_q_norm, mem_k_norm=mem_k_norm, w_br_attn=w_br_attn, w_br_conv=w_br_conv,
             w_br_mem=w_br_mem, w_out=w_out)
    mo = dict(norm_g=m_norm_g, mem_norm_g=m_mem_norm_g, w_in=m_w_in, attn_q_norm=m_attn_q_norm, attn_k_norm=m_attn_k_norm,
              conv_w=m_conv_w, mem_w_kv=m_mem_w_kv, mem_q_norm=m_mem_q_norm, mem_k_norm=m_mem_k_norm, w_br_attn=m_w_br_attn,
              w_br_conv=m_w_br_conv, w_br_mem=m_w_br_mem, w_out=m_w_out)
    vo = dict(norm_g=v_norm_g, mem_norm_g=v_mem_norm_g, w_in=v_w_in, attn_q_norm=v_attn_q_norm, attn_k_norm=v_attn_k_norm,
              conv_w=v_conv_w, mem_w_kv=v_mem_w_kv, mem_q_norm=v_mem_q_norm, mem_k_norm=v_mem_k_norm, w_br_attn=v_w_br_attn,
              w_br_conv=v_w_br_conv, w_br_mem=v_w_br_mem, w_out=v_w_out)
    B, S, D = x.shape
    me = (4 * lax.axis_index("x") + 2 * lax.axis_index("y") + lax.axis_index("c")).astype(jnp.int32)

    T = B * S
    x2, t2, mem2, ng = x.reshape(T, D), loss_target.reshape(T, D), mem.reshape(B * MEM_LEN, D), norm_g.reshape(1, D)
    h = _rms_fwd(x2, ng, "rms_x")
    rows_sharded, cols_sharded = ("mem_w_kv", "w_out"), ("w_br_attn", "w_br_conv", "w_br_mem")
    later = rows_sharded + cols_sharded
    later_axes = [BIG_AXIS[n] for n in later] + [None]
    conv_pad = jnp.pad(conv_w, ((0, 8 - conv_w.shape[0]), (0, 0)))
    proj, w_in_full, spread_a = _proj_gather(
        h, w_in.astype(bf16), _shard_order(),
        _spread_comm([w[n].astype(bf16) for n in rows_sharded], [BIG_AXIS[n] for n in rows_sharded]), comm_at=N_DEV - 3)
    *attn, spread_b = _attn_fwd(proj, attn_q_norm, attn_k_norm, B, S,
                                comm=_spread_comm([w[n].astype(bf16) for n in cols_sharded] + [conv_pad],
                                                  [BIG_AXIS[n] for n in cols_sharded] + [None]))
    *fulls, conv_g = _forward_blocks(spread_a + spread_b, later_axes)
    wf = dict(zip(later, fulls), w_in=w_in_full)
    conv_full = conv_g[:, :3, :].transpose(1, 0, 2).reshape(3, CONV_W)

    sq_err, g, t = _fwd_bwd_head(x2, mem2, t2, proj, attn, B, S, mem_norm_g, attn_q_norm, attn_k_norm, conv_full, mem_q_norm,
                                 mem_k_norm, wf["mem_w_kv"], wf["w_br_attn"], wf["w_br_conv"], wf["w_br_mem"], wf["w_out"])
    t.update(x2=x2, ng=ng, h=h)

    xyc = jnp.stack([lax.axis_index("x"), lax.axis_index("y"), lax.axis_index("c")]).astype(jnp.int32)
    me1 = me.reshape(1)
    early = ("w_out", "w_br_attn", "w_br_conv", "w_br_mem")
    g["mem_w_kv"], sib_early = _mm(t["mh"], t["dmkv"], mode="tn", out_dtype=f32, name="dw_kv",
                                   comm=_sibling_comm([g[n] for n in early], [BIG_AXIS[n] for n in early]))
    sums_early = [_rs_chip_sum(g[n], ls, xyc, BIG_AXIS[n], "rs_sum_" + n) for n, ls in zip(early, sib_early)]
    rows_top = D * 5 // 8
    h_top, h_bot = t["h"][:, :rows_top], t["h"][:, rows_top:]
    g_top, (*chips_early, sib_kv) = _mm(h_top, t["dproj"], mode="tn", out_dtype=f32, name="dw_in_top", tm=rows_top,
                                        comm=_join(_chips_comm(sums_early), _sibling_comm([g["mem_w_kv"]], [0])))
    sum_kv = _rs_chip_sum(g["mem_w_kv"], sib_kv, xyc, 0, "rs_sum_mem_w_kv")
    g_bot, (chips_kv, sib_top) = _mm(h_bot, t["dproj"], mode="tn", out_dtype=f32, name="dw_in_bot", tm=D - rows_top,
                                     comm=_join(_chips_comm([sum_kv]), _sibling_comm([g_top], [1])))
    sum_top = _rs_chip_sum(g_top, sib_top, xyc, 1, "rs_sum_w_in_top")
    tm_t = _pick(T // 2, 1024)
    halfm = (T // 2) // tm_t
    dh, (part_top, relay_top, sib_bot) = _mm(t["dproj"], wf["w_in"], mode="nt", out_dtype=f32, name="d_h_a", tm=tm_t, tk=D_H_TK,
                                             m_tiles=(0, halfm), into="new",
                                             comm=_join(_chips_first_hop([sum_top]), _sibling_comm([g_bot], [1])))
    sum_bot = _rs_chip_sum(g_bot, sib_bot, xyc, 1, "rs_sum_w_in_bot")
    dh, (part_bot, relay_bot, chips_top) = _mm(t["dproj"], wf["w_in"], mode="nt", out_dtype=f32, name="d_h_b", tm=tm_t,
                                               tk=D_H_TK, m_tiles=(halfm, halfm), into=dh,
                                               comm=_join(_chips_first_hop([sum_bot]), _chips_relay([relay_top], [part_top])))
    dx, dng, (chips_bot,) = _rms_bwd(t["x2"], t["ng"], dh, t["dy"], "rms_x_bwd", comm=_chips_relay([relay_bot], [part_bot]))
    g["norm_g"] = dng.reshape(D)

    grad, delta, new_m, new_v = {}, {}, {}, {}
    for n, ls, lc in zip(early + ("mem_w_kv",), sib_early + [sib_kv], chips_early + [chips_kv]):
        grad[n], delta[n], new_m[n], new_v[n] = _adamw_shard(g[n], ls, lc, w[n], mo[n], vo[n], me1, BIG_AXIS[n], "adamw_" + n)
    top = _adamw_shard(g_top, sib_top, chips_top, w_in, m_w_in, v_w_in, me1, 1, "adamw_w_in_top")
    grad["w_in"], delta["w_in"], new_m["w_in"], new_v["w_in"] = _adamw_shard(
        g_bot, sib_bot, chips_bot, w_in, m_w_in, v_w_in, me1, 1, "adamw_w_in_bot", row0=rows_top, prev=top)

    n_rep = sum(w[n].size for n in SMALL) // LANES
    conv_rows = g["conv_w"].reshape(3, N_DEV, LANES).transpose(1, 0, 2).reshape(3 * N_DEV, LANES)
    n_rows = -(-(n_rep + 3 * N_DEV + 1) // 8) * 8
    part = _pack_rows([g[n] for n in SMALL] + [conv_rows, jnp.full((1, LANES), sq_err, f32)], n_rows)
    tot = _allreduce_small(part)
    loss = tot[n_rep + 3 * N_DEV, 0] * (0.5 / D)
    n_small = -(-(n_rep + 3) // 8) * 8
    g_small = _pack_rows([tot[:n_rep], lax.dynamic_slice(tot, (n_rep + 3 * me, 0), (3, LANES))], n_small)
    names = SMALL + ("conv_w",)
    d_s, m_s, v_s = _adamw_small(g_small, _pack_rows([w[n] for n in names], n_small), _pack_rows([mo[n] for n in names], n_small),
                                 _pack_rows([vo[n] for n in names], n_small))
    off = 0
    for n in names:
        r = w[n].size // LANES
        grad[n], delta[n] = g_small[off:off + r].reshape(w[n].shape), d_s[off:off + r].reshape(w[n].shape)
        new_m[n], new_v[n] = m_s[off:off + r].reshape(w[n].shape), v_s[off:off + r].reshape(w[n].shape)
        off += r
    return (loss, dx.reshape(B, S, D), *[grad[n] for n in ALL_W], *[delta[n] for n in ALL_W], *[new_m[n] for n in ALL_W],
            *[new_v[n] for n in ALL_W])
```

```python
import functools

import jax
import jax.numpy as jnp
from jax import lax
from jax.experimental import pallas as pl
from jax.experimental.pallas import tpu as pltpu

f32 = jnp.float32
bf16 = jnp.bfloat16

N_DEV = 8
HEAD_DIM = 128
DILATIONS = (1, 4, 16)
N_GROUPS = 3
HEADS = 4
BLK = 128
ATTN_QKV = N_GROUPS * HEADS * HEAD_DIM
ATTN_OUT = HEADS * HEAD_DIM
CONV_W = 1024
MEM_LEN = 256
MEM_HEADS = 4
MEM_HD = 256
MEM_W = MEM_HEADS * MEM_HD
EPS = 1e-6
Q0, K0, V0 = 0, ATTN_QKV, 2 * ATTN_QKV
ZA = 3 * ATTN_QKV
CB, CC, CV, ZC = ZA + ATTN_OUT, ZA + ATTN_OUT + CONV_W, ZA + ATTN_OUT + 2 * CONV_W, ZA + ATTN_OUT + 3 * CONV_W
MQ = ZC + CONV_W
ZM = MQ + MEM_W
GT = ZM + MEM_W

ADAM_LR, ADAM_B1, ADAM_B2, ADAM_EPS, ADAM_WD, ADAM_STEP = 0.001, 0.9, 0.999, 1e-08, 0.01, 10

VMEM_LIMIT = 56 * 1024 * 1024
D_H_TK = 4352
LANES = 128

NT_DIMS = (((1,), (1,)), ((), ()))
TN_DIMS = (((0,), (0,)), ((), ()))
NN_DIMS = (((1,), (0,)), ((), ()))


def _params(sem=None):
    return pltpu.CompilerParams(dimension_semantics=sem, vmem_limit_bytes=VMEM_LIMIT)


def _pick(n, pref, q=LANES):
    t = (min(pref, n) // q) * q
    while t >= q:
        if n % t == 0:
            return t
        t -= q
    return n


def _dot(a, b, dims):
    return lax.dot_general(a.astype(bf16), b.astype(bf16), dims, preferred_element_type=f32)


def _sigmoid(z):
    return 0.5 * jnp.tanh(0.5 * z) + 0.5


def _rstd(v):
    return lax.rsqrt(jnp.mean(v * v, axis=-1, keepdims=True) + EPS)


class _Deferred:
    def __init__(self, stage, sems, step, last, n):
        assert last >= 1
        self.stage, self.sems, self.step, self.last, self.n = stage, sems, step, last, n
        self.slot = step % 2

    def _drain(self, slot, like):
        for c in range(self.n):
            pltpu.make_async_copy(self.stage.at[slot, c], like(c), self.sems.at[slot, c]).wait()

    def begin(self, like):
        pl.when(self.step >= 2)(lambda: self._drain(self.slot, like))

    def start(self, c, dst):
        pltpu.make_async_copy(self.stage.at[self.slot, c], dst, self.sems.at[self.slot, c]).start()

    def end(self, like):
        @pl.when(self.step == self.last)
        def _():
            self._drain(self.slot, like)
            self._drain(1 - self.slot, like)


def _row_sum(v):
    hi = v.astype(bf16)
    lo = (v - hi.astype(f32)).astype(bf16)
    ones = jnp.ones((v.shape[1], v.shape[1]), bf16)
    return _dot(hi, ones, NN_DIMS) + _dot(lo, ones, NN_DIMS)


def _rstd_head(v):
    return lax.rsqrt(_row_sum(v * v) * (1.0 / v.shape[1]) + EPS)


class _Comm:
    def __init__(self, ins, out_shapes, sem_shape, copies, aliases=None):
        self.ins, self.out_shapes, self.sem_shape, self.copies = list(ins), list(out_shapes), sem_shape, copies
        self.aliases = dict(aliases or {})

    def scratch(self):
        return [pltpu.SemaphoreType.DMA(self.sem_shape), pltpu.SemaphoreType.DMA(self.sem_shape)]

    def io_aliases(self, first_in, first_out):
        return {first_in + k: first_out + v for k, v in self.aliases.items()}


def _mm(a, b, *, mode, out_dtype, name, tm=1024, tn=1024, tk=4096, m_tiles=None, into=None, comm=None):
    if mode == "nn":
        (M, K), (K2, N) = a.shape, b.shape
    elif mode == "nt":
        (M, K), (N, K2) = a.shape, b.shape
    else:
        (K, M), (K2, N) = a.shape, b.shape
    assert K == K2
    tm, tn, tk = _pick(M, tm), _pick(N, tn), _pick(K, tk)
    nk = K // tk
    m0, mc = m_tiles if m_tiles is not None else (0, M // tm)
    o0 = 0 if into is None else m0
    aliased = into is not None and not isinstance(into, str)
    n_ci = len(comm.ins) if comm else 0
    n_co = len(comm.out_shapes) if comm else 0
    grid = (mc, N // tn, nk)
    dims = {"nn": NN_DIMS, "nt": NT_DIMS, "tn": TN_DIMS}[mode]

    def body(*refs, nk):
        a_ref, b_ref = refs[:2]
        pos = 3 if aliased else 2
        c_ins, o_ref, c_outs = refs[pos:pos + n_ci], refs[pos + n_ci], refs[pos + n_ci + 1:pos + n_ci + 1 + n_co]
        scratch = refs[pos + n_ci + 1 + n_co:]
        ids = [pl.program_id(d) for d in range(3)]
        if comm:
            sems = scratch[-2:]

            @pl.when((ids[0] == 0) & (ids[1] == 0) & (ids[2] == 0))
            def _():
                for cp in comm.copies(c_ins, c_outs, *sems):
                    cp.start()

        if nk == 1:
            o_ref[...] = _dot(a_ref[...], b_ref[...], dims).astype(o_ref.dtype)
        else:
            acc_ref, k = scratch[0], ids[2]

            @pl.when(k == 0)
            def _():
                acc_ref[...] = _dot(a_ref[...], b_ref[...], dims)

            @pl.when((k > 0) & (k < nk - 1))
            def _():
                acc_ref[...] += _dot(a_ref[...], b_ref[...], dims)

            @pl.when(k == nk - 1)
            def _():
                o_ref[...] = (acc_ref[...] + _dot(a_ref[...], b_ref[...], dims)).astype(o_ref.dtype)

        if comm:
            @pl.when((ids[0] == grid[0] - 1) & (ids[1] == grid[1] - 1) & (ids[2] == grid[2] - 1))
            def _():
                for cp in comm.copies(c_ins, c_outs, *sems):
                    cp.wait()

    if mode == "tn":
        a_spec = pl.BlockSpec((tk, tm), lambda i, j, k: (k, m0 + i))
    else:
        a_spec = pl.BlockSpec((tm, tk), lambda i, j, k: (m0 + i, k))
    if mode == "nt":
        b_spec = pl.BlockSpec((tn, tk), lambda i, j, k: (j, k))
    else:
        b_spec = pl.BlockSpec((tk, tn), lambda i, j, k: (k, j))
    hbm = pl.BlockSpec(memory_space=pl.ANY)
    res = pl.pallas_call(
        functools.partial(body, nk=nk),
        grid=grid,
        in_specs=[a_spec, b_spec] + [hbm] * (aliased + n_ci),
        out_specs=[pl.BlockSpec((tm, tn), lambda i, j, k: (o0 + i, j))] + [hbm] * n_co,
        out_shape=[jax.ShapeDtypeStruct((mc * tm if into is None else M, N), out_dtype)] + (comm.out_shapes if comm else []),
        scratch_shapes=([pltpu.VMEM((tm, tn), f32)] if nk > 1 else []) + (comm.scratch() if comm else []),
        input_output_aliases={**({2: 0} if aliased else {}), **(comm.io_aliases(2 + aliased, 1) if comm else {})},
        compiler_params=_params(("arbitrary",) * 3 if comm else ("parallel", "parallel", "arbitrary")),
        name=name,
    )(a, b, *([into] if aliased else []), *(comm.ins if comm else []))
    return (res[0], list(res[1:])) if comm else res[0]


def _rms_fwd(x, g, name):
    T, D = x.shape
    tm = _pick(T, 256, 8)

    def body(x_ref, g_ref, h_ref):
        xv = x_ref[...]
        h_ref[...] = (xv * _rstd(xv) * g_ref[...]).astype(bf16)

    return pl.pallas_call(
        body, grid=(T // tm,),
        in_specs=[pl.BlockSpec((tm, D), lambda i: (i, 0)), pl.BlockSpec((1, D), lambda i: (0, 0))],
        out_specs=pl.BlockSpec((tm, D), lambda i: (i, 0)),
        out_shape=jax.ShapeDtypeStruct((T, D), bf16),
        compiler_params=_params(("parallel",)), name=name,
    )(x, g)


def _rms_bwd(x, g, dh, dy, name, comm=None):
    T, D = x.shape
    tm = _pick(T, 256, 8)
    with_dx = dy is not None
    n_ci = len(comm.ins) if comm else 0
    n_co = len(comm.out_shapes) if comm else 0

    def body(*refs):
        if with_dx:
            x_ref, g_ref, dh_ref, dy_ref = refs[:4]
            c_ins, (dx_ref, dg_ref) = refs[4:4 + n_ci], refs[4 + n_ci:6 + n_ci]
            c_outs, sems = refs[6 + n_ci:6 + n_ci + n_co], refs[6 + n_ci + n_co:]
        else:
            x_ref, g_ref, dh_ref, dg_ref = refs
        if comm:
            @pl.when(pl.program_id(0) == 0)
            def _():
                for cp in comm.copies(c_ins, c_outs, *sems):
                    cp.start()

        xv = x_ref[...]
        r = _rstd(xv)
        xh = xv * r
        dhv = dh_ref[...]
        part = jnp.sum(dhv * xh, axis=0, keepdims=True)

        @pl.when(pl.program_id(0) == 0)
        def _():
            dg_ref[...] = part

        @pl.when(pl.program_id(0) > 0)
        def _():
            dg_ref[...] += part

        if with_dx:
            dxh = dhv * g_ref[...]
            dx_ref[...] = dy_ref[...] + r * (dxh - xh * jnp.mean(dxh * xh, axis=-1, keepdims=True))

        if comm:
            @pl.when(pl.program_id(0) == T // tm - 1)
            def _():
                for cp in comm.copies(c_ins, c_outs, *sems):
                    cp.wait()

    row = pl.BlockSpec((tm, D), lambda i: (i, 0))
    vec = pl.BlockSpec((1, D), lambda i: (0, 0))
    hbm = pl.BlockSpec(memory_space=pl.ANY)
    if with_dx:
        res = pl.pallas_call(
            body, grid=(T // tm,), in_specs=[row, vec, row, row] + [hbm] * n_ci, out_specs=[row, vec] + [hbm] * n_co,
            out_shape=[jax.ShapeDtypeStruct((T, D), f32), jax.ShapeDtypeStruct((1, D), f32)] + (comm.out_shapes if comm else []),
            scratch_shapes=comm.scratch() if comm else [],
            input_output_aliases=comm.io_aliases(4, 2) if comm else {},
            compiler_params=_params(("arbitrary",)), name=name,
        )(x, g, dh, dy, *(comm.ins if comm else []))
        return (res[0], res[1], list(res[2:])) if comm else res
    return pl.pallas_call(
        body, grid=(T // tm,), in_specs=[row, vec, row], out_specs=vec,
        out_shape=jax.ShapeDtypeStruct((1, D), f32),
        compiler_params=_params(("arbitrary",)), name=name,
    )(x, g, dh)


MAX_UNIT_UNROLL = 6


def _unit_unroll(trips):
    return max(u for u in range(1, MAX_UNIT_UNROLL + 1) if trips % u == 0)


def _rows(start, size, stride):
    return pl.ds(start, size) if stride == 1 else pl.ds(start, size, stride=stride)


def _attn_units(S, d, first, prev):
    nb = S // d // BLK

    def do_first(r, c):
        first(_rows(r, BLK, d))
        return c

    lax.fori_loop(0, d, do_first, 0, unroll=_unit_unroll(d))
    if nb > 1:
        def do_prev(u, c):
            r = u // (nb - 1)
            b = u % (nb - 1) + 1
            base = r + d * b * BLK
            prev(_rows(base, BLK, d), _rows(base - d * BLK, 2 * BLK, d))
            return c

        lax.fori_loop(0, d * (nb - 1), do_prev, 0, unroll=_unit_unroll(d * (nb - 1)))


def _band_bias(nkeys):
    i = lax.broadcasted_iota(jnp.int32, (BLK, nkeys), 0)
    j = lax.broadcasted_iota(jnp.int32, (BLK, nkeys), 1)
    ok = (j <= i) if nkeys == BLK else ((j >= i) & (j <= i + BLK))
    return jnp.where(ok, 0.0, -jnp.inf).astype(f32)


def _normalize_into(src_ref, gain, dst_ref, S):
    for c in range(S // 256):
        rows = pl.ds(c * 256, 256)
        v = src_ref[rows, :]
        dst_ref[rows, :] = v * _rstd_head(v) * gain


def _attn_fwd(proj, gq, gk, B, S, comm=None):
    T, NC = proj.shape
    scale = HEAD_DIM ** -0.5
    n_ci = len(comm.ins) if comm else 0
    n_co = len(comm.out_shapes) if comm else 0

    def body(*refs):
        q_ref, k_ref, v_ref, z_ref, gq_ref, gk_ref = refs[:6]
        c_ins = refs[6:6 + n_ci]
        ag_ref, a_ref, lse_ref = refs[6 + n_ci:9 + n_ci]
        c_outs = refs[9 + n_ci:9 + n_ci + n_co]
        qs, ks, o0, o1, o2, l0, l1, l2, bias1, bias2 = refs[9 + n_ci + n_co:19 + n_ci + n_co]
        sems = refs[19 + n_ci + n_co:]
        g = pl.program_id(2)
        if comm:
            @pl.when((pl.program_id(0) == 0) & (pl.program_id(1) == 0) & (g == 0))
            def _():
                for cp in comm.copies(c_ins, c_outs, *sems):
                    cp.start()

        bias1[...] = _band_bias(BLK)
        bias2[...] = _band_bias(2 * BLK)
        _normalize_into(q_ref, gq_ref[pl.ds(g, 1), :], qs, S)
        _normalize_into(k_ref, gk_ref[pl.ds(g, 1), :], ks, S)
        o_s, l_s = (o0, o1, o2), (l0, l1, l2)

        def run(gi):
            def unit(qr, kr, nkeys):
                s = _dot(qs[qr, :], ks[kr, :], NT_DIMS) * scale + (bias1 if nkeys == BLK else bias2)[...]
                m = jnp.max(s, axis=-1, keepdims=True)
                p = jnp.exp(s - m)
                den = jnp.sum(p, axis=-1, keepdims=True)
                o_s[gi][qr, :] = _dot(p, v_ref[kr, :], NN_DIMS) * (1.0 / den)
                l_s[gi][qr, :] = jnp.broadcast_to(m + jnp.log(den), (BLK, HEAD_DIM))

            _attn_units(S, DILATIONS[gi], lambda qr: unit(qr, qr, BLK), lambda qr, kr: unit(qr, kr, 2 * BLK))

        for gi in range(N_GROUPS):
            pl.when(g == gi)(functools.partial(run, gi))

        @pl.when(g == N_GROUPS - 1)
        def _():
            for c in range(S // 256):
                rows = pl.ds(c * 256, 256)
                la, lb, lc = l0[rows, :], l1[rows, :], l2[rows, :]
                m = jnp.maximum(jnp.maximum(la, lb), lc)
                wa, wb, wc = jnp.exp(la - m), jnp.exp(lb - m), jnp.exp(lc - m)
                tot = wa + wb + wc
                a = (wa / tot) * o0[rows, :] + (wb / tot) * o1[rows, :] + (wc / tot) * o2[rows, :]
                z = z_ref[rows, :]
                a_ref[rows, :] = a
                lse_ref[rows, :] = m + jnp.log(tot)
                ag_ref[rows, :] = (a * (z * _sigmoid(z))).astype(bf16)

        if comm:
            @pl.when((pl.program_id(0) == B - 1) & (pl.program_id(1) == HEADS - 1) & (g == N_GROUPS - 1))
            def _():
                for cp in comm.copies(c_ins, c_outs, *sems):
                    cp.wait()

    def slab(col0):
        return pl.BlockSpec((S, HEAD_DIM), lambda b, h, g: (b, col0 // HEAD_DIM + g * HEADS + h))

    gain = pl.BlockSpec((N_GROUPS, HEAD_DIM), lambda b, h, g: (0, 0))
    out = pl.BlockSpec((S, HEAD_DIM), lambda b, h, g: (b, h))
    hbm = pl.BlockSpec(memory_space=pl.ANY)
    res = pl.pallas_call(
        body, grid=(B, HEADS, N_GROUPS),
        in_specs=[slab(Q0), slab(K0), slab(V0), pl.BlockSpec((S, HEAD_DIM), lambda b, h, g: (b, ZA // HEAD_DIM + h)), gain, gain]
        + [hbm] * n_ci,
        out_specs=[out, out, out] + [hbm] * n_co,
        out_shape=[jax.ShapeDtypeStruct((T, ATTN_OUT), bf16), jax.ShapeDtypeStruct((T, ATTN_OUT), f32),
                   jax.ShapeDtypeStruct((T, ATTN_OUT), f32)] + (comm.out_shapes if comm else []),
        scratch_shapes=[pltpu.VMEM((S, HEAD_DIM), f32)] * 8 + [pltpu.VMEM((BLK, BLK), f32), pltpu.VMEM((BLK, 2 * BLK), f32)]
        + (comm.scratch() if comm else []),
        compiler_params=_params(("arbitrary", "arbitrary", "arbitrary")), name="attn_fwd",
    )(proj, proj, proj, proj, gq, gk, *(comm.ins if comm else []))
    return res[0], res[1], res[2], list(res[3:])


def _rms_bwd_rows(raw, gain, dn, on_mxu=True):
    r = _rstd_head(raw) if on_mxu else _rstd(raw)
    xh = raw * r
    dxh = dn * gain
    if on_mxu:
        mean = _row_sum(dxh * xh) * (1.0 / raw.shape[1])
    else:
        mean = jnp.mean(dxh * xh, axis=-1, keepdims=True)
    return r * (dxh - xh * mean), jnp.sum(dn * xh, axis=0, keepdims=True)


def _attn_bwd(proj, gq, gk, a_comb, lse, dag, dproj, B, S):
    T, NC = proj.shape
    scale = HEAD_DIM ** -0.5

    def body(q_ref, k_ref, v_ref, z_ref, gq_ref, gk_ref, a_ref, lse_ref, dag_ref, dproj_in, dproj_ref, dgq_ref, dgk_ref,
             qs, ks, da_s, dl_s, dq_s, dk_s, dv_s, bias1, bias2, stage, dz_stage, sem, dz_sem):
        b, h, g = pl.program_id(0), pl.program_id(1), pl.program_id(2)
        bias1[...] = _band_bias(BLK)
        bias2[...] = _band_bias(2 * BLK)
        gq_row, gk_row = gq_ref[pl.ds(g, 1), :], gk_ref[pl.ds(g, 1), :]
        _normalize_into(q_ref, gq_row, qs, S)
        _normalize_into(k_ref, gk_row, ks, S)

        def dst(c):
            return dproj_ref.at[pl.ds(b * S, S), pl.ds((Q0, K0, V0)[c] + (g * HEADS + h) * HEAD_DIM, HEAD_DIM)]

        out = _Deferred(stage, sem, (b * HEADS + h) * N_GROUPS + g, B * HEADS * N_GROUPS - 1, 3)
        out.begin(dst)

        @pl.when((b == 0) & (h == 0) & (g == 0))
        def _():
            dgq_ref[...] = jnp.zeros_like(dgq_ref)
            dgk_ref[...] = jnp.zeros_like(dgk_ref)

        @pl.when(g == 0)
        def _():
            for c in range(S // 256):
                rows = pl.ds(c * 256, 256)
                z, a, dg_ = z_ref[rows, :], a_ref[rows, :], dag_ref[rows, :]
                sg = _sigmoid(z)
                da = dg_ * (z * sg)
                da_s[rows, :] = da
                dl_s[rows, :] = _row_sum(da * a)
                dz_stage[rows, :] = (dg_ * a * (sg * (1.0 + z * (1.0 - sg)))).astype(bf16)
            cp = pltpu.make_async_copy(dz_stage, dproj_ref.at[pl.ds(b * S, S), pl.ds(ZA + h * HEAD_DIM, HEAD_DIM)], dz_sem)
            cp.start()
            cp.wait()

        dk_s[...] = jnp.zeros_like(dk_s)
        dv_s[...] = jnp.zeros_like(dv_s)

        def run(gi):
            def unit(qr, kr, nkeys):
                q, k, v = qs[qr, :], ks[kr, :], v_ref[kr, :]
                s = _dot(q, k, NT_DIMS) * scale
                p = jnp.exp(s + (bias1 if nkeys == BLK else bias2)[...] - lse_ref[qr, :][:, :1])
                da = da_s[qr, :]
                dp = _dot(da, v, NT_DIMS)
                ds = p * (dp - dl_s[qr, :][:, :1]) * scale
                dq_s[qr, :] = _dot(ds, k, NN_DIMS)
                dk_s[kr, :] += _dot(ds, q, TN_DIMS)
                dv_s[kr, :] += _dot(p, da, TN_DIMS)

            _attn_units(S, DILATIONS[gi], lambda qr: unit(qr, qr, BLK), lambda qr, kr: unit(qr, kr, 2 * BLK))

        for gi in range(N_GROUPS):
            pl.when(g == gi)(functools.partial(run, gi))

        gq_acc = jnp.zeros((1, HEAD_DIM), f32)
        gk_acc = jnp.zeros((1, HEAD_DIM), f32)
        for c in range(S // 256):
            rows = pl.ds(c * 256, 256)
            dq, gq_p = _rms_bwd_rows(q_ref[rows, :], gq_row, dq_s[rows, :])
            dk, gk_p = _rms_bwd_rows(k_ref[rows, :], gk_row, dk_s[rows, :])
            gq_acc, gk_acc = gq_acc + gq_p, gk_acc + gk_p
            stage[out.slot, 0, rows, :] = dq.astype(bf16)
            stage[out.slot, 1, rows, :] = dk.astype(bf16)
            stage[out.slot, 2, rows, :] = dv_s[rows, :].astype(bf16)
        dgq_ref[pl.ds(g, 1), :] += gq_acc
        dgk_ref[pl.ds(g, 1), :] += gk_acc
        for c in range(3):
            out.start(c, dst(c))
        out.end(dst)

    def slab(col0):
        return pl.BlockSpec((S, HEAD_DIM), lambda b, h, g: (b, col0 // HEAD_DIM + g * HEADS + h))

    gain = pl.BlockSpec((N_GROUPS, HEAD_DIM), lambda b, h, g: (0, 0))
    per_slot = pl.BlockSpec((S, HEAD_DIM), lambda b, h, g: (b, h))
    return pl.pallas_call(
        body, grid=(B, HEADS, N_GROUPS),
        in_specs=[slab(Q0), slab(K0), slab(V0), pl.BlockSpec((S, HEAD_DIM), lambda b, h, g: (b, ZA // HEAD_DIM + h)), gain, gain,
                  per_slot, per_slot, per_slot, pl.BlockSpec(memory_space=pl.ANY)],
        out_specs=[pl.BlockSpec(memory_space=pl.ANY), gain, gain],
        out_shape=[jax.ShapeDtypeStruct(dproj.shape, dproj.dtype), jax.ShapeDtypeStruct((N_GROUPS, HEAD_DIM), f32),
                   jax.ShapeDtypeStruct((N_GROUPS, HEAD_DIM), f32)],
        scratch_shapes=[pltpu.VMEM((S, HEAD_DIM), f32)] * 7 + [pltpu.VMEM((BLK, BLK), f32), pltpu.VMEM((BLK, 2 * BLK), f32),
                                                                pltpu.VMEM((2, 3, S, HEAD_DIM), bf16), pltpu.VMEM((S, HEAD_DIM), bf16),
                                                                pltpu.SemaphoreType.DMA((2, 3)), pltpu.SemaphoreType.DMA],
        input_output_aliases={9: 0},
        compiler_params=_params(("arbitrary", "arbitrary", "arbitrary")), name="attn_bwd",
    )(proj, proj, proj, proj, gq, gk, a_comb, lse, dag, dproj)


def _conv_fwd(proj, conv_w, B, S):
    T, NC = proj.shape
    tc = 2 * LANES

    def body(cb_ref, cc_ref, cv_ref, z_ref, w_ref, c_ref, ub):
        u = cc_ref[...] * cv_ref[...]
        ub[0:8, :] = jnp.zeros((8, tc), f32)
        ub[8:8 + S, :] = u
        w = w_ref[...]
        y = w[0:1] * u + w[1:2] * ub[pl.ds(7, S), :] + w[2:3] * ub[pl.ds(6, S), :]
        z = z_ref[...]
        c_ref[...] = (cb_ref[...] * y * (z * _sigmoid(z))).astype(bf16)

    def seg(col0):
        return pl.BlockSpec((S, tc), lambda j, b: (b, col0 // tc + j))

    return pl.pallas_call(
        body, grid=(CONV_W // tc, B),
        in_specs=[seg(CB), seg(CC), seg(CV), seg(ZC), pl.BlockSpec((3, tc), lambda j, b: (0, j))],
        out_specs=pl.BlockSpec((S, tc), lambda j, b: (b, j)),
        out_shape=jax.ShapeDtypeStruct((T, CONV_W), bf16),
        scratch_shapes=[pltpu.VMEM((S + 8, tc), f32)],
        compiler_params=_params(("parallel", "parallel")), name="conv_fwd",
    )(proj, proj, proj, proj, conv_w)


def _conv_bwd(proj, conv_w, dc, dproj, B, S):
    T, NC = proj.shape
    tc = 2 * LANES

    def body(cb_ref, cc_ref, cv_ref, z_ref, w_ref, dc_ref, dproj_in, dproj_ref, dw_ref, ub, db, stage, sem):
        j, b = pl.program_id(0), pl.program_id(1)

        def dst(c):
            return dproj_ref.at[pl.ds(b * S, S), pl.ds((CB, CC, CV, ZC)[c] + j * tc, tc)]

        out = _Deferred(stage, sem, j * B + b, (CONV_W // tc) * B - 1, 4)
        out.begin(dst)
        cb, cc, cv, z, dcv = cb_ref[...], cc_ref[...], cv_ref[...], z_ref[...], dc_ref[...]
        u = cc * cv
        ub[0:8, :] = jnp.zeros((8, tc), f32)
        ub[8:8 + S, :] = u
        u1, u2 = ub[pl.ds(7, S), :], ub[pl.ds(6, S), :]
        w = w_ref[...]
        y = w[0:1] * u + w[1:2] * u1 + w[2:3] * u2
        sg = _sigmoid(z)
        si = z * sg
        dyv = dcv * cb * si
        db[0:S, :] = dyv
        db[S:S + 8, :] = jnp.zeros((8, tc), f32)
        du = w[0:1] * dyv + w[1:2] * db[pl.ds(1, S), :] + w[2:3] * db[pl.ds(2, S), :]
        stage[out.slot, 0] = (dcv * y * si).astype(bf16)
        stage[out.slot, 1] = (du * cv).astype(bf16)
        stage[out.slot, 2] = (du * cc).astype(bf16)
        stage[out.slot, 3] = (dcv * cb * y * (sg * (1.0 + z * (1.0 - sg)))).astype(bf16)
        for c in range(4):
            out.start(c, dst(c))
        part = jnp.concatenate([jnp.sum(dyv * u, axis=0, keepdims=True), jnp.sum(dyv * u1, axis=0, keepdims=True),
                                jnp.sum(dyv * u2, axis=0, keepdims=True)], axis=0)

        @pl.when(b == 0)
        def _():
            dw_ref[...] = part

        @pl.when(b > 0)
        def _():
            dw_ref[...] += part

        out.end(dst)

    def seg(col0):
        return pl.BlockSpec((S, tc), lambda j, b: (b, col0 // tc + j))

    return pl.pallas_call(
        body, grid=(CONV_W // tc, B),
        in_specs=[seg(CB), seg(CC), seg(CV), seg(ZC), pl.BlockSpec((3, tc), lambda j, b: (0, j)),
                  pl.BlockSpec((S, tc), lambda j, b: (b, j)), pl.BlockSpec(memory_space=pl.ANY)],
        out_specs=[pl.BlockSpec(memory_space=pl.ANY), pl.BlockSpec((3, tc), lambda j, b: (0, j))],
        out_shape=[jax.ShapeDtypeStruct(dproj.shape, dproj.dtype), jax.ShapeDtypeStruct((3, CONV_W), f32)],
        scratch_shapes=[pltpu.VMEM((S + 8, tc), f32), pltpu.VMEM((S + 8, tc), f32), pltpu.VMEM((2, 4, S, tc), bf16),
                        pltpu.SemaphoreType.DMA((2, 4))],
        input_output_aliases={6: 0},
        compiler_params=_params(("arbitrary", "arbitrary")), name="conv_bwd",
    )(proj, proj, proj, proj, conv_w, dc, dproj)


MEM_CHUNK = 1024


def _mem_fwd(proj, mkv, gq, gk, B, S):
    T, NC = proj.shape
    scale = MEM_HD ** -0.5

    def body(q_ref, z_ref, k_ref, v_ref, gq_ref, gk_ref, o_ref):
        kv = k_ref[...]
        kn = (kv * _rstd_head(kv) * gk_ref[...]).astype(bf16)
        vv = v_ref[...].astype(bf16)

        def chunk(c, carry):
            rows = pl.ds(pl.multiple_of(c * MEM_CHUNK, MEM_CHUNK), MEM_CHUNK)
            q = q_ref[rows, :]
            qn = q * _rstd(q) * gq_ref[...]
            s = _dot(qn, kn, NT_DIMS) * scale
            p = jnp.exp(s - jnp.max(s, axis=-1, keepdims=True))
            p = p / jnp.sum(p, axis=-1, keepdims=True)
            z = z_ref[rows, :]
            o_ref[rows, :] = (_dot(p, vv, NN_DIMS) * (z * _sigmoid(z))).astype(bf16)
            return carry

        lax.fori_loop(0, S // MEM_CHUNK, chunk, 0)

    gain = pl.BlockSpec((1, MEM_HD), lambda b, h: (0, 0))
    return pl.pallas_call(
        body, grid=(B, MEM_HEADS),
        in_specs=[pl.BlockSpec((S, MEM_HD), lambda b, h: (b, MQ // MEM_HD + h)),
                  pl.BlockSpec((S, MEM_HD), lambda b, h: (b, ZM // MEM_HD + h)),
                  pl.BlockSpec((MEM_LEN, MEM_HD), lambda b, h: (b, h)),
                  pl.BlockSpec((MEM_LEN, MEM_HD), lambda b, h: (b, MEM_HEADS + h)), gain, gain],
        out_specs=pl.BlockSpec((S, MEM_HD), lambda b, h: (b, h)),
        out_shape=jax.ShapeDtypeStruct((T, MEM_W), bf16),
        compiler_params=_params(("parallel", "parallel")), name="mem_fwd",
    )(proj, proj, mkv, mkv, gq, gk)


def _mem_bwd(proj, mkv, gq, gk, dmo, dproj, B, S):
    T, NC = proj.shape
    scale = MEM_HD ** -0.5

    def body(q_ref, z_ref, k_ref, v_ref, gq_ref, gk_ref, dmo_ref, dproj_in, dproj_ref, dmk_ref, dmv_ref, dgq_ref, dgk_ref,
             dkn_s, dv_s, gq_s, stage, sem):
        b, h = pl.program_id(0), pl.program_id(1)

        def dst(c):
            return dproj_ref.at[pl.ds(b * S, S), pl.ds((MQ, ZM)[c] + h * MEM_HD, MEM_HD)]

        out = _Deferred(stage, sem, b * MEM_HEADS + h, B * MEM_HEADS - 1, 2)
        out.begin(dst)
        kv = k_ref[...]
        kn = (kv * _rstd_head(kv) * gk_ref[...]).astype(bf16)
        vv = v_ref[...].astype(bf16)
        dkn_s[...] = jnp.zeros_like(dkn_s)
        dv_s[...] = jnp.zeros_like(dv_s)
        gq_s[...] = jnp.zeros_like(gq_s)

        def chunk(c, carry):
            rows = pl.ds(pl.multiple_of(c * MEM_CHUNK, MEM_CHUNK), MEM_CHUNK)
            q = q_ref[rows, :]
            qn = q * _rstd(q) * gq_ref[...]
            s = _dot(qn, kn, NT_DIMS) * scale
            p = jnp.exp(s - jnp.max(s, axis=-1, keepdims=True))
            p = p / jnp.sum(p, axis=-1, keepdims=True)
            mo = _dot(p, vv, NN_DIMS)
            z, dg_ = z_ref[rows, :], dmo_ref[rows, :]
            sg = _sigmoid(z)
            do = dg_ * (z * sg)
            stage[out.slot, 1, rows, :] = (dg_ * mo * (sg * (1.0 + z * (1.0 - sg)))).astype(bf16)
            dp = _dot(do, vv, NT_DIMS)
            ds = p * (dp - jnp.sum(do * mo, axis=-1, keepdims=True)) * scale
            dq, gq_p = _rms_bwd_rows(q, gq_ref[...], _dot(ds, kn, NN_DIMS), on_mxu=False)
            stage[out.slot, 0, rows, :] = dq.astype(bf16)
            gq_s[...] += gq_p
            dkn_s[...] += _dot(ds, qn, TN_DIMS)
            dv_s[...] += _dot(p, do, TN_DIMS)
            return carry

        lax.fori_loop(0, S // MEM_CHUNK, chunk, 0)
        for c in range(2):
            out.start(c, dst(c))
        dk, gk_p = _rms_bwd_rows(kv, gk_ref[...], dkn_s[...])
        dmk_ref[...] = dk
        dmv_ref[...] = dv_s[...]

        @pl.when((b == 0) & (h == 0))
        def _():
            dgq_ref[...] = gq_s[...]
            dgk_ref[...] = gk_p

        @pl.when((b > 0) | (h > 0))
        def _():
            dgq_ref[...] += gq_s[...]
            dgk_ref[...] += gk_p

        out.end(dst)

    gain = pl.BlockSpec((1, MEM_HD), lambda b, h: (0, 0))
    kvb = pl.BlockSpec((MEM_LEN, MEM_HD), lambda b, h: (b, h))
    return pl.pallas_call(
        body, grid=(B, MEM_HEADS),
        in_specs=[pl.BlockSpec((S, MEM_HD), lambda b, h: (b, MQ // MEM_HD + h)),
                  pl.BlockSpec((S, MEM_HD), lambda b, h: (b, ZM // MEM_HD + h)),
                  kvb, pl.BlockSpec((MEM_LEN, MEM_HD), lambda b, h: (b, MEM_HEADS + h)), gain, gain,
                  pl.BlockSpec((S, MEM_HD), lambda b, h: (b, h)), pl.BlockSpec(memory_space=pl.ANY)],
        out_specs=[pl.BlockSpec(memory_space=pl.ANY), kvb, kvb, gain, gain],
        out_shape=[jax.ShapeDtypeStruct(dproj.shape, dproj.dtype), jax.ShapeDtypeStruct((B * MEM_LEN, MEM_W), f32),
                   jax.ShapeDtypeStruct((B * MEM_LEN, MEM_W), f32), jax.ShapeDtypeStruct((1, MEM_HD), f32),
                   jax.ShapeDtypeStruct((1, MEM_HD), f32)],
        scratch_shapes=[pltpu.VMEM((MEM_LEN, MEM_HD), f32), pltpu.VMEM((MEM_LEN, MEM_HD), f32), pltpu.VMEM((1, MEM_HD), f32),
                        pltpu.VMEM((2, 2, S, MEM_HD), bf16), pltpu.SemaphoreType.DMA((2, 2))],
        input_output_aliases={7: 0},
        compiler_params=_params(("arbitrary", "arbitrary")), name="mem_bwd",
    )(proj, proj, mkv, mkv, gq, gk, dmo, dproj)


def _merge_fwd(proj, ag, cg, mg, wa, wc, wm):
    T, NC = proj.shape
    D = wa.shape[1]
    tm, tn = _pick(T, 1024), _pick(D, 512)
    assert GT % tn == 0

    def body(ag_ref, cg_ref, mg_ref, wa_ref, wc_ref, wm_ref, g0_ref, g1_ref, g2_ref, mer_ref, ba_ref, bc_ref, bm_ref):
        ba = _dot(ag_ref[...], wa_ref[...], NN_DIMS)
        bc = _dot(cg_ref[...], wc_ref[...], NN_DIMS)
        bm = _dot(mg_ref[...], wm_ref[...], NN_DIMS)
        mer_ref[...] = (_sigmoid(g0_ref[...]) * ba + _sigmoid(g1_ref[...]) * bc + _sigmoid(g2_ref[...]) * bm).astype(bf16)
        ba_ref[...] = ba.astype(bf16)
        bc_ref[...] = bc.astype(bf16)
        bm_ref[...] = bm.astype(bf16)

    def act(w):
        return pl.BlockSpec((tm, w), lambda i, j: (i, 0))

    def wt(k):
        return pl.BlockSpec((k, tn), lambda i, j: (0, j))

    def gate(n):
        return pl.BlockSpec((tm, tn), lambda i, j: (i, (GT + n * D) // tn + j))

    out = pl.BlockSpec((tm, tn), lambda i, j: (i, j))
    return pl.pallas_call(
        body, grid=(T // tm, D // tn),
        in_specs=[act(ATTN_OUT), act(CONV_W), act(MEM_W), wt(ATTN_OUT), wt(CONV_W), wt(MEM_W), gate(0), gate(1), gate(2)],
        out_specs=[out] * 4, out_shape=[jax.ShapeDtypeStruct((T, D), bf16)] * 4,
        compiler_params=_params(("parallel", "parallel")), name="merge_fwd",
    )(ag, cg, mg, wa, wc, wm, proj, proj, proj)


def _out_loss(merged, w_out, x, tgt):
    T, D = x.shape
    tm, tn = _pick(T, 1024), _pick(D, 512)

    def body(m_ref, w_ref, x_ref, t_ref, dy_ref, dyb_ref, lp_ref):
        e = x_ref[...] + _dot(m_ref[...], w_ref[...], NN_DIMS) - t_ref[...]
        dy = e / D
        dy_ref[...] = dy
        dyb_ref[...] = dy.astype(bf16)
        part = jnp.full((1, 8, LANES), jnp.sum(e * e), f32)

        @pl.when(pl.program_id(1) == 0)
        def _():
            lp_ref[...] = part

        @pl.when(pl.program_id(1) > 0)
        def _():
            lp_ref[...] += part

    tile = pl.BlockSpec((tm, tn), lambda i, j: (i, j))
    return pl.pallas_call(
        body, grid=(T // tm, D // tn),
        in_specs=[pl.BlockSpec((tm, D), lambda i, j: (i, 0)), pl.BlockSpec((D, tn), lambda i, j: (0, j)), tile, tile],
        out_specs=[tile, tile, pl.BlockSpec((1, 8, LANES), lambda i, j: (i, 0, 0))],
        out_shape=[jax.ShapeDtypeStruct((T, D), f32), jax.ShapeDtypeStruct((T, D), bf16),
                   jax.ShapeDtypeStruct((T // tm, 8, LANES), f32)],
        compiler_params=_params(("parallel", "arbitrary")), name="out_loss",
    )(merged, w_out, x, tgt)


def _merge_bwd(proj, dyb, w_out, ba, bc, bm):
    T, NC = proj.shape
    D = w_out.shape[0]
    tm, tn = _pick(T, 1024), _pick(D, 512)
    assert GT % tn == 0

    def body(dy_ref, w_ref, ba_ref, bc_ref, bm_ref, g0_ref, g1_ref, g2_ref, da_ref, dc_ref, dm_ref, dproj_ref, stage, sem):
        i, j = pl.program_id(0), pl.program_id(1)

        def dst(n):
            return dproj_ref.at[pl.ds(i * tm, tm), pl.ds(GT + n * D + j * tn, tn)]

        out = _Deferred(stage, sem, i * (D // tn) + j, (T // tm) * (D // tn) - 1, 3)
        out.begin(dst)
        dmer = _dot(dy_ref[...], w_ref[...], NT_DIMS)
        for n, (g_ref, br_ref, o_ref) in enumerate(((g0_ref, ba_ref, da_ref), (g1_ref, bc_ref, dc_ref), (g2_ref, bm_ref, dm_ref))):
            sg = _sigmoid(g_ref[...])
            o_ref[...] = (sg * dmer).astype(bf16)
            stage[out.slot, n] = (dmer * br_ref[...].astype(f32) * (sg * (1.0 - sg))).astype(bf16)
            out.start(n, dst(n))
        out.end(dst)

    tile = pl.BlockSpec((tm, tn), lambda i, j: (i, j))

    def gate(n):
        return pl.BlockSpec((tm, tn), lambda i, j: (i, (GT + n * D) // tn + j))

    return pl.pallas_call(
        body, grid=(T // tm, D // tn),
        in_specs=[pl.BlockSpec((tm, D), lambda i, j: (i, 0)), pl.BlockSpec((tn, D), lambda i, j: (j, 0)), tile, tile, tile,
                  gate(0), gate(1), gate(2)],
        out_specs=[tile, tile, tile, pl.BlockSpec(memory_space=pl.ANY)],
        out_shape=[jax.ShapeDtypeStruct((T, D), bf16)] * 3 + [jax.ShapeDtypeStruct((T, NC), bf16)],
        scratch_shapes=[pltpu.VMEM((2, 3, tm, tn), bf16), pltpu.SemaphoreType.DMA((2, 3))],
        compiler_params=_params(("arbitrary", "arbitrary")), name="merge_bwd",
    )(dyb, w_out, ba, bc, bm, proj, proj, proj)


def _fwd_bwd_head(x2, mem2, t2, proj, attn, B, S, mem_norm_g, attn_q_norm, attn_k_norm, conv_w, mem_q_norm, mem_k_norm,
                  w_kv, w_a, w_c, w_m, w_out):
    D = x2.shape[1]
    mng = mem_norm_g.reshape(1, D)
    mqn, mkn = mem_q_norm.reshape(1, MEM_HD), mem_k_norm.reshape(1, MEM_HD)
    ag, a_comb, lse = attn

    mh = _rms_fwd(mem2, mng, "rms_mem")
    mkv = _mm(mh, w_kv, mode="nn", out_dtype=f32, name="mkv")
    cg = _conv_fwd(proj, conv_w, B, S)
    mg = _mem_fwd(proj, mkv, mqn, mkn, B, S)
    merged, ba, bc, bm = _merge_fwd(proj, ag, cg, mg, w_a, w_c, w_m)
    dy, dyb, lp = _out_loss(merged, w_out, x2, t2)
    sq_err = jnp.sum(lp[:, 0, 0])

    g = {}
    g["w_out"] = _mm(merged, dyb, mode="tn", out_dtype=f32, name="dw_out")
    dba, dbc, dbm, dproj = _merge_bwd(proj, dyb, w_out, ba, bc, bm)
    g["w_br_attn"] = _mm(ag, dba, mode="tn", out_dtype=f32, name="dw_br_attn")
    g["w_br_conv"] = _mm(cg, dbc, mode="tn", out_dtype=f32, name="dw_br_conv")
    g["w_br_mem"] = _mm(mg, dbm, mode="tn", out_dtype=f32, name="dw_br_mem")
    dag = _mm(dba, w_a, mode="nt", out_dtype=f32, name="d_attn_out")
    dcg = _mm(dbc, w_c, mode="nt", out_dtype=f32, name="d_conv_out")
    dmg = _mm(dbm, w_m, mode="nt", out_dtype=f32, name="d_mem_out")
    dproj, g["attn_q_norm"], g["attn_k_norm"] = _attn_bwd(proj, attn_q_norm, attn_k_norm, a_comb, lse, dag, dproj, B, S)
    dproj, g["conv_w"] = _conv_bwd(proj, conv_w, dcg, dproj, B, S)
    dproj, dmk, dmv, dgmq, dgmk = _mem_bwd(proj, mkv, mqn, mkn, dmg, dproj, B, S)
    g["mem_q_norm"], g["mem_k_norm"] = dgmq.reshape(MEM_HD), dgmk.reshape(MEM_HD)
    dmkv = jnp.concatenate([dmk, dmv], axis=1)
    dmh = _mm(dmkv, w_kv, mode="nt", out_dtype=f32, name="d_mem_h")
    g["mem_norm_g"] = _rms_bwd(mem2, mng, dmh, None, "rms_mem_bwd").reshape(D)
    return sq_err, g, dict(dy=dy, dproj=dproj, mh=mh, dmkv=dmkv)


MESH = pl.DeviceIdType.MESH
HBM = pl.BlockSpec(memory_space=pl.ANY)


def _me():
    x, y, c = lax.axis_index("x"), lax.axis_index("y"), lax.axis_index("c")
    return (x, y, c), 4 * x + 2 * y + c


def _peer(k):
    (x, y, c), _ = _me()
    p = (1 - x if k & 4 else x, 1 - y if k & 2 else y, 1 - c if k & 1 else c)
    return p, 4 * p[0] + 2 * p[1] + p[2]


def _window(ref, axis, size, idx):
    if axis is None:
        return ref.at[idx]
    if axis == 0:
        return ref.at[pl.ds(idx * size, size), :]
    return ref.at[:, pl.ds(idx * size, size)]


def _full_shape(s, axis):
    if axis is None:
        return (N_DEV,) + s.shape
    return tuple(N_DEV * d if i == axis else d for i, d in enumerate(s.shape))


OTHER_CHIPS = (4, 2, 6)


def _spread_comm(shards, axes):
    n = len(shards)

    def copies(ins, outs, send_sems, recv_sems, base=0):
        _, me = _me()
        out = []
        for a in range(n):
            mine = _window(outs[a], axes[a], None if axes[a] is None else shards[a].shape[axes[a]], me)
            out.append(pltpu.make_async_copy(ins[a], mine, send_sems.at[base + a, N_CHIP]))
            for k, dist in enumerate((1,) + OTHER_CHIPS):
                dev, _ = _peer(dist)
                out.append(pltpu.make_async_remote_copy(
                    src_ref=ins[a], dst_ref=mine, send_sem=send_sems.at[base + a, k], recv_sem=recv_sems.at[base + a, k],
                    device_id=dev, device_id_type=MESH))
        return out

    shapes = [jax.ShapeDtypeStruct(_full_shape(s, ax), s.dtype) for s, ax in zip(shards, axes)]
    return _Comm(shards, shapes, (n, N_CHIP + 1), copies)


def _forward_blocks(fulls, axes):
    n = len(fulls)
    sizes = [None if ax is None else f.shape[ax] // N_DEV for f, ax in zip(fulls, axes)]

    def body(*refs):
        outs = refs[n:2 * n]
        send_sems, recv_sems = refs[2 * n:]
        sibling, _ = _peer(1)
        copies = []
        for j, dist in enumerate(OTHER_CHIPS):
            _, held = _peer(dist)
            for a in range(n):
                block = _window(outs[a], axes[a], sizes[a], held)
                copies.append(pltpu.make_async_remote_copy(
                    src_ref=block, dst_ref=block, send_sem=send_sems.at[a, j], recv_sem=recv_sems.at[a, j],
                    device_id=sibling, device_id_type=MESH))
        for cp in copies:
            cp.start()
        for cp in copies:
            cp.wait()

    return pl.pallas_call(
        body, in_specs=[HBM] * n, out_specs=[HBM] * n,
        out_shape=[jax.ShapeDtypeStruct(f.shape, f.dtype) for f in fulls],
        scratch_shapes=[pltpu.SemaphoreType.DMA((n, len(OTHER_CHIPS))), pltpu.SemaphoreType.DMA((n, len(OTHER_CHIPS)))],
        input_output_aliases={a: a for a in range(n)},
        name="forward_blocks",
    )(*fulls)


def _shard_order():
    (x, y, c), me = _me()
    xs, ys, xo, yo = _peer(4)[1], _peer(2)[1], _peer(5)[1], _peer(3)[1]
    north = c == 1
    ids = [me, _peer(1)[1], jnp.where(north, xs, ys), jnp.where(north, yo, xo), jnp.where(north, ys, xs),
           jnp.where(north, xo, yo), _peer(6)[1], _peer(7)[1]]
    return jnp.stack(ids).astype(jnp.int32)


def _proj_gather(h, w_shard, order, comm, comm_at):
    T, K = h.shape
    C = w_shard.shape[1]
    tm = _pick(T, 1024)
    nm = T // tm
    ahead = max(nm - 2, 0)
    n_ci, n_co = len(comm.ins), len(comm.out_shapes)

    def body(*refs):
        order_ref, h_ref, ws_ref = refs[:3]
        c_ins = refs[3:3 + n_ci]
        o_ref, wf_ref = refs[3 + n_ci:5 + n_ci]
        c_outs = refs[5 + n_ci:5 + n_ci + n_co]
        wbuf, send_sems, recv_sems, own_sem, buf_sems, c_send, c_recv = refs[5 + n_ci + n_co:]
        t, i = pl.program_id(0), pl.program_id(1)

        @pl.when((t == comm_at) & (i == 0))
        def _():
            for cp in comm.copies(c_ins, c_outs, c_send, c_recv):
                cp.start()
        (x, y, c), me = _me()
        sibling, sib_id = _peer(1)
        chips = [_peer(dist) for dist in OTHER_CHIPS]

        def win(idx):
            return wf_ref.at[:, pl.ds(idx * C, C)]

        def copy(k, block, to, src=None):
            return pltpu.make_async_remote_copy(
                src_ref=win(block) if src is None else src, dst_ref=win(block),
                send_sem=send_sems.at[k], recv_sem=recv_sems.at[k], device_id=to, device_id_type=MESH)

        def fetch(tt, src):
            return pltpu.make_async_copy(src, wbuf.at[tt % 2], buf_sems.at[tt % 2])

        own = pltpu.make_async_copy(ws_ref, win(me), own_sem)
        (x_nbr, x_id), (y_nbr, y_id), (_, d_id) = chips

        def half(k, block, top, to):
            rows = wf_ref.at[pl.ds(0 if top else K // 2, K // 2), pl.ds(block * C, C)]
            return pltpu.make_async_remote_copy(src_ref=rows, dst_ref=rows, send_sem=send_sems.at[k], recv_sem=recv_sems.at[k],
                                                device_id=to, device_id_type=MESH)

        here = (x, y, c)

        def pass_on(from_x):
            if from_x:
                copy(4, x_id, sibling).start()
                half(8, x_id, False, y_nbr).start()
            else:
                copy(5, y_id, sibling).start()
                half(7, y_id, True, x_nbr).start()

        @pl.when((t == 0) & (i == 0))
        def _():
            fetch(0, ws_ref).start()
            own.start()
            copy(0, me, sibling, src=ws_ref).start()

        for tt in range(N_DEV):
            @pl.when((t == tt) & (i == 0))
            def _(tt=tt):
                fetch(tt, ws_ref).wait()

        o_ref[...] = _dot(h_ref[...], wbuf[t % 2], NN_DIMS)

        def schedule(first_x):
            on = c == (1 if first_x else 0)
            (f_dev, f_id), (g_dev, g_id) = ((x_nbr, x_id), (y_nbr, y_id)) if first_x else ((y_nbr, y_id), (x_nbr, x_id))
            kf, kg = (1, 2) if first_x else (2, 1)
            pf, pg = (4, 5) if first_x else (5, 4)

            @pl.when((t == 0) & (i == 0) & on)
            def _():
                copy(kf, me, f_dev, src=ws_ref).start()

            def event(nxt):
                if nxt == 1:
                    copy(0, sib_id, here).wait_recv()
                    return sib_id
                if nxt == 2:
                    copy(kf, f_id, here).wait_recv()
                    copy(kf, me, f_dev, src=ws_ref).wait_send()
                    copy(kg, me, g_dev, src=ws_ref).start()
                    pass_on(first_x)
                    return f_id
                if nxt == 3:
                    copy(pg, g_id + 1 - 2 * c, here).wait_recv()
                    return g_id + 1 - 2 * c
                if nxt == 4:
                    copy(kg, g_id, here).wait_recv()
                    pass_on(not first_x)
                    return g_id
                if nxt == 5:
                    copy(pf, f_id + 1 - 2 * c, here).wait_recv()
                    return f_id + 1 - 2 * c
                if nxt == 6:
                    half(7, d_id, True, here).wait_recv()
                    half(8, d_id, False, here).wait_recv()
                    copy(6, d_id, sibling).start()
                    return d_id
                copy(6, d_id + 1 - 2 * c, here).wait_recv()
                return d_id + 1 - 2 * c

            for tt in range(N_DEV - 1):
                @pl.when((t == tt) & (i == ahead) & on)
                def _(tt=tt):
                    fetch(tt + 1, win(event(tt + 1))).start()

            @pl.when((t == N_DEV - 1) & (i == nm - 1) & on)
            def _():
                copy(kg, me, g_dev, src=ws_ref).wait_send()

        schedule(True)
        schedule(False)

        @pl.when((t == N_DEV - 1) & (i == nm - 1))
        def _():
            copy(0, me, sibling, src=ws_ref).wait_send()
            half(7, y_id, True, x_nbr).wait_send()
            half(8, x_id, False, y_nbr).wait_send()
            for j, (_, dev_id) in enumerate(chips):
                copy(4 + j, dev_id, sibling).wait_send()
            own.wait()
            for cp in comm.copies(c_ins, c_outs, c_send, c_recv):
                cp.wait()

    hbm = pl.BlockSpec(memory_space=pl.ANY)
    res = pl.pallas_call(
        body,
        grid_spec=pltpu.PrefetchScalarGridSpec(
            num_scalar_prefetch=1, grid=(N_DEV, nm),
            in_specs=[pl.BlockSpec((tm, K), lambda t, i, order_ref: (i, 0)), hbm] + [hbm] * n_ci,
            out_specs=[pl.BlockSpec((tm, C), lambda t, i, order_ref: (i, order_ref[t])), hbm] + [hbm] * n_co,
            scratch_shapes=[pltpu.VMEM((2, K, C), bf16), pltpu.SemaphoreType.DMA((9,)), pltpu.SemaphoreType.DMA((9,)),
                            pltpu.SemaphoreType.DMA, pltpu.SemaphoreType.DMA((2,))] + comm.scratch()),
        out_shape=[jax.ShapeDtypeStruct((T, N_DEV * C), f32), jax.ShapeDtypeStruct((K, N_DEV * C), bf16)] + comm.out_shapes,
        compiler_params=_params(("arbitrary", "arbitrary")), name="proj_gather",
    )(order, h, w_shard, *comm.ins)
    return res[0], res[1], list(res[2:])


N_CHIP = 4


def _shard_shape(g, ax):
    return tuple(d // N_DEV if i == ax else d for i, d in enumerate(g.shape))


def _sibling_comm(grads, axes):
    n = len(grads)
    sizes = [g.shape[ax] // N_DEV for g, ax in zip(grads, axes)]

    def copies(ins, lands, send_sems, recv_sems, base=0):
        sibling, _ = _peer(1)
        out = []
        for j in range(N_CHIP):
            _, owner = _peer(2 * j + 1)
            for a in range(n):
                out.append(pltpu.make_async_remote_copy(
                    src_ref=_window(ins[a], axes[a], sizes[a], owner), dst_ref=lands[a].at[j],
                    send_sem=send_sems.at[base + a, j], recv_sem=recv_sems.at[base + a, j], device_id=sibling,
                    device_id_type=MESH))
        return out

    shapes = [jax.ShapeDtypeStruct((N_CHIP,) + _shard_shape(g, ax), g.dtype) for g, ax in zip(grads, axes)]
    return _Comm(grads, shapes, (n, N_CHIP), copies)


def _chips_comm(sums):
    n = len(sums)

    def copies(ins, lands, send_sems, recv_sems, base=0):
        out = []
        for j in range(1, N_CHIP):
            dev, _ = _peer(2 * j)
            for a in range(n):
                out.append(pltpu.make_async_remote_copy(
                    src_ref=ins[a].at[j - 1], dst_ref=lands[a].at[j - 1],
                    send_sem=send_sems.at[base + a, j - 1], recv_sem=recv_sems.at[base + a, j - 1], device_id=dev,
                    device_id_type=MESH))
        return out

    return _Comm(sums, [jax.ShapeDtypeStruct(s.shape, s.dtype) for s in sums], (n, N_CHIP), copies)


def _join(c1, c2):
    n1, m1, k1 = len(c1.ins), len(c1.out_shapes), c1.sem_shape[0]

    def copies(ins, lands, send_sems, recv_sems):
        return (c1.copies(ins[:n1], lands[:m1], send_sems, recv_sems, base=0)
                + c2.copies(ins[n1:], lands[m1:], send_sems, recv_sems, base=k1))

    aliases = {**c1.aliases, **{n1 + k: m1 + v for k, v in c2.aliases.items()}}
    return _Comm(c1.ins + c2.ins, c1.out_shapes + c2.out_shapes, (k1 + c2.sem_shape[0], N_CHIP), copies, aliases)


def _chips_first_hop(sums):
    n = len(sums)

    def copies(ins, outs, send_sems, recv_sems, base=0):
        lands, relays = outs[:n], outs[n:]
        (y_dev, _), (x_dev, _) = _peer(2), _peer(4)
        out = []
        for a in range(n):
            half = sums[a].shape[1] // 2
            for k, (src, dst, dev) in enumerate((
                    (ins[a].at[0], lands[a].at[0], y_dev), (ins[a].at[1], lands[a].at[1], x_dev),
                    (ins[a].at[2, pl.ds(0, half)], relays[a].at[0], x_dev),
                    (ins[a].at[2, pl.ds(half, half)], relays[a].at[1], y_dev))):
                out.append(pltpu.make_async_remote_copy(
                    src_ref=src, dst_ref=dst, send_sem=send_sems.at[base + a, k], recv_sem=recv_sems.at[base + a, k],
                    device_id=dev, device_id_type=MESH))
        return out

    shapes = ([jax.ShapeDtypeStruct(s.shape, s.dtype) for s in sums]
              + [jax.ShapeDtypeStruct((2, s.shape[1] // 2) + s.shape[2:], s.dtype) for s in sums])
    return _Comm(sums, shapes, (n, N_CHIP), copies)


def _chips_relay(relays, lands):
    n = len(relays)

    def copies(ins, outs, send_sems, recv_sems, base=0):
        (y_dev, _), (x_dev, _) = _peer(2), _peer(4)
        out = []
        for a in range(n):
            half = relays[a].shape[1]
            for k, (rows, dev) in enumerate(((pl.ds(0, half), y_dev), (pl.ds(half, half), x_dev))):
                out.append(pltpu.make_async_remote_copy(
                    src_ref=ins[a].at[k], dst_ref=outs[a].at[2, rows], send_sem=send_sems.at[base + a, k],
                    recv_sem=recv_sems.at[base + a, k], device_id=dev, device_id_type=MESH))
        return out

    return _Comm(list(relays) + list(lands), [jax.ShapeDtypeStruct(l.shape, l.dtype) for l in lands], (n, N_CHIP), copies,
                 aliases={n + a: a for a in range(n)})


def _rs_chip_sum(grad, land, xyc, axis, name):
    R, C = land.shape[1:]
    tr = _pick(R, max(8, (640 * 1024) // C), 8)

    def body(xyc_ref, g_ref, l_ref, o_ref):
        o_ref[0] = (g_ref[...] + l_ref[0]).astype(bf16)

    def owner(j, xyc_ref):
        jj = j + 1
        return 4 * (xyc_ref[0] ^ (jj >> 1)) + 2 * (xyc_ref[1] ^ (jj & 1)) + xyc_ref[2]

    if axis == 0:
        g_spec = pl.BlockSpec((tr, C), lambda j, i, xyc_ref: (owner(j, xyc_ref) * (R // tr) + i, 0))
    else:
        g_spec = pl.BlockSpec((tr, C), lambda j, i, xyc_ref: (i, owner(j, xyc_ref)))
    return pl.pallas_call(
        body,
        grid_spec=pltpu.PrefetchScalarGridSpec(
            num_scalar_prefetch=1, grid=(N_CHIP - 1, R // tr),
            in_specs=[g_spec, pl.BlockSpec((1, tr, C), lambda j, i, xyc_ref: (j + 1, i, 0))],
            out_specs=pl.BlockSpec((1, tr, C), lambda j, i, xyc_ref: (j, i, 0))),
        out_shape=jax.ShapeDtypeStruct((N_CHIP - 1, R, C), bf16),
        compiler_params=_params(("parallel", "parallel")), name=name,
    )(xyc, grad, land)


def _allreduce_small(part):
    R = part.shape[0]

    def body(p_ref, o_ref, buf, send_sems, recv_sems):
        (x, y, c), me = _me()
        buf[me] = p_ref[...]
        copies = []
        for k in range(1, N_DEV):
            dev, dev_id = _peer(k)
            copies.append(pltpu.make_async_remote_copy(
                src_ref=p_ref, dst_ref=buf.at[me], send_sem=send_sems.at[k - 1], recv_sem=recv_sems.at[k - 1],
                device_id=dev, device_id_type=MESH))
        for cp in copies:
            cp.start()
        for k in range(1, N_DEV):
            _, dev_id = _peer(k)
            pltpu.make_async_remote_copy(
                src_ref=p_ref, dst_ref=buf.at[dev_id], send_sem=send_sems.at[k - 1], recv_sem=recv_sems.at[k - 1],
                device_id=(x, y, c), device_id_type=MESH).wait_recv()
        for cp in copies:
            cp.wait_send()
        acc = buf[0]
        for d in range(1, N_DEV):
            acc = acc + buf[d]
        o_ref[...] = acc

    return pl.pallas_call(
        body, in_specs=[pl.BlockSpec(memory_space=pltpu.VMEM)], out_specs=pl.BlockSpec(memory_space=pltpu.VMEM),
        out_shape=jax.ShapeDtypeStruct((R, LANES), f32),
        scratch_shapes=[pltpu.VMEM((N_DEV, R, LANES), f32), pltpu.SemaphoreType.DMA((N_DEV - 1,)), pltpu.SemaphoreType.DMA((N_DEV - 1,))],
        name="allreduce_small",
    )(part)


def _adamw_math(w, g, m, v):
    m2 = ADAM_B1 * m + (1.0 - ADAM_B1) * g
    v2 = ADAM_B2 * v + (1.0 - ADAM_B2) * (g * g)
    m_hat = m2 / (1.0 - ADAM_B1 ** ADAM_STEP)
    v_hat = v2 / (1.0 - ADAM_B2 ** ADAM_STEP)
    return -ADAM_LR * (m_hat / (jnp.sqrt(v_hat) + ADAM_EPS) + ADAM_WD * w), m2, v2


def _adamw_shard(grad, land_sib, land_chips, w, m, v, me, axis, name, row0=0, prev=None):
    R, C = land_sib.shape[1:]
    assert axis == 1 or R == w.shape[0]
    tr = _pick(R, max(8, (320 * 1024) // C), 8)
    r0 = row0 // tr
    n_prev = len(prev) if prev else 0

    def body(me_ref, g_ref, s_ref, l_ref, w_ref, m_ref, v_ref, *rest):
        go_ref, d_ref, mo_ref, vo_ref = rest[n_prev:]
        g = g_ref[...] + s_ref[0]
        for j in range(N_CHIP - 1):
            g = g + l_ref[j].astype(f32)
        d, m2, v2 = _adamw_math(w_ref[...], g, m_ref[...], v_ref[...])
        go_ref[...] = g
        d_ref[...] = d
        mo_ref[...] = m2
        vo_ref[...] = v2

    if axis == 0:
        g_spec = pl.BlockSpec((tr, C), lambda i, me_ref: (me_ref[0] * (R // tr) + i, 0))
    else:
        g_spec = pl.BlockSpec((tr, C), lambda i, me_ref: (i, me_ref[0]))
    blk = pl.BlockSpec((tr, C), lambda i, me_ref: (r0 + i, 0))
    return pl.pallas_call(
        body,
        grid_spec=pltpu.PrefetchScalarGridSpec(
            num_scalar_prefetch=1, grid=(R // tr,),
            in_specs=[g_spec, pl.BlockSpec((1, tr, C), lambda i, me_ref: (0, i, 0)),
                      pl.BlockSpec((N_CHIP - 1, tr, C), lambda i, me_ref: (0, i, 0)), blk, blk, blk]
            + [pl.BlockSpec(memory_space=pl.ANY)] * n_prev,
            out_specs=[blk] * 4),
        out_shape=[jax.ShapeDtypeStruct(w.shape, f32)] * 4,
        input_output_aliases={7 + i: i for i in range(n_prev)},
        compiler_params=_params(("parallel",)), name=name,
    )(me, grad, land_sib, land_chips, w, m, v, *(prev or []))


def _adamw_small(g, w, m, v):
    def body(g_ref, w_ref, m_ref, v_ref, d_ref, mo_ref, vo_ref):
        d, m2, v2 = _adamw_math(w_ref[...], g_ref[...], m_ref[...], v_ref[...])
        d_ref[...] = d
        mo_ref[...] = m2
        vo_ref[...] = v2

    return pl.pallas_call(body, out_shape=[jax.ShapeDtypeStruct(g.shape, f32)] * 3, name="adamw_small")(g, w, m, v)


def _pack_rows(parts, total_rows):
    rows = jnp.concatenate([p.reshape(-1, LANES) for p in parts], axis=0)
    return jnp.pad(rows, ((0, total_rows - rows.shape[0]), (0, 0)))


BIG = ("w_in", "mem_w_kv", "w_br_attn", "w_br_conv", "w_br_mem", "w_out")
BIG_AXIS = {"w_in": 1, "mem_w_kv": 0, "w_br_attn": 1, "w_br_conv": 1, "w_br_mem": 1, "w_out": 0}
SMALL = ("norm_g", "mem_norm_g", "attn_q_norm", "attn_k_norm", "mem_q_norm", "mem_k_norm")
ALL_W = ("norm_g", "mem_norm_g", "w_in", "attn_q_norm", "attn_k_norm", "conv_w", "mem_w_kv", "mem_q_norm", "mem_k_norm",
         "w_br_attn", "w_br_conv", "w_br_mem", "w_out")


def kernel(x, mem, norm_g, mem_norm_g, w_in, attn_q_norm, attn_k_norm, conv_w, mem_w_kv, mem_q_norm, mem_k_norm, w_br_attn, w_br_conv, w_br_mem, w_out, loss_target, m_norm_g, m_mem_norm_g, m_w_in, m_attn_q_norm, m_attn_k_norm, m_conv_w, m_mem_w_kv, m_mem_q_norm, m_mem_k_norm, m_w_br_attn, m_w_br_conv, m_w_br_mem, m_w_out, v_norm_g, v_mem_norm_g, v_w_in, v_attn_q_norm, v_attn_k_norm, v_conv_w, v_mem_w_kv, v_mem_q_norm, v_mem_k_norm, v_w_br_attn, v_w_br_conv, v_w_br_mem, v_w_out):
    w = dict(norm_g=norm_g, mem_norm_g=mem_norm_g, w_in=w_in, attn_q_norm=attn_q_norm, attn_k_norm=attn_k_norm, conv_w=conv_w,
             mem_w_kv=mem_w_kv, mem_q_norm=mem_q_norm, mem_k_norm=mem_k_norm, w_br_attn=w_br_attn, w_br_conv=w_br_conv,
             w_br_mem=w_br_mem, w_out=w_out)
    mo = dict(norm_g=m_norm_g, mem_norm_g=m_mem_norm_g, w_in=m_w_in, attn_q_norm=m_attn_q_norm, attn_k_norm=m_attn_k_norm,
              conv_w=m_conv_w, mem_w_kv=m_mem_w_kv, mem_q_norm=m_mem_q_norm, mem_k_norm=m_mem_k_norm, w_br_attn=m_w_br_attn,
              w_br_conv=m_w_br_conv, w_br_mem=m_w_br_mem, w_out=m_w_out)
    vo = dict(norm_g=v_norm_g, mem_norm_g=v_mem_norm_g, w_in=v_w_in, attn_q_norm=v_attn_q_norm, attn_k_norm=v_attn_k_norm,
              conv_w=v_conv_w, mem_w_kv=v_mem_w_kv, mem_q_norm=v_mem_q_norm, mem_k_norm=v_mem_k_norm, w_br_attn=v_w_br_attn,
              w_br_conv=v_w_br_conv, w_br_mem=v_w_br_mem, w_out=v_w_out)
    B, S, D = x.shape
    me = (4 * lax.axis_index("x") + 2 * lax.axis_index("y") + lax.axis_index("c")).astype(jnp.int32)

    T = B * S
    x2, t2, mem2, ng = x.reshape(T, D), loss_target.reshape(T, D), mem.reshape(B * MEM_LEN, D), norm_g.reshape(1, D)
    h = _rms_fwd(x2, ng, "rms_x")
    rows_sharded, cols_sharded = ("mem_w_kv", "w_out"), ("w_br_attn", "w_br_conv", "w_br_mem")
    later = rows_sharded + cols_sharded
    later_axes = [BIG_AXIS[n] for n in later] + [None]
    conv_pad = jnp.pad(conv_w, ((0, 8 - conv_w.shape[0]), (0, 0)))
    proj, w_in_full, spread_a = _proj_gather(
        h, w_in.astype(bf16), _shard_order(),
        _spread_comm([w[n].astype(bf16) for n in rows_sharded], [BIG_AXIS[n] for n in rows_sharded]), comm_at=N_DEV - 3)
    *attn, spread_b = _attn_fwd(proj, attn_q_norm, attn_k_norm, B, S,
                                comm=_spread_comm([w[n].astype(bf16) for n in cols_sharded] + [conv_pad],
                                                  [BIG_AXIS[n] for n in cols_sharded] + [None]))
    *fulls, conv_g = _forward_blocks(spread_a + spread_b, later_axes)
    wf = dict(zip(later, fulls), w_in=w_in_full)
    conv_full = conv_g[:, :3, :].transpose(1, 0, 2).reshape(3, CONV_W)

    sq_err, g, t = _fwd_bwd_head(x2, mem2, t2, proj, attn, B, S, mem_norm_g, attn_q_norm, attn_k_norm, conv_full, mem_q_norm,
                                 mem_k_norm, wf["mem_w_kv"], wf["w_br_attn"], wf["w_br_conv"], wf["w_br_mem"], wf["w_out"])
    t.update(x2=x2, ng=ng, h=h)

    xyc = jnp.stack([lax.axis_index("x"), lax.axis_index("y"), lax.axis_index("c")]).astype(jnp.int32)
    me1 = me.reshape(1)
    early = ("w_out", "w_br_attn", "w_br_conv", "w_br_mem")
    g["mem_w_kv"], sib_early = _mm(t["mh"], t["dmkv"], mode="tn", out_dtype=f32, name="dw_kv",
                                   comm=_sibling_comm([g[n] for n in early], [BIG_AXIS[n] for n in early]))
    sums_early = [_rs_chip_sum(g[n], ls, xyc, BIG_AXIS[n], "rs_sum_" + n) for n, ls in zip(early, sib_early)]
    tm_w = _pick(D // 2, 1024)
    half = (D // 2) // tm_w
    g_top, (*chips_early, sib_kv) = _mm(t["h"], t["dproj"], mode="tn", out_dtype=f32, name="dw_in_top", tm=tm_w,
                                        m_tiles=(0, half),
                                        comm=_join(_chips_comm(sums_early), _sibling_comm([g["mem_w_kv"]], [0])))
    sum_kv = _rs_chip_sum(g["mem_w_kv"], sib_kv, xyc, 0, "rs_sum_mem_w_kv")
    g_bot, (chips_kv, sib_top) = _mm(t["h"], t["dproj"], mode="tn", out_dtype=f32, name="dw_in_bot", tm=tm_w,
                                     m_tiles=(half, half), comm=_join(_chips_comm([sum_kv]), _sibling_comm([g_top], [1])))
    sum_top = _rs_chip_sum(g_top, sib_top, xyc, 1, "rs_sum_w_in_top")
    tm_t = _pick(T // 2, 1024)
    halfm = (T // 2) // tm_t
    dh, (part_top, relay_top, sib_bot) = _mm(t["dproj"], wf["w_in"], mode="nt", out_dtype=f32, name="d_h_a", tm=tm_t, tk=D_H_TK,
                                             m_tiles=(0, halfm), into="new",
                                             comm=_join(_chips_first_hop([sum_top]), _sibling_comm([g_bot], [1])))
    sum_bot = _rs_chip_sum(g_bot, sib_bot, xyc, 1, "rs_sum_w_in_bot")
    dh, (part_bot, relay_bot, chips_top) = _mm(t["dproj"], wf["w_in"], mode="nt", out_dtype=f32, name="d_h_b", tm=tm_t,
                                               tk=D_H_TK, m_tiles=(halfm, halfm), into=dh,
                                               comm=_join(_chips_first_hop([sum_bot]), _chips_relay([relay_top], [part_top])))
    dx, dng, (chips_bot,) = _rms_bwd(t["x2"], t["ng"], dh, t["dy"], "rms_x_bwd", comm=_chips_relay([relay_bot], [part_bot]))
    g["norm_g"] = dng.reshape(D)

    grad, delta, new_m, new_v = {}, {}, {}, {}
    for n, ls, lc in zip(early + ("mem_w_kv",), sib_early + [sib_kv], chips_early + [chips_kv]):
        grad[n], delta[n], new_m[n], new_v[n] = _adamw_shard(g[n], ls, lc, w[n], mo[n], vo[n], me1, BIG_AXIS[n], "adamw_" + n)
    top = _adamw_shard(g_top, sib_top, chips_top, w_in, m_w_in, v_w_in, me1, 1, "adamw_w_in_top")
    grad["w_in"], delta["w_in"], new_m["w_in"], new_v["w_in"] = _adamw_shard(
        g_bot, sib_bot, chips_bot, w_in, m_w_in, v_w_in, me1, 1, "adamw_w_in_bot", row0=D // 2, prev=top)

    n_rep = sum(w[n].size for n in SMALL) // LANES
    conv_rows = g["conv_w"].reshape(3, N_DEV, LANES).transpose(1, 0, 2).reshape(3 * N_DEV, LANES)
    n_rows = -(-(n_rep + 3 * N_DEV + 1) // 8) * 8
    part = _pack_rows([g[n] for n in SMALL] + [conv_rows, jnp.full((1, LANES), sq_err, f32)], n_rows)
    tot = _allreduce_small(part)
    loss = tot[n_rep + 3 * N_DEV, 0] * (0.5 / D)
    n_small = -(-(n_rep + 3) // 8) * 8
    g_small = _pack_rows([tot[:n_rep], lax.dynamic_slice(tot, (n_rep + 3 * me, 0), (3, LANES))], n_small)
    names = SMALL + ("conv_w",)
    d_s, m_s, v_s = _adamw_small(g_small, _pack_rows([w[n] for n in names], n_small), _pack_rows([mo[n] for n in names], n_small),
                                 _pack_rows([vo[n] for n in names], n_small))
    off = 0
    for n in names:
        r = w[n].size // LANES
        grad[n], delta[n] = g_small[off:off + r].reshape(w[n].shape), d_s[off:off + r].reshape(w[n].shape)
        new_m[n], new_v[n] = m_s[off:off + r].reshape(w[n].shape), v_s[off:off + r].reshape(w[n].shape)
        off += r
    return (loss, dx.reshape(B, S, D), *[grad[n] for n in ALL_W], *[delta[n] for n in ALL_W], *[new_m[n] for n in ALL_W],
            *[new_v[n] for n in ALL_W])
```

```python
import functools

import jax
import jax.numpy as jnp
from jax import lax
from jax.experimental import pallas as pl
from jax.experimental.pallas import tpu as pltpu

f32 = jnp.float32
bf16 = jnp.bfloat16

N_DEV = 8
HEAD_DIM = 128
DILATIONS = (1, 4, 16)
N_GROUPS = 3
HEADS = 4
BLK = 128
ATTN_QKV = N_GROUPS * HEADS * HEAD_DIM
ATTN_OUT = HEADS * HEAD_DIM
CONV_W = 1024
MEM_LEN = 256
MEM_HEADS = 4
MEM_HD = 256
MEM_W = MEM_HEADS * MEM_HD
EPS = 1e-6
Q0, K0, V0 = 0, ATTN_QKV, 2 * ATTN_QKV
ZA = 3 * ATTN_QKV
CB, CC, CV, ZC = ZA + ATTN_OUT, ZA + ATTN_OUT + CONV_W, ZA + ATTN_OUT + 2 * CONV_W, ZA + ATTN_OUT + 3 * CONV_W
MQ = ZC + CONV_W
ZM = MQ + MEM_W
GT = ZM + MEM_W

ADAM_LR, ADAM_B1, ADAM_B2, ADAM_EPS, ADAM_WD, ADAM_STEP = 0.001, 0.9, 0.999, 1e-08, 0.01, 10

VMEM_LIMIT = 56 * 1024 * 1024
D_H_TK = 4352
LANES = 128

NT_DIMS = (((1,), (1,)), ((), ()))
TN_DIMS = (((0,), (0,)), ((), ()))
NN_DIMS = (((1,), (0,)), ((), ()))


def _params(sem=None):
    return pltpu.CompilerParams(dimension_semantics=sem, vmem_limit_bytes=VMEM_LIMIT)


def _pick(n, pref, q=LANES):
    t = (min(pref, n) // q) * q
    while t >= q:
        if n % t == 0:
            return t
        t -= q
    return n


def _dot(a, b, dims):
    return lax.dot_general(a.astype(bf16), b.astype(bf16), dims, preferred_element_type=f32)


def _sigmoid(z):
    return 0.5 * jnp.tanh(0.5 * z) + 0.5


def _rstd(v):
    return lax.rsqrt(jnp.mean(v * v, axis=-1, keepdims=True) + EPS)


class _Deferred:
    def __init__(self, stage, sems, step, last, n):
        assert last >= 1
        self.stage, self.sems, self.step, self.last, self.n = stage, sems, step, last, n
        self.slot = step % 2

    def _drain(self, slot, like):
        for c in range(self.n):
            pltpu.make_async_copy(self.stage.at[slot, c], like(c), self.sems.at[slot, c]).wait()

    def begin(self, like):
        pl.when(self.step >= 2)(lambda: self._drain(self.slot, like))

    def start(self, c, dst):
        pltpu.make_async_copy(self.stage.at[self.slot, c], dst, self.sems.at[self.slot, c]).start()

    def end(self, like):
        @pl.when(self.step == self.last)
        def _():
            self._drain(self.slot, like)
            self._drain(1 - self.slot, like)


def _row_sum(v):
    return _dot(v, jnp.ones((v.shape[1], v.shape[1]), bf16), NN_DIMS)


def _rstd_head(v):
    return lax.rsqrt(_row_sum(v * v) * (1.0 / v.shape[1]) + EPS)


class _Comm:
    def __init__(self, ins, out_shapes, sem_shape, copies, aliases=None):
        self.ins, self.out_shapes, self.sem_shape, self.copies = list(ins), list(out_shapes), sem_shape, copies
        self.aliases = dict(aliases or {})

    def scratch(self):
        return [pltpu.SemaphoreType.DMA(self.sem_shape), pltpu.SemaphoreType.DMA(self.sem_shape)]

    def io_aliases(self, first_in, first_out):
        return {first_in + k: first_out + v for k, v in self.aliases.items()}


def _mm(a, b, *, mode, out_dtype, name, tm=1024, tn=1024, tk=4096, m_tiles=None, into=None, comm=None):
    if mode == "nn":
        (M, K), (K2, N) = a.shape, b.shape
    elif mode == "nt":
        (M, K), (N, K2) = a.shape, b.shape
    else:
        (K, M), (K2, N) = a.shape, b.shape
    assert K == K2
    tm, tn, tk = _pick(M, tm), _pick(N, tn), _pick(K, tk)
    nk = K // tk
    m0, mc = m_tiles if m_tiles is not None else (0, M // tm)
    o0 = 0 if into is None else m0
    aliased = into is not None and not isinstance(into, str)
    n_ci = len(comm.ins) if comm else 0
    n_co = len(comm.out_shapes) if comm else 0
    grid = (mc, N // tn, nk)
    dims = {"nn": NN_DIMS, "nt": NT_DIMS, "tn": TN_DIMS}[mode]

    def body(*refs, nk):
        a_ref, b_ref = refs[:2]
        pos = 3 if aliased else 2
        c_ins, o_ref, c_outs = refs[pos:pos + n_ci], refs[pos + n_ci], refs[pos + n_ci + 1:pos + n_ci + 1 + n_co]
        scratch = refs[pos + n_ci + 1 + n_co:]
        ids = [pl.program_id(d) for d in range(3)]
        if comm:
            sems = scratch[-2:]

            @pl.when((ids[0] == 0) & (ids[1] == 0) & (ids[2] == 0))
            def _():
                for cp in comm.copies(c_ins, c_outs, *sems):
                    cp.start()

        if nk == 1:
            o_ref[...] = _dot(a_ref[...], b_ref[...], dims).astype(o_ref.dtype)
        else:
            acc_ref, k = scratch[0], ids[2]

            @pl.when(k == 0)
            def _():
                acc_ref[...] = _dot(a_ref[...], b_ref[...], dims)

            @pl.when((k > 0) & (k < nk - 1))
            def _():
                acc_ref[...] += _dot(a_ref[...], b_ref[...], dims)

            @pl.when(k == nk - 1)
            def _():
                o_ref[...] = (acc_ref[...] + _dot(a_ref[...], b_ref[...], dims)).astype(o_ref.dtype)

        if comm:
            @pl.when((ids[0] == grid[0] - 1) & (ids[1] == grid[1] - 1) & (ids[2] == grid[2] - 1))
            def _():
                for cp in comm.copies(c_ins, c_outs, *sems):
                    cp.wait()

    if mode == "tn":
        a_spec = pl.BlockSpec((tk, tm), lambda i, j, k: (k, m0 + i))
    else:
        a_spec = pl.BlockSpec((tm, tk), lambda i, j, k: (m0 + i, k))
    if mode == "nt":
        b_spec = pl.BlockSpec((tn, tk), lambda i, j, k: (j, k))
    else:
        b_spec = pl.BlockSpec((tk, tn), lambda i, j, k: (k, j))
    hbm = pl.BlockSpec(memory_space=pl.ANY)
    res = pl.pallas_call(
        functools.partial(body, nk=nk),
        grid=grid,
        in_specs=[a_spec, b_spec] + [hbm] * (aliased + n_ci),
        out_specs=[pl.BlockSpec((tm, tn), lambda i, j, k: (o0 + i, j))] + [hbm] * n_co,
        out_shape=[jax.ShapeDtypeStruct((mc * tm if into is None else M, N), out_dtype)] + (comm.out_shapes if comm else []),
        scratch_shapes=([pltpu.VMEM((tm, tn), f32)] if nk > 1 else []) + (comm.scratch() if comm else []),
        input_output_aliases={**({2: 0} if aliased else {}), **(comm.io_aliases(2 + aliased, 1) if comm else {})},
        compiler_params=_params(("arbitrary",) * 3 if comm else ("parallel", "parallel", "arbitrary")),
        name=name,
    )(a, b, *([into] if aliased else []), *(comm.ins if comm else []))
    return (res[0], list(res[1:])) if comm else res[0]


def _rms_fwd(x, g, name):
    T, D = x.shape
    tm = _pick(T, 256, 8)

    def body(x_ref, g_ref, h_ref):
        xv = x_ref[...]
        h_ref[...] = (xv * _rstd(xv) * g_ref[...]).astype(bf16)

    return pl.pallas_call(
        body, grid=(T // tm,),
        in_specs=[pl.BlockSpec((tm, D), lambda i: (i, 0)), pl.BlockSpec((1, D), lambda i: (0, 0))],
        out_specs=pl.BlockSpec((tm, D), lambda i: (i, 0)),
        out_shape=jax.ShapeDtypeStruct((T, D), bf16),
        compiler_params=_params(("parallel",)), name=name,
    )(x, g)


def _rms_bwd(x, g, dh, dy, name, comm=None):
    T, D = x.shape
    tm = _pick(T, 256, 8)
    with_dx = dy is not None
    n_ci = len(comm.ins) if comm else 0
    n_co = len(comm.out_shapes) if comm else 0

    def body(*refs):
        if with_dx:
            x_ref, g_ref, dh_ref, dy_ref = refs[:4]
            c_ins, (dx_ref, dg_ref) = refs[4:4 + n_ci], refs[4 + n_ci:6 + n_ci]
            c_outs, sems = refs[6 + n_ci:6 + n_ci + n_co], refs[6 + n_ci + n_co:]
        else:
            x_ref, g_ref, dh_ref, dg_ref = refs
        if comm:
            @pl.when(pl.program_id(0) == 0)
            def _():
                for cp in comm.copies(c_ins, c_outs, *sems):
                    cp.start()

        xv = x_ref[...]
        r = _rstd(xv)
        xh = xv * r
        dhv = dh_ref[...]
        part = jnp.sum(dhv * xh, axis=0, keepdims=True)

        @pl.when(pl.program_id(0) == 0)
        def _():
            dg_ref[...] = part

        @pl.when(pl.program_id(0) > 0)
        def _():
            dg_ref[...] += part

        if with_dx:
            dxh = dhv * g_ref[...]
            dx_ref[...] = dy_ref[...] + r * (dxh - xh * jnp.mean(dxh * xh, axis=-1, keepdims=True))

        if comm:
            @pl.when(pl.program_id(0) == T // tm - 1)
            def _():
                for cp in comm.copies(c_ins, c_outs, *sems):
                    cp.wait()

    row = pl.BlockSpec((tm, D), lambda i: (i, 0))
    vec = pl.BlockSpec((1, D), lambda i: (0, 0))
    hbm = pl.BlockSpec(memory_space=pl.ANY)
    if with_dx:
        res = pl.pallas_call(
            body, grid=(T // tm,), in_specs=[row, vec, row, row] + [hbm] * n_ci, out_specs=[row, vec] + [hbm] * n_co,
            out_shape=[jax.ShapeDtypeStruct((T, D), f32), jax.ShapeDtypeStruct((1, D), f32)] + (comm.out_shapes if comm else []),
            scratch_shapes=comm.scratch() if comm else [],
            input_output_aliases=comm.io_aliases(4, 2) if comm else {},
            compiler_params=_params(("arbitrary",)), name=name,
        )(x, g, dh, dy, *(comm.ins if comm else []))
        return (res[0], res[1], list(res[2:])) if comm else res
    return pl.pallas_call(
        body, grid=(T // tm,), in_specs=[row, vec, row], out_specs=vec,
        out_shape=jax.ShapeDtypeStruct((1, D), f32),
        compiler_params=_params(("arbitrary",)), name=name,
    )(x, g, dh)


MAX_UNIT_UNROLL = 6


def _unit_unroll(trips):
    return max(u for u in range(1, MAX_UNIT_UNROLL + 1) if trips % u == 0)


def _rows(start, size, stride):
    return pl.ds(start, size) if stride == 1 else pl.ds(start, size, stride=stride)


def _attn_units(S, d, first, prev):
    nb = S // d // BLK

    def do_first(r, c):
        first(_rows(r, BLK, d))
        return c

    lax.fori_loop(0, d, do_first, 0, unroll=_unit_unroll(d))
    if nb > 1:
        def do_prev(u, c):
            r = u // (nb - 1)
            b = u % (nb - 1) + 1
            base = r + d * b * BLK
            prev(_rows(base, BLK, d), _rows(base - d * BLK, 2 * BLK, d))
            return c

        lax.fori_loop(0, d * (nb - 1), do_prev, 0, unroll=_unit_unroll(d * (nb - 1)))


def _band_bias(nkeys):
    i = lax.broadcasted_iota(jnp.int32, (BLK, nkeys), 0)
    j = lax.broadcasted_iota(jnp.int32, (BLK, nkeys), 1)
    ok = (j <= i) if nkeys == BLK else ((j >= i) & (j <= i + BLK))
    return jnp.where(ok, 0.0, -jnp.inf).astype(f32)


def _normalize_into(src_ref, gain, dst_ref, S):
    for c in range(S // 256):
        rows = pl.ds(c * 256, 256)
        v = src_ref[rows, :]
        dst_ref[rows, :] = v * _rstd_head(v) * gain


def _attn_fwd(proj, gq, gk, B, S, comm=None):
    T, NC = proj.shape
    scale = HEAD_DIM ** -0.5
    n_ci = len(comm.ins) if comm else 0
    n_co = len(comm.out_shapes) if comm else 0

    def body(*refs):
        q_ref, k_ref, v_ref, z_ref, gq_ref, gk_ref = refs[:6]
        c_ins = refs[6:6 + n_ci]
        ag_ref, a_ref, lse_ref = refs[6 + n_ci:9 + n_ci]
        c_outs = refs[9 + n_ci:9 + n_ci + n_co]
        qs, ks, o0, o1, o2, l0, l1, l2, bias1, bias2 = refs[9 + n_ci + n_co:19 + n_ci + n_co]
        sems = refs[19 + n_ci + n_co:]
        g = pl.program_id(2)
        if comm:
            @pl.when((pl.program_id(0) == 0) & (pl.program_id(1) == 0) & (g == 0))
            def _():
                for cp in comm.copies(c_ins, c_outs, *sems):
                    cp.start()

        bias1[...] = _band_bias(BLK)
        bias2[...] = _band_bias(2 * BLK)
        _normalize_into(q_ref, gq_ref[pl.ds(g, 1), :], qs, S)
        _normalize_into(k_ref, gk_ref[pl.ds(g, 1), :], ks, S)
        o_s, l_s = (o0, o1, o2), (l0, l1, l2)

        def run(gi):
            def unit(qr, kr, nkeys):
                s = _dot(qs[qr, :], ks[kr, :], NT_DIMS) * scale + (bias1 if nkeys == BLK else bias2)[...]
                m = jnp.max(s, axis=-1, keepdims=True)
                p = jnp.exp(s - m)
                den = jnp.sum(p, axis=-1, keepdims=True)
                o_s[gi][qr, :] = _dot(p, v_ref[kr, :], NN_DIMS) * (1.0 / den)
                l_s[gi][qr, :] = jnp.broadcast_to(m + jnp.log(den), (BLK, HEAD_DIM))

            _attn_units(S, DILATIONS[gi], lambda qr: unit(qr, qr, BLK), lambda qr, kr: unit(qr, kr, 2 * BLK))

        for gi in range(N_GROUPS):
            pl.when(g == gi)(functools.partial(run, gi))

        @pl.when(g == N_GROUPS - 1)
        def _():
            for c in range(S // 256):
                rows = pl.ds(c * 256, 256)
                la, lb, lc = l0[rows, :], l1[rows, :], l2[rows, :]
                m = jnp.maximum(jnp.maximum(la, lb), lc)
                wa, wb, wc = jnp.exp(la - m), jnp.exp(lb - m), jnp.exp(lc - m)
                tot = wa + wb + wc
                a = (wa / tot) * o0[rows, :] + (wb / tot) * o1[rows, :] + (wc / tot) * o2[rows, :]
                z = z_ref[rows, :]
                a_ref[rows, :] = a
                lse_ref[rows, :] = m + jnp.log(tot)
                ag_ref[rows, :] = (a * (z * _sigmoid(z))).astype(bf16)

        if comm:
            @pl.when((pl.program_id(0) == B - 1) & (pl.program_id(1) == HEADS - 1) & (g == N_GROUPS - 1))
            def _():
                for cp in comm.copies(c_ins, c_outs, *sems):
                    cp.wait()

    def slab(col0):
        return pl.BlockSpec((S, HEAD_DIM), lambda b, h, g: (b, col0 // HEAD_DIM + g * HEADS + h))

    gain = pl.BlockSpec((N_GROUPS, HEAD_DIM), lambda b, h, g: (0, 0))
    out = pl.BlockSpec((S, HEAD_DIM), lambda b, h, g: (b, h))
    hbm = pl.BlockSpec(memory_space=pl.ANY)
    res = pl.pallas_call(
        body, grid=(B, HEADS, N_GROUPS),
        in_specs=[slab(Q0), slab(K0), slab(V0), pl.BlockSpec((S, HEAD_DIM), lambda b, h, g: (b, ZA // HEAD_DIM + h)), gain, gain]
        + [hbm] * n_ci,
        out_specs=[out, out, out] + [hbm] * n_co,
        out_shape=[jax.ShapeDtypeStruct((T, ATTN_OUT), bf16), jax.ShapeDtypeStruct((T, ATTN_OUT), f32),
                   jax.ShapeDtypeStruct((T, ATTN_OUT), f32)] + (comm.out_shapes if comm else []),
        scratch_shapes=[pltpu.VMEM((S, HEAD_DIM), f32)] * 8 + [pltpu.VMEM((BLK, BLK), f32), pltpu.VMEM((BLK, 2 * BLK), f32)]
        + (comm.scratch() if comm else []),
        compiler_params=_params(("arbitrary", "arbitrary", "arbitrary")), name="attn_fwd",
    )(proj, proj, proj, proj, gq, gk, *(comm.ins if comm else []))
    return res[0], res[1], res[2], list(res[3:])


def _rms_bwd_rows(raw, gain, dn, on_mxu=True):
    r = _rstd_head(raw) if on_mxu else _rstd(raw)
    xh = raw * r
    dxh = dn * gain
    if on_mxu:
        mean = _row_sum(dxh * xh) * (1.0 / raw.shape[1])
    else:
        mean = jnp.mean(dxh * xh, axis=-1, keepdims=True)
    return r * (dxh - xh * mean), jnp.sum(dn * xh, axis=0, keepdims=True)


def _attn_bwd(proj, gq, gk, a_comb, lse, dag, dproj, B, S):
    T, NC = proj.shape
    scale = HEAD_DIM ** -0.5

    def body(q_ref, k_ref, v_ref, z_ref, gq_ref, gk_ref, a_ref, lse_ref, dag_ref, dproj_in, dproj_ref, dgq_ref, dgk_ref,
             qs, ks, da_s, dl_s, dq_s, dk_s, dv_s, bias1, bias2, stage, dz_stage, sem, dz_sem):
        b, h, g = pl.program_id(0), pl.program_id(1), pl.program_id(2)
        bias1[...] = _band_bias(BLK)
        bias2[...] = _band_bias(2 * BLK)
        gq_row, gk_row = gq_ref[pl.ds(g, 1), :], gk_ref[pl.ds(g, 1), :]
        _normalize_into(q_ref, gq_row, qs, S)
        _normalize_into(k_ref, gk_row, ks, S)

        def dst(c):
            return dproj_ref.at[pl.ds(b * S, S), pl.ds((Q0, K0, V0)[c] + (g * HEADS + h) * HEAD_DIM, HEAD_DIM)]

        out = _Deferred(stage, sem, (b * HEADS + h) * N_GROUPS + g, B * HEADS * N_GROUPS - 1, 3)
        out.begin(dst)

        @pl.when((b == 0) & (h == 0) & (g == 0))
        def _():
            dgq_ref[...] = jnp.zeros_like(dgq_ref)
            dgk_ref[...] = jnp.zeros_like(dgk_ref)

        @pl.when(g == 0)
        def _():
            for c in range(S // 256):
                rows = pl.ds(c * 256, 256)
                z, a, dg_ = z_ref[rows, :], a_ref[rows, :], dag_ref[rows, :]
                sg = _sigmoid(z)
                da = dg_ * (z * sg)
                da_s[rows, :] = da
                dl_s[rows, :] = _row_sum(da * a)
                dz_stage[rows, :] = (dg_ * a * (sg * (1.0 + z * (1.0 - sg)))).astype(bf16)
            cp = pltpu.make_async_copy(dz_stage, dproj_ref.at[pl.ds(b * S, S), pl.ds(ZA + h * HEAD_DIM, HEAD_DIM)], dz_sem)
            cp.start()
            cp.wait()

        dk_s[...] = jnp.zeros_like(dk_s)
        dv_s[...] = jnp.zeros_like(dv_s)

        def run(gi):
            def unit(qr, kr, nkeys):
                q, k, v = qs[qr, :], ks[kr, :], v_ref[kr, :]
                s = _dot(q, k, NT_DIMS) * scale
                p = jnp.exp(s + (bias1 if nkeys == BLK else bias2)[...] - lse_ref[qr, :][:, :1])
                da = da_s[qr, :]
                dp = _dot(da, v, NT_DIMS)
                ds = p * (dp - dl_s[qr, :][:, :1]) * scale
                dq_s[qr, :] = _dot(ds, k, NN_DIMS)
                dk_s[kr, :] += _dot(ds, q, TN_DIMS)
                dv_s[kr, :] += _dot(p, da, TN_DIMS)

            _attn_units(S, DILATIONS[gi], lambda qr: unit(qr, qr, BLK), lambda qr, kr: unit(qr, kr, 2 * BLK))

        for gi in range(N_GROUPS):
            pl.when(g == gi)(functools.partial(run, gi))

        gq_acc = jnp.zeros((1, HEAD_DIM), f32)
        gk_acc = jnp.zeros((1, HEAD_DIM), f32)
        for c in range(S // 256):
            rows = pl.ds(c * 256, 256)
            dq, gq_p = _rms_bwd_rows(q_ref[rows, :], gq_row, dq_s[rows, :])
            dk, gk_p = _rms_bwd_rows(k_ref[rows, :], gk_row, dk_s[rows, :])
            gq_acc, gk_acc = gq_acc + gq_p, gk_acc + gk_p
            stage[out.slot, 0, rows, :] = dq.astype(bf16)
            stage[out.slot, 1, rows, :] = dk.astype(bf16)
            stage[out.slot, 2, rows, :] = dv_s[rows, :].astype(bf16)
        dgq_ref[pl.ds(g, 1), :] += gq_acc
        dgk_ref[pl.ds(g, 1), :] += gk_acc
        for c in range(3):
            out.start(c, dst(c))
        out.end(dst)

    def slab(col0):
        return pl.BlockSpec((S, HEAD_DIM), lambda b, h, g: (b, col0 // HEAD_DIM + g * HEADS + h))

    gain = pl.BlockSpec((N_GROUPS, HEAD_DIM), lambda b, h, g: (0, 0))
    per_slot = pl.BlockSpec((S, HEAD_DIM), lambda b, h, g: (b, h))
    return pl.pallas_call(
        body, grid=(B, HEADS, N_GROUPS),
        in_specs=[slab(Q0), slab(K0), slab(V0), pl.BlockSpec((S, HEAD_DIM), lambda b, h, g: (b, ZA // HEAD_DIM + h)), gain, gain,
                  per_slot, per_slot, per_slot, pl.BlockSpec(memory_space=pl.ANY)],
        out_specs=[pl.BlockSpec(memory_space=pl.ANY), gain, gain],
        out_shape=[jax.ShapeDtypeStruct(dproj.shape, dproj.dtype), jax.ShapeDtypeStruct((N_GROUPS, HEAD_DIM), f32),
                   jax.ShapeDtypeStruct((N_GROUPS, HEAD_DIM), f32)],
        scratch_shapes=[pltpu.VMEM((S, HEAD_DIM), f32)] * 7 + [pltpu.VMEM((BLK, BLK), f32), pltpu.VMEM((BLK, 2 * BLK), f32),
                                                                pltpu.VMEM((2, 3, S, HEAD_DIM), bf16), pltpu.VMEM((S, HEAD_DIM), bf16),
                                                                pltpu.SemaphoreType.DMA((2, 3)), pltpu.SemaphoreType.DMA],
        input_output_aliases={9: 0},
        compiler_params=_params(("arbitrary", "arbitrary", "arbitrary")), name="attn_bwd",
    )(proj, proj, proj, proj, gq, gk, a_comb, lse, dag, dproj)


def _conv_fwd(proj, conv_w, B, S):
    T, NC = proj.shape
    tc = LANES

    def body(cb_ref, cc_ref, cv_ref, z_ref, w_ref, c_ref, ub):
        u = cc_ref[...] * cv_ref[...]
        ub[0:8, :] = jnp.zeros((8, tc), f32)
        ub[8:8 + S, :] = u
        w = w_ref[...]
        y = w[0:1] * u + w[1:2] * ub[pl.ds(7, S), :] + w[2:3] * ub[pl.ds(6, S), :]
        z = z_ref[...]
        c_ref[...] = (cb_ref[...] * y * (z * _sigmoid(z))).astype(bf16)

    def seg(col0):
        return pl.BlockSpec((S, tc), lambda j, b: (b, col0 // tc + j))

    return pl.pallas_call(
        body, grid=(CONV_W // tc, B),
        in_specs=[seg(CB), seg(CC), seg(CV), seg(ZC), pl.BlockSpec((3, tc), lambda j, b: (0, j))],
        out_specs=pl.BlockSpec((S, tc), lambda j, b: (b, j)),
        out_shape=jax.ShapeDtypeStruct((T, CONV_W), bf16),
        scratch_shapes=[pltpu.VMEM((S + 8, tc), f32)],
        compiler_params=_params(("parallel", "parallel")), name="conv_fwd",
    )(proj, proj, proj, proj, conv_w)


def _conv_bwd(proj, conv_w, dc, dproj, B, S):
    T, NC = proj.shape
    tc = LANES

    def body(cb_ref, cc_ref, cv_ref, z_ref, w_ref, dc_ref, dproj_in, dproj_ref, dw_ref, ub, db, stage, sem):
        j, b = pl.program_id(0), pl.program_id(1)

        def dst(c):
            return dproj_ref.at[pl.ds(b * S, S), pl.ds((CB, CC, CV, ZC)[c] + j * tc, tc)]

        out = _Deferred(stage, sem, j * B + b, (CONV_W // tc) * B - 1, 4)
        out.begin(dst)
        cb, cc, cv, z, dcv = cb_ref[...], cc_ref[...], cv_ref[...], z_ref[...], dc_ref[...]
        u = cc * cv
        ub[0:8, :] = jnp.zeros((8, tc), f32)
        ub[8:8 + S, :] = u
        u1, u2 = ub[pl.ds(7, S), :], ub[pl.ds(6, S), :]
        w = w_ref[...]
        y = w[0:1] * u + w[1:2] * u1 + w[2:3] * u2
        sg = _sigmoid(z)
        si = z * sg
        dyv = dcv * cb * si
        db[0:S, :] = dyv
        db[S:S + 8, :] = jnp.zeros((8, tc), f32)
        du = w[0:1] * dyv + w[1:2] * db[pl.ds(1, S), :] + w[2:3] * db[pl.ds(2, S), :]
        stage[out.slot, 0] = (dcv * y * si).astype(bf16)
        stage[out.slot, 1] = (du * cv).astype(bf16)
        stage[out.slot, 2] = (du * cc).astype(bf16)
        stage[out.slot, 3] = (dcv * cb * y * (sg * (1.0 + z * (1.0 - sg)))).astype(bf16)
        for c in range(4):
            out.start(c, dst(c))
        part = jnp.concatenate([jnp.sum(dyv * u, axis=0, keepdims=True), jnp.sum(dyv * u1, axis=0, keepdims=True),
                                jnp.sum(dyv * u2, axis=0, keepdims=True)], axis=0)

        @pl.when(b == 0)
        def _():
            dw_ref[...] = part

        @pl.when(b > 0)
        def _():
            dw_ref[...] += part

        out.end(dst)

    def seg(col0):
        return pl.BlockSpec((S, tc), lambda j, b: (b, col0 // tc + j))

    return pl.pallas_call(
        body, grid=(CONV_W // tc, B),
        in_specs=[seg(CB), seg(CC), seg(CV), seg(ZC), pl.BlockSpec((3, tc), lambda j, b: (0, j)),
                  pl.BlockSpec((S, tc), lambda j, b: (b, j)), pl.BlockSpec(memory_space=pl.ANY)],
        out_specs=[pl.BlockSpec(memory_space=pl.ANY), pl.BlockSpec((3, tc), lambda j, b: (0, j))],
        out_shape=[jax.ShapeDtypeStruct(dproj.shape, dproj.dtype), jax.ShapeDtypeStruct((3, CONV_W), f32)],
        scratch_shapes=[pltpu.VMEM((S + 8, tc), f32), pltpu.VMEM((S + 8, tc), f32), pltpu.VMEM((2, 4, S, tc), bf16),
                        pltpu.SemaphoreType.DMA((2, 4))],
        input_output_aliases={6: 0},
        compiler_params=_params(("arbitrary", "arbitrary")), name="conv_bwd",
    )(proj, proj, proj, proj, conv_w, dc, dproj)


MEM_CHUNK = 1024


def _mem_fwd(proj, mkv, gq, gk, B, S):
    T, NC = proj.shape
    scale = MEM_HD ** -0.5

    def body(q_ref, z_ref, k_ref, v_ref, gq_ref, gk_ref, o_ref):
        kv = k_ref[...]
        kn = (kv * _rstd_head(kv) * gk_ref[...]).astype(bf16)
        vv = v_ref[...].astype(bf16)

        def chunk(c, carry):
            rows = pl.ds(pl.multiple_of(c * MEM_CHUNK, MEM_CHUNK), MEM_CHUNK)
            q = q_ref[rows, :]
            qn = q * _rstd(q) * gq_ref[...]
            s = _dot(qn, kn, NT_DIMS) * scale
            p = jnp.exp(s - jnp.max(s, axis=-1, keepdims=True))
            p = p / jnp.sum(p, axis=-1, keepdims=True)
            z = z_ref[rows, :]
            o_ref[rows, :] = (_dot(p, vv, NN_DIMS) * (z * _sigmoid(z))).astype(bf16)
            return carry

        lax.fori_loop(0, S // MEM_CHUNK, chunk, 0)

    gain = pl.BlockSpec((1, MEM_HD), lambda b, h: (0, 0))
    return pl.pallas_call(
        body, grid=(B, MEM_HEADS),
        in_specs=[pl.BlockSpec((S, MEM_HD), lambda b, h: (b, MQ // MEM_HD + h)),
                  pl.BlockSpec((S, MEM_HD), lambda b, h: (b, ZM // MEM_HD + h)),
                  pl.BlockSpec((MEM_LEN, MEM_HD), lambda b, h: (b, h)),
                  pl.BlockSpec((MEM_LEN, MEM_HD), lambda b, h: (b, MEM_HEADS + h)), gain, gain],
        out_specs=pl.BlockSpec((S, MEM_HD), lambda b, h: (b, h)),
        out_shape=jax.ShapeDtypeStruct((T, MEM_W), bf16),
        compiler_params=_params(("parallel", "parallel")), name="mem_fwd",
    )(proj, proj, mkv, mkv, gq, gk)


def _mem_bwd(proj, mkv, gq, gk, dmo, dproj, B, S):
    T, NC = proj.shape
    scale = MEM_HD ** -0.5

    def body(q_ref, z_ref, k_ref, v_ref, gq_ref, gk_ref, dmo_ref, dproj_in, dproj_ref, dmk_ref, dmv_ref, dgq_ref, dgk_ref,
             dkn_s, dv_s, gq_s, stage, sem):
        b, h = pl.program_id(0), pl.program_id(1)

        def dst(c):
            return dproj_ref.at[pl.ds(b * S, S), pl.ds((MQ, ZM)[c] + h * MEM_HD, MEM_HD)]

        out = _Deferred(stage, sem, b * MEM_HEADS + h, B * MEM_HEADS - 1, 2)
        out.begin(dst)
        kv = k_ref[...]
        kn = (kv * _rstd_head(kv) * gk_ref[...]).astype(bf16)
        vv = v_ref[...].astype(bf16)
        dkn_s[...] = jnp.zeros_like(dkn_s)
        dv_s[...] = jnp.zeros_like(dv_s)
        gq_s[...] = jnp.zeros_like(gq_s)

        def chunk(c, carry):
            rows = pl.ds(pl.multiple_of(c * MEM_CHUNK, MEM_CHUNK), MEM_CHUNK)
            q = q_ref[rows, :]
            qn = q * _rstd(q) * gq_ref[...]
            s = _dot(qn, kn, NT_DIMS) * scale
            p = jnp.exp(s - jnp.max(s, axis=-1, keepdims=True))
            p = p / jnp.sum(p, axis=-1, keepdims=True)
            mo = _dot(p, vv, NN_DIMS)
            z, dg_ = z_ref[rows, :], dmo_ref[rows, :]
            sg = _sigmoid(z)
            do = dg_ * (z * sg)
            stage[out.slot, 1, rows, :] = (dg_ * mo * (sg * (1.0 + z * (1.0 - sg)))).astype(bf16)
            dp = _dot(do, vv, NT_DIMS)
            ds = p * (dp - jnp.sum(do * mo, axis=-1, keepdims=True)) * scale
            dq, gq_p = _rms_bwd_rows(q, gq_ref[...], _dot(ds, kn, NN_DIMS), on_mxu=False)
            stage[out.slot, 0, rows, :] = dq.astype(bf16)
            gq_s[...] += gq_p
            dkn_s[...] += _dot(ds, qn, TN_DIMS)
            dv_s[...] += _dot(p, do, TN_DIMS)
            return carry

        lax.fori_loop(0, S // MEM_CHUNK, chunk, 0)
        for c in range(2):
            out.start(c, dst(c))
        dk, gk_p = _rms_bwd_rows(kv, gk_ref[...], dkn_s[...])
        dmk_ref[...] = dk
        dmv_ref[...] = dv_s[...]

        @pl.when((b == 0) & (h == 0))
        def _():
            dgq_ref[...] = gq_s[...]
            dgk_ref[...] = gk_p

        @pl.when((b > 0) | (h > 0))
        def _():
            dgq_ref[...] += gq_s[...]
            dgk_ref[...] += gk_p

        out.end(dst)

    gain = pl.BlockSpec((1, MEM_HD), lambda b, h: (0, 0))
    kvb = pl.BlockSpec((MEM_LEN, MEM_HD), lambda b, h: (b, h))
    return pl.pallas_call(
        body, grid=(B, MEM_HEADS),
        in_specs=[pl.BlockSpec((S, MEM_HD), lambda b, h: (b, MQ // MEM_HD + h)),
                  pl.BlockSpec((S, MEM_HD), lambda b, h: (b, ZM // MEM_HD + h)),
                  kvb, pl.BlockSpec((MEM_LEN, MEM_HD), lambda b, h: (b, MEM_HEADS + h)), gain, gain,
                  pl.BlockSpec((S, MEM_HD), lambda b, h: (b, h)), pl.BlockSpec(memory_space=pl.ANY)],
        out_specs=[pl.BlockSpec(memory_space=pl.ANY), kvb, kvb, gain, gain],
        out_shape=[jax.ShapeDtypeStruct(dproj.shape, dproj.dtype), jax.ShapeDtypeStruct((B * MEM_LEN, MEM_W), f32),
                   jax.ShapeDtypeStruct((B * MEM_LEN, MEM_W), f32), jax.ShapeDtypeStruct((1, MEM_HD), f32),
                   jax.ShapeDtypeStruct((1, MEM_HD), f32)],
        scratch_shapes=[pltpu.VMEM((MEM_LEN, MEM_HD), f32), pltpu.VMEM((MEM_LEN, MEM_HD), f32), pltpu.VMEM((1, MEM_HD), f32),
                        pltpu.VMEM((2, 2, S, MEM_HD), bf16), pltpu.SemaphoreType.DMA((2, 2))],
        input_output_aliases={7: 0},
        compiler_params=_params(("arbitrary", "arbitrary")), name="mem_bwd",
    )(proj, proj, mkv, mkv, gq, gk, dmo, dproj)


def _merge_fwd(proj, ag, cg, mg, wa, wc, wm):
    T, NC = proj.shape
    D = wa.shape[1]
    tm, tn = _pick(T, 1024), _pick(D, 512)
    assert GT % tn == 0

    def body(ag_ref, cg_ref, mg_ref, wa_ref, wc_ref, wm_ref, g0_ref, g1_ref, g2_ref, mer_ref, ba_ref, bc_ref, bm_ref):
        ba = _dot(ag_ref[...], wa_ref[...], NN_DIMS)
        bc = _dot(cg_ref[...], wc_ref[...], NN_DIMS)
        bm = _dot(mg_ref[...], wm_ref[...], NN_DIMS)
        mer_ref[...] = (_sigmoid(g0_ref[...]) * ba + _sigmoid(g1_ref[...]) * bc + _sigmoid(g2_ref[...]) * bm).astype(bf16)
        ba_ref[...] = ba.astype(bf16)
        bc_ref[...] = bc.astype(bf16)
        bm_ref[...] = bm.astype(bf16)

    def act(w):
        return pl.BlockSpec((tm, w), lambda i, j: (i, 0))

    def wt(k):
        return pl.BlockSpec((k, tn), lambda i, j: (0, j))

    def gate(n):
        return pl.BlockSpec((tm, tn), lambda i, j: (i, (GT + n * D) // tn + j))

    out = pl.BlockSpec((tm, tn), lambda i, j: (i, j))
    return pl.pallas_call(
        body, grid=(T // tm, D // tn),
        in_specs=[act(ATTN_OUT), act(CONV_W), act(MEM_W), wt(ATTN_OUT), wt(CONV_W), wt(MEM_W), gate(0), gate(1), gate(2)],
        out_specs=[out] * 4, out_shape=[jax.ShapeDtypeStruct((T, D), bf16)] * 4,
        compiler_params=_params(("parallel", "parallel")), name="merge_fwd",
    )(ag, cg, mg, wa, wc, wm, proj, proj, proj)


def _out_loss(merged, w_out, x, tgt):
    T, D = x.shape
    tm, tn = _pick(T, 1024), _pick(D, 512)

    def body(m_ref, w_ref, x_ref, t_ref, dy_ref, dyb_ref, lp_ref):
        e = x_ref[...] + _dot(m_ref[...], w_ref[...], NN_DIMS) - t_ref[...]
        dy = e / D
        dy_ref[...] = dy
        dyb_ref[...] = dy.astype(bf16)
        part = jnp.full((1, 8, LANES), jnp.sum(e * e), f32)

        @pl.when(pl.program_id(1) == 0)
        def _():
            lp_ref[...] = part

        @pl.when(pl.program_id(1) > 0)
        def _():
            lp_ref[...] += part

    tile = pl.BlockSpec((tm, tn), lambda i, j: (i, j))
    return pl.pallas_call(
        body, grid=(T // tm, D // tn),
        in_specs=[pl.BlockSpec((tm, D), lambda i, j: (i, 0)), pl.BlockSpec((D, tn), lambda i, j: (0, j)), tile, tile],
        out_specs=[tile, tile, pl.BlockSpec((1, 8, LANES), lambda i, j: (i, 0, 0))],
        out_shape=[jax.ShapeDtypeStruct((T, D), f32), jax.ShapeDtypeStruct((T, D), bf16),
                   jax.ShapeDtypeStruct((T // tm, 8, LANES), f32)],
        compiler_params=_params(("parallel", "arbitrary")), name="out_loss",
    )(merged, w_out, x, tgt)


def _merge_bwd(proj, dyb, w_out, ba, bc, bm):
    T, NC = proj.shape
    D = w_out.shape[0]
    tm, tn = _pick(T, 1024), _pick(D, 512)
    assert GT % tn == 0

    def body(dy_ref, w_ref, ba_ref, bc_ref, bm_ref, g0_ref, g1_ref, g2_ref, da_ref, dc_ref, dm_ref, dproj_ref, stage, sem):
        i, j = pl.program_id(0), pl.program_id(1)

        def dst(n):
            return dproj_ref.at[pl.ds(i * tm, tm), pl.ds(GT + n * D + j * tn, tn)]

        out = _Deferred(stage, sem, i * (D // tn) + j, (T // tm) * (D // tn) - 1, 3)
        out.begin(dst)
        dmer = _dot(dy_ref[...], w_ref[...], NT_DIMS)
        for n, (g_ref, br_ref, o_ref) in enumerate(((g0_ref, ba_ref, da_ref), (g1_ref, bc_ref, dc_ref), (g2_ref, bm_ref, dm_ref))):
            sg = _sigmoid(g_ref[...])
            o_ref[...] = (sg * dmer).astype(bf16)
            stage[out.slot, n] = (dmer * br_ref[...].astype(f32) * (sg * (1.0 - sg))).astype(bf16)
            out.start(n, dst(n))
        out.end(dst)

    tile = pl.BlockSpec((tm, tn), lambda i, j: (i, j))

    def gate(n):
        return pl.BlockSpec((tm, tn), lambda i, j: (i, (GT + n * D) // tn + j))

    return pl.pallas_call(
        body, grid=(T // tm, D // tn),
        in_specs=[pl.BlockSpec((tm, D), lambda i, j: (i, 0)), pl.BlockSpec((tn, D), lambda i, j: (j, 0)), tile, tile, tile,
                  gate(0), gate(1), gate(2)],
        out_specs=[tile, tile, tile, pl.BlockSpec(memory_space=pl.ANY)],
        out_shape=[jax.ShapeDtypeStruct((T, D), bf16)] * 3 + [jax.ShapeDtypeStruct((T, NC), bf16)],
        scratch_shapes=[pltpu.VMEM((2, 3, tm, tn), bf16), pltpu.SemaphoreType.DMA((2, 3))],
        compiler_params=_params(("arbitrary", "arbitrary")), name="merge_bwd",
    )(dyb, w_out, ba, bc, bm, proj, proj, proj)


def _fwd_bwd_head(x2, mem2, t2, proj, attn, B, S, mem_norm_g, attn_q_norm, attn_k_norm, conv_w, mem_q_norm, mem_k_norm,
                  w_kv, w_a, w_c, w_m, w_out):
    D = x2.shape[1]
    mng = mem_norm_g.reshape(1, D)
    mqn, mkn = mem_q_norm.reshape(1, MEM_HD), mem_k_norm.reshape(1, MEM_HD)
    ag, a_comb, lse = attn

    mh = _rms_fwd(mem2, mng, "rms_mem")
    mkv = _mm(mh, w_kv, mode="nn", out_dtype=f32, name="mkv")
    cg = _conv_fwd(proj, conv_w, B, S)
    mg = _mem_fwd(proj, mkv, mqn, mkn, B, S)
    merged, ba, bc, bm = _merge_fwd(proj, ag, cg, mg, w_a, w_c, w_m)
    dy, dyb, lp = _out_loss(merged, w_out, x2, t2)
    sq_err = jnp.sum(lp[:, 0, 0])

    g = {}
    g["w_out"] = _mm(merged, dyb, mode="tn", out_dtype=f32, name="dw_out")
    dba, dbc, dbm, dproj = _merge_bwd(proj, dyb, w_out, ba, bc, bm)
    g["w_br_attn"] = _mm(ag, dba, mode="tn", out_dtype=f32, name="dw_br_attn")
    g["w_br_conv"] = _mm(cg, dbc, mode="tn", out_dtype=f32, name="dw_br_conv")
    g["w_br_mem"] = _mm(mg, dbm, mode="tn", out_dtype=f32, name="dw_br_mem")
    dag = _mm(dba, w_a, mode="nt", out_dtype=f32, name="d_attn_out")
    dcg = _mm(dbc, w_c, mode="nt", out_dtype=f32, name="d_conv_out")
    dmg = _mm(dbm, w_m, mode="nt", out_dtype=f32, name="d_mem_out")
    dproj, g["attn_q_norm"], g["attn_k_norm"] = _attn_bwd(proj, attn_q_norm, attn_k_norm, a_comb, lse, dag, dproj, B, S)
    dproj, g["conv_w"] = _conv_bwd(proj, conv_w, dcg, dproj, B, S)
    dproj, dmk, dmv, dgmq, dgmk = _mem_bwd(proj, mkv, mqn, mkn, dmg, dproj, B, S)
    g["mem_q_norm"], g["mem_k_norm"] = dgmq.reshape(MEM_HD), dgmk.reshape(MEM_HD)
    dmkv = jnp.concatenate([dmk, dmv], axis=1)
    dmh = _mm(dmkv, w_kv, mode="nt", out_dtype=f32, name="d_mem_h")
    g["mem_norm_g"] = _rms_bwd(mem2, mng, dmh, None, "rms_mem_bwd").reshape(D)
    return sq_err, g, dict(dy=dy, dproj=dproj, mh=mh, dmkv=dmkv)


MESH = pl.DeviceIdType.MESH
HBM = pl.BlockSpec(memory_space=pl.ANY)


def _me():
    x, y, c = lax.axis_index("x"), lax.axis_index("y"), lax.axis_index("c")
    return (x, y, c), 4 * x + 2 * y + c


def _peer(k):
    (x, y, c), _ = _me()
    p = (1 - x if k & 4 else x, 1 - y if k & 2 else y, 1 - c if k & 1 else c)
    return p, 4 * p[0] + 2 * p[1] + p[2]


def _window(ref, axis, size, idx):
    if axis is None:
        return ref.at[idx]
    if axis == 0:
        return ref.at[pl.ds(idx * size, size), :]
    return ref.at[:, pl.ds(idx * size, size)]


def _full_shape(s, axis):
    if axis is None:
        return (N_DEV,) + s.shape
    return tuple(N_DEV * d if i == axis else d for i, d in enumerate(s.shape))


OTHER_CHIPS = (4, 2, 6)


def _spread_comm(shards, axes):
    n = len(shards)

    def copies(ins, outs, send_sems, recv_sems, base=0):
        _, me = _me()
        out = []
        for a in range(n):
            mine = _window(outs[a], axes[a], None if axes[a] is None else shards[a].shape[axes[a]], me)
            out.append(pltpu.make_async_copy(ins[a], mine, send_sems.at[base + a, N_CHIP]))
            for k, dist in enumerate((1,) + OTHER_CHIPS):
                dev, _ = _peer(dist)
                out.append(pltpu.make_async_remote_copy(
                    src_ref=ins[a], dst_ref=mine, send_sem=send_sems.at[base + a, k], recv_sem=recv_sems.at[base + a, k],
                    device_id=dev, device_id_type=MESH))
        return out

    shapes = [jax.ShapeDtypeStruct(_full_shape(s, ax), s.dtype) for s, ax in zip(shards, axes)]
    return _Comm(shards, shapes, (n, N_CHIP + 1), copies)


def _forward_blocks(fulls, axes):
    n = len(fulls)
    sizes = [None if ax is None else f.shape[ax] // N_DEV for f, ax in zip(fulls, axes)]

    def body(*refs):
        outs = refs[n:2 * n]
        send_sems, recv_sems = refs[2 * n:]
        sibling, _ = _peer(1)
        copies = []
        for j, dist in enumerate(OTHER_CHIPS):
            _, held = _peer(dist)
            for a in range(n):
                block = _window(outs[a], axes[a], sizes[a], held)
                copies.append(pltpu.make_async_remote_copy(
                    src_ref=block, dst_ref=block, send_sem=send_sems.at[a, j], recv_sem=recv_sems.at[a, j],
                    device_id=sibling, device_id_type=MESH))
        for cp in copies:
            cp.start()
        for cp in copies:
            cp.wait()

    return pl.pallas_call(
        body, in_specs=[HBM] * n, out_specs=[HBM] * n,
        out_shape=[jax.ShapeDtypeStruct(f.shape, f.dtype) for f in fulls],
        scratch_shapes=[pltpu.SemaphoreType.DMA((n, len(OTHER_CHIPS))), pltpu.SemaphoreType.DMA((n, len(OTHER_CHIPS)))],
        input_output_aliases={a: a for a in range(n)},
        name="forward_blocks",
    )(*fulls)


def _shard_order():
    (x, y, c), me = _me()
    xs, ys, xo, yo = _peer(4)[1], _peer(2)[1], _peer(5)[1], _peer(3)[1]
    north = c == 1
    ids = [me, _peer(1)[1], jnp.where(north, xs, ys), jnp.where(north, yo, xo), jnp.where(north, ys, xs),
           jnp.where(north, xo, yo), _peer(6)[1], _peer(7)[1]]
    return jnp.stack(ids).astype(jnp.int32)


def _proj_gather(h, w_shard, order, comm, comm_at):
    T, K = h.shape
    C = w_shard.shape[1]
    tm = _pick(T, 1024)
    nm = T // tm
    ahead = max(nm - 2, 0)
    n_ci, n_co = len(comm.ins), len(comm.out_shapes)

    def body(*refs):
        order_ref, h_ref, ws_ref = refs[:3]
        c_ins = refs[3:3 + n_ci]
        o_ref, wf_ref = refs[3 + n_ci:5 + n_ci]
        c_outs = refs[5 + n_ci:5 + n_ci + n_co]
        wbuf, send_sems, recv_sems, own_sem, buf_sems, c_send, c_recv = refs[5 + n_ci + n_co:]
        t, i = pl.program_id(0), pl.program_id(1)

        @pl.when((t == comm_at) & (i == 0))
        def _():
            for cp in comm.copies(c_ins, c_outs, c_send, c_recv):
                cp.start()
        (x, y, c), me = _me()
        sibling, sib_id = _peer(1)
        chips = [_peer(dist) for dist in OTHER_CHIPS]

        def win(idx):
            return wf_ref.at[:, pl.ds(idx * C, C)]

        def copy(k, block, to, src=None):
            return pltpu.make_async_remote_copy(
                src_ref=win(block) if src is None else src, dst_ref=win(block),
                send_sem=send_sems.at[k], recv_sem=recv_sems.at[k], device_id=to, device_id_type=MESH)

        def fetch(tt, src):
            return pltpu.make_async_copy(src, wbuf.at[tt % 2], buf_sems.at[tt % 2])

        own = pltpu.make_async_copy(ws_ref, win(me), own_sem)
        (x_nbr, x_id), (y_nbr, y_id), (_, d_id) = chips

        def half(k, block, top, to):
            rows = wf_ref.at[pl.ds(0 if top else K // 2, K // 2), pl.ds(block * C, C)]
            return pltpu.make_async_remote_copy(src_ref=rows, dst_ref=rows, send_sem=send_sems.at[k], recv_sem=recv_sems.at[k],
                                                device_id=to, device_id_type=MESH)

        here = (x, y, c)

        def pass_on(from_x):
            if from_x:
                copy(4, x_id, sibling).start()
                half(8, x_id, False, y_nbr).start()
            else:
                copy(5, y_id, sibling).start()
                half(7, y_id, True, x_nbr).start()

        @pl.when((t == 0) & (i == 0))
        def _():
            fetch(0, ws_ref).start()
            own.start()
            copy(0, me, sibling, src=ws_ref).start()

        for tt in range(N_DEV):
            @pl.when((t == tt) & (i == 0))
            def _(tt=tt):
                fetch(tt, ws_ref).wait()

        o_ref[...] = _dot(h_ref[...], wbuf[t % 2], NN_DIMS)

        def schedule(first_x):
            on = c == (1 if first_x else 0)
            (f_dev, f_id), (g_dev, g_id) = ((x_nbr, x_id), (y_nbr, y_id)) if first_x else ((y_nbr, y_id), (x_nbr, x_id))
            kf, kg = (1, 2) if first_x else (2, 1)
            pf, pg = (4, 5) if first_x else (5, 4)

            @pl.when((t == 0) & (i == 0) & on)
            def _():
                copy(kf, me, f_dev, src=ws_ref).start()

            def event(nxt):
                if nxt == 1:
                    copy(0, sib_id, here).wait_recv()
                    return sib_id
                if nxt == 2:
                    copy(kf, f_id, here).wait_recv()
                    copy(kf, me, f_dev, src=ws_ref).wait_send()
                    copy(kg, me, g_dev, src=ws_ref).start()
                    pass_on(first_x)
                    return f_id
                if nxt == 3:
                    copy(pg, g_id + 1 - 2 * c, here).wait_recv()
                    return g_id + 1 - 2 * c
                if nxt == 4:
                    copy(kg, g_id, here).wait_recv()
                    pass_on(not first_x)
                    return g_id
                if nxt == 5:
                    copy(pf, f_id + 1 - 2 * c, here).wait_recv()
                    return f_id + 1 - 2 * c
                if nxt == 6:
                    half(7, d_id, True, here).wait_recv()
                    half(8, d_id, False, here).wait_recv()
                    copy(6, d_id, sibling).start()
                    return d_id
                copy(6, d_id + 1 - 2 * c, here).wait_recv()
                return d_id + 1 - 2 * c

            for tt in range(N_DEV - 1):
                @pl.when((t == tt) & (i == ahead) & on)
                def _(tt=tt):
                    fetch(tt + 1, win(event(tt + 1))).start()

            @pl.when((t == N_DEV - 1) & (i == nm - 1) & on)
            def _():
                copy(kg, me, g_dev, src=ws_ref).wait_send()

        schedule(True)
        schedule(False)

        @pl.when((t == N_DEV - 1) & (i == nm - 1))
        def _():
            copy(0, me, sibling, src=ws_ref).wait_send()
            half(7, y_id, True, x_nbr).wait_send()
            half(8, x_id, False, y_nbr).wait_send()
            for j, (_, dev_id) in enumerate(chips):
                copy(4 + j, dev_id, sibling).wait_send()
            own.wait()
            for cp in comm.copies(c_ins, c_outs, c_send, c_recv):
                cp.wait()

    hbm = pl.BlockSpec(memory_space=pl.ANY)
    res = pl.pallas_call(
        body,
        grid_spec=pltpu.PrefetchScalarGridSpec(
            num_scalar_prefetch=1, grid=(N_DEV, nm),
            in_specs=[pl.BlockSpec((tm, K), lambda t, i, order_ref: (i, 0)), hbm] + [hbm] * n_ci,
            out_specs=[pl.BlockSpec((tm, C), lambda t, i, order_ref: (i, order_ref[t])), hbm] + [hbm] * n_co,
            scratch_shapes=[pltpu.VMEM((2, K, C), bf16), pltpu.SemaphoreType.DMA((9,)), pltpu.SemaphoreType.DMA((9,)),
                            pltpu.SemaphoreType.DMA, pltpu.SemaphoreType.DMA((2,))] + comm.scratch()),
        out_shape=[jax.ShapeDtypeStruct((T, N_DEV * C), f32), jax.ShapeDtypeStruct((K, N_DEV * C), bf16)] + comm.out_shapes,
        compiler_params=_params(("arbitrary", "arbitrary")), name="proj_gather",
    )(order, h, w_shard, *comm.ins)
    return res[0], res[1], list(res[2:])


N_CHIP = 4


def _shard_shape(g, ax):
    return tuple(d // N_DEV if i == ax else d for i, d in enumerate(g.shape))


def _sibling_comm(grads, axes):
    n = len(grads)
    sizes = [g.shape[ax] // N_DEV for g, ax in zip(grads, axes)]

    def copies(ins, lands, send_sems, recv_sems, base=0):
        sibling, _ = _peer(1)
        out = []
        for j in range(N_CHIP):
            _, owner = _peer(2 * j + 1)
            for a in range(n):
                out.append(pltpu.make_async_remote_copy(
                    src_ref=_window(ins[a], axes[a], sizes[a], owner), dst_ref=lands[a].at[j],
                    send_sem=send_sems.at[base + a, j], recv_sem=recv_sems.at[base + a, j], device_id=sibling,
                    device_id_type=MESH))
        return out

    shapes = [jax.ShapeDtypeStruct((N_CHIP,) + _shard_shape(g, ax), g.dtype) for g, ax in zip(grads, axes)]
    return _Comm(grads, shapes, (n, N_CHIP), copies)


def _chips_comm(sums):
    n = len(sums)

    def copies(ins, lands, send_sems, recv_sems, base=0):
        out = []
        for j in range(1, N_CHIP):
            dev, _ = _peer(2 * j)
            for a in range(n):
                out.append(pltpu.make_async_remote_copy(
                    src_ref=ins[a].at[j - 1], dst_ref=lands[a].at[j - 1],
                    send_sem=send_sems.at[base + a, j - 1], recv_sem=recv_sems.at[base + a, j - 1], device_id=dev,
                    device_id_type=MESH))
        return out

    return _Comm(sums, [jax.ShapeDtypeStruct(s.shape, s.dtype) for s in sums], (n, N_CHIP), copies)


def _join(c1, c2):
    n1, m1, k1 = len(c1.ins), len(c1.out_shapes), c1.sem_shape[0]

    def copies(ins, lands, send_sems, recv_sems):
        return (c1.copies(ins[:n1], lands[:m1], send_sems, recv_sems, base=0)
                + c2.copies(ins[n1:], lands[m1:], send_sems, recv_sems, base=k1))

    aliases = {**c1.aliases, **{n1 + k: m1 + v for k, v in c2.aliases.items()}}
    return _Comm(c1.ins + c2.ins, c1.out_shapes + c2.out_shapes, (k1 + c2.sem_shape[0], N_CHIP), copies, aliases)


def _chips_first_hop(sums):
    n = len(sums)

    def copies(ins, outs, send_sems, recv_sems, base=0):
        lands, relays = outs[:n], outs[n:]
        (y_dev, _), (x_dev, _) = _peer(2), _peer(4)
        out = []
        for a in range(n):
            half = sums[a].shape[1] // 2
            for k, (src, dst, dev) in enumerate((
                    (ins[a].at[0], lands[a].at[0], y_dev), (ins[a].at[1], lands[a].at[1], x_dev),
                    (ins[a].at[2, pl.ds(0, half)], relays[a].at[0], x_dev),
                    (ins[a].at[2, pl.ds(half, half)], relays[a].at[1], y_dev))):
                out.append(pltpu.make_async_remote_copy(
                    src_ref=src, dst_ref=dst, send_sem=send_sems.at[base + a, k], recv_sem=recv_sems.at[base + a, k],
                    device_id=dev, device_id_type=MESH))
        return out

    shapes = ([jax.ShapeDtypeStruct(s.shape, s.dtype) for s in sums]
              + [jax.ShapeDtypeStruct((2, s.shape[1] // 2) + s.shape[2:], s.dtype) for s in sums])
    return _Comm(sums, shapes, (n, N_CHIP), copies)


def _chips_relay(relays, lands):
    n = len(relays)

    def copies(ins, outs, send_sems, recv_sems, base=0):
        (y_dev, _), (x_dev, _) = _peer(2), _peer(4)
        out = []
        for a in range(n):
            half = relays[a].shape[1]
            for k, (rows, dev) in enumerate(((pl.ds(0, half), y_dev), (pl.ds(half, half), x_dev))):
                out.append(pltpu.make_async_remote_copy(
                    src_ref=ins[a].at[k], dst_ref=outs[a].at[2, rows], send_sem=send_sems.at[base + a, k],
                    recv_sem=recv_sems.at[base + a, k], device_id=dev, device_id_type=MESH))
        return out

    return _Comm(list(relays) + list(lands), [jax.ShapeDtypeStruct(l.shape, l.dtype) for l in lands], (n, N_CHIP), copies,
                 aliases={n + a: a for a in range(n)})


def _rs_chip_sum(grad, land, xyc, axis, name):
    R, C = land.shape[1:]
    tr = _pick(R, max(8, (640 * 1024) // C), 8)

    def body(xyc_ref, g_ref, l_ref, o_ref):
        o_ref[0] = (g_ref[...] + l_ref[0]).astype(bf16)

    def owner(j, xyc_ref):
        jj = j + 1
        return 4 * (xyc_ref[0] ^ (jj >> 1)) + 2 * (xyc_ref[1] ^ (jj & 1)) + xyc_ref[2]

    if axis == 0:
        g_spec = pl.BlockSpec((tr, C), lambda j, i, xyc_ref: (owner(j, xyc_ref) * (R // tr) + i, 0))
    else:
        g_spec = pl.BlockSpec((tr, C), lambda j, i, xyc_ref: (i, owner(j, xyc_ref)))
    return pl.pallas_call(
        body,
        grid_spec=pltpu.PrefetchScalarGridSpec(
            num_scalar_prefetch=1, grid=(N_CHIP - 1, R // tr),
            in_specs=[g_spec, pl.BlockSpec((1, tr, C), lambda j, i, xyc_ref: (j + 1, i, 0))],
            out_specs=pl.BlockSpec((1, tr, C), lambda j, i, xyc_ref: (j, i, 0))),
        out_shape=jax.ShapeDtypeStruct((N_CHIP - 1, R, C), bf16),
        compiler_params=_params(("parallel", "parallel")), name=name,
    )(xyc, grad, land)


def _allreduce_small(part):
    R = part.shape[0]

    def body(p_ref, o_ref, buf, send_sems, recv_sems):
        (x, y, c), me = _me()
        buf[me] = p_ref[...]
        copies = []
        for k in range(1, N_DEV):
            dev, dev_id = _peer(k)
            copies.append(pltpu.make_async_remote_copy(
                src_ref=p_ref, dst_ref=buf.at[me], send_sem=send_sems.at[k - 1], recv_sem=recv_sems.at[k - 1],
                device_id=dev, device_id_type=MESH))
        for cp in copies:
            cp.start()
        for k in range(1, N_DEV):
            _, dev_id = _peer(k)
            pltpu.make_async_remote_copy(
                src_ref=p_ref, dst_ref=buf.at[dev_id], send_sem=send_sems.at[k - 1], recv_sem=recv_sems.at[k - 1],
                device_id=(x, y, c), device_id_type=MESH).wait_recv()
        for cp in copies:
            cp.wait_send()
        acc = buf[0]
        for d in range(1, N_DEV):
            acc = acc + buf[d]
        o_ref[...] = acc

    return pl.pallas_call(
        body, in_specs=[pl.BlockSpec(memory_space=pltpu.VMEM)], out_specs=pl.BlockSpec(memory_space=pltpu.VMEM),
        out_shape=jax.ShapeDtypeStruct((R, LANES), f32),
        scratch_shapes=[pltpu.VMEM((N_DEV, R, LANES), f32), pltpu.SemaphoreType.DMA((N_DEV - 1,)), pltpu.SemaphoreType.DMA((N_DEV - 1,))],
        name="allreduce_small",
    )(part)


def _adamw_math(w, g, m, v):
    m2 = ADAM_B1 * m + (1.0 - ADAM_B1) * g
    v2 = ADAM_B2 * v + (1.0 - ADAM_B2) * (g * g)
    m_hat = m2 / (1.0 - ADAM_B1 ** ADAM_STEP)
    v_hat = v2 / (1.0 - ADAM_B2 ** ADAM_STEP)
    return -ADAM_LR * (m_hat / (jnp.sqrt(v_hat) + ADAM_EPS) + ADAM_WD * w), m2, v2


def _adamw_shard(grad, land_sib, land_chips, w, m, v, me, axis, name, row0=0, prev=None):
    R, C = land_sib.shape[1:]
    assert axis == 1 or R == w.shape[0]
    tr = _pick(R, max(8, (320 * 1024) // C), 8)
    r0 = row0 // tr
    n_prev = len(prev) if prev else 0

    def body(me_ref, g_ref, s_ref, l_ref, w_ref, m_ref, v_ref, *rest):
        go_ref, d_ref, mo_ref, vo_ref = rest[n_prev:]
        g = g_ref[...] + s_ref[0]
        for j in range(N_CHIP - 1):
            g = g + l_ref[j].astype(f32)
        d, m2, v2 = _adamw_math(w_ref[...], g, m_ref[...], v_ref[...])
        go_ref[...] = g
        d_ref[...] = d
        mo_ref[...] = m2
        vo_ref[...] = v2

    if axis == 0:
        g_spec = pl.BlockSpec((tr, C), lambda i, me_ref: (me_ref[0] * (R // tr) + i, 0))
    else:
        g_spec = pl.BlockSpec((tr, C), lambda i, me_ref: (i, me_ref[0]))
    blk = pl.BlockSpec((tr, C), lambda i, me_ref: (r0 + i, 0))
    return pl.pallas_call(
        body,
        grid_spec=pltpu.PrefetchScalarGridSpec(
            num_scalar_prefetch=1, grid=(R // tr,),
            in_specs=[g_spec, pl.BlockSpec((1, tr, C), lambda i, me_ref: (0, i, 0)),
                      pl.BlockSpec((N_CHIP - 1, tr, C), lambda i, me_ref: (0, i, 0)), blk, blk, blk]
            + [pl.BlockSpec(memory_space=pl.ANY)] * n_prev,
            out_specs=[blk] * 4),
        out_shape=[jax.ShapeDtypeStruct(w.shape, f32)] * 4,
        input_output_aliases={7 + i: i for i in range(n_prev)},
        compiler_params=_params(("parallel",)), name=name,
    )(me, grad, land_sib, land_chips, w, m, v, *(prev or []))


def _adamw_small(g, w, m, v):
    def body(g_ref, w_ref, m_ref, v_ref, d_ref, mo_ref, vo_ref):
        d, m2, v2 = _adamw_math(w_ref[...], g_ref[...], m_ref[...], v_ref[...])
        d_ref[...] = d
        mo_ref[...] = m2
        vo_ref[...] = v2

    return pl.pallas_call(body, out_shape=[jax.ShapeDtypeStruct(g.shape, f32)] * 3, name="adamw_small")(g, w, m, v)


def _pack_rows(parts, total_rows):
    rows = jnp.concatenate([p.reshape(-1, LANES) for p in parts], axis=0)
    return jnp.pad(rows, ((0, total_rows - rows.shape[0]), (0, 0)))


BIG = ("w_in", "mem_w_kv", "w_br_attn", "w_br_conv", "w_br_mem", "w_out")
BIG_AXIS = {"w_in": 1, "mem_w_kv": 0, "w_br_attn": 1, "w_br_conv": 1, "w_br_mem": 1, "w_out": 0}
SMALL = ("norm_g", "mem_norm_g", "attn_q_norm", "attn_k_norm", "mem_q_norm", "mem_k_norm")
ALL_W = ("norm_g", "mem_norm_g", "w_in", "attn_q_norm", "attn_k_norm", "conv_w", "mem_w_kv", "mem_q_norm", "mem_k_norm",
         "w_br_attn", "w_br_conv", "w_br_mem", "w_out")


def kernel(x, mem, norm_g, mem_norm_g, w_in, attn_q_norm, attn_k_norm, conv_w, mem_w_kv, mem_q_norm, mem_k_norm, w_br_attn, w_br_conv, w_br_mem, w_out, loss_target, m_norm_g, m_mem_norm_g, m_w_in, m_attn_q_norm, m_attn_k_norm, m_conv_w, m_mem_w_kv, m_mem_q_norm, m_mem_k_norm, m_w_br_attn, m_w_br_conv, m_w_br_mem, m_w_out, v_norm_g, v_mem_norm_g, v_w_in, v_attn_q_norm, v_attn_k_norm, v_conv_w, v_mem_w_kv, v_mem_q_norm, v_mem_k_norm, v_w_br_attn, v_w_br_conv, v_w_br_mem, v_w_out):
    w = dict(norm_g=norm_g, mem_norm_g=mem_norm_g, w_in=w_in, attn_q_norm=attn_q_norm, attn_k_norm=attn_k_norm, conv_w=conv_w,
             mem_w_kv=mem_w_kv, mem_q_norm=mem_q_norm, mem_k_norm=mem_k_norm, w_br_attn=w_br_attn, w_br_conv=w_br_conv,
             w_br_mem=w_br_mem, w_out=w_out)
    mo = dict(norm_g=m_norm_g, mem_norm_g=m_mem_norm_g, w_in=m_w_in, attn_q_norm=m_attn_q_norm, attn_k_norm=m_attn_k_norm,
              conv_w=m_conv_w, mem_w_kv=m_mem_w_kv, mem_q_norm=m_mem_q_norm, mem_k_norm=m_mem_k_norm, w_br_attn=m_w_br_attn,
              w_br_conv=m_w_br_conv, w_br_mem=m_w_br_mem, w_out=m_w_out)
    vo = dict(norm_g=v_norm_g, mem_norm_g=v_mem_norm_g, w_in=v_w_in, attn_q_norm=v_attn_q_norm, attn_k_norm=v_attn_k_norm,
              conv_w=v_conv_w, mem_w_kv=v_mem_w_kv, mem_q_norm=v_mem_q_norm, mem_k_norm=v_mem_k_norm, w_br_attn=v_w_br_attn,
              w_br_conv=v_w_br_conv, w_br_mem=v_w_br_mem, w_out=v_w_out)
    B, S, D = x.shape
    me = (4 * lax.axis_index("x") + 2 * lax.axis_index("y") + lax.axis_index("c")).astype(jnp.int32)

    T = B * S
    x2, t2, mem2, ng = x.reshape(T, D), loss_target.reshape(T, D), mem.reshape(B * MEM_LEN, D), norm_g.reshape(1, D)
    h = _rms_fwd(x2, ng, "rms_x")
    rows_sharded, cols_sharded = ("mem_w_kv", "w_out"), ("w_br_attn", "w_br_conv", "w_br_mem")
    later = rows_sharded + cols_sharded
    later_axes = [BIG_AXIS[n] for n in later] + [None]
    conv_pad = jnp.pad(conv_w, ((0, 8 - conv_w.shape[0]), (0, 0)))
    proj, w_in_full, spread_a = _proj_gather(
        h, w_in.astype(bf16), _shard_order(),
        _spread_comm([w[n].astype(bf16) for n in rows_sharded], [BIG_AXIS[n] for n in rows_sharded]), comm_at=N_DEV - 3)
    *attn, spread_b = _attn_fwd(proj, attn_q_norm, attn_k_norm, B, S,
                                comm=_spread_comm([w[n].astype(bf16) for n in cols_sharded] + [conv_pad],
                                                  [BIG_AXIS[n] for n in cols_sharded] + [None]))
    *fulls, conv_g = _forward_blocks(spread_a + spread_b, later_axes)
    wf = dict(zip(later, fulls), w_in=w_in_full)
    conv_full = conv_g[:, :3, :].transpose(1, 0, 2).reshape(3, CONV_W)

    sq_err, g, t = _fwd_bwd_head(x2, mem2, t2, proj, attn, B, S, mem_norm_g, attn_q_norm, attn_k_norm, conv_full, mem_q_norm,
                                 mem_k_norm, wf["mem_w_kv"], wf["w_br_attn"], wf["w_br_conv"], wf["w_br_mem"], wf["w_out"])
    t.update(x2=x2, ng=ng, h=h)

    xyc = jnp.stack([lax.axis_index("x"), lax.axis_index("y"), lax.axis_index("c")]).astype(jnp.int32)
    me1 = me.reshape(1)
    early = ("w_out", "w_br_attn", "w_br_conv", "w_br_mem")
    g["mem_w_kv"], sib_early = _mm(t["mh"], t["dmkv"], mode="tn", out_dtype=f32, name="dw_kv",
                                   comm=_sibling_comm([g[n] for n in early], [BIG_AXIS[n] for n in early]))
    sums_early = [_rs_chip_sum(g[n], ls, xyc, BIG_AXIS[n], "rs_sum_" + n) for n, ls in zip(early, sib_early)]
    tm_w = _pick(D // 2, 1024)
    half = (D // 2) // tm_w
    g_top, (*chips_early, sib_kv) = _mm(t["h"], t["dproj"], mode="tn", out_dtype=f32, name="dw_in_top", tm=tm_w,
                                        m_tiles=(0, half),
                                        comm=_join(_chips_comm(sums_early), _sibling_comm([g["mem_w_kv"]], [0])))
    sum_kv = _rs_chip_sum(g["mem_w_kv"], sib_kv, xyc, 0, "rs_sum_mem_w_kv")
    g_bot, (chips_kv, sib_top) = _mm(t["h"], t["dproj"], mode="tn", out_dtype=f32, name="dw_in_bot", tm=tm_w,
                                     m_tiles=(half, half), comm=_join(_chips_comm([sum_kv]), _sibling_comm([g_top], [1])))
    sum_top = _rs_chip_sum(g_top, sib_top, xyc, 1, "rs_sum_w_in_top")
    tm_t = _pick(T // 2, 1024)
    halfm = (T // 2) // tm_t
    dh, (part_top, relay_top, sib_bot) = _mm(t["dproj"], wf["w_in"], mode="nt", out_dtype=f32, name="d_h_a", tm=tm_t, tk=D_H_TK,
                                             m_tiles=(0, halfm), into="new",
                                             comm=_join(_chips_first_hop([sum_top]), _sibling_comm([g_bot], [1])))
    sum_bot = _rs_chip_sum(g_bot, sib_bot, xyc, 1, "rs_sum_w_in_bot")
    dh, (part_bot, relay_bot, chips_top) = _mm(t["dproj"], wf["w_in"], mode="nt", out_dtype=f32, name="d_h_b", tm=tm_t,
                                               tk=D_H_TK, m_tiles=(halfm, halfm), into=dh,
                                               comm=_join(_chips_first_hop([sum_bot]), _chips_relay([relay_top], [part_top])))
    dx, dng, (chips_bot,) = _rms_bwd(t["x2"], t["ng"], dh, t["dy"], "rms_x_bwd", comm=_chips_relay([relay_bot], [part_bot]))
    g["norm_g"] = dng.reshape(D)

    grad, delta, new_m, new_v = {}, {}, {}, {}
    for n, ls, lc in zip(early + ("mem_w_kv",), sib_early + [sib_kv], chips_early + [chips_kv]):
        grad[n], delta[n], new_m[n], new_v[n] = _adamw_shard(g[n], ls, lc, w[n], mo[n], vo[n], me1, BIG_AXIS[n], "adamw_" + n)
    top = _adamw_shard(g_top, sib_top, chips_top, w_in, m_w_in, v_w_in, me1, 1, "adamw_w_in_top")
    grad["w_in"], delta["w_in"], new_m["w_in"], new_v["w_in"] = _adamw_shard(
        g_bot, sib_bot, chips_bot, w_in, m_w_in, v_w_in, me1, 1, "adamw_w_in_bot", row0=D // 2, prev=top)

    n_rep = sum(w[n].size for n in SMALL) // LANES
    conv_rows = g["conv_w"].reshape(3, N_DEV, LANES).transpose(1, 0, 2).reshape(3 * N_DEV, LANES)
    n_rows = -(-(n_rep + 3 * N_DEV + 1) // 8) * 8
    part = _pack_rows([g[n] for n in SMALL] + [conv_rows, jnp.full((1, LANES), sq_err, f32)], n_rows)
    tot = _allreduce_small(part)
    loss = tot[n_rep + 3 * N_DEV, 0] * (0.5 / D)
    n_small = -(-(n_rep + 3) // 8) * 8
    g_small = _pack_rows([tot[:n_rep], lax.dynamic_slice(tot, (n_rep + 3 * me, 0), (3, LANES))], n_small)
    names = SMALL + ("conv_w",)
    d_s, m_s, v_s = _adamw_small(g_small, _pack_rows([w[n] for n in names], n_small), _pack_rows([mo[n] for n in names], n_small),
                                 _pack_rows([vo[n] for n in names], n_small))
    off = 0
    for n in names:
        r = w[n].size // LANES
        grad[n], delta[n] = g_small[off:off + r].reshape(w[n].shape), d_s[off:off + r].reshape(w[n].shape)
        new_m[n], new_v[n] = m_s[off:off + r].reshape(w[n].shape), v_s[off:off + r].reshape(w[n].shape)
        off += r
    return (loss, dx.reshape(B, S, D), *[grad[n] for n in ALL_W], *[delta[n] for n in ALL_W], *[new_m[n] for n in ALL_W],
            *[new_v[n] for n in ALL_W])
```

```python
import functools

import jax
import jax.numpy as jnp
from jax import lax
from jax.experimental import pallas as pl
from jax.experimental.pallas import tpu as pltpu

f32 = jnp.float32
bf16 = jnp.bfloat16

N_DEV = 8
HEAD_DIM = 128
DILATIONS = (1, 4, 16)
N_GROUPS = 3
HEADS = 4
BLK = 128
ATTN_QKV = N_GROUPS * HEADS * HEAD_DIM
ATTN_OUT = HEADS * HEAD_DIM
CONV_W = 1024
MEM_LEN = 256
MEM_HEADS = 4
MEM_HD = 256
MEM_W = MEM_HEADS * MEM_HD
EPS = 1e-6
Q0, K0, V0 = 0, ATTN_QKV, 2 * ATTN_QKV
ZA = 3 * ATTN_QKV
CB, CC, CV, ZC = ZA + ATTN_OUT, ZA + ATTN_OUT + CONV_W, ZA + ATTN_OUT + 2 * CONV_W, ZA + ATTN_OUT + 3 * CONV_W
MQ = ZC + CONV_W
ZM = MQ + MEM_W
GT = ZM + MEM_W

ADAM_LR, ADAM_B1, ADAM_B2, ADAM_EPS, ADAM_WD, ADAM_STEP = 0.001, 0.9, 0.999, 1e-08, 0.01, 10

VMEM_LIMIT = 56 * 1024 * 1024
D_H_TK = 4352
LANES = 128

NT_DIMS = (((1,), (1,)), ((), ()))
TN_DIMS = (((0,), (0,)), ((), ()))
NN_DIMS = (((1,), (0,)), ((), ()))


def _params(sem=None):
    return pltpu.CompilerParams(dimension_semantics=sem, vmem_limit_bytes=VMEM_LIMIT)


def _pick(n, pref, q=LANES):
    t = (min(pref, n) // q) * q
    while t >= q:
        if n % t == 0:
            return t
        t -= q
    return n


def _dot(a, b, dims):
    return lax.dot_general(a.astype(bf16), b.astype(bf16), dims, preferred_element_type=f32)


def _sigmoid(z):
    return 0.5 * jnp.tanh(0.5 * z) + 0.5


def _rstd(v):
    return lax.rsqrt(jnp.mean(v * v, axis=-1, keepdims=True) + EPS)


class _Deferred:
    def __init__(self, stage, sems, step, last, n):
        assert last >= 1
        self.stage, self.sems, self.step, self.last, self.n = stage, sems, step, last, n
        self.slot = step % 2

    def _drain(self, slot, like):
        for c in range(self.n):
            pltpu.make_async_copy(self.stage.at[slot, c], like(c), self.sems.at[slot, c]).wait()

    def begin(self, like):
        pl.when(self.step >= 2)(lambda: self._drain(self.slot, like))

    def start(self, c, dst):
        pltpu.make_async_copy(self.stage.at[self.slot, c], dst, self.sems.at[self.slot, c]).start()

    def end(self, like):
        @pl.when(self.step == self.last)
        def _():
            self._drain(self.slot, like)
            self._drain(1 - self.slot, like)


def _row_sum(v):
    return _dot(v, jnp.ones((v.shape[1], v.shape[1]), bf16), NN_DIMS)


def _rstd_head(v):
    return lax.rsqrt(_row_sum(v * v) * (1.0 / v.shape[1]) + EPS)


class _Comm:
    def __init__(self, ins, out_shapes, sem_shape, copies, aliases=None):
        self.ins, self.out_shapes, self.sem_shape, self.copies = list(ins), list(out_shapes), sem_shape, copies
        self.aliases = dict(aliases or {})

    def scratch(self):
        return [pltpu.SemaphoreType.DMA(self.sem_shape), pltpu.SemaphoreType.DMA(self.sem_shape)]

    def io_aliases(self, first_in, first_out):
        return {first_in + k: first_out + v for k, v in self.aliases.items()}


def _mm(a, b, *, mode, out_dtype, name, tm=1024, tn=1024, tk=4096, m_tiles=None, into=None, comm=None):
    if mode == "nn":
        (M, K), (K2, N) = a.shape, b.shape
    elif mode == "nt":
        (M, K), (N, K2) = a.shape, b.shape
    else:
        (K, M), (K2, N) = a.shape, b.shape
    assert K == K2
    tm, tn, tk = _pick(M, tm), _pick(N, tn), _pick(K, tk)
    nk = K // tk
    m0, mc = m_tiles if m_tiles is not None else (0, M // tm)
    o0 = 0 if into is None else m0
    aliased = into is not None and not isinstance(into, str)
    n_ci = len(comm.ins) if comm else 0
    n_co = len(comm.out_shapes) if comm else 0
    grid = (mc, N // tn, nk)
    dims = {"nn": NN_DIMS, "nt": NT_DIMS, "tn": TN_DIMS}[mode]

    def body(*refs, nk):
        a_ref, b_ref = refs[:2]
        pos = 3 if aliased else 2
        c_ins, o_ref, c_outs = refs[pos:pos + n_ci], refs[pos + n_ci], refs[pos + n_ci + 1:pos + n_ci + 1 + n_co]
        scratch = refs[pos + n_ci + 1 + n_co:]
        ids = [pl.program_id(d) for d in range(3)]
        if comm:
            sems = scratch[-2:]

            @pl.when((ids[0] == 0) & (ids[1] == 0) & (ids[2] == 0))
            def _():
                for cp in comm.copies(c_ins, c_outs, *sems):
                    cp.start()

        if nk == 1:
            o_ref[...] = _dot(a_ref[...], b_ref[...], dims).astype(o_ref.dtype)
        else:
            acc_ref, k = scratch[0], ids[2]

            @pl.when(k == 0)
            def _():
                acc_ref[...] = _dot(a_ref[...], b_ref[...], dims)

            @pl.when((k > 0) & (k < nk - 1))
            def _():
                acc_ref[...] += _dot(a_ref[...], b_ref[...], dims)

            @pl.when(k == nk - 1)
            def _():
                o_ref[...] = (acc_ref[...] + _dot(a_ref[...], b_ref[...], dims)).astype(o_ref.dtype)

        if comm:
            @pl.when((ids[0] == grid[0] - 1) & (ids[1] == grid[1] - 1) & (ids[2] == grid[2] - 1))
            def _():
                for cp in comm.copies(c_ins, c_outs, *sems):
                    cp.wait()

    if mode == "tn":
        a_spec = pl.BlockSpec((tk, tm), lambda i, j, k: (k, m0 + i))
    else:
        a_spec = pl.BlockSpec((tm, tk), lambda i, j, k: (m0 + i, k))
    if mode == "nt":
        b_spec = pl.BlockSpec((tn, tk), lambda i, j, k: (j, k))
    else:
        b_spec = pl.BlockSpec((tk, tn), lambda i, j, k: (k, j))
    hbm = pl.BlockSpec(memory_space=pl.ANY)
    res = pl.pallas_call(
        functools.partial(body, nk=nk),
        grid=grid,
        in_specs=[a_spec, b_spec] + [hbm] * (aliased + n_ci),
        out_specs=[pl.BlockSpec((tm, tn), lambda i, j, k: (o0 + i, j))] + [hbm] * n_co,
        out_shape=[jax.ShapeDtypeStruct((mc * tm if into is None else M, N), out_dtype)] + (comm.out_shapes if comm else []),
        scratch_shapes=([pltpu.VMEM((tm, tn), f32)] if nk > 1 else []) + (comm.scratch() if comm else []),
        input_output_aliases={**({2: 0} if aliased else {}), **(comm.io_aliases(2 + aliased, 1) if comm else {})},
        compiler_params=_params(("arbitrary",) * 3 if comm else ("parallel", "parallel", "arbitrary")),
        name=name,
    )(a, b, *([into] if aliased else []), *(comm.ins if comm else []))
    return (res[0], list(res[1:])) if comm else res[0]


def _rms_fwd(x, g, name):
    T, D = x.shape
    tm = _pick(T, 256, 8)

    def body(x_ref, g_ref, h_ref):
        xv = x_ref[...]
        h_ref[...] = (xv * _rstd(xv) * g_ref[...]).astype(bf16)

    return pl.pallas_call(
        body, grid=(T // tm,),
        in_specs=[pl.BlockSpec((tm, D), lambda i: (i, 0)), pl.BlockSpec((1, D), lambda i: (0, 0))],
        out_specs=pl.BlockSpec((tm, D), lambda i: (i, 0)),
        out_shape=jax.ShapeDtypeStruct((T, D), bf16),
        compiler_params=_params(("parallel",)), name=name,
    )(x, g)


def _rms_bwd(x, g, dh, dy, name, comm=None):
    T, D = x.shape
    tm = _pick(T, 256, 8)
    with_dx = dy is not None
    n_ci = len(comm.ins) if comm else 0
    n_co = len(comm.out_shapes) if comm else 0

    def body(*refs):
        if with_dx:
            x_ref, g_ref, dh_ref, dy_ref = refs[:4]
            c_ins, (dx_ref, dg_ref) = refs[4:4 + n_ci], refs[4 + n_ci:6 + n_ci]
            c_outs, sems = refs[6 + n_ci:6 + n_ci + n_co], refs[6 + n_ci + n_co:]
        else:
            x_ref, g_ref, dh_ref, dg_ref = refs
        if comm:
            @pl.when(pl.program_id(0) == 0)
            def _():
                for cp in comm.copies(c_ins, c_outs, *sems):
                    cp.start()

        xv = x_ref[...]
        r = _rstd(xv)
        xh = xv * r
        dhv = dh_ref[...]
        part = jnp.sum(dhv * xh, axis=0, keepdims=True)

        @pl.when(pl.program_id(0) == 0)
        def _():
            dg_ref[...] = part

        @pl.when(pl.program_id(0) > 0)
        def _():
            dg_ref[...] += part

        if with_dx:
            dxh = dhv * g_ref[...]
            dx_ref[...] = dy_ref[...] + r * (dxh - xh * jnp.mean(dxh * xh, axis=-1, keepdims=True))

        if comm:
            @pl.when(pl.program_id(0) == T // tm - 1)
            def _():
                for cp in comm.copies(c_ins, c_outs, *sems):
                    cp.wait()

    row = pl.BlockSpec((tm, D), lambda i: (i, 0))
    vec = pl.BlockSpec((1, D), lambda i: (0, 0))
    hbm = pl.BlockSpec(memory_space=pl.ANY)
    if with_dx:
        res = pl.pallas_call(
            body, grid=(T // tm,), in_specs=[row, vec, row, row] + [hbm] * n_ci, out_specs=[row, vec] + [hbm] * n_co,
            out_shape=[jax.ShapeDtypeStruct((T, D), f32), jax.ShapeDtypeStruct((1, D), f32)] + (comm.out_shapes if comm else []),
            scratch_shapes=comm.scratch() if comm else [],
            input_output_aliases=comm.io_aliases(4, 2) if comm else {},
            compiler_params=_params(("arbitrary",)), name=name,
        )(x, g, dh, dy, *(comm.ins if comm else []))
        return (res[0], res[1], list(res[2:])) if comm else res
    return pl.pallas_call(
        body, grid=(T // tm,), in_specs=[row, vec, row], out_specs=vec,
        out_shape=jax.ShapeDtypeStruct((1, D), f32),
        compiler_params=_params(("arbitrary",)), name=name,
    )(x, g, dh)


MAX_UNIT_UNROLL = 6


def _unit_unroll(trips):
    return max(u for u in range(1, MAX_UNIT_UNROLL + 1) if trips % u == 0)


def _rows(start, size, stride):
    return pl.ds(start, size) if stride == 1 else pl.ds(start, size, stride=stride)


def _attn_units(S, d, first, prev):
    nb = S // d // BLK

    def do_first(r, c):
        first(_rows(r, BLK, d))
        return c

    lax.fori_loop(0, d, do_first, 0, unroll=_unit_unroll(d))
    if nb > 1:
        def do_prev(u, c):
            r = u // (nb - 1)
            b = u % (nb - 1) + 1
            base = r + d * b * BLK
            prev(_rows(base, BLK, d), _rows(base - d * BLK, 2 * BLK, d))
            return c

        lax.fori_loop(0, d * (nb - 1), do_prev, 0, unroll=_unit_unroll(d * (nb - 1)))


def _band_bias(nkeys):
    i = lax.broadcasted_iota(jnp.int32, (BLK, nkeys), 0)
    j = lax.broadcasted_iota(jnp.int32, (BLK, nkeys), 1)
    ok = (j <= i) if nkeys == BLK else ((j >= i) & (j <= i + BLK))
    return jnp.where(ok, 0.0, -jnp.inf).astype(f32)


def _normalize_into(src_ref, gain, dst_ref, S, rstd_ref=None):
    for c in range(S // 256):
        rows = pl.ds(c * 256, 256)
        v = src_ref[rows, :]
        r = _rstd_head(v)
        dst_ref[rows, :] = v * r * gain
        if rstd_ref is not None:
            rstd_ref[rows, :] = r


def _attn_fwd(proj, gq, gk, B, S, comm=None):
    T, NC = proj.shape
    scale = HEAD_DIM ** -0.5
    n_ci = len(comm.ins) if comm else 0
    n_co = len(comm.out_shapes) if comm else 0

    def body(*refs):
        q_ref, k_ref, v_ref, z_ref, gq_ref, gk_ref = refs[:6]
        c_ins = refs[6:6 + n_ci]
        ag_ref, a_ref, lse_ref = refs[6 + n_ci:9 + n_ci]
        c_outs = refs[9 + n_ci:9 + n_ci + n_co]
        qs, ks, o0, o1, o2, l0, l1, l2, bias1, bias2 = refs[9 + n_ci + n_co:19 + n_ci + n_co]
        sems = refs[19 + n_ci + n_co:]
        g = pl.program_id(2)
        if comm:
            @pl.when((pl.program_id(0) == 0) & (pl.program_id(1) == 0) & (g == 0))
            def _():
                for cp in comm.copies(c_ins, c_outs, *sems):
                    cp.start()

        bias1[...] = _band_bias(BLK)
        bias2[...] = _band_bias(2 * BLK)
        _normalize_into(q_ref, gq_ref[pl.ds(g, 1), :], qs, S)
        _normalize_into(k_ref, gk_ref[pl.ds(g, 1), :], ks, S)
        o_s, l_s = (o0, o1, o2), (l0, l1, l2)

        def run(gi):
            def unit(qr, kr, nkeys):
                s = _dot(qs[qr, :], ks[kr, :], NT_DIMS) * scale + (bias1 if nkeys == BLK else bias2)[...]
                m = jnp.max(s, axis=-1, keepdims=True)
                p = jnp.exp(s - m)
                den = jnp.sum(p, axis=-1, keepdims=True)
                o_s[gi][qr, :] = _dot(p, v_ref[kr, :], NN_DIMS) * (1.0 / den)
                l_s[gi][qr, :] = jnp.broadcast_to(m + jnp.log(den), (BLK, HEAD_DIM))

            _attn_units(S, DILATIONS[gi], lambda qr: unit(qr, qr, BLK), lambda qr, kr: unit(qr, kr, 2 * BLK))

        for gi in range(N_GROUPS):
            pl.when(g == gi)(functools.partial(run, gi))

        @pl.when(g == N_GROUPS - 1)
        def _():
            for c in range(S // 256):
                rows = pl.ds(c * 256, 256)
                la, lb, lc = l0[rows, :], l1[rows, :], l2[rows, :]
                m = jnp.maximum(jnp.maximum(la, lb), lc)
                wa, wb, wc = jnp.exp(la - m), jnp.exp(lb - m), jnp.exp(lc - m)
                tot = wa + wb + wc
                a = (wa / tot) * o0[rows, :] + (wb / tot) * o1[rows, :] + (wc / tot) * o2[rows, :]
                z = z_ref[rows, :]
                a_ref[rows, :] = a
                lse_ref[rows, :] = m + jnp.log(tot)
                ag_ref[rows, :] = (a * (z * _sigmoid(z))).astype(bf16)

        if comm:
            @pl.when((pl.program_id(0) == B - 1) & (pl.program_id(1) == HEADS - 1) & (g == N_GROUPS - 1))
            def _():
                for cp in comm.copies(c_ins, c_outs, *sems):
                    cp.wait()

    def slab(col0):
        return pl.BlockSpec((S, HEAD_DIM), lambda b, h, g: (b, col0 // HEAD_DIM + g * HEADS + h))

    gain = pl.BlockSpec((N_GROUPS, HEAD_DIM), lambda b, h, g: (0, 0))
    out = pl.BlockSpec((S, HEAD_DIM), lambda b, h, g: (b, h))
    hbm = pl.BlockSpec(memory_space=pl.ANY)
    res = pl.pallas_call(
        body, grid=(B, HEADS, N_GROUPS),
        in_specs=[slab(Q0), slab(K0), slab(V0), pl.BlockSpec((S, HEAD_DIM), lambda b, h, g: (b, ZA // HEAD_DIM + h)), gain, gain]
        + [hbm] * n_ci,
        out_specs=[out, out, out] + [hbm] * n_co,
        out_shape=[jax.ShapeDtypeStruct((T, ATTN_OUT), bf16), jax.ShapeDtypeStruct((T, ATTN_OUT), f32),
                   jax.ShapeDtypeStruct((T, ATTN_OUT), f32)] + (comm.out_shapes if comm else []),
        scratch_shapes=[pltpu.VMEM((S, HEAD_DIM), f32)] * 8 + [pltpu.VMEM((BLK, BLK), f32), pltpu.VMEM((BLK, 2 * BLK), f32)]
        + (comm.scratch() if comm else []),
        compiler_params=_params(("arbitrary", "arbitrary", "arbitrary")), name="attn_fwd",
    )(proj, proj, proj, proj, gq, gk, *(comm.ins if comm else []))
    return res[0], res[1], res[2], list(res[3:])


def _rms_bwd_rows(raw, gain, dn, on_mxu=True, r=None):
    if r is None:
        r = _rstd_head(raw) if on_mxu else _rstd(raw)
    xh = raw * r
    dxh = dn * gain
    if on_mxu:
        mean = _row_sum(dxh * xh) * (1.0 / raw.shape[1])
    else:
        mean = jnp.mean(dxh * xh, axis=-1, keepdims=True)
    return r * (dxh - xh * mean), jnp.sum(dn * xh, axis=0, keepdims=True)


def _attn_bwd(proj, gq, gk, a_comb, lse, dag, dproj, B, S):
    T, NC = proj.shape
    scale = HEAD_DIM ** -0.5

    def body(q_ref, k_ref, v_ref, z_ref, gq_ref, gk_ref, a_ref, lse_ref, dag_ref, dproj_in, dproj_ref, dgq_ref, dgk_ref,
             qs, ks, da_s, dl_s, dq_s, dk_s, dv_s, rq_s, rk_s, bias1, bias2, stage, dz_stage, sem, dz_sem):
        b, h, g = pl.program_id(0), pl.program_id(1), pl.program_id(2)
        bias1[...] = _band_bias(BLK)
        bias2[...] = _band_bias(2 * BLK)
        gq_row, gk_row = gq_ref[pl.ds(g, 1), :], gk_ref[pl.ds(g, 1), :]
        _normalize_into(q_ref, gq_row, qs, S, rq_s)
        _normalize_into(k_ref, gk_row, ks, S, rk_s)

        def dst(c):
            return dproj_ref.at[pl.ds(b * S, S), pl.ds((Q0, K0, V0)[c] + (g * HEADS + h) * HEAD_DIM, HEAD_DIM)]

        out = _Deferred(stage, sem, (b * HEADS + h) * N_GROUPS + g, B * HEADS * N_GROUPS - 1, 3)
        out.begin(dst)

        @pl.when((b == 0) & (h == 0) & (g == 0))
        def _():
            dgq_ref[...] = jnp.zeros_like(dgq_ref)
            dgk_ref[...] = jnp.zeros_like(dgk_ref)

        @pl.when(g == 0)
        def _():
            for c in range(S // 256):
                rows = pl.ds(c * 256, 256)
                z, a, dg_ = z_ref[rows, :], a_ref[rows, :], dag_ref[rows, :]
                sg = _sigmoid(z)
                da = dg_ * (z * sg)
                da_s[rows, :] = da
                dl_s[rows, :] = _row_sum(da * a)
                dz_stage[rows, :] = (dg_ * a * (sg * (1.0 + z * (1.0 - sg)))).astype(bf16)
            cp = pltpu.make_async_copy(dz_stage, dproj_ref.at[pl.ds(b * S, S), pl.ds(ZA + h * HEAD_DIM, HEAD_DIM)], dz_sem)
            cp.start()
            cp.wait()

        dk_s[...] = jnp.zeros_like(dk_s)
        dv_s[...] = jnp.zeros_like(dv_s)

        def run(gi):
            def unit(qr, kr, nkeys):
                q, k, v = qs[qr, :], ks[kr, :], v_ref[kr, :]
                s = _dot(q, k, NT_DIMS) * scale
                p = jnp.exp(s + (bias1 if nkeys == BLK else bias2)[...] - lse_ref[qr, :][:, :1])
                da = da_s[qr, :]
                dp = _dot(da, v, NT_DIMS)
                ds = p * (dp - dl_s[qr, :][:, :1]) * scale
                dq_s[qr, :] = _dot(ds, k, NN_DIMS)
                dk_s[kr, :] += _dot(ds, q, TN_DIMS)
                dv_s[kr, :] += _dot(p, da, TN_DIMS)

            _attn_units(S, DILATIONS[gi], lambda qr: unit(qr, qr, BLK), lambda qr, kr: unit(qr, kr, 2 * BLK))

        for gi in range(N_GROUPS):
            pl.when(g == gi)(functools.partial(run, gi))

        gq_acc = jnp.zeros((1, HEAD_DIM), f32)
        gk_acc = jnp.zeros((1, HEAD_DIM), f32)
        for c in range(S // 256):
            rows = pl.ds(c * 256, 256)
            dq, gq_p = _rms_bwd_rows(q_ref[rows, :], gq_row, dq_s[rows, :], r=rq_s[rows, :])
            dk, gk_p = _rms_bwd_rows(k_ref[rows, :], gk_row, dk_s[rows, :], r=rk_s[rows, :])
            gq_acc, gk_acc = gq_acc + gq_p, gk_acc + gk_p
            stage[out.slot, 0, rows, :] = dq.astype(bf16)
            stage[out.slot, 1, rows, :] = dk.astype(bf16)
            stage[out.slot, 2, rows, :] = dv_s[rows, :].astype(bf16)
        dgq_ref[pl.ds(g, 1), :] += gq_acc
        dgk_ref[pl.ds(g, 1), :] += gk_acc
        for c in range(3):
            out.start(c, dst(c))
        out.end(dst)

    def slab(col0):
        return pl.BlockSpec((S, HEAD_DIM), lambda b, h, g: (b, col0 // HEAD_DIM + g * HEADS + h))

    gain = pl.BlockSpec((N_GROUPS, HEAD_DIM), lambda b, h, g: (0, 0))
    per_slot = pl.BlockSpec((S, HEAD_DIM), lambda b, h, g: (b, h))
    return pl.pallas_call(
        body, grid=(B, HEADS, N_GROUPS),
        in_specs=[slab(Q0), slab(K0), slab(V0), pl.BlockSpec((S, HEAD_DIM), lambda b, h, g: (b, ZA // HEAD_DIM + h)), gain, gain,
                  per_slot, per_slot, per_slot, pl.BlockSpec(memory_space=pl.ANY)],
        out_specs=[pl.BlockSpec(memory_space=pl.ANY), gain, gain],
        out_shape=[jax.ShapeDtypeStruct(dproj.shape, dproj.dtype), jax.ShapeDtypeStruct((N_GROUPS, HEAD_DIM), f32),
                   jax.ShapeDtypeStruct((N_GROUPS, HEAD_DIM), f32)],
        scratch_shapes=[pltpu.VMEM((S, HEAD_DIM), f32)] * 9 + [pltpu.VMEM((BLK, BLK), f32), pltpu.VMEM((BLK, 2 * BLK), f32),
                                                                pltpu.VMEM((2, 3, S, HEAD_DIM), bf16), pltpu.VMEM((S, HEAD_DIM), bf16),
                                                                pltpu.SemaphoreType.DMA((2, 3)), pltpu.SemaphoreType.DMA],
        input_output_aliases={9: 0},
        compiler_params=_params(("arbitrary", "arbitrary", "arbitrary")), name="attn_bwd",
    )(proj, proj, proj, proj, gq, gk, a_comb, lse, dag, dproj)


def _conv_fwd(proj, conv_w, B, S):
    T, NC = proj.shape
    tc = LANES

    def body(cb_ref, cc_ref, cv_ref, z_ref, w_ref, c_ref, ub):
        u = cc_ref[...] * cv_ref[...]
        ub[0:8, :] = jnp.zeros((8, tc), f32)
        ub[8:8 + S, :] = u
        w = w_ref[...]
        y = w[0:1] * u + w[1:2] * ub[pl.ds(7, S), :] + w[2:3] * ub[pl.ds(6, S), :]
        z = z_ref[...]
        c_ref[...] = (cb_ref[...] * y * (z * _sigmoid(z))).astype(bf16)

    def seg(col0):
        return pl.BlockSpec((S, tc), lambda j, b: (b, col0 // tc + j))

    return pl.pallas_call(
        body, grid=(CONV_W // tc, B),
        in_specs=[seg(CB), seg(CC), seg(CV), seg(ZC), pl.BlockSpec((3, tc), lambda j, b: (0, j))],
        out_specs=pl.BlockSpec((S, tc), lambda j, b: (b, j)),
        out_shape=jax.ShapeDtypeStruct((T, CONV_W), bf16),
        scratch_shapes=[pltpu.VMEM((S + 8, tc), f32)],
        compiler_params=_params(("parallel", "parallel")), name="conv_fwd",
    )(proj, proj, proj, proj, conv_w)


def _conv_bwd(proj, conv_w, dc, dproj, B, S):
    T, NC = proj.shape
    tc = LANES

    def body(cb_ref, cc_ref, cv_ref, z_ref, w_ref, dc_ref, dproj_in, dproj_ref, dw_ref, ub, db, stage, sem):
        j, b = pl.program_id(0), pl.program_id(1)

        def dst(c):
            return dproj_ref.at[pl.ds(b * S, S), pl.ds((CB, CC, CV, ZC)[c] + j * tc, tc)]

        out = _Deferred(stage, sem, j * B + b, (CONV_W // tc) * B - 1, 4)
        out.begin(dst)
        cb, cc, cv, z, dcv = cb_ref[...], cc_ref[...], cv_ref[...], z_ref[...], dc_ref[...]
        u = cc * cv
        ub[0:8, :] = jnp.zeros((8, tc), f32)
        ub[8:8 + S, :] = u
        u1, u2 = ub[pl.ds(7, S), :], ub[pl.ds(6, S), :]
        w = w_ref[...]
        y = w[0:1] * u + w[1:2] * u1 + w[2:3] * u2
        sg = _sigmoid(z)
        si = z * sg
        dyv = dcv * cb * si
        db[0:S, :] = dyv
        db[S:S + 8, :] = jnp.zeros((8, tc), f32)
        du = w[0:1] * dyv + w[1:2] * db[pl.ds(1, S), :] + w[2:3] * db[pl.ds(2, S), :]
        stage[out.slot, 0] = (dcv * y * si).astype(bf16)
        stage[out.slot, 1] = (du * cv).astype(bf16)
        stage[out.slot, 2] = (du * cc).astype(bf16)
        stage[out.slot, 3] = (dcv * cb * y * (sg * (1.0 + z * (1.0 - sg)))).astype(bf16)
        for c in range(4):
            out.start(c, dst(c))
        part = jnp.concatenate([jnp.sum(dyv * u, axis=0, keepdims=True), jnp.sum(dyv * u1, axis=0, keepdims=True),
                                jnp.sum(dyv * u2, axis=0, keepdims=True)], axis=0)

        @pl.when(b == 0)
        def _():
            dw_ref[...] = part

        @pl.when(b > 0)
        def _():
            dw_ref[...] += part

        out.end(dst)

    def seg(col0):
        return pl.BlockSpec((S, tc), lambda j, b: (b, col0 // tc + j))

    return pl.pallas_call(
        body, grid=(CONV_W // tc, B),
        in_specs=[seg(CB), seg(CC), seg(CV), seg(ZC), pl.BlockSpec((3, tc), lambda j, b: (0, j)),
                  pl.BlockSpec((S, tc), lambda j, b: (b, j)), pl.BlockSpec(memory_space=pl.ANY)],
        out_specs=[pl.BlockSpec(memory_space=pl.ANY), pl.BlockSpec((3, tc), lambda j, b: (0, j))],
        out_shape=[jax.ShapeDtypeStruct(dproj.shape, dproj.dtype), jax.ShapeDtypeStruct((3, CONV_W), f32)],
        scratch_shapes=[pltpu.VMEM((S + 8, tc), f32), pltpu.VMEM((S + 8, tc), f32), pltpu.VMEM((2, 4, S, tc), bf16),
                        pltpu.SemaphoreType.DMA((2, 4))],
        input_output_aliases={6: 0},
        compiler_params=_params(("arbitrary", "arbitrary")), name="conv_bwd",
    )(proj, proj, proj, proj, conv_w, dc, dproj)


MEM_CHUNK = 1024


def _mem_fwd(proj, mkv, gq, gk, B, S):
    T, NC = proj.shape
    scale = MEM_HD ** -0.5

    def body(q_ref, z_ref, k_ref, v_ref, gq_ref, gk_ref, o_ref):
        kv = k_ref[...]
        kn = (kv * _rstd_head(kv) * gk_ref[...]).astype(bf16)
        vv = v_ref[...].astype(bf16)

        def chunk(c, carry):
            rows = pl.ds(pl.multiple_of(c * MEM_CHUNK, MEM_CHUNK), MEM_CHUNK)
            q = q_ref[rows, :]
            qn = q * _rstd(q) * gq_ref[...]
            s = _dot(qn, kn, NT_DIMS) * scale
            p = jnp.exp(s - jnp.max(s, axis=-1, keepdims=True))
            p = p / jnp.sum(p, axis=-1, keepdims=True)
            z = z_ref[rows, :]
            o_ref[rows, :] = (_dot(p, vv, NN_DIMS) * (z * _sigmoid(z))).astype(bf16)
            return carry

        lax.fori_loop(0, S // MEM_CHUNK, chunk, 0)

    gain = pl.BlockSpec((1, MEM_HD), lambda b, h: (0, 0))
    return pl.pallas_call(
        body, grid=(B, MEM_HEADS),
        in_specs=[pl.BlockSpec((S, MEM_HD), lambda b, h: (b, MQ // MEM_HD + h)),
                  pl.BlockSpec((S, MEM_HD), lambda b, h: (b, ZM // MEM_HD + h)),
                  pl.BlockSpec((MEM_LEN, MEM_HD), lambda b, h: (b, h)),
                  pl.BlockSpec((MEM_LEN, MEM_HD), lambda b, h: (b, MEM_HEADS + h)), gain, gain],
        out_specs=pl.BlockSpec((S, MEM_HD), lambda b, h: (b, h)),
        out_shape=jax.ShapeDtypeStruct((T, MEM_W), bf16),
        compiler_params=_params(("parallel", "parallel")), name="mem_fwd",
    )(proj, proj, mkv, mkv, gq, gk)


def _mem_bwd(proj, mkv, gq, gk, dmo, dproj, B, S):
    T, NC = proj.shape
    scale = MEM_HD ** -0.5

    def body(q_ref, z_ref, k_ref, v_ref, gq_ref, gk_ref, dmo_ref, dproj_in, dproj_ref, dmk_ref, dmv_ref, dgq_ref, dgk_ref,
             dkn_s, dv_s, gq_s, stage, sem):
        b, h = pl.program_id(0), pl.program_id(1)

        def dst(c):
            return dproj_ref.at[pl.ds(b * S, S), pl.ds((MQ, ZM)[c] + h * MEM_HD, MEM_HD)]

        out = _Deferred(stage, sem, b * MEM_HEADS + h, B * MEM_HEADS - 1, 2)
        out.begin(dst)
        kv = k_ref[...]
        kn = (kv * _rstd_head(kv) * gk_ref[...]).astype(bf16)
        vv = v_ref[...].astype(bf16)
        dkn_s[...] = jnp.zeros_like(dkn_s)
        dv_s[...] = jnp.zeros_like(dv_s)
        gq_s[...] = jnp.zeros_like(gq_s)

        def chunk(c, carry):
            rows = pl.ds(pl.multiple_of(c * MEM_CHUNK, MEM_CHUNK), MEM_CHUNK)
            q = q_ref[rows, :]
            qn = q * _rstd(q) * gq_ref[...]
            s = _dot(qn, kn, NT_DIMS) * scale
            p = jnp.exp(s - jnp.max(s, axis=-1, keepdims=True))
            p = p / jnp.sum(p, axis=-1, keepdims=True)
            mo = _dot(p, vv, NN_DIMS)
            z, dg_ = z_ref[rows, :], dmo_ref[rows, :]
            sg = _sigmoid(z)
            do = dg_ * (z * sg)
            stage[out.slot, 1, rows, :] = (dg_ * mo * (sg * (1.0 + z * (1.0 - sg)))).astype(bf16)
            dp = _dot(do, vv, NT_DIMS)
            ds = p * (dp - jnp.sum(do * mo, axis=-1, keepdims=True)) * scale
            dq, gq_p = _rms_bwd_rows(q, gq_ref[...], _dot(ds, kn, NN_DIMS), on_mxu=False)
            stage[out.slot, 0, rows, :] = dq.astype(bf16)
            gq_s[...] += gq_p
            dkn_s[...] += _dot(ds, qn, TN_DIMS)
            dv_s[...] += _dot(p, do, TN_DIMS)
            return carry

        lax.fori_loop(0, S // MEM_CHUNK, chunk, 0)
        for c in range(2):
            out.start(c, dst(c))
        dk, gk_p = _rms_bwd_rows(kv, gk_ref[...], dkn_s[...])
        dmk_ref[...] = dk
        dmv_ref[...] = dv_s[...]

        @pl.when((b == 0) & (h == 0))
        def _():
            dgq_ref[...] = gq_s[...]
            dgk_ref[...] = gk_p

        @pl.when((b > 0) | (h > 0))
        def _():
            dgq_ref[...] += gq_s[...]
            dgk_ref[...] += gk_p

        out.end(dst)

    gain = pl.BlockSpec((1, MEM_HD), lambda b, h: (0, 0))
    kvb = pl.BlockSpec((MEM_LEN, MEM_HD), lambda b, h: (b, h))
    return pl.pallas_call(
        body, grid=(B, MEM_HEADS),
        in_specs=[pl.BlockSpec((S, MEM_HD), lambda b, h: (b, MQ // MEM_HD + h)),
                  pl.BlockSpec((S, MEM_HD), lambda b, h: (b, ZM // MEM_HD + h)),
                  kvb, pl.BlockSpec((MEM_LEN, MEM_HD), lambda b, h: (b, MEM_HEADS + h)), gain, gain,
                  pl.BlockSpec((S, MEM_HD), lambda b, h: (b, h)), pl.BlockSpec(memory_space=pl.ANY)],
        out_specs=[pl.BlockSpec(memory_space=pl.ANY), kvb, kvb, gain, gain],
        out_shape=[jax.ShapeDtypeStruct(dproj.shape, dproj.dtype), jax.ShapeDtypeStruct((B * MEM_LEN, MEM_W), f32),
                   jax.ShapeDtypeStruct((B * MEM_LEN, MEM_W), f32), jax.ShapeDtypeStruct((1, MEM_HD), f32),
                   jax.ShapeDtypeStruct((1, MEM_HD), f32)],
        scratch_shapes=[pltpu.VMEM((MEM_LEN, MEM_HD), f32), pltpu.VMEM((MEM_LEN, MEM_HD), f32), pltpu.VMEM((1, MEM_HD), f32),
                        pltpu.VMEM((2, 2, S, MEM_HD), bf16), pltpu.SemaphoreType.DMA((2, 2))],
        input_output_aliases={7: 0},
        compiler_params=_params(("arbitrary", "arbitrary")), name="mem_bwd",
    )(proj, proj, mkv, mkv, gq, gk, dmo, dproj)


def _merge_fwd(proj, ag, cg, mg, wa, wc, wm):
    T, NC = proj.shape
    D = wa.shape[1]
    tm, tn = _pick(T, 1024), _pick(D, 512)
    assert GT % tn == 0

    def body(ag_ref, cg_ref, mg_ref, wa_ref, wc_ref, wm_ref, g0_ref, g1_ref, g2_ref, mer_ref, ba_ref, bc_ref, bm_ref):
        ba = _dot(ag_ref[...], wa_ref[...], NN_DIMS)
        bc = _dot(cg_ref[...], wc_ref[...], NN_DIMS)
        bm = _dot(mg_ref[...], wm_ref[...], NN_DIMS)
        mer_ref[...] = (_sigmoid(g0_ref[...]) * ba + _sigmoid(g1_ref[...]) * bc + _sigmoid(g2_ref[...]) * bm).astype(bf16)
        ba_ref[...] = ba.astype(bf16)
        bc_ref[...] = bc.astype(bf16)
        bm_ref[...] = bm.astype(bf16)

    def act(w):
        return pl.BlockSpec((tm, w), lambda i, j: (i, 0))

    def wt(k):
        return pl.BlockSpec((k, tn), lambda i, j: (0, j))

    def gate(n):
        return pl.BlockSpec((tm, tn), lambda i, j: (i, (GT + n * D) // tn + j))

    out = pl.BlockSpec((tm, tn), lambda i, j: (i, j))
    return pl.pallas_call(
        body, grid=(T // tm, D // tn),
        in_specs=[act(ATTN_OUT), act(CONV_W), act(MEM_W), wt(ATTN_OUT), wt(CONV_W), wt(MEM_W), gate(0), gate(1), gate(2)],
        out_specs=[out] * 4, out_shape=[jax.ShapeDtypeStruct((T, D), bf16)] * 4,
        compiler_params=_params(("parallel", "parallel")), name="merge_fwd",
    )(ag, cg, mg, wa, wc, wm, proj, proj, proj)


def _out_loss(merged, w_out, x, tgt):
    T, D = x.shape
    tm, tn = _pick(T, 1024), _pick(D, 512)

    def body(m_ref, w_ref, x_ref, t_ref, dy_ref, dyb_ref, lp_ref):
        e = x_ref[...] + _dot(m_ref[...], w_ref[...], NN_DIMS) - t_ref[...]
        dy = e / D
        dy_ref[...] = dy
        dyb_ref[...] = dy.astype(bf16)
        part = jnp.full((1, 8, LANES), jnp.sum(e * e), f32)

        @pl.when(pl.program_id(1) == 0)
        def _():
            lp_ref[...] = part

        @pl.when(pl.program_id(1) > 0)
        def _():
            lp_ref[...] += part

    tile = pl.BlockSpec((tm, tn), lambda i, j: (i, j))
    return pl.pallas_call(
        body, grid=(T // tm, D // tn),
        in_specs=[pl.BlockSpec((tm, D), lambda i, j: (i, 0)), pl.BlockSpec((D, tn), lambda i, j: (0, j)), tile, tile],
        out_specs=[tile, tile, pl.BlockSpec((1, 8, LANES), lambda i, j: (i, 0, 0))],
        out_shape=[jax.ShapeDtypeStruct((T, D), f32), jax.ShapeDtypeStruct((T, D), bf16),
                   jax.ShapeDtypeStruct((T // tm, 8, LANES), f32)],
        compiler_params=_params(("parallel", "arbitrary")), name="out_loss",
    )(merged, w_out, x, tgt)


def _merge_bwd(proj, dyb, w_out, ba, bc, bm):
    T, NC = proj.shape
    D = w_out.shape[0]
    tm, tn = _pick(T, 1024), _pick(D, 512)
    assert GT % tn == 0

    def body(dy_ref, w_ref, ba_ref, bc_ref, bm_ref, g0_ref, g1_ref, g2_ref, da_ref, dc_ref, dm_ref, dproj_ref, stage, sem):
        i, j = pl.program_id(0), pl.program_id(1)

        def dst(n):
            return dproj_ref.at[pl.ds(i * tm, tm), pl.ds(GT + n * D + j * tn, tn)]

        out = _Deferred(stage, sem, i * (D // tn) + j, (T // tm) * (D // tn) - 1, 3)
        out.begin(dst)
        dmer = _dot(dy_ref[...], w_ref[...], NT_DIMS)
        for n, (g_ref, br_ref, o_ref) in enumerate(((g0_ref, ba_ref, da_ref), (g1_ref, bc_ref, dc_ref), (g2_ref, bm_ref, dm_ref))):
            sg = _sigmoid(g_ref[...])
            o_ref[...] = (sg * dmer).astype(bf16)
            stage[out.slot, n] = (dmer * br_ref[...].astype(f32) * (sg * (1.0 - sg))).astype(bf16)
            out.start(n, dst(n))
        out.end(dst)

    tile = pl.BlockSpec((tm, tn), lambda i, j: (i, j))

    def gate(n):
        return pl.BlockSpec((tm, tn), lambda i, j: (i, (GT + n * D) // tn + j))

    return pl.pallas_call(
        body, grid=(T // tm, D // tn),
        in_specs=[pl.BlockSpec((tm, D), lambda i, j: (i, 0)), pl.BlockSpec((tn, D), lambda i, j: (j, 0)), tile, tile, tile,
                  gate(0), gate(1), gate(2)],
        out_specs=[tile, tile, tile, pl.BlockSpec(memory_space=pl.ANY)],
        out_shape=[jax.ShapeDtypeStruct((T, D), bf16)] * 3 + [jax.ShapeDtypeStruct((T, NC), bf16)],
        scratch_shapes=[pltpu.VMEM((2, 3, tm, tn), bf16), pltpu.SemaphoreType.DMA((2, 3))],
        compiler_params=_params(("arbitrary", "arbitrary")), name="merge_bwd",
    )(dyb, w_out, ba, bc, bm, proj, proj, proj)


def _fwd_bwd_head(x2, mem2, t2, proj, attn, B, S, mem_norm_g, attn_q_norm, attn_k_norm, conv_w, mem_q_norm, mem_k_norm,
                  w_kv, w_a, w_c, w_m, w_out):
    D = x2.shape[1]
    mng = mem_norm_g.reshape(1, D)
    mqn, mkn = mem_q_norm.reshape(1, MEM_HD), mem_k_norm.reshape(1, MEM_HD)
    ag, a_comb, lse = attn

    mh = _rms_fwd(mem2, mng, "rms_mem")
    mkv = _mm(mh, w_kv, mode="nn", out_dtype=f32, name="mkv")
    cg = _conv_fwd(proj, conv_w, B, S)
    mg = _mem_fwd(proj, mkv, mqn, mkn, B, S)
    merged, ba, bc, bm = _merge_fwd(proj, ag, cg, mg, w_a, w_c, w_m)
    dy, dyb, lp = _out_loss(merged, w_out, x2, t2)
    sq_err = jnp.sum(lp[:, 0, 0])

    g = {}
    g["w_out"] = _mm(merged, dyb, mode="tn", out_dtype=f32, name="dw_out")
    dba, dbc, dbm, dproj = _merge_bwd(proj, dyb, w_out, ba, bc, bm)
    g["w_br_attn"] = _mm(ag, dba, mode="tn", out_dtype=f32, name="dw_br_attn")
    g["w_br_conv"] = _mm(cg, dbc, mode="tn", out_dtype=f32, name="dw_br_conv")
    g["w_br_mem"] = _mm(mg, dbm, mode="tn", out_dtype=f32, name="dw_br_mem")
    dag = _mm(dba, w_a, mode="nt", out_dtype=f32, name="d_attn_out")
    dcg = _mm(dbc, w_c, mode="nt", out_dtype=f32, name="d_conv_out")
    dmg = _mm(dbm, w_m, mode="nt", out_dtype=f32, name="d_mem_out")
    dproj, g["attn_q_norm"], g["attn_k_norm"] = _attn_bwd(proj, attn_q_norm, attn_k_norm, a_comb, lse, dag, dproj, B, S)
    dproj, g["conv_w"] = _conv_bwd(proj, conv_w, dcg, dproj, B, S)
    dproj, dmk, dmv, dgmq, dgmk = _mem_bwd(proj, mkv, mqn, mkn, dmg, dproj, B, S)
    g["mem_q_norm"], g["mem_k_norm"] = dgmq.reshape(MEM_HD), dgmk.reshape(MEM_HD)
    dmkv = jnp.concatenate([dmk, dmv], axis=1)
    dmh = _mm(dmkv, w_kv, mode="nt", out_dtype=f32, name="d_mem_h")
    g["mem_norm_g"] = _rms_bwd(mem2, mng, dmh, None, "rms_mem_bwd").reshape(D)
    return sq_err, g, dict(dy=dy, dproj=dproj, mh=mh, dmkv=dmkv)


MESH = pl.DeviceIdType.MESH
HBM = pl.BlockSpec(memory_space=pl.ANY)


def _me():
    x, y, c = lax.axis_index("x"), lax.axis_index("y"), lax.axis_index("c")
    return (x, y, c), 4 * x + 2 * y + c


def _peer(k):
    (x, y, c), _ = _me()
    p = (1 - x if k & 4 else x, 1 - y if k & 2 else y, 1 - c if k & 1 else c)
    return p, 4 * p[0] + 2 * p[1] + p[2]


def _window(ref, axis, size, idx):
    if axis is None:
        return ref.at[idx]
    if axis == 0:
        return ref.at[pl.ds(idx * size, size), :]
    return ref.at[:, pl.ds(idx * size, size)]


def _full_shape(s, axis):
    if axis is None:
        return (N_DEV,) + s.shape
    return tuple(N_DEV * d if i == axis else d for i, d in enumerate(s.shape))


OTHER_CHIPS = (4, 2, 6)


def _spread_comm(shards, axes):
    n = len(shards)

    def copies(ins, outs, send_sems, recv_sems, base=0):
        _, me = _me()
        out = []
        for a in range(n):
            mine = _window(outs[a], axes[a], None if axes[a] is None else shards[a].shape[axes[a]], me)
            out.append(pltpu.make_async_copy(ins[a], mine, send_sems.at[base + a, N_CHIP]))
            for k, dist in enumerate((1,) + OTHER_CHIPS):
                dev, _ = _peer(dist)
                out.append(pltpu.make_async_remote_copy(
                    src_ref=ins[a], dst_ref=mine, send_sem=send_sems.at[base + a, k], recv_sem=recv_sems.at[base + a, k],
                    device_id=dev, device_id_type=MESH))
        return out

    shapes = [jax.ShapeDtypeStruct(_full_shape(s, ax), s.dtype) for s, ax in zip(shards, axes)]
    return _Comm(shards, shapes, (n, N_CHIP + 1), copies)


def _forward_blocks(fulls, axes):
    n = len(fulls)
    sizes = [None if ax is None else f.shape[ax] // N_DEV for f, ax in zip(fulls, axes)]

    def body(*refs):
        outs = refs[n:2 * n]
        send_sems, recv_sems = refs[2 * n:]
        sibling, _ = _peer(1)
        copies = []
        for j, dist in enumerate(OTHER_CHIPS):
            _, held = _peer(dist)
            for a in range(n):
                block = _window(outs[a], axes[a], sizes[a], held)
                copies.append(pltpu.make_async_remote_copy(
                    src_ref=block, dst_ref=block, send_sem=send_sems.at[a, j], recv_sem=recv_sems.at[a, j],
                    device_id=sibling, device_id_type=MESH))
        for cp in copies:
            cp.start()
        for cp in copies:
            cp.wait()

    return pl.pallas_call(
        body, in_specs=[HBM] * n, out_specs=[HBM] * n,
        out_shape=[jax.ShapeDtypeStruct(f.shape, f.dtype) for f in fulls],
        scratch_shapes=[pltpu.SemaphoreType.DMA((n, len(OTHER_CHIPS))), pltpu.SemaphoreType.DMA((n, len(OTHER_CHIPS)))],
        input_output_aliases={a: a for a in range(n)},
        name="forward_blocks",
    )(*fulls)


def _shard_order():
    (x, y, c), me = _me()
    xs, ys, xo, yo = _peer(4)[1], _peer(2)[1], _peer(5)[1], _peer(3)[1]
    north = c == 1
    ids = [me, _peer(1)[1], jnp.where(north, xs, ys), jnp.where(north, yo, xo), jnp.where(north, ys, xs),
           jnp.where(north, xo, yo), _peer(6)[1], _peer(7)[1]]
    return jnp.stack(ids).astype(jnp.int32)


def _proj_gather(h, w_shard, order, comm, comm_at):
    T, K = h.shape
    C = w_shard.shape[1]
    tm = _pick(T, 1024)
    nm = T // tm
    ahead = max(nm - 2, 0)
    n_ci, n_co = len(comm.ins), len(comm.out_shapes)

    def body(*refs):
        order_ref, h_ref, ws_ref = refs[:3]
        c_ins = refs[3:3 + n_ci]
        o_ref, wf_ref = refs[3 + n_ci:5 + n_ci]
        c_outs = refs[5 + n_ci:5 + n_ci + n_co]
        wbuf, send_sems, recv_sems, own_sem, buf_sems, c_send, c_recv = refs[5 + n_ci + n_co:]
        t, i = pl.program_id(0), pl.program_id(1)

        @pl.when((t == comm_at) & (i == 0))
        def _():
            for cp in comm.copies(c_ins, c_outs, c_send, c_recv):
                cp.start()
        (x, y, c), me = _me()
        sibling, sib_id = _peer(1)
        chips = [_peer(dist) for dist in OTHER_CHIPS]

        def win(idx):
            return wf_ref.at[:, pl.ds(idx * C, C)]

        def copy(k, block, to, src=None):
            return pltpu.make_async_remote_copy(
                src_ref=win(block) if src is None else src, dst_ref=win(block),
                send_sem=send_sems.at[k], recv_sem=recv_sems.at[k], device_id=to, device_id_type=MESH)

        def fetch(tt, src):
            return pltpu.make_async_copy(src, wbuf.at[tt % 2], buf_sems.at[tt % 2])

        own = pltpu.make_async_copy(ws_ref, win(me), own_sem)
        (x_nbr, x_id), (y_nbr, y_id), (_, d_id) = chips

        def half(k, block, top, to):
            rows = wf_ref.at[pl.ds(0 if top else K // 2, K // 2), pl.ds(block * C, C)]
            return pltpu.make_async_remote_copy(src_ref=rows, dst_ref=rows, send_sem=send_sems.at[k], recv_sem=recv_sems.at[k],
                                                device_id=to, device_id_type=MESH)

        here = (x, y, c)

        def pass_on(from_x):
            if from_x:
                copy(4, x_id, sibling).start()
                half(8, x_id, False, y_nbr).start()
            else:
                copy(5, y_id, sibling).start()
                half(7, y_id, True, x_nbr).start()

        @pl.when((t == 0) & (i == 0))
        def _():
            fetch(0, ws_ref).start()
            own.start()
            copy(0, me, sibling, src=ws_ref).start()

        for tt in range(N_DEV):
            @pl.when((t == tt) & (i == 0))
            def _(tt=tt):
                fetch(tt, ws_ref).wait()

        o_ref[...] = _dot(h_ref[...], wbuf[t % 2], NN_DIMS)

        def schedule(first_x):
            on = c == (1 if first_x else 0)
            (f_dev, f_id), (g_dev, g_id) = ((x_nbr, x_id), (y_nbr, y_id)) if first_x else ((y_nbr, y_id), (x_nbr, x_id))
            kf, kg = (1, 2) if first_x else (2, 1)
            pf, pg = (4, 5) if first_x else (5, 4)

            @pl.when((t == 0) & (i == 0) & on)
            def _():
                copy(kf, me, f_dev, src=ws_ref).start()

            def event(nxt):
                if nxt == 1:
                    copy(0, sib_id, here).wait_recv()
                    return sib_id
                if nxt == 2:
                    copy(kf, f_id, here).wait_recv()
                    copy(kf, me, f_dev, src=ws_ref).wait_send()
                    copy(kg, me, g_dev, src=ws_ref).start()
                    pass_on(first_x)
                    return f_id
                if nxt == 3:
                    copy(pg, g_id + 1 - 2 * c, here).wait_recv()
                    return g_id + 1 - 2 * c
                if nxt == 4:
                    copy(kg, g_id, here).wait_recv()
                    pass_on(not first_x)
                    return g_id
                if nxt == 5:
                    copy(pf, f_id + 1 - 2 * c, here).wait_recv()
                    return f_id + 1 - 2 * c
                if nxt == 6:
                    half(7, d_id, True, here).wait_recv()
                    half(8, d_id, False, here).wait_recv()
                    copy(6, d_id, sibling).start()
                    return d_id
                copy(6, d_id + 1 - 2 * c, here).wait_recv()
                return d_id + 1 - 2 * c

            for tt in range(N_DEV - 1):
                @pl.when((t == tt) & (i == ahead) & on)
                def _(tt=tt):
                    fetch(tt + 1, win(event(tt + 1))).start()

            @pl.when((t == N_DEV - 1) & (i == nm - 1) & on)
            def _():
                copy(kg, me, g_dev, src=ws_ref).wait_send()

        schedule(True)
        schedule(False)

        @pl.when((t == N_DEV - 1) & (i == nm - 1))
        def _():
            copy(0, me, sibling, src=ws_ref).wait_send()
            half(7, y_id, True, x_nbr).wait_send()
            half(8, x_id, False, y_nbr).wait_send()
            for j, (_, dev_id) in enumerate(chips):
                copy(4 + j, dev_id, sibling).wait_send()
            own.wait()
            for cp in comm.copies(c_ins, c_outs, c_send, c_recv):
                cp.wait()

    hbm = pl.BlockSpec(memory_space=pl.ANY)
    res = pl.pallas_call(
        body,
        grid_spec=pltpu.PrefetchScalarGridSpec(
            num_scalar_prefetch=1, grid=(N_DEV, nm),
            in_specs=[pl.BlockSpec((tm, K), lambda t, i, order_ref: (i, 0)), hbm] + [hbm] * n_ci,
            out_specs=[pl.BlockSpec((tm, C), lambda t, i, order_ref: (i, order_ref[t])), hbm] + [hbm] * n_co,
            scratch_shapes=[pltpu.VMEM((2, K, C), bf16), pltpu.SemaphoreType.DMA((9,)), pltpu.SemaphoreType.DMA((9,)),
                            pltpu.SemaphoreType.DMA, pltpu.SemaphoreType.DMA((2,))] + comm.scratch()),
        out_shape=[jax.ShapeDtypeStruct((T, N_DEV * C), f32), jax.ShapeDtypeStruct((K, N_DEV * C), bf16)] + comm.out_shapes,
        compiler_params=_params(("arbitrary", "arbitrary")), name="proj_gather",
    )(order, h, w_shard, *comm.ins)
    return res[0], res[1], list(res[2:])


N_CHIP = 4


def _shard_shape(g, ax):
    return tuple(d // N_DEV if i == ax else d for i, d in enumerate(g.shape))


def _sibling_comm(grads, axes):
    n = len(grads)
    sizes = [g.shape[ax] // N_DEV for g, ax in zip(grads, axes)]

    def copies(ins, lands, send_sems, recv_sems, base=0):
        sibling, _ = _peer(1)
        out = []
        for j in range(N_CHIP):
            _, owner = _peer(2 * j + 1)
            for a in range(n):
                out.append(pltpu.make_async_remote_copy(
                    src_ref=_window(ins[a], axes[a], sizes[a], owner), dst_ref=lands[a].at[j],
                    send_sem=send_sems.at[base + a, j], recv_sem=recv_sems.at[base + a, j], device_id=sibling,
                    device_id_type=MESH))
        return out

    shapes = [jax.ShapeDtypeStruct((N_CHIP,) + _shard_shape(g, ax), g.dtype) for g, ax in zip(grads, axes)]
    return _Comm(grads, shapes, (n, N_CHIP), copies)


def _chips_comm(sums):
    n = len(sums)

    def copies(ins, lands, send_sems, recv_sems, base=0):
        out = []
        for j in range(1, N_CHIP):
            dev, _ = _peer(2 * j)
            for a in range(n):
                out.append(pltpu.make_async_remote_copy(
                    src_ref=ins[a].at[j - 1], dst_ref=lands[a].at[j - 1],
                    send_sem=send_sems.at[base + a, j - 1], recv_sem=recv_sems.at[base + a, j - 1], device_id=dev,
                    device_id_type=MESH))
        return out

    return _Comm(sums, [jax.ShapeDtypeStruct(s.shape, s.dtype) for s in sums], (n, N_CHIP), copies)


def _join(c1, c2):
    n1, m1, k1 = len(c1.ins), len(c1.out_shapes), c1.sem_shape[0]

    def copies(ins, lands, send_sems, recv_sems):
        return (c1.copies(ins[:n1], lands[:m1], send_sems, recv_sems, base=0)
                + c2.copies(ins[n1:], lands[m1:], send_sems, recv_sems, base=k1))

    aliases = {**c1.aliases, **{n1 + k: m1 + v for k, v in c2.aliases.items()}}
    return _Comm(c1.ins + c2.ins, c1.out_shapes + c2.out_shapes, (k1 + c2.sem_shape[0], N_CHIP), copies, aliases)


def _chips_first_hop(sums):
    n = len(sums)

    def copies(ins, outs, send_sems, recv_sems, base=0):
        lands, relays = outs[:n], outs[n:]
        (y_dev, _), (x_dev, _) = _peer(2), _peer(4)
        out = []
        for a in range(n):
            half = sums[a].shape[1] // 2
            for k, (src, dst, dev) in enumerate((
                    (ins[a].at[0], lands[a].at[0], y_dev), (ins[a].at[1], lands[a].at[1], x_dev),
                    (ins[a].at[2, pl.ds(0, half)], relays[a].at[0], x_dev),
                    (ins[a].at[2, pl.ds(half, half)], relays[a].at[1], y_dev))):
                out.append(pltpu.make_async_remote_copy(
                    src_ref=src, dst_ref=dst, send_sem=send_sems.at[base + a, k], recv_sem=recv_sems.at[base + a, k],
                    device_id=dev, device_id_type=MESH))
        return out

    shapes = ([jax.ShapeDtypeStruct(s.shape, s.dtype) for s in sums]
              + [jax.ShapeDtypeStruct((2, s.shape[1] // 2) + s.shape[2:], s.dtype) for s in sums])
    return _Comm(sums, shapes, (n, N_CHIP), copies)


def _chips_relay(relays, lands):
    n = len(relays)

    def copies(ins, outs, send_sems, recv_sems, base=0):
        (y_dev, _), (x_dev, _) = _peer(2), _peer(4)
        out = []
        for a in range(n):
            half = relays[a].shape[1]
            for k, (rows, dev) in enumerate(((pl.ds(0, half), y_dev), (pl.ds(half, half), x_dev))):
                out.append(pltpu.make_async_remote_copy(
                    src_ref=ins[a].at[k], dst_ref=outs[a].at[2, rows], send_sem=send_sems.at[base + a, k],
                    recv_sem=recv_sems.at[base + a, k], device_id=dev, device_id_type=MESH))
        return out

    return _Comm(list(relays) + list(lands), [jax.ShapeDtypeStruct(l.shape, l.dtype) for l in lands], (n, N_CHIP), copies,
                 aliases={n + a: a for a in range(n)})


def _rs_chip_sum(grad, land, xyc, axis, name):
    R, C = land.shape[1:]
    tr = _pick(R, max(8, (640 * 1024) // C), 8)

    def body(xyc_ref, g_ref, l_ref, o_ref):
        o_ref[0] = (g_ref[...] + l_ref[0]).astype(bf16)

    def owner(j, xyc_ref):
        jj = j + 1
        return 4 * (xyc_ref[0] ^ (jj >> 1)) + 2 * (xyc_ref[1] ^ (jj & 1)) + xyc_ref[2]

    if axis == 0:
        g_spec = pl.BlockSpec((tr, C), lambda j, i, xyc_ref: (owner(j, xyc_ref) * (R // tr) + i, 0))
    else:
        g_spec = pl.BlockSpec((tr, C), lambda j, i, xyc_ref: (i, owner(j, xyc_ref)))
    return pl.pallas_call(
        body,
        grid_spec=pltpu.PrefetchScalarGridSpec(
            num_scalar_prefetch=1, grid=(N_CHIP - 1, R // tr),
            in_specs=[g_spec, pl.BlockSpec((1, tr, C), lambda j, i, xyc_ref: (j + 1, i, 0))],
            out_specs=pl.BlockSpec((1, tr, C), lambda j, i, xyc_ref: (j, i, 0))),
        out_shape=jax.ShapeDtypeStruct((N_CHIP - 1, R, C), bf16),
        compiler_params=_params(("parallel", "parallel")), name=name,
    )(xyc, grad, land)


def _allreduce_small(part):
    R = part.shape[0]

    def body(p_ref, o_ref, buf, send_sems, recv_sems):
        (x, y, c), me = _me()
        buf[me] = p_ref[...]
        copies = []
        for k in range(1, N_DEV):
            dev, dev_id = _peer(k)
            copies.append(pltpu.make_async_remote_copy(
                src_ref=p_ref, dst_ref=buf.at[me], send_sem=send_sems.at[k - 1], recv_sem=recv_sems.at[k - 1],
                device_id=dev, device_id_type=MESH))
        for cp in copies:
            cp.start()
        for k in range(1, N_DEV):
            _, dev_id = _peer(k)
            pltpu.make_async_remote_copy(
                src_ref=p_ref, dst_ref=buf.at[dev_id], send_sem=send_sems.at[k - 1], recv_sem=recv_sems.at[k - 1],
                device_id=(x, y, c), device_id_type=MESH).wait_recv()
        for cp in copies:
            cp.wait_send()
        acc = buf[0]
        for d in range(1, N_DEV):
            acc = acc + buf[d]
        o_ref[...] = acc

    return pl.pallas_call(
        body, in_specs=[pl.BlockSpec(memory_space=pltpu.VMEM)], out_specs=pl.BlockSpec(memory_space=pltpu.VMEM),
        out_shape=jax.ShapeDtypeStruct((R, LANES), f32),
        scratch_shapes=[pltpu.VMEM((N_DEV, R, LANES), f32), pltpu.SemaphoreType.DMA((N_DEV - 1,)), pltpu.SemaphoreType.DMA((N_DEV - 1,))],
        name="allreduce_small",
    )(part)


def _adamw_math(w, g, m, v):
    m2 = ADAM_B1 * m + (1.0 - ADAM_B1) * g
    v2 = ADAM_B2 * v + (1.0 - ADAM_B2) * (g * g)
    m_hat = m2 / (1.0 - ADAM_B1 ** ADAM_STEP)
    v_hat = v2 / (1.0 - ADAM_B2 ** ADAM_STEP)
    return -ADAM_LR * (m_hat / (jnp.sqrt(v_hat) + ADAM_EPS) + ADAM_WD * w), m2, v2


def _adamw_shard(grad, land_sib, land_chips, w, m, v, me, axis, name, row0=0, prev=None):
    R, C = land_sib.shape[1:]
    assert axis == 1 or R == w.shape[0]
    tr = _pick(R, max(8, (320 * 1024) // C), 8)
    r0 = row0 // tr
    n_prev = len(prev) if prev else 0

    def body(me_ref, g_ref, s_ref, l_ref, w_ref, m_ref, v_ref, *rest):
        go_ref, d_ref, mo_ref, vo_ref = rest[n_prev:]
        g = g_ref[...] + s_ref[0]
        for j in range(N_CHIP - 1):
            g = g + l_ref[j].astype(f32)
        d, m2, v2 = _adamw_math(w_ref[...], g, m_ref[...], v_ref[...])
        go_ref[...] = g
        d_ref[...] = d
        mo_ref[...] = m2
        vo_ref[...] = v2

    if axis == 0:
        g_spec = pl.BlockSpec((tr, C), lambda i, me_ref: (me_ref[0] * (R // tr) + i, 0))
    else:
        g_spec = pl.BlockSpec((tr, C), lambda i, me_ref: (i, me_ref[0]))
    blk = pl.BlockSpec((tr, C), lambda i, me_ref: (r0 + i, 0))
    return pl.pallas_call(
        body,
        grid_spec=pltpu.PrefetchScalarGridSpec(
            num_scalar_prefetch=1, grid=(R // tr,),
            in_specs=[g_spec, pl.BlockSpec((1, tr, C), lambda i, me_ref: (0, i, 0)),
                      pl.BlockSpec((N_CHIP - 1, tr, C), lambda i, me_ref: (0, i, 0)), blk, blk, blk]
            + [pl.BlockSpec(memory_space=pl.ANY)] * n_prev,
            out_specs=[blk] * 4),
        out_shape=[jax.ShapeDtypeStruct(w.shape, f32)] * 4,
        input_output_aliases={7 + i: i for i in range(n_prev)},
        compiler_params=_params(("parallel",)), name=name,
    )(me, grad, land_sib, land_chips, w, m, v, *(prev or []))


def _adamw_small(g, w, m, v):
    def body(g_ref, w_ref, m_ref, v_ref, d_ref, mo_ref, vo_ref):
        d, m2, v2 = _adamw_math(w_ref[...], g_ref[...], m_ref[...], v_ref[...])
        d_ref[...] = d
        mo_ref[...] = m2
        vo_ref[...] = v2

    return pl.pallas_call(body, out_shape=[jax.ShapeDtypeStruct(g.shape, f32)] * 3, name="adamw_small")(g, w, m, v)


def _pack_rows(parts, total_rows):
    rows = jnp.concatenate([p.reshape(-1, LANES) for p in parts], axis=0)
    return jnp.pad(rows, ((0, total_rows - rows.shape[0]), (0, 0)))


BIG = ("w_in", "mem_w_kv", "w_br_attn", "w_br_conv", "w_br_mem", "w_out")
BIG_AXIS = {"w_in": 1, "mem_w_kv": 0, "w_br_attn": 1, "w_br_conv": 1, "w_br_mem": 1, "w_out": 0}
SMALL = ("norm_g", "mem_norm_g", "attn_q_norm", "attn_k_norm", "mem_q_norm", "mem_k_norm")
ALL_W = ("norm_g", "mem_norm_g", "w_in", "attn_q_norm", "attn_k_norm", "conv_w", "mem_w_kv", "mem_q_norm", "mem_k_norm",
         "w_br_attn", "w_br_conv", "w_br_mem", "w_out")


def kernel(x, mem, norm_g, mem_norm_g, w_in, attn_q_norm, attn_k_norm, conv_w, mem_w_kv, mem_q_norm, mem_k_norm, w_br_attn, w_br_conv, w_br_mem, w_out, loss_target, m_norm_g, m_mem_norm_g, m_w_in, m_attn_q_norm, m_attn_k_norm, m_conv_w, m_mem_w_kv, m_mem_q_norm, m_mem_k_norm, m_w_br_attn, m_w_br_conv, m_w_br_mem, m_w_out, v_norm_g, v_mem_norm_g, v_w_in, v_attn_q_norm, v_attn_k_norm, v_conv_w, v_mem_w_kv, v_mem_q_norm, v_mem_k_norm, v_w_br_attn, v_w_br_conv, v_w_br_mem, v_w_out):
    w = dict(norm_g=norm_g, mem_norm_g=mem_norm_g, w_in=w_in, attn_q_norm=attn_q_norm, attn_k_norm=attn_k_norm, conv_w=conv_w,
             mem_w_kv=mem_w_kv, mem_q_norm=mem_q_norm, mem_k_norm=mem_k_norm, w_br_attn=w_br_attn, w_br_conv=w_br_conv,
             w_br_mem=w_br_mem, w_out=w_out)
    mo = dict(norm_g=m_norm_g, mem_norm_g=m_mem_norm_g, w_in=m_w_in, attn_q_norm=m_attn_q_norm, attn_k_norm=m_attn_k_norm,
              conv_w=m_conv_w, mem_w_kv=m_mem_w_kv, mem_q_norm=m_mem_q_norm, mem_k_norm=m_mem_k_norm, w_br_attn=m_w_br_attn,
              w_br_conv=m_w_br_conv, w_br_mem=m_w_br_mem, w_out=m_w_out)
    vo = dict(norm_g=v_norm_g, mem_norm_g=v_mem_norm_g, w_in=v_w_in, attn_q_norm=v_attn_q_norm, attn_k_norm=v_attn_k_norm,
              conv_w=v_conv_w, mem_w_kv=v_mem_w_kv, mem_q_norm=v_mem_q_norm, mem_k_norm=v_mem_k_norm, w_br_attn=v_w_br_attn,
              w_br_conv=v_w_br_conv, w_br_mem=v_w_br_mem, w_out=v_w_out)
    B, S, D = x.shape
    me = (4 * lax.axis_index("x") + 2 * lax.axis_index("y") + lax.axis_index("c")).astype(jnp.int32)

    T = B * S
    x2, t2, mem2, ng = x.reshape(T, D), loss_target.reshape(T, D), mem.reshape(B * MEM_LEN, D), norm_g.reshape(1, D)
    h = _rms_fwd(x2, ng, "rms_x")
    rows_sharded, cols_sharded = ("mem_w_kv", "w_out"), ("w_br_attn", "w_br_conv", "w_br_mem")
    later = rows_sharded + cols_sharded
    later_axes = [BIG_AXIS[n] for n in later] + [None]
    conv_pad = jnp.pad(conv_w, ((0, 8 - conv_w.shape[0]), (0, 0)))
    proj, w_in_full, spread_a = _proj_gather(
        h, w_in.astype(bf16), _shard_order(),
        _spread_comm([w[n].astype(bf16) for n in rows_sharded], [BIG_AXIS[n] for n in rows_sharded]), comm_at=N_DEV - 3)
    *attn, spread_b = _attn_fwd(proj, attn_q_norm, attn_k_norm, B, S,
                                comm=_spread_comm([w[n].astype(bf16) for n in cols_sharded] + [conv_pad],
                                                  [BIG_AXIS[n] for n in cols_sharded] + [None]))
    *fulls, conv_g = _forward_blocks(spread_a + spread_b, later_axes)
    wf = dict(zip(later, fulls), w_in=w_in_full)
    conv_full = conv_g[:, :3, :].transpose(1, 0, 2).reshape(3, CONV_W)

    sq_err, g, t = _fwd_bwd_head(x2, mem2, t2, proj, attn, B, S, mem_norm_g, attn_q_norm, attn_k_norm, conv_full, mem_q_norm,
                                 mem_k_norm, wf["mem_w_kv"], wf["w_br_attn"], wf["w_br_conv"], wf["w_br_mem"], wf["w_out"])
    t.update(x2=x2, ng=ng, h=h)

    xyc = jnp.stack([lax.axis_index("x"), lax.axis_index("y"), lax.axis_index("c")]).astype(jnp.int32)
    me1 = me.reshape(1)
    early = ("w_out", "w_br_attn", "w_br_conv", "w_br_mem")
    g["mem_w_kv"], sib_early = _mm(t["mh"], t["dmkv"], mode="tn", out_dtype=f32, name="dw_kv",
                                   comm=_sibling_comm([g[n] for n in early], [BIG_AXIS[n] for n in early]))
    sums_early = [_rs_chip_sum(g[n], ls, xyc, BIG_AXIS[n], "rs_sum_" + n) for n, ls in zip(early, sib_early)]
    tm_w = _pick(D // 2, 1024)
    half = (D // 2) // tm_w
    g_top, (*chips_early, sib_kv) = _mm(t["h"], t["dproj"], mode="tn", out_dtype=f32, name="dw_in_top", tm=tm_w,
                                        m_tiles=(0, half),
                                        comm=_join(_chips_comm(sums_early), _sibling_comm([g["mem_w_kv"]], [0])))
    sum_kv = _rs_chip_sum(g["mem_w_kv"], sib_kv, xyc, 0, "rs_sum_mem_w_kv")
    g_bot, (chips_kv, sib_top) = _mm(t["h"], t["dproj"], mode="tn", out_dtype=f32, name="dw_in_bot", tm=tm_w,
                                     m_tiles=(half, half), comm=_join(_chips_comm([sum_kv]), _sibling_comm([g_top], [1])))
    sum_top = _rs_chip_sum(g_top, sib_top, xyc, 1, "rs_sum_w_in_top")
    tm_t = _pick(T // 2, 1024)
    halfm = (T // 2) // tm_t
    dh, (part_top, relay_top, sib_bot) = _mm(t["dproj"], wf["w_in"], mode="nt", out_dtype=f32, name="d_h_a", tm=tm_t, tk=D_H_TK,
                                             m_tiles=(0, halfm), into="new",
                                             comm=_join(_chips_first_hop([sum_top]), _sibling_comm([g_bot], [1])))
    sum_bot = _rs_chip_sum(g_bot, sib_bot, xyc, 1, "rs_sum_w_in_bot")
    dh, (part_bot, relay_bot, chips_top) = _mm(t["dproj"], wf["w_in"], mode="nt", out_dtype=f32, name="d_h_b", tm=tm_t,
                                               tk=D_H_TK, m_tiles=(halfm, halfm), into=dh,
                                               comm=_join(_chips_first_hop([sum_bot]), _chips_relay([relay_top], [part_top])))
    dx, dng, (chips_bot,) = _rms_bwd(t["x2"], t["ng"], dh, t["dy"], "rms_x_bwd", comm=_chips_relay([relay_bot], [part_bot]))
    g["norm_g"] = dng.reshape(D)

    grad, delta, new_m, new_v = {}, {}, {}, {}
    for n, ls, lc in zip(early + ("mem_w_kv",), sib_early + [sib_kv], chips_early + [chips_kv]):
        grad[n], delta[n], new_m[n], new_v[n] = _adamw_shard(g[n], ls, lc, w[n], mo[n], vo[n], me1, BIG_AXIS[n], "adamw_" + n)
    top = _adamw_shard(g_top, sib_top, chips_top, w_in, m_w_in, v_w_in, me1, 1, "adamw_w_in_top")
    grad["w_in"], delta["w_in"], new_m["w_in"], new_v["w_in"] = _adamw_shard(
        g_bot, sib_bot, chips_bot, w_in, m_w_in, v_w_in, me1, 1, "adamw_w_in_bot", row0=D // 2, prev=top)

    n_rep = sum(w[n].size for n in SMALL) // LANES
    conv_rows = g["conv_w"].reshape(3, N_DEV, LANES).transpose(1, 0, 2).reshape(3 * N_DEV, LANES)
    n_rows = -(-(n_rep + 3 * N_DEV + 1) // 8) * 8
    part = _pack_rows([g[n] for n in SMALL] + [conv_rows, jnp.full((1, LANES), sq_err, f32)], n_rows)
    tot = _allreduce_small(part)
    loss = tot[n_rep + 3 * N_DEV, 0] * (0.5 / D)
    n_small = -(-(n_rep + 3) // 8) * 8
    g_small = _pack_rows([tot[:n_rep], lax.dynamic_slice(tot, (n_rep + 3 * me, 0), (3, LANES))], n_small)
    names = SMALL + ("conv_w",)
    d_s, m_s, v_s = _adamw_small(g_small, _pack_rows([w[n] for n in names], n_small), _pack_rows([mo[n] for n in names], n_small),
                                 _pack_rows([vo[n] for n in names], n_small))
    off = 0
    for n in names:
        r = w[n].size // LANES
        grad[n], delta[n] = g_small[off:off + r].reshape(w[n].shape), d_s[off:off + r].reshape(w[n].shape)
        new_m[n], new_v[n] = m_s[off:off + r].reshape(w[n].shape), v_s[off:off + r].reshape(w[n].shape)
        off += r
    return (loss, dx.reshape(B, S, D), *[grad[n] for n in ALL_W], *[delta[n] for n in ALL_W], *[new_m[n] for n in ALL_W],
            *[new_v[n] for n in ALL_W])
```

```python
import functools

import jax
import jax.numpy as jnp
from jax import lax
from jax.experimental import pallas as pl
from jax.experimental.pallas import tpu as pltpu

f32 = jnp.float32
bf16 = jnp.bfloat16

N_DEV = 8
HEAD_DIM = 128
DILATIONS = (1, 4, 16)
N_GROUPS = 3
HEADS = 4
BLK = 128
ATTN_QKV = N_GROUPS * HEADS * HEAD_DIM
ATTN_OUT = HEADS * HEAD_DIM
CONV_W = 1024
MEM_LEN = 256
MEM_HEADS = 4
MEM_HD = 256
MEM_W = MEM_HEADS * MEM_HD
EPS = 1e-6
Q0, K0, V0 = 0, ATTN_QKV, 2 * ATTN_QKV
ZA = 3 * ATTN_QKV
CB, CC, CV, ZC = ZA + ATTN_OUT, ZA + ATTN_OUT + CONV_W, ZA + ATTN_OUT + 2 * CONV_W, ZA + ATTN_OUT + 3 * CONV_W
MQ = ZC + CONV_W
ZM = MQ + MEM_W
GT = ZM + MEM_W

ADAM_LR, ADAM_B1, ADAM_B2, ADAM_EPS, ADAM_WD, ADAM_STEP = 0.001, 0.9, 0.999, 1e-08, 0.01, 10

VMEM_LIMIT = 56 * 1024 * 1024
D_H_TK = 4352
LANES = 128

NT_DIMS = (((1,), (1,)), ((), ()))
TN_DIMS = (((0,), (0,)), ((), ()))
NN_DIMS = (((1,), (0,)), ((), ()))


def _params(sem=None):
    return pltpu.CompilerParams(dimension_semantics=sem, vmem_limit_bytes=VMEM_LIMIT)


def _pick(n, pref, q=LANES):
    t = (min(pref, n) // q) * q
    while t >= q:
        if n % t == 0:
            return t
        t -= q
    return n


def _dot(a, b, dims):
    return lax.dot_general(a.astype(bf16), b.astype(bf16), dims, preferred_element_type=f32)


def _sigmoid(z):
    return 0.5 * jnp.tanh(0.5 * z) + 0.5


def _rstd(v):
    return lax.rsqrt(jnp.mean(v * v, axis=-1, keepdims=True) + EPS)


class _Deferred:
    def __init__(self, stage, sems, step, last, n):
        assert last >= 1
        self.stage, self.sems, self.step, self.last, self.n = stage, sems, step, last, n
        self.slot = step % 2

    def _drain(self, slot, like):
        for c in range(self.n):
            pltpu.make_async_copy(self.stage.at[slot, c], like(c), self.sems.at[slot, c]).wait()

    def begin(self, like):
        pl.when(self.step >= 2)(lambda: self._drain(self.slot, like))

    def start(self, c, dst):
        pltpu.make_async_copy(self.stage.at[self.slot, c], dst, self.sems.at[self.slot, c]).start()

    def end(self, like):
        @pl.when(self.step == self.last)
        def _():
            self._drain(self.slot, like)
            self._drain(1 - self.slot, like)


def _row_sum(v):
    return _dot(v, jnp.ones((v.shape[1], v.shape[1]), bf16), NN_DIMS)


def _rstd_head(v):
    return lax.rsqrt(_row_sum(v * v) * (1.0 / v.shape[1]) + EPS)


class _Comm:
    def __init__(self, ins, out_shapes, sem_shape, copies, aliases=None):
        self.ins, self.out_shapes, self.sem_shape, self.copies = list(ins), list(out_shapes), sem_shape, copies
        self.aliases = dict(aliases or {})

    def scratch(self):
        return [pltpu.SemaphoreType.DMA(self.sem_shape), pltpu.SemaphoreType.DMA(self.sem_shape)]

    def io_aliases(self, first_in, first_out):
        return {first_in + k: first_out + v for k, v in self.aliases.items()}


def _mm(a, b, *, mode, out_dtype, name, tm=1024, tn=1024, tk=4096, m_tiles=None, into=None, comm=None):
    if mode == "nn":
        (M, K), (K2, N) = a.shape, b.shape
    elif mode == "nt":
        (M, K), (N, K2) = a.shape, b.shape
    else:
        (K, M), (K2, N) = a.shape, b.shape
    assert K == K2
    tm, tn, tk = _pick(M, tm), _pick(N, tn), _pick(K, tk)
    nk = K // tk
    m0, mc = m_tiles if m_tiles is not None else (0, M // tm)
    o0 = 0 if into is None else m0
    aliased = into is not None and not isinstance(into, str)
    n_ci = len(comm.ins) if comm else 0
    n_co = len(comm.out_shapes) if comm else 0
    grid = (mc, N // tn, nk)
    dims = {"nn": NN_DIMS, "nt": NT_DIMS, "tn": TN_DIMS}[mode]

    def body(*refs, nk):
        a_ref, b_ref = refs[:2]
        pos = 3 if aliased else 2
        c_ins, o_ref, c_outs = refs[pos:pos + n_ci], refs[pos + n_ci], refs[pos + n_ci + 1:pos + n_ci + 1 + n_co]
        scratch = refs[pos + n_ci + 1 + n_co:]
        ids = [pl.program_id(d) for d in range(3)]
        if comm:
            sems = scratch[-2:]

            @pl.when((ids[0] == 0) & (ids[1] == 0) & (ids[2] == 0))
            def _():
                for cp in comm.copies(c_ins, c_outs, *sems):
                    cp.start()

        if nk == 1:
            o_ref[...] = _dot(a_ref[...], b_ref[...], dims).astype(o_ref.dtype)
        else:
            acc_ref, k = scratch[0], ids[2]

            @pl.when(k == 0)
            def _():
                acc_ref[...] = _dot(a_ref[...], b_ref[...], dims)

            @pl.when((k > 0) & (k < nk - 1))
            def _():
                acc_ref[...] += _dot(a_ref[...], b_ref[...], dims)

            @pl.when(k == nk - 1)
            def _():
                o_ref[...] = (acc_ref[...] + _dot(a_ref[...], b_ref[...], dims)).astype(o_ref.dtype)

        if comm:
            @pl.when((ids[0] == grid[0] - 1) & (ids[1] == grid[1] - 1) & (ids[2] == grid[2] - 1))
            def _():
                for cp in comm.copies(c_ins, c_outs, *sems):
                    cp.wait()

    if mode == "tn":
        a_spec = pl.BlockSpec((tk, tm), lambda i, j, k: (k, m0 + i))
    else:
        a_spec = pl.BlockSpec((tm, tk), lambda i, j, k: (m0 + i, k))
    if mode == "nt":
        b_spec = pl.BlockSpec((tn, tk), lambda i, j, k: (j, k))
    else:
        b_spec = pl.BlockSpec((tk, tn), lambda i, j, k: (k, j))
    hbm = pl.BlockSpec(memory_space=pl.ANY)
    res = pl.pallas_call(
        functools.partial(body, nk=nk),
        grid=grid,
        in_specs=[a_spec, b_spec] + [hbm] * (aliased + n_ci),
        out_specs=[pl.BlockSpec((tm, tn), lambda i, j, k: (o0 + i, j))] + [hbm] * n_co,
        out_shape=[jax.ShapeDtypeStruct((mc * tm if into is None else M, N), out_dtype)] + (comm.out_shapes if comm else []),
        scratch_shapes=([pltpu.VMEM((tm, tn), f32)] if nk > 1 else []) + (comm.scratch() if comm else []),
        input_output_aliases={**({2: 0} if aliased else {}), **(comm.io_aliases(2 + aliased, 1) if comm else {})},
        compiler_params=_params(("arbitrary",) * 3 if comm else ("parallel", "parallel", "arbitrary")),
        name=name,
    )(a, b, *([into] if aliased else []), *(comm.ins if comm else []))
    return (res[0], list(res[1:])) if comm else res[0]


def _rms_fwd(x, g, name):
    T, D = x.shape
    tm = _pick(T, 256, 8)

    def body(x_ref, g_ref, h_ref):
        xv = x_ref[...]
        h_ref[...] = (xv * _rstd(xv) * g_ref[...]).astype(bf16)

    return pl.pallas_call(
        body, grid=(T // tm,),
        in_specs=[pl.BlockSpec((tm, D), lambda i: (i, 0)), pl.BlockSpec((1, D), lambda i: (0, 0))],
        out_specs=pl.BlockSpec((tm, D), lambda i: (i, 0)),
        out_shape=jax.ShapeDtypeStruct((T, D), bf16),
        compiler_params=_params(("parallel",)), name=name,
    )(x, g)


def _rms_bwd(x, g, dh, dy, name, comm=None):
    T, D = x.shape
    tm = _pick(T, 256, 8)
    with_dx = dy is not None
    n_ci = len(comm.ins) if comm else 0
    n_co = len(comm.out_shapes) if comm else 0

    def body(*refs):
        if with_dx:
            x_ref, g_ref, dh_ref, dy_ref = refs[:4]
            c_ins, (dx_ref, dg_ref) = refs[4:4 + n_ci], refs[4 + n_ci:6 + n_ci]
            c_outs, sems = refs[6 + n_ci:6 + n_ci + n_co], refs[6 + n_ci + n_co:]
        else:
            x_ref, g_ref, dh_ref, dg_ref = refs
        if comm:
            @pl.when(pl.program_id(0) == 0)
            def _():
                for cp in comm.copies(c_ins, c_outs, *sems):
                    cp.start()

        xv = x_ref[...]
        r = _rstd(xv)
        xh = xv * r
        dhv = dh_ref[...]
        part = jnp.sum(dhv * xh, axis=0, keepdims=True)

        @pl.when(pl.program_id(0) == 0)
        def _():
            dg_ref[...] = part

        @pl.when(pl.program_id(0) > 0)
        def _():
            dg_ref[...] += part

        if with_dx:
            dxh = dhv * g_ref[...]
            dx_ref[...] = dy_ref[...] + r * (dxh - xh * jnp.mean(dxh * xh, axis=-1, keepdims=True))

        if comm:
            @pl.when(pl.program_id(0) == T // tm - 1)
            def _():
                for cp in comm.copies(c_ins, c_outs, *sems):
                    cp.wait()

    row = pl.BlockSpec((tm, D), lambda i: (i, 0))
    vec = pl.BlockSpec((1, D), lambda i: (0, 0))
    hbm = pl.BlockSpec(memory_space=pl.ANY)
    if with_dx:
        res = pl.pallas_call(
            body, grid=(T // tm,), in_specs=[row, vec, row, row] + [hbm] * n_ci, out_specs=[row, vec] + [hbm] * n_co,
            out_shape=[jax.ShapeDtypeStruct((T, D), f32), jax.ShapeDtypeStruct((1, D), f32)] + (comm.out_shapes if comm else []),
            scratch_shapes=comm.scratch() if comm else [],
            input_output_aliases=comm.io_aliases(4, 2) if comm else {},
            compiler_params=_params(("arbitrary",)), name=name,
        )(x, g, dh, dy, *(comm.ins if comm else []))
        return (res[0], res[1], list(res[2:])) if comm else res
    return pl.pallas_call(
        body, grid=(T // tm,), in_specs=[row, vec, row], out_specs=vec,
        out_shape=jax.ShapeDtypeStruct((1, D), f32),
        compiler_params=_params(("arbitrary",)), name=name,
    )(x, g, dh)


MAX_UNIT_UNROLL = 6


def _unit_unroll(trips):
    return max(u for u in range(1, MAX_UNIT_UNROLL + 1) if trips % u == 0)


def _rows(start, size, stride):
    return pl.ds(start, size) if stride == 1 else pl.ds(start, size, stride=stride)


def _attn_units(S, d, first, prev):
    nb = S // d // BLK

    def do_first(r, c):
        first(_rows(r, BLK, d))
        return c

    lax.fori_loop(0, d, do_first, 0, unroll=_unit_unroll(d))
    if nb > 1:
        def do_prev(u, c):
            r = u // (nb - 1)
            b = u % (nb - 1) + 1
            base = r + d * b * BLK
            prev(_rows(base, BLK, d), _rows(base - d * BLK, 2 * BLK, d))
            return c

        lax.fori_loop(0, d * (nb - 1), do_prev, 0, unroll=_unit_unroll(d * (nb - 1)))


def _band_bias(nkeys):
    i = lax.broadcasted_iota(jnp.int32, (BLK, nkeys), 0)
    j = lax.broadcasted_iota(jnp.int32, (BLK, nkeys), 1)
    ok = (j <= i) if nkeys == BLK else ((j >= i) & (j <= i + BLK))
    return jnp.where(ok, 0.0, -jnp.inf).astype(f32)


def _normalize_into(src_ref, gain, dst_ref, S, rstd_ref=None):
    for c in range(S // 256):
        rows = pl.ds(c * 256, 256)
        v = src_ref[rows, :]
        r = _rstd_head(v)
        dst_ref[rows, :] = v * r * gain
        if rstd_ref is not None:
            rstd_ref[rows, :] = r


def _attn_fwd(proj, gq, gk, B, S, comm=None):
    T, NC = proj.shape
    scale = HEAD_DIM ** -0.5
    n_ci = len(comm.ins) if comm else 0
    n_co = len(comm.out_shapes) if comm else 0

    def body(*refs):
        q_ref, k_ref, v_ref, z_ref, gq_ref, gk_ref = refs[:6]
        c_ins = refs[6:6 + n_ci]
        ag_ref, a_ref, lse_ref = refs[6 + n_ci:9 + n_ci]
        c_outs = refs[9 + n_ci:9 + n_ci + n_co]
        qs, ks, o0, o1, o2, l0, l1, l2, bias1, bias2 = refs[9 + n_ci + n_co:19 + n_ci + n_co]
        sems = refs[19 + n_ci + n_co:]
        g = pl.program_id(2)
        if comm:
            @pl.when((pl.program_id(0) == 0) & (pl.program_id(1) == 0) & (g == 0))
            def _():
                for cp in comm.copies(c_ins, c_outs, *sems):
                    cp.start()

        bias1[...] = _band_bias(BLK)
        bias2[...] = _band_bias(2 * BLK)
        _normalize_into(q_ref, gq_ref[pl.ds(g, 1), :], qs, S)
        _normalize_into(k_ref, gk_ref[pl.ds(g, 1), :], ks, S)
        o_s, l_s = (o0, o1, o2), (l0, l1, l2)

        def run(gi):
            def unit(qr, kr, nkeys):
                s = _dot(qs[qr, :], ks[kr, :], NT_DIMS) * scale + (bias1 if nkeys == BLK else bias2)[...]
                m = jnp.max(s, axis=-1, keepdims=True)
                p = jnp.exp(s - m)
                den = jnp.sum(p, axis=-1, keepdims=True)
                o_s[gi][qr, :] = _dot(p, v_ref[kr, :], NN_DIMS) * (1.0 / den)
                l_s[gi][qr, :] = jnp.broadcast_to(m + jnp.log(den), (BLK, HEAD_DIM))

            _attn_units(S, DILATIONS[gi], lambda qr: unit(qr, qr, BLK), lambda qr, kr: unit(qr, kr, 2 * BLK))

        for gi in range(N_GROUPS):
            pl.when(g == gi)(functools.partial(run, gi))

        @pl.when(g == N_GROUPS - 1)
        def _():
            for c in range(S // 256):
                rows = pl.ds(c * 256, 256)
                la, lb, lc = l0[rows, :], l1[rows, :], l2[rows, :]
                m = jnp.maximum(jnp.maximum(la, lb), lc)
                wa, wb, wc = jnp.exp(la - m), jnp.exp(lb - m), jnp.exp(lc - m)
                tot = wa + wb + wc
                a = (wa / tot) * o0[rows, :] + (wb / tot) * o1[rows, :] + (wc / tot) * o2[rows, :]
                z = z_ref[rows, :]
                a_ref[rows, :] = a
                lse_ref[rows, :] = m + jnp.log(tot)
                ag_ref[rows, :] = (a * (z * _sigmoid(z))).astype(bf16)

        if comm:
            @pl.when((pl.program_id(0) == B - 1) & (pl.program_id(1) == HEADS - 1) & (g == N_GROUPS - 1))
            def _():
                for cp in comm.copies(c_ins, c_outs, *sems):
                    cp.wait()

    def slab(col0):
        return pl.BlockSpec((S, HEAD_DIM), lambda b, h, g: (b, col0 // HEAD_DIM + g * HEADS + h))

    gain = pl.BlockSpec((N_GROUPS, HEAD_DIM), lambda b, h, g: (0, 0))
    out = pl.BlockSpec((S, HEAD_DIM), lambda b, h, g: (b, h))
    hbm = pl.BlockSpec(memory_space=pl.ANY)
    res = pl.pallas_call(
        body, grid=(B, HEADS, N_GROUPS),
        in_specs=[slab(Q0), slab(K0), slab(V0), pl.BlockSpec((S, HEAD_DIM), lambda b, h, g: (b, ZA // HEAD_DIM + h)), gain, gain]
        + [hbm] * n_ci,
        out_specs=[out, out, out] + [hbm] * n_co,
        out_shape=[jax.ShapeDtypeStruct((T, ATTN_OUT), bf16), jax.ShapeDtypeStruct((T, ATTN_OUT), f32),
                   jax.ShapeDtypeStruct((T, ATTN_OUT), f32)] + (comm.out_shapes if comm else []),
        scratch_shapes=[pltpu.VMEM((S, HEAD_DIM), f32)] * 8 + [pltpu.VMEM((BLK, BLK), f32), pltpu.VMEM((BLK, 2 * BLK), f32)]
        + (comm.scratch() if comm else []),
        compiler_params=_params(("arbitrary", "arbitrary", "arbitrary")), name="attn_fwd",
    )(proj, proj, proj, proj, gq, gk, *(comm.ins if comm else []))
    return res[0], res[1], res[2], list(res[3:])


def _rms_bwd_rows(raw, gain, dn, on_mxu=True, r=None):
    if r is None:
        r = _rstd_head(raw) if on_mxu else _rstd(raw)
    xh = raw * r
    dxh = dn * gain
    if on_mxu:
        mean = _row_sum(dxh * xh) * (1.0 / raw.shape[1])
    else:
        mean = jnp.mean(dxh * xh, axis=-1, keepdims=True)
    return r * (dxh - xh * mean), jnp.sum(dn * xh, axis=0, keepdims=True)


def _attn_bwd(proj, gq, gk, a_comb, lse, dag, dproj, B, S):
    T, NC = proj.shape
    scale = HEAD_DIM ** -0.5

    def body(q_ref, k_ref, v_ref, z_ref, gq_ref, gk_ref, a_ref, lse_ref, dag_ref, dproj_in, dproj_ref, dgq_ref, dgk_ref,
             qs, ks, da_s, dl_s, dq_s, dk_s, dv_s, rq_s, rk_s, bias1, bias2, stage, dz_stage, sem, dz_sem):
        b, h, g = pl.program_id(0), pl.program_id(1), pl.program_id(2)
        bias1[...] = _band_bias(BLK)
        bias2[...] = _band_bias(2 * BLK)
        gq_row, gk_row = gq_ref[pl.ds(g, 1), :], gk_ref[pl.ds(g, 1), :]
        _normalize_into(q_ref, gq_row, qs, S, rq_s)
        _normalize_into(k_ref, gk_row, ks, S, rk_s)

        def dst(c):
            return dproj_ref.at[pl.ds(b * S, S), pl.ds((Q0, K0, V0)[c] + (g * HEADS + h) * HEAD_DIM, HEAD_DIM)]

        out = _Deferred(stage, sem, (b * HEADS + h) * N_GROUPS + g, B * HEADS * N_GROUPS - 1, 3)
        out.begin(dst)

        @pl.when((b == 0) & (h == 0) & (g == 0))
        def _():
            dgq_ref[...] = jnp.zeros_like(dgq_ref)
            dgk_ref[...] = jnp.zeros_like(dgk_ref)

        @pl.when(g == 0)
        def _():
            for c in range(S // 256):
                rows = pl.ds(c * 256, 256)
                z, a, dg_ = z_ref[rows, :], a_ref[rows, :], dag_ref[rows, :]
                sg = _sigmoid(z)
                da = dg_ * (z * sg)
                da_s[rows, :] = da
                dl_s[rows, :] = _row_sum(da * a)
                dz_stage[rows, :] = (dg_ * a * (sg * (1.0 + z * (1.0 - sg)))).astype(bf16)
            cp = pltpu.make_async_copy(dz_stage, dproj_ref.at[pl.ds(b * S, S), pl.ds(ZA + h * HEAD_DIM, HEAD_DIM)], dz_sem)
            cp.start()
            cp.wait()

        dk_s[...] = jnp.zeros_like(dk_s)
        dv_s[...] = jnp.zeros_like(dv_s)

        def run(gi):
            def unit(qr, kr, nkeys):
                q, k, v = qs[qr, :], ks[kr, :], v_ref[kr, :]
                s = _dot(q, k, NT_DIMS) * scale
                p = jnp.exp(s + (bias1 if nkeys == BLK else bias2)[...] - lse_ref[qr, :][:, :1])
                da = da_s[qr, :]
                dp = _dot(da, v, NT_DIMS)
                ds = p * (dp - dl_s[qr, :][:, :1]) * scale
                dq_s[qr, :] = _dot(ds, k, NN_DIMS)
                dk_s[kr, :] += _dot(ds, q, TN_DIMS)
                dv_s[kr, :] += _dot(p, da, TN_DIMS)

            _attn_units(S, DILATIONS[gi], lambda qr: unit(qr, qr, BLK), lambda qr, kr: unit(qr, kr, 2 * BLK))

        for gi in range(N_GROUPS):
            pl.when(g == gi)(functools.partial(run, gi))

        gq_acc = jnp.zeros((1, HEAD_DIM), f32)
        gk_acc = jnp.zeros((1, HEAD_DIM), f32)
        for c in range(S // 256):
            rows = pl.ds(c * 256, 256)
            dq, gq_p = _rms_bwd_rows(q_ref[rows, :], gq_row, dq_s[rows, :], r=rq_s[rows, :])
            dk, gk_p = _rms_bwd_rows(k_ref[rows, :], gk_row, dk_s[rows, :], r=rk_s[rows, :])
            gq_acc, gk_acc = gq_acc + gq_p, gk_acc + gk_p
            stage[out.slot, 0, rows, :] = dq.astype(bf16)
            stage[out.slot, 1, rows, :] = dk.astype(bf16)
            stage[out.slot, 2, rows, :] = dv_s[rows, :].astype(bf16)
        dgq_ref[pl.ds(g, 1), :] += gq_acc
        dgk_ref[pl.ds(g, 1), :] += gk_acc
        for c in range(3):
            out.start(c, dst(c))
        out.end(dst)

    def slab(col0):
        return pl.BlockSpec((S, HEAD_DIM), lambda b, h, g: (b, col0 // HEAD_DIM + g * HEADS + h))

    gain = pl.BlockSpec((N_GROUPS, HEAD_DIM), lambda b, h, g: (0, 0))
    per_slot = pl.BlockSpec((S, HEAD_DIM), lambda b, h, g: (b, h))
    return pl.pallas_call(
        body, grid=(B, HEADS, N_GROUPS),
        in_specs=[slab(Q0), slab(K0), slab(V0), pl.BlockSpec((S, HEAD_DIM), lambda b, h, g: (b, ZA // HEAD_DIM + h)), gain, gain,
                  per_slot, per_slot, per_slot, pl.BlockSpec(memory_space=pl.ANY)],
        out_specs=[pl.BlockSpec(memory_space=pl.ANY), gain, gain],
        out_shape=[jax.ShapeDtypeStruct(dproj.shape, dproj.dtype), jax.ShapeDtypeStruct((N_GROUPS, HEAD_DIM), f32),
                   jax.ShapeDtypeStruct((N_GROUPS, HEAD_DIM), f32)],
        scratch_shapes=[pltpu.VMEM((S, HEAD_DIM), f32)] * 9 + [pltpu.VMEM((BLK, BLK), f32), pltpu.VMEM((BLK, 2 * BLK), f32),
                                                                pltpu.VMEM((2, 3, S, HEAD_DIM), bf16), pltpu.VMEM((S, HEAD_DIM), bf16),
                                                                pltpu.SemaphoreType.DMA((2, 3)), pltpu.SemaphoreType.DMA],
        input_output_aliases={9: 0},
        compiler_params=_params(("arbitrary", "arbitrary", "arbitrary")), name="attn_bwd",
    )(proj, proj, proj, proj, gq, gk, a_comb, lse, dag, dproj)


def _conv_fwd(proj, conv_w, B, S):
    T, NC = proj.shape
    tc = LANES

    def body(cb_ref, cc_ref, cv_ref, z_ref, w_ref, c_ref, ub):
        u = cc_ref[...] * cv_ref[...]
        ub[0:8, :] = jnp.zeros((8, tc), f32)
        ub[8:8 + S, :] = u
        w = w_ref[...]
        y = w[0:1] * u + w[1:2] * ub[pl.ds(7, S), :] + w[2:3] * ub[pl.ds(6, S), :]
        z = z_ref[...]
        c_ref[...] = (cb_ref[...] * y * (z * _sigmoid(z))).astype(bf16)

    def seg(col0):
        return pl.BlockSpec((S, tc), lambda j, b: (b, col0 // tc + j))

    return pl.pallas_call(
        body, grid=(CONV_W // tc, B),
        in_specs=[seg(CB), seg(CC), seg(CV), seg(ZC), pl.BlockSpec((3, tc), lambda j, b: (0, j))],
        out_specs=pl.BlockSpec((S, tc), lambda j, b: (b, j)),
        out_shape=jax.ShapeDtypeStruct((T, CONV_W), bf16),
        scratch_shapes=[pltpu.VMEM((S + 8, tc), f32)],
        compiler_params=_params(("parallel", "parallel")), name="conv_fwd",
    )(proj, proj, proj, proj, conv_w)


def _conv_bwd(proj, conv_w, dc, dproj, B, S):
    T, NC = proj.shape
    tc = LANES

    def body(cb_ref, cc_ref, cv_ref, z_ref, w_ref, dc_ref, dproj_in, dproj_ref, dw_ref, ub, db, stage, sem):
        j, b = pl.program_id(0), pl.program_id(1)

        def dst(c):
            return dproj_ref.at[pl.ds(b * S, S), pl.ds((CB, CC, CV, ZC)[c] + j * tc, tc)]

        out = _Deferred(stage, sem, j * B + b, (CONV_W // tc) * B - 1, 4)
        out.begin(dst)
        cb, cc, cv, z, dcv = cb_ref[...], cc_ref[...], cv_ref[...], z_ref[...], dc_ref[...]
        u = cc * cv
        ub[0:8, :] = jnp.zeros((8, tc), f32)
        ub[8:8 + S, :] = u
        u1, u2 = ub[pl.ds(7, S), :], ub[pl.ds(6, S), :]
        w = w_ref[...]
        y = w[0:1] * u + w[1:2] * u1 + w[2:3] * u2
        sg = _sigmoid(z)
        si = z * sg
        dyv = dcv * cb * si
        db[0:S, :] = dyv
        db[S:S + 8, :] = jnp.zeros((8, tc), f32)
        du = w[0:1] * dyv + w[1:2] * db[pl.ds(1, S), :] + w[2:3] * db[pl.ds(2, S), :]
        stage[out.slot, 0] = (dcv * y * si).astype(bf16)
        stage[out.slot, 1] = (du * cv).astype(bf16)
        stage[out.slot, 2] = (du * cc).astype(bf16)
        stage[out.slot, 3] = (dcv * cb * y * (sg * (1.0 + z * (1.0 - sg)))).astype(bf16)
        for c in range(4):
            out.start(c, dst(c))
        part = jnp.concatenate([jnp.sum(dyv * u, axis=0, keepdims=True), jnp.sum(dyv * u1, axis=0, keepdims=True),
                                jnp.sum(dyv * u2, axis=0, keepdims=True)], axis=0)

        @pl.when(b == 0)
        def _():
            dw_ref[...] = part

        @pl.when(b > 0)
        def _():
            dw_ref[...] += part

        out.end(dst)

    def seg(col0):
        return pl.BlockSpec((S, tc), lambda j, b: (b, col0 // tc + j))

    return pl.pallas_call(
        body, grid=(CONV_W // tc, B),
        in_specs=[seg(CB), seg(CC), seg(CV), seg(ZC), pl.BlockSpec((3, tc), lambda j, b: (0, j)),
                  pl.BlockSpec((S, tc), lambda j, b: (b, j)), pl.BlockSpec(memory_space=pl.ANY)],
        out_specs=[pl.BlockSpec(memory_space=pl.ANY), pl.BlockSpec((3, tc), lambda j, b: (0, j))],
        out_shape=[jax.ShapeDtypeStruct(dproj.shape, dproj.dtype), jax.ShapeDtypeStruct((3, CONV_W), f32)],
        scratch_shapes=[pltpu.VMEM((S + 8, tc), f32), pltpu.VMEM((S + 8, tc), f32), pltpu.VMEM((2, 4, S, tc), bf16),
                        pltpu.SemaphoreType.DMA((2, 4))],
        input_output_aliases={6: 0},
        compiler_params=_params(("arbitrary", "arbitrary")), name="conv_bwd",
    )(proj, proj, proj, proj, conv_w, dc, dproj)


MEM_CHUNK = 1024


def _mem_fwd(proj, mkv, gq, gk, B, S):
    T, NC = proj.shape
    scale = MEM_HD ** -0.5

    def body(q_ref, z_ref, k_ref, v_ref, gq_ref, gk_ref, o_ref):
        kv = k_ref[...]
        kn = (kv * _rstd_head(kv) * gk_ref[...]).astype(bf16)
        vv = v_ref[...].astype(bf16)

        def chunk(c, carry):
            rows = pl.ds(pl.multiple_of(c * MEM_CHUNK, MEM_CHUNK), MEM_CHUNK)
            q = q_ref[rows, :]
            qn = q * _rstd(q) * gq_ref[...]
            s = _dot(qn, kn, NT_DIMS) * scale
            p = jnp.exp(s - jnp.max(s, axis=-1, keepdims=True))
            p = p / jnp.sum(p, axis=-1, keepdims=True)
            z = z_ref[rows, :]
            o_ref[rows, :] = (_dot(p, vv, NN_DIMS) * (z * _sigmoid(z))).astype(bf16)
            return carry

        lax.fori_loop(0, S // MEM_CHUNK, chunk, 0)

    gain = pl.BlockSpec((1, MEM_HD), lambda b, h: (0, 0))
    return pl.pallas_call(
        body, grid=(B, MEM_HEADS),
        in_specs=[pl.BlockSpec((S, MEM_HD), lambda b, h: (b, MQ // MEM_HD + h)),
                  pl.BlockSpec((S, MEM_HD), lambda b, h: (b, ZM // MEM_HD + h)),
                  pl.BlockSpec((MEM_LEN, MEM_HD), lambda b, h: (b, h)),
                  pl.BlockSpec((MEM_LEN, MEM_HD), lambda b, h: (b, MEM_HEADS + h)), gain, gain],
        out_specs=pl.BlockSpec((S, MEM_HD), lambda b, h: (b, h)),
        out_shape=jax.ShapeDtypeStruct((T, MEM_W), bf16),
        compiler_params=_params(("parallel", "parallel")), name="mem_fwd",
    )(proj, proj, mkv, mkv, gq, gk)


def _mem_bwd(proj, mkv, gq, gk, dmo, dproj, B, S):
    T, NC = proj.shape
    scale = MEM_HD ** -0.5

    def body(q_ref, z_ref, k_ref, v_ref, gq_ref, gk_ref, dmo_ref, dproj_in, dproj_ref, dmk_ref, dmv_ref, dgq_ref, dgk_ref,
             dkn_s, dv_s, gq_s, stage, sem):
        b, h = pl.program_id(0), pl.program_id(1)

        def dst(c):
            return dproj_ref.at[pl.ds(b * S, S), pl.ds((MQ, ZM)[c] + h * MEM_HD, MEM_HD)]

        out = _Deferred(stage, sem, b * MEM_HEADS + h, B * MEM_HEADS - 1, 2)
        out.begin(dst)
        kv = k_ref[...]
        kn = (kv * _rstd_head(kv) * gk_ref[...]).astype(bf16)
        vv = v_ref[...].astype(bf16)
        dkn_s[...] = jnp.zeros_like(dkn_s)
        dv_s[...] = jnp.zeros_like(dv_s)
        gq_s[...] = jnp.zeros_like(gq_s)

        def chunk(c, carry):
            rows = pl.ds(pl.multiple_of(c * MEM_CHUNK, MEM_CHUNK), MEM_CHUNK)
            q = q_ref[rows, :]
            qn = q * _rstd(q) * gq_ref[...]
            s = _dot(qn, kn, NT_DIMS) * scale
            p = jnp.exp(s - jnp.max(s, axis=-1, keepdims=True))
            p = p / jnp.sum(p, axis=-1, keepdims=True)
            mo = _dot(p, vv, NN_DIMS)
            z, dg_ = z_ref[rows, :], dmo_ref[rows, :]
            sg = _sigmoid(z)
            do = dg_ * (z * sg)
            stage[out.slot, 1, rows, :] = (dg_ * mo * (sg * (1.0 + z * (1.0 - sg)))).astype(bf16)
            dp = _dot(do, vv, NT_DIMS)
            ds = p * (dp - jnp.sum(do * mo, axis=-1, keepdims=True)) * scale
            dq, gq_p = _rms_bwd_rows(q, gq_ref[...], _dot(ds, kn, NN_DIMS), on_mxu=False)
            stage[out.slot, 0, rows, :] = dq.astype(bf16)
            gq_s[...] += gq_p
            dkn_s[...] += _dot(ds, qn, TN_DIMS)
            dv_s[...] += _dot(p, do, TN_DIMS)
            return carry

        lax.fori_loop(0, S // MEM_CHUNK, chunk, 0)
        for c in range(2):
            out.start(c, dst(c))
        dk, gk_p = _rms_bwd_rows(kv, gk_ref[...], dkn_s[...])
        dmk_ref[...] = dk
        dmv_ref[...] = dv_s[...]

        @pl.when((b == 0) & (h == 0))
        def _():
            dgq_ref[...] = gq_s[...]
            dgk_ref[...] = gk_p

        @pl.when((b > 0) | (h > 0))
        def _():
            dgq_ref[...] += gq_s[...]
            dgk_ref[...] += gk_p

        out.end(dst)

    gain = pl.BlockSpec((1, MEM_HD), lambda b, h: (0, 0))
    kvb = pl.BlockSpec((MEM_LEN, MEM_HD), lambda b, h: (b, h))
    return pl.pallas_call(
        body, grid=(B, MEM_HEADS),
        in_specs=[pl.BlockSpec((S, MEM_HD), lambda b, h: (b, MQ // MEM_HD + h)),
                  pl.BlockSpec((S, MEM_HD), lambda b, h: (b, ZM // MEM_HD + h)),
                  kvb, pl.BlockSpec((MEM_LEN, MEM_HD), lambda b, h: (b, MEM_HEADS + h)), gain, gain,
                  pl.BlockSpec((S, MEM_HD), lambda b, h: (b, h)), pl.BlockSpec(memory_space=pl.ANY)],
        out_specs=[pl.BlockSpec(memory_space=pl.ANY), kvb, kvb, gain, gain],
        out_shape=[jax.ShapeDtypeStruct(dproj.shape, dproj.dtype), jax.ShapeDtypeStruct((B * MEM_LEN, MEM_W), f32),
                   jax.ShapeDtypeStruct((B * MEM_LEN, MEM_W), f32), jax.ShapeDtypeStruct((1, MEM_HD), f32),
                   jax.ShapeDtypeStruct((1, MEM_HD), f32)],
        scratch_shapes=[pltpu.VMEM((MEM_LEN, MEM_HD), f32), pltpu.VMEM((MEM_LEN, MEM_HD), f32), pltpu.VMEM((1, MEM_HD), f32),
                        pltpu.VMEM((2, 2, S, MEM_HD), bf16), pltpu.SemaphoreType.DMA((2, 2))],
        input_output_aliases={7: 0},
        compiler_params=_params(("arbitrary", "arbitrary")), name="mem_bwd",
    )(proj, proj, mkv, mkv, gq, gk, dmo, dproj)


def _merge_fwd(proj, ag, cg, mg, wa, wc, wm):
    T, NC = proj.shape
    D = wa.shape[1]
    tm, tn = _pick(T, 1024), _pick(D, 512)
    assert GT % tn == 0

    def body(ag_ref, cg_ref, mg_ref, wa_ref, wc_ref, wm_ref, g0_ref, g1_ref, g2_ref, mer_ref, ba_ref, bc_ref, bm_ref):
        ba = _dot(ag_ref[...], wa_ref[...], NN_DIMS)
        bc = _dot(cg_ref[...], wc_ref[...], NN_DIMS)
        bm = _dot(mg_ref[...], wm_ref[...], NN_DIMS)
        mer_ref[...] = (_sigmoid(g0_ref[...]) * ba + _sigmoid(g1_ref[...]) * bc + _sigmoid(g2_ref[...]) * bm).astype(bf16)
        ba_ref[...] = ba.astype(bf16)
        bc_ref[...] = bc.astype(bf16)
        bm_ref[...] = bm.astype(bf16)

    def act(w):
        return pl.BlockSpec((tm, w), lambda i, j: (i, 0))

    def wt(k):
        return pl.BlockSpec((k, tn), lambda i, j: (0, j))

    def gate(n):
        return pl.BlockSpec((tm, tn), lambda i, j: (i, (GT + n * D) // tn + j))

    out = pl.BlockSpec((tm, tn), lambda i, j: (i, j))
    return pl.pallas_call(
        body, grid=(T // tm, D // tn),
        in_specs=[act(ATTN_OUT), act(CONV_W), act(MEM_W), wt(ATTN_OUT), wt(CONV_W), wt(MEM_W), gate(0), gate(1), gate(2)],
        out_specs=[out] * 4, out_shape=[jax.ShapeDtypeStruct((T, D), bf16)] * 4,
        compiler_params=_params(("parallel", "parallel")), name="merge_fwd",
    )(ag, cg, mg, wa, wc, wm, proj, proj, proj)


def _out_loss(merged, w_out, x, tgt):
    T, D = x.shape
    tm, tn = _pick(T, 1024), _pick(D, 512)

    def body(m_ref, w_ref, x_ref, t_ref, dy_ref, dyb_ref, lp_ref):
        e = x_ref[...] + _dot(m_ref[...], w_ref[...], NN_DIMS) - t_ref[...]
        dy = e / D
        dy_ref[...] = dy
        dyb_ref[...] = dy.astype(bf16)
        part = jnp.full((1, 8, LANES), jnp.sum(e * e), f32)

        @pl.when(pl.program_id(1) == 0)
        def _():
            lp_ref[...] = part

        @pl.when(pl.program_id(1) > 0)
        def _():
            lp_ref[...] += part

    tile = pl.BlockSpec((tm, tn), lambda i, j: (i, j))
    return pl.pallas_call(
        body, grid=(T // tm, D // tn),
        in_specs=[pl.BlockSpec((tm, D), lambda i, j: (i, 0)), pl.BlockSpec((D, tn), lambda i, j: (0, j)), tile, tile],
        out_specs=[tile, tile, pl.BlockSpec((1, 8, LANES), lambda i, j: (i, 0, 0))],
        out_shape=[jax.ShapeDtypeStruct((T, D), f32), jax.ShapeDtypeStruct((T, D), bf16),
                   jax.ShapeDtypeStruct((T // tm, 8, LANES), f32)],
        compiler_params=_params(("parallel", "arbitrary")), name="out_loss",
    )(merged, w_out, x, tgt)


def _merge_bwd(proj, dyb, w_out, ba, bc, bm):
    T, NC = proj.shape
    D = w_out.shape[0]
    tm, tn = _pick(T, 1024), _pick(D, 512)
    assert GT % tn == 0

    def body(dy_ref, w_ref, ba_ref, bc_ref, bm_ref, g0_ref, g1_ref, g2_ref, da_ref, dc_ref, dm_ref, dproj_ref, stage, sem):
        i, j = pl.program_id(0), pl.program_id(1)

        def dst(n):
            return dproj_ref.at[pl.ds(i * tm, tm), pl.ds(GT + n * D + j * tn, tn)]

        out = _Deferred(stage, sem, i * (D // tn) + j, (T // tm) * (D // tn) - 1, 3)
        out.begin(dst)
        dmer = _dot(dy_ref[...], w_ref[...], NT_DIMS).astype(bf16)
        for n, (g_ref, br_ref, o_ref) in enumerate(((g0_ref, ba_ref, da_ref), (g1_ref, bc_ref, dc_ref), (g2_ref, bm_ref, dm_ref))):
            th = jnp.tanh(0.5 * g_ref[...])
            o_ref[...] = (0.5 * th + 0.5).astype(bf16) * dmer
            stage[out.slot, n] = dmer * br_ref[...] * (0.25 - 0.25 * (th * th)).astype(bf16)
            out.start(n, dst(n))
        out.end(dst)

    tile = pl.BlockSpec((tm, tn), lambda i, j: (i, j))

    def gate(n):
        return pl.BlockSpec((tm, tn), lambda i, j: (i, (GT + n * D) // tn + j))

    return pl.pallas_call(
        body, grid=(T // tm, D // tn),
        in_specs=[pl.BlockSpec((tm, D), lambda i, j: (i, 0)), pl.BlockSpec((tn, D), lambda i, j: (j, 0)), tile, tile, tile,
                  gate(0), gate(1), gate(2)],
        out_specs=[tile, tile, tile, pl.BlockSpec(memory_space=pl.ANY)],
        out_shape=[jax.ShapeDtypeStruct((T, D), bf16)] * 3 + [jax.ShapeDtypeStruct((T, NC), bf16)],
        scratch_shapes=[pltpu.VMEM((2, 3, tm, tn), bf16), pltpu.SemaphoreType.DMA((2, 3))],
        compiler_params=_params(("arbitrary", "arbitrary")), name="merge_bwd",
    )(dyb, w_out, ba, bc, bm, proj, proj, proj)


def _fwd_bwd_head(x2, mem2, t2, proj, attn, B, S, mem_norm_g, attn_q_norm, attn_k_norm, conv_w, mem_q_norm, mem_k_norm,
                  w_kv, w_a, w_c, w_m, w_out):
    D = x2.shape[1]
    mng = mem_norm_g.reshape(1, D)
    mqn, mkn = mem_q_norm.reshape(1, MEM_HD), mem_k_norm.reshape(1, MEM_HD)
    ag, a_comb, lse = attn

    mh = _rms_fwd(mem2, mng, "rms_mem")
    mkv = _mm(mh, w_kv, mode="nn", out_dtype=f32, name="mkv")
    cg = _conv_fwd(proj, conv_w, B, S)
    mg = _mem_fwd(proj, mkv, mqn, mkn, B, S)
    merged, ba, bc, bm = _merge_fwd(proj, ag, cg, mg, w_a, w_c, w_m)
    dy, dyb, lp = _out_loss(merged, w_out, x2, t2)
    sq_err = jnp.sum(lp[:, 0, 0])

    g = {}
    g["w_out"] = _mm(merged, dyb, mode="tn", out_dtype=f32, name="dw_out")
    dba, dbc, dbm, dproj = _merge_bwd(proj, dyb, w_out, ba, bc, bm)
    g["w_br_attn"] = _mm(ag, dba, mode="tn", out_dtype=f32, name="dw_br_attn")
    g["w_br_conv"] = _mm(cg, dbc, mode="tn", out_dtype=f32, name="dw_br_conv")
    g["w_br_mem"] = _mm(mg, dbm, mode="tn", out_dtype=f32, name="dw_br_mem")
    dag = _mm(dba, w_a, mode="nt", out_dtype=f32, name="d_attn_out")
    dcg = _mm(dbc, w_c, mode="nt", out_dtype=f32, name="d_conv_out")
    dmg = _mm(dbm, w_m, mode="nt", out_dtype=f32, name="d_mem_out")
    dproj, g["attn_q_norm"], g["attn_k_norm"] = _attn_bwd(proj, attn_q_norm, attn_k_norm, a_comb, lse, dag, dproj, B, S)
    dproj, g["conv_w"] = _conv_bwd(proj, conv_w, dcg, dproj, B, S)
    dproj, dmk, dmv, dgmq, dgmk = _mem_bwd(proj, mkv, mqn, mkn, dmg, dproj, B, S)
    g["mem_q_norm"], g["mem_k_norm"] = dgmq.reshape(MEM_HD), dgmk.reshape(MEM_HD)
    dmkv = jnp.concatenate([dmk, dmv], axis=1)
    dmh = _mm(dmkv, w_kv, mode="nt", out_dtype=f32, name="d_mem_h")
    g["mem_norm_g"] = _rms_bwd(mem2, mng, dmh, None, "rms_mem_bwd").reshape(D)
    return sq_err, g, dict(dy=dy, dproj=dproj, mh=mh, dmkv=dmkv)


MESH = pl.DeviceIdType.MESH
HBM = pl.BlockSpec(memory_space=pl.ANY)


def _me():
    x, y, c = lax.axis_index("x"), lax.axis_index("y"), lax.axis_index("c")
    return (x, y, c), 4 * x + 2 * y + c


def _peer(k):
    (x, y, c), _ = _me()
    p = (1 - x if k & 4 else x, 1 - y if k & 2 else y, 1 - c if k & 1 else c)
    return p, 4 * p[0] + 2 * p[1] + p[2]


def _window(ref, axis, size, idx):
    if axis is None:
        return ref.at[idx]
    if axis == 0:
        return ref.at[pl.ds(idx * size, size), :]
    return ref.at[:, pl.ds(idx * size, size)]


def _full_shape(s, axis):
    if axis is None:
        return (N_DEV,) + s.shape
    return tuple(N_DEV * d if i == axis else d for i, d in enumerate(s.shape))


OTHER_CHIPS = (4, 2, 6)


def _spread_comm(shards, axes):
    n = len(shards)

    def copies(ins, outs, send_sems, recv_sems, base=0):
        _, me = _me()
        out = []
        for a in range(n):
            mine = _window(outs[a], axes[a], None if axes[a] is None else shards[a].shape[axes[a]], me)
            out.append(pltpu.make_async_copy(ins[a], mine, send_sems.at[base + a, N_CHIP]))
            for k, dist in enumerate((1,) + OTHER_CHIPS):
                dev, _ = _peer(dist)
                out.append(pltpu.make_async_remote_copy(
                    src_ref=ins[a], dst_ref=mine, send_sem=send_sems.at[base + a, k], recv_sem=recv_sems.at[base + a, k],
                    device_id=dev, device_id_type=MESH))
        return out

    shapes = [jax.ShapeDtypeStruct(_full_shape(s, ax), s.dtype) for s, ax in zip(shards, axes)]
    return _Comm(shards, shapes, (n, N_CHIP + 1), copies)


def _forward_blocks(fulls, axes):
    n = len(fulls)
    sizes = [None if ax is None else f.shape[ax] // N_DEV for f, ax in zip(fulls, axes)]

    def body(*refs):
        outs = refs[n:2 * n]
        send_sems, recv_sems = refs[2 * n:]
        sibling, _ = _peer(1)
        copies = []
        for j, dist in enumerate(OTHER_CHIPS):
            _, held = _peer(dist)
            for a in range(n):
                block = _window(outs[a], axes[a], sizes[a], held)
                copies.append(pltpu.make_async_remote_copy(
                    src_ref=block, dst_ref=block, send_sem=send_sems.at[a, j], recv_sem=recv_sems.at[a, j],
                    device_id=sibling, device_id_type=MESH))
        for cp in copies:
            cp.start()
        for cp in copies:
            cp.wait()

    return pl.pallas_call(
        body, in_specs=[HBM] * n, out_specs=[HBM] * n,
        out_shape=[jax.ShapeDtypeStruct(f.shape, f.dtype) for f in fulls],
        scratch_shapes=[pltpu.SemaphoreType.DMA((n, len(OTHER_CHIPS))), pltpu.SemaphoreType.DMA((n, len(OTHER_CHIPS)))],
        input_output_aliases={a: a for a in range(n)},
        name="forward_blocks",
    )(*fulls)


def _shard_order():
    (x, y, c), me = _me()
    xs, ys, xo, yo = _peer(4)[1], _peer(2)[1], _peer(5)[1], _peer(3)[1]
    north = c == 1
    ids = [me, _peer(1)[1], jnp.where(north, xs, ys), jnp.where(north, yo, xo), jnp.where(north, ys, xs),
           jnp.where(north, xo, yo), _peer(6)[1], _peer(7)[1]]
    return jnp.stack(ids).astype(jnp.int32)


def _proj_gather(h, w_shard, order, comm, comm_at):
    T, K = h.shape
    C = w_shard.shape[1]
    tm = _pick(T, 1024)
    nm = T // tm
    ahead = max(nm - 2, 0)
    n_ci, n_co = len(comm.ins), len(comm.out_shapes)

    def body(*refs):
        order_ref, h_ref, ws_ref = refs[:3]
        c_ins = refs[3:3 + n_ci]
        o_ref, wf_ref = refs[3 + n_ci:5 + n_ci]
        c_outs = refs[5 + n_ci:5 + n_ci + n_co]
        wbuf, send_sems, recv_sems, own_sem, buf_sems, c_send, c_recv = refs[5 + n_ci + n_co:]
        t, i = pl.program_id(0), pl.program_id(1)

        @pl.when((t == comm_at) & (i == 0))
        def _():
            for cp in comm.copies(c_ins, c_outs, c_send, c_recv):
                cp.start()
        (x, y, c), me = _me()
        sibling, sib_id = _peer(1)
        chips = [_peer(dist) for dist in OTHER_CHIPS]

        def win(idx):
            return wf_ref.at[:, pl.ds(idx * C, C)]

        def copy(k, block, to, src=None):
            return pltpu.make_async_remote_copy(
                src_ref=win(block) if src is None else src, dst_ref=win(block),
                send_sem=send_sems.at[k], recv_sem=recv_sems.at[k], device_id=to, device_id_type=MESH)

        def fetch(tt, src):
            return pltpu.make_async_copy(src, wbuf.at[tt % 2], buf_sems.at[tt % 2])

        own = pltpu.make_async_copy(ws_ref, win(me), own_sem)
        (x_nbr, x_id), (y_nbr, y_id), (_, d_id) = chips

        def half(k, block, top, to):
            rows = wf_ref.at[pl.ds(0 if top else K // 2, K // 2), pl.ds(block * C, C)]
            return pltpu.make_async_remote_copy(src_ref=rows, dst_ref=rows, send_sem=send_sems.at[k], recv_sem=recv_sems.at[k],
                                                device_id=to, device_id_type=MESH)

        here = (x, y, c)

        def pass_on(from_x):
            if from_x:
                copy(4, x_id, sibling).start()
                half(8, x_id, False, y_nbr).start()
            else:
                copy(5, y_id, sibling).start()
                half(7, y_id, True, x_nbr).start()

        @pl.when((t == 0) & (i == 0))
        def _():
            fetch(0, ws_ref).start()
            own.start()
            copy(0, me, sibling, src=ws_ref).start()

        for tt in range(N_DEV):
            @pl.when((t == tt) & (i == 0))
            def _(tt=tt):
                fetch(tt, ws_ref).wait()

        o_ref[...] = _dot(h_ref[...], wbuf[t % 2], NN_DIMS)

        def schedule(first_x):
            on = c == (1 if first_x else 0)
            (f_dev, f_id), (g_dev, g_id) = ((x_nbr, x_id), (y_nbr, y_id)) if first_x else ((y_nbr, y_id), (x_nbr, x_id))
            kf, kg = (1, 2) if first_x else (2, 1)
            pf, pg = (4, 5) if first_x else (5, 4)

            @pl.when((t == 0) & (i == 0) & on)
            def _():
                copy(kf, me, f_dev, src=ws_ref).start()

            def event(nxt):
                if nxt == 1:
                    copy(0, sib_id, here).wait_recv()
                    return sib_id
                if nxt == 2:
                    copy(kf, f_id, here).wait_recv()
                    copy(kf, me, f_dev, src=ws_ref).wait_send()
                    copy(kg, me, g_dev, src=ws_ref).start()
                    pass_on(first_x)
                    return f_id
                if nxt == 3:
                    copy(pg, g_id + 1 - 2 * c, here).wait_recv()
                    return g_id + 1 - 2 * c
                if nxt == 4:
                    copy(kg, g_id, here).wait_recv()
                    pass_on(not first_x)
                    return g_id
                if nxt == 5:
                    copy(pf, f_id + 1 - 2 * c, here).wait_recv()
                    return f_id + 1 - 2 * c
                if nxt == 6:
                    half(7, d_id, True, here).wait_recv()
                    half(8, d_id, False, here).wait_recv()
                    copy(6, d_id, sibling).start()
                    return d_id
                copy(6, d_id + 1 - 2 * c, here).wait_recv()
                return d_id + 1 - 2 * c

            for tt in range(N_DEV - 1):
                @pl.when((t == tt) & (i == ahead) & on)
                def _(tt=tt):
                    fetch(tt + 1, win(event(tt + 1))).start()

            @pl.when((t == N_DEV - 1) & (i == nm - 1) & on)
            def _():
                copy(kg, me, g_dev, src=ws_ref).wait_send()

        schedule(True)
        schedule(False)

        @pl.when((t == N_DEV - 1) & (i == nm - 1))
        def _():
            copy(0, me, sibling, src=ws_ref).wait_send()
            half(7, y_id, True, x_nbr).wait_send()
            half(8, x_id, False, y_nbr).wait_send()
            for j, (_, dev_id) in enumerate(chips):
                copy(4 + j, dev_id, sibling).wait_send()
            own.wait()
            for cp in comm.copies(c_ins, c_outs, c_send, c_recv):
                cp.wait()

    hbm = pl.BlockSpec(memory_space=pl.ANY)
    res = pl.pallas_call(
        body,
        grid_spec=pltpu.PrefetchScalarGridSpec(
            num_scalar_prefetch=1, grid=(N_DEV, nm),
            in_specs=[pl.BlockSpec((tm, K), lambda t, i, order_ref: (i, 0)), hbm] + [hbm] * n_ci,
            out_specs=[pl.BlockSpec((tm, C), lambda t, i, order_ref: (i, order_ref[t])), hbm] + [hbm] * n_co,
            scratch_shapes=[pltpu.VMEM((2, K, C), bf16), pltpu.SemaphoreType.DMA((9,)), pltpu.SemaphoreType.DMA((9,)),
                            pltpu.SemaphoreType.DMA, pltpu.SemaphoreType.DMA((2,))] + comm.scratch()),
        out_shape=[jax.ShapeDtypeStruct((T, N_DEV * C), f32), jax.ShapeDtypeStruct((K, N_DEV * C), bf16)] + comm.out_shapes,
        compiler_params=_params(("arbitrary", "arbitrary")), name="proj_gather",
    )(order, h, w_shard, *comm.ins)
    return res[0], res[1], list(res[2:])


N_CHIP = 4


def _shard_shape(g, ax):
    return tuple(d // N_DEV if i == ax else d for i, d in enumerate(g.shape))


def _sibling_comm(grads, axes):
    n = len(grads)
    sizes = [g.shape[ax] // N_DEV for g, ax in zip(grads, axes)]

    def copies(ins, lands, send_sems, recv_sems, base=0):
        sibling, _ = _peer(1)
        out = []
        for j in range(N_CHIP):
            _, owner = _peer(2 * j + 1)
            for a in range(n):
                out.append(pltpu.make_async_remote_copy(
                    src_ref=_window(ins[a], axes[a], sizes[a], owner), dst_ref=lands[a].at[j],
                    send_sem=send_sems.at[base + a, j], recv_sem=recv_sems.at[base + a, j], device_id=sibling,
                    device_id_type=MESH))
        return out

    shapes = [jax.ShapeDtypeStruct((N_CHIP,) + _shard_shape(g, ax), g.dtype) for g, ax in zip(grads, axes)]
    return _Comm(grads, shapes, (n, N_CHIP), copies)


def _chips_comm(sums):
    n = len(sums)

    def copies(ins, lands, send_sems, recv_sems, base=0):
        out = []
        for j in range(1, N_CHIP):
            dev, _ = _peer(2 * j)
            for a in range(n):
                out.append(pltpu.make_async_remote_copy(
                    src_ref=ins[a].at[j - 1], dst_ref=lands[a].at[j - 1],
                    send_sem=send_sems.at[base + a, j - 1], recv_sem=recv_sems.at[base + a, j - 1], device_id=dev,
                    device_id_type=MESH))
        return out

    return _Comm(sums, [jax.ShapeDtypeStruct(s.shape, s.dtype) for s in sums], (n, N_CHIP), copies)


def _join(c1, c2):
    n1, m1, k1 = len(c1.ins), len(c1.out_shapes), c1.sem_shape[0]

    def copies(ins, lands, send_sems, recv_sems):
        return (c1.copies(ins[:n1], lands[:m1], send_sems, recv_sems, base=0)
                + c2.copies(ins[n1:], lands[m1:], send_sems, recv_sems, base=k1))

    aliases = {**c1.aliases, **{n1 + k: m1 + v for k, v in c2.aliases.items()}}
    return _Comm(c1.ins + c2.ins, c1.out_shapes + c2.out_shapes, (k1 + c2.sem_shape[0], N_CHIP), copies, aliases)


def _chips_first_hop(sums):
    n = len(sums)

    def copies(ins, outs, send_sems, recv_sems, base=0):
        lands, relays = outs[:n], outs[n:]
        (y_dev, _), (x_dev, _) = _peer(2), _peer(4)
        out = []
        for a in range(n):
            half = sums[a].shape[1] // 2
            for k, (src, dst, dev) in enumerate((
                    (ins[a].at[0], lands[a].at[0], y_dev), (ins[a].at[1], lands[a].at[1], x_dev),
                    (ins[a].at[2, pl.ds(0, half)], relays[a].at[0], x_dev),
                    (ins[a].at[2, pl.ds(half, half)], relays[a].at[1], y_dev))):
                out.append(pltpu.make_async_remote_copy(
                    src_ref=src, dst_ref=dst, send_sem=send_sems.at[base + a, k], recv_sem=recv_sems.at[base + a, k],
                    device_id=dev, device_id_type=MESH))
        return out

    shapes = ([jax.ShapeDtypeStruct(s.shape, s.dtype) for s in sums]
              + [jax.ShapeDtypeStruct((2, s.shape[1] // 2) + s.shape[2:], s.dtype) for s in sums])
    return _Comm(sums, shapes, (n, N_CHIP), copies)


def _chips_relay(relays, lands):
    n = len(relays)

    def copies(ins, outs, send_sems, recv_sems, base=0):
        (y_dev, _), (x_dev, _) = _peer(2), _peer(4)
        out = []
        for a in range(n):
            half = relays[a].shape[1]
            for k, (rows, dev) in enumerate(((pl.ds(0, half), y_dev), (pl.ds(half, half), x_dev))):
                out.append(pltpu.make_async_remote_copy(
                    src_ref=ins[a].at[k], dst_ref=outs[a].at[2, rows], send_sem=send_sems.at[base + a, k],
                    recv_sem=recv_sems.at[base + a, k], device_id=dev, device_id_type=MESH))
        return out

    return _Comm(list(relays) + list(lands), [jax.ShapeDtypeStruct(l.shape, l.dtype) for l in lands], (n, N_CHIP), copies,
                 aliases={n + a: a for a in range(n)})


def _rs_chip_sum(grad, land, xyc, axis, name):
    R, C = land.shape[1:]
    tr = _pick(R, max(8, (640 * 1024) // C), 8)

    def body(xyc_ref, g_ref, l_ref, o_ref):
        o_ref[0] = (g_ref[...] + l_ref[0]).astype(bf16)

    def owner(j, xyc_ref):
        jj = j + 1
        return 4 * (xyc_ref[0] ^ (jj >> 1)) + 2 * (xyc_ref[1] ^ (jj & 1)) + xyc_ref[2]

    if axis == 0:
        g_spec = pl.BlockSpec((tr, C), lambda j, i, xyc_ref: (owner(j, xyc_ref) * (R // tr) + i, 0))
    else:
        g_spec = pl.BlockSpec((tr, C), lambda j, i, xyc_ref: (i, owner(j, xyc_ref)))
    return pl.pallas_call(
        body,
        grid_spec=pltpu.PrefetchScalarGridSpec(
            num_scalar_prefetch=1, grid=(N_CHIP - 1, R // tr),
            in_specs=[g_spec, pl.BlockSpec((1, tr, C), lambda j, i, xyc_ref: (j + 1, i, 0))],
            out_specs=pl.BlockSpec((1, tr, C), lambda j, i, xyc_ref: (j, i, 0))),
        out_shape=jax.ShapeDtypeStruct((N_CHIP - 1, R, C), bf16),
        compiler_params=_params(("parallel", "parallel")), name=name,
    )(xyc, grad, land)


def _allreduce_small(part):
    R = part.shape[0]

    def body(p_ref, o_ref, buf, send_sems, recv_sems):
        (x, y, c), me = _me()
        buf[me] = p_ref[...]
        copies = []
        for k in range(1, N_DEV):
            dev, dev_id = _peer(k)
            copies.append(pltpu.make_async_remote_copy(
                src_ref=p_ref, dst_ref=buf.at[me], send_sem=send_sems.at[k - 1], recv_sem=recv_sems.at[k - 1],
                device_id=dev, device_id_type=MESH))
        for cp in copies:
            cp.start()
        for k in range(1, N_DEV):
            _, dev_id = _peer(k)
            pltpu.make_async_remote_copy(
                src_ref=p_ref, dst_ref=buf.at[dev_id], send_sem=send_sems.at[k - 1], recv_sem=recv_sems.at[k - 1],
                device_id=(x, y, c), device_id_type=MESH).wait_recv()
        for cp in copies:
            cp.wait_send()
        acc = buf[0]
        for d in range(1, N_DEV):
            acc = acc + buf[d]
        o_ref[...] = acc

    return pl.pallas_call(
        body, in_specs=[pl.BlockSpec(memory_space=pltpu.VMEM)], out_specs=pl.BlockSpec(memory_space=pltpu.VMEM),
        out_shape=jax.ShapeDtypeStruct((R, LANES), f32),
        scratch_shapes=[pltpu.VMEM((N_DEV, R, LANES), f32), pltpu.SemaphoreType.DMA((N_DEV - 1,)), pltpu.SemaphoreType.DMA((N_DEV - 1,))],
        name="allreduce_small",
    )(part)


def _adamw_math(w, g, m, v):
    m2 = ADAM_B1 * m + (1.0 - ADAM_B1) * g
    v2 = ADAM_B2 * v + (1.0 - ADAM_B2) * (g * g)
    m_hat = m2 / (1.0 - ADAM_B1 ** ADAM_STEP)
    v_hat = v2 / (1.0 - ADAM_B2 ** ADAM_STEP)
    return -ADAM_LR * (m_hat / (jnp.sqrt(v_hat) + ADAM_EPS) + ADAM_WD * w), m2, v2


def _adamw_shard(grad, land_sib, land_chips, w, m, v, me, axis, name, row0=0, prev=None):
    R, C = land_sib.shape[1:]
    assert axis == 1 or R == w.shape[0]
    tr = _pick(R, max(8, (320 * 1024) // C), 8)
    r0 = row0 // tr
    n_prev = len(prev) if prev else 0

    def body(me_ref, g_ref, s_ref, l_ref, w_ref, m_ref, v_ref, *rest):
        go_ref, d_ref, mo_ref, vo_ref = rest[n_prev:]
        g = g_ref[...] + s_ref[0]
        for j in range(N_CHIP - 1):
            g = g + l_ref[j].astype(f32)
        d, m2, v2 = _adamw_math(w_ref[...], g, m_ref[...], v_ref[...])
        go_ref[...] = g
        d_ref[...] = d
        mo_ref[...] = m2
        vo_ref[...] = v2

    if axis == 0:
        g_spec = pl.BlockSpec((tr, C), lambda i, me_ref: (me_ref[0] * (R // tr) + i, 0))
    else:
        g_spec = pl.BlockSpec((tr, C), lambda i, me_ref: (i, me_ref[0]))
    blk = pl.BlockSpec((tr, C), lambda i, me_ref: (r0 + i, 0))
    return pl.pallas_call(
        body,
        grid_spec=pltpu.PrefetchScalarGridSpec(
            num_scalar_prefetch=1, grid=(R // tr,),
            in_specs=[g_spec, pl.BlockSpec((1, tr, C), lambda i, me_ref: (0, i, 0)),
                      pl.BlockSpec((N_CHIP - 1, tr, C), lambda i, me_ref: (0, i, 0)), blk, blk, blk]
            + [pl.BlockSpec(memory_space=pl.ANY)] * n_prev,
            out_specs=[blk] * 4),
        out_shape=[jax.ShapeDtypeStruct(w.shape, f32)] * 4,
        input_output_aliases={7 + i: i for i in range(n_prev)},
        compiler_params=_params(("parallel",)), name=name,
    )(me, grad, land_sib, land_chips, w, m, v, *(prev or []))


def _adamw_small(g, w, m, v):
    def body(g_ref, w_ref, m_ref, v_ref, d_ref, mo_ref, vo_ref):
        d, m2, v2 = _adamw_math(w_ref[...], g_ref[...], m_ref[...], v_ref[...])
        d_ref[...] = d
        mo_ref[...] = m2
        vo_ref[...] = v2

    return pl.pallas_call(body, out_shape=[jax.ShapeDtypeStruct(g.shape, f32)] * 3, name="adamw_small")(g, w, m, v)


def _pack_rows(parts, total_rows):
    rows = jnp.concatenate([p.reshape(-1, LANES) for p in parts], axis=0)
    return jnp.pad(rows, ((0, total_rows - rows.shape[0]), (0, 0)))


BIG = ("w_in", "mem_w_kv", "w_br_attn", "w_br_conv", "w_br_mem", "w_out")
BIG_AXIS = {"w_in": 1, "mem_w_kv": 0, "w_br_attn": 1, "w_br_conv": 1, "w_br_mem": 1, "w_out": 0}
SMALL = ("norm_g", "mem_norm_g", "attn_q_norm", "attn_k_norm", "mem_q_norm", "mem_k_norm")
ALL_W = ("norm_g", "mem_norm_g", "w_in", "attn_q_norm", "attn_k_norm", "conv_w", "mem_w_kv", "mem_q_norm", "mem_k_norm",
         "w_br_attn", "w_br_conv", "w_br_mem", "w_out")


def kernel(x, mem, norm_g, mem_norm_g, w_in, attn_q_norm, attn_k_norm, conv_w, mem_w_kv, mem_q_norm, mem_k_norm, w_br_attn, w_br_conv, w_br_mem, w_out, loss_target, m_norm_g, m_mem_norm_g, m_w_in, m_attn_q_norm, m_attn_k_norm, m_conv_w, m_mem_w_kv, m_mem_q_norm, m_mem_k_norm, m_w_br_attn, m_w_br_conv, m_w_br_mem, m_w_out, v_norm_g, v_mem_norm_g, v_w_in, v_attn_q_norm, v_attn_k_norm, v_conv_w, v_mem_w_kv, v_mem_q_norm, v_mem_k_norm, v_w_br_attn, v_w_br_conv, v_w_br_mem, v_w_out):
    w = dict(norm_g=norm_g, mem_norm_g=mem_norm_g, w_in=w_in, attn_q_norm=attn_q_norm, attn_k_norm=attn_k_norm, conv_w=conv_w,
             mem_w_kv=mem_w_kv, mem_q_norm=mem_q_norm, mem_k_norm=mem_k_norm, w_br_attn=w_br_attn, w_br_conv=w_br_conv,
             w_br_mem=w_br_mem, w_out=w_out)
    mo = dict(norm_g=m_norm_g, mem_norm_g=m_mem_norm_g, w_in=m_w_in, attn_q_norm=m_attn_q_norm, attn_k_norm=m_attn_k_norm,
              conv_w=m_conv_w, mem_w_kv=m_mem_w_kv, mem_q_norm=m_mem_q_norm, mem_k_norm=m_mem_k_norm, w_br_attn=m_w_br_attn,
              w_br_conv=m_w_br_conv, w_br_mem=m_w_br_mem, w_out=m_w_out)
    vo = dict(norm_g=v_norm_g, mem_norm_g=v_mem_norm_g, w_in=v_w_in, attn_q_norm=v_attn_q_norm, attn_k_norm=v_attn_k_norm,
              conv_w=v_conv_w, mem_w_kv=v_mem_w_kv, mem_q_norm=v_mem_q_norm, mem_k_norm=v_mem_k_norm, w_br_attn=v_w_br_attn,
              w_br_conv=v_w_br_conv, w_br_mem=v_w_br_mem, w_out=v_w_out)
    B, S, D = x.shape
    me = (4 * lax.axis_index("x") + 2 * lax.axis_index("y") + lax.axis_index("c")).astype(jnp.int32)

    T = B * S
    x2, t2, mem2, ng = x.reshape(T, D), loss_target.reshape(T, D), mem.reshape(B * MEM_LEN, D), norm_g.reshape(1, D)
    h = _rms_fwd(x2, ng, "rms_x")
    rows_sharded, cols_sharded = ("mem_w_kv", "w_out"), ("w_br_attn", "w_br_conv", "w_br_mem")
    later = rows_sharded + cols_sharded
    later_axes = [BIG_AXIS[n] for n in later] + [None]
    conv_pad = jnp.pad(conv_w, ((0, 8 - conv_w.shape[0]), (0, 0)))
    proj, w_in_full, spread_a = _proj_gather(
        h, w_in.astype(bf16), _shard_order(),
        _spread_comm([w[n].astype(bf16) for n in rows_sharded], [BIG_AXIS[n] for n in rows_sharded]), comm_at=N_DEV - 3)
    *attn, spread_b = _attn_fwd(proj, attn_q_norm, attn_k_norm, B, S,
                                comm=_spread_comm([w[n].astype(bf16) for n in cols_sharded] + [conv_pad],
                                                  [BIG_AXIS[n] for n in cols_sharded] + [None]))
    *fulls, conv_g = _forward_blocks(spread_a + spread_b, later_axes)
    wf = dict(zip(later, fulls), w_in=w_in_full)
    conv_full = conv_g[:, :3, :].transpose(1, 0, 2).reshape(3, CONV_W)

    sq_err, g, t = _fwd_bwd_head(x2, mem2, t2, proj, attn, B, S, mem_norm_g, attn_q_norm, attn_k_norm, conv_full, mem_q_norm,
                                 mem_k_norm, wf["mem_w_kv"], wf["w_br_attn"], wf["w_br_conv"], wf["w_br_mem"], wf["w_out"])
    t.update(x2=x2, ng=ng, h=h)

    xyc = jnp.stack([lax.axis_index("x"), lax.axis_index("y"), lax.axis_index("c")]).astype(jnp.int32)
    me1 = me.reshape(1)
    early = ("w_out", "w_br_attn", "w_br_conv", "w_br_mem")
    g["mem_w_kv"], sib_early = _mm(t["mh"], t["dmkv"], mode="tn", out_dtype=f32, name="dw_kv",
                                   comm=_sibling_comm([g[n] for n in early], [BIG_AXIS[n] for n in early]))
    sums_early = [_rs_chip_sum(g[n], ls, xyc, BIG_AXIS[n], "rs_sum_" + n) for n, ls in zip(early, sib_early)]
    tm_w = _pick(D // 2, 1024)
    half = (D // 2) // tm_w
    g_top, (*chips_early, sib_kv) = _mm(t["h"], t["dproj"], mode="tn", out_dtype=f32, name="dw_in_top", tm=tm_w,
                                        m_tiles=(0, half),
                                        comm=_join(_chips_comm(sums_early), _sibling_comm([g["mem_w_kv"]], [0])))
    sum_kv = _rs_chip_sum(g["mem_w_kv"], sib_kv, xyc, 0, "rs_sum_mem_w_kv")
    g_bot, (chips_kv, sib_top) = _mm(t["h"], t["dproj"], mode="tn", out_dtype=f32, name="dw_in_bot", tm=tm_w,
                                     m_tiles=(half, half), comm=_join(_chips_comm([sum_kv]), _sibling_comm([g_top], [1])))
    sum_top = _rs_chip_sum(g_top, sib_top, xyc, 1, "rs_sum_w_in_top")
    tm_t = _pick(T // 2, 1024)
    halfm = (T // 2) // tm_t
    dh, (part_top, relay_top, sib_bot) = _mm(t["dproj"], wf["w_in"], mode="nt", out_dtype=f32, name="d_h_a", tm=tm_t, tk=D_H_TK,
                                             m_tiles=(0, halfm), into="new",
                                             comm=_join(_chips_first_hop([sum_top]), _sibling_comm([g_bot], [1])))
    sum_bot = _rs_chip_sum(g_bot, sib_bot, xyc, 1, "rs_sum_w_in_bot")
    dh, (part_bot, relay_bot, chips_top) = _mm(t["dproj"], wf["w_in"], mode="nt", out_dtype=f32, name="d_h_b", tm=tm_t,
                                               tk=D_H_TK, m_tiles=(halfm, halfm), into=dh,
                                               comm=_join(_chips_first_hop([sum_bot]), _chips_relay([relay_top], [part_top])))
    dx, dng, (chips_bot,) = _rms_bwd(t["x2"], t["ng"], dh, t["dy"], "rms_x_bwd", comm=_chips_relay([relay_bot], [part_bot]))
    g["norm_g"] = dng.reshape(D)

    grad, delta, new_m, new_v = {}, {}, {}, {}
    for n, ls, lc in zip(early + ("mem_w_kv",), sib_early + [sib_kv], chips_early + [chips_kv]):
        grad[n], delta[n], new_m[n], new_v[n] = _adamw_shard(g[n], ls, lc, w[n], mo[n], vo[n], me1, BIG_AXIS[n], "adamw_" + n)
    top = _adamw_shard(g_top, sib_top, chips_top, w_in, m_w_in, v_w_in, me1, 1, "adamw_w_in_top")
    grad["w_in"], delta["w_in"], new_m["w_in"], new_v["w_in"] = _adamw_shard(
        g_bot, sib_bot, chips_bot, w_in, m_w_in, v_w_in, me1, 1, "adamw_w_in_bot", row0=D // 2, prev=top)

    n_rep = sum(w[n].size for n in SMALL) // LANES
    conv_rows = g["conv_w"].reshape(3, N_DEV, LANES).transpose(1, 0, 2).reshape(3 * N_DEV, LANES)
    n_rows = -(-(n_rep + 3 * N_DEV + 1) // 8) * 8
    part = _pack_rows([g[n] for n in SMALL] + [conv_rows, jnp.full((1, LANES), sq_err, f32)], n_rows)
    tot = _allreduce_small(part)
    loss = tot[n_rep + 3 * N_DEV, 0] * (0.5 / D)
    n_small = -(-(n_rep + 3) // 8) * 8
    g_small = _pack_rows([tot[:n_rep], lax.dynamic_slice(tot, (n_rep + 3 * me, 0), (3, LANES))], n_small)
    names = SMALL + ("conv_w",)
    d_s, m_s, v_s = _adamw_small(g_small, _pack_rows([w[n] for n in names], n_small), _pack_rows([mo[n] for n in names], n_small),
                                 _pack_rows([vo[n] for n in names], n_small))
    off = 0
    for n in names:
        r = w[n].size // LANES
        grad[n], delta[n] = g_small[off:off + r].reshape(w[n].shape), d_s[off:off + r].reshape(w[n].shape)
        new_m[n], new_v[n] = m_s[off:off + r].reshape(w[n].shape), v_s[off:off + r].reshape(w[n].shape)
        off += r
    return (loss, dx.reshape(B, S, D), *[grad[n] for n in ALL_W], *[delta[n] for n in ALL_W], *[new_m[n] for n in ALL_W],
            *[new_v[n] for n in ALL_W])
```

```python
import functools

import jax
import jax.numpy as jnp
from jax import lax
from jax.experimental import pallas as pl
from jax.experimental.pallas import tpu as pltpu

f32 = jnp.float32
bf16 = jnp.bfloat16

N_DEV = 8
HEAD_DIM = 128
DILATIONS = (1, 4, 16)
N_GROUPS = 3
HEADS = 4
BLK = 128
ATTN_QKV = N_GROUPS * HEADS * HEAD_DIM
ATTN_OUT = HEADS * HEAD_DIM
CONV_W = 1024
MEM_LEN = 256
MEM_HEADS = 4
MEM_HD = 256
MEM_W = MEM_HEADS * MEM_HD
EPS = 1e-6
Q0, K0, V0 = 0, ATTN_QKV, 2 * ATTN_QKV
ZA = 3 * ATTN_QKV
CB, CC, CV, ZC = ZA + ATTN_OUT, ZA + ATTN_OUT + CONV_W, ZA + ATTN_OUT + 2 * CONV_W, ZA + ATTN_OUT + 3 * CONV_W
MQ = ZC + CONV_W
ZM = MQ + MEM_W
GT = ZM + MEM_W

ADAM_LR, ADAM_B1, ADAM_B2, ADAM_EPS, ADAM_WD, ADAM_STEP = 0.001, 0.9, 0.999, 1e-08, 0.01, 10

VMEM_LIMIT = 56 * 1024 * 1024
D_H_TK = 4352
LANES = 128

NT_DIMS = (((1,), (1,)), ((), ()))
TN_DIMS = (((0,), (0,)), ((), ()))
NN_DIMS = (((1,), (0,)), ((), ()))


def _params(sem=None):
    return pltpu.CompilerParams(dimension_semantics=sem, vmem_limit_bytes=VMEM_LIMIT)


def _pick(n, pref, q=LANES):
    t = (min(pref, n) // q) * q
    while t >= q:
        if n % t == 0:
            return t
        t -= q
    return n


def _dot(a, b, dims):
    return lax.dot_general(a.astype(bf16), b.astype(bf16), dims, preferred_element_type=f32)


def _sigmoid(z):
    return 0.5 * jnp.tanh(0.5 * z) + 0.5


def _rstd(v):
    return lax.rsqrt(jnp.mean(v * v, axis=-1, keepdims=True) + EPS)


class _Deferred:
    def __init__(self, stage, sems, step, last, n):
        assert last >= 1
        self.stage, self.sems, self.step, self.last, self.n = stage, sems, step, last, n
        self.slot = step % 2

    def _drain(self, slot, like):
        for c in range(self.n):
            pltpu.make_async_copy(self.stage.at[slot, c], like(c), self.sems.at[slot, c]).wait()

    def begin(self, like):
        pl.when(self.step >= 2)(lambda: self._drain(self.slot, like))

    def start(self, c, dst):
        pltpu.make_async_copy(self.stage.at[self.slot, c], dst, self.sems.at[self.slot, c]).start()

    def end(self, like):
        @pl.when(self.step == self.last)
        def _():
            self._drain(self.slot, like)
            self._drain(1 - self.slot, like)


def _row_sum(v):
    return _dot(v, jnp.ones((v.shape[1], v.shape[1]), bf16), NN_DIMS)


def _rstd_head(v):
    return lax.rsqrt(_row_sum(v * v) * (1.0 / v.shape[1]) + EPS)


class _Comm:
    def __init__(self, ins, out_shapes, sem_shape, copies, aliases=None):
        self.ins, self.out_shapes, self.sem_shape, self.copies = list(ins), list(out_shapes), sem_shape, copies
        self.aliases = dict(aliases or {})

    def scratch(self):
        return [pltpu.SemaphoreType.DMA(self.sem_shape), pltpu.SemaphoreType.DMA(self.sem_shape)]

    def io_aliases(self, first_in, first_out):
        return {first_in + k: first_out + v for k, v in self.aliases.items()}


def _mm(a, b, *, mode, out_dtype, name, tm=1024, tn=1024, tk=4096, m_tiles=None, into=None, comm=None):
    if mode == "nn":
        (M, K), (K2, N) = a.shape, b.shape
    elif mode == "nt":
        (M, K), (N, K2) = a.shape, b.shape
    else:
        (K, M), (K2, N) = a.shape, b.shape
    assert K == K2
    tm, tn, tk = _pick(M, tm), _pick(N, tn), _pick(K, tk)
    nk = K // tk
    m0, mc = m_tiles if m_tiles is not None else (0, M // tm)
    o0 = 0 if into is None else m0
    aliased = into is not None and not isinstance(into, str)
    n_ci = len(comm.ins) if comm else 0
    n_co = len(comm.out_shapes) if comm else 0
    grid = (mc, N // tn, nk)
    dims = {"nn": NN_DIMS, "nt": NT_DIMS, "tn": TN_DIMS}[mode]

    def body(*refs, nk):
        a_ref, b_ref = refs[:2]
        pos = 3 if aliased else 2
        c_ins, o_ref, c_outs = refs[pos:pos + n_ci], refs[pos + n_ci], refs[pos + n_ci + 1:pos + n_ci + 1 + n_co]
        scratch = refs[pos + n_ci + 1 + n_co:]
        ids = [pl.program_id(d) for d in range(3)]
        if comm:
            sems = scratch[-2:]

            @pl.when((ids[0] == 0) & (ids[1] == 0) & (ids[2] == 0))
            def _():
                for cp in comm.copies(c_ins, c_outs, *sems):
                    cp.start()

        if nk == 1:
            o_ref[...] = _dot(a_ref[...], b_ref[...], dims).astype(o_ref.dtype)
        else:
            acc_ref, k = scratch[0], ids[2]

            @pl.when(k == 0)
            def _():
                acc_ref[...] = _dot(a_ref[...], b_ref[...], dims)

            @pl.when((k > 0) & (k < nk - 1))
            def _():
                acc_ref[...] += _dot(a_ref[...], b_ref[...], dims)

            @pl.when(k == nk - 1)
            def _():
                o_ref[...] = (acc_ref[...] + _dot(a_ref[...], b_ref[...], dims)).astype(o_ref.dtype)

        if comm:
            @pl.when((ids[0] == grid[0] - 1) & (ids[1] == grid[1] - 1) & (ids[2] == grid[2] - 1))
            def _():
                for cp in comm.copies(c_ins, c_outs, *sems):
                    cp.wait()

    if mode == "tn":
        a_spec = pl.BlockSpec((tk, tm), lambda i, j, k: (k, m0 + i))
    else:
        a_spec = pl.BlockSpec((tm, tk), lambda i, j, k: (m0 + i, k))
    if mode == "nt":
        b_spec = pl.BlockSpec((tn, tk), lambda i, j, k: (j, k))
    else:
        b_spec = pl.BlockSpec((tk, tn), lambda i, j, k: (k, j))
    hbm = pl.BlockSpec(memory_space=pl.ANY)
    res = pl.pallas_call(
        functools.partial(body, nk=nk),
        grid=grid,
        in_specs=[a_spec, b_spec] + [hbm] * (aliased + n_ci),
        out_specs=[pl.BlockSpec((tm, tn), lambda i, j, k: (o0 + i, j))] + [hbm] * n_co,
        out_shape=[jax.ShapeDtypeStruct((mc * tm if into is None else M, N), out_dtype)] + (comm.out_shapes if comm else []),
        scratch_shapes=([pltpu.VMEM((tm, tn), f32)] if nk > 1 else []) + (comm.scratch() if comm else []),
        input_output_aliases={**({2: 0} if aliased else {}), **(comm.io_aliases(2 + aliased, 1) if comm else {})},
        compiler_params=_params(("arbitrary",) * 3 if comm else ("parallel", "parallel", "arbitrary")),
        name=name,
    )(a, b, *([into] if aliased else []), *(comm.ins if comm else []))
    return (res[0], list(res[1:])) if comm else res[0]


def _rms_fwd(x, g, name):
    T, D = x.shape
    tm = _pick(T, 256, 8)

    def body(x_ref, g_ref, h_ref):
        xv = x_ref[...]
        h_ref[...] = (xv * _rstd(xv) * g_ref[...]).astype(bf16)

    return pl.pallas_call(
        body, grid=(T // tm,),
        in_specs=[pl.BlockSpec((tm, D), lambda i: (i, 0)), pl.BlockSpec((1, D), lambda i: (0, 0))],
        out_specs=pl.BlockSpec((tm, D), lambda i: (i, 0)),
        out_shape=jax.ShapeDtypeStruct((T, D), bf16),
        compiler_params=_params(("parallel",)), name=name,
    )(x, g)


def _rms_bwd(x, g, dh, dy, name, comm=None):
    T, D = x.shape
    tm = _pick(T, 256, 8)
    with_dx = dy is not None
    n_ci = len(comm.ins) if comm else 0
    n_co = len(comm.out_shapes) if comm else 0

    def body(*refs):
        if with_dx:
            x_ref, g_ref, dh_ref, dy_ref = refs[:4]
            c_ins, (dx_ref, dg_ref) = refs[4:4 + n_ci], refs[4 + n_ci:6 + n_ci]
            c_outs, sems = refs[6 + n_ci:6 + n_ci + n_co], refs[6 + n_ci + n_co:]
        else:
            x_ref, g_ref, dh_ref, dg_ref = refs
        if comm:
            @pl.when(pl.program_id(0) == 0)
            def _():
                for cp in comm.copies(c_ins, c_outs, *sems):
                    cp.start()

        xv = x_ref[...]
        r = _rstd(xv)
        xh = xv * r
        dhv = dh_ref[...]
        part = jnp.sum(dhv * xh, axis=0, keepdims=True)

        @pl.when(pl.program_id(0) == 0)
        def _():
            dg_ref[...] = part

        @pl.when(pl.program_id(0) > 0)
        def _():
            dg_ref[...] += part

        if with_dx:
            dxh = dhv * g_ref[...]
            dx_ref[...] = dy_ref[...] + r * (dxh - xh * jnp.mean(dxh * xh, axis=-1, keepdims=True))

        if comm:
            @pl.when(pl.program_id(0) == T // tm - 1)
            def _():
                for cp in comm.copies(c_ins, c_outs, *sems):
                    cp.wait()

    row = pl.BlockSpec((tm, D), lambda i: (i, 0))
    vec = pl.BlockSpec((1, D), lambda i: (0, 0))
    hbm = pl.BlockSpec(memory_space=pl.ANY)
    if with_dx:
        res = pl.pallas_call(
            body, grid=(T // tm,), in_specs=[row, vec, row, row] + [hbm] * n_ci, out_specs=[row, vec] + [hbm] * n_co,
            out_shape=[jax.ShapeDtypeStruct((T, D), f32), jax.ShapeDtypeStruct((1, D), f32)] + (comm.out_shapes if comm else []),
            scratch_shapes=comm.scratch() if comm else [],
            input_output_aliases=comm.io_aliases(4, 2) if comm else {},
            compiler_params=_params(("arbitrary",)), name=name,
        )(x, g, dh, dy, *(comm.ins if comm else []))
        return (res[0], res[1], list(res[2:])) if comm else res
    return pl.pallas_call(
        body, grid=(T // tm,), in_specs=[row, vec, row], out_specs=vec,
        out_shape=jax.ShapeDtypeStruct((1, D), f32),
        compiler_params=_params(("arbitrary",)), name=name,
    )(x, g, dh)


MAX_UNIT_UNROLL = 6


def _unit_unroll(trips):
    return max(u for u in range(1, MAX_UNIT_UNROLL + 1) if trips % u == 0)


def _rows(start, size, stride):
    return pl.ds(start, size) if stride == 1 else pl.ds(start, size, stride=stride)


def _attn_units(S, d, first, prev):
    nb = S // d // BLK

    def do_first(r, c):
        first(_rows(r, BLK, d))
        return c

    lax.fori_loop(0, d, do_first, 0, unroll=_unit_unroll(d))
    if nb > 1:
        def do_prev(u, c):
            r = u // (nb - 1)
            b = u % (nb - 1) + 1
            base = r + d * b * BLK
            prev(_rows(base, BLK, d), _rows(base - d * BLK, 2 * BLK, d))
            return c

        lax.fori_loop(0, d * (nb - 1), do_prev, 0, unroll=_unit_unroll(d * (nb - 1)))


def _band_bias(nkeys):
    i = lax.broadcasted_iota(jnp.int32, (BLK, nkeys), 0)
    j = lax.broadcasted_iota(jnp.int32, (BLK, nkeys), 1)
    ok = (j <= i) if nkeys == BLK else ((j >= i) & (j <= i + BLK))
    return jnp.where(ok, 0.0, -jnp.inf).astype(f32)


def _normalize_into(src_ref, gain, dst_ref, S, rstd_ref=None):
    for c in range(S // 256):
        rows = pl.ds(c * 256, 256)
        v = src_ref[rows, :]
        r = _rstd_head(v)
        dst_ref[rows, :] = v * r * gain
        if rstd_ref is not None:
            rstd_ref[rows, :] = r


def _attn_fwd(proj, gq, gk, B, S, comm=None):
    T, NC = proj.shape
    scale = HEAD_DIM ** -0.5
    n_ci = len(comm.ins) if comm else 0
    n_co = len(comm.out_shapes) if comm else 0

    def body(*refs):
        q_ref, k_ref, v_ref, z_ref, gq_ref, gk_ref = refs[:6]
        c_ins = refs[6:6 + n_ci]
        ag_ref, a_ref, lse_ref = refs[6 + n_ci:9 + n_ci]
        c_outs = refs[9 + n_ci:9 + n_ci + n_co]
        qs, ks, o0, o1, o2, l0, l1, l2, bias1, bias2 = refs[9 + n_ci + n_co:19 + n_ci + n_co]
        sems = refs[19 + n_ci + n_co:]
        g = pl.program_id(2)
        if comm:
            @pl.when((pl.program_id(0) == 0) & (pl.program_id(1) == 0) & (g == 0))
            def _():
                for cp in comm.copies(c_ins, c_outs, *sems):
                    cp.start()

        bias1[...] = _band_bias(BLK)
        bias2[...] = _band_bias(2 * BLK)
        _normalize_into(q_ref, gq_ref[pl.ds(g, 1), :], qs, S)
        _normalize_into(k_ref, gk_ref[pl.ds(g, 1), :], ks, S)
        o_s, l_s = (o0, o1, o2), (l0, l1, l2)

        def run(gi):
            def unit(qr, kr, nkeys):
                s = _dot(qs[qr, :], ks[kr, :], NT_DIMS) * scale + (bias1 if nkeys == BLK else bias2)[...]
                m = jnp.max(s, axis=-1, keepdims=True)
                p = jnp.exp(s - m)
                den = jnp.sum(p, axis=-1, keepdims=True)
                o_s[gi][qr, :] = _dot(p, v_ref[kr, :], NN_DIMS) * (1.0 / den)
                l_s[gi][qr, :] = jnp.broadcast_to(m + jnp.log(den), (BLK, HEAD_DIM))

            _attn_units(S, DILATIONS[gi], lambda qr: unit(qr, qr, BLK), lambda qr, kr: unit(qr, kr, 2 * BLK))

        for gi in range(N_GROUPS):
            pl.when(g == gi)(functools.partial(run, gi))

        @pl.when(g == N_GROUPS - 1)
        def _():
            for c in range(S // 256):
                rows = pl.ds(c * 256, 256)
                la, lb, lc = l0[rows, :], l1[rows, :], l2[rows, :]
                m = jnp.maximum(jnp.maximum(la, lb), lc)
                wa, wb, wc = jnp.exp(la - m), jnp.exp(lb - m), jnp.exp(lc - m)
                tot = wa + wb + wc
                a = (wa / tot) * o0[rows, :] + (wb / tot) * o1[rows, :] + (wc / tot) * o2[rows, :]
                z = z_ref[rows, :]
                a_ref[rows, :] = a
                lse_ref[rows, :] = m + jnp.log(tot)
                ag_ref[rows, :] = (a * (z * _sigmoid(z))).astype(bf16)

        if comm:
            @pl.when((pl.program_id(0) == B - 1) & (pl.program_id(1) == HEADS - 1) & (g == N_GROUPS - 1))
            def _():
                for cp in comm.copies(c_ins, c_outs, *sems):
                    cp.wait()

    def slab(col0):
        return pl.BlockSpec((S, HEAD_DIM), lambda b, h, g: (b, col0 // HEAD_DIM + g * HEADS + h))

    gain = pl.BlockSpec((N_GROUPS, HEAD_DIM), lambda b, h, g: (0, 0))
    out = pl.BlockSpec((S, HEAD_DIM), lambda b, h, g: (b, h))
    hbm = pl.BlockSpec(memory_space=pl.ANY)
    res = pl.pallas_call(
        body, grid=(B, HEADS, N_GROUPS),
        in_specs=[slab(Q0), slab(K0), slab(V0), pl.BlockSpec((S, HEAD_DIM), lambda b, h, g: (b, ZA // HEAD_DIM + h)), gain, gain]
        + [hbm] * n_ci,
        out_specs=[out, out, out] + [hbm] * n_co,
        out_shape=[jax.ShapeDtypeStruct((T, ATTN_OUT), bf16), jax.ShapeDtypeStruct((T, ATTN_OUT), f32),
                   jax.ShapeDtypeStruct((T, ATTN_OUT), f32)] + (comm.out_shapes if comm else []),
        scratch_shapes=[pltpu.VMEM((S, HEAD_DIM), f32)] * 8 + [pltpu.VMEM((BLK, BLK), f32), pltpu.VMEM((BLK, 2 * BLK), f32)]
        + (comm.scratch() if comm else []),
        compiler_params=_params(("arbitrary", "arbitrary", "arbitrary")), name="attn_fwd",
    )(proj, proj, proj, proj, gq, gk, *(comm.ins if comm else []))
    return res[0], res[1], res[2], list(res[3:])


def _rms_bwd_rows(raw, gain, dn, on_mxu=True, r=None):
    if r is None:
        r = _rstd_head(raw) if on_mxu else _rstd(raw)
    xh = raw * r
    dxh = dn * gain
    if on_mxu:
        mean = _row_sum(dxh * xh) * (1.0 / raw.shape[1])
    else:
        mean = jnp.mean(dxh * xh, axis=-1, keepdims=True)
    return r * (dxh - xh * mean), jnp.sum(dn * xh, axis=0, keepdims=True)


def _attn_bwd(proj, gq, gk, a_comb, lse, dag, dproj, B, S):
    T, NC = proj.shape
    scale = HEAD_DIM ** -0.5

    def body(q_ref, k_ref, v_ref, z_ref, gq_ref, gk_ref, a_ref, lse_ref, dag_ref, dproj_in, dproj_ref, dgq_ref, dgk_ref,
             qs, ks, da_s, dl_s, dq_s, dk_s, dv_s, rq_s, rk_s, bias1, bias2, stage, dz_stage, sem, dz_sem):
        b, h, g = pl.program_id(0), pl.program_id(1), pl.program_id(2)
        bias1[...] = _band_bias(BLK)
        bias2[...] = _band_bias(2 * BLK)
        gq_row, gk_row = gq_ref[pl.ds(g, 1), :], gk_ref[pl.ds(g, 1), :]
        _normalize_into(q_ref, gq_row, qs, S, rq_s)
        _normalize_into(k_ref, gk_row, ks, S, rk_s)

        def dst(c):
            return dproj_ref.at[pl.ds(b * S, S), pl.ds((Q0, K0, V0)[c] + (g * HEADS + h) * HEAD_DIM, HEAD_DIM)]

        out = _Deferred(stage, sem, (b * HEADS + h) * N_GROUPS + g, B * HEADS * N_GROUPS - 1, 3)
        out.begin(dst)

        @pl.when((b == 0) & (h == 0) & (g == 0))
        def _():
            dgq_ref[...] = jnp.zeros_like(dgq_ref)
            dgk_ref[...] = jnp.zeros_like(dgk_ref)

        @pl.when(g == 0)
        def _():
            for c in range(S // 256):
                rows = pl.ds(c * 256, 256)
                z, a, dg_ = z_ref[rows, :], a_ref[rows, :], dag_ref[rows, :]
                sg = _sigmoid(z)
                da = dg_ * (z * sg)
                da_s[rows, :] = da
                dl_s[rows, :] = _row_sum(da * a)
                dz_stage[rows, :] = (dg_ * a * (sg * (1.0 + z * (1.0 - sg)))).astype(bf16)
            cp = pltpu.make_async_copy(dz_stage, dproj_ref.at[pl.ds(b * S, S), pl.ds(ZA + h * HEAD_DIM, HEAD_DIM)], dz_sem)
            cp.start()
            cp.wait()

        dk_s[...] = jnp.zeros_like(dk_s)
        dv_s[...] = jnp.zeros_like(dv_s)

        def run(gi):
            def unit(qr, kr, nkeys):
                q, k, v = qs[qr, :], ks[kr, :], v_ref[kr, :]
                s = _dot(q, k, NT_DIMS) * scale
                p = jnp.exp(s + (bias1 if nkeys == BLK else bias2)[...] - lse_ref[qr, :][:, :1])
                da = da_s[qr, :]
                dp = _dot(da, v, NT_DIMS)
                ds = p * (dp - dl_s[qr, :][:, :1]) * scale
                dq_s[qr, :] = _dot(ds, k, NN_DIMS)
                dk_s[kr, :] += _dot(ds, q, TN_DIMS)
                dv_s[kr, :] += _dot(p, da, TN_DIMS)

            _attn_units(S, DILATIONS[gi], lambda qr: unit(qr, qr, BLK), lambda qr, kr: unit(qr, kr, 2 * BLK))

        for gi in range(N_GROUPS):
            pl.when(g == gi)(functools.partial(run, gi))

        gq_acc = jnp.zeros((1, HEAD_DIM), f32)
        gk_acc = jnp.zeros((1, HEAD_DIM), f32)
        for c in range(S // 256):
            rows = pl.ds(c * 256, 256)
            dq, gq_p = _rms_bwd_rows(q_ref[rows, :], gq_row, dq_s[rows, :], r=rq_s[rows, :])
            dk, gk_p = _rms_bwd_rows(k_ref[rows, :], gk_row, dk_s[rows, :], r=rk_s[rows, :])
            gq_acc, gk_acc = gq_acc + gq_p, gk_acc + gk_p
            stage[out.slot, 0, rows, :] = dq.astype(bf16)
            stage[out.slot, 1, rows, :] = dk.astype(bf16)
            stage[out.slot, 2, rows, :] = dv_s[rows, :].astype(bf16)
        dgq_ref[pl.ds(g, 1), :] += gq_acc
        dgk_ref[pl.ds(g, 1), :] += gk_acc
        for c in range(3):
            out.start(c, dst(c))
        out.end(dst)

    def slab(col0):
        return pl.BlockSpec((S, HEAD_DIM), lambda b, h, g: (b, col0 // HEAD_DIM + g * HEADS + h))

    gain = pl.BlockSpec((N_GROUPS, HEAD_DIM), lambda b, h, g: (0, 0))
    per_slot = pl.BlockSpec((S, HEAD_DIM), lambda b, h, g: (b, h))
    return pl.pallas_call(
        body, grid=(B, HEADS, N_GROUPS),
        in_specs=[slab(Q0), slab(K0), slab(V0), pl.BlockSpec((S, HEAD_DIM), lambda b, h, g: (b, ZA // HEAD_DIM + h)), gain, gain,
                  per_slot, per_slot, per_slot, pl.BlockSpec(memory_space=pl.ANY)],
        out_specs=[pl.BlockSpec(memory_space=pl.ANY), gain, gain],
        out_shape=[jax.ShapeDtypeStruct(dproj.shape, dproj.dtype), jax.ShapeDtypeStruct((N_GROUPS, HEAD_DIM), f32),
                   jax.ShapeDtypeStruct((N_GROUPS, HEAD_DIM), f32)],
        scratch_shapes=[pltpu.VMEM((S, HEAD_DIM), f32)] * 9 + [pltpu.VMEM((BLK, BLK), f32), pltpu.VMEM((BLK, 2 * BLK), f32),
                                                                pltpu.VMEM((2, 3, S, HEAD_DIM), bf16), pltpu.VMEM((S, HEAD_DIM), bf16),
                                                                pltpu.SemaphoreType.DMA((2, 3)), pltpu.SemaphoreType.DMA],
        input_output_aliases={9: 0},
        compiler_params=_params(("arbitrary", "arbitrary", "arbitrary")), name="attn_bwd",
    )(proj, proj, proj, proj, gq, gk, a_comb, lse, dag, dproj)


def _conv_fwd(proj, conv_w, B, S):
    T, NC = proj.shape
    tc = LANES

    def body(cb_ref, cc_ref, cv_ref, z_ref, w_ref, c_ref, ub):
        u = cc_ref[...] * cv_ref[...]
        ub[0:8, :] = jnp.zeros((8, tc), f32)
        ub[8:8 + S, :] = u
        w = w_ref[...]
        y = w[0:1] * u + w[1:2] * ub[pl.ds(7, S), :] + w[2:3] * ub[pl.ds(6, S), :]
        z = z_ref[...]
        c_ref[...] = (cb_ref[...] * y * (z * _sigmoid(z))).astype(bf16)

    def seg(col0):
        return pl.BlockSpec((S, tc), lambda j, b: (b, col0 // tc + j))

    return pl.pallas_call(
        body, grid=(CONV_W // tc, B),
        in_specs=[seg(CB), seg(CC), seg(CV), seg(ZC), pl.BlockSpec((3, tc), lambda j, b: (0, j))],
        out_specs=pl.BlockSpec((S, tc), lambda j, b: (b, j)),
        out_shape=jax.ShapeDtypeStruct((T, CONV_W), bf16),
        scratch_shapes=[pltpu.VMEM((S + 8, tc), f32)],
        compiler_params=_params(("parallel", "parallel")), name="conv_fwd",
    )(proj, proj, proj, proj, conv_w)


def _conv_bwd(proj, conv_w, dc, dproj, B, S):
    T, NC = proj.shape
    tc = LANES

    def body(cb_ref, cc_ref, cv_ref, z_ref, w_ref, dc_ref, dproj_in, dproj_ref, dw_ref, ub, db, stage, sem):
        j, b = pl.program_id(0), pl.program_id(1)

        def dst(c):
            return dproj_ref.at[pl.ds(b * S, S), pl.ds((CB, CC, CV, ZC)[c] + j * tc, tc)]

        out = _Deferred(stage, sem, j * B + b, (CONV_W // tc) * B - 1, 4)
        out.begin(dst)
        cb, cc, cv, z, dcv = cb_ref[...], cc_ref[...], cv_ref[...], z_ref[...], dc_ref[...]
        u = cc * cv
        ub[0:8, :] = jnp.zeros((8, tc), f32)
        ub[8:8 + S, :] = u
        u1, u2 = ub[pl.ds(7, S), :], ub[pl.ds(6, S), :]
        w = w_ref[...]
        y = w[0:1] * u + w[1:2] * u1 + w[2:3] * u2
        sg = _sigmoid(z)
        si = z * sg
        dyv = dcv * cb * si
        db[0:S, :] = dyv
        db[S:S + 8, :] = jnp.zeros((8, tc), f32)
        du = w[0:1] * dyv + w[1:2] * db[pl.ds(1, S), :] + w[2:3] * db[pl.ds(2, S), :]
        stage[out.slot, 0] = (dcv * y * si).astype(bf16)
        stage[out.slot, 1] = (du * cv).astype(bf16)
        stage[out.slot, 2] = (du * cc).astype(bf16)
        stage[out.slot, 3] = (dcv * cb * y * (sg * (1.0 + z * (1.0 - sg)))).astype(bf16)
        for c in range(4):
            out.start(c, dst(c))
        part = jnp.concatenate([jnp.sum(dyv * u, axis=0, keepdims=True), jnp.sum(dyv * u1, axis=0, keepdims=True),
                                jnp.sum(dyv * u2, axis=0, keepdims=True)], axis=0)

        @pl.when(b == 0)
        def _():
            dw_ref[...] = part

        @pl.when(b > 0)
        def _():
            dw_ref[...] += part

        out.end(dst)

    def seg(col0):
        return pl.BlockSpec((S, tc), lambda j, b: (b, col0 // tc + j))

    return pl.pallas_call(
        body, grid=(CONV_W // tc, B),
        in_specs=[seg(CB), seg(CC), seg(CV), seg(ZC), pl.BlockSpec((3, tc), lambda j, b: (0, j)),
                  pl.BlockSpec((S, tc), lambda j, b: (b, j)), pl.BlockSpec(memory_space=pl.ANY)],
        out_specs=[pl.BlockSpec(memory_space=pl.ANY), pl.BlockSpec((3, tc), lambda j, b: (0, j))],
        out_shape=[jax.ShapeDtypeStruct(dproj.shape, dproj.dtype), jax.ShapeDtypeStruct((3, CONV_W), f32)],
        scratch_shapes=[pltpu.VMEM((S + 8, tc), f32), pltpu.VMEM((S + 8, tc), f32), pltpu.VMEM((2, 4, S, tc), bf16),
                        pltpu.SemaphoreType.DMA((2, 4))],
        input_output_aliases={6: 0},
        compiler_params=_params(("arbitrary", "arbitrary")), name="conv_bwd",
    )(proj, proj, proj, proj, conv_w, dc, dproj)


MEM_CHUNK = 1024


def _mem_fwd(proj, mkv, gq, gk, B, S):
    T, NC = proj.shape
    scale = MEM_HD ** -0.5

    def body(q_ref, z_ref, k_ref, v_ref, gq_ref, gk_ref, o_ref):
        kv = k_ref[...]
        kn = (kv * _rstd_head(kv) * gk_ref[...]).astype(bf16)
        vv = v_ref[...].astype(bf16)

        def chunk(c, carry):
            rows = pl.ds(pl.multiple_of(c * MEM_CHUNK, MEM_CHUNK), MEM_CHUNK)
            q = q_ref[rows, :]
            qn = q * _rstd(q) * gq_ref[...]
            s = _dot(qn, kn, NT_DIMS) * scale
            p = jnp.exp(s - jnp.max(s, axis=-1, keepdims=True))
            p = p / jnp.sum(p, axis=-1, keepdims=True)
            z = z_ref[rows, :]
            o_ref[rows, :] = (_dot(p, vv, NN_DIMS) * (z * _sigmoid(z))).astype(bf16)
            return carry

        lax.fori_loop(0, S // MEM_CHUNK, chunk, 0)

    gain = pl.BlockSpec((1, MEM_HD), lambda b, h: (0, 0))
    return pl.pallas_call(
        body, grid=(B, MEM_HEADS),
        in_specs=[pl.BlockSpec((S, MEM_HD), lambda b, h: (b, MQ // MEM_HD + h)),
                  pl.BlockSpec((S, MEM_HD), lambda b, h: (b, ZM // MEM_HD + h)),
                  pl.BlockSpec((MEM_LEN, MEM_HD), lambda b, h: (b, h)),
                  pl.BlockSpec((MEM_LEN, MEM_HD), lambda b, h: (b, MEM_HEADS + h)), gain, gain],
        out_specs=pl.BlockSpec((S, MEM_HD), lambda b, h: (b, h)),
        out_shape=jax.ShapeDtypeStruct((T, MEM_W), bf16),
        compiler_params=_params(("parallel", "parallel")), name="mem_fwd",
    )(proj, proj, mkv, mkv, gq, gk)


def _mem_bwd(proj, mkv, gq, gk, dmo, dproj, B, S):
    T, NC = proj.shape
    scale = MEM_HD ** -0.5

    def body(q_ref, z_ref, k_ref, v_ref, gq_ref, gk_ref, dmo_ref, dproj_in, dproj_ref, dmk_ref, dmv_ref, dgq_ref, dgk_ref,
             dkn_s, dv_s, gq_s, stage, sem):
        b, h = pl.program_id(0), pl.program_id(1)

        def dst(c):
            return dproj_ref.at[pl.ds(b * S, S), pl.ds((MQ, ZM)[c] + h * MEM_HD, MEM_HD)]

        out = _Deferred(stage, sem, b * MEM_HEADS + h, B * MEM_HEADS - 1, 2)
        out.begin(dst)
        kv = k_ref[...]
        kn = (kv * _rstd_head(kv) * gk_ref[...]).astype(bf16)
        vv = v_ref[...].astype(bf16)
        dkn_s[...] = jnp.zeros_like(dkn_s)
        dv_s[...] = jnp.zeros_like(dv_s)
        gq_s[...] = jnp.zeros_like(gq_s)

        def chunk(c, carry):
            rows = pl.ds(pl.multiple_of(c * MEM_CHUNK, MEM_CHUNK), MEM_CHUNK)
            q = q_ref[rows, :]
            qn = q * _rstd(q) * gq_ref[...]
            s = _dot(qn, kn, NT_DIMS) * scale
            p = jnp.exp(s - jnp.max(s, axis=-1, keepdims=True))
            p = p / jnp.sum(p, axis=-1, keepdims=True)
            mo = _dot(p, vv, NN_DIMS)
            z, dg_ = z_ref[rows, :], dmo_ref[rows, :]
            sg = _sigmoid(z)
            do = dg_ * (z * sg)
            stage[out.slot, 1, rows, :] = (dg_ * mo * (sg * (1.0 + z * (1.0 - sg)))).astype(bf16)
            dp = _dot(do, vv, NT_DIMS)
            ds = p * (dp - jnp.sum(do * mo, axis=-1, keepdims=True)) * scale
            dq, gq_p = _rms_bwd_rows(q, gq_ref[...], _dot(ds, kn, NN_DIMS), on_mxu=False)
            stage[out.slot, 0, rows, :] = dq.astype(bf16)
            gq_s[...] += gq_p
            dkn_s[...] += _dot(ds, qn, TN_DIMS)
            dv_s[...] += _dot(p, do, TN_DIMS)
            return carry

        lax.fori_loop(0, S // MEM_CHUNK, chunk, 0)
        for c in range(2):
            out.start(c, dst(c))
        dk, gk_p = _rms_bwd_rows(kv, gk_ref[...], dkn_s[...])
        dmk_ref[...] = dk
        dmv_ref[...] = dv_s[...]

        @pl.when((b == 0) & (h == 0))
        def _():
            dgq_ref[...] = gq_s[...]
            dgk_ref[...] = gk_p

        @pl.when((b > 0) | (h > 0))
        def _():
            dgq_ref[...] += gq_s[...]
            dgk_ref[...] += gk_p

        out.end(dst)

    gain = pl.BlockSpec((1, MEM_HD), lambda b, h: (0, 0))
    kvb = pl.BlockSpec((MEM_LEN, MEM_HD), lambda b, h: (b, h))
    return pl.pallas_call(
        body, grid=(B, MEM_HEADS),
        in_specs=[pl.BlockSpec((S, MEM_HD), lambda b, h: (b, MQ // MEM_HD + h)),
                  pl.BlockSpec((S, MEM_HD), lambda b, h: (b, ZM // MEM_HD + h)),
                  kvb, pl.BlockSpec((MEM_LEN, MEM_HD), lambda b, h: (b, MEM_HEADS + h)), gain, gain,
                  pl.BlockSpec((S, MEM_HD), lambda b, h: (b, h)), pl.BlockSpec(memory_space=pl.ANY)],
        out_specs=[pl.BlockSpec(memory_space=pl.ANY), kvb, kvb, gain, gain],
        out_shape=[jax.ShapeDtypeStruct(dproj.shape, dproj.dtype), jax.ShapeDtypeStruct((B * MEM_LEN, MEM_W), f32),
                   jax.ShapeDtypeStruct((B * MEM_LEN, MEM_W), f32), jax.ShapeDtypeStruct((1, MEM_HD), f32),
                   jax.ShapeDtypeStruct((1, MEM_HD), f32)],
        scratch_shapes=[pltpu.VMEM((MEM_LEN, MEM_HD), f32), pltpu.VMEM((MEM_LEN, MEM_HD), f32), pltpu.VMEM((1, MEM_HD), f32),
                        pltpu.VMEM((2, 2, S, MEM_HD), bf16), pltpu.SemaphoreType.DMA((2, 2))],
        input_output_aliases={7: 0},
        compiler_params=_params(("arbitrary", "arbitrary")), name="mem_bwd",
    )(proj, proj, mkv, mkv, gq, gk, dmo, dproj)


def _merge_fwd(proj, ag, cg, mg, wa, wc, wm):
    T, NC = proj.shape
    D = wa.shape[1]
    tm, tn = _pick(T, 1024), _pick(D, 512)
    assert GT % tn == 0

    def body(ag_ref, cg_ref, mg_ref, wa_ref, wc_ref, wm_ref, g0_ref, g1_ref, g2_ref, mer_ref, ba_ref, bc_ref, bm_ref):
        ba = _dot(ag_ref[...], wa_ref[...], NN_DIMS)
        bc = _dot(cg_ref[...], wc_ref[...], NN_DIMS)
        bm = _dot(mg_ref[...], wm_ref[...], NN_DIMS)
        mer_ref[...] = (_sigmoid(g0_ref[...]) * ba + _sigmoid(g1_ref[...]) * bc + _sigmoid(g2_ref[...]) * bm).astype(bf16)
        ba_ref[...] = ba.astype(bf16)
        bc_ref[...] = bc.astype(bf16)
        bm_ref[...] = bm.astype(bf16)

    def act(w):
        return pl.BlockSpec((tm, w), lambda i, j: (i, 0))

    def wt(k):
        return pl.BlockSpec((k, tn), lambda i, j: (0, j))

    def gate(n):
        return pl.BlockSpec((tm, tn), lambda i, j: (i, (GT + n * D) // tn + j))

    out = pl.BlockSpec((tm, tn), lambda i, j: (i, j))
    return pl.pallas_call(
        body, grid=(T // tm, D // tn),
        in_specs=[act(ATTN_OUT), act(CONV_W), act(MEM_W), wt(ATTN_OUT), wt(CONV_W), wt(MEM_W), gate(0), gate(1), gate(2)],
        out_specs=[out] * 4, out_shape=[jax.ShapeDtypeStruct((T, D), bf16)] * 4,
        compiler_params=_params(("parallel", "parallel")), name="merge_fwd",
    )(ag, cg, mg, wa, wc, wm, proj, proj, proj)


def _out_loss(merged, w_out, x, tgt):
    T, D = x.shape
    tm, tn = _pick(T, 1024), _pick(D, 512)

    def body(m_ref, w_ref, x_ref, t_ref, dy_ref, dyb_ref, lp_ref):
        e = x_ref[...] + _dot(m_ref[...], w_ref[...], NN_DIMS) - t_ref[...]
        dy = e / D
        dy_ref[...] = dy
        dyb_ref[...] = dy.astype(bf16)
        part = jnp.full((1, 8, LANES), jnp.sum(e * e), f32)

        @pl.when(pl.program_id(1) == 0)
        def _():
            lp_ref[...] = part

        @pl.when(pl.program_id(1) > 0)
        def _():
            lp_ref[...] += part

    tile = pl.BlockSpec((tm, tn), lambda i, j: (i, j))
    return pl.pallas_call(
        body, grid=(T // tm, D // tn),
        in_specs=[pl.BlockSpec((tm, D), lambda i, j: (i, 0)), pl.BlockSpec((D, tn), lambda i, j: (0, j)), tile, tile],
        out_specs=[tile, tile, pl.BlockSpec((1, 8, LANES), lambda i, j: (i, 0, 0))],
        out_shape=[jax.ShapeDtypeStruct((T, D), f32), jax.ShapeDtypeStruct((T, D), bf16),
                   jax.ShapeDtypeStruct((T // tm, 8, LANES), f32)],
        compiler_params=_params(("parallel", "arbitrary")), name="out_loss",
    )(merged, w_out, x, tgt)


def _merge_bwd(proj, dyb, w_out, ba, bc, bm):
    T, NC = proj.shape
    D = w_out.shape[0]
    tm, tn = _pick(T, 1024), _pick(D, 512)
    assert GT % tn == 0

    def body(dy_ref, w_ref, ba_ref, bc_ref, bm_ref, g0_ref, g1_ref, g2_ref, da_ref, dc_ref, dm_ref, dproj_ref, stage, sem):
        i, j = pl.program_id(0), pl.program_id(1)

        def dst(n):
            return dproj_ref.at[pl.ds(i * tm, tm), pl.ds(GT + n * D + j * tn, tn)]

        out = _Deferred(stage, sem, i * (D // tn) + j, (T // tm) * (D // tn) - 1, 3)
        out.begin(dst)
        dmer = _dot(dy_ref[...], w_ref[...], NT_DIMS)
        for n, (g_ref, br_ref, o_ref) in enumerate(((g0_ref, ba_ref, da_ref), (g1_ref, bc_ref, dc_ref), (g2_ref, bm_ref, dm_ref))):
            sg = _sigmoid(g_ref[...])
            o_ref[...] = (sg * dmer).astype(bf16)
            stage[out.slot, n] = (dmer * br_ref[...].astype(f32) * (sg * (1.0 - sg))).astype(bf16)
            out.start(n, dst(n))
        out.end(dst)

    tile = pl.BlockSpec((tm, tn), lambda i, j: (i, j))

    def gate(n):
        return pl.BlockSpec((tm, tn), lambda i, j: (i, (GT + n * D) // tn + j))

    return pl.pallas_call(
        body, grid=(T // tm, D // tn),
        in_specs=[pl.BlockSpec((tm, D), lambda i, j: (i, 0)), pl.BlockSpec((tn, D), lambda i, j: (j, 0)), tile, tile, tile,
                  gate(0), gate(1), gate(2)],
        out_specs=[tile, tile, tile, pl.BlockSpec(memory_space=pl.ANY)],
        out_shape=[jax.ShapeDtypeStruct((T, D), bf16)] * 3 + [jax.ShapeDtypeStruct((T, NC), bf16)],
        scratch_shapes=[pltpu.VMEM((2, 3, tm, tn), bf16), pltpu.SemaphoreType.DMA((2, 3))],
        compiler_params=_params(("arbitrary", "arbitrary")), name="merge_bwd",
    )(dyb, w_out, ba, bc, bm, proj, proj, proj)


def _fwd_bwd_head(x2, mem2, t2, proj, attn, B, S, mem_norm_g, attn_q_norm, attn_k_norm, conv_w, mem_q_norm, mem_k_norm,
                  w_kv, w_a, w_c, w_m, w_out):
    D = x2.shape[1]
    mng = mem_norm_g.reshape(1, D)
    mqn, mkn = mem_q_norm.reshape(1, MEM_HD), mem_k_norm.reshape(1, MEM_HD)
    ag, a_comb, lse = attn

    mh = _rms_fwd(mem2, mng, "rms_mem")
    mkv = _mm(mh, w_kv, mode="nn", out_dtype=f32, name="mkv")
    cg = _conv_fwd(proj, conv_w, B, S)
    mg = _mem_fwd(proj, mkv, mqn, mkn, B, S)
    merged, ba, bc, bm = _merge_fwd(proj, ag, cg, mg, w_a, w_c, w_m)
    dy, dyb, lp = _out_loss(merged, w_out, x2, t2)
    sq_err = jnp.sum(lp[:, 0, 0])

    g = {}
    g["w_out"] = _mm(merged, dyb, mode="tn", out_dtype=f32, name="dw_out")
    dba, dbc, dbm, dproj = _merge_bwd(proj, dyb, w_out, ba, bc, bm)
    g["w_br_attn"] = _mm(ag, dba, mode="tn", out_dtype=f32, name="dw_br_attn")
    g["w_br_conv"] = _mm(cg, dbc, mode="tn", out_dtype=f32, name="dw_br_conv")
    g["w_br_mem"] = _mm(mg, dbm, mode="tn", out_dtype=f32, name="dw_br_mem")
    dag = _mm(dba, w_a, mode="nt", out_dtype=f32, name="d_attn_out")
    dcg = _mm(dbc, w_c, mode="nt", out_dtype=f32, name="d_conv_out")
    dmg = _mm(dbm, w_m, mode="nt", out_dtype=f32, name="d_mem_out")
    dproj, g["attn_q_norm"], g["attn_k_norm"] = _attn_bwd(proj, attn_q_norm, attn_k_norm, a_comb, lse, dag, dproj, B, S)
    dproj, g["conv_w"] = _conv_bwd(proj, conv_w, dcg, dproj, B, S)
    dproj, dmk, dmv, dgmq, dgmk = _mem_bwd(proj, mkv, mqn, mkn, dmg, dproj, B, S)
    g["mem_q_norm"], g["mem_k_norm"] = dgmq.reshape(MEM_HD), dgmk.reshape(MEM_HD)
    dmkv = jnp.concatenate([dmk, dmv], axis=1)
    dmh = _mm(dmkv, w_kv, mode="nt", out_dtype=f32, name="d_mem_h")
    g["mem_norm_g"] = _rms_bwd(mem2, mng, dmh, None, "rms_mem_bwd").reshape(D)
    return sq_err, g, dict(dy=dy, dproj=dproj, mh=mh, dmkv=dmkv)


MESH = pl.DeviceIdType.MESH
HBM = pl.BlockSpec(memory_space=pl.ANY)


def _me():
    x, y, c = lax.axis_index("x"), lax.axis_index("y"), lax.axis_index("c")
    return (x, y, c), 4 * x + 2 * y + c


def _peer(k):
    (x, y, c), _ = _me()
    p = (1 - x if k & 4 else x, 1 - y if k & 2 else y, 1 - c if k & 1 else c)
    return p, 4 * p[0] + 2 * p[1] + p[2]


def _window(ref, axis, size, idx):
    if axis is None:
        return ref.at[idx]
    if axis == 0:
        return ref.at[pl.ds(idx * size, size), :]
    return ref.at[:, pl.ds(idx * size, size)]


def _full_shape(s, axis):
    if axis is None:
        return (N_DEV,) + s.shape
    return tuple(N_DEV * d if i == axis else d for i, d in enumerate(s.shape))


OTHER_CHIPS = (4, 2, 6)


def _spread_comm(shards, axes):
    n = len(shards)

    def copies(ins, outs, send_sems, recv_sems, base=0):
        _, me = _me()
        out = []
        for a in range(n):
            mine = _window(outs[a], axes[a], None if axes[a] is None else shards[a].shape[axes[a]], me)
            out.append(pltpu.make_async_copy(ins[a], mine, send_sems.at[base + a, N_CHIP]))
            for k, dist in enumerate((1,) + OTHER_CHIPS):
                dev, _ = _peer(dist)
                out.append(pltpu.make_async_remote_copy(
                    src_ref=ins[a], dst_ref=mine, send_sem=send_sems.at[base + a, k], recv_sem=recv_sems.at[base + a, k],
                    device_id=dev, device_id_type=MESH))
        return out

    shapes = [jax.ShapeDtypeStruct(_full_shape(s, ax), s.dtype) for s, ax in zip(shards, axes)]
    return _Comm(shards, shapes, (n, N_CHIP + 1), copies)


def _forward_blocks(fulls, axes):
    n = len(fulls)
    sizes = [None if ax is None else f.shape[ax] // N_DEV for f, ax in zip(fulls, axes)]

    def body(*refs):
        outs = refs[n:2 * n]
        send_sems, recv_sems = refs[2 * n:]
        sibling, _ = _peer(1)
        copies = []
        for j, dist in enumerate(OTHER_CHIPS):
            _, held = _peer(dist)
            for a in range(n):
                block = _window(outs[a], axes[a], sizes[a], held)
                copies.append(pltpu.make_async_remote_copy(
                    src_ref=block, dst_ref=block, send_sem=send_sems.at[a, j], recv_sem=recv_sems.at[a, j],
                    device_id=sibling, device_id_type=MESH))
        for cp in copies:
            cp.start()
        for cp in copies:
            cp.wait()

    return pl.pallas_call(
        body, in_specs=[HBM] * n, out_specs=[HBM] * n,
        out_shape=[jax.ShapeDtypeStruct(f.shape, f.dtype) for f in fulls],
        scratch_shapes=[pltpu.SemaphoreType.DMA((n, len(OTHER_CHIPS))), pltpu.SemaphoreType.DMA((n, len(OTHER_CHIPS)))],
        input_output_aliases={a: a for a in range(n)},
        name="forward_blocks",
    )(*fulls)


def _shard_order():
    (x, y, c), me = _me()
    xs, ys, xo, yo = _peer(4)[1], _peer(2)[1], _peer(5)[1], _peer(3)[1]
    north = c == 1
    ids = [me, _peer(1)[1], jnp.where(north, xs, ys), jnp.where(north, yo, xo), jnp.where(north, ys, xs),
           jnp.where(north, xo, yo), _peer(6)[1], _peer(7)[1]]
    return jnp.stack(ids).astype(jnp.int32)


def _proj_gather(h, w_shard, order, comm, comm_at):
    T, K = h.shape
    C = w_shard.shape[1]
    tm = _pick(T, 1024)
    nm = T // tm
    ahead = max(nm - 2, 0)
    n_ci, n_co = len(comm.ins), len(comm.out_shapes)

    def body(*refs):
        order_ref, h_ref, ws_ref = refs[:3]
        c_ins = refs[3:3 + n_ci]
        o_ref, wf_ref = refs[3 + n_ci:5 + n_ci]
        c_outs = refs[5 + n_ci:5 + n_ci + n_co]
        wbuf, send_sems, recv_sems, own_sem, buf_sems, c_send, c_recv = refs[5 + n_ci + n_co:]
        t, i = pl.program_id(0), pl.program_id(1)

        @pl.when((t == comm_at) & (i == 0))
        def _():
            for cp in comm.copies(c_ins, c_outs, c_send, c_recv):
                cp.start()
        (x, y, c), me = _me()
        sibling, sib_id = _peer(1)
        chips = [_peer(dist) for dist in OTHER_CHIPS]

        def win(idx):
            return wf_ref.at[:, pl.ds(idx * C, C)]

        def copy(k, block, to, src=None):
            return pltpu.make_async_remote_copy(
                src_ref=win(block) if src is None else src, dst_ref=win(block),
                send_sem=send_sems.at[k], recv_sem=recv_sems.at[k], device_id=to, device_id_type=MESH)

        def fetch(tt, src):
            return pltpu.make_async_copy(src, wbuf.at[tt % 2], buf_sems.at[tt % 2])

        own = pltpu.make_async_copy(ws_ref, win(me), own_sem)
        (x_nbr, x_id), (y_nbr, y_id), (_, d_id) = chips

        def half(k, block, top, to):
            rows = wf_ref.at[pl.ds(0 if top else K // 2, K // 2), pl.ds(block * C, C)]
            return pltpu.make_async_remote_copy(src_ref=rows, dst_ref=rows, send_sem=send_sems.at[k], recv_sem=recv_sems.at[k],
                                                device_id=to, device_id_type=MESH)

        here = (x, y, c)

        def pass_on(from_x):
            if from_x:
                copy(4, x_id, sibling).start()
                half(8, x_id, False, y_nbr).start()
            else:
                copy(5, y_id, sibling).start()
                half(7, y_id, True, x_nbr).start()

        @pl.when((t == 0) & (i == 0))
        def _():
            fetch(0, ws_ref).start()
            own.start()
            copy(0, me, sibling, src=ws_ref).start()

        for tt in range(N_DEV):
            @pl.when((t == tt) & (i == 0))
            def _(tt=tt):
                fetch(tt, ws_ref).wait()

        o_ref[...] = _dot(h_ref[...], wbuf[t % 2], NN_DIMS)

        def schedule(first_x):
            on = c == (1 if first_x else 0)
            (f_dev, f_id), (g_dev, g_id) = ((x_nbr, x_id), (y_nbr, y_id)) if first_x else ((y_nbr, y_id), (x_nbr, x_id))
            kf, kg = (1, 2) if first_x else (2, 1)
            pf, pg = (4, 5) if first_x else (5, 4)

            @pl.when((t == 0) & (i == 0) & on)
            def _():
                copy(kf, me, f_dev, src=ws_ref).start()

            def event(nxt):
                if nxt == 1:
                    copy(0, sib_id, here).wait_recv()
                    return sib_id
                if nxt == 2:
                    copy(kf, f_id, here).wait_recv()
                    copy(kf, me, f_dev, src=ws_ref).wait_send()
                    copy(kg, me, g_dev, src=ws_ref).start()
                    pass_on(first_x)
                    return f_id
                if nxt == 3:
                    copy(pg, g_id + 1 - 2 * c, here).wait_recv()
                    return g_id + 1 - 2 * c
                if nxt == 4:
                    copy(kg, g_id, here).wait_recv()
                    pass_on(not first_x)
                    return g_id
                if nxt == 5:
                    copy(pf, f_id + 1 - 2 * c, here).wait_recv()
                    return f_id + 1 - 2 * c
                if nxt == 6:
                    half(7, d_id, True, here).wait_recv()
                    half(8, d_id, False, here).wait_recv()
                    copy(6, d_id, sibling).start()
                    return d_id
                copy(6, d_id + 1 - 2 * c, here).wait_recv()
                return d_id + 1 - 2 * c

            for tt in range(N_DEV - 1):
                @pl.when((t == tt) & (i == ahead) & on)
                def _(tt=tt):
                    fetch(tt + 1, win(event(tt + 1))).start()

            @pl.when((t == N_DEV - 1) & (i == nm - 1) & on)
            def _():
                copy(kg, me, g_dev, src=ws_ref).wait_send()

        schedule(True)
        schedule(False)

        @pl.when((t == N_DEV - 1) & (i == nm - 1))
        def _():
            copy(0, me, sibling, src=ws_ref).wait_send()
            half(7, y_id, True, x_nbr).wait_send()
            half(8, x_id, False, y_nbr).wait_send()
            for j, (_, dev_id) in enumerate(chips):
                copy(4 + j, dev_id, sibling).wait_send()
            own.wait()
            for cp in comm.copies(c_ins, c_outs, c_send, c_recv):
                cp.wait()

    hbm = pl.BlockSpec(memory_space=pl.ANY)
    res = pl.pallas_call(
        body,
        grid_spec=pltpu.PrefetchScalarGridSpec(
            num_scalar_prefetch=1, grid=(N_DEV, nm),
            in_specs=[pl.BlockSpec((tm, K), lambda t, i, order_ref: (i, 0)), hbm] + [hbm] * n_ci,
            out_specs=[pl.BlockSpec((tm, C), lambda t, i, order_ref: (i, order_ref[t])), hbm] + [hbm] * n_co,
            scratch_shapes=[pltpu.VMEM((2, K, C), bf16), pltpu.SemaphoreType.DMA((9,)), pltpu.SemaphoreType.DMA((9,)),
                            pltpu.SemaphoreType.DMA, pltpu.SemaphoreType.DMA((2,))] + comm.scratch()),
        out_shape=[jax.ShapeDtypeStruct((T, N_DEV * C), f32), jax.ShapeDtypeStruct((K, N_DEV * C), bf16)] + comm.out_shapes,
        compiler_params=_params(("arbitrary", "arbitrary")), name="proj_gather",
    )(order, h, w_shard, *comm.ins)
    return res[0], res[1], list(res[2:])


N_CHIP = 4


def _shard_shape(g, ax):
    return tuple(d // N_DEV if i == ax else d for i, d in enumerate(g.shape))


def _sibling_comm(grads, axes):
    n = len(grads)
    sizes = [g.shape[ax] // N_DEV for g, ax in zip(grads, axes)]

    def copies(ins, lands, send_sems, recv_sems, base=0):
        sibling, _ = _peer(1)
        out = []
        for j in range(N_CHIP):
            _, owner = _peer(2 * j + 1)
            for a in range(n):
                out.append(pltpu.make_async_remote_copy(
                    src_ref=_window(ins[a], axes[a], sizes[a], owner), dst_ref=lands[a].at[j],
                    send_sem=send_sems.at[base + a, j], recv_sem=recv_sems.at[base + a, j], device_id=sibling,
                    device_id_type=MESH))
        return out

    shapes = [jax.ShapeDtypeStruct((N_CHIP,) + _shard_shape(g, ax), g.dtype) for g, ax in zip(grads, axes)]
    return _Comm(grads, shapes, (n, N_CHIP), copies)


def _chips_comm(sums):
    n = len(sums)

    def copies(ins, lands, send_sems, recv_sems, base=0):
        out = []
        for j in range(1, N_CHIP):
            dev, _ = _peer(2 * j)
            for a in range(n):
                out.append(pltpu.make_async_remote_copy(
                    src_ref=ins[a].at[j - 1], dst_ref=lands[a].at[j - 1],
                    send_sem=send_sems.at[base + a, j - 1], recv_sem=recv_sems.at[base + a, j - 1], device_id=dev,
                    device_id_type=MESH))
        return out

    return _Comm(sums, [jax.ShapeDtypeStruct(s.shape, s.dtype) for s in sums], (n, N_CHIP), copies)


def _join(c1, c2):
    n1, m1, k1 = len(c1.ins), len(c1.out_shapes), c1.sem_shape[0]

    def copies(ins, lands, send_sems, recv_sems):
        return (c1.copies(ins[:n1], lands[:m1], send_sems, recv_sems, base=0)
                + c2.copies(ins[n1:], lands[m1:], send_sems, recv_sems, base=k1))

    aliases = {**c1.aliases, **{n1 + k: m1 + v for k, v in c2.aliases.items()}}
    return _Comm(c1.ins + c2.ins, c1.out_shapes + c2.out_shapes, (k1 + c2.sem_shape[0], N_CHIP), copies, aliases)


def _chips_first_hop(sums):
    n = len(sums)

    def copies(ins, outs, send_sems, recv_sems, base=0):
        lands, relays = outs[:n], outs[n:]
        (y_dev, _), (x_dev, _) = _peer(2), _peer(4)
        out = []
        for a in range(n):
            half = sums[a].shape[1] // 2
            for k, (src, dst, dev) in enumerate((
                    (ins[a].at[0], lands[a].at[0], y_dev), (ins[a].at[1], lands[a].at[1], x_dev),
                    (ins[a].at[2, pl.ds(0, half)], relays[a].at[0], x_dev),
                    (ins[a].at[2, pl.ds(half, half)], relays[a].at[1], y_dev))):
                out.append(pltpu.make_async_remote_copy(
                    src_ref=src, dst_ref=dst, send_sem=send_sems.at[base + a, k], recv_sem=recv_sems.at[base + a, k],
                    device_id=dev, device_id_type=MESH))
        return out

    shapes = ([jax.ShapeDtypeStruct(s.shape, s.dtype) for s in sums]
              + [jax.ShapeDtypeStruct((2, s.shape[1] // 2) + s.shape[2:], s.dtype) for s in sums])
    return _Comm(sums, shapes, (n, N_CHIP), copies)


def _chips_relay(relays, lands):
    n = len(relays)

    def copies(ins, outs, send_sems, recv_sems, base=0):
        (y_dev, _), (x_dev, _) = _peer(2), _peer(4)
        out = []
        for a in range(n):
            half = relays[a].shape[1]
            for k, (rows, dev) in enumerate(((pl.ds(0, half), y_dev), (pl.ds(half, half), x_dev))):
                out.append(pltpu.make_async_remote_copy(
                    src_ref=ins[a].at[k], dst_ref=outs[a].at[2, rows], send_sem=send_sems.at[base + a, k],
                    recv_sem=recv_sems.at[base + a, k], device_id=dev, device_id_type=MESH))
        return out

    return _Comm(list(relays) + list(lands), [jax.ShapeDtypeStruct(l.shape, l.dtype) for l in lands], (n, N_CHIP), copies,
                 aliases={n + a: a for a in range(n)})


def _rs_chip_sum(grad, land, xyc, axis, name):
    R, C = land.shape[1:]
    tr = _pick(R, max(8, (640 * 1024) // C), 8)

    def body(xyc_ref, g_ref, l_ref, o_ref):
        o_ref[0] = (g_ref[...] + l_ref[0]).astype(bf16)

    def owner(j, xyc_ref):
        jj = j + 1
        return 4 * (xyc_ref[0] ^ (jj >> 1)) + 2 * (xyc_ref[1] ^ (jj & 1)) + xyc_ref[2]

    if axis == 0:
        g_spec = pl.BlockSpec((tr, C), lambda j, i, xyc_ref: (owner(j, xyc_ref) * (R // tr) + i, 0))
    else:
        g_spec = pl.BlockSpec((tr, C), lambda j, i, xyc_ref: (i, owner(j, xyc_ref)))
    return pl.pallas_call(
        body,
        grid_spec=pltpu.PrefetchScalarGridSpec(
            num_scalar_prefetch=1, grid=(N_CHIP - 1, R // tr),
            in_specs=[g_spec, pl.BlockSpec((1, tr, C), lambda j, i, xyc_ref: (j + 1, i, 0))],
            out_specs=pl.BlockSpec((1, tr, C), lambda j, i, xyc_ref: (j, i, 0))),
        out_shape=jax.ShapeDtypeStruct((N_CHIP - 1, R, C), bf16),
        compiler_params=_params(("parallel", "parallel")), name=name,
    )(xyc, grad, land)


def _allreduce_small(part):
    R = part.shape[0]

    def body(p_ref, o_ref, buf, send_sems, recv_sems):
        (x, y, c), me = _me()
        buf[me] = p_ref[...]
        copies = []
        for k in range(1, N_DEV):
            dev, dev_id = _peer(k)
            copies.append(pltpu.make_async_remote_copy(
                src_ref=p_ref, dst_ref=buf.at[me], send_sem=send_sems.at[k - 1], recv_sem=recv_sems.at[k - 1],
                device_id=dev, device_id_type=MESH))
        for cp in copies:
            cp.start()
        for k in range(1, N_DEV):
            _, dev_id = _peer(k)
            pltpu.make_async_remote_copy(
                src_ref=p_ref, dst_ref=buf.at[dev_id], send_sem=send_sems.at[k - 1], recv_sem=recv_sems.at[k - 1],
                device_id=(x, y, c), device_id_type=MESH).wait_recv()
        for cp in copies:
            cp.wait_send()
        acc = buf[0]
        for d in range(1, N_DEV):
            acc = acc + buf[d]
        o_ref[...] = acc

    return pl.pallas_call(
        body, in_specs=[pl.BlockSpec(memory_space=pltpu.VMEM)], out_specs=pl.BlockSpec(memory_space=pltpu.VMEM),
        out_shape=jax.ShapeDtypeStruct((R, LANES), f32),
        scratch_shapes=[pltpu.VMEM((N_DEV, R, LANES), f32), pltpu.SemaphoreType.DMA((N_DEV - 1,)), pltpu.SemaphoreType.DMA((N_DEV - 1,))],
        name="allreduce_small",
    )(part)


def _adamw_math(w, g, m, v):
    m2 = ADAM_B1 * m + (1.0 - ADAM_B1) * g
    v2 = ADAM_B2 * v + (1.0 - ADAM_B2) * (g * g)
    m_hat = m2 / (1.0 - ADAM_B1 ** ADAM_STEP)
    v_hat = v2 / (1.0 - ADAM_B2 ** ADAM_STEP)
    return -ADAM_LR * (m_hat / (jnp.sqrt(v_hat) + ADAM_EPS) + ADAM_WD * w), m2, v2


def _adamw_shard(grad, land_sib, land_chips, w, m, v, me, axis, name, row0=0, prev=None):
    R, C = land_sib.shape[1:]
    assert axis == 1 or R == w.shape[0]
    tr = _pick(R, max(8, (320 * 1024) // C), 8)
    r0 = row0 // tr
    n_prev = len(prev) if prev else 0

    def body(me_ref, g_ref, s_ref, l_ref, w_ref, m_ref, v_ref, *rest):
        go_ref, d_ref, mo_ref, vo_ref = rest[n_prev:]
        g = g_ref[...] + s_ref[0]
        for j in range(N_CHIP - 1):
            g = g + l_ref[j].astype(f32)
        d, m2, v2 = _adamw_math(w_ref[...], g, m_ref[...], v_ref[...])
        go_ref[...] = g
        d_ref[...] = d
        mo_ref[...] = m2
        vo_ref[...] = v2

    if axis == 0:
        g_spec = pl.BlockSpec((tr, C), lambda i, me_ref: (me_ref[0] * (R // tr) + i, 0))
    else:
        g_spec = pl.BlockSpec((tr, C), lambda i, me_ref: (i, me_ref[0]))
    blk = pl.BlockSpec((tr, C), lambda i, me_ref: (r0 + i, 0))
    return pl.pallas_call(
        body,
        grid_spec=pltpu.PrefetchScalarGridSpec(
            num_scalar_prefetch=1, grid=(R // tr,),
            in_specs=[g_spec, pl.BlockSpec((1, tr, C), lambda i, me_ref: (0, i, 0)),
                      pl.BlockSpec((N_CHIP - 1, tr, C), lambda i, me_ref: (0, i, 0)), blk, blk, blk]
            + [pl.BlockSpec(memory_space=pl.ANY)] * n_prev,
            out_specs=[blk] * 4),
        out_shape=[jax.ShapeDtypeStruct(w.shape, f32)] * 4,
        input_output_aliases={7 + i: i for i in range(n_prev)},
        compiler_params=_params(("parallel",)), name=name,
    )(me, grad, land_sib, land_chips, w, m, v, *(prev or []))


def _adamw_small(g, w, m, v):
    def body(g_ref, w_ref, m_ref, v_ref, d_ref, mo_ref, vo_ref):
        d, m2, v2 = _adamw_math(w_ref[...], g_ref[...], m_ref[...], v_ref[...])
        d_ref[...] = d
        mo_ref[...] = m2
        vo_ref[...] = v2

    return pl.pallas_call(body, out_shape=[jax.ShapeDtypeStruct(g.shape, f32)] * 3, name="adamw_small")(g, w, m, v)


def _pack_rows(parts, total_rows):
    rows = jnp.concatenate([p.reshape(-1, LANES) for p in parts], axis=0)
    return jnp.pad(rows, ((0, total_rows - rows.shape[0]), (0, 0)))


BIG = ("w_in", "mem_w_kv", "w_br_attn", "w_br_conv", "w_br_mem", "w_out")
BIG_AXIS = {"w_in": 1, "mem_w_kv": 0, "w_br_attn": 1, "w_br_conv": 1, "w_br_mem": 1, "w_out": 0}
SMALL = ("norm_g", "mem_norm_g", "attn_q_norm", "attn_k_norm", "mem_q_norm", "mem_k_norm")
ALL_W = ("norm_g", "mem_norm_g", "w_in", "attn_q_norm", "attn_k_norm", "conv_w", "mem_w_kv", "mem_q_norm", "mem_k_norm",
         "w_br_attn", "w_br_conv", "w_br_mem", "w_out")


def kernel(x, mem, norm_g, mem_norm_g, w_in, attn_q_norm, attn_k_norm, conv_w, mem_w_kv, mem_q_norm, mem_k_norm, w_br_attn, w_br_conv, w_br_mem, w_out, loss_target, m_norm_g, m_mem_norm_g, m_w_in, m_attn_q_norm, m_attn_k_norm, m_conv_w, m_mem_w_kv, m_mem_q_norm, m_mem_k_norm, m_w_br_attn, m_w_br_conv, m_w_br_mem, m_w_out, v_norm_g, v_mem_norm_g, v_w_in, v_attn_q_norm, v_attn_k_norm, v_conv_w, v_mem_w_kv, v_mem_q_norm, v_mem_k_norm, v_w_br_attn, v_w_br_conv, v_w_br_mem, v_w_out):
    w = dict(norm_g=norm_g, mem_norm_g=mem_norm_g, w_in=w_in, attn_q_norm=attn_q_norm, attn_k_norm=attn_k_norm, conv_w=conv_w,
             mem_w_kv=mem_w_kv, mem_q_norm=mem_q_norm, mem_k_norm=mem_k_norm, w_br_attn=w_br_attn, w_br_conv=w_br_conv,
             w_br_mem=w_br_mem, w_out=w_out)
    mo = dict(norm_g=m_norm_g, mem_norm_g=m_mem_norm_g, w_in=m_w_in, attn_q_norm=m_attn_q_norm, attn_k_norm=m_attn_k_norm,
              conv_w=m_conv_w, mem_w_kv=m_mem_w_kv, mem_q_norm=m_mem_q_norm, mem_k_norm=m_mem_k_norm, w_br_attn=m_w_br_attn,
              w_br_conv=m_w_br_conv, w_br_mem=m_w_br_mem, w_out=m_w_out)
    vo = dict(norm_g=v_norm_g, mem_norm_g=v_mem_norm_g, w_in=v_w_in, attn_q_norm=v_attn_q_norm, attn_k_norm=v_attn_k_norm,
              conv_w=v_conv_w, mem_w_kv=v_mem_w_kv, mem_q_norm=v_mem_q_norm, mem_k_norm=v_mem_k_norm, w_br_attn=v_w_br_attn,
              w_br_conv=v_w_br_conv, w_br_mem=v_w_br_mem, w_out=v_w_out)
    B, S, D = x.shape
    me = (4 * lax.axis_index("x") + 2 * lax.axis_index("y") + lax.axis_index("c")).astype(jnp.int32)

    T = B * S
    x2, t2, mem2, ng = x.reshape(T, D), loss_target.reshape(T, D), mem.reshape(B * MEM_LEN, D), norm_g.reshape(1, D)
    h = _rms_fwd(x2, ng, "rms_x")
    with_proj, with_attn = ("mem_w_kv",), ("w_out", "w_br_attn", "w_br_conv", "w_br_mem")
    later = with_proj + with_attn
    later_axes = [BIG_AXIS[n] for n in later] + [None]
    conv_pad = jnp.pad(conv_w, ((0, 8 - conv_w.shape[0]), (0, 0)))
    proj, w_in_full, spread_a = _proj_gather(
        h, w_in.astype(bf16), _shard_order(),
        _spread_comm([w[n].astype(bf16) for n in with_proj], [BIG_AXIS[n] for n in with_proj]), comm_at=N_DEV - 3)
    *attn, spread_b = _attn_fwd(proj, attn_q_norm, attn_k_norm, B, S,
                                comm=_spread_comm([w[n].astype(bf16) for n in with_attn] + [conv_pad],
                                                  [BIG_AXIS[n] for n in with_attn] + [None]))
    *fulls, conv_g = _forward_blocks(spread_a + spread_b, later_axes)
    wf = dict(zip(later, fulls), w_in=w_in_full)
    conv_full = conv_g[:, :3, :].transpose(1, 0, 2).reshape(3, CONV_W)

    sq_err, g, t = _fwd_bwd_head(x2, mem2, t2, proj, attn, B, S, mem_norm_g, attn_q_norm, attn_k_norm, conv_full, mem_q_norm,
                                 mem_k_norm, wf["mem_w_kv"], wf["w_br_attn"], wf["w_br_conv"], wf["w_br_mem"], wf["w_out"])
    t.update(x2=x2, ng=ng, h=h)

    xyc = jnp.stack([lax.axis_index("x"), lax.axis_index("y"), lax.axis_index("c")]).astype(jnp.int32)
    me1 = me.reshape(1)
    early = ("w_out", "w_br_attn", "w_br_conv", "w_br_mem")
    g["mem_w_kv"], sib_early = _mm(t["mh"], t["dmkv"], mode="tn", out_dtype=f32, name="dw_kv",
                                   comm=_sibling_comm([g[n] for n in early], [BIG_AXIS[n] for n in early]))
    sums_early = [_rs_chip_sum(g[n], ls, xyc, BIG_AXIS[n], "rs_sum_" + n) for n, ls in zip(early, sib_early)]
    tm_w = _pick(D // 2, 1024)
    half = (D // 2) // tm_w
    g_top, (*chips_early, sib_kv) = _mm(t["h"], t["dproj"], mode="tn", out_dtype=f32, name="dw_in_top", tm=tm_w,
                                        m_tiles=(0, half),
                                        comm=_join(_chips_comm(sums_early), _sibling_comm([g["mem_w_kv"]], [0])))
    sum_kv = _rs_chip_sum(g["mem_w_kv"], sib_kv, xyc, 0, "rs_sum_mem_w_kv")
    g_bot, (chips_kv, sib_top) = _mm(t["h"], t["dproj"], mode="tn", out_dtype=f32, name="dw_in_bot", tm=tm_w,
                                     m_tiles=(half, half), comm=_join(_chips_comm([sum_kv]), _sibling_comm([g_top], [1])))
    sum_top = _rs_chip_sum(g_top, sib_top, xyc, 1, "rs_sum_w_in_top")
    tm_t = _pick(T // 2, 1024)
    halfm = (T // 2) // tm_t
    dh, (part_top, relay_top, sib_bot) = _mm(t["dproj"], wf["w_in"], mode="nt", out_dtype=f32, name="d_h_a", tm=tm_t, tk=D_H_TK,
                                             m_tiles=(0, halfm), into="new",
                                             comm=_join(_chips_first_hop([sum_top]), _sibling_comm([g_bot], [1])))
    sum_bot = _rs_chip_sum(g_bot, sib_bot, xyc, 1, "rs_sum_w_in_bot")
    dh, (part_bot, relay_bot, chips_top) = _mm(t["dproj"], wf["w_in"], mode="nt", out_dtype=f32, name="d_h_b", tm=tm_t,
                                               tk=D_H_TK, m_tiles=(halfm, halfm), into=dh,
                                               comm=_join(_chips_first_hop([sum_bot]), _chips_relay([relay_top], [part_top])))
    dx, dng, (chips_bot,) = _rms_bwd(t["x2"], t["ng"], dh, t["dy"], "rms_x_bwd", comm=_chips_relay([relay_bot], [part_bot]))
    g["norm_g"] = dng.reshape(D)

    grad, delta, new_m, new_v = {}, {}, {}, {}
    for n, ls, lc in zip(early + ("mem_w_kv",), sib_early + [sib_kv], chips_early + [chips_kv]):
        grad[n], delta[n], new_m[n], new_v[n] = _adamw_shard(g[n], ls, lc, w[n], mo[n], vo[n], me1, BIG_AXIS[n], "adamw_" + n)
    top = _adamw_shard(g_top, sib_top, chips_top, w_in, m_w_in, v_w_in, me1, 1, "adamw_w_in_top")
    grad["w_in"], delta["w_in"], new_m["w_in"], new_v["w_in"] = _adamw_shard(
        g_bot, sib_bot, chips_bot, w_in, m_w_in, v_w_in, me1, 1, "adamw_w_in_bot", row0=D // 2, prev=top)

    n_rep = sum(w[n].size for n in SMALL) // LANES
    conv_rows = g["conv_w"].reshape(3, N_DEV, LANES).transpose(1, 0, 2).reshape(3 * N_DEV, LANES)
    n_rows = -(-(n_rep + 3 * N_DEV + 1) // 8) * 8
    part = _pack_rows([g[n] for n in SMALL] + [conv_rows, jnp.full((1, LANES), sq_err, f32)], n_rows)
    tot = _allreduce_small(part)
    loss = tot[n_rep + 3 * N_DEV, 0] * (0.5 / D)
    n_small = -(-(n_rep + 3) // 8) * 8
    g_small = _pack_rows([tot[:n_rep], lax.dynamic_slice(tot, (n_rep + 3 * me, 0), (3, LANES))], n_small)
    names = SMALL + ("conv_w",)
    d_s, m_s, v_s = _adamw_small(g_small, _pack_rows([w[n] for n in names], n_small), _pack_rows([mo[n] for n in names], n_small),
                                 _pack_rows([vo[n] for n in names], n_small))
    off = 0
    for n in names:
        r = w[n].size // LANES
        grad[n], delta[n] = g_small[off:off + r].reshape(w[n].shape), d_s[off:off + r].reshape(w[n].shape)
        new_m[n], new_v[n] = m_s[off:off + r].reshape(w[n].shape), v_s[off:off + r].reshape(w[n].shape)
        off += r
    return (loss, dx.reshape(B, S, D), *[grad[n] for n in ALL_W], *[delta[n] for n in ALL_W], *[new_m[n] for n in ALL_W],
            *[new_v[n] for n in ALL_W])
```

```python
import functools

import jax
import jax.numpy as jnp
from jax import lax
from jax.experimental import pallas as pl
from jax.experimental.pallas import tpu as pltpu

f32 = jnp.float32
bf16 = jnp.bfloat16

N_DEV = 8
HEAD_DIM = 128
DILATIONS = (1, 4, 16)
N_GROUPS = 3
HEADS = 4
BLK = 128
ATTN_QKV = N_GROUPS * HEADS * HEAD_DIM
ATTN_OUT = HEADS * HEAD_DIM
CONV_W = 1024
MEM_LEN = 256
MEM_HEADS = 4
MEM_HD = 256
MEM_W = MEM_HEADS * MEM_HD
EPS = 1e-6
Q0, K0, V0 = 0, ATTN_QKV, 2 * ATTN_QKV
ZA = 3 * ATTN_QKV
CB, CC, CV, ZC = ZA + ATTN_OUT, ZA + ATTN_OUT + CONV_W, ZA + ATTN_OUT + 2 * CONV_W, ZA + ATTN_OUT + 3 * CONV_W
MQ = ZC + CONV_W
ZM = MQ + MEM_W
GT = ZM + MEM_W

ADAM_LR, ADAM_B1, ADAM_B2, ADAM_EPS, ADAM_WD, ADAM_STEP = 0.001, 0.9, 0.999, 1e-08, 0.01, 10

VMEM_LIMIT = 56 * 1024 * 1024
D_H_TK = 4352
PROJ_DTYPE = bf16
LANES = 128

NT_DIMS = (((1,), (1,)), ((), ()))
TN_DIMS = (((0,), (0,)), ((), ()))
NN_DIMS = (((1,), (0,)), ((), ()))


def _params(sem=None):
    return pltpu.CompilerParams(dimension_semantics=sem, vmem_limit_bytes=VMEM_LIMIT)


def _pick(n, pref, q=LANES):
    t = (min(pref, n) // q) * q
    while t >= q:
        if n % t == 0:
            return t
        t -= q
    return n


def _dot(a, b, dims):
    return lax.dot_general(a.astype(bf16), b.astype(bf16), dims, preferred_element_type=f32)


def _sigmoid(z):
    return 0.5 * jnp.tanh(0.5 * z) + 0.5


def _rstd(v):
    return lax.rsqrt(jnp.mean(v * v, axis=-1, keepdims=True) + EPS)


class _Deferred:
    def __init__(self, stage, sems, step, last, n):
        assert last >= 1
        self.stage, self.sems, self.step, self.last, self.n = stage, sems, step, last, n
        self.slot = step % 2

    def _drain(self, slot, like):
        for c in range(self.n):
            pltpu.make_async_copy(self.stage.at[slot, c], like(c), self.sems.at[slot, c]).wait()

    def begin(self, like):
        pl.when(self.step >= 2)(lambda: self._drain(self.slot, like))

    def start(self, c, dst):
        pltpu.make_async_copy(self.stage.at[self.slot, c], dst, self.sems.at[self.slot, c]).start()

    def end(self, like):
        @pl.when(self.step == self.last)
        def _():
            self._drain(self.slot, like)
            self._drain(1 - self.slot, like)


def _row_sum(v):
    return _dot(v, jnp.ones((v.shape[1], v.shape[1]), bf16), NN_DIMS)


def _rstd_head(v):
    return lax.rsqrt(_row_sum(v * v) * (1.0 / v.shape[1]) + EPS)


class _Comm:
    def __init__(self, ins, out_shapes, sem_shape, copies, aliases=None):
        self.ins, self.out_shapes, self.sem_shape, self.copies = list(ins), list(out_shapes), sem_shape, copies
        self.aliases = dict(aliases or {})

    def scratch(self):
        return [pltpu.SemaphoreType.DMA(self.sem_shape), pltpu.SemaphoreType.DMA(self.sem_shape)]

    def io_aliases(self, first_in, first_out):
        return {first_in + k: first_out + v for k, v in self.aliases.items()}


def _mm(a, b, *, mode, out_dtype, name, tm=1024, tn=1024, tk=4096, m_tiles=None, into=None, comm=None):
    if mode == "nn":
        (M, K), (K2, N) = a.shape, b.shape
    elif mode == "nt":
        (M, K), (N, K2) = a.shape, b.shape
    else:
        (K, M), (K2, N) = a.shape, b.shape
    assert K == K2
    tm, tn, tk = _pick(M, tm), _pick(N, tn), _pick(K, tk)
    nk = K // tk
    m0, mc = m_tiles if m_tiles is not None else (0, M // tm)
    o0 = 0 if into is None else m0
    aliased = into is not None and not isinstance(into, str)
    n_ci = len(comm.ins) if comm else 0
    n_co = len(comm.out_shapes) if comm else 0
    grid = (mc, N // tn, nk)
    dims = {"nn": NN_DIMS, "nt": NT_DIMS, "tn": TN_DIMS}[mode]

    def body(*refs, nk):
        a_ref, b_ref = refs[:2]
        pos = 3 if aliased else 2
        c_ins, o_ref, c_outs = refs[pos:pos + n_ci], refs[pos + n_ci], refs[pos + n_ci + 1:pos + n_ci + 1 + n_co]
        scratch = refs[pos + n_ci + 1 + n_co:]
        ids = [pl.program_id(d) for d in range(3)]
        if comm:
            sems = scratch[-2:]

            @pl.when((ids[0] == 0) & (ids[1] == 0) & (ids[2] == 0))
            def _():
                for cp in comm.copies(c_ins, c_outs, *sems):
                    cp.start()

        if nk == 1:
            o_ref[...] = _dot(a_ref[...], b_ref[...], dims).astype(o_ref.dtype)
        else:
            acc_ref, k = scratch[0], ids[2]

            @pl.when(k == 0)
            def _():
                acc_ref[...] = _dot(a_ref[...], b_ref[...], dims)

            @pl.when((k > 0) & (k < nk - 1))
            def _():
                acc_ref[...] += _dot(a_ref[...], b_ref[...], dims)

            @pl.when(k == nk - 1)
            def _():
                o_ref[...] = (acc_ref[...] + _dot(a_ref[...], b_ref[...], dims)).astype(o_ref.dtype)

        if comm:
            @pl.when((ids[0] == grid[0] - 1) & (ids[1] == grid[1] - 1) & (ids[2] == grid[2] - 1))
            def _():
                for cp in comm.copies(c_ins, c_outs, *sems):
                    cp.wait()

    if mode == "tn":
        a_spec = pl.BlockSpec((tk, tm), lambda i, j, k: (k, m0 + i))
    else:
        a_spec = pl.BlockSpec((tm, tk), lambda i, j, k: (m0 + i, k))
    if mode == "nt":
        b_spec = pl.BlockSpec((tn, tk), lambda i, j, k: (j, k))
    else:
        b_spec = pl.BlockSpec((tk, tn), lambda i, j, k: (k, j))
    hbm = pl.BlockSpec(memory_space=pl.ANY)
    res = pl.pallas_call(
        functools.partial(body, nk=nk),
        grid=grid,
        in_specs=[a_spec, b_spec] + [hbm] * (aliased + n_ci),
        out_specs=[pl.BlockSpec((tm, tn), lambda i, j, k: (o0 + i, j))] + [hbm] * n_co,
        out_shape=[jax.ShapeDtypeStruct((mc * tm if into is None else M, N), out_dtype)] + (comm.out_shapes if comm else []),
        scratch_shapes=([pltpu.VMEM((tm, tn), f32)] if nk > 1 else []) + (comm.scratch() if comm else []),
        input_output_aliases={**({2: 0} if aliased else {}), **(comm.io_aliases(2 + aliased, 1) if comm else {})},
        compiler_params=_params(("arbitrary",) * 3 if comm else ("parallel", "parallel", "arbitrary")),
        name=name,
    )(a, b, *([into] if aliased else []), *(comm.ins if comm else []))
    return (res[0], list(res[1:])) if comm else res[0]


def _rms_fwd(x, g, name):
    T, D = x.shape
    tm = _pick(T, 256, 8)

    def body(x_ref, g_ref, h_ref):
        xv = x_ref[...]
        h_ref[...] = (xv * _rstd(xv) * g_ref[...]).astype(bf16)

    return pl.pallas_call(
        body, grid=(T // tm,),
        in_specs=[pl.BlockSpec((tm, D), lambda i: (i, 0)), pl.BlockSpec((1, D), lambda i: (0, 0))],
        out_specs=pl.BlockSpec((tm, D), lambda i: (i, 0)),
        out_shape=jax.ShapeDtypeStruct((T, D), bf16),
        compiler_params=_params(("parallel",)), name=name,
    )(x, g)


def _rms_bwd(x, g, dh, dy, name, comm=None):
    T, D = x.shape
    tm = _pick(T, 256, 8)
    with_dx = dy is not None
    n_ci = len(comm.ins) if comm else 0
    n_co = len(comm.out_shapes) if comm else 0

    def body(*refs):
        if with_dx:
            x_ref, g_ref, dh_ref, dy_ref = refs[:4]
            c_ins, (dx_ref, dg_ref) = refs[4:4 + n_ci], refs[4 + n_ci:6 + n_ci]
            c_outs, sems = refs[6 + n_ci:6 + n_ci + n_co], refs[6 + n_ci + n_co:]
        else:
            x_ref, g_ref, dh_ref, dg_ref = refs
        if comm:
            @pl.when(pl.program_id(0) == 0)
            def _():
                for cp in comm.copies(c_ins, c_outs, *sems):
                    cp.start()

        xv = x_ref[...]
        r = _rstd(xv)
        xh = xv * r
        dhv = dh_ref[...]
        part = jnp.sum(dhv * xh, axis=0, keepdims=True)

        @pl.when(pl.program_id(0) == 0)
        def _():
            dg_ref[...] = part

        @pl.when(pl.program_id(0) > 0)
        def _():
            dg_ref[...] += part

        if with_dx:
            dxh = dhv * g_ref[...]
            dx_ref[...] = dy_ref[...] + r * (dxh - xh * jnp.mean(dxh * xh, axis=-1, keepdims=True))

        if comm:
            @pl.when(pl.program_id(0) == T // tm - 1)
            def _():
                for cp in comm.copies(c_ins, c_outs, *sems):
                    cp.wait()

    row = pl.BlockSpec((tm, D), lambda i: (i, 0))
    vec = pl.BlockSpec((1, D), lambda i: (0, 0))
    hbm = pl.BlockSpec(memory_space=pl.ANY)
    if with_dx:
        res = pl.pallas_call(
            body, grid=(T // tm,), in_specs=[row, vec, row, row] + [hbm] * n_ci, out_specs=[row, vec] + [hbm] * n_co,
            out_shape=[jax.ShapeDtypeStruct((T, D), f32), jax.ShapeDtypeStruct((1, D), f32)] + (comm.out_shapes if comm else []),
            scratch_shapes=comm.scratch() if comm else [],
            input_output_aliases=comm.io_aliases(4, 2) if comm else {},
            compiler_params=_params(("arbitrary",)), name=name,
        )(x, g, dh, dy, *(comm.ins if comm else []))
        return (res[0], res[1], list(res[2:])) if comm else res
    return pl.pallas_call(
        body, grid=(T // tm,), in_specs=[row, vec, row], out_specs=vec,
        out_shape=jax.ShapeDtypeStruct((1, D), f32),
        compiler_params=_params(("arbitrary",)), name=name,
    )(x, g, dh)


MAX_UNIT_UNROLL = 6


def _unit_unroll(trips):
    return max(u for u in range(1, MAX_UNIT_UNROLL + 1) if trips % u == 0)


def _rows(start, size, stride):
    return pl.ds(start, size) if stride == 1 else pl.ds(start, size, stride=stride)


def _attn_units(S, d, first, prev):
    nb = S // d // BLK

    def do_first(r, c):
        first(_rows(r, BLK, d))
        return c

    lax.fori_loop(0, d, do_first, 0, unroll=_unit_unroll(d))
    if nb > 1:
        def do_prev(u, c):
            r = u // (nb - 1)
            b = u % (nb - 1) + 1
            base = r + d * b * BLK
            prev(_rows(base, BLK, d), _rows(base - d * BLK, 2 * BLK, d))
            return c

        lax.fori_loop(0, d * (nb - 1), do_prev, 0, unroll=_unit_unroll(d * (nb - 1)))


def _band_bias(nkeys):
    i = lax.broadcasted_iota(jnp.int32, (BLK, nkeys), 0)
    j = lax.broadcasted_iota(jnp.int32, (BLK, nkeys), 1)
    ok = (j <= i) if nkeys == BLK else ((j >= i) & (j <= i + BLK))
    return jnp.where(ok, 0.0, -jnp.inf).astype(f32)


def _widen_into(src_ref, dst_ref, S):
    for c in range(S // 256):
        rows = pl.ds(c * 256, 256)
        dst_ref[rows, :] = src_ref[rows, :].astype(f32)


def _normalize_into(src_ref, gain, dst_ref, S, rstd_ref=None):
    for c in range(S // 256):
        rows = pl.ds(c * 256, 256)
        v = src_ref[rows, :].astype(f32)
        r = _rstd_head(v)
        dst_ref[rows, :] = v * r * gain
        if rstd_ref is not None:
            rstd_ref[rows, :] = r


def _attn_fwd(proj, gq, gk, B, S, comm=None):
    T, NC = proj.shape
    scale = HEAD_DIM ** -0.5
    n_ci = len(comm.ins) if comm else 0
    n_co = len(comm.out_shapes) if comm else 0

    def body(*refs):
        q_ref, k_ref, v_ref, z_ref, gq_ref, gk_ref = refs[:6]
        c_ins = refs[6:6 + n_ci]
        ag_ref, a_ref, lse_ref = refs[6 + n_ci:9 + n_ci]
        c_outs = refs[9 + n_ci:9 + n_ci + n_co]
        qs, ks, vs, o0, o1, o2, l0, l1, l2, bias1, bias2 = refs[9 + n_ci + n_co:20 + n_ci + n_co]
        sems = refs[20 + n_ci + n_co:]
        g = pl.program_id(2)
        if comm:
            @pl.when((pl.program_id(0) == 0) & (pl.program_id(1) == 0) & (g == 0))
            def _():
                for cp in comm.copies(c_ins, c_outs, *sems):
                    cp.start()

        bias1[...] = _band_bias(BLK)
        bias2[...] = _band_bias(2 * BLK)
        _normalize_into(q_ref, gq_ref[pl.ds(g, 1), :], qs, S)
        _normalize_into(k_ref, gk_ref[pl.ds(g, 1), :], ks, S)
        _widen_into(v_ref, vs, S)
        o_s, l_s = (o0, o1, o2), (l0, l1, l2)

        def run(gi):
            def unit(qr, kr, nkeys):
                s = _dot(qs[qr, :], ks[kr, :], NT_DIMS) * scale + (bias1 if nkeys == BLK else bias2)[...]
                m = jnp.max(s, axis=-1, keepdims=True)
                p = jnp.exp(s - m)
                den = jnp.sum(p, axis=-1, keepdims=True)
                o_s[gi][qr, :] = _dot(p, vs[kr, :], NN_DIMS) * (1.0 / den)
                l_s[gi][qr, :] = jnp.broadcast_to(m + jnp.log(den), (BLK, HEAD_DIM))

            _attn_units(S, DILATIONS[gi], lambda qr: unit(qr, qr, BLK), lambda qr, kr: unit(qr, kr, 2 * BLK))

        for gi in range(N_GROUPS):
            pl.when(g == gi)(functools.partial(run, gi))

        @pl.when(g == N_GROUPS - 1)
        def _():
            for c in range(S // 256):
                rows = pl.ds(c * 256, 256)
                la, lb, lc = l0[rows, :], l1[rows, :], l2[rows, :]
                m = jnp.maximum(jnp.maximum(la, lb), lc)
                wa, wb, wc = jnp.exp(la - m), jnp.exp(lb - m), jnp.exp(lc - m)
                tot = wa + wb + wc
                a = (wa / tot) * o0[rows, :] + (wb / tot) * o1[rows, :] + (wc / tot) * o2[rows, :]
                z = z_ref[rows, :].astype(f32)
                a_ref[rows, :] = a
                lse_ref[rows, :] = m + jnp.log(tot)
                ag_ref[rows, :] = (a * (z * _sigmoid(z))).astype(bf16)

        if comm:
            @pl.when((pl.program_id(0) == B - 1) & (pl.program_id(1) == HEADS - 1) & (g == N_GROUPS - 1))
            def _():
                for cp in comm.copies(c_ins, c_outs, *sems):
                    cp.wait()

    def slab(col0):
        return pl.BlockSpec((S, HEAD_DIM), lambda b, h, g: (b, col0 // HEAD_DIM + g * HEADS + h))

    gain = pl.BlockSpec((N_GROUPS, HEAD_DIM), lambda b, h, g: (0, 0))
    out = pl.BlockSpec((S, HEAD_DIM), lambda b, h, g: (b, h))
    hbm = pl.BlockSpec(memory_space=pl.ANY)
    res = pl.pallas_call(
        body, grid=(B, HEADS, N_GROUPS),
        in_specs=[slab(Q0), slab(K0), slab(V0), pl.BlockSpec((S, HEAD_DIM), lambda b, h, g: (b, ZA // HEAD_DIM + h)), gain, gain]
        + [hbm] * n_ci,
        out_specs=[out, out, out] + [hbm] * n_co,
        out_shape=[jax.ShapeDtypeStruct((T, ATTN_OUT), bf16), jax.ShapeDtypeStruct((T, ATTN_OUT), f32),
                   jax.ShapeDtypeStruct((T, ATTN_OUT), f32)] + (comm.out_shapes if comm else []),
        scratch_shapes=[pltpu.VMEM((S, HEAD_DIM), f32)] * 9 + [pltpu.VMEM((BLK, BLK), f32), pltpu.VMEM((BLK, 2 * BLK), f32)]
        + (comm.scratch() if comm else []),
        compiler_params=_params(("arbitrary", "arbitrary", "arbitrary")), name="attn_fwd",
    )(proj, proj, proj, proj, gq, gk, *(comm.ins if comm else []))
    return res[0], res[1], res[2], list(res[3:])


def _rms_bwd_rows(raw, gain, dn, on_mxu=True, r=None):
    if r is None:
        r = _rstd_head(raw) if on_mxu else _rstd(raw)
    xh = raw * r
    dxh = dn * gain
    if on_mxu:
        mean = _row_sum(dxh * xh) * (1.0 / raw.shape[1])
    else:
        mean = jnp.mean(dxh * xh, axis=-1, keepdims=True)
    return r * (dxh - xh * mean), jnp.sum(dn * xh, axis=0, keepdims=True)


def _attn_bwd(proj, gq, gk, a_comb, lse, dag, dproj, B, S):
    T, NC = proj.shape
    scale = HEAD_DIM ** -0.5

    def body(q_ref, k_ref, v_ref, z_ref, gq_ref, gk_ref, a_ref, lse_ref, dag_ref, dproj_in, dproj_ref, dgq_ref, dgk_ref,
             qs, ks, da_s, dl_s, dq_s, dk_s, dv_s, rq_s, rk_s, vs, bias1, bias2, stage, dz_stage, sem, dz_sem):
        b, h, g = pl.program_id(0), pl.program_id(1), pl.program_id(2)
        bias1[...] = _band_bias(BLK)
        bias2[...] = _band_bias(2 * BLK)
        gq_row, gk_row = gq_ref[pl.ds(g, 1), :], gk_ref[pl.ds(g, 1), :]
        _normalize_into(q_ref, gq_row, qs, S, rq_s)
        _normalize_into(k_ref, gk_row, ks, S, rk_s)
        _widen_into(v_ref, vs, S)

        def dst(c):
            return dproj_ref.at[pl.ds(b * S, S), pl.ds((Q0, K0, V0)[c] + (g * HEADS + h) * HEAD_DIM, HEAD_DIM)]

        out = _Deferred(stage, sem, (b * HEADS + h) * N_GROUPS + g, B * HEADS * N_GROUPS - 1, 3)
        out.begin(dst)

        @pl.when((b == 0) & (h == 0) & (g == 0))
        def _():
            dgq_ref[...] = jnp.zeros_like(dgq_ref)
            dgk_ref[...] = jnp.zeros_like(dgk_ref)

        @pl.when(g == 0)
        def _():
            for c in range(S // 256):
                rows = pl.ds(c * 256, 256)
                z, a, dg_ = z_ref[rows, :].astype(f32), a_ref[rows, :], dag_ref[rows, :]
                sg = _sigmoid(z)
                da = dg_ * (z * sg)
                da_s[rows, :] = da
                dl_s[rows, :] = _row_sum(da * a)
                dz_stage[rows, :] = (dg_ * a * (sg * (1.0 + z * (1.0 - sg)))).astype(bf16)
            cp = pltpu.make_async_copy(dz_stage, dproj_ref.at[pl.ds(b * S, S), pl.ds(ZA + h * HEAD_DIM, HEAD_DIM)], dz_sem)
            cp.start()
            cp.wait()

        dk_s[...] = jnp.zeros_like(dk_s)
        dv_s[...] = jnp.zeros_like(dv_s)

        def run(gi):
            def unit(qr, kr, nkeys):
                q, k, v = qs[qr, :], ks[kr, :], vs[kr, :]
                s = _dot(q, k, NT_DIMS) * scale
                p = jnp.exp(s + (bias1 if nkeys == BLK else bias2)[...] - lse_ref[qr, :][:, :1])
                da = da_s[qr, :]
                dp = _dot(da, v, NT_DIMS)
                ds = p * (dp - dl_s[qr, :][:, :1]) * scale
                dq_s[qr, :] = _dot(ds, k, NN_DIMS)
                dk_s[kr, :] += _dot(ds, q, TN_DIMS)
                dv_s[kr, :] += _dot(p, da, TN_DIMS)

            _attn_units(S, DILATIONS[gi], lambda qr: unit(qr, qr, BLK), lambda qr, kr: unit(qr, kr, 2 * BLK))

        for gi in range(N_GROUPS):
            pl.when(g == gi)(functools.partial(run, gi))

        gq_acc = jnp.zeros((1, HEAD_DIM), f32)
        gk_acc = jnp.zeros((1, HEAD_DIM), f32)
        for c in range(S // 256):
            rows = pl.ds(c * 256, 256)
            dq, gq_p = _rms_bwd_rows(q_ref[rows, :].astype(f32), gq_row, dq_s[rows, :], r=rq_s[rows, :])
            dk, gk_p = _rms_bwd_rows(k_ref[rows, :].astype(f32), gk_row, dk_s[rows, :], r=rk_s[rows, :])
            gq_acc, gk_acc = gq_acc + gq_p, gk_acc + gk_p
            stage[out.slot, 0, rows, :] = dq.astype(bf16)
            stage[out.slot, 1, rows, :] = dk.astype(bf16)
            stage[out.slot, 2, rows, :] = dv_s[rows, :].astype(bf16)
        dgq_ref[pl.ds(g, 1), :] += gq_acc
        dgk_ref[pl.ds(g, 1), :] += gk_acc
        for c in range(3):
            out.start(c, dst(c))
        out.end(dst)

    def slab(col0):
        return pl.BlockSpec((S, HEAD_DIM), lambda b, h, g: (b, col0 // HEAD_DIM + g * HEADS + h))

    gain = pl.BlockSpec((N_GROUPS, HEAD_DIM), lambda b, h, g: (0, 0))
    per_slot = pl.BlockSpec((S, HEAD_DIM), lambda b, h, g: (b, h))
    return pl.pallas_call(
        body, grid=(B, HEADS, N_GROUPS),
        in_specs=[slab(Q0), slab(K0), slab(V0), pl.BlockSpec((S, HEAD_DIM), lambda b, h, g: (b, ZA // HEAD_DIM + h)), gain, gain,
                  per_slot, per_slot, per_slot, pl.BlockSpec(memory_space=pl.ANY)],
        out_specs=[pl.BlockSpec(memory_space=pl.ANY), gain, gain],
        out_shape=[jax.ShapeDtypeStruct(dproj.shape, dproj.dtype), jax.ShapeDtypeStruct((N_GROUPS, HEAD_DIM), f32),
                   jax.ShapeDtypeStruct((N_GROUPS, HEAD_DIM), f32)],
        scratch_shapes=[pltpu.VMEM((S, HEAD_DIM), f32)] * 10 + [pltpu.VMEM((BLK, BLK), f32), pltpu.VMEM((BLK, 2 * BLK), f32),
                                                                 pltpu.VMEM((2, 3, S, HEAD_DIM), bf16), pltpu.VMEM((S, HEAD_DIM), bf16),
                                                                pltpu.SemaphoreType.DMA((2, 3)), pltpu.SemaphoreType.DMA],
        input_output_aliases={9: 0},
        compiler_params=_params(("arbitrary", "arbitrary", "arbitrary")), name="attn_bwd",
    )(proj, proj, proj, proj, gq, gk, a_comb, lse, dag, dproj)


def _conv_fwd(proj, conv_w, B, S):
    T, NC = proj.shape
    tc = LANES

    def body(cb_ref, cc_ref, cv_ref, z_ref, w_ref, c_ref, ub):
        u = cc_ref[...].astype(f32) * cv_ref[...].astype(f32)
        ub[0:8, :] = jnp.zeros((8, tc), f32)
        ub[8:8 + S, :] = u
        w = w_ref[...]
        y = w[0:1] * u + w[1:2] * ub[pl.ds(7, S), :] + w[2:3] * ub[pl.ds(6, S), :]
        z = z_ref[...].astype(f32)
        c_ref[...] = (cb_ref[...].astype(f32) * y * (z * _sigmoid(z))).astype(bf16)

    def seg(col0):
        return pl.BlockSpec((S, tc), lambda j, b: (b, col0 // tc + j))

    return pl.pallas_call(
        body, grid=(CONV_W // tc, B),
        in_specs=[seg(CB), seg(CC), seg(CV), seg(ZC), pl.BlockSpec((3, tc), lambda j, b: (0, j))],
        out_specs=pl.BlockSpec((S, tc), lambda j, b: (b, j)),
        out_shape=jax.ShapeDtypeStruct((T, CONV_W), bf16),
        scratch_shapes=[pltpu.VMEM((S + 8, tc), f32)],
        compiler_params=_params(("parallel", "parallel")), name="conv_fwd",
    )(proj, proj, proj, proj, conv_w)


def _conv_bwd(proj, conv_w, dc, dproj, B, S):
    T, NC = proj.shape
    tc = LANES

    def body(cb_ref, cc_ref, cv_ref, z_ref, w_ref, dc_ref, dproj_in, dproj_ref, dw_ref, ub, db, stage, sem):
        j, b = pl.program_id(0), pl.program_id(1)

        def dst(c):
            return dproj_ref.at[pl.ds(b * S, S), pl.ds((CB, CC, CV, ZC)[c] + j * tc, tc)]

        out = _Deferred(stage, sem, j * B + b, (CONV_W // tc) * B - 1, 4)
        out.begin(dst)
        cb, cc, cv, z = (r[...].astype(f32) for r in (cb_ref, cc_ref, cv_ref, z_ref))
        dcv = dc_ref[...]
        u = cc * cv
        ub[0:8, :] = jnp.zeros((8, tc), f32)
        ub[8:8 + S, :] = u
        u1, u2 = ub[pl.ds(7, S), :], ub[pl.ds(6, S), :]
        w = w_ref[...]
        y = w[0:1] * u + w[1:2] * u1 + w[2:3] * u2
        sg = _sigmoid(z)
        si = z * sg
        dyv = dcv * cb * si
        db[0:S, :] = dyv
        db[S:S + 8, :] = jnp.zeros((8, tc), f32)
        du = w[0:1] * dyv + w[1:2] * db[pl.ds(1, S), :] + w[2:3] * db[pl.ds(2, S), :]
        stage[out.slot, 0] = (dcv * y * si).astype(bf16)
        stage[out.slot, 1] = (du * cv).astype(bf16)
        stage[out.slot, 2] = (du * cc).astype(bf16)
        stage[out.slot, 3] = (dcv * cb * y * (sg * (1.0 + z * (1.0 - sg)))).astype(bf16)
        for c in range(4):
            out.start(c, dst(c))
        part = jnp.concatenate([jnp.sum(dyv * u, axis=0, keepdims=True), jnp.sum(dyv * u1, axis=0, keepdims=True),
                                jnp.sum(dyv * u2, axis=0, keepdims=True)], axis=0)

        @pl.when(b == 0)
        def _():
            dw_ref[...] = part

        @pl.when(b > 0)
        def _():
            dw_ref[...] += part

        out.end(dst)

    def seg(col0):
        return pl.BlockSpec((S, tc), lambda j, b: (b, col0 // tc + j))

    return pl.pallas_call(
        body, grid=(CONV_W // tc, B),
        in_specs=[seg(CB), seg(CC), seg(CV), seg(ZC), pl.BlockSpec((3, tc), lambda j, b: (0, j)),
                  pl.BlockSpec((S, tc), lambda j, b: (b, j)), pl.BlockSpec(memory_space=pl.ANY)],
        out_specs=[pl.BlockSpec(memory_space=pl.ANY), pl.BlockSpec((3, tc), lambda j, b: (0, j))],
        out_shape=[jax.ShapeDtypeStruct(dproj.shape, dproj.dtype), jax.ShapeDtypeStruct((3, CONV_W), f32)],
        scratch_shapes=[pltpu.VMEM((S + 8, tc), f32), pltpu.VMEM((S + 8, tc), f32), pltpu.VMEM((2, 4, S, tc), bf16),
                        pltpu.SemaphoreType.DMA((2, 4))],
        input_output_aliases={6: 0},
        compiler_params=_params(("arbitrary", "arbitrary")), name="conv_bwd",
    )(proj, proj, proj, proj, conv_w, dc, dproj)


MEM_CHUNK = 1024


def _mem_fwd(proj, mkv, gq, gk, B, S):
    T, NC = proj.shape
    scale = MEM_HD ** -0.5

    def body(q_ref, z_ref, k_ref, v_ref, gq_ref, gk_ref, o_ref):
        kv = k_ref[...]
        kn = (kv * _rstd_head(kv) * gk_ref[...]).astype(bf16)
        vv = v_ref[...].astype(bf16)

        def chunk(c, carry):
            rows = pl.ds(pl.multiple_of(c * MEM_CHUNK, MEM_CHUNK), MEM_CHUNK)
            q = q_ref[rows, :].astype(f32)
            qn = q * _rstd(q) * gq_ref[...]
            s = _dot(qn, kn, NT_DIMS) * scale
            p = jnp.exp(s - jnp.max(s, axis=-1, keepdims=True))
            p = p / jnp.sum(p, axis=-1, keepdims=True)
            z = z_ref[rows, :].astype(f32)
            o_ref[rows, :] = (_dot(p, vv, NN_DIMS) * (z * _sigmoid(z))).astype(bf16)
            return carry

        lax.fori_loop(0, S // MEM_CHUNK, chunk, 0)

    gain = pl.BlockSpec((1, MEM_HD), lambda b, h: (0, 0))
    return pl.pallas_call(
        body, grid=(B, MEM_HEADS),
        in_specs=[pl.BlockSpec((S, MEM_HD), lambda b, h: (b, MQ // MEM_HD + h)),
                  pl.BlockSpec((S, MEM_HD), lambda b, h: (b, ZM // MEM_HD + h)),
                  pl.BlockSpec((MEM_LEN, MEM_HD), lambda b, h: (b, h)),
                  pl.BlockSpec((MEM_LEN, MEM_HD), lambda b, h: (b, MEM_HEADS + h)), gain, gain],
        out_specs=pl.BlockSpec((S, MEM_HD), lambda b, h: (b, h)),
        out_shape=jax.ShapeDtypeStruct((T, MEM_W), bf16),
        compiler_params=_params(("parallel", "parallel")), name="mem_fwd",
    )(proj, proj, mkv, mkv, gq, gk)


def _mem_bwd(proj, mkv, gq, gk, dmo, dproj, B, S):
    T, NC = proj.shape
    scale = MEM_HD ** -0.5

    def body(q_ref, z_ref, k_ref, v_ref, gq_ref, gk_ref, dmo_ref, dproj_in, dproj_ref, dmk_ref, dmv_ref, dgq_ref, dgk_ref,
             dkn_s, dv_s, gq_s, stage, sem):
        b, h = pl.program_id(0), pl.program_id(1)

        def dst(c):
            return dproj_ref.at[pl.ds(b * S, S), pl.ds((MQ, ZM)[c] + h * MEM_HD, MEM_HD)]

        out = _Deferred(stage, sem, b * MEM_HEADS + h, B * MEM_HEADS - 1, 2)
        out.begin(dst)
        kv = k_ref[...]
        kn = (kv * _rstd_head(kv) * gk_ref[...]).astype(bf16)
        vv = v_ref[...].astype(bf16)
        dkn_s[...] = jnp.zeros_like(dkn_s)
        dv_s[...] = jnp.zeros_like(dv_s)
        gq_s[...] = jnp.zeros_like(gq_s)

        def chunk(c, carry):
            rows = pl.ds(pl.multiple_of(c * MEM_CHUNK, MEM_CHUNK), MEM_CHUNK)
            q = q_ref[rows, :].astype(f32)
            qn = q * _rstd(q) * gq_ref[...]
            s = _dot(qn, kn, NT_DIMS) * scale
            p = jnp.exp(s - jnp.max(s, axis=-1, keepdims=True))
            p = p / jnp.sum(p, axis=-1, keepdims=True)
            mo = _dot(p, vv, NN_DIMS)
            z, dg_ = z_ref[rows, :].astype(f32), dmo_ref[rows, :]
            sg = _sigmoid(z)
            do = dg_ * (z * sg)
            stage[out.slot, 1, rows, :] = (dg_ * mo * (sg * (1.0 + z * (1.0 - sg)))).astype(bf16)
            dp = _dot(do, vv, NT_DIMS)
            ds = p * (dp - jnp.sum(do * mo, axis=-1, keepdims=True)) * scale
            dq, gq_p = _rms_bwd_rows(q, gq_ref[...], _dot(ds, kn, NN_DIMS), on_mxu=False)
            stage[out.slot, 0, rows, :] = dq.astype(bf16)
            gq_s[...] += gq_p
            dkn_s[...] += _dot(ds, qn, TN_DIMS)
            dv_s[...] += _dot(p, do, TN_DIMS)
            return carry

        lax.fori_loop(0, S // MEM_CHUNK, chunk, 0)
        for c in range(2):
            out.start(c, dst(c))
        dk, gk_p = _rms_bwd_rows(kv, gk_ref[...], dkn_s[...])
        dmk_ref[...] = dk
        dmv_ref[...] = dv_s[...]

        @pl.when((b == 0) & (h == 0))
        def _():
            dgq_ref[...] = gq_s[...]
            dgk_ref[...] = gk_p

        @pl.when((b > 0) | (h > 0))
        def _():
            dgq_ref[...] += gq_s[...]
            dgk_ref[...] += gk_p

        out.end(dst)

    gain = pl.BlockSpec((1, MEM_HD), lambda b, h: (0, 0))
    kvb = pl.BlockSpec((MEM_LEN, MEM_HD), lambda b, h: (b, h))
    return pl.pallas_call(
        body, grid=(B, MEM_HEADS),
        in_specs=[pl.BlockSpec((S, MEM_HD), lambda b, h: (b, MQ // MEM_HD + h)),
                  pl.BlockSpec((S, MEM_HD), lambda b, h: (b, ZM // MEM_HD + h)),
                  kvb, pl.BlockSpec((MEM_LEN, MEM_HD), lambda b, h: (b, MEM_HEADS + h)), gain, gain,
                  pl.BlockSpec((S, MEM_HD), lambda b, h: (b, h)), pl.BlockSpec(memory_space=pl.ANY)],
        out_specs=[pl.BlockSpec(memory_space=pl.ANY), kvb, kvb, gain, gain],
        out_shape=[jax.ShapeDtypeStruct(dproj.shape, dproj.dtype), jax.ShapeDtypeStruct((B * MEM_LEN, MEM_W), f32),
                   jax.ShapeDtypeStruct((B * MEM_LEN, MEM_W), f32), jax.ShapeDtypeStruct((1, MEM_HD), f32),
                   jax.ShapeDtypeStruct((1, MEM_HD), f32)],
        scratch_shapes=[pltpu.VMEM((MEM_LEN, MEM_HD), f32), pltpu.VMEM((MEM_LEN, MEM_HD), f32), pltpu.VMEM((1, MEM_HD), f32),
                        pltpu.VMEM((2, 2, S, MEM_HD), bf16), pltpu.SemaphoreType.DMA((2, 2))],
        input_output_aliases={7: 0},
        compiler_params=_params(("arbitrary", "arbitrary")), name="mem_bwd",
    )(proj, proj, mkv, mkv, gq, gk, dmo, dproj)


def _merge_fwd(proj, ag, cg, mg, wa, wc, wm):
    T, NC = proj.shape
    D = wa.shape[1]
    tm, tn = _pick(T, 1024), _pick(D, 512)
    assert GT % tn == 0

    def body(ag_ref, cg_ref, mg_ref, wa_ref, wc_ref, wm_ref, g0_ref, g1_ref, g2_ref, mer_ref, ba_ref, bc_ref, bm_ref):
        ba = _dot(ag_ref[...], wa_ref[...], NN_DIMS)
        bc = _dot(cg_ref[...], wc_ref[...], NN_DIMS)
        bm = _dot(mg_ref[...], wm_ref[...], NN_DIMS)
        s0, s1, s2 = (_sigmoid(r[...].astype(f32)) for r in (g0_ref, g1_ref, g2_ref))
        mer_ref[...] = (s0 * ba + s1 * bc + s2 * bm).astype(bf16)
        ba_ref[...] = ba.astype(bf16)
        bc_ref[...] = bc.astype(bf16)
        bm_ref[...] = bm.astype(bf16)

    def act(w):
        return pl.BlockSpec((tm, w), lambda i, j: (i, 0))

    def wt(k):
        return pl.BlockSpec((k, tn), lambda i, j: (0, j))

    def gate(n):
        return pl.BlockSpec((tm, tn), lambda i, j: (i, (GT + n * D) // tn + j))

    out = pl.BlockSpec((tm, tn), lambda i, j: (i, j))
    return pl.pallas_call(
        body, grid=(T // tm, D // tn),
        in_specs=[act(ATTN_OUT), act(CONV_W), act(MEM_W), wt(ATTN_OUT), wt(CONV_W), wt(MEM_W), gate(0), gate(1), gate(2)],
        out_specs=[out] * 4, out_shape=[jax.ShapeDtypeStruct((T, D), bf16)] * 4,
        compiler_params=_params(("parallel", "parallel")), name="merge_fwd",
    )(ag, cg, mg, wa, wc, wm, proj, proj, proj)


def _out_loss(merged, w_out, x, tgt):
    T, D = x.shape
    tm, tn = _pick(T, 1024), _pick(D, 512)

    def body(m_ref, w_ref, x_ref, t_ref, dy_ref, dyb_ref, lp_ref):
        e = x_ref[...] + _dot(m_ref[...], w_ref[...], NN_DIMS) - t_ref[...]
        dy = e / D
        dy_ref[...] = dy
        dyb_ref[...] = dy.astype(bf16)
        part = jnp.full((1, 8, LANES), jnp.sum(e * e), f32)

        @pl.when(pl.program_id(1) == 0)
        def _():
            lp_ref[...] = part

        @pl.when(pl.program_id(1) > 0)
        def _():
            lp_ref[...] += part

    tile = pl.BlockSpec((tm, tn), lambda i, j: (i, j))
    return pl.pallas_call(
        body, grid=(T // tm, D // tn),
        in_specs=[pl.BlockSpec((tm, D), lambda i, j: (i, 0)), pl.BlockSpec((D, tn), lambda i, j: (0, j)), tile, tile],
        out_specs=[tile, tile, pl.BlockSpec((1, 8, LANES), lambda i, j: (i, 0, 0))],
        out_shape=[jax.ShapeDtypeStruct((T, D), f32), jax.ShapeDtypeStruct((T, D), bf16),
                   jax.ShapeDtypeStruct((T // tm, 8, LANES), f32)],
        compiler_params=_params(("parallel", "arbitrary")), name="out_loss",
    )(merged, w_out, x, tgt)


def _merge_bwd(proj, dyb, w_out, ba, bc, bm):
    T, NC = proj.shape
    D = w_out.shape[0]
    tm, tn = _pick(T, 1024), _pick(D, 512)
    assert GT % tn == 0

    def body(dy_ref, w_ref, ba_ref, bc_ref, bm_ref, g0_ref, g1_ref, g2_ref, da_ref, dc_ref, dm_ref, dproj_ref, stage, sem):
        i, j = pl.program_id(0), pl.program_id(1)

        def dst(n):
            return dproj_ref.at[pl.ds(i * tm, tm), pl.ds(GT + n * D + j * tn, tn)]

        out = _Deferred(stage, sem, i * (D // tn) + j, (T // tm) * (D // tn) - 1, 3)
        out.begin(dst)
        dmer = _dot(dy_ref[...], w_ref[...], NT_DIMS)
        for n, (g_ref, br_ref, o_ref) in enumerate(((g0_ref, ba_ref, da_ref), (g1_ref, bc_ref, dc_ref), (g2_ref, bm_ref, dm_ref))):
            sg = _sigmoid(g_ref[...].astype(f32))
            o_ref[...] = (sg * dmer).astype(bf16)
            stage[out.slot, n] = (dmer * br_ref[...].astype(f32) * (sg * (1.0 - sg))).astype(bf16)
            out.start(n, dst(n))
        out.end(dst)

    tile = pl.BlockSpec((tm, tn), lambda i, j: (i, j))

    def gate(n):
        return pl.BlockSpec((tm, tn), lambda i, j: (i, (GT + n * D) // tn + j))

    return pl.pallas_call(
        body, grid=(T // tm, D // tn),
        in_specs=[pl.BlockSpec((tm, D), lambda i, j: (i, 0)), pl.BlockSpec((tn, D), lambda i, j: (j, 0)), tile, tile, tile,
                  gate(0), gate(1), gate(2)],
        out_specs=[tile, tile, tile, pl.BlockSpec(memory_space=pl.ANY)],
        out_shape=[jax.ShapeDtypeStruct((T, D), bf16)] * 3 + [jax.ShapeDtypeStruct((T, NC), bf16)],
        scratch_shapes=[pltpu.VMEM((2, 3, tm, tn), bf16), pltpu.SemaphoreType.DMA((2, 3))],
        compiler_params=_params(("arbitrary", "arbitrary")), name="merge_bwd",
    )(dyb, w_out, ba, bc, bm, proj, proj, proj)


def _fwd_bwd_head(x2, mem2, t2, proj, attn, B, S, mem_norm_g, attn_q_norm, attn_k_norm, conv_w, mem_q_norm, mem_k_norm,
                  w_kv, w_a, w_c, w_m, w_out):
    D = x2.shape[1]
    mng = mem_norm_g.reshape(1, D)
    mqn, mkn = mem_q_norm.reshape(1, MEM_HD), mem_k_norm.reshape(1, MEM_HD)
    ag, a_comb, lse = attn

    mh = _rms_fwd(mem2, mng, "rms_mem")
    mkv = _mm(mh, w_kv, mode="nn", out_dtype=f32, name="mkv")
    cg = _conv_fwd(proj, conv_w, B, S)
    mg = _mem_fwd(proj, mkv, mqn, mkn, B, S)
    merged, ba, bc, bm = _merge_fwd(proj, ag, cg, mg, w_a, w_c, w_m)
    dy, dyb, lp = _out_loss(merged, w_out, x2, t2)
    sq_err = jnp.sum(lp[:, 0, 0])

    g = {}
    g["w_out"] = _mm(merged, dyb, mode="tn", out_dtype=f32, name="dw_out")
    dba, dbc, dbm, dproj = _merge_bwd(proj, dyb, w_out, ba, bc, bm)
    g["w_br_attn"] = _mm(ag, dba, mode="tn", out_dtype=f32, name="dw_br_attn")
    g["w_br_conv"] = _mm(cg, dbc, mode="tn", out_dtype=f32, name="dw_br_conv")
    g["w_br_mem"] = _mm(mg, dbm, mode="tn", out_dtype=f32, name="dw_br_mem")
    dag = _mm(dba, w_a, mode="nt", out_dtype=f32, name="d_attn_out")
    dcg = _mm(dbc, w_c, mode="nt", out_dtype=f32, name="d_conv_out")
    dmg = _mm(dbm, w_m, mode="nt", out_dtype=f32, name="d_mem_out")
    dproj, g["attn_q_norm"], g["attn_k_norm"] = _attn_bwd(proj, attn_q_norm, attn_k_norm, a_comb, lse, dag, dproj, B, S)
    dproj, g["conv_w"] = _conv_bwd(proj, conv_w, dcg, dproj, B, S)
    dproj, dmk, dmv, dgmq, dgmk = _mem_bwd(proj, mkv, mqn, mkn, dmg, dproj, B, S)
    g["mem_q_norm"], g["mem_k_norm"] = dgmq.reshape(MEM_HD), dgmk.reshape(MEM_HD)
    dmkv = jnp.concatenate([dmk, dmv], axis=1)
    dmh = _mm(dmkv, w_kv, mode="nt", out_dtype=f32, name="d_mem_h")
    g["mem_norm_g"] = _rms_bwd(mem2, mng, dmh, None, "rms_mem_bwd").reshape(D)
    return sq_err, g, dict(dy=dy, dproj=dproj, mh=mh, dmkv=dmkv)


MESH = pl.DeviceIdType.MESH
HBM = pl.BlockSpec(memory_space=pl.ANY)


def _me():
    x, y, c = lax.axis_index("x"), lax.axis_index("y"), lax.axis_index("c")
    return (x, y, c), 4 * x + 2 * y + c


def _peer(k):
    (x, y, c), _ = _me()
    p = (1 - x if k & 4 else x, 1 - y if k & 2 else y, 1 - c if k & 1 else c)
    return p, 4 * p[0] + 2 * p[1] + p[2]


def _window(ref, axis, size, idx):
    if axis is None:
        return ref.at[idx]
    if axis == 0:
        return ref.at[pl.ds(idx * size, size), :]
    return ref.at[:, pl.ds(idx * size, size)]


def _full_shape(s, axis):
    if axis is None:
        return (N_DEV,) + s.shape
    return tuple(N_DEV * d if i == axis else d for i, d in enumerate(s.shape))


OTHER_CHIPS = (4, 2, 6)


def _spread_comm(shards, axes):
    n = len(shards)

    def copies(ins, outs, send_sems, recv_sems, base=0):
        _, me = _me()
        out = []
        for a in range(n):
            mine = _window(outs[a], axes[a], None if axes[a] is None else shards[a].shape[axes[a]], me)
            out.append(pltpu.make_async_copy(ins[a], mine, send_sems.at[base + a, N_CHIP]))
            for k, dist in enumerate((1,) + OTHER_CHIPS):
                dev, _ = _peer(dist)
                out.append(pltpu.make_async_remote_copy(
                    src_ref=ins[a], dst_ref=mine, send_sem=send_sems.at[base + a, k], recv_sem=recv_sems.at[base + a, k],
                    device_id=dev, device_id_type=MESH))
        return out

    shapes = [jax.ShapeDtypeStruct(_full_shape(s, ax), s.dtype) for s, ax in zip(shards, axes)]
    return _Comm(shards, shapes, (n, N_CHIP + 1), copies)


def _forward_blocks(fulls, axes):
    n = len(fulls)
    sizes = [None if ax is None else f.shape[ax] // N_DEV for f, ax in zip(fulls, axes)]

    def body(*refs):
        outs = refs[n:2 * n]
        send_sems, recv_sems = refs[2 * n:]
        sibling, _ = _peer(1)
        copies = []
        for j, dist in enumerate(OTHER_CHIPS):
            _, held = _peer(dist)
            for a in range(n):
                block = _window(outs[a], axes[a], sizes[a], held)
                copies.append(pltpu.make_async_remote_copy(
                    src_ref=block, dst_ref=block, send_sem=send_sems.at[a, j], recv_sem=recv_sems.at[a, j],
                    device_id=sibling, device_id_type=MESH))
        for cp in copies:
            cp.start()
        for cp in copies:
            cp.wait()

    return pl.pallas_call(
        body, in_specs=[HBM] * n, out_specs=[HBM] * n,
        out_shape=[jax.ShapeDtypeStruct(f.shape, f.dtype) for f in fulls],
        scratch_shapes=[pltpu.SemaphoreType.DMA((n, len(OTHER_CHIPS))), pltpu.SemaphoreType.DMA((n, len(OTHER_CHIPS)))],
        input_output_aliases={a: a for a in range(n)},
        name="forward_blocks",
    )(*fulls)


def _shard_order():
    (x, y, c), me = _me()
    xs, ys, xo, yo = _peer(4)[1], _peer(2)[1], _peer(5)[1], _peer(3)[1]
    north = c == 1
    ids = [me, _peer(1)[1], jnp.where(north, xs, ys), jnp.where(north, yo, xo), jnp.where(north, ys, xs),
           jnp.where(north, xo, yo), _peer(6)[1], _peer(7)[1]]
    return jnp.stack(ids).astype(jnp.int32)


def _proj_gather(h, w_shard, order, comm, comm_at):
    T, K = h.shape
    C = w_shard.shape[1]
    tm = _pick(T, 1024)
    nm = T // tm
    ahead = max(nm - 2, 0)
    n_ci, n_co = len(comm.ins), len(comm.out_shapes)

    def body(*refs):
        order_ref, h_ref, ws_ref = refs[:3]
        c_ins = refs[3:3 + n_ci]
        o_ref, wf_ref = refs[3 + n_ci:5 + n_ci]
        c_outs = refs[5 + n_ci:5 + n_ci + n_co]
        wbuf, send_sems, recv_sems, own_sem, buf_sems, c_send, c_recv = refs[5 + n_ci + n_co:]
        t, i = pl.program_id(0), pl.program_id(1)

        @pl.when((t == comm_at) & (i == 0))
        def _():
            for cp in comm.copies(c_ins, c_outs, c_send, c_recv):
                cp.start()
        (x, y, c), me = _me()
        sibling, sib_id = _peer(1)
        chips = [_peer(dist) for dist in OTHER_CHIPS]

        def win(idx):
            return wf_ref.at[:, pl.ds(idx * C, C)]

        def copy(k, block, to, src=None):
            return pltpu.make_async_remote_copy(
                src_ref=win(block) if src is None else src, dst_ref=win(block),
                send_sem=send_sems.at[k], recv_sem=recv_sems.at[k], device_id=to, device_id_type=MESH)

        def fetch(tt, src):
            return pltpu.make_async_copy(src, wbuf.at[tt % 2], buf_sems.at[tt % 2])

        own = pltpu.make_async_copy(ws_ref, win(me), own_sem)
        (x_nbr, x_id), (y_nbr, y_id), (_, d_id) = chips

        def half(k, block, top, to):
            rows = wf_ref.at[pl.ds(0 if top else K // 2, K // 2), pl.ds(block * C, C)]
            return pltpu.make_async_remote_copy(src_ref=rows, dst_ref=rows, send_sem=send_sems.at[k], recv_sem=recv_sems.at[k],
                                                device_id=to, device_id_type=MESH)

        here = (x, y, c)

        def pass_on(from_x):
            if from_x:
                copy(4, x_id, sibling).start()
                half(8, x_id, False, y_nbr).start()
            else:
                copy(5, y_id, sibling).start()
                half(7, y_id, True, x_nbr).start()

        @pl.when((t == 0) & (i == 0))
        def _():
            fetch(0, ws_ref).start()
            own.start()
            copy(0, me, sibling, src=ws_ref).start()

        for tt in range(N_DEV):
            @pl.when((t == tt) & (i == 0))
            def _(tt=tt):
                fetch(tt, ws_ref).wait()

        o_ref[...] = _dot(h_ref[...], wbuf[t % 2], NN_DIMS).astype(o_ref.dtype)

        def schedule(first_x):
            on = c == (1 if first_x else 0)
            (f_dev, f_id), (g_dev, g_id) = ((x_nbr, x_id), (y_nbr, y_id)) if first_x else ((y_nbr, y_id), (x_nbr, x_id))
            kf, kg = (1, 2) if first_x else (2, 1)
            pf, pg = (4, 5) if first_x else (5, 4)

            @pl.when((t == 0) & (i == 0) & on)
            def _():
                copy(kf, me, f_dev, src=ws_ref).start()

            def event(nxt):
                if nxt == 1:
                    copy(0, sib_id, here).wait_recv()
                    return sib_id
                if nxt == 2:
                    copy(kf, f_id, here).wait_recv()
                    copy(kf, me, f_dev, src=ws_ref).wait_send()
                    copy(kg, me, g_dev, src=ws_ref).start()
                    pass_on(first_x)
                    return f_id
                if nxt == 3:
                    copy(pg, g_id + 1 - 2 * c, here).wait_recv()
                    return g_id + 1 - 2 * c
                if nxt == 4:
                    copy(kg, g_id, here).wait_recv()
                    pass_on(not first_x)
                    return g_id
                if nxt == 5:
                    copy(pf, f_id + 1 - 2 * c, here).wait_recv()
                    return f_id + 1 - 2 * c
                if nxt == 6:
                    half(7, d_id, True, here).wait_recv()
                    half(8, d_id, False, here).wait_recv()
                    copy(6, d_id, sibling).start()
                    return d_id
                copy(6, d_id + 1 - 2 * c, here).wait_recv()
                return d_id + 1 - 2 * c

            for tt in range(N_DEV - 1):
                @pl.when((t == tt) & (i == ahead) & on)
                def _(tt=tt):
                    fetch(tt + 1, win(event(tt + 1))).start()

            @pl.when((t == N_DEV - 1) & (i == nm - 1) & on)
            def _():
                copy(kg, me, g_dev, src=ws_ref).wait_send()

        schedule(True)
        schedule(False)

        @pl.when((t == N_DEV - 1) & (i == nm - 1))
        def _():
            copy(0, me, sibling, src=ws_ref).wait_send()
            half(7, y_id, True, x_nbr).wait_send()
            half(8, x_id, False, y_nbr).wait_send()
            for j, (_, dev_id) in enumerate(chips):
                copy(4 + j, dev_id, sibling).wait_send()
            own.wait()
            for cp in comm.copies(c_ins, c_outs, c_send, c_recv):
                cp.wait()

    hbm = pl.BlockSpec(memory_space=pl.ANY)
    res = pl.pallas_call(
        body,
        grid_spec=pltpu.PrefetchScalarGridSpec(
            num_scalar_prefetch=1, grid=(N_DEV, nm),
            in_specs=[pl.BlockSpec((tm, K), lambda t, i, order_ref: (i, 0)), hbm] + [hbm] * n_ci,
            out_specs=[pl.BlockSpec((tm, C), lambda t, i, order_ref: (i, order_ref[t])), hbm] + [hbm] * n_co,
            scratch_shapes=[pltpu.VMEM((2, K, C), bf16), pltpu.SemaphoreType.DMA((9,)), pltpu.SemaphoreType.DMA((9,)),
                            pltpu.SemaphoreType.DMA, pltpu.SemaphoreType.DMA((2,))] + comm.scratch()),
        out_shape=[jax.ShapeDtypeStruct((T, N_DEV * C), PROJ_DTYPE), jax.ShapeDtypeStruct((K, N_DEV * C), bf16)] + comm.out_shapes,
        compiler_params=_params(("arbitrary", "arbitrary")), name="proj_gather",
    )(order, h, w_shard, *comm.ins)
    return res[0], res[1], list(res[2:])


N_CHIP = 4


def _shard_shape(g, ax):
    return tuple(d // N_DEV if i == ax else d for i, d in enumerate(g.shape))


def _sibling_comm(grads, axes):
    n = len(grads)
    sizes = [g.shape[ax] // N_DEV for g, ax in zip(grads, axes)]

    def copies(ins, lands, send_sems, recv_sems, base=0):
        sibling, _ = _peer(1)
        out = []
        for j in range(N_CHIP):
            _, owner = _peer(2 * j + 1)
            for a in range(n):
                out.append(pltpu.make_async_remote_copy(
                    src_ref=_window(ins[a], axes[a], sizes[a], owner), dst_ref=lands[a].at[j],
                    send_sem=send_sems.at[base + a, j], recv_sem=recv_sems.at[base + a, j], device_id=sibling,
                    device_id_type=MESH))
        return out

    shapes = [jax.ShapeDtypeStruct((N_CHIP,) + _shard_shape(g, ax), g.dtype) for g, ax in zip(grads, axes)]
    return _Comm(grads, shapes, (n, N_CHIP), copies)


def _chips_comm(sums):
    n = len(sums)

    def copies(ins, lands, send_sems, recv_sems, base=0):
        out = []
        for j in range(1, N_CHIP):
            dev, _ = _peer(2 * j)
            for a in range(n):
                out.append(pltpu.make_async_remote_copy(
                    src_ref=ins[a].at[j - 1], dst_ref=lands[a].at[j - 1],
                    send_sem=send_sems.at[base + a, j - 1], recv_sem=recv_sems.at[base + a, j - 1], device_id=dev,
                    device_id_type=MESH))
        return out

    return _Comm(sums, [jax.ShapeDtypeStruct(s.shape, s.dtype) for s in sums], (n, N_CHIP), copies)


def _join(c1, c2):
    n1, m1, k1 = len(c1.ins), len(c1.out_shapes), c1.sem_shape[0]

    def copies(ins, lands, send_sems, recv_sems):
        return (c1.copies(ins[:n1], lands[:m1], send_sems, recv_sems, base=0)
                + c2.copies(ins[n1:], lands[m1:], send_sems, recv_sems, base=k1))

    aliases = {**c1.aliases, **{n1 + k: m1 + v for k, v in c2.aliases.items()}}
    return _Comm(c1.ins + c2.ins, c1.out_shapes + c2.out_shapes, (k1 + c2.sem_shape[0], N_CHIP), copies, aliases)


def _chips_first_hop(sums):
    n = len(sums)

    def copies(ins, outs, send_sems, recv_sems, base=0):
        lands, relays = outs[:n], outs[n:]
        (y_dev, _), (x_dev, _) = _peer(2), _peer(4)
        out = []
        for a in range(n):
            half = sums[a].shape[1] // 2
            for k, (src, dst, dev) in enumerate((
                    (ins[a].at[0], lands[a].at[0], y_dev), (ins[a].at[1], lands[a].at[1], x_dev),
                    (ins[a].at[2, pl.ds(0, half)], relays[a].at[0], x_dev),
                    (ins[a].at[2, pl.ds(half, half)], relays[a].at[1], y_dev))):
                out.append(pltpu.make_async_remote_copy(
                    src_ref=src, dst_ref=dst, send_sem=send_sems.at[base + a, k], recv_sem=recv_sems.at[base + a, k],
                    device_id=dev, device_id_type=MESH))
        return out

    shapes = ([jax.ShapeDtypeStruct(s.shape, s.dtype) for s in sums]
              + [jax.ShapeDtypeStruct((2, s.shape[1] // 2) + s.shape[2:], s.dtype) for s in sums])
    return _Comm(sums, shapes, (n, N_CHIP), copies)


def _chips_relay(relays, lands):
    n = len(relays)

    def copies(ins, outs, send_sems, recv_sems, base=0):
        (y_dev, _), (x_dev, _) = _peer(2), _peer(4)
        out = []
        for a in range(n):
            half = relays[a].shape[1]
            for k, (rows, dev) in enumerate(((pl.ds(0, half), y_dev), (pl.ds(half, half), x_dev))):
                out.append(pltpu.make_async_remote_copy(
                    src_ref=ins[a].at[k], dst_ref=outs[a].at[2, rows], send_sem=send_sems.at[base + a, k],
                    recv_sem=recv_sems.at[base + a, k], device_id=dev, device_id_type=MESH))
        return out

    return _Comm(list(relays) + list(lands), [jax.ShapeDtypeStruct(l.shape, l.dtype) for l in lands], (n, N_CHIP), copies,
                 aliases={n + a: a for a in range(n)})


def _rs_chip_sum(grad, land, xyc, axis, name):
    R, C = land.shape[1:]
    tr = _pick(R, max(8, (640 * 1024) // C), 8)

    def body(xyc_ref, g_ref, l_ref, o_ref):
        o_ref[0] = (g_ref[...] + l_ref[0]).astype(bf16)

    def owner(j, xyc_ref):
        jj = j + 1
        return 4 * (xyc_ref[0] ^ (jj >> 1)) + 2 * (xyc_ref[1] ^ (jj & 1)) + xyc_ref[2]

    if axis == 0:
        g_spec = pl.BlockSpec((tr, C), lambda j, i, xyc_ref: (owner(j, xyc_ref) * (R // tr) + i, 0))
    else:
        g_spec = pl.BlockSpec((tr, C), lambda j, i, xyc_ref: (i, owner(j, xyc_ref)))
    return pl.pallas_call(
        body,
        grid_spec=pltpu.PrefetchScalarGridSpec(
            num_scalar_prefetch=1, grid=(N_CHIP - 1, R // tr),
            in_specs=[g_spec, pl.BlockSpec((1, tr, C), lambda j, i, xyc_ref: (j + 1, i, 0))],
            out_specs=pl.BlockSpec((1, tr, C), lambda j, i, xyc_ref: (j, i, 0))),
        out_shape=jax.ShapeDtypeStruct((N_CHIP - 1, R, C), bf16),
        compiler_params=_params(("parallel", "parallel")), name=name,
    )(xyc, grad, land)


def _allreduce_small(part):
    R = part.shape[0]

    def body(p_ref, o_ref, buf, send_sems, recv_sems):
        (x, y, c), me = _me()
        buf[me] = p_ref[...]
        copies = []
        for k in range(1, N_DEV):
            dev, dev_id = _peer(k)
            copies.append(pltpu.make_async_remote_copy(
                src_ref=p_ref, dst_ref=buf.at[me], send_sem=send_sems.at[k - 1], recv_sem=recv_sems.at[k - 1],
                device_id=dev, device_id_type=MESH))
        for cp in copies:
            cp.start()
        for k in range(1, N_DEV):
            _, dev_id = _peer(k)
            pltpu.make_async_remote_copy(
                src_ref=p_ref, dst_ref=buf.at[dev_id], send_sem=send_sems.at[k - 1], recv_sem=recv_sems.at[k - 1],
                device_id=(x, y, c), device_id_type=MESH).wait_recv()
        for cp in copies:
            cp.wait_send()
        acc = buf[0]
        for d in range(1, N_DEV):
            acc = acc + buf[d]
        o_ref[...] = acc

    return pl.pallas_call(
        body, in_specs=[pl.BlockSpec(memory_space=pltpu.VMEM)], out_specs=pl.BlockSpec(memory_space=pltpu.VMEM),
        out_shape=jax.ShapeDtypeStruct((R, LANES), f32),
        scratch_shapes=[pltpu.VMEM((N_DEV, R, LANES), f32), pltpu.SemaphoreType.DMA((N_DEV - 1,)), pltpu.SemaphoreType.DMA((N_DEV - 1,))],
        name="allreduce_small",
    )(part)


def _adamw_math(w, g, m, v):
    m2 = ADAM_B1 * m + (1.0 - ADAM_B1) * g
    v2 = ADAM_B2 * v + (1.0 - ADAM_B2) * (g * g)
    m_hat = m2 / (1.0 - ADAM_B1 ** ADAM_STEP)
    v_hat = v2 / (1.0 - ADAM_B2 ** ADAM_STEP)
    return -ADAM_LR * (m_hat / (jnp.sqrt(v_hat) + ADAM_EPS) + ADAM_WD * w), m2, v2


def _adamw_shard(grad, land_sib, land_chips, w, m, v, me, axis, name, row0=0, prev=None):
    R, C = land_sib.shape[1:]
    assert axis == 1 or R == w.shape[0]
    tr = _pick(R, max(8, (320 * 1024) // C), 8)
    r0 = row0 // tr
    n_prev = len(prev) if prev else 0

    def body(me_ref, g_ref, s_ref, l_ref, w_ref, m_ref, v_ref, *rest):
        go_ref, d_ref, mo_ref, vo_ref = rest[n_prev:]
        g = g_ref[...] + s_ref[0]
        for j in range(N_CHIP - 1):
            g = g + l_ref[j].astype(f32)
        d, m2, v2 = _adamw_math(w_ref[...], g, m_ref[...], v_ref[...])
        go_ref[...] = g
        d_ref[...] = d
        mo_ref[...] = m2
        vo_ref[...] = v2

    if axis == 0:
        g_spec = pl.BlockSpec((tr, C), lambda i, me_ref: (me_ref[0] * (R // tr) + i, 0))
    else:
        g_spec = pl.BlockSpec((tr, C), lambda i, me_ref: (i, me_ref[0]))
    blk = pl.BlockSpec((tr, C), lambda i, me_ref: (r0 + i, 0))
    return pl.pallas_call(
        body,
        grid_spec=pltpu.PrefetchScalarGridSpec(
            num_scalar_prefetch=1, grid=(R // tr,),
            in_specs=[g_spec, pl.BlockSpec((1, tr, C), lambda i, me_ref: (0, i, 0)),
                      pl.BlockSpec((N_CHIP - 1, tr, C), lambda i, me_ref: (0, i, 0)), blk, blk, blk]
            + [pl.BlockSpec(memory_space=pl.ANY)] * n_prev,
            out_specs=[blk] * 4),
        out_shape=[jax.ShapeDtypeStruct(w.shape, f32)] * 4,
        input_output_aliases={7 + i: i for i in range(n_prev)},
        compiler_params=_params(("parallel",)), name=name,
    )(me, grad, land_sib, land_chips, w, m, v, *(prev or []))


def _adamw_small(g, w, m, v):
    def body(g_ref, w_ref, m_ref, v_ref, d_ref, mo_ref, vo_ref):
        d, m2, v2 = _adamw_math(w_ref[...], g_ref[...], m_ref[...], v_ref[...])
        d_ref[...] = d
        mo_ref[...] = m2
        vo_ref[...] = v2

    return pl.pallas_call(body, out_shape=[jax.ShapeDtypeStruct(g.shape, f32)] * 3, name="adamw_small")(g, w, m, v)


def _pack_rows(parts, total_rows):
    rows = jnp.concatenate([p.reshape(-1, LANES) for p in parts], axis=0)
    return jnp.pad(rows, ((0, total_rows - rows.shape[0]), (0, 0)))


BIG = ("w_in", "mem_w_kv", "w_br_attn", "w_br_conv", "w_br_mem", "w_out")
BIG_AXIS = {"w_in": 1, "mem_w_kv": 0, "w_br_attn": 1, "w_br_conv": 1, "w_br_mem": 1, "w_out": 0}
SMALL = ("norm_g", "mem_norm_g", "attn_q_norm", "attn_k_norm", "mem_q_norm", "mem_k_norm")
ALL_W = ("norm_g", "mem_norm_g", "w_in", "attn_q_norm", "attn_k_norm", "conv_w", "mem_w_kv", "mem_q_norm", "mem_k_norm",
         "w_br_attn", "w_br_conv", "w_br_mem", "w_out")


def kernel(x, mem, norm_g, mem_norm_g, w_in, attn_q_norm, attn_k_norm, conv_w, mem_w_kv, mem_q_norm, mem_k_norm, w_br_attn, w_br_conv, w_br_mem, w_out, loss_target, m_norm_g, m_mem_norm_g, m_w_in, m_attn_q_norm, m_attn_k_norm, m_conv_w, m_mem_w_kv, m_mem_q_norm, m_mem_k_norm, m_w_br_attn, m_w_br_conv, m_w_br_mem, m_w_out, v_norm_g, v_mem_norm_g, v_w_in, v_attn_q_norm, v_attn_k_norm, v_conv_w, v_mem_w_kv, v_mem_q_norm, v_mem_k_norm, v_w_br_attn, v_w_br_conv, v_w_br_mem, v_w_out):
    w = dict(norm_g=norm_g, mem_norm_g=mem_norm_g, w_in=w_in, attn_q_norm=attn_q_norm, attn_k_norm=attn_k_norm, conv_w=conv_w,
             mem_w_kv=mem_w_kv, mem_q_norm=mem_q_norm, mem_k_norm=mem_k_norm, w_br_attn=w_br_attn, w_br_conv=w_br_conv,
             w_br_mem=w_br_mem, w_out=w_out)
    mo = dict(norm_g=m_norm_g, mem_norm_g=m_mem_norm_g, w_in=m_w_in, attn_q_norm=m_attn_q_norm, attn_k_norm=m_attn_k_norm,
              conv_w=m_conv_w, mem_w_kv=m_mem_w_kv, mem_q_norm=m_mem_q_norm, mem_k_norm=m_mem_k_norm, w_br_attn=m_w_br_attn,
              w_br_conv=m_w_br_conv, w_br_mem=m_w_br_mem, w_out=m_w_out)
    vo = dict(norm_g=v_norm_g, mem_norm_g=v_mem_norm_g, w_in=v_w_in, attn_q_norm=v_attn_q_norm, attn_k_norm=v_attn_k_norm,
              conv_w=v_conv_w, mem_w_kv=v_mem_w_kv, mem_q_norm=v_mem_q_norm, mem_k_norm=v_mem_k_norm, w_br_attn=v_w_br_attn,
              w_br_conv=v_w_br_conv, w_br_mem=v_w_br_mem, w_out=v_w_out)
    B, S, D = x.shape
    me = (4 * lax.axis_index("x") + 2 * lax.axis_index("y") + lax.axis_index("c")).astype(jnp.int32)

    T = B * S
    x2, t2, mem2, ng = x.reshape(T, D), loss_target.reshape(T, D), mem.reshape(B * MEM_LEN, D), norm_g.reshape(1, D)
    h = _rms_fwd(x2, ng, "rms_x")
    with_proj, with_attn = ("mem_w_kv",), ("w_out", "w_br_attn", "w_br_conv", "w_br_mem")
    later = with_proj + with_attn
    later_axes = [BIG_AXIS[n] for n in later] + [None]
    conv_pad = jnp.pad(conv_w, ((0, 8 - conv_w.shape[0]), (0, 0)))
    proj, w_in_full, spread_a = _proj_gather(
        h, w_in.astype(bf16), _shard_order(),
        _spread_comm([w[n].astype(bf16) for n in with_proj], [BIG_AXIS[n] for n in with_proj]), comm_at=N_DEV - 3)
    *attn, spread_b = _attn_fwd(proj, attn_q_norm, attn_k_norm, B, S,
                                comm=_spread_comm([w[n].astype(bf16) for n in with_attn] + [conv_pad],
                                                  [BIG_AXIS[n] for n in with_attn] + [None]))
    *fulls, conv_g = _forward_blocks(spread_a + spread_b, later_axes)
    wf = dict(zip(later, fulls), w_in=w_in_full)
    conv_full = conv_g[:, :3, :].transpose(1, 0, 2).reshape(3, CONV_W)

    sq_err, g, t = _fwd_bwd_head(x2, mem2, t2, proj, attn, B, S, mem_norm_g, attn_q_norm, attn_k_norm, conv_full, mem_q_norm,
                                 mem_k_norm, wf["mem_w_kv"], wf["w_br_attn"], wf["w_br_conv"], wf["w_br_mem"], wf["w_out"])
    t.update(x2=x2, ng=ng, h=h)

    xyc = jnp.stack([lax.axis_index("x"), lax.axis_index("y"), lax.axis_index("c")]).astype(jnp.int32)
    me1 = me.reshape(1)
    early = ("w_out", "w_br_attn", "w_br_conv", "w_br_mem")
    g["mem_w_kv"], sib_early = _mm(t["mh"], t["dmkv"], mode="tn", out_dtype=f32, name="dw_kv",
                                   comm=_sibling_comm([g[n] for n in early], [BIG_AXIS[n] for n in early]))
    sums_early = [_rs_chip_sum(g[n], ls, xyc, BIG_AXIS[n], "rs_sum_" + n) for n, ls in zip(early, sib_early)]
    tm_w = _pick(D // 2, 1024)
    half = (D // 2) // tm_w
    g_top, (*chips_early, sib_kv) = _mm(t["h"], t["dproj"], mode="tn", out_dtype=f32, name="dw_in_top", tm=tm_w,
                                        m_tiles=(0, half),
                                        comm=_join(_chips_comm(sums_early), _sibling_comm([g["mem_w_kv"]], [0])))
    sum_kv = _rs_chip_sum(g["mem_w_kv"], sib_kv, xyc, 0, "rs_sum_mem_w_kv")
    g_bot, (chips_kv, sib_top) = _mm(t["h"], t["dproj"], mode="tn", out_dtype=f32, name="dw_in_bot", tm=tm_w,
                                     m_tiles=(half, half), comm=_join(_chips_comm([sum_kv]), _sibling_comm([g_top], [1])))
    sum_top = _rs_chip_sum(g_top, sib_top, xyc, 1, "rs_sum_w_in_top")
    tm_t = _pick(T // 2, 1024)
    halfm = (T // 2) // tm_t
    dh, (part_top, relay_top, sib_bot) = _mm(t["dproj"], wf["w_in"], mode="nt", out_dtype=f32, name="d_h_a", tm=tm_t, tk=D_H_TK,
                                             m_tiles=(0, halfm), into="new",
                                             comm=_join(_chips_first_hop([sum_top]), _sibling_comm([g_bot], [1])))
    sum_bot = _rs_chip_sum(g_bot, sib_bot, xyc, 1, "rs_sum_w_in_bot")
    dh, (part_bot, relay_bot, chips_top) = _mm(t["dproj"], wf["w_in"], mode="nt", out_dtype=f32, name="d_h_b", tm=tm_t,
                                               tk=D_H_TK, m_tiles=(halfm, halfm), into=dh,
                                               comm=_join(_chips_first_hop([sum_bot]), _chips_relay([relay_top], [part_top])))
    dx, dng, (chips_bot,) = _rms_bwd(t["x2"], t["ng"], dh, t["dy"], "rms_x_bwd", comm=_chips_relay([relay_bot], [part_bot]))
    g["norm_g"] = dng.reshape(D)

    grad, delta, new_m, new_v = {}, {}, {}, {}
    for n, ls, lc in zip(early + ("mem_w_kv",), sib_early + [sib_kv], chips_early + [chips_kv]):
        grad[n], delta[n], new_m[n], new_v[n] = _adamw_shard(g[n], ls, lc, w[n], mo[n], vo[n], me1, BIG_AXIS[n], "adamw_" + n)
    top = _adamw_shard(g_top, sib_top, chips_top, w_in, m_w_in, v_w_in, me1, 1, "adamw_w_in_top")
    grad["w_in"], delta["w_in"], new_m["w_in"], new_v["w_in"] = _adamw_shard(
        g_bot, sib_bot, chips_bot, w_in, m_w_in, v_w_in, me1, 1, "adamw_w_in_bot", row0=D // 2, prev=top)

    n_rep = sum(w[n].size for n in SMALL) // LANES
    conv_rows = g["conv_w"].reshape(3, N_DEV, LANES).transpose(1, 0, 2).reshape(3 * N_DEV, LANES)
    n_rows = -(-(n_rep + 3 * N_DEV + 1) // 8) * 8
    part = _pack_rows([g[n] for n in SMALL] + [conv_rows, jnp.full((1, LANES), sq_err, f32)], n_rows)
    tot = _allreduce_small(part)
    loss = tot[n_rep + 3 * N_DEV, 0] * (0.5 / D)
    n_small = -(-(n_rep + 3) // 8) * 8
    g_small = _pack_rows([tot[:n_rep], lax.dynamic_slice(tot, (n_rep + 3 * me, 0), (3, LANES))], n_small)
    names = SMALL + ("conv_w",)
    d_s, m_s, v_s = _adamw_small(g_small, _pack_rows([w[n] for n in names], n_small), _pack_rows([mo[n] for n in names], n_small),
                                 _pack_rows([vo[n] for n in names], n_small))
    off = 0
    for n in names:
        r = w[n].size // LANES
        grad[n], delta[n] = g_small[off:off + r].reshape(w[n].shape), d_s[off:off + r].reshape(w[n].shape)
        new_m[n], new_v[n] = m_s[off:off + r].reshape(w[n].shape), v_s[off:off + r].reshape(w[n].shape)
        off += r
    return (loss, dx.reshape(B, S, D), *[grad[n] for n in ALL_W], *[delta[n] for n in ALL_W], *[new_m[n] for n in ALL_W],
            *[new_v[n] for n in ALL_W])
```

```python
import functools

import jax
import jax.numpy as jnp
from jax import lax
from jax.experimental import pallas as pl
from jax.experimental.pallas import tpu as pltpu

f32 = jnp.float32
bf16 = jnp.bfloat16

N_DEV = 8
HEAD_DIM = 128
DILATIONS = (1, 4, 16)
N_GROUPS = 3
HEADS = 4
BLK = 128
ATTN_QKV = N_GROUPS * HEADS * HEAD_DIM
ATTN_OUT = HEADS * HEAD_DIM
CONV_W = 1024
MEM_LEN = 256
MEM_HEADS = 4
MEM_HD = 256
MEM_W = MEM_HEADS * MEM_HD
EPS = 1e-6
Q0, K0, V0 = 0, ATTN_QKV, 2 * ATTN_QKV
ZA = 3 * ATTN_QKV
CB, CC, CV, ZC = ZA + ATTN_OUT, ZA + ATTN_OUT + CONV_W, ZA + ATTN_OUT + 2 * CONV_W, ZA + ATTN_OUT + 3 * CONV_W
MQ = ZC + CONV_W
ZM = MQ + MEM_W
GT = ZM + MEM_W

ADAM_LR, ADAM_B1, ADAM_B2, ADAM_EPS, ADAM_WD, ADAM_STEP = 0.001, 0.9, 0.999, 1e-08, 0.01, 10

VMEM_LIMIT = 56 * 1024 * 1024
D_H_TK = 4352
PROJ_DTYPE = bf16
DW_DTYPE = bf16
LANES = 128

NT_DIMS = (((1,), (1,)), ((), ()))
TN_DIMS = (((0,), (0,)), ((), ()))
NN_DIMS = (((1,), (0,)), ((), ()))


def _params(sem=None):
    return pltpu.CompilerParams(dimension_semantics=sem, vmem_limit_bytes=VMEM_LIMIT)


def _pick(n, pref, q=LANES):
    t = (min(pref, n) // q) * q
    while t >= q:
        if n % t == 0:
            return t
        t -= q
    return n


def _dot(a, b, dims):
    return lax.dot_general(a.astype(bf16), b.astype(bf16), dims, preferred_element_type=f32)


def _sigmoid(z):
    return 0.5 * jnp.tanh(0.5 * z) + 0.5


def _rstd(v):
    return lax.rsqrt(jnp.mean(v * v, axis=-1, keepdims=True) + EPS)


class _Deferred:
    def __init__(self, stage, sems, step, last, n):
        assert last >= 1
        self.stage, self.sems, self.step, self.last, self.n = stage, sems, step, last, n
        self.slot = step % 2

    def _drain(self, slot, like):
        for c in range(self.n):
            pltpu.make_async_copy(self.stage.at[slot, c], like(c), self.sems.at[slot, c]).wait()

    def begin(self, like):
        pl.when(self.step >= 2)(lambda: self._drain(self.slot, like))

    def start(self, c, dst):
        pltpu.make_async_copy(self.stage.at[self.slot, c], dst, self.sems.at[self.slot, c]).start()

    def end(self, like):
        @pl.when(self.step == self.last)
        def _():
            self._drain(self.slot, like)
            self._drain(1 - self.slot, like)


def _row_sum(v):
    return _dot(v, jnp.ones((v.shape[1], v.shape[1]), bf16), NN_DIMS)


def _rstd_head(v):
    return lax.rsqrt(_row_sum(v * v) * (1.0 / v.shape[1]) + EPS)


class _Comm:
    def __init__(self, ins, out_shapes, sem_shape, copies, aliases=None):
        self.ins, self.out_shapes, self.sem_shape, self.copies = list(ins), list(out_shapes), sem_shape, copies
        self.aliases = dict(aliases or {})

    def scratch(self):
        return [pltpu.SemaphoreType.DMA(self.sem_shape), pltpu.SemaphoreType.DMA(self.sem_shape)]

    def io_aliases(self, first_in, first_out):
        return {first_in + k: first_out + v for k, v in self.aliases.items()}


def _mm(a, b, *, mode, out_dtype, name, tm=1024, tn=1024, tk=4096, m_tiles=None, into=None, comm=None):
    if mode == "nn":
        (M, K), (K2, N) = a.shape, b.shape
    elif mode == "nt":
        (M, K), (N, K2) = a.shape, b.shape
    else:
        (K, M), (K2, N) = a.shape, b.shape
    assert K == K2
    tm, tn, tk = _pick(M, tm), _pick(N, tn), _pick(K, tk)
    nk = K // tk
    m0, mc = m_tiles if m_tiles is not None else (0, M // tm)
    o0 = 0 if into is None else m0
    aliased = into is not None and not isinstance(into, str)
    n_ci = len(comm.ins) if comm else 0
    n_co = len(comm.out_shapes) if comm else 0
    grid = (mc, N // tn, nk)
    dims = {"nn": NN_DIMS, "nt": NT_DIMS, "tn": TN_DIMS}[mode]

    def body(*refs, nk):
        a_ref, b_ref = refs[:2]
        pos = 3 if aliased else 2
        c_ins, o_ref, c_outs = refs[pos:pos + n_ci], refs[pos + n_ci], refs[pos + n_ci + 1:pos + n_ci + 1 + n_co]
        scratch = refs[pos + n_ci + 1 + n_co:]
        ids = [pl.program_id(d) for d in range(3)]
        if comm:
            sems = scratch[-2:]

            @pl.when((ids[0] == 0) & (ids[1] == 0) & (ids[2] == 0))
            def _():
                for cp in comm.copies(c_ins, c_outs, *sems):
                    cp.start()

        if nk == 1:
            o_ref[...] = _dot(a_ref[...], b_ref[...], dims).astype(o_ref.dtype)
        else:
            acc_ref, k = scratch[0], ids[2]

            @pl.when(k == 0)
            def _():
                acc_ref[...] = _dot(a_ref[...], b_ref[...], dims)

            @pl.when((k > 0) & (k < nk - 1))
            def _():
                acc_ref[...] += _dot(a_ref[...], b_ref[...], dims)

            @pl.when(k == nk - 1)
            def _():
                o_ref[...] = (acc_ref[...] + _dot(a_ref[...], b_ref[...], dims)).astype(o_ref.dtype)

        if comm:
            @pl.when((ids[0] == grid[0] - 1) & (ids[1] == grid[1] - 1) & (ids[2] == grid[2] - 1))
            def _():
                for cp in comm.copies(c_ins, c_outs, *sems):
                    cp.wait()

    if mode == "tn":
        a_spec = pl.BlockSpec((tk, tm), lambda i, j, k: (k, m0 + i))
    else:
        a_spec = pl.BlockSpec((tm, tk), lambda i, j, k: (m0 + i, k))
    if mode == "nt":
        b_spec = pl.BlockSpec((tn, tk), lambda i, j, k: (j, k))
    else:
        b_spec = pl.BlockSpec((tk, tn), lambda i, j, k: (k, j))
    hbm = pl.BlockSpec(memory_space=pl.ANY)
    res = pl.pallas_call(
        functools.partial(body, nk=nk),
        grid=grid,
        in_specs=[a_spec, b_spec] + [hbm] * (aliased + n_ci),
        out_specs=[pl.BlockSpec((tm, tn), lambda i, j, k: (o0 + i, j))] + [hbm] * n_co,
        out_shape=[jax.ShapeDtypeStruct((mc * tm if into is None else M, N), out_dtype)] + (comm.out_shapes if comm else []),
        scratch_shapes=([pltpu.VMEM((tm, tn), f32)] if nk > 1 else []) + (comm.scratch() if comm else []),
        input_output_aliases={**({2: 0} if aliased else {}), **(comm.io_aliases(2 + aliased, 1) if comm else {})},
        compiler_params=_params(("arbitrary",) * 3 if comm else ("parallel", "parallel", "arbitrary")),
        name=name,
    )(a, b, *([into] if aliased else []), *(comm.ins if comm else []))
    return (res[0], list(res[1:])) if comm else res[0]


def _rms_fwd(x, g, name):
    T, D = x.shape
    tm = _pick(T, 256, 8)

    def body(x_ref, g_ref, h_ref):
        xv = x_ref[...]
        h_ref[...] = (xv * _rstd(xv) * g_ref[...]).astype(bf16)

    return pl.pallas_call(
        body, grid=(T // tm,),
        in_specs=[pl.BlockSpec((tm, D), lambda i: (i, 0)), pl.BlockSpec((1, D), lambda i: (0, 0))],
        out_specs=pl.BlockSpec((tm, D), lambda i: (i, 0)),
        out_shape=jax.ShapeDtypeStruct((T, D), bf16),
        compiler_params=_params(("parallel",)), name=name,
    )(x, g)


def _rms_bwd(x, g, dh, dy, name, comm=None):
    T, D = x.shape
    tm = _pick(T, 256, 8)
    with_dx = dy is not None
    n_ci = len(comm.ins) if comm else 0
    n_co = len(comm.out_shapes) if comm else 0

    def body(*refs):
        if with_dx:
            x_ref, g_ref, dh_ref, dy_ref = refs[:4]
            c_ins, (dx_ref, dg_ref) = refs[4:4 + n_ci], refs[4 + n_ci:6 + n_ci]
            c_outs, sems = refs[6 + n_ci:6 + n_ci + n_co], refs[6 + n_ci + n_co:]
        else:
            x_ref, g_ref, dh_ref, dg_ref = refs
        if comm:
            @pl.when(pl.program_id(0) == 0)
            def _():
                for cp in comm.copies(c_ins, c_outs, *sems):
                    cp.start()

        xv = x_ref[...]
        r = _rstd(xv)
        xh = xv * r
        dhv = dh_ref[...]
        part = jnp.sum(dhv * xh, axis=0, keepdims=True)

        @pl.when(pl.program_id(0) == 0)
        def _():
            dg_ref[...] = part

        @pl.when(pl.program_id(0) > 0)
        def _():
            dg_ref[...] += part

        if with_dx:
            dxh = dhv * g_ref[...]
            dx_ref[...] = dy_ref[...] + r * (dxh - xh * jnp.mean(dxh * xh, axis=-1, keepdims=True))

        if comm:
            @pl.when(pl.program_id(0) == T // tm - 1)
            def _():
                for cp in comm.copies(c_ins, c_outs, *sems):
                    cp.wait()

    row = pl.BlockSpec((tm, D), lambda i: (i, 0))
    vec = pl.BlockSpec((1, D), lambda i: (0, 0))
    hbm = pl.BlockSpec(memory_space=pl.ANY)
    if with_dx:
        res = pl.pallas_call(
            body, grid=(T // tm,), in_specs=[row, vec, row, row] + [hbm] * n_ci, out_specs=[row, vec] + [hbm] * n_co,
            out_shape=[jax.ShapeDtypeStruct((T, D), f32), jax.ShapeDtypeStruct((1, D), f32)] + (comm.out_shapes if comm else []),
            scratch_shapes=comm.scratch() if comm else [],
            input_output_aliases=comm.io_aliases(4, 2) if comm else {},
            compiler_params=_params(("arbitrary",)), name=name,
        )(x, g, dh, dy, *(comm.ins if comm else []))
        return (res[0], res[1], list(res[2:])) if comm else res
    return pl.pallas_call(
        body, grid=(T // tm,), in_specs=[row, vec, row], out_specs=vec,
        out_shape=jax.ShapeDtypeStruct((1, D), f32),
        compiler_params=_params(("arbitrary",)), name=name,
    )(x, g, dh)


MAX_UNIT_UNROLL = 6


def _unit_unroll(trips):
    return max(u for u in range(1, MAX_UNIT_UNROLL + 1) if trips % u == 0)


def _rows(start, size, stride):
    return pl.ds(start, size) if stride == 1 else pl.ds(start, size, stride=stride)


def _attn_units(S, d, first, prev):
    nb = S // d // BLK

    def do_first(r, c):
        first(_rows(r, BLK, d))
        return c

    lax.fori_loop(0, d, do_first, 0, unroll=_unit_unroll(d))
    if nb > 1:
        def do_prev(u, c):
            r = u // (nb - 1)
            b = u % (nb - 1) + 1
            base = r + d * b * BLK
            prev(_rows(base, BLK, d), _rows(base - d * BLK, 2 * BLK, d))
            return c

        lax.fori_loop(0, d * (nb - 1), do_prev, 0, unroll=_unit_unroll(d * (nb - 1)))


def _band_bias(nkeys):
    i = lax.broadcasted_iota(jnp.int32, (BLK, nkeys), 0)
    j = lax.broadcasted_iota(jnp.int32, (BLK, nkeys), 1)
    ok = (j <= i) if nkeys == BLK else ((j >= i) & (j <= i + BLK))
    return jnp.where(ok, 0.0, -jnp.inf).astype(f32)


def _widen_into(src_ref, dst_ref, S):
    for c in range(S // 256):
        rows = pl.ds(c * 256, 256)
        dst_ref[rows, :] = src_ref[rows, :].astype(f32)


def _normalize_into(src_ref, gain, dst_ref, S, rstd_ref=None):
    for c in range(S // 256):
        rows = pl.ds(c * 256, 256)
        v = src_ref[rows, :].astype(f32)
        r = _rstd_head(v)
        dst_ref[rows, :] = v * r * gain
        if rstd_ref is not None:
            rstd_ref[rows, :] = r


def _attn_fwd(proj, gq, gk, B, S, comm=None):
    T, NC = proj.shape
    scale = HEAD_DIM ** -0.5
    n_ci = len(comm.ins) if comm else 0
    n_co = len(comm.out_shapes) if comm else 0

    def body(*refs):
        q_ref, k_ref, v_ref, z_ref, gq_ref, gk_ref = refs[:6]
        c_ins = refs[6:6 + n_ci]
        ag_ref, a_ref, lse_ref = refs[6 + n_ci:9 + n_ci]
        c_outs = refs[9 + n_ci:9 + n_ci + n_co]
        qs, ks, vs, o0, o1, o2, l0, l1, l2, bias1, bias2 = refs[9 + n_ci + n_co:20 + n_ci + n_co]
        sems = refs[20 + n_ci + n_co:]
        g = pl.program_id(2)
        if comm:
            @pl.when((pl.program_id(0) == 0) & (pl.program_id(1) == 0) & (g == 0))
            def _():
                for cp in comm.copies(c_ins, c_outs, *sems):
                    cp.start()

        bias1[...] = _band_bias(BLK)
        bias2[...] = _band_bias(2 * BLK)
        _normalize_into(q_ref, gq_ref[pl.ds(g, 1), :], qs, S)
        _normalize_into(k_ref, gk_ref[pl.ds(g, 1), :], ks, S)
        _widen_into(v_ref, vs, S)
        o_s, l_s = (o0, o1, o2), (l0, l1, l2)

        def run(gi):
            def unit(qr, kr, nkeys):
                s = _dot(qs[qr, :], ks[kr, :], NT_DIMS) * scale + (bias1 if nkeys == BLK else bias2)[...]
                m = jnp.max(s, axis=-1, keepdims=True)
                p = jnp.exp(s - m)
                den = jnp.sum(p, axis=-1, keepdims=True)
                o_s[gi][qr, :] = _dot(p, vs[kr, :], NN_DIMS) * (1.0 / den)
                l_s[gi][qr, :] = jnp.broadcast_to(m + jnp.log(den), (BLK, HEAD_DIM))

            _attn_units(S, DILATIONS[gi], lambda qr: unit(qr, qr, BLK), lambda qr, kr: unit(qr, kr, 2 * BLK))

        for gi in range(N_GROUPS):
            pl.when(g == gi)(functools.partial(run, gi))

        @pl.when(g == N_GROUPS - 1)
        def _():
            for c in range(S // 256):
                rows = pl.ds(c * 256, 256)
                la, lb, lc = l0[rows, :], l1[rows, :], l2[rows, :]
                m = jnp.maximum(jnp.maximum(la, lb), lc)
                wa, wb, wc = jnp.exp(la - m), jnp.exp(lb - m), jnp.exp(lc - m)
                tot = wa + wb + wc
                a = (wa / tot) * o0[rows, :] + (wb / tot) * o1[rows, :] + (wc / tot) * o2[rows, :]
                z = z_ref[rows, :].astype(f32)
                a_ref[rows, :] = a
                lse_ref[rows, :] = m + jnp.log(tot)
                ag_ref[rows, :] = (a * (z * _sigmoid(z))).astype(bf16)

        if comm:
            @pl.when((pl.program_id(0) == B - 1) & (pl.program_id(1) == HEADS - 1) & (g == N_GROUPS - 1))
            def _():
                for cp in comm.copies(c_ins, c_outs, *sems):
                    cp.wait()

    def slab(col0):
        return pl.BlockSpec((S, HEAD_DIM), lambda b, h, g: (b, col0 // HEAD_DIM + g * HEADS + h))

    gain = pl.BlockSpec((N_GROUPS, HEAD_DIM), lambda b, h, g: (0, 0))
    out = pl.BlockSpec((S, HEAD_DIM), lambda b, h, g: (b, h))
    hbm = pl.BlockSpec(memory_space=pl.ANY)
    res = pl.pallas_call(
        body, grid=(B, HEADS, N_GROUPS),
        in_specs=[slab(Q0), slab(K0), slab(V0), pl.BlockSpec((S, HEAD_DIM), lambda b, h, g: (b, ZA // HEAD_DIM + h)), gain, gain]
        + [hbm] * n_ci,
        out_specs=[out, out, out] + [hbm] * n_co,
        out_shape=[jax.ShapeDtypeStruct((T, ATTN_OUT), bf16), jax.ShapeDtypeStruct((T, ATTN_OUT), f32),
                   jax.ShapeDtypeStruct((T, ATTN_OUT), f32)] + (comm.out_shapes if comm else []),
        scratch_shapes=[pltpu.VMEM((S, HEAD_DIM), f32)] * 9 + [pltpu.VMEM((BLK, BLK), f32), pltpu.VMEM((BLK, 2 * BLK), f32)]
        + (comm.scratch() if comm else []),
        compiler_params=_params(("arbitrary", "arbitrary", "arbitrary")), name="attn_fwd",
    )(proj, proj, proj, proj, gq, gk, *(comm.ins if comm else []))
    return res[0], res[1], res[2], list(res[3:])


def _rms_bwd_rows(raw, gain, dn, on_mxu=True, r=None):
    if r is None:
        r = _rstd_head(raw) if on_mxu else _rstd(raw)
    xh = raw * r
    dxh = dn * gain
    if on_mxu:
        mean = _row_sum(dxh * xh) * (1.0 / raw.shape[1])
    else:
        mean = jnp.mean(dxh * xh, axis=-1, keepdims=True)
    return r * (dxh - xh * mean), jnp.sum(dn * xh, axis=0, keepdims=True)


def _attn_bwd(proj, gq, gk, a_comb, lse, dag, dproj, B, S):
    T, NC = proj.shape
    scale = HEAD_DIM ** -0.5

    def body(q_ref, k_ref, v_ref, z_ref, gq_ref, gk_ref, a_ref, lse_ref, dag_ref, dproj_in, dproj_ref, dgq_ref, dgk_ref,
             qs, ks, da_s, dl_s, dq_s, dk_s, dv_s, rq_s, rk_s, vs, bias1, bias2, stage, dz_stage, sem, dz_sem):
        b, h, g = pl.program_id(0), pl.program_id(1), pl.program_id(2)
        bias1[...] = _band_bias(BLK)
        bias2[...] = _band_bias(2 * BLK)
        gq_row, gk_row = gq_ref[pl.ds(g, 1), :], gk_ref[pl.ds(g, 1), :]
        _normalize_into(q_ref, gq_row, qs, S, rq_s)
        _normalize_into(k_ref, gk_row, ks, S, rk_s)
        _widen_into(v_ref, vs, S)

        def dst(c):
            return dproj_ref.at[pl.ds(b * S, S), pl.ds((Q0, K0, V0)[c] + (g * HEADS + h) * HEAD_DIM, HEAD_DIM)]

        out = _Deferred(stage, sem, (b * HEADS + h) * N_GROUPS + g, B * HEADS * N_GROUPS - 1, 3)
        out.begin(dst)

        @pl.when((b == 0) & (h == 0) & (g == 0))
        def _():
            dgq_ref[...] = jnp.zeros_like(dgq_ref)
            dgk_ref[...] = jnp.zeros_like(dgk_ref)

        @pl.when(g == 0)
        def _():
            for c in range(S // 256):
                rows = pl.ds(c * 256, 256)
                z, a, dg_ = z_ref[rows, :].astype(f32), a_ref[rows, :], dag_ref[rows, :]
                sg = _sigmoid(z)
                da = dg_ * (z * sg)
                da_s[rows, :] = da
                dl_s[rows, :] = _row_sum(da * a)
                dz_stage[rows, :] = (dg_ * a * (sg * (1.0 + z * (1.0 - sg)))).astype(bf16)
            cp = pltpu.make_async_copy(dz_stage, dproj_ref.at[pl.ds(b * S, S), pl.ds(ZA + h * HEAD_DIM, HEAD_DIM)], dz_sem)
            cp.start()
            cp.wait()

        dk_s[...] = jnp.zeros_like(dk_s)
        dv_s[...] = jnp.zeros_like(dv_s)

        def run(gi):
            def unit(qr, kr, nkeys):
                q, k, v = qs[qr, :], ks[kr, :], vs[kr, :]
                s = _dot(q, k, NT_DIMS) * scale
                p = jnp.exp(s + (bias1 if nkeys == BLK else bias2)[...] - lse_ref[qr, :][:, :1])
                da = da_s[qr, :]
                dp = _dot(da, v, NT_DIMS)
                ds = p * (dp - dl_s[qr, :][:, :1]) * scale
                dq_s[qr, :] = _dot(ds, k, NN_DIMS)
                dk_s[kr, :] += _dot(ds, q, TN_DIMS)
                dv_s[kr, :] += _dot(p, da, TN_DIMS)

            _attn_units(S, DILATIONS[gi], lambda qr: unit(qr, qr, BLK), lambda qr, kr: unit(qr, kr, 2 * BLK))

        for gi in range(N_GROUPS):
            pl.when(g == gi)(functools.partial(run, gi))

        gq_acc = jnp.zeros((1, HEAD_DIM), f32)
        gk_acc = jnp.zeros((1, HEAD_DIM), f32)
        for c in range(S // 256):
            rows = pl.ds(c * 256, 256)
            dq, gq_p = _rms_bwd_rows(q_ref[rows, :].astype(f32), gq_row, dq_s[rows, :], r=rq_s[rows, :])
            dk, gk_p = _rms_bwd_rows(k_ref[rows, :].astype(f32), gk_row, dk_s[rows, :], r=rk_s[rows, :])
            gq_acc, gk_acc = gq_acc + gq_p, gk_acc + gk_p
            stage[out.slot, 0, rows, :] = dq.astype(bf16)
            stage[out.slot, 1, rows, :] = dk.astype(bf16)
            stage[out.slot, 2, rows, :] = dv_s[rows, :].astype(bf16)
        dgq_ref[pl.ds(g, 1), :] += gq_acc
        dgk_ref[pl.ds(g, 1), :] += gk_acc
        for c in range(3):
            out.start(c, dst(c))
        out.end(dst)

    def slab(col0):
        return pl.BlockSpec((S, HEAD_DIM), lambda b, h, g: (b, col0 // HEAD_DIM + g * HEADS + h))

    gain = pl.BlockSpec((N_GROUPS, HEAD_DIM), lambda b, h, g: (0, 0))
    per_slot = pl.BlockSpec((S, HEAD_DIM), lambda b, h, g: (b, h))
    return pl.pallas_call(
        body, grid=(B, HEADS, N_GROUPS),
        in_specs=[slab(Q0), slab(K0), slab(V0), pl.BlockSpec((S, HEAD_DIM), lambda b, h, g: (b, ZA // HEAD_DIM + h)), gain, gain,
                  per_slot, per_slot, per_slot, pl.BlockSpec(memory_space=pl.ANY)],
        out_specs=[pl.BlockSpec(memory_space=pl.ANY), gain, gain],
        out_shape=[jax.ShapeDtypeStruct(dproj.shape, dproj.dtype), jax.ShapeDtypeStruct((N_GROUPS, HEAD_DIM), f32),
                   jax.ShapeDtypeStruct((N_GROUPS, HEAD_DIM), f32)],
        scratch_shapes=[pltpu.VMEM((S, HEAD_DIM), f32)] * 10 + [pltpu.VMEM((BLK, BLK), f32), pltpu.VMEM((BLK, 2 * BLK), f32),
                                                                 pltpu.VMEM((2, 3, S, HEAD_DIM), bf16), pltpu.VMEM((S, HEAD_DIM), bf16),
                                                                pltpu.SemaphoreType.DMA((2, 3)), pltpu.SemaphoreType.DMA],
        input_output_aliases={9: 0},
        compiler_params=_params(("arbitrary", "arbitrary", "arbitrary")), name="attn_bwd",
    )(proj, proj, proj, proj, gq, gk, a_comb, lse, dag, dproj)


def _conv_fwd(proj, conv_w, B, S):
    T, NC = proj.shape
    tc = LANES

    def body(cb_ref, cc_ref, cv_ref, z_ref, w_ref, c_ref, ub):
        u = cc_ref[...].astype(f32) * cv_ref[...].astype(f32)
        ub[0:8, :] = jnp.zeros((8, tc), f32)
        ub[8:8 + S, :] = u
        w = w_ref[...]
        y = w[0:1] * u + w[1:2] * ub[pl.ds(7, S), :] + w[2:3] * ub[pl.ds(6, S), :]
        z = z_ref[...].astype(f32)
        c_ref[...] = (cb_ref[...].astype(f32) * y * (z * _sigmoid(z))).astype(bf16)

    def seg(col0):
        return pl.BlockSpec((S, tc), lambda j, b: (b, col0 // tc + j))

    return pl.pallas_call(
        body, grid=(CONV_W // tc, B),
        in_specs=[seg(CB), seg(CC), seg(CV), seg(ZC), pl.BlockSpec((3, tc), lambda j, b: (0, j))],
        out_specs=pl.BlockSpec((S, tc), lambda j, b: (b, j)),
        out_shape=jax.ShapeDtypeStruct((T, CONV_W), bf16),
        scratch_shapes=[pltpu.VMEM((S + 8, tc), f32)],
        compiler_params=_params(("parallel", "parallel")), name="conv_fwd",
    )(proj, proj, proj, proj, conv_w)


def _conv_bwd(proj, conv_w, dc, dproj, B, S):
    T, NC = proj.shape
    tc = LANES

    def body(cb_ref, cc_ref, cv_ref, z_ref, w_ref, dc_ref, dproj_in, dproj_ref, dw_ref, ub, db, stage, sem):
        j, b = pl.program_id(0), pl.program_id(1)

        def dst(c):
            return dproj_ref.at[pl.ds(b * S, S), pl.ds((CB, CC, CV, ZC)[c] + j * tc, tc)]

        out = _Deferred(stage, sem, j * B + b, (CONV_W // tc) * B - 1, 4)
        out.begin(dst)
        cb, cc, cv, z = (r[...].astype(f32) for r in (cb_ref, cc_ref, cv_ref, z_ref))
        dcv = dc_ref[...]
        u = cc * cv
        ub[0:8, :] = jnp.zeros((8, tc), f32)
        ub[8:8 + S, :] = u
        u1, u2 = ub[pl.ds(7, S), :], ub[pl.ds(6, S), :]
        w = w_ref[...]
        y = w[0:1] * u + w[1:2] * u1 + w[2:3] * u2
        sg = _sigmoid(z)
        si = z * sg
        dyv = dcv * cb * si
        db[0:S, :] = dyv
        db[S:S + 8, :] = jnp.zeros((8, tc), f32)
        du = w[0:1] * dyv + w[1:2] * db[pl.ds(1, S), :] + w[2:3] * db[pl.ds(2, S), :]
        stage[out.slot, 0] = (dcv * y * si).astype(bf16)
        stage[out.slot, 1] = (du * cv).astype(bf16)
        stage[out.slot, 2] = (du * cc).astype(bf16)
        stage[out.slot, 3] = (dcv * cb * y * (sg * (1.0 + z * (1.0 - sg)))).astype(bf16)
        for c in range(4):
            out.start(c, dst(c))
        part = jnp.concatenate([jnp.sum(dyv * u, axis=0, keepdims=True), jnp.sum(dyv * u1, axis=0, keepdims=True),
                                jnp.sum(dyv * u2, axis=0, keepdims=True)], axis=0)

        @pl.when(b == 0)
        def _():
            dw_ref[...] = part

        @pl.when(b > 0)
        def _():
            dw_ref[...] += part

        out.end(dst)

    def seg(col0):
        return pl.BlockSpec((S, tc), lambda j, b: (b, col0 // tc + j))

    return pl.pallas_call(
        body, grid=(CONV_W // tc, B),
        in_specs=[seg(CB), seg(CC), seg(CV), seg(ZC), pl.BlockSpec((3, tc), lambda j, b: (0, j)),
                  pl.BlockSpec((S, tc), lambda j, b: (b, j)), pl.BlockSpec(memory_space=pl.ANY)],
        out_specs=[pl.BlockSpec(memory_space=pl.ANY), pl.BlockSpec((3, tc), lambda j, b: (0, j))],
        out_shape=[jax.ShapeDtypeStruct(dproj.shape, dproj.dtype), jax.ShapeDtypeStruct((3, CONV_W), f32)],
        scratch_shapes=[pltpu.VMEM((S + 8, tc), f32), pltpu.VMEM((S + 8, tc), f32), pltpu.VMEM((2, 4, S, tc), bf16),
                        pltpu.SemaphoreType.DMA((2, 4))],
        input_output_aliases={6: 0},
        compiler_params=_params(("arbitrary", "arbitrary")), name="conv_bwd",
    )(proj, proj, proj, proj, conv_w, dc, dproj)


MEM_CHUNK = 1024


def _mem_fwd(proj, mkv, gq, gk, B, S):
    T, NC = proj.shape
    scale = MEM_HD ** -0.5

    def body(q_ref, z_ref, k_ref, v_ref, gq_ref, gk_ref, o_ref):
        kv = k_ref[...]
        kn = (kv * _rstd_head(kv) * gk_ref[...]).astype(bf16)
        vv = v_ref[...].astype(bf16)

        def chunk(c, carry):
            rows = pl.ds(pl.multiple_of(c * MEM_CHUNK, MEM_CHUNK), MEM_CHUNK)
            q = q_ref[rows, :].astype(f32)
            qn = q * _rstd(q) * gq_ref[...]
            s = _dot(qn, kn, NT_DIMS) * scale
            p = jnp.exp(s - jnp.max(s, axis=-1, keepdims=True))
            p = p / jnp.sum(p, axis=-1, keepdims=True)
            z = z_ref[rows, :].astype(f32)
            o_ref[rows, :] = (_dot(p, vv, NN_DIMS) * (z * _sigmoid(z))).astype(bf16)
            return carry

        lax.fori_loop(0, S // MEM_CHUNK, chunk, 0)

    gain = pl.BlockSpec((1, MEM_HD), lambda b, h: (0, 0))
    return pl.pallas_call(
        body, grid=(B, MEM_HEADS),
        in_specs=[pl.BlockSpec((S, MEM_HD), lambda b, h: (b, MQ // MEM_HD + h)),
                  pl.BlockSpec((S, MEM_HD), lambda b, h: (b, ZM // MEM_HD + h)),
                  pl.BlockSpec((MEM_LEN, MEM_HD), lambda b, h: (b, h)),
                  pl.BlockSpec((MEM_LEN, MEM_HD), lambda b, h: (b, MEM_HEADS + h)), gain, gain],
        out_specs=pl.BlockSpec((S, MEM_HD), lambda b, h: (b, h)),
        out_shape=jax.ShapeDtypeStruct((T, MEM_W), bf16),
        compiler_params=_params(("parallel", "parallel")), name="mem_fwd",
    )(proj, proj, mkv, mkv, gq, gk)


def _mem_bwd(proj, mkv, gq, gk, dmo, dproj, B, S):
    T, NC = proj.shape
    scale = MEM_HD ** -0.5

    def body(q_ref, z_ref, k_ref, v_ref, gq_ref, gk_ref, dmo_ref, dproj_in, dproj_ref, dmk_ref, dmv_ref, dgq_ref, dgk_ref,
             dkn_s, dv_s, gq_s, stage, sem):
        b, h = pl.program_id(0), pl.program_id(1)

        def dst(c):
            return dproj_ref.at[pl.ds(b * S, S), pl.ds((MQ, ZM)[c] + h * MEM_HD, MEM_HD)]

        out = _Deferred(stage, sem, b * MEM_HEADS + h, B * MEM_HEADS - 1, 2)
        out.begin(dst)
        kv = k_ref[...]
        kn = (kv * _rstd_head(kv) * gk_ref[...]).astype(bf16)
        vv = v_ref[...].astype(bf16)
        dkn_s[...] = jnp.zeros_like(dkn_s)
        dv_s[...] = jnp.zeros_like(dv_s)
        gq_s[...] = jnp.zeros_like(gq_s)

        def chunk(c, carry):
            rows = pl.ds(pl.multiple_of(c * MEM_CHUNK, MEM_CHUNK), MEM_CHUNK)
            q = q_ref[rows, :].astype(f32)
            qn = q * _rstd(q) * gq_ref[...]
            s = _dot(qn, kn, NT_DIMS) * scale
            p = jnp.exp(s - jnp.max(s, axis=-1, keepdims=True))
            p = p / jnp.sum(p, axis=-1, keepdims=True)
            mo = _dot(p, vv, NN_DIMS)
            z, dg_ = z_ref[rows, :].astype(f32), dmo_ref[rows, :]
            sg = _sigmoid(z)
            do = dg_ * (z * sg)
            stage[out.slot, 1, rows, :] = (dg_ * mo * (sg * (1.0 + z * (1.0 - sg)))).astype(bf16)
            dp = _dot(do, vv, NT_DIMS)
            ds = p * (dp - jnp.sum(do * mo, axis=-1, keepdims=True)) * scale
            dq, gq_p = _rms_bwd_rows(q, gq_ref[...], _dot(ds, kn, NN_DIMS), on_mxu=False)
            stage[out.slot, 0, rows, :] = dq.astype(bf16)
            gq_s[...] += gq_p
            dkn_s[...] += _dot(ds, qn, TN_DIMS)
            dv_s[...] += _dot(p, do, TN_DIMS)
            return carry

        lax.fori_loop(0, S // MEM_CHUNK, chunk, 0)
        for c in range(2):
            out.start(c, dst(c))
        dk, gk_p = _rms_bwd_rows(kv, gk_ref[...], dkn_s[...])
        dmk_ref[...] = dk
        dmv_ref[...] = dv_s[...]

        @pl.when((b == 0) & (h == 0))
        def _():
            dgq_ref[...] = gq_s[...]
            dgk_ref[...] = gk_p

        @pl.when((b > 0) | (h > 0))
        def _():
            dgq_ref[...] += gq_s[...]
            dgk_ref[...] += gk_p

        out.end(dst)

    gain = pl.BlockSpec((1, MEM_HD), lambda b, h: (0, 0))
    kvb = pl.BlockSpec((MEM_LEN, MEM_HD), lambda b, h: (b, h))
    return pl.pallas_call(
        body, grid=(B, MEM_HEADS),
        in_specs=[pl.BlockSpec((S, MEM_HD), lambda b, h: (b, MQ // MEM_HD + h)),
                  pl.BlockSpec((S, MEM_HD), lambda b, h: (b, ZM // MEM_HD + h)),
                  kvb, pl.BlockSpec((MEM_LEN, MEM_HD), lambda b, h: (b, MEM_HEADS + h)), gain, gain,
                  pl.BlockSpec((S, MEM_HD), lambda b, h: (b, h)), pl.BlockSpec(memory_space=pl.ANY)],
        out_specs=[pl.BlockSpec(memory_space=pl.ANY), kvb, kvb, gain, gain],
        out_shape=[jax.ShapeDtypeStruct(dproj.shape, dproj.dtype), jax.ShapeDtypeStruct((B * MEM_LEN, MEM_W), f32),
                   jax.ShapeDtypeStruct((B * MEM_LEN, MEM_W), f32), jax.ShapeDtypeStruct((1, MEM_HD), f32),
                   jax.ShapeDtypeStruct((1, MEM_HD), f32)],
        scratch_shapes=[pltpu.VMEM((MEM_LEN, MEM_HD), f32), pltpu.VMEM((MEM_LEN, MEM_HD), f32), pltpu.VMEM((1, MEM_HD), f32),
                        pltpu.VMEM((2, 2, S, MEM_HD), bf16), pltpu.SemaphoreType.DMA((2, 2))],
        input_output_aliases={7: 0},
        compiler_params=_params(("arbitrary", "arbitrary")), name="mem_bwd",
    )(proj, proj, mkv, mkv, gq, gk, dmo, dproj)


def _merge_fwd(proj, ag, cg, mg, wa, wc, wm):
    T, NC = proj.shape
    D = wa.shape[1]
    tm, tn = _pick(T, 1024), _pick(D, 512)
    assert GT % tn == 0

    def body(ag_ref, cg_ref, mg_ref, wa_ref, wc_ref, wm_ref, g0_ref, g1_ref, g2_ref, mer_ref, ba_ref, bc_ref, bm_ref):
        ba = _dot(ag_ref[...], wa_ref[...], NN_DIMS)
        bc = _dot(cg_ref[...], wc_ref[...], NN_DIMS)
        bm = _dot(mg_ref[...], wm_ref[...], NN_DIMS)
        s0, s1, s2 = (_sigmoid(r[...].astype(f32)) for r in (g0_ref, g1_ref, g2_ref))
        mer_ref[...] = (s0 * ba + s1 * bc + s2 * bm).astype(bf16)
        ba_ref[...] = ba.astype(bf16)
        bc_ref[...] = bc.astype(bf16)
        bm_ref[...] = bm.astype(bf16)

    def act(w):
        return pl.BlockSpec((tm, w), lambda i, j: (i, 0))

    def wt(k):
        return pl.BlockSpec((k, tn), lambda i, j: (0, j))

    def gate(n):
        return pl.BlockSpec((tm, tn), lambda i, j: (i, (GT + n * D) // tn + j))

    out = pl.BlockSpec((tm, tn), lambda i, j: (i, j))
    return pl.pallas_call(
        body, grid=(T // tm, D // tn),
        in_specs=[act(ATTN_OUT), act(CONV_W), act(MEM_W), wt(ATTN_OUT), wt(CONV_W), wt(MEM_W), gate(0), gate(1), gate(2)],
        out_specs=[out] * 4, out_shape=[jax.ShapeDtypeStruct((T, D), bf16)] * 4,
        compiler_params=_params(("parallel", "parallel")), name="merge_fwd",
    )(ag, cg, mg, wa, wc, wm, proj, proj, proj)


def _out_loss(merged, w_out, x, tgt):
    T, D = x.shape
    tm, tn = _pick(T, 1024), _pick(D, 512)

    def body(m_ref, w_ref, x_ref, t_ref, dy_ref, dyb_ref, lp_ref):
        e = x_ref[...] + _dot(m_ref[...], w_ref[...], NN_DIMS) - t_ref[...]
        dy = e / D
        dy_ref[...] = dy
        dyb_ref[...] = dy.astype(bf16)
        part = jnp.full((1, 8, LANES), jnp.sum(e * e), f32)

        @pl.when(pl.program_id(1) == 0)
        def _():
            lp_ref[...] = part

        @pl.when(pl.program_id(1) > 0)
        def _():
            lp_ref[...] += part

    tile = pl.BlockSpec((tm, tn), lambda i, j: (i, j))
    return pl.pallas_call(
        body, grid=(T // tm, D // tn),
        in_specs=[pl.BlockSpec((tm, D), lambda i, j: (i, 0)), pl.BlockSpec((D, tn), lambda i, j: (0, j)), tile, tile],
        out_specs=[tile, tile, pl.BlockSpec((1, 8, LANES), lambda i, j: (i, 0, 0))],
        out_shape=[jax.ShapeDtypeStruct((T, D), f32), jax.ShapeDtypeStruct((T, D), bf16),
                   jax.ShapeDtypeStruct((T // tm, 8, LANES), f32)],
        compiler_params=_params(("parallel", "arbitrary")), name="out_loss",
    )(merged, w_out, x, tgt)


def _merge_bwd(proj, dyb, w_out, ba, bc, bm):
    T, NC = proj.shape
    D = w_out.shape[0]
    tm, tn = _pick(T, 1024), _pick(D, 512)
    assert GT % tn == 0

    def body(dy_ref, w_ref, ba_ref, bc_ref, bm_ref, g0_ref, g1_ref, g2_ref, da_ref, dc_ref, dm_ref, dproj_ref, stage, sem):
        i, j = pl.program_id(0), pl.program_id(1)

        def dst(n):
            return dproj_ref.at[pl.ds(i * tm, tm), pl.ds(GT + n * D + j * tn, tn)]

        out = _Deferred(stage, sem, i * (D // tn) + j, (T // tm) * (D // tn) - 1, 3)
        out.begin(dst)
        dmer = _dot(dy_ref[...], w_ref[...], NT_DIMS)
        for n, (g_ref, br_ref, o_ref) in enumerate(((g0_ref, ba_ref, da_ref), (g1_ref, bc_ref, dc_ref), (g2_ref, bm_ref, dm_ref))):
            sg = _sigmoid(g_ref[...].astype(f32))
            o_ref[...] = (sg * dmer).astype(bf16)
            stage[out.slot, n] = (dmer * br_ref[...].astype(f32) * (sg * (1.0 - sg))).astype(bf16)
            out.start(n, dst(n))
        out.end(dst)

    tile = pl.BlockSpec((tm, tn), lambda i, j: (i, j))

    def gate(n):
        return pl.BlockSpec((tm, tn), lambda i, j: (i, (GT + n * D) // tn + j))

    return pl.pallas_call(
        body, grid=(T // tm, D // tn),
        in_specs=[pl.BlockSpec((tm, D), lambda i, j: (i, 0)), pl.BlockSpec((tn, D), lambda i, j: (j, 0)), tile, tile, tile,
                  gate(0), gate(1), gate(2)],
        out_specs=[tile, tile, tile, pl.BlockSpec(memory_space=pl.ANY)],
        out_shape=[jax.ShapeDtypeStruct((T, D), bf16)] * 3 + [jax.ShapeDtypeStruct((T, NC), bf16)],
        scratch_shapes=[pltpu.VMEM((2, 3, tm, tn), bf16), pltpu.SemaphoreType.DMA((2, 3))],
        compiler_params=_params(("arbitrary", "arbitrary")), name="merge_bwd",
    )(dyb, w_out, ba, bc, bm, proj, proj, proj)


def _fwd_bwd_head(x2, mem2, t2, proj, attn, B, S, mem_norm_g, attn_q_norm, attn_k_norm, conv_w, mem_q_norm, mem_k_norm,
                  w_kv, w_a, w_c, w_m, w_out):
    D = x2.shape[1]
    mng = mem_norm_g.reshape(1, D)
    mqn, mkn = mem_q_norm.reshape(1, MEM_HD), mem_k_norm.reshape(1, MEM_HD)
    ag, a_comb, lse = attn

    mh = _rms_fwd(mem2, mng, "rms_mem")
    mkv = _mm(mh, w_kv, mode="nn", out_dtype=f32, name="mkv")
    cg = _conv_fwd(proj, conv_w, B, S)
    mg = _mem_fwd(proj, mkv, mqn, mkn, B, S)
    merged, ba, bc, bm = _merge_fwd(proj, ag, cg, mg, w_a, w_c, w_m)
    dy, dyb, lp = _out_loss(merged, w_out, x2, t2)
    sq_err = jnp.sum(lp[:, 0, 0])

    g = {}
    g["w_out"] = _mm(merged, dyb, mode="tn", out_dtype=DW_DTYPE, name="dw_out")
    dba, dbc, dbm, dproj = _merge_bwd(proj, dyb, w_out, ba, bc, bm)
    g["w_br_attn"] = _mm(ag, dba, mode="tn", out_dtype=DW_DTYPE, name="dw_br_attn")
    g["w_br_conv"] = _mm(cg, dbc, mode="tn", out_dtype=DW_DTYPE, name="dw_br_conv")
    g["w_br_mem"] = _mm(mg, dbm, mode="tn", out_dtype=DW_DTYPE, name="dw_br_mem")
    dag = _mm(dba, w_a, mode="nt", out_dtype=f32, name="d_attn_out")
    dcg = _mm(dbc, w_c, mode="nt", out_dtype=f32, name="d_conv_out")
    dmg = _mm(dbm, w_m, mode="nt", out_dtype=f32, name="d_mem_out")
    dproj, g["attn_q_norm"], g["attn_k_norm"] = _attn_bwd(proj, attn_q_norm, attn_k_norm, a_comb, lse, dag, dproj, B, S)
    dproj, g["conv_w"] = _conv_bwd(proj, conv_w, dcg, dproj, B, S)
    dproj, dmk, dmv, dgmq, dgmk = _mem_bwd(proj, mkv, mqn, mkn, dmg, dproj, B, S)
    g["mem_q_norm"], g["mem_k_norm"] = dgmq.reshape(MEM_HD), dgmk.reshape(MEM_HD)
    dmkv = jnp.concatenate([dmk, dmv], axis=1)
    dmh = _mm(dmkv, w_kv, mode="nt", out_dtype=f32, name="d_mem_h")
    g["mem_norm_g"] = _rms_bwd(mem2, mng, dmh, None, "rms_mem_bwd").reshape(D)
    return sq_err, g, dict(dy=dy, dproj=dproj, mh=mh, dmkv=dmkv)


MESH = pl.DeviceIdType.MESH
HBM = pl.BlockSpec(memory_space=pl.ANY)


def _me():
    x, y, c = lax.axis_index("x"), lax.axis_index("y"), lax.axis_index("c")
    return (x, y, c), 4 * x + 2 * y + c


def _peer(k):
    (x, y, c), _ = _me()
    p = (1 - x if k & 4 else x, 1 - y if k & 2 else y, 1 - c if k & 1 else c)
    return p, 4 * p[0] + 2 * p[1] + p[2]


def _window(ref, axis, size, idx):
    if axis is None:
        return ref.at[idx]
    if axis == 0:
        return ref.at[pl.ds(idx * size, size), :]
    return ref.at[:, pl.ds(idx * size, size)]


def _full_shape(s, axis):
    if axis is None:
        return (N_DEV,) + s.shape
    return tuple(N_DEV * d if i == axis else d for i, d in enumerate(s.shape))


OTHER_CHIPS = (4, 2, 6)


def _spread_comm(shards, axes):
    n = len(shards)

    def copies(ins, outs, send_sems, recv_sems, base=0):
        _, me = _me()
        out = []
        for a in range(n):
            mine = _window(outs[a], axes[a], None if axes[a] is None else shards[a].shape[axes[a]], me)
            out.append(pltpu.make_async_copy(ins[a], mine, send_sems.at[base + a, N_CHIP]))
            for k, dist in enumerate((1,) + OTHER_CHIPS):
                dev, _ = _peer(dist)
                out.append(pltpu.make_async_remote_copy(
                    src_ref=ins[a], dst_ref=mine, send_sem=send_sems.at[base + a, k], recv_sem=recv_sems.at[base + a, k],
                    device_id=dev, device_id_type=MESH))
        return out

    shapes = [jax.ShapeDtypeStruct(_full_shape(s, ax), s.dtype) for s, ax in zip(shards, axes)]
    return _Comm(shards, shapes, (n, N_CHIP + 1), copies)


def _forward_blocks(fulls, axes):
    n = len(fulls)
    sizes = [None if ax is None else f.shape[ax] // N_DEV for f, ax in zip(fulls, axes)]

    def body(*refs):
        outs = refs[n:2 * n]
        send_sems, recv_sems = refs[2 * n:]
        sibling, _ = _peer(1)
        copies = []
        for j, dist in enumerate(OTHER_CHIPS):
            _, held = _peer(dist)
            for a in range(n):
                block = _window(outs[a], axes[a], sizes[a], held)
                copies.append(pltpu.make_async_remote_copy(
                    src_ref=block, dst_ref=block, send_sem=send_sems.at[a, j], recv_sem=recv_sems.at[a, j],
                    device_id=sibling, device_id_type=MESH))
        for cp in copies:
            cp.start()
        for cp in copies:
            cp.wait()

    return pl.pallas_call(
        body, in_specs=[HBM] * n, out_specs=[HBM] * n,
        out_shape=[jax.ShapeDtypeStruct(f.shape, f.dtype) for f in fulls],
        scratch_shapes=[pltpu.SemaphoreType.DMA((n, len(OTHER_CHIPS))), pltpu.SemaphoreType.DMA((n, len(OTHER_CHIPS)))],
        input_output_aliases={a: a for a in range(n)},
        name="forward_blocks",
    )(*fulls)


def _shard_order():
    (x, y, c), me = _me()
    xs, ys, xo, yo = _peer(4)[1], _peer(2)[1], _peer(5)[1], _peer(3)[1]
    north = c == 1
    ids = [me, _peer(1)[1], jnp.where(north, xs, ys), jnp.where(north, yo, xo), jnp.where(north, ys, xs),
           jnp.where(north, xo, yo), _peer(6)[1], _peer(7)[1]]
    return jnp.stack(ids).astype(jnp.int32)


def _proj_gather(h, w_shard, order, comm, comm_at):
    T, K = h.shape
    C = w_shard.shape[1]
    tm = _pick(T, 1024)
    nm = T // tm
    ahead = max(nm - 2, 0)
    n_ci, n_co = len(comm.ins), len(comm.out_shapes)

    def body(*refs):
        order_ref, h_ref, ws_ref = refs[:3]
        c_ins = refs[3:3 + n_ci]
        o_ref, wf_ref = refs[3 + n_ci:5 + n_ci]
        c_outs = refs[5 + n_ci:5 + n_ci + n_co]
        wbuf, send_sems, recv_sems, own_sem, buf_sems, c_send, c_recv = refs[5 + n_ci + n_co:]
        t, i = pl.program_id(0), pl.program_id(1)

        @pl.when((t == comm_at) & (i == 0))
        def _():
            for cp in comm.copies(c_ins, c_outs, c_send, c_recv):
                cp.start()
        (x, y, c), me = _me()
        sibling, sib_id = _peer(1)
        chips = [_peer(dist) for dist in OTHER_CHIPS]

        def win(idx):
            return wf_ref.at[:, pl.ds(idx * C, C)]

        def copy(k, block, to, src=None):
            return pltpu.make_async_remote_copy(
                src_ref=win(block) if src is None else src, dst_ref=win(block),
                send_sem=send_sems.at[k], recv_sem=recv_sems.at[k], device_id=to, device_id_type=MESH)

        def fetch(tt, src):
            return pltpu.make_async_copy(src, wbuf.at[tt % 2], buf_sems.at[tt % 2])

        own = pltpu.make_async_copy(ws_ref, win(me), own_sem)
        (x_nbr, x_id), (y_nbr, y_id), (_, d_id) = chips

        def half(k, block, top, to):
            rows = wf_ref.at[pl.ds(0 if top else K // 2, K // 2), pl.ds(block * C, C)]
            return pltpu.make_async_remote_copy(src_ref=rows, dst_ref=rows, send_sem=send_sems.at[k], recv_sem=recv_sems.at[k],
                                                device_id=to, device_id_type=MESH)

        here = (x, y, c)

        def pass_on(from_x):
            if from_x:
                copy(4, x_id, sibling).start()
                half(8, x_id, False, y_nbr).start()
            else:
                copy(5, y_id, sibling).start()
                half(7, y_id, True, x_nbr).start()

        @pl.when((t == 0) & (i == 0))
        def _():
            fetch(0, ws_ref).start()
            own.start()
            copy(0, me, sibling, src=ws_ref).start()

        for tt in range(N_DEV):
            @pl.when((t == tt) & (i == 0))
            def _(tt=tt):
                fetch(tt, ws_ref).wait()

        o_ref[...] = _dot(h_ref[...], wbuf[t % 2], NN_DIMS).astype(o_ref.dtype)

        def schedule(first_x):
            on = c == (1 if first_x else 0)
            (f_dev, f_id), (g_dev, g_id) = ((x_nbr, x_id), (y_nbr, y_id)) if first_x else ((y_nbr, y_id), (x_nbr, x_id))
            kf, kg = (1, 2) if first_x else (2, 1)
            pf, pg = (4, 5) if first_x else (5, 4)

            @pl.when((t == 0) & (i == 0) & on)
            def _():
                copy(kf, me, f_dev, src=ws_ref).start()

            def event(nxt):
                if nxt == 1:
                    copy(0, sib_id, here).wait_recv()
                    return sib_id
                if nxt == 2:
                    copy(kf, f_id, here).wait_recv()
                    copy(kf, me, f_dev, src=ws_ref).wait_send()
                    copy(kg, me, g_dev, src=ws_ref).start()
                    pass_on(first_x)
                    return f_id
                if nxt == 3:
                    copy(pg, g_id + 1 - 2 * c, here).wait_recv()
                    return g_id + 1 - 2 * c
                if nxt == 4:
                    copy(kg, g_id, here).wait_recv()
                    pass_on(not first_x)
                    return g_id
                if nxt == 5:
                    copy(pf, f_id + 1 - 2 * c, here).wait_recv()
                    return f_id + 1 - 2 * c
                if nxt == 6:
                    half(7, d_id, True, here).wait_recv()
                    half(8, d_id, False, here).wait_recv()
                    copy(6, d_id, sibling).start()
                    return d_id
                copy(6, d_id + 1 - 2 * c, here).wait_recv()
                return d_id + 1 - 2 * c

            for tt in range(N_DEV - 1):
                @pl.when((t == tt) & (i == ahead) & on)
                def _(tt=tt):
                    fetch(tt + 1, win(event(tt + 1))).start()

            @pl.when((t == N_DEV - 1) & (i == nm - 1) & on)
            def _():
                copy(kg, me, g_dev, src=ws_ref).wait_send()

        schedule(True)
        schedule(False)

        @pl.when((t == N_DEV - 1) & (i == nm - 1))
        def _():
            copy(0, me, sibling, src=ws_ref).wait_send()
            half(7, y_id, True, x_nbr).wait_send()
            half(8, x_id, False, y_nbr).wait_send()
            for j, (_, dev_id) in enumerate(chips):
                copy(4 + j, dev_id, sibling).wait_send()
            own.wait()
            for cp in comm.copies(c_ins, c_outs, c_send, c_recv):
                cp.wait()

    hbm = pl.BlockSpec(memory_space=pl.ANY)
    res = pl.pallas_call(
        body,
        grid_spec=pltpu.PrefetchScalarGridSpec(
            num_scalar_prefetch=1, grid=(N_DEV, nm),
            in_specs=[pl.BlockSpec((tm, K), lambda t, i, order_ref: (i, 0)), hbm] + [hbm] * n_ci,
            out_specs=[pl.BlockSpec((tm, C), lambda t, i, order_ref: (i, order_ref[t])), hbm] + [hbm] * n_co,
            scratch_shapes=[pltpu.VMEM((2, K, C), bf16), pltpu.SemaphoreType.DMA((9,)), pltpu.SemaphoreType.DMA((9,)),
                            pltpu.SemaphoreType.DMA, pltpu.SemaphoreType.DMA((2,))] + comm.scratch()),
        out_shape=[jax.ShapeDtypeStruct((T, N_DEV * C), PROJ_DTYPE), jax.ShapeDtypeStruct((K, N_DEV * C), bf16)] + comm.out_shapes,
        compiler_params=_params(("arbitrary", "arbitrary")), name="proj_gather",
    )(order, h, w_shard, *comm.ins)
    return res[0], res[1], list(res[2:])


N_CHIP = 4


def _shard_shape(g, ax):
    return tuple(d // N_DEV if i == ax else d for i, d in enumerate(g.shape))


def _sibling_comm(grads, axes):
    n = len(grads)
    sizes = [g.shape[ax] // N_DEV for g, ax in zip(grads, axes)]

    def copies(ins, lands, send_sems, recv_sems, base=0):
        sibling, _ = _peer(1)
        out = []
        for j in range(N_CHIP):
            _, owner = _peer(2 * j + 1)
            for a in range(n):
                out.append(pltpu.make_async_remote_copy(
                    src_ref=_window(ins[a], axes[a], sizes[a], owner), dst_ref=lands[a].at[j],
                    send_sem=send_sems.at[base + a, j], recv_sem=recv_sems.at[base + a, j], device_id=sibling,
                    device_id_type=MESH))
        return out

    shapes = [jax.ShapeDtypeStruct((N_CHIP,) + _shard_shape(g, ax), g.dtype) for g, ax in zip(grads, axes)]
    return _Comm(grads, shapes, (n, N_CHIP), copies)


def _chips_comm(sums):
    n = len(sums)

    def copies(ins, lands, send_sems, recv_sems, base=0):
        out = []
        for j in range(1, N_CHIP):
            dev, _ = _peer(2 * j)
            for a in range(n):
                out.append(pltpu.make_async_remote_copy(
                    src_ref=ins[a].at[j - 1], dst_ref=lands[a].at[j - 1],
                    send_sem=send_sems.at[base + a, j - 1], recv_sem=recv_sems.at[base + a, j - 1], device_id=dev,
                    device_id_type=MESH))
        return out

    return _Comm(sums, [jax.ShapeDtypeStruct(s.shape, s.dtype) for s in sums], (n, N_CHIP), copies)


def _join(c1, c2):
    n1, m1, k1 = len(c1.ins), len(c1.out_shapes), c1.sem_shape[0]

    def copies(ins, lands, send_sems, recv_sems):
        return (c1.copies(ins[:n1], lands[:m1], send_sems, recv_sems, base=0)
                + c2.copies(ins[n1:], lands[m1:], send_sems, recv_sems, base=k1))

    aliases = {**c1.aliases, **{n1 + k: m1 + v for k, v in c2.aliases.items()}}
    return _Comm(c1.ins + c2.ins, c1.out_shapes + c2.out_shapes, (k1 + c2.sem_shape[0], N_CHIP), copies, aliases)


def _chips_first_hop(sums):
    n = len(sums)

    def copies(ins, outs, send_sems, recv_sems, base=0):
        lands, relays = outs[:n], outs[n:]
        (y_dev, _), (x_dev, _) = _peer(2), _peer(4)
        out = []
        for a in range(n):
            half = sums[a].shape[1] // 2
            for k, (src, dst, dev) in enumerate((
                    (ins[a].at[0], lands[a].at[0], y_dev), (ins[a].at[1], lands[a].at[1], x_dev),
                    (ins[a].at[2, pl.ds(0, half)], relays[a].at[0], x_dev),
                    (ins[a].at[2, pl.ds(half, half)], relays[a].at[1], y_dev))):
                out.append(pltpu.make_async_remote_copy(
                    src_ref=src, dst_ref=dst, send_sem=send_sems.at[base + a, k], recv_sem=recv_sems.at[base + a, k],
                    device_id=dev, device_id_type=MESH))
        return out

    shapes = ([jax.ShapeDtypeStruct(s.shape, s.dtype) for s in sums]
              + [jax.ShapeDtypeStruct((2, s.shape[1] // 2) + s.shape[2:], s.dtype) for s in sums])
    return _Comm(sums, shapes, (n, N_CHIP), copies)


def _chips_relay(relays, lands):
    n = len(relays)

    def copies(ins, outs, send_sems, recv_sems, base=0):
        (y_dev, _), (x_dev, _) = _peer(2), _peer(4)
        out = []
        for a in range(n):
            half = relays[a].shape[1]
            for k, (rows, dev) in enumerate(((pl.ds(0, half), y_dev), (pl.ds(half, half), x_dev))):
                out.append(pltpu.make_async_remote_copy(
                    src_ref=ins[a].at[k], dst_ref=outs[a].at[2, rows], send_sem=send_sems.at[base + a, k],
                    recv_sem=recv_sems.at[base + a, k], device_id=dev, device_id_type=MESH))
        return out

    return _Comm(list(relays) + list(lands), [jax.ShapeDtypeStruct(l.shape, l.dtype) for l in lands], (n, N_CHIP), copies,
                 aliases={n + a: a for a in range(n)})


def _rs_chip_sum(grad, land, xyc, axis, name):
    R, C = land.shape[1:]
    tr = _pick(R, max(8, (640 * 1024) // C), 8)

    def body(xyc_ref, g_ref, l_ref, o_ref):
        o_ref[0] = (g_ref[...].astype(f32) + l_ref[0].astype(f32)).astype(bf16)

    def owner(j, xyc_ref):
        jj = j + 1
        return 4 * (xyc_ref[0] ^ (jj >> 1)) + 2 * (xyc_ref[1] ^ (jj & 1)) + xyc_ref[2]

    if axis == 0:
        g_spec = pl.BlockSpec((tr, C), lambda j, i, xyc_ref: (owner(j, xyc_ref) * (R // tr) + i, 0))
    else:
        g_spec = pl.BlockSpec((tr, C), lambda j, i, xyc_ref: (i, owner(j, xyc_ref)))
    return pl.pallas_call(
        body,
        grid_spec=pltpu.PrefetchScalarGridSpec(
            num_scalar_prefetch=1, grid=(N_CHIP - 1, R // tr),
            in_specs=[g_spec, pl.BlockSpec((1, tr, C), lambda j, i, xyc_ref: (j + 1, i, 0))],
            out_specs=pl.BlockSpec((1, tr, C), lambda j, i, xyc_ref: (j, i, 0))),
        out_shape=jax.ShapeDtypeStruct((N_CHIP - 1, R, C), bf16),
        compiler_params=_params(("parallel", "parallel")), name=name,
    )(xyc, grad, land)


def _allreduce_small(part):
    R = part.shape[0]

    def body(p_ref, o_ref, buf, send_sems, recv_sems):
        (x, y, c), me = _me()
        buf[me] = p_ref[...]
        copies = []
        for k in range(1, N_DEV):
            dev, dev_id = _peer(k)
            copies.append(pltpu.make_async_remote_copy(
                src_ref=p_ref, dst_ref=buf.at[me], send_sem=send_sems.at[k - 1], recv_sem=recv_sems.at[k - 1],
                device_id=dev, device_id_type=MESH))
        for cp in copies:
            cp.start()
        for k in range(1, N_DEV):
            _, dev_id = _peer(k)
            pltpu.make_async_remote_copy(
                src_ref=p_ref, dst_ref=buf.at[dev_id], send_sem=send_sems.at[k - 1], recv_sem=recv_sems.at[k - 1],
                device_id=(x, y, c), device_id_type=MESH).wait_recv()
        for cp in copies:
            cp.wait_send()
        acc = buf[0]
        for d in range(1, N_DEV):
            acc = acc + buf[d]
        o_ref[...] = acc

    return pl.pallas_call(
        body, in_specs=[pl.BlockSpec(memory_space=pltpu.VMEM)], out_specs=pl.BlockSpec(memory_space=pltpu.VMEM),
        out_shape=jax.ShapeDtypeStruct((R, LANES), f32),
        scratch_shapes=[pltpu.VMEM((N_DEV, R, LANES), f32), pltpu.SemaphoreType.DMA((N_DEV - 1,)), pltpu.SemaphoreType.DMA((N_DEV - 1,))],
        name="allreduce_small",
    )(part)


def _adamw_math(w, g, m, v):
    m2 = ADAM_B1 * m + (1.0 - ADAM_B1) * g
    v2 = ADAM_B2 * v + (1.0 - ADAM_B2) * (g * g)
    m_hat = m2 / (1.0 - ADAM_B1 ** ADAM_STEP)
    v_hat = v2 / (1.0 - ADAM_B2 ** ADAM_STEP)
    return -ADAM_LR * (m_hat / (jnp.sqrt(v_hat) + ADAM_EPS) + ADAM_WD * w), m2, v2


def _adamw_shard(grad, land_sib, land_chips, w, m, v, me, axis, name, row0=0, prev=None):
    R, C = land_sib.shape[1:]
    assert axis == 1 or R == w.shape[0]
    tr = _pick(R, max(8, (320 * 1024) // C), 8)
    r0 = row0 // tr
    n_prev = len(prev) if prev else 0

    def body(me_ref, g_ref, s_ref, l_ref, w_ref, m_ref, v_ref, *rest):
        go_ref, d_ref, mo_ref, vo_ref = rest[n_prev:]
        g = g_ref[...].astype(f32) + s_ref[0].astype(f32)
        for j in range(N_CHIP - 1):
            g = g + l_ref[j].astype(f32)
        d, m2, v2 = _adamw_math(w_ref[...], g, m_ref[...], v_ref[...])
        go_ref[...] = g
        d_ref[...] = d
        mo_ref[...] = m2
        vo_ref[...] = v2

    if axis == 0:
        g_spec = pl.BlockSpec((tr, C), lambda i, me_ref: (me_ref[0] * (R // tr) + i, 0))
    else:
        g_spec = pl.BlockSpec((tr, C), lambda i, me_ref: (i, me_ref[0]))
    blk = pl.BlockSpec((tr, C), lambda i, me_ref: (r0 + i, 0))
    return pl.pallas_call(
        body,
        grid_spec=pltpu.PrefetchScalarGridSpec(
            num_scalar_prefetch=1, grid=(R // tr,),
            in_specs=[g_spec, pl.BlockSpec((1, tr, C), lambda i, me_ref: (0, i, 0)),
                      pl.BlockSpec((N_CHIP - 1, tr, C), lambda i, me_ref: (0, i, 0)), blk, blk, blk]
            + [pl.BlockSpec(memory_space=pl.ANY)] * n_prev,
            out_specs=[blk] * 4),
        out_shape=[jax.ShapeDtypeStruct(w.shape, f32)] * 4,
        input_output_aliases={7 + i: i for i in range(n_prev)},
        compiler_params=_params(("parallel",)), name=name,
    )(me, grad, land_sib, land_chips, w, m, v, *(prev or []))


def _adamw_small(g, w, m, v):
    def body(g_ref, w_ref, m_ref, v_ref, d_ref, mo_ref, vo_ref):
        d, m2, v2 = _adamw_math(w_ref[...], g_ref[...], m_ref[...], v_ref[...])
        d_ref[...] = d
        mo_ref[...] = m2
        vo_ref[...] = v2

    return pl.pallas_call(body, out_shape=[jax.ShapeDtypeStruct(g.shape, f32)] * 3, name="adamw_small")(g, w, m, v)


def _pack_rows(parts, total_rows):
    rows = jnp.concatenate([p.reshape(-1, LANES) for p in parts], axis=0)
    return jnp.pad(rows, ((0, total_rows - rows.shape[0]), (0, 0)))


BIG = ("w_in", "mem_w_kv", "w_br_attn", "w_br_conv", "w_br_mem", "w_out")
BIG_AXIS = {"w_in": 1, "mem_w_kv": 0, "w_br_attn": 1, "w_br_conv": 1, "w_br_mem": 1, "w_out": 0}
SMALL = ("norm_g", "mem_norm_g", "attn_q_norm", "attn_k_norm", "mem_q_norm", "mem_k_norm")
ALL_W = ("norm_g", "mem_norm_g", "w_in", "attn_q_norm", "attn_k_norm", "conv_w", "mem_w_kv", "mem_q_norm", "mem_k_norm",
         "w_br_attn", "w_br_conv", "w_br_mem", "w_out")


def kernel(x, mem, norm_g, mem_norm_g, w_in, attn_q_norm, attn_k_norm, conv_w, mem_w_kv, mem_q_norm, mem_k_norm, w_br_attn, w_br_conv, w_br_mem, w_out, loss_target, m_norm_g, m_mem_norm_g, m_w_in, m_attn_q_norm, m_attn_k_norm, m_conv_w, m_mem_w_kv, m_mem_q_norm, m_mem_k_norm, m_w_br_attn, m_w_br_conv, m_w_br_mem, m_w_out, v_norm_g, v_mem_norm_g, v_w_in, v_attn_q_norm, v_attn_k_norm, v_conv_w, v_mem_w_kv, v_mem_q_norm, v_mem_k_norm, v_w_br_attn, v_w_br_conv, v_w_br_mem, v_w_out):
    w = dict(norm_g=norm_g, mem_norm_g=mem_norm_g, w_in=w_in, attn_q_norm=attn_q_norm, attn_k_norm=attn_k_norm, conv_w=conv_w,
             mem_w_kv=mem_w_kv, mem_q_norm=mem_q_norm, mem_k_norm=mem_k_norm, w_br_attn=w_br_attn, w_br_conv=w_br_conv,
             w_br_mem=w_br_mem, w_out=w_out)
    mo = dict(norm_g=m_norm_g, mem_norm_g=m_mem_norm_g, w_in=m_w_in, attn_q_norm=m_attn_q_norm, attn_k_norm=m_attn_k_norm,
              conv_w=m_conv_w, mem_w_kv=m_mem_w_kv, mem_q_norm=m_mem_q_norm, mem_k_norm=m_mem_k_norm, w_br_attn=m_w_br_attn,
              w_br_conv=m_w_br_conv, w_br_mem=m_w_br_mem, w_out=m_w_out)
    vo = dict(norm_g=v_norm_g, mem_norm_g=v_mem_norm_g, w_in=v_w_in, attn_q_norm=v_attn_q_norm, attn_k_norm=v_attn_k_norm,
              conv_w=v_conv_w, mem_w_kv=v_mem_w_kv, mem_q_norm=v_mem_q_norm, mem_k_norm=v_mem_k_norm, w_br_attn=v_w_br_attn,
              w_br_conv=v_w_br_conv, w_br_mem=v_w_br_mem, w_out=v_w_out)
    B, S, D = x.shape
    me = (4 * lax.axis_index("x") + 2 * lax.axis_index("y") + lax.axis_index("c")).astype(jnp.int32)

    T = B * S
    x2, t2, mem2, ng = x.reshape(T, D), loss_target.reshape(T, D), mem.reshape(B * MEM_LEN, D), norm_g.reshape(1, D)
    h = _rms_fwd(x2, ng, "rms_x")
    with_proj, with_attn = ("mem_w_kv",), ("w_out", "w_br_attn", "w_br_conv", "w_br_mem")
    later = with_proj + with_attn
    later_axes = [BIG_AXIS[n] for n in later] + [None]
    conv_pad = jnp.pad(conv_w, ((0, 8 - conv_w.shape[0]), (0, 0)))
    proj, w_in_full, spread_a = _proj_gather(
        h, w_in.astype(bf16), _shard_order(),
        _spread_comm([w[n].astype(bf16) for n in with_proj], [BIG_AXIS[n] for n in with_proj]), comm_at=N_DEV - 3)
    *attn, spread_b = _attn_fwd(proj, attn_q_norm, attn_k_norm, B, S,
                                comm=_spread_comm([w[n].astype(bf16) for n in with_attn] + [conv_pad],
                                                  [BIG_AXIS[n] for n in with_attn] + [None]))
    *fulls, conv_g = _forward_blocks(spread_a + spread_b, later_axes)
    wf = dict(zip(later, fulls), w_in=w_in_full)
    conv_full = conv_g[:, :3, :].transpose(1, 0, 2).reshape(3, CONV_W)

    sq_err, g, t = _fwd_bwd_head(x2, mem2, t2, proj, attn, B, S, mem_norm_g, attn_q_norm, attn_k_norm, conv_full, mem_q_norm,
                                 mem_k_norm, wf["mem_w_kv"], wf["w_br_attn"], wf["w_br_conv"], wf["w_br_mem"], wf["w_out"])
    t.update(x2=x2, ng=ng, h=h)

    xyc = jnp.stack([lax.axis_index("x"), lax.axis_index("y"), lax.axis_index("c")]).astype(jnp.int32)
    me1 = me.reshape(1)
    early = ("w_out", "w_br_attn", "w_br_conv", "w_br_mem")
    g["mem_w_kv"], sib_early = _mm(t["mh"], t["dmkv"], mode="tn", out_dtype=DW_DTYPE, name="dw_kv",
                                   comm=_sibling_comm([g[n] for n in early], [BIG_AXIS[n] for n in early]))
    sums_early = [_rs_chip_sum(g[n], ls, xyc, BIG_AXIS[n], "rs_sum_" + n) for n, ls in zip(early, sib_early)]
    tm_w = _pick(D // 2, 1024)
    half = (D // 2) // tm_w
    g_top, (*chips_early, sib_kv) = _mm(t["h"], t["dproj"], mode="tn", out_dtype=DW_DTYPE, name="dw_in_top", tm=tm_w,
                                        m_tiles=(0, half),
                                        comm=_join(_chips_comm(sums_early), _sibling_comm([g["mem_w_kv"]], [0])))
    sum_kv = _rs_chip_sum(g["mem_w_kv"], sib_kv, xyc, 0, "rs_sum_mem_w_kv")
    g_bot, (chips_kv, sib_top) = _mm(t["h"], t["dproj"], mode="tn", out_dtype=DW_DTYPE, name="dw_in_bot", tm=tm_w,
                                     m_tiles=(half, half), comm=_join(_chips_comm([sum_kv]), _sibling_comm([g_top], [1])))
    sum_top = _rs_chip_sum(g_top, sib_top, xyc, 1, "rs_sum_w_in_top")
    tm_t = _pick(T // 2, 1024)
    halfm = (T // 2) // tm_t
    dh, (part_top, relay_top, sib_bot) = _mm(t["dproj"], wf["w_in"], mode="nt", out_dtype=f32, name="d_h_a", tm=tm_t, tk=D_H_TK,
                                             m_tiles=(0, halfm), into="new",
                                             comm=_join(_chips_first_hop([sum_top]), _sibling_comm([g_bot], [1])))
    sum_bot = _rs_chip_sum(g_bot, sib_bot, xyc, 1, "rs_sum_w_in_bot")
    dh, (part_bot, relay_bot, chips_top) = _mm(t["dproj"], wf["w_in"], mode="nt", out_dtype=f32, name="d_h_b", tm=tm_t,
                                               tk=D_H_TK, m_tiles=(halfm, halfm), into=dh,
                                               comm=_join(_chips_first_hop([sum_bot]), _chips_relay([relay_top], [part_top])))
    dx, dng, (chips_bot,) = _rms_bwd(t["x2"], t["ng"], dh, t["dy"], "rms_x_bwd", comm=_chips_relay([relay_bot], [part_bot]))
    g["norm_g"] = dng.reshape(D)

    grad, delta, new_m, new_v = {}, {}, {}, {}
    for n, ls, lc in zip(early + ("mem_w_kv",), sib_early + [sib_kv], chips_early + [chips_kv]):
        grad[n], delta[n], new_m[n], new_v[n] = _adamw_shard(g[n], ls, lc, w[n], mo[n], vo[n], me1, BIG_AXIS[n], "adamw_" + n)
    top = _adamw_shard(g_top, sib_top, chips_top, w_in, m_w_in, v_w_in, me1, 1, "adamw_w_in_top")
    grad["w_in"], delta["w_in"], new_m["w_in"], new_v["w_in"] = _adamw_shard(
        g_bot, sib_bot, chips_bot, w_in, m_w_in, v_w_in, me1, 1, "adamw_w_in_bot", row0=D // 2, prev=top)

    n_rep = sum(w[n].size for n in SMALL) // LANES
    conv_rows = g["conv_w"].reshape(3, N_DEV, LANES).transpose(1, 0, 2).reshape(3 * N_DEV, LANES)
    n_rows = -(-(n_rep + 3 * N_DEV + 1) // 8) * 8
    part = _pack_rows([g[n] for n in SMALL] + [conv_rows, jnp.full((1, LANES), sq_err, f32)], n_rows)
    tot = _allreduce_small(part)
    loss = tot[n_rep + 3 * N_DEV, 0] * (0.5 / D)
    n_small = -(-(n_rep + 3) // 8) * 8
    g_small = _pack_rows([tot[:n_rep], lax.dynamic_slice(tot, (n_rep + 3 * me, 0), (3, LANES))], n_small)
    names = SMALL + ("conv_w",)
    d_s, m_s, v_s = _adamw_small(g_small, _pack_rows([w[n] for n in names], n_small), _pack_rows([mo[n] for n in names], n_small),
                                 _pack_rows([vo[n] for n in names], n_small))
    off = 0
    for n in names:
        r = w[n].size // LANES
        grad[n], delta[n] = g_small[off:off + r].reshape(w[n].shape), d_s[off:off + r].reshape(w[n].shape)
        new_m[n], new_v[n] = m_s[off:off + r].reshape(w[n].shape), v_s[off:off + r].reshape(w[n].shape)
        off += r
    return (loss, dx.reshape(B, S, D), *[grad[n] for n in ALL_W], *[delta[n] for n in ALL_W], *[new_m[n] for n in ALL_W],
            *[new_v[n] for n in ALL_W])
```

```python
import functools

import jax
import jax.numpy as jnp
from jax import lax
from jax.experimental import pallas as pl
from jax.experimental.pallas import tpu as pltpu

f32 = jnp.float32
bf16 = jnp.bfloat16

N_DEV = 8
HEAD_DIM = 128
DILATIONS = (1, 4, 16)
N_GROUPS = 3
HEADS = 4
BLK = 128
ATTN_QKV = N_GROUPS * HEADS * HEAD_DIM
ATTN_OUT = HEADS * HEAD_DIM
CONV_W = 1024
MEM_LEN = 256
MEM_HEADS = 4
MEM_HD = 256
MEM_W = MEM_HEADS * MEM_HD
EPS = 1e-6
Q0, K0, V0 = 0, ATTN_QKV, 2 * ATTN_QKV
ZA = 3 * ATTN_QKV
CB, CC, CV, ZC = ZA + ATTN_OUT, ZA + ATTN_OUT + CONV_W, ZA + ATTN_OUT + 2 * CONV_W, ZA + ATTN_OUT + 3 * CONV_W
MQ = ZC + CONV_W
ZM = MQ + MEM_W
GT = ZM + MEM_W

ADAM_LR, ADAM_B1, ADAM_B2, ADAM_EPS, ADAM_WD, ADAM_STEP = 0.001, 0.9, 0.999, 1e-08, 0.01, 10

VMEM_LIMIT = 56 * 1024 * 1024
D_H_TK = 4352
PROJ_DTYPE = bf16
DW_DTYPE = bf16
LANES = 128

NT_DIMS = (((1,), (1,)), ((), ()))
TN_DIMS = (((0,), (0,)), ((), ()))
NN_DIMS = (((1,), (0,)), ((), ()))


def _params(sem=None):
    return pltpu.CompilerParams(dimension_semantics=sem, vmem_limit_bytes=VMEM_LIMIT)


def _pick(n, pref, q=LANES):
    t = (min(pref, n) // q) * q
    while t >= q:
        if n % t == 0:
            return t
        t -= q
    return n


def _dot(a, b, dims):
    return lax.dot_general(a.astype(bf16), b.astype(bf16), dims, preferred_element_type=f32)


def _sigmoid(z):
    return 0.5 * jnp.tanh(0.5 * z) + 0.5


def _rstd(v):
    return lax.rsqrt(jnp.mean(v * v, axis=-1, keepdims=True) + EPS)


class _Deferred:
    def __init__(self, stage, sems, step, last, n):
        assert last >= 1
        self.stage, self.sems, self.step, self.last, self.n = stage, sems, step, last, n
        self.slot = step % 2

    def _drain(self, slot, like):
        for c in range(self.n):
            pltpu.make_async_copy(self.stage.at[slot, c], like(c), self.sems.at[slot, c]).wait()

    def begin(self, like):
        pl.when(self.step >= 2)(lambda: self._drain(self.slot, like))

    def start(self, c, dst):
        pltpu.make_async_copy(self.stage.at[self.slot, c], dst, self.sems.at[self.slot, c]).start()

    def end(self, like):
        @pl.when(self.step == self.last)
        def _():
            self._drain(self.slot, like)
            self._drain(1 - self.slot, like)


def _row_sum(v):
    return _dot(v, jnp.ones((v.shape[1], v.shape[1]), bf16), NN_DIMS)


def _rstd_head(v):
    return lax.rsqrt(_row_sum(v * v) * (1.0 / v.shape[1]) + EPS)


class _Comm:
    def __init__(self, ins, out_shapes, sem_shape, copies, aliases=None):
        self.ins, self.out_shapes, self.sem_shape, self.copies = list(ins), list(out_shapes), sem_shape, copies
        self.aliases = dict(aliases or {})

    def scratch(self):
        return [pltpu.SemaphoreType.DMA(self.sem_shape), pltpu.SemaphoreType.DMA(self.sem_shape)]

    def io_aliases(self, first_in, first_out):
        return {first_in + k: first_out + v for k, v in self.aliases.items()}


def _mm(a, b, *, mode, out_dtype, name, tm=1024, tn=1024, tk=4096, m_tiles=None, into=None, comm=None):
    if mode == "nn":
        (M, K), (K2, N) = a.shape, b.shape
    elif mode == "nt":
        (M, K), (N, K2) = a.shape, b.shape
    else:
        (K, M), (K2, N) = a.shape, b.shape
    assert K == K2
    tm, tn, tk = _pick(M, tm), _pick(N, tn), _pick(K, tk)
    nk = K // tk
    m0, mc = m_tiles if m_tiles is not None else (0, M // tm)
    o0 = 0 if into is None else m0
    aliased = into is not None and not isinstance(into, str)
    n_ci = len(comm.ins) if comm else 0
    n_co = len(comm.out_shapes) if comm else 0
    grid = (mc, N // tn, nk)
    dims = {"nn": NN_DIMS, "nt": NT_DIMS, "tn": TN_DIMS}[mode]

    def body(*refs, nk):
        a_ref, b_ref = refs[:2]
        pos = 3 if aliased else 2
        c_ins, o_ref, c_outs = refs[pos:pos + n_ci], refs[pos + n_ci], refs[pos + n_ci + 1:pos + n_ci + 1 + n_co]
        scratch = refs[pos + n_ci + 1 + n_co:]
        ids = [pl.program_id(d) for d in range(3)]
        if comm:
            sems = scratch[-2:]

            @pl.when((ids[0] == 0) & (ids[1] == 0) & (ids[2] == 0))
            def _():
                for cp in comm.copies(c_ins, c_outs, *sems):
                    cp.start()

        if nk == 1:
            o_ref[...] = _dot(a_ref[...], b_ref[...], dims).astype(o_ref.dtype)
        else:
            acc_ref, k = scratch[0], ids[2]

            @pl.when(k == 0)
            def _():
                acc_ref[...] = _dot(a_ref[...], b_ref[...], dims)

            @pl.when((k > 0) & (k < nk - 1))
            def _():
                acc_ref[...] += _dot(a_ref[...], b_ref[...], dims)

            @pl.when(k == nk - 1)
            def _():
                o_ref[...] = (acc_ref[...] + _dot(a_ref[...], b_ref[...], dims)).astype(o_ref.dtype)

        if comm:
            @pl.when((ids[0] == grid[0] - 1) & (ids[1] == grid[1] - 1) & (ids[2] == grid[2] - 1))
            def _():
                for cp in comm.copies(c_ins, c_outs, *sems):
                    cp.wait()

    if mode == "tn":
        a_spec = pl.BlockSpec((tk, tm), lambda i, j, k: (k, m0 + i))
    else:
        a_spec = pl.BlockSpec((tm, tk), lambda i, j, k: (m0 + i, k))
    if mode == "nt":
        b_spec = pl.BlockSpec((tn, tk), lambda i, j, k: (j, k))
    else:
        b_spec = pl.BlockSpec((tk, tn), lambda i, j, k: (k, j))
    hbm = pl.BlockSpec(memory_space=pl.ANY)
    res = pl.pallas_call(
        functools.partial(body, nk=nk),
        grid=grid,
        in_specs=[a_spec, b_spec] + [hbm] * (aliased + n_ci),
        out_specs=[pl.BlockSpec((tm, tn), lambda i, j, k: (o0 + i, j))] + [hbm] * n_co,
        out_shape=[jax.ShapeDtypeStruct((mc * tm if into is None else M, N), out_dtype)] + (comm.out_shapes if comm else []),
        scratch_shapes=([pltpu.VMEM((tm, tn), f32)] if nk > 1 else []) + (comm.scratch() if comm else []),
        input_output_aliases={**({2: 0} if aliased else {}), **(comm.io_aliases(2 + aliased, 1) if comm else {})},
        compiler_params=_params(("arbitrary",) * 3 if comm else ("parallel", "parallel", "arbitrary")),
        name=name,
    )(a, b, *([into] if aliased else []), *(comm.ins if comm else []))
    return (res[0], list(res[1:])) if comm else res[0]


def _rms_fwd(x, g, name):
    T, D = x.shape
    tm = _pick(T, 256, 8)

    def body(x_ref, g_ref, h_ref):
        xv = x_ref[...]
        h_ref[...] = (xv * _rstd(xv) * g_ref[...]).astype(bf16)

    return pl.pallas_call(
        body, grid=(T // tm,),
        in_specs=[pl.BlockSpec((tm, D), lambda i: (i, 0)), pl.BlockSpec((1, D), lambda i: (0, 0))],
        out_specs=pl.BlockSpec((tm, D), lambda i: (i, 0)),
        out_shape=jax.ShapeDtypeStruct((T, D), bf16),
        compiler_params=_params(("parallel",)), name=name,
    )(x, g)


def _rms_bwd(x, g, dh, dy, name, comm=None):
    T, D = x.shape
    tm = _pick(T, 256, 8)
    with_dx = dy is not None
    n_ci = len(comm.ins) if comm else 0
    n_co = len(comm.out_shapes) if comm else 0

    def body(*refs):
        if with_dx:
            x_ref, g_ref, dh_ref, dy_ref = refs[:4]
            c_ins, (dx_ref, dg_ref) = refs[4:4 + n_ci], refs[4 + n_ci:6 + n_ci]
            c_outs, sems = refs[6 + n_ci:6 + n_ci + n_co], refs[6 + n_ci + n_co:]
        else:
            x_ref, g_ref, dh_ref, dg_ref = refs
        if comm:
            @pl.when(pl.program_id(0) == 0)
            def _():
                for cp in comm.copies(c_ins, c_outs, *sems):
                    cp.start()

        xv = x_ref[...]
        r = _rstd(xv)
        xh = xv * r
        dhv = dh_ref[...]
        part = jnp.sum(dhv * xh, axis=0, keepdims=True)

        @pl.when(pl.program_id(0) == 0)
        def _():
            dg_ref[...] = part

        @pl.when(pl.program_id(0) > 0)
        def _():
            dg_ref[...] += part

        if with_dx:
            dxh = dhv * g_ref[...]
            dx_ref[...] = dy_ref[...].astype(f32) + r * (dxh - xh * jnp.mean(dxh * xh, axis=-1, keepdims=True))

        if comm:
            @pl.when(pl.program_id(0) == T // tm - 1)
            def _():
                for cp in comm.copies(c_ins, c_outs, *sems):
                    cp.wait()

    row = pl.BlockSpec((tm, D), lambda i: (i, 0))
    vec = pl.BlockSpec((1, D), lambda i: (0, 0))
    hbm = pl.BlockSpec(memory_space=pl.ANY)
    if with_dx:
        res = pl.pallas_call(
            body, grid=(T // tm,), in_specs=[row, vec, row, row] + [hbm] * n_ci, out_specs=[row, vec] + [hbm] * n_co,
            out_shape=[jax.ShapeDtypeStruct((T, D), f32), jax.ShapeDtypeStruct((1, D), f32)] + (comm.out_shapes if comm else []),
            scratch_shapes=comm.scratch() if comm else [],
            input_output_aliases=comm.io_aliases(4, 2) if comm else {},
            compiler_params=_params(("arbitrary",)), name=name,
        )(x, g, dh, dy, *(comm.ins if comm else []))
        return (res[0], res[1], list(res[2:])) if comm else res
    return pl.pallas_call(
        body, grid=(T // tm,), in_specs=[row, vec, row], out_specs=vec,
        out_shape=jax.ShapeDtypeStruct((1, D), f32),
        compiler_params=_params(("arbitrary",)), name=name,
    )(x, g, dh)


MAX_UNIT_UNROLL = 6


def _unit_unroll(trips):
    return max(u for u in range(1, MAX_UNIT_UNROLL + 1) if trips % u == 0)


def _rows(start, size, stride):
    return pl.ds(start, size) if stride == 1 else pl.ds(start, size, stride=stride)


def _attn_units(S, d, first, prev):
    nb = S // d // BLK

    def do_first(r, c):
        first(_rows(r, BLK, d))
        return c

    lax.fori_loop(0, d, do_first, 0, unroll=_unit_unroll(d))
    if nb > 1:
        def do_prev(u, c):
            r = u // (nb - 1)
            b = u % (nb - 1) + 1
            base = r + d * b * BLK
            prev(_rows(base, BLK, d), _rows(base - d * BLK, 2 * BLK, d))
            return c

        lax.fori_loop(0, d * (nb - 1), do_prev, 0, unroll=_unit_unroll(d * (nb - 1)))


def _band_bias(nkeys):
    i = lax.broadcasted_iota(jnp.int32, (BLK, nkeys), 0)
    j = lax.broadcasted_iota(jnp.int32, (BLK, nkeys), 1)
    ok = (j <= i) if nkeys == BLK else ((j >= i) & (j <= i + BLK))
    return jnp.where(ok, 0.0, -jnp.inf).astype(f32)


def _widen_into(src_ref, dst_ref, S):
    for c in range(S // 256):
        rows = pl.ds(c * 256, 256)
        dst_ref[rows, :] = src_ref[rows, :].astype(f32)


def _normalize_into(src_ref, gain, dst_ref, S, rstd_ref=None):
    for c in range(S // 256):
        rows = pl.ds(c * 256, 256)
        v = src_ref[rows, :].astype(f32)
        r = _rstd_head(v)
        dst_ref[rows, :] = v * r * gain
        if rstd_ref is not None:
            rstd_ref[rows, :] = r


def _attn_fwd(proj, gq, gk, B, S, comm=None):
    T, NC = proj.shape
    scale = HEAD_DIM ** -0.5
    n_ci = len(comm.ins) if comm else 0
    n_co = len(comm.out_shapes) if comm else 0

    def body(*refs):
        q_ref, k_ref, v_ref, z_ref, gq_ref, gk_ref = refs[:6]
        c_ins = refs[6:6 + n_ci]
        ag_ref, a_ref, lse_ref = refs[6 + n_ci:9 + n_ci]
        c_outs = refs[9 + n_ci:9 + n_ci + n_co]
        qs, ks, vs, o0, o1, o2, l0, l1, l2, bias1, bias2 = refs[9 + n_ci + n_co:20 + n_ci + n_co]
        sems = refs[20 + n_ci + n_co:]
        g = pl.program_id(2)
        if comm:
            @pl.when((pl.program_id(0) == 0) & (pl.program_id(1) == 0) & (g == 0))
            def _():
                for cp in comm.copies(c_ins, c_outs, *sems):
                    cp.start()

        bias1[...] = _band_bias(BLK)
        bias2[...] = _band_bias(2 * BLK)
        _normalize_into(q_ref, gq_ref[pl.ds(g, 1), :], qs, S)
        _normalize_into(k_ref, gk_ref[pl.ds(g, 1), :], ks, S)
        _widen_into(v_ref, vs, S)
        o_s, l_s = (o0, o1, o2), (l0, l1, l2)

        def run(gi):
            def unit(qr, kr, nkeys):
                s = _dot(qs[qr, :], ks[kr, :], NT_DIMS) * scale + (bias1 if nkeys == BLK else bias2)[...]
                m = jnp.max(s, axis=-1, keepdims=True)
                p = jnp.exp(s - m)
                den = jnp.sum(p, axis=-1, keepdims=True)
                o_s[gi][qr, :] = _dot(p, vs[kr, :], NN_DIMS) * (1.0 / den)
                l_s[gi][qr, :] = jnp.broadcast_to(m + jnp.log(den), (BLK, HEAD_DIM))

            _attn_units(S, DILATIONS[gi], lambda qr: unit(qr, qr, BLK), lambda qr, kr: unit(qr, kr, 2 * BLK))

        for gi in range(N_GROUPS):
            pl.when(g == gi)(functools.partial(run, gi))

        @pl.when(g == N_GROUPS - 1)
        def _():
            for c in range(S // 256):
                rows = pl.ds(c * 256, 256)
                la, lb, lc = l0[rows, :], l1[rows, :], l2[rows, :]
                m = jnp.maximum(jnp.maximum(la, lb), lc)
                wa, wb, wc = jnp.exp(la - m), jnp.exp(lb - m), jnp.exp(lc - m)
                tot = wa + wb + wc
                a = (wa / tot) * o0[rows, :] + (wb / tot) * o1[rows, :] + (wc / tot) * o2[rows, :]
                z = z_ref[rows, :].astype(f32)
                a_ref[rows, :] = a
                lse_ref[rows, :] = m + jnp.log(tot)
                ag_ref[rows, :] = (a * (z * _sigmoid(z))).astype(bf16)

        if comm:
            @pl.when((pl.program_id(0) == B - 1) & (pl.program_id(1) == HEADS - 1) & (g == N_GROUPS - 1))
            def _():
                for cp in comm.copies(c_ins, c_outs, *sems):
                    cp.wait()

    def slab(col0):
        return pl.BlockSpec((S, HEAD_DIM), lambda b, h, g: (b, col0 // HEAD_DIM + g * HEADS + h))

    gain = pl.BlockSpec((N_GROUPS, HEAD_DIM), lambda b, h, g: (0, 0))
    out = pl.BlockSpec((S, HEAD_DIM), lambda b, h, g: (b, h))
    hbm = pl.BlockSpec(memory_space=pl.ANY)
    res = pl.pallas_call(
        body, grid=(B, HEADS, N_GROUPS),
        in_specs=[slab(Q0), slab(K0), slab(V0), pl.BlockSpec((S, HEAD_DIM), lambda b, h, g: (b, ZA // HEAD_DIM + h)), gain, gain]
        + [hbm] * n_ci,
        out_specs=[out, out, out] + [hbm] * n_co,
        out_shape=[jax.ShapeDtypeStruct((T, ATTN_OUT), bf16), jax.ShapeDtypeStruct((T, ATTN_OUT), f32),
                   jax.ShapeDtypeStruct((T, ATTN_OUT), f32)] + (comm.out_shapes if comm else []),
        scratch_shapes=[pltpu.VMEM((S, HEAD_DIM), f32)] * 9 + [pltpu.VMEM((BLK, BLK), f32), pltpu.VMEM((BLK, 2 * BLK), f32)]
        + (comm.scratch() if comm else []),
        compiler_params=_params(("arbitrary", "arbitrary", "arbitrary")), name="attn_fwd",
    )(proj, proj, proj, proj, gq, gk, *(comm.ins if comm else []))
    return res[0], res[1], res[2], list(res[3:])


def _rms_bwd_rows(raw, gain, dn, on_mxu=True, r=None):
    if r is None:
        r = _rstd_head(raw) if on_mxu else _rstd(raw)
    xh = raw * r
    dxh = dn * gain
    if on_mxu:
        mean = _row_sum(dxh * xh) * (1.0 / raw.shape[1])
    else:
        mean = jnp.mean(dxh * xh, axis=-1, keepdims=True)
    return r * (dxh - xh * mean), jnp.sum(dn * xh, axis=0, keepdims=True)


def _attn_bwd(proj, gq, gk, a_comb, lse, dag, dproj, B, S):
    T, NC = proj.shape
    scale = HEAD_DIM ** -0.5

    def body(q_ref, k_ref, v_ref, z_ref, gq_ref, gk_ref, a_ref, lse_ref, dag_ref, dproj_in, dproj_ref, dgq_ref, dgk_ref,
             qs, ks, da_s, dl_s, dq_s, dk_s, dv_s, rq_s, rk_s, vs, bias1, bias2, stage, dz_stage, sem, dz_sem):
        b, h, g = pl.program_id(0), pl.program_id(1), pl.program_id(2)
        bias1[...] = _band_bias(BLK)
        bias2[...] = _band_bias(2 * BLK)
        gq_row, gk_row = gq_ref[pl.ds(g, 1), :], gk_ref[pl.ds(g, 1), :]
        _normalize_into(q_ref, gq_row, qs, S, rq_s)
        _normalize_into(k_ref, gk_row, ks, S, rk_s)
        _widen_into(v_ref, vs, S)

        def dst(c):
            return dproj_ref.at[pl.ds(b * S, S), pl.ds((Q0, K0, V0)[c] + (g * HEADS + h) * HEAD_DIM, HEAD_DIM)]

        out = _Deferred(stage, sem, (b * HEADS + h) * N_GROUPS + g, B * HEADS * N_GROUPS - 1, 3)
        out.begin(dst)

        @pl.when((b == 0) & (h == 0) & (g == 0))
        def _():
            dgq_ref[...] = jnp.zeros_like(dgq_ref)
            dgk_ref[...] = jnp.zeros_like(dgk_ref)

        @pl.when(g == 0)
        def _():
            for c in range(S // 256):
                rows = pl.ds(c * 256, 256)
                z, a, dg_ = z_ref[rows, :].astype(f32), a_ref[rows, :], dag_ref[rows, :].astype(f32)
                sg = _sigmoid(z)
                da = dg_ * (z * sg)
                da_s[rows, :] = da
                dl_s[rows, :] = _row_sum(da * a)
                dz_stage[rows, :] = (dg_ * a * (sg * (1.0 + z * (1.0 - sg)))).astype(bf16)
            cp = pltpu.make_async_copy(dz_stage, dproj_ref.at[pl.ds(b * S, S), pl.ds(ZA + h * HEAD_DIM, HEAD_DIM)], dz_sem)
            cp.start()
            cp.wait()

        dk_s[...] = jnp.zeros_like(dk_s)
        dv_s[...] = jnp.zeros_like(dv_s)

        def run(gi):
            def unit(qr, kr, nkeys):
                q, k, v = qs[qr, :], ks[kr, :], vs[kr, :]
                s = _dot(q, k, NT_DIMS) * scale
                p = jnp.exp(s + (bias1 if nkeys == BLK else bias2)[...] - lse_ref[qr, :][:, :1])
                da = da_s[qr, :]
                dp = _dot(da, v, NT_DIMS)
                ds = p * (dp - dl_s[qr, :][:, :1]) * scale
                dq_s[qr, :] = _dot(ds, k, NN_DIMS)
                dk_s[kr, :] += _dot(ds, q, TN_DIMS)
                dv_s[kr, :] += _dot(p, da, TN_DIMS)

            _attn_units(S, DILATIONS[gi], lambda qr: unit(qr, qr, BLK), lambda qr, kr: unit(qr, kr, 2 * BLK))

        for gi in range(N_GROUPS):
            pl.when(g == gi)(functools.partial(run, gi))

        gq_acc = jnp.zeros((1, HEAD_DIM), f32)
        gk_acc = jnp.zeros((1, HEAD_DIM), f32)
        for c in range(S // 256):
            rows = pl.ds(c * 256, 256)
            dq, gq_p = _rms_bwd_rows(q_ref[rows, :].astype(f32), gq_row, dq_s[rows, :], r=rq_s[rows, :])
            dk, gk_p = _rms_bwd_rows(k_ref[rows, :].astype(f32), gk_row, dk_s[rows, :], r=rk_s[rows, :])
            gq_acc, gk_acc = gq_acc + gq_p, gk_acc + gk_p
            stage[out.slot, 0, rows, :] = dq.astype(bf16)
            stage[out.slot, 1, rows, :] = dk.astype(bf16)
            stage[out.slot, 2, rows, :] = dv_s[rows, :].astype(bf16)
        dgq_ref[pl.ds(g, 1), :] += gq_acc
        dgk_ref[pl.ds(g, 1), :] += gk_acc
        for c in range(3):
            out.start(c, dst(c))
        out.end(dst)

    def slab(col0):
        return pl.BlockSpec((S, HEAD_DIM), lambda b, h, g: (b, col0 // HEAD_DIM + g * HEADS + h))

    gain = pl.BlockSpec((N_GROUPS, HEAD_DIM), lambda b, h, g: (0, 0))
    per_slot = pl.BlockSpec((S, HEAD_DIM), lambda b, h, g: (b, h))
    return pl.pallas_call(
        body, grid=(B, HEADS, N_GROUPS),
        in_specs=[slab(Q0), slab(K0), slab(V0), pl.BlockSpec((S, HEAD_DIM), lambda b, h, g: (b, ZA // HEAD_DIM + h)), gain, gain,
                  per_slot, per_slot, per_slot, pl.BlockSpec(memory_space=pl.ANY)],
        out_specs=[pl.BlockSpec(memory_space=pl.ANY), gain, gain],
        out_shape=[jax.ShapeDtypeStruct(dproj.shape, dproj.dtype), jax.ShapeDtypeStruct((N_GROUPS, HEAD_DIM), f32),
                   jax.ShapeDtypeStruct((N_GROUPS, HEAD_DIM), f32)],
        scratch_shapes=[pltpu.VMEM((S, HEAD_DIM), f32)] * 10 + [pltpu.VMEM((BLK, BLK), f32), pltpu.VMEM((BLK, 2 * BLK), f32),
                                                                 pltpu.VMEM((2, 3, S, HEAD_DIM), bf16), pltpu.VMEM((S, HEAD_DIM), bf16),
                                                                pltpu.SemaphoreType.DMA((2, 3)), pltpu.SemaphoreType.DMA],
        input_output_aliases={9: 0},
        compiler_params=_params(("arbitrary", "arbitrary", "arbitrary")), name="attn_bwd",
    )(proj, proj, proj, proj, gq, gk, a_comb, lse, dag, dproj)


def _conv_fwd(proj, conv_w, B, S):
    T, NC = proj.shape
    tc = LANES

    def body(cb_ref, cc_ref, cv_ref, z_ref, w_ref, c_ref, ub):
        u = cc_ref[...].astype(f32) * cv_ref[...].astype(f32)
        ub[0:8, :] = jnp.zeros((8, tc), f32)
        ub[8:8 + S, :] = u
        w = w_ref[...]
        y = w[0:1] * u + w[1:2] * ub[pl.ds(7, S), :] + w[2:3] * ub[pl.ds(6, S), :]
        z = z_ref[...].astype(f32)
        c_ref[...] = (cb_ref[...].astype(f32) * y * (z * _sigmoid(z))).astype(bf16)

    def seg(col0):
        return pl.BlockSpec((S, tc), lambda j, b: (b, col0 // tc + j))

    return pl.pallas_call(
        body, grid=(CONV_W // tc, B),
        in_specs=[seg(CB), seg(CC), seg(CV), seg(ZC), pl.BlockSpec((3, tc), lambda j, b: (0, j))],
        out_specs=pl.BlockSpec((S, tc), lambda j, b: (b, j)),
        out_shape=jax.ShapeDtypeStruct((T, CONV_W), bf16),
        scratch_shapes=[pltpu.VMEM((S + 8, tc), f32)],
        compiler_params=_params(("parallel", "parallel")), name="conv_fwd",
    )(proj, proj, proj, proj, conv_w)


def _conv_bwd(proj, conv_w, dc, dproj, B, S):
    T, NC = proj.shape
    tc = LANES

    def body(cb_ref, cc_ref, cv_ref, z_ref, w_ref, dc_ref, dproj_in, dproj_ref, dw_ref, ub, db, stage, sem):
        j, b = pl.program_id(0), pl.program_id(1)

        def dst(c):
            return dproj_ref.at[pl.ds(b * S, S), pl.ds((CB, CC, CV, ZC)[c] + j * tc, tc)]

        out = _Deferred(stage, sem, j * B + b, (CONV_W // tc) * B - 1, 4)
        out.begin(dst)
        cb, cc, cv, z = (r[...].astype(f32) for r in (cb_ref, cc_ref, cv_ref, z_ref))
        dcv = dc_ref[...].astype(f32)
        u = cc * cv
        ub[0:8, :] = jnp.zeros((8, tc), f32)
        ub[8:8 + S, :] = u
        u1, u2 = ub[pl.ds(7, S), :], ub[pl.ds(6, S), :]
        w = w_ref[...]
        y = w[0:1] * u + w[1:2] * u1 + w[2:3] * u2
        sg = _sigmoid(z)
        si = z * sg
        dyv = dcv * cb * si
        db[0:S, :] = dyv
        db[S:S + 8, :] = jnp.zeros((8, tc), f32)
        du = w[0:1] * dyv + w[1:2] * db[pl.ds(1, S), :] + w[2:3] * db[pl.ds(2, S), :]
        stage[out.slot, 0] = (dcv * y * si).astype(bf16)
        stage[out.slot, 1] = (du * cv).astype(bf16)
        stage[out.slot, 2] = (du * cc).astype(bf16)
        stage[out.slot, 3] = (dcv * cb * y * (sg * (1.0 + z * (1.0 - sg)))).astype(bf16)
        for c in range(4):
            out.start(c, dst(c))
        part = jnp.concatenate([jnp.sum(dyv * u, axis=0, keepdims=True), jnp.sum(dyv * u1, axis=0, keepdims=True),
                                jnp.sum(dyv * u2, axis=0, keepdims=True)], axis=0)

        @pl.when(b == 0)
        def _():
            dw_ref[...] = part

        @pl.when(b > 0)
        def _():
            dw_ref[...] += part

        out.end(dst)

    def seg(col0):
        return pl.BlockSpec((S, tc), lambda j, b: (b, col0 // tc + j))

    return pl.pallas_call(
        body, grid=(CONV_W // tc, B),
        in_specs=[seg(CB), seg(CC), seg(CV), seg(ZC), pl.BlockSpec((3, tc), lambda j, b: (0, j)),
                  pl.BlockSpec((S, tc), lambda j, b: (b, j)), pl.BlockSpec(memory_space=pl.ANY)],
        out_specs=[pl.BlockSpec(memory_space=pl.ANY), pl.BlockSpec((3, tc), lambda j, b: (0, j))],
        out_shape=[jax.ShapeDtypeStruct(dproj.shape, dproj.dtype), jax.ShapeDtypeStruct((3, CONV_W), f32)],
        scratch_shapes=[pltpu.VMEM((S + 8, tc), f32), pltpu.VMEM((S + 8, tc), f32), pltpu.VMEM((2, 4, S, tc), bf16),
                        pltpu.SemaphoreType.DMA((2, 4))],
        input_output_aliases={6: 0},
        compiler_params=_params(("arbitrary", "arbitrary")), name="conv_bwd",
    )(proj, proj, proj, proj, conv_w, dc, dproj)


MEM_CHUNK = 1024


def _mem_fwd(proj, mkv, gq, gk, B, S):
    T, NC = proj.shape
    scale = MEM_HD ** -0.5

    def body(q_ref, z_ref, k_ref, v_ref, gq_ref, gk_ref, o_ref):
        kv = k_ref[...]
        kn = (kv * _rstd_head(kv) * gk_ref[...]).astype(bf16)
        vv = v_ref[...].astype(bf16)

        def chunk(c, carry):
            rows = pl.ds(pl.multiple_of(c * MEM_CHUNK, MEM_CHUNK), MEM_CHUNK)
            q = q_ref[rows, :].astype(f32)
            qn = q * _rstd(q) * gq_ref[...]
            s = _dot(qn, kn, NT_DIMS) * scale
            p = jnp.exp(s - jnp.max(s, axis=-1, keepdims=True))
            p = p / jnp.sum(p, axis=-1, keepdims=True)
            z = z_ref[rows, :].astype(f32)
            o_ref[rows, :] = (_dot(p, vv, NN_DIMS) * (z * _sigmoid(z))).astype(bf16)
            return carry

        lax.fori_loop(0, S // MEM_CHUNK, chunk, 0)

    gain = pl.BlockSpec((1, MEM_HD), lambda b, h: (0, 0))
    return pl.pallas_call(
        body, grid=(B, MEM_HEADS),
        in_specs=[pl.BlockSpec((S, MEM_HD), lambda b, h: (b, MQ // MEM_HD + h)),
                  pl.BlockSpec((S, MEM_HD), lambda b, h: (b, ZM // MEM_HD + h)),
                  pl.BlockSpec((MEM_LEN, MEM_HD), lambda b, h: (b, h)),
                  pl.BlockSpec((MEM_LEN, MEM_HD), lambda b, h: (b, MEM_HEADS + h)), gain, gain],
        out_specs=pl.BlockSpec((S, MEM_HD), lambda b, h: (b, h)),
        out_shape=jax.ShapeDtypeStruct((T, MEM_W), bf16),
        compiler_params=_params(("parallel", "parallel")), name="mem_fwd",
    )(proj, proj, mkv, mkv, gq, gk)


def _mem_bwd(proj, mkv, gq, gk, dmo, dproj, B, S):
    T, NC = proj.shape
    scale = MEM_HD ** -0.5

    def body(q_ref, z_ref, k_ref, v_ref, gq_ref, gk_ref, dmo_ref, dproj_in, dproj_ref, dmk_ref, dmv_ref, dgq_ref, dgk_ref,
             dkn_s, dv_s, gq_s, stage, sem):
        b, h = pl.program_id(0), pl.program_id(1)

        def dst(c):
            return dproj_ref.at[pl.ds(b * S, S), pl.ds((MQ, ZM)[c] + h * MEM_HD, MEM_HD)]

        out = _Deferred(stage, sem, b * MEM_HEADS + h, B * MEM_HEADS - 1, 2)
        out.begin(dst)
        kv = k_ref[...]
        kn = (kv * _rstd_head(kv) * gk_ref[...]).astype(bf16)
        vv = v_ref[...].astype(bf16)
        dkn_s[...] = jnp.zeros_like(dkn_s)
        dv_s[...] = jnp.zeros_like(dv_s)
        gq_s[...] = jnp.zeros_like(gq_s)

        def chunk(c, carry):
            rows = pl.ds(pl.multiple_of(c * MEM_CHUNK, MEM_CHUNK), MEM_CHUNK)
            q = q_ref[rows, :].astype(f32)
            qn = q * _rstd(q) * gq_ref[...]
            s = _dot(qn, kn, NT_DIMS) * scale
            p = jnp.exp(s - jnp.max(s, axis=-1, keepdims=True))
            p = p / jnp.sum(p, axis=-1, keepdims=True)
            mo = _dot(p, vv, NN_DIMS)
            z, dg_ = z_ref[rows, :].astype(f32), dmo_ref[rows, :].astype(f32)
            sg = _sigmoid(z)
            do = dg_ * (z * sg)
            stage[out.slot, 1, rows, :] = (dg_ * mo * (sg * (1.0 + z * (1.0 - sg)))).astype(bf16)
            dp = _dot(do, vv, NT_DIMS)
            ds = p * (dp - jnp.sum(do * mo, axis=-1, keepdims=True)) * scale
            dq, gq_p = _rms_bwd_rows(q, gq_ref[...], _dot(ds, kn, NN_DIMS), on_mxu=False)
            stage[out.slot, 0, rows, :] = dq.astype(bf16)
            gq_s[...] += gq_p
            dkn_s[...] += _dot(ds, qn, TN_DIMS)
            dv_s[...] += _dot(p, do, TN_DIMS)
            return carry

        lax.fori_loop(0, S // MEM_CHUNK, chunk, 0)
        for c in range(2):
            out.start(c, dst(c))
        dk, gk_p = _rms_bwd_rows(kv, gk_ref[...], dkn_s[...])
        dmk_ref[...] = dk
        dmv_ref[...] = dv_s[...]

        @pl.when((b == 0) & (h == 0))
        def _():
            dgq_ref[...] = gq_s[...]
            dgk_ref[...] = gk_p

        @pl.when((b > 0) | (h > 0))
        def _():
            dgq_ref[...] += gq_s[...]
            dgk_ref[...] += gk_p

        out.end(dst)

    gain = pl.BlockSpec((1, MEM_HD), lambda b, h: (0, 0))
    kvb = pl.BlockSpec((MEM_LEN, MEM_HD), lambda b, h: (b, h))
    return pl.pallas_call(
        body, grid=(B, MEM_HEADS),
        in_specs=[pl.BlockSpec((S, MEM_HD), lambda b, h: (b, MQ // MEM_HD + h)),
                  pl.BlockSpec((S, MEM_HD), lambda b, h: (b, ZM // MEM_HD + h)),
                  kvb, pl.BlockSpec((MEM_LEN, MEM_HD), lambda b, h: (b, MEM_HEADS + h)), gain, gain,
                  pl.BlockSpec((S, MEM_HD), lambda b, h: (b, h)), pl.BlockSpec(memory_space=pl.ANY)],
        out_specs=[pl.BlockSpec(memory_space=pl.ANY), kvb, kvb, gain, gain],
        out_shape=[jax.ShapeDtypeStruct(dproj.shape, dproj.dtype), jax.ShapeDtypeStruct((B * MEM_LEN, MEM_W), f32),
                   jax.ShapeDtypeStruct((B * MEM_LEN, MEM_W), f32), jax.ShapeDtypeStruct((1, MEM_HD), f32),
                   jax.ShapeDtypeStruct((1, MEM_HD), f32)],
        scratch_shapes=[pltpu.VMEM((MEM_LEN, MEM_HD), f32), pltpu.VMEM((MEM_LEN, MEM_HD), f32), pltpu.VMEM((1, MEM_HD), f32),
                        pltpu.VMEM((2, 2, S, MEM_HD), bf16), pltpu.SemaphoreType.DMA((2, 2))],
        input_output_aliases={7: 0},
        compiler_params=_params(("arbitrary", "arbitrary")), name="mem_bwd",
    )(proj, proj, mkv, mkv, gq, gk, dmo, dproj)


def _merge_fwd(proj, ag, cg, mg, wa, wc, wm):
    T, NC = proj.shape
    D = wa.shape[1]
    tm, tn = _pick(T, 1024), _pick(D, 512)
    assert GT % tn == 0

    def body(ag_ref, cg_ref, mg_ref, wa_ref, wc_ref, wm_ref, g0_ref, g1_ref, g2_ref, mer_ref, ba_ref, bc_ref, bm_ref):
        ba = _dot(ag_ref[...], wa_ref[...], NN_DIMS)
        bc = _dot(cg_ref[...], wc_ref[...], NN_DIMS)
        bm = _dot(mg_ref[...], wm_ref[...], NN_DIMS)
        s0, s1, s2 = (_sigmoid(r[...].astype(f32)) for r in (g0_ref, g1_ref, g2_ref))
        mer_ref[...] = (s0 * ba + s1 * bc + s2 * bm).astype(bf16)
        ba_ref[...] = ba.astype(bf16)
        bc_ref[...] = bc.astype(bf16)
        bm_ref[...] = bm.astype(bf16)

    def act(w):
        return pl.BlockSpec((tm, w), lambda i, j: (i, 0))

    def wt(k):
        return pl.BlockSpec((k, tn), lambda i, j: (0, j))

    def gate(n):
        return pl.BlockSpec((tm, tn), lambda i, j: (i, (GT + n * D) // tn + j))

    out = pl.BlockSpec((tm, tn), lambda i, j: (i, j))
    return pl.pallas_call(
        body, grid=(T // tm, D // tn),
        in_specs=[act(ATTN_OUT), act(CONV_W), act(MEM_W), wt(ATTN_OUT), wt(CONV_W), wt(MEM_W), gate(0), gate(1), gate(2)],
        out_specs=[out] * 4, out_shape=[jax.ShapeDtypeStruct((T, D), bf16)] * 4,
        compiler_params=_params(("parallel", "parallel")), name="merge_fwd",
    )(ag, cg, mg, wa, wc, wm, proj, proj, proj)


def _out_loss(merged, w_out, x, tgt):
    T, D = x.shape
    tm, tn = _pick(T, 1024), _pick(D, 512)

    def body(m_ref, w_ref, x_ref, t_ref, dyb_ref, lp_ref):
        e = x_ref[...] + _dot(m_ref[...], w_ref[...], NN_DIMS) - t_ref[...]
        dyb_ref[...] = (e / D).astype(bf16)
        part = jnp.full((1, 8, LANES), jnp.sum(e * e), f32)

        @pl.when(pl.program_id(1) == 0)
        def _():
            lp_ref[...] = part

        @pl.when(pl.program_id(1) > 0)
        def _():
            lp_ref[...] += part

    tile = pl.BlockSpec((tm, tn), lambda i, j: (i, j))
    return pl.pallas_call(
        body, grid=(T // tm, D // tn),
        in_specs=[pl.BlockSpec((tm, D), lambda i, j: (i, 0)), pl.BlockSpec((D, tn), lambda i, j: (0, j)), tile, tile],
        out_specs=[tile, pl.BlockSpec((1, 8, LANES), lambda i, j: (i, 0, 0))],
        out_shape=[jax.ShapeDtypeStruct((T, D), bf16), jax.ShapeDtypeStruct((T // tm, 8, LANES), f32)],
        compiler_params=_params(("parallel", "arbitrary")), name="out_loss",
    )(merged, w_out, x, tgt)


def _merge_bwd(proj, dyb, w_out, ba, bc, bm):
    T, NC = proj.shape
    D = w_out.shape[0]
    tm, tn = _pick(T, 1024), _pick(D, 512)
    assert GT % tn == 0

    def body(dy_ref, w_ref, ba_ref, bc_ref, bm_ref, g0_ref, g1_ref, g2_ref, da_ref, dc_ref, dm_ref, dproj_ref, stage, sem):
        i, j = pl.program_id(0), pl.program_id(1)

        def dst(n):
            return dproj_ref.at[pl.ds(i * tm, tm), pl.ds(GT + n * D + j * tn, tn)]

        out = _Deferred(stage, sem, i * (D // tn) + j, (T // tm) * (D // tn) - 1, 3)
        out.begin(dst)
        dmer = _dot(dy_ref[...], w_ref[...], NT_DIMS)
        for n, (g_ref, br_ref, o_ref) in enumerate(((g0_ref, ba_ref, da_ref), (g1_ref, bc_ref, dc_ref), (g2_ref, bm_ref, dm_ref))):
            sg = _sigmoid(g_ref[...].astype(f32))
            o_ref[...] = (sg * dmer).astype(bf16)
            stage[out.slot, n] = (dmer * br_ref[...].astype(f32) * (sg * (1.0 - sg))).astype(bf16)
            out.start(n, dst(n))
        out.end(dst)

    tile = pl.BlockSpec((tm, tn), lambda i, j: (i, j))

    def gate(n):
        return pl.BlockSpec((tm, tn), lambda i, j: (i, (GT + n * D) // tn + j))

    return pl.pallas_call(
        body, grid=(T // tm, D // tn),
        in_specs=[pl.BlockSpec((tm, D), lambda i, j: (i, 0)), pl.BlockSpec((tn, D), lambda i, j: (j, 0)), tile, tile, tile,
                  gate(0), gate(1), gate(2)],
        out_specs=[tile, tile, tile, pl.BlockSpec(memory_space=pl.ANY)],
        out_shape=[jax.ShapeDtypeStruct((T, D), bf16)] * 3 + [jax.ShapeDtypeStruct((T, NC), bf16)],
        scratch_shapes=[pltpu.VMEM((2, 3, tm, tn), bf16), pltpu.SemaphoreType.DMA((2, 3))],
        compiler_params=_params(("arbitrary", "arbitrary")), name="merge_bwd",
    )(dyb, w_out, ba, bc, bm, proj, proj, proj)


def _fwd_bwd_head(x2, mem2, t2, proj, attn, B, S, mem_norm_g, attn_q_norm, attn_k_norm, conv_w, mem_q_norm, mem_k_norm,
                  w_kv, w_a, w_c, w_m, w_out):
    D = x2.shape[1]
    mng = mem_norm_g.reshape(1, D)
    mqn, mkn = mem_q_norm.reshape(1, MEM_HD), mem_k_norm.reshape(1, MEM_HD)
    ag, a_comb, lse = attn

    mh = _rms_fwd(mem2, mng, "rms_mem")
    mkv = _mm(mh, w_kv, mode="nn", out_dtype=f32, name="mkv")
    cg = _conv_fwd(proj, conv_w, B, S)
    mg = _mem_fwd(proj, mkv, mqn, mkn, B, S)
    merged, ba, bc, bm = _merge_fwd(proj, ag, cg, mg, w_a, w_c, w_m)
    dyb, lp = _out_loss(merged, w_out, x2, t2)
    sq_err = jnp.sum(lp[:, 0, 0])

    g = {}
    g["w_out"] = _mm(merged, dyb, mode="tn", out_dtype=DW_DTYPE, name="dw_out")
    dba, dbc, dbm, dproj = _merge_bwd(proj, dyb, w_out, ba, bc, bm)
    g["w_br_attn"] = _mm(ag, dba, mode="tn", out_dtype=DW_DTYPE, name="dw_br_attn")
    g["w_br_conv"] = _mm(cg, dbc, mode="tn", out_dtype=DW_DTYPE, name="dw_br_conv")
    g["w_br_mem"] = _mm(mg, dbm, mode="tn", out_dtype=DW_DTYPE, name="dw_br_mem")
    dag = _mm(dba, w_a, mode="nt", out_dtype=bf16, name="d_attn_out")
    dcg = _mm(dbc, w_c, mode="nt", out_dtype=bf16, name="d_conv_out")
    dmg = _mm(dbm, w_m, mode="nt", out_dtype=bf16, name="d_mem_out")
    dproj, g["attn_q_norm"], g["attn_k_norm"] = _attn_bwd(proj, attn_q_norm, attn_k_norm, a_comb, lse, dag, dproj, B, S)
    dproj, g["conv_w"] = _conv_bwd(proj, conv_w, dcg, dproj, B, S)
    dproj, dmk, dmv, dgmq, dgmk = _mem_bwd(proj, mkv, mqn, mkn, dmg, dproj, B, S)
    g["mem_q_norm"], g["mem_k_norm"] = dgmq.reshape(MEM_HD), dgmk.reshape(MEM_HD)
    dmkv = jnp.concatenate([dmk, dmv], axis=1)
    dmh = _mm(dmkv, w_kv, mode="nt", out_dtype=f32, name="d_mem_h")
    g["mem_norm_g"] = _rms_bwd(mem2, mng, dmh, None, "rms_mem_bwd").reshape(D)
    return sq_err, g, dict(dy=dyb, dproj=dproj, mh=mh, dmkv=dmkv)


MESH = pl.DeviceIdType.MESH
HBM = pl.BlockSpec(memory_space=pl.ANY)


def _me():
    x, y, c = lax.axis_index("x"), lax.axis_index("y"), lax.axis_index("c")
    return (x, y, c), 4 * x + 2 * y + c


def _peer(k):
    (x, y, c), _ = _me()
    p = (1 - x if k & 4 else x, 1 - y if k & 2 else y, 1 - c if k & 1 else c)
    return p, 4 * p[0] + 2 * p[1] + p[2]


def _window(ref, axis, size, idx):
    if axis is None:
        return ref.at[idx]
    if axis == 0:
        return ref.at[pl.ds(idx * size, size), :]
    return ref.at[:, pl.ds(idx * size, size)]


def _full_shape(s, axis):
    if axis is None:
        return (N_DEV,) + s.shape
    return tuple(N_DEV * d if i == axis else d for i, d in enumerate(s.shape))


OTHER_CHIPS = (4, 2, 6)


def _spread_comm(shards, axes):
    n = len(shards)

    def copies(ins, outs, send_sems, recv_sems, base=0):
        _, me = _me()
        out = []
        for a in range(n):
            mine = _window(outs[a], axes[a], None if axes[a] is None else shards[a].shape[axes[a]], me)
            out.append(pltpu.make_async_copy(ins[a], mine, send_sems.at[base + a, N_CHIP]))
            for k, dist in enumerate((1,) + OTHER_CHIPS):
                dev, _ = _peer(dist)
                out.append(pltpu.make_async_remote_copy(
                    src_ref=ins[a], dst_ref=mine, send_sem=send_sems.at[base + a, k], recv_sem=recv_sems.at[base + a, k],
                    device_id=dev, device_id_type=MESH))
        return out

    shapes = [jax.ShapeDtypeStruct(_full_shape(s, ax), s.dtype) for s, ax in zip(shards, axes)]
    return _Comm(shards, shapes, (n, N_CHIP + 1), copies)


def _forward_blocks(fulls, axes):
    n = len(fulls)
    sizes = [None if ax is None else f.shape[ax] // N_DEV for f, ax in zip(fulls, axes)]

    def body(*refs):
        outs = refs[n:2 * n]
        send_sems, recv_sems = refs[2 * n:]
        sibling, _ = _peer(1)
        copies = []
        for j, dist in enumerate(OTHER_CHIPS):
            _, held = _peer(dist)
            for a in range(n):
                block = _window(outs[a], axes[a], sizes[a], held)
                copies.append(pltpu.make_async_remote_copy(
                    src_ref=block, dst_ref=block, send_sem=send_sems.at[a, j], recv_sem=recv_sems.at[a, j],
                    device_id=sibling, device_id_type=MESH))
        for cp in copies:
            cp.start()
        for cp in copies:
            cp.wait()

    return pl.pallas_call(
        body, in_specs=[HBM] * n, out_specs=[HBM] * n,
        out_shape=[jax.ShapeDtypeStruct(f.shape, f.dtype) for f in fulls],
        scratch_shapes=[pltpu.SemaphoreType.DMA((n, len(OTHER_CHIPS))), pltpu.SemaphoreType.DMA((n, len(OTHER_CHIPS)))],
        input_output_aliases={a: a for a in range(n)},
        name="forward_blocks",
    )(*fulls)


def _shard_order():
    (x, y, c), me = _me()
    xs, ys, xo, yo = _peer(4)[1], _peer(2)[1], _peer(5)[1], _peer(3)[1]
    north = c == 1
    ids = [me, _peer(1)[1], jnp.where(north, xs, ys), jnp.where(north, yo, xo), jnp.where(north, ys, xs),
           jnp.where(north, xo, yo), _peer(6)[1], _peer(7)[1]]
    return jnp.stack(ids).astype(jnp.int32)


def _proj_gather(h, w_shard, order, comm, comm_at):
    T, K = h.shape
    C = w_shard.shape[1]
    tm = _pick(T, 1024)
    nm = T // tm
    ahead = max(nm - 2, 0)
    n_ci, n_co = len(comm.ins), len(comm.out_shapes)

    def body(*refs):
        order_ref, h_ref, ws_ref = refs[:3]
        c_ins = refs[3:3 + n_ci]
        o_ref, wf_ref = refs[3 + n_ci:5 + n_ci]
        c_outs = refs[5 + n_ci:5 + n_ci + n_co]
        wbuf, send_sems, recv_sems, own_sem, buf_sems, c_send, c_recv = refs[5 + n_ci + n_co:]
        t, i = pl.program_id(0), pl.program_id(1)

        @pl.when((t == comm_at) & (i == 0))
        def _():
            for cp in comm.copies(c_ins, c_outs, c_send, c_recv):
                cp.start()
        (x, y, c), me = _me()
        sibling, sib_id = _peer(1)
        chips = [_peer(dist) for dist in OTHER_CHIPS]

        def win(idx):
            return wf_ref.at[:, pl.ds(idx * C, C)]

        def copy(k, block, to, src=None):
            return pltpu.make_async_remote_copy(
                src_ref=win(block) if src is None else src, dst_ref=win(block),
                send_sem=send_sems.at[k], recv_sem=recv_sems.at[k], device_id=to, device_id_type=MESH)

        def fetch(tt, src):
            return pltpu.make_async_copy(src, wbuf.at[tt % 2], buf_sems.at[tt % 2])

        own = pltpu.make_async_copy(ws_ref, win(me), own_sem)
        (x_nbr, x_id), (y_nbr, y_id), (_, d_id) = chips

        def half(k, block, top, to):
            rows = wf_ref.at[pl.ds(0 if top else K // 2, K // 2), pl.ds(block * C, C)]
            return pltpu.make_async_remote_copy(src_ref=rows, dst_ref=rows, send_sem=send_sems.at[k], recv_sem=recv_sems.at[k],
                                                device_id=to, device_id_type=MESH)

        here = (x, y, c)

        def pass_on(from_x):
            if from_x:
                copy(4, x_id, sibling).start()
                half(8, x_id, False, y_nbr).start()
            else:
                copy(5, y_id, sibling).start()
                half(7, y_id, True, x_nbr).start()

        @pl.when((t == 0) & (i == 0))
        def _():
            fetch(0, ws_ref).start()
            own.start()
            copy(0, me, sibling, src=ws_ref).start()

        for tt in range(N_DEV):
            @pl.when((t == tt) & (i == 0))
            def _(tt=tt):
                fetch(tt, ws_ref).wait()

        o_ref[...] = _dot(h_ref[...], wbuf[t % 2], NN_DIMS).astype(o_ref.dtype)

        def schedule(first_x):
            on = c == (1 if first_x else 0)
            (f_dev, f_id), (g_dev, g_id) = ((x_nbr, x_id), (y_nbr, y_id)) if first_x else ((y_nbr, y_id), (x_nbr, x_id))
            kf, kg = (1, 2) if first_x else (2, 1)
            pf, pg = (4, 5) if first_x else (5, 4)

            @pl.when((t == 0) & (i == 0) & on)
            def _():
                copy(kf, me, f_dev, src=ws_ref).start()

            def event(nxt):
                if nxt == 1:
                    copy(0, sib_id, here).wait_recv()
                    return sib_id
                if nxt == 2:
                    copy(kf, f_id, here).wait_recv()
                    copy(kf, me, f_dev, src=ws_ref).wait_send()
                    copy(kg, me, g_dev, src=ws_ref).start()
                    pass_on(first_x)
                    return f_id
                if nxt == 3:
                    copy(pg, g_id + 1 - 2 * c, here).wait_recv()
                    return g_id + 1 - 2 * c
                if nxt == 4:
                    copy(kg, g_id, here).wait_recv()
                    pass_on(not first_x)
                    return g_id
                if nxt == 5:
                    copy(pf, f_id + 1 - 2 * c, here).wait_recv()
                    return f_id + 1 - 2 * c
                if nxt == 6:
                    half(7, d_id, True, here).wait_recv()
                    half(8, d_id, False, here).wait_recv()
                    copy(6, d_id, sibling).start()
                    return d_id
                copy(6, d_id + 1 - 2 * c, here).wait_recv()
                return d_id + 1 - 2 * c

            for tt in range(N_DEV - 1):
                @pl.when((t == tt) & (i == ahead) & on)
                def _(tt=tt):
                    fetch(tt + 1, win(event(tt + 1))).start()

            @pl.when((t == N_DEV - 1) & (i == nm - 1) & on)
            def _():
                copy(kg, me, g_dev, src=ws_ref).wait_send()

        schedule(True)
        schedule(False)

        @pl.when((t == N_DEV - 1) & (i == nm - 1))
        def _():
            copy(0, me, sibling, src=ws_ref).wait_send()
            half(7, y_id, True, x_nbr).wait_send()
            half(8, x_id, False, y_nbr).wait_send()
            for j, (_, dev_id) in enumerate(chips):
                copy(4 + j, dev_id, sibling).wait_send()
            own.wait()
            for cp in comm.copies(c_ins, c_outs, c_send, c_recv):
                cp.wait()

    hbm = pl.BlockSpec(memory_space=pl.ANY)
    res = pl.pallas_call(
        body,
        grid_spec=pltpu.PrefetchScalarGridSpec(
            num_scalar_prefetch=1, grid=(N_DEV, nm),
            in_specs=[pl.BlockSpec((tm, K), lambda t, i, order_ref: (i, 0)), hbm] + [hbm] * n_ci,
            out_specs=[pl.BlockSpec((tm, C), lambda t, i, order_ref: (i, order_ref[t])), hbm] + [hbm] * n_co,
            scratch_shapes=[pltpu.VMEM((2, K, C), bf16), pltpu.SemaphoreType.DMA((9,)), pltpu.SemaphoreType.DMA((9,)),
                            pltpu.SemaphoreType.DMA, pltpu.SemaphoreType.DMA((2,))] + comm.scratch()),
        out_shape=[jax.ShapeDtypeStruct((T, N_DEV * C), PROJ_DTYPE), jax.ShapeDtypeStruct((K, N_DEV * C), bf16)] + comm.out_shapes,
        compiler_params=_params(("arbitrary", "arbitrary")), name="proj_gather",
    )(order, h, w_shard, *comm.ins)
    return res[0], res[1], list(res[2:])


N_CHIP = 4


def _shard_shape(g, ax):
    return tuple(d // N_DEV if i == ax else d for i, d in enumerate(g.shape))


def _sibling_comm(grads, axes):
    n = len(grads)
    sizes = [g.shape[ax] // N_DEV for g, ax in zip(grads, axes)]

    def copies(ins, lands, send_sems, recv_sems, base=0):
        sibling, _ = _peer(1)
        out = []
        for j in range(N_CHIP):
            _, owner = _peer(2 * j + 1)
            for a in range(n):
                out.append(pltpu.make_async_remote_copy(
                    src_ref=_window(ins[a], axes[a], sizes[a], owner), dst_ref=lands[a].at[j],
                    send_sem=send_sems.at[base + a, j], recv_sem=recv_sems.at[base + a, j], device_id=sibling,
                    device_id_type=MESH))
        return out

    shapes = [jax.ShapeDtypeStruct((N_CHIP,) + _shard_shape(g, ax), g.dtype) for g, ax in zip(grads, axes)]
    return _Comm(grads, shapes, (n, N_CHIP), copies)


def _chips_comm(sums):
    n = len(sums)

    def copies(ins, lands, send_sems, recv_sems, base=0):
        out = []
        for j in range(1, N_CHIP):
            dev, _ = _peer(2 * j)
            for a in range(n):
                out.append(pltpu.make_async_remote_copy(
                    src_ref=ins[a].at[j - 1], dst_ref=lands[a].at[j - 1],
                    send_sem=send_sems.at[base + a, j - 1], recv_sem=recv_sems.at[base + a, j - 1], device_id=dev,
                    device_id_type=MESH))
        return out

    return _Comm(sums, [jax.ShapeDtypeStruct(s.shape, s.dtype) for s in sums], (n, N_CHIP), copies)


def _join(c1, c2):
    n1, m1, k1 = len(c1.ins), len(c1.out_shapes), c1.sem_shape[0]

    def copies(ins, lands, send_sems, recv_sems):
        return (c1.copies(ins[:n1], lands[:m1], send_sems, recv_sems, base=0)
                + c2.copies(ins[n1:], lands[m1:], send_sems, recv_sems, base=k1))

    aliases = {**c1.aliases, **{n1 + k: m1 + v for k, v in c2.aliases.items()}}
    return _Comm(c1.ins + c2.ins, c1.out_shapes + c2.out_shapes, (k1 + c2.sem_shape[0], N_CHIP), copies, aliases)


def _chips_first_hop(sums):
    n = len(sums)

    def copies(ins, outs, send_sems, recv_sems, base=0):
        lands, relays = outs[:n], outs[n:]
        (y_dev, _), (x_dev, _) = _peer(2), _peer(4)
        out = []
        for a in range(n):
            half = sums[a].shape[1] // 2
            for k, (src, dst, dev) in enumerate((
                    (ins[a].at[0], lands[a].at[0], y_dev), (ins[a].at[1], lands[a].at[1], x_dev),
                    (ins[a].at[2, pl.ds(0, half)], relays[a].at[0], x_dev),
                    (ins[a].at[2, pl.ds(half, half)], relays[a].at[1], y_dev))):
                out.append(pltpu.make_async_remote_copy(
                    src_ref=src, dst_ref=dst, send_sem=send_sems.at[base + a, k], recv_sem=recv_sems.at[base + a, k],
                    device_id=dev, device_id_type=MESH))
        return out

    shapes = ([jax.ShapeDtypeStruct(s.shape, s.dtype) for s in sums]
              + [jax.ShapeDtypeStruct((2, s.shape[1] // 2) + s.shape[2:], s.dtype) for s in sums])
    return _Comm(sums, shapes, (n, N_CHIP), copies)


def _chips_relay(relays, lands):
    n = len(relays)

    def copies(ins, outs, send_sems, recv_sems, base=0):
        (y_dev, _), (x_dev, _) = _peer(2), _peer(4)
        out = []
        for a in range(n):
            half = relays[a].shape[1]
            for k, (rows, dev) in enumerate(((pl.ds(0, half), y_dev), (pl.ds(half, half), x_dev))):
                out.append(pltpu.make_async_remote_copy(
                    src_ref=ins[a].at[k], dst_ref=outs[a].at[2, rows], send_sem=send_sems.at[base + a, k],
                    recv_sem=recv_sems.at[base + a, k], device_id=dev, device_id_type=MESH))
        return out

    return _Comm(list(relays) + list(lands), [jax.ShapeDtypeStruct(l.shape, l.dtype) for l in lands], (n, N_CHIP), copies,
                 aliases={n + a: a for a in range(n)})


def _rs_chip_sum(grad, land, xyc, axis, name):
    R, C = land.shape[1:]
    tr = _pick(R, max(8, (640 * 1024) // C), 8)

    def body(xyc_ref, g_ref, l_ref, o_ref):
        o_ref[0] = (g_ref[...].astype(f32) + l_ref[0].astype(f32)).astype(bf16)

    def owner(j, xyc_ref):
        jj = j + 1
        return 4 * (xyc_ref[0] ^ (jj >> 1)) + 2 * (xyc_ref[1] ^ (jj & 1)) + xyc_ref[2]

    if axis == 0:
        g_spec = pl.BlockSpec((tr, C), lambda j, i, xyc_ref: (owner(j, xyc_ref) * (R // tr) + i, 0))
    else:
        g_spec = pl.BlockSpec((tr, C), lambda j, i, xyc_ref: (i, owner(j, xyc_ref)))
    return pl.pallas_call(
        body,
        grid_spec=pltpu.PrefetchScalarGridSpec(
            num_scalar_prefetch=1, grid=(N_CHIP - 1, R // tr),
            in_specs=[g_spec, pl.BlockSpec((1, tr, C), lambda j, i, xyc_ref: (j + 1, i, 0))],
            out_specs=pl.BlockSpec((1, tr, C), lambda j, i, xyc_ref: (j, i, 0))),
        out_shape=jax.ShapeDtypeStruct((N_CHIP - 1, R, C), bf16),
        compiler_params=_params(("parallel", "parallel")), name=name,
    )(xyc, grad, land)


def _allreduce_small(part):
    R = part.shape[0]

    def body(p_ref, o_ref, buf, send_sems, recv_sems):
        (x, y, c), me = _me()
        buf[me] = p_ref[...]
        copies = []
        for k in range(1, N_DEV):
            dev, dev_id = _peer(k)
            copies.append(pltpu.make_async_remote_copy(
                src_ref=p_ref, dst_ref=buf.at[me], send_sem=send_sems.at[k - 1], recv_sem=recv_sems.at[k - 1],
                device_id=dev, device_id_type=MESH))
        for cp in copies:
            cp.start()
        for k in range(1, N_DEV):
            _, dev_id = _peer(k)
            pltpu.make_async_remote_copy(
                src_ref=p_ref, dst_ref=buf.at[dev_id], send_sem=send_sems.at[k - 1], recv_sem=recv_sems.at[k - 1],
                device_id=(x, y, c), device_id_type=MESH).wait_recv()
        for cp in copies:
            cp.wait_send()
        acc = buf[0]
        for d in range(1, N_DEV):
            acc = acc + buf[d]
        o_ref[...] = acc

    return pl.pallas_call(
        body, in_specs=[pl.BlockSpec(memory_space=pltpu.VMEM)], out_specs=pl.BlockSpec(memory_space=pltpu.VMEM),
        out_shape=jax.ShapeDtypeStruct((R, LANES), f32),
        scratch_shapes=[pltpu.VMEM((N_DEV, R, LANES), f32), pltpu.SemaphoreType.DMA((N_DEV - 1,)), pltpu.SemaphoreType.DMA((N_DEV - 1,))],
        name="allreduce_small",
    )(part)


def _adamw_math(w, g, m, v):
    m2 = ADAM_B1 * m + (1.0 - ADAM_B1) * g
    v2 = ADAM_B2 * v + (1.0 - ADAM_B2) * (g * g)
    m_hat = m2 / (1.0 - ADAM_B1 ** ADAM_STEP)
    v_hat = v2 / (1.0 - ADAM_B2 ** ADAM_STEP)
    return -ADAM_LR * (m_hat / (jnp.sqrt(v_hat) + ADAM_EPS) + ADAM_WD * w), m2, v2


def _adamw_shard(grad, land_sib, land_chips, w, m, v, me, axis, name, row0=0, prev=None):
    R, C = land_sib.shape[1:]
    assert axis == 1 or R == w.shape[0]
    tr = _pick(R, max(8, (320 * 1024) // C), 8)
    r0 = row0 // tr
    n_prev = len(prev) if prev else 0

    def body(me_ref, g_ref, s_ref, l_ref, w_ref, m_ref, v_ref, *rest):
        go_ref, d_ref, mo_ref, vo_ref = rest[n_prev:]
        g = g_ref[...].astype(f32) + s_ref[0].astype(f32)
        for j in range(N_CHIP - 1):
            g = g + l_ref[j].astype(f32)
        d, m2, v2 = _adamw_math(w_ref[...], g, m_ref[...], v_ref[...])
        go_ref[...] = g
        d_ref[...] = d
        mo_ref[...] = m2
        vo_ref[...] = v2

    if axis == 0:
        g_spec = pl.BlockSpec((tr, C), lambda i, me_ref: (me_ref[0] * (R // tr) + i, 0))
    else:
        g_spec = pl.BlockSpec((tr, C), lambda i, me_ref: (i, me_ref[0]))
    blk = pl.BlockSpec((tr, C), lambda i, me_ref: (r0 + i, 0))
    return pl.pallas_call(
        body,
        grid_spec=pltpu.PrefetchScalarGridSpec(
            num_scalar_prefetch=1, grid=(R // tr,),
            in_specs=[g_spec, pl.BlockSpec((1, tr, C), lambda i, me_ref: (0, i, 0)),
                      pl.BlockSpec((N_CHIP - 1, tr, C), lambda i, me_ref: (0, i, 0)), blk, blk, blk]
            + [pl.BlockSpec(memory_space=pl.ANY)] * n_prev,
            out_specs=[blk] * 4),
        out_shape=[jax.ShapeDtypeStruct(w.shape, f32)] * 4,
        input_output_aliases={7 + i: i for i in range(n_prev)},
        compiler_params=_params(("parallel",)), name=name,
    )(me, grad, land_sib, land_chips, w, m, v, *(prev or []))


def _adamw_small(g, w, m, v):
    def body(g_ref, w_ref, m_ref, v_ref, d_ref, mo_ref, vo_ref):
        d, m2, v2 = _adamw_math(w_ref[...], g_ref[...], m_ref[...], v_ref[...])
        d_ref[...] = d
        mo_ref[...] = m2
        vo_ref[...] = v2

    return pl.pallas_call(body, out_shape=[jax.ShapeDtypeStruct(g.shape, f32)] * 3, name="adamw_small")(g, w, m, v)


def _pack_rows(parts, total_rows):
    rows = jnp.concatenate([p.reshape(-1, LANES) for p in parts], axis=0)
    return jnp.pad(rows, ((0, total_rows - rows.shape[0]), (0, 0)))


BIG = ("w_in", "mem_w_kv", "w_br_attn", "w_br_conv", "w_br_mem", "w_out")
BIG_AXIS = {"w_in": 1, "mem_w_kv": 0, "w_br_attn": 1, "w_br_conv": 1, "w_br_mem": 1, "w_out": 0}
SMALL = ("norm_g", "mem_norm_g", "attn_q_norm", "attn_k_norm", "mem_q_norm", "mem_k_norm")
ALL_W = ("norm_g", "mem_norm_g", "w_in", "attn_q_norm", "attn_k_norm", "conv_w", "mem_w_kv", "mem_q_norm", "mem_k_norm",
         "w_br_attn", "w_br_conv", "w_br_mem", "w_out")


def kernel(x, mem, norm_g, mem_norm_g, w_in, attn_q_norm, attn_k_norm, conv_w, mem_w_kv, mem_q_norm, mem_k_norm, w_br_attn, w_br_conv, w_br_mem, w_out, loss_target, m_norm_g, m_mem_norm_g, m_w_in, m_attn_q_norm, m_attn_k_norm, m_conv_w, m_mem_w_kv, m_mem_q_norm, m_mem_k_norm, m_w_br_attn, m_w_br_conv, m_w_br_mem, m_w_out, v_norm_g, v_mem_norm_g, v_w_in, v_attn_q_norm, v_attn_k_norm, v_conv_w, v_mem_w_kv, v_mem_q_norm, v_mem_k_norm, v_w_br_attn, v_w_br_conv, v_w_br_mem, v_w_out):
    w = dict(norm_g=norm_g, mem_norm_g=mem_norm_g, w_in=w_in, attn_q_norm=attn_q_norm, attn_k_norm=attn_k_norm, conv_w=conv_w,
             mem_w_kv=mem_w_kv, mem_q_norm=mem_q_norm, mem_k_norm=mem_k_norm, w_br_attn=w_br_attn, w_br_conv=w_br_conv,
             w_br_mem=w_br_mem, w_out=w_out)
    mo = dict(norm_g=m_norm_g, mem_norm_g=m_mem_norm_g, w_in=m_w_in, attn_q_norm=m_attn_q_norm, attn_k_norm=m_attn_k_norm,
              conv_w=m_conv_w, mem_w_kv=m_mem_w_kv, mem_q_norm=m_mem_q_norm, mem_k_norm=m_mem_k_norm, w_br_attn=m_w_br_attn,
              w_br_conv=m_w_br_conv, w_br_mem=m_w_br_mem, w_out=m_w_out)
    vo = dict(norm_g=v_norm_g, mem_norm_g=v_mem_norm_g, w_in=v_w_in, attn_q_norm=v_attn_q_norm, attn_k_norm=v_attn_k_norm,
              conv_w=v_conv_w, mem_w_kv=v_mem_w_kv, mem_q_norm=v_mem_q_norm, mem_k_norm=v_mem_k_norm, w_br_attn=v_w_br_attn,
              w_br_conv=v_w_br_conv, w_br_mem=v_w_br_mem, w_out=v_w_out)
    B, S, D = x.shape
    me = (4 * lax.axis_index("x") + 2 * lax.axis_index("y") + lax.axis_index("c")).astype(jnp.int32)

    T = B * S
    x2, t2, mem2, ng = x.reshape(T, D), loss_target.reshape(T, D), mem.reshape(B * MEM_LEN, D), norm_g.reshape(1, D)
    h = _rms_fwd(x2, ng, "rms_x")
    with_proj, with_attn = ("mem_w_kv",), ("w_out", "w_br_attn", "w_br_conv", "w_br_mem")
    later = with_proj + with_attn
    later_axes = [BIG_AXIS[n] for n in later] + [None]
    conv_pad = jnp.pad(conv_w, ((0, 8 - conv_w.shape[0]), (0, 0)))
    proj, w_in_full, spread_a = _proj_gather(
        h, w_in.astype(bf16), _shard_order(),
        _spread_comm([w[n].astype(bf16) for n in with_proj], [BIG_AXIS[n] for n in with_proj]), comm_at=N_DEV - 3)
    *attn, spread_b = _attn_fwd(proj, attn_q_norm, attn_k_norm, B, S,
                                comm=_spread_comm([w[n].astype(bf16) for n in with_attn] + [conv_pad],
                                                  [BIG_AXIS[n] for n in with_attn] + [None]))
    *fulls, conv_g = _forward_blocks(spread_a + spread_b, later_axes)
    wf = dict(zip(later, fulls), w_in=w_in_full)
    conv_full = conv_g[:, :3, :].transpose(1, 0, 2).reshape(3, CONV_W)

    sq_err, g, t = _fwd_bwd_head(x2, mem2, t2, proj, attn, B, S, mem_norm_g, attn_q_norm, attn_k_norm, conv_full, mem_q_norm,
                                 mem_k_norm, wf["mem_w_kv"], wf["w_br_attn"], wf["w_br_conv"], wf["w_br_mem"], wf["w_out"])
    t.update(x2=x2, ng=ng, h=h)

    xyc = jnp.stack([lax.axis_index("x"), lax.axis_index("y"), lax.axis_index("c")]).astype(jnp.int32)
    me1 = me.reshape(1)
    early = ("w_out", "w_br_attn", "w_br_conv", "w_br_mem")
    g["mem_w_kv"], sib_early = _mm(t["mh"], t["dmkv"], mode="tn", out_dtype=DW_DTYPE, name="dw_kv",
                                   comm=_sibling_comm([g[n] for n in early], [BIG_AXIS[n] for n in early]))
    sums_early = [_rs_chip_sum(g[n], ls, xyc, BIG_AXIS[n], "rs_sum_" + n) for n, ls in zip(early, sib_early)]
    tm_w = _pick(D // 2, 1024)
    half = (D // 2) // tm_w
    g_top, (*chips_early, sib_kv) = _mm(t["h"], t["dproj"], mode="tn", out_dtype=DW_DTYPE, name="dw_in_top", tm=tm_w,
                                        m_tiles=(0, half),
                                        comm=_join(_chips_comm(sums_early), _sibling_comm([g["mem_w_kv"]], [0])))
    sum_kv = _rs_chip_sum(g["mem_w_kv"], sib_kv, xyc, 0, "rs_sum_mem_w_kv")
    g_bot, (chips_kv, sib_top) = _mm(t["h"], t["dproj"], mode="tn", out_dtype=DW_DTYPE, name="dw_in_bot", tm=tm_w,
                                     m_tiles=(half, half), comm=_join(_chips_comm([sum_kv]), _sibling_comm([g_top], [1])))
    sum_top = _rs_chip_sum(g_top, sib_top, xyc, 1, "rs_sum_w_in_top")
    tm_t = _pick(T // 2, 1024)
    halfm = (T // 2) // tm_t
    dh, (part_top, relay_top, sib_bot) = _mm(t["dproj"], wf["w_in"], mode="nt", out_dtype=f32, name="d_h_a", tm=tm_t, tk=D_H_TK,
                                             m_tiles=(0, halfm), into="new",
                                             comm=_join(_chips_first_hop([sum_top]), _sibling_comm([g_bot], [1])))
    sum_bot = _rs_chip_sum(g_bot, sib_bot, xyc, 1, "rs_sum_w_in_bot")
    dh, (part_bot, relay_bot, chips_top) = _mm(t["dproj"], wf["w_in"], mode="nt", out_dtype=f32, name="d_h_b", tm=tm_t,
                                               tk=D_H_TK, m_tiles=(halfm, halfm), into=dh,
                                               comm=_join(_chips_first_hop([sum_bot]), _chips_relay([relay_top], [part_top])))
    dx, dng, (chips_bot,) = _rms_bwd(t["x2"], t["ng"], dh, t["dy"], "rms_x_bwd", comm=_chips_relay([relay_bot], [part_bot]))
    g["norm_g"] = dng.reshape(D)

    grad, delta, new_m, new_v = {}, {}, {}, {}
    for n, ls, lc in zip(early + ("mem_w_kv",), sib_early + [sib_kv], chips_early + [chips_kv]):
        grad[n], delta[n], new_m[n], new_v[n] = _adamw_shard(g[n], ls, lc, w[n], mo[n], vo[n], me1, BIG_AXIS[n], "adamw_" + n)
    top = _adamw_shard(g_top, sib_top, chips_top, w_in, m_w_in, v_w_in, me1, 1, "adamw_w_in_top")
    grad["w_in"], delta["w_in"], new_m["w_in"], new_v["w_in"] = _adamw_shard(
        g_bot, sib_bot, chips_bot, w_in, m_w_in, v_w_in, me1, 1, "adamw_w_in_bot", row0=D // 2, prev=top)

    n_rep = sum(w[n].size for n in SMALL) // LANES
    conv_rows = g["conv_w"].reshape(3, N_DEV, LANES).transpose(1, 0, 2).reshape(3 * N_DEV, LANES)
    n_rows = -(-(n_rep + 3 * N_DEV + 1) // 8) * 8
    part = _pack_rows([g[n] for n in SMALL] + [conv_rows, jnp.full((1, LANES), sq_err, f32)], n_rows)
    tot = _allreduce_small(part)
    loss = tot[n_rep + 3 * N_DEV, 0] * (0.5 / D)
    n_small = -(-(n_rep + 3) // 8) * 8
    g_small = _pack_rows([tot[:n_rep], lax.dynamic_slice(tot, (n_rep + 3 * me, 0), (3, LANES))], n_small)
    names = SMALL + ("conv_w",)
    d_s, m_s, v_s = _adamw_small(g_small, _pack_rows([w[n] for n in names], n_small), _pack_rows([mo[n] for n in names], n_small),
                                 _pack_rows([vo[n] for n in names], n_small))
    off = 0
    for n in names:
        r = w[n].size // LANES
        grad[n], delta[n] = g_small[off:off + r].reshape(w[n].shape), d_s[off:off + r].reshape(w[n].shape)
        new_m[n], new_v[n] = m_s[off:off + r].reshape(w[n].shape), v_s[off:off + r].reshape(w[n].shape)
        off += r
    return (loss, dx.reshape(B, S, D), *[grad[n] for n in ALL_W], *[delta[n] for n in ALL_W], *[new_m[n] for n in ALL_W],
            *[new_v[n] for n in ALL_W])
```

```python
import functools

import jax
import jax.numpy as jnp
from jax import lax
from jax.experimental import pallas as pl
from jax.experimental.pallas import tpu as pltpu

f32 = jnp.float32
bf16 = jnp.bfloat16

N_DEV = 8
HEAD_DIM = 128
DILATIONS = (1, 4, 16)
N_GROUPS = 3
HEADS = 4
BLK = 128
ATTN_QKV = N_GROUPS * HEADS * HEAD_DIM
ATTN_OUT = HEADS * HEAD_DIM
CONV_W = 1024
MEM_LEN = 256
MEM_HEADS = 4
MEM_HD = 256
MEM_W = MEM_HEADS * MEM_HD
EPS = 1e-6
Q0, K0, V0 = 0, ATTN_QKV, 2 * ATTN_QKV
ZA = 3 * ATTN_QKV
CB, CC, CV, ZC = ZA + ATTN_OUT, ZA + ATTN_OUT + CONV_W, ZA + ATTN_OUT + 2 * CONV_W, ZA + ATTN_OUT + 3 * CONV_W
MQ = ZC + CONV_W
ZM = MQ + MEM_W
GT = ZM + MEM_W

ADAM_LR, ADAM_B1, ADAM_B2, ADAM_EPS, ADAM_WD, ADAM_STEP = 0.001, 0.9, 0.999, 1e-08, 0.01, 10

VMEM_LIMIT = 56 * 1024 * 1024
D_H_TK = 4352
PROJ_DTYPE = bf16
DW_DTYPE = bf16
LANES = 128

NT_DIMS = (((1,), (1,)), ((), ()))
TN_DIMS = (((0,), (0,)), ((), ()))
NN_DIMS = (((1,), (0,)), ((), ()))


def _params(sem=None):
    return pltpu.CompilerParams(dimension_semantics=sem, vmem_limit_bytes=VMEM_LIMIT)


def _pick(n, pref, q=LANES):
    t = (min(pref, n) // q) * q
    while t >= q:
        if n % t == 0:
            return t
        t -= q
    return n


def _dot(a, b, dims):
    return lax.dot_general(a.astype(bf16), b.astype(bf16), dims, preferred_element_type=f32)


def _sigmoid(z):
    return 0.5 * jnp.tanh(0.5 * z) + 0.5


def _rstd(v):
    return lax.rsqrt(jnp.mean(v * v, axis=-1, keepdims=True) + EPS)


class _Deferred:
    def __init__(self, stage, sems, step, last, n):
        assert last >= 1
        self.stage, self.sems, self.step, self.last, self.n = stage, sems, step, last, n
        self.slot = step % 2

    def _drain(self, slot, like):
        for c in range(self.n):
            pltpu.make_async_copy(self.stage.at[slot, c], like(c), self.sems.at[slot, c]).wait()

    def begin(self, like):
        pl.when(self.step >= 2)(lambda: self._drain(self.slot, like))

    def start(self, c, dst):
        pltpu.make_async_copy(self.stage.at[self.slot, c], dst, self.sems.at[self.slot, c]).start()

    def end(self, like):
        @pl.when(self.step == self.last)
        def _():
            self._drain(self.slot, like)
            self._drain(1 - self.slot, like)


def _row_sum(v):
    return _dot(v, jnp.ones((v.shape[1], v.shape[1]), bf16), NN_DIMS)


def _rstd_head(v):
    return lax.rsqrt(_row_sum(v * v) * (1.0 / v.shape[1]) + EPS)


class _Comm:
    def __init__(self, ins, out_shapes, sem_shape, copies, aliases=None):
        self.ins, self.out_shapes, self.sem_shape, self.copies = list(ins), list(out_shapes), sem_shape, copies
        self.aliases = dict(aliases or {})

    def scratch(self):
        return [pltpu.SemaphoreType.DMA(self.sem_shape), pltpu.SemaphoreType.DMA(self.sem_shape)]

    def io_aliases(self, first_in, first_out):
        return {first_in + k: first_out + v for k, v in self.aliases.items()}


def _mm(a, b, *, mode, out_dtype, name, tm=1024, tn=1024, tk=4096, m_tiles=None, into=None, comm=None):
    if mode == "nn":
        (M, K), (K2, N) = a.shape, b.shape
    elif mode == "nt":
        (M, K), (N, K2) = a.shape, b.shape
    else:
        (K, M), (K2, N) = a.shape, b.shape
    assert K == K2
    tm, tn, tk = _pick(M, tm), _pick(N, tn), _pick(K, tk)
    nk = K // tk
    m0, mc = m_tiles if m_tiles is not None else (0, M // tm)
    o0 = 0 if into is None else m0
    aliased = into is not None and not isinstance(into, str)
    n_ci = len(comm.ins) if comm else 0
    n_co = len(comm.out_shapes) if comm else 0
    grid = (mc, N // tn, nk)
    dims = {"nn": NN_DIMS, "nt": NT_DIMS, "tn": TN_DIMS}[mode]

    def body(*refs, nk):
        a_ref, b_ref = refs[:2]
        pos = 3 if aliased else 2
        c_ins, o_ref, c_outs = refs[pos:pos + n_ci], refs[pos + n_ci], refs[pos + n_ci + 1:pos + n_ci + 1 + n_co]
        scratch = refs[pos + n_ci + 1 + n_co:]
        ids = [pl.program_id(d) for d in range(3)]
        if comm:
            sems = scratch[-2:]

            @pl.when((ids[0] == 0) & (ids[1] == 0) & (ids[2] == 0))
            def _():
                for cp in comm.copies(c_ins, c_outs, *sems):
                    cp.start()

        if nk == 1:
            o_ref[...] = _dot(a_ref[...], b_ref[...], dims).astype(o_ref.dtype)
        else:
            acc_ref, k = scratch[0], ids[2]

            @pl.when(k == 0)
            def _():
                acc_ref[...] = _dot(a_ref[...], b_ref[...], dims)

            @pl.when((k > 0) & (k < nk - 1))
            def _():
                acc_ref[...] += _dot(a_ref[...], b_ref[...], dims)

            @pl.when(k == nk - 1)
            def _():
                o_ref[...] = (acc_ref[...] + _dot(a_ref[...], b_ref[...], dims)).astype(o_ref.dtype)

        if comm:
            @pl.when((ids[0] == grid[0] - 1) & (ids[1] == grid[1] - 1) & (ids[2] == grid[2] - 1))
            def _():
                for cp in comm.copies(c_ins, c_outs, *sems):
                    cp.wait()

    if mode == "tn":
        a_spec = pl.BlockSpec((tk, tm), lambda i, j, k: (k, m0 + i))
    else:
        a_spec = pl.BlockSpec((tm, tk), lambda i, j, k: (m0 + i, k))
    if mode == "nt":
        b_spec = pl.BlockSpec((tn, tk), lambda i, j, k: (j, k))
    else:
        b_spec = pl.BlockSpec((tk, tn), lambda i, j, k: (k, j))
    hbm = pl.BlockSpec(memory_space=pl.ANY)
    res = pl.pallas_call(
        functools.partial(body, nk=nk),
        grid=grid,
        in_specs=[a_spec, b_spec] + [hbm] * (aliased + n_ci),
        out_specs=[pl.BlockSpec((tm, tn), lambda i, j, k: (o0 + i, j))] + [hbm] * n_co,
        out_shape=[jax.ShapeDtypeStruct((mc * tm if into is None else M, N), out_dtype)] + (comm.out_shapes if comm else []),
        scratch_shapes=([pltpu.VMEM((tm, tn), f32)] if nk > 1 else []) + (comm.scratch() if comm else []),
        input_output_aliases={**({2: 0} if aliased else {}), **(comm.io_aliases(2 + aliased, 1) if comm else {})},
        compiler_params=_params(("arbitrary",) * 3 if comm else ("parallel", "parallel", "arbitrary")),
        name=name,
    )(a, b, *([into] if aliased else []), *(comm.ins if comm else []))
    return (res[0], list(res[1:])) if comm else res[0]


def _rms_fwd(x, g, name):
    T, D = x.shape
    tm = _pick(T, 256, 8)

    def body(x_ref, g_ref, h_ref):
        xv = x_ref[...]
        h_ref[...] = (xv * _rstd(xv) * g_ref[...]).astype(bf16)

    return pl.pallas_call(
        body, grid=(T // tm,),
        in_specs=[pl.BlockSpec((tm, D), lambda i: (i, 0)), pl.BlockSpec((1, D), lambda i: (0, 0))],
        out_specs=pl.BlockSpec((tm, D), lambda i: (i, 0)),
        out_shape=jax.ShapeDtypeStruct((T, D), bf16),
        compiler_params=_params(("parallel",)), name=name,
    )(x, g)


def _rms_bwd(x, g, dh, dy, name, comm=None):
    T, D = x.shape
    tm = _pick(T, 256, 8)
    with_dx = dy is not None
    n_ci = len(comm.ins) if comm else 0
    n_co = len(comm.out_shapes) if comm else 0

    def body(*refs):
        if with_dx:
            x_ref, g_ref, dh_ref, dy_ref = refs[:4]
            c_ins, (dx_ref, dg_ref) = refs[4:4 + n_ci], refs[4 + n_ci:6 + n_ci]
            c_outs, sems = refs[6 + n_ci:6 + n_ci + n_co], refs[6 + n_ci + n_co:]
        else:
            x_ref, g_ref, dh_ref, dg_ref = refs
        if comm:
            @pl.when(pl.program_id(0) == 0)
            def _():
                for cp in comm.copies(c_ins, c_outs, *sems):
                    cp.start()

        xv = x_ref[...]
        r = _rstd(xv)
        xh = xv * r
        dhv = dh_ref[...]
        part = jnp.sum(dhv * xh, axis=0, keepdims=True)

        @pl.when(pl.program_id(0) == 0)
        def _():
            dg_ref[...] = part

        @pl.when(pl.program_id(0) > 0)
        def _():
            dg_ref[...] += part

        if with_dx:
            dxh = dhv * g_ref[...]
            dx_ref[...] = dy_ref[...].astype(f32) + r * (dxh - xh * jnp.mean(dxh * xh, axis=-1, keepdims=True))

        if comm:
            @pl.when(pl.program_id(0) == T // tm - 1)
            def _():
                for cp in comm.copies(c_ins, c_outs, *sems):
                    cp.wait()

    row = pl.BlockSpec((tm, D), lambda i: (i, 0))
    vec = pl.BlockSpec((1, D), lambda i: (0, 0))
    hbm = pl.BlockSpec(memory_space=pl.ANY)
    if with_dx:
        res = pl.pallas_call(
            body, grid=(T // tm,), in_specs=[row, vec, row, row] + [hbm] * n_ci, out_specs=[row, vec] + [hbm] * n_co,
            out_shape=[jax.ShapeDtypeStruct((T, D), f32), jax.ShapeDtypeStruct((1, D), f32)] + (comm.out_shapes if comm else []),
            scratch_shapes=comm.scratch() if comm else [],
            input_output_aliases=comm.io_aliases(4, 2) if comm else {},
            compiler_params=_params(("arbitrary",)), name=name,
        )(x, g, dh, dy, *(comm.ins if comm else []))
        return (res[0], res[1], list(res[2:])) if comm else res
    return pl.pallas_call(
        body, grid=(T // tm,), in_specs=[row, vec, row], out_specs=vec,
        out_shape=jax.ShapeDtypeStruct((1, D), f32),
        compiler_params=_params(("arbitrary",)), name=name,
    )(x, g, dh)


MAX_UNIT_UNROLL = 6


def _unit_unroll(trips):
    return max(u for u in range(1, MAX_UNIT_UNROLL + 1) if trips % u == 0)


def _rows(start, size, stride):
    return pl.ds(start, size) if stride == 1 else pl.ds(start, size, stride=stride)


def _attn_units(S, d, first, prev):
    nb = S // d // BLK

    def do_first(r, c):
        first(_rows(r, BLK, d))
        return c

    lax.fori_loop(0, d, do_first, 0, unroll=_unit_unroll(d))
    if nb > 1:
        def do_prev(u, c):
            r = u // (nb - 1)
            b = u % (nb - 1) + 1
            base = r + d * b * BLK
            prev(_rows(base, BLK, d), _rows(base - d * BLK, 2 * BLK, d))
            return c

        lax.fori_loop(0, d * (nb - 1), do_prev, 0, unroll=_unit_unroll(d * (nb - 1)))


def _band_bias(nkeys):
    i = lax.broadcasted_iota(jnp.int32, (BLK, nkeys), 0)
    j = lax.broadcasted_iota(jnp.int32, (BLK, nkeys), 1)
    ok = (j <= i) if nkeys == BLK else ((j >= i) & (j <= i + BLK))
    return jnp.where(ok, 0.0, -jnp.inf).astype(f32)


def _widen_into(src_ref, dst_ref, S):
    for c in range(S // 256):
        rows = pl.ds(c * 256, 256)
        dst_ref[rows, :] = src_ref[rows, :].astype(f32)


def _normalize_into(src_ref, gain, dst_ref, S, rstd_ref=None):
    for c in range(S // 256):
        rows = pl.ds(c * 256, 256)
        v = src_ref[rows, :].astype(f32)
        r = _rstd_head(v)
        dst_ref[rows, :] = v * r * gain
        if rstd_ref is not None:
            rstd_ref[rows, :] = r


def _attn_fwd(proj, gq, gk, B, S, comm=None):
    T, NC = proj.shape
    scale = HEAD_DIM ** -0.5
    n_ci = len(comm.ins) if comm else 0
    n_co = len(comm.out_shapes) if comm else 0

    def body(*refs):
        q_ref, k_ref, v_ref, z_ref, gq_ref, gk_ref = refs[:6]
        c_ins = refs[6:6 + n_ci]
        ag_ref, a_ref, lse_ref = refs[6 + n_ci:9 + n_ci]
        c_outs = refs[9 + n_ci:9 + n_ci + n_co]
        qs, ks, vs, o0, o1, o2, l0, l1, l2, bias1, bias2 = refs[9 + n_ci + n_co:20 + n_ci + n_co]
        sems = refs[20 + n_ci + n_co:]
        g = pl.program_id(2)
        if comm:
            @pl.when((pl.program_id(0) == 0) & (pl.program_id(1) == 0) & (g == 0))
            def _():
                for cp in comm.copies(c_ins, c_outs, *sems):
                    cp.start()

        bias1[...] = _band_bias(BLK)
        bias2[...] = _band_bias(2 * BLK)
        _normalize_into(q_ref, gq_ref[pl.ds(g, 1), :], qs, S)
        _normalize_into(k_ref, gk_ref[pl.ds(g, 1), :], ks, S)
        _widen_into(v_ref, vs, S)
        o_s, l_s = (o0, o1, o2), (l0, l1, l2)

        def run(gi):
            def unit(qr, kr, nkeys):
                s = _dot(qs[qr, :], ks[kr, :], NT_DIMS) * scale + (bias1 if nkeys == BLK else bias2)[...]
                m = jnp.max(s, axis=-1, keepdims=True)
                p = jnp.exp(s - m)
                den = jnp.sum(p, axis=-1, keepdims=True)
                o_s[gi][qr, :] = _dot(p, vs[kr, :], NN_DIMS) * (1.0 / den)
                l_s[gi][qr, :] = jnp.broadcast_to(m + jnp.log(den), (BLK, HEAD_DIM))

            _attn_units(S, DILATIONS[gi], lambda qr: unit(qr, qr, BLK), lambda qr, kr: unit(qr, kr, 2 * BLK))

        for gi in range(N_GROUPS):
            pl.when(g == gi)(functools.partial(run, gi))

        @pl.when(g == N_GROUPS - 1)
        def _():
            for c in range(S // 256):
                rows = pl.ds(c * 256, 256)
                la, lb, lc = l0[rows, :], l1[rows, :], l2[rows, :]
                m = jnp.maximum(jnp.maximum(la, lb), lc)
                wa, wb, wc = jnp.exp(la - m), jnp.exp(lb - m), jnp.exp(lc - m)
                tot = wa + wb + wc
                a = (wa / tot) * o0[rows, :] + (wb / tot) * o1[rows, :] + (wc / tot) * o2[rows, :]
                z = z_ref[rows, :].astype(f32)
                a_ref[rows, :] = a
                lse_ref[rows, :] = m + jnp.log(tot)
                ag_ref[rows, :] = (a * (z * _sigmoid(z))).astype(bf16)

        if comm:
            @pl.when((pl.program_id(0) == B - 1) & (pl.program_id(1) == HEADS - 1) & (g == N_GROUPS - 1))
            def _():
                for cp in comm.copies(c_ins, c_outs, *sems):
                    cp.wait()

    def slab(col0):
        return pl.BlockSpec((S, HEAD_DIM), lambda b, h, g: (b, col0 // HEAD_DIM + g * HEADS + h))

    gain = pl.BlockSpec((N_GROUPS, HEAD_DIM), lambda b, h, g: (0, 0))
    out = pl.BlockSpec((S, HEAD_DIM), lambda b, h, g: (b, h))
    hbm = pl.BlockSpec(memory_space=pl.ANY)
    res = pl.pallas_call(
        body, grid=(B, HEADS, N_GROUPS),
        in_specs=[slab(Q0), slab(K0), slab(V0), pl.BlockSpec((S, HEAD_DIM), lambda b, h, g: (b, ZA // HEAD_DIM + h)), gain, gain]
        + [hbm] * n_ci,
        out_specs=[out, out, out] + [hbm] * n_co,
        out_shape=[jax.ShapeDtypeStruct((T, ATTN_OUT), bf16), jax.ShapeDtypeStruct((T, ATTN_OUT), f32),
                   jax.ShapeDtypeStruct((T, ATTN_OUT), f32)] + (comm.out_shapes if comm else []),
        scratch_shapes=[pltpu.VMEM((S, HEAD_DIM), f32)] * 9 + [pltpu.VMEM((BLK, BLK), f32), pltpu.VMEM((BLK, 2 * BLK), f32)]
        + (comm.scratch() if comm else []),
        compiler_params=_params(("arbitrary", "arbitrary", "arbitrary")), name="attn_fwd",
    )(proj, proj, proj, proj, gq, gk, *(comm.ins if comm else []))
    return res[0], res[1], res[2], list(res[3:])


def _rms_bwd_rows(raw, gain, dn, on_mxu=True, r=None):
    if r is None:
        r = _rstd_head(raw) if on_mxu else _rstd(raw)
    xh = raw * r
    dxh = dn * gain
    if on_mxu:
        mean = _row_sum(dxh * xh) * (1.0 / raw.shape[1])
    else:
        mean = jnp.mean(dxh * xh, axis=-1, keepdims=True)
    return r * (dxh - xh * mean), jnp.sum(dn * xh, axis=0, keepdims=True)


def _attn_bwd(proj, gq, gk, a_comb, lse, dag, dproj, B, S):
    T, NC = proj.shape
    scale = HEAD_DIM ** -0.5

    def body(q_ref, k_ref, v_ref, z_ref, gq_ref, gk_ref, a_ref, lse_ref, dag_ref, dproj_in, dproj_ref, dgq_ref, dgk_ref,
             qs, ks, da_s, dl_s, dq_s, dk_s, dv_s, rq_s, rk_s, vs, bias1, bias2, stage, dz_stage, sem, dz_sem):
        b, h, g = pl.program_id(0), pl.program_id(1), pl.program_id(2)
        bias1[...] = _band_bias(BLK)
        bias2[...] = _band_bias(2 * BLK)
        gq_row, gk_row = gq_ref[pl.ds(g, 1), :], gk_ref[pl.ds(g, 1), :]
        _normalize_into(q_ref, gq_row, qs, S, rq_s)
        _normalize_into(k_ref, gk_row, ks, S, rk_s)
        _widen_into(v_ref, vs, S)

        def dst(c):
            return dproj_ref.at[pl.ds(b * S, S), pl.ds((Q0, K0, V0)[c] + (g * HEADS + h) * HEAD_DIM, HEAD_DIM)]

        out = _Deferred(stage, sem, (b * HEADS + h) * N_GROUPS + g, B * HEADS * N_GROUPS - 1, 3)
        out.begin(dst)

        @pl.when((b == 0) & (h == 0) & (g == 0))
        def _():
            dgq_ref[...] = jnp.zeros_like(dgq_ref)
            dgk_ref[...] = jnp.zeros_like(dgk_ref)

        @pl.when(g == 0)
        def _():
            for c in range(S // 256):
                rows = pl.ds(c * 256, 256)
                z, a, dg_ = z_ref[rows, :].astype(f32), a_ref[rows, :], dag_ref[rows, :].astype(f32)
                sg = _sigmoid(z)
                da = dg_ * (z * sg)
                da_s[rows, :] = da
                dl_s[rows, :] = _row_sum(da * a)
                dz_stage[rows, :] = (dg_ * a * (sg * (1.0 + z * (1.0 - sg)))).astype(bf16)
            cp = pltpu.make_async_copy(dz_stage, dproj_ref.at[pl.ds(b * S, S), pl.ds(ZA + h * HEAD_DIM, HEAD_DIM)], dz_sem)
            cp.start()
            cp.wait()

        dk_s[...] = jnp.zeros_like(dk_s)
        dv_s[...] = jnp.zeros_like(dv_s)

        def run(gi):
            def unit(qr, kr, nkeys):
                q, k, v = qs[qr, :], ks[kr, :], vs[kr, :]
                s = _dot(q, k, NT_DIMS) * scale
                p = jnp.exp(s + (bias1 if nkeys == BLK else bias2)[...] - lse_ref[qr, :][:, :1])
                da = da_s[qr, :]
                dp = _dot(da, v, NT_DIMS)
                ds = p * (dp - dl_s[qr, :][:, :1]) * scale
                dq_s[qr, :] = _dot(ds, k, NN_DIMS)
                dk_s[kr, :] += _dot(ds, q, TN_DIMS)
                dv_s[kr, :] += _dot(p, da, TN_DIMS)

            _attn_units(S, DILATIONS[gi], lambda qr: unit(qr, qr, BLK), lambda qr, kr: unit(qr, kr, 2 * BLK))

        for gi in range(N_GROUPS):
            pl.when(g == gi)(functools.partial(run, gi))

        gq_acc = jnp.zeros((1, HEAD_DIM), f32)
        gk_acc = jnp.zeros((1, HEAD_DIM), f32)
        for c in range(S // 256):
            rows = pl.ds(c * 256, 256)
            dq, gq_p = _rms_bwd_rows(q_ref[rows, :].astype(f32), gq_row, dq_s[rows, :], r=rq_s[rows, :])
            dk, gk_p = _rms_bwd_rows(k_ref[rows, :].astype(f32), gk_row, dk_s[rows, :], r=rk_s[rows, :])
            gq_acc, gk_acc = gq_acc + gq_p, gk_acc + gk_p
            stage[out.slot, 0, rows, :] = dq.astype(bf16)
            stage[out.slot, 1, rows, :] = dk.astype(bf16)
            stage[out.slot, 2, rows, :] = dv_s[rows, :].astype(bf16)
        dgq_ref[pl.ds(g, 1), :] += gq_acc
        dgk_ref[pl.ds(g, 1), :] += gk_acc
        for c in range(3):
            out.start(c, dst(c))
        out.end(dst)

    def slab(col0):
        return pl.BlockSpec((S, HEAD_DIM), lambda b, h, g: (b, col0 // HEAD_DIM + g * HEADS + h))

    gain = pl.BlockSpec((N_GROUPS, HEAD_DIM), lambda b, h, g: (0, 0))
    per_slot = pl.BlockSpec((S, HEAD_DIM), lambda b, h, g: (b, h))
    return pl.pallas_call(
        body, grid=(B, HEADS, N_GROUPS),
        in_specs=[slab(Q0), slab(K0), slab(V0), pl.BlockSpec((S, HEAD_DIM), lambda b, h, g: (b, ZA // HEAD_DIM + h)), gain, gain,
                  per_slot, per_slot, per_slot, pl.BlockSpec(memory_space=pl.ANY)],
        out_specs=[pl.BlockSpec(memory_space=pl.ANY), gain, gain],
        out_shape=[jax.ShapeDtypeStruct(dproj.shape, dproj.dtype), jax.ShapeDtypeStruct((N_GROUPS, HEAD_DIM), f32),
                   jax.ShapeDtypeStruct((N_GROUPS, HEAD_DIM), f32)],
        scratch_shapes=[pltpu.VMEM((S, HEAD_DIM), f32)] * 10 + [pltpu.VMEM((BLK, BLK), f32), pltpu.VMEM((BLK, 2 * BLK), f32),
                                                                 pltpu.VMEM((2, 3, S, HEAD_DIM), bf16), pltpu.VMEM((S, HEAD_DIM), bf16),
                                                                pltpu.SemaphoreType.DMA((2, 3)), pltpu.SemaphoreType.DMA],
        input_output_aliases={9: 0},
        compiler_params=_params(("arbitrary", "arbitrary", "arbitrary")), name="attn_bwd",
    )(proj, proj, proj, proj, gq, gk, a_comb, lse, dag, dproj)


def _conv_fwd(proj, conv_w, B, S):
    T, NC = proj.shape
    tc = 2 * LANES

    def body(cb_ref, cc_ref, cv_ref, z_ref, w_ref, c_ref, ub):
        u = cc_ref[...].astype(f32) * cv_ref[...].astype(f32)
        ub[0:8, :] = jnp.zeros((8, tc), f32)
        ub[8:8 + S, :] = u
        w = w_ref[...]
        y = w[0:1] * u + w[1:2] * ub[pl.ds(7, S), :] + w[2:3] * ub[pl.ds(6, S), :]
        z = z_ref[...].astype(f32)
        c_ref[...] = (cb_ref[...].astype(f32) * y * (z * _sigmoid(z))).astype(bf16)

    def seg(col0):
        return pl.BlockSpec((S, tc), lambda j, b: (b, col0 // tc + j))

    return pl.pallas_call(
        body, grid=(CONV_W // tc, B),
        in_specs=[seg(CB), seg(CC), seg(CV), seg(ZC), pl.BlockSpec((3, tc), lambda j, b: (0, j))],
        out_specs=pl.BlockSpec((S, tc), lambda j, b: (b, j)),
        out_shape=jax.ShapeDtypeStruct((T, CONV_W), bf16),
        scratch_shapes=[pltpu.VMEM((S + 8, tc), f32)],
        compiler_params=_params(("parallel", "parallel")), name="conv_fwd",
    )(proj, proj, proj, proj, conv_w)


def _conv_bwd(proj, conv_w, dc, dproj, B, S):
    T, NC = proj.shape
    tc = 2 * LANES

    def body(cb_ref, cc_ref, cv_ref, z_ref, w_ref, dc_ref, dproj_in, dproj_ref, dw_ref, ub, db, stage, sem):
        j, b = pl.program_id(0), pl.program_id(1)

        def dst(c):
            return dproj_ref.at[pl.ds(b * S, S), pl.ds((CB, CC, CV, ZC)[c] + j * tc, tc)]

        out = _Deferred(stage, sem, j * B + b, (CONV_W // tc) * B - 1, 4)
        out.begin(dst)
        cb, cc, cv, z = (r[...].astype(f32) for r in (cb_ref, cc_ref, cv_ref, z_ref))
        dcv = dc_ref[...].astype(f32)
        u = cc * cv
        ub[0:8, :] = jnp.zeros((8, tc), f32)
        ub[8:8 + S, :] = u
        u1, u2 = ub[pl.ds(7, S), :], ub[pl.ds(6, S), :]
        w = w_ref[...]
        y = w[0:1] * u + w[1:2] * u1 + w[2:3] * u2
        sg = _sigmoid(z)
        si = z * sg
        dyv = dcv * cb * si
        db[0:S, :] = dyv
        db[S:S + 8, :] = jnp.zeros((8, tc), f32)
        du = w[0:1] * dyv + w[1:2] * db[pl.ds(1, S), :] + w[2:3] * db[pl.ds(2, S), :]
        stage[out.slot, 0] = (dcv * y * si).astype(bf16)
        stage[out.slot, 1] = (du * cv).astype(bf16)
        stage[out.slot, 2] = (du * cc).astype(bf16)
        stage[out.slot, 3] = (dcv * cb * y * (sg * (1.0 + z * (1.0 - sg)))).astype(bf16)
        for c in range(4):
            out.start(c, dst(c))
        part = jnp.concatenate([jnp.sum(dyv * u, axis=0, keepdims=True), jnp.sum(dyv * u1, axis=0, keepdims=True),
                                jnp.sum(dyv * u2, axis=0, keepdims=True)], axis=0)

        @pl.when(b == 0)
        def _():
            dw_ref[...] = part

        @pl.when(b > 0)
        def _():
            dw_ref[...] += part

        out.end(dst)

    def seg(col0):
        return pl.BlockSpec((S, tc), lambda j, b: (b, col0 // tc + j))

    return pl.pallas_call(
        body, grid=(CONV_W // tc, B),
        in_specs=[seg(CB), seg(CC), seg(CV), seg(ZC), pl.BlockSpec((3, tc), lambda j, b: (0, j)),
                  pl.BlockSpec((S, tc), lambda j, b: (b, j)), pl.BlockSpec(memory_space=pl.ANY)],
        out_specs=[pl.BlockSpec(memory_space=pl.ANY), pl.BlockSpec((3, tc), lambda j, b: (0, j))],
        out_shape=[jax.ShapeDtypeStruct(dproj.shape, dproj.dtype), jax.ShapeDtypeStruct((3, CONV_W), f32)],
        scratch_shapes=[pltpu.VMEM((S + 8, tc), f32), pltpu.VMEM((S + 8, tc), f32), pltpu.VMEM((2, 4, S, tc), bf16),
                        pltpu.SemaphoreType.DMA((2, 4))],
        input_output_aliases={6: 0},
        compiler_params=_params(("arbitrary", "arbitrary")), name="conv_bwd",
    )(proj, proj, proj, proj, conv_w, dc, dproj)


MEM_CHUNK = 1024


def _mem_fwd(proj, mkv, gq, gk, B, S):
    T, NC = proj.shape
    scale = MEM_HD ** -0.5

    def body(q_ref, z_ref, k_ref, v_ref, gq_ref, gk_ref, o_ref):
        kv = k_ref[...]
        kn = (kv * _rstd_head(kv) * gk_ref[...]).astype(bf16)
        vv = v_ref[...].astype(bf16)

        def chunk(c, carry):
            rows = pl.ds(pl.multiple_of(c * MEM_CHUNK, MEM_CHUNK), MEM_CHUNK)
            q = q_ref[rows, :].astype(f32)
            qn = q * _rstd(q) * gq_ref[...]
            s = _dot(qn, kn, NT_DIMS) * scale
            p = jnp.exp(s - jnp.max(s, axis=-1, keepdims=True))
            p = p / jnp.sum(p, axis=-1, keepdims=True)
            z = z_ref[rows, :].astype(f32)
            o_ref[rows, :] = (_dot(p, vv, NN_DIMS) * (z * _sigmoid(z))).astype(bf16)
            return carry

        lax.fori_loop(0, S // MEM_CHUNK, chunk, 0)

    gain = pl.BlockSpec((1, MEM_HD), lambda b, h: (0, 0))
    return pl.pallas_call(
        body, grid=(B, MEM_HEADS),
        in_specs=[pl.BlockSpec((S, MEM_HD), lambda b, h: (b, MQ // MEM_HD + h)),
                  pl.BlockSpec((S, MEM_HD), lambda b, h: (b, ZM // MEM_HD + h)),
                  pl.BlockSpec((MEM_LEN, MEM_HD), lambda b, h: (b, h)),
                  pl.BlockSpec((MEM_LEN, MEM_HD), lambda b, h: (b, MEM_HEADS + h)), gain, gain],
        out_specs=pl.BlockSpec((S, MEM_HD), lambda b, h: (b, h)),
        out_shape=jax.ShapeDtypeStruct((T, MEM_W), bf16),
        compiler_params=_params(("parallel", "parallel")), name="mem_fwd",
    )(proj, proj, mkv, mkv, gq, gk)


def _mem_bwd(proj, mkv, gq, gk, dmo, dproj, B, S):
    T, NC = proj.shape
    scale = MEM_HD ** -0.5

    def body(q_ref, z_ref, k_ref, v_ref, gq_ref, gk_ref, dmo_ref, dproj_in, dproj_ref, dmk_ref, dmv_ref, dgq_ref, dgk_ref,
             dkn_s, dv_s, gq_s, stage, sem):
        b, h = pl.program_id(0), pl.program_id(1)

        def dst(c):
            return dproj_ref.at[pl.ds(b * S, S), pl.ds((MQ, ZM)[c] + h * MEM_HD, MEM_HD)]

        out = _Deferred(stage, sem, b * MEM_HEADS + h, B * MEM_HEADS - 1, 2)
        out.begin(dst)
        kv = k_ref[...]
        kn = (kv * _rstd_head(kv) * gk_ref[...]).astype(bf16)
        vv = v_ref[...].astype(bf16)
        dkn_s[...] = jnp.zeros_like(dkn_s)
        dv_s[...] = jnp.zeros_like(dv_s)
        gq_s[...] = jnp.zeros_like(gq_s)

        def chunk(c, carry):
            rows = pl.ds(pl.multiple_of(c * MEM_CHUNK, MEM_CHUNK), MEM_CHUNK)
            q = q_ref[rows, :].astype(f32)
            qn = q * _rstd(q) * gq_ref[...]
            s = _dot(qn, kn, NT_DIMS) * scale
            p = jnp.exp(s - jnp.max(s, axis=-1, keepdims=True))
            p = p / jnp.sum(p, axis=-1, keepdims=True)
            mo = _dot(p, vv, NN_DIMS)
            z, dg_ = z_ref[rows, :].astype(f32), dmo_ref[rows, :].astype(f32)
            sg = _sigmoid(z)
            do = dg_ * (z * sg)
            stage[out.slot, 1, rows, :] = (dg_ * mo * (sg * (1.0 + z * (1.0 - sg)))).astype(bf16)
            dp = _dot(do, vv, NT_DIMS)
            ds = p * (dp - jnp.sum(do * mo, axis=-1, keepdims=True)) * scale
            dq, gq_p = _rms_bwd_rows(q, gq_ref[...], _dot(ds, kn, NN_DIMS), on_mxu=False)
            stage[out.slot, 0, rows, :] = dq.astype(bf16)
            gq_s[...] += gq_p
            dkn_s[...] += _dot(ds, qn, TN_DIMS)
            dv_s[...] += _dot(p, do, TN_DIMS)
            return carry

        lax.fori_loop(0, S // MEM_CHUNK, chunk, 0)
        for c in range(2):
            out.start(c, dst(c))
        dk, gk_p = _rms_bwd_rows(kv, gk_ref[...], dkn_s[...])
        dmk_ref[...] = dk
        dmv_ref[...] = dv_s[...]

        @pl.when((b == 0) & (h == 0))
        def _():
            dgq_ref[...] = gq_s[...]
            dgk_ref[...] = gk_p

        @pl.when((b > 0) | (h > 0))
        def _():
            dgq_ref[...] += gq_s[...]
            dgk_ref[...] += gk_p

        out.end(dst)

    gain = pl.BlockSpec((1, MEM_HD), lambda b, h: (0, 0))
    kvb = pl.BlockSpec((MEM_LEN, MEM_HD), lambda b, h: (b, h))
    return pl.pallas_call(
        body, grid=(B, MEM_HEADS),
        in_specs=[pl.BlockSpec((S, MEM_HD), lambda b, h: (b, MQ // MEM_HD + h)),
                  pl.BlockSpec((S, MEM_HD), lambda b, h: (b, ZM // MEM_HD + h)),
                  kvb, pl.BlockSpec((MEM_LEN, MEM_HD), lambda b, h: (b, MEM_HEADS + h)), gain, gain,
                  pl.BlockSpec((S, MEM_HD), lambda b, h: (b, h)), pl.BlockSpec(memory_space=pl.ANY)],
        out_specs=[pl.BlockSpec(memory_space=pl.ANY), kvb, kvb, gain, gain],
        out_shape=[jax.ShapeDtypeStruct(dproj.shape, dproj.dtype), jax.ShapeDtypeStruct((B * MEM_LEN, MEM_W), f32),
                   jax.ShapeDtypeStruct((B * MEM_LEN, MEM_W), f32), jax.ShapeDtypeStruct((1, MEM_HD), f32),
                   jax.ShapeDtypeStruct((1, MEM_HD), f32)],
        scratch_shapes=[pltpu.VMEM((MEM_LEN, MEM_HD), f32), pltpu.VMEM((MEM_LEN, MEM_HD), f32), pltpu.VMEM((1, MEM_HD), f32),
                        pltpu.VMEM((2, 2, S, MEM_HD), bf16), pltpu.SemaphoreType.DMA((2, 2))],
        input_output_aliases={7: 0},
        compiler_params=_params(("arbitrary", "arbitrary")), name="mem_bwd",
    )(proj, proj, mkv, mkv, gq, gk, dmo, dproj)


def _merge_fwd(proj, ag, cg, mg, wa, wc, wm):
    T, NC = proj.shape
    D = wa.shape[1]
    tm, tn = _pick(T, 1024), _pick(D, 512)
    assert GT % tn == 0

    def body(ag_ref, cg_ref, mg_ref, wa_ref, wc_ref, wm_ref, g0_ref, g1_ref, g2_ref, mer_ref, ba_ref, bc_ref, bm_ref):
        ba = _dot(ag_ref[...], wa_ref[...], NN_DIMS)
        bc = _dot(cg_ref[...], wc_ref[...], NN_DIMS)
        bm = _dot(mg_ref[...], wm_ref[...], NN_DIMS)
        s0, s1, s2 = (_sigmoid(r[...].astype(f32)) for r in (g0_ref, g1_ref, g2_ref))
        mer_ref[...] = (s0 * ba + s1 * bc + s2 * bm).astype(bf16)
        ba_ref[...] = ba.astype(bf16)
        bc_ref[...] = bc.astype(bf16)
        bm_ref[...] = bm.astype(bf16)

    def act(w):
        return pl.BlockSpec((tm, w), lambda i, j: (i, 0))

    def wt(k):
        return pl.BlockSpec((k, tn), lambda i, j: (0, j))

    def gate(n):
        return pl.BlockSpec((tm, tn), lambda i, j: (i, (GT + n * D) // tn + j))

    out = pl.BlockSpec((tm, tn), lambda i, j: (i, j))
    return pl.pallas_call(
        body, grid=(T // tm, D // tn),
        in_specs=[act(ATTN_OUT), act(CONV_W), act(MEM_W), wt(ATTN_OUT), wt(CONV_W), wt(MEM_W), gate(0), gate(1), gate(2)],
        out_specs=[out] * 4, out_shape=[jax.ShapeDtypeStruct((T, D), bf16)] * 4,
        compiler_params=_params(("parallel", "parallel")), name="merge_fwd",
    )(ag, cg, mg, wa, wc, wm, proj, proj, proj)


def _out_loss(merged, w_out, x, tgt):
    T, D = x.shape
    tm, tn = _pick(T, 1024), _pick(D, 512)

    def body(m_ref, w_ref, x_ref, t_ref, dyb_ref, lp_ref):
        e = x_ref[...] + _dot(m_ref[...], w_ref[...], NN_DIMS) - t_ref[...]
        dyb_ref[...] = (e / D).astype(bf16)
        part = jnp.full((1, 8, LANES), jnp.sum(e * e), f32)

        @pl.when(pl.program_id(1) == 0)
        def _():
            lp_ref[...] = part

        @pl.when(pl.program_id(1) > 0)
        def _():
            lp_ref[...] += part

    tile = pl.BlockSpec((tm, tn), lambda i, j: (i, j))
    return pl.pallas_call(
        body, grid=(T // tm, D // tn),
        in_specs=[pl.BlockSpec((tm, D), lambda i, j: (i, 0)), pl.BlockSpec((D, tn), lambda i, j: (0, j)), tile, tile],
        out_specs=[tile, pl.BlockSpec((1, 8, LANES), lambda i, j: (i, 0, 0))],
        out_shape=[jax.ShapeDtypeStruct((T, D), bf16), jax.ShapeDtypeStruct((T // tm, 8, LANES), f32)],
        compiler_params=_params(("parallel", "arbitrary")), name="out_loss",
    )(merged, w_out, x, tgt)


def _merge_bwd(proj, dyb, w_out, ba, bc, bm):
    T, NC = proj.shape
    D = w_out.shape[0]
    tm, tn = _pick(T, 1024), _pick(D, 512)
    assert GT % tn == 0

    def body(dy_ref, w_ref, ba_ref, bc_ref, bm_ref, g0_ref, g1_ref, g2_ref, da_ref, dc_ref, dm_ref, dproj_ref, stage, sem):
        i, j = pl.program_id(0), pl.program_id(1)

        def dst(n):
            return dproj_ref.at[pl.ds(i * tm, tm), pl.ds(GT + n * D + j * tn, tn)]

        out = _Deferred(stage, sem, i * (D // tn) + j, (T // tm) * (D // tn) - 1, 3)
        out.begin(dst)
        dmer = _dot(dy_ref[...], w_ref[...], NT_DIMS)
        for n, (g_ref, br_ref, o_ref) in enumerate(((g0_ref, ba_ref, da_ref), (g1_ref, bc_ref, dc_ref), (g2_ref, bm_ref, dm_ref))):
            sg = _sigmoid(g_ref[...].astype(f32))
            o_ref[...] = (sg * dmer).astype(bf16)
            stage[out.slot, n] = (dmer * br_ref[...].astype(f32) * (sg * (1.0 - sg))).astype(bf16)
            out.start(n, dst(n))
        out.end(dst)

    tile = pl.BlockSpec((tm, tn), lambda i, j: (i, j))

    def gate(n):
        return pl.BlockSpec((tm, tn), lambda i, j: (i, (GT + n * D) // tn + j))

    return pl.pallas_call(
        body, grid=(T // tm, D // tn),
        in_specs=[pl.BlockSpec((tm, D), lambda i, j: (i, 0)), pl.BlockSpec((tn, D), lambda i, j: (j, 0)), tile, tile, tile,
                  gate(0), gate(1), gate(2)],
        out_specs=[tile, tile, tile, pl.BlockSpec(memory_space=pl.ANY)],
        out_shape=[jax.ShapeDtypeStruct((T, D), bf16)] * 3 + [jax.ShapeDtypeStruct((T, NC), bf16)],
        scratch_shapes=[pltpu.VMEM((2, 3, tm, tn), bf16), pltpu.SemaphoreType.DMA((2, 3))],
        compiler_params=_params(("arbitrary", "arbitrary")), name="merge_bwd",
    )(dyb, w_out, ba, bc, bm, proj, proj, proj)


def _fwd_bwd_head(x2, mem2, t2, proj, attn, B, S, mem_norm_g, attn_q_norm, attn_k_norm, conv_w, mem_q_norm, mem_k_norm,
                  w_kv, w_a, w_c, w_m, w_out):
    D = x2.shape[1]
    mng = mem_norm_g.reshape(1, D)
    mqn, mkn = mem_q_norm.reshape(1, MEM_HD), mem_k_norm.reshape(1, MEM_HD)
    ag, a_comb, lse = attn

    mh = _rms_fwd(mem2, mng, "rms_mem")
    mkv = _mm(mh, w_kv, mode="nn", out_dtype=f32, name="mkv")
    cg = _conv_fwd(proj, conv_w, B, S)
    mg = _mem_fwd(proj, mkv, mqn, mkn, B, S)
    merged, ba, bc, bm = _merge_fwd(proj, ag, cg, mg, w_a, w_c, w_m)
    dyb, lp = _out_loss(merged, w_out, x2, t2)
    sq_err = jnp.sum(lp[:, 0, 0])

    g = {}
    g["w_out"] = _mm(merged, dyb, mode="tn", out_dtype=DW_DTYPE, name="dw_out")
    dba, dbc, dbm, dproj = _merge_bwd(proj, dyb, w_out, ba, bc, bm)
    g["w_br_attn"] = _mm(ag, dba, mode="tn", out_dtype=DW_DTYPE, name="dw_br_attn")
    g["w_br_conv"] = _mm(cg, dbc, mode="tn", out_dtype=DW_DTYPE, name="dw_br_conv")
    g["w_br_mem"] = _mm(mg, dbm, mode="tn", out_dtype=DW_DTYPE, name="dw_br_mem")
    dag = _mm(dba, w_a, mode="nt", out_dtype=bf16, name="d_attn_out")
    dcg = _mm(dbc, w_c, mode="nt", out_dtype=bf16, name="d_conv_out")
    dmg = _mm(dbm, w_m, mode="nt", out_dtype=bf16, name="d_mem_out")
    dproj, g["attn_q_norm"], g["attn_k_norm"] = _attn_bwd(proj, attn_q_norm, attn_k_norm, a_comb, lse, dag, dproj, B, S)
    dproj, g["conv_w"] = _conv_bwd(proj, conv_w, dcg, dproj, B, S)
    dproj, dmk, dmv, dgmq, dgmk = _mem_bwd(proj, mkv, mqn, mkn, dmg, dproj, B, S)
    g["mem_q_norm"], g["mem_k_norm"] = dgmq.reshape(MEM_HD), dgmk.reshape(MEM_HD)
    dmkv = jnp.concatenate([dmk, dmv], axis=1)
    dmh = _mm(dmkv, w_kv, mode="nt", out_dtype=f32, name="d_mem_h")
    g["mem_norm_g"] = _rms_bwd(mem2, mng, dmh, None, "rms_mem_bwd").reshape(D)
    return sq_err, g, dict(dy=dyb, dproj=dproj, mh=mh, dmkv=dmkv)


MESH = pl.DeviceIdType.MESH
HBM = pl.BlockSpec(memory_space=pl.ANY)


def _me():
    x, y, c = lax.axis_index("x"), lax.axis_index("y"), lax.axis_index("c")
    return (x, y, c), 4 * x + 2 * y + c


def _peer(k):
    (x, y, c), _ = _me()
    p = (1 - x if k & 4 else x, 1 - y if k & 2 else y, 1 - c if k & 1 else c)
    return p, 4 * p[0] + 2 * p[1] + p[2]


def _window(ref, axis, size, idx):
    if axis is None:
        return ref.at[idx]
    if axis == 0:
        return ref.at[pl.ds(idx * size, size), :]
    return ref.at[:, pl.ds(idx * size, size)]


def _full_shape(s, axis):
    if axis is None:
        return (N_DEV,) + s.shape
    return tuple(N_DEV * d if i == axis else d for i, d in enumerate(s.shape))


OTHER_CHIPS = (4, 2, 6)


def _spread_comm(shards, axes):
    n = len(shards)

    def copies(ins, outs, send_sems, recv_sems, base=0):
        _, me = _me()
        out = []
        for a in range(n):
            mine = _window(outs[a], axes[a], None if axes[a] is None else shards[a].shape[axes[a]], me)
            out.append(pltpu.make_async_copy(ins[a], mine, send_sems.at[base + a, N_CHIP]))
            for k, dist in enumerate((1,) + OTHER_CHIPS):
                dev, _ = _peer(dist)
                out.append(pltpu.make_async_remote_copy(
                    src_ref=ins[a], dst_ref=mine, send_sem=send_sems.at[base + a, k], recv_sem=recv_sems.at[base + a, k],
                    device_id=dev, device_id_type=MESH))
        return out

    shapes = [jax.ShapeDtypeStruct(_full_shape(s, ax), s.dtype) for s, ax in zip(shards, axes)]
    return _Comm(shards, shapes, (n, N_CHIP + 1), copies)


def _forward_blocks(fulls, axes):
    n = len(fulls)
    sizes = [None if ax is None else f.shape[ax] // N_DEV for f, ax in zip(fulls, axes)]

    def body(*refs):
        outs = refs[n:2 * n]
        send_sems, recv_sems = refs[2 * n:]
        sibling, _ = _peer(1)
        copies = []
        for j, dist in enumerate(OTHER_CHIPS):
            _, held = _peer(dist)
            for a in range(n):
                block = _window(outs[a], axes[a], sizes[a], held)
                copies.append(pltpu.make_async_remote_copy(
                    src_ref=block, dst_ref=block, send_sem=send_sems.at[a, j], recv_sem=recv_sems.at[a, j],
                    device_id=sibling, device_id_type=MESH))
        for cp in copies:
            cp.start()
        for cp in copies:
            cp.wait()

    return pl.pallas_call(
        body, in_specs=[HBM] * n, out_specs=[HBM] * n,
        out_shape=[jax.ShapeDtypeStruct(f.shape, f.dtype) for f in fulls],
        scratch_shapes=[pltpu.SemaphoreType.DMA((n, len(OTHER_CHIPS))), pltpu.SemaphoreType.DMA((n, len(OTHER_CHIPS)))],
        input_output_aliases={a: a for a in range(n)},
        name="forward_blocks",
    )(*fulls)


def _shard_order():
    (x, y, c), me = _me()
    xs, ys, xo, yo = _peer(4)[1], _peer(2)[1], _peer(5)[1], _peer(3)[1]
    north = c == 1
    ids = [me, _peer(1)[1], jnp.where(north, xs, ys), jnp.where(north, yo, xo), jnp.where(north, ys, xs),
           jnp.where(north, xo, yo), _peer(6)[1], _peer(7)[1]]
    return jnp.stack(ids).astype(jnp.int32)


def _proj_gather(h, w_shard, order, comm, comm_at):
    T, K = h.shape
    C = w_shard.shape[1]
    tm = _pick(T, 1024)
    nm = T // tm
    ahead = max(nm - 2, 0)
    n_ci, n_co = len(comm.ins), len(comm.out_shapes)

    def body(*refs):
        order_ref, h_ref, ws_ref = refs[:3]
        c_ins = refs[3:3 + n_ci]
        o_ref, wf_ref = refs[3 + n_ci:5 + n_ci]
        c_outs = refs[5 + n_ci:5 + n_ci + n_co]
        wbuf, send_sems, recv_sems, own_sem, buf_sems, c_send, c_recv = refs[5 + n_ci + n_co:]
        t, i = pl.program_id(0), pl.program_id(1)

        @pl.when((t == comm_at) & (i == 0))
        def _():
            for cp in comm.copies(c_ins, c_outs, c_send, c_recv):
                cp.start()
        (x, y, c), me = _me()
        sibling, sib_id = _peer(1)
        chips = [_peer(dist) for dist in OTHER_CHIPS]

        def win(idx):
            return wf_ref.at[:, pl.ds(idx * C, C)]

        def copy(k, block, to, src=None):
            return pltpu.make_async_remote_copy(
                src_ref=win(block) if src is None else src, dst_ref=win(block),
                send_sem=send_sems.at[k], recv_sem=recv_sems.at[k], device_id=to, device_id_type=MESH)

        def fetch(tt, src):
            return pltpu.make_async_copy(src, wbuf.at[tt % 2], buf_sems.at[tt % 2])

        own = pltpu.make_async_copy(ws_ref, win(me), own_sem)
        (x_nbr, x_id), (y_nbr, y_id), (_, d_id) = chips

        def half(k, block, top, to):
            rows = wf_ref.at[pl.ds(0 if top else K // 2, K // 2), pl.ds(block * C, C)]
            return pltpu.make_async_remote_copy(src_ref=rows, dst_ref=rows, send_sem=send_sems.at[k], recv_sem=recv_sems.at[k],
                                                device_id=to, device_id_type=MESH)

        here = (x, y, c)

        def pass_on(from_x):
            if from_x:
                copy(4, x_id, sibling).start()
                half(8, x_id, False, y_nbr).start()
            else:
                copy(5, y_id, sibling).start()
                half(7, y_id, True, x_nbr).start()

        @pl.when((t == 0) & (i == 0))
        def _():
            fetch(0, ws_ref).start()
            own.start()
            copy(0, me, sibling, src=ws_ref).start()

        for tt in range(N_DEV):
            @pl.when((t == tt) & (i == 0))
            def _(tt=tt):
                fetch(tt, ws_ref).wait()

        o_ref[...] = _dot(h_ref[...], wbuf[t % 2], NN_DIMS).astype(o_ref.dtype)

        def schedule(first_x):
            on = c == (1 if first_x else 0)
            (f_dev, f_id), (g_dev, g_id) = ((x_nbr, x_id), (y_nbr, y_id)) if first_x else ((y_nbr, y_id), (x_nbr, x_id))
            kf, kg = (1, 2) if first_x else (2, 1)
            pf, pg = (4, 5) if first_x else (5, 4)

            @pl.when((t == 0) & (i == 0) & on)
            def _():
                copy(kf, me, f_dev, src=ws_ref).start()

            def event(nxt):
                if nxt == 1:
                    copy(0, sib_id, here).wait_recv()
                    return sib_id
                if nxt == 2:
                    copy(kf, f_id, here).wait_recv()
                    copy(kf, me, f_dev, src=ws_ref).wait_send()
                    copy(kg, me, g_dev, src=ws_ref).start()
                    pass_on(first_x)
                    return f_id
                if nxt == 3:
                    copy(pg, g_id + 1 - 2 * c, here).wait_recv()
                    return g_id + 1 - 2 * c
                if nxt == 4:
                    copy(kg, g_id, here).wait_recv()
                    pass_on(not first_x)
                    return g_id
                if nxt == 5:
                    copy(pf, f_id + 1 - 2 * c, here).wait_recv()
                    return f_id + 1 - 2 * c
                if nxt == 6:
                    half(7, d_id, True, here).wait_recv()
                    half(8, d_id, False, here).wait_recv()
                    copy(6, d_id, sibling).start()
                    return d_id
                copy(6, d_id + 1 - 2 * c, here).wait_recv()
                return d_id + 1 - 2 * c

            for tt in range(N_DEV - 1):
                @pl.when((t == tt) & (i == ahead) & on)
                def _(tt=tt):
                    fetch(tt + 1, win(event(tt + 1))).start()

            @pl.when((t == N_DEV - 1) & (i == nm - 1) & on)
            def _():
                copy(kg, me, g_dev, src=ws_ref).wait_send()

        schedule(True)
        schedule(False)

        @pl.when((t == N_DEV - 1) & (i == nm - 1))
        def _():
            copy(0, me, sibling, src=ws_ref).wait_send()
            half(7, y_id, True, x_nbr).wait_send()
            half(8, x_id, False, y_nbr).wait_send()
            for j, (_, dev_id) in enumerate(chips):
                copy(4 + j, dev_id, sibling).wait_send()
            own.wait()
            for cp in comm.copies(c_ins, c_outs, c_send, c_recv):
                cp.wait()

    hbm = pl.BlockSpec(memory_space=pl.ANY)
    res = pl.pallas_call(
        body,
        grid_spec=pltpu.PrefetchScalarGridSpec(
            num_scalar_prefetch=1, grid=(N_DEV, nm),
            in_specs=[pl.BlockSpec((tm, K), lambda t, i, order_ref: (i, 0)), hbm] + [hbm] * n_ci,
            out_specs=[pl.BlockSpec((tm, C), lambda t, i, order_ref: (i, order_ref[t])), hbm] + [hbm] * n_co,
            scratch_shapes=[pltpu.VMEM((2, K, C), bf16), pltpu.SemaphoreType.DMA((9,)), pltpu.SemaphoreType.DMA((9,)),
                            pltpu.SemaphoreType.DMA, pltpu.SemaphoreType.DMA((2,))] + comm.scratch()),
        out_shape=[jax.ShapeDtypeStruct((T, N_DEV * C), PROJ_DTYPE), jax.ShapeDtypeStruct((K, N_DEV * C), bf16)] + comm.out_shapes,
        compiler_params=_params(("arbitrary", "arbitrary")), name="proj_gather",
    )(order, h, w_shard, *comm.ins)
    return res[0], res[1], list(res[2:])


N_CHIP = 4


def _shard_shape(g, ax):
    return tuple(d // N_DEV if i == ax else d for i, d in enumerate(g.shape))


def _sibling_comm(grads, axes):
    n = len(grads)
    sizes = [g.shape[ax] // N_DEV for g, ax in zip(grads, axes)]

    def copies(ins, lands, send_sems, recv_sems, base=0):
        sibling, _ = _peer(1)
        out = []
        for j in range(N_CHIP):
            _, owner = _peer(2 * j + 1)
            for a in range(n):
                out.append(pltpu.make_async_remote_copy(
                    src_ref=_window(ins[a], axes[a], sizes[a], owner), dst_ref=lands[a].at[j],
                    send_sem=send_sems.at[base + a, j], recv_sem=recv_sems.at[base + a, j], device_id=sibling,
                    device_id_type=MESH))
        return out

    shapes = [jax.ShapeDtypeStruct((N_CHIP,) + _shard_shape(g, ax), g.dtype) for g, ax in zip(grads, axes)]
    return _Comm(grads, shapes, (n, N_CHIP), copies)


def _chips_comm(sums):
    n = len(sums)

    def copies(ins, lands, send_sems, recv_sems, base=0):
        out = []
        for j in range(1, N_CHIP):
            dev, _ = _peer(2 * j)
            for a in range(n):
                out.append(pltpu.make_async_remote_copy(
                    src_ref=ins[a].at[j - 1], dst_ref=lands[a].at[j - 1],
                    send_sem=send_sems.at[base + a, j - 1], recv_sem=recv_sems.at[base + a, j - 1], device_id=dev,
                    device_id_type=MESH))
        return out

    return _Comm(sums, [jax.ShapeDtypeStruct(s.shape, s.dtype) for s in sums], (n, N_CHIP), copies)


def _join(c1, c2):
    n1, m1, k1 = len(c1.ins), len(c1.out_shapes), c1.sem_shape[0]

    def copies(ins, lands, send_sems, recv_sems):
        return (c1.copies(ins[:n1], lands[:m1], send_sems, recv_sems, base=0)
                + c2.copies(ins[n1:], lands[m1:], send_sems, recv_sems, base=k1))

    aliases = {**c1.aliases, **{n1 + k: m1 + v for k, v in c2.aliases.items()}}
    return _Comm(c1.ins + c2.ins, c1.out_shapes + c2.out_shapes, (k1 + c2.sem_shape[0], N_CHIP), copies, aliases)


def _chips_first_hop(sums):
    n = len(sums)

    def copies(ins, outs, send_sems, recv_sems, base=0):
        lands, relays = outs[:n], outs[n:]
        (y_dev, _), (x_dev, _) = _peer(2), _peer(4)
        out = []
        for a in range(n):
            half = sums[a].shape[1] // 2
            for k, (src, dst, dev) in enumerate((
                    (ins[a].at[0], lands[a].at[0], y_dev), (ins[a].at[1], lands[a].at[1], x_dev),
                    (ins[a].at[2, pl.ds(0, half)], relays[a].at[0], x_dev),
                    (ins[a].at[2, pl.ds(half, half)], relays[a].at[1], y_dev))):
                out.append(pltpu.make_async_remote_copy(
                    src_ref=src, dst_ref=dst, send_sem=send_sems.at[base + a, k], recv_sem=recv_sems.at[base + a, k],
                    device_id=dev, device_id_type=MESH))
        return out

    shapes = ([jax.ShapeDtypeStruct(s.shape, s.dtype) for s in sums]
              + [jax.ShapeDtypeStruct((2, s.shape[1] // 2) + s.shape[2:], s.dtype) for s in sums])
    return _Comm(sums, shapes, (n, N_CHIP), copies)


def _chips_relay(relays, lands):
    n = len(relays)

    def copies(ins, outs, send_sems, recv_sems, base=0):
        (y_dev, _), (x_dev, _) = _peer(2), _peer(4)
        out = []
        for a in range(n):
            half = relays[a].shape[1]
            for k, (rows, dev) in enumerate(((pl.ds(0, half), y_dev), (pl.ds(half, half), x_dev))):
                out.append(pltpu.make_async_remote_copy(
                    src_ref=ins[a].at[k], dst_ref=outs[a].at[2, rows], send_sem=send_sems.at[base + a, k],
                    recv_sem=recv_sems.at[base + a, k], device_id=dev, device_id_type=MESH))
        return out

    return _Comm(list(relays) + list(lands), [jax.ShapeDtypeStruct(l.shape, l.dtype) for l in lands], (n, N_CHIP), copies,
                 aliases={n + a: a for a in range(n)})


def _rs_chip_sum(grad, land, xyc, axis, name):
    R, C = land.shape[1:]
    tr = _pick(R, max(8, (640 * 1024) // C), 8)

    def body(xyc_ref, g_ref, l_ref, o_ref):
        o_ref[0] = (g_ref[...].astype(f32) + l_ref[0].astype(f32)).astype(bf16)

    def owner(j, xyc_ref):
        jj = j + 1
        return 4 * (xyc_ref[0] ^ (jj >> 1)) + 2 * (xyc_ref[1] ^ (jj & 1)) + xyc_ref[2]

    if axis == 0:
        g_spec = pl.BlockSpec((tr, C), lambda j, i, xyc_ref: (owner(j, xyc_ref) * (R // tr) + i, 0))
    else:
        g_spec = pl.BlockSpec((tr, C), lambda j, i, xyc_ref: (i, owner(j, xyc_ref)))
    return pl.pallas_call(
        body,
        grid_spec=pltpu.PrefetchScalarGridSpec(
            num_scalar_prefetch=1, grid=(N_CHIP - 1, R // tr),
            in_specs=[g_spec, pl.BlockSpec((1, tr, C), lambda j, i, xyc_ref: (j + 1, i, 0))],
            out_specs=pl.BlockSpec((1, tr, C), lambda j, i, xyc_ref: (j, i, 0))),
        out_shape=jax.ShapeDtypeStruct((N_CHIP - 1, R, C), bf16),
        compiler_params=_params(("parallel", "parallel")), name=name,
    )(xyc, grad, land)


def _allreduce_small(part):
    R = part.shape[0]

    def body(p_ref, o_ref, buf, send_sems, recv_sems):
        (x, y, c), me = _me()
        buf[me] = p_ref[...]
        copies = []
        for k in range(1, N_DEV):
            dev, dev_id = _peer(k)
            copies.append(pltpu.make_async_remote_copy(
                src_ref=p_ref, dst_ref=buf.at[me], send_sem=send_sems.at[k - 1], recv_sem=recv_sems.at[k - 1],
                device_id=dev, device_id_type=MESH))
        for cp in copies:
            cp.start()
        for k in range(1, N_DEV):
            _, dev_id = _peer(k)
            pltpu.make_async_remote_copy(
                src_ref=p_ref, dst_ref=buf.at[dev_id], send_sem=send_sems.at[k - 1], recv_sem=recv_sems.at[k - 1],
                device_id=(x, y, c), device_id_type=MESH).wait_recv()
        for cp in copies:
            cp.wait_send()
        acc = buf[0]
        for d in range(1, N_DEV):
            acc = acc + buf[d]
        o_ref[...] = acc

    return pl.pallas_call(
        body, in_specs=[pl.BlockSpec(memory_space=pltpu.VMEM)], out_specs=pl.BlockSpec(memory_space=pltpu.VMEM),
        out_shape=jax.ShapeDtypeStruct((R, LANES), f32),
        scratch_shapes=[pltpu.VMEM((N_DEV, R, LANES), f32), pltpu.SemaphoreType.DMA((N_DEV - 1,)), pltpu.SemaphoreType.DMA((N_DEV - 1,))],
        name="allreduce_small",
    )(part)


def _adamw_math(w, g, m, v):
    m2 = ADAM_B1 * m + (1.0 - ADAM_B1) * g
    v2 = ADAM_B2 * v + (1.0 - ADAM_B2) * (g * g)
    m_hat = m2 / (1.0 - ADAM_B1 ** ADAM_STEP)
    v_hat = v2 / (1.0 - ADAM_B2 ** ADAM_STEP)
    return -ADAM_LR * (m_hat / (jnp.sqrt(v_hat) + ADAM_EPS) + ADAM_WD * w), m2, v2


def _adamw_shard(grad, land_sib, land_chips, w, m, v, me, axis, name, row0=0, prev=None):
    R, C = land_sib.shape[1:]
    assert axis == 1 or R == w.shape[0]
    tr = _pick(R, max(8, (320 * 1024) // C), 8)
    r0 = row0 // tr
    n_prev = len(prev) if prev else 0

    def body(me_ref, g_ref, s_ref, l_ref, w_ref, m_ref, v_ref, *rest):
        go_ref, d_ref, mo_ref, vo_ref = rest[n_prev:]
        g = g_ref[...].astype(f32) + s_ref[0].astype(f32)
        for j in range(N_CHIP - 1):
            g = g + l_ref[j].astype(f32)
        d, m2, v2 = _adamw_math(w_ref[...], g, m_ref[...], v_ref[...])
        go_ref[...] = g
        d_ref[...] = d
        mo_ref[...] = m2
        vo_ref[...] = v2

    if axis == 0:
        g_spec = pl.BlockSpec((tr, C), lambda i, me_ref: (me_ref[0] * (R // tr) + i, 0))
    else:
        g_spec = pl.BlockSpec((tr, C), lambda i, me_ref: (i, me_ref[0]))
    blk = pl.BlockSpec((tr, C), lambda i, me_ref: (r0 + i, 0))
    return pl.pallas_call(
        body,
        grid_spec=pltpu.PrefetchScalarGridSpec(
            num_scalar_prefetch=1, grid=(R // tr,),
            in_specs=[g_spec, pl.BlockSpec((1, tr, C), lambda i, me_ref: (0, i, 0)),
                      pl.BlockSpec((N_CHIP - 1, tr, C), lambda i, me_ref: (0, i, 0)), blk, blk, blk]
            + [pl.BlockSpec(memory_space=pl.ANY)] * n_prev,
            out_specs=[blk] * 4),
        out_shape=[jax.ShapeDtypeStruct(w.shape, f32)] * 4,
        input_output_aliases={7 + i: i for i in range(n_prev)},
        compiler_params=_params(("parallel",)), name=name,
    )(me, grad, land_sib, land_chips, w, m, v, *(prev or []))


def _adamw_small(g, w, m, v):
    def body(g_ref, w_ref, m_ref, v_ref, d_ref, mo_ref, vo_ref):
        d, m2, v2 = _adamw_math(w_ref[...], g_ref[...], m_ref[...], v_ref[...])
        d_ref[...] = d
        mo_ref[...] = m2
        vo_ref[...] = v2

    return pl.pallas_call(body, out_shape=[jax.ShapeDtypeStruct(g.shape, f32)] * 3, name="adamw_small")(g, w, m, v)


def _pack_rows(parts, total_rows):
    rows = jnp.concatenate([p.reshape(-1, LANES) for p in parts], axis=0)
    return jnp.pad(rows, ((0, total_rows - rows.shape[0]), (0, 0)))


BIG = ("w_in", "mem_w_kv", "w_br_attn", "w_br_conv", "w_br_mem", "w_out")
BIG_AXIS = {"w_in": 1, "mem_w_kv": 0, "w_br_attn": 1, "w_br_conv": 1, "w_br_mem": 1, "w_out": 0}
SMALL = ("norm_g", "mem_norm_g", "attn_q_norm", "attn_k_norm", "mem_q_norm", "mem_k_norm")
ALL_W = ("norm_g", "mem_norm_g", "w_in", "attn_q_norm", "attn_k_norm", "conv_w", "mem_w_kv", "mem_q_norm", "mem_k_norm",
         "w_br_attn", "w_br_conv", "w_br_mem", "w_out")


def kernel(x, mem, norm_g, mem_norm_g, w_in, attn_q_norm, attn_k_norm, conv_w, mem_w_kv, mem_q_norm, mem_k_norm, w_br_attn, w_br_conv, w_br_mem, w_out, loss_target, m_norm_g, m_mem_norm_g, m_w_in, m_attn_q_norm, m_attn_k_norm, m_conv_w, m_mem_w_kv, m_mem_q_norm, m_mem_k_norm, m_w_br_attn, m_w_br_conv, m_w_br_mem, m_w_out, v_norm_g, v_mem_norm_g, v_w_in, v_attn_q_norm, v_attn_k_norm, v_conv_w, v_mem_w_kv, v_mem_q_norm, v_mem_k_norm, v_w_br_attn, v_w_br_conv, v_w_br_mem, v_w_out):
    w = dict(norm_g=norm_g, mem_norm_g=mem_norm_g, w_in=w_in, attn_q_norm=attn_q_norm, attn_k_norm=attn_k_norm, conv_w=conv_w,
             mem_w_kv=mem_w_kv, mem_q_norm=mem_q_norm, mem_k_norm=mem_k_norm, w_br_attn=w_br_attn, w_br_conv=w_br_conv,
             w_br_mem=w_br_mem, w_out=w_out)
    mo = dict(norm_g=m_norm_g, mem_norm_g=m_mem_norm_g, w_in=m_w_in, attn_q_norm=m_attn_q_norm, attn_k_norm=m_attn_k_norm,
              conv_w=m_conv_w, mem_w_kv=m_mem_w_kv, mem_q_norm=m_mem_q_norm, mem_k_norm=m_mem_k_norm, w_br_attn=m_w_br_attn,
              w_br_conv=m_w_br_conv, w_br_mem=m_w_br_mem, w_out=m_w_out)
    vo = dict(norm_g=v_norm_g, mem_norm_g=v_mem_norm_g, w_in=v_w_in, attn_q_norm=v_attn_q_norm, attn_k_norm=v_attn_k_norm,
              conv_w=v_conv_w, mem_w_kv=v_mem_w_kv, mem_q_norm=v_mem_q_norm, mem_k_norm=v_mem_k_norm, w_br_attn=v_w_br_attn,
              w_br_conv=v_w_br_conv, w_br_mem=v_w_br_mem, w_out=v_w_out)
    B, S, D = x.shape
    me = (4 * lax.axis_index("x") + 2 * lax.axis_index("y") + lax.axis_index("c")).astype(jnp.int32)

    T = B * S
    x2, t2, mem2, ng = x.reshape(T, D), loss_target.reshape(T, D), mem.reshape(B * MEM_LEN, D), norm_g.reshape(1, D)
    h = _rms_fwd(x2, ng, "rms_x")
    with_proj, with_attn = ("mem_w_kv",), ("w_out", "w_br_attn", "w_br_conv", "w_br_mem")
    later = with_proj + with_attn
    later_axes = [BIG_AXIS[n] for n in later] + [None]
    conv_pad = jnp.pad(conv_w, ((0, 8 - conv_w.shape[0]), (0, 0)))
    proj, w_in_full, spread_a = _proj_gather(
        h, w_in.astype(bf16), _shard_order(),
        _spread_comm([w[n].astype(bf16) for n in with_proj], [BIG_AXIS[n] for n in with_proj]), comm_at=N_DEV - 3)
    *attn, spread_b = _attn_fwd(proj, attn_q_norm, attn_k_norm, B, S,
                                comm=_spread_comm([w[n].astype(bf16) for n in with_attn] + [conv_pad],
                                                  [BIG_AXIS[n] for n in with_attn] + [None]))
    *fulls, conv_g = _forward_blocks(spread_a + spread_b, later_axes)
    wf = dict(zip(later, fulls), w_in=w_in_full)
    conv_full = conv_g[:, :3, :].transpose(1, 0, 2).reshape(3, CONV_W)

    sq_err, g, t = _fwd_bwd_head(x2, mem2, t2, proj, attn, B, S, mem_norm_g, attn_q_norm, attn_k_norm, conv_full, mem_q_norm,
                                 mem_k_norm, wf["mem_w_kv"], wf["w_br_attn"], wf["w_br_conv"], wf["w_br_mem"], wf["w_out"])
    t.update(x2=x2, ng=ng, h=h)

    xyc = jnp.stack([lax.axis_index("x"), lax.axis_index("y"), lax.axis_index("c")]).astype(jnp.int32)
    me1 = me.reshape(1)
    early = ("w_out", "w_br_attn", "w_br_conv", "w_br_mem")
    g["mem_w_kv"], sib_early = _mm(t["mh"], t["dmkv"], mode="tn", out_dtype=DW_DTYPE, name="dw_kv",
                                   comm=_sibling_comm([g[n] for n in early], [BIG_AXIS[n] for n in early]))
    sums_early = [_rs_chip_sum(g[n], ls, xyc, BIG_AXIS[n], "rs_sum_" + n) for n, ls in zip(early, sib_early)]
    tm_w = _pick(D // 2, 1024)
    half = (D // 2) // tm_w
    g_top, (*chips_early, sib_kv) = _mm(t["h"], t["dproj"], mode="tn", out_dtype=DW_DTYPE, name="dw_in_top", tm=tm_w,
                                        m_tiles=(0, half),
                                        comm=_join(_chips_comm(sums_early), _sibling_comm([g["mem_w_kv"]], [0])))
    sum_kv = _rs_chip_sum(g["mem_w_kv"], sib_kv, xyc, 0, "rs_sum_mem_w_kv")
    g_bot, (chips_kv, sib_top) = _mm(t["h"], t["dproj"], mode="tn", out_dtype=DW_DTYPE, name="dw_in_bot", tm=tm_w,
                                     m_tiles=(half, half), comm=_join(_chips_comm([sum_kv]), _sibling_comm([g_top], [1])))
    sum_top = _rs_chip_sum(g_top, sib_top, xyc, 1, "rs_sum_w_in_top")
    tm_t = _pick(T // 2, 1024)
    halfm = (T // 2) // tm_t
    dh, (part_top, relay_top, sib_bot) = _mm(t["dproj"], wf["w_in"], mode="nt", out_dtype=f32, name="d_h_a", tm=tm_t, tk=D_H_TK,
                                             m_tiles=(0, halfm), into="new",
                                             comm=_join(_chips_first_hop([sum_top]), _sibling_comm([g_bot], [1])))
    sum_bot = _rs_chip_sum(g_bot, sib_bot, xyc, 1, "rs_sum_w_in_bot")
    dh, (part_bot, relay_bot, chips_top) = _mm(t["dproj"], wf["w_in"], mode="nt", out_dtype=f32, name="d_h_b", tm=tm_t,
                                               tk=D_H_TK, m_tiles=(halfm, halfm), into=dh,
                                               comm=_join(_chips_first_hop([sum_bot]), _chips_relay([relay_top], [part_top])))
    dx, dng, (chips_bot,) = _rms_bwd(t["x2"], t["ng"], dh, t["dy"], "rms_x_bwd", comm=_chips_relay([relay_bot], [part_bot]))
    g["norm_g"] = dng.reshape(D)

    grad, delta, new_m, new_v = {}, {}, {}, {}
    for n, ls, lc in zip(early + ("mem_w_kv",), sib_early + [sib_kv], chips_early + [chips_kv]):
        grad[n], delta[n], new_m[n], new_v[n] = _adamw_shard(g[n], ls, lc, w[n], mo[n], vo[n], me1, BIG_AXIS[n], "adamw_" + n)
    top = _adamw_shard(g_top, sib_top, chips_top, w_in, m_w_in, v_w_in, me1, 1, "adamw_w_in_top")
    grad["w_in"], delta["w_in"], new_m["w_in"], new_v["w_in"] = _adamw_shard(
        g_bot, sib_bot, chips_bot, w_in, m_w_in, v_w_in, me1, 1, "adamw_w_in_bot", row0=D // 2, prev=top)

    n_rep = sum(w[n].size for n in SMALL) // LANES
    conv_rows = g["conv_w"].reshape(3, N_DEV, LANES).transpose(1, 0, 2).reshape(3 * N_DEV, LANES)
    n_rows = -(-(n_rep + 3 * N_DEV + 1) // 8) * 8
    part = _pack_rows([g[n] for n in SMALL] + [conv_rows, jnp.full((1, LANES), sq_err, f32)], n_rows)
    tot = _allreduce_small(part)
    loss = tot[n_rep + 3 * N_DEV, 0] * (0.5 / D)
    n_small = -(-(n_rep + 3) // 8) * 8
    g_small = _pack_rows([tot[:n_rep], lax.dynamic_slice(tot, (n_rep + 3 * me, 0), (3, LANES))], n_small)
    names = SMALL + ("conv_w",)
    d_s, m_s, v_s = _adamw_small(g_small, _pack_rows([w[n] for n in names], n_small), _pack_rows([mo[n] for n in names], n_small),
                                 _pack_rows([vo[n] for n in names], n_small))
    off = 0
    for n in names:
        r = w[n].size // LANES
        grad[n], delta[n] = g_small[off:off + r].reshape(w[n].shape), d_s[off:off + r].reshape(w[n].shape)
        new_m[n], new_v[n] = m_s[off:off + r].reshape(w[n].shape), v_s[off:off + r].reshape(w[n].shape)
        off += r
    return (loss, dx.reshape(B, S, D), *[grad[n] for n in ALL_W], *[delta[n] for n in ALL_W], *[new_m[n] for n in ALL_W],
            *[new_v[n] for n in ALL_W])
```

```python
import functools

import jax
import jax.numpy as jnp
from jax import lax
from jax.experimental import pallas as pl
from jax.experimental.pallas import tpu as pltpu

f32 = jnp.float32
bf16 = jnp.bfloat16

N_DEV = 8
HEAD_DIM = 128
DILATIONS = (1, 4, 16)
N_GROUPS = 3
HEADS = 4
BLK = 128
ATTN_QKV = N_GROUPS * HEADS * HEAD_DIM
ATTN_OUT = HEADS * HEAD_DIM
CONV_W = 1024
MEM_LEN = 256
MEM_HEADS = 4
MEM_HD = 256
MEM_W = MEM_HEADS * MEM_HD
EPS = 1e-6
Q0, K0, V0 = 0, ATTN_QKV, 2 * ATTN_QKV
ZA = 3 * ATTN_QKV
CB, CC, CV, ZC = ZA + ATTN_OUT, ZA + ATTN_OUT + CONV_W, ZA + ATTN_OUT + 2 * CONV_W, ZA + ATTN_OUT + 3 * CONV_W
MQ = ZC + CONV_W
ZM = MQ + MEM_W
GT = ZM + MEM_W

ADAM_LR, ADAM_B1, ADAM_B2, ADAM_EPS, ADAM_WD, ADAM_STEP = 0.001, 0.9, 0.999, 1e-08, 0.01, 10

VMEM_LIMIT = 56 * 1024 * 1024
D_H_TK = 4352
PROJ_DTYPE = bf16
DW_DTYPE = bf16
LANES = 128

NT_DIMS = (((1,), (1,)), ((), ()))
TN_DIMS = (((0,), (0,)), ((), ()))
NN_DIMS = (((1,), (0,)), ((), ()))


def _params(sem=None):
    return pltpu.CompilerParams(dimension_semantics=sem, vmem_limit_bytes=VMEM_LIMIT)


def _pick(n, pref, q=LANES):
    t = (min(pref, n) // q) * q
    while t >= q:
        if n % t == 0:
            return t
        t -= q
    return n


def _dot(a, b, dims):
    return lax.dot_general(a.astype(bf16), b.astype(bf16), dims, preferred_element_type=f32)


def _sigmoid(z):
    return 0.5 * jnp.tanh(0.5 * z) + 0.5


def _rstd(v):
    return lax.rsqrt(jnp.mean(v * v, axis=-1, keepdims=True) + EPS)


class _Deferred:
    def __init__(self, stage, sems, step, last, n):
        assert last >= 1
        self.stage, self.sems, self.step, self.last, self.n = stage, sems, step, last, n
        self.slot = step % 2

    def _drain(self, slot, like):
        for c in range(self.n):
            pltpu.make_async_copy(self.stage.at[slot, c], like(c), self.sems.at[slot, c]).wait()

    def begin(self, like):
        pl.when(self.step >= 2)(lambda: self._drain(self.slot, like))

    def start(self, c, dst):
        pltpu.make_async_copy(self.stage.at[self.slot, c], dst, self.sems.at[self.slot, c]).start()

    def end(self, like):
        @pl.when(self.step == self.last)
        def _():
            self._drain(self.slot, like)
            self._drain(1 - self.slot, like)


def _row_sum(v):
    return _dot(v, jnp.ones((v.shape[1], v.shape[1]), bf16), NN_DIMS)


def _rstd_head(v):
    return lax.rsqrt(_row_sum(v * v) * (1.0 / v.shape[1]) + EPS)


class _Comm:
    def __init__(self, ins, out_shapes, sem_shape, copies, aliases=None):
        self.ins, self.out_shapes, self.sem_shape, self.copies = list(ins), list(out_shapes), sem_shape, copies
        self.aliases = dict(aliases or {})

    def scratch(self):
        return [pltpu.SemaphoreType.DMA(self.sem_shape), pltpu.SemaphoreType.DMA(self.sem_shape)]

    def io_aliases(self, first_in, first_out):
        return {first_in + k: first_out + v for k, v in self.aliases.items()}


def _mm(a, b, *, mode, out_dtype, name, tm=1024, tn=1024, tk=4096, m_tiles=None, into=None, comm=None):
    if mode == "nn":
        (M, K), (K2, N) = a.shape, b.shape
    elif mode == "nt":
        (M, K), (N, K2) = a.shape, b.shape
    else:
        (K, M), (K2, N) = a.shape, b.shape
    assert K == K2
    tm, tn, tk = _pick(M, tm), _pick(N, tn), _pick(K, tk)
    nk = K // tk
    m0, mc = m_tiles if m_tiles is not None else (0, M // tm)
    o0 = 0 if into is None else m0
    aliased = into is not None and not isinstance(into, str)
    n_ci = len(comm.ins) if comm else 0
    n_co = len(comm.out_shapes) if comm else 0
    grid = (mc, N // tn, nk)
    dims = {"nn": NN_DIMS, "nt": NT_DIMS, "tn": TN_DIMS}[mode]

    def body(*refs, nk):
        a_ref, b_ref = refs[:2]
        pos = 3 if aliased else 2
        c_ins, o_ref, c_outs = refs[pos:pos + n_ci], refs[pos + n_ci], refs[pos + n_ci + 1:pos + n_ci + 1 + n_co]
        scratch = refs[pos + n_ci + 1 + n_co:]
        ids = [pl.program_id(d) for d in range(3)]
        if comm:
            sems = scratch[-2:]

            @pl.when((ids[0] == 0) & (ids[1] == 0) & (ids[2] == 0))
            def _():
                for cp in comm.copies(c_ins, c_outs, *sems):
                    cp.start()

        if nk == 1:
            o_ref[...] = _dot(a_ref[...], b_ref[...], dims).astype(o_ref.dtype)
        else:
            acc_ref, k = scratch[0], ids[2]

            @pl.when(k == 0)
            def _():
                acc_ref[...] = _dot(a_ref[...], b_ref[...], dims)

            @pl.when((k > 0) & (k < nk - 1))
            def _():
                acc_ref[...] += _dot(a_ref[...], b_ref[...], dims)

            @pl.when(k == nk - 1)
            def _():
                o_ref[...] = (acc_ref[...] + _dot(a_ref[...], b_ref[...], dims)).astype(o_ref.dtype)

        if comm:
            @pl.when((ids[0] == grid[0] - 1) & (ids[1] == grid[1] - 1) & (ids[2] == grid[2] - 1))
            def _():
                for cp in comm.copies(c_ins, c_outs, *sems):
                    cp.wait()

    if mode == "tn":
        a_spec = pl.BlockSpec((tk, tm), lambda i, j, k: (k, m0 + i))
    else:
        a_spec = pl.BlockSpec((tm, tk), lambda i, j, k: (m0 + i, k))
    if mode == "nt":
        b_spec = pl.BlockSpec((tn, tk), lambda i, j, k: (j, k))
    else:
        b_spec = pl.BlockSpec((tk, tn), lambda i, j, k: (k, j))
    hbm = pl.BlockSpec(memory_space=pl.ANY)
    res = pl.pallas_call(
        functools.partial(body, nk=nk),
        grid=grid,
        in_specs=[a_spec, b_spec] + [hbm] * (aliased + n_ci),
        out_specs=[pl.BlockSpec((tm, tn), lambda i, j, k: (o0 + i, j))] + [hbm] * n_co,
        out_shape=[jax.ShapeDtypeStruct((mc * tm if into is None else M, N), out_dtype)] + (comm.out_shapes if comm else []),
        scratch_shapes=([pltpu.VMEM((tm, tn), f32)] if nk > 1 else []) + (comm.scratch() if comm else []),
        input_output_aliases={**({2: 0} if aliased else {}), **(comm.io_aliases(2 + aliased, 1) if comm else {})},
        compiler_params=_params(("arbitrary",) * 3 if comm else ("parallel", "parallel", "arbitrary")),
        name=name,
    )(a, b, *([into] if aliased else []), *(comm.ins if comm else []))
    return (res[0], list(res[1:])) if comm else res[0]


def _rms_fwd(x, g, name):
    T, D = x.shape
    tm = _pick(T, 256, 8)

    def body(x_ref, g_ref, h_ref):
        xv = x_ref[...]
        h_ref[...] = (xv * _rstd(xv) * g_ref[...]).astype(bf16)

    return pl.pallas_call(
        body, grid=(T // tm,),
        in_specs=[pl.BlockSpec((tm, D), lambda i: (i, 0)), pl.BlockSpec((1, D), lambda i: (0, 0))],
        out_specs=pl.BlockSpec((tm, D), lambda i: (i, 0)),
        out_shape=jax.ShapeDtypeStruct((T, D), bf16),
        compiler_params=_params(("parallel",)), name=name,
    )(x, g)


def _rms_bwd(x, g, dh, dy, name, comm=None):
    T, D = x.shape
    tm = _pick(T, 256, 8)
    with_dx = dy is not None
    n_ci = len(comm.ins) if comm else 0
    n_co = len(comm.out_shapes) if comm else 0

    def body(*refs):
        if with_dx:
            x_ref, g_ref, dh_ref, dy_ref = refs[:4]
            c_ins, (dx_ref, dg_ref) = refs[4:4 + n_ci], refs[4 + n_ci:6 + n_ci]
            c_outs, sems = refs[6 + n_ci:6 + n_ci + n_co], refs[6 + n_ci + n_co:]
        else:
            x_ref, g_ref, dh_ref, dg_ref = refs
        if comm:
            @pl.when(pl.program_id(0) == 0)
            def _():
                for cp in comm.copies(c_ins, c_outs, *sems):
                    cp.start()

        xv = x_ref[...]
        r = _rstd(xv)
        xh = xv * r
        dhv = dh_ref[...]
        part = jnp.sum(dhv * xh, axis=0, keepdims=True)

        @pl.when(pl.program_id(0) == 0)
        def _():
            dg_ref[...] = part

        @pl.when(pl.program_id(0) > 0)
        def _():
            dg_ref[...] += part

        if with_dx:
            dxh = dhv * g_ref[...]
            dx_ref[...] = dy_ref[...].astype(f32) + r * (dxh - xh * jnp.mean(dxh * xh, axis=-1, keepdims=True))

        if comm:
            @pl.when(pl.program_id(0) == T // tm - 1)
            def _():
                for cp in comm.copies(c_ins, c_outs, *sems):
                    cp.wait()

    row = pl.BlockSpec((tm, D), lambda i: (i, 0))
    vec = pl.BlockSpec((1, D), lambda i: (0, 0))
    hbm = pl.BlockSpec(memory_space=pl.ANY)
    if with_dx:
        res = pl.pallas_call(
            body, grid=(T // tm,), in_specs=[row, vec, row, row] + [hbm] * n_ci, out_specs=[row, vec] + [hbm] * n_co,
            out_shape=[jax.ShapeDtypeStruct((T, D), f32), jax.ShapeDtypeStruct((1, D), f32)] + (comm.out_shapes if comm else []),
            scratch_shapes=comm.scratch() if comm else [],
            input_output_aliases=comm.io_aliases(4, 2) if comm else {},
            compiler_params=_params(("arbitrary",)), name=name,
        )(x, g, dh, dy, *(comm.ins if comm else []))
        return (res[0], res[1], list(res[2:])) if comm else res
    return pl.pallas_call(
        body, grid=(T // tm,), in_specs=[row, vec, row], out_specs=vec,
        out_shape=jax.ShapeDtypeStruct((1, D), f32),
        compiler_params=_params(("arbitrary",)), name=name,
    )(x, g, dh)


MAX_UNIT_UNROLL = 6


def _unit_unroll(trips):
    return max(u for u in range(1, MAX_UNIT_UNROLL + 1) if trips % u == 0)


def _rows(start, size, stride):
    return pl.ds(start, size) if stride == 1 else pl.ds(start, size, stride=stride)


def _attn_units(S, d, first, prev):
    nb = S // d // BLK

    def do_first(r, c):
        first(_rows(r, BLK, d))
        return c

    lax.fori_loop(0, d, do_first, 0, unroll=_unit_unroll(d))
    if nb > 1:
        def do_prev(u, c):
            r = u // (nb - 1)
            b = u % (nb - 1) + 1
            base = r + d * b * BLK
            prev(_rows(base, BLK, d), _rows(base - d * BLK, 2 * BLK, d))
            return c

        lax.fori_loop(0, d * (nb - 1), do_prev, 0, unroll=_unit_unroll(d * (nb - 1)))


def _band_bias(nkeys):
    i = lax.broadcasted_iota(jnp.int32, (BLK, nkeys), 0)
    j = lax.broadcasted_iota(jnp.int32, (BLK, nkeys), 1)
    ok = (j <= i) if nkeys == BLK else ((j >= i) & (j <= i + BLK))
    return jnp.where(ok, 0.0, -jnp.inf).astype(f32)


def _widen_into(src_ref, dst_ref, S):
    for c in range(S // 256):
        rows = pl.ds(c * 256, 256)
        dst_ref[rows, :] = src_ref[rows, :].astype(f32)


def _normalize_into(src_ref, gain, dst_ref, S, rstd_ref=None):
    for c in range(S // 256):
        rows = pl.ds(c * 256, 256)
        v = src_ref[rows, :].astype(f32)
        r = _rstd_head(v)
        dst_ref[rows, :] = v * r * gain
        if rstd_ref is not None:
            rstd_ref[rows, :] = r


def _attn_fwd(proj, gq, gk, B, S, comm=None):
    T, NC = proj.shape
    scale = HEAD_DIM ** -0.5
    n_ci = len(comm.ins) if comm else 0
    n_co = len(comm.out_shapes) if comm else 0

    def body(*refs):
        q_ref, k_ref, v_ref, z_ref, gq_ref, gk_ref = refs[:6]
        c_ins = refs[6:6 + n_ci]
        ag_ref, a_ref, lse_ref = refs[6 + n_ci:9 + n_ci]
        c_outs = refs[9 + n_ci:9 + n_ci + n_co]
        qs, ks, vs, o0, o1, o2, l0, l1, l2, bias1, bias2 = refs[9 + n_ci + n_co:20 + n_ci + n_co]
        sems = refs[20 + n_ci + n_co:]
        g = pl.program_id(2)
        if comm:
            @pl.when((pl.program_id(0) == 0) & (pl.program_id(1) == 0) & (g == 0))
            def _():
                for cp in comm.copies(c_ins, c_outs, *sems):
                    cp.start()

        bias1[...] = _band_bias(BLK)
        bias2[...] = _band_bias(2 * BLK)
        _normalize_into(q_ref, gq_ref[pl.ds(g, 1), :], qs, S)
        _normalize_into(k_ref, gk_ref[pl.ds(g, 1), :], ks, S)
        _widen_into(v_ref, vs, S)
        o_s, l_s = (o0, o1, o2), (l0, l1, l2)

        def run(gi):
            def unit(qr, kr, nkeys):
                s = _dot(qs[qr, :], ks[kr, :], NT_DIMS) * scale + (bias1 if nkeys == BLK else bias2)[...]
                m = jnp.max(s, axis=-1, keepdims=True)
                p = jnp.exp(s - m)
                den = jnp.sum(p, axis=-1, keepdims=True)
                o_s[gi][qr, :] = _dot(p, vs[kr, :], NN_DIMS) * (1.0 / den)
                l_s[gi][qr, :] = jnp.broadcast_to(m + jnp.log(den), (BLK, HEAD_DIM))

            _attn_units(S, DILATIONS[gi], lambda qr: unit(qr, qr, BLK), lambda qr, kr: unit(qr, kr, 2 * BLK))

        for gi in range(N_GROUPS):
            pl.when(g == gi)(functools.partial(run, gi))

        @pl.when(g == N_GROUPS - 1)
        def _():
            for c in range(S // 256):
                rows = pl.ds(c * 256, 256)
                la, lb, lc = l0[rows, :], l1[rows, :], l2[rows, :]
                m = jnp.maximum(jnp.maximum(la, lb), lc)
                wa, wb, wc = jnp.exp(la - m), jnp.exp(lb - m), jnp.exp(lc - m)
                tot = wa + wb + wc
                a = (wa / tot) * o0[rows, :] + (wb / tot) * o1[rows, :] + (wc / tot) * o2[rows, :]
                z = z_ref[rows, :].astype(f32)
                a_ref[rows, :] = a
                lse_ref[rows, :] = m + jnp.log(tot)
                ag_ref[rows, :] = (a * (z * _sigmoid(z))).astype(bf16)

        if comm:
            @pl.when((pl.program_id(0) == B - 1) & (pl.program_id(1) == HEADS - 1) & (g == N_GROUPS - 1))
            def _():
                for cp in comm.copies(c_ins, c_outs, *sems):
                    cp.wait()

    def slab(col0):
        return pl.BlockSpec((S, HEAD_DIM), lambda b, h, g: (b, col0 // HEAD_DIM + g * HEADS + h))

    gain = pl.BlockSpec((N_GROUPS, HEAD_DIM), lambda b, h, g: (0, 0))
    out = pl.BlockSpec((S, HEAD_DIM), lambda b, h, g: (b, h))
    hbm = pl.BlockSpec(memory_space=pl.ANY)
    res = pl.pallas_call(
        body, grid=(B, HEADS, N_GROUPS),
        in_specs=[slab(Q0), slab(K0), slab(V0), pl.BlockSpec((S, HEAD_DIM), lambda b, h, g: (b, ZA // HEAD_DIM + h)), gain, gain]
        + [hbm] * n_ci,
        out_specs=[out, out, out] + [hbm] * n_co,
        out_shape=[jax.ShapeDtypeStruct((T, ATTN_OUT), bf16), jax.ShapeDtypeStruct((T, ATTN_OUT), f32),
                   jax.ShapeDtypeStruct((T, ATTN_OUT), f32)] + (comm.out_shapes if comm else []),
        scratch_shapes=[pltpu.VMEM((S, HEAD_DIM), f32)] * 9 + [pltpu.VMEM((BLK, BLK), f32), pltpu.VMEM((BLK, 2 * BLK), f32)]
        + (comm.scratch() if comm else []),
        compiler_params=_params(("arbitrary", "arbitrary", "arbitrary")), name="attn_fwd",
    )(proj, proj, proj, proj, gq, gk, *(comm.ins if comm else []))
    return res[0], res[1], res[2], list(res[3:])


def _rms_bwd_rows(raw, gain, dn, on_mxu=True, r=None):
    if r is None:
        r = _rstd_head(raw) if on_mxu else _rstd(raw)
    xh = raw * r
    dxh = dn * gain
    if on_mxu:
        mean = _row_sum(dxh * xh) * (1.0 / raw.shape[1])
    else:
        mean = jnp.mean(dxh * xh, axis=-1, keepdims=True)
    return r * (dxh - xh * mean), jnp.sum(dn * xh, axis=0, keepdims=True)


def _attn_bwd(proj, gq, gk, a_comb, lse, dag, dproj, B, S):
    T, NC = proj.shape
    scale = HEAD_DIM ** -0.5

    def body(q_ref, k_ref, v_ref, z_ref, gq_ref, gk_ref, a_ref, lse_ref, dag_ref, dproj_in, dproj_ref, dgq_ref, dgk_ref,
             qs, ks, da_s, dl_s, dq_s, dk_s, dv_s, rq_s, rk_s, vs, bias1, bias2, stage, dz_stage, sem, dz_sem):
        b, h, g = pl.program_id(0), pl.program_id(1), pl.program_id(2)
        bias1[...] = _band_bias(BLK)
        bias2[...] = _band_bias(2 * BLK)
        gq_row, gk_row = gq_ref[pl.ds(g, 1), :], gk_ref[pl.ds(g, 1), :]
        _normalize_into(q_ref, gq_row, qs, S, rq_s)
        _normalize_into(k_ref, gk_row, ks, S, rk_s)
        _widen_into(v_ref, vs, S)

        def dst(c):
            return dproj_ref.at[pl.ds(b * S, S), pl.ds((Q0, K0, V0)[c] + (g * HEADS + h) * HEAD_DIM, HEAD_DIM)]

        out = _Deferred(stage, sem, (b * HEADS + h) * N_GROUPS + g, B * HEADS * N_GROUPS - 1, 3)
        out.begin(dst)

        @pl.when((b == 0) & (h == 0) & (g == 0))
        def _():
            dgq_ref[...] = jnp.zeros_like(dgq_ref)
            dgk_ref[...] = jnp.zeros_like(dgk_ref)

        @pl.when(g == 0)
        def _():
            for c in range(S // 256):
                rows = pl.ds(c * 256, 256)
                z, a, dg_ = z_ref[rows, :].astype(f32), a_ref[rows, :], dag_ref[rows, :].astype(f32)
                sg = _sigmoid(z)
                da = dg_ * (z * sg)
                da_s[rows, :] = da
                dl_s[rows, :] = _row_sum(da * a)
                dz_stage[rows, :] = (dg_ * a * (sg * (1.0 + z * (1.0 - sg)))).astype(bf16)
            cp = pltpu.make_async_copy(dz_stage, dproj_ref.at[pl.ds(b * S, S), pl.ds(ZA + h * HEAD_DIM, HEAD_DIM)], dz_sem)
            cp.start()
            cp.wait()

        dk_s[...] = jnp.zeros_like(dk_s)
        dv_s[...] = jnp.zeros_like(dv_s)

        def run(gi):
            def unit(qr, kr, nkeys):
                q, k, v = qs[qr, :], ks[kr, :], vs[kr, :]
                s = _dot(q, k, NT_DIMS) * scale
                p = jnp.exp(s + (bias1 if nkeys == BLK else bias2)[...] - lse_ref[qr, :][:, :1])
                da = da_s[qr, :]
                dp = _dot(da, v, NT_DIMS)
                ds = p * (dp - dl_s[qr, :][:, :1]) * scale
                dq_s[qr, :] = _dot(ds, k, NN_DIMS)
                dk_s[kr, :] += _dot(ds, q, TN_DIMS)
                dv_s[kr, :] += _dot(p, da, TN_DIMS)

            _attn_units(S, DILATIONS[gi], lambda qr: unit(qr, qr, BLK), lambda qr, kr: unit(qr, kr, 2 * BLK))

        for gi in range(N_GROUPS):
            pl.when(g == gi)(functools.partial(run, gi))

        gq_acc = jnp.zeros((1, HEAD_DIM), f32)
        gk_acc = jnp.zeros((1, HEAD_DIM), f32)
        for c in range(S // 256):
            rows = pl.ds(c * 256, 256)
            dq, gq_p = _rms_bwd_rows(q_ref[rows, :].astype(f32), gq_row, dq_s[rows, :], r=rq_s[rows, :])
            dk, gk_p = _rms_bwd_rows(k_ref[rows, :].astype(f32), gk_row, dk_s[rows, :], r=rk_s[rows, :])
            gq_acc, gk_acc = gq_acc + gq_p, gk_acc + gk_p
            stage[out.slot, 0, rows, :] = dq.astype(bf16)
            stage[out.slot, 1, rows, :] = dk.astype(bf16)
            stage[out.slot, 2, rows, :] = dv_s[rows, :].astype(bf16)
        dgq_ref[pl.ds(g, 1), :] += gq_acc
        dgk_ref[pl.ds(g, 1), :] += gk_acc
        for c in range(3):
            out.start(c, dst(c))
        out.end(dst)

    def slab(col0):
        return pl.BlockSpec((S, HEAD_DIM), lambda b, h, g: (b, col0 // HEAD_DIM + g * HEADS + h))

    gain = pl.BlockSpec((N_GROUPS, HEAD_DIM), lambda b, h, g: (0, 0))
    per_slot = pl.BlockSpec((S, HEAD_DIM), lambda b, h, g: (b, h))
    return pl.pallas_call(
        body, grid=(B, HEADS, N_GROUPS),
        in_specs=[slab(Q0), slab(K0), slab(V0), pl.BlockSpec((S, HEAD_DIM), lambda b, h, g: (b, ZA // HEAD_DIM + h)), gain, gain,
                  per_slot, per_slot, per_slot, pl.BlockSpec(memory_space=pl.ANY)],
        out_specs=[pl.BlockSpec(memory_space=pl.ANY), gain, gain],
        out_shape=[jax.ShapeDtypeStruct(dproj.shape, dproj.dtype), jax.ShapeDtypeStruct((N_GROUPS, HEAD_DIM), f32),
                   jax.ShapeDtypeStruct((N_GROUPS, HEAD_DIM), f32)],
        scratch_shapes=[pltpu.VMEM((S, HEAD_DIM), f32)] * 10 + [pltpu.VMEM((BLK, BLK), f32), pltpu.VMEM((BLK, 2 * BLK), f32),
                                                                 pltpu.VMEM((2, 3, S, HEAD_DIM), bf16), pltpu.VMEM((S, HEAD_DIM), bf16),
                                                                pltpu.SemaphoreType.DMA((2, 3)), pltpu.SemaphoreType.DMA],
        input_output_aliases={9: 0},
        compiler_params=_params(("arbitrary", "arbitrary", "arbitrary")), name="attn_bwd",
    )(proj, proj, proj, proj, gq, gk, a_comb, lse, dag, dproj)


def _conv_fwd(proj, conv_w, B, S):
    T, NC = proj.shape
    tc = LANES

    def body(cb_ref, cc_ref, cv_ref, z_ref, w_ref, c_ref, ub):
        u = cc_ref[...].astype(f32) * cv_ref[...].astype(f32)
        ub[0:8, :] = jnp.zeros((8, tc), f32)
        ub[8:8 + S, :] = u
        w = w_ref[...]
        y = w[0:1] * u + w[1:2] * ub[pl.ds(7, S), :] + w[2:3] * ub[pl.ds(6, S), :]
        z = z_ref[...].astype(f32)
        c_ref[...] = (cb_ref[...].astype(f32) * y * (z * _sigmoid(z))).astype(bf16)

    def seg(col0):
        return pl.BlockSpec((S, tc), lambda j, b: (b, col0 // tc + j))

    return pl.pallas_call(
        body, grid=(CONV_W // tc, B),
        in_specs=[seg(CB), seg(CC), seg(CV), seg(ZC), pl.BlockSpec((3, tc), lambda j, b: (0, j))],
        out_specs=pl.BlockSpec((S, tc), lambda j, b: (b, j)),
        out_shape=jax.ShapeDtypeStruct((T, CONV_W), bf16),
        scratch_shapes=[pltpu.VMEM((S + 8, tc), f32)],
        compiler_params=_params(("parallel", "parallel")), name="conv_fwd",
    )(proj, proj, proj, proj, conv_w)


def _conv_bwd(proj, conv_w, dc, dproj, B, S):
    T, NC = proj.shape
    tc = LANES

    def body(cb_ref, cc_ref, cv_ref, z_ref, w_ref, dc_ref, dproj_in, dproj_ref, dw_ref, ub, db, stage, sem):
        j, b = pl.program_id(0), pl.program_id(1)

        def dst(c):
            return dproj_ref.at[pl.ds(b * S, S), pl.ds((CB, CC, CV, ZC)[c] + j * tc, tc)]

        out = _Deferred(stage, sem, j * B + b, (CONV_W // tc) * B - 1, 4)
        out.begin(dst)
        cb, cc, cv, z = (r[...].astype(f32) for r in (cb_ref, cc_ref, cv_ref, z_ref))
        dcv = dc_ref[...].astype(f32)
        u = cc * cv
        ub[0:8, :] = jnp.zeros((8, tc), f32)
        ub[8:8 + S, :] = u
        u1, u2 = ub[pl.ds(7, S), :], ub[pl.ds(6, S), :]
        w = w_ref[...]
        y = w[0:1] * u + w[1:2] * u1 + w[2:3] * u2
        sg = _sigmoid(z)
        si = z * sg
        dyv = dcv * cb * si
        db[0:S, :] = dyv
        db[S:S + 8, :] = jnp.zeros((8, tc), f32)
        du = w[0:1] * dyv + w[1:2] * db[pl.ds(1, S), :] + w[2:3] * db[pl.ds(2, S), :]
        stage[out.slot, 0] = (dcv * y * si).astype(bf16)
        stage[out.slot, 1] = (du * cv).astype(bf16)
        stage[out.slot, 2] = (du * cc).astype(bf16)
        stage[out.slot, 3] = (dcv * cb * y * (sg * (1.0 + z * (1.0 - sg)))).astype(bf16)
        for c in range(4):
            out.start(c, dst(c))
        part = jnp.concatenate([jnp.sum(dyv * u, axis=0, keepdims=True), jnp.sum(dyv * u1, axis=0, keepdims=True),
                                jnp.sum(dyv * u2, axis=0, keepdims=True)], axis=0)

        @pl.when(b == 0)
        def _():
            dw_ref[...] = part

        @pl.when(b > 0)
        def _():
            dw_ref[...] += part

        out.end(dst)

    def seg(col0):
        return pl.BlockSpec((S, tc), lambda j, b: (b, col0 // tc + j))

    return pl.pallas_call(
        body, grid=(CONV_W // tc, B),
        in_specs=[seg(CB), seg(CC), seg(CV), seg(ZC), pl.BlockSpec((3, tc), lambda j, b: (0, j)),
                  pl.BlockSpec((S, tc), lambda j, b: (b, j)), pl.BlockSpec(memory_space=pl.ANY)],
        out_specs=[pl.BlockSpec(memory_space=pl.ANY), pl.BlockSpec((3, tc), lambda j, b: (0, j))],
        out_shape=[jax.ShapeDtypeStruct(dproj.shape, dproj.dtype), jax.ShapeDtypeStruct((3, CONV_W), f32)],
        scratch_shapes=[pltpu.VMEM((S + 8, tc), f32), pltpu.VMEM((S + 8, tc), f32), pltpu.VMEM((2, 4, S, tc), bf16),
                        pltpu.SemaphoreType.DMA((2, 4))],
        input_output_aliases={6: 0},
        compiler_params=_params(("arbitrary", "arbitrary")), name="conv_bwd",
    )(proj, proj, proj, proj, conv_w, dc, dproj)


MEM_CHUNK = 1024


def _mem_fwd(proj, mkv, gq, gk, B, S):
    T, NC = proj.shape
    scale = MEM_HD ** -0.5

    def body(q_ref, z_ref, k_ref, v_ref, gq_ref, gk_ref, o_ref):
        kv = k_ref[...]
        kn = (kv * _rstd_head(kv) * gk_ref[...]).astype(bf16)
        vv = v_ref[...].astype(bf16)

        def chunk(c, carry):
            rows = pl.ds(pl.multiple_of(c * MEM_CHUNK, MEM_CHUNK), MEM_CHUNK)
            q = q_ref[rows, :].astype(f32)
            qn = q * _rstd(q) * gq_ref[...]
            s = _dot(qn, kn, NT_DIMS) * scale
            p = jnp.exp(s - jnp.max(s, axis=-1, keepdims=True))
            p = p / jnp.sum(p, axis=-1, keepdims=True)
            z = z_ref[rows, :].astype(f32)
            o_ref[rows, :] = (_dot(p, vv, NN_DIMS) * (z * _sigmoid(z))).astype(bf16)
            return carry

        lax.fori_loop(0, S // MEM_CHUNK, chunk, 0)

    gain = pl.BlockSpec((1, MEM_HD), lambda b, h: (0, 0))
    return pl.pallas_call(
        body, grid=(B, MEM_HEADS),
        in_specs=[pl.BlockSpec((S, MEM_HD), lambda b, h: (b, MQ // MEM_HD + h)),
                  pl.BlockSpec((S, MEM_HD), lambda b, h: (b, ZM // MEM_HD + h)),
                  pl.BlockSpec((MEM_LEN, MEM_HD), lambda b, h: (b, h)),
                  pl.BlockSpec((MEM_LEN, MEM_HD), lambda b, h: (b, MEM_HEADS + h)), gain, gain],
        out_specs=pl.BlockSpec((S, MEM_HD), lambda b, h: (b, h)),
        out_shape=jax.ShapeDtypeStruct((T, MEM_W), bf16),
        compiler_params=_params(("parallel", "parallel")), name="mem_fwd",
    )(proj, proj, mkv, mkv, gq, gk)


def _mem_bwd(proj, mkv, gq, gk, dmo, dproj, B, S):
    T, NC = proj.shape
    scale = MEM_HD ** -0.5

    def body(q_ref, z_ref, k_ref, v_ref, gq_ref, gk_ref, dmo_ref, dproj_in, dproj_ref, dmk_ref, dmv_ref, dgq_ref, dgk_ref,
             dkn_s, dv_s, gq_s, stage, sem):
        b, h = pl.program_id(0), pl.program_id(1)

        def dst(c):
            return dproj_ref.at[pl.ds(b * S, S), pl.ds((MQ, ZM)[c] + h * MEM_HD, MEM_HD)]

        out = _Deferred(stage, sem, b * MEM_HEADS + h, B * MEM_HEADS - 1, 2)
        out.begin(dst)
        kv = k_ref[...]
        kn = (kv * _rstd_head(kv) * gk_ref[...]).astype(bf16)
        vv = v_ref[...].astype(bf16)
        dkn_s[...] = jnp.zeros_like(dkn_s)
        dv_s[...] = jnp.zeros_like(dv_s)
        gq_s[...] = jnp.zeros_like(gq_s)

        def chunk(c, carry):
            rows = pl.ds(pl.multiple_of(c * MEM_CHUNK, MEM_CHUNK), MEM_CHUNK)
            q = q_ref[rows, :].astype(f32)
            qn = q * _rstd(q) * gq_ref[...]
            s = _dot(qn, kn, NT_DIMS) * scale
            p = jnp.exp(s - jnp.max(s, axis=-1, keepdims=True))
            p = p / jnp.sum(p, axis=-1, keepdims=True)
            mo = _dot(p, vv, NN_DIMS)
            z, dg_ = z_ref[rows, :].astype(f32), dmo_ref[rows, :].astype(f32)
            sg = _sigmoid(z)
            do = dg_ * (z * sg)
            stage[out.slot, 1, rows, :] = (dg_ * mo * (sg * (1.0 + z * (1.0 - sg)))).astype(bf16)
            dp = _dot(do, vv, NT_DIMS)
            ds = p * (dp - jnp.sum(do * mo, axis=-1, keepdims=True)) * scale
            dq, gq_p = _rms_bwd_rows(q, gq_ref[...], _dot(ds, kn, NN_DIMS), on_mxu=False)
            stage[out.slot, 0, rows, :] = dq.astype(bf16)
            gq_s[...] += gq_p
            dkn_s[...] += _dot(ds, qn, TN_DIMS)
            dv_s[...] += _dot(p, do, TN_DIMS)
            return carry

        lax.fori_loop(0, S // MEM_CHUNK, chunk, 0)
        for c in range(2):
            out.start(c, dst(c))
        dk, gk_p = _rms_bwd_rows(kv, gk_ref[...], dkn_s[...])
        dmk_ref[...] = dk
        dmv_ref[...] = dv_s[...]

        @pl.when((b == 0) & (h == 0))
        def _():
            dgq_ref[...] = gq_s[...]
            dgk_ref[...] = gk_p

        @pl.when((b > 0) | (h > 0))
        def _():
            dgq_ref[...] += gq_s[...]
            dgk_ref[...] += gk_p

        out.end(dst)

    gain = pl.BlockSpec((1, MEM_HD), lambda b, h: (0, 0))
    kvb = pl.BlockSpec((MEM_LEN, MEM_HD), lambda b, h: (b, h))
    return pl.pallas_call(
        body, grid=(B, MEM_HEADS),
        in_specs=[pl.BlockSpec((S, MEM_HD), lambda b, h: (b, MQ // MEM_HD + h)),
                  pl.BlockSpec((S, MEM_HD), lambda b, h: (b, ZM // MEM_HD + h)),
                  kvb, pl.BlockSpec((MEM_LEN, MEM_HD), lambda b, h: (b, MEM_HEADS + h)), gain, gain,
                  pl.BlockSpec((S, MEM_HD), lambda b, h: (b, h)), pl.BlockSpec(memory_space=pl.ANY)],
        out_specs=[pl.BlockSpec(memory_space=pl.ANY), kvb, kvb, gain, gain],
        out_shape=[jax.ShapeDtypeStruct(dproj.shape, dproj.dtype), jax.ShapeDtypeStruct((B * MEM_LEN, MEM_W), f32),
                   jax.ShapeDtypeStruct((B * MEM_LEN, MEM_W), f32), jax.ShapeDtypeStruct((1, MEM_HD), f32),
                   jax.ShapeDtypeStruct((1, MEM_HD), f32)],
        scratch_shapes=[pltpu.VMEM((MEM_LEN, MEM_HD), f32), pltpu.VMEM((MEM_LEN, MEM_HD), f32), pltpu.VMEM((1, MEM_HD), f32),
                        pltpu.VMEM((2, 2, S, MEM_HD), bf16), pltpu.SemaphoreType.DMA((2, 2))],
        input_output_aliases={7: 0},
        compiler_params=_params(("arbitrary", "arbitrary")), name="mem_bwd",
    )(proj, proj, mkv, mkv, gq, gk, dmo, dproj)


def _merge_fwd(proj, ag, cg, mg, wa, wc, wm):
    T, NC = proj.shape
    D = wa.shape[1]
    tm, tn = _pick(T, 1024), _pick(D, 512)
    assert GT % tn == 0

    def body(ag_ref, cg_ref, mg_ref, wa_ref, wc_ref, wm_ref, g0_ref, g1_ref, g2_ref, mer_ref, ba_ref, bc_ref, bm_ref):
        ba = _dot(ag_ref[...], wa_ref[...], NN_DIMS)
        bc = _dot(cg_ref[...], wc_ref[...], NN_DIMS)
        bm = _dot(mg_ref[...], wm_ref[...], NN_DIMS)
        s0, s1, s2 = (_sigmoid(r[...].astype(f32)) for r in (g0_ref, g1_ref, g2_ref))
        mer_ref[...] = (s0 * ba + s1 * bc + s2 * bm).astype(bf16)
        ba_ref[...] = ba.astype(bf16)
        bc_ref[...] = bc.astype(bf16)
        bm_ref[...] = bm.astype(bf16)

    def act(w):
        return pl.BlockSpec((tm, w), lambda i, j: (i, 0))

    def wt(k):
        return pl.BlockSpec((k, tn), lambda i, j: (0, j))

    def gate(n):
        return pl.BlockSpec((tm, tn), lambda i, j: (i, (GT + n * D) // tn + j))

    out = pl.BlockSpec((tm, tn), lambda i, j: (i, j))
    return pl.pallas_call(
        body, grid=(T // tm, D // tn),
        in_specs=[act(ATTN_OUT), act(CONV_W), act(MEM_W), wt(ATTN_OUT), wt(CONV_W), wt(MEM_W), gate(0), gate(1), gate(2)],
        out_specs=[out] * 4, out_shape=[jax.ShapeDtypeStruct((T, D), bf16)] * 4,
        compiler_params=_params(("parallel", "parallel")), name="merge_fwd",
    )(ag, cg, mg, wa, wc, wm, proj, proj, proj)


def _out_loss(merged, w_out, x, tgt):
    T, D = x.shape
    tm, tn = _pick(T, 1024), _pick(D, 1024)

    def body(m_ref, w_ref, x_ref, t_ref, dyb_ref, lp_ref):
        e = x_ref[...] + _dot(m_ref[...], w_ref[...], NN_DIMS) - t_ref[...]
        dyb_ref[...] = (e / D).astype(bf16)
        part = jnp.full((1, 8, LANES), jnp.sum(e * e), f32)

        @pl.when(pl.program_id(1) == 0)
        def _():
            lp_ref[...] = part

        @pl.when(pl.program_id(1) > 0)
        def _():
            lp_ref[...] += part

    tile = pl.BlockSpec((tm, tn), lambda i, j: (i, j))
    return pl.pallas_call(
        body, grid=(T // tm, D // tn),
        in_specs=[pl.BlockSpec((tm, D), lambda i, j: (i, 0)), pl.BlockSpec((D, tn), lambda i, j: (0, j)), tile, tile],
        out_specs=[tile, pl.BlockSpec((1, 8, LANES), lambda i, j: (i, 0, 0))],
        out_shape=[jax.ShapeDtypeStruct((T, D), bf16), jax.ShapeDtypeStruct((T // tm, 8, LANES), f32)],
        compiler_params=_params(("parallel", "arbitrary")), name="out_loss",
    )(merged, w_out, x, tgt)


def _merge_bwd(proj, dyb, w_out, ba, bc, bm):
    T, NC = proj.shape
    D = w_out.shape[0]
    tm, tn = _pick(T, 1024), _pick(D, 512)
    assert GT % tn == 0

    def body(dy_ref, w_ref, ba_ref, bc_ref, bm_ref, g0_ref, g1_ref, g2_ref, da_ref, dc_ref, dm_ref, dproj_ref, stage, sem):
        i, j = pl.program_id(0), pl.program_id(1)

        def dst(n):
            return dproj_ref.at[pl.ds(i * tm, tm), pl.ds(GT + n * D + j * tn, tn)]

        out = _Deferred(stage, sem, i * (D // tn) + j, (T // tm) * (D // tn) - 1, 3)
        out.begin(dst)
        dmer = _dot(dy_ref[...], w_ref[...], NT_DIMS)
        for n, (g_ref, br_ref, o_ref) in enumerate(((g0_ref, ba_ref, da_ref), (g1_ref, bc_ref, dc_ref), (g2_ref, bm_ref, dm_ref))):
            sg = _sigmoid(g_ref[...].astype(f32))
            o_ref[...] = (sg * dmer).astype(bf16)
            stage[out.slot, n] = (dmer * br_ref[...].astype(f32) * (sg * (1.0 - sg))).astype(bf16)
            out.start(n, dst(n))
        out.end(dst)

    tile = pl.BlockSpec((tm, tn), lambda i, j: (i, j))

    def gate(n):
        return pl.BlockSpec((tm, tn), lambda i, j: (i, (GT + n * D) // tn + j))

    return pl.pallas_call(
        body, grid=(T // tm, D // tn),
        in_specs=[pl.BlockSpec((tm, D), lambda i, j: (i, 0)), pl.BlockSpec((tn, D), lambda i, j: (j, 0)), tile, tile, tile,
                  gate(0), gate(1), gate(2)],
        out_specs=[tile, tile, tile, pl.BlockSpec(memory_space=pl.ANY)],
        out_shape=[jax.ShapeDtypeStruct((T, D), bf16)] * 3 + [jax.ShapeDtypeStruct((T, NC), bf16)],
        scratch_shapes=[pltpu.VMEM((2, 3, tm, tn), bf16), pltpu.SemaphoreType.DMA((2, 3))],
        compiler_params=_params(("arbitrary", "arbitrary")), name="merge_bwd",
    )(dyb, w_out, ba, bc, bm, proj, proj, proj)


def _fwd_bwd_head(x2, mem2, t2, proj, attn, B, S, mem_norm_g, attn_q_norm, attn_k_norm, conv_w, mem_q_norm, mem_k_norm,
                  w_kv, w_a, w_c, w_m, w_out):
    D = x2.shape[1]
    mng = mem_norm_g.reshape(1, D)
    mqn, mkn = mem_q_norm.reshape(1, MEM_HD), mem_k_norm.reshape(1, MEM_HD)
    ag, a_comb, lse = attn

    mh = _rms_fwd(mem2, mng, "rms_mem")
    mkv = _mm(mh, w_kv, mode="nn", out_dtype=f32, name="mkv")
    cg = _conv_fwd(proj, conv_w, B, S)
    mg = _mem_fwd(proj, mkv, mqn, mkn, B, S)
    merged, ba, bc, bm = _merge_fwd(proj, ag, cg, mg, w_a, w_c, w_m)
    dyb, lp = _out_loss(merged, w_out, x2, t2)
    sq_err = jnp.sum(lp[:, 0, 0])

    g = {}
    g["w_out"] = _mm(merged, dyb, mode="tn", out_dtype=DW_DTYPE, name="dw_out")
    dba, dbc, dbm, dproj = _merge_bwd(proj, dyb, w_out, ba, bc, bm)
    g["w_br_attn"] = _mm(ag, dba, mode="tn", out_dtype=DW_DTYPE, name="dw_br_attn")
    g["w_br_conv"] = _mm(cg, dbc, mode="tn", out_dtype=DW_DTYPE, name="dw_br_conv")
    g["w_br_mem"] = _mm(mg, dbm, mode="tn", out_dtype=DW_DTYPE, name="dw_br_mem")
    dag = _mm(dba, w_a, mode="nt", out_dtype=bf16, name="d_attn_out")
    dcg = _mm(dbc, w_c, mode="nt", out_dtype=bf16, name="d_conv_out")
    dmg = _mm(dbm, w_m, mode="nt", out_dtype=bf16, name="d_mem_out")
    dproj, g["attn_q_norm"], g["attn_k_norm"] = _attn_bwd(proj, attn_q_norm, attn_k_norm, a_comb, lse, dag, dproj, B, S)
    dproj, g["conv_w"] = _conv_bwd(proj, conv_w, dcg, dproj, B, S)
    dproj, dmk, dmv, dgmq, dgmk = _mem_bwd(proj, mkv, mqn, mkn, dmg, dproj, B, S)
    g["mem_q_norm"], g["mem_k_norm"] = dgmq.reshape(MEM_HD), dgmk.reshape(MEM_HD)
    dmkv = jnp.concatenate([dmk, dmv], axis=1)
    dmh = _mm(dmkv, w_kv, mode="nt", out_dtype=f32, name="d_mem_h")
    g["mem_norm_g"] = _rms_bwd(mem2, mng, dmh, None, "rms_mem_bwd").reshape(D)
    return sq_err, g, dict(dy=dyb, dproj=dproj, mh=mh, dmkv=dmkv)


MESH = pl.DeviceIdType.MESH
HBM = pl.BlockSpec(memory_space=pl.ANY)


def _me():
    x, y, c = lax.axis_index("x"), lax.axis_index("y"), lax.axis_index("c")
    return (x, y, c), 4 * x + 2 * y + c


def _peer(k):
    (x, y, c), _ = _me()
    p = (1 - x if k & 4 else x, 1 - y if k & 2 else y, 1 - c if k & 1 else c)
    return p, 4 * p[0] + 2 * p[1] + p[2]


def _window(ref, axis, size, idx):
    if axis is None:
        return ref.at[idx]
    if axis == 0:
        return ref.at[pl.ds(idx * size, size), :]
    return ref.at[:, pl.ds(idx * size, size)]


def _full_shape(s, axis):
    if axis is None:
        return (N_DEV,) + s.shape
    return tuple(N_DEV * d if i == axis else d for i, d in enumerate(s.shape))


OTHER_CHIPS = (4, 2, 6)


def _spread_comm(shards, axes):
    n = len(shards)

    def copies(ins, outs, send_sems, recv_sems, base=0):
        _, me = _me()
        out = []
        for a in range(n):
            mine = _window(outs[a], axes[a], None if axes[a] is None else shards[a].shape[axes[a]], me)
            out.append(pltpu.make_async_copy(ins[a], mine, send_sems.at[base + a, N_CHIP]))
            for k, dist in enumerate((1,) + OTHER_CHIPS):
                dev, _ = _peer(dist)
                out.append(pltpu.make_async_remote_copy(
                    src_ref=ins[a], dst_ref=mine, send_sem=send_sems.at[base + a, k], recv_sem=recv_sems.at[base + a, k],
                    device_id=dev, device_id_type=MESH))
        return out

    shapes = [jax.ShapeDtypeStruct(_full_shape(s, ax), s.dtype) for s, ax in zip(shards, axes)]
    return _Comm(shards, shapes, (n, N_CHIP + 1), copies)


def _forward_blocks(fulls, axes):
    n = len(fulls)
    sizes = [None if ax is None else f.shape[ax] // N_DEV for f, ax in zip(fulls, axes)]

    def body(*refs):
        outs = refs[n:2 * n]
        send_sems, recv_sems = refs[2 * n:]
        sibling, _ = _peer(1)
        copies = []
        for j, dist in enumerate(OTHER_CHIPS):
            _, held = _peer(dist)
            for a in range(n):
                block = _window(outs[a], axes[a], sizes[a], held)
                copies.append(pltpu.make_async_remote_copy(
                    src_ref=block, dst_ref=block, send_sem=send_sems.at[a, j], recv_sem=recv_sems.at[a, j],
                    device_id=sibling, device_id_type=MESH))
        for cp in copies:
            cp.start()
        for cp in copies:
            cp.wait()

    return pl.pallas_call(
        body, in_specs=[HBM] * n, out_specs=[HBM] * n,
        out_shape=[jax.ShapeDtypeStruct(f.shape, f.dtype) for f in fulls],
        scratch_shapes=[pltpu.SemaphoreType.DMA((n, len(OTHER_CHIPS))), pltpu.SemaphoreType.DMA((n, len(OTHER_CHIPS)))],
        input_output_aliases={a: a for a in range(n)},
        name="forward_blocks",
    )(*fulls)


def _shard_order():
    (x, y, c), me = _me()
    xs, ys, xo, yo = _peer(4)[1], _peer(2)[1], _peer(5)[1], _peer(3)[1]
    north = c == 1
    ids = [me, _peer(1)[1], jnp.where(north, xs, ys), jnp.where(north, yo, xo), jnp.where(north, ys, xs),
           jnp.where(north, xo, yo), _peer(6)[1], _peer(7)[1]]
    return jnp.stack(ids).astype(jnp.int32)


def _proj_gather(h, w_shard, order, comm, comm_at):
    T, K = h.shape
    C = w_shard.shape[1]
    tm = _pick(T, 1024)
    nm = T // tm
    ahead = max(nm - 2, 0)
    n_ci, n_co = len(comm.ins), len(comm.out_shapes)

    def body(*refs):
        order_ref, h_ref, ws_ref = refs[:3]
        c_ins = refs[3:3 + n_ci]
        o_ref, wf_ref = refs[3 + n_ci:5 + n_ci]
        c_outs = refs[5 + n_ci:5 + n_ci + n_co]
        wbuf, send_sems, recv_sems, own_sem, buf_sems, c_send, c_recv = refs[5 + n_ci + n_co:]
        t, i = pl.program_id(0), pl.program_id(1)

        @pl.when((t == comm_at) & (i == 0))
        def _():
            for cp in comm.copies(c_ins, c_outs, c_send, c_recv):
                cp.start()
        (x, y, c), me = _me()
        sibling, sib_id = _peer(1)
        chips = [_peer(dist) for dist in OTHER_CHIPS]

        def win(idx):
            return wf_ref.at[:, pl.ds(idx * C, C)]

        def copy(k, block, to, src=None):
            return pltpu.make_async_remote_copy(
                src_ref=win(block) if src is None else src, dst_ref=win(block),
                send_sem=send_sems.at[k], recv_sem=recv_sems.at[k], device_id=to, device_id_type=MESH)

        def fetch(tt, src):
            return pltpu.make_async_copy(src, wbuf.at[tt % 2], buf_sems.at[tt % 2])

        own = pltpu.make_async_copy(ws_ref, win(me), own_sem)
        (x_nbr, x_id), (y_nbr, y_id), (_, d_id) = chips

        def half(k, block, top, to):
            rows = wf_ref.at[pl.ds(0 if top else K // 2, K // 2), pl.ds(block * C, C)]
            return pltpu.make_async_remote_copy(src_ref=rows, dst_ref=rows, send_sem=send_sems.at[k], recv_sem=recv_sems.at[k],
                                                device_id=to, device_id_type=MESH)

        here = (x, y, c)

        def pass_on(from_x):
            if from_x:
                copy(4, x_id, sibling).start()
                half(8, x_id, False, y_nbr).start()
            else:
                copy(5, y_id, sibling).start()
                half(7, y_id, True, x_nbr).start()

        @pl.when((t == 0) & (i == 0))
        def _():
            fetch(0, ws_ref).start()
            own.start()
            copy(0, me, sibling, src=ws_ref).start()

        for tt in range(N_DEV):
            @pl.when((t == tt) & (i == 0))
            def _(tt=tt):
                fetch(tt, ws_ref).wait()

        o_ref[...] = _dot(h_ref[...], wbuf[t % 2], NN_DIMS).astype(o_ref.dtype)

        def schedule(first_x):
            on = c == (1 if first_x else 0)
            (f_dev, f_id), (g_dev, g_id) = ((x_nbr, x_id), (y_nbr, y_id)) if first_x else ((y_nbr, y_id), (x_nbr, x_id))
            kf, kg = (1, 2) if first_x else (2, 1)
            pf, pg = (4, 5) if first_x else (5, 4)

            @pl.when((t == 0) & (i == 0) & on)
            def _():
                copy(kf, me, f_dev, src=ws_ref).start()

            def event(nxt):
                if nxt == 1:
                    copy(0, sib_id, here).wait_recv()
                    return sib_id
                if nxt == 2:
                    copy(kf, f_id, here).wait_recv()
                    copy(kf, me, f_dev, src=ws_ref).wait_send()
                    copy(kg, me, g_dev, src=ws_ref).start()
                    pass_on(first_x)
                    return f_id
                if nxt == 3:
                    copy(pg, g_id + 1 - 2 * c, here).wait_recv()
                    return g_id + 1 - 2 * c
                if nxt == 4:
                    copy(kg, g_id, here).wait_recv()
                    pass_on(not first_x)
                    return g_id
                if nxt == 5:
                    copy(pf, f_id + 1 - 2 * c, here).wait_recv()
                    return f_id + 1 - 2 * c
                if nxt == 6:
                    half(7, d_id, True, here).wait_recv()
                    half(8, d_id, False, here).wait_recv()
                    copy(6, d_id, sibling).start()
                    return d_id
                copy(6, d_id + 1 - 2 * c, here).wait_recv()
                return d_id + 1 - 2 * c

            for tt in range(N_DEV - 1):
                @pl.when((t == tt) & (i == ahead) & on)
                def _(tt=tt):
                    fetch(tt + 1, win(event(tt + 1))).start()

            @pl.when((t == N_DEV - 1) & (i == nm - 1) & on)
            def _():
                copy(kg, me, g_dev, src=ws_ref).wait_send()

        schedule(True)
        schedule(False)

        @pl.when((t == N_DEV - 1) & (i == nm - 1))
        def _():
            copy(0, me, sibling, src=ws_ref).wait_send()
            half(7, y_id, True, x_nbr).wait_send()
            half(8, x_id, False, y_nbr).wait_send()
            for j, (_, dev_id) in enumerate(chips):
                copy(4 + j, dev_id, sibling).wait_send()
            own.wait()
            for cp in comm.copies(c_ins, c_outs, c_send, c_recv):
                cp.wait()

    hbm = pl.BlockSpec(memory_space=pl.ANY)
    res = pl.pallas_call(
        body,
        grid_spec=pltpu.PrefetchScalarGridSpec(
            num_scalar_prefetch=1, grid=(N_DEV, nm),
            in_specs=[pl.BlockSpec((tm, K), lambda t, i, order_ref: (i, 0)), hbm] + [hbm] * n_ci,
            out_specs=[pl.BlockSpec((tm, C), lambda t, i, order_ref: (i, order_ref[t])), hbm] + [hbm] * n_co,
            scratch_shapes=[pltpu.VMEM((2, K, C), bf16), pltpu.SemaphoreType.DMA((9,)), pltpu.SemaphoreType.DMA((9,)),
                            pltpu.SemaphoreType.DMA, pltpu.SemaphoreType.DMA((2,))] + comm.scratch()),
        out_shape=[jax.ShapeDtypeStruct((T, N_DEV * C), PROJ_DTYPE), jax.ShapeDtypeStruct((K, N_DEV * C), bf16)] + comm.out_shapes,
        compiler_params=_params(("arbitrary", "arbitrary")), name="proj_gather",
    )(order, h, w_shard, *comm.ins)
    return res[0], res[1], list(res[2:])


N_CHIP = 4


def _shard_shape(g, ax):
    return tuple(d // N_DEV if i == ax else d for i, d in enumerate(g.shape))


def _sibling_comm(grads, axes):
    n = len(grads)
    sizes = [g.shape[ax] // N_DEV for g, ax in zip(grads, axes)]

    def copies(ins, lands, send_sems, recv_sems, base=0):
        sibling, _ = _peer(1)
        out = []
        for j in range(N_CHIP):
            _, owner = _peer(2 * j + 1)
            for a in range(n):
                out.append(pltpu.make_async_remote_copy(
                    src_ref=_window(ins[a], axes[a], sizes[a], owner), dst_ref=lands[a].at[j],
                    send_sem=send_sems.at[base + a, j], recv_sem=recv_sems.at[base + a, j], device_id=sibling,
                    device_id_type=MESH))
        return out

    shapes = [jax.ShapeDtypeStruct((N_CHIP,) + _shard_shape(g, ax), g.dtype) for g, ax in zip(grads, axes)]
    return _Comm(grads, shapes, (n, N_CHIP), copies)


def _chips_comm(sums):
    n = len(sums)

    def copies(ins, lands, send_sems, recv_sems, base=0):
        out = []
        for j in range(1, N_CHIP):
            dev, _ = _peer(2 * j)
            for a in range(n):
                out.append(pltpu.make_async_remote_copy(
                    src_ref=ins[a].at[j - 1], dst_ref=lands[a].at[j - 1],
                    send_sem=send_sems.at[base + a, j - 1], recv_sem=recv_sems.at[base + a, j - 1], device_id=dev,
                    device_id_type=MESH))
        return out

    return _Comm(sums, [jax.ShapeDtypeStruct(s.shape, s.dtype) for s in sums], (n, N_CHIP), copies)


def _join(c1, c2):
    n1, m1, k1 = len(c1.ins), len(c1.out_shapes), c1.sem_shape[0]

    def copies(ins, lands, send_sems, recv_sems):
        return (c1.copies(ins[:n1], lands[:m1], send_sems, recv_sems, base=0)
                + c2.copies(ins[n1:], lands[m1:], send_sems, recv_sems, base=k1))

    aliases = {**c1.aliases, **{n1 + k: m1 + v for k, v in c2.aliases.items()}}
    return _Comm(c1.ins + c2.ins, c1.out_shapes + c2.out_shapes, (k1 + c2.sem_shape[0], N_CHIP), copies, aliases)


def _chips_first_hop(sums):
    n = len(sums)

    def copies(ins, outs, send_sems, recv_sems, base=0):
        lands, relays = outs[:n], outs[n:]
        (y_dev, _), (x_dev, _) = _peer(2), _peer(4)
        out = []
        for a in range(n):
            half = sums[a].shape[1] // 2
            for k, (src, dst, dev) in enumerate((
                    (ins[a].at[0], lands[a].at[0], y_dev), (ins[a].at[1], lands[a].at[1], x_dev),
                    (ins[a].at[2, pl.ds(0, half)], relays[a].at[0], x_dev),
                    (ins[a].at[2, pl.ds(half, half)], relays[a].at[1], y_dev))):
                out.append(pltpu.make_async_remote_copy(
                    src_ref=src, dst_ref=dst, send_sem=send_sems.at[base + a, k], recv_sem=recv_sems.at[base + a, k],
                    device_id=dev, device_id_type=MESH))
        return out

    shapes = ([jax.ShapeDtypeStruct(s.shape, s.dtype) for s in sums]
              + [jax.ShapeDtypeStruct((2, s.shape[1] // 2) + s.shape[2:], s.dtype) for s in sums])
    return _Comm(sums, shapes, (n, N_CHIP), copies)


def _chips_relay(relays, lands):
    n = len(relays)

    def copies(ins, outs, send_sems, recv_sems, base=0):
        (y_dev, _), (x_dev, _) = _peer(2), _peer(4)
        out = []
        for a in range(n):
            half = relays[a].shape[1]
            for k, (rows, dev) in enumerate(((pl.ds(0, half), y_dev), (pl.ds(half, half), x_dev))):
                out.append(pltpu.make_async_remote_copy(
                    src_ref=ins[a].at[k], dst_ref=outs[a].at[2, rows], send_sem=send_sems.at[base + a, k],
                    recv_sem=recv_sems.at[base + a, k], device_id=dev, device_id_type=MESH))
        return out

    return _Comm(list(relays) + list(lands), [jax.ShapeDtypeStruct(l.shape, l.dtype) for l in lands], (n, N_CHIP), copies,
                 aliases={n + a: a for a in range(n)})


def _rs_chip_sum(grad, land, xyc, axis, name):
    R, C = land.shape[1:]
    tr = _pick(R, max(8, (640 * 1024) // C), 8)

    def body(xyc_ref, g_ref, l_ref, o_ref):
        o_ref[0] = (g_ref[...].astype(f32) + l_ref[0].astype(f32)).astype(bf16)

    def owner(j, xyc_ref):
        jj = j + 1
        return 4 * (xyc_ref[0] ^ (jj >> 1)) + 2 * (xyc_ref[1] ^ (jj & 1)) + xyc_ref[2]

    if axis == 0:
        g_spec = pl.BlockSpec((tr, C), lambda j, i, xyc_ref: (owner(j, xyc_ref) * (R // tr) + i, 0))
    else:
        g_spec = pl.BlockSpec((tr, C), lambda j, i, xyc_ref: (i, owner(j, xyc_ref)))
    return pl.pallas_call(
        body,
        grid_spec=pltpu.PrefetchScalarGridSpec(
            num_scalar_prefetch=1, grid=(N_CHIP - 1, R // tr),
            in_specs=[g_spec, pl.BlockSpec((1, tr, C), lambda j, i, xyc_ref: (j + 1, i, 0))],
            out_specs=pl.BlockSpec((1, tr, C), lambda j, i, xyc_ref: (j, i, 0))),
        out_shape=jax.ShapeDtypeStruct((N_CHIP - 1, R, C), bf16),
        compiler_params=_params(("parallel", "parallel")), name=name,
    )(xyc, grad, land)


def _allreduce_small(part):
    R = part.shape[0]

    def body(p_ref, o_ref, buf, send_sems, recv_sems):
        (x, y, c), me = _me()
        buf[me] = p_ref[...]
        copies = []
        for k in range(1, N_DEV):
            dev, dev_id = _peer(k)
            copies.append(pltpu.make_async_remote_copy(
                src_ref=p_ref, dst_ref=buf.at[me], send_sem=send_sems.at[k - 1], recv_sem=recv_sems.at[k - 1],
                device_id=dev, device_id_type=MESH))
        for cp in copies:
            cp.start()
        for k in range(1, N_DEV):
            _, dev_id = _peer(k)
            pltpu.make_async_remote_copy(
                src_ref=p_ref, dst_ref=buf.at[dev_id], send_sem=send_sems.at[k - 1], recv_sem=recv_sems.at[k - 1],
                device_id=(x, y, c), device_id_type=MESH).wait_recv()
        for cp in copies:
            cp.wait_send()
        acc = buf[0]
        for d in range(1, N_DEV):
            acc = acc + buf[d]
        o_ref[...] = acc

    return pl.pallas_call(
        body, in_specs=[pl.BlockSpec(memory_space=pltpu.VMEM)], out_specs=pl.BlockSpec(memory_space=pltpu.VMEM),
        out_shape=jax.ShapeDtypeStruct((R, LANES), f32),
        scratch_shapes=[pltpu.VMEM((N_DEV, R, LANES), f32), pltpu.SemaphoreType.DMA((N_DEV - 1,)), pltpu.SemaphoreType.DMA((N_DEV - 1,))],
        name="allreduce_small",
    )(part)


def _adamw_math(w, g, m, v):
    m2 = ADAM_B1 * m + (1.0 - ADAM_B1) * g
    v2 = ADAM_B2 * v + (1.0 - ADAM_B2) * (g * g)
    m_hat = m2 / (1.0 - ADAM_B1 ** ADAM_STEP)
    v_hat = v2 / (1.0 - ADAM_B2 ** ADAM_STEP)
    return -ADAM_LR * (m_hat / (jnp.sqrt(v_hat) + ADAM_EPS) + ADAM_WD * w), m2, v2


def _adamw_shard(grad, land_sib, land_chips, w, m, v, me, axis, name, row0=0, prev=None):
    R, C = land_sib.shape[1:]
    assert axis == 1 or R == w.shape[0]
    tr = _pick(R, max(8, (320 * 1024) // C), 8)
    r0 = row0 // tr
    n_prev = len(prev) if prev else 0

    def body(me_ref, g_ref, s_ref, l_ref, w_ref, m_ref, v_ref, *rest):
        go_ref, d_ref, mo_ref, vo_ref = rest[n_prev:]
        g = g_ref[...].astype(f32) + s_ref[0].astype(f32)
        for j in range(N_CHIP - 1):
            g = g + l_ref[j].astype(f32)
        d, m2, v2 = _adamw_math(w_ref[...], g, m_ref[...], v_ref[...])
        go_ref[...] = g
        d_ref[...] = d
        mo_ref[...] = m2
        vo_ref[...] = v2

    if axis == 0:
        g_spec = pl.BlockSpec((tr, C), lambda i, me_ref: (me_ref[0] * (R // tr) + i, 0))
    else:
        g_spec = pl.BlockSpec((tr, C), lambda i, me_ref: (i, me_ref[0]))
    blk = pl.BlockSpec((tr, C), lambda i, me_ref: (r0 + i, 0))
    return pl.pallas_call(
        body,
        grid_spec=pltpu.PrefetchScalarGridSpec(
            num_scalar_prefetch=1, grid=(R // tr,),
            in_specs=[g_spec, pl.BlockSpec((1, tr, C), lambda i, me_ref: (0, i, 0)),
                      pl.BlockSpec((N_CHIP - 1, tr, C), lambda i, me_ref: (0, i, 0)), blk, blk, blk]
            + [pl.BlockSpec(memory_space=pl.ANY)] * n_prev,
            out_specs=[blk] * 4),
        out_shape=[jax.ShapeDtypeStruct(w.shape, f32)] * 4,
        input_output_aliases={7 + i: i for i in range(n_prev)},
        compiler_params=_params(("parallel",)), name=name,
    )(me, grad, land_sib, land_chips, w, m, v, *(prev or []))


def _adamw_small(g, w, m, v):
    def body(g_ref, w_ref, m_ref, v_ref, d_ref, mo_ref, vo_ref):
        d, m2, v2 = _adamw_math(w_ref[...], g_ref[...], m_ref[...], v_ref[...])
        d_ref[...] = d
        mo_ref[...] = m2
        vo_ref[...] = v2

    return pl.pallas_call(body, out_shape=[jax.ShapeDtypeStruct(g.shape, f32)] * 3, name="adamw_small")(g, w, m, v)


def _pack_rows(parts, total_rows):
    rows = jnp.concatenate([p.reshape(-1, LANES) for p in parts], axis=0)
    return jnp.pad(rows, ((0, total_rows - rows.shape[0]), (0, 0)))


BIG = ("w_in", "mem_w_kv", "w_br_attn", "w_br_conv", "w_br_mem", "w_out")
BIG_AXIS = {"w_in": 1, "mem_w_kv": 0, "w_br_attn": 1, "w_br_conv": 1, "w_br_mem": 1, "w_out": 0}
SMALL = ("norm_g", "mem_norm_g", "attn_q_norm", "attn_k_norm", "mem_q_norm", "mem_k_norm")
ALL_W = ("norm_g", "mem_norm_g", "w_in", "attn_q_norm", "attn_k_norm", "conv_w", "mem_w_kv", "mem_q_norm", "mem_k_norm",
         "w_br_attn", "w_br_conv", "w_br_mem", "w_out")


def kernel(x, mem, norm_g, mem_norm_g, w_in, attn_q_norm, attn_k_norm, conv_w, mem_w_kv, mem_q_norm, mem_k_norm, w_br_attn, w_br_conv, w_br_mem, w_out, loss_target, m_norm_g, m_mem_norm_g, m_w_in, m_attn_q_norm, m_attn_k_norm, m_conv_w, m_mem_w_kv, m_mem_q_norm, m_mem_k_norm, m_w_br_attn, m_w_br_conv, m_w_br_mem, m_w_out, v_norm_g, v_mem_norm_g, v_w_in, v_attn_q_norm, v_attn_k_norm, v_conv_w, v_mem_w_kv, v_mem_q_norm, v_mem_k_norm, v_w_br_attn, v_w_br_conv, v_w_br_mem, v_w_out):
    w = dict(norm_g=norm_g, mem_norm_g=mem_norm_g, w_in=w_in, attn_q_norm=attn_q_norm, attn_k_norm=attn_k_norm, conv_w=conv_w,
             mem_w_kv=mem_w_kv, mem_q_norm=mem_q_norm, mem_k_norm=mem_k_norm, w_br_attn=w_br_attn, w_br_conv=w_br_conv,
             w_br_mem=w_br_mem, w_out=w_out)
    mo = dict(norm_g=m_norm_g, mem_norm_g=m_mem_norm_g, w_in=m_w_in, attn_q_norm=m_attn_q_norm, attn_k_norm=m_attn_k_norm,
              conv_w=m_conv_w, mem_w_kv=m_mem_w_kv, mem_q_norm=m_mem_q_norm, mem_k_norm=m_mem_k_norm, w_br_attn=m_w_br_attn,
              w_br_conv=m_w_br_conv, w_br_mem=m_w_br_mem, w_out=m_w_out)
    vo = dict(norm_g=v_norm_g, mem_norm_g=v_mem_norm_g, w_in=v_w_in, attn_q_norm=v_attn_q_norm, attn_k_norm=v_attn_k_norm,
              conv_w=v_conv_w, mem_w_kv=v_mem_w_kv, mem_q_norm=v_mem_q_norm, mem_k_norm=v_mem_k_norm, w_br_attn=v_w_br_attn,
              w_br_conv=v_w_br_conv, w_br_mem=v_w_br_mem, w_out=v_w_out)
    B, S, D = x.shape
    me = (4 * lax.axis_index("x") + 2 * lax.axis_index("y") + lax.axis_index("c")).astype(jnp.int32)

    T = B * S
    x2, t2, mem2, ng = x.reshape(T, D), loss_target.reshape(T, D), mem.reshape(B * MEM_LEN, D), norm_g.reshape(1, D)
    h = _rms_fwd(x2, ng, "rms_x")
    with_proj, with_attn = ("mem_w_kv",), ("w_out", "w_br_attn", "w_br_conv", "w_br_mem")
    later = with_proj + with_attn
    later_axes = [BIG_AXIS[n] for n in later] + [None]
    conv_pad = jnp.pad(conv_w, ((0, 8 - conv_w.shape[0]), (0, 0)))
    proj, w_in_full, spread_a = _proj_gather(
        h, w_in.astype(bf16), _shard_order(),
        _spread_comm([w[n].astype(bf16) for n in with_proj], [BIG_AXIS[n] for n in with_proj]), comm_at=N_DEV - 3)
    *attn, spread_b = _attn_fwd(proj, attn_q_norm, attn_k_norm, B, S,
                                comm=_spread_comm([w[n].astype(bf16) for n in with_attn] + [conv_pad],
                                                  [BIG_AXIS[n] for n in with_attn] + [None]))
    *fulls, conv_g = _forward_blocks(spread_a + spread_b, later_axes)
    wf = dict(zip(later, fulls), w_in=w_in_full)
    conv_full = conv_g[:, :3, :].transpose(1, 0, 2).reshape(3, CONV_W)

    sq_err, g, t = _fwd_bwd_head(x2, mem2, t2, proj, attn, B, S, mem_norm_g, attn_q_norm, attn_k_norm, conv_full, mem_q_norm,
                                 mem_k_norm, wf["mem_w_kv"], wf["w_br_attn"], wf["w_br_conv"], wf["w_br_mem"], wf["w_out"])
    t.update(x2=x2, ng=ng, h=h)

    xyc = jnp.stack([lax.axis_index("x"), lax.axis_index("y"), lax.axis_index("c")]).astype(jnp.int32)
    me1 = me.reshape(1)
    early = ("w_out", "w_br_attn", "w_br_conv", "w_br_mem")
    g["mem_w_kv"], sib_early = _mm(t["mh"], t["dmkv"], mode="tn", out_dtype=DW_DTYPE, name="dw_kv",
                                   comm=_sibling_comm([g[n] for n in early], [BIG_AXIS[n] for n in early]))
    sums_early = [_rs_chip_sum(g[n], ls, xyc, BIG_AXIS[n], "rs_sum_" + n) for n, ls in zip(early, sib_early)]
    tm_w = _pick(D // 2, 1024)
    half = (D // 2) // tm_w
    g_top, (*chips_early, sib_kv) = _mm(t["h"], t["dproj"], mode="tn", out_dtype=DW_DTYPE, name="dw_in_top", tm=tm_w,
                                        m_tiles=(0, half),
                                        comm=_join(_chips_comm(sums_early), _sibling_comm([g["mem_w_kv"]], [0])))
    sum_kv = _rs_chip_sum(g["mem_w_kv"], sib_kv, xyc, 0, "rs_sum_mem_w_kv")
    g_bot, (chips_kv, sib_top) = _mm(t["h"], t["dproj"], mode="tn", out_dtype=DW_DTYPE, name="dw_in_bot", tm=tm_w,
                                     m_tiles=(half, half), comm=_join(_chips_comm([sum_kv]), _sibling_comm([g_top], [1])))
    sum_top = _rs_chip_sum(g_top, sib_top, xyc, 1, "rs_sum_w_in_top")
    tm_t = _pick(T // 2, 1024)
    halfm = (T // 2) // tm_t
    dh, (part_top, relay_top, sib_bot) = _mm(t["dproj"], wf["w_in"], mode="nt", out_dtype=f32, name="d_h_a", tm=tm_t, tk=D_H_TK,
                                             m_tiles=(0, halfm), into="new",
                                             comm=_join(_chips_first_hop([sum_top]), _sibling_comm([g_bot], [1])))
    sum_bot = _rs_chip_sum(g_bot, sib_bot, xyc, 1, "rs_sum_w_in_bot")
    dh, (part_bot, relay_bot, chips_top) = _mm(t["dproj"], wf["w_in"], mode="nt", out_dtype=f32, name="d_h_b", tm=tm_t,
                                               tk=D_H_TK, m_tiles=(halfm, halfm), into=dh,
                                               comm=_join(_chips_first_hop([sum_bot]), _chips_relay([relay_top], [part_top])))
    dx, dng, (chips_bot,) = _rms_bwd(t["x2"], t["ng"], dh, t["dy"], "rms_x_bwd", comm=_chips_relay([relay_bot], [part_bot]))
    g["norm_g"] = dng.reshape(D)

    grad, delta, new_m, new_v = {}, {}, {}, {}
    for n, ls, lc in zip(early + ("mem_w_kv",), sib_early + [sib_kv], chips_early + [chips_kv]):
        grad[n], delta[n], new_m[n], new_v[n] = _adamw_shard(g[n], ls, lc, w[n], mo[n], vo[n], me1, BIG_AXIS[n], "adamw_" + n)
    top = _adamw_shard(g_top, sib_top, chips_top, w_in, m_w_in, v_w_in, me1, 1, "adamw_w_in_top")
    grad["w_in"], delta["w_in"], new_m["w_in"], new_v["w_in"] = _adamw_shard(
        g_bot, sib_bot, chips_bot, w_in, m_w_in, v_w_in, me1, 1, "adamw_w_in_bot", row0=D // 2, prev=top)

    n_rep = sum(w[n].size for n in SMALL) // LANES
    conv_rows = g["conv_w"].reshape(3, N_DEV, LANES).transpose(1, 0, 2).reshape(3 * N_DEV, LANES)
    n_rows = -(-(n_rep + 3 * N_DEV + 1) // 8) * 8
    part = _pack_rows([g[n] for n in SMALL] + [conv_rows, jnp.full((1, LANES), sq_err, f32)], n_rows)
    tot = _allreduce_small(part)
    loss = tot[n_rep + 3 * N_DEV, 0] * (0.5 / D)
    n_small = -(-(n_rep + 3) // 8) * 8
    g_small = _pack_rows([tot[:n_rep], lax.dynamic_slice(tot, (n_rep + 3 * me, 0), (3, LANES))], n_small)
    names = SMALL + ("conv_w",)
    d_s, m_s, v_s = _adamw_small(g_small, _pack_rows([w[n] for n in names], n_small), _pack_rows([mo[n] for n in names], n_small),
                                 _pack_rows([vo[n] for n in names], n_small))
    off = 0
    for n in names:
        r = w[n].size // LANES
        grad[n], delta[n] = g_small[off:off + r].reshape(w[n].shape), d_s[off:off + r].reshape(w[n].shape)
        new_m[n], new_v[n] = m_s[off:off + r].reshape(w[n].shape), v_s[off:off + r].reshape(w[n].shape)
        off += r
    return (loss, dx.reshape(B, S, D), *[grad[n] for n in ALL_W], *[delta[n] for n in ALL_W], *[new_m[n] for n in ALL_W],
            *[new_v[n] for n in ALL_W])
```

```python
import functools

import jax
import jax.numpy as jnp
from jax import lax
from jax.experimental import pallas as pl
from jax.experimental.pallas import tpu as pltpu

f32 = jnp.float32
bf16 = jnp.bfloat16

N_DEV = 8
HEAD_DIM = 128
DILATIONS = (1, 4, 16)
N_GROUPS = 3
HEADS = 4
BLK = 128
ATTN_QKV = N_GROUPS * HEADS * HEAD_DIM
ATTN_OUT = HEADS * HEAD_DIM
CONV_W = 1024
MEM_LEN = 256
MEM_HEADS = 4
MEM_HD = 256
MEM_W = MEM_HEADS * MEM_HD
EPS = 1e-6
Q0, K0, V0 = 0, ATTN_QKV, 2 * ATTN_QKV
ZA = 3 * ATTN_QKV
CB, CC, CV, ZC = ZA + ATTN_OUT, ZA + ATTN_OUT + CONV_W, ZA + ATTN_OUT + 2 * CONV_W, ZA + ATTN_OUT + 3 * CONV_W
MQ = ZC + CONV_W
ZM = MQ + MEM_W
GT = ZM + MEM_W

ADAM_LR, ADAM_B1, ADAM_B2, ADAM_EPS, ADAM_WD, ADAM_STEP = 0.001, 0.9, 0.999, 1e-08, 0.01, 10

VMEM_LIMIT = 56 * 1024 * 1024
D_H_TK = 4352
PROJ_DTYPE = bf16
DW_DTYPE = bf16
LANES = 128

NT_DIMS = (((1,), (1,)), ((), ()))
TN_DIMS = (((0,), (0,)), ((), ()))
NN_DIMS = (((1,), (0,)), ((), ()))


def _params(sem=None):
    return pltpu.CompilerParams(dimension_semantics=sem, vmem_limit_bytes=VMEM_LIMIT)


def _pick(n, pref, q=LANES):
    t = (min(pref, n) // q) * q
    while t >= q:
        if n % t == 0:
            return t
        t -= q
    return n


def _dot(a, b, dims):
    return lax.dot_general(a.astype(bf16), b.astype(bf16), dims, preferred_element_type=f32)


def _sigmoid(z):
    return 0.5 * jnp.tanh(0.5 * z) + 0.5


def _rstd(v):
    return lax.rsqrt(jnp.mean(v * v, axis=-1, keepdims=True) + EPS)


class _Deferred:
    def __init__(self, stage, sems, step, last, n):
        assert last >= 1
        self.stage, self.sems, self.step, self.last, self.n = stage, sems, step, last, n
        self.slot = step % 2

    def _drain(self, slot, like):
        for c in range(self.n):
            pltpu.make_async_copy(self.stage.at[slot, c], like(c), self.sems.at[slot, c]).wait()

    def begin(self, like):
        pl.when(self.step >= 2)(lambda: self._drain(self.slot, like))

    def start(self, c, dst):
        pltpu.make_async_copy(self.stage.at[self.slot, c], dst, self.sems.at[self.slot, c]).start()

    def end(self, like):
        @pl.when(self.step == self.last)
        def _():
            self._drain(self.slot, like)
            self._drain(1 - self.slot, like)


def _row_sum(v):
    return _dot(v, jnp.ones((v.shape[1], v.shape[1]), bf16), NN_DIMS)


def _rstd_head(v):
    return lax.rsqrt(_row_sum(v * v) * (1.0 / v.shape[1]) + EPS)


class _Comm:
    def __init__(self, ins, out_shapes, sem_shape, copies, aliases=None):
        self.ins, self.out_shapes, self.sem_shape, self.copies = list(ins), list(out_shapes), sem_shape, copies
        self.aliases = dict(aliases or {})

    def scratch(self):
        return [pltpu.SemaphoreType.DMA(self.sem_shape), pltpu.SemaphoreType.DMA(self.sem_shape)]

    def io_aliases(self, first_in, first_out):
        return {first_in + k: first_out + v for k, v in self.aliases.items()}


def _mm(a, b, *, mode, out_dtype, name, tm=1024, tn=1024, tk=4096, m_tiles=None, into=None, comm=None):
    if mode == "nn":
        (M, K), (K2, N) = a.shape, b.shape
    elif mode == "nt":
        (M, K), (N, K2) = a.shape, b.shape
    else:
        (K, M), (K2, N) = a.shape, b.shape
    assert K == K2
    tm, tn, tk = _pick(M, tm), _pick(N, tn), _pick(K, tk)
    nk = K // tk
    m0, mc = m_tiles if m_tiles is not None else (0, M // tm)
    o0 = 0 if into is None else m0
    aliased = into is not None and not isinstance(into, str)
    n_ci = len(comm.ins) if comm else 0
    n_co = len(comm.out_shapes) if comm else 0
    grid = (mc, N // tn, nk)
    dims = {"nn": NN_DIMS, "nt": NT_DIMS, "tn": TN_DIMS}[mode]

    def body(*refs, nk):
        a_ref, b_ref = refs[:2]
        pos = 3 if aliased else 2
        c_ins, o_ref, c_outs = refs[pos:pos + n_ci], refs[pos + n_ci], refs[pos + n_ci + 1:pos + n_ci + 1 + n_co]
        scratch = refs[pos + n_ci + 1 + n_co:]
        ids = [pl.program_id(d) for d in range(3)]
        if comm:
            sems = scratch[-2:]

            @pl.when((ids[0] == 0) & (ids[1] == 0) & (ids[2] == 0))
            def _():
                for cp in comm.copies(c_ins, c_outs, *sems):
                    cp.start()

        if nk == 1:
            o_ref[...] = _dot(a_ref[...], b_ref[...], dims).astype(o_ref.dtype)
        else:
            acc_ref, k = scratch[0], ids[2]

            @pl.when(k == 0)
            def _():
                acc_ref[...] = _dot(a_ref[...], b_ref[...], dims)

            @pl.when((k > 0) & (k < nk - 1))
            def _():
                acc_ref[...] += _dot(a_ref[...], b_ref[...], dims)

            @pl.when(k == nk - 1)
            def _():
                o_ref[...] = (acc_ref[...] + _dot(a_ref[...], b_ref[...], dims)).astype(o_ref.dtype)

        if comm:
            @pl.when((ids[0] == grid[0] - 1) & (ids[1] == grid[1] - 1) & (ids[2] == grid[2] - 1))
            def _():
                for cp in comm.copies(c_ins, c_outs, *sems):
                    cp.wait()

    if mode == "tn":
        a_spec = pl.BlockSpec((tk, tm), lambda i, j, k: (k, m0 + i))
    else:
        a_spec = pl.BlockSpec((tm, tk), lambda i, j, k: (m0 + i, k))
    if mode == "nt":
        b_spec = pl.BlockSpec((tn, tk), lambda i, j, k: (j, k))
    else:
        b_spec = pl.BlockSpec((tk, tn), lambda i, j, k: (k, j))
    hbm = pl.BlockSpec(memory_space=pl.ANY)
    res = pl.pallas_call(
        functools.partial(body, nk=nk),
        grid=grid,
        in_specs=[a_spec, b_spec] + [hbm] * (aliased + n_ci),
        out_specs=[pl.BlockSpec((tm, tn), lambda i, j, k: (o0 + i, j))] + [hbm] * n_co,
        out_shape=[jax.ShapeDtypeStruct((mc * tm if into is None else M, N), out_dtype)] + (comm.out_shapes if comm else []),
        scratch_shapes=([pltpu.VMEM((tm, tn), f32)] if nk > 1 else []) + (comm.scratch() if comm else []),
        input_output_aliases={**({2: 0} if aliased else {}), **(comm.io_aliases(2 + aliased, 1) if comm else {})},
        compiler_params=_params(("arbitrary",) * 3 if comm else ("parallel", "parallel", "arbitrary")),
        name=name,
    )(a, b, *([into] if aliased else []), *(comm.ins if comm else []))
    return (res[0], list(res[1:])) if comm else res[0]


def _rms_fwd(x, g, name):
    T, D = x.shape
    tm = _pick(T, 256, 8)

    def body(x_ref, g_ref, h_ref):
        xv = x_ref[...]
        h_ref[...] = (xv * _rstd(xv) * g_ref[...]).astype(bf16)

    return pl.pallas_call(
        body, grid=(T // tm,),
        in_specs=[pl.BlockSpec((tm, D), lambda i: (i, 0)), pl.BlockSpec((1, D), lambda i: (0, 0))],
        out_specs=pl.BlockSpec((tm, D), lambda i: (i, 0)),
        out_shape=jax.ShapeDtypeStruct((T, D), bf16),
        compiler_params=_params(("parallel",)), name=name,
    )(x, g)


def _rms_bwd(x, g, dh, dy, name, comm=None):
    T, D = x.shape
    tm = _pick(T, 256, 8)
    with_dx = dy is not None
    n_ci = len(comm.ins) if comm else 0
    n_co = len(comm.out_shapes) if comm else 0

    def body(*refs):
        if with_dx:
            x_ref, g_ref, dh_ref, dy_ref = refs[:4]
            c_ins, (dx_ref, dg_ref) = refs[4:4 + n_ci], refs[4 + n_ci:6 + n_ci]
            c_outs, sems = refs[6 + n_ci:6 + n_ci + n_co], refs[6 + n_ci + n_co:]
        else:
            x_ref, g_ref, dh_ref, dg_ref = refs
        if comm:
            @pl.when(pl.program_id(0) == 0)
            def _():
                for cp in comm.copies(c_ins, c_outs, *sems):
                    cp.start()

        xv = x_ref[...]
        r = _rstd(xv)
        xh = xv * r
        dhv = dh_ref[...]
        part = jnp.sum(dhv * xh, axis=0, keepdims=True)

        @pl.when(pl.program_id(0) == 0)
        def _():
            dg_ref[...] = part

        @pl.when(pl.program_id(0) > 0)
        def _():
            dg_ref[...] += part

        if with_dx:
            dxh = dhv * g_ref[...]
            dx_ref[...] = dy_ref[...].astype(f32) + r * (dxh - xh * jnp.mean(dxh * xh, axis=-1, keepdims=True))

        if comm:
            @pl.when(pl.program_id(0) == T // tm - 1)
            def _():
                for cp in comm.copies(c_ins, c_outs, *sems):
                    cp.wait()

    row = pl.BlockSpec((tm, D), lambda i: (i, 0))
    vec = pl.BlockSpec((1, D), lambda i: (0, 0))
    hbm = pl.BlockSpec(memory_space=pl.ANY)
    if with_dx:
        res = pl.pallas_call(
            body, grid=(T // tm,), in_specs=[row, vec, row, row] + [hbm] * n_ci, out_specs=[row, vec] + [hbm] * n_co,
            out_shape=[jax.ShapeDtypeStruct((T, D), f32), jax.ShapeDtypeStruct((1, D), f32)] + (comm.out_shapes if comm else []),
            scratch_shapes=comm.scratch() if comm else [],
            input_output_aliases=comm.io_aliases(4, 2) if comm else {},
            compiler_params=_params(("arbitrary",)), name=name,
        )(x, g, dh, dy, *(comm.ins if comm else []))
        return (res[0], res[1], list(res[2:])) if comm else res
    return pl.pallas_call(
        body, grid=(T // tm,), in_specs=[row, vec, row], out_specs=vec,
        out_shape=jax.ShapeDtypeStruct((1, D), f32),
        compiler_params=_params(("arbitrary",)), name=name,
    )(x, g, dh)


MAX_UNIT_UNROLL = 6


def _unit_unroll(trips):
    return max(u for u in range(1, MAX_UNIT_UNROLL + 1) if trips % u == 0)


def _rows(start, size, stride):
    return pl.ds(start, size) if stride == 1 else pl.ds(start, size, stride=stride)


def _attn_units(S, d, first, prev):
    nb = S // d // BLK

    def do_first(r, c):
        first(_rows(r, BLK, d))
        return c

    lax.fori_loop(0, d, do_first, 0, unroll=_unit_unroll(d))
    if nb > 1:
        def do_prev(u, c):
            r = u // (nb - 1)
            b = u % (nb - 1) + 1
            base = r + d * b * BLK
            prev(_rows(base, BLK, d), _rows(base - d * BLK, 2 * BLK, d))
            return c

        lax.fori_loop(0, d * (nb - 1), do_prev, 0, unroll=_unit_unroll(d * (nb - 1)))


def _band_bias(nkeys):
    i = lax.broadcasted_iota(jnp.int32, (BLK, nkeys), 0)
    j = lax.broadcasted_iota(jnp.int32, (BLK, nkeys), 1)
    ok = (j <= i) if nkeys == BLK else ((j >= i) & (j <= i + BLK))
    return jnp.where(ok, 0.0, -jnp.inf).astype(f32)


def _widen_into(src_ref, dst_ref, S):
    for c in range(S // 256):
        rows = pl.ds(c * 256, 256)
        dst_ref[rows, :] = src_ref[rows, :].astype(f32)


def _normalize_into(src_ref, gain, dst_ref, S, rstd_ref=None):
    for c in range(S // 256):
        rows = pl.ds(c * 256, 256)
        v = src_ref[rows, :].astype(f32)
        r = _rstd_head(v)
        dst_ref[rows, :] = v * r * gain
        if rstd_ref is not None:
            rstd_ref[rows, :] = r


def _attn_fwd(proj, gq, gk, B, S, comm=None):
    T, NC = proj.shape
    scale = HEAD_DIM ** -0.5
    n_ci = len(comm.ins) if comm else 0
    n_co = len(comm.out_shapes) if comm else 0

    def body(*refs):
        q_ref, k_ref, v_ref, z_ref, gq_ref, gk_ref = refs[:6]
        c_ins = refs[6:6 + n_ci]
        ag_ref, a_ref, lse_ref = refs[6 + n_ci:9 + n_ci]
        c_outs = refs[9 + n_ci:9 + n_ci + n_co]
        qs, ks, vs, o0, o1, o2, l0, l1, l2, bias1, bias2 = refs[9 + n_ci + n_co:20 + n_ci + n_co]
        sems = refs[20 + n_ci + n_co:]
        g = pl.program_id(2)
        if comm:
            @pl.when((pl.program_id(0) == 0) & (pl.program_id(1) == 0) & (g == 0))
            def _():
                for cp in comm.copies(c_ins, c_outs, *sems):
                    cp.start()

        bias1[...] = _band_bias(BLK)
        bias2[...] = _band_bias(2 * BLK)
        _normalize_into(q_ref, gq_ref[pl.ds(g, 1), :], qs, S)
        _normalize_into(k_ref, gk_ref[pl.ds(g, 1), :], ks, S)
        _widen_into(v_ref, vs, S)
        o_s, l_s = (o0, o1, o2), (l0, l1, l2)

        def run(gi):
            def unit(qr, kr, nkeys):
                s = _dot(qs[qr, :], ks[kr, :], NT_DIMS) * scale + (bias1 if nkeys == BLK else bias2)[...]
                m = jnp.max(s, axis=-1, keepdims=True)
                p = jnp.exp(s - m)
                den = jnp.sum(p, axis=-1, keepdims=True)
                o_s[gi][qr, :] = _dot(p, vs[kr, :], NN_DIMS) * (1.0 / den)
                l_s[gi][qr, :] = jnp.broadcast_to(m + jnp.log(den), (BLK, HEAD_DIM))

            _attn_units(S, DILATIONS[gi], lambda qr: unit(qr, qr, BLK), lambda qr, kr: unit(qr, kr, 2 * BLK))

        for gi in range(N_GROUPS):
            pl.when(g == gi)(functools.partial(run, gi))

        @pl.when(g == N_GROUPS - 1)
        def _():
            for c in range(S // 256):
                rows = pl.ds(c * 256, 256)
                la, lb, lc = l0[rows, :], l1[rows, :], l2[rows, :]
                m = jnp.maximum(jnp.maximum(la, lb), lc)
                wa, wb, wc = jnp.exp(la - m), jnp.exp(lb - m), jnp.exp(lc - m)
                tot = wa + wb + wc
                a = (wa / tot) * o0[rows, :] + (wb / tot) * o1[rows, :] + (wc / tot) * o2[rows, :]
                z = z_ref[rows, :].astype(f32)
                a_ref[rows, :] = a
                lse_ref[rows, :] = m + jnp.log(tot)
                ag_ref[rows, :] = (a * (z * _sigmoid(z))).astype(bf16)

        if comm:
            @pl.when((pl.program_id(0) == B - 1) & (pl.program_id(1) == HEADS - 1) & (g == N_GROUPS - 1))
            def _():
                for cp in comm.copies(c_ins, c_outs, *sems):
                    cp.wait()

    def slab(col0):
        return pl.BlockSpec((S, HEAD_DIM), lambda b, h, g: (b, col0 // HEAD_DIM + g * HEADS + h))

    gain = pl.BlockSpec((N_GROUPS, HEAD_DIM), lambda b, h, g: (0, 0))
    out = pl.BlockSpec((S, HEAD_DIM), lambda b, h, g: (b, h))
    hbm = pl.BlockSpec(memory_space=pl.ANY)
    res = pl.pallas_call(
        body, grid=(B, HEADS, N_GROUPS),
        in_specs=[slab(Q0), slab(K0), slab(V0), pl.BlockSpec((S, HEAD_DIM), lambda b, h, g: (b, ZA // HEAD_DIM + h)), gain, gain]
        + [hbm] * n_ci,
        out_specs=[out, out, out] + [hbm] * n_co,
        out_shape=[jax.ShapeDtypeStruct((T, ATTN_OUT), bf16), jax.ShapeDtypeStruct((T, ATTN_OUT), f32),
                   jax.ShapeDtypeStruct((T, ATTN_OUT), f32)] + (comm.out_shapes if comm else []),
        scratch_shapes=[pltpu.VMEM((S, HEAD_DIM), f32)] * 9 + [pltpu.VMEM((BLK, BLK), f32), pltpu.VMEM((BLK, 2 * BLK), f32)]
        + (comm.scratch() if comm else []),
        compiler_params=_params(("arbitrary", "arbitrary", "arbitrary")), name="attn_fwd",
    )(proj, proj, proj, proj, gq, gk, *(comm.ins if comm else []))
    return res[0], res[1], res[2], list(res[3:])


def _rms_bwd_rows(raw, gain, dn, on_mxu=True, r=None):
    if r is None:
        r = _rstd_head(raw) if on_mxu else _rstd(raw)
    xh = raw * r
    dxh = dn * gain
    if on_mxu:
        mean = _row_sum(dxh * xh) * (1.0 / raw.shape[1])
    else:
        mean = jnp.mean(dxh * xh, axis=-1, keepdims=True)
    return r * (dxh - xh * mean), jnp.sum(dn * xh, axis=0, keepdims=True)


def _attn_bwd(proj, gq, gk, a_comb, lse, dag, dproj, B, S):
    T, NC = proj.shape
    scale = HEAD_DIM ** -0.5

    def body(q_ref, k_ref, v_ref, z_ref, gq_ref, gk_ref, a_ref, lse_ref, dag_ref, dproj_in, dproj_ref, dgq_ref, dgk_ref,
             qs, ks, da_s, dl_s, dq_s, dk_s, dv_s, rq_s, rk_s, vs, bias1, bias2, stage, dz_stage, sem, dz_sem):
        b, h, g = pl.program_id(0), pl.program_id(1), pl.program_id(2)
        bias1[...] = _band_bias(BLK)
        bias2[...] = _band_bias(2 * BLK)
        gq_row, gk_row = gq_ref[pl.ds(g, 1), :], gk_ref[pl.ds(g, 1), :]
        _normalize_into(q_ref, gq_row, qs, S, rq_s)
        _normalize_into(k_ref, gk_row, ks, S, rk_s)
        _widen_into(v_ref, vs, S)

        def dst(c):
            return dproj_ref.at[pl.ds(b * S, S), pl.ds((Q0, K0, V0)[c] + (g * HEADS + h) * HEAD_DIM, HEAD_DIM)]

        out = _Deferred(stage, sem, (b * HEADS + h) * N_GROUPS + g, B * HEADS * N_GROUPS - 1, 3)
        out.begin(dst)

        @pl.when((b == 0) & (h == 0) & (g == 0))
        def _():
            dgq_ref[...] = jnp.zeros_like(dgq_ref)
            dgk_ref[...] = jnp.zeros_like(dgk_ref)

        @pl.when(g == 0)
        def _():
            for c in range(S // 256):
                rows = pl.ds(c * 256, 256)
                z, a, dg_ = z_ref[rows, :].astype(f32), a_ref[rows, :], dag_ref[rows, :].astype(f32)
                sg = _sigmoid(z)
                da = dg_ * (z * sg)
                da_s[rows, :] = da
                dl_s[rows, :] = _row_sum(da * a)
                dz_stage[rows, :] = (dg_ * a * (sg * (1.0 + z * (1.0 - sg)))).astype(bf16)
            cp = pltpu.make_async_copy(dz_stage, dproj_ref.at[pl.ds(b * S, S), pl.ds(ZA + h * HEAD_DIM, HEAD_DIM)], dz_sem)
            cp.start()
            cp.wait()

        dk_s[...] = jnp.zeros_like(dk_s)
        dv_s[...] = jnp.zeros_like(dv_s)

        def run(gi):
            def unit(qr, kr, nkeys):
                q, k, v = qs[qr, :], ks[kr, :], vs[kr, :]
                s = _dot(q, k, NT_DIMS) * scale
                p = jnp.exp(s + (bias1 if nkeys == BLK else bias2)[...] - lse_ref[qr, :][:, :1])
                da = da_s[qr, :]
                dp = _dot(da, v, NT_DIMS)
                ds = p * (dp - dl_s[qr, :][:, :1]) * scale
                dq_s[qr, :] = _dot(ds, k, NN_DIMS)
                dk_s[kr, :] += _dot(ds, q, TN_DIMS)
                dv_s[kr, :] += _dot(p, da, TN_DIMS)

            _attn_units(S, DILATIONS[gi], lambda qr: unit(qr, qr, BLK), lambda qr, kr: unit(qr, kr, 2 * BLK))

        for gi in range(N_GROUPS):
            pl.when(g == gi)(functools.partial(run, gi))

        gq_acc = jnp.zeros((1, HEAD_DIM), f32)
        gk_acc = jnp.zeros((1, HEAD_DIM), f32)
        for c in range(S // 256):
            rows = pl.ds(c * 256, 256)
            dq, gq_p = _rms_bwd_rows(q_ref[rows, :].astype(f32), gq_row, dq_s[rows, :], r=rq_s[rows, :])
            dk, gk_p = _rms_bwd_rows(k_ref[rows, :].astype(f32), gk_row, dk_s[rows, :], r=rk_s[rows, :])
            gq_acc, gk_acc = gq_acc + gq_p, gk_acc + gk_p
            stage[out.slot, 0, rows, :] = dq.astype(bf16)
            stage[out.slot, 1, rows, :] = dk.astype(bf16)
            stage[out.slot, 2, rows, :] = dv_s[rows, :].astype(bf16)
        dgq_ref[pl.ds(g, 1), :] += gq_acc
        dgk_ref[pl.ds(g, 1), :] += gk_acc
        for c in range(3):
            out.start(c, dst(c))
        out.end(dst)

    def slab(col0):
        return pl.BlockSpec((S, HEAD_DIM), lambda b, h, g: (b, col0 // HEAD_DIM + g * HEADS + h))

    gain = pl.BlockSpec((N_GROUPS, HEAD_DIM), lambda b, h, g: (0, 0))
    per_slot = pl.BlockSpec((S, HEAD_DIM), lambda b, h, g: (b, h))
    return pl.pallas_call(
        body, grid=(B, HEADS, N_GROUPS),
        in_specs=[slab(Q0), slab(K0), slab(V0), pl.BlockSpec((S, HEAD_DIM), lambda b, h, g: (b, ZA // HEAD_DIM + h)), gain, gain,
                  per_slot, per_slot, per_slot, pl.BlockSpec(memory_space=pl.ANY)],
        out_specs=[pl.BlockSpec(memory_space=pl.ANY), gain, gain],
        out_shape=[jax.ShapeDtypeStruct(dproj.shape, dproj.dtype), jax.ShapeDtypeStruct((N_GROUPS, HEAD_DIM), f32),
                   jax.ShapeDtypeStruct((N_GROUPS, HEAD_DIM), f32)],
        scratch_shapes=[pltpu.VMEM((S, HEAD_DIM), f32)] * 10 + [pltpu.VMEM((BLK, BLK), f32), pltpu.VMEM((BLK, 2 * BLK), f32),
                                                                 pltpu.VMEM((2, 3, S, HEAD_DIM), bf16), pltpu.VMEM((S, HEAD_DIM), bf16),
                                                                pltpu.SemaphoreType.DMA((2, 3)), pltpu.SemaphoreType.DMA],
        input_output_aliases={9: 0},
        compiler_params=_params(("arbitrary", "arbitrary", "arbitrary")), name="attn_bwd",
    )(proj, proj, proj, proj, gq, gk, a_comb, lse, dag, dproj)


CONV_ROWS = 256


def _conv_fwd(proj, conv_w, B, S):
    T, NC = proj.shape
    tc = LANES

    def body(cb_ref, cc_ref, cv_ref, z_ref, w_ref, c_ref, ub):
        ub[0:8, :] = jnp.zeros((8, tc), f32)
        for r0 in range(0, S, CONV_ROWS):
            rows = pl.ds(r0, CONV_ROWS)
            ub[pl.ds(8 + r0, CONV_ROWS), :] = cc_ref[rows, :].astype(f32) * cv_ref[rows, :].astype(f32)
        w = w_ref[...]
        for r0 in range(0, S, CONV_ROWS):
            rows = pl.ds(r0, CONV_ROWS)
            y = (w[0:1] * ub[pl.ds(8 + r0, CONV_ROWS), :] + w[1:2] * ub[pl.ds(7 + r0, CONV_ROWS), :]
                 + w[2:3] * ub[pl.ds(6 + r0, CONV_ROWS), :])
            z = z_ref[rows, :].astype(f32)
            c_ref[rows, :] = (cb_ref[rows, :].astype(f32) * y * (z * _sigmoid(z))).astype(bf16)

    def seg(col0):
        return pl.BlockSpec((S, tc), lambda j, b: (b, col0 // tc + j))

    return pl.pallas_call(
        body, grid=(CONV_W // tc, B),
        in_specs=[seg(CB), seg(CC), seg(CV), seg(ZC), pl.BlockSpec((3, tc), lambda j, b: (0, j))],
        out_specs=pl.BlockSpec((S, tc), lambda j, b: (b, j)),
        out_shape=jax.ShapeDtypeStruct((T, CONV_W), bf16),
        scratch_shapes=[pltpu.VMEM((S + 8, tc), f32)],
        compiler_params=_params(("parallel", "parallel")), name="conv_fwd",
    )(proj, proj, proj, proj, conv_w)


def _conv_bwd(proj, conv_w, dc, dproj, B, S):
    T, NC = proj.shape
    tc = LANES

    def body(cb_ref, cc_ref, cv_ref, z_ref, w_ref, dc_ref, dproj_in, dproj_ref, dw_ref, ub, db, stage, sem):
        j, b = pl.program_id(0), pl.program_id(1)

        def dst(c):
            return dproj_ref.at[pl.ds(b * S, S), pl.ds((CB, CC, CV, ZC)[c] + j * tc, tc)]

        out = _Deferred(stage, sem, j * B + b, (CONV_W // tc) * B - 1, 4)
        out.begin(dst)
        chunks = [(r0, pl.ds(r0, CONV_ROWS)) for r0 in range(0, S, CONV_ROWS)]
        ub[0:8, :] = jnp.zeros((8, tc), f32)
        db[S:S + 8, :] = jnp.zeros((8, tc), f32)
        for r0, rows in chunks:
            ub[pl.ds(8 + r0, CONV_ROWS), :] = cc_ref[rows, :].astype(f32) * cv_ref[rows, :].astype(f32)
        w = w_ref[...]
        sums = [jnp.zeros((1, tc), f32)] * 3
        for r0, rows in chunks:
            us = [ub[pl.ds(8 - k + r0, CONV_ROWS), :] for k in range(3)]
            y = w[0:1] * us[0] + w[1:2] * us[1] + w[2:3] * us[2]
            z, cb, dcv = z_ref[rows, :].astype(f32), cb_ref[rows, :].astype(f32), dc_ref[rows, :].astype(f32)
            sg = _sigmoid(z)
            si = z * sg
            dyv = dcv * cb * si
            db[rows, :] = dyv
            stage[out.slot, 0, rows, :] = (dcv * y * si).astype(bf16)
            stage[out.slot, 3, rows, :] = (dcv * cb * y * (sg * (1.0 + z * (1.0 - sg)))).astype(bf16)
            sums = [s + jnp.sum(dyv * u, axis=0, keepdims=True) for s, u in zip(sums, us)]
        for r0, rows in chunks:
            du = (w[0:1] * db[rows, :] + w[1:2] * db[pl.ds(r0 + 1, CONV_ROWS), :] + w[2:3] * db[pl.ds(r0 + 2, CONV_ROWS), :])
            stage[out.slot, 1, rows, :] = (du * cv_ref[rows, :].astype(f32)).astype(bf16)
            stage[out.slot, 2, rows, :] = (du * cc_ref[rows, :].astype(f32)).astype(bf16)
        for c in range(4):
            out.start(c, dst(c))
        part = jnp.concatenate(sums, axis=0)

        @pl.when(b == 0)
        def _():
            dw_ref[...] = part

        @pl.when(b > 0)
        def _():
            dw_ref[...] += part

        out.end(dst)

    def seg(col0):
        return pl.BlockSpec((S, tc), lambda j, b: (b, col0 // tc + j))

    return pl.pallas_call(
        body, grid=(CONV_W // tc, B),
        in_specs=[seg(CB), seg(CC), seg(CV), seg(ZC), pl.BlockSpec((3, tc), lambda j, b: (0, j)),
                  pl.BlockSpec((S, tc), lambda j, b: (b, j)), pl.BlockSpec(memory_space=pl.ANY)],
        out_specs=[pl.BlockSpec(memory_space=pl.ANY), pl.BlockSpec((3, tc), lambda j, b: (0, j))],
        out_shape=[jax.ShapeDtypeStruct(dproj.shape, dproj.dtype), jax.ShapeDtypeStruct((3, CONV_W), f32)],
        scratch_shapes=[pltpu.VMEM((S + 8, tc), f32), pltpu.VMEM((S + 8, tc), f32), pltpu.VMEM((2, 4, S, tc), bf16),
                        pltpu.SemaphoreType.DMA((2, 4))],
        input_output_aliases={6: 0},
        compiler_params=_params(("arbitrary", "arbitrary")), name="conv_bwd",
    )(proj, proj, proj, proj, conv_w, dc, dproj)


MEM_CHUNK = 1024


def _mem_fwd(proj, mkv, gq, gk, B, S):
    T, NC = proj.shape
    scale = MEM_HD ** -0.5

    def body(q_ref, z_ref, k_ref, v_ref, gq_ref, gk_ref, o_ref):
        kv = k_ref[...]
        kn = (kv * _rstd_head(kv) * gk_ref[...]).astype(bf16)
        vv = v_ref[...].astype(bf16)

        def chunk(c, carry):
            rows = pl.ds(pl.multiple_of(c * MEM_CHUNK, MEM_CHUNK), MEM_CHUNK)
            q = q_ref[rows, :].astype(f32)
            qn = q * _rstd(q) * gq_ref[...]
            s = _dot(qn, kn, NT_DIMS) * scale
            p = jnp.exp(s - jnp.max(s, axis=-1, keepdims=True))
            p = p / jnp.sum(p, axis=-1, keepdims=True)
            z = z_ref[rows, :].astype(f32)
            o_ref[rows, :] = (_dot(p, vv, NN_DIMS) * (z * _sigmoid(z))).astype(bf16)
            return carry

        lax.fori_loop(0, S // MEM_CHUNK, chunk, 0)

    gain = pl.BlockSpec((1, MEM_HD), lambda b, h: (0, 0))
    return pl.pallas_call(
        body, grid=(B, MEM_HEADS),
        in_specs=[pl.BlockSpec((S, MEM_HD), lambda b, h: (b, MQ // MEM_HD + h)),
                  pl.BlockSpec((S, MEM_HD), lambda b, h: (b, ZM // MEM_HD + h)),
                  pl.BlockSpec((MEM_LEN, MEM_HD), lambda b, h: (b, h)),
                  pl.BlockSpec((MEM_LEN, MEM_HD), lambda b, h: (b, MEM_HEADS + h)), gain, gain],
        out_specs=pl.BlockSpec((S, MEM_HD), lambda b, h: (b, h)),
        out_shape=jax.ShapeDtypeStruct((T, MEM_W), bf16),
        compiler_params=_params(("parallel", "parallel")), name="mem_fwd",
    )(proj, proj, mkv, mkv, gq, gk)


def _mem_bwd(proj, mkv, gq, gk, dmo, dproj, B, S):
    T, NC = proj.shape
    scale = MEM_HD ** -0.5

    def body(q_ref, z_ref, k_ref, v_ref, gq_ref, gk_ref, dmo_ref, dproj_in, dproj_ref, dmk_ref, dmv_ref, dgq_ref, dgk_ref,
             dkn_s, dv_s, gq_s, stage, sem):
        b, h = pl.program_id(0), pl.program_id(1)

        def dst(c):
            return dproj_ref.at[pl.ds(b * S, S), pl.ds((MQ, ZM)[c] + h * MEM_HD, MEM_HD)]

        out = _Deferred(stage, sem, b * MEM_HEADS + h, B * MEM_HEADS - 1, 2)
        out.begin(dst)
        kv = k_ref[...]
        kn = (kv * _rstd_head(kv) * gk_ref[...]).astype(bf16)
        vv = v_ref[...].astype(bf16)
        dkn_s[...] = jnp.zeros_like(dkn_s)
        dv_s[...] = jnp.zeros_like(dv_s)
        gq_s[...] = jnp.zeros_like(gq_s)

        def chunk(c, carry):
            rows = pl.ds(pl.multiple_of(c * MEM_CHUNK, MEM_CHUNK), MEM_CHUNK)
            q = q_ref[rows, :].astype(f32)
            qn = q * _rstd(q) * gq_ref[...]
            s = _dot(qn, kn, NT_DIMS) * scale
            p = jnp.exp(s - jnp.max(s, axis=-1, keepdims=True))
            p = p / jnp.sum(p, axis=-1, keepdims=True)
            mo = _dot(p, vv, NN_DIMS)
            z, dg_ = z_ref[rows, :].astype(f32), dmo_ref[rows, :].astype(f32)
            sg = _sigmoid(z)
            do = dg_ * (z * sg)
            stage[out.slot, 1, rows, :] = (dg_ * mo * (sg * (1.0 + z * (1.0 - sg)))).astype(bf16)
            dp = _dot(do, vv, NT_DIMS)
            ds = p * (dp - jnp.sum(do * mo, axis=-1, keepdims=True)) * scale
            dq, gq_p = _rms_bwd_rows(q, gq_ref[...], _dot(ds, kn, NN_DIMS), on_mxu=False)
            stage[out.slot, 0, rows, :] = dq.astype(bf16)
            gq_s[...] += gq_p
            dkn_s[...] += _dot(ds, qn, TN_DIMS)
            dv_s[...] += _dot(p, do, TN_DIMS)
            return carry

        lax.fori_loop(0, S // MEM_CHUNK, chunk, 0)
        for c in range(2):
            out.start(c, dst(c))
        dk, gk_p = _rms_bwd_rows(kv, gk_ref[...], dkn_s[...])
        dmk_ref[...] = dk
        dmv_ref[...] = dv_s[...]

        @pl.when((b == 0) & (h == 0))
        def _():
            dgq_ref[...] = gq_s[...]
            dgk_ref[...] = gk_p

        @pl.when((b > 0) | (h > 0))
        def _():
            dgq_ref[...] += gq_s[...]
            dgk_ref[...] += gk_p

        out.end(dst)

    gain = pl.BlockSpec((1, MEM_HD), lambda b, h: (0, 0))
    kvb = pl.BlockSpec((MEM_LEN, MEM_HD), lambda b, h: (b, h))
    return pl.pallas_call(
        body, grid=(B, MEM_HEADS),
        in_specs=[pl.BlockSpec((S, MEM_HD), lambda b, h: (b, MQ // MEM_HD + h)),
                  pl.BlockSpec((S, MEM_HD), lambda b, h: (b, ZM // MEM_HD + h)),
                  kvb, pl.BlockSpec((MEM_LEN, MEM_HD), lambda b, h: (b, MEM_HEADS + h)), gain, gain,
                  pl.BlockSpec((S, MEM_HD), lambda b, h: (b, h)), pl.BlockSpec(memory_space=pl.ANY)],
        out_specs=[pl.BlockSpec(memory_space=pl.ANY), kvb, kvb, gain, gain],
        out_shape=[jax.ShapeDtypeStruct(dproj.shape, dproj.dtype), jax.ShapeDtypeStruct((B * MEM_LEN, MEM_W), f32),
                   jax.ShapeDtypeStruct((B * MEM_LEN, MEM_W), f32), jax.ShapeDtypeStruct((1, MEM_HD), f32),
                   jax.ShapeDtypeStruct((1, MEM_HD), f32)],
        scratch_shapes=[pltpu.VMEM((MEM_LEN, MEM_HD), f32), pltpu.VMEM((MEM_LEN, MEM_HD), f32), pltpu.VMEM((1, MEM_HD), f32),
                        pltpu.VMEM((2, 2, S, MEM_HD), bf16), pltpu.SemaphoreType.DMA((2, 2))],
        input_output_aliases={7: 0},
        compiler_params=_params(("arbitrary", "arbitrary")), name="mem_bwd",
    )(proj, proj, mkv, mkv, gq, gk, dmo, dproj)


def _merge_fwd(proj, ag, cg, mg, wa, wc, wm):
    T, NC = proj.shape
    D = wa.shape[1]
    tm, tn = _pick(T, 1024), _pick(D, 512)
    assert GT % tn == 0

    def body(ag_ref, cg_ref, mg_ref, wa_ref, wc_ref, wm_ref, g0_ref, g1_ref, g2_ref, mer_ref, ba_ref, bc_ref, bm_ref):
        ba = _dot(ag_ref[...], wa_ref[...], NN_DIMS)
        bc = _dot(cg_ref[...], wc_ref[...], NN_DIMS)
        bm = _dot(mg_ref[...], wm_ref[...], NN_DIMS)
        s0, s1, s2 = (_sigmoid(r[...].astype(f32)) for r in (g0_ref, g1_ref, g2_ref))
        mer_ref[...] = (s0 * ba + s1 * bc + s2 * bm).astype(bf16)
        ba_ref[...] = ba.astype(bf16)
        bc_ref[...] = bc.astype(bf16)
        bm_ref[...] = bm.astype(bf16)

    def act(w):
        return pl.BlockSpec((tm, w), lambda i, j: (i, 0))

    def wt(k):
        return pl.BlockSpec((k, tn), lambda i, j: (0, j))

    def gate(n):
        return pl.BlockSpec((tm, tn), lambda i, j: (i, (GT + n * D) // tn + j))

    out = pl.BlockSpec((tm, tn), lambda i, j: (i, j))
    return pl.pallas_call(
        body, grid=(T // tm, D // tn),
        in_specs=[act(ATTN_OUT), act(CONV_W), act(MEM_W), wt(ATTN_OUT), wt(CONV_W), wt(MEM_W), gate(0), gate(1), gate(2)],
        out_specs=[out] * 4, out_shape=[jax.ShapeDtypeStruct((T, D), bf16)] * 4,
        compiler_params=_params(("parallel", "parallel")), name="merge_fwd",
    )(ag, cg, mg, wa, wc, wm, proj, proj, proj)


def _out_loss(merged, w_out, x, tgt):
    T, D = x.shape
    tm, tn = _pick(T, 1024), _pick(D, 512)

    def body(m_ref, w_ref, x_ref, t_ref, dyb_ref, lp_ref):
        e = x_ref[...] + _dot(m_ref[...], w_ref[...], NN_DIMS) - t_ref[...]
        dyb_ref[...] = (e / D).astype(bf16)
        part = jnp.full((1, 8, LANES), jnp.sum(e * e), f32)

        @pl.when(pl.program_id(1) == 0)
        def _():
            lp_ref[...] = part

        @pl.when(pl.program_id(1) > 0)
        def _():
            lp_ref[...] += part

    tile = pl.BlockSpec((tm, tn), lambda i, j: (i, j))
    return pl.pallas_call(
        body, grid=(T // tm, D // tn),
        in_specs=[pl.BlockSpec((tm, D), lambda i, j: (i, 0)), pl.BlockSpec((D, tn), lambda i, j: (0, j)), tile, tile],
        out_specs=[tile, pl.BlockSpec((1, 8, LANES), lambda i, j: (i, 0, 0))],
        out_shape=[jax.ShapeDtypeStruct((T, D), bf16), jax.ShapeDtypeStruct((T // tm, 8, LANES), f32)],
        compiler_params=_params(("parallel", "arbitrary")), name="out_loss",
    )(merged, w_out, x, tgt)


def _merge_bwd(proj, dyb, w_out, ba, bc, bm):
    T, NC = proj.shape
    D = w_out.shape[0]
    tm, tn = _pick(T, 1024), _pick(D, 512)
    assert GT % tn == 0

    def body(dy_ref, w_ref, ba_ref, bc_ref, bm_ref, g0_ref, g1_ref, g2_ref, da_ref, dc_ref, dm_ref, dproj_ref, stage, sem):
        i, j = pl.program_id(0), pl.program_id(1)

        def dst(n):
            return dproj_ref.at[pl.ds(i * tm, tm), pl.ds(GT + n * D + j * tn, tn)]

        out = _Deferred(stage, sem, i * (D // tn) + j, (T // tm) * (D // tn) - 1, 3)
        out.begin(dst)
        dmer = _dot(dy_ref[...], w_ref[...], NT_DIMS)
        for n, (g_ref, br_ref, o_ref) in enumerate(((g0_ref, ba_ref, da_ref), (g1_ref, bc_ref, dc_ref), (g2_ref, bm_ref, dm_ref))):
            sg = _sigmoid(g_ref[...].astype(f32))
            o_ref[...] = (sg * dmer).astype(bf16)
            stage[out.slot, n] = (dmer * br_ref[...].astype(f32) * (sg * (1.0 - sg))).astype(bf16)
            out.start(n, dst(n))
        out.end(dst)

    tile = pl.BlockSpec((tm, tn), lambda i, j: (i, j))

    def gate(n):
        return pl.BlockSpec((tm, tn), lambda i, j: (i, (GT + n * D) // tn + j))

    return pl.pallas_call(
        body, grid=(T // tm, D // tn),
        in_specs=[pl.BlockSpec((tm, D), lambda i, j: (i, 0)), pl.BlockSpec((tn, D), lambda i, j: (j, 0)), tile, tile, tile,
                  gate(0), gate(1), gate(2)],
        out_specs=[tile, tile, tile, pl.BlockSpec(memory_space=pl.ANY)],
        out_shape=[jax.ShapeDtypeStruct((T, D), bf16)] * 3 + [jax.ShapeDtypeStruct((T, NC), bf16)],
        scratch_shapes=[pltpu.VMEM((2, 3, tm, tn), bf16), pltpu.SemaphoreType.DMA((2, 3))],
        compiler_params=_params(("arbitrary", "arbitrary")), name="merge_bwd",
    )(dyb, w_out, ba, bc, bm, proj, proj, proj)


def _fwd_bwd_head(x2, mem2, t2, proj, attn, B, S, mem_norm_g, attn_q_norm, attn_k_norm, conv_w, mem_q_norm, mem_k_norm,
                  w_kv, w_a, w_c, w_m, w_out):
    D = x2.shape[1]
    mng = mem_norm_g.reshape(1, D)
    mqn, mkn = mem_q_norm.reshape(1, MEM_HD), mem_k_norm.reshape(1, MEM_HD)
    ag, a_comb, lse = attn

    mh = _rms_fwd(mem2, mng, "rms_mem")
    mkv = _mm(mh, w_kv, mode="nn", out_dtype=f32, name="mkv")
    cg = _conv_fwd(proj, conv_w, B, S)
    mg = _mem_fwd(proj, mkv, mqn, mkn, B, S)
    merged, ba, bc, bm = _merge_fwd(proj, ag, cg, mg, w_a, w_c, w_m)
    dyb, lp = _out_loss(merged, w_out, x2, t2)
    sq_err = jnp.sum(lp[:, 0, 0])

    g = {}
    g["w_out"] = _mm(merged, dyb, mode="tn", out_dtype=DW_DTYPE, name="dw_out")
    dba, dbc, dbm, dproj = _merge_bwd(proj, dyb, w_out, ba, bc, bm)
    g["w_br_attn"] = _mm(ag, dba, mode="tn", out_dtype=DW_DTYPE, name="dw_br_attn")
    g["w_br_conv"] = _mm(cg, dbc, mode="tn", out_dtype=DW_DTYPE, name="dw_br_conv")
    g["w_br_mem"] = _mm(mg, dbm, mode="tn", out_dtype=DW_DTYPE, name="dw_br_mem")
    dag = _mm(dba, w_a, mode="nt", out_dtype=bf16, name="d_attn_out")
    dcg = _mm(dbc, w_c, mode="nt", out_dtype=bf16, name="d_conv_out")
    dmg = _mm(dbm, w_m, mode="nt", out_dtype=bf16, name="d_mem_out")
    dproj, g["attn_q_norm"], g["attn_k_norm"] = _attn_bwd(proj, attn_q_norm, attn_k_norm, a_comb, lse, dag, dproj, B, S)
    dproj, g["conv_w"] = _conv_bwd(proj, conv_w, dcg, dproj, B, S)
    dproj, dmk, dmv, dgmq, dgmk = _mem_bwd(proj, mkv, mqn, mkn, dmg, dproj, B, S)
    g["mem_q_norm"], g["mem_k_norm"] = dgmq.reshape(MEM_HD), dgmk.reshape(MEM_HD)
    dmkv = jnp.concatenate([dmk, dmv], axis=1)
    dmh = _mm(dmkv, w_kv, mode="nt", out_dtype=f32, name="d_mem_h")
    g["mem_norm_g"] = _rms_bwd(mem2, mng, dmh, None, "rms_mem_bwd").reshape(D)
    return sq_err, g, dict(dy=dyb, dproj=dproj, mh=mh, dmkv=dmkv)


MESH = pl.DeviceIdType.MESH
HBM = pl.BlockSpec(memory_space=pl.ANY)


def _me():
    x, y, c = lax.axis_index("x"), lax.axis_index("y"), lax.axis_index("c")
    return (x, y, c), 4 * x + 2 * y + c


def _peer(k):
    (x, y, c), _ = _me()
    p = (1 - x if k & 4 else x, 1 - y if k & 2 else y, 1 - c if k & 1 else c)
    return p, 4 * p[0] + 2 * p[1] + p[2]


def _window(ref, axis, size, idx):
    if axis is None:
        return ref.at[idx]
    if axis == 0:
        return ref.at[pl.ds(idx * size, size), :]
    return ref.at[:, pl.ds(idx * size, size)]


def _full_shape(s, axis):
    if axis is None:
        return (N_DEV,) + s.shape
    return tuple(N_DEV * d if i == axis else d for i, d in enumerate(s.shape))


OTHER_CHIPS = (4, 2, 6)


def _spread_comm(shards, axes):
    n = len(shards)

    def copies(ins, outs, send_sems, recv_sems, base=0):
        _, me = _me()
        out = []
        for a in range(n):
            mine = _window(outs[a], axes[a], None if axes[a] is None else shards[a].shape[axes[a]], me)
            out.append(pltpu.make_async_copy(ins[a], mine, send_sems.at[base + a, N_CHIP]))
            for k, dist in enumerate((1,) + OTHER_CHIPS):
                dev, _ = _peer(dist)
                out.append(pltpu.make_async_remote_copy(
                    src_ref=ins[a], dst_ref=mine, send_sem=send_sems.at[base + a, k], recv_sem=recv_sems.at[base + a, k],
                    device_id=dev, device_id_type=MESH))
        return out

    shapes = [jax.ShapeDtypeStruct(_full_shape(s, ax), s.dtype) for s, ax in zip(shards, axes)]
    return _Comm(shards, shapes, (n, N_CHIP + 1), copies)


def _forward_blocks(fulls, axes):
    n = len(fulls)
    sizes = [None if ax is None else f.shape[ax] // N_DEV for f, ax in zip(fulls, axes)]

    def body(*refs):
        outs = refs[n:2 * n]
        send_sems, recv_sems = refs[2 * n:]
        sibling, _ = _peer(1)
        copies = []
        for j, dist in enumerate(OTHER_CHIPS):
            _, held = _peer(dist)
            for a in range(n):
                block = _window(outs[a], axes[a], sizes[a], held)
                copies.append(pltpu.make_async_remote_copy(
                    src_ref=block, dst_ref=block, send_sem=send_sems.at[a, j], recv_sem=recv_sems.at[a, j],
                    device_id=sibling, device_id_type=MESH))
        for cp in copies:
            cp.start()
        for cp in copies:
            cp.wait()

    return pl.pallas_call(
        body, in_specs=[HBM] * n, out_specs=[HBM] * n,
        out_shape=[jax.ShapeDtypeStruct(f.shape, f.dtype) for f in fulls],
        scratch_shapes=[pltpu.SemaphoreType.DMA((n, len(OTHER_CHIPS))), pltpu.SemaphoreType.DMA((n, len(OTHER_CHIPS)))],
        input_output_aliases={a: a for a in range(n)},
        name="forward_blocks",
    )(*fulls)


def _shard_order():
    (x, y, c), me = _me()
    xs, ys, xo, yo = _peer(4)[1], _peer(2)[1], _peer(5)[1], _peer(3)[1]
    north = c == 1
    ids = [me, _peer(1)[1], jnp.where(north, xs, ys), jnp.where(north, yo, xo), jnp.where(north, ys, xs),
           jnp.where(north, xo, yo), _peer(6)[1], _peer(7)[1]]
    return jnp.stack(ids).astype(jnp.int32)


def _proj_gather(h, w_shard, order, comm, comm_at):
    T, K = h.shape
    C = w_shard.shape[1]
    tm = _pick(T, 1024)
    nm = T // tm
    ahead = max(nm - 2, 0)
    n_ci, n_co = len(comm.ins), len(comm.out_shapes)

    def body(*refs):
        order_ref, h_ref, ws_ref = refs[:3]
        c_ins = refs[3:3 + n_ci]
        o_ref, wf_ref = refs[3 + n_ci:5 + n_ci]
        c_outs = refs[5 + n_ci:5 + n_ci + n_co]
        wbuf, send_sems, recv_sems, own_sem, buf_sems, c_send, c_recv = refs[5 + n_ci + n_co:]
        t, i = pl.program_id(0), pl.program_id(1)

        @pl.when((t == comm_at) & (i == 0))
        def _():
            for cp in comm.copies(c_ins, c_outs, c_send, c_recv):
                cp.start()
        (x, y, c), me = _me()
        sibling, sib_id = _peer(1)
        chips = [_peer(dist) for dist in OTHER_CHIPS]

        def win(idx):
            return wf_ref.at[:, pl.ds(idx * C, C)]

        def copy(k, block, to, src=None):
            return pltpu.make_async_remote_copy(
                src_ref=win(block) if src is None else src, dst_ref=win(block),
                send_sem=send_sems.at[k], recv_sem=recv_sems.at[k], device_id=to, device_id_type=MESH)

        def fetch(tt, src):
            return pltpu.make_async_copy(src, wbuf.at[tt % 2], buf_sems.at[tt % 2])

        own = pltpu.make_async_copy(ws_ref, win(me), own_sem)
        (x_nbr, x_id), (y_nbr, y_id), (_, d_id) = chips

        def half(k, block, top, to):
            rows = wf_ref.at[pl.ds(0 if top else K // 2, K // 2), pl.ds(block * C, C)]
            return pltpu.make_async_remote_copy(src_ref=rows, dst_ref=rows, send_sem=send_sems.at[k], recv_sem=recv_sems.at[k],
                                                device_id=to, device_id_type=MESH)

        here = (x, y, c)

        def pass_on(from_x):
            if from_x:
                copy(4, x_id, sibling).start()
                half(8, x_id, False, y_nbr).start()
            else:
                copy(5, y_id, sibling).start()
                half(7, y_id, True, x_nbr).start()

        @pl.when((t == 0) & (i == 0))
        def _():
            fetch(0, ws_ref).start()
            own.start()
            copy(0, me, sibling, src=ws_ref).start()

        for tt in range(N_DEV):
            @pl.when((t == tt) & (i == 0))
            def _(tt=tt):
                fetch(tt, ws_ref).wait()

        o_ref[...] = _dot(h_ref[...], wbuf[t % 2], NN_DIMS).astype(o_ref.dtype)

        def schedule(first_x):
            on = c == (1 if first_x else 0)
            (f_dev, f_id), (g_dev, g_id) = ((x_nbr, x_id), (y_nbr, y_id)) if first_x else ((y_nbr, y_id), (x_nbr, x_id))
            kf, kg = (1, 2) if first_x else (2, 1)
            pf, pg = (4, 5) if first_x else (5, 4)

            @pl.when((t == 0) & (i == 0) & on)
            def _():
                copy(kf, me, f_dev, src=ws_ref).start()

            def event(nxt):
                if nxt == 1:
                    copy(0, sib_id, here).wait_recv()
                    return sib_id
                if nxt == 2:
                    copy(kf, f_id, here).wait_recv()
                    copy(kf, me, f_dev, src=ws_ref).wait_send()
                    copy(kg, me, g_dev, src=ws_ref).start()
                    pass_on(first_x)
                    return f_id
                if nxt == 3:
                    copy(pg, g_id + 1 - 2 * c, here).wait_recv()
                    return g_id + 1 - 2 * c
                if nxt == 4:
                    copy(kg, g_id, here).wait_recv()
                    pass_on(not first_x)
                    return g_id
                if nxt == 5:
                    copy(pf, f_id + 1 - 2 * c, here).wait_recv()
                    return f_id + 1 - 2 * c
                if nxt == 6:
                    half(7, d_id, True, here).wait_recv()
                    half(8, d_id, False, here).wait_recv()
                    copy(6, d_id, sibling).start()
                    return d_id
                copy(6, d_id + 1 - 2 * c, here).wait_recv()
                return d_id + 1 - 2 * c

            for tt in range(N_DEV - 1):
                @pl.when((t == tt) & (i == ahead) & on)
                def _(tt=tt):
                    fetch(tt + 1, win(event(tt + 1))).start()

            @pl.when((t == N_DEV - 1) & (i == nm - 1) & on)
            def _():
                copy(kg, me, g_dev, src=ws_ref).wait_send()

        schedule(True)
        schedule(False)

        @pl.when((t == N_DEV - 1) & (i == nm - 1))
        def _():
            copy(0, me, sibling, src=ws_ref).wait_send()
            half(7, y_id, True, x_nbr).wait_send()
            half(8, x_id, False, y_nbr).wait_send()
            for j, (_, dev_id) in enumerate(chips):
                copy(4 + j, dev_id, sibling).wait_send()
            own.wait()
            for cp in comm.copies(c_ins, c_outs, c_send, c_recv):
                cp.wait()

    hbm = pl.BlockSpec(memory_space=pl.ANY)
    res = pl.pallas_call(
        body,
        grid_spec=pltpu.PrefetchScalarGridSpec(
            num_scalar_prefetch=1, grid=(N_DEV, nm),
            in_specs=[pl.BlockSpec((tm, K), lambda t, i, order_ref: (i, 0)), hbm] + [hbm] * n_ci,
            out_specs=[pl.BlockSpec((tm, C), lambda t, i, order_ref: (i, order_ref[t])), hbm] + [hbm] * n_co,
            scratch_shapes=[pltpu.VMEM((2, K, C), bf16), pltpu.SemaphoreType.DMA((9,)), pltpu.SemaphoreType.DMA((9,)),
                            pltpu.SemaphoreType.DMA, pltpu.SemaphoreType.DMA((2,))] + comm.scratch()),
        out_shape=[jax.ShapeDtypeStruct((T, N_DEV * C), PROJ_DTYPE), jax.ShapeDtypeStruct((K, N_DEV * C), bf16)] + comm.out_shapes,
        compiler_params=_params(("arbitrary", "arbitrary")), name="proj_gather",
    )(order, h, w_shard, *comm.ins)
    return res[0], res[1], list(res[2:])


N_CHIP = 4


def _shard_shape(g, ax):
    return tuple(d // N_DEV if i == ax else d for i, d in enumerate(g.shape))


def _sibling_comm(grads, axes):
    n = len(grads)
    sizes = [g.shape[ax] // N_DEV for g, ax in zip(grads, axes)]

    def copies(ins, lands, send_sems, recv_sems, base=0):
        sibling, _ = _peer(1)
        out = []
        for j in range(N_CHIP):
            _, owner = _peer(2 * j + 1)
            for a in range(n):
                out.append(pltpu.make_async_remote_copy(
                    src_ref=_window(ins[a], axes[a], sizes[a], owner), dst_ref=lands[a].at[j],
                    send_sem=send_sems.at[base + a, j], recv_sem=recv_sems.at[base + a, j], device_id=sibling,
                    device_id_type=MESH))
        return out

    shapes = [jax.ShapeDtypeStruct((N_CHIP,) + _shard_shape(g, ax), g.dtype) for g, ax in zip(grads, axes)]
    return _Comm(grads, shapes, (n, N_CHIP), copies)


def _chips_comm(sums):
    n = len(sums)

    def copies(ins, lands, send_sems, recv_sems, base=0):
        out = []
        for j in range(1, N_CHIP):
            dev, _ = _peer(2 * j)
            for a in range(n):
                out.append(pltpu.make_async_remote_copy(
                    src_ref=ins[a].at[j - 1], dst_ref=lands[a].at[j - 1],
                    send_sem=send_sems.at[base + a, j - 1], recv_sem=recv_sems.at[base + a, j - 1], device_id=dev,
                    device_id_type=MESH))
        return out

    return _Comm(sums, [jax.ShapeDtypeStruct(s.shape, s.dtype) for s in sums], (n, N_CHIP), copies)


def _join(c1, c2):
    n1, m1, k1 = len(c1.ins), len(c1.out_shapes), c1.sem_shape[0]

    def copies(ins, lands, send_sems, recv_sems):
        return (c1.copies(ins[:n1], lands[:m1], send_sems, recv_sems, base=0)
                + c2.copies(ins[n1:], lands[m1:], send_sems, recv_sems, base=k1))

    aliases = {**c1.aliases, **{n1 + k: m1 + v for k, v in c2.aliases.items()}}
    return _Comm(c1.ins + c2.ins, c1.out_shapes + c2.out_shapes, (k1 + c2.sem_shape[0], N_CHIP), copies, aliases)


def _chips_first_hop(sums):
    n = len(sums)

    def copies(ins, outs, send_sems, recv_sems, base=0):
        lands, relays = outs[:n], outs[n:]
        (y_dev, _), (x_dev, _) = _peer(2), _peer(4)
        out = []
        for a in range(n):
            half = sums[a].shape[1] // 2
            for k, (src, dst, dev) in enumerate((
                    (ins[a].at[0], lands[a].at[0], y_dev), (ins[a].at[1], lands[a].at[1], x_dev),
                    (ins[a].at[2, pl.ds(0, half)], relays[a].at[0], x_dev),
                    (ins[a].at[2, pl.ds(half, half)], relays[a].at[1], y_dev))):
                out.append(pltpu.make_async_remote_copy(
                    src_ref=src, dst_ref=dst, send_sem=send_sems.at[base + a, k], recv_sem=recv_sems.at[base + a, k],
                    device_id=dev, device_id_type=MESH))
        return out

    shapes = ([jax.ShapeDtypeStruct(s.shape, s.dtype) for s in sums]
              + [jax.ShapeDtypeStruct((2, s.shape[1] // 2) + s.shape[2:], s.dtype) for s in sums])
    return _Comm(sums, shapes, (n, N_CHIP), copies)


def _chips_relay(relays, lands):
    n = len(relays)

    def copies(ins, outs, send_sems, recv_sems, base=0):
        (y_dev, _), (x_dev, _) = _peer(2), _peer(4)
        out = []
        for a in range(n):
            half = relays[a].shape[1]
            for k, (rows, dev) in enumerate(((pl.ds(0, half), y_dev), (pl.ds(half, half), x_dev))):
                out.append(pltpu.make_async_remote_copy(
                    src_ref=ins[a].at[k], dst_ref=outs[a].at[2, rows], send_sem=send_sems.at[base + a, k],
                    recv_sem=recv_sems.at[base + a, k], device_id=dev, device_id_type=MESH))
        return out

    return _Comm(list(relays) + list(lands), [jax.ShapeDtypeStruct(l.shape, l.dtype) for l in lands], (n, N_CHIP), copies,
                 aliases={n + a: a for a in range(n)})


def _rs_chip_sum(grad, land, xyc, axis, name):
    R, C = land.shape[1:]
    tr = _pick(R, max(8, (640 * 1024) // C), 8)

    def body(xyc_ref, g_ref, l_ref, o_ref):
        o_ref[0] = (g_ref[...].astype(f32) + l_ref[0].astype(f32)).astype(bf16)

    def owner(j, xyc_ref):
        jj = j + 1
        return 4 * (xyc_ref[0] ^ (jj >> 1)) + 2 * (xyc_ref[1] ^ (jj & 1)) + xyc_ref[2]

    if axis == 0:
        g_spec = pl.BlockSpec((tr, C), lambda j, i, xyc_ref: (owner(j, xyc_ref) * (R // tr) + i, 0))
    else:
        g_spec = pl.BlockSpec((tr, C), lambda j, i, xyc_ref: (i, owner(j, xyc_ref)))
    return pl.pallas_call(
        body,
        grid_spec=pltpu.PrefetchScalarGridSpec(
            num_scalar_prefetch=1, grid=(N_CHIP - 1, R // tr),
            in_specs=[g_spec, pl.BlockSpec((1, tr, C), lambda j, i, xyc_ref: (j + 1, i, 0))],
            out_specs=pl.BlockSpec((1, tr, C), lambda j, i, xyc_ref: (j, i, 0))),
        out_shape=jax.ShapeDtypeStruct((N_CHIP - 1, R, C), bf16),
        compiler_params=_params(("parallel", "parallel")), name=name,
    )(xyc, grad, land)


def _allreduce_small(part):
    R = part.shape[0]

    def body(p_ref, o_ref, buf, send_sems, recv_sems):
        (x, y, c), me = _me()
        buf[me] = p_ref[...]
        copies = []
        for k in range(1, N_DEV):
            dev, dev_id = _peer(k)
            copies.append(pltpu.make_async_remote_copy(
                src_ref=p_ref, dst_ref=buf.at[me], send_sem=send_sems.at[k - 1], recv_sem=recv_sems.at[k - 1],
                device_id=dev, device_id_type=MESH))
        for cp in copies:
            cp.start()
        for k in range(1, N_DEV):
            _, dev_id = _peer(k)
            pltpu.make_async_remote_copy(
                src_ref=p_ref, dst_ref=buf.at[dev_id], send_sem=send_sems.at[k - 1], recv_sem=recv_sems.at[k - 1],
                device_id=(x, y, c), device_id_type=MESH).wait_recv()
        for cp in copies:
            cp.wait_send()
        acc = buf[0]
        for d in range(1, N_DEV):
            acc = acc + buf[d]
        o_ref[...] = acc

    return pl.pallas_call(
        body, in_specs=[pl.BlockSpec(memory_space=pltpu.VMEM)], out_specs=pl.BlockSpec(memory_space=pltpu.VMEM),
        out_shape=jax.ShapeDtypeStruct((R, LANES), f32),
        scratch_shapes=[pltpu.VMEM((N_DEV, R, LANES), f32), pltpu.SemaphoreType.DMA((N_DEV - 1,)), pltpu.SemaphoreType.DMA((N_DEV - 1,))],
        name="allreduce_small",
    )(part)


def _adamw_math(w, g, m, v):
    m2 = ADAM_B1 * m + (1.0 - ADAM_B1) * g
    v2 = ADAM_B2 * v + (1.0 - ADAM_B2) * (g * g)
    m_hat = m2 / (1.0 - ADAM_B1 ** ADAM_STEP)
    v_hat = v2 / (1.0 - ADAM_B2 ** ADAM_STEP)
    return -ADAM_LR * (m_hat / (jnp.sqrt(v_hat) + ADAM_EPS) + ADAM_WD * w), m2, v2


def _adamw_shard(grad, land_sib, land_chips, w, m, v, me, axis, name, row0=0, prev=None):
    R, C = land_sib.shape[1:]
    assert axis == 1 or R == w.shape[0]
    tr = _pick(R, max(8, (320 * 1024) // C), 8)
    r0 = row0 // tr
    n_prev = len(prev) if prev else 0

    def body(me_ref, g_ref, s_ref, l_ref, w_ref, m_ref, v_ref, *rest):
        go_ref, d_ref, mo_ref, vo_ref = rest[n_prev:]
        g = g_ref[...].astype(f32) + s_ref[0].astype(f32)
        for j in range(N_CHIP - 1):
            g = g + l_ref[j].astype(f32)
        d, m2, v2 = _adamw_math(w_ref[...], g, m_ref[...], v_ref[...])
        go_ref[...] = g
        d_ref[...] = d
        mo_ref[...] = m2
        vo_ref[...] = v2

    if axis == 0:
        g_spec = pl.BlockSpec((tr, C), lambda i, me_ref: (me_ref[0] * (R // tr) + i, 0))
    else:
        g_spec = pl.BlockSpec((tr, C), lambda i, me_ref: (i, me_ref[0]))
    blk = pl.BlockSpec((tr, C), lambda i, me_ref: (r0 + i, 0))
    return pl.pallas_call(
        body,
        grid_spec=pltpu.PrefetchScalarGridSpec(
            num_scalar_prefetch=1, grid=(R // tr,),
            in_specs=[g_spec, pl.BlockSpec((1, tr, C), lambda i, me_ref: (0, i, 0)),
                      pl.BlockSpec((N_CHIP - 1, tr, C), lambda i, me_ref: (0, i, 0)), blk, blk, blk]
            + [pl.BlockSpec(memory_space=pl.ANY)] * n_prev,
            out_specs=[blk] * 4),
        out_shape=[jax.ShapeDtypeStruct(w.shape, f32)] * 4,
        input_output_aliases={7 + i: i for i in range(n_prev)},
        compiler_params=_params(("parallel",)), name=name,
    )(me, grad, land_sib, land_chips, w, m, v, *(prev or []))


def _adamw_small(g, w, m, v):
    def body(g_ref, w_ref, m_ref, v_ref, d_ref, mo_ref, vo_ref):
        d, m2, v2 = _adamw_math(w_ref[...], g_ref[...], m_ref[...], v_ref[...])
        d_ref[...] = d
        mo_ref[...] = m2
        vo_ref[...] = v2

    return pl.pallas_call(body, out_shape=[jax.ShapeDtypeStruct(g.shape, f32)] * 3, name="adamw_small")(g, w, m, v)


def _pack_rows(parts, total_rows):
    rows = jnp.concatenate([p.reshape(-1, LANES) for p in parts], axis=0)
    return jnp.pad(rows, ((0, total_rows - rows.shape[0]), (0, 0)))


BIG = ("w_in", "mem_w_kv", "w_br_attn", "w_br_conv", "w_br_mem", "w_out")
BIG_AXIS = {"w_in": 1, "mem_w_kv": 0, "w_br_attn": 1, "w_br_conv": 1, "w_br_mem": 1, "w_out": 0}
SMALL = ("norm_g", "mem_norm_g", "attn_q_norm", "attn_k_norm", "mem_q_norm", "mem_k_norm")
ALL_W = ("norm_g", "mem_norm_g", "w_in", "attn_q_norm", "attn_k_norm", "conv_w", "mem_w_kv", "mem_q_norm", "mem_k_norm",
         "w_br_attn", "w_br_conv", "w_br_mem", "w_out")


def kernel(x, mem, norm_g, mem_norm_g, w_in, attn_q_norm, attn_k_norm, conv_w, mem_w_kv, mem_q_norm, mem_k_norm, w_br_attn, w_br_conv, w_br_mem, w_out, loss_target, m_norm_g, m_mem_norm_g, m_w_in, m_attn_q_norm, m_attn_k_norm, m_conv_w, m_mem_w_kv, m_mem_q_norm, m_mem_k_norm, m_w_br_attn, m_w_br_conv, m_w_br_mem, m_w_out, v_norm_g, v_mem_norm_g, v_w_in, v_attn_q_norm, v_attn_k_norm, v_conv_w, v_mem_w_kv, v_mem_q_norm, v_mem_k_norm, v_w_br_attn, v_w_br_conv, v_w_br_mem, v_w_out):
    w = dict(norm_g=norm_g, mem_norm_g=mem_norm_g, w_in=w_in, attn_q_norm=attn_q_norm, attn_k_norm=attn_k_norm, conv_w=conv_w,
             mem_w_kv=mem_w_kv, mem_q_norm=mem_q_norm, mem_k_norm=mem_k_norm, w_br_attn=w_br_attn, w_br_conv=w_br_conv,
             w_br_mem=w_br_mem, w_out=w_out)
    mo = dict(norm_g=m_norm_g, mem_norm_g=m_mem_norm_g, w_in=m_w_in, attn_q_norm=m_attn_q_norm, attn_k_norm=m_attn_k_norm,
              conv_w=m_conv_w, mem_w_kv=m_mem_w_kv, mem_q_norm=m_mem_q_norm, mem_k_norm=m_mem_k_norm, w_br_attn=m_w_br_attn,
              w_br_conv=m_w_br_conv, w_br_mem=m_w_br_mem, w_out=m_w_out)
    vo = dict(norm_g=v_norm_g, mem_norm_g=v_mem_norm_g, w_in=v_w_in, attn_q_norm=v_attn_q_norm, attn_k_norm=v_attn_k_norm,
              conv_w=v_conv_w, mem_w_kv=v_mem_w_kv, mem_q_norm=v_mem_q_norm, mem_k_norm=v_mem_k_norm, w_br_attn=v_w_br_attn,
              w_br_conv=v_w_br_conv, w_br_mem=v_w_br_mem, w_out=v_w_out)
    B, S, D = x.shape
    me = (4 * lax.axis_index("x") + 2 * lax.axis_index("y") + lax.axis_index("c")).astype(jnp.int32)

    T = B * S
    x2, t2, mem2, ng = x.reshape(T, D), loss_target.reshape(T, D), mem.reshape(B * MEM_LEN, D), norm_g.reshape(1, D)
    h = _rms_fwd(x2, ng, "rms_x")
    with_proj, with_attn = ("mem_w_kv",), ("w_out", "w_br_attn", "w_br_conv", "w_br_mem")
    later = with_proj + with_attn
    later_axes = [BIG_AXIS[n] for n in later] + [None]
    conv_pad = jnp.pad(conv_w, ((0, 8 - conv_w.shape[0]), (0, 0)))
    proj, w_in_full, spread_a = _proj_gather(
        h, w_in.astype(bf16), _shard_order(),
        _spread_comm([w[n].astype(bf16) for n in with_proj], [BIG_AXIS[n] for n in with_proj]), comm_at=N_DEV - 3)
    *attn, spread_b = _attn_fwd(proj, attn_q_norm, attn_k_norm, B, S,
                                comm=_spread_comm([w[n].astype(bf16) for n in with_attn] + [conv_pad],
                                                  [BIG_AXIS[n] for n in with_attn] + [None]))
    *fulls, conv_g = _forward_blocks(spread_a + spread_b, later_axes)
    wf = dict(zip(later, fulls), w_in=w_in_full)
    conv_full = conv_g[:, :3, :].transpose(1, 0, 2).reshape(3, CONV_W)

    sq_err, g, t = _fwd_bwd_head(x2, mem2, t2, proj, attn, B, S, mem_norm_g, attn_q_norm, attn_k_norm, conv_full, mem_q_norm,
                                 mem_k_norm, wf["mem_w_kv"], wf["w_br_attn"], wf["w_br_conv"], wf["w_br_mem"], wf["w_out"])
    t.update(x2=x2, ng=ng, h=h)

    xyc = jnp.stack([lax.axis_index("x"), lax.axis_index("y"), lax.axis_index("c")]).astype(jnp.int32)
    me1 = me.reshape(1)
    early = ("w_out", "w_br_attn", "w_br_conv", "w_br_mem")
    g["mem_w_kv"], sib_early = _mm(t["mh"], t["dmkv"], mode="tn", out_dtype=DW_DTYPE, name="dw_kv",
                                   comm=_sibling_comm([g[n] for n in early], [BIG_AXIS[n] for n in early]))
    sums_early = [_rs_chip_sum(g[n], ls, xyc, BIG_AXIS[n], "rs_sum_" + n) for n, ls in zip(early, sib_early)]
    tm_w = _pick(D // 2, 1024)
    half = (D // 2) // tm_w
    g_top, (*chips_early, sib_kv) = _mm(t["h"], t["dproj"], mode="tn", out_dtype=DW_DTYPE, name="dw_in_top", tm=tm_w,
                                        m_tiles=(0, half),
                                        comm=_join(_chips_comm(sums_early), _sibling_comm([g["mem_w_kv"]], [0])))
    sum_kv = _rs_chip_sum(g["mem_w_kv"], sib_kv, xyc, 0, "rs_sum_mem_w_kv")
    g_bot, (chips_kv, sib_top) = _mm(t["h"], t["dproj"], mode="tn", out_dtype=DW_DTYPE, name="dw_in_bot", tm=tm_w,
                                     m_tiles=(half, half), comm=_join(_chips_comm([sum_kv]), _sibling_comm([g_top], [1])))
    sum_top = _rs_chip_sum(g_top, sib_top, xyc, 1, "rs_sum_w_in_top")
    tm_t = _pick(T // 2, 1024)
    halfm = (T // 2) // tm_t
    dh, (part_top, relay_top, sib_bot) = _mm(t["dproj"], wf["w_in"], mode="nt", out_dtype=f32, name="d_h_a", tm=tm_t, tk=D_H_TK,
                                             m_tiles=(0, halfm), into="new",
                                             comm=_join(_chips_first_hop([sum_top]), _sibling_comm([g_bot], [1])))
    sum_bot = _rs_chip_sum(g_bot, sib_bot, xyc, 1, "rs_sum_w_in_bot")
    dh, (part_bot, relay_bot, chips_top) = _mm(t["dproj"], wf["w_in"], mode="nt", out_dtype=f32, name="d_h_b", tm=tm_t,
                                               tk=D_H_TK, m_tiles=(halfm, halfm), into=dh,
                                               comm=_join(_chips_first_hop([sum_bot]), _chips_relay([relay_top], [part_top])))
    dx, dng, (chips_bot,) = _rms_bwd(t["x2"], t["ng"], dh, t["dy"], "rms_x_bwd", comm=_chips_relay([relay_bot], [part_bot]))
    g["norm_g"] = dng.reshape(D)

    grad, delta, new_m, new_v = {}, {}, {}, {}
    for n, ls, lc in zip(early + ("mem_w_kv",), sib_early + [sib_kv], chips_early + [chips_kv]):
        grad[n], delta[n], new_m[n], new_v[n] = _adamw_shard(g[n], ls, lc, w[n], mo[n], vo[n], me1, BIG_AXIS[n], "adamw_" + n)
    top = _adamw_shard(g_top, sib_top, chips_top, w_in, m_w_in, v_w_in, me1, 1, "adamw_w_in_top")
    grad["w_in"], delta["w_in"], new_m["w_in"], new_v["w_in"] = _adamw_shard(
        g_bot, sib_bot, chips_bot, w_in, m_w_in, v_w_in, me1, 1, "adamw_w_in_bot", row0=D // 2, prev=top)

    n_rep = sum(w[n].size for n in SMALL) // LANES
    conv_rows = g["conv_w"].reshape(3, N_DEV, LANES).transpose(1, 0, 2).reshape(3 * N_DEV, LANES)
    n_rows = -(-(n_rep + 3 * N_DEV + 1) // 8) * 8
    part = _pack_rows([g[n] for n in SMALL] + [conv_rows, jnp.full((1, LANES), sq_err, f32)], n_rows)
    tot = _allreduce_small(part)
    loss = tot[n_rep + 3 * N_DEV, 0] * (0.5 / D)
    n_small = -(-(n_rep + 3) // 8) * 8
    g_small = _pack_rows([tot[:n_rep], lax.dynamic_slice(tot, (n_rep + 3 * me, 0), (3, LANES))], n_small)
    names = SMALL + ("conv_w",)
    d_s, m_s, v_s = _adamw_small(g_small, _pack_rows([w[n] for n in names], n_small), _pack_rows([mo[n] for n in names], n_small),
                                 _pack_rows([vo[n] for n in names], n_small))
    off = 0
    for n in names:
        r = w[n].size // LANES
        grad[n], delta[n] = g_small[off:off + r].reshape(w[n].shape), d_s[off:off + r].reshape(w[n].shape)
        new_m[n], new_v[n] = m_s[off:off + r].reshape(w[n].shape), v_s[off:off + r].reshape(w[n].shape)
        off += r
    return (loss, dx.reshape(B, S, D), *[grad[n] for n in ALL_W], *[delta[n] for n in ALL_W], *[new_m[n] for n in ALL_W],
            *[new_v[n] for n in ALL_W])
```
